```python
import jax, jax.numpy as jnp
from jax import lax
import numpy as np

D_MODEL = 1024
BATCH = 8
SEQ = 2048
DEPTH = 1

CONV_WIDTH = D_MODEL
CONV_GROUPS = 8
LRU_WIDTH = D_MODEL
LRU_HEADS = 4
LRU_HEAD_DIM = LRU_WIDTH // LRU_HEADS
SHORT_CONV_K = 3
LRU_CONV_K = 4
FFN_CONV_K = 3
D_FF = 3 * D_MODEL
LRU_C = 8.0
RMS_EPS = 1e-6
IN_COLS = 3 * CONV_WIDTH + 2 * LRU_WIDTH + 2 * D_MODEL
SPLITS = (CONV_WIDTH, 2 * CONV_WIDTH, 3 * CONV_WIDTH,
          3 * CONV_WIDTH + LRU_WIDTH, 3 * CONV_WIDTH + 2 * LRU_WIDTH,
          3 * CONV_WIDTH + 2 * LRU_WIDTH + D_MODEL)

kernel_name = "hybrid_shortconv_rglru_convffn_sandwich"


def rmsnorm(x, g):
    xf = x.astype(jnp.float32)
    y = xf * lax.rsqrt(jnp.mean(xf * xf, axis=-1, keepdims=True) + RMS_EPS)
    return (y * g.astype(jnp.float32)).astype(x.dtype)


def causal_dwconv(x, w, b=None):
    k_width = w.shape[0]
    s = x.shape[1]
    xp = jnp.pad(x, ((0, 0), (k_width - 1, 0), (0, 0)))
    y = xp[:, 0:s] * w[0]
    for k in range(1, k_width):
        y = y + xp[:, k:k + s] * w[k]
    if b is not None:
        y = y + b
    return y


def rg_lru(x, w_a, b_a, w_x, b_x, lam):
    bsz, s, w = x.shape
    xf = x.astype(jnp.float32)
    xh = xf.reshape(bsz, s, LRU_HEADS, LRU_HEAD_DIM)
    r = jax.nn.sigmoid(jnp.einsum("bshi,hij->bshj", xh, w_a.astype(jnp.float32)) + b_a.astype(jnp.float32)).reshape(bsz, s, w)
    i = jax.nn.sigmoid(jnp.einsum("bshi,hij->bshj", xh, w_x.astype(jnp.float32)) + b_x.astype(jnp.float32)).reshape(bsz, s, w)
    log_a = LRU_C * r * jax.nn.log_sigmoid(lam.astype(jnp.float32))
    a = jnp.exp(log_a)
    mult = jnp.sqrt(-jnp.expm1(2.0 * log_a))
    first = (jnp.arange(s) == 0)[None, :, None]
    mult = jnp.where(first, 1.0, mult)
    u = mult * (i * xf)

    def combine(left, right):
        a1, b1 = left
        a2, b2 = right
        return a1 * a2, a2 * b1 + b2

    _, h = lax.associative_scan(combine, (a, u), axis=1)
    return h.astype(x.dtype)


def _fwd_setup_inputs(seed: int = 0) -> dict:
    key = jax.random.key(seed)
    ks = jax.random.split(key, 24)
    f32 = jnp.float32

    def nrm(k, shape, fan_in):
        return jax.random.normal(k, shape, f32) * (fan_in ** -0.5)

    def gain(k, n):
        return 1.0 + 0.05 * jax.random.normal(k, (DEPTH, n), f32)

    u = jax.random.uniform(ks[14], (DEPTH, LRU_WIDTH), f32, 0.9, 0.999)
    a0 = u ** (1.0 / LRU_C)
    lam = jnp.log(a0) - jnp.log1p(-a0)

    return {
        "x": jax.random.normal(ks[0], (BATCH, SEQ, D_MODEL), f32),
        "norm_mix_pre": gain(ks[1], D_MODEL),
        "norm_mix_post": gain(ks[2], D_MODEL),
        "norm_ffn_pre": gain(ks[3], D_MODEL),
        "norm_ffn_post": gain(ks[4], D_MODEL),
        "w_in": nrm(ks[5], (DEPTH, D_MODEL, IN_COLS), D_MODEL),
        "conv_short_w": nrm(ks[6], (DEPTH, SHORT_CONV_K, CONV_WIDTH), SHORT_CONV_K),
        "w_conv_branch": nrm(ks[7], (DEPTH, CONV_WIDTH, D_MODEL), CONV_WIDTH),
        "lru_conv_w": nrm(ks[8], (DEPTH, LRU_CONV_K, LRU_WIDTH), LRU_CONV_K),
        "lru_conv_b": 0.02 * jax.random.normal(ks[9], (DEPTH, LRU_WIDTH), f32),
        "lru_wa": nrm(ks[10], (DEPTH, LRU_HEADS, LRU_HEAD_DIM, LRU_HEAD_DIM), LRU_HEAD_DIM),
        "lru_ba": 0.02 * jax.random.normal(ks[11], (DEPTH, LRU_HEADS, LRU_HEAD_DIM), f32),
        "lru_wx": nrm(ks[12], (DEPTH, LRU_HEADS, LRU_HEAD_DIM, LRU_HEAD_DIM), LRU_HEAD_DIM),
        "lru_bx": 0.02 * jax.random.normal(ks[13], (DEPTH, LRU_HEADS, LRU_HEAD_DIM), f32),
        "lru_lambda": lam,
        "w_lru_branch": nrm(ks[15], (DEPTH, LRU_WIDTH, D_MODEL), LRU_WIDTH),
        "w_out": nrm(ks[16], (DEPTH, D_MODEL, D_MODEL), D_MODEL),
        "ffn_w_up": nrm(ks[17], (DEPTH, D_MODEL, 2 * D_FF), D_MODEL),
        "ffn_conv_w": nrm(ks[18], (DEPTH, FFN_CONV_K, 2 * D_FF), FFN_CONV_K),
        "ffn_conv_b": 0.02 * jax.random.normal(ks[19], (DEPTH, 2 * D_FF), f32),
        "ffn_w_down": nrm(ks[20], (DEPTH, D_FF, D_MODEL), D_FF),
    }


def _fwd_reference(x, norm_mix_pre, norm_mix_post, norm_ffn_pre, norm_ffn_post, w_in, conv_short_w,
              w_conv_branch, lru_conv_w, lru_conv_b, lru_wa, lru_ba, lru_wx, lru_bx, lru_lambda,
              w_lru_branch, w_out, ffn_w_up, ffn_conv_w, ffn_conv_b, ffn_w_down):
    for l in range(DEPTH):
        h = rmsnorm(x, norm_mix_pre[l])
        proj = jnp.einsum("bsd,dc->bsc", h, w_in[l])
        c_b, c_c, c_x, l_x, l_y, g_conv, g_lru = jnp.split(proj, SPLITS, axis=-1)
        y_a = c_b * causal_dwconv(c_c * c_x, conv_short_w[l])
        xl = causal_dwconv(l_x, lru_conv_w[l], lru_conv_b[l])
        hl = rg_lru(xl, lru_wa[l], lru_ba[l], lru_wx[l], lru_bx[l], lru_lambda[l])
        y_b = hl * jax.nn.gelu(l_y, approximate=True)
        merged = (jax.nn.sigmoid(g_conv) * jnp.einsum("bsc,cd->bsd", y_a, w_conv_branch[l])
                  + jax.nn.sigmoid(g_lru) * jnp.einsum("bsc,cd->bsd", y_b, w_lru_branch[l]))
        mix = jnp.einsum("bsd,de->bse", merged, w_out[l])
        x = x + rmsnorm(mix, norm_mix_post[l])
        h = rmsnorm(x, norm_ffn_pre[l])
        up = jnp.einsum("bsd,df->bsf", h, ffn_w_up[l])
        up = causal_dwconv(up, ffn_conv_w[l], ffn_conv_b[l])
        gate, val = jnp.split(up, 2, axis=-1)
        f = jax.nn.gelu(gate, approximate=True) * val
        out = jnp.einsum("bsf,fd->bsd", f, ffn_w_down[l])
        x = x + rmsnorm(out, norm_ffn_post[l])
    return x


import jax as _jax
import jax.numpy as _jnp

TWIN_FORMAT = 'train_step'
FWD_PARAMS = ['x', 'norm_mix_pre', 'norm_mix_post', 'norm_ffn_pre', 'norm_ffn_post', 'w_in', 'conv_short_w', 'w_conv_branch', 'lru_conv_w', 'lru_conv_b', 'lru_wa', 'lru_ba', 'lru_wx', 'lru_bx', 'lru_lambda', 'w_lru_branch', 'w_out', 'ffn_w_up', 'ffn_conv_w', 'ffn_conv_b', 'ffn_w_down']
TWIN_WEIGHTS = ['norm_mix_pre', 'norm_mix_post', 'norm_ffn_pre', 'norm_ffn_post', 'w_in', 'conv_short_w', 'w_conv_branch', 'lru_conv_w', 'lru_conv_b', 'lru_wa', 'lru_ba', 'lru_wx', 'lru_bx', 'lru_lambda', 'w_lru_branch', 'w_out', 'ffn_w_up', 'ffn_conv_w', 'ffn_conv_b', 'ffn_w_down']
TWIN_DIFF_INPUT = 'x'
TWIN_INPUTS = ['x', 'norm_mix_pre', 'norm_mix_post', 'norm_ffn_pre', 'norm_ffn_post', 'w_in', 'conv_short_w', 'w_conv_branch', 'lru_conv_w', 'lru_conv_b', 'lru_wa', 'lru_ba', 'lru_wx', 'lru_bx', 'lru_lambda', 'w_lru_branch', 'w_out', 'ffn_w_up', 'ffn_conv_w', 'ffn_conv_b', 'ffn_w_down', 'loss_target', 'm_norm_mix_pre', 'm_norm_mix_post', 'm_norm_ffn_pre', 'm_norm_ffn_post', 'm_w_in', 'm_conv_short_w', 'm_w_conv_branch', 'm_lru_conv_w', 'm_lru_conv_b', 'm_lru_wa', 'm_lru_ba', 'm_lru_wx', 'm_lru_bx', 'm_lru_lambda', 'm_w_lru_branch', 'm_w_out', 'm_ffn_w_up', 'm_ffn_conv_w', 'm_ffn_conv_b', 'm_ffn_w_down', 'v_norm_mix_pre', 'v_norm_mix_post', 'v_norm_ffn_pre', 'v_norm_ffn_post', 'v_w_in', 'v_conv_short_w', 'v_w_conv_branch', 'v_lru_conv_w', 'v_lru_conv_b', 'v_lru_wa', 'v_lru_ba', 'v_lru_wx', 'v_lru_bx', 'v_lru_lambda', 'v_w_lru_branch', 'v_w_out', 'v_ffn_w_up', 'v_ffn_conv_w', 'v_ffn_conv_b', 'v_ffn_w_down']
TWIN_OUTPUTS = ['loss', 'grad_x', 'grad_norm_mix_pre', 'grad_norm_mix_post', 'grad_norm_ffn_pre', 'grad_norm_ffn_post', 'grad_w_in', 'grad_conv_short_w', 'grad_w_conv_branch', 'grad_lru_conv_w', 'grad_lru_conv_b', 'grad_lru_wa', 'grad_lru_ba', 'grad_lru_wx', 'grad_lru_bx', 'grad_lru_lambda', 'grad_w_lru_branch', 'grad_w_out', 'grad_ffn_w_up', 'grad_ffn_conv_w', 'grad_ffn_conv_b', 'grad_ffn_w_down', 'delta_norm_mix_pre', 'delta_norm_mix_post', 'delta_norm_ffn_pre', 'delta_norm_ffn_post', 'delta_w_in', 'delta_conv_short_w', 'delta_w_conv_branch', 'delta_lru_conv_w', 'delta_lru_conv_b', 'delta_lru_wa', 'delta_lru_ba', 'delta_lru_wx', 'delta_lru_bx', 'delta_lru_lambda', 'delta_w_lru_branch', 'delta_w_out', 'delta_ffn_w_up', 'delta_ffn_conv_w', 'delta_ffn_conv_b', 'delta_ffn_w_down', 'new_m_norm_mix_pre', 'new_m_norm_mix_post', 'new_m_norm_ffn_pre', 'new_m_norm_ffn_post', 'new_m_w_in', 'new_m_conv_short_w', 'new_m_w_conv_branch', 'new_m_lru_conv_w', 'new_m_lru_conv_b', 'new_m_lru_wa', 'new_m_lru_ba', 'new_m_lru_wx', 'new_m_lru_bx', 'new_m_lru_lambda', 'new_m_w_lru_branch', 'new_m_w_out', 'new_m_ffn_w_up', 'new_m_ffn_conv_w', 'new_m_ffn_conv_b', 'new_m_ffn_w_down', 'new_v_norm_mix_pre', 'new_v_norm_mix_post', 'new_v_norm_ffn_pre', 'new_v_norm_ffn_post', 'new_v_w_in', 'new_v_conv_short_w', 'new_v_w_conv_branch', 'new_v_lru_conv_w', 'new_v_lru_conv_b', 'new_v_lru_wa', 'new_v_lru_ba', 'new_v_lru_wx', 'new_v_lru_bx', 'new_v_lru_lambda', 'new_v_w_lru_branch', 'new_v_w_out', 'new_v_ffn_w_up', 'new_v_ffn_conv_w', 'new_v_ffn_conv_b', 'new_v_ffn_w_down']
TWIN_LEAF_KINDS = {'loss': 'loss', 'grad_x': 'grad_x', 'grad_norm_mix_pre': 'grad_w', 'grad_norm_mix_post': 'grad_w', 'grad_norm_ffn_pre': 'grad_w', 'grad_norm_ffn_post': 'grad_w', 'grad_w_in': 'grad_w', 'grad_conv_short_w': 'grad_w', 'grad_w_conv_branch': 'grad_w', 'grad_lru_conv_w': 'grad_w', 'grad_lru_conv_b': 'grad_w', 'grad_lru_wa': 'grad_w', 'grad_lru_ba': 'grad_w', 'grad_lru_wx': 'grad_w', 'grad_lru_bx': 'grad_w', 'grad_lru_lambda': 'grad_w', 'grad_w_lru_branch': 'grad_w', 'grad_w_out': 'grad_w', 'grad_ffn_w_up': 'grad_w', 'grad_ffn_conv_w': 'grad_w', 'grad_ffn_conv_b': 'grad_w', 'grad_ffn_w_down': 'grad_w', 'delta_norm_mix_pre': 'delta_w', 'delta_norm_mix_post': 'delta_w', 'delta_norm_ffn_pre': 'delta_w', 'delta_norm_ffn_post': 'delta_w', 'delta_w_in': 'delta_w', 'delta_conv_short_w': 'delta_w', 'delta_w_conv_branch': 'delta_w', 'delta_lru_conv_w': 'delta_w', 'delta_lru_conv_b': 'delta_w', 'delta_lru_wa': 'delta_w', 'delta_lru_ba': 'delta_w', 'delta_lru_wx': 'delta_w', 'delta_lru_bx': 'delta_w', 'delta_lru_lambda': 'delta_w', 'delta_w_lru_branch': 'delta_w', 'delta_w_out': 'delta_w', 'delta_ffn_w_up': 'delta_w', 'delta_ffn_conv_w': 'delta_w', 'delta_ffn_conv_b': 'delta_w', 'delta_ffn_w_down': 'delta_w', 'new_m_norm_mix_pre': 'new_m', 'new_m_norm_mix_post': 'new_m', 'new_m_norm_ffn_pre': 'new_m', 'new_m_norm_ffn_post': 'new_m', 'new_m_w_in': 'new_m', 'new_m_conv_short_w': 'new_m', 'new_m_w_conv_branch': 'new_m', 'new_m_lru_conv_w': 'new_m', 'new_m_lru_conv_b': 'new_m', 'new_m_lru_wa': 'new_m', 'new_m_lru_ba': 'new_m', 'new_m_lru_wx': 'new_m', 'new_m_lru_bx': 'new_m', 'new_m_lru_lambda': 'new_m', 'new_m_w_lru_branch': 'new_m', 'new_m_w_out': 'new_m', 'new_m_ffn_w_up': 'new_m', 'new_m_ffn_conv_w': 'new_m', 'new_m_ffn_conv_b': 'new_m', 'new_m_ffn_w_down': 'new_m', 'new_v_norm_mix_pre': 'new_v', 'new_v_norm_mix_post': 'new_v', 'new_v_norm_ffn_pre': 'new_v', 'new_v_norm_ffn_post': 'new_v', 'new_v_w_in': 'new_v', 'new_v_conv_short_w': 'new_v', 'new_v_w_conv_branch': 'new_v', 'new_v_lru_conv_w': 'new_v', 'new_v_lru_conv_b': 'new_v', 'new_v_lru_wa': 'new_v', 'new_v_lru_ba': 'new_v', 'new_v_lru_wx': 'new_v', 'new_v_lru_bx': 'new_v', 'new_v_lru_lambda': 'new_v', 'new_v_w_lru_branch': 'new_v', 'new_v_w_out': 'new_v', 'new_v_ffn_w_up': 'new_v', 'new_v_ffn_conv_w': 'new_v', 'new_v_ffn_conv_b': 'new_v', 'new_v_ffn_w_down': 'new_v'}


def _forward(args):
    return _fwd_reference(*[args[k] for k in FWD_PARAMS])


def _output_shape():
    out = _jax.eval_shape(lambda: _forward(_fwd_setup_inputs(0)))
    return out.shape, out.dtype

N_MICROBATCH = 1
ADAM_LR = 0.001
ADAM_B1 = 0.9
ADAM_B2 = 0.999
ADAM_EPS = 1e-08
ADAM_WD = 0.01
ADAM_STEP = 10
PER_EXAMPLE_BATCH_AXIS = {'x': 0, 'loss_target': 0}
SHARED_INPUTS = []
_WEIGHT_DTYPES = {'norm_mix_pre': _jnp.float32, 'norm_mix_post': _jnp.float32, 'norm_ffn_pre': _jnp.float32, 'norm_ffn_post': _jnp.float32, 'w_in': _jnp.float32, 'conv_short_w': _jnp.float32, 'w_conv_branch': _jnp.float32, 'lru_conv_w': _jnp.float32, 'lru_conv_b': _jnp.float32, 'lru_wa': _jnp.float32, 'lru_ba': _jnp.float32, 'lru_wx': _jnp.float32, 'lru_bx': _jnp.float32, 'lru_lambda': _jnp.float32, 'w_lru_branch': _jnp.float32, 'w_out': _jnp.float32, 'ffn_w_up': _jnp.float32, 'ffn_conv_w': _jnp.float32, 'ffn_conv_b': _jnp.float32, 'ffn_w_down': _jnp.float32}
MOMENT_SCALE = {'norm_mix_pre': 5.814019e-01, 'norm_mix_post': 1.603664e+01, 'norm_ffn_pre': 3.512444e-01, 'norm_ffn_post': 1.595593e+01, 'w_in': 2.067850e-01, 'conv_short_w': 3.011765e-01, 'w_conv_branch': 2.999619e-01, 'lru_conv_w': 1.239461e-01, 'lru_conv_b': 1.833417e+00, 'lru_wa': 3.742103e-02, 'lru_ba': 2.978942e-02, 'lru_wx': 6.733611e-02, 'lru_bx': 4.489041e-02, 'lru_lambda': 5.243269e-02, 'w_lru_branch': 1.295661e-01, 'w_out': 3.412052e-01, 'ffn_w_up': 1.428337e-01, 'ffn_conv_w': 1.488039e-01, 'ffn_conv_b': 2.460552e-01, 'ffn_w_down': 2.726249e-01}


def _to_microbatches(a, axis):
    t = _jnp.moveaxis(a, axis, 0)
    t = t.reshape((N_MICROBATCH, t.shape[0] // N_MICROBATCH) + t.shape[1:])
    return _jnp.moveaxis(t, 1, axis + 1)


def setup_inputs(seed: int = 0) -> dict:
    inp = _fwd_setup_inputs(seed)
    key = _jax.random.fold_in(_jax.random.key(seed), 7919)
    shape, _ = _output_shape()
    out = dict(inp)
    out["loss_target"] = _jax.random.normal(_jax.random.fold_in(key, 0), shape, _jnp.float32)
    for i, name in enumerate(TWIN_WEIGHTS):
        w = inp[name].astype(_jnp.float32)
        if MOMENT_SCALE is None:
            s = _jnp.sqrt(_jnp.mean(_jnp.square(w)) + 1e-30)
        else:
            s = MOMENT_SCALE[name]
        km, kv = _jax.random.split(_jax.random.fold_in(key, i + 1))
        out[name] = w
        out["m_" + name] = s * _jax.random.normal(km, w.shape, _jnp.float32)
        out["v_" + name] = (s * s) * _jax.random.uniform(kv, w.shape, _jnp.float32, 0.5, 1.5)
    if N_MICROBATCH > 1:
        for name, axis in PER_EXAMPLE_BATCH_AXIS.items():
            out[name] = _to_microbatches(out[name], axis)
    return {'x': out['x'], 'norm_mix_pre': out['norm_mix_pre'], 'norm_mix_post': out['norm_mix_post'], 'norm_ffn_pre': out['norm_ffn_pre'], 'norm_ffn_post': out['norm_ffn_post'], 'w_in': out['w_in'], 'conv_short_w': out['conv_short_w'], 'w_conv_branch': out['w_conv_branch'], 'lru_conv_w': out['lru_conv_w'], 'lru_conv_b': out['lru_conv_b'], 'lru_wa': out['lru_wa'], 'lru_ba': out['lru_ba'], 'lru_wx': out['lru_wx'], 'lru_bx': out['lru_bx'], 'lru_lambda': out['lru_lambda'], 'w_lru_branch': out['w_lru_branch'], 'w_out': out['w_out'], 'ffn_w_up': out['ffn_w_up'], 'ffn_conv_w': out['ffn_conv_w'], 'ffn_conv_b': out['ffn_conv_b'], 'ffn_w_down': out['ffn_w_down'], 'loss_target': out['loss_target'], 'm_norm_mix_pre': out['m_norm_mix_pre'], 'm_norm_mix_post': out['m_norm_mix_post'], 'm_norm_ffn_pre': out['m_norm_ffn_pre'], 'm_norm_ffn_post': out['m_norm_ffn_post'], 'm_w_in': out['m_w_in'], 'm_conv_short_w': out['m_conv_short_w'], 'm_w_conv_branch': out['m_w_conv_branch'], 'm_lru_conv_w': out['m_lru_conv_w'], 'm_lru_conv_b': out['m_lru_conv_b'], 'm_lru_wa': out['m_lru_wa'], 'm_lru_ba': out['m_lru_ba'], 'm_lru_wx': out['m_lru_wx'], 'm_lru_bx': out['m_lru_bx'], 'm_lru_lambda': out['m_lru_lambda'], 'm_w_lru_branch': out['m_w_lru_branch'], 'm_w_out': out['m_w_out'], 'm_ffn_w_up': out['m_ffn_w_up'], 'm_ffn_conv_w': out['m_ffn_conv_w'], 'm_ffn_conv_b': out['m_ffn_conv_b'], 'm_ffn_w_down': out['m_ffn_w_down'], 'v_norm_mix_pre': out['v_norm_mix_pre'], 'v_norm_mix_post': out['v_norm_mix_post'], 'v_norm_ffn_pre': out['v_norm_ffn_pre'], 'v_norm_ffn_post': out['v_norm_ffn_post'], 'v_w_in': out['v_w_in'], 'v_conv_short_w': out['v_conv_short_w'], 'v_w_conv_branch': out['v_w_conv_branch'], 'v_lru_conv_w': out['v_lru_conv_w'], 'v_lru_conv_b': out['v_lru_conv_b'], 'v_lru_wa': out['v_lru_wa'], 'v_lru_ba': out['v_lru_ba'], 'v_lru_wx': out['v_lru_wx'], 'v_lru_bx': out['v_lru_bx'], 'v_lru_lambda': out['v_lru_lambda'], 'v_w_lru_branch': out['v_w_lru_branch'], 'v_w_out': out['v_w_out'], 'v_ffn_w_up': out['v_ffn_w_up'], 'v_ffn_conv_w': out['v_ffn_conv_w'], 'v_ffn_conv_b': out['v_ffn_conv_b'], 'v_ffn_w_down': out['v_ffn_w_down']}


def _loss(weights, diff, rest, loss_target):
    with _jax.named_scope("forward"):
        args = {**rest, TWIN_DIFF_INPUT: diff, **{k: w.astype(_WEIGHT_DTYPES[k]) for k, w in weights.items()}}
        y = _forward(args)
    with _jax.named_scope("loss_head"):
        err = _jnp.square(y.astype(_jnp.float32) - loss_target)
        return 0.5 * _jnp.sum(_jnp.mean(err, axis=-1)) if err.ndim else 0.5 * err


def _adamw(w, g, m, v):
    m = ADAM_B1 * m + (1.0 - ADAM_B1) * g
    v = ADAM_B2 * v + (1.0 - ADAM_B2) * _jnp.square(g)
    m_hat = m / (1.0 - ADAM_B1 ** ADAM_STEP)
    v_hat = v / (1.0 - ADAM_B2 ** ADAM_STEP)
    delta = -ADAM_LR * (m_hat / (_jnp.sqrt(v_hat) + ADAM_EPS) + ADAM_WD * w)
    return delta, m, v


def reference(x, norm_mix_pre, norm_mix_post, norm_ffn_pre, norm_ffn_post, w_in, conv_short_w, w_conv_branch, lru_conv_w, lru_conv_b, lru_wa, lru_ba, lru_wx, lru_bx, lru_lambda, w_lru_branch, w_out, ffn_w_up, ffn_conv_w, ffn_conv_b, ffn_w_down, loss_target, m_norm_mix_pre, m_norm_mix_post, m_norm_ffn_pre, m_norm_ffn_post, m_w_in, m_conv_short_w, m_w_conv_branch, m_lru_conv_w, m_lru_conv_b, m_lru_wa, m_lru_ba, m_lru_wx, m_lru_bx, m_lru_lambda, m_w_lru_branch, m_w_out, m_ffn_w_up, m_ffn_conv_w, m_ffn_conv_b, m_ffn_w_down, v_norm_mix_pre, v_norm_mix_post, v_norm_ffn_pre, v_norm_ffn_post, v_w_in, v_conv_short_w, v_w_conv_branch, v_lru_conv_w, v_lru_conv_b, v_lru_wa, v_lru_ba, v_lru_wx, v_lru_bx, v_lru_lambda, v_w_lru_branch, v_w_out, v_ffn_w_up, v_ffn_conv_w, v_ffn_conv_b, v_ffn_w_down):
    given = dict(x=x, norm_mix_pre=norm_mix_pre, norm_mix_post=norm_mix_post, norm_ffn_pre=norm_ffn_pre, norm_ffn_post=norm_ffn_post, w_in=w_in, conv_short_w=conv_short_w, w_conv_branch=w_conv_branch, lru_conv_w=lru_conv_w, lru_conv_b=lru_conv_b, lru_wa=lru_wa, lru_ba=lru_ba, lru_wx=lru_wx, lru_bx=lru_bx, lru_lambda=lru_lambda, w_lru_branch=w_lru_branch, w_out=w_out, ffn_w_up=ffn_w_up, ffn_conv_w=ffn_conv_w, ffn_conv_b=ffn_conv_b, ffn_w_down=ffn_w_down, loss_target=loss_target, m_norm_mix_pre=m_norm_mix_pre, m_norm_mix_post=m_norm_mix_post, m_norm_ffn_pre=m_norm_ffn_pre, m_norm_ffn_post=m_norm_ffn_post, m_w_in=m_w_in, m_conv_short_w=m_conv_short_w, m_w_conv_branch=m_w_conv_branch, m_lru_conv_w=m_lru_conv_w, m_lru_conv_b=m_lru_conv_b, m_lru_wa=m_lru_wa, m_lru_ba=m_lru_ba, m_lru_wx=m_lru_wx, m_lru_bx=m_lru_bx, m_lru_lambda=m_lru_lambda, m_w_lru_branch=m_w_lru_branch, m_w_out=m_w_out, m_ffn_w_up=m_ffn_w_up, m_ffn_conv_w=m_ffn_conv_w, m_ffn_conv_b=m_ffn_conv_b, m_ffn_w_down=m_ffn_w_down, v_norm_mix_pre=v_norm_mix_pre, v_norm_mix_post=v_norm_mix_post, v_norm_ffn_pre=v_norm_ffn_pre, v_norm_ffn_post=v_norm_ffn_post, v_w_in=v_w_in, v_conv_short_w=v_conv_short_w, v_w_conv_branch=v_w_conv_branch, v_lru_conv_w=v_lru_conv_w, v_lru_conv_b=v_lru_conv_b, v_lru_wa=v_lru_wa, v_lru_ba=v_lru_ba, v_lru_wx=v_lru_wx, v_lru_bx=v_lru_bx, v_lru_lambda=v_lru_lambda, v_w_lru_branch=v_w_lru_branch, v_w_out=v_w_out, v_ffn_w_up=v_ffn_w_up, v_ffn_conv_w=v_ffn_conv_w, v_ffn_conv_b=v_ffn_conv_b, v_ffn_w_down=v_ffn_w_down)
    weights = {n: given[n] for n in TWIN_WEIGHTS}
    shared = {n: given[n] for n in SHARED_INPUTS}
    per_example = {n: given[n] for n in ['x']}
    grad_fn = _jax.value_and_grad(_loss, argnums=(0, 1))

    def one_microbatch(ex, loss_target):
        ex = dict(ex)
        diff = ex.pop(TWIN_DIFF_INPUT)
        return grad_fn(weights, diff, {**shared, **ex}, loss_target)

    if N_MICROBATCH == 1:
        loss, (grad_w, grad_x) = one_microbatch(per_example, given["loss_target"])
    else:
        def body(carry, xs):
            loss_sum, grad_sum = carry
            l_k, (gw_k, gx_k) = one_microbatch(xs[0], xs[1])
            with _jax.named_scope("update"):
                return (loss_sum + l_k, _jax.tree.map(_jnp.add, grad_sum, gw_k)), gx_k

        init = (_jnp.zeros((), _jnp.float32), _jax.tree.map(_jnp.zeros_like, weights))
        (loss, grad_w), grad_x = _jax.lax.scan(body, init, (per_example, given["loss_target"]))
    with _jax.named_scope("update"):
        delta_w, new_m, new_v = {}, {}, {}
        for n in TWIN_WEIGHTS:
            delta_w[n], new_m[n], new_v[n] = _adamw(weights[n], grad_w[n], given["m_" + n], given["v_" + n])
    return (loss, grad_x, *[grad_w[n] for n in TWIN_WEIGHTS], *[delta_w[n] for n in TWIN_WEIGHTS],
            *[new_m[n] for n in TWIN_WEIGHTS], *[new_v[n] for n in TWIN_WEIGHTS])
```

```python
import functools
import math

import jax
import jax.numpy as jnp
from jax import lax
from jax.experimental import pallas as pl
from jax.experimental.pallas import tpu as pltpu

F32 = jnp.float32
BF16 = jnp.bfloat16

D_MODEL = 1024
N_CHIPS = 4
N_SEG = 7
D_FF = 3 * D_MODEL
LRU_HEADS = 4
HEAD_DIM = D_MODEL // LRU_HEADS
LRU_C = 8.0
RMS_EPS = 1e-6
CW = 256
FW = 512
SUBLANES = 8
VMEM_LIMIT = 56 * 1024 * 1024

ADAM_LR = 0.001
ADAM_B1 = 0.9
ADAM_B2 = 0.999
ADAM_EPS = 1e-08
ADAM_WD = 0.01
ADAM_STEP = 10

_GELU_C = math.sqrt(2.0 / math.pi)
_GELU_K = 0.044715


def _params(**kw):
    return pltpu.CompilerParams(vmem_limit_bytes=VMEM_LIMIT, **kw)


def _sigmoid(x):
    return 1.0 / (1.0 + jnp.exp(-x))


def _gelu(x):
    t = jnp.tanh(_GELU_C * (x + _GELU_K * x * x * x))
    return 0.5 * x * (1.0 + t)


def _gelu_and_grad(x):
    x2 = x * x
    t = jnp.tanh(_GELU_C * (x + _GELU_K * x * x2))
    g = 0.5 * x * (1.0 + t)
    dg = 0.5 * (1.0 + t) + 0.5 * x * (1.0 - t * t) * _GELU_C * (1.0 + 3.0 * _GELU_K * x2)
    return g, dg


def _log_sigmoid(x):
    e = jnp.exp(-jnp.abs(x))
    u = 1.0 + e
    l1p = jnp.where(u == 1.0, e, jnp.log(u) * e / (u - 1.0))
    return jnp.minimum(x, 0.0) - l1p


def _neg_expm1(z):
    series = -z * (1.0 + z * (0.5 + z * (1.0 / 6.0 + z * (1.0 / 24.0 + z * (1.0 / 120.0 + z * (1.0 / 720.0))))))
    return jnp.where(z > -0.2, series, 1.0 - jnp.exp(z))


def _rows(shape):
    return lax.broadcasted_iota(jnp.int32, shape, 0)


def _shift_down(x, k):
    return jnp.where(_rows(x.shape) >= k, pltpu.roll(x, k, 0), 0.0)


def _shift_up(x, k):
    n = x.shape[0]
    return jnp.where(_rows(x.shape) < n - k, pltpu.roll(x, n - k, 0), 0.0)


def _causal_conv(x, w_ref, b=None):
    k_width = w_ref.shape[0]
    y = w_ref[k_width - 1:k_width, :] * x
    for j in range(1, k_width):
        y = y + w_ref[k_width - 1 - j:k_width - j, :] * _shift_down(x, j)
    if b is not None:
        y = y + b
    return y


def _causal_conv_t(dy, w_ref):
    k_width = w_ref.shape[0]
    dx = w_ref[k_width - 1:k_width, :] * dy
    for j in range(1, k_width):
        dx = dx + w_ref[k_width - 1 - j:k_width - j, :] * _shift_up(dy, j)
    return dx


def _conv_wgrad(dy, x, k_width):
    rows = [None] * k_width
    for j in range(k_width):
        xs = x if j == 0 else _shift_down(x, j)
        rows[k_width - 1 - j] = jnp.sum(dy * xs, axis=0, keepdims=True)
    return jnp.concatenate(rows, axis=0)


def _dot(a, b):
    return jnp.dot(a, b, preferred_element_type=F32)


def _dot_nt(a, b):
    return lax.dot_general(a, b, (((1,), (1,)), ((), ())), preferred_element_type=F32)


def _dot_tn(a, b):
    return lax.dot_general(a, b, (((0,), (0,)), ((), ())), preferred_element_type=F32)


def _rms_stats(x):
    r = lax.rsqrt(jnp.mean(x * x, axis=-1, keepdims=True) + RMS_EPS)
    return x * r, r


def _rms_bwd(n, r, g, dy):
    dn = dy * g
    dx = r * (dn - n * jnp.mean(dn * n, axis=-1, keepdims=True))
    return dx, dy * n


def _scan_forward(a_ref, b_ref, h_ref):
    n, c = a_ref.shape
    row = lax.broadcasted_iota(jnp.int32, (SUBLANES, c), 0)

    def group(g, carry):
        r0 = pl.multiple_of(g * SUBLANES, SUBLANES)
        a = a_ref[pl.ds(r0, SUBLANES), :]
        b = b_ref[pl.ds(r0, SUBLANES), :]
        for k in (1, 2, 4):
            ap = jnp.where(row >= k, pltpu.roll(a, k, 0), 1.0)
            bp = jnp.where(row >= k, pltpu.roll(b, k, 0), 0.0)
            b = a * bp + b
            a = a * ap
        h = a * carry + b
        h_ref[pl.ds(r0, SUBLANES), :] = h
        return h[SUBLANES - 1:SUBLANES, :]

    lax.fori_loop(0, n // SUBLANES, group, jnp.zeros((1, c), F32))


def _scan_backward(c_ref, b_ref, g_ref):
    n, ch = c_ref.shape
    row = lax.broadcasted_iota(jnp.int32, (SUBLANES, ch), 0)
    n_groups = n // SUBLANES

    def group(i, carry):
        r0 = pl.multiple_of((n_groups - 1 - i) * SUBLANES, SUBLANES)
        a = c_ref[pl.ds(r0, SUBLANES), :]
        b = b_ref[pl.ds(r0, SUBLANES), :]
        for k in (1, 2, 4):
            keep = row < SUBLANES - k
            ap = jnp.where(keep, pltpu.roll(a, SUBLANES - k, 0), 1.0)
            bp = jnp.where(keep, pltpu.roll(b, SUBLANES - k, 0), 0.0)
            b = a * bp + b
            a = a * ap
        g = a * carry + b
        g_ref[pl.ds(r0, SUBLANES), :] = g
        return g[0:1, :]

    lax.fori_loop(0, n_groups, group, jnp.zeros((1, ch), F32))


def _token_tile(s):
    return min(s, 512)


def norm_in(x, g):
    s, d = x.shape
    t = _token_tile(s)

    def body(x_ref, g_ref, o_ref):
        n, _ = _rms_stats(x_ref[...])
        o_ref[...] = (n * g_ref[...]).astype(BF16)

    return pl.pallas_call(
        body, name="norm_in", grid=(s // t,),
        in_specs=[pl.BlockSpec((t, d), lambda i: (i, 0)), pl.BlockSpec((1, d), lambda i: (0, 0))],
        out_specs=pl.BlockSpec((t, d), lambda i: (i, 0)),
        out_shape=jax.ShapeDtypeStruct((s, d), BF16),
        compiler_params=_params(),
    )(x, g)


def matmul_cols(a, w4, name):
    m, k = a.shape
    nj, _, ns = w4.shape
    nb = ns // CW

    def body(a_ref, w_ref, o_ref):
        o_ref[...] = _dot(a_ref[...], w_ref[0])

    return pl.pallas_call(
        body, name=name, grid=(nj, nb),
        in_specs=[pl.BlockSpec((m, k), lambda j, b: (0, 0)),
                  pl.BlockSpec((1, k, CW), lambda j, b: (j, 0, b))],
        out_specs=pl.BlockSpec((m, CW), lambda j, b: (0, j * nb + b)),
        out_shape=jax.ShapeDtypeStruct((m, nj * ns), F32),
        compiler_params=_params(),
    )(a, w4)


def mix_conv_fwd(proj, ws):
    s = proj.shape[0]
    nblk = D_MODEL // CW

    def body(cb_ref, cc_ref, cx_ref, ws_ref, q_ref, ya_ref):
        q = _causal_conv(cc_ref[...] * cx_ref[...], ws_ref)
        q_ref[...] = q
        ya_ref[...] = (cb_ref[...] * q).astype(BF16)

    seg = lambda k: pl.BlockSpec((s, CW), lambda c, k=k: (0, k * nblk + c))
    return pl.pallas_call(
        body, name="mix_conv_fwd", grid=(nblk,),
        in_specs=[seg(0), seg(1), seg(2), pl.BlockSpec((3, CW), lambda c: (0, c))],
        out_specs=[pl.BlockSpec((s, CW), lambda c: (0, c))] * 2,
        out_shape=[jax.ShapeDtypeStruct((s, D_MODEL), F32), jax.ShapeDtypeStruct((s, D_MODEL), BF16)],
        compiler_params=_params(),
    )(proj, proj, proj, ws)


def _lru_gates(r, ls):
    log_a = LRU_C * r * ls
    a = jnp.exp(log_a)
    mult = jnp.sqrt(_neg_expm1(2.0 * log_a))
    mult = jnp.where(_rows(r.shape) == 0, 1.0, mult)
    return a, mult


def mix_lru_fwd(proj, wl, bl, wa, ba, wx, bx, lam):
    s = proj.shape[0]
    nblk = D_MODEL // CW

    def body(lx_ref, ly_ref, wl_ref, bl_ref, wa_ref, ba_ref, wx_ref, bx_ref, lam_ref,
             xl_ref, r_ref, i_ref, h_ref, yb_ref, a_scr, u_scr):
        xl = _causal_conv(lx_ref[...], wl_ref, bl_ref[...])
        xl_ref[...] = xl
        xlb = xl.astype(BF16)
        r = _sigmoid(_dot(xlb, wa_ref[0]) + ba_ref[...])
        i = _sigmoid(_dot(xlb, wx_ref[0]) + bx_ref[...])
        r_ref[...] = r
        i_ref[...] = i
        a, mult = _lru_gates(r, _log_sigmoid(lam_ref[...]))
        a_scr[...] = a
        u_scr[...] = mult * i * xl
        _scan_forward(a_scr, u_scr, h_ref)
        yb_ref[...] = (h_ref[...] * _gelu(ly_ref[...])).astype(BF16)

    blk = lambda k: pl.BlockSpec((s, CW), lambda c, k=k: (0, k * nblk + c))
    vec = pl.BlockSpec((1, CW), lambda c: (0, c))
    mat = pl.BlockSpec((1, CW, CW), lambda c: (c, 0, 0))
    out = pl.BlockSpec((s, CW), lambda c: (0, c))
    f = jax.ShapeDtypeStruct((s, D_MODEL), F32)
    return pl.pallas_call(
        body, name="mix_lru_fwd", grid=(nblk,),
        in_specs=[blk(3), blk(4), pl.BlockSpec((4, CW), lambda c: (0, c)), vec, mat, vec, mat, vec, vec],
        out_specs=[out] * 5,
        out_shape=[f, f, f, f, jax.ShapeDtypeStruct((s, D_MODEL), BF16)],
        scratch_shapes=[pltpu.VMEM((s, CW), F32), pltpu.VMEM((s, CW), F32)],
        compiler_params=_params(),
    )(proj, proj, wl, bl, wa, ba, wx, bx, lam)


def branch_merge_fwd(ya, yb, wcb, wlb, proj):
    s = ya.shape[0]
    nblk = D_MODEL // CW

    def body(ya_ref, yb_ref, wcb_ref, wlb_ref, gc_ref, gl_ref, a_ref, b_ref, m_ref):
        a = _dot(ya_ref[...], wcb_ref[...])
        b = _dot(yb_ref[...], wlb_ref[...])
        a_ref[...] = a
        b_ref[...] = b
        m_ref[...] = (_sigmoid(gc_ref[...]) * a + _sigmoid(gl_ref[...]) * b).astype(BF16)

    res = pl.BlockSpec((s, D_MODEL), lambda n: (0, 0))
    wcol = pl.BlockSpec((D_MODEL, CW), lambda n: (0, n))
    blk = lambda k: pl.BlockSpec((s, CW), lambda n, k=k: (0, k * nblk + n))
    out = pl.BlockSpec((s, CW), lambda n: (0, n))
    f = jax.ShapeDtypeStruct((s, D_MODEL), F32)
    return pl.pallas_call(
        body, name="branch_merge_fwd", grid=(nblk,),
        in_specs=[res, res, wcol, wcol, blk(5), blk(6)],
        out_specs=[out] * 3,
        out_shape=[f, f, jax.ShapeDtypeStruct((s, D_MODEL), BF16)],
        compiler_params=_params(),
    )(ya, yb, wcb, wlb, proj, proj)


def mix_out_fwd(merged, wout, x, g2, g3):
    s, d = x.shape
    t = _token_tile(s)

    def body(m_ref, w_ref, x_ref, g2_ref, g3_ref, mix_ref, x2_ref, h2_ref):
        mix = _dot(m_ref[...], w_ref[...])
        mix_ref[...] = mix
        n, _ = _rms_stats(mix)
        x2 = x_ref[...] + n * g2_ref[...]
        x2_ref[...] = x2
        n2, _ = _rms_stats(x2)
        h2_ref[...] = (n2 * g3_ref[...]).astype(BF16)

    tile = pl.BlockSpec((t, d), lambda i: (i, 0))
    vec = pl.BlockSpec((1, d), lambda i: (0, 0))
    f = jax.ShapeDtypeStruct((s, d), F32)
    return pl.pallas_call(
        body, name="mix_out_fwd", grid=(s // t,),
        in_specs=[tile, pl.BlockSpec((d, d), lambda i: (0, 0)), tile, vec, vec],
        out_specs=[tile] * 3,
        out_shape=[f, f, jax.ShapeDtypeStruct((s, d), BF16)],
        compiler_params=_params(),
    )(merged, wout, x, g2, g3)


def ffn_act_fwd(up, fw, fb):
    s = up.shape[0]
    nblk = D_FF // FW

    def body(ug_ref, uv_ref, wg_ref, wv_ref, bg_ref, bv_ref, f_ref):
        gate = _causal_conv(ug_ref[...], wg_ref, bg_ref[...])
        val = _causal_conv(uv_ref[...], wv_ref, bv_ref[...])
        f_ref[...] = (_gelu(gate) * val).astype(BF16)

    half = lambda h, rows: pl.BlockSpec((rows, FW), lambda n, h=h: (0, h * nblk + n))
    return pl.pallas_call(
        body, name="ffn_act_fwd", grid=(nblk,),
        in_specs=[half(0, s), half(1, s), half(0, 3), half(1, 3), half(0, 1), half(1, 1)],
        out_specs=pl.BlockSpec((s, FW), lambda n: (0, n)),
        out_shape=jax.ShapeDtypeStruct((s, D_FF), BF16),
        compiler_params=_params(),
    )(up, up, fw, fw, fb, fb)


def ffn_down_loss(f, wdown, x2, target, g4):
    s, d = x2.shape
    t = _token_tile(s)

    def body(f_ref, w_ref, x2_ref, tg_ref, g4_ref, dy_ref, dout_ref, loss_ref, dg4_ref):
        @pl.when(pl.program_id(0) == 0)
        def _():
            loss_ref[...] = jnp.zeros_like(loss_ref)
            dg4_ref[...] = jnp.zeros_like(dg4_ref)

        out = _dot(f_ref[...], w_ref[...])
        n, r = _rms_stats(out)
        err = x2_ref[...] + n * g4_ref[...] - tg_ref[...]
        loss_ref[...] += jnp.full(loss_ref.shape, (0.5 / d) * jnp.sum(err * err), F32)
        dy = err * (1.0 / d)
        dy_ref[...] = dy
        dout, dg = _rms_bwd(n, r, g4_ref[...], dy)
        dout_ref[...] = dout.astype(BF16)
        dg4_ref[...] += jnp.sum(dg, axis=0, keepdims=True)

    tile = pl.BlockSpec((t, d), lambda i: (i, 0))
    vec = pl.BlockSpec((1, d), lambda i: (0, 0))
    return pl.pallas_call(
        body, name="ffn_down_loss", grid=(s // t,),
        in_specs=[pl.BlockSpec((t, D_FF), lambda i: (i, 0)), pl.BlockSpec((D_FF, d), lambda i: (0, 0)), tile, tile, vec],
        out_specs=[tile, tile, pl.BlockSpec((1, 128), lambda i: (0, 0)), vec],
        out_shape=[jax.ShapeDtypeStruct((s, d), F32), jax.ShapeDtypeStruct((s, d), BF16),
                   jax.ShapeDtypeStruct((1, 128), F32), jax.ShapeDtypeStruct((1, d), F32)],
        compiler_params=_params(),
    )(f, wdown, x2, target, g4)


def ffn_bwd(dout, wdown, up, fw, fb):
    s = up.shape[0]
    nblk = D_FF // FW

    def body(do_ref, wd_ref, ug_ref, uv_ref, wg_ref, wv_ref, bg_ref, bv_ref,
             dug_ref, duv_ref, dwd_ref, dwg_ref, dwv_ref, dbg_ref, dbv_ref):
        do = do_ref[...]
        df = _dot_nt(do, wd_ref[...])
        ug = ug_ref[...]
        uv = uv_ref[...]
        gate = _causal_conv(ug, wg_ref, bg_ref[...])
        val = _causal_conv(uv, wv_ref, bv_ref[...])
        ge, dge = _gelu_and_grad(gate)
        dwd_ref[...] = _dot_tn((ge * val).astype(BF16), do).astype(BF16)
        dgate = df * val * dge
        dval = df * ge
        dug_ref[...] = _causal_conv_t(dgate, wg_ref).astype(BF16)
        duv_ref[...] = _causal_conv_t(dval, wv_ref).astype(BF16)
        dwg_ref[...] = _conv_wgrad(dgate, ug, 3)
        dwv_ref[...] = _conv_wgrad(dval, uv, 3)
        dbg_ref[...] = jnp.sum(dgate, axis=0, keepdims=True)
        dbv_ref[...] = jnp.sum(dval, axis=0, keepdims=True)

    half = lambda h, rows: pl.BlockSpec((rows, FW), lambda n, h=h: (0, h * nblk + n))
    own = lambda rows: pl.BlockSpec((rows, FW), lambda n: (0, n))
    act = jax.ShapeDtypeStruct((s, D_FF), BF16)
    taps = jax.ShapeDtypeStruct((3, D_FF), F32)
    bias = jax.ShapeDtypeStruct((1, D_FF), F32)
    return pl.pallas_call(
        body, name="ffn_bwd", grid=(nblk,),
        in_specs=[pl.BlockSpec((s, D_MODEL), lambda n: (0, 0)), pl.BlockSpec((FW, D_MODEL), lambda n: (n, 0)),
                  half(0, s), half(1, s), half(0, 3), half(1, 3), half(0, 1), half(1, 1)],
        out_specs=[own(s), own(s), pl.BlockSpec((FW, D_MODEL), lambda n: (n, 0)), own(3), own(3), own(1), own(1)],
        out_shape=[act, act, jax.ShapeDtypeStruct((D_FF, D_MODEL), BF16), taps, taps, bias, bias],
        compiler_params=_params(),
    )(dout, wdown, up, up, fw, fw, fb, fb)


def dgrad_wgrad_cols(dy, w4, a, name):
    m, k = a.shape
    nj, _, ns = w4.shape
    nb = ns // CW

    def body(dy_ref, w_ref, a_ref, da_ref, dw_ref):
        @pl.when((pl.program_id(0) == 0) & (pl.program_id(1) == 0))
        def _():
            da_ref[...] = jnp.zeros_like(da_ref)

        dyb = dy_ref[...]
        da_ref[...] += _dot_nt(dyb, w_ref[0])
        dw_ref[...] = _dot_tn(a_ref[...], dyb).astype(BF16)

    return pl.pallas_call(
        body, name=name, grid=(nj, nb),
        in_specs=[pl.BlockSpec((m, CW), lambda j, b: (0, j * nb + b)),
                  pl.BlockSpec((1, k, CW), lambda j, b: (j, 0, b)),
                  pl.BlockSpec((m, k), lambda j, b: (0, 0))],
        out_specs=[pl.BlockSpec((m, k), lambda j, b: (0, 0)),
                   pl.BlockSpec((k, CW), lambda j, b: (0, j * nb + b))],
        out_shape=[jax.ShapeDtypeStruct((m, k), F32), jax.ShapeDtypeStruct((k, nj * ns), BF16)],
        compiler_params=_params(),
    )(dy, w4, a)


def norms_mid_bwd(dh2, x2, dy, mix, g3, g2):
    s, d = x2.shape
    t = _token_tile(s)

    def body(dh2_ref, x2_ref, dy_ref, mix_ref, g3_ref, g2_ref, dx2_ref, dmix_ref, dg3_ref, dg2_ref):
        @pl.when(pl.program_id(0) == 0)
        def _():
            dg3_ref[...] = jnp.zeros_like(dg3_ref)
            dg2_ref[...] = jnp.zeros_like(dg2_ref)

        n3, r3 = _rms_stats(x2_ref[...])
        dx, dg3 = _rms_bwd(n3, r3, g3_ref[...], dh2_ref[...])
        dx2 = dy_ref[...] + dx
        dx2_ref[...] = dx2
        dg3_ref[...] += jnp.sum(dg3, axis=0, keepdims=True)
        n2, r2 = _rms_stats(mix_ref[...])
        dmix, dg2 = _rms_bwd(n2, r2, g2_ref[...], dx2)
        dmix_ref[...] = dmix.astype(BF16)
        dg2_ref[...] += jnp.sum(dg2, axis=0, keepdims=True)

    tile = pl.BlockSpec((t, d), lambda i: (i, 0))
    vec = pl.BlockSpec((1, d), lambda i: (0, 0))
    v = jax.ShapeDtypeStruct((1, d), F32)
    return pl.pallas_call(
        body, name="norms_mid_bwd", grid=(s // t,),
        in_specs=[tile, tile, tile, tile, vec, vec],
        out_specs=[tile, tile, vec, vec],
        out_shape=[jax.ShapeDtypeStruct((s, d), F32), jax.ShapeDtypeStruct((s, d), BF16), v, v],
        compiler_params=_params(),
    )(dh2, x2, dy, mix, g3, g2)


def mix_out_bwd(dmix, wout, merged, a, b, proj):
    s = dmix.shape[0]
    nblk = D_MODEL // CW

    def body(dm_ref, w_ref, mg_ref, a_ref, b_ref, gc_ref, gl_ref, da_ref, db_ref, dw_ref, dgc_ref, dgl_ref):
        dm = dm_ref[...]
        dmerged = _dot_nt(dm, w_ref[...])
        dw_ref[...] = _dot_tn(mg_ref[...], dm).astype(BF16)
        sc = _sigmoid(gc_ref[...])
        sl = _sigmoid(gl_ref[...])
        da_ref[...] = (dmerged * sc).astype(BF16)
        db_ref[...] = (dmerged * sl).astype(BF16)
        dgc_ref[...] = (dmerged * a_ref[...] * sc * (1.0 - sc)).astype(BF16)
        dgl_ref[...] = (dmerged * b_ref[...] * sl * (1.0 - sl)).astype(BF16)

    res = pl.BlockSpec((s, D_MODEL), lambda n: (0, 0))
    rows = pl.BlockSpec((CW, D_MODEL), lambda n: (n, 0))
    col = pl.BlockSpec((s, CW), lambda n: (0, n))
    blk = lambda k: pl.BlockSpec((s, CW), lambda n, k=k: (0, k * nblk + n))
    hb = jax.ShapeDtypeStruct((s, D_MODEL), BF16)
    outs = pl.pallas_call(
        body, name="mix_out_bwd", grid=(nblk,),
        in_specs=[res, rows, col, col, col, blk(5), blk(6)],
        out_specs=[col, col, rows, blk(5), blk(6)],
        out_shape=[hb, hb, jax.ShapeDtypeStruct((D_MODEL, D_MODEL), BF16),
                   jax.ShapeDtypeStruct((s, N_SEG * D_MODEL), BF16), jax.ShapeDtypeStruct((s, N_SEG * D_MODEL), BF16)],
        compiler_params=_params(),
    )(dmix, wout, merged, a, b, proj, proj)
    return outs


def mix_conv_bwd(da, wcb, proj, q, ws):
    s = da.shape[0]
    nblk = D_MODEL // CW

    def body(da_ref, w_ref, cb_ref, cc_ref, cx_ref, q_ref, ws_ref, dcb_ref, dcc_ref, dcx_ref, dw_ref, dws_ref):
        dab = da_ref[...]
        dya = _dot_nt(dab, w_ref[...])
        cb = cb_ref[...]
        cc = cc_ref[...]
        cx = cx_ref[...]
        q = q_ref[...]
        dw_ref[...] = _dot_tn((cb * q).astype(BF16), dab).astype(BF16)
        dcb_ref[...] = (dya * q).astype(BF16)
        dq = dya * cb
        dp = _causal_conv_t(dq, ws_ref)
        dws_ref[...] = _conv_wgrad(dq, cc * cx, 3)
        dcc_ref[...] = (dp * cx).astype(BF16)
        dcx_ref[...] = (dp * cc).astype(BF16)

    res = pl.BlockSpec((s, D_MODEL), lambda n: (0, 0))
    rows = pl.BlockSpec((CW, D_MODEL), lambda n: (n, 0))
    col = pl.BlockSpec((s, CW), lambda n: (0, n))
    blk = lambda k: pl.BlockSpec((s, CW), lambda n, k=k: (0, k * nblk + n))
    taps = pl.BlockSpec((3, CW), lambda n: (0, n))
    hb = jax.ShapeDtypeStruct((s, D_MODEL), BF16)
    return pl.pallas_call(
        body, name="mix_conv_bwd", grid=(nblk,),
        in_specs=[res, rows, blk(0), blk(1), blk(2), col, taps],
        out_specs=[col, col, col, rows, taps],
        out_shape=[hb, hb, hb, jax.ShapeDtypeStruct((D_MODEL, D_MODEL), BF16), jax.ShapeDtypeStruct((3, D_MODEL), F32)],
        compiler_params=_params(),
    )(da, wcb, proj, proj, proj, q, ws)


def mix_lru_bwd(db, wlb, proj, xl, r, i, h, wl, wa, wx, lam):
    s = db.shape[0]
    nblk = D_MODEL // CW

    def body(db_ref, w_ref, lx_ref, ly_ref, xl_ref, r_ref, i_ref, h_ref, wl_ref, wa_ref, wx_ref, lam_ref,
             dlx_ref, dly_ref, dw_ref, dwa_ref, dwx_ref, dba_ref, dbx_ref, dwl_ref, dbl_ref, dlam_ref,
             c_scr, dh_scr, g_scr):
        dbb = db_ref[...]
        dyb = _dot_nt(dbb, w_ref[...])
        h = h_ref[...]
        ge, dge = _gelu_and_grad(ly_ref[...])
        dw_ref[...] = _dot_tn((h * ge).astype(BF16), dbb).astype(BF16)
        dly_ref[...] = (dyb * h * dge).astype(BF16)
        r = r_ref[...]
        gi = i_ref[...]
        xl = xl_ref[...]
        lam = lam_ref[...]
        ls = _log_sigmoid(lam)
        a, mult = _lru_gates(r, ls)
        c_scr[...] = _shift_up(a, 1)
        dh_scr[...] = dyb * ge
        _scan_backward(c_scr, dh_scr, g_scr)
        du = g_scr[...]
        da = du * _shift_down(h, 1)
        dmult = du * gi * xl
        di = du * mult * xl
        dxl = du * mult * gi
        first = _rows(a.shape) == 0
        dlog_a = da * a - jnp.where(first, 0.0, dmult * a * a / mult)
        dr = dlog_a * (LRU_C * ls)
        dlam_ref[...] = jnp.sum(dlog_a * r, axis=0, keepdims=True) * (LRU_C * (1.0 - _sigmoid(lam)))
        dzr = dr * r * (1.0 - r)
        dzi = di * gi * (1.0 - gi)
        dba_ref[...] = jnp.sum(dzr, axis=0, keepdims=True)
        dbx_ref[...] = jnp.sum(dzi, axis=0, keepdims=True)
        xlb = xl.astype(BF16)
        dzrb = dzr.astype(BF16)
        dzib = dzi.astype(BF16)
        dwa_ref[0] = _dot_tn(xlb, dzrb)
        dwx_ref[0] = _dot_tn(xlb, dzib)
        dxl = dxl + _dot_nt(dzrb, wa_ref[0]) + _dot_nt(dzib, wx_ref[0])
        dlx_ref[...] = _causal_conv_t(dxl, wl_ref).astype(BF16)
        dwl_ref[...] = _conv_wgrad(dxl, lx_ref[...], 4)
        dbl_ref[...] = jnp.sum(dxl, axis=0, keepdims=True)

    res = pl.BlockSpec((s, D_MODEL), lambda n: (0, 0))
    rows = pl.BlockSpec((CW, D_MODEL), lambda n: (n, 0))
    col = pl.BlockSpec((s, CW), lambda n: (0, n))
    blk = lambda k: pl.BlockSpec((s, CW), lambda n, k=k: (0, k * nblk + n))
    taps = pl.BlockSpec((4, CW), lambda n: (0, n))
    vec = pl.BlockSpec((1, CW), lambda n: (0, n))
    mat = pl.BlockSpec((1, CW, CW), lambda n: (n, 0, 0))
    hb = jax.ShapeDtypeStruct((s, D_MODEL), BF16)
    v = jax.ShapeDtypeStruct((1, D_MODEL), F32)
    m = jax.ShapeDtypeStruct((LRU_HEADS, HEAD_DIM, HEAD_DIM), F32)
    scr = pltpu.VMEM((s, CW), F32)
    return pl.pallas_call(
        body, name="mix_lru_bwd", grid=(nblk,),
        in_specs=[res, rows, blk(3), blk(4), col, col, col, col, taps, mat, mat, vec],
        out_specs=[col, col, rows, mat, mat, vec, vec, taps, vec, vec],
        out_shape=[hb, hb, jax.ShapeDtypeStruct((D_MODEL, D_MODEL), BF16), m, m, v, v,
                   jax.ShapeDtypeStruct((4, D_MODEL), F32), v, v],
        scratch_shapes=[scr, scr, scr],
        compiler_params=_params(),
    )(db, wlb, proj, proj, xl, r, i, h, wl, wa, wx, lam)


def norm_in_bwd(dh1, x, dx2, g1):
    s, d = x.shape
    t = _token_tile(s)

    def body(dh_ref, x_ref, dx2_ref, g_ref, dx_ref, dg_ref):
        @pl.when(pl.program_id(0) == 0)
        def _():
            dg_ref[...] = jnp.zeros_like(dg_ref)

        n, r = _rms_stats(x_ref[...])
        dx, dg = _rms_bwd(n, r, g_ref[...], dh_ref[...])
        dx_ref[...] = dx2_ref[...] + dx
        dg_ref[...] += jnp.sum(dg, axis=0, keepdims=True)

    tile = pl.BlockSpec((t, d), lambda i: (i, 0))
    vec = pl.BlockSpec((1, d), lambda i: (0, 0))
    return pl.pallas_call(
        body, name="norm_in_bwd", grid=(s // t,),
        in_specs=[tile, tile, tile, vec],
        out_specs=[tile, vec],
        out_shape=[jax.ShapeDtypeStruct((s, d), F32), jax.ShapeDtypeStruct((1, d), F32)],
        compiler_params=_params(),
    )(dh1, x, dx2, g1)


def local_step(x, target, g1, g2, g3, g4, win4, ws, wcb, wl, bl, wa, ba, wx, bx, lam, wlb, wout, wup4, fw, fb, wdown):
    h1 = norm_in(x, g1)
    proj = matmul_cols(h1, win4, "proj_fwd")
    q, ya = mix_conv_fwd(proj, ws)
    xl, r, gi, h, yb = mix_lru_fwd(proj, wl, bl, wa, ba, wx, bx, lam)
    a, b, merged = branch_merge_fwd(ya, yb, wcb, wlb, proj)
    mix, x2, h2 = mix_out_fwd(merged, wout, x, g2, g3)
    up = matmul_cols(h2, wup4, "up_fwd")
    f = ffn_act_fwd(up, fw, fb)
    dy, dout, loss, dg4 = ffn_down_loss(f, wdown, x2, target, g4)

    dug, duv, dwdown, dfw_g, dfw_v, dfb_g, dfb_v = ffn_bwd(dout, wdown, up, fw, fb)
    dup = jnp.concatenate([dug, duv], axis=1)
    dfw = jnp.concatenate([dfw_g, dfw_v], axis=1)
    dfb = jnp.concatenate([dfb_g, dfb_v], axis=1)
    dh2, dwup = dgrad_wgrad_cols(dup, wup4, h2, "up_bwd")
    dx2, dmix, dg3, dg2 = norms_mid_bwd(dh2, x2, dy, mix, g3, g2)
    da, db, dwout, dgc, dgl = mix_out_bwd(dmix, wout, merged, a, b, proj)
    dcb, dcc, dcx, dwcb, dws = mix_conv_bwd(da, wcb, proj, q, ws)
    dlx, dly, dwlb, dwa, dwx, dba, dbx, dwl, dbl, dlam = mix_lru_bwd(db, wlb, proj, xl, r, gi, h, wl, wa, wx, lam)
    dproj = jnp.concatenate([dcb, dcc, dcx, dlx, dly, dgc[:, 5 * D_MODEL:6 * D_MODEL], dgl[:, 6 * D_MODEL:]], axis=1)
    dh1, dwin = dgrad_wgrad_cols(dproj, win4, h1, "proj_bwd")
    dx, dg1 = norm_in_bwd(dh1, x, dx2, g1)
    grads = dict(norm_mix_pre=dg1, norm_mix_post=dg2, norm_ffn_pre=dg3, norm_ffn_post=dg4,
                 w_in=dwin, conv_short_w=dws, w_conv_branch=dwcb, lru_conv_w=dwl, lru_conv_b=dbl,
                 lru_wa=dwa, lru_ba=dba, lru_wx=dwx, lru_bx=dbx, lru_lambda=dlam,
                 w_lru_branch=dwlb, w_out=dwout, ffn_w_up=dwup, ffn_conv_w=dfw, ffn_conv_b=dfb,
                 ffn_w_down=dwdown)
    return loss[0, 0], dx, grads


MESH = pl.DeviceIdType.MESH
_HBM = pl.BlockSpec(memory_space=pltpu.HBM)
_OTHER_CHIPS = ((1, 0), (0, 1), (1, 1))
_OTHER_DEVICES = tuple((dx, dy, dc) for dx in (0, 1) for dy in (0, 1) for dc in (0, 1) if dx or dy or dc)
N_DEVICES = 8


def _position():
    return lax.axis_index("x"), lax.axis_index("y"), lax.axis_index("c")


def _flip(v, d):
    return 1 - v if d else v


def _half_rows(ref, h, hr):
    return ref.at[pl.ds(h * hr, hr), :]


def gather_chips(shards):
    n = len(shards)
    nrel = len(_OTHER_CHIPS)

    def body(*refs):
        ins, outs = refs[:n], refs[n:2 * n]
        loc_sem, ici_send, ici_recv, sib_send, sib_recv = refs[2 * n:]
        x, y, c = _position()
        j = 2 * x + y
        hr = [s.shape[0] // 2 for s in shards]

        def chip(p):
            px, py = _flip(x, _OTHER_CHIPS[p][0]), _flip(y, _OTHER_CHIPS[p][1])
            return px, py, 2 * px + py

        def ici(a, p, slot):
            px, py, _ = chip(p)
            return pltpu.make_async_remote_copy(
                src_ref=_half_rows(ins[a], c, hr[a]), dst_ref=_half_rows(outs[a].at[slot], c, hr[a]),
                send_sem=ici_send.at[a * nrel + p], recv_sem=ici_recv.at[a * nrel + p],
                device_id=(px, py, c), device_id_type=MESH)

        def sib(a, p, h):
            _, _, k = chip(p)
            part = _half_rows(outs[a].at[k], h, hr[a])
            return pltpu.make_async_remote_copy(
                src_ref=part, dst_ref=part, send_sem=sib_send.at[a * nrel + p], recv_sem=sib_recv.at[a * nrel + p],
                device_id=(x, y, 1 - c), device_id_type=MESH)

        own = [pltpu.make_async_copy(ins[a], outs[a].at[j], loc_sem.at[a]) for a in range(n)]
        for cp in own:
            cp.start()
        pairs = [(a, p) for a in range(n) for p in range(nrel)]
        for a, p in pairs:
            ici(a, p, j).start()
        for a, p in pairs:
            ici(a, p, chip(p)[2]).wait_recv()
            sib(a, p, c).start()
        for a, p in pairs:
            sib(a, p, 1 - c).wait_recv()
        for a, p in pairs:
            ici(a, p, j).wait_send()
            sib(a, p, c).wait_send()
        for cp in own:
            cp.wait()

    sems = [pltpu.SemaphoreType.DMA((n,))] + [pltpu.SemaphoreType.DMA((n * nrel,))] * 4
    return pl.pallas_call(
        body, name="gather_chips",
        in_specs=[_HBM] * n, out_specs=[_HBM] * n,
        out_shape=[jax.ShapeDtypeStruct((N_CHIPS,) + s.shape, s.dtype) for s in shards],
        scratch_shapes=sems,
    )(*shards)


def _owned_part(ref, kind, k, h, hr):
    if kind == "col":
        ns = ref.shape[1] // N_CHIPS
        return ref.at[pl.ds(h * hr, hr), pl.ds(k * ns, ns)]
    if kind == "row":
        return ref.at[pl.ds(k * 2 * hr + h * hr, hr), :]
    return ref.at[k, pl.ds(h * hr, hr), :]


def _part_shape(g, kind):
    if kind == "col":
        return g.shape[0] // 2, g.shape[1] // N_CHIPS
    if kind == "row":
        return g.shape[0] // (2 * N_CHIPS), g.shape[1]
    return g.shape[1] // 2, g.shape[2]


def pair_split(grads, kinds):
    n = len(grads)
    shapes = [_part_shape(g, k) for g, k in zip(grads, kinds)]

    def body(*refs):
        ins, mine, theirs = refs[:n], refs[n:2 * n], refs[2 * n:3 * n]
        loc_sem, send_sem, recv_sem = refs[3 * n:]
        x, y, c = _position()
        copies = []
        for a in range(n):
            hr = shapes[a][0]
            for k in range(N_CHIPS):
                s = a * N_CHIPS + k
                copies.append(pltpu.make_async_copy(_owned_part(ins[a], kinds[a], k, c, hr), mine[a].at[k], loc_sem.at[s]))
                copies.append(pltpu.make_async_remote_copy(
                    src_ref=_owned_part(ins[a], kinds[a], k, 1 - c, hr), dst_ref=theirs[a].at[k],
                    send_sem=send_sem.at[s], recv_sem=recv_sem.at[s], device_id=(x, y, 1 - c), device_id_type=MESH))
        for cp in copies:
            cp.start()
        for cp in copies:
            cp.wait()

    out = [jax.ShapeDtypeStruct((N_CHIPS,) + shp, g.dtype) for shp, g in zip(shapes, grads)]
    res = pl.pallas_call(
        body, name="pair_split",
        in_specs=[_HBM] * n, out_specs=[_HBM] * (2 * n), out_shape=out + out,
        scratch_shapes=[pltpu.SemaphoreType.DMA((n * N_CHIPS,))] * 3,
    )(*grads)
    return res[:n], res[n:]


def chip_exchange(sums, rep):
    n = len(sums)
    nrel = len(_OTHER_CHIPS)
    ndev = len(_OTHER_DEVICES)

    def body(*refs):
        ins, rep_ref = refs[:n], refs[n]
        outs, rep_out = refs[n + 1:2 * n + 1], refs[2 * n + 1]
        loc_sem, send_sem, recv_sem, rep_send, rep_recv = refs[2 * n + 2:]
        x, y, c = _position()
        j = 2 * x + y
        me = 4 * x + 2 * y + c

        def chip(p):
            px, py = _flip(x, _OTHER_CHIPS[p][0]), _flip(y, _OTHER_CHIPS[p][1])
            return px, py, 2 * px + py

        def part(a, p, src_slot, dst_slot):
            px, py, _ = chip(p)
            return pltpu.make_async_remote_copy(
                src_ref=ins[a].at[src_slot], dst_ref=outs[a].at[dst_slot],
                send_sem=send_sem.at[a * nrel + p], recv_sem=recv_sem.at[a * nrel + p],
                device_id=(px, py, c), device_id_type=MESH)

        def device(q):
            dx, dy, dc = _OTHER_DEVICES[q]
            return _flip(x, dx), _flip(y, dy), _flip(c, dc)

        def rep_copy(q, slot):
            return pltpu.make_async_remote_copy(
                src_ref=rep_ref, dst_ref=rep_out.at[slot], send_sem=rep_send.at[q], recv_sem=rep_recv.at[q],
                device_id=device(q), device_id_type=MESH)

        own = [pltpu.make_async_copy(ins[a].at[j], outs[a].at[j], loc_sem.at[a]) for a in range(n)]
        own.append(pltpu.make_async_copy(rep_ref, rep_out.at[me], loc_sem.at[n]))
        for cp in own:
            cp.start()
        pairs = [(a, p) for a in range(n) for p in range(nrel)]
        for a, p in pairs:
            part(a, p, chip(p)[2], j).start()
        for q in range(ndev):
            rep_copy(q, me).start()
        for a, p in pairs:
            part(a, p, chip(p)[2], chip(p)[2]).wait_recv()
        for q in range(ndev):
            px, py, pc = device(q)
            rep_copy(q, 4 * px + 2 * py + pc).wait_recv()
        for a, p in pairs:
            part(a, p, chip(p)[2], j).wait_send()
        for q in range(ndev):
            rep_copy(q, me).wait_send()
        for cp in own:
            cp.wait()

    return pl.pallas_call(
        body, name="chip_exchange",
        in_specs=[_HBM] * (n + 1), out_specs=[_HBM] * (n + 1),
        out_shape=[jax.ShapeDtypeStruct(s.shape, s.dtype) for s in sums]
        + [jax.ShapeDtypeStruct((N_DEVICES,) + rep.shape, rep.dtype)],
        scratch_shapes=[pltpu.SemaphoreType.DMA((n + 1,)), pltpu.SemaphoreType.DMA((n * nrel,)),
                        pltpu.SemaphoreType.DMA((n * nrel,)), pltpu.SemaphoreType.DMA((ndev,)),
                        pltpu.SemaphoreType.DMA((ndev,))],
    )(*sums, rep)


def pair_join(halves):
    n = len(halves)

    def body(*refs):
        ins, outs = refs[:n], refs[n:2 * n]
        loc_sem, send_sem, recv_sem = refs[2 * n:]
        x, y, c = _position()
        copies = []
        for a in range(n):
            hr = halves[a].shape[0]
            copies.append(pltpu.make_async_copy(ins[a], _half_rows(outs[a], c, hr), loc_sem.at[a]))
        sends = [pltpu.make_async_remote_copy(
            src_ref=ins[a], dst_ref=_half_rows(outs[a], c, halves[a].shape[0]),
            send_sem=send_sem.at[a], recv_sem=recv_sem.at[a], device_id=(x, y, 1 - c), device_id_type=MESH)
            for a in range(n)]
        arrivals = [pltpu.make_async_remote_copy(
            src_ref=ins[a], dst_ref=_half_rows(outs[a], 1 - c, halves[a].shape[0]),
            send_sem=send_sem.at[a], recv_sem=recv_sem.at[a], device_id=(x, y, 1 - c), device_id_type=MESH)
            for a in range(n)]
        for cp in copies + sends:
            cp.start()
        for cp in arrivals:
            cp.wait_recv()
        for cp in sends:
            cp.wait_send()
        for cp in copies:
            cp.wait()

    return pl.pallas_call(
        body, name="pair_join",
        in_specs=[_HBM] * n, out_specs=[_HBM] * n,
        out_shape=[jax.ShapeDtypeStruct((2 * h.shape[0], h.shape[1]), h.dtype) for h in halves],
        scratch_shapes=[pltpu.SemaphoreType.DMA((n,))] * 3,
    )(*halves)


def _row_tile(rows, cols, limit_bytes=1 << 20):
    best = None
    for t in range(SUBLANES, rows + 1, SUBLANES):
        if rows % t == 0 and t * cols * 4 <= limit_bytes:
            best = t
    return best or rows


def add_pair(a, b, name):
    nc, rows, cols = a.shape
    t = _row_tile(rows, cols)

    def body(a_ref, b_ref, o_ref):
        o_ref[...] = (a_ref[...].astype(F32) + b_ref[...].astype(F32)).astype(o_ref.dtype)

    spec = pl.BlockSpec((1, t, cols), lambda k, i: (k, i, 0))
    return pl.pallas_call(
        body, name=name, grid=(nc, rows // t), in_specs=[spec, spec], out_specs=spec,
        out_shape=jax.ShapeDtypeStruct(a.shape, a.dtype), compiler_params=_params(),
    )(a, b)


def sum_lead(a, name):
    nl, rows, cols = a.shape
    t = _row_tile(rows, cols, (1 << 20) // 2)

    def body(a_ref, o_ref):
        acc = a_ref[0].astype(F32)
        for s in range(1, nl):
            acc = acc + a_ref[s].astype(F32)
        o_ref[...] = acc

    return pl.pallas_call(
        body, name=name, grid=(rows // t,),
        in_specs=[pl.BlockSpec((nl, t, cols), lambda i: (0, i, 0))],
        out_specs=pl.BlockSpec((t, cols), lambda i: (i, 0)),
        out_shape=jax.ShapeDtypeStruct((rows, cols), F32), compiler_params=_params(),
    )(a)


def adamw(w, g, m, v, name):
    rows, cols = w.shape
    t = _row_tile(rows, cols)
    c1 = 1.0 / (1.0 - ADAM_B1 ** ADAM_STEP)
    c2 = 1.0 / (1.0 - ADAM_B2 ** ADAM_STEP)

    def body(w_ref, g_ref, m_ref, v_ref, d_ref, nm_ref, nv_ref):
        g = g_ref[...]
        nm = ADAM_B1 * m_ref[...] + (1.0 - ADAM_B1) * g
        nv = ADAM_B2 * v_ref[...] + (1.0 - ADAM_B2) * (g * g)
        nm_ref[...] = nm
        nv_ref[...] = nv
        d_ref[...] = -ADAM_LR * ((nm * c1) / (jnp.sqrt(nv * c2) + ADAM_EPS) + ADAM_WD * w_ref[...])

    spec = pl.BlockSpec((t, cols), lambda i: (i, 0))
    shp = jax.ShapeDtypeStruct((rows, cols), F32)
    return pl.pallas_call(
        body, name=name, grid=(rows // t,), in_specs=[spec] * 4, out_specs=[spec] * 3,
        out_shape=[shp, shp, shp], compiler_params=_params(),
    )(w, g, m, v)


WEIGHTS = ("norm_mix_pre", "norm_mix_post", "norm_ffn_pre", "norm_ffn_post", "w_in", "conv_short_w",
           "w_conv_branch", "lru_conv_w", "lru_conv_b", "lru_wa", "lru_ba", "lru_wx", "lru_bx", "lru_lambda",
           "w_lru_branch", "w_out", "ffn_w_up", "ffn_conv_w", "ffn_conv_b", "ffn_w_down")
BIG = ("w_in", "ffn_w_up", "w_conv_branch", "w_lru_branch", "w_out", "ffn_w_down")
BIG_KIND = ("col", "col", "row", "row", "row", "row")
SMALL = ("conv_short_w", "lru_conv_w", "lru_wa", "lru_ba", "lru_wx", "lru_bx", "ffn_conv_w")
REPL = ("norm_mix_pre", "norm_mix_post", "norm_ffn_pre", "norm_ffn_post", "lru_conv_b", "lru_lambda", "ffn_conv_b")
PACK_W = 256
SMALL_ROWS = 544
REPL_ROWS = 16
FFN_SHARD = 2 * D_FF // N_CHIPS
QUARTER = HEAD_DIM // N_CHIPS


def _pack_small_shard(p):
    rows = [p["conv_short_w"].reshape(3, PACK_W), p["lru_conv_w"].reshape(4, PACK_W),
            p["lru_wa"].reshape(LRU_HEADS * QUARTER, PACK_W), p["lru_ba"].reshape(1, PACK_W),
            p["lru_wx"].reshape(LRU_HEADS * QUARTER, PACK_W), p["lru_bx"].reshape(1, PACK_W),
            p["ffn_conv_w"].reshape(3 * FFN_SHARD // PACK_W, PACK_W)]
    used = sum(r.shape[0] for r in rows)
    return jnp.concatenate(rows + [jnp.zeros((SMALL_ROWS - used, PACK_W), F32)], axis=0)


def _unpack_small_shard(buf):
    out, r = {}, 0
    for name, nr, shape in (("conv_short_w", 3, (1, 3, PACK_W)), ("lru_conv_w", 4, (1, 4, PACK_W)),
                            ("lru_wa", LRU_HEADS * QUARTER, (1, LRU_HEADS, QUARTER, HEAD_DIM)),
                            ("lru_ba", 1, (1, LRU_HEADS, QUARTER)),
                            ("lru_wx", LRU_HEADS * QUARTER, (1, LRU_HEADS, QUARTER, HEAD_DIM)),
                            ("lru_bx", 1, (1, LRU_HEADS, QUARTER)),
                            ("ffn_conv_w", 3 * FFN_SHARD // PACK_W, (1, 3, FFN_SHARD))):
        out[name] = buf[r:r + nr].reshape(shape)
        r += nr
    return out


def _full_small(g4):
    per = [_unpack_small_shard(g4[k]) for k in range(N_CHIPS)]
    cat = lambda name, axis: jnp.concatenate([per[k][name][0] for k in range(N_CHIPS)], axis=axis)
    return dict(conv_short_w=cat("conv_short_w", 1), lru_conv_w=cat("lru_conv_w", 1),
                lru_wa=cat("lru_wa", 1), lru_ba=cat("lru_ba", 1).reshape(1, D_MODEL),
                lru_wx=cat("lru_wx", 1), lru_bx=cat("lru_bx", 1).reshape(1, D_MODEL),
                ffn_conv_w=cat("ffn_conv_w", 1))


def _split_small(full):
    shards = []
    for k in range(N_CHIPS):
        cols = lambda a, w: a[:, k * w:(k + 1) * w]
        q = slice(k * QUARTER, (k + 1) * QUARTER)
        shards.append(_pack_small_shard(dict(
            conv_short_w=cols(full["conv_short_w"], PACK_W), lru_conv_w=cols(full["lru_conv_w"], PACK_W),
            lru_wa=full["lru_wa"][:, q, :], lru_ba=full["lru_ba"].reshape(LRU_HEADS, HEAD_DIM)[:, q],
            lru_wx=full["lru_wx"][:, q, :], lru_bx=full["lru_bx"].reshape(LRU_HEADS, HEAD_DIM)[:, q],
            ffn_conv_w=cols(full["ffn_conv_w"], FFN_SHARD))))
    return jnp.stack(shards)


def _pack_repl(p):
    rows = [p[n].reshape(-1, D_MODEL) for n in REPL]
    used = sum(r.shape[0] for r in rows)
    return jnp.concatenate(rows + [jnp.zeros((REPL_ROWS - used, D_MODEL), F32)], axis=0)


def _unpack_repl(buf):
    out, r = {}, 0
    for n in REPL:
        nr = (2 * D_FF // D_MODEL) if n == "ffn_conv_b" else 1
        out[n] = buf[r:r + nr].reshape(1, nr * D_MODEL)
        r += nr
    return out


def kernel(x, norm_mix_pre, norm_mix_post, norm_ffn_pre, norm_ffn_post, w_in, conv_short_w, w_conv_branch, lru_conv_w, lru_conv_b, lru_wa, lru_ba, lru_wx, lru_bx, lru_lambda, w_lru_branch, w_out, ffn_w_up, ffn_conv_w, ffn_conv_b, ffn_w_down, loss_target, m_norm_mix_pre, m_norm_mix_post, m_norm_ffn_pre, m_norm_ffn_post, m_w_in, m_conv_short_w, m_w_conv_branch, m_lru_conv_w, m_lru_conv_b, m_lru_wa, m_lru_ba, m_lru_wx, m_lru_bx, m_lru_lambda, m_w_lru_branch, m_w_out, m_ffn_w_up, m_ffn_conv_w, m_ffn_conv_b, m_ffn_w_down, v_norm_mix_pre, v_norm_mix_post, v_norm_ffn_pre, v_norm_ffn_post, v_w_in, v_conv_short_w, v_w_conv_branch, v_lru_conv_w, v_lru_conv_b, v_lru_wa, v_lru_ba, v_lru_wx, v_lru_bx, v_lru_lambda, v_w_lru_branch, v_w_out, v_ffn_w_up, v_ffn_conv_w, v_ffn_conv_b, v_ffn_w_down):
    given = dict(locals())
    w = {n: given[n] for n in WEIGHTS}
    m = {n: given["m_" + n] for n in WEIGHTS}
    v = {n: given["v_" + n] for n in WEIGHTS}

    big_shards = [w[n][0].astype(BF16) for n in BIG]
    gathered = gather_chips(big_shards + [_pack_small_shard(w)])
    win4, wup4 = gathered[0], gathered[1]
    wcb, wlb, wout, wdown = [g.reshape(-1, D_MODEL) for g in gathered[2:6]]
    small = _full_small(gathered[6])

    loss, dx, grads = local_step(
        x[0], loss_target[0], w["norm_mix_pre"], w["norm_mix_post"], w["norm_ffn_pre"], w["norm_ffn_post"],
        win4, small["conv_short_w"], wcb, small["lru_conv_w"], w["lru_conv_b"],
        small["lru_wa"].astype(BF16), small["lru_ba"], small["lru_wx"].astype(BF16), small["lru_bx"], w["lru_lambda"],
        wlb, wout, wup4, small["ffn_conv_w"], w["ffn_conv_b"], wdown)

    local = [grads[n] for n in BIG] + [_split_small(grads)]
    kinds = BIG_KIND + ("lead",)
    mine, theirs = pair_split(local, kinds)
    chip_sums = [add_pair(a, b, "pair_add_%d" % i) for i, (a, b) in enumerate(zip(mine, theirs))]
    exchanged = chip_exchange(chip_sums, _pack_repl(grads))
    halves = [sum_lead(e, "chip_sum_%d" % i) for i, e in enumerate(exchanged[:-1])]
    rep_grad = sum_lead(exchanged[-1], "device_sum")
    reduced = pair_join(halves)

    g_out, d_out, m_out, v_out = {}, {}, {}, {}
    for n, g in zip(BIG, reduced[:-1]):
        d, nm, nv = adamw(w[n][0], g, m[n][0], v[n][0], "adamw_" + n)
        g_out[n], d_out[n], m_out[n], v_out[n] = g[None], d[None], nm[None], nv[None]
    d, nm, nv = adamw(_pack_small_shard(w), reduced[-1], _pack_small_shard(m), _pack_small_shard(v), "adamw_small")
    for dst, buf in ((g_out, reduced[-1]), (d_out, d), (m_out, nm), (v_out, nv)):
        dst.update(_unpack_small_shard(buf))
    d, nm, nv = adamw(_pack_repl(w), rep_grad, _pack_repl(m), _pack_repl(v), "adamw_repl")
    for dst, buf in ((g_out, rep_grad), (d_out, d), (m_out, nm), (v_out, nv)):
        dst.update(_unpack_repl(buf))

    total = lax.psum(loss, ("x", "y", "c"))
    return (total, dx[None], *[g_out[n] for n in WEIGHTS], *[d_out[n] for n in WEIGHTS],
            *[m_out[n] for n in WEIGHTS], *[v_out[n] for n in WEIGHTS])
```

```python
import functools
import math

import jax
import jax.numpy as jnp
from jax import lax
from jax.experimental import pallas as pl
from jax.experimental.pallas import tpu as pltpu

F32 = jnp.float32
BF16 = jnp.bfloat16

D_MODEL = 1024
N_CHIPS = 4
N_SEG = 7
D_FF = 3 * D_MODEL
LRU_HEADS = 4
HEAD_DIM = D_MODEL // LRU_HEADS
LRU_C = 8.0
RMS_EPS = 1e-6
CW = 256
FW = 512
SUBLANES = 8
VMEM_LIMIT = 56 * 1024 * 1024

ADAM_LR = 0.001
ADAM_B1 = 0.9
ADAM_B2 = 0.999
ADAM_EPS = 1e-08
ADAM_WD = 0.01
ADAM_STEP = 10

_GELU_C = math.sqrt(2.0 / math.pi)
_GELU_K = 0.044715


def _params(**kw):
    return pltpu.CompilerParams(vmem_limit_bytes=VMEM_LIMIT, **kw)


def _sigmoid(x):
    return 1.0 / (1.0 + jnp.exp(-x))


def _gelu(x):
    t = jnp.tanh(_GELU_C * (x + _GELU_K * x * x * x))
    return 0.5 * x * (1.0 + t)


def _gelu_and_grad(x):
    x2 = x * x
    t = jnp.tanh(_GELU_C * (x + _GELU_K * x * x2))
    g = 0.5 * x * (1.0 + t)
    dg = 0.5 * (1.0 + t) + 0.5 * x * (1.0 - t * t) * _GELU_C * (1.0 + 3.0 * _GELU_K * x2)
    return g, dg


def _log_sigmoid(x):
    e = jnp.exp(-jnp.abs(x))
    u = 1.0 + e
    l1p = jnp.where(u == 1.0, e, jnp.log(u) * e / (u - 1.0))
    return jnp.minimum(x, 0.0) - l1p


def _neg_expm1(z):
    series = -z * (1.0 + z * (0.5 + z * (1.0 / 6.0 + z * (1.0 / 24.0 + z * (1.0 / 120.0 + z * (1.0 / 720.0))))))
    return jnp.where(z > -0.2, series, 1.0 - jnp.exp(z))


def _rows(shape):
    return lax.broadcasted_iota(jnp.int32, shape, 0)


def _shift_down(x, k):
    return jnp.where(_rows(x.shape) >= k, pltpu.roll(x, k, 0), 0.0)


def _shift_up(x, k):
    n = x.shape[0]
    return jnp.where(_rows(x.shape) < n - k, pltpu.roll(x, n - k, 0), 0.0)


def _causal_conv(x, w_ref, b=None):
    k_width = w_ref.shape[0]
    y = w_ref[k_width - 1:k_width, :] * x
    for j in range(1, k_width):
        y = y + w_ref[k_width - 1 - j:k_width - j, :] * _shift_down(x, j)
    if b is not None:
        y = y + b
    return y


def _causal_conv_t(dy, w_ref):
    k_width = w_ref.shape[0]
    dx = w_ref[k_width - 1:k_width, :] * dy
    for j in range(1, k_width):
        dx = dx + w_ref[k_width - 1 - j:k_width - j, :] * _shift_up(dy, j)
    return dx


def _conv_wgrad(dy, x, k_width):
    rows = [None] * k_width
    for j in range(k_width):
        xs = x if j == 0 else _shift_down(x, j)
        rows[k_width - 1 - j] = jnp.sum(dy * xs, axis=0, keepdims=True)
    return jnp.concatenate(rows, axis=0)


def _dot(a, b):
    return jnp.dot(a, b, preferred_element_type=F32)


def _dot_nt(a, b):
    return lax.dot_general(a, b, (((1,), (1,)), ((), ())), preferred_element_type=F32)


def _dot_tn(a, b):
    return lax.dot_general(a, b, (((0,), (0,)), ((), ())), preferred_element_type=F32)


def _rms_stats(x):
    r = lax.rsqrt(jnp.mean(x * x, axis=-1, keepdims=True) + RMS_EPS)
    return x * r, r


def _rms_bwd(n, r, g, dy):
    dn = dy * g
    dx = r * (dn - n * jnp.mean(dn * n, axis=-1, keepdims=True))
    return dx, dy * n


def _scan_forward(a_ref, b_ref, h_ref):
    n, c = a_ref.shape
    row = lax.broadcasted_iota(jnp.int32, (SUBLANES, c), 0)

    def group(g, carry):
        r0 = pl.multiple_of(g * SUBLANES, SUBLANES)
        a = a_ref[pl.ds(r0, SUBLANES), :]
        b = b_ref[pl.ds(r0, SUBLANES), :]
        for k in (1, 2, 4):
            ap = jnp.where(row >= k, pltpu.roll(a, k, 0), 1.0)
            bp = jnp.where(row >= k, pltpu.roll(b, k, 0), 0.0)
            b = a * bp + b
            a = a * ap
        h = a * carry + b
        h_ref[pl.ds(r0, SUBLANES), :] = h
        return h[SUBLANES - 1:SUBLANES, :]

    lax.fori_loop(0, n // SUBLANES, group, jnp.zeros((1, c), F32))


def _scan_backward(c_ref, b_ref, g_ref):
    n, ch = c_ref.shape
    row = lax.broadcasted_iota(jnp.int32, (SUBLANES, ch), 0)
    n_groups = n // SUBLANES

    def group(i, carry):
        r0 = pl.multiple_of((n_groups - 1 - i) * SUBLANES, SUBLANES)
        a = c_ref[pl.ds(r0, SUBLANES), :]
        b = b_ref[pl.ds(r0, SUBLANES), :]
        for k in (1, 2, 4):
            keep = row < SUBLANES - k
            ap = jnp.where(keep, pltpu.roll(a, SUBLANES - k, 0), 1.0)
            bp = jnp.where(keep, pltpu.roll(b, SUBLANES - k, 0), 0.0)
            b = a * bp + b
            a = a * ap
        g = a * carry + b
        g_ref[pl.ds(r0, SUBLANES), :] = g
        return g[0:1, :]

    lax.fori_loop(0, n_groups, group, jnp.zeros((1, ch), F32))


def _token_tile(s):
    return min(s, 512)


def norm_in(x, g):
    s, d = x.shape
    t = _token_tile(s)

    def body(x_ref, g_ref, o_ref):
        n, _ = _rms_stats(x_ref[...])
        o_ref[...] = (n * g_ref[...]).astype(BF16)

    return pl.pallas_call(
        body, name="norm_in", grid=(s // t,),
        in_specs=[pl.BlockSpec((t, d), lambda i: (i, 0)), pl.BlockSpec((1, d), lambda i: (0, 0))],
        out_specs=pl.BlockSpec((t, d), lambda i: (i, 0)),
        out_shape=jax.ShapeDtypeStruct((s, d), BF16),
        compiler_params=_params(),
    )(x, g)


def matmul_cols(a, w4, name):
    m, k = a.shape
    nj, _, ns = w4.shape
    nb = ns // CW

    def body(a_ref, w_ref, o_ref):
        o_ref[...] = _dot(a_ref[...], w_ref[0])

    return pl.pallas_call(
        body, name=name, grid=(nj, nb),
        in_specs=[pl.BlockSpec((m, k), lambda j, b: (0, 0)),
                  pl.BlockSpec((1, k, CW), lambda j, b: (j, 0, b))],
        out_specs=pl.BlockSpec((m, CW), lambda j, b: (0, j * nb + b)),
        out_shape=jax.ShapeDtypeStruct((m, nj * ns), F32),
        compiler_params=_params(),
    )(a, w4)


def mix_conv_fwd(proj, ws):
    s = proj.shape[0]
    nblk = D_MODEL // CW

    def body(cb_ref, cc_ref, cx_ref, ws_ref, q_ref, ya_ref):
        q = _causal_conv(cc_ref[...] * cx_ref[...], ws_ref)
        q_ref[...] = q
        ya_ref[...] = (cb_ref[...] * q).astype(BF16)

    seg = lambda k: pl.BlockSpec((s, CW), lambda c, k=k: (0, k * nblk + c))
    return pl.pallas_call(
        body, name="mix_conv_fwd", grid=(nblk,),
        in_specs=[seg(0), seg(1), seg(2), pl.BlockSpec((3, CW), lambda c: (0, c))],
        out_specs=[pl.BlockSpec((s, CW), lambda c: (0, c))] * 2,
        out_shape=[jax.ShapeDtypeStruct((s, D_MODEL), F32), jax.ShapeDtypeStruct((s, D_MODEL), BF16)],
        compiler_params=_params(),
    )(proj, proj, proj, ws)


def _lru_gates(r, ls):
    log_a = LRU_C * r * ls
    a = jnp.exp(log_a)
    mult = jnp.sqrt(_neg_expm1(2.0 * log_a))
    mult = jnp.where(_rows(r.shape) == 0, 1.0, mult)
    return a, mult


def mix_lru_fwd(proj, wl, bl, wa, ba, wx, bx, lam):
    s = proj.shape[0]
    nblk = D_MODEL // CW

    def body(lx_ref, ly_ref, wl_ref, bl_ref, wa_ref, ba_ref, wx_ref, bx_ref, lam_ref,
             xl_ref, r_ref, i_ref, h_ref, yb_ref, a_scr, u_scr):
        xl = _causal_conv(lx_ref[...], wl_ref, bl_ref[...])
        xl_ref[...] = xl
        xlb = xl.astype(BF16)
        r = _sigmoid(_dot(xlb, wa_ref[0]) + ba_ref[...])
        i = _sigmoid(_dot(xlb, wx_ref[0]) + bx_ref[...])
        r_ref[...] = r
        i_ref[...] = i
        a, mult = _lru_gates(r, _log_sigmoid(lam_ref[...]))
        a_scr[...] = a
        u_scr[...] = mult * i * xl
        _scan_forward(a_scr, u_scr, h_ref)
        yb_ref[...] = (h_ref[...] * _gelu(ly_ref[...])).astype(BF16)

    blk = lambda k: pl.BlockSpec((s, CW), lambda c, k=k: (0, k * nblk + c))
    vec = pl.BlockSpec((1, CW), lambda c: (0, c))
    mat = pl.BlockSpec((1, CW, CW), lambda c: (c, 0, 0))
    out = pl.BlockSpec((s, CW), lambda c: (0, c))
    f = jax.ShapeDtypeStruct((s, D_MODEL), F32)
    return pl.pallas_call(
        body, name="mix_lru_fwd", grid=(nblk,),
        in_specs=[blk(3), blk(4), pl.BlockSpec((4, CW), lambda c: (0, c)), vec, mat, vec, mat, vec, vec],
        out_specs=[out] * 5,
        out_shape=[f, f, f, f, jax.ShapeDtypeStruct((s, D_MODEL), BF16)],
        scratch_shapes=[pltpu.VMEM((s, CW), F32), pltpu.VMEM((s, CW), F32)],
        compiler_params=_params(),
    )(proj, proj, wl, bl, wa, ba, wx, bx, lam)


def branch_merge_fwd(ya, yb, wcb, wlb, proj):
    s = ya.shape[0]
    nblk = D_MODEL // CW

    def body(ya_ref, yb_ref, wcb_ref, wlb_ref, gc_ref, gl_ref, a_ref, b_ref, m_ref):
        a = _dot(ya_ref[...], wcb_ref[...])
        b = _dot(yb_ref[...], wlb_ref[...])
        a_ref[...] = a
        b_ref[...] = b
        m_ref[...] = (_sigmoid(gc_ref[...]) * a + _sigmoid(gl_ref[...]) * b).astype(BF16)

    res = pl.BlockSpec((s, D_MODEL), lambda n: (0, 0))
    wcol = pl.BlockSpec((D_MODEL, CW), lambda n: (0, n))
    blk = lambda k: pl.BlockSpec((s, CW), lambda n, k=k: (0, k * nblk + n))
    out = pl.BlockSpec((s, CW), lambda n: (0, n))
    f = jax.ShapeDtypeStruct((s, D_MODEL), F32)
    return pl.pallas_call(
        body, name="branch_merge_fwd", grid=(nblk,),
        in_specs=[res, res, wcol, wcol, blk(5), blk(6)],
        out_specs=[out] * 3,
        out_shape=[f, f, jax.ShapeDtypeStruct((s, D_MODEL), BF16)],
        compiler_params=_params(),
    )(ya, yb, wcb, wlb, proj, proj)


def mix_out_fwd(merged, wout, x, g2, g3):
    s, d = x.shape
    t = _token_tile(s)

    def body(m_ref, w_ref, x_ref, g2_ref, g3_ref, mix_ref, x2_ref, h2_ref):
        mix = _dot(m_ref[...], w_ref[...])
        mix_ref[...] = mix
        n, _ = _rms_stats(mix)
        x2 = x_ref[...] + n * g2_ref[...]
        x2_ref[...] = x2
        n2, _ = _rms_stats(x2)
        h2_ref[...] = (n2 * g3_ref[...]).astype(BF16)

    tile = pl.BlockSpec((t, d), lambda i: (i, 0))
    vec = pl.BlockSpec((1, d), lambda i: (0, 0))
    f = jax.ShapeDtypeStruct((s, d), F32)
    return pl.pallas_call(
        body, name="mix_out_fwd", grid=(s // t,),
        in_specs=[tile, pl.BlockSpec((d, d), lambda i: (0, 0)), tile, vec, vec],
        out_specs=[tile] * 3,
        out_shape=[f, f, jax.ShapeDtypeStruct((s, d), BF16)],
        compiler_params=_params(),
    )(merged, wout, x, g2, g3)


def ffn_act_fwd(up, fw, fb):
    s = up.shape[0]
    nblk = D_FF // FW

    def body(ug_ref, uv_ref, wg_ref, wv_ref, bg_ref, bv_ref, f_ref):
        gate = _causal_conv(ug_ref[...], wg_ref, bg_ref[...])
        val = _causal_conv(uv_ref[...], wv_ref, bv_ref[...])
        f_ref[...] = (_gelu(gate) * val).astype(BF16)

    half = lambda h, rows: pl.BlockSpec((rows, FW), lambda n, h=h: (0, h * nblk + n))
    return pl.pallas_call(
        body, name="ffn_act_fwd", grid=(nblk,),
        in_specs=[half(0, s), half(1, s), half(0, 3), half(1, 3), half(0, 1), half(1, 1)],
        out_specs=pl.BlockSpec((s, FW), lambda n: (0, n)),
        out_shape=jax.ShapeDtypeStruct((s, D_FF), BF16),
        compiler_params=_params(),
    )(up, up, fw, fw, fb, fb)


def ffn_down_loss(f, wdown, x2, target, g4):
    s, d = x2.shape
    t = _token_tile(s)

    def body(f_ref, w_ref, x2_ref, tg_ref, g4_ref, dy_ref, dout_ref, loss_ref, dg4_ref):
        @pl.when(pl.program_id(0) == 0)
        def _():
            loss_ref[...] = jnp.zeros_like(loss_ref)
            dg4_ref[...] = jnp.zeros_like(dg4_ref)

        out = _dot(f_ref[...], w_ref[...])
        n, r = _rms_stats(out)
        err = x2_ref[...] + n * g4_ref[...] - tg_ref[...]
        loss_ref[...] += jnp.full(loss_ref.shape, (0.5 / d) * jnp.sum(err * err), F32)
        dy = err * (1.0 / d)
        dy_ref[...] = dy
        dout, dg = _rms_bwd(n, r, g4_ref[...], dy)
        dout_ref[...] = dout.astype(BF16)
        dg4_ref[...] += jnp.sum(dg, axis=0, keepdims=True)

    tile = pl.BlockSpec((t, d), lambda i: (i, 0))
    vec = pl.BlockSpec((1, d), lambda i: (0, 0))
    return pl.pallas_call(
        body, name="ffn_down_loss", grid=(s // t,),
        in_specs=[pl.BlockSpec((t, D_FF), lambda i: (i, 0)), pl.BlockSpec((D_FF, d), lambda i: (0, 0)), tile, tile, vec],
        out_specs=[tile, tile, pl.BlockSpec((1, 128), lambda i: (0, 0)), vec],
        out_shape=[jax.ShapeDtypeStruct((s, d), F32), jax.ShapeDtypeStruct((s, d), BF16),
                   jax.ShapeDtypeStruct((1, 128), F32), jax.ShapeDtypeStruct((1, d), F32)],
        compiler_params=_params(),
    )(f, wdown, x2, target, g4)


def ffn_bwd(dout, wdown, up, fw, fb):
    s = up.shape[0]
    nblk = D_FF // FW

    def body(do_ref, wd_ref, ug_ref, uv_ref, wg_ref, wv_ref, bg_ref, bv_ref,
             dug_ref, duv_ref, dwd_ref, dwg_ref, dwv_ref, dbg_ref, dbv_ref):
        do = do_ref[...]
        df = _dot_nt(do, wd_ref[...])
        ug = ug_ref[...]
        uv = uv_ref[...]
        gate = _causal_conv(ug, wg_ref, bg_ref[...])
        val = _causal_conv(uv, wv_ref, bv_ref[...])
        ge, dge = _gelu_and_grad(gate)
        dwd_ref[...] = _dot_tn((ge * val).astype(BF16), do).astype(BF16)
        dgate = df * val * dge
        dval = df * ge
        dug_ref[...] = _causal_conv_t(dgate, wg_ref).astype(BF16)
        duv_ref[...] = _causal_conv_t(dval, wv_ref).astype(BF16)
        dwg_ref[...] = _conv_wgrad(dgate, ug, 3)
        dwv_ref[...] = _conv_wgrad(dval, uv, 3)
        dbg_ref[...] = jnp.sum(dgate, axis=0, keepdims=True)
        dbv_ref[...] = jnp.sum(dval, axis=0, keepdims=True)

    half = lambda h, rows: pl.BlockSpec((rows, FW), lambda n, h=h: (0, h * nblk + n))
    own = lambda rows: pl.BlockSpec((rows, FW), lambda n: (0, n))
    act = jax.ShapeDtypeStruct((s, D_FF), BF16)
    taps = jax.ShapeDtypeStruct((3, D_FF), F32)
    bias = jax.ShapeDtypeStruct((1, D_FF), F32)
    return pl.pallas_call(
        body, name="ffn_bwd", grid=(nblk,),
        in_specs=[pl.BlockSpec((s, D_MODEL), lambda n: (0, 0)), pl.BlockSpec((FW, D_MODEL), lambda n: (n, 0)),
                  half(0, s), half(1, s), half(0, 3), half(1, 3), half(0, 1), half(1, 1)],
        out_specs=[own(s), own(s), pl.BlockSpec((FW, D_MODEL), lambda n: (n, 0)), own(3), own(3), own(1), own(1)],
        out_shape=[act, act, jax.ShapeDtypeStruct((D_FF, D_MODEL), BF16), taps, taps, bias, bias],
        compiler_params=_params(),
    )(dout, wdown, up, up, fw, fw, fb, fb)


def dgrad_wgrad_cols(dy, w4, a, name):
    m, k = a.shape
    nj, _, ns = w4.shape
    nb = ns // CW

    def body(dy_ref, w_ref, a_ref, da_ref, dw_ref):
        @pl.when((pl.program_id(0) == 0) & (pl.program_id(1) == 0))
        def _():
            da_ref[...] = jnp.zeros_like(da_ref)

        dyb = dy_ref[...]
        da_ref[...] += _dot_nt(dyb, w_ref[0])
        dw_ref[...] = _dot_tn(a_ref[...], dyb).astype(BF16)

    return pl.pallas_call(
        body, name=name, grid=(nj, nb),
        in_specs=[pl.BlockSpec((m, CW), lambda j, b: (0, j * nb + b)),
                  pl.BlockSpec((1, k, CW), lambda j, b: (j, 0, b)),
                  pl.BlockSpec((m, k), lambda j, b: (0, 0))],
        out_specs=[pl.BlockSpec((m, k), lambda j, b: (0, 0)),
                   pl.BlockSpec((k, CW), lambda j, b: (0, j * nb + b))],
        out_shape=[jax.ShapeDtypeStruct((m, k), F32), jax.ShapeDtypeStruct((k, nj * ns), BF16)],
        compiler_params=_params(),
    )(dy, w4, a)


def norms_mid_bwd(dh2, x2, dy, mix, g3, g2):
    s, d = x2.shape
    t = _token_tile(s)

    def body(dh2_ref, x2_ref, dy_ref, mix_ref, g3_ref, g2_ref, dx2_ref, dmix_ref, dg3_ref, dg2_ref):
        @pl.when(pl.program_id(0) == 0)
        def _():
            dg3_ref[...] = jnp.zeros_like(dg3_ref)
            dg2_ref[...] = jnp.zeros_like(dg2_ref)

        n3, r3 = _rms_stats(x2_ref[...])
        dx, dg3 = _rms_bwd(n3, r3, g3_ref[...], dh2_ref[...])
        dx2 = dy_ref[...] + dx
        dx2_ref[...] = dx2
        dg3_ref[...] += jnp.sum(dg3, axis=0, keepdims=True)
        n2, r2 = _rms_stats(mix_ref[...])
        dmix, dg2 = _rms_bwd(n2, r2, g2_ref[...], dx2)
        dmix_ref[...] = dmix.astype(BF16)
        dg2_ref[...] += jnp.sum(dg2, axis=0, keepdims=True)

    tile = pl.BlockSpec((t, d), lambda i: (i, 0))
    vec = pl.BlockSpec((1, d), lambda i: (0, 0))
    v = jax.ShapeDtypeStruct((1, d), F32)
    return pl.pallas_call(
        body, name="norms_mid_bwd", grid=(s // t,),
        in_specs=[tile, tile, tile, tile, vec, vec],
        out_specs=[tile, tile, vec, vec],
        out_shape=[jax.ShapeDtypeStruct((s, d), F32), jax.ShapeDtypeStruct((s, d), BF16), v, v],
        compiler_params=_params(),
    )(dh2, x2, dy, mix, g3, g2)


def mix_out_bwd(dmix, wout, merged, a, b, proj):
    s = dmix.shape[0]
    nblk = D_MODEL // CW

    def body(dm_ref, w_ref, mg_ref, a_ref, b_ref, gc_ref, gl_ref, da_ref, db_ref, dw_ref, dgc_ref, dgl_ref):
        dm = dm_ref[...]
        dmerged = _dot_nt(dm, w_ref[...])
        dw_ref[...] = _dot_tn(mg_ref[...], dm).astype(BF16)
        sc = _sigmoid(gc_ref[...])
        sl = _sigmoid(gl_ref[...])
        da_ref[...] = (dmerged * sc).astype(BF16)
        db_ref[...] = (dmerged * sl).astype(BF16)
        dgc_ref[...] = (dmerged * a_ref[...] * sc * (1.0 - sc)).astype(BF16)
        dgl_ref[...] = (dmerged * b_ref[...] * sl * (1.0 - sl)).astype(BF16)

    res = pl.BlockSpec((s, D_MODEL), lambda n: (0, 0))
    rows = pl.BlockSpec((CW, D_MODEL), lambda n: (n, 0))
    col = pl.BlockSpec((s, CW), lambda n: (0, n))
    blk = lambda k: pl.BlockSpec((s, CW), lambda n, k=k: (0, k * nblk + n))
    hb = jax.ShapeDtypeStruct((s, D_MODEL), BF16)
    outs = pl.pallas_call(
        body, name="mix_out_bwd", grid=(nblk,),
        in_specs=[res, rows, col, col, col, blk(5), blk(6)],
        out_specs=[col, col, rows, blk(5), blk(6)],
        out_shape=[hb, hb, jax.ShapeDtypeStruct((D_MODEL, D_MODEL), BF16),
                   jax.ShapeDtypeStruct((s, N_SEG * D_MODEL), BF16), jax.ShapeDtypeStruct((s, N_SEG * D_MODEL), BF16)],
        compiler_params=_params(),
    )(dmix, wout, merged, a, b, proj, proj)
    return outs


def mix_conv_bwd(da, wcb, proj, q, ws):
    s = da.shape[0]
    nblk = D_MODEL // CW

    def body(da_ref, w_ref, cb_ref, cc_ref, cx_ref, q_ref, ws_ref, dcb_ref, dcc_ref, dcx_ref, dw_ref, dws_ref):
        dab = da_ref[...]
        dya = _dot_nt(dab, w_ref[...])
        cb = cb_ref[...]
        cc = cc_ref[...]
        cx = cx_ref[...]
        q = q_ref[...]
        dw_ref[...] = _dot_tn((cb * q).astype(BF16), dab).astype(BF16)
        dcb_ref[...] = (dya * q).astype(BF16)
        dq = dya * cb
        dp = _causal_conv_t(dq, ws_ref)
        dws_ref[...] = _conv_wgrad(dq, cc * cx, 3)
        dcc_ref[...] = (dp * cx).astype(BF16)
        dcx_ref[...] = (dp * cc).astype(BF16)

    res = pl.BlockSpec((s, D_MODEL), lambda n: (0, 0))
    rows = pl.BlockSpec((CW, D_MODEL), lambda n: (n, 0))
    col = pl.BlockSpec((s, CW), lambda n: (0, n))
    blk = lambda k: pl.BlockSpec((s, CW), lambda n, k=k: (0, k * nblk + n))
    taps = pl.BlockSpec((3, CW), lambda n: (0, n))
    hb = jax.ShapeDtypeStruct((s, D_MODEL), BF16)
    return pl.pallas_call(
        body, name="mix_conv_bwd", grid=(nblk,),
        in_specs=[res, rows, blk(0), blk(1), blk(2), col, taps],
        out_specs=[col, col, col, rows, taps],
        out_shape=[hb, hb, hb, jax.ShapeDtypeStruct((D_MODEL, D_MODEL), BF16), jax.ShapeDtypeStruct((3, D_MODEL), F32)],
        compiler_params=_params(),
    )(da, wcb, proj, proj, proj, q, ws)


def mix_lru_bwd(db, wlb, proj, xl, r, i, h, wl, wa, wx, lam):
    s = db.shape[0]
    nblk = D_MODEL // CW

    def body(db_ref, w_ref, lx_ref, ly_ref, xl_ref, r_ref, i_ref, h_ref, wl_ref, wa_ref, wx_ref, lam_ref,
             dlx_ref, dly_ref, dw_ref, dwa_ref, dwx_ref, dba_ref, dbx_ref, dwl_ref, dbl_ref, dlam_ref,
             c_scr, dh_scr, g_scr):
        dbb = db_ref[...]
        dyb = _dot_nt(dbb, w_ref[...])
        h = h_ref[...]
        ge, dge = _gelu_and_grad(ly_ref[...])
        dw_ref[...] = _dot_tn((h * ge).astype(BF16), dbb).astype(BF16)
        dly_ref[...] = (dyb * h * dge).astype(BF16)
        r = r_ref[...]
        gi = i_ref[...]
        xl = xl_ref[...]
        lam = lam_ref[...]
        ls = _log_sigmoid(lam)
        a, mult = _lru_gates(r, ls)
        c_scr[...] = _shift_up(a, 1)
        dh_scr[...] = dyb * ge
        _scan_backward(c_scr, dh_scr, g_scr)
        du = g_scr[...]
        da = du * _shift_down(h, 1)
        dmult = du * gi * xl
        di = du * mult * xl
        dxl = du * mult * gi
        first = _rows(a.shape) == 0
        dlog_a = da * a - jnp.where(first, 0.0, dmult * a * a / mult)
        dr = dlog_a * (LRU_C * ls)
        dlam_ref[...] = jnp.sum(dlog_a * r, axis=0, keepdims=True) * (LRU_C * (1.0 - _sigmoid(lam)))
        dzr = dr * r * (1.0 - r)
        dzi = di * gi * (1.0 - gi)
        dba_ref[...] = jnp.sum(dzr, axis=0, keepdims=True)
        dbx_ref[...] = jnp.sum(dzi, axis=0, keepdims=True)
        xlb = xl.astype(BF16)
        dzrb = dzr.astype(BF16)
        dzib = dzi.astype(BF16)
        dwa_ref[0] = _dot_tn(xlb, dzrb)
        dwx_ref[0] = _dot_tn(xlb, dzib)
        dxl = dxl + _dot_nt(dzrb, wa_ref[0]) + _dot_nt(dzib, wx_ref[0])
        dlx_ref[...] = _causal_conv_t(dxl, wl_ref).astype(BF16)
        dwl_ref[...] = _conv_wgrad(dxl, lx_ref[...], 4)
        dbl_ref[...] = jnp.sum(dxl, axis=0, keepdims=True)

    res = pl.BlockSpec((s, D_MODEL), lambda n: (0, 0))
    rows = pl.BlockSpec((CW, D_MODEL), lambda n: (n, 0))
    col = pl.BlockSpec((s, CW), lambda n: (0, n))
    blk = lambda k: pl.BlockSpec((s, CW), lambda n, k=k: (0, k * nblk + n))
    taps = pl.BlockSpec((4, CW), lambda n: (0, n))
    vec = pl.BlockSpec((1, CW), lambda n: (0, n))
    mat = pl.BlockSpec((1, CW, CW), lambda n: (n, 0, 0))
    hb = jax.ShapeDtypeStruct((s, D_MODEL), BF16)
    v = jax.ShapeDtypeStruct((1, D_MODEL), F32)
    m = jax.ShapeDtypeStruct((LRU_HEADS, HEAD_DIM, HEAD_DIM), F32)
    scr = pltpu.VMEM((s, CW), F32)
    return pl.pallas_call(
        body, name="mix_lru_bwd", grid=(nblk,),
        in_specs=[res, rows, blk(3), blk(4), col, col, col, col, taps, mat, mat, vec],
        out_specs=[col, col, rows, mat, mat, vec, vec, taps, vec, vec],
        out_shape=[hb, hb, jax.ShapeDtypeStruct((D_MODEL, D_MODEL), BF16), m, m, v, v,
                   jax.ShapeDtypeStruct((4, D_MODEL), F32), v, v],
        scratch_shapes=[scr, scr, scr],
        compiler_params=_params(),
    )(db, wlb, proj, proj, xl, r, i, h, wl, wa, wx, lam)


def norm_in_bwd(dh1, x, dx2, g1):
    s, d = x.shape
    t = _token_tile(s)

    def body(dh_ref, x_ref, dx2_ref, g_ref, dx_ref, dg_ref):
        @pl.when(pl.program_id(0) == 0)
        def _():
            dg_ref[...] = jnp.zeros_like(dg_ref)

        n, r = _rms_stats(x_ref[...])
        dx, dg = _rms_bwd(n, r, g_ref[...], dh_ref[...])
        dx_ref[...] = dx2_ref[...] + dx
        dg_ref[...] += jnp.sum(dg, axis=0, keepdims=True)

    tile = pl.BlockSpec((t, d), lambda i: (i, 0))
    vec = pl.BlockSpec((1, d), lambda i: (0, 0))
    return pl.pallas_call(
        body, name="norm_in_bwd", grid=(s // t,),
        in_specs=[tile, tile, tile, vec],
        out_specs=[tile, vec],
        out_shape=[jax.ShapeDtypeStruct((s, d), F32), jax.ShapeDtypeStruct((1, d), F32)],
        compiler_params=_params(),
    )(dh1, x, dx2, g1)


def local_step(x, target, g1, g2, g3, g4, win4, ws, wcb, wl, bl, wa, ba, wx, bx, lam, wlb, wout, wup4, fw, fb, wdown):
    h1 = norm_in(x, g1)
    proj = matmul_cols(h1, win4, "proj_fwd")
    q, ya = mix_conv_fwd(proj, ws)
    xl, r, gi, h, yb = mix_lru_fwd(proj, wl, bl, wa, ba, wx, bx, lam)
    a, b, merged = branch_merge_fwd(ya, yb, wcb, wlb, proj)
    mix, x2, h2 = mix_out_fwd(merged, wout, x, g2, g3)
    up = matmul_cols(h2, wup4, "up_fwd")
    f = ffn_act_fwd(up, fw, fb)
    dy, dout, loss, dg4 = ffn_down_loss(f, wdown, x2, target, g4)

    dug, duv, dwdown, dfw_g, dfw_v, dfb_g, dfb_v = ffn_bwd(dout, wdown, up, fw, fb)
    dup = jnp.concatenate([dug, duv], axis=1)
    dfw = jnp.concatenate([dfw_g, dfw_v], axis=1)
    dfb = jnp.concatenate([dfb_g, dfb_v], axis=1)
    dh2, dwup = dgrad_wgrad_cols(dup, wup4, h2, "up_bwd")
    dx2, dmix, dg3, dg2 = norms_mid_bwd(dh2, x2, dy, mix, g3, g2)
    da, db, dwout, dgc, dgl = mix_out_bwd(dmix, wout, merged, a, b, proj)
    dcb, dcc, dcx, dwcb, dws = mix_conv_bwd(da, wcb, proj, q, ws)
    dlx, dly, dwlb, dwa, dwx, dba, dbx, dwl, dbl, dlam = mix_lru_bwd(db, wlb, proj, xl, r, gi, h, wl, wa, wx, lam)
    dproj = jnp.concatenate([dcb, dcc, dcx, dlx, dly, dgc[:, 5 * D_MODEL:6 * D_MODEL], dgl[:, 6 * D_MODEL:]], axis=1)
    dh1, dwin = dgrad_wgrad_cols(dproj, win4, h1, "proj_bwd")
    dx, dg1 = norm_in_bwd(dh1, x, dx2, g1)
    grads = dict(norm_mix_pre=dg1, norm_mix_post=dg2, norm_ffn_pre=dg3, norm_ffn_post=dg4,
                 w_in=dwin, conv_short_w=dws, w_conv_branch=dwcb, lru_conv_w=dwl, lru_conv_b=dbl,
                 lru_wa=dwa, lru_ba=dba, lru_wx=dwx, lru_bx=dbx, lru_lambda=dlam,
                 w_lru_branch=dwlb, w_out=dwout, ffn_w_up=dwup, ffn_conv_w=dfw, ffn_conv_b=dfb,
                 ffn_w_down=dwdown)
    return loss[0, 0], dx, grads


MESH = pl.DeviceIdType.MESH
_HBM = pl.BlockSpec(memory_space=pltpu.HBM)
_OTHER_CHIPS = ((1, 0), (0, 1), (1, 1))
_OTHER_DEVICES = tuple((dx, dy, dc) for dx in (0, 1) for dy in (0, 1) for dc in (0, 1) if dx or dy or dc)
N_DEVICES = 8


def _position():
    return lax.axis_index("x"), lax.axis_index("y"), lax.axis_index("c")


def _flip(v, d):
    return 1 - v if d else v


def _half_rows(ref, h, hr):
    return ref.at[pl.ds(h * hr, hr), :]


def gather_chips(shards):
    n = len(shards)
    nrel = len(_OTHER_CHIPS)

    def body(*refs):
        ins, outs = refs[:n], refs[n:2 * n]
        ici_send, ici_recv, sib_send, sib_recv = refs[2 * n:]
        x, y, c = _position()
        j = 2 * x + y
        hr = [s.shape[0] // 2 for s in shards]

        def chip(p):
            px, py = _flip(x, _OTHER_CHIPS[p][0]), _flip(y, _OTHER_CHIPS[p][1])
            return px, py, 2 * px + py

        def ici(a, p, slot):
            px, py, _ = chip(p)
            return pltpu.make_async_remote_copy(
                src_ref=_half_rows(ins[a], c, hr[a]), dst_ref=_half_rows(outs[a].at[slot], c, hr[a]),
                send_sem=ici_send.at[a * nrel + p], recv_sem=ici_recv.at[a * nrel + p],
                device_id=(px, py, c), device_id_type=MESH)

        def sib(a, p, h):
            _, _, k = chip(p)
            part = _half_rows(outs[a].at[k], h, hr[a])
            return pltpu.make_async_remote_copy(
                src_ref=part, dst_ref=part, send_sem=sib_send.at[a * nrel + p], recv_sem=sib_recv.at[a * nrel + p],
                device_id=(x, y, 1 - c), device_id_type=MESH)

        pairs = [(a, p) for a in range(n) for p in range(nrel)]
        for a, p in pairs:
            ici(a, p, j).start()
        for a, p in pairs:
            ici(a, p, chip(p)[2]).wait_recv()
            sib(a, p, c).start()
        for a, p in pairs:
            sib(a, p, 1 - c).wait_recv()
        for a, p in pairs:
            ici(a, p, j).wait_send()
            sib(a, p, c).wait_send()

    got = pl.pallas_call(
        body, name="gather_chips",
        in_specs=[_HBM] * n, out_specs=[_HBM] * n,
        out_shape=[jax.ShapeDtypeStruct((N_CHIPS,) + s.shape, s.dtype) for s in shards],
        scratch_shapes=[pltpu.SemaphoreType.DMA((n * nrel,))] * 4,
    )(*shards)
    j = 2 * lax.axis_index("x") + lax.axis_index("y")
    return [lax.dynamic_update_slice(g, s[None], (j, 0, 0)) for g, s in zip(got, shards)]


def _owned_part(ref, kind, k, h, hr):
    if kind == "col":
        ns = ref.shape[1] // N_CHIPS
        return ref.at[pl.ds(h * hr, hr), pl.ds(k * ns, ns)]
    if kind == "row":
        return ref.at[pl.ds(k * 2 * hr + h * hr, hr), :]
    return ref.at[k, pl.ds(h * hr, hr), :]


def _part_shape(g, kind):
    if kind == "col":
        return g.shape[0] // 2, g.shape[1] // N_CHIPS
    if kind == "row":
        return g.shape[0] // (2 * N_CHIPS), g.shape[1]
    return g.shape[1] // 2, g.shape[2]


def pair_split(grads, kinds):
    n = len(grads)
    shapes = [_part_shape(g, k) for g, k in zip(grads, kinds)]

    def body(*refs):
        ins, theirs = refs[:n], refs[n:2 * n]
        send_sem, recv_sem = refs[2 * n:]
        x, y, c = _position()
        copies = []
        for a in range(n):
            hr = shapes[a][0]
            for k in range(N_CHIPS):
                s = a * N_CHIPS + k
                copies.append(pltpu.make_async_remote_copy(
                    src_ref=_owned_part(ins[a], kinds[a], k, 1 - c, hr), dst_ref=theirs[a].at[k],
                    send_sem=send_sem.at[s], recv_sem=recv_sem.at[s], device_id=(x, y, 1 - c), device_id_type=MESH))
        for cp in copies:
            cp.start()
        for cp in copies:
            cp.wait()

    return pl.pallas_call(
        body, name="pair_split",
        in_specs=[_HBM] * n, out_specs=[_HBM] * n,
        out_shape=[jax.ShapeDtypeStruct((N_CHIPS,) + shp, g.dtype) for shp, g in zip(shapes, grads)],
        scratch_shapes=[pltpu.SemaphoreType.DMA((n * N_CHIPS,))] * 2,
    )(*grads)


def chip_exchange(sums, rep):
    n = len(sums)
    nrel = len(_OTHER_CHIPS)
    ndev = len(_OTHER_DEVICES)

    def body(*refs):
        ins, rep_ref = refs[:n], refs[n]
        outs, rep_out = refs[n + 1:2 * n + 1], refs[2 * n + 1]
        loc_sem, send_sem, recv_sem, rep_send, rep_recv = refs[2 * n + 2:]
        x, y, c = _position()
        j = 2 * x + y
        me = 4 * x + 2 * y + c

        def chip(p):
            px, py = _flip(x, _OTHER_CHIPS[p][0]), _flip(y, _OTHER_CHIPS[p][1])
            return px, py, 2 * px + py

        def part(a, p, src_slot, dst_slot):
            px, py, _ = chip(p)
            return pltpu.make_async_remote_copy(
                src_ref=ins[a].at[src_slot], dst_ref=outs[a].at[dst_slot],
                send_sem=send_sem.at[a * nrel + p], recv_sem=recv_sem.at[a * nrel + p],
                device_id=(px, py, c), device_id_type=MESH)

        def device(q):
            dx, dy, dc = _OTHER_DEVICES[q]
            return _flip(x, dx), _flip(y, dy), _flip(c, dc)

        def rep_copy(q, slot):
            return pltpu.make_async_remote_copy(
                src_ref=rep_ref, dst_ref=rep_out.at[slot], send_sem=rep_send.at[q], recv_sem=rep_recv.at[q],
                device_id=device(q), device_id_type=MESH)

        own = [pltpu.make_async_copy(ins[a].at[j], outs[a].at[j], loc_sem.at[a]) for a in range(n)]
        own.append(pltpu.make_async_copy(rep_ref, rep_out.at[me], loc_sem.at[n]))
        for cp in own:
            cp.start()
        pairs = [(a, p) for a in range(n) for p in range(nrel)]
        for a, p in pairs:
            part(a, p, chip(p)[2], j).start()
        for q in range(ndev):
            rep_copy(q, me).start()
        for a, p in pairs:
            part(a, p, chip(p)[2], chip(p)[2]).wait_recv()
        for q in range(ndev):
            px, py, pc = device(q)
            rep_copy(q, 4 * px + 2 * py + pc).wait_recv()
        for a, p in pairs:
            part(a, p, chip(p)[2], j).wait_send()
        for q in range(ndev):
            rep_copy(q, me).wait_send()
        for cp in own:
            cp.wait()

    return pl.pallas_call(
        body, name="chip_exchange",
        in_specs=[_HBM] * (n + 1), out_specs=[_HBM] * (n + 1),
        out_shape=[jax.ShapeDtypeStruct(s.shape, s.dtype) for s in sums]
        + [jax.ShapeDtypeStruct((N_DEVICES,) + rep.shape, rep.dtype)],
        scratch_shapes=[pltpu.SemaphoreType.DMA((n + 1,)), pltpu.SemaphoreType.DMA((n * nrel,)),
                        pltpu.SemaphoreType.DMA((n * nrel,)), pltpu.SemaphoreType.DMA((ndev,)),
                        pltpu.SemaphoreType.DMA((ndev,))],
    )(*sums, rep)


def pair_swap(halves):
    n = len(halves)

    def body(*refs):
        ins, outs = refs[:n], refs[n:2 * n]
        send_sem, recv_sem = refs[2 * n:]
        x, y, c = _position()
        copies = [pltpu.make_async_remote_copy(
            src_ref=ins[a], dst_ref=outs[a], send_sem=send_sem.at[a], recv_sem=recv_sem.at[a],
            device_id=(x, y, 1 - c), device_id_type=MESH) for a in range(n)]
        for cp in copies:
            cp.start()
        for cp in copies:
            cp.wait()

    return pl.pallas_call(
        body, name="pair_swap",
        in_specs=[_HBM] * n, out_specs=[_HBM] * n,
        out_shape=[jax.ShapeDtypeStruct(h.shape, h.dtype) for h in halves],
        scratch_shapes=[pltpu.SemaphoreType.DMA((n,))] * 2,
    )(*halves)


def _row_tile(rows, cols, limit_bytes=1 << 20):
    best = None
    for t in range(SUBLANES, rows + 1, SUBLANES):
        if rows % t == 0 and t * cols * 4 <= limit_bytes:
            best = t
    return best or rows


def add_pair(g, kind, theirs, core, name):
    nc, rows, cols = theirs.shape
    t = _row_tile(rows, cols)
    nt = rows // t

    def body(core_ref, g_ref, b_ref, o_ref):
        mine = g_ref[...].reshape(t, cols)
        o_ref[0] = (mine.astype(F32) + b_ref[0].astype(F32)).astype(o_ref.dtype)

    if kind == "col":
        own = pl.BlockSpec((t, cols), lambda k, i, c: (c[0] * nt + i, k))
    elif kind == "row":
        own = pl.BlockSpec((t, cols), lambda k, i, c: ((2 * k + c[0]) * nt + i, 0))
    else:
        own = pl.BlockSpec((1, t, cols), lambda k, i, c: (k, c[0] * nt + i, 0))
    spec = pl.BlockSpec((1, t, cols), lambda k, i, c: (k, i, 0))
    return pl.pallas_call(
        body, name=name,
        grid_spec=pltpu.PrefetchScalarGridSpec(num_scalar_prefetch=1, grid=(nc, nt), in_specs=[own, spec], out_specs=spec),
        out_shape=jax.ShapeDtypeStruct(theirs.shape, theirs.dtype), compiler_params=_params(),
    )(core, g, theirs)


def sum_lead(a, name):
    nl, rows, cols = a.shape
    t = _row_tile(rows, cols, (1 << 20) // 2)

    def body(a_ref, o_ref):
        acc = a_ref[0].astype(F32)
        for s in range(1, nl):
            acc = acc + a_ref[s].astype(F32)
        o_ref[...] = acc

    return pl.pallas_call(
        body, name=name, grid=(rows // t,),
        in_specs=[pl.BlockSpec((nl, t, cols), lambda i: (0, i, 0))],
        out_specs=pl.BlockSpec((t, cols), lambda i: (i, 0)),
        out_shape=jax.ShapeDtypeStruct((rows, cols), F32), compiler_params=_params(),
    )(a)


def _adamw_update(w, g, m, v):
    nm = ADAM_B1 * m + (1.0 - ADAM_B1) * g
    nv = ADAM_B2 * v + (1.0 - ADAM_B2) * (g * g)
    m_hat = nm * (1.0 / (1.0 - ADAM_B1 ** ADAM_STEP))
    v_hat = nv * (1.0 / (1.0 - ADAM_B2 ** ADAM_STEP))
    return -ADAM_LR * (m_hat / (jnp.sqrt(v_hat) + ADAM_EPS) + ADAM_WD * w), nm, nv


def adamw(w, g, m, v, name):
    rows, cols = w.shape
    t = _row_tile(rows, cols)

    def body(w_ref, g_ref, m_ref, v_ref, d_ref, nm_ref, nv_ref):
        d_ref[...], nm_ref[...], nv_ref[...] = _adamw_update(w_ref[...], g_ref[...], m_ref[...], v_ref[...])

    spec = pl.BlockSpec((t, cols), lambda i: (i, 0))
    shp = jax.ShapeDtypeStruct((rows, cols), F32)
    return pl.pallas_call(
        body, name=name, grid=(rows // t,), in_specs=[spec] * 4, out_specs=[spec] * 3,
        out_shape=[shp, shp, shp], compiler_params=_params(),
    )(w, g, m, v)


def adamw_halves(w, g_mine, g_other, m, v, core, name):
    rows, cols = w.shape
    hr = rows // 2
    t = _row_tile(hr, cols)
    nt = hr // t

    def body(core_ref, w_ref, gm_ref, go_ref, m_ref, v_ref, g_ref, d_ref, nm_ref, nv_ref):
        g = jnp.where(pl.program_id(0) // nt == core_ref[0], gm_ref[...], go_ref[...])
        g_ref[...] = g
        d_ref[...], nm_ref[...], nv_ref[...] = _adamw_update(w_ref[...], g, m_ref[...], v_ref[...])

    spec = pl.BlockSpec((t, cols), lambda i, c: (i, 0))
    half = pl.BlockSpec((t, cols), lambda i, c: (i % nt, 0))
    shp = jax.ShapeDtypeStruct((rows, cols), F32)
    return pl.pallas_call(
        body, name=name,
        grid_spec=pltpu.PrefetchScalarGridSpec(num_scalar_prefetch=1, grid=(2 * nt,),
                                               in_specs=[spec, half, half, spec, spec], out_specs=[spec] * 4),
        out_shape=[shp] * 4, compiler_params=_params(),
    )(core, w, g_mine, g_other, m, v)


WEIGHTS = ("norm_mix_pre", "norm_mix_post", "norm_ffn_pre", "norm_ffn_post", "w_in", "conv_short_w",
           "w_conv_branch", "lru_conv_w", "lru_conv_b", "lru_wa", "lru_ba", "lru_wx", "lru_bx", "lru_lambda",
           "w_lru_branch", "w_out", "ffn_w_up", "ffn_conv_w", "ffn_conv_b", "ffn_w_down")
BIG = ("w_in", "ffn_w_up", "w_conv_branch", "w_lru_branch", "w_out", "ffn_w_down")
BIG_KIND = ("col", "col", "row", "row", "row", "row")
SMALL = ("conv_short_w", "lru_conv_w", "lru_wa", "lru_ba", "lru_wx", "lru_bx", "ffn_conv_w")
REPL = ("norm_mix_pre", "norm_mix_post", "norm_ffn_pre", "norm_ffn_post", "lru_conv_b", "lru_lambda", "ffn_conv_b")
PACK_W = 256
SMALL_ROWS = 544
REPL_ROWS = 16
FFN_SHARD = 2 * D_FF // N_CHIPS
QUARTER = HEAD_DIM // N_CHIPS


def _pack_small_shard(p):
    rows = [p["conv_short_w"].reshape(3, PACK_W), p["lru_conv_w"].reshape(4, PACK_W),
            p["lru_wa"].reshape(LRU_HEADS * QUARTER, PACK_W), p["lru_ba"].reshape(1, PACK_W),
            p["lru_wx"].reshape(LRU_HEADS * QUARTER, PACK_W), p["lru_bx"].reshape(1, PACK_W),
            p["ffn_conv_w"].reshape(3 * FFN_SHARD // PACK_W, PACK_W)]
    used = sum(r.shape[0] for r in rows)
    return jnp.concatenate(rows + [jnp.zeros((SMALL_ROWS - used, PACK_W), F32)], axis=0)


def _unpack_small_shard(buf):
    out, r = {}, 0
    for name, nr, shape in (("conv_short_w", 3, (1, 3, PACK_W)), ("lru_conv_w", 4, (1, 4, PACK_W)),
                            ("lru_wa", LRU_HEADS * QUARTER, (1, LRU_HEADS, QUARTER, HEAD_DIM)),
                            ("lru_ba", 1, (1, LRU_HEADS, QUARTER)),
                            ("lru_wx", LRU_HEADS * QUARTER, (1, LRU_HEADS, QUARTER, HEAD_DIM)),
                            ("lru_bx", 1, (1, LRU_HEADS, QUARTER)),
                            ("ffn_conv_w", 3 * FFN_SHARD // PACK_W, (1, 3, FFN_SHARD))):
        out[name] = buf[r:r + nr].reshape(shape)
        r += nr
    return out


def _full_small(g4):
    per = [_unpack_small_shard(g4[k]) for k in range(N_CHIPS)]
    cat = lambda name, axis: jnp.concatenate([per[k][name][0] for k in range(N_CHIPS)], axis=axis)
    return dict(conv_short_w=cat("conv_short_w", 1), lru_conv_w=cat("lru_conv_w", 1),
                lru_wa=cat("lru_wa", 1), lru_ba=cat("lru_ba", 1).reshape(1, D_MODEL),
                lru_wx=cat("lru_wx", 1), lru_bx=cat("lru_bx", 1).reshape(1, D_MODEL),
                ffn_conv_w=cat("ffn_conv_w", 1))


def _split_small(full):
    shards = []
    for k in range(N_CHIPS):
        cols = lambda a, w: a[:, k * w:(k + 1) * w]
        q = slice(k * QUARTER, (k + 1) * QUARTER)
        shards.append(_pack_small_shard(dict(
            conv_short_w=cols(full["conv_short_w"], PACK_W), lru_conv_w=cols(full["lru_conv_w"], PACK_W),
            lru_wa=full["lru_wa"][:, q, :], lru_ba=full["lru_ba"].reshape(LRU_HEADS, HEAD_DIM)[:, q],
            lru_wx=full["lru_wx"][:, q, :], lru_bx=full["lru_bx"].reshape(LRU_HEADS, HEAD_DIM)[:, q],
            ffn_conv_w=cols(full["ffn_conv_w"], FFN_SHARD))))
    return jnp.stack(shards)


def _pack_repl(p):
    rows = [p[n].reshape(-1, D_MODEL) for n in REPL]
    used = sum(r.shape[0] for r in rows)
    return jnp.concatenate(rows + [jnp.zeros((REPL_ROWS - used, D_MODEL), F32)], axis=0)


def _unpack_repl(buf):
    out, r = {}, 0
    for n in REPL:
        nr = (2 * D_FF // D_MODEL) if n == "ffn_conv_b" else 1
        out[n] = buf[r:r + nr].reshape(1, nr * D_MODEL)
        r += nr
    return out


def kernel(x, norm_mix_pre, norm_mix_post, norm_ffn_pre, norm_ffn_post, w_in, conv_short_w, w_conv_branch, lru_conv_w, lru_conv_b, lru_wa, lru_ba, lru_wx, lru_bx, lru_lambda, w_lru_branch, w_out, ffn_w_up, ffn_conv_w, ffn_conv_b, ffn_w_down, loss_target, m_norm_mix_pre, m_norm_mix_post, m_norm_ffn_pre, m_norm_ffn_post, m_w_in, m_conv_short_w, m_w_conv_branch, m_lru_conv_w, m_lru_conv_b, m_lru_wa, m_lru_ba, m_lru_wx, m_lru_bx, m_lru_lambda, m_w_lru_branch, m_w_out, m_ffn_w_up, m_ffn_conv_w, m_ffn_conv_b, m_ffn_w_down, v_norm_mix_pre, v_norm_mix_post, v_norm_ffn_pre, v_norm_ffn_post, v_w_in, v_conv_short_w, v_w_conv_branch, v_lru_conv_w, v_lru_conv_b, v_lru_wa, v_lru_ba, v_lru_wx, v_lru_bx, v_lru_lambda, v_w_lru_branch, v_w_out, v_ffn_w_up, v_ffn_conv_w, v_ffn_conv_b, v_ffn_w_down):
    given = dict(locals())
    w = {n: given[n] for n in WEIGHTS}
    m = {n: given["m_" + n] for n in WEIGHTS}
    v = {n: given["v_" + n] for n in WEIGHTS}

    big_shards = [w[n][0].astype(BF16) for n in BIG]
    gathered = gather_chips(big_shards + [_pack_small_shard(w)])
    win4, wup4 = gathered[0], gathered[1]
    wcb, wlb, wout, wdown = [g.reshape(-1, D_MODEL) for g in gathered[2:6]]
    small = _full_small(gathered[6])

    loss, dx, grads = local_step(
        x[0], loss_target[0], w["norm_mix_pre"], w["norm_mix_post"], w["norm_ffn_pre"], w["norm_ffn_post"],
        win4, small["conv_short_w"], wcb, small["lru_conv_w"], w["lru_conv_b"],
        small["lru_wa"].astype(BF16), small["lru_ba"], small["lru_wx"].astype(BF16), small["lru_bx"], w["lru_lambda"],
        wlb, wout, wup4, small["ffn_conv_w"], w["ffn_conv_b"], wdown)

    local = [grads[n] for n in BIG] + [_split_small(grads)]
    kinds = BIG_KIND + ("lead",)
    core = lax.axis_index("c").astype(jnp.int32).reshape(1)
    theirs = pair_split(local, kinds)
    chip_sums = [add_pair(g, k, t, core, "pair_add_%d" % i) for i, (g, k, t) in enumerate(zip(local, kinds, theirs))]
    exchanged = chip_exchange(chip_sums, _pack_repl(grads))
    halves = [sum_lead(e, "chip_sum_%d" % i) for i, e in enumerate(exchanged[:-1])]
    rep_grad = sum_lead(exchanged[-1], "device_sum")
    others = pair_swap(halves)

    g_out, d_out, m_out, v_out = {}, {}, {}, {}
    for n, gm, go in zip(BIG, halves[:-1], others[:-1]):
        g, d, nm, nv = adamw_halves(w[n][0], gm, go, m[n][0], v[n][0], core, "adamw_" + n)
        g_out[n], d_out[n], m_out[n], v_out[n] = g[None], d[None], nm[None], nv[None]
    bufs = adamw_halves(_pack_small_shard(w), halves[-1], others[-1], _pack_small_shard(m), _pack_small_shard(v),
                        core, "adamw_small")
    for dst, buf in zip((g_out, d_out, m_out, v_out), bufs):
        dst.update(_unpack_small_shard(buf))
    d, nm, nv = adamw(_pack_repl(w), rep_grad, _pack_repl(m), _pack_repl(v), "adamw_repl")
    for dst, buf in ((g_out, rep_grad), (d_out, d), (m_out, nm), (v_out, nv)):
        dst.update(_unpack_repl(buf))

    total = lax.psum(loss, ("x", "y", "c"))
    return (total, dx[None], *[g_out[n] for n in WEIGHTS], *[d_out[n] for n in WEIGHTS],
            *[m_out[n] for n in WEIGHTS], *[v_out[n] for n in WEIGHTS])
```

```python
import functools
import math

import jax
import jax.numpy as jnp
from jax import lax
from jax.experimental import pallas as pl
from jax.experimental.pallas import tpu as pltpu

F32 = jnp.float32
BF16 = jnp.bfloat16

D_MODEL = 1024
N_CHIPS = 4
N_SEG = 7
D_FF = 3 * D_MODEL
LRU_HEADS = 4
HEAD_DIM = D_MODEL // LRU_HEADS
LRU_C = 8.0
RMS_EPS = 1e-6
CW = 256
FW = 512
SUBLANES = 8
VMEM_LIMIT = 60 * 1024 * 1024

ADAM_LR = 0.001
ADAM_B1 = 0.9
ADAM_B2 = 0.999
ADAM_EPS = 1e-08
ADAM_WD = 0.01
ADAM_STEP = 10

_GELU_C = math.sqrt(2.0 / math.pi)
_GELU_K = 0.044715


def _params(**kw):
    return pltpu.CompilerParams(vmem_limit_bytes=VMEM_LIMIT, **kw)


def _sigmoid(x):
    return 1.0 / (1.0 + jnp.exp(-x))


def _gelu(x):
    t = jnp.tanh(_GELU_C * (x + _GELU_K * x * x * x))
    return 0.5 * x * (1.0 + t)


def _gelu_and_grad(x):
    x2 = x * x
    t = jnp.tanh(_GELU_C * (x + _GELU_K * x * x2))
    g = 0.5 * x * (1.0 + t)
    dg = 0.5 * (1.0 + t) + 0.5 * x * (1.0 - t * t) * _GELU_C * (1.0 + 3.0 * _GELU_K * x2)
    return g, dg


def _log_sigmoid(x):
    e = jnp.exp(-jnp.abs(x))
    u = 1.0 + e
    l1p = jnp.where(u == 1.0, e, jnp.log(u) * e / (u - 1.0))
    return jnp.minimum(x, 0.0) - l1p


def _neg_expm1(z):
    series = -z * (1.0 + z * (0.5 + z * (1.0 / 6.0 + z * (1.0 / 24.0 + z * (1.0 / 120.0 + z * (1.0 / 720.0))))))
    return jnp.where(z > -0.2, series, 1.0 - jnp.exp(z))


def _rows(shape):
    return lax.broadcasted_iota(jnp.int32, shape, 0)


def _shift_down(x, k):
    return jnp.where(_rows(x.shape) >= k, pltpu.roll(x, k, 0), 0.0)


def _shift_up(x, k):
    n = x.shape[0]
    return jnp.where(_rows(x.shape) < n - k, pltpu.roll(x, n - k, 0), 0.0)


def _causal_conv(x, w_ref, b=None):
    k_width = w_ref.shape[0]
    y = w_ref[k_width - 1:k_width, :] * x
    for j in range(1, k_width):
        y = y + w_ref[k_width - 1 - j:k_width - j, :] * _shift_down(x, j)
    if b is not None:
        y = y + b
    return y


def _causal_conv_t(dy, w_ref):
    k_width = w_ref.shape[0]
    dx = w_ref[k_width - 1:k_width, :] * dy
    for j in range(1, k_width):
        dx = dx + w_ref[k_width - 1 - j:k_width - j, :] * _shift_up(dy, j)
    return dx


def _conv_wgrad(dy, x, k_width):
    rows = [None] * k_width
    for j in range(k_width):
        xs = x if j == 0 else _shift_down(x, j)
        rows[k_width - 1 - j] = jnp.sum(dy * xs, axis=0, keepdims=True)
    return jnp.concatenate(rows, axis=0)


def _dot(a, b):
    return jnp.dot(a, b, preferred_element_type=F32)


def _dot_nt(a, b):
    return lax.dot_general(a, b, (((1,), (1,)), ((), ())), preferred_element_type=F32)


def _dot_tn(a, b):
    return lax.dot_general(a, b, (((0,), (0,)), ((), ())), preferred_element_type=F32)


def _rms_stats(x):
    r = lax.rsqrt(jnp.mean(x * x, axis=-1, keepdims=True) + RMS_EPS)
    return x * r, r


def _rms_bwd(n, r, g, dy):
    dn = dy * g
    dx = r * (dn - n * jnp.mean(dn * n, axis=-1, keepdims=True))
    return dx, dy * n


def _scan_forward(a_ref, b_ref, h_ref):
    n, c = a_ref.shape
    row = lax.broadcasted_iota(jnp.int32, (SUBLANES, c), 0)

    def group(g, carry):
        r0 = pl.multiple_of(g * SUBLANES, SUBLANES)
        a = a_ref[pl.ds(r0, SUBLANES), :]
        b = b_ref[pl.ds(r0, SUBLANES), :]
        for k in (1, 2, 4):
            ap = jnp.where(row >= k, pltpu.roll(a, k, 0), 1.0)
            bp = jnp.where(row >= k, pltpu.roll(b, k, 0), 0.0)
            b = a * bp + b
            a = a * ap
        h = a * carry + b
        h_ref[pl.ds(r0, SUBLANES), :] = h
        return h[SUBLANES - 1:SUBLANES, :]

    lax.fori_loop(0, n // SUBLANES, group, jnp.zeros((1, c), F32))


def _scan_backward(c_ref, b_ref, g_ref):
    n, ch = c_ref.shape
    row = lax.broadcasted_iota(jnp.int32, (SUBLANES, ch), 0)
    n_groups = n // SUBLANES

    def group(i, carry):
        r0 = pl.multiple_of((n_groups - 1 - i) * SUBLANES, SUBLANES)
        a = c_ref[pl.ds(r0, SUBLANES), :]
        b = b_ref[pl.ds(r0, SUBLANES), :]
        for k in (1, 2, 4):
            keep = row < SUBLANES - k
            ap = jnp.where(keep, pltpu.roll(a, SUBLANES - k, 0), 1.0)
            bp = jnp.where(keep, pltpu.roll(b, SUBLANES - k, 0), 0.0)
            b = a * bp + b
            a = a * ap
        g = a * carry + b
        g_ref[pl.ds(r0, SUBLANES), :] = g
        return g[0:1, :]

    lax.fori_loop(0, n_groups, group, jnp.zeros((1, ch), F32))


MESH = pl.DeviceIdType.MESH
_HBM = pl.BlockSpec(memory_space=pltpu.HBM)
_OTHER_CHIPS = ((1, 0), (0, 1), (1, 1))
_OTHER_DEVICES = tuple((dx, dy, dc) for dx in (0, 1) for dy in (0, 1) for dc in (0, 1) if dx or dy or dc)
N_DEVICES = 8


def _position():
    return lax.axis_index("x"), lax.axis_index("y"), lax.axis_index("c")


def _flip(v, d):
    return 1 - v if d else v


def _chip(x, y, p):
    px, py = _flip(x, _OTHER_CHIPS[p][0]), _flip(y, _OTHER_CHIPS[p][1])
    return px, py, 2 * px + py


class _Ride:
    def __init__(self, srcs, bufs, scratch, plan):
        self.srcs, self.bufs, self.scratch, self.plan = list(srcs), list(bufs), list(scratch), plan


def _call(body, *, name, grid, in_specs, out_specs, out_shape, operands, scratch_shapes=(), ride=None):
    in_specs, out_specs, out_shape = list(in_specs), list(out_specs), list(out_shape)
    scratch_shapes = list(scratch_shapes)
    if ride is None:
        return pl.pallas_call(body, name=name, grid=grid, in_specs=in_specs, out_specs=out_specs, out_shape=out_shape,
                              scratch_shapes=scratch_shapes, compiler_params=_params())(*operands)
    n_in, n_out, n_scr = len(in_specs), len(out_shape), len(scratch_shapes)
    old = [i for i, b in enumerate(ride.bufs) if not isinstance(b, jax.ShapeDtypeStruct)]
    n_src, n_old, n_buf = len(ride.srcs), len(old), len(ride.bufs)

    def full_body(*refs):
        o0 = n_in + n_src + n_old
        s0 = o0 + n_out + n_buf
        start, finish = ride.plan(refs[n_in:n_in + n_src], refs[o0 + n_out:s0], refs[s0 + n_scr:])
        ids = [pl.program_id(i) for i in range(len(grid))]
        first = functools.reduce(jnp.logical_and, [i == 0 for i in ids])
        last = functools.reduce(jnp.logical_and, [i == g - 1 for i, g in zip(ids, grid)])
        pl.when(first)(start)
        body(*refs[:n_in], *refs[o0:o0 + n_out], *refs[s0:s0 + n_scr])
        pl.when(last)(finish)

    shapes = [jax.ShapeDtypeStruct(b.shape, b.dtype) for b in ride.bufs]
    res = pl.pallas_call(
        full_body, name=name, grid=grid,
        in_specs=in_specs + [_HBM] * (n_src + n_old), out_specs=out_specs + [_HBM] * n_buf,
        out_shape=out_shape + shapes, scratch_shapes=scratch_shapes + ride.scratch,
        input_output_aliases={n_in + n_src + k: n_out + i for k, i in enumerate(old)},
        compiler_params=_params(),
    )(*operands, *ride.srcs, *[ride.bufs[i] for i in old])
    return res[:n_out], res[n_out:]


def run_ride(ride, name):
    def body():
        pass

    return _call(body, name=name, grid=(1,), in_specs=[], out_specs=[], out_shape=[], operands=[], ride=ride)[1]


def gather_ride(shards, items=None, into=None):
    items = items or [(a, 0, s.shape[0]) for a, s in enumerate(shards)]
    bufs = into or [jax.ShapeDtypeStruct((N_CHIPS,) + s.shape, s.dtype) for s in shards]
    nrel = len(_OTHER_CHIPS)

    def plan(srcs, dsts, sems):
        ici_send, ici_recv, sib_send, sib_recv = sems
        x, y, c = _position()
        j = 2 * x + y

        def rows(ref, it, h):
            return ref.at[pl.ds(it[1] + h * (it[2] // 2), it[2] // 2), :]

        def ici(i, p, slot):
            it = items[i]
            px, py, _ = _chip(x, y, p)
            return pltpu.make_async_remote_copy(
                src_ref=rows(srcs[it[0]], it, c), dst_ref=rows(dsts[it[0]].at[slot], it, c),
                send_sem=ici_send.at[i * nrel + p], recv_sem=ici_recv.at[i * nrel + p],
                device_id=(px, py, c), device_id_type=MESH)

        def sib(i, p, h):
            it = items[i]
            part = rows(dsts[it[0]].at[_chip(x, y, p)[2]], it, h)
            return pltpu.make_async_remote_copy(
                src_ref=part, dst_ref=part, send_sem=sib_send.at[i * nrel + p], recv_sem=sib_recv.at[i * nrel + p],
                device_id=(x, y, 1 - c), device_id_type=MESH)

        pairs = [(i, p) for i in range(len(items)) for p in range(nrel)]

        def start():
            for i, p in pairs:
                ici(i, p, j).start()

        def finish():
            for i, p in pairs:
                ici(i, p, _chip(x, y, p)[2]).wait_recv()
                sib(i, p, c).start()
            for i, p in pairs:
                sib(i, p, 1 - c).wait_recv()
            for i, p in pairs:
                ici(i, p, j).wait_send()
                sib(i, p, c).wait_send()

        return start, finish

    return _Ride(shards, bufs, [pltpu.SemaphoreType.DMA((len(items) * nrel,))] * 4, plan)


def exchange_ride(sums, items=None, into=None, rep=None):
    items = items or [(a, 0, s.shape[1]) for a, s in enumerate(sums)]
    bufs = list(into or [jax.ShapeDtypeStruct(s.shape, s.dtype) for s in sums])
    srcs = list(sums)
    scratch = [pltpu.SemaphoreType.DMA((len(items) * len(_OTHER_CHIPS),))] * 2
    if rep is not None:
        srcs.append(rep)
        bufs.append(jax.ShapeDtypeStruct((N_DEVICES,) + rep.shape, rep.dtype))
        scratch += [pltpu.SemaphoreType.DMA((len(_OTHER_DEVICES),))] * 2
    nrel = len(_OTHER_CHIPS)

    def plan(src_refs, dst_refs, sems):
        x, y, c = _position()
        j = 2 * x + y
        me = 4 * x + 2 * y + c

        def part(i, p, src_slot, dst_slot):
            a, r0, nr = items[i]
            px, py, _ = _chip(x, y, p)
            return pltpu.make_async_remote_copy(
                src_ref=src_refs[a].at[src_slot, pl.ds(r0, nr), :], dst_ref=dst_refs[a].at[dst_slot, pl.ds(r0, nr), :],
                send_sem=sems[0].at[i * nrel + p], recv_sem=sems[1].at[i * nrel + p],
                device_id=(px, py, c), device_id_type=MESH)

        def device(q):
            dx, dy, dc = _OTHER_DEVICES[q]
            return _flip(x, dx), _flip(y, dy), _flip(c, dc)

        def rep_copy(q, slot):
            return pltpu.make_async_remote_copy(
                src_ref=src_refs[-1], dst_ref=dst_refs[-1].at[slot], send_sem=sems[2].at[q], recv_sem=sems[3].at[q],
                device_id=device(q), device_id_type=MESH)

        pairs = [(i, p) for i in range(len(items)) for p in range(nrel)]
        others = range(len(_OTHER_DEVICES)) if rep is not None else ()

        def start():
            for i, p in pairs:
                part(i, p, _chip(x, y, p)[2], j).start()
            for q in others:
                rep_copy(q, me).start()

        def finish():
            for i, p in pairs:
                k = _chip(x, y, p)[2]
                part(i, p, k, k).wait_recv()
            for q in others:
                px, py, pc = device(q)
                rep_copy(q, 4 * px + 2 * py + pc).wait_recv()
            for i, p in pairs:
                part(i, p, _chip(x, y, p)[2], j).wait_send()
            for q in others:
                rep_copy(q, me).wait_send()

        return start, finish

    return _Ride(srcs, bufs, scratch, plan)


def _own_slot(buf, own, index):
    return lax.dynamic_update_slice(buf, own[None], (index,) + (0,) * own.ndim)


def _token_tile(s):
    return min(s, 512)


def norm_in(x, g):
    s, d = x.shape
    t = _token_tile(s)

    def body(x_ref, g_ref, o_ref):
        n, _ = _rms_stats(x_ref[...])
        o_ref[...] = (n * g_ref[...]).astype(BF16)

    return pl.pallas_call(
        body, name="norm_in", grid=(s // t,),
        in_specs=[pl.BlockSpec((t, d), lambda i: (i, 0)), pl.BlockSpec((1, d), lambda i: (0, 0))],
        out_specs=pl.BlockSpec((t, d), lambda i: (i, 0)),
        out_shape=jax.ShapeDtypeStruct((s, d), BF16),
        compiler_params=_params(),
    )(x, g)


def matmul_cols(a, w4, name, ride=None):
    m, k = a.shape
    nj, _, ns = w4.shape
    nb = ns // CW

    def body(a_ref, w_ref, o_ref):
        o_ref[...] = _dot(a_ref[...], w_ref[0])

    return _call(
        body, name=name, grid=(nj, nb),
        in_specs=[pl.BlockSpec((m, k), lambda j, b: (0, 0)),
                  pl.BlockSpec((1, k, CW), lambda j, b: (j, 0, b))],
        out_specs=[pl.BlockSpec((m, CW), lambda j, b: (0, j * nb + b))],
        out_shape=[jax.ShapeDtypeStruct((m, nj * ns), F32)],
        operands=(a, w4), ride=ride)


def mix_conv_fwd(proj, ws):
    s = proj.shape[0]
    nblk = D_MODEL // CW

    def body(cb_ref, cc_ref, cx_ref, ws_ref, q_ref, ya_ref):
        q = _causal_conv(cc_ref[...] * cx_ref[...], ws_ref)
        q_ref[...] = q
        ya_ref[...] = (cb_ref[...] * q).astype(BF16)

    seg = lambda k: pl.BlockSpec((s, CW), lambda c, k=k: (0, k * nblk + c))
    return pl.pallas_call(
        body, name="mix_conv_fwd", grid=(nblk,),
        in_specs=[seg(0), seg(1), seg(2), pl.BlockSpec((3, CW), lambda c: (0, c))],
        out_specs=[pl.BlockSpec((s, CW), lambda c: (0, c))] * 2,
        out_shape=[jax.ShapeDtypeStruct((s, D_MODEL), F32), jax.ShapeDtypeStruct((s, D_MODEL), BF16)],
        compiler_params=_params(),
    )(proj, proj, proj, ws)


def _lru_gates(r, ls):
    log_a = LRU_C * r * ls
    a = jnp.exp(log_a)
    mult = jnp.sqrt(_neg_expm1(2.0 * log_a))
    mult = jnp.where(_rows(r.shape) == 0, 1.0, mult)
    return a, mult


def mix_lru_fwd(proj, wl, bl, wa, ba, wx, bx, lam, ride=None):
    s = proj.shape[0]
    nblk = D_MODEL // CW

    def body(lx_ref, ly_ref, wl_ref, bl_ref, wa_ref, ba_ref, wx_ref, bx_ref, lam_ref,
             xl_ref, r_ref, i_ref, h_ref, yb_ref, a_scr, u_scr):
        xl = _causal_conv(lx_ref[...], wl_ref, bl_ref[...])
        xl_ref[...] = xl
        xlb = xl.astype(BF16)
        r = _sigmoid(_dot(xlb, wa_ref[0]) + ba_ref[...])
        i = _sigmoid(_dot(xlb, wx_ref[0]) + bx_ref[...])
        r_ref[...] = r
        i_ref[...] = i
        a, mult = _lru_gates(r, _log_sigmoid(lam_ref[...]))
        a_scr[...] = a
        u_scr[...] = mult * i * xl
        _scan_forward(a_scr, u_scr, h_ref)
        yb_ref[...] = (h_ref[...] * _gelu(ly_ref[...])).astype(BF16)

    blk = lambda k: pl.BlockSpec((s, CW), lambda c, k=k: (0, k * nblk + c))
    vec = pl.BlockSpec((1, CW), lambda c: (0, c))
    mat = pl.BlockSpec((1, CW, CW), lambda c: (c, 0, 0))
    out = pl.BlockSpec((s, CW), lambda c: (0, c))
    f = jax.ShapeDtypeStruct((s, D_MODEL), F32)
    return _call(
        body, name="mix_lru_fwd", grid=(nblk,),
        in_specs=[blk(3), blk(4), pl.BlockSpec((4, CW), lambda c: (0, c)), vec, mat, vec, mat, vec, vec],
        out_specs=[out] * 5,
        out_shape=[f, f, f, f, jax.ShapeDtypeStruct((s, D_MODEL), BF16)],
        scratch_shapes=[pltpu.VMEM((s, CW), F32), pltpu.VMEM((s, CW), F32)],
        operands=(proj, proj, wl, bl, wa, ba, wx, bx, lam), ride=ride)


def branch_merge_fwd(ya, yb, wcb, wlb, proj, ride=None):
    s = ya.shape[0]
    nblk = D_MODEL // CW

    def body(ya_ref, yb_ref, wcb_ref, wlb_ref, gc_ref, gl_ref, a_ref, b_ref, m_ref):
        a = _dot(ya_ref[...], wcb_ref[...])
        b = _dot(yb_ref[...], wlb_ref[...])
        a_ref[...] = a
        b_ref[...] = b
        m_ref[...] = (_sigmoid(gc_ref[...]) * a + _sigmoid(gl_ref[...]) * b).astype(BF16)

    res = pl.BlockSpec((s, D_MODEL), lambda n: (0, 0))
    wcol = pl.BlockSpec((D_MODEL, CW), lambda n: (0, n))
    blk = lambda k: pl.BlockSpec((s, CW), lambda n, k=k: (0, k * nblk + n))
    out = pl.BlockSpec((s, CW), lambda n: (0, n))
    f = jax.ShapeDtypeStruct((s, D_MODEL), F32)
    return _call(
        body, name="branch_merge_fwd", grid=(nblk,),
        in_specs=[res, res, wcol, wcol, blk(5), blk(6)],
        out_specs=[out] * 3,
        out_shape=[f, f, jax.ShapeDtypeStruct((s, D_MODEL), BF16)],
        operands=(ya, yb, wcb, wlb, proj, proj), ride=ride)


def mix_out_fwd(merged, wout, x, g2, g3):
    s, d = x.shape
    t = _token_tile(s)

    def body(m_ref, w_ref, x_ref, g2_ref, g3_ref, mix_ref, x2_ref, h2_ref):
        mix = _dot(m_ref[...], w_ref[...])
        mix_ref[...] = mix
        n, _ = _rms_stats(mix)
        x2 = x_ref[...] + n * g2_ref[...]
        x2_ref[...] = x2
        n2, _ = _rms_stats(x2)
        h2_ref[...] = (n2 * g3_ref[...]).astype(BF16)

    tile = pl.BlockSpec((t, d), lambda i: (i, 0))
    vec = pl.BlockSpec((1, d), lambda i: (0, 0))
    f = jax.ShapeDtypeStruct((s, d), F32)
    return pl.pallas_call(
        body, name="mix_out_fwd", grid=(s // t,),
        in_specs=[tile, pl.BlockSpec((d, d), lambda i: (0, 0)), tile, vec, vec],
        out_specs=[tile] * 3,
        out_shape=[f, f, jax.ShapeDtypeStruct((s, d), BF16)],
        compiler_params=_params(),
    )(merged, wout, x, g2, g3)


def ffn_act_fwd(up, fw, fb):
    s = up.shape[0]
    nblk = D_FF // FW

    def body(ug_ref, uv_ref, wg_ref, wv_ref, bg_ref, bv_ref, f_ref):
        gate = _causal_conv(ug_ref[...], wg_ref, bg_ref[...])
        val = _causal_conv(uv_ref[...], wv_ref, bv_ref[...])
        f_ref[...] = (_gelu(gate) * val).astype(BF16)

    half = lambda h, rows: pl.BlockSpec((rows, FW), lambda n, h=h: (0, h * nblk + n))
    return pl.pallas_call(
        body, name="ffn_act_fwd", grid=(nblk,),
        in_specs=[half(0, s), half(1, s), half(0, 3), half(1, 3), half(0, 1), half(1, 1)],
        out_specs=pl.BlockSpec((s, FW), lambda n: (0, n)),
        out_shape=jax.ShapeDtypeStruct((s, D_FF), BF16),
        compiler_params=_params(),
    )(up, up, fw, fw, fb, fb)


def ffn_down_loss(f, wdown, x2, target, g4):
    s, d = x2.shape
    t = _token_tile(s)

    def body(f_ref, w_ref, x2_ref, tg_ref, g4_ref, dy_ref, dout_ref, loss_ref, dg4_ref):
        @pl.when(pl.program_id(0) == 0)
        def _():
            loss_ref[...] = jnp.zeros_like(loss_ref)
            dg4_ref[...] = jnp.zeros_like(dg4_ref)

        out = _dot(f_ref[...], w_ref[...])
        n, r = _rms_stats(out)
        err = x2_ref[...] + n * g4_ref[...] - tg_ref[...]
        loss_ref[...] += jnp.full(loss_ref.shape, (0.5 / d) * jnp.sum(err * err), F32)
        dy = err * (1.0 / d)
        dy_ref[...] = dy
        dout, dg = _rms_bwd(n, r, g4_ref[...], dy)
        dout_ref[...] = dout.astype(BF16)
        dg4_ref[...] += jnp.sum(dg, axis=0, keepdims=True)

    tile = pl.BlockSpec((t, d), lambda i: (i, 0))
    vec = pl.BlockSpec((1, d), lambda i: (0, 0))
    return pl.pallas_call(
        body, name="ffn_down_loss", grid=(s // t,),
        in_specs=[pl.BlockSpec((t, D_FF), lambda i: (i, 0)), pl.BlockSpec((D_FF, d), lambda i: (0, 0)), tile, tile, vec],
        out_specs=[tile, tile, pl.BlockSpec((1, 128), lambda i: (0, 0)), vec],
        out_shape=[jax.ShapeDtypeStruct((s, d), F32), jax.ShapeDtypeStruct((s, d), BF16),
                   jax.ShapeDtypeStruct((1, 128), F32), jax.ShapeDtypeStruct((1, d), F32)],
        compiler_params=_params(),
    )(f, wdown, x2, target, g4)


def ffn_bwd(dout, wdown, up, fw, fb):
    s = up.shape[0]
    nblk = D_FF // FW

    def body(do_ref, wd_ref, ug_ref, uv_ref, wg_ref, wv_ref, bg_ref, bv_ref,
             dug_ref, duv_ref, dwd_ref, dwg_ref, dwv_ref, dbg_ref, dbv_ref):
        do = do_ref[...]
        df = _dot_nt(do, wd_ref[...])
        ug = ug_ref[...]
        uv = uv_ref[...]
        gate = _causal_conv(ug, wg_ref, bg_ref[...])
        val = _causal_conv(uv, wv_ref, bv_ref[...])
        ge, dge = _gelu_and_grad(gate)
        dwd_ref[...] = _dot_tn((ge * val).astype(BF16), do).astype(BF16)
        dgate = df * val * dge
        dval = df * ge
        dug_ref[...] = _causal_conv_t(dgate, wg_ref).astype(BF16)
        duv_ref[...] = _causal_conv_t(dval, wv_ref).astype(BF16)
        dwg_ref[...] = _conv_wgrad(dgate, ug, 3)
        dwv_ref[...] = _conv_wgrad(dval, uv, 3)
        dbg_ref[...] = jnp.sum(dgate, axis=0, keepdims=True)
        dbv_ref[...] = jnp.sum(dval, axis=0, keepdims=True)

    half = lambda h, rows: pl.BlockSpec((rows, FW), lambda n, h=h: (0, h * nblk + n))
    own = lambda rows: pl.BlockSpec((rows, FW), lambda n: (0, n))
    act = jax.ShapeDtypeStruct((s, D_FF), BF16)
    taps = jax.ShapeDtypeStruct((3, D_FF), F32)
    bias = jax.ShapeDtypeStruct((1, D_FF), F32)
    return pl.pallas_call(
        body, name="ffn_bwd", grid=(nblk,),
        in_specs=[pl.BlockSpec((s, D_MODEL), lambda n: (0, 0)), pl.BlockSpec((FW, D_MODEL), lambda n: (n, 0)),
                  half(0, s), half(1, s), half(0, 3), half(1, 3), half(0, 1), half(1, 1)],
        out_specs=[own(s), own(s), pl.BlockSpec((FW, D_MODEL), lambda n: (n, 0)), own(3), own(3), own(1), own(1)],
        out_shape=[act, act, jax.ShapeDtypeStruct((D_FF, D_MODEL), BF16), taps, taps, bias, bias],
        compiler_params=_params(),
    )(dout, wdown, up, up, fw, fw, fb, fb)


def dgrad_wgrad_cols(dy, w4, a, name, ride=None):
    m, k = a.shape
    nj, _, ns = w4.shape
    nb = ns // CW

    def body(dy_ref, w_ref, a_ref, da_ref, dw_ref):
        @pl.when((pl.program_id(0) == 0) & (pl.program_id(1) == 0))
        def _():
            da_ref[...] = jnp.zeros_like(da_ref)

        dyb = dy_ref[...]
        da_ref[...] += _dot_nt(dyb, w_ref[0])
        dw_ref[...] = _dot_tn(a_ref[...], dyb).astype(BF16)

    return _call(
        body, name=name, grid=(nj, nb),
        in_specs=[pl.BlockSpec((m, CW), lambda j, b: (0, j * nb + b)),
                  pl.BlockSpec((1, k, CW), lambda j, b: (j, 0, b)),
                  pl.BlockSpec((m, k), lambda j, b: (0, 0))],
        out_specs=[pl.BlockSpec((m, k), lambda j, b: (0, 0)),
                   pl.BlockSpec((k, CW), lambda j, b: (0, j * nb + b))],
        out_shape=[jax.ShapeDtypeStruct((m, k), F32), jax.ShapeDtypeStruct((k, nj * ns), BF16)],
        operands=(dy, w4, a), ride=ride)


def norms_mid_bwd(dh2, x2, dy, mix, g3, g2):
    s, d = x2.shape
    t = _token_tile(s)

    def body(dh2_ref, x2_ref, dy_ref, mix_ref, g3_ref, g2_ref, dx2_ref, dmix_ref, dg3_ref, dg2_ref):
        @pl.when(pl.program_id(0) == 0)
        def _():
            dg3_ref[...] = jnp.zeros_like(dg3_ref)
            dg2_ref[...] = jnp.zeros_like(dg2_ref)

        n3, r3 = _rms_stats(x2_ref[...])
        dx, dg3 = _rms_bwd(n3, r3, g3_ref[...], dh2_ref[...])
        dx2 = dy_ref[...] + dx
        dx2_ref[...] = dx2
        dg3_ref[...] += jnp.sum(dg3, axis=0, keepdims=True)
        n2, r2 = _rms_stats(mix_ref[...])
        dmix, dg2 = _rms_bwd(n2, r2, g2_ref[...], dx2)
        dmix_ref[...] = dmix.astype(BF16)
        dg2_ref[...] += jnp.sum(dg2, axis=0, keepdims=True)

    tile = pl.BlockSpec((t, d), lambda i: (i, 0))
    vec = pl.BlockSpec((1, d), lambda i: (0, 0))
    v = jax.ShapeDtypeStruct((1, d), F32)
    return pl.pallas_call(
        body, name="norms_mid_bwd", grid=(s // t,),
        in_specs=[tile, tile, tile, tile, vec, vec],
        out_specs=[tile, tile, vec, vec],
        out_shape=[jax.ShapeDtypeStruct((s, d), F32), jax.ShapeDtypeStruct((s, d), BF16), v, v],
        compiler_params=_params(),
    )(dh2, x2, dy, mix, g3, g2)


def mix_out_bwd(dmix, wout, merged, a, b, proj, ride=None):
    s = dmix.shape[0]
    nblk = D_MODEL // CW

    def body(dm_ref, w_ref, mg_ref, a_ref, b_ref, gc_ref, gl_ref, da_ref, db_ref, dw_ref, dgc_ref, dgl_ref):
        dm = dm_ref[...]
        dmerged = _dot_nt(dm, w_ref[...])
        dw_ref[...] = _dot_tn(mg_ref[...], dm).astype(BF16)
        sc = _sigmoid(gc_ref[...])
        sl = _sigmoid(gl_ref[...])
        da_ref[...] = (dmerged * sc).astype(BF16)
        db_ref[...] = (dmerged * sl).astype(BF16)
        dgc_ref[...] = (dmerged * a_ref[...] * sc * (1.0 - sc)).astype(BF16)
        dgl_ref[...] = (dmerged * b_ref[...] * sl * (1.0 - sl)).astype(BF16)

    res = pl.BlockSpec((s, D_MODEL), lambda n: (0, 0))
    rows = pl.BlockSpec((CW, D_MODEL), lambda n: (n, 0))
    col = pl.BlockSpec((s, CW), lambda n: (0, n))
    blk = lambda k: pl.BlockSpec((s, CW), lambda n, k=k: (0, k * nblk + n))
    hb = jax.ShapeDtypeStruct((s, D_MODEL), BF16)
    return _call(
        body, name="mix_out_bwd", grid=(nblk,),
        in_specs=[res, rows, col, col, col, blk(5), blk(6)],
        out_specs=[col, col, rows, col, col],
        out_shape=[hb, hb, jax.ShapeDtypeStruct((D_MODEL, D_MODEL), BF16), hb, hb],
        operands=(dmix, wout, merged, a, b, proj, proj), ride=ride)


def mix_conv_bwd(da, wcb, proj, q, ws, ride=None):
    s = da.shape[0]
    nblk = D_MODEL // CW

    def body(da_ref, w_ref, cb_ref, cc_ref, cx_ref, q_ref, ws_ref, dcb_ref, dcc_ref, dcx_ref, dw_ref, dws_ref):
        dab = da_ref[...]
        dya = _dot_nt(dab, w_ref[...])
        cb = cb_ref[...]
        cc = cc_ref[...]
        cx = cx_ref[...]
        q = q_ref[...]
        dw_ref[...] = _dot_tn((cb * q).astype(BF16), dab).astype(BF16)
        dcb_ref[...] = (dya * q).astype(BF16)
        dq = dya * cb
        dp = _causal_conv_t(dq, ws_ref)
        dws_ref[...] = _conv_wgrad(dq, cc * cx, 3)
        dcc_ref[...] = (dp * cx).astype(BF16)
        dcx_ref[...] = (dp * cc).astype(BF16)

    res = pl.BlockSpec((s, D_MODEL), lambda n: (0, 0))
    rows = pl.BlockSpec((CW, D_MODEL), lambda n: (n, 0))
    col = pl.BlockSpec((s, CW), lambda n: (0, n))
    blk = lambda k: pl.BlockSpec((s, CW), lambda n, k=k: (0, k * nblk + n))
    taps = pl.BlockSpec((3, CW), lambda n: (0, n))
    hb = jax.ShapeDtypeStruct((s, D_MODEL), BF16)
    return _call(
        body, name="mix_conv_bwd", grid=(nblk,),
        in_specs=[res, rows, blk(0), blk(1), blk(2), col, taps],
        out_specs=[col, col, col, rows, taps],
        out_shape=[hb, hb, hb, jax.ShapeDtypeStruct((D_MODEL, D_MODEL), BF16), jax.ShapeDtypeStruct((3, D_MODEL), F32)],
        operands=(da, wcb, proj, proj, proj, q, ws), ride=ride)


def mix_lru_bwd(db, wlb, proj, xl, r, i, h, wl, wa, wx, lam, ride=None):
    s = db.shape[0]
    nblk = D_MODEL // CW

    def body(db_ref, w_ref, lx_ref, ly_ref, xl_ref, r_ref, i_ref, h_ref, wl_ref, wa_ref, wx_ref, lam_ref,
             dlx_ref, dly_ref, dw_ref, dwa_ref, dwx_ref, dba_ref, dbx_ref, dwl_ref, dbl_ref, dlam_ref,
             c_scr, dh_scr, g_scr):
        dbb = db_ref[...]
        dyb = _dot_nt(dbb, w_ref[...])
        h = h_ref[...]
        ge, dge = _gelu_and_grad(ly_ref[...])
        dw_ref[...] = _dot_tn((h * ge).astype(BF16), dbb).astype(BF16)
        dly_ref[...] = (dyb * h * dge).astype(BF16)
        r = r_ref[...]
        gi = i_ref[...]
        xl = xl_ref[...]
        lam = lam_ref[...]
        ls = _log_sigmoid(lam)
        a, mult = _lru_gates(r, ls)
        c_scr[...] = _shift_up(a, 1)
        dh_scr[...] = dyb * ge
        _scan_backward(c_scr, dh_scr, g_scr)
        du = g_scr[...]
        da = du * _shift_down(h, 1)
        dmult = du * gi * xl
        di = du * mult * xl
        dxl = du * mult * gi
        first = _rows(a.shape) == 0
        dlog_a = da * a - jnp.where(first, 0.0, dmult * a * a / mult)
        dr = dlog_a * (LRU_C * ls)
        dlam_ref[...] = jnp.sum(dlog_a * r, axis=0, keepdims=True) * (LRU_C * (1.0 - _sigmoid(lam)))
        dzr = dr * r * (1.0 - r)
        dzi = di * gi * (1.0 - gi)
        dba_ref[...] = jnp.sum(dzr, axis=0, keepdims=True)
        dbx_ref[...] = jnp.sum(dzi, axis=0, keepdims=True)
        xlb = xl.astype(BF16)
        dzrb = dzr.astype(BF16)
        dzib = dzi.astype(BF16)
        dwa_ref[0] = _dot_tn(xlb, dzrb)
        dwx_ref[0] = _dot_tn(xlb, dzib)
        dxl = dxl + _dot_nt(dzrb, wa_ref[0]) + _dot_nt(dzib, wx_ref[0])
        dlx_ref[...] = _causal_conv_t(dxl, wl_ref).astype(BF16)
        dwl_ref[...] = _conv_wgrad(dxl, lx_ref[...], 4)
        dbl_ref[...] = jnp.sum(dxl, axis=0, keepdims=True)

    res = pl.BlockSpec((s, D_MODEL), lambda n: (0, 0))
    rows = pl.BlockSpec((CW, D_MODEL), lambda n: (n, 0))
    col = pl.BlockSpec((s, CW), lambda n: (0, n))
    blk = lambda k: pl.BlockSpec((s, CW), lambda n, k=k: (0, k * nblk + n))
    taps = pl.BlockSpec((4, CW), lambda n: (0, n))
    vec = pl.BlockSpec((1, CW), lambda n: (0, n))
    mat = pl.BlockSpec((1, CW, CW), lambda n: (n, 0, 0))
    hb = jax.ShapeDtypeStruct((s, D_MODEL), BF16)
    v = jax.ShapeDtypeStruct((1, D_MODEL), F32)
    m = jax.ShapeDtypeStruct((LRU_HEADS, HEAD_DIM, HEAD_DIM), F32)
    scr = pltpu.VMEM((s, CW), F32)
    return _call(
        body, name="mix_lru_bwd", grid=(nblk,),
        in_specs=[res, rows, blk(3), blk(4), col, col, col, col, taps, mat, mat, vec],
        out_specs=[col, col, rows, mat, mat, vec, vec, taps, vec, vec],
        out_shape=[hb, hb, jax.ShapeDtypeStruct((D_MODEL, D_MODEL), BF16), m, m, v, v,
                   jax.ShapeDtypeStruct((4, D_MODEL), F32), v, v],
        scratch_shapes=[scr, scr, scr],
        operands=(db, wlb, proj, proj, xl, r, i, h, wl, wa, wx, lam), ride=ride)


def norm_in_bwd(dh1, x, dx2, g1):
    s, d = x.shape
    t = _token_tile(s)

    def body(dh_ref, x_ref, dx2_ref, g_ref, dx_ref, dg_ref):
        @pl.when(pl.program_id(0) == 0)
        def _():
            dg_ref[...] = jnp.zeros_like(dg_ref)

        n, r = _rms_stats(x_ref[...])
        dx, dg = _rms_bwd(n, r, g_ref[...], dh_ref[...])
        dx_ref[...] = dx2_ref[...] + dx
        dg_ref[...] += jnp.sum(dg, axis=0, keepdims=True)

    tile = pl.BlockSpec((t, d), lambda i: (i, 0))
    vec = pl.BlockSpec((1, d), lambda i: (0, 0))
    return pl.pallas_call(
        body, name="norm_in_bwd", grid=(s // t,),
        in_specs=[tile, tile, tile, vec],
        out_specs=[tile, vec],
        out_shape=[jax.ShapeDtypeStruct((s, d), F32), jax.ShapeDtypeStruct((1, d), F32)],
        compiler_params=_params(),
    )(dh1, x, dx2, g1)


def local_step(x, target, g1, g2, g3, g4, win4, ws, wcb, wl, bl, wa, ba, wx, bx, lam, wlb, wout, wup4, fw, fb, wdown):
    h1 = norm_in(x, g1)
    proj = matmul_cols(h1, win4, "proj_fwd")
    q, ya = mix_conv_fwd(proj, ws)
    xl, r, gi, h, yb = mix_lru_fwd(proj, wl, bl, wa, ba, wx, bx, lam)
    a, b, merged = branch_merge_fwd(ya, yb, wcb, wlb, proj)
    mix, x2, h2 = mix_out_fwd(merged, wout, x, g2, g3)
    up = matmul_cols(h2, wup4, "up_fwd")
    f = ffn_act_fwd(up, fw, fb)
    dy, dout, loss, dg4 = ffn_down_loss(f, wdown, x2, target, g4)

    dug, duv, dwdown, dfw_g, dfw_v, dfb_g, dfb_v = ffn_bwd(dout, wdown, up, fw, fb)
    dup = jnp.concatenate([dug, duv], axis=1)
    dfw = jnp.concatenate([dfw_g, dfw_v], axis=1)
    dfb = jnp.concatenate([dfb_g, dfb_v], axis=1)
    dh2, dwup = dgrad_wgrad_cols(dup, wup4, h2, "up_bwd")
    dx2, dmix, dg3, dg2 = norms_mid_bwd(dh2, x2, dy, mix, g3, g2)
    da, db, dwout, dgc, dgl = mix_out_bwd(dmix, wout, merged, a, b, proj)
    dcb, dcc, dcx, dwcb, dws = mix_conv_bwd(da, wcb, proj, q, ws)
    dlx, dly, dwlb, dwa, dwx, dba, dbx, dwl, dbl, dlam = mix_lru_bwd(db, wlb, proj, xl, r, gi, h, wl, wa, wx, lam)
    dproj = jnp.concatenate([dcb, dcc, dcx, dlx, dly, dgc[:, 5 * D_MODEL:6 * D_MODEL], dgl[:, 6 * D_MODEL:]], axis=1)
    dh1, dwin = dgrad_wgrad_cols(dproj, win4, h1, "proj_bwd")
    dx, dg1 = norm_in_bwd(dh1, x, dx2, g1)
    grads = dict(norm_mix_pre=dg1, norm_mix_post=dg2, norm_ffn_pre=dg3, norm_ffn_post=dg4,
                 w_in=dwin, conv_short_w=dws, w_conv_branch=dwcb, lru_conv_w=dwl, lru_conv_b=dbl,
                 lru_wa=dwa, lru_ba=dba, lru_wx=dwx, lru_bx=dbx, lru_lambda=dlam,
                 w_lru_branch=dwlb, w_out=dwout, ffn_w_up=dwup, ffn_conv_w=dfw, ffn_conv_b=dfb,
                 ffn_w_down=dwdown)
    return loss[0, 0], dx, grads


MESH = pl.DeviceIdType.MESH
_HBM = pl.BlockSpec(memory_space=pltpu.HBM)
_OTHER_CHIPS = ((1, 0), (0, 1), (1, 1))
_OTHER_DEVICES = tuple((dx, dy, dc) for dx in (0, 1) for dy in (0, 1) for dc in (0, 1) if dx or dy or dc)
N_DEVICES = 8


def _position():
    return lax.axis_index("x"), lax.axis_index("y"), lax.axis_index("c")


def _flip(v, d):
    return 1 - v if d else v


def _half_rows(ref, h, hr):
    return ref.at[pl.ds(h * hr, hr), :]


def gather_chips(shards):
    n = len(shards)
    nrel = len(_OTHER_CHIPS)

    def body(*refs):
        ins, outs = refs[:n], refs[n:2 * n]
        ici_send, ici_recv, sib_send, sib_recv = refs[2 * n:]
        x, y, c = _position()
        j = 2 * x + y
        hr = [s.shape[0] // 2 for s in shards]

        def chip(p):
            px, py = _flip(x, _OTHER_CHIPS[p][0]), _flip(y, _OTHER_CHIPS[p][1])
            return px, py, 2 * px + py

        def ici(a, p, slot):
            px, py, _ = chip(p)
            return pltpu.make_async_remote_copy(
                src_ref=_half_rows(ins[a], c, hr[a]), dst_ref=_half_rows(outs[a].at[slot], c, hr[a]),
                send_sem=ici_send.at[a * nrel + p], recv_sem=ici_recv.at[a * nrel + p],
                device_id=(px, py, c), device_id_type=MESH)

        def sib(a, p, h):
            _, _, k = chip(p)
            part = _half_rows(outs[a].at[k], h, hr[a])
            return pltpu.make_async_remote_copy(
                src_ref=part, dst_ref=part, send_sem=sib_send.at[a * nrel + p], recv_sem=sib_recv.at[a * nrel + p],
                device_id=(x, y, 1 - c), device_id_type=MESH)

        pairs = [(a, p) for a in range(n) for p in range(nrel)]
        for a, p in pairs:
            ici(a, p, j).start()
        for a, p in pairs:
            ici(a, p, chip(p)[2]).wait_recv()
            sib(a, p, c).start()
        for a, p in pairs:
            sib(a, p, 1 - c).wait_recv()
        for a, p in pairs:
            ici(a, p, j).wait_send()
            sib(a, p, c).wait_send()

    got = pl.pallas_call(
        body, name="gather_chips",
        in_specs=[_HBM] * n, out_specs=[_HBM] * n,
        out_shape=[jax.ShapeDtypeStruct((N_CHIPS,) + s.shape, s.dtype) for s in shards],
        scratch_shapes=[pltpu.SemaphoreType.DMA((n * nrel,))] * 4,
    )(*shards)
    j = 2 * lax.axis_index("x") + lax.axis_index("y")
    return [lax.dynamic_update_slice(g, s[None], (j, 0, 0)) for g, s in zip(got, shards)]


def _owned_part(ref, kind, k, h, hr):
    if kind == "col":
        ns = ref.shape[1] // N_CHIPS
        return ref.at[pl.ds(h * hr, hr), pl.ds(k * ns, ns)]
    if kind == "row":
        return ref.at[pl.ds(k * 2 * hr + h * hr, hr), :]
    return ref.at[k, pl.ds(h * hr, hr), :]


def _part_shape(g, kind):
    if kind == "col":
        return g.shape[0] // 2, g.shape[1] // N_CHIPS
    if kind == "row":
        return g.shape[0] // (2 * N_CHIPS), g.shape[1]
    return g.shape[1] // 2, g.shape[2]


def pair_split(grads, kinds, name):
    n = len(grads)
    shapes = [_part_shape(g, k) for g, k in zip(grads, kinds)]

    def body(*refs):
        ins, theirs = refs[:n], refs[n:2 * n]
        send_sem, recv_sem = refs[2 * n:]
        x, y, c = _position()
        copies = []
        for a in range(n):
            hr = shapes[a][0]
            for k in range(N_CHIPS):
                s = a * N_CHIPS + k
                copies.append(pltpu.make_async_remote_copy(
                    src_ref=_owned_part(ins[a], kinds[a], k, 1 - c, hr), dst_ref=theirs[a].at[k],
                    send_sem=send_sem.at[s], recv_sem=recv_sem.at[s], device_id=(x, y, 1 - c), device_id_type=MESH))
        for cp in copies:
            cp.start()
        for cp in copies:
            cp.wait()

    return pl.pallas_call(
        body, name=name,
        in_specs=[_HBM] * n, out_specs=[_HBM] * n,
        out_shape=[jax.ShapeDtypeStruct((N_CHIPS,) + shp, g.dtype) for shp, g in zip(shapes, grads)],
        scratch_shapes=[pltpu.SemaphoreType.DMA((n * N_CHIPS,))] * 2,
    )(*grads)


def chip_exchange(sums, rep):
    n = len(sums)
    nrel = len(_OTHER_CHIPS)
    ndev = len(_OTHER_DEVICES)

    def body(*refs):
        ins, rep_ref = refs[:n], refs[n]
        outs, rep_out = refs[n + 1:2 * n + 1], refs[2 * n + 1]
        loc_sem, send_sem, recv_sem, rep_send, rep_recv = refs[2 * n + 2:]
        x, y, c = _position()
        j = 2 * x + y
        me = 4 * x + 2 * y + c

        def chip(p):
            px, py = _flip(x, _OTHER_CHIPS[p][0]), _flip(y, _OTHER_CHIPS[p][1])
            return px, py, 2 * px + py

        def part(a, p, src_slot, dst_slot):
            px, py, _ = chip(p)
            return pltpu.make_async_remote_copy(
                src_ref=ins[a].at[src_slot], dst_ref=outs[a].at[dst_slot],
                send_sem=send_sem.at[a * nrel + p], recv_sem=recv_sem.at[a * nrel + p],
                device_id=(px, py, c), device_id_type=MESH)

        def device(q):
            dx, dy, dc = _OTHER_DEVICES[q]
            return _flip(x, dx), _flip(y, dy), _flip(c, dc)

        def rep_copy(q, slot):
            return pltpu.make_async_remote_copy(
                src_ref=rep_ref, dst_ref=rep_out.at[slot], send_sem=rep_send.at[q], recv_sem=rep_recv.at[q],
                device_id=device(q), device_id_type=MESH)

        own = [pltpu.make_async_copy(ins[a].at[j], outs[a].at[j], loc_sem.at[a]) for a in range(n)]
        own.append(pltpu.make_async_copy(rep_ref, rep_out.at[me], loc_sem.at[n]))
        for cp in own:
            cp.start()
        pairs = [(a, p) for a in range(n) for p in range(nrel)]
        for a, p in pairs:
            part(a, p, chip(p)[2], j).start()
        for q in range(ndev):
            rep_copy(q, me).start()
        for a, p in pairs:
            part(a, p, chip(p)[2], chip(p)[2]).wait_recv()
        for q in range(ndev):
            px, py, pc = device(q)
            rep_copy(q, 4 * px + 2 * py + pc).wait_recv()
        for a, p in pairs:
            part(a, p, chip(p)[2], j).wait_send()
        for q in range(ndev):
            rep_copy(q, me).wait_send()
        for cp in own:
            cp.wait()

    return pl.pallas_call(
        body, name="chip_exchange",
        in_specs=[_HBM] * (n + 1), out_specs=[_HBM] * (n + 1),
        out_shape=[jax.ShapeDtypeStruct(s.shape, s.dtype) for s in sums]
        + [jax.ShapeDtypeStruct((N_DEVICES,) + rep.shape, rep.dtype)],
        scratch_shapes=[pltpu.SemaphoreType.DMA((n + 1,)), pltpu.SemaphoreType.DMA((n * nrel,)),
                        pltpu.SemaphoreType.DMA((n * nrel,)), pltpu.SemaphoreType.DMA((ndev,)),
                        pltpu.SemaphoreType.DMA((ndev,))],
    )(*sums, rep)


def pair_swap(halves):
    n = len(halves)

    def body(*refs):
        ins, outs = refs[:n], refs[n:2 * n]
        send_sem, recv_sem = refs[2 * n:]
        x, y, c = _position()
        copies = [pltpu.make_async_remote_copy(
            src_ref=ins[a], dst_ref=outs[a], send_sem=send_sem.at[a], recv_sem=recv_sem.at[a],
            device_id=(x, y, 1 - c), device_id_type=MESH) for a in range(n)]
        for cp in copies:
            cp.start()
        for cp in copies:
            cp.wait()

    return pl.pallas_call(
        body, name="pair_swap",
        in_specs=[_HBM] * n, out_specs=[_HBM] * n,
        out_shape=[jax.ShapeDtypeStruct(h.shape, h.dtype) for h in halves],
        scratch_shapes=[pltpu.SemaphoreType.DMA((n,))] * 2,
    )(*halves)


def _row_tile(rows, cols, limit_bytes=1 << 20):
    best = None
    for t in range(SUBLANES, rows + 1, SUBLANES):
        if rows % t == 0 and t * cols * 4 <= limit_bytes:
            best = t
    return best or rows


def add_pair(g, kind, theirs, core, name):
    nc, rows, cols = theirs.shape
    t = _row_tile(rows, cols)
    nt = rows // t

    def body(core_ref, g_ref, b_ref, o_ref):
        mine = g_ref[...].reshape(t, cols)
        o_ref[0] = (mine.astype(F32) + b_ref[0].astype(F32)).astype(o_ref.dtype)

    if kind == "col":
        own = pl.BlockSpec((t, cols), lambda k, i, c: (c[0] * nt + i, k))
    elif kind == "row":
        own = pl.BlockSpec((t, cols), lambda k, i, c: ((2 * k + c[0]) * nt + i, 0))
    else:
        own = pl.BlockSpec((1, t, cols), lambda k, i, c: (k, c[0] * nt + i, 0))
    spec = pl.BlockSpec((1, t, cols), lambda k, i, c: (k, i, 0))
    return pl.pallas_call(
        body, name=name,
        grid_spec=pltpu.PrefetchScalarGridSpec(num_scalar_prefetch=1, grid=(nc, nt), in_specs=[own, spec], out_specs=spec),
        out_shape=jax.ShapeDtypeStruct(theirs.shape, theirs.dtype), compiler_params=_params(),
    )(core, g, theirs)


def sum_lead(a, name):
    nl, rows, cols = a.shape
    t = _row_tile(rows, cols, (1 << 20) // 2)

    def body(a_ref, o_ref):
        acc = a_ref[0].astype(F32)
        for s in range(1, nl):
            acc = acc + a_ref[s].astype(F32)
        o_ref[...] = acc

    return pl.pallas_call(
        body, name=name, grid=(rows // t,),
        in_specs=[pl.BlockSpec((nl, t, cols), lambda i: (0, i, 0))],
        out_specs=pl.BlockSpec((t, cols), lambda i: (i, 0)),
        out_shape=jax.ShapeDtypeStruct((rows, cols), F32), compiler_params=_params(),
    )(a)


def sum_chips(rx, csum, chip, name):
    nc, rows, cols = rx.shape
    t = _row_tile(rows, cols, (1 << 20) // 2)

    def body(chip_ref, r0, r1, r2, r3, own_ref, o_ref):
        acc = None
        for s, ref in enumerate((r0, r1, r2, r3)):
            term = jnp.where(chip_ref[0] == s, own_ref[0], ref[0]).astype(F32)
            acc = term if acc is None else acc + term
        o_ref[...] = acc

    def slot(s):
        return pl.BlockSpec((1, t, cols), lambda i, c, s=s: (jnp.where(c[0] == s, c[0] ^ 1, s), i, 0))

    return pl.pallas_call(
        body, name=name,
        grid_spec=pltpu.PrefetchScalarGridSpec(
            num_scalar_prefetch=1, grid=(rows // t,),
            in_specs=[slot(s) for s in range(nc)] + [pl.BlockSpec((1, t, cols), lambda i, c: (c[0], i, 0))],
            out_specs=pl.BlockSpec((t, cols), lambda i, c: (i, 0))),
        out_shape=jax.ShapeDtypeStruct((rows, cols), F32), compiler_params=_params(),
    )(chip, rx, rx, rx, rx, csum)


def _adamw_update(w, g, m, v):
    nm = ADAM_B1 * m + (1.0 - ADAM_B1) * g
    nv = ADAM_B2 * v + (1.0 - ADAM_B2) * (g * g)
    m_hat = nm * (1.0 / (1.0 - ADAM_B1 ** ADAM_STEP))
    v_hat = nv * (1.0 / (1.0 - ADAM_B2 ** ADAM_STEP))
    return -ADAM_LR * (m_hat / (jnp.sqrt(v_hat) + ADAM_EPS) + ADAM_WD * w), nm, nv


def adamw(w, g, m, v, name):
    rows, cols = w.shape
    t = _row_tile(rows, cols)

    def body(w_ref, g_ref, m_ref, v_ref, d_ref, nm_ref, nv_ref):
        d_ref[...], nm_ref[...], nv_ref[...] = _adamw_update(w_ref[...], g_ref[...], m_ref[...], v_ref[...])

    spec = pl.BlockSpec((t, cols), lambda i: (i, 0))
    shp = jax.ShapeDtypeStruct((rows, cols), F32)
    return pl.pallas_call(
        body, name=name, grid=(rows // t,), in_specs=[spec] * 4, out_specs=[spec] * 3,
        out_shape=[shp, shp, shp], compiler_params=_params(),
    )(w, g, m, v)


def adamw_halves(w, g_mine, g_other, m, v, core, name):
    rows, cols = w.shape
    hr = rows // 2
    t = _row_tile(hr, cols)
    nt = hr // t

    def body(core_ref, w_ref, gm_ref, go_ref, m_ref, v_ref, g_ref, d_ref, nm_ref, nv_ref):
        g = jnp.where(pl.program_id(0) // nt == core_ref[0], gm_ref[...], go_ref[...])
        g_ref[...] = g
        d_ref[...], nm_ref[...], nv_ref[...] = _adamw_update(w_ref[...], g, m_ref[...], v_ref[...])

    spec = pl.BlockSpec((t, cols), lambda i, c: (i, 0))
    half = pl.BlockSpec((t, cols), lambda i, c: (i % nt, 0))
    shp = jax.ShapeDtypeStruct((rows, cols), F32)
    return pl.pallas_call(
        body, name=name,
        grid_spec=pltpu.PrefetchScalarGridSpec(num_scalar_prefetch=1, grid=(2 * nt,),
                                               in_specs=[spec, half, half, spec, spec], out_specs=[spec] * 4),
        out_shape=[shp] * 4, compiler_params=_params(),
    )(core, w, g_mine, g_other, m, v)


WEIGHTS = ("norm_mix_pre", "norm_mix_post", "norm_ffn_pre", "norm_ffn_post", "w_in", "conv_short_w",
           "w_conv_branch", "lru_conv_w", "lru_conv_b", "lru_wa", "lru_ba", "lru_wx", "lru_bx", "lru_lambda",
           "w_lru_branch", "w_out", "ffn_w_up", "ffn_conv_w", "ffn_conv_b", "ffn_w_down")
BIG = ("w_in", "ffn_w_up", "w_conv_branch", "w_lru_branch", "w_out", "ffn_w_down")
BIG_KIND = ("col", "col", "row", "row", "row", "row")
SMALL = ("conv_short_w", "lru_conv_w", "lru_wa", "lru_ba", "lru_wx", "lru_bx", "ffn_conv_w")
REPL = ("norm_mix_pre", "norm_mix_post", "norm_ffn_pre", "norm_ffn_post", "lru_conv_b", "lru_lambda", "ffn_conv_b")
PACK_W = 256
SMALL_ROWS = 544
REPL_ROWS = 16
FFN_SHARD = 2 * D_FF // N_CHIPS
QUARTER = HEAD_DIM // N_CHIPS


def _pack_small_shard(p):
    rows = [p["conv_short_w"].reshape(3, PACK_W), p["lru_conv_w"].reshape(4, PACK_W),
            p["lru_wa"].reshape(LRU_HEADS * QUARTER, PACK_W), p["lru_ba"].reshape(1, PACK_W),
            p["lru_wx"].reshape(LRU_HEADS * QUARTER, PACK_W), p["lru_bx"].reshape(1, PACK_W),
            p["ffn_conv_w"].reshape(3 * FFN_SHARD // PACK_W, PACK_W)]
    used = sum(r.shape[0] for r in rows)
    return jnp.concatenate(rows + [jnp.zeros((SMALL_ROWS - used, PACK_W), F32)], axis=0)


def _unpack_small_shard(buf):
    out, r = {}, 0
    for name, nr, shape in (("conv_short_w", 3, (1, 3, PACK_W)), ("lru_conv_w", 4, (1, 4, PACK_W)),
                            ("lru_wa", LRU_HEADS * QUARTER, (1, LRU_HEADS, QUARTER, HEAD_DIM)),
                            ("lru_ba", 1, (1, LRU_HEADS, QUARTER)),
                            ("lru_wx", LRU_HEADS * QUARTER, (1, LRU_HEADS, QUARTER, HEAD_DIM)),
                            ("lru_bx", 1, (1, LRU_HEADS, QUARTER)),
                            ("ffn_conv_w", 3 * FFN_SHARD // PACK_W, (1, 3, FFN_SHARD))):
        out[name] = buf[r:r + nr].reshape(shape)
        r += nr
    return out


def _full_small(g4):
    per = [_unpack_small_shard(g4[k]) for k in range(N_CHIPS)]
    cat = lambda name, axis: jnp.concatenate([per[k][name][0] for k in range(N_CHIPS)], axis=axis)
    return dict(conv_short_w=cat("conv_short_w", 1), lru_conv_w=cat("lru_conv_w", 1),
                lru_wa=cat("lru_wa", 1), lru_ba=cat("lru_ba", 1).reshape(1, D_MODEL),
                lru_wx=cat("lru_wx", 1), lru_bx=cat("lru_bx", 1).reshape(1, D_MODEL),
                ffn_conv_w=cat("ffn_conv_w", 1))


def _split_small(full):
    shards = []
    for k in range(N_CHIPS):
        cols = lambda a, w: a[:, k * w:(k + 1) * w]
        q = slice(k * QUARTER, (k + 1) * QUARTER)
        shards.append(_pack_small_shard(dict(
            conv_short_w=cols(full["conv_short_w"], PACK_W), lru_conv_w=cols(full["lru_conv_w"], PACK_W),
            lru_wa=full["lru_wa"][:, q, :], lru_ba=full["lru_ba"].reshape(LRU_HEADS, HEAD_DIM)[:, q],
            lru_wx=full["lru_wx"][:, q, :], lru_bx=full["lru_bx"].reshape(LRU_HEADS, HEAD_DIM)[:, q],
            ffn_conv_w=cols(full["ffn_conv_w"], FFN_SHARD))))
    return jnp.stack(shards)


def _pack_repl(p):
    rows = [p[n].reshape(-1, D_MODEL) for n in REPL]
    used = sum(r.shape[0] for r in rows)
    return jnp.concatenate(rows + [jnp.zeros((REPL_ROWS - used, D_MODEL), F32)], axis=0)


def _unpack_repl(buf):
    out, r = {}, 0
    for n in REPL:
        nr = (2 * D_FF // D_MODEL) if n == "ffn_conv_b" else 1
        out[n] = buf[r:r + nr].reshape(1, nr * D_MODEL)
        r += nr
    return out


def kernel(x, norm_mix_pre, norm_mix_post, norm_ffn_pre, norm_ffn_post, w_in, conv_short_w, w_conv_branch, lru_conv_w, lru_conv_b, lru_wa, lru_ba, lru_wx, lru_bx, lru_lambda, w_lru_branch, w_out, ffn_w_up, ffn_conv_w, ffn_conv_b, ffn_w_down, loss_target, m_norm_mix_pre, m_norm_mix_post, m_norm_ffn_pre, m_norm_ffn_post, m_w_in, m_conv_short_w, m_w_conv_branch, m_lru_conv_w, m_lru_conv_b, m_lru_wa, m_lru_ba, m_lru_wx, m_lru_bx, m_lru_lambda, m_w_lru_branch, m_w_out, m_ffn_w_up, m_ffn_conv_w, m_ffn_conv_b, m_ffn_w_down, v_norm_mix_pre, v_norm_mix_post, v_norm_ffn_pre, v_norm_ffn_post, v_w_in, v_conv_short_w, v_w_conv_branch, v_lru_conv_w, v_lru_conv_b, v_lru_wa, v_lru_ba, v_lru_wx, v_lru_bx, v_lru_lambda, v_w_lru_branch, v_w_out, v_ffn_w_up, v_ffn_conv_w, v_ffn_conv_b, v_ffn_w_down):
    given = dict(locals())
    w = {n: given[n] for n in WEIGHTS}
    m = {n: given["m_" + n] for n in WEIGHTS}
    v = {n: given["v_" + n] for n in WEIGHTS}

    xi, yi, ci = _position()
    chip_i = 2 * xi + yi
    chip = chip_i.astype(jnp.int32).reshape(1)
    core = ci.astype(jnp.int32).reshape(1)
    xs, target = x[0], loss_target[0]
    g1, g2, g3, g4 = w["norm_mix_pre"], w["norm_mix_post"], w["norm_ffn_pre"], w["norm_ffn_post"]
    shard = {n: w[n][0].astype(BF16) for n in BIG}
    small_shard = _pack_small_shard(w)

    def gathered(bufs, names):
        return [_own_slot(b, small_shard if n == "small" else shard[n], chip_i) for b, n in zip(bufs, names)]

    def chip_sums(arrays, kinds, tag):
        theirs = pair_split(arrays, kinds, "pair_split_" + tag)
        return [add_pair(g, k, t, core, "pair_add_%s_%d" % (tag, i)) for i, (g, k, t) in enumerate(zip(arrays, kinds, theirs))]

    h1 = norm_in(xs, g1)
    win4, small4 = gathered(run_ride(gather_ride([shard["w_in"], small_shard]), "gather_first"), ("w_in", "small"))
    small = _full_small(small4)
    (proj,), got = matmul_cols(h1, win4, "proj_fwd",
                               ride=gather_ride([shard["w_conv_branch"], shard["w_lru_branch"], shard["w_out"]]))
    wcb, wlb, wout = [g.reshape(-1, D_MODEL) for g in gathered(got, ("w_conv_branch", "w_lru_branch", "w_out"))]
    q, ya = mix_conv_fwd(proj, small["conv_short_w"])
    half_k = D_MODEL // 2
    (xl, r, gi, h, yb), got = mix_lru_fwd(
        proj, small["lru_conv_w"], w["lru_conv_b"], small["lru_wa"].astype(BF16), small["lru_ba"],
        small["lru_wx"].astype(BF16), small["lru_bx"], w["lru_lambda"],
        ride=gather_ride([shard["ffn_w_up"]], items=[(0, 0, half_k)]))
    (a, b, merged), got = branch_merge_fwd(ya, yb, wcb, wlb, proj,
                                           ride=gather_ride([shard["ffn_w_up"]], items=[(0, half_k, half_k)], into=got))
    (wup4,) = gathered(got, ("ffn_w_up",))
    mix, x2, h2 = mix_out_fwd(merged, wout, xs, g2, g3)
    (up,), got = matmul_cols(h2, wup4, "up_fwd", ride=gather_ride([shard["ffn_w_down"]]))
    wdown = gathered(got, ("ffn_w_down",))[0].reshape(-1, D_MODEL)
    f = ffn_act_fwd(up, small["ffn_conv_w"], w["ffn_conv_b"])
    dy, dout, loss, dg4 = ffn_down_loss(f, wdown, x2, target, g4)

    dug, duv, dwdown, dfw_g, dfw_v, dfb_g, dfb_v = ffn_bwd(dout, wdown, up, small["ffn_conv_w"], w["ffn_conv_b"])
    dup = jnp.concatenate([dug, duv], axis=1)
    cs_down = chip_sums([dwdown], ["row"], "down")
    (dh2, dwup), rx_down = dgrad_wgrad_cols(dup, wup4, h2, "up_bwd", ride=exchange_ride(cs_down))
    cs_up = chip_sums([dwup], ["col"], "up")
    dx2, dmix, dg3, dg2 = norms_mid_bwd(dh2, x2, dy, mix, g3, g2)
    quarter_k = half_k // 2
    (da, db, dwout, dgc, dgl), rx_up = mix_out_bwd(dmix, wout, merged, a, b, proj,
                                                   ride=exchange_ride(cs_up, items=[(0, 0, quarter_k)]))
    (dcb, dcc, dcx, dwcb, dws), rx_up = mix_conv_bwd(
        da, wcb, proj, q, small["conv_short_w"],
        ride=exchange_ride(cs_up, items=[(0, quarter_k, quarter_k)], into=rx_up))
    cs_mid = chip_sums([dwout, dwcb], ["row", "row"], "mid")
    (dlx, dly, dwlb, dwa, dwx, dba, dbx, dwl, dbl, dlam), rx_mid = mix_lru_bwd(
        db, wlb, proj, xl, r, gi, h, small["lru_conv_w"], small["lru_wa"].astype(BF16), small["lru_wx"].astype(BF16),
        w["lru_lambda"], ride=exchange_ride(cs_mid))
    grads = dict(norm_mix_post=dg2, norm_ffn_pre=dg3, norm_ffn_post=dg4, conv_short_w=dws, lru_conv_w=dwl,
                 lru_conv_b=dbl, lru_wa=dwa, lru_ba=dba, lru_wx=dwx, lru_bx=dbx, lru_lambda=dlam,
                 ffn_conv_w=jnp.concatenate([dfw_g, dfw_v], axis=1), ffn_conv_b=jnp.concatenate([dfb_g, dfb_v], axis=1))
    cs_late = chip_sums([dwlb, _split_small(grads)], ["row", "lead"], "late")
    dproj = jnp.concatenate([dcb, dcc, dcx, dlx, dly, dgc, dgl], axis=1)
    (dh1, dwin), rx_late = dgrad_wgrad_cols(dproj, win4, h1, "proj_bwd", ride=exchange_ride(cs_late))
    dx, grads["norm_mix_pre"] = norm_in_bwd(dh1, xs, dx2, g1)
    cs_in = chip_sums([dwin], ["col"], "in")
    rx_in, rep_all = run_ride(exchange_ride(cs_in, rep=_pack_repl(grads)), "exchange_last")

    order = (("w_in", cs_in[0], rx_in), ("ffn_w_up", cs_up[0], rx_up[0]), ("w_conv_branch", cs_mid[1], rx_mid[1]),
             ("w_lru_branch", cs_late[0], rx_late[0]), ("w_out", cs_mid[0], rx_mid[0]),
             ("ffn_w_down", cs_down[0], rx_down[0]), ("small", cs_late[1], rx_late[1]))
    halves = [sum_chips(rx, cs, chip, "chip_sum_" + n) for n, cs, rx in order]
    me = 4 * xi + 2 * yi + ci
    rep_grad = sum_lead(_own_slot(rep_all, _pack_repl(grads), me), "device_sum")
    others = pair_swap(halves)

    g_out, d_out, m_out, v_out = {}, {}, {}, {}
    for n, gm, go in zip(BIG, halves[:-1], others[:-1]):
        g, d, nm, nv = adamw_halves(w[n][0], gm, go, m[n][0], v[n][0], core, "adamw_" + n)
        g_out[n], d_out[n], m_out[n], v_out[n] = g[None], d[None], nm[None], nv[None]
    bufs = adamw_halves(small_shard, halves[-1], others[-1], _pack_small_shard(m), _pack_small_shard(v),
                        core, "adamw_small")
    for dst, buf in zip((g_out, d_out, m_out, v_out), bufs):
        dst.update(_unpack_small_shard(buf))
    d, nm, nv = adamw(_pack_repl(w), rep_grad, _pack_repl(m), _pack_repl(v), "adamw_repl")
    for dst, buf in ((g_out, rep_grad), (d_out, d), (m_out, nm), (v_out, nv)):
        dst.update(_unpack_repl(buf))

    total = lax.psum(loss[0, 0], ("x", "y", "c"))
    return (total, dx[None], *[g_out[n] for n in WEIGHTS], *[d_out[n] for n in WEIGHTS],
            *[m_out[n] for n in WEIGHTS], *[v_out[n] for n in WEIGHTS])
```

```python
import functools
import math

import jax
import jax.numpy as jnp
from jax import lax
from jax.experimental import pallas as pl
from jax.experimental.pallas import tpu as pltpu

F32 = jnp.float32
BF16 = jnp.bfloat16

D_MODEL = 1024
N_CHIPS = 4
N_SEG = 7
D_FF = 3 * D_MODEL
LRU_HEADS = 4
HEAD_DIM = D_MODEL // LRU_HEADS
LRU_C = 8.0
RMS_EPS = 1e-6
CW = 256
FW = 512
SUBLANES = 8
VMEM_LIMIT = 60 * 1024 * 1024

ADAM_LR = 0.001
ADAM_B1 = 0.9
ADAM_B2 = 0.999
ADAM_EPS = 1e-08
ADAM_WD = 0.01
ADAM_STEP = 10

_GELU_C = math.sqrt(2.0 / math.pi)
_GELU_K = 0.044715


def _params(**kw):
    return pltpu.CompilerParams(vmem_limit_bytes=VMEM_LIMIT, **kw)


def _sigmoid(x):
    return 1.0 / (1.0 + jnp.exp(-x))


def _gelu(x):
    t = jnp.tanh(_GELU_C * (x + _GELU_K * x * x * x))
    return 0.5 * x * (1.0 + t)


def _gelu_and_grad(x):
    x2 = x * x
    t = jnp.tanh(_GELU_C * (x + _GELU_K * x * x2))
    g = 0.5 * x * (1.0 + t)
    dg = 0.5 * (1.0 + t) + 0.5 * x * (1.0 - t * t) * _GELU_C * (1.0 + 3.0 * _GELU_K * x2)
    return g, dg


def _log_sigmoid(x):
    e = jnp.exp(-jnp.abs(x))
    u = 1.0 + e
    l1p = jnp.where(u == 1.0, e, jnp.log(u) * e / (u - 1.0))
    return jnp.minimum(x, 0.0) - l1p


def _neg_expm1(z):
    series = -z * (1.0 + z * (0.5 + z * (1.0 / 6.0 + z * (1.0 / 24.0 + z * (1.0 / 120.0 + z * (1.0 / 720.0))))))
    return jnp.where(z > -0.2, series, 1.0 - jnp.exp(z))


def _rows(shape):
    return lax.broadcasted_iota(jnp.int32, shape, 0)


def _shift_down(x, k):
    return jnp.where(_rows(x.shape) >= k, pltpu.roll(x, k, 0), 0.0)


def _shift_up(x, k):
    n = x.shape[0]
    return jnp.where(_rows(x.shape) < n - k, pltpu.roll(x, n - k, 0), 0.0)


def _delays(x, k_width):
    return [x] + [_shift_down(x, j) for j in range(1, k_width)]


def _advances(dy, k_width):
    return [dy] + [_shift_up(dy, j) for j in range(1, k_width)]


def _taps_sum(shifted, w_ref, b=None):
    k_width = w_ref.shape[0]
    y = w_ref[k_width - 1:k_width, :] * shifted[0]
    for j in range(1, k_width):
        y = y + w_ref[k_width - 1 - j:k_width - j, :] * shifted[j]
    if b is not None:
        y = y + b
    return y


def _causal_conv(x, w_ref, b=None):
    return _taps_sum(_delays(x, w_ref.shape[0]), w_ref, b)


def _conv_wgrad(advanced, x):
    k_width = len(advanced)
    rows = [jnp.sum(advanced[k_width - 1 - k] * x, axis=0, keepdims=True) for k in range(k_width)]
    return jnp.concatenate(rows, axis=0)


def _dot(a, b):
    return jnp.dot(a, b, preferred_element_type=F32)


def _dot_nt(a, b):
    return lax.dot_general(a, b, (((1,), (1,)), ((), ())), preferred_element_type=F32)


def _dot_tn(a, b):
    return lax.dot_general(a, b, (((0,), (0,)), ((), ())), preferred_element_type=F32)


def _rms_stats(x):
    r = lax.rsqrt(jnp.mean(x * x, axis=-1, keepdims=True) + RMS_EPS)
    return x * r, r


def _rms_bwd(n, r, g, dy):
    dn = dy * g
    dx = r * (dn - n * jnp.mean(dn * n, axis=-1, keepdims=True))
    return dx, dy * n


def _scan_forward(a_ref, b_ref, h_ref):
    n, c = a_ref.shape
    row = lax.broadcasted_iota(jnp.int32, (SUBLANES, c), 0)

    def group(g, carry):
        r0 = pl.multiple_of(g * SUBLANES, SUBLANES)
        a = a_ref[pl.ds(r0, SUBLANES), :]
        b = b_ref[pl.ds(r0, SUBLANES), :]
        for k in (1, 2, 4):
            ap = jnp.where(row >= k, pltpu.roll(a, k, 0), 1.0)
            bp = jnp.where(row >= k, pltpu.roll(b, k, 0), 0.0)
            b = a * bp + b
            a = a * ap
        h = a * carry + b
        h_ref[pl.ds(r0, SUBLANES), :] = h
        return h[SUBLANES - 1:SUBLANES, :]

    lax.fori_loop(0, n // SUBLANES, group, jnp.zeros((1, c), F32))


def _scan_backward(c_ref, b_ref, g_ref):
    n, ch = c_ref.shape
    row = lax.broadcasted_iota(jnp.int32, (SUBLANES, ch), 0)
    n_groups = n // SUBLANES

    def group(i, carry):
        r0 = pl.multiple_of((n_groups - 1 - i) * SUBLANES, SUBLANES)
        a = c_ref[pl.ds(r0, SUBLANES), :]
        b = b_ref[pl.ds(r0, SUBLANES), :]
        for k in (1, 2, 4):
            keep = row < SUBLANES - k
            ap = jnp.where(keep, pltpu.roll(a, SUBLANES - k, 0), 1.0)
            bp = jnp.where(keep, pltpu.roll(b, SUBLANES - k, 0), 0.0)
            b = a * bp + b
            a = a * ap
        g = a * carry + b
        g_ref[pl.ds(r0, SUBLANES), :] = g
        return g[0:1, :]

    lax.fori_loop(0, n_groups, group, jnp.zeros((1, ch), F32))


MESH = pl.DeviceIdType.MESH
_HBM = pl.BlockSpec(memory_space=pltpu.HBM)
_OTHER_CHIPS = ((1, 0), (0, 1), (1, 1))
_OTHER_DEVICES = tuple((dx, dy, dc) for dx in (0, 1) for dy in (0, 1) for dc in (0, 1) if dx or dy or dc)
N_DEVICES = 8


def _position():
    return lax.axis_index("x"), lax.axis_index("y"), lax.axis_index("c")


def _flip(v, d):
    return 1 - v if d else v


def _chip(x, y, p):
    px, py = _flip(x, _OTHER_CHIPS[p][0]), _flip(y, _OTHER_CHIPS[p][1])
    return px, py, 2 * px + py


class _Ride:
    def __init__(self, srcs, bufs, scratch, plan):
        self.srcs, self.bufs, self.scratch, self.plan = list(srcs), list(bufs), list(scratch), plan


def _call(body, *, name, grid, in_specs, out_specs, out_shape, operands, scratch_shapes=(), ride=None):
    in_specs, out_specs, out_shape = list(in_specs), list(out_specs), list(out_shape)
    scratch_shapes = list(scratch_shapes)
    if ride is None:
        return pl.pallas_call(body, name=name, grid=grid, in_specs=in_specs, out_specs=out_specs, out_shape=out_shape,
                              scratch_shapes=scratch_shapes, compiler_params=_params())(*operands)
    n_in, n_out, n_scr = len(in_specs), len(out_shape), len(scratch_shapes)
    old = [i for i, b in enumerate(ride.bufs) if not isinstance(b, jax.ShapeDtypeStruct)]
    n_src, n_old, n_buf = len(ride.srcs), len(old), len(ride.bufs)

    def full_body(*refs):
        o0 = n_in + n_src + n_old
        s0 = o0 + n_out + n_buf
        start, finish = ride.plan(refs[n_in:n_in + n_src], refs[o0 + n_out:s0], refs[s0 + n_scr:])
        ids = [pl.program_id(i) for i in range(len(grid))]
        first = functools.reduce(jnp.logical_and, [i == 0 for i in ids])
        last = functools.reduce(jnp.logical_and, [i == g - 1 for i, g in zip(ids, grid)])
        pl.when(first)(start)
        body(*refs[:n_in], *refs[o0:o0 + n_out], *refs[s0:s0 + n_scr])
        pl.when(last)(finish)

    shapes = [jax.ShapeDtypeStruct(b.shape, b.dtype) for b in ride.bufs]
    res = pl.pallas_call(
        full_body, name=name, grid=grid,
        in_specs=in_specs + [_HBM] * (n_src + n_old), out_specs=out_specs + [_HBM] * n_buf,
        out_shape=out_shape + shapes, scratch_shapes=scratch_shapes + ride.scratch,
        input_output_aliases={n_in + n_src + k: n_out + i for k, i in enumerate(old)},
        compiler_params=_params(),
    )(*operands, *ride.srcs, *[ride.bufs[i] for i in old])
    return list(res[:n_out]), list(res[n_out:])


def run_ride(ride, name):
    def body():
        pass

    return _call(body, name=name, grid=(1,), in_specs=[], out_specs=[], out_shape=[], operands=[], ride=ride)[1]


def gather_ride(shards, items=None, into=None):
    items = items or [(a, 0, s.shape[0]) for a, s in enumerate(shards)]
    bufs = into or [jax.ShapeDtypeStruct((N_CHIPS,) + s.shape, s.dtype) for s in shards]
    nrel = len(_OTHER_CHIPS)

    def plan(srcs, dsts, sems):
        ici_send, ici_recv, sib_send, sib_recv = sems
        x, y, c = _position()
        j = 2 * x + y

        def rows(ref, it, h):
            return ref.at[pl.ds(it[1] + h * (it[2] // 2), it[2] // 2), :]

        def ici(i, p, slot):
            it = items[i]
            px, py, _ = _chip(x, y, p)
            return pltpu.make_async_remote_copy(
                src_ref=rows(srcs[it[0]], it, c), dst_ref=rows(dsts[it[0]].at[slot], it, c),
                send_sem=ici_send.at[i * nrel + p], recv_sem=ici_recv.at[i * nrel + p],
                device_id=(px, py, c), device_id_type=MESH)

        def sib(i, p, h):
            it = items[i]
            part = rows(dsts[it[0]].at[_chip(x, y, p)[2]], it, h)
            return pltpu.make_async_remote_copy(
                src_ref=part, dst_ref=part, send_sem=sib_send.at[i * nrel + p], recv_sem=sib_recv.at[i * nrel + p],
                device_id=(x, y, 1 - c), device_id_type=MESH)

        pairs = [(i, p) for i in range(len(items)) for p in range(nrel)]

        def start():
            for i, p in pairs:
                ici(i, p, j).start()

        def finish():
            for i, p in pairs:
                ici(i, p, _chip(x, y, p)[2]).wait_recv()
                sib(i, p, c).start()
            for i, p in pairs:
                sib(i, p, 1 - c).wait_recv()
            for i, p in pairs:
                ici(i, p, j).wait_send()
                sib(i, p, c).wait_send()

        return start, finish

    return _Ride(shards, bufs, [pltpu.SemaphoreType.DMA((len(items) * nrel,))] * 4, plan)


def exchange_ride(sums, items=None, into=None, rep=None):
    items = items or [(a, 0, s.shape[1]) for a, s in enumerate(sums)]
    into = into or [None] * len(sums)
    bufs = [jax.ShapeDtypeStruct(s.shape, s.dtype) if b is None else b for s, b in zip(sums, into)]
    srcs = list(sums)
    scratch = [pltpu.SemaphoreType.DMA((len(items) * len(_OTHER_CHIPS),))] * 2
    if rep is not None:
        srcs.append(rep)
        bufs.append(jax.ShapeDtypeStruct((N_DEVICES,) + rep.shape, rep.dtype))
        scratch += [pltpu.SemaphoreType.DMA((len(_OTHER_DEVICES),))] * 2
    nrel = len(_OTHER_CHIPS)

    def plan(src_refs, dst_refs, sems):
        x, y, c = _position()
        j = 2 * x + y
        me = 4 * x + 2 * y + c

        def part(i, p, src_slot, dst_slot):
            a, r0, nr = items[i]
            px, py, _ = _chip(x, y, p)
            return pltpu.make_async_remote_copy(
                src_ref=src_refs[a].at[src_slot, pl.ds(r0, nr), :], dst_ref=dst_refs[a].at[dst_slot, pl.ds(r0, nr), :],
                send_sem=sems[0].at[i * nrel + p], recv_sem=sems[1].at[i * nrel + p],
                device_id=(px, py, c), device_id_type=MESH)

        def device(q):
            dx, dy, dc = _OTHER_DEVICES[q]
            return _flip(x, dx), _flip(y, dy), _flip(c, dc)

        def rep_copy(q, slot):
            return pltpu.make_async_remote_copy(
                src_ref=src_refs[-1], dst_ref=dst_refs[-1].at[slot], send_sem=sems[2].at[q], recv_sem=sems[3].at[q],
                device_id=device(q), device_id_type=MESH)

        pairs = [(i, p) for i in range(len(items)) for p in range(nrel)]
        others = range(len(_OTHER_DEVICES)) if rep is not None else ()

        def start():
            for i, p in pairs:
                part(i, p, _chip(x, y, p)[2], j).start()
            for q in others:
                rep_copy(q, me).start()

        def finish():
            for i, p in pairs:
                k = _chip(x, y, p)[2]
                part(i, p, k, k).wait_recv()
            for q in others:
                px, py, pc = device(q)
                rep_copy(q, 4 * px + 2 * py + pc).wait_recv()
            for i, p in pairs:
                part(i, p, _chip(x, y, p)[2], j).wait_send()
            for q in others:
                rep_copy(q, me).wait_send()

        return start, finish

    return _Ride(srcs, bufs, scratch, plan)


def _own_slot(buf, own, index):
    return lax.dynamic_update_slice(buf, own[None], (index,) + (0,) * own.ndim)


def _token_tile(s):
    return min(s, 512)


def norm_in(x, g):
    s, d = x.shape
    t = _token_tile(s)

    def body(x_ref, g_ref, o_ref):
        n, _ = _rms_stats(x_ref[...])
        o_ref[...] = (n * g_ref[...]).astype(BF16)

    return pl.pallas_call(
        body, name="norm_in", grid=(s // t,),
        in_specs=[pl.BlockSpec((t, d), lambda i: (i, 0)), pl.BlockSpec((1, d), lambda i: (0, 0))],
        out_specs=pl.BlockSpec((t, d), lambda i: (i, 0)),
        out_shape=jax.ShapeDtypeStruct((s, d), BF16),
        compiler_params=_params(),
    )(x, g)


def matmul_cols(a, w4, name, ride=None):
    m, k = a.shape
    nj, _, ns = w4.shape
    nb = ns // CW

    def body(a_ref, w_ref, o_ref):
        o_ref[...] = _dot(a_ref[...], w_ref[0])

    return _call(
        body, name=name, grid=(nj, nb),
        in_specs=[pl.BlockSpec((m, k), lambda j, b: (0, 0)),
                  pl.BlockSpec((1, k, CW), lambda j, b: (j, 0, b))],
        out_specs=[pl.BlockSpec((m, CW), lambda j, b: (0, j * nb + b))],
        out_shape=[jax.ShapeDtypeStruct((m, nj * ns), F32)],
        operands=(a, w4), ride=ride)


def mix_conv_fwd(proj, ws, ride=None):
    s = proj.shape[0]
    nblk = D_MODEL // CW

    def body(cb_ref, cc_ref, cx_ref, ws_ref, q_ref, ya_ref):
        q = _causal_conv(cc_ref[...] * cx_ref[...], ws_ref)
        q_ref[...] = q
        ya_ref[...] = (cb_ref[...] * q).astype(BF16)

    seg = lambda k: pl.BlockSpec((s, CW), lambda c, k=k: (0, k * nblk + c))
    return _call(
        body, name="mix_conv_fwd", grid=(nblk,),
        in_specs=[seg(0), seg(1), seg(2), pl.BlockSpec((3, CW), lambda c: (0, c))],
        out_specs=[pl.BlockSpec((s, CW), lambda c: (0, c))] * 2,
        out_shape=[jax.ShapeDtypeStruct((s, D_MODEL), F32), jax.ShapeDtypeStruct((s, D_MODEL), BF16)],
        operands=(proj, proj, proj, ws), ride=ride)


def _lru_gates(r, ls):
    log_a = LRU_C * r * ls
    a = jnp.exp(log_a)
    mult = jnp.sqrt(_neg_expm1(2.0 * log_a))
    mult = jnp.where(_rows(r.shape) == 0, 1.0, mult)
    return a, mult


def mix_lru_fwd(proj, wl, bl, wa, ba, wx, bx, lam, ride=None):
    s = proj.shape[0]
    nblk = D_MODEL // CW

    def body(lx_ref, ly_ref, wl_ref, bl_ref, wa_ref, ba_ref, wx_ref, bx_ref, lam_ref,
             xl_ref, r_ref, i_ref, h_ref, yb_ref, a_scr, u_scr):
        xl = _causal_conv(lx_ref[...], wl_ref, bl_ref[...])
        xl_ref[...] = xl
        xlb = xl.astype(BF16)
        r = _sigmoid(_dot(xlb, wa_ref[0]) + ba_ref[...])
        i = _sigmoid(_dot(xlb, wx_ref[0]) + bx_ref[...])
        r_ref[...] = r
        i_ref[...] = i
        a, mult = _lru_gates(r, _log_sigmoid(lam_ref[...]))
        a_scr[...] = a
        u_scr[...] = mult * i * xl
        _scan_forward(a_scr, u_scr, h_ref)
        yb_ref[...] = (h_ref[...] * _gelu(ly_ref[...])).astype(BF16)

    blk = lambda k: pl.BlockSpec((s, CW), lambda c, k=k: (0, k * nblk + c))
    vec = pl.BlockSpec((1, CW), lambda c: (0, c))
    mat = pl.BlockSpec((1, CW, CW), lambda c: (c, 0, 0))
    out = pl.BlockSpec((s, CW), lambda c: (0, c))
    f = jax.ShapeDtypeStruct((s, D_MODEL), F32)
    return _call(
        body, name="mix_lru_fwd", grid=(nblk,),
        in_specs=[blk(3), blk(4), pl.BlockSpec((4, CW), lambda c: (0, c)), vec, mat, vec, mat, vec, vec],
        out_specs=[out] * 5,
        out_shape=[f, f, f, f, jax.ShapeDtypeStruct((s, D_MODEL), BF16)],
        scratch_shapes=[pltpu.VMEM((s, CW), F32), pltpu.VMEM((s, CW), F32)],
        operands=(proj, proj, wl, bl, wa, ba, wx, bx, lam), ride=ride)


def branch_merge_fwd(ya, yb, wcb, wlb, proj, ride=None):
    s = ya.shape[0]
    nblk = D_MODEL // CW

    def body(ya_ref, yb_ref, wcb_ref, wlb_ref, gc_ref, gl_ref, a_ref, b_ref, m_ref):
        a = _dot(ya_ref[...], wcb_ref[...])
        b = _dot(yb_ref[...], wlb_ref[...])
        a_ref[...] = a
        b_ref[...] = b
        m_ref[...] = (_sigmoid(gc_ref[...]) * a + _sigmoid(gl_ref[...]) * b).astype(BF16)

    res = pl.BlockSpec((s, D_MODEL), lambda n: (0, 0))
    wcol = pl.BlockSpec((D_MODEL, CW), lambda n: (0, n))
    blk = lambda k: pl.BlockSpec((s, CW), lambda n, k=k: (0, k * nblk + n))
    out = pl.BlockSpec((s, CW), lambda n: (0, n))
    f = jax.ShapeDtypeStruct((s, D_MODEL), F32)
    return _call(
        body, name="branch_merge_fwd", grid=(nblk,),
        in_specs=[res, res, wcol, wcol, blk(5), blk(6)],
        out_specs=[out] * 3,
        out_shape=[f, f, jax.ShapeDtypeStruct((s, D_MODEL), BF16)],
        operands=(ya, yb, wcb, wlb, proj, proj), ride=ride)


def mix_out_fwd(merged, wout, x, g2, g3, ride=None):
    s, d = x.shape
    t = _token_tile(s)

    def body(m_ref, w_ref, x_ref, g2_ref, g3_ref, mix_ref, x2_ref, h2_ref):
        mix = _dot(m_ref[...], w_ref[...])
        mix_ref[...] = mix
        n, _ = _rms_stats(mix)
        x2 = x_ref[...] + n * g2_ref[...]
        x2_ref[...] = x2
        n2, _ = _rms_stats(x2)
        h2_ref[...] = (n2 * g3_ref[...]).astype(BF16)

    tile = pl.BlockSpec((t, d), lambda i: (i, 0))
    vec = pl.BlockSpec((1, d), lambda i: (0, 0))
    f = jax.ShapeDtypeStruct((s, d), F32)
    return _call(
        body, name="mix_out_fwd", grid=(s // t,),
        in_specs=[tile, pl.BlockSpec((d, d), lambda i: (0, 0)), tile, vec, vec],
        out_specs=[tile] * 3,
        out_shape=[f, f, jax.ShapeDtypeStruct((s, d), BF16)],
        operands=(merged, wout, x, g2, g3), ride=ride)


def ffn_act_fwd(up, fw, fb):
    s = up.shape[0]
    nblk = D_FF // FW

    def body(ug_ref, uv_ref, wg_ref, wv_ref, bg_ref, bv_ref, f_ref):
        gate = _causal_conv(ug_ref[...], wg_ref, bg_ref[...])
        val = _causal_conv(uv_ref[...], wv_ref, bv_ref[...])
        f_ref[...] = (_gelu(gate) * val).astype(BF16)

    half = lambda h, rows: pl.BlockSpec((rows, FW), lambda n, h=h: (0, h * nblk + n))
    return pl.pallas_call(
        body, name="ffn_act_fwd", grid=(nblk,),
        in_specs=[half(0, s), half(1, s), half(0, 3), half(1, 3), half(0, 1), half(1, 1)],
        out_specs=pl.BlockSpec((s, FW), lambda n: (0, n)),
        out_shape=jax.ShapeDtypeStruct((s, D_FF), BF16),
        compiler_params=_params(),
    )(up, up, fw, fw, fb, fb)


def ffn_down_loss(f, wdown, x2, target, g4):
    s, d = x2.shape
    t = _token_tile(s)

    def body(f_ref, w_ref, x2_ref, tg_ref, g4_ref, dy_ref, dout_ref, loss_ref, dg4_ref):
        @pl.when(pl.program_id(0) == 0)
        def _():
            loss_ref[...] = jnp.zeros_like(loss_ref)
            dg4_ref[...] = jnp.zeros_like(dg4_ref)

        out = _dot(f_ref[...], w_ref[...])
        n, r = _rms_stats(out)
        err = x2_ref[...] + n * g4_ref[...] - tg_ref[...]
        loss_ref[...] += jnp.full(loss_ref.shape, (0.5 / d) * jnp.sum(err * err), F32)
        dy = err * (1.0 / d)
        dy_ref[...] = dy
        dout, dg = _rms_bwd(n, r, g4_ref[...], dy)
        dout_ref[...] = dout.astype(BF16)
        dg4_ref[...] += jnp.sum(dg, axis=0, keepdims=True)

    tile = pl.BlockSpec((t, d), lambda i: (i, 0))
    vec = pl.BlockSpec((1, d), lambda i: (0, 0))
    return pl.pallas_call(
        body, name="ffn_down_loss", grid=(s // t,),
        in_specs=[pl.BlockSpec((t, D_FF), lambda i: (i, 0)), pl.BlockSpec((D_FF, d), lambda i: (0, 0)), tile, tile, vec],
        out_specs=[tile, tile, pl.BlockSpec((1, 128), lambda i: (0, 0)), vec],
        out_shape=[jax.ShapeDtypeStruct((s, d), F32), jax.ShapeDtypeStruct((s, d), BF16),
                   jax.ShapeDtypeStruct((1, 128), F32), jax.ShapeDtypeStruct((1, d), F32)],
        compiler_params=_params(),
    )(f, wdown, x2, target, g4)


def ffn_bwd(dout, wdown, up, fw, fb):
    s = up.shape[0]
    nblk = D_FF // FW

    def body(do_ref, wd_ref, ug_ref, uv_ref, wg_ref, wv_ref, bg_ref, bv_ref,
             dug_ref, duv_ref, dwd_ref, dwg_ref, dwv_ref, dbg_ref, dbv_ref):
        do = do_ref[...]
        df = _dot_nt(do, wd_ref[...])
        ug = ug_ref[...]
        uv = uv_ref[...]
        gate = _causal_conv(ug, wg_ref, bg_ref[...])
        val = _causal_conv(uv, wv_ref, bv_ref[...])
        ge, dge = _gelu_and_grad(gate)
        dwd_ref[...] = _dot_tn((ge * val).astype(BF16), do).astype(BF16)
        dgate = _advances(df * val * dge, 3)
        dval = _advances(df * ge, 3)
        dug_ref[...] = _taps_sum(dgate, wg_ref).astype(BF16)
        duv_ref[...] = _taps_sum(dval, wv_ref).astype(BF16)
        dwg_ref[...] = _conv_wgrad(dgate, ug)
        dwv_ref[...] = _conv_wgrad(dval, uv)
        dbg_ref[...] = jnp.sum(dgate[0], axis=0, keepdims=True)
        dbv_ref[...] = jnp.sum(dval[0], axis=0, keepdims=True)

    half = lambda h, rows: pl.BlockSpec((rows, FW), lambda n, h=h: (0, h * nblk + n))
    own = lambda rows: pl.BlockSpec((rows, FW), lambda n: (0, n))
    act = jax.ShapeDtypeStruct((s, D_FF), BF16)
    taps = jax.ShapeDtypeStruct((3, D_FF), F32)
    bias = jax.ShapeDtypeStruct((1, D_FF), F32)
    return pl.pallas_call(
        body, name="ffn_bwd", grid=(nblk,),
        in_specs=[pl.BlockSpec((s, D_MODEL), lambda n: (0, 0)), pl.BlockSpec((FW, D_MODEL), lambda n: (n, 0)),
                  half(0, s), half(1, s), half(0, 3), half(1, 3), half(0, 1), half(1, 1)],
        out_specs=[own(s), own(s), pl.BlockSpec((FW, D_MODEL), lambda n: (n, 0)), own(3), own(3), own(1), own(1)],
        out_shape=[act, act, jax.ShapeDtypeStruct((D_FF, D_MODEL), BF16), taps, taps, bias, bias],
        compiler_params=_params(),
    )(dout, wdown, up, up, fw, fw, fb, fb)


def dgrad_wgrad_cols(dy, w4, a, name, ride=None):
    m, k = a.shape
    nj, _, ns = w4.shape
    nb = ns // CW

    def body(dy_ref, w_ref, a_ref, da_ref, dw_ref):
        @pl.when((pl.program_id(0) == 0) & (pl.program_id(1) == 0))
        def _():
            da_ref[...] = jnp.zeros_like(da_ref)

        dyb = dy_ref[...]
        da_ref[...] += _dot_nt(dyb, w_ref[0])
        dw_ref[...] = _dot_tn(a_ref[...], dyb).astype(BF16)

    return _call(
        body, name=name, grid=(nj, nb),
        in_specs=[pl.BlockSpec((m, CW), lambda j, b: (0, j * nb + b)),
                  pl.BlockSpec((1, k, CW), lambda j, b: (j, 0, b)),
                  pl.BlockSpec((m, k), lambda j, b: (0, 0))],
        out_specs=[pl.BlockSpec((m, k), lambda j, b: (0, 0)),
                   pl.BlockSpec((k, CW), lambda j, b: (0, j * nb + b))],
        out_shape=[jax.ShapeDtypeStruct((m, k), F32), jax.ShapeDtypeStruct((k, nj * ns), BF16)],
        operands=(dy, w4, a), ride=ride)


def norms_mid_bwd(dh2, x2, dy, mix, g3, g2, ride=None):
    s, d = x2.shape
    t = _token_tile(s)

    def body(dh2_ref, x2_ref, dy_ref, mix_ref, g3_ref, g2_ref, dx2_ref, dmix_ref, dg3_ref, dg2_ref):
        @pl.when(pl.program_id(0) == 0)
        def _():
            dg3_ref[...] = jnp.zeros_like(dg3_ref)
            dg2_ref[...] = jnp.zeros_like(dg2_ref)

        n3, r3 = _rms_stats(x2_ref[...])
        dx, dg3 = _rms_bwd(n3, r3, g3_ref[...], dh2_ref[...])
        dx2 = dy_ref[...] + dx
        dx2_ref[...] = dx2
        dg3_ref[...] += jnp.sum(dg3, axis=0, keepdims=True)
        n2, r2 = _rms_stats(mix_ref[...])
        dmix, dg2 = _rms_bwd(n2, r2, g2_ref[...], dx2)
        dmix_ref[...] = dmix.astype(BF16)
        dg2_ref[...] += jnp.sum(dg2, axis=0, keepdims=True)

    tile = pl.BlockSpec((t, d), lambda i: (i, 0))
    vec = pl.BlockSpec((1, d), lambda i: (0, 0))
    v = jax.ShapeDtypeStruct((1, d), F32)
    return _call(
        body, name="norms_mid_bwd", grid=(s // t,),
        in_specs=[tile, tile, tile, tile, vec, vec],
        out_specs=[tile, tile, vec, vec],
        out_shape=[jax.ShapeDtypeStruct((s, d), F32), jax.ShapeDtypeStruct((s, d), BF16), v, v],
        operands=(dh2, x2, dy, mix, g3, g2), ride=ride)


def mix_out_bwd(dmix, wout, merged, a, b, proj, ride=None):
    s = dmix.shape[0]
    nblk = D_MODEL // CW

    def body(dm_ref, w_ref, mg_ref, a_ref, b_ref, gc_ref, gl_ref, da_ref, db_ref, dw_ref, dgc_ref, dgl_ref):
        dm = dm_ref[...]
        dmerged = _dot_nt(dm, w_ref[...])
        dw_ref[...] = _dot_tn(mg_ref[...], dm).astype(BF16)
        sc = _sigmoid(gc_ref[...])
        sl = _sigmoid(gl_ref[...])
        da_ref[...] = (dmerged * sc).astype(BF16)
        db_ref[...] = (dmerged * sl).astype(BF16)
        dgc_ref[...] = (dmerged * a_ref[...] * sc * (1.0 - sc)).astype(BF16)
        dgl_ref[...] = (dmerged * b_ref[...] * sl * (1.0 - sl)).astype(BF16)

    res = pl.BlockSpec((s, D_MODEL), lambda n: (0, 0))
    rows = pl.BlockSpec((CW, D_MODEL), lambda n: (n, 0))
    col = pl.BlockSpec((s, CW), lambda n: (0, n))
    blk = lambda k: pl.BlockSpec((s, CW), lambda n, k=k: (0, k * nblk + n))
    hb = jax.ShapeDtypeStruct((s, D_MODEL), BF16)
    return _call(
        body, name="mix_out_bwd", grid=(nblk,),
        in_specs=[res, rows, col, col, col, blk(5), blk(6)],
        out_specs=[col, col, rows, col, col],
        out_shape=[hb, hb, jax.ShapeDtypeStruct((D_MODEL, D_MODEL), BF16), hb, hb],
        operands=(dmix, wout, merged, a, b, proj, proj), ride=ride)


def mix_conv_bwd(da, wcb, proj, q, ws, ride=None):
    s = da.shape[0]
    nblk = D_MODEL // CW

    def body(da_ref, w_ref, cb_ref, cc_ref, cx_ref, q_ref, ws_ref, dcb_ref, dcc_ref, dcx_ref, dw_ref, dws_ref):
        dab = da_ref[...]
        dya = _dot_nt(dab, w_ref[...])
        cb = cb_ref[...]
        cc = cc_ref[...]
        cx = cx_ref[...]
        q = q_ref[...]
        dw_ref[...] = _dot_tn((cb * q).astype(BF16), dab).astype(BF16)
        dcb_ref[...] = (dya * q).astype(BF16)
        dq = _advances(dya * cb, 3)
        dp = _taps_sum(dq, ws_ref)
        dws_ref[...] = _conv_wgrad(dq, cc * cx)
        dcc_ref[...] = (dp * cx).astype(BF16)
        dcx_ref[...] = (dp * cc).astype(BF16)

    res = pl.BlockSpec((s, D_MODEL), lambda n: (0, 0))
    rows = pl.BlockSpec((CW, D_MODEL), lambda n: (n, 0))
    col = pl.BlockSpec((s, CW), lambda n: (0, n))
    blk = lambda k: pl.BlockSpec((s, CW), lambda n, k=k: (0, k * nblk + n))
    taps = pl.BlockSpec((3, CW), lambda n: (0, n))
    hb = jax.ShapeDtypeStruct((s, D_MODEL), BF16)
    return _call(
        body, name="mix_conv_bwd", grid=(nblk,),
        in_specs=[res, rows, blk(0), blk(1), blk(2), col, taps],
        out_specs=[col, col, col, rows, taps],
        out_shape=[hb, hb, hb, jax.ShapeDtypeStruct((D_MODEL, D_MODEL), BF16), jax.ShapeDtypeStruct((3, D_MODEL), F32)],
        operands=(da, wcb, proj, proj, proj, q, ws), ride=ride)


def mix_lru_bwd(db, wlb, proj, xl, r, i, h, wl, wa, wx, lam, ride=None):
    s = db.shape[0]
    nblk = D_MODEL // CW

    def body(db_ref, w_ref, lx_ref, ly_ref, xl_ref, r_ref, i_ref, h_ref, wl_ref, wa_ref, wx_ref, lam_ref,
             dlx_ref, dly_ref, dw_ref, dwa_ref, dwx_ref, dba_ref, dbx_ref, dwl_ref, dbl_ref, dlam_ref,
             c_scr, dh_scr, g_scr):
        dbb = db_ref[...]
        dyb = _dot_nt(dbb, w_ref[...])
        h = h_ref[...]
        ge, dge = _gelu_and_grad(ly_ref[...])
        dw_ref[...] = _dot_tn((h * ge).astype(BF16), dbb).astype(BF16)
        dly_ref[...] = (dyb * h * dge).astype(BF16)
        r = r_ref[...]
        gi = i_ref[...]
        xl = xl_ref[...]
        lam = lam_ref[...]
        ls = _log_sigmoid(lam)
        a, mult = _lru_gates(r, ls)
        c_scr[...] = _shift_up(a, 1)
        dh_scr[...] = dyb * ge
        _scan_backward(c_scr, dh_scr, g_scr)
        du = g_scr[...]
        da = du * _shift_down(h, 1)
        dmult = du * gi * xl
        di = du * mult * xl
        dxl = du * mult * gi
        first = _rows(a.shape) == 0
        dlog_a = da * a - jnp.where(first, 0.0, dmult * a * a / mult)
        dr = dlog_a * (LRU_C * ls)
        dlam_ref[...] = jnp.sum(dlog_a * r, axis=0, keepdims=True) * (LRU_C * (1.0 - _sigmoid(lam)))
        dzr = dr * r * (1.0 - r)
        dzi = di * gi * (1.0 - gi)
        dba_ref[...] = jnp.sum(dzr, axis=0, keepdims=True)
        dbx_ref[...] = jnp.sum(dzi, axis=0, keepdims=True)
        xlb = xl.astype(BF16)
        dzrb = dzr.astype(BF16)
        dzib = dzi.astype(BF16)
        dwa_ref[0] = _dot_tn(xlb, dzrb)
        dwx_ref[0] = _dot_tn(xlb, dzib)
        dxl = _advances(dxl + _dot_nt(dzrb, wa_ref[0]) + _dot_nt(dzib, wx_ref[0]), 4)
        dlx_ref[...] = _taps_sum(dxl, wl_ref).astype(BF16)
        dwl_ref[...] = _conv_wgrad(dxl, lx_ref[...])
        dbl_ref[...] = jnp.sum(dxl[0], axis=0, keepdims=True)

    res = pl.BlockSpec((s, D_MODEL), lambda n: (0, 0))
    rows = pl.BlockSpec((CW, D_MODEL), lambda n: (n, 0))
    col = pl.BlockSpec((s, CW), lambda n: (0, n))
    blk = lambda k: pl.BlockSpec((s, CW), lambda n, k=k: (0, k * nblk + n))
    taps = pl.BlockSpec((4, CW), lambda n: (0, n))
    vec = pl.BlockSpec((1, CW), lambda n: (0, n))
    mat = pl.BlockSpec((1, CW, CW), lambda n: (n, 0, 0))
    hb = jax.ShapeDtypeStruct((s, D_MODEL), BF16)
    v = jax.ShapeDtypeStruct((1, D_MODEL), F32)
    m = jax.ShapeDtypeStruct((LRU_HEADS, HEAD_DIM, HEAD_DIM), F32)
    scr = pltpu.VMEM((s, CW), F32)
    return _call(
        body, name="mix_lru_bwd", grid=(nblk,),
        in_specs=[res, rows, blk(3), blk(4), col, col, col, col, taps, mat, mat, vec],
        out_specs=[col, col, rows, mat, mat, vec, vec, taps, vec, vec],
        out_shape=[hb, hb, jax.ShapeDtypeStruct((D_MODEL, D_MODEL), BF16), m, m, v, v,
                   jax.ShapeDtypeStruct((4, D_MODEL), F32), v, v],
        scratch_shapes=[scr, scr, scr],
        operands=(db, wlb, proj, proj, xl, r, i, h, wl, wa, wx, lam), ride=ride)


def norm_in_bwd(dh1, x, dx2, g1):
    s, d = x.shape
    t = _token_tile(s)

    def body(dh_ref, x_ref, dx2_ref, g_ref, dx_ref, dg_ref):
        @pl.when(pl.program_id(0) == 0)
        def _():
            dg_ref[...] = jnp.zeros_like(dg_ref)

        n, r = _rms_stats(x_ref[...])
        dx, dg = _rms_bwd(n, r, g_ref[...], dh_ref[...])
        dx_ref[...] = dx2_ref[...] + dx
        dg_ref[...] += jnp.sum(dg, axis=0, keepdims=True)

    tile = pl.BlockSpec((t, d), lambda i: (i, 0))
    vec = pl.BlockSpec((1, d), lambda i: (0, 0))
    return pl.pallas_call(
        body, name="norm_in_bwd", grid=(s // t,),
        in_specs=[tile, tile, tile, vec],
        out_specs=[tile, vec],
        out_shape=[jax.ShapeDtypeStruct((s, d), F32), jax.ShapeDtypeStruct((1, d), F32)],
        compiler_params=_params(),
    )(dh1, x, dx2, g1)


def local_step(x, target, g1, g2, g3, g4, win4, ws, wcb, wl, bl, wa, ba, wx, bx, lam, wlb, wout, wup4, fw, fb, wdown):
    h1 = norm_in(x, g1)
    proj = matmul_cols(h1, win4, "proj_fwd")
    q, ya = mix_conv_fwd(proj, ws)
    xl, r, gi, h, yb = mix_lru_fwd(proj, wl, bl, wa, ba, wx, bx, lam)
    a, b, merged = branch_merge_fwd(ya, yb, wcb, wlb, proj)
    mix, x2, h2 = mix_out_fwd(merged, wout, x, g2, g3)
    up = matmul_cols(h2, wup4, "up_fwd")
    f = ffn_act_fwd(up, fw, fb)
    dy, dout, loss, dg4 = ffn_down_loss(f, wdown, x2, target, g4)

    dug, duv, dwdown, dfw_g, dfw_v, dfb_g, dfb_v = ffn_bwd(dout, wdown, up, fw, fb)
    dup = jnp.concatenate([dug, duv], axis=1)
    dfw = jnp.concatenate([dfw_g, dfw_v], axis=1)
    dfb = jnp.concatenate([dfb_g, dfb_v], axis=1)
    dh2, dwup = dgrad_wgrad_cols(dup, wup4, h2, "up_bwd")
    dx2, dmix, dg3, dg2 = norms_mid_bwd(dh2, x2, dy, mix, g3, g2)
    da, db, dwout, dgc, dgl = mix_out_bwd(dmix, wout, merged, a, b, proj)
    dcb, dcc, dcx, dwcb, dws = mix_conv_bwd(da, wcb, proj, q, ws)
    dlx, dly, dwlb, dwa, dwx, dba, dbx, dwl, dbl, dlam = mix_lru_bwd(db, wlb, proj, xl, r, gi, h, wl, wa, wx, lam)
    dproj = jnp.concatenate([dcb, dcc, dcx, dlx, dly, dgc[:, 5 * D_MODEL:6 * D_MODEL], dgl[:, 6 * D_MODEL:]], axis=1)
    dh1, dwin = dgrad_wgrad_cols(dproj, win4, h1, "proj_bwd")
    dx, dg1 = norm_in_bwd(dh1, x, dx2, g1)
    grads = dict(norm_mix_pre=dg1, norm_mix_post=dg2, norm_ffn_pre=dg3, norm_ffn_post=dg4,
                 w_in=dwin, conv_short_w=dws, w_conv_branch=dwcb, lru_conv_w=dwl, lru_conv_b=dbl,
                 lru_wa=dwa, lru_ba=dba, lru_wx=dwx, lru_bx=dbx, lru_lambda=dlam,
                 w_lru_branch=dwlb, w_out=dwout, ffn_w_up=dwup, ffn_conv_w=dfw, ffn_conv_b=dfb,
                 ffn_w_down=dwdown)
    return loss[0, 0], dx, grads


MESH = pl.DeviceIdType.MESH
_HBM = pl.BlockSpec(memory_space=pltpu.HBM)
_OTHER_CHIPS = ((1, 0), (0, 1), (1, 1))
_OTHER_DEVICES = tuple((dx, dy, dc) for dx in (0, 1) for dy in (0, 1) for dc in (0, 1) if dx or dy or dc)
N_DEVICES = 8


def _position():
    return lax.axis_index("x"), lax.axis_index("y"), lax.axis_index("c")


def _flip(v, d):
    return 1 - v if d else v


def _half_rows(ref, h, hr):
    return ref.at[pl.ds(h * hr, hr), :]


def gather_chips(shards):
    n = len(shards)
    nrel = len(_OTHER_CHIPS)

    def body(*refs):
        ins, outs = refs[:n], refs[n:2 * n]
        ici_send, ici_recv, sib_send, sib_recv = refs[2 * n:]
        x, y, c = _position()
        j = 2 * x + y
        hr = [s.shape[0] // 2 for s in shards]

        def chip(p):
            px, py = _flip(x, _OTHER_CHIPS[p][0]), _flip(y, _OTHER_CHIPS[p][1])
            return px, py, 2 * px + py

        def ici(a, p, slot):
            px, py, _ = chip(p)
            return pltpu.make_async_remote_copy(
                src_ref=_half_rows(ins[a], c, hr[a]), dst_ref=_half_rows(outs[a].at[slot], c, hr[a]),
                send_sem=ici_send.at[a * nrel + p], recv_sem=ici_recv.at[a * nrel + p],
                device_id=(px, py, c), device_id_type=MESH)

        def sib(a, p, h):
            _, _, k = chip(p)
            part = _half_rows(outs[a].at[k], h, hr[a])
            return pltpu.make_async_remote_copy(
                src_ref=part, dst_ref=part, send_sem=sib_send.at[a * nrel + p], recv_sem=sib_recv.at[a * nrel + p],
                device_id=(x, y, 1 - c), device_id_type=MESH)

        pairs = [(a, p) for a in range(n) for p in range(nrel)]
        for a, p in pairs:
            ici(a, p, j).start()
        for a, p in pairs:
            ici(a, p, chip(p)[2]).wait_recv()
            sib(a, p, c).start()
        for a, p in pairs:
            sib(a, p, 1 - c).wait_recv()
        for a, p in pairs:
            ici(a, p, j).wait_send()
            sib(a, p, c).wait_send()

    got = pl.pallas_call(
        body, name="gather_chips",
        in_specs=[_HBM] * n, out_specs=[_HBM] * n,
        out_shape=[jax.ShapeDtypeStruct((N_CHIPS,) + s.shape, s.dtype) for s in shards],
        scratch_shapes=[pltpu.SemaphoreType.DMA((n * nrel,))] * 4,
    )(*shards)
    j = 2 * lax.axis_index("x") + lax.axis_index("y")
    return [lax.dynamic_update_slice(g, s[None], (j, 0, 0)) for g, s in zip(got, shards)]


def _owned_part(ref, kind, k, h, hr):
    if kind == "col":
        ns = ref.shape[1] // N_CHIPS
        return ref.at[pl.ds(h * hr, hr), pl.ds(k * ns, ns)]
    if kind == "row":
        return ref.at[pl.ds(k * 2 * hr + h * hr, hr), :]
    return ref.at[k, pl.ds(h * hr, hr), :]


def _part_shape(g, kind):
    if kind == "col":
        return g.shape[0] // 2, g.shape[1] // N_CHIPS
    if kind == "row":
        return g.shape[0] // (2 * N_CHIPS), g.shape[1]
    return g.shape[1] // 2, g.shape[2]


def pair_split(grads, kinds, name):
    n = len(grads)
    shapes = [_part_shape(g, k) for g, k in zip(grads, kinds)]

    def body(*refs):
        ins, theirs = refs[:n], refs[n:2 * n]
        send_sem, recv_sem = refs[2 * n:]
        x, y, c = _position()
        copies = []
        for a in range(n):
            hr = shapes[a][0]
            for k in range(N_CHIPS):
                s = a * N_CHIPS + k
                copies.append(pltpu.make_async_remote_copy(
                    src_ref=_owned_part(ins[a], kinds[a], k, 1 - c, hr), dst_ref=theirs[a].at[k],
                    send_sem=send_sem.at[s], recv_sem=recv_sem.at[s], device_id=(x, y, 1 - c), device_id_type=MESH))
        for cp in copies:
            cp.start()
        for cp in copies:
            cp.wait()

    return pl.pallas_call(
        body, name=name,
        in_specs=[_HBM] * n, out_specs=[_HBM] * n,
        out_shape=[jax.ShapeDtypeStruct((N_CHIPS,) + shp, g.dtype) for shp, g in zip(shapes, grads)],
        scratch_shapes=[pltpu.SemaphoreType.DMA((n * N_CHIPS,))] * 2,
    )(*grads)


def chip_exchange(sums, rep):
    n = len(sums)
    nrel = len(_OTHER_CHIPS)
    ndev = len(_OTHER_DEVICES)

    def body(*refs):
        ins, rep_ref = refs[:n], refs[n]
        outs, rep_out = refs[n + 1:2 * n + 1], refs[2 * n + 1]
        loc_sem, send_sem, recv_sem, rep_send, rep_recv = refs[2 * n + 2:]
        x, y, c = _position()
        j = 2 * x + y
        me = 4 * x + 2 * y + c

        def chip(p):
            px, py = _flip(x, _OTHER_CHIPS[p][0]), _flip(y, _OTHER_CHIPS[p][1])
            return px, py, 2 * px + py

        def part(a, p, src_slot, dst_slot):
            px, py, _ = chip(p)
            return pltpu.make_async_remote_copy(
                src_ref=ins[a].at[src_slot], dst_ref=outs[a].at[dst_slot],
                send_sem=send_sem.at[a * nrel + p], recv_sem=recv_sem.at[a * nrel + p],
                device_id=(px, py, c), device_id_type=MESH)

        def device(q):
            dx, dy, dc = _OTHER_DEVICES[q]
            return _flip(x, dx), _flip(y, dy), _flip(c, dc)

        def rep_copy(q, slot):
            return pltpu.make_async_remote_copy(
                src_ref=rep_ref, dst_ref=rep_out.at[slot], send_sem=rep_send.at[q], recv_sem=rep_recv.at[q],
                device_id=device(q), device_id_type=MESH)

        own = [pltpu.make_async_copy(ins[a].at[j], outs[a].at[j], loc_sem.at[a]) for a in range(n)]
        own.append(pltpu.make_async_copy(rep_ref, rep_out.at[me], loc_sem.at[n]))
        for cp in own:
            cp.start()
        pairs = [(a, p) for a in range(n) for p in range(nrel)]
        for a, p in pairs:
            part(a, p, chip(p)[2], j).start()
        for q in range(ndev):
            rep_copy(q, me).start()
        for a, p in pairs:
            part(a, p, chip(p)[2], chip(p)[2]).wait_recv()
        for q in range(ndev):
            px, py, pc = device(q)
            rep_copy(q, 4 * px + 2 * py + pc).wait_recv()
        for a, p in pairs:
            part(a, p, chip(p)[2], j).wait_send()
        for q in range(ndev):
            rep_copy(q, me).wait_send()
        for cp in own:
            cp.wait()

    return pl.pallas_call(
        body, name="chip_exchange",
        in_specs=[_HBM] * (n + 1), out_specs=[_HBM] * (n + 1),
        out_shape=[jax.ShapeDtypeStruct(s.shape, s.dtype) for s in sums]
        + [jax.ShapeDtypeStruct((N_DEVICES,) + rep.shape, rep.dtype)],
        scratch_shapes=[pltpu.SemaphoreType.DMA((n + 1,)), pltpu.SemaphoreType.DMA((n * nrel,)),
                        pltpu.SemaphoreType.DMA((n * nrel,)), pltpu.SemaphoreType.DMA((ndev,)),
                        pltpu.SemaphoreType.DMA((ndev,))],
    )(*sums, rep)


def pair_swap(halves):
    n = len(halves)

    def body(*refs):
        ins, outs = refs[:n], refs[n:2 * n]
        send_sem, recv_sem = refs[2 * n:]
        x, y, c = _position()
        copies = [pltpu.make_async_remote_copy(
            src_ref=ins[a], dst_ref=outs[a], send_sem=send_sem.at[a], recv_sem=recv_sem.at[a],
            device_id=(x, y, 1 - c), device_id_type=MESH) for a in range(n)]
        for cp in copies:
            cp.start()
        for cp in copies:
            cp.wait()

    return pl.pallas_call(
        body, name="pair_swap",
        in_specs=[_HBM] * n, out_specs=[_HBM] * n,
        out_shape=[jax.ShapeDtypeStruct(h.shape, h.dtype) for h in halves],
        scratch_shapes=[pltpu.SemaphoreType.DMA((n,))] * 2,
    )(*halves)


def _row_tile(rows, cols, limit_bytes=1 << 20):
    best = None
    for t in range(SUBLANES, rows + 1, SUBLANES):
        if rows % t == 0 and t * cols * 4 <= limit_bytes:
            best = t
    return best or rows


def add_pair(g, kind, theirs, core, name):
    nc, rows, cols = theirs.shape
    t = _row_tile(rows, cols)
    nt = rows // t

    def body(core_ref, g_ref, b_ref, o_ref):
        mine = g_ref[...].reshape(t, cols)
        o_ref[0] = (mine.astype(F32) + b_ref[0].astype(F32)).astype(o_ref.dtype)

    if kind == "col":
        own = pl.BlockSpec((t, cols), lambda k, i, c: (c[0] * nt + i, k))
    elif kind == "row":
        own = pl.BlockSpec((t, cols), lambda k, i, c: ((2 * k + c[0]) * nt + i, 0))
    else:
        own = pl.BlockSpec((1, t, cols), lambda k, i, c: (k, c[0] * nt + i, 0))
    spec = pl.BlockSpec((1, t, cols), lambda k, i, c: (k, i, 0))
    return pl.pallas_call(
        body, name=name,
        grid_spec=pltpu.PrefetchScalarGridSpec(num_scalar_prefetch=1, grid=(nc, nt), in_specs=[own, spec], out_specs=spec),
        out_shape=jax.ShapeDtypeStruct(theirs.shape, theirs.dtype), compiler_params=_params(),
    )(core, g, theirs)


def sum_lead(a, name):
    nl, rows, cols = a.shape
    t = _row_tile(rows, cols, (1 << 20) // 2)

    def body(a_ref, o_ref):
        acc = a_ref[0].astype(F32)
        for s in range(1, nl):
            acc = acc + a_ref[s].astype(F32)
        o_ref[...] = acc

    return pl.pallas_call(
        body, name=name, grid=(rows // t,),
        in_specs=[pl.BlockSpec((nl, t, cols), lambda i: (0, i, 0))],
        out_specs=pl.BlockSpec((t, cols), lambda i: (i, 0)),
        out_shape=jax.ShapeDtypeStruct((rows, cols), F32), compiler_params=_params(),
    )(a)


def sum_chips(rx, csum, chip, name):
    nc, rows, cols = rx.shape
    t = _row_tile(rows, cols, (1 << 20) // 2)

    def body(chip_ref, r0, r1, r2, r3, own_ref, o_ref):
        acc = None
        for s, ref in enumerate((r0, r1, r2, r3)):
            term = jnp.where(chip_ref[0] == s, own_ref[0], ref[0]).astype(F32)
            acc = term if acc is None else acc + term
        o_ref[...] = acc

    def slot(s):
        return pl.BlockSpec((1, t, cols), lambda i, c, s=s: (jnp.where(c[0] == s, c[0] ^ 1, s), i, 0))

    return pl.pallas_call(
        body, name=name,
        grid_spec=pltpu.PrefetchScalarGridSpec(
            num_scalar_prefetch=1, grid=(rows // t,),
            in_specs=[slot(s) for s in range(nc)] + [pl.BlockSpec((1, t, cols), lambda i, c: (c[0], i, 0))],
            out_specs=pl.BlockSpec((t, cols), lambda i, c: (i, 0))),
        out_shape=jax.ShapeDtypeStruct((rows, cols), F32), compiler_params=_params(),
    )(chip, rx, rx, rx, rx, csum)


def _adamw_update(w, g, m, v):
    nm = ADAM_B1 * m + (1.0 - ADAM_B1) * g
    nv = ADAM_B2 * v + (1.0 - ADAM_B2) * (g * g)
    m_hat = nm * (1.0 / (1.0 - ADAM_B1 ** ADAM_STEP))
    v_hat = nv * (1.0 / (1.0 - ADAM_B2 ** ADAM_STEP))
    return -ADAM_LR * (m_hat / (jnp.sqrt(v_hat) + ADAM_EPS) + ADAM_WD * w), nm, nv


def adamw(w, g, m, v, name):
    rows, cols = w.shape
    t = _row_tile(rows, cols)

    def body(w_ref, g_ref, m_ref, v_ref, d_ref, nm_ref, nv_ref):
        d_ref[...], nm_ref[...], nv_ref[...] = _adamw_update(w_ref[...], g_ref[...], m_ref[...], v_ref[...])

    spec = pl.BlockSpec((t, cols), lambda i: (i, 0))
    shp = jax.ShapeDtypeStruct((rows, cols), F32)
    return pl.pallas_call(
        body, name=name, grid=(rows // t,), in_specs=[spec] * 4, out_specs=[spec] * 3,
        out_shape=[shp, shp, shp], compiler_params=_params(),
    )(w, g, m, v)


def adamw_halves(w, g_mine, g_other, m, v, core, name):
    rows, cols = w.shape
    hr = rows // 2
    t = _row_tile(hr, cols)
    nt = hr // t

    def body(core_ref, w_ref, gm_ref, go_ref, m_ref, v_ref, g_ref, d_ref, nm_ref, nv_ref):
        g = jnp.where(pl.program_id(0) // nt == core_ref[0], gm_ref[...], go_ref[...])
        g_ref[...] = g
        d_ref[...], nm_ref[...], nv_ref[...] = _adamw_update(w_ref[...], g, m_ref[...], v_ref[...])

    spec = pl.BlockSpec((t, cols), lambda i, c: (i, 0))
    half = pl.BlockSpec((t, cols), lambda i, c: (i % nt, 0))
    shp = jax.ShapeDtypeStruct((rows, cols), F32)
    return pl.pallas_call(
        body, name=name,
        grid_spec=pltpu.PrefetchScalarGridSpec(num_scalar_prefetch=1, grid=(2 * nt,),
                                               in_specs=[spec, half, half, spec, spec], out_specs=[spec] * 4),
        out_shape=[shp] * 4, compiler_params=_params(),
    )(core, w, g_mine, g_other, m, v)


WEIGHTS = ("norm_mix_pre", "norm_mix_post", "norm_ffn_pre", "norm_ffn_post", "w_in", "conv_short_w",
           "w_conv_branch", "lru_conv_w", "lru_conv_b", "lru_wa", "lru_ba", "lru_wx", "lru_bx", "lru_lambda",
           "w_lru_branch", "w_out", "ffn_w_up", "ffn_conv_w", "ffn_conv_b", "ffn_w_down")
BIG = ("w_in", "ffn_w_up", "w_conv_branch", "w_lru_branch", "w_out", "ffn_w_down")
BIG_KIND = ("col", "col", "row", "row", "row", "row")
SMALL = ("conv_short_w", "lru_conv_w", "lru_wa", "lru_ba", "lru_wx", "lru_bx", "ffn_conv_w")
REPL = ("norm_mix_pre", "norm_mix_post", "norm_ffn_pre", "norm_ffn_post", "lru_conv_b", "lru_lambda", "ffn_conv_b")
PACK_W = 256
SMALL_ROWS = 544
REPL_ROWS = 16
FFN_SHARD = 2 * D_FF // N_CHIPS
QUARTER = HEAD_DIM // N_CHIPS


def _pack_small_shard(p):
    rows = [p["conv_short_w"].reshape(3, PACK_W), p["lru_conv_w"].reshape(4, PACK_W),
            p["lru_wa"].reshape(LRU_HEADS * QUARTER, PACK_W), p["lru_ba"].reshape(1, PACK_W),
            p["lru_wx"].reshape(LRU_HEADS * QUARTER, PACK_W), p["lru_bx"].reshape(1, PACK_W),
            p["ffn_conv_w"].reshape(3 * FFN_SHARD // PACK_W, PACK_W)]
    used = sum(r.shape[0] for r in rows)
    return jnp.concatenate(rows + [jnp.zeros((SMALL_ROWS - used, PACK_W), F32)], axis=0)


def _unpack_small_shard(buf):
    out, r = {}, 0
    for name, nr, shape in (("conv_short_w", 3, (1, 3, PACK_W)), ("lru_conv_w", 4, (1, 4, PACK_W)),
                            ("lru_wa", LRU_HEADS * QUARTER, (1, LRU_HEADS, QUARTER, HEAD_DIM)),
                            ("lru_ba", 1, (1, LRU_HEADS, QUARTER)),
                            ("lru_wx", LRU_HEADS * QUARTER, (1, LRU_HEADS, QUARTER, HEAD_DIM)),
                            ("lru_bx", 1, (1, LRU_HEADS, QUARTER)),
                            ("ffn_conv_w", 3 * FFN_SHARD // PACK_W, (1, 3, FFN_SHARD))):
        out[name] = buf[r:r + nr].reshape(shape)
        r += nr
    return out


def _full_small(g4):
    per = [_unpack_small_shard(g4[k]) for k in range(N_CHIPS)]
    cat = lambda name, axis: jnp.concatenate([per[k][name][0] for k in range(N_CHIPS)], axis=axis)
    return dict(conv_short_w=cat("conv_short_w", 1), lru_conv_w=cat("lru_conv_w", 1),
                lru_wa=cat("lru_wa", 1), lru_ba=cat("lru_ba", 1).reshape(1, D_MODEL),
                lru_wx=cat("lru_wx", 1), lru_bx=cat("lru_bx", 1).reshape(1, D_MODEL),
                ffn_conv_w=cat("ffn_conv_w", 1))


def _split_small(full):
    shards = []
    for k in range(N_CHIPS):
        cols = lambda a, w: a[:, k * w:(k + 1) * w]
        q = slice(k * QUARTER, (k + 1) * QUARTER)
        shards.append(_pack_small_shard(dict(
            conv_short_w=cols(full["conv_short_w"], PACK_W), lru_conv_w=cols(full["lru_conv_w"], PACK_W),
            lru_wa=full["lru_wa"][:, q, :], lru_ba=full["lru_ba"].reshape(LRU_HEADS, HEAD_DIM)[:, q],
            lru_wx=full["lru_wx"][:, q, :], lru_bx=full["lru_bx"].reshape(LRU_HEADS, HEAD_DIM)[:, q],
            ffn_conv_w=cols(full["ffn_conv_w"], FFN_SHARD))))
    return jnp.stack(shards)


def _pack_repl(p):
    rows = [p[n].reshape(-1, D_MODEL) for n in REPL]
    used = sum(r.shape[0] for r in rows)
    return jnp.concatenate(rows + [jnp.zeros((REPL_ROWS - used, D_MODEL), F32)], axis=0)


def _unpack_repl(buf):
    out, r = {}, 0
    for n in REPL:
        nr = (2 * D_FF // D_MODEL) if n == "ffn_conv_b" else 1
        out[n] = buf[r:r + nr].reshape(1, nr * D_MODEL)
        r += nr
    return out


def kernel(x, norm_mix_pre, norm_mix_post, norm_ffn_pre, norm_ffn_post, w_in, conv_short_w, w_conv_branch, lru_conv_w, lru_conv_b, lru_wa, lru_ba, lru_wx, lru_bx, lru_lambda, w_lru_branch, w_out, ffn_w_up, ffn_conv_w, ffn_conv_b, ffn_w_down, loss_target, m_norm_mix_pre, m_norm_mix_post, m_norm_ffn_pre, m_norm_ffn_post, m_w_in, m_conv_short_w, m_w_conv_branch, m_lru_conv_w, m_lru_conv_b, m_lru_wa, m_lru_ba, m_lru_wx, m_lru_bx, m_lru_lambda, m_w_lru_branch, m_w_out, m_ffn_w_up, m_ffn_conv_w, m_ffn_conv_b, m_ffn_w_down, v_norm_mix_pre, v_norm_mix_post, v_norm_ffn_pre, v_norm_ffn_post, v_w_in, v_conv_short_w, v_w_conv_branch, v_lru_conv_w, v_lru_conv_b, v_lru_wa, v_lru_ba, v_lru_wx, v_lru_bx, v_lru_lambda, v_w_lru_branch, v_w_out, v_ffn_w_up, v_ffn_conv_w, v_ffn_conv_b, v_ffn_w_down):
    given = dict(locals())
    w = {n: given[n] for n in WEIGHTS}
    m = {n: given["m_" + n] for n in WEIGHTS}
    v = {n: given["v_" + n] for n in WEIGHTS}

    xi, yi, ci = _position()
    chip_i = 2 * xi + yi
    chip = chip_i.astype(jnp.int32).reshape(1)
    core = ci.astype(jnp.int32).reshape(1)
    xs, target = x[0], loss_target[0]
    g1, g2, g3, g4 = w["norm_mix_pre"], w["norm_mix_post"], w["norm_ffn_pre"], w["norm_ffn_post"]
    shard = {n: w[n][0].astype(BF16) for n in BIG}
    small_shard = _pack_small_shard(w)

    def gathered(bufs, names):
        return [_own_slot(b, small_shard if n == "small" else shard[n], chip_i) for b, n in zip(bufs, names)]

    def chip_sums(arrays, kinds, tag):
        theirs = pair_split(arrays, kinds, "pair_split_" + tag)
        return [add_pair(g, k, t, core, "pair_add_%s_%d" % (tag, i)) for i, (g, k, t) in enumerate(zip(arrays, kinds, theirs))]

    h1 = norm_in(xs, g1)
    win4, small4 = gathered(run_ride(gather_ride([shard["w_in"], small_shard]), "gather_first"), ("w_in", "small"))
    small = _full_small(small4)
    (proj,), got = matmul_cols(h1, win4, "proj_fwd",
                               ride=gather_ride([shard["w_conv_branch"], shard["w_lru_branch"], shard["w_out"]]))
    wcb, wlb, wout = [g.reshape(-1, D_MODEL) for g in gathered(got, ("w_conv_branch", "w_lru_branch", "w_out"))]
    up_piece = lambda r0, nr, into=None: gather_ride([shard["ffn_w_up"]], items=[(0, r0, nr)], into=into)
    down_piece = lambda r0, nr, into=None: gather_ride([shard["ffn_w_down"]], items=[(0, r0, nr)], into=into)
    (q, ya), got = mix_conv_fwd(proj, small["conv_short_w"], ride=up_piece(0, 160))
    (xl, r, gi, h, yb), got = mix_lru_fwd(
        proj, small["lru_conv_w"], w["lru_conv_b"], small["lru_wa"].astype(BF16), small["lru_ba"],
        small["lru_wx"].astype(BF16), small["lru_bx"], w["lru_lambda"], ride=up_piece(160, 512, got))
    (a, b, merged), got = branch_merge_fwd(ya, yb, wcb, wlb, proj, ride=up_piece(672, 352, got))
    (wup4,) = gathered(got, ("ffn_w_up",))
    (mix, x2, h2), got = mix_out_fwd(merged, wout, xs, g2, g3, ride=down_piece(0, 256))
    (up,), got = matmul_cols(h2, wup4, "up_fwd", ride=down_piece(256, 512, got))
    wdown = gathered(got, ("ffn_w_down",))[0].reshape(-1, D_MODEL)
    f = ffn_act_fwd(up, small["ffn_conv_w"], w["ffn_conv_b"])
    dy, dout, loss, dg4 = ffn_down_loss(f, wdown, x2, target, g4)

    dug, duv, dwdown, dfw_g, dfw_v, dfb_g, dfb_v = ffn_bwd(dout, wdown, up, small["ffn_conv_w"], w["ffn_conv_b"])
    dup = jnp.concatenate([dug, duv], axis=1)
    cs_down = chip_sums([dwdown], ["row"], "down")
    (dh2, dwup), rx_down = dgrad_wgrad_cols(dup, wup4, h2, "up_bwd", ride=exchange_ride(cs_down))
    cs_up = chip_sums([dwup], ["col"], "up")
    up_rows = lambda r0, nr, into=None: exchange_ride(cs_up, items=[(0, r0, nr)], into=into)
    (dx2, dmix, dg3, dg2), rx_up = norms_mid_bwd(dh2, x2, dy, mix, g3, g2, ride=up_rows(0, 96))
    (da, db, dwout, dgc, dgl), rx_up = mix_out_bwd(dmix, wout, merged, a, b, proj, ride=up_rows(96, 160, rx_up))
    (dcb, dcc, dcx, dwcb, dws), rx_up = mix_conv_bwd(da, wcb, proj, q, small["conv_short_w"],
                                                     ride=up_rows(256, 160, rx_up))
    cs_mid = chip_sums([dwout, dwcb], ["row", "row"], "mid")
    (dlx, dly, dwlb, dwa, dwx, dba, dbx, dwl, dbl, dlam), rx_mid = mix_lru_bwd(
        db, wlb, proj, xl, r, gi, h, small["lru_conv_w"], small["lru_wa"].astype(BF16), small["lru_wx"].astype(BF16),
        w["lru_lambda"], ride=exchange_ride(cs_up + cs_mid, items=[(0, 416, 96), (1, 0, 128), (2, 0, 128)],
                                            into=rx_up + [None, None]))
    rx_up, rx_mid = rx_mid[:1], rx_mid[1:]
    grads = dict(norm_mix_post=dg2, norm_ffn_pre=dg3, norm_ffn_post=dg4, conv_short_w=dws, lru_conv_w=dwl,
                 lru_conv_b=dbl, lru_wa=dwa, lru_ba=dba, lru_wx=dwx, lru_bx=dbx, lru_lambda=dlam,
                 ffn_conv_w=jnp.concatenate([dfw_g, dfw_v], axis=1), ffn_conv_b=jnp.concatenate([dfb_g, dfb_v], axis=1))
    cs_late = chip_sums([dwlb, _split_small(grads)], ["row", "lead"], "late")
    dproj = jnp.concatenate([dcb, dcc, dcx, dlx, dly, dgc, dgl], axis=1)
    (dh1, dwin), rx_late = dgrad_wgrad_cols(dproj, win4, h1, "proj_bwd", ride=exchange_ride(cs_late))
    dx, grads["norm_mix_pre"] = norm_in_bwd(dh1, xs, dx2, g1)
    cs_in = chip_sums([dwin], ["col"], "in")
    rx_in, rep_all = run_ride(exchange_ride(cs_in, rep=_pack_repl(grads)), "exchange_last")

    order = (("w_in", cs_in[0], rx_in), ("ffn_w_up", cs_up[0], rx_up[0]), ("w_conv_branch", cs_mid[1], rx_mid[1]),
             ("w_lru_branch", cs_late[0], rx_late[0]), ("w_out", cs_mid[0], rx_mid[0]),
             ("ffn_w_down", cs_down[0], rx_down[0]), ("small", cs_late[1], rx_late[1]))
    halves = [sum_chips(rx, cs, chip, "chip_sum_" + n) for n, cs, rx in order]
    me = 4 * xi + 2 * yi + ci
    rep_grad = sum_lead(_own_slot(rep_all, _pack_repl(grads), me), "device_sum")
    others = pair_swap(halves)

    g_out, d_out, m_out, v_out = {}, {}, {}, {}
    for n, gm, go in zip(BIG, halves[:-1], others[:-1]):
        g, d, nm, nv = adamw_halves(w[n][0], gm, go, m[n][0], v[n][0], core, "adamw_" + n)
        g_out[n], d_out[n], m_out[n], v_out[n] = g[None], d[None], nm[None], nv[None]
    bufs = adamw_halves(small_shard, halves[-1], others[-1], _pack_small_shard(m), _pack_small_shard(v),
                        core, "adamw_small")
    for dst, buf in zip((g_out, d_out, m_out, v_out), bufs):
        dst.update(_unpack_small_shard(buf))
    d, nm, nv = adamw(_pack_repl(w), rep_grad, _pack_repl(m), _pack_repl(v), "adamw_repl")
    for dst, buf in ((g_out, rep_grad), (d_out, d), (m_out, nm), (v_out, nv)):
        dst.update(_unpack_repl(buf))

    total = lax.psum(loss[0, 0], ("x", "y", "c"))
    return (total, dx[None], *[g_out[n] for n in WEIGHTS], *[d_out[n] for n in WEIGHTS],
            *[m_out[n] for n in WEIGHTS], *[v_out[n] for n in WEIGHTS])
```

```python
import functools
import math

import jax
import jax.numpy as jnp
from jax import lax
from jax.experimental import pallas as pl
from jax.experimental.pallas import tpu as pltpu

F32 = jnp.float32
BF16 = jnp.bfloat16

D_MODEL = 1024
N_CHIPS = 4
N_SEG = 7
D_FF = 3 * D_MODEL
LRU_HEADS = 4
HEAD_DIM = D_MODEL // LRU_HEADS
LRU_C = 8.0
RMS_EPS = 1e-6
CW = 256
FW = 512
SUBLANES = 8
VMEM_LIMIT = 60 * 1024 * 1024

ADAM_LR = 0.001
ADAM_B1 = 0.9
ADAM_B2 = 0.999
ADAM_EPS = 1e-08
ADAM_WD = 0.01
ADAM_STEP = 10

_GELU_C = math.sqrt(2.0 / math.pi)
_GELU_K = 0.044715


def _params(**kw):
    return pltpu.CompilerParams(vmem_limit_bytes=VMEM_LIMIT, **kw)


def _sigmoid(x):
    return 1.0 / (1.0 + jnp.exp(-x))


def _gelu(x):
    t = jnp.tanh(_GELU_C * (x + _GELU_K * x * x * x))
    return 0.5 * x * (1.0 + t)


def _gelu_and_grad(x):
    x2 = x * x
    t = jnp.tanh(_GELU_C * (x + _GELU_K * x * x2))
    g = 0.5 * x * (1.0 + t)
    dg = 0.5 * (1.0 + t) + 0.5 * x * (1.0 - t * t) * _GELU_C * (1.0 + 3.0 * _GELU_K * x2)
    return g, dg


def _log_sigmoid(x):
    e = jnp.exp(-jnp.abs(x))
    u = 1.0 + e
    l1p = jnp.where(u == 1.0, e, jnp.log(u) * e / (u - 1.0))
    return jnp.minimum(x, 0.0) - l1p


def _neg_expm1(z):
    series = -z * (1.0 + z * (0.5 + z * (1.0 / 6.0 + z * (1.0 / 24.0 + z * (1.0 / 120.0 + z * (1.0 / 720.0))))))
    return jnp.where(z > -0.2, series, 1.0 - jnp.exp(z))


def _rows(shape):
    return lax.broadcasted_iota(jnp.int32, shape, 0)


def _shift_down(x, k):
    return jnp.where(_rows(x.shape) >= k, pltpu.roll(x, k, 0), 0.0)


def _shift_up(x, k):
    n = x.shape[0]
    return jnp.where(_rows(x.shape) < n - k, pltpu.roll(x, n - k, 0), 0.0)


def _delays(x, k_width):
    return [x] + [_shift_down(x, j) for j in range(1, k_width)]


def _advances(dy, k_width):
    return [dy] + [_shift_up(dy, j) for j in range(1, k_width)]


def _taps_sum(shifted, w_ref, b=None):
    k_width = w_ref.shape[0]
    y = w_ref[k_width - 1:k_width, :] * shifted[0]
    for j in range(1, k_width):
        y = y + w_ref[k_width - 1 - j:k_width - j, :] * shifted[j]
    if b is not None:
        y = y + b
    return y


def _causal_conv(x, w_ref, b=None):
    return _taps_sum(_delays(x, w_ref.shape[0]), w_ref, b)


def _conv_wgrad(advanced, x):
    k_width = len(advanced)
    rows = [jnp.sum(advanced[k_width - 1 - k] * x, axis=0, keepdims=True) for k in range(k_width)]
    return jnp.concatenate(rows, axis=0)


def _dot(a, b):
    return jnp.dot(a, b, preferred_element_type=F32)


def _dot_nt(a, b):
    return lax.dot_general(a, b, (((1,), (1,)), ((), ())), preferred_element_type=F32)


def _dot_tn(a, b):
    return lax.dot_general(a, b, (((0,), (0,)), ((), ())), preferred_element_type=F32)


def _rms_stats(x):
    r = lax.rsqrt(jnp.mean(x * x, axis=-1, keepdims=True) + RMS_EPS)
    return x * r, r


def _rms_bwd(n, r, g, dy):
    dn = dy * g
    dx = r * (dn - n * jnp.mean(dn * n, axis=-1, keepdims=True))
    return dx, dy * n


def _scan_forward(a_ref, b_ref, h_ref):
    n, c = a_ref.shape
    row = lax.broadcasted_iota(jnp.int32, (SUBLANES, c), 0)

    def group(g, carry):
        r0 = pl.multiple_of(g * SUBLANES, SUBLANES)
        a = a_ref[pl.ds(r0, SUBLANES), :]
        b = b_ref[pl.ds(r0, SUBLANES), :]
        for k in (1, 2, 4):
            ap = jnp.where(row >= k, pltpu.roll(a, k, 0), 1.0)
            bp = jnp.where(row >= k, pltpu.roll(b, k, 0), 0.0)
            b = a * bp + b
            a = a * ap
        h = a * carry + b
        h_ref[pl.ds(r0, SUBLANES), :] = h
        return h[SUBLANES - 1:SUBLANES, :]

    lax.fori_loop(0, n // SUBLANES, group, jnp.zeros((1, c), F32))


def _scan_backward(c_ref, b_ref, g_ref):
    n, ch = c_ref.shape
    row = lax.broadcasted_iota(jnp.int32, (SUBLANES, ch), 0)
    n_groups = n // SUBLANES

    def group(i, carry):
        r0 = pl.multiple_of((n_groups - 1 - i) * SUBLANES, SUBLANES)
        a = c_ref[pl.ds(r0, SUBLANES), :]
        b = b_ref[pl.ds(r0, SUBLANES), :]
        for k in (1, 2, 4):
            keep = row < SUBLANES - k
            ap = jnp.where(keep, pltpu.roll(a, SUBLANES - k, 0), 1.0)
            bp = jnp.where(keep, pltpu.roll(b, SUBLANES - k, 0), 0.0)
            b = a * bp + b
            a = a * ap
        g = a * carry + b
        g_ref[pl.ds(r0, SUBLANES), :] = g
        return g[0:1, :]

    lax.fori_loop(0, n_groups, group, jnp.zeros((1, ch), F32))


MESH = pl.DeviceIdType.MESH
_HBM = pl.BlockSpec(memory_space=pltpu.HBM)
_OTHER_CHIPS = ((1, 0), (0, 1), (1, 1))
_OTHER_DEVICES = tuple((dx, dy, dc) for dx in (0, 1) for dy in (0, 1) for dc in (0, 1) if dx or dy or dc)
N_DEVICES = 8


def _position():
    return lax.axis_index("x"), lax.axis_index("y"), lax.axis_index("c")


def _flip(v, d):
    return 1 - v if d else v


def _chip(x, y, p):
    px, py = _flip(x, _OTHER_CHIPS[p][0]), _flip(y, _OTHER_CHIPS[p][1])
    return px, py, 2 * px + py


class _Ride:
    def __init__(self, srcs, bufs, scratch, plan):
        self.srcs, self.bufs, self.scratch, self.plan = list(srcs), list(bufs), list(scratch), plan


def _call(body, *, name, grid, in_specs, out_specs, out_shape, operands, scratch_shapes=(), ride=None):
    in_specs, out_specs, out_shape = list(in_specs), list(out_specs), list(out_shape)
    scratch_shapes = list(scratch_shapes)
    if ride is None:
        return pl.pallas_call(body, name=name, grid=grid, in_specs=in_specs, out_specs=out_specs, out_shape=out_shape,
                              scratch_shapes=scratch_shapes, compiler_params=_params())(*operands)
    n_in, n_out, n_scr = len(in_specs), len(out_shape), len(scratch_shapes)
    old = [i for i, b in enumerate(ride.bufs) if not isinstance(b, jax.ShapeDtypeStruct)]
    n_src, n_old, n_buf = len(ride.srcs), len(old), len(ride.bufs)

    def full_body(*refs):
        o0 = n_in + n_src + n_old
        s0 = o0 + n_out + n_buf
        start, finish = ride.plan(refs[n_in:n_in + n_src], refs[o0 + n_out:s0], refs[s0 + n_scr:])
        ids = [pl.program_id(i) for i in range(len(grid))]
        first = functools.reduce(jnp.logical_and, [i == 0 for i in ids])
        last = functools.reduce(jnp.logical_and, [i == g - 1 for i, g in zip(ids, grid)])
        pl.when(first)(start)
        body(*refs[:n_in], *refs[o0:o0 + n_out], *refs[s0:s0 + n_scr])
        pl.when(last)(finish)

    shapes = [jax.ShapeDtypeStruct(b.shape, b.dtype) for b in ride.bufs]
    res = pl.pallas_call(
        full_body, name=name, grid=grid,
        in_specs=in_specs + [_HBM] * (n_src + n_old), out_specs=out_specs + [_HBM] * n_buf,
        out_shape=out_shape + shapes, scratch_shapes=scratch_shapes + ride.scratch,
        input_output_aliases={n_in + n_src + k: n_out + i for k, i in enumerate(old)},
        compiler_params=_params(),
    )(*operands, *ride.srcs, *[ride.bufs[i] for i in old])
    return list(res[:n_out]), list(res[n_out:])


def run_ride(ride, name):
    def body():
        pass

    return _call(body, name=name, grid=(1,), in_specs=[], out_specs=[], out_shape=[], operands=[], ride=ride)[1]


def gather_ride(shards, items=None, into=None):
    items = items or [(a, 0, s.shape[0]) for a, s in enumerate(shards)]
    bufs = into or [jax.ShapeDtypeStruct((N_CHIPS,) + s.shape, s.dtype) for s in shards]
    nrel = len(_OTHER_CHIPS)

    def plan(srcs, dsts, sems):
        ici_send, ici_recv, sib_send, sib_recv = sems
        x, y, c = _position()
        j = 2 * x + y

        def rows(ref, it, h):
            return ref.at[pl.ds(it[1] + h * (it[2] // 2), it[2] // 2), :]

        def ici(i, p, slot):
            it = items[i]
            px, py, _ = _chip(x, y, p)
            return pltpu.make_async_remote_copy(
                src_ref=rows(srcs[it[0]], it, c), dst_ref=rows(dsts[it[0]].at[slot], it, c),
                send_sem=ici_send.at[i * nrel + p], recv_sem=ici_recv.at[i * nrel + p],
                device_id=(px, py, c), device_id_type=MESH)

        def sib(i, p, h):
            it = items[i]
            part = rows(dsts[it[0]].at[_chip(x, y, p)[2]], it, h)
            return pltpu.make_async_remote_copy(
                src_ref=part, dst_ref=part, send_sem=sib_send.at[i * nrel + p], recv_sem=sib_recv.at[i * nrel + p],
                device_id=(x, y, 1 - c), device_id_type=MESH)

        pairs = [(i, p) for i in range(len(items)) for p in range(nrel)]

        def start():
            for i, p in pairs:
                ici(i, p, j).start()

        def finish():
            for i, p in pairs:
                ici(i, p, _chip(x, y, p)[2]).wait_recv()
                sib(i, p, c).start()
            for i, p in pairs:
                sib(i, p, 1 - c).wait_recv()
            for i, p in pairs:
                ici(i, p, j).wait_send()
                sib(i, p, c).wait_send()

        return start, finish

    return _Ride(shards, bufs, [pltpu.SemaphoreType.DMA((len(items) * nrel,))] * 4, plan)


def exchange_ride(sums, items=None, into=None, rep=None):
    items = items or [(a, 0, s.shape[1]) for a, s in enumerate(sums)]
    into = into or [None] * len(sums)
    bufs = [jax.ShapeDtypeStruct(s.shape, s.dtype) if b is None else b for s, b in zip(sums, into)]
    srcs = list(sums)
    scratch = [pltpu.SemaphoreType.DMA((len(items) * len(_OTHER_CHIPS),))] * 2
    if rep is not None:
        srcs.append(rep)
        bufs.append(jax.ShapeDtypeStruct((N_DEVICES,) + rep.shape, rep.dtype))
        scratch += [pltpu.SemaphoreType.DMA((len(_OTHER_DEVICES),))] * 2
    nrel = len(_OTHER_CHIPS)

    def plan(src_refs, dst_refs, sems):
        x, y, c = _position()
        j = 2 * x + y
        me = 4 * x + 2 * y + c

        def part(i, p, src_slot, dst_slot):
            a, r0, nr = items[i]
            px, py, _ = _chip(x, y, p)
            return pltpu.make_async_remote_copy(
                src_ref=src_refs[a].at[src_slot, pl.ds(r0, nr), :], dst_ref=dst_refs[a].at[dst_slot, pl.ds(r0, nr), :],
                send_sem=sems[0].at[i * nrel + p], recv_sem=sems[1].at[i * nrel + p],
                device_id=(px, py, c), device_id_type=MESH)

        def device(q):
            dx, dy, dc = _OTHER_DEVICES[q]
            return _flip(x, dx), _flip(y, dy), _flip(c, dc)

        def rep_copy(q, slot):
            return pltpu.make_async_remote_copy(
                src_ref=src_refs[-1], dst_ref=dst_refs[-1].at[slot], send_sem=sems[2].at[q], recv_sem=sems[3].at[q],
                device_id=device(q), device_id_type=MESH)

        pairs = [(i, p) for i in range(len(items)) for p in range(nrel)]
        others = range(len(_OTHER_DEVICES)) if rep is not None else ()

        def start():
            for i, p in pairs:
                part(i, p, _chip(x, y, p)[2], j).start()
            for q in others:
                rep_copy(q, me).start()

        def finish():
            for i, p in pairs:
                k = _chip(x, y, p)[2]
                part(i, p, k, k).wait_recv()
            for q in others:
                px, py, pc = device(q)
                rep_copy(q, 4 * px + 2 * py + pc).wait_recv()
            for i, p in pairs:
                part(i, p, _chip(x, y, p)[2], j).wait_send()
            for q in others:
                rep_copy(q, me).wait_send()

        return start, finish

    return _Ride(srcs, bufs, scratch, plan)


def _own_slot(buf, own, index):
    return lax.dynamic_update_slice(buf, own[None], (index,) + (0,) * own.ndim)


def _token_tile(s):
    return min(s, 512)


def norm_in(x, g):
    s, d = x.shape
    t = _token_tile(s)

    def body(x_ref, g_ref, o_ref):
        n, _ = _rms_stats(x_ref[...])
        o_ref[...] = (n * g_ref[...]).astype(BF16)

    return pl.pallas_call(
        body, name="norm_in", grid=(s // t,),
        in_specs=[pl.BlockSpec((t, d), lambda i: (i, 0)), pl.BlockSpec((1, d), lambda i: (0, 0))],
        out_specs=pl.BlockSpec((t, d), lambda i: (i, 0)),
        out_shape=jax.ShapeDtypeStruct((s, d), BF16),
        compiler_params=_params(),
    )(x, g)


def matmul_cols(a, w4, name, ride=None):
    m, k = a.shape
    nj, _, ns = w4.shape
    nb = ns // CW

    def body(a_ref, w_ref, o_ref):
        o_ref[...] = _dot(a_ref[...], w_ref[0])

    return _call(
        body, name=name, grid=(nj, nb),
        in_specs=[pl.BlockSpec((m, k), lambda j, b: (0, 0)),
                  pl.BlockSpec((1, k, CW), lambda j, b: (j, 0, b))],
        out_specs=[pl.BlockSpec((m, CW), lambda j, b: (0, j * nb + b))],
        out_shape=[jax.ShapeDtypeStruct((m, nj * ns), F32)],
        operands=(a, w4), ride=ride)


def mix_conv_fwd(proj, ws, ride=None):
    s = proj.shape[0]
    nblk = D_MODEL // CW

    def body(cb_ref, cc_ref, cx_ref, ws_ref, q_ref, ya_ref):
        q = _causal_conv(cc_ref[...] * cx_ref[...], ws_ref)
        q_ref[...] = q
        ya_ref[...] = (cb_ref[...] * q).astype(BF16)

    seg = lambda k: pl.BlockSpec((s, CW), lambda c, k=k: (0, k * nblk + c))
    return _call(
        body, name="mix_conv_fwd", grid=(nblk,),
        in_specs=[seg(0), seg(1), seg(2), pl.BlockSpec((3, CW), lambda c: (0, c))],
        out_specs=[pl.BlockSpec((s, CW), lambda c: (0, c))] * 2,
        out_shape=[jax.ShapeDtypeStruct((s, D_MODEL), F32), jax.ShapeDtypeStruct((s, D_MODEL), BF16)],
        operands=(proj, proj, proj, ws), ride=ride)


def _lru_gates(r, ls):
    log_a = LRU_C * r * ls
    a = jnp.exp(log_a)
    mult = jnp.sqrt(_neg_expm1(2.0 * log_a))
    mult = jnp.where(_rows(r.shape) == 0, 1.0, mult)
    return a, mult


def mix_lru_fwd(proj, wl, bl, wa, ba, wx, bx, lam, ride=None):
    s = proj.shape[0]
    nblk = D_MODEL // CW

    def body(lx_ref, ly_ref, wl_ref, bl_ref, wa_ref, ba_ref, wx_ref, bx_ref, lam_ref,
             xl_ref, r_ref, i_ref, h_ref, yb_ref, a_scr, u_scr):
        xl = _causal_conv(lx_ref[...], wl_ref, bl_ref[...])
        xl_ref[...] = xl
        xlb = xl.astype(BF16)
        r = _sigmoid(_dot(xlb, wa_ref[0]) + ba_ref[...])
        i = _sigmoid(_dot(xlb, wx_ref[0]) + bx_ref[...])
        r_ref[...] = r
        i_ref[...] = i
        a, mult = _lru_gates(r, _log_sigmoid(lam_ref[...]))
        a_scr[...] = a
        u_scr[...] = mult * i * xl
        _scan_forward(a_scr, u_scr, h_ref)
        yb_ref[...] = (h_ref[...] * _gelu(ly_ref[...])).astype(BF16)

    blk = lambda k: pl.BlockSpec((s, CW), lambda c, k=k: (0, k * nblk + c))
    vec = pl.BlockSpec((1, CW), lambda c: (0, c))
    mat = pl.BlockSpec((1, CW, CW), lambda c: (c, 0, 0))
    out = pl.BlockSpec((s, CW), lambda c: (0, c))
    f = jax.ShapeDtypeStruct((s, D_MODEL), F32)
    return _call(
        body, name="mix_lru_fwd", grid=(nblk,),
        in_specs=[blk(3), blk(4), pl.BlockSpec((4, CW), lambda c: (0, c)), vec, mat, vec, mat, vec, vec],
        out_specs=[out] * 5,
        out_shape=[f, f, f, f, jax.ShapeDtypeStruct((s, D_MODEL), BF16)],
        scratch_shapes=[pltpu.VMEM((s, CW), F32), pltpu.VMEM((s, CW), F32)],
        operands=(proj, proj, wl, bl, wa, ba, wx, bx, lam), ride=ride)


def branch_merge_fwd(ya, yb, wcb, wlb, proj, ride=None):
    s = ya.shape[0]
    nblk = D_MODEL // CW

    def body(ya_ref, yb_ref, wcb_ref, wlb_ref, gc_ref, gl_ref, a_ref, b_ref, m_ref):
        a = _dot(ya_ref[...], wcb_ref[...])
        b = _dot(yb_ref[...], wlb_ref[...])
        a_ref[...] = a
        b_ref[...] = b
        m_ref[...] = (_sigmoid(gc_ref[...]) * a + _sigmoid(gl_ref[...]) * b).astype(BF16)

    res = pl.BlockSpec((s, D_MODEL), lambda n: (0, 0))
    wcol = pl.BlockSpec((D_MODEL, CW), lambda n: (0, n))
    blk = lambda k: pl.BlockSpec((s, CW), lambda n, k=k: (0, k * nblk + n))
    out = pl.BlockSpec((s, CW), lambda n: (0, n))
    f = jax.ShapeDtypeStruct((s, D_MODEL), F32)
    return _call(
        body, name="branch_merge_fwd", grid=(nblk,),
        in_specs=[res, res, wcol, wcol, blk(5), blk(6)],
        out_specs=[out] * 3,
        out_shape=[f, f, jax.ShapeDtypeStruct((s, D_MODEL), BF16)],
        operands=(ya, yb, wcb, wlb, proj, proj), ride=ride)


def mix_out_fwd(merged, wout, x, g2, g3, ride=None):
    s, d = x.shape
    t = _token_tile(s)

    def body(m_ref, w_ref, x_ref, g2_ref, g3_ref, mix_ref, x2_ref, h2_ref):
        mix = _dot(m_ref[...], w_ref[...])
        mix_ref[...] = mix
        n, _ = _rms_stats(mix)
        x2 = x_ref[...] + n * g2_ref[...]
        x2_ref[...] = x2
        n2, _ = _rms_stats(x2)
        h2_ref[...] = (n2 * g3_ref[...]).astype(BF16)

    tile = pl.BlockSpec((t, d), lambda i: (i, 0))
    vec = pl.BlockSpec((1, d), lambda i: (0, 0))
    f = jax.ShapeDtypeStruct((s, d), F32)
    return _call(
        body, name="mix_out_fwd", grid=(s // t,),
        in_specs=[tile, pl.BlockSpec((d, d), lambda i: (0, 0)), tile, vec, vec],
        out_specs=[tile] * 3,
        out_shape=[f, f, jax.ShapeDtypeStruct((s, d), BF16)],
        operands=(merged, wout, x, g2, g3), ride=ride)


def ffn_up_act_fwd(h2, wup4, fw, fb, ride=None):
    s, k = h2.shape
    ns = wup4.shape[2]
    per_chip = ns // CW
    nblk = D_FF // CW

    def body(h_ref, wg_ref, wv_ref, cg_ref, cv_ref, bg_ref, bv_ref, up_ref, f_ref):
        h = h_ref[...]
        ug = _dot(h, wg_ref[0])
        uv = _dot(h, wv_ref[0])
        up_ref[0] = ug
        up_ref[1] = uv
        gate = _causal_conv(ug, cg_ref, bg_ref[...])
        val = _causal_conv(uv, cv_ref, bv_ref[...])
        f_ref[...] = (_gelu(gate) * val).astype(BF16)

    wcols = lambda h: pl.BlockSpec((1, k, CW), lambda n, h=h: (n // per_chip + 2 * h, 0, n % per_chip))
    half = lambda h, rows: pl.BlockSpec((rows, CW), lambda n, h=h: (0, h * nblk + n))
    return _call(
        body, name="ffn_up_act_fwd", grid=(nblk,),
        in_specs=[pl.BlockSpec((s, k), lambda n: (0, 0)), wcols(0), wcols(1),
                  half(0, 3), half(1, 3), half(0, 1), half(1, 1)],
        out_specs=[pl.BlockSpec((2, s, CW), lambda n: (0, 0, n)), pl.BlockSpec((s, CW), lambda n: (0, n))],
        out_shape=[jax.ShapeDtypeStruct((2, s, D_FF), F32), jax.ShapeDtypeStruct((s, D_FF), BF16)],
        operands=(h2, wup4, wup4, fw, fw, fb, fb), ride=ride)


def ffn_down_loss(f, wdown, x2, target, g4):
    s, d = x2.shape
    t = _token_tile(s)

    def body(f_ref, w_ref, x2_ref, tg_ref, g4_ref, dy_ref, dout_ref, loss_ref, dg4_ref):
        @pl.when(pl.program_id(0) == 0)
        def _():
            loss_ref[...] = jnp.zeros_like(loss_ref)
            dg4_ref[...] = jnp.zeros_like(dg4_ref)

        out = _dot(f_ref[...], w_ref[...])
        n, r = _rms_stats(out)
        err = x2_ref[...] + n * g4_ref[...] - tg_ref[...]
        loss_ref[...] += jnp.full(loss_ref.shape, (0.5 / d) * jnp.sum(err * err), F32)
        dy = err * (1.0 / d)
        dy_ref[...] = dy
        dout, dg = _rms_bwd(n, r, g4_ref[...], dy)
        dout_ref[...] = dout.astype(BF16)
        dg4_ref[...] += jnp.sum(dg, axis=0, keepdims=True)

    tile = pl.BlockSpec((t, d), lambda i: (i, 0))
    vec = pl.BlockSpec((1, d), lambda i: (0, 0))
    return pl.pallas_call(
        body, name="ffn_down_loss", grid=(s // t,),
        in_specs=[pl.BlockSpec((t, D_FF), lambda i: (i, 0)), pl.BlockSpec((D_FF, d), lambda i: (0, 0)), tile, tile, vec],
        out_specs=[tile, tile, pl.BlockSpec((1, 128), lambda i: (0, 0)), vec],
        out_shape=[jax.ShapeDtypeStruct((s, d), F32), jax.ShapeDtypeStruct((s, d), BF16),
                   jax.ShapeDtypeStruct((1, 128), F32), jax.ShapeDtypeStruct((1, d), F32)],
        compiler_params=_params(),
    )(f, wdown, x2, target, g4)


def ffn_bwd(dout, wdown, up, fw, fb):
    s = up.shape[1]
    nblk = D_FF // FW

    def body(do_ref, wd_ref, up_ref, wg_ref, wv_ref, bg_ref, bv_ref, dup_ref, dwd_ref, dw_ref, db_ref):
        do = do_ref[...]
        df = _dot_nt(do, wd_ref[...])
        ug = up_ref[0]
        uv = up_ref[1]
        gate = _causal_conv(ug, wg_ref, bg_ref[...])
        val = _causal_conv(uv, wv_ref, bv_ref[...])
        ge, dge = _gelu_and_grad(gate)
        dwd_ref[...] = _dot_tn((ge * val).astype(BF16), do).astype(BF16)
        dgate = _advances(df * val * dge, 3)
        dval = _advances(df * ge, 3)
        dup_ref[0] = _taps_sum(dgate, wg_ref).astype(BF16)
        dup_ref[1] = _taps_sum(dval, wv_ref).astype(BF16)
        dw_ref[0] = _conv_wgrad(dgate, ug)
        dw_ref[1] = _conv_wgrad(dval, uv)
        db_ref[0] = jnp.sum(dgate[0], axis=0, keepdims=True)
        db_ref[1] = jnp.sum(dval[0], axis=0, keepdims=True)

    half = lambda h, rows: pl.BlockSpec((rows, FW), lambda n, h=h: (0, h * nblk + n))
    both = lambda rows: pl.BlockSpec((2, rows, FW), lambda n: (0, 0, n))
    return pl.pallas_call(
        body, name="ffn_bwd", grid=(nblk,),
        in_specs=[pl.BlockSpec((s, D_MODEL), lambda n: (0, 0)), pl.BlockSpec((FW, D_MODEL), lambda n: (n, 0)),
                  both(s), half(0, 3), half(1, 3), half(0, 1), half(1, 1)],
        out_specs=[both(s), pl.BlockSpec((FW, D_MODEL), lambda n: (n, 0)), both(3), both(1)],
        out_shape=[jax.ShapeDtypeStruct((2, s, D_FF), BF16), jax.ShapeDtypeStruct((D_FF, D_MODEL), BF16),
                   jax.ShapeDtypeStruct((2, 3, D_FF), F32), jax.ShapeDtypeStruct((2, 1, D_FF), F32)],
        compiler_params=_params(),
    )(dout, wdown, up, fw, fw, fb, fb)


def dgrad_wgrad_cols(dy, w4, a, name, ride=None):
    m, k = a.shape
    nj, _, ns = w4.shape
    nb = ns // CW

    def body(dy_ref, w_ref, a_ref, da_ref, dw_ref):
        @pl.when((pl.program_id(0) == 0) & (pl.program_id(1) == 0))
        def _():
            da_ref[...] = jnp.zeros_like(da_ref)

        dyb = dy_ref[...].reshape(m, CW)
        da_ref[...] += _dot_nt(dyb, w_ref[0])
        dw_ref[...] = _dot_tn(a_ref[...], dyb).astype(BF16)

    if dy.ndim == 3:
        dy_spec = pl.BlockSpec((1, m, CW), lambda j, b: (j // 2, 0, (j % 2) * nb + b))
    else:
        dy_spec = pl.BlockSpec((m, CW), lambda j, b: (0, j * nb + b))
    return _call(
        body, name=name, grid=(nj, nb),
        in_specs=[dy_spec,
                  pl.BlockSpec((1, k, CW), lambda j, b: (j, 0, b)),
                  pl.BlockSpec((m, k), lambda j, b: (0, 0))],
        out_specs=[pl.BlockSpec((m, k), lambda j, b: (0, 0)),
                   pl.BlockSpec((k, CW), lambda j, b: (0, j * nb + b))],
        out_shape=[jax.ShapeDtypeStruct((m, k), F32), jax.ShapeDtypeStruct((k, nj * ns), BF16)],
        operands=(dy, w4, a), ride=ride)


def norms_mid_bwd(dh2, x2, dy, mix, g3, g2, ride=None):
    s, d = x2.shape
    t = _token_tile(s)

    def body(dh2_ref, x2_ref, dy_ref, mix_ref, g3_ref, g2_ref, dx2_ref, dmix_ref, dg3_ref, dg2_ref):
        @pl.when(pl.program_id(0) == 0)
        def _():
            dg3_ref[...] = jnp.zeros_like(dg3_ref)
            dg2_ref[...] = jnp.zeros_like(dg2_ref)

        n3, r3 = _rms_stats(x2_ref[...])
        dx, dg3 = _rms_bwd(n3, r3, g3_ref[...], dh2_ref[...])
        dx2 = dy_ref[...] + dx
        dx2_ref[...] = dx2
        dg3_ref[...] += jnp.sum(dg3, axis=0, keepdims=True)
        n2, r2 = _rms_stats(mix_ref[...])
        dmix, dg2 = _rms_bwd(n2, r2, g2_ref[...], dx2)
        dmix_ref[...] = dmix.astype(BF16)
        dg2_ref[...] += jnp.sum(dg2, axis=0, keepdims=True)

    tile = pl.BlockSpec((t, d), lambda i: (i, 0))
    vec = pl.BlockSpec((1, d), lambda i: (0, 0))
    v = jax.ShapeDtypeStruct((1, d), F32)
    return _call(
        body, name="norms_mid_bwd", grid=(s // t,),
        in_specs=[tile, tile, tile, tile, vec, vec],
        out_specs=[tile, tile, vec, vec],
        out_shape=[jax.ShapeDtypeStruct((s, d), F32), jax.ShapeDtypeStruct((s, d), BF16), v, v],
        operands=(dh2, x2, dy, mix, g3, g2), ride=ride)


def mix_out_bwd(dmix, wout, merged, a, b, proj, ride=None):
    s = dmix.shape[0]
    nblk = D_MODEL // CW

    def body(dm_ref, w_ref, mg_ref, a_ref, b_ref, gc_ref, gl_ref, da_ref, db_ref, dw_ref, dgc_ref, dgl_ref):
        dm = dm_ref[...]
        dmerged = _dot_nt(dm, w_ref[...])
        dw_ref[...] = _dot_tn(mg_ref[...], dm).astype(BF16)
        sc = _sigmoid(gc_ref[...])
        sl = _sigmoid(gl_ref[...])
        da_ref[...] = (dmerged * sc).astype(BF16)
        db_ref[...] = (dmerged * sl).astype(BF16)
        dgc_ref[...] = (dmerged * a_ref[...] * sc * (1.0 - sc)).astype(BF16)
        dgl_ref[...] = (dmerged * b_ref[...] * sl * (1.0 - sl)).astype(BF16)

    res = pl.BlockSpec((s, D_MODEL), lambda n: (0, 0))
    rows = pl.BlockSpec((CW, D_MODEL), lambda n: (n, 0))
    col = pl.BlockSpec((s, CW), lambda n: (0, n))
    blk = lambda k: pl.BlockSpec((s, CW), lambda n, k=k: (0, k * nblk + n))
    hb = jax.ShapeDtypeStruct((s, D_MODEL), BF16)
    return _call(
        body, name="mix_out_bwd", grid=(nblk,),
        in_specs=[res, rows, col, col, col, blk(5), blk(6)],
        out_specs=[col, col, rows, col, col],
        out_shape=[hb, hb, jax.ShapeDtypeStruct((D_MODEL, D_MODEL), BF16), hb, hb],
        operands=(dmix, wout, merged, a, b, proj, proj), ride=ride)


def mix_conv_bwd(da, wcb, proj, q, ws, ride=None):
    s = da.shape[0]
    nblk = D_MODEL // CW

    def body(da_ref, w_ref, cb_ref, cc_ref, cx_ref, q_ref, ws_ref, dcb_ref, dcc_ref, dcx_ref, dw_ref, dws_ref):
        dab = da_ref[...]
        dya = _dot_nt(dab, w_ref[...])
        cb = cb_ref[...]
        cc = cc_ref[...]
        cx = cx_ref[...]
        q = q_ref[...]
        dw_ref[...] = _dot_tn((cb * q).astype(BF16), dab).astype(BF16)
        dcb_ref[...] = (dya * q).astype(BF16)
        dq = _advances(dya * cb, 3)
        dp = _taps_sum(dq, ws_ref)
        dws_ref[...] = _conv_wgrad(dq, cc * cx)
        dcc_ref[...] = (dp * cx).astype(BF16)
        dcx_ref[...] = (dp * cc).astype(BF16)

    res = pl.BlockSpec((s, D_MODEL), lambda n: (0, 0))
    rows = pl.BlockSpec((CW, D_MODEL), lambda n: (n, 0))
    col = pl.BlockSpec((s, CW), lambda n: (0, n))
    blk = lambda k: pl.BlockSpec((s, CW), lambda n, k=k: (0, k * nblk + n))
    taps = pl.BlockSpec((3, CW), lambda n: (0, n))
    hb = jax.ShapeDtypeStruct((s, D_MODEL), BF16)
    return _call(
        body, name="mix_conv_bwd", grid=(nblk,),
        in_specs=[res, rows, blk(0), blk(1), blk(2), col, taps],
        out_specs=[col, col, col, rows, taps],
        out_shape=[hb, hb, hb, jax.ShapeDtypeStruct((D_MODEL, D_MODEL), BF16), jax.ShapeDtypeStruct((3, D_MODEL), F32)],
        operands=(da, wcb, proj, proj, proj, q, ws), ride=ride)


def mix_lru_bwd(db, wlb, proj, xl, r, i, h, wl, wa, wx, lam, ride=None):
    s = db.shape[0]
    nblk = D_MODEL // CW

    def body(db_ref, w_ref, lx_ref, ly_ref, xl_ref, r_ref, i_ref, h_ref, wl_ref, wa_ref, wx_ref, lam_ref,
             dlx_ref, dly_ref, dw_ref, dwa_ref, dwx_ref, dba_ref, dbx_ref, dwl_ref, dbl_ref, dlam_ref,
             c_scr, dh_scr, g_scr):
        dbb = db_ref[...]
        dyb = _dot_nt(dbb, w_ref[...])
        h = h_ref[...]
        ge, dge = _gelu_and_grad(ly_ref[...])
        dw_ref[...] = _dot_tn((h * ge).astype(BF16), dbb).astype(BF16)
        dly_ref[...] = (dyb * h * dge).astype(BF16)
        r = r_ref[...]
        gi = i_ref[...]
        xl = xl_ref[...]
        lam = lam_ref[...]
        ls = _log_sigmoid(lam)
        a, mult = _lru_gates(r, ls)
        c_scr[...] = _shift_up(a, 1)
        dh_scr[...] = dyb * ge
        _scan_backward(c_scr, dh_scr, g_scr)
        du = g_scr[...]
        da = du * _shift_down(h, 1)
        dmult = du * gi * xl
        di = du * mult * xl
        dxl = du * mult * gi
        first = _rows(a.shape) == 0
        dlog_a = da * a - jnp.where(first, 0.0, dmult * a * a / mult)
        dr = dlog_a * (LRU_C * ls)
        dlam_ref[...] = jnp.sum(dlog_a * r, axis=0, keepdims=True) * (LRU_C * (1.0 - _sigmoid(lam)))
        dzr = dr * r * (1.0 - r)
        dzi = di * gi * (1.0 - gi)
        dba_ref[...] = jnp.sum(dzr, axis=0, keepdims=True)
        dbx_ref[...] = jnp.sum(dzi, axis=0, keepdims=True)
        xlb = xl.astype(BF16)
        dzrb = dzr.astype(BF16)
        dzib = dzi.astype(BF16)
        dwa_ref[0] = _dot_tn(xlb, dzrb)
        dwx_ref[0] = _dot_tn(xlb, dzib)
        dxl = _advances(dxl + _dot_nt(dzrb, wa_ref[0]) + _dot_nt(dzib, wx_ref[0]), 4)
        dlx_ref[...] = _taps_sum(dxl, wl_ref).astype(BF16)
        dwl_ref[...] = _conv_wgrad(dxl, lx_ref[...])
        dbl_ref[...] = jnp.sum(dxl[0], axis=0, keepdims=True)

    res = pl.BlockSpec((s, D_MODEL), lambda n: (0, 0))
    rows = pl.BlockSpec((CW, D_MODEL), lambda n: (n, 0))
    col = pl.BlockSpec((s, CW), lambda n: (0, n))
    blk = lambda k: pl.BlockSpec((s, CW), lambda n, k=k: (0, k * nblk + n))
    taps = pl.BlockSpec((4, CW), lambda n: (0, n))
    vec = pl.BlockSpec((1, CW), lambda n: (0, n))
    mat = pl.BlockSpec((1, CW, CW), lambda n: (n, 0, 0))
    hb = jax.ShapeDtypeStruct((s, D_MODEL), BF16)
    v = jax.ShapeDtypeStruct((1, D_MODEL), F32)
    m = jax.ShapeDtypeStruct((LRU_HEADS, HEAD_DIM, HEAD_DIM), F32)
    scr = pltpu.VMEM((s, CW), F32)
    return _call(
        body, name="mix_lru_bwd", grid=(nblk,),
        in_specs=[res, rows, blk(3), blk(4), col, col, col, col, taps, mat, mat, vec],
        out_specs=[col, col, rows, mat, mat, vec, vec, taps, vec, vec],
        out_shape=[hb, hb, jax.ShapeDtypeStruct((D_MODEL, D_MODEL), BF16), m, m, v, v,
                   jax.ShapeDtypeStruct((4, D_MODEL), F32), v, v],
        scratch_shapes=[scr, scr, scr],
        operands=(db, wlb, proj, proj, xl, r, i, h, wl, wa, wx, lam), ride=ride)


def norm_in_bwd(dh1, x, dx2, g1):
    s, d = x.shape
    t = _token_tile(s)

    def body(dh_ref, x_ref, dx2_ref, g_ref, dx_ref, dg_ref):
        @pl.when(pl.program_id(0) == 0)
        def _():
            dg_ref[...] = jnp.zeros_like(dg_ref)

        n, r = _rms_stats(x_ref[...])
        dx, dg = _rms_bwd(n, r, g_ref[...], dh_ref[...])
        dx_ref[...] = dx2_ref[...] + dx
        dg_ref[...] += jnp.sum(dg, axis=0, keepdims=True)

    tile = pl.BlockSpec((t, d), lambda i: (i, 0))
    vec = pl.BlockSpec((1, d), lambda i: (0, 0))
    return pl.pallas_call(
        body, name="norm_in_bwd", grid=(s // t,),
        in_specs=[tile, tile, tile, vec],
        out_specs=[tile, vec],
        out_shape=[jax.ShapeDtypeStruct((s, d), F32), jax.ShapeDtypeStruct((1, d), F32)],
        compiler_params=_params(),
    )(dh1, x, dx2, g1)


def local_step(x, target, g1, g2, g3, g4, win4, ws, wcb, wl, bl, wa, ba, wx, bx, lam, wlb, wout, wup4, fw, fb, wdown):
    h1 = norm_in(x, g1)
    proj = matmul_cols(h1, win4, "proj_fwd")
    q, ya = mix_conv_fwd(proj, ws)
    xl, r, gi, h, yb = mix_lru_fwd(proj, wl, bl, wa, ba, wx, bx, lam)
    a, b, merged = branch_merge_fwd(ya, yb, wcb, wlb, proj)
    mix, x2, h2 = mix_out_fwd(merged, wout, x, g2, g3)
    up = matmul_cols(h2, wup4, "up_fwd")
    f = ffn_act_fwd(up, fw, fb)
    dy, dout, loss, dg4 = ffn_down_loss(f, wdown, x2, target, g4)

    dug, duv, dwdown, dfw_g, dfw_v, dfb_g, dfb_v = ffn_bwd(dout, wdown, up, fw, fb)
    dup = jnp.concatenate([dug, duv], axis=1)
    dfw = jnp.concatenate([dfw_g, dfw_v], axis=1)
    dfb = jnp.concatenate([dfb_g, dfb_v], axis=1)
    dh2, dwup = dgrad_wgrad_cols(dup, wup4, h2, "up_bwd")
    dx2, dmix, dg3, dg2 = norms_mid_bwd(dh2, x2, dy, mix, g3, g2)
    da, db, dwout, dgc, dgl = mix_out_bwd(dmix, wout, merged, a, b, proj)
    dcb, dcc, dcx, dwcb, dws = mix_conv_bwd(da, wcb, proj, q, ws)
    dlx, dly, dwlb, dwa, dwx, dba, dbx, dwl, dbl, dlam = mix_lru_bwd(db, wlb, proj, xl, r, gi, h, wl, wa, wx, lam)
    dproj = jnp.concatenate([dcb, dcc, dcx, dlx, dly, dgc[:, 5 * D_MODEL:6 * D_MODEL], dgl[:, 6 * D_MODEL:]], axis=1)
    dh1, dwin = dgrad_wgrad_cols(dproj, win4, h1, "proj_bwd")
    dx, dg1 = norm_in_bwd(dh1, x, dx2, g1)
    grads = dict(norm_mix_pre=dg1, norm_mix_post=dg2, norm_ffn_pre=dg3, norm_ffn_post=dg4,
                 w_in=dwin, conv_short_w=dws, w_conv_branch=dwcb, lru_conv_w=dwl, lru_conv_b=dbl,
                 lru_wa=dwa, lru_ba=dba, lru_wx=dwx, lru_bx=dbx, lru_lambda=dlam,
                 w_lru_branch=dwlb, w_out=dwout, ffn_w_up=dwup, ffn_conv_w=dfw, ffn_conv_b=dfb,
                 ffn_w_down=dwdown)
    return loss[0, 0], dx, grads


MESH = pl.DeviceIdType.MESH
_HBM = pl.BlockSpec(memory_space=pltpu.HBM)
_OTHER_CHIPS = ((1, 0), (0, 1), (1, 1))
_OTHER_DEVICES = tuple((dx, dy, dc) for dx in (0, 1) for dy in (0, 1) for dc in (0, 1) if dx or dy or dc)
N_DEVICES = 8


def _position():
    return lax.axis_index("x"), lax.axis_index("y"), lax.axis_index("c")


def _flip(v, d):
    return 1 - v if d else v


def _half_rows(ref, h, hr):
    return ref.at[pl.ds(h * hr, hr), :]


def gather_chips(shards):
    n = len(shards)
    nrel = len(_OTHER_CHIPS)

    def body(*refs):
        ins, outs = refs[:n], refs[n:2 * n]
        ici_send, ici_recv, sib_send, sib_recv = refs[2 * n:]
        x, y, c = _position()
        j = 2 * x + y
        hr = [s.shape[0] // 2 for s in shards]

        def chip(p):
            px, py = _flip(x, _OTHER_CHIPS[p][0]), _flip(y, _OTHER_CHIPS[p][1])
            return px, py, 2 * px + py

        def ici(a, p, slot):
            px, py, _ = chip(p)
            return pltpu.make_async_remote_copy(
                src_ref=_half_rows(ins[a], c, hr[a]), dst_ref=_half_rows(outs[a].at[slot], c, hr[a]),
                send_sem=ici_send.at[a * nrel + p], recv_sem=ici_recv.at[a * nrel + p],
                device_id=(px, py, c), device_id_type=MESH)

        def sib(a, p, h):
            _, _, k = chip(p)
            part = _half_rows(outs[a].at[k], h, hr[a])
            return pltpu.make_async_remote_copy(
                src_ref=part, dst_ref=part, send_sem=sib_send.at[a * nrel + p], recv_sem=sib_recv.at[a * nrel + p],
                device_id=(x, y, 1 - c), device_id_type=MESH)

        pairs = [(a, p) for a in range(n) for p in range(nrel)]
        for a, p in pairs:
            ici(a, p, j).start()
        for a, p in pairs:
            ici(a, p, chip(p)[2]).wait_recv()
            sib(a, p, c).start()
        for a, p in pairs:
            sib(a, p, 1 - c).wait_recv()
        for a, p in pairs:
            ici(a, p, j).wait_send()
            sib(a, p, c).wait_send()

    got = pl.pallas_call(
        body, name="gather_chips",
        in_specs=[_HBM] * n, out_specs=[_HBM] * n,
        out_shape=[jax.ShapeDtypeStruct((N_CHIPS,) + s.shape, s.dtype) for s in shards],
        scratch_shapes=[pltpu.SemaphoreType.DMA((n * nrel,))] * 4,
    )(*shards)
    j = 2 * lax.axis_index("x") + lax.axis_index("y")
    return [lax.dynamic_update_slice(g, s[None], (j, 0, 0)) for g, s in zip(got, shards)]


def _owned_part(ref, kind, k, h, hr):
    if kind == "col":
        ns = ref.shape[1] // N_CHIPS
        return ref.at[pl.ds(h * hr, hr), pl.ds(k * ns, ns)]
    if kind == "row":
        return ref.at[pl.ds(k * 2 * hr + h * hr, hr), :]
    return ref.at[k, pl.ds(h * hr, hr), :]


def _part_shape(g, kind):
    if kind == "col":
        return g.shape[0] // 2, g.shape[1] // N_CHIPS
    if kind == "row":
        return g.shape[0] // (2 * N_CHIPS), g.shape[1]
    return g.shape[1] // 2, g.shape[2]


def pair_split(grads, kinds, name):
    n = len(grads)
    shapes = [_part_shape(g, k) for g, k in zip(grads, kinds)]

    def body(*refs):
        ins, theirs = refs[:n], refs[n:2 * n]
        send_sem, recv_sem = refs[2 * n:]
        x, y, c = _position()
        copies = []
        for a in range(n):
            hr = shapes[a][0]
            for k in range(N_CHIPS):
                s = a * N_CHIPS + k
                copies.append(pltpu.make_async_remote_copy(
                    src_ref=_owned_part(ins[a], kinds[a], k, 1 - c, hr), dst_ref=theirs[a].at[k],
                    send_sem=send_sem.at[s], recv_sem=recv_sem.at[s], device_id=(x, y, 1 - c), device_id_type=MESH))
        for cp in copies:
            cp.start()
        for cp in copies:
            cp.wait()

    return pl.pallas_call(
        body, name=name,
        in_specs=[_HBM] * n, out_specs=[_HBM] * n,
        out_shape=[jax.ShapeDtypeStruct((N_CHIPS,) + shp, g.dtype) for shp, g in zip(shapes, grads)],
        scratch_shapes=[pltpu.SemaphoreType.DMA((n * N_CHIPS,))] * 2,
    )(*grads)


def chip_exchange(sums, rep):
    n = len(sums)
    nrel = len(_OTHER_CHIPS)
    ndev = len(_OTHER_DEVICES)

    def body(*refs):
        ins, rep_ref = refs[:n], refs[n]
        outs, rep_out = refs[n + 1:2 * n + 1], refs[2 * n + 1]
        loc_sem, send_sem, recv_sem, rep_send, rep_recv = refs[2 * n + 2:]
        x, y, c = _position()
        j = 2 * x + y
        me = 4 * x + 2 * y + c

        def chip(p):
            px, py = _flip(x, _OTHER_CHIPS[p][0]), _flip(y, _OTHER_CHIPS[p][1])
            return px, py, 2 * px + py

        def part(a, p, src_slot, dst_slot):
            px, py, _ = chip(p)
            return pltpu.make_async_remote_copy(
                src_ref=ins[a].at[src_slot], dst_ref=outs[a].at[dst_slot],
                send_sem=send_sem.at[a * nrel + p], recv_sem=recv_sem.at[a * nrel + p],
                device_id=(px, py, c), device_id_type=MESH)

        def device(q):
            dx, dy, dc = _OTHER_DEVICES[q]
            return _flip(x, dx), _flip(y, dy), _flip(c, dc)

        def rep_copy(q, slot):
            return pltpu.make_async_remote_copy(
                src_ref=rep_ref, dst_ref=rep_out.at[slot], send_sem=rep_send.at[q], recv_sem=rep_recv.at[q],
                device_id=device(q), device_id_type=MESH)

        own = [pltpu.make_async_copy(ins[a].at[j], outs[a].at[j], loc_sem.at[a]) for a in range(n)]
        own.append(pltpu.make_async_copy(rep_ref, rep_out.at[me], loc_sem.at[n]))
        for cp in own:
            cp.start()
        pairs = [(a, p) for a in range(n) for p in range(nrel)]
        for a, p in pairs:
            part(a, p, chip(p)[2], j).start()
        for q in range(ndev):
            rep_copy(q, me).start()
        for a, p in pairs:
            part(a, p, chip(p)[2], chip(p)[2]).wait_recv()
        for q in range(ndev):
            px, py, pc = device(q)
            rep_copy(q, 4 * px + 2 * py + pc).wait_recv()
        for a, p in pairs:
            part(a, p, chip(p)[2], j).wait_send()
        for q in range(ndev):
            rep_copy(q, me).wait_send()
        for cp in own:
            cp.wait()

    return pl.pallas_call(
        body, name="chip_exchange",
        in_specs=[_HBM] * (n + 1), out_specs=[_HBM] * (n + 1),
        out_shape=[jax.ShapeDtypeStruct(s.shape, s.dtype) for s in sums]
        + [jax.ShapeDtypeStruct((N_DEVICES,) + rep.shape, rep.dtype)],
        scratch_shapes=[pltpu.SemaphoreType.DMA((n + 1,)), pltpu.SemaphoreType.DMA((n * nrel,)),
                        pltpu.SemaphoreType.DMA((n * nrel,)), pltpu.SemaphoreType.DMA((ndev,)),
                        pltpu.SemaphoreType.DMA((ndev,))],
    )(*sums, rep)


def pair_swap(halves):
    n = len(halves)

    def body(*refs):
        ins, outs = refs[:n], refs[n:2 * n]
        send_sem, recv_sem = refs[2 * n:]
        x, y, c = _position()
        copies = [pltpu.make_async_remote_copy(
            src_ref=ins[a], dst_ref=outs[a], send_sem=send_sem.at[a], recv_sem=recv_sem.at[a],
            device_id=(x, y, 1 - c), device_id_type=MESH) for a in range(n)]
        for cp in copies:
            cp.start()
        for cp in copies:
            cp.wait()

    return pl.pallas_call(
        body, name="pair_swap",
        in_specs=[_HBM] * n, out_specs=[_HBM] * n,
        out_shape=[jax.ShapeDtypeStruct(h.shape, h.dtype) for h in halves],
        scratch_shapes=[pltpu.SemaphoreType.DMA((n,))] * 2,
    )(*halves)


def _row_tile(rows, cols, limit_bytes=1 << 20):
    best = None
    for t in range(SUBLANES, rows + 1, SUBLANES):
        if rows % t == 0 and t * cols * 4 <= limit_bytes:
            best = t
    return best or rows


def add_pair(g, kind, theirs, core, name):
    nc, rows, cols = theirs.shape
    t = _row_tile(rows, cols)
    nt = rows // t

    def body(core_ref, g_ref, b_ref, o_ref):
        mine = g_ref[...].reshape(t, cols)
        o_ref[0] = (mine.astype(F32) + b_ref[0].astype(F32)).astype(o_ref.dtype)

    if kind == "col":
        own = pl.BlockSpec((t, cols), lambda k, i, c: (c[0] * nt + i, k))
    elif kind == "row":
        own = pl.BlockSpec((t, cols), lambda k, i, c: ((2 * k + c[0]) * nt + i, 0))
    else:
        own = pl.BlockSpec((1, t, cols), lambda k, i, c: (k, c[0] * nt + i, 0))
    spec = pl.BlockSpec((1, t, cols), lambda k, i, c: (k, i, 0))
    return pl.pallas_call(
        body, name=name,
        grid_spec=pltpu.PrefetchScalarGridSpec(num_scalar_prefetch=1, grid=(nc, nt), in_specs=[own, spec], out_specs=spec),
        out_shape=jax.ShapeDtypeStruct(theirs.shape, theirs.dtype), compiler_params=_params(),
    )(core, g, theirs)


def sum_lead(a, name):
    nl, rows, cols = a.shape
    t = _row_tile(rows, cols, (1 << 20) // 2)

    def body(a_ref, o_ref):
        acc = a_ref[0].astype(F32)
        for s in range(1, nl):
            acc = acc + a_ref[s].astype(F32)
        o_ref[...] = acc

    return pl.pallas_call(
        body, name=name, grid=(rows // t,),
        in_specs=[pl.BlockSpec((nl, t, cols), lambda i: (0, i, 0))],
        out_specs=pl.BlockSpec((t, cols), lambda i: (i, 0)),
        out_shape=jax.ShapeDtypeStruct((rows, cols), F32), compiler_params=_params(),
    )(a)


def sum_chips(rx, csum, chip, name):
    nc, rows, cols = rx.shape
    t = _row_tile(rows, cols, (1 << 20) // 2)

    def body(chip_ref, r0, r1, r2, r3, own_ref, o_ref):
        acc = None
        for s, ref in enumerate((r0, r1, r2, r3)):
            term = jnp.where(chip_ref[0] == s, own_ref[0], ref[0]).astype(F32)
            acc = term if acc is None else acc + term
        o_ref[...] = acc

    def slot(s):
        return pl.BlockSpec((1, t, cols), lambda i, c, s=s: (jnp.where(c[0] == s, c[0] ^ 1, s), i, 0))

    return pl.pallas_call(
        body, name=name,
        grid_spec=pltpu.PrefetchScalarGridSpec(
            num_scalar_prefetch=1, grid=(rows // t,),
            in_specs=[slot(s) for s in range(nc)] + [pl.BlockSpec((1, t, cols), lambda i, c: (c[0], i, 0))],
            out_specs=pl.BlockSpec((t, cols), lambda i, c: (i, 0))),
        out_shape=jax.ShapeDtypeStruct((rows, cols), F32), compiler_params=_params(),
    )(chip, rx, rx, rx, rx, csum)


def _adamw_update(w, g, m, v):
    nm = ADAM_B1 * m + (1.0 - ADAM_B1) * g
    nv = ADAM_B2 * v + (1.0 - ADAM_B2) * (g * g)
    m_hat = nm * (1.0 / (1.0 - ADAM_B1 ** ADAM_STEP))
    v_hat = nv * (1.0 / (1.0 - ADAM_B2 ** ADAM_STEP))
    return -ADAM_LR * (m_hat / (jnp.sqrt(v_hat) + ADAM_EPS) + ADAM_WD * w), nm, nv


def adamw(w, g, m, v, name):
    rows, cols = w.shape
    t = _row_tile(rows, cols)

    def body(w_ref, g_ref, m_ref, v_ref, d_ref, nm_ref, nv_ref):
        d_ref[...], nm_ref[...], nv_ref[...] = _adamw_update(w_ref[...], g_ref[...], m_ref[...], v_ref[...])

    spec = pl.BlockSpec((t, cols), lambda i: (i, 0))
    shp = jax.ShapeDtypeStruct((rows, cols), F32)
    return pl.pallas_call(
        body, name=name, grid=(rows // t,), in_specs=[spec] * 4, out_specs=[spec] * 3,
        out_shape=[shp, shp, shp], compiler_params=_params(),
    )(w, g, m, v)


def adamw_halves(w, g_mine, g_other, m, v, core, name):
    rows, cols = w.shape
    hr = rows // 2
    t = _row_tile(hr, cols)
    nt = hr // t

    def body(core_ref, w_ref, gm_ref, go_ref, m_ref, v_ref, g_ref, d_ref, nm_ref, nv_ref):
        g = jnp.where(pl.program_id(0) // nt == core_ref[0], gm_ref[...], go_ref[...])
        g_ref[...] = g
        d_ref[...], nm_ref[...], nv_ref[...] = _adamw_update(w_ref[...], g, m_ref[...], v_ref[...])

    spec = pl.BlockSpec((t, cols), lambda i, c: (i, 0))
    half = pl.BlockSpec((t, cols), lambda i, c: (i % nt, 0))
    shp = jax.ShapeDtypeStruct((rows, cols), F32)
    return pl.pallas_call(
        body, name=name,
        grid_spec=pltpu.PrefetchScalarGridSpec(num_scalar_prefetch=1, grid=(2 * nt,),
                                               in_specs=[spec, half, half, spec, spec], out_specs=[spec] * 4),
        out_shape=[shp] * 4, compiler_params=_params(),
    )(core, w, g_mine, g_other, m, v)


WEIGHTS = ("norm_mix_pre", "norm_mix_post", "norm_ffn_pre", "norm_ffn_post", "w_in", "conv_short_w",
           "w_conv_branch", "lru_conv_w", "lru_conv_b", "lru_wa", "lru_ba", "lru_wx", "lru_bx", "lru_lambda",
           "w_lru_branch", "w_out", "ffn_w_up", "ffn_conv_w", "ffn_conv_b", "ffn_w_down")
BIG = ("w_in", "ffn_w_up", "w_conv_branch", "w_lru_branch", "w_out", "ffn_w_down")
BIG_KIND = ("col", "col", "row", "row", "row", "row")
SMALL = ("conv_short_w", "lru_conv_w", "lru_wa", "lru_ba", "lru_wx", "lru_bx", "ffn_conv_w")
REPL = ("norm_mix_pre", "norm_mix_post", "norm_ffn_pre", "norm_ffn_post", "lru_conv_b", "lru_lambda", "ffn_conv_b")
PACK_W = 256
SMALL_ROWS = 544
REPL_ROWS = 16
FFN_SHARD = 2 * D_FF // N_CHIPS
QUARTER = HEAD_DIM // N_CHIPS


def _pack_small_shard(p):
    rows = [p["conv_short_w"].reshape(3, PACK_W), p["lru_conv_w"].reshape(4, PACK_W),
            p["lru_wa"].reshape(LRU_HEADS * QUARTER, PACK_W), p["lru_ba"].reshape(1, PACK_W),
            p["lru_wx"].reshape(LRU_HEADS * QUARTER, PACK_W), p["lru_bx"].reshape(1, PACK_W),
            p["ffn_conv_w"].reshape(3 * FFN_SHARD // PACK_W, PACK_W)]
    used = sum(r.shape[0] for r in rows)
    return jnp.concatenate(rows + [jnp.zeros((SMALL_ROWS - used, PACK_W), F32)], axis=0)


def _unpack_small_shard(buf):
    out, r = {}, 0
    for name, nr, shape in (("conv_short_w", 3, (1, 3, PACK_W)), ("lru_conv_w", 4, (1, 4, PACK_W)),
                            ("lru_wa", LRU_HEADS * QUARTER, (1, LRU_HEADS, QUARTER, HEAD_DIM)),
                            ("lru_ba", 1, (1, LRU_HEADS, QUARTER)),
                            ("lru_wx", LRU_HEADS * QUARTER, (1, LRU_HEADS, QUARTER, HEAD_DIM)),
                            ("lru_bx", 1, (1, LRU_HEADS, QUARTER)),
                            ("ffn_conv_w", 3 * FFN_SHARD // PACK_W, (1, 3, FFN_SHARD))):
        out[name] = buf[r:r + nr].reshape(shape)
        r += nr
    return out


def _full_small(g4):
    per = [_unpack_small_shard(g4[k]) for k in range(N_CHIPS)]
    cat = lambda name, axis: jnp.concatenate([per[k][name][0] for k in range(N_CHIPS)], axis=axis)
    return dict(conv_short_w=cat("conv_short_w", 1), lru_conv_w=cat("lru_conv_w", 1),
                lru_wa=cat("lru_wa", 1), lru_ba=cat("lru_ba", 1).reshape(1, D_MODEL),
                lru_wx=cat("lru_wx", 1), lru_bx=cat("lru_bx", 1).reshape(1, D_MODEL),
                ffn_conv_w=cat("ffn_conv_w", 1))


def _split_small(full):
    shards = []
    for k in range(N_CHIPS):
        cols = lambda a, w: a[:, k * w:(k + 1) * w]
        q = slice(k * QUARTER, (k + 1) * QUARTER)
        shards.append(_pack_small_shard(dict(
            conv_short_w=cols(full["conv_short_w"], PACK_W), lru_conv_w=cols(full["lru_conv_w"], PACK_W),
            lru_wa=full["lru_wa"][:, q, :], lru_ba=full["lru_ba"].reshape(LRU_HEADS, HEAD_DIM)[:, q],
            lru_wx=full["lru_wx"][:, q, :], lru_bx=full["lru_bx"].reshape(LRU_HEADS, HEAD_DIM)[:, q],
            ffn_conv_w=cols(full["ffn_conv_w"], FFN_SHARD))))
    return jnp.stack(shards)


def _pack_repl(p):
    rows = [p[n].reshape(-1, D_MODEL) for n in REPL]
    used = sum(r.shape[0] for r in rows)
    return jnp.concatenate(rows + [jnp.zeros((REPL_ROWS - used, D_MODEL), F32)], axis=0)


def _unpack_repl(buf):
    out, r = {}, 0
    for n in REPL:
        nr = (2 * D_FF // D_MODEL) if n == "ffn_conv_b" else 1
        out[n] = buf[r:r + nr].reshape(1, nr * D_MODEL)
        r += nr
    return out


def kernel(x, norm_mix_pre, norm_mix_post, norm_ffn_pre, norm_ffn_post, w_in, conv_short_w, w_conv_branch, lru_conv_w, lru_conv_b, lru_wa, lru_ba, lru_wx, lru_bx, lru_lambda, w_lru_branch, w_out, ffn_w_up, ffn_conv_w, ffn_conv_b, ffn_w_down, loss_target, m_norm_mix_pre, m_norm_mix_post, m_norm_ffn_pre, m_norm_ffn_post, m_w_in, m_conv_short_w, m_w_conv_branch, m_lru_conv_w, m_lru_conv_b, m_lru_wa, m_lru_ba, m_lru_wx, m_lru_bx, m_lru_lambda, m_w_lru_branch, m_w_out, m_ffn_w_up, m_ffn_conv_w, m_ffn_conv_b, m_ffn_w_down, v_norm_mix_pre, v_norm_mix_post, v_norm_ffn_pre, v_norm_ffn_post, v_w_in, v_conv_short_w, v_w_conv_branch, v_lru_conv_w, v_lru_conv_b, v_lru_wa, v_lru_ba, v_lru_wx, v_lru_bx, v_lru_lambda, v_w_lru_branch, v_w_out, v_ffn_w_up, v_ffn_conv_w, v_ffn_conv_b, v_ffn_w_down):
    given = dict(locals())
    w = {n: given[n] for n in WEIGHTS}
    m = {n: given["m_" + n] for n in WEIGHTS}
    v = {n: given["v_" + n] for n in WEIGHTS}

    xi, yi, ci = _position()
    chip_i = 2 * xi + yi
    chip = chip_i.astype(jnp.int32).reshape(1)
    core = ci.astype(jnp.int32).reshape(1)
    xs, target = x[0], loss_target[0]
    g1, g2, g3, g4 = w["norm_mix_pre"], w["norm_mix_post"], w["norm_ffn_pre"], w["norm_ffn_post"]
    shard = {n: w[n][0].astype(BF16) for n in BIG}
    small_shard = _pack_small_shard(w)

    def gathered(bufs, names):
        return [_own_slot(b, small_shard if n == "small" else shard[n], chip_i) for b, n in zip(bufs, names)]

    def chip_sums(arrays, kinds, tag):
        theirs = pair_split(arrays, kinds, "pair_split_" + tag)
        return [add_pair(g, k, t, core, "pair_add_%s_%d" % (tag, i)) for i, (g, k, t) in enumerate(zip(arrays, kinds, theirs))]

    h1 = norm_in(xs, g1)
    win4, small4 = gathered(run_ride(gather_ride([shard["w_in"], small_shard]), "gather_first"), ("w_in", "small"))
    small = _full_small(small4)
    (proj,), got = matmul_cols(h1, win4, "proj_fwd",
                               ride=gather_ride([shard["w_conv_branch"], shard["w_lru_branch"], shard["w_out"]]))
    wcb, wlb, wout = [g.reshape(-1, D_MODEL) for g in gathered(got, ("w_conv_branch", "w_lru_branch", "w_out"))]
    up_piece = lambda r0, nr, into=None: gather_ride([shard["ffn_w_up"]], items=[(0, r0, nr)], into=into)
    down_piece = lambda r0, nr, into=None: gather_ride([shard["ffn_w_down"]], items=[(0, r0, nr)], into=into)
    (q, ya), got = mix_conv_fwd(proj, small["conv_short_w"], ride=up_piece(0, 160))
    (xl, r, gi, h, yb), got = mix_lru_fwd(
        proj, small["lru_conv_w"], w["lru_conv_b"], small["lru_wa"].astype(BF16), small["lru_ba"],
        small["lru_wx"].astype(BF16), small["lru_bx"], w["lru_lambda"], ride=up_piece(160, 512, got))
    (a, b, merged), got = branch_merge_fwd(ya, yb, wcb, wlb, proj, ride=up_piece(672, 352, got))
    (wup4,) = gathered(got, ("ffn_w_up",))
    (mix, x2, h2), got = mix_out_fwd(merged, wout, xs, g2, g3, ride=down_piece(0, 256))
    (up, f), got = ffn_up_act_fwd(h2, wup4, small["ffn_conv_w"], w["ffn_conv_b"], ride=down_piece(256, 512, got))
    wdown = gathered(got, ("ffn_w_down",))[0].reshape(-1, D_MODEL)
    dy, dout, loss, dg4 = ffn_down_loss(f, wdown, x2, target, g4)

    dup, dwdown, dfw, dfb = ffn_bwd(dout, wdown, up, small["ffn_conv_w"], w["ffn_conv_b"])
    cs_down = chip_sums([dwdown], ["row"], "down")
    (dh2, dwup), rx_down = dgrad_wgrad_cols(dup, wup4, h2, "up_bwd", ride=exchange_ride(cs_down))
    cs_up = chip_sums([dwup], ["col"], "up")
    up_rows = lambda r0, nr, into=None: exchange_ride(cs_up, items=[(0, r0, nr)], into=into)
    (dx2, dmix, dg3, dg2), rx_up = norms_mid_bwd(dh2, x2, dy, mix, g3, g2, ride=up_rows(0, 96))
    (da, db, dwout, dgc, dgl), rx_up = mix_out_bwd(dmix, wout, merged, a, b, proj, ride=up_rows(96, 160, rx_up))
    (dcb, dcc, dcx, dwcb, dws), rx_up = mix_conv_bwd(da, wcb, proj, q, small["conv_short_w"],
                                                     ride=up_rows(256, 160, rx_up))
    cs_mid = chip_sums([dwout, dwcb], ["row", "row"], "mid")
    (dlx, dly, dwlb, dwa, dwx, dba, dbx, dwl, dbl, dlam), rx_mid = mix_lru_bwd(
        db, wlb, proj, xl, r, gi, h, small["lru_conv_w"], small["lru_wa"].astype(BF16), small["lru_wx"].astype(BF16),
        w["lru_lambda"], ride=exchange_ride(cs_up + cs_mid, items=[(0, 416, 96), (1, 0, 128), (2, 0, 128)],
                                            into=rx_up + [None, None]))
    rx_up, rx_mid = rx_mid[:1], rx_mid[1:]
    grads = dict(norm_mix_post=dg2, norm_ffn_pre=dg3, norm_ffn_post=dg4, conv_short_w=dws, lru_conv_w=dwl,
                 lru_conv_b=dbl, lru_wa=dwa, lru_ba=dba, lru_wx=dwx, lru_bx=dbx, lru_lambda=dlam,
                 ffn_conv_w=jnp.concatenate([dfw[0], dfw[1]], axis=1), ffn_conv_b=jnp.concatenate([dfb[0], dfb[1]], axis=1))
    cs_late = chip_sums([dwlb, _split_small(grads)], ["row", "lead"], "late")
    dproj = jnp.concatenate([dcb, dcc, dcx, dlx, dly, dgc, dgl], axis=1)
    (dh1, dwin), rx_late = dgrad_wgrad_cols(dproj, win4, h1, "proj_bwd", ride=exchange_ride(cs_late))
    dx, grads["norm_mix_pre"] = norm_in_bwd(dh1, xs, dx2, g1)
    cs_in = chip_sums([dwin], ["col"], "in")
    rx_in, rep_all = run_ride(exchange_ride(cs_in, rep=_pack_repl(grads)), "exchange_last")

    order = (("w_in", cs_in[0], rx_in), ("ffn_w_up", cs_up[0], rx_up[0]), ("w_conv_branch", cs_mid[1], rx_mid[1]),
             ("w_lru_branch", cs_late[0], rx_late[0]), ("w_out", cs_mid[0], rx_mid[0]),
             ("ffn_w_down", cs_down[0], rx_down[0]), ("small", cs_late[1], rx_late[1]))
    halves = [sum_chips(rx, cs, chip, "chip_sum_" + n) for n, cs, rx in order]
    me = 4 * xi + 2 * yi + ci
    rep_grad = sum_lead(_own_slot(rep_all, _pack_repl(grads), me), "device_sum")
    others = pair_swap(halves)

    g_out, d_out, m_out, v_out = {}, {}, {}, {}
    for n, gm, go in zip(BIG, halves[:-1], others[:-1]):
        g, d, nm, nv = adamw_halves(w[n][0], gm, go, m[n][0], v[n][0], core, "adamw_" + n)
        g_out[n], d_out[n], m_out[n], v_out[n] = g[None], d[None], nm[None], nv[None]
    bufs = adamw_halves(small_shard, halves[-1], others[-1], _pack_small_shard(m), _pack_small_shard(v),
                        core, "adamw_small")
    for dst, buf in zip((g_out, d_out, m_out, v_out), bufs):
        dst.update(_unpack_small_shard(buf))
    d, nm, nv = adamw(_pack_repl(w), rep_grad, _pack_repl(m), _pack_repl(v), "adamw_repl")
    for dst, buf in ((g_out, rep_grad), (d_out, d), (m_out, nm), (v_out, nv)):
        dst.update(_unpack_repl(buf))

    total = lax.psum(loss[0, 0], ("x", "y", "c"))
    return (total, dx[None], *[g_out[n] for n in WEIGHTS], *[d_out[n] for n in WEIGHTS],
            *[m_out[n] for n in WEIGHTS], *[v_out[n] for n in WEIGHTS])
```

```python
import functools
import math

import jax
import jax.numpy as jnp
from jax import lax
from jax.experimental import pallas as pl
from jax.experimental.pallas import tpu as pltpu

F32 = jnp.float32
BF16 = jnp.bfloat16

D_MODEL = 1024
N_CHIPS = 4
N_SEG = 7
D_FF = 3 * D_MODEL
LRU_HEADS = 4
HEAD_DIM = D_MODEL // LRU_HEADS
LRU_C = 8.0
RMS_EPS = 1e-6
CW = 256
FW = 256
SUBLANES = 8
VMEM_LIMIT = 60 * 1024 * 1024

ADAM_LR = 0.001
ADAM_B1 = 0.9
ADAM_B2 = 0.999
ADAM_EPS = 1e-08
ADAM_WD = 0.01
ADAM_STEP = 10

_GELU_C = math.sqrt(2.0 / math.pi)
_GELU_K = 0.044715


def _params(**kw):
    return pltpu.CompilerParams(vmem_limit_bytes=VMEM_LIMIT, **kw)


def _sigmoid(x):
    return 1.0 / (1.0 + jnp.exp(-x))


def _gelu(x):
    t = jnp.tanh(_GELU_C * (x + _GELU_K * x * x * x))
    return 0.5 * x * (1.0 + t)


def _gelu_and_grad(x):
    x2 = x * x
    t = jnp.tanh(_GELU_C * (x + _GELU_K * x * x2))
    g = 0.5 * x * (1.0 + t)
    dg = 0.5 * (1.0 + t) + 0.5 * x * (1.0 - t * t) * _GELU_C * (1.0 + 3.0 * _GELU_K * x2)
    return g, dg


def _log_sigmoid(x):
    e = jnp.exp(-jnp.abs(x))
    u = 1.0 + e
    l1p = jnp.where(u == 1.0, e, jnp.log(u) * e / (u - 1.0))
    return jnp.minimum(x, 0.0) - l1p


def _neg_expm1(z):
    series = -z * (1.0 + z * (0.5 + z * (1.0 / 6.0 + z * (1.0 / 24.0 + z * (1.0 / 120.0 + z * (1.0 / 720.0))))))
    return jnp.where(z > -0.2, series, 1.0 - jnp.exp(z))


def _rows(shape):
    return lax.broadcasted_iota(jnp.int32, shape, 0)


def _shift_down(x, k):
    return jnp.where(_rows(x.shape) >= k, pltpu.roll(x, k, 0), 0.0)


def _shift_up(x, k):
    n = x.shape[0]
    return jnp.where(_rows(x.shape) < n - k, pltpu.roll(x, n - k, 0), 0.0)


def _delays(x, k_width):
    return [x] + [_shift_down(x, j) for j in range(1, k_width)]


def _advances(dy, k_width):
    return [dy] + [_shift_up(dy, j) for j in range(1, k_width)]


def _taps_sum(shifted, w_ref, b=None):
    k_width = w_ref.shape[0]
    y = w_ref[k_width - 1:k_width, :] * shifted[0]
    for j in range(1, k_width):
        y = y + w_ref[k_width - 1 - j:k_width - j, :] * shifted[j]
    if b is not None:
        y = y + b
    return y


def _causal_conv(x, w_ref, b=None):
    return _taps_sum(_delays(x, w_ref.shape[0]), w_ref, b)


def _conv_wgrad(advanced, x):
    k_width = len(advanced)
    rows = [jnp.sum(advanced[k_width - 1 - k] * x, axis=0, keepdims=True) for k in range(k_width)]
    return jnp.concatenate(rows, axis=0)


def _dot(a, b):
    return jnp.dot(a, b, preferred_element_type=F32)


def _dot_nt(a, b):
    return lax.dot_general(a, b, (((1,), (1,)), ((), ())), preferred_element_type=F32)


def _dot_tn(a, b):
    return lax.dot_general(a, b, (((0,), (0,)), ((), ())), preferred_element_type=F32)


def _rms_stats(x):
    r = lax.rsqrt(jnp.mean(x * x, axis=-1, keepdims=True) + RMS_EPS)
    return x * r, r


def _rms_bwd(n, r, g, dy):
    dn = dy * g
    dx = r * (dn - n * jnp.mean(dn * n, axis=-1, keepdims=True))
    return dx, dy * n


def _scan_forward(a_ref, b_ref, h_ref):
    n, c = a_ref.shape
    row = lax.broadcasted_iota(jnp.int32, (SUBLANES, c), 0)

    def group(g, carry):
        r0 = pl.multiple_of(g * SUBLANES, SUBLANES)
        a = a_ref[pl.ds(r0, SUBLANES), :]
        b = b_ref[pl.ds(r0, SUBLANES), :]
        for k in (1, 2, 4):
            ap = jnp.where(row >= k, pltpu.roll(a, k, 0), 1.0)
            bp = jnp.where(row >= k, pltpu.roll(b, k, 0), 0.0)
            b = a * bp + b
            a = a * ap
        h = a * carry + b
        h_ref[pl.ds(r0, SUBLANES), :] = h
        return h[SUBLANES - 1:SUBLANES, :]

    lax.fori_loop(0, n // SUBLANES, group, jnp.zeros((1, c), F32))


def _scan_backward(c_ref, b_ref, g_ref):
    n, ch = c_ref.shape
    row = lax.broadcasted_iota(jnp.int32, (SUBLANES, ch), 0)
    n_groups = n // SUBLANES

    def group(i, carry):
        r0 = pl.multiple_of((n_groups - 1 - i) * SUBLANES, SUBLANES)
        a = c_ref[pl.ds(r0, SUBLANES), :]
        b = b_ref[pl.ds(r0, SUBLANES), :]
        for k in (1, 2, 4):
            keep = row < SUBLANES - k
            ap = jnp.where(keep, pltpu.roll(a, SUBLANES - k, 0), 1.0)
            bp = jnp.where(keep, pltpu.roll(b, SUBLANES - k, 0), 0.0)
            b = a * bp + b
            a = a * ap
        g = a * carry + b
        g_ref[pl.ds(r0, SUBLANES), :] = g
        return g[0:1, :]

    lax.fori_loop(0, n_groups, group, jnp.zeros((1, ch), F32))


MESH = pl.DeviceIdType.MESH
_HBM = pl.BlockSpec(memory_space=pltpu.HBM)
_OTHER_CHIPS = ((1, 0), (0, 1), (1, 1))
_OTHER_DEVICES = tuple((dx, dy, dc) for dx in (0, 1) for dy in (0, 1) for dc in (0, 1) if dx or dy or dc)
N_DEVICES = 8


def _position():
    return lax.axis_index("x"), lax.axis_index("y"), lax.axis_index("c")


def _flip(v, d):
    return 1 - v if d else v


def _chip(x, y, p):
    px, py = _flip(x, _OTHER_CHIPS[p][0]), _flip(y, _OTHER_CHIPS[p][1])
    return px, py, 2 * px + py


class _Ride:
    def __init__(self, srcs, bufs, scratch, plan):
        self.srcs, self.bufs, self.scratch, self.plan = list(srcs), list(bufs), list(scratch), plan


def _call(body, *, name, grid, in_specs, out_specs, out_shape, operands, scratch_shapes=(), ride=None):
    in_specs, out_specs, out_shape = list(in_specs), list(out_specs), list(out_shape)
    scratch_shapes = list(scratch_shapes)
    if ride is None:
        return pl.pallas_call(body, name=name, grid=grid, in_specs=in_specs, out_specs=out_specs, out_shape=out_shape,
                              scratch_shapes=scratch_shapes, compiler_params=_params())(*operands)
    n_in, n_out, n_scr = len(in_specs), len(out_shape), len(scratch_shapes)
    old = [i for i, b in enumerate(ride.bufs) if not isinstance(b, jax.ShapeDtypeStruct)]
    n_src, n_old, n_buf = len(ride.srcs), len(old), len(ride.bufs)

    def full_body(*refs):
        o0 = n_in + n_src + n_old
        s0 = o0 + n_out + n_buf
        start, finish = ride.plan(refs[n_in:n_in + n_src], refs[o0 + n_out:s0], refs[s0 + n_scr:])
        ids = [pl.program_id(i) for i in range(len(grid))]
        first = functools.reduce(jnp.logical_and, [i == 0 for i in ids])
        last = functools.reduce(jnp.logical_and, [i == g - 1 for i, g in zip(ids, grid)])
        pl.when(first)(start)
        body(*refs[:n_in], *refs[o0:o0 + n_out], *refs[s0:s0 + n_scr])
        pl.when(last)(finish)

    shapes = [jax.ShapeDtypeStruct(b.shape, b.dtype) for b in ride.bufs]
    res = pl.pallas_call(
        full_body, name=name, grid=grid,
        in_specs=in_specs + [_HBM] * (n_src + n_old), out_specs=out_specs + [_HBM] * n_buf,
        out_shape=out_shape + shapes, scratch_shapes=scratch_shapes + ride.scratch,
        input_output_aliases={n_in + n_src + k: n_out + i for k, i in enumerate(old)},
        compiler_params=_params(),
    )(*operands, *ride.srcs, *[ride.bufs[i] for i in old])
    return list(res[:n_out]), list(res[n_out:])


def run_ride(ride, name):
    def body():
        pass

    return _call(body, name=name, grid=(1,), in_specs=[], out_specs=[], out_shape=[], operands=[], ride=ride)[1]


def gather_ride(shards, items=None, into=None):
    items = items or [(a, 0, s.shape[0]) for a, s in enumerate(shards)]
    bufs = into or [jax.ShapeDtypeStruct((N_CHIPS,) + s.shape, s.dtype) for s in shards]
    nrel = len(_OTHER_CHIPS)

    def plan(srcs, dsts, sems):
        ici_send, ici_recv, sib_send, sib_recv = sems
        x, y, c = _position()
        j = 2 * x + y

        def rows(ref, it, h):
            return ref.at[pl.ds(it[1] + h * (it[2] // 2), it[2] // 2), :]

        def ici(i, p, slot):
            it = items[i]
            px, py, _ = _chip(x, y, p)
            return pltpu.make_async_remote_copy(
                src_ref=rows(srcs[it[0]], it, c), dst_ref=rows(dsts[it[0]].at[slot], it, c),
                send_sem=ici_send.at[i * nrel + p], recv_sem=ici_recv.at[i * nrel + p],
                device_id=(px, py, c), device_id_type=MESH)

        def sib(i, p, h):
            it = items[i]
            part = rows(dsts[it[0]].at[_chip(x, y, p)[2]], it, h)
            return pltpu.make_async_remote_copy(
                src_ref=part, dst_ref=part, send_sem=sib_send.at[i * nrel + p], recv_sem=sib_recv.at[i * nrel + p],
                device_id=(x, y, 1 - c), device_id_type=MESH)

        pairs = [(i, p) for i in range(len(items)) for p in range(nrel)]

        def start():
            for i, p in pairs:
                ici(i, p, j).start()

        def finish():
            for i, p in pairs:
                ici(i, p, _chip(x, y, p)[2]).wait_recv()
                sib(i, p, c).start()
            for i, p in pairs:
                sib(i, p, 1 - c).wait_recv()
            for i, p in pairs:
                ici(i, p, j).wait_send()
                sib(i, p, c).wait_send()

        return start, finish

    return _Ride(shards, bufs, [pltpu.SemaphoreType.DMA((len(items) * nrel,))] * 4, plan)


def exchange_ride(sums, items=None, into=None, rep=None):
    items = items or [(a, 0, s.shape[1]) for a, s in enumerate(sums)]
    into = into or [None] * len(sums)
    bufs = [jax.ShapeDtypeStruct(s.shape, s.dtype) if b is None else b for s, b in zip(sums, into)]
    srcs = list(sums)
    scratch = [pltpu.SemaphoreType.DMA((len(items) * len(_OTHER_CHIPS),))] * 2
    if rep is not None:
        srcs.append(rep)
        bufs.append(jax.ShapeDtypeStruct((N_DEVICES,) + rep.shape, rep.dtype))
        scratch += [pltpu.SemaphoreType.DMA((len(_OTHER_DEVICES),))] * 2
    nrel = len(_OTHER_CHIPS)

    def plan(src_refs, dst_refs, sems):
        x, y, c = _position()
        j = 2 * x + y
        me = 4 * x + 2 * y + c

        def part(i, p, src_slot, dst_slot):
            a, r0, nr = items[i]
            px, py, _ = _chip(x, y, p)
            return pltpu.make_async_remote_copy(
                src_ref=src_refs[a].at[src_slot, pl.ds(r0, nr), :], dst_ref=dst_refs[a].at[dst_slot, pl.ds(r0, nr), :],
                send_sem=sems[0].at[i * nrel + p], recv_sem=sems[1].at[i * nrel + p],
                device_id=(px, py, c), device_id_type=MESH)

        def device(q):
            dx, dy, dc = _OTHER_DEVICES[q]
            return _flip(x, dx), _flip(y, dy), _flip(c, dc)

        def rep_copy(q, slot):
            return pltpu.make_async_remote_copy(
                src_ref=src_refs[-1], dst_ref=dst_refs[-1].at[slot], send_sem=sems[2].at[q], recv_sem=sems[3].at[q],
                device_id=device(q), device_id_type=MESH)

        pairs = [(i, p) for i in range(len(items)) for p in range(nrel)]
        others = range(len(_OTHER_DEVICES)) if rep is not None else ()

        def start():
            for i, p in pairs:
                part(i, p, _chip(x, y, p)[2], j).start()
            for q in others:
                rep_copy(q, me).start()

        def finish():
            for i, p in pairs:
                k = _chip(x, y, p)[2]
                part(i, p, k, k).wait_recv()
            for q in others:
                px, py, pc = device(q)
                rep_copy(q, 4 * px + 2 * py + pc).wait_recv()
            for i, p in pairs:
                part(i, p, _chip(x, y, p)[2], j).wait_send()
            for q in others:
                rep_copy(q, me).wait_send()

        return start, finish

    return _Ride(srcs, bufs, scratch, plan)


def _own_slot(buf, own, index):
    return lax.dynamic_update_slice(buf, own[None], (index,) + (0,) * own.ndim)


def _token_tile(s):
    return min(s, 512)


def norm_in(x, g):
    s, d = x.shape
    t = _token_tile(s)

    def body(x_ref, g_ref, o_ref):
        n, _ = _rms_stats(x_ref[...])
        o_ref[...] = (n * g_ref[...]).astype(BF16)

    return pl.pallas_call(
        body, name="norm_in", grid=(s // t,),
        in_specs=[pl.BlockSpec((t, d), lambda i: (i, 0)), pl.BlockSpec((1, d), lambda i: (0, 0))],
        out_specs=pl.BlockSpec((t, d), lambda i: (i, 0)),
        out_shape=jax.ShapeDtypeStruct((s, d), BF16),
        compiler_params=_params(),
    )(x, g)


def matmul_cols(a, w4, name, ride=None):
    m, k = a.shape
    nj, _, ns = w4.shape
    nb = ns // CW

    def body(a_ref, w_ref, o_ref):
        o_ref[...] = _dot(a_ref[...], w_ref[0])

    return _call(
        body, name=name, grid=(nj, nb),
        in_specs=[pl.BlockSpec((m, k), lambda j, b: (0, 0)),
                  pl.BlockSpec((1, k, CW), lambda j, b: (j, 0, b))],
        out_specs=[pl.BlockSpec((m, CW), lambda j, b: (0, j * nb + b))],
        out_shape=[jax.ShapeDtypeStruct((m, nj * ns), F32)],
        operands=(a, w4), ride=ride)


def mix_conv_fwd(proj, ws, ride=None):
    s = proj.shape[0]
    nblk = D_MODEL // CW

    def body(cb_ref, cc_ref, cx_ref, ws_ref, q_ref, ya_ref):
        q = _causal_conv(cc_ref[...] * cx_ref[...], ws_ref)
        q_ref[...] = q
        ya_ref[...] = (cb_ref[...] * q).astype(BF16)

    seg = lambda k: pl.BlockSpec((s, CW), lambda c, k=k: (0, k * nblk + c))
    return _call(
        body, name="mix_conv_fwd", grid=(nblk,),
        in_specs=[seg(0), seg(1), seg(2), pl.BlockSpec((3, CW), lambda c: (0, c))],
        out_specs=[pl.BlockSpec((s, CW), lambda c: (0, c))] * 2,
        out_shape=[jax.ShapeDtypeStruct((s, D_MODEL), F32), jax.ShapeDtypeStruct((s, D_MODEL), BF16)],
        operands=(proj, proj, proj, ws), ride=ride)


def _lru_gates(r, ls):
    log_a = LRU_C * r * ls
    a = jnp.exp(log_a)
    mult = jnp.sqrt(_neg_expm1(2.0 * log_a))
    mult = jnp.where(_rows(r.shape) == 0, 1.0, mult)
    return a, mult


def mix_lru_fwd(proj, wl, bl, wa, ba, wx, bx, lam, ride=None):
    s = proj.shape[0]
    nblk = D_MODEL // CW

    def body(lx_ref, ly_ref, wl_ref, bl_ref, wa_ref, ba_ref, wx_ref, bx_ref, lam_ref,
             xl_ref, r_ref, i_ref, h_ref, yb_ref, a_scr, u_scr):
        xl = _causal_conv(lx_ref[...], wl_ref, bl_ref[...])
        xl_ref[...] = xl
        xlb = xl.astype(BF16)
        r = _sigmoid(_dot(xlb, wa_ref[0]) + ba_ref[...])
        i = _sigmoid(_dot(xlb, wx_ref[0]) + bx_ref[...])
        r_ref[...] = r
        i_ref[...] = i
        a, mult = _lru_gates(r, _log_sigmoid(lam_ref[...]))
        a_scr[...] = a
        u_scr[...] = mult * i * xl
        _scan_forward(a_scr, u_scr, h_ref)
        yb_ref[...] = (h_ref[...] * _gelu(ly_ref[...])).astype(BF16)

    blk = lambda k: pl.BlockSpec((s, CW), lambda c, k=k: (0, k * nblk + c))
    vec = pl.BlockSpec((1, CW), lambda c: (0, c))
    mat = pl.BlockSpec((1, CW, CW), lambda c: (c, 0, 0))
    out = pl.BlockSpec((s, CW), lambda c: (0, c))
    f = jax.ShapeDtypeStruct((s, D_MODEL), F32)
    return _call(
        body, name="mix_lru_fwd", grid=(nblk,),
        in_specs=[blk(3), blk(4), pl.BlockSpec((4, CW), lambda c: (0, c)), vec, mat, vec, mat, vec, vec],
        out_specs=[out] * 5,
        out_shape=[f, f, f, f, jax.ShapeDtypeStruct((s, D_MODEL), BF16)],
        scratch_shapes=[pltpu.VMEM((s, CW), F32), pltpu.VMEM((s, CW), F32)],
        operands=(proj, proj, wl, bl, wa, ba, wx, bx, lam), ride=ride)


def branch_merge_fwd(ya, yb, wcb, wlb, proj, ride=None):
    s = ya.shape[0]
    nblk = D_MODEL // CW

    def body(ya_ref, yb_ref, wcb_ref, wlb_ref, gc_ref, gl_ref, a_ref, b_ref, m_ref):
        a = _dot(ya_ref[...], wcb_ref[...])
        b = _dot(yb_ref[...], wlb_ref[...])
        a_ref[...] = a
        b_ref[...] = b
        m_ref[...] = (_sigmoid(gc_ref[...]) * a + _sigmoid(gl_ref[...]) * b).astype(BF16)

    res = pl.BlockSpec((s, D_MODEL), lambda n: (0, 0))
    wcol = pl.BlockSpec((D_MODEL, CW), lambda n: (0, n))
    blk = lambda k: pl.BlockSpec((s, CW), lambda n, k=k: (0, k * nblk + n))
    out = pl.BlockSpec((s, CW), lambda n: (0, n))
    f = jax.ShapeDtypeStruct((s, D_MODEL), F32)
    return _call(
        body, name="branch_merge_fwd", grid=(nblk,),
        in_specs=[res, res, wcol, wcol, blk(5), blk(6)],
        out_specs=[out] * 3,
        out_shape=[f, f, jax.ShapeDtypeStruct((s, D_MODEL), BF16)],
        operands=(ya, yb, wcb, wlb, proj, proj), ride=ride)


def mix_out_fwd(merged, wout, x, g2, g3, ride=None):
    s, d = x.shape
    t = _token_tile(s)

    def body(m_ref, w_ref, x_ref, g2_ref, g3_ref, mix_ref, x2_ref, h2_ref):
        mix = _dot(m_ref[...], w_ref[...])
        mix_ref[...] = mix
        n, _ = _rms_stats(mix)
        x2 = x_ref[...] + n * g2_ref[...]
        x2_ref[...] = x2
        n2, _ = _rms_stats(x2)
        h2_ref[...] = (n2 * g3_ref[...]).astype(BF16)

    tile = pl.BlockSpec((t, d), lambda i: (i, 0))
    vec = pl.BlockSpec((1, d), lambda i: (0, 0))
    f = jax.ShapeDtypeStruct((s, d), F32)
    return _call(
        body, name="mix_out_fwd", grid=(s // t,),
        in_specs=[tile, pl.BlockSpec((d, d), lambda i: (0, 0)), tile, vec, vec],
        out_specs=[tile] * 3,
        out_shape=[f, f, jax.ShapeDtypeStruct((s, d), BF16)],
        operands=(merged, wout, x, g2, g3), ride=ride)


def ffn_up_act_fwd(h2, wup4, fw, fb, ride=None):
    s, k = h2.shape
    ns = wup4.shape[2]
    per_chip = ns // CW
    nblk = D_FF // CW

    def body(h_ref, wg_ref, wv_ref, cg_ref, cv_ref, bg_ref, bv_ref, up_ref, act_ref, f_ref):
        h = h_ref[...]
        ug = _dot(h, wg_ref[0])
        uv = _dot(h, wv_ref[0])
        up_ref[0] = ug
        up_ref[1] = uv
        gate = _causal_conv(ug, cg_ref, bg_ref[...])
        val = _causal_conv(uv, cv_ref, bv_ref[...])
        act_ref[0] = gate.astype(BF16)
        act_ref[1] = val.astype(BF16)
        f_ref[...] = (_gelu(gate) * val).astype(BF16)

    wcols = lambda h: pl.BlockSpec((1, k, CW), lambda n, h=h: (n // per_chip + 2 * h, 0, n % per_chip))
    half = lambda h, rows: pl.BlockSpec((rows, CW), lambda n, h=h: (0, h * nblk + n))
    both = pl.BlockSpec((2, s, CW), lambda n: (0, 0, n))
    return _call(
        body, name="ffn_up_act_fwd", grid=(nblk,),
        in_specs=[pl.BlockSpec((s, k), lambda n: (0, 0)), wcols(0), wcols(1),
                  half(0, 3), half(1, 3), half(0, 1), half(1, 1)],
        out_specs=[both, both, pl.BlockSpec((s, CW), lambda n: (0, n))],
        out_shape=[jax.ShapeDtypeStruct((2, s, D_FF), F32), jax.ShapeDtypeStruct((2, s, D_FF), BF16),
                   jax.ShapeDtypeStruct((s, D_FF), BF16)],
        operands=(h2, wup4, wup4, fw, fw, fb, fb), ride=ride)


def ffn_down_loss(f, wdown, x2, target, g4):
    s, d = x2.shape
    t = _token_tile(s)

    def body(f_ref, w_ref, x2_ref, tg_ref, g4_ref, dy_ref, dout_ref, loss_ref, dg4_ref):
        @pl.when(pl.program_id(0) == 0)
        def _():
            loss_ref[...] = jnp.zeros_like(loss_ref)
            dg4_ref[...] = jnp.zeros_like(dg4_ref)

        out = _dot(f_ref[...], w_ref[...])
        n, r = _rms_stats(out)
        err = x2_ref[...] + n * g4_ref[...] - tg_ref[...]
        loss_ref[...] += jnp.full(loss_ref.shape, (0.5 / d) * jnp.sum(err * err), F32)
        dy = err * (1.0 / d)
        dy_ref[...] = dy
        dout, dg = _rms_bwd(n, r, g4_ref[...], dy)
        dout_ref[...] = dout.astype(BF16)
        dg4_ref[...] += jnp.sum(dg, axis=0, keepdims=True)

    tile = pl.BlockSpec((t, d), lambda i: (i, 0))
    vec = pl.BlockSpec((1, d), lambda i: (0, 0))
    return pl.pallas_call(
        body, name="ffn_down_loss", grid=(s // t,),
        in_specs=[pl.BlockSpec((t, D_FF), lambda i: (i, 0)), pl.BlockSpec((D_FF, d), lambda i: (0, 0)), tile, tile, vec],
        out_specs=[tile, tile, pl.BlockSpec((1, 128), lambda i: (0, 0)), vec],
        out_shape=[jax.ShapeDtypeStruct((s, d), F32), jax.ShapeDtypeStruct((s, d), BF16),
                   jax.ShapeDtypeStruct((1, 128), F32), jax.ShapeDtypeStruct((1, d), F32)],
        compiler_params=_params(),
    )(f, wdown, x2, target, g4)


def ffn_bwd(dout, wdown, up, act, f, fw):
    s = up.shape[1]
    nblk = D_FF // FW

    def body(do_ref, wd_ref, up_ref, act_ref, f_ref, wg_ref, wv_ref, dup_ref, dwd_ref, dw_ref, db_ref):
        do = do_ref[...]
        df = _dot_nt(do, wd_ref[...])
        dwd_ref[...] = _dot_tn(f_ref[...], do).astype(BF16)
        val = act_ref[1].astype(F32)
        ge, dge = _gelu_and_grad(act_ref[0].astype(F32))
        dgate = _advances(df * val * dge, 3)
        dval = _advances(df * ge, 3)
        dup_ref[0] = _taps_sum(dgate, wg_ref).astype(BF16)
        dup_ref[1] = _taps_sum(dval, wv_ref).astype(BF16)
        dw_ref[0] = _conv_wgrad(dgate, up_ref[0])
        dw_ref[1] = _conv_wgrad(dval, up_ref[1])
        db_ref[0] = jnp.sum(dgate[0], axis=0, keepdims=True)
        db_ref[1] = jnp.sum(dval[0], axis=0, keepdims=True)

    half = lambda h, rows: pl.BlockSpec((rows, FW), lambda n, h=h: (0, h * nblk + n))
    both = lambda rows: pl.BlockSpec((2, rows, FW), lambda n: (0, 0, n))
    return pl.pallas_call(
        body, name="ffn_bwd", grid=(nblk,),
        in_specs=[pl.BlockSpec((s, D_MODEL), lambda n: (0, 0)), pl.BlockSpec((FW, D_MODEL), lambda n: (n, 0)),
                  both(s), both(s), pl.BlockSpec((s, FW), lambda n: (0, n)), half(0, 3), half(1, 3)],
        out_specs=[both(s), pl.BlockSpec((FW, D_MODEL), lambda n: (n, 0)), both(3), both(1)],
        out_shape=[jax.ShapeDtypeStruct((2, s, D_FF), BF16), jax.ShapeDtypeStruct((D_FF, D_MODEL), BF16),
                   jax.ShapeDtypeStruct((2, 3, D_FF), F32), jax.ShapeDtypeStruct((2, 1, D_FF), F32)],
        compiler_params=_params(),
    )(dout, wdown, up, act, f, fw, fw)


def dgrad_wgrad_cols(dy, w4, a, name, ride=None):
    m, k = a.shape
    nj, _, ns = w4.shape
    nb = ns // CW
    per_seg = dy[0].shape[2] // CW
    first = [sum(d.shape[0] for d in dy[:i]) for i in range(len(dy))]

    def segment(j, b):
        return (j * nb + b) // per_seg, (j * nb + b) % per_seg

    def body(*refs):
        dy_refs, (w_ref, a_ref, da_ref, dw_ref) = refs[:len(dy)], refs[len(dy):]

        @pl.when((pl.program_id(0) == 0) & (pl.program_id(1) == 0))
        def _():
            da_ref[...] = jnp.zeros_like(da_ref)

        seg, _ = segment(pl.program_id(0), pl.program_id(1))
        dyb = dy_refs[-1][0]
        for i in range(len(dy) - 2, -1, -1):
            dyb = jnp.where(seg < first[i + 1], dy_refs[i][0], dyb)
        da_ref[...] += _dot_nt(dyb, w_ref[0])
        dw_ref[...] = _dot_tn(a_ref[...], dyb).astype(BF16)

    def dy_spec(i):
        nseg = dy[i].shape[0]

        def index(j, b):
            seg, col = segment(j, b)
            local = seg - first[i]
            return (jnp.clip(local, 0, nseg - 1), 0,
                    jnp.where(local < 0, 0, jnp.where(local >= nseg, per_seg - 1, col)))

        return pl.BlockSpec((1, m, CW), index)

    return _call(
        body, name=name, grid=(nj, nb),
        in_specs=[dy_spec(i) for i in range(len(dy))]
        + [pl.BlockSpec((1, k, CW), lambda j, b: (j, 0, b)), pl.BlockSpec((m, k), lambda j, b: (0, 0))],
        out_specs=[pl.BlockSpec((m, k), lambda j, b: (0, 0)),
                   pl.BlockSpec((k, CW), lambda j, b: (0, j * nb + b))],
        out_shape=[jax.ShapeDtypeStruct((m, k), F32), jax.ShapeDtypeStruct((k, nj * ns), BF16)],
        operands=(*dy, w4, a), ride=ride)


def norms_mid_bwd(dh2, x2, dy, mix, g3, g2, ride=None):
    s, d = x2.shape
    t = _token_tile(s)

    def body(dh2_ref, x2_ref, dy_ref, mix_ref, g3_ref, g2_ref, dx2_ref, dmix_ref, dg3_ref, dg2_ref):
        @pl.when(pl.program_id(0) == 0)
        def _():
            dg3_ref[...] = jnp.zeros_like(dg3_ref)
            dg2_ref[...] = jnp.zeros_like(dg2_ref)

        n3, r3 = _rms_stats(x2_ref[...])
        dx, dg3 = _rms_bwd(n3, r3, g3_ref[...], dh2_ref[...])
        dx2 = dy_ref[...] + dx
        dx2_ref[...] = dx2
        dg3_ref[...] += jnp.sum(dg3, axis=0, keepdims=True)
        n2, r2 = _rms_stats(mix_ref[...])
        dmix, dg2 = _rms_bwd(n2, r2, g2_ref[...], dx2)
        dmix_ref[...] = dmix.astype(BF16)
        dg2_ref[...] += jnp.sum(dg2, axis=0, keepdims=True)

    tile = pl.BlockSpec((t, d), lambda i: (i, 0))
    vec = pl.BlockSpec((1, d), lambda i: (0, 0))
    v = jax.ShapeDtypeStruct((1, d), F32)
    return _call(
        body, name="norms_mid_bwd", grid=(s // t,),
        in_specs=[tile, tile, tile, tile, vec, vec],
        out_specs=[tile, tile, vec, vec],
        out_shape=[jax.ShapeDtypeStruct((s, d), F32), jax.ShapeDtypeStruct((s, d), BF16), v, v],
        operands=(dh2, x2, dy, mix, g3, g2), ride=ride)


def mix_out_bwd(dmix, wout, merged, a, b, proj, ride=None):
    s = dmix.shape[0]
    nblk = D_MODEL // CW

    def body(dm_ref, w_ref, mg_ref, a_ref, b_ref, gc_ref, gl_ref, da_ref, db_ref, dw_ref, dg_ref):
        dm = dm_ref[...]
        dmerged = _dot_nt(dm, w_ref[...])
        dw_ref[...] = _dot_tn(mg_ref[...], dm).astype(BF16)
        sc = _sigmoid(gc_ref[...])
        sl = _sigmoid(gl_ref[...])
        da_ref[...] = (dmerged * sc).astype(BF16)
        db_ref[...] = (dmerged * sl).astype(BF16)
        dg_ref[0] = (dmerged * a_ref[...] * sc * (1.0 - sc)).astype(BF16)
        dg_ref[1] = (dmerged * b_ref[...] * sl * (1.0 - sl)).astype(BF16)

    res = pl.BlockSpec((s, D_MODEL), lambda n: (0, 0))
    rows = pl.BlockSpec((CW, D_MODEL), lambda n: (n, 0))
    col = pl.BlockSpec((s, CW), lambda n: (0, n))
    blk = lambda k: pl.BlockSpec((s, CW), lambda n, k=k: (0, k * nblk + n))
    hb = jax.ShapeDtypeStruct((s, D_MODEL), BF16)
    return _call(
        body, name="mix_out_bwd", grid=(nblk,),
        in_specs=[res, rows, col, col, col, blk(5), blk(6)],
        out_specs=[col, col, rows, pl.BlockSpec((2, s, CW), lambda n: (0, 0, n))],
        out_shape=[hb, hb, jax.ShapeDtypeStruct((D_MODEL, D_MODEL), BF16), jax.ShapeDtypeStruct((2, s, D_MODEL), BF16)],
        operands=(dmix, wout, merged, a, b, proj, proj), ride=ride)


def mix_conv_bwd(da, wcb, proj, q, ws, ride=None):
    s = da.shape[0]
    nblk = D_MODEL // CW

    def body(da_ref, w_ref, cb_ref, cc_ref, cx_ref, q_ref, ws_ref, dc_ref, dw_ref, dws_ref):
        dab = da_ref[...]
        dya = _dot_nt(dab, w_ref[...])
        cb = cb_ref[...]
        cc = cc_ref[...]
        cx = cx_ref[...]
        q = q_ref[...]
        dw_ref[...] = _dot_tn((cb * q).astype(BF16), dab).astype(BF16)
        dc_ref[0] = (dya * q).astype(BF16)
        dq = _advances(dya * cb, 3)
        dp = _taps_sum(dq, ws_ref)
        dws_ref[...] = _conv_wgrad(dq, cc * cx)
        dc_ref[1] = (dp * cx).astype(BF16)
        dc_ref[2] = (dp * cc).astype(BF16)

    res = pl.BlockSpec((s, D_MODEL), lambda n: (0, 0))
    rows = pl.BlockSpec((CW, D_MODEL), lambda n: (n, 0))
    col = pl.BlockSpec((s, CW), lambda n: (0, n))
    blk = lambda k: pl.BlockSpec((s, CW), lambda n, k=k: (0, k * nblk + n))
    taps = pl.BlockSpec((3, CW), lambda n: (0, n))
    hb = jax.ShapeDtypeStruct((s, D_MODEL), BF16)
    return _call(
        body, name="mix_conv_bwd", grid=(nblk,),
        in_specs=[res, rows, blk(0), blk(1), blk(2), col, taps],
        out_specs=[pl.BlockSpec((3, s, CW), lambda n: (0, 0, n)), rows, taps],
        out_shape=[jax.ShapeDtypeStruct((3, s, D_MODEL), BF16), jax.ShapeDtypeStruct((D_MODEL, D_MODEL), BF16),
                   jax.ShapeDtypeStruct((3, D_MODEL), F32)],
        operands=(da, wcb, proj, proj, proj, q, ws), ride=ride)


def mix_lru_bwd(db, wlb, proj, xl, r, i, h, wl, wa, wx, lam, ride=None):
    s = db.shape[0]
    nblk = D_MODEL // CW

    def body(db_ref, w_ref, lx_ref, ly_ref, xl_ref, r_ref, i_ref, h_ref, wl_ref, wa_ref, wx_ref, lam_ref,
             dl_ref, dw_ref, dwa_ref, dwx_ref, dba_ref, dbx_ref, dwl_ref, dbl_ref, dlam_ref,
             c_scr, dh_scr, g_scr):
        dbb = db_ref[...]
        dyb = _dot_nt(dbb, w_ref[...])
        h = h_ref[...]
        ge, dge = _gelu_and_grad(ly_ref[...])
        dw_ref[...] = _dot_tn((h * ge).astype(BF16), dbb).astype(BF16)
        dl_ref[1] = (dyb * h * dge).astype(BF16)
        r = r_ref[...]
        gi = i_ref[...]
        xl = xl_ref[...]
        lam = lam_ref[...]
        ls = _log_sigmoid(lam)
        a, mult = _lru_gates(r, ls)
        c_scr[...] = _shift_up(a, 1)
        dh_scr[...] = dyb * ge
        _scan_backward(c_scr, dh_scr, g_scr)
        du = g_scr[...]
        da = du * _shift_down(h, 1)
        dmult = du * gi * xl
        di = du * mult * xl
        dxl = du * mult * gi
        first = _rows(a.shape) == 0
        dlog_a = da * a - jnp.where(first, 0.0, dmult * a * a / mult)
        dr = dlog_a * (LRU_C * ls)
        dlam_ref[...] = jnp.sum(dlog_a * r, axis=0, keepdims=True) * (LRU_C * (1.0 - _sigmoid(lam)))
        dzr = dr * r * (1.0 - r)
        dzi = di * gi * (1.0 - gi)
        dba_ref[...] = jnp.sum(dzr, axis=0, keepdims=True)
        dbx_ref[...] = jnp.sum(dzi, axis=0, keepdims=True)
        xlb = xl.astype(BF16)
        dzrb = dzr.astype(BF16)
        dzib = dzi.astype(BF16)
        dwa_ref[0] = _dot_tn(xlb, dzrb)
        dwx_ref[0] = _dot_tn(xlb, dzib)
        dxl = _advances(dxl + _dot_nt(dzrb, wa_ref[0]) + _dot_nt(dzib, wx_ref[0]), 4)
        dl_ref[0] = _taps_sum(dxl, wl_ref).astype(BF16)
        dwl_ref[...] = _conv_wgrad(dxl, lx_ref[...])
        dbl_ref[...] = jnp.sum(dxl[0], axis=0, keepdims=True)

    res = pl.BlockSpec((s, D_MODEL), lambda n: (0, 0))
    rows = pl.BlockSpec((CW, D_MODEL), lambda n: (n, 0))
    col = pl.BlockSpec((s, CW), lambda n: (0, n))
    blk = lambda k: pl.BlockSpec((s, CW), lambda n, k=k: (0, k * nblk + n))
    taps = pl.BlockSpec((4, CW), lambda n: (0, n))
    vec = pl.BlockSpec((1, CW), lambda n: (0, n))
    mat = pl.BlockSpec((1, CW, CW), lambda n: (n, 0, 0))
    hb = jax.ShapeDtypeStruct((s, D_MODEL), BF16)
    v = jax.ShapeDtypeStruct((1, D_MODEL), F32)
    m = jax.ShapeDtypeStruct((LRU_HEADS, HEAD_DIM, HEAD_DIM), F32)
    scr = pltpu.VMEM((s, CW), F32)
    return _call(
        body, name="mix_lru_bwd", grid=(nblk,),
        in_specs=[res, rows, blk(3), blk(4), col, col, col, col, taps, mat, mat, vec],
        out_specs=[pl.BlockSpec((2, s, CW), lambda n: (0, 0, n)), rows, mat, mat, vec, vec, taps, vec, vec],
        out_shape=[jax.ShapeDtypeStruct((2, s, D_MODEL), BF16), jax.ShapeDtypeStruct((D_MODEL, D_MODEL), BF16), m, m, v, v,
                   jax.ShapeDtypeStruct((4, D_MODEL), F32), v, v],
        scratch_shapes=[scr, scr, scr],
        operands=(db, wlb, proj, proj, xl, r, i, h, wl, wa, wx, lam), ride=ride)


def norm_in_bwd(dh1, x, dx2, g1):
    s, d = x.shape
    t = _token_tile(s)

    def body(dh_ref, x_ref, dx2_ref, g_ref, dx_ref, dg_ref):
        @pl.when(pl.program_id(0) == 0)
        def _():
            dg_ref[...] = jnp.zeros_like(dg_ref)

        n, r = _rms_stats(x_ref[...])
        dx, dg = _rms_bwd(n, r, g_ref[...], dh_ref[...])
        dx_ref[...] = dx2_ref[...] + dx
        dg_ref[...] += jnp.sum(dg, axis=0, keepdims=True)

    tile = pl.BlockSpec((t, d), lambda i: (i, 0))
    vec = pl.BlockSpec((1, d), lambda i: (0, 0))
    return pl.pallas_call(
        body, name="norm_in_bwd", grid=(s // t,),
        in_specs=[tile, tile, tile, vec],
        out_specs=[tile, vec],
        out_shape=[jax.ShapeDtypeStruct((s, d), F32), jax.ShapeDtypeStruct((1, d), F32)],
        compiler_params=_params(),
    )(dh1, x, dx2, g1)


def local_step(x, target, g1, g2, g3, g4, win4, ws, wcb, wl, bl, wa, ba, wx, bx, lam, wlb, wout, wup4, fw, fb, wdown):
    h1 = norm_in(x, g1)
    proj = matmul_cols(h1, win4, "proj_fwd")
    q, ya = mix_conv_fwd(proj, ws)
    xl, r, gi, h, yb = mix_lru_fwd(proj, wl, bl, wa, ba, wx, bx, lam)
    a, b, merged = branch_merge_fwd(ya, yb, wcb, wlb, proj)
    mix, x2, h2 = mix_out_fwd(merged, wout, x, g2, g3)
    up = matmul_cols(h2, wup4, "up_fwd")
    f = ffn_act_fwd(up, fw, fb)
    dy, dout, loss, dg4 = ffn_down_loss(f, wdown, x2, target, g4)

    dug, duv, dwdown, dfw_g, dfw_v, dfb_g, dfb_v = ffn_bwd(dout, wdown, up, fw, fb)
    dup = jnp.concatenate([dug, duv], axis=1)
    dfw = jnp.concatenate([dfw_g, dfw_v], axis=1)
    dfb = jnp.concatenate([dfb_g, dfb_v], axis=1)
    dh2, dwup = dgrad_wgrad_cols(dup, wup4, h2, "up_bwd")
    dx2, dmix, dg3, dg2 = norms_mid_bwd(dh2, x2, dy, mix, g3, g2)
    da, db, dwout, dgc, dgl = mix_out_bwd(dmix, wout, merged, a, b, proj)
    dcb, dcc, dcx, dwcb, dws = mix_conv_bwd(da, wcb, proj, q, ws)
    dlx, dly, dwlb, dwa, dwx, dba, dbx, dwl, dbl, dlam = mix_lru_bwd(db, wlb, proj, xl, r, gi, h, wl, wa, wx, lam)
    dproj = jnp.concatenate([dcb, dcc, dcx, dlx, dly, dgc[:, 5 * D_MODEL:6 * D_MODEL], dgl[:, 6 * D_MODEL:]], axis=1)
    dh1, dwin = dgrad_wgrad_cols(dproj, win4, h1, "proj_bwd")
    dx, dg1 = norm_in_bwd(dh1, x, dx2, g1)
    grads = dict(norm_mix_pre=dg1, norm_mix_post=dg2, norm_ffn_pre=dg3, norm_ffn_post=dg4,
                 w_in=dwin, conv_short_w=dws, w_conv_branch=dwcb, lru_conv_w=dwl, lru_conv_b=dbl,
                 lru_wa=dwa, lru_ba=dba, lru_wx=dwx, lru_bx=dbx, lru_lambda=dlam,
                 w_lru_branch=dwlb, w_out=dwout, ffn_w_up=dwup, ffn_conv_w=dfw, ffn_conv_b=dfb,
                 ffn_w_down=dwdown)
    return loss[0, 0], dx, grads


MESH = pl.DeviceIdType.MESH
_HBM = pl.BlockSpec(memory_space=pltpu.HBM)
_OTHER_CHIPS = ((1, 0), (0, 1), (1, 1))
_OTHER_DEVICES = tuple((dx, dy, dc) for dx in (0, 1) for dy in (0, 1) for dc in (0, 1) if dx or dy or dc)
N_DEVICES = 8


def _position():
    return lax.axis_index("x"), lax.axis_index("y"), lax.axis_index("c")


def _flip(v, d):
    return 1 - v if d else v


def _half_rows(ref, h, hr):
    return ref.at[pl.ds(h * hr, hr), :]


def gather_chips(shards):
    n = len(shards)
    nrel = len(_OTHER_CHIPS)

    def body(*refs):
        ins, outs = refs[:n], refs[n:2 * n]
        ici_send, ici_recv, sib_send, sib_recv = refs[2 * n:]
        x, y, c = _position()
        j = 2 * x + y
        hr = [s.shape[0] // 2 for s in shards]

        def chip(p):
            px, py = _flip(x, _OTHER_CHIPS[p][0]), _flip(y, _OTHER_CHIPS[p][1])
            return px, py, 2 * px + py

        def ici(a, p, slot):
            px, py, _ = chip(p)
            return pltpu.make_async_remote_copy(
                src_ref=_half_rows(ins[a], c, hr[a]), dst_ref=_half_rows(outs[a].at[slot], c, hr[a]),
                send_sem=ici_send.at[a * nrel + p], recv_sem=ici_recv.at[a * nrel + p],
                device_id=(px, py, c), device_id_type=MESH)

        def sib(a, p, h):
            _, _, k = chip(p)
            part = _half_rows(outs[a].at[k], h, hr[a])
            return pltpu.make_async_remote_copy(
                src_ref=part, dst_ref=part, send_sem=sib_send.at[a * nrel + p], recv_sem=sib_recv.at[a * nrel + p],
                device_id=(x, y, 1 - c), device_id_type=MESH)

        pairs = [(a, p) for a in range(n) for p in range(nrel)]
        for a, p in pairs:
            ici(a, p, j).start()
        for a, p in pairs:
            ici(a, p, chip(p)[2]).wait_recv()
            sib(a, p, c).start()
        for a, p in pairs:
            sib(a, p, 1 - c).wait_recv()
        for a, p in pairs:
            ici(a, p, j).wait_send()
            sib(a, p, c).wait_send()

    got = pl.pallas_call(
        body, name="gather_chips",
        in_specs=[_HBM] * n, out_specs=[_HBM] * n,
        out_shape=[jax.ShapeDtypeStruct((N_CHIPS,) + s.shape, s.dtype) for s in shards],
        scratch_shapes=[pltpu.SemaphoreType.DMA((n * nrel,))] * 4,
    )(*shards)
    j = 2 * lax.axis_index("x") + lax.axis_index("y")
    return [lax.dynamic_update_slice(g, s[None], (j, 0, 0)) for g, s in zip(got, shards)]


def _owned_part(ref, kind, k, h, hr):
    if kind == "col":
        ns = ref.shape[1] // N_CHIPS
        return ref.at[pl.ds(h * hr, hr), pl.ds(k * ns, ns)]
    if kind == "row":
        return ref.at[pl.ds(k * 2 * hr + h * hr, hr), :]
    return ref.at[k, pl.ds(h * hr, hr), :]


def _part_shape(g, kind):
    if kind == "col":
        return g.shape[0] // 2, g.shape[1] // N_CHIPS
    if kind == "row":
        return g.shape[0] // (2 * N_CHIPS), g.shape[1]
    return g.shape[1] // 2, g.shape[2]


def pair_split(grads, kinds, name):
    n = len(grads)
    shapes = [_part_shape(g, k) for g, k in zip(grads, kinds)]

    def body(*refs):
        ins, theirs = refs[:n], refs[n:2 * n]
        send_sem, recv_sem = refs[2 * n:]
        x, y, c = _position()
        copies = []
        for a in range(n):
            hr = shapes[a][0]
            for k in range(N_CHIPS):
                s = a * N_CHIPS + k
                copies.append(pltpu.make_async_remote_copy(
                    src_ref=_owned_part(ins[a], kinds[a], k, 1 - c, hr), dst_ref=theirs[a].at[k],
                    send_sem=send_sem.at[s], recv_sem=recv_sem.at[s], device_id=(x, y, 1 - c), device_id_type=MESH))
        for cp in copies:
            cp.start()
        for cp in copies:
            cp.wait()

    return pl.pallas_call(
        body, name=name,
        in_specs=[_HBM] * n, out_specs=[_HBM] * n,
        out_shape=[jax.ShapeDtypeStruct((N_CHIPS,) + shp, g.dtype) for shp, g in zip(shapes, grads)],
        scratch_shapes=[pltpu.SemaphoreType.DMA((n * N_CHIPS,))] * 2,
    )(*grads)


def chip_exchange(sums, rep):
    n = len(sums)
    nrel = len(_OTHER_CHIPS)
    ndev = len(_OTHER_DEVICES)

    def body(*refs):
        ins, rep_ref = refs[:n], refs[n]
        outs, rep_out = refs[n + 1:2 * n + 1], refs[2 * n + 1]
        loc_sem, send_sem, recv_sem, rep_send, rep_recv = refs[2 * n + 2:]
        x, y, c = _position()
        j = 2 * x + y
        me = 4 * x + 2 * y + c

        def chip(p):
            px, py = _flip(x, _OTHER_CHIPS[p][0]), _flip(y, _OTHER_CHIPS[p][1])
            return px, py, 2 * px + py

        def part(a, p, src_slot, dst_slot):
            px, py, _ = chip(p)
            return pltpu.make_async_remote_copy(
                src_ref=ins[a].at[src_slot], dst_ref=outs[a].at[dst_slot],
                send_sem=send_sem.at[a * nrel + p], recv_sem=recv_sem.at[a * nrel + p],
                device_id=(px, py, c), device_id_type=MESH)

        def device(q):
            dx, dy, dc = _OTHER_DEVICES[q]
            return _flip(x, dx), _flip(y, dy), _flip(c, dc)

        def rep_copy(q, slot):
            return pltpu.make_async_remote_copy(
                src_ref=rep_ref, dst_ref=rep_out.at[slot], send_sem=rep_send.at[q], recv_sem=rep_recv.at[q],
                device_id=device(q), device_id_type=MESH)

        own = [pltpu.make_async_copy(ins[a].at[j], outs[a].at[j], loc_sem.at[a]) for a in range(n)]
        own.append(pltpu.make_async_copy(rep_ref, rep_out.at[me], loc_sem.at[n]))
        for cp in own:
            cp.start()
        pairs = [(a, p) for a in range(n) for p in range(nrel)]
        for a, p in pairs:
            part(a, p, chip(p)[2], j).start()
        for q in range(ndev):
            rep_copy(q, me).start()
        for a, p in pairs:
            part(a, p, chip(p)[2], chip(p)[2]).wait_recv()
        for q in range(ndev):
            px, py, pc = device(q)
            rep_copy(q, 4 * px + 2 * py + pc).wait_recv()
        for a, p in pairs:
            part(a, p, chip(p)[2], j).wait_send()
        for q in range(ndev):
            rep_copy(q, me).wait_send()
        for cp in own:
            cp.wait()

    return pl.pallas_call(
        body, name="chip_exchange",
        in_specs=[_HBM] * (n + 1), out_specs=[_HBM] * (n + 1),
        out_shape=[jax.ShapeDtypeStruct(s.shape, s.dtype) for s in sums]
        + [jax.ShapeDtypeStruct((N_DEVICES,) + rep.shape, rep.dtype)],
        scratch_shapes=[pltpu.SemaphoreType.DMA((n + 1,)), pltpu.SemaphoreType.DMA((n * nrel,)),
                        pltpu.SemaphoreType.DMA((n * nrel,)), pltpu.SemaphoreType.DMA((ndev,)),
                        pltpu.SemaphoreType.DMA((ndev,))],
    )(*sums, rep)


def pair_swap(halves):
    n = len(halves)

    def body(*refs):
        ins, outs = refs[:n], refs[n:2 * n]
        send_sem, recv_sem = refs[2 * n:]
        x, y, c = _position()
        copies = [pltpu.make_async_remote_copy(
            src_ref=ins[a], dst_ref=outs[a], send_sem=send_sem.at[a], recv_sem=recv_sem.at[a],
            device_id=(x, y, 1 - c), device_id_type=MESH) for a in range(n)]
        for cp in copies:
            cp.start()
        for cp in copies:
            cp.wait()

    return pl.pallas_call(
        body, name="pair_swap",
        in_specs=[_HBM] * n, out_specs=[_HBM] * n,
        out_shape=[jax.ShapeDtypeStruct(h.shape, h.dtype) for h in halves],
        scratch_shapes=[pltpu.SemaphoreType.DMA((n,))] * 2,
    )(*halves)


def _row_tile(rows, cols, limit_bytes=1 << 20):
    best = None
    for t in range(SUBLANES, rows + 1, SUBLANES):
        if rows % t == 0 and t * cols * 4 <= limit_bytes:
            best = t
    return best or rows


def add_pair(g, kind, theirs, core, name):
    nc, rows, cols = theirs.shape
    t = _row_tile(rows, cols)
    nt = rows // t

    def body(core_ref, g_ref, b_ref, o_ref):
        mine = g_ref[...].reshape(t, cols)
        o_ref[0] = (mine.astype(F32) + b_ref[0].astype(F32)).astype(o_ref.dtype)

    if kind == "col":
        own = pl.BlockSpec((t, cols), lambda k, i, c: (c[0] * nt + i, k))
    elif kind == "row":
        own = pl.BlockSpec((t, cols), lambda k, i, c: ((2 * k + c[0]) * nt + i, 0))
    else:
        own = pl.BlockSpec((1, t, cols), lambda k, i, c: (k, c[0] * nt + i, 0))
    spec = pl.BlockSpec((1, t, cols), lambda k, i, c: (k, i, 0))
    return pl.pallas_call(
        body, name=name,
        grid_spec=pltpu.PrefetchScalarGridSpec(num_scalar_prefetch=1, grid=(nc, nt), in_specs=[own, spec], out_specs=spec),
        out_shape=jax.ShapeDtypeStruct(theirs.shape, theirs.dtype), compiler_params=_params(),
    )(core, g, theirs)


def sum_lead(a, name):
    nl, rows, cols = a.shape
    t = _row_tile(rows, cols, (1 << 20) // 2)

    def body(a_ref, o_ref):
        acc = a_ref[0].astype(F32)
        for s in range(1, nl):
            acc = acc + a_ref[s].astype(F32)
        o_ref[...] = acc

    return pl.pallas_call(
        body, name=name, grid=(rows // t,),
        in_specs=[pl.BlockSpec((nl, t, cols), lambda i: (0, i, 0))],
        out_specs=pl.BlockSpec((t, cols), lambda i: (i, 0)),
        out_shape=jax.ShapeDtypeStruct((rows, cols), F32), compiler_params=_params(),
    )(a)


def sum_chips(rx, csum, chip, name):
    nc, rows, cols = rx.shape
    t = _row_tile(rows, cols, (1 << 20) // 2)

    def body(chip_ref, r0, r1, r2, r3, own_ref, o_ref):
        acc = None
        for s, ref in enumerate((r0, r1, r2, r3)):
            term = jnp.where(chip_ref[0] == s, own_ref[0], ref[0]).astype(F32)
            acc = term if acc is None else acc + term
        o_ref[...] = acc

    def slot(s):
        return pl.BlockSpec((1, t, cols), lambda i, c, s=s: (jnp.where(c[0] == s, c[0] ^ 1, s), i, 0))

    return pl.pallas_call(
        body, name=name,
        grid_spec=pltpu.PrefetchScalarGridSpec(
            num_scalar_prefetch=1, grid=(rows // t,),
            in_specs=[slot(s) for s in range(nc)] + [pl.BlockSpec((1, t, cols), lambda i, c: (c[0], i, 0))],
            out_specs=pl.BlockSpec((t, cols), lambda i, c: (i, 0))),
        out_shape=jax.ShapeDtypeStruct((rows, cols), F32), compiler_params=_params(),
    )(chip, rx, rx, rx, rx, csum)


def _adamw_update(w, g, m, v):
    nm = ADAM_B1 * m + (1.0 - ADAM_B1) * g
    nv = ADAM_B2 * v + (1.0 - ADAM_B2) * (g * g)
    m_hat = nm * (1.0 / (1.0 - ADAM_B1 ** ADAM_STEP))
    v_hat = nv * (1.0 / (1.0 - ADAM_B2 ** ADAM_STEP))
    return -ADAM_LR * (m_hat / (jnp.sqrt(v_hat) + ADAM_EPS) + ADAM_WD * w), nm, nv


def adamw(w, g, m, v, name):
    rows, cols = w.shape
    t = _row_tile(rows, cols)

    def body(w_ref, g_ref, m_ref, v_ref, d_ref, nm_ref, nv_ref):
        d_ref[...], nm_ref[...], nv_ref[...] = _adamw_update(w_ref[...], g_ref[...], m_ref[...], v_ref[...])

    spec = pl.BlockSpec((t, cols), lambda i: (i, 0))
    shp = jax.ShapeDtypeStruct((rows, cols), F32)
    return pl.pallas_call(
        body, name=name, grid=(rows // t,), in_specs=[spec] * 4, out_specs=[spec] * 3,
        out_shape=[shp, shp, shp], compiler_params=_params(),
    )(w, g, m, v)


def adamw_halves(w, g_mine, g_other, m, v, core, name):
    rows, cols = w.shape
    hr = rows // 2
    t = _row_tile(hr, cols)
    nt = hr // t

    def body(core_ref, w_ref, gm_ref, go_ref, m_ref, v_ref, g_ref, d_ref, nm_ref, nv_ref):
        g = jnp.where(pl.program_id(0) // nt == core_ref[0], gm_ref[...], go_ref[...])
        g_ref[...] = g
        d_ref[...], nm_ref[...], nv_ref[...] = _adamw_update(w_ref[...], g, m_ref[...], v_ref[...])

    spec = pl.BlockSpec((t, cols), lambda i, c: (i, 0))
    half = pl.BlockSpec((t, cols), lambda i, c: (i % nt, 0))
    shp = jax.ShapeDtypeStruct((rows, cols), F32)
    return pl.pallas_call(
        body, name=name,
        grid_spec=pltpu.PrefetchScalarGridSpec(num_scalar_prefetch=1, grid=(2 * nt,),
                                               in_specs=[spec, half, half, spec, spec], out_specs=[spec] * 4),
        out_shape=[shp] * 4, compiler_params=_params(),
    )(core, w, g_mine, g_other, m, v)


WEIGHTS = ("norm_mix_pre", "norm_mix_post", "norm_ffn_pre", "norm_ffn_post", "w_in", "conv_short_w",
           "w_conv_branch", "lru_conv_w", "lru_conv_b", "lru_wa", "lru_ba", "lru_wx", "lru_bx", "lru_lambda",
           "w_lru_branch", "w_out", "ffn_w_up", "ffn_conv_w", "ffn_conv_b", "ffn_w_down")
BIG = ("w_in", "ffn_w_up", "w_conv_branch", "w_lru_branch", "w_out", "ffn_w_down")
BIG_KIND = ("col", "col", "row", "row", "row", "row")
SMALL = ("conv_short_w", "lru_conv_w", "lru_wa", "lru_ba", "lru_wx", "lru_bx", "ffn_conv_w")
REPL = ("norm_mix_pre", "norm_mix_post", "norm_ffn_pre", "norm_ffn_post", "lru_conv_b", "lru_lambda", "ffn_conv_b")
PACK_W = 256
SMALL_ROWS = 576
REPL_ROWS = 16
LOSS_ROW = 12
FFN_SHARD = 2 * D_FF // N_CHIPS
QUARTER = HEAD_DIM // N_CHIPS
SMALL_PARTS = (("conv_short_w", 3, (1, 3, PACK_W)), ("lru_conv_w", 4, (1, 4, PACK_W)),
               ("lru_wa", LRU_HEADS * QUARTER, (1, LRU_HEADS, QUARTER, HEAD_DIM)), ("lru_ba", 1, (1, LRU_HEADS, QUARTER)),
               ("lru_wx", LRU_HEADS * QUARTER, (1, LRU_HEADS, QUARTER, HEAD_DIM)), ("lru_bx", 1, (1, LRU_HEADS, QUARTER)),
               ("ffn_conv_w", 3 * FFN_SHARD // PACK_W, (1, 3, FFN_SHARD)))


def _pad8(nr):
    return -(-nr // SUBLANES) * SUBLANES


def _pack_small_shard(p):
    rows = [jnp.pad(p[name].reshape(nr, PACK_W), ((0, _pad8(nr) - nr), (0, 0))) for name, nr, _ in SMALL_PARTS]
    used = sum(r.shape[0] for r in rows)
    return jnp.concatenate(rows + [jnp.zeros((SMALL_ROWS - used, PACK_W), F32)], axis=0)


def _unpack_small_shard(buf):
    out, r = {}, 0
    for name, nr, shape in SMALL_PARTS:
        out[name] = buf[r:r + nr].reshape(shape)
        r += _pad8(nr)
    return out


def _full_small(g4):
    per = [_unpack_small_shard(g4[k]) for k in range(N_CHIPS)]
    cat = lambda name, axis: jnp.concatenate([per[k][name][0] for k in range(N_CHIPS)], axis=axis)
    return dict(conv_short_w=cat("conv_short_w", 1), lru_conv_w=cat("lru_conv_w", 1),
                lru_wa=cat("lru_wa", 1), lru_ba=cat("lru_ba", 1).reshape(1, D_MODEL),
                lru_wx=cat("lru_wx", 1), lru_bx=cat("lru_bx", 1).reshape(1, D_MODEL),
                ffn_conv_w=cat("ffn_conv_w", 1))


def _split_small(full):
    shards = []
    for k in range(N_CHIPS):
        cols = lambda a, w: a[:, k * w:(k + 1) * w]
        q = slice(k * QUARTER, (k + 1) * QUARTER)
        shards.append(_pack_small_shard(dict(
            conv_short_w=cols(full["conv_short_w"], PACK_W), lru_conv_w=cols(full["lru_conv_w"], PACK_W),
            lru_wa=full["lru_wa"][:, q, :], lru_ba=full["lru_ba"].reshape(LRU_HEADS, HEAD_DIM)[:, q],
            lru_wx=full["lru_wx"][:, q, :], lru_bx=full["lru_bx"].reshape(LRU_HEADS, HEAD_DIM)[:, q],
            ffn_conv_w=cols(full["ffn_conv_w"], FFN_SHARD))))
    return jnp.stack(shards)


def _pack_repl(p, loss=None):
    rows = [p[n].reshape(-1, D_MODEL) for n in REPL]
    if loss is not None:
        rows.append(jnp.broadcast_to(loss.reshape(1, 1), (1, D_MODEL)))
    used = sum(r.shape[0] for r in rows)
    return jnp.concatenate(rows + [jnp.zeros((REPL_ROWS - used, D_MODEL), F32)], axis=0)


def _unpack_repl(buf):
    out, r = {}, 0
    for n in REPL:
        nr = (2 * D_FF // D_MODEL) if n == "ffn_conv_b" else 1
        out[n] = buf[r:r + nr].reshape(1, nr * D_MODEL)
        r += nr
    return out


def kernel(x, norm_mix_pre, norm_mix_post, norm_ffn_pre, norm_ffn_post, w_in, conv_short_w, w_conv_branch, lru_conv_w, lru_conv_b, lru_wa, lru_ba, lru_wx, lru_bx, lru_lambda, w_lru_branch, w_out, ffn_w_up, ffn_conv_w, ffn_conv_b, ffn_w_down, loss_target, m_norm_mix_pre, m_norm_mix_post, m_norm_ffn_pre, m_norm_ffn_post, m_w_in, m_conv_short_w, m_w_conv_branch, m_lru_conv_w, m_lru_conv_b, m_lru_wa, m_lru_ba, m_lru_wx, m_lru_bx, m_lru_lambda, m_w_lru_branch, m_w_out, m_ffn_w_up, m_ffn_conv_w, m_ffn_conv_b, m_ffn_w_down, v_norm_mix_pre, v_norm_mix_post, v_norm_ffn_pre, v_norm_ffn_post, v_w_in, v_conv_short_w, v_w_conv_branch, v_lru_conv_w, v_lru_conv_b, v_lru_wa, v_lru_ba, v_lru_wx, v_lru_bx, v_lru_lambda, v_w_lru_branch, v_w_out, v_ffn_w_up, v_ffn_conv_w, v_ffn_conv_b, v_ffn_w_down):
    given = dict(locals())
    w = {n: given[n] for n in WEIGHTS}
    m = {n: given["m_" + n] for n in WEIGHTS}
    v = {n: given["v_" + n] for n in WEIGHTS}

    xi, yi, ci = _position()
    chip_i = 2 * xi + yi
    chip = chip_i.astype(jnp.int32).reshape(1)
    core = ci.astype(jnp.int32).reshape(1)
    xs, target = x[0], loss_target[0]
    g1, g2, g3, g4 = w["norm_mix_pre"], w["norm_mix_post"], w["norm_ffn_pre"], w["norm_ffn_post"]
    shard = {n: w[n][0].astype(BF16) for n in BIG}
    small_shard = _pack_small_shard(w)

    def gathered(bufs, names):
        return [_own_slot(b, small_shard if n == "small" else shard[n], chip_i) for b, n in zip(bufs, names)]

    def chip_sums(arrays, kinds, tag):
        theirs = pair_split(arrays, kinds, "pair_split_" + tag)
        return [add_pair(g, k, t, core, "pair_add_%s_%d" % (tag, i)) for i, (g, k, t) in enumerate(zip(arrays, kinds, theirs))]

    h1 = norm_in(xs, g1)
    win4, small4 = gathered(run_ride(gather_ride([shard["w_in"], small_shard]), "gather_first"), ("w_in", "small"))
    small = _full_small(small4)
    (proj,), got = matmul_cols(h1, win4, "proj_fwd",
                               ride=gather_ride([shard["w_conv_branch"], shard["w_lru_branch"], shard["w_out"]]))
    wcb, wlb, wout = [g.reshape(-1, D_MODEL) for g in gathered(got, ("w_conv_branch", "w_lru_branch", "w_out"))]
    up_piece = lambda r0, nr, into=None: gather_ride([shard["ffn_w_up"]], items=[(0, r0, nr)], into=into)
    down_piece = lambda r0, nr, into=None: gather_ride([shard["ffn_w_down"]], items=[(0, r0, nr)], into=into)
    (q, ya), got = mix_conv_fwd(proj, small["conv_short_w"], ride=up_piece(0, 160))
    (xl, r, gi, h, yb), got = mix_lru_fwd(
        proj, small["lru_conv_w"], w["lru_conv_b"], small["lru_wa"].astype(BF16), small["lru_ba"],
        small["lru_wx"].astype(BF16), small["lru_bx"], w["lru_lambda"], ride=up_piece(160, 512, got))
    (a, b, merged), got = branch_merge_fwd(ya, yb, wcb, wlb, proj, ride=up_piece(672, 352, got))
    (wup4,) = gathered(got, ("ffn_w_up",))
    (mix, x2, h2), got = mix_out_fwd(merged, wout, xs, g2, g3, ride=down_piece(0, 256))
    (up, act, f), got = ffn_up_act_fwd(h2, wup4, small["ffn_conv_w"], w["ffn_conv_b"], ride=down_piece(256, 512, got))
    wdown = gathered(got, ("ffn_w_down",))[0].reshape(-1, D_MODEL)
    dy, dout, loss, dg4 = ffn_down_loss(f, wdown, x2, target, g4)

    dup, dwdown, dfw, dfb = ffn_bwd(dout, wdown, up, act, f, small["ffn_conv_w"])
    cs_down = chip_sums([dwdown], ["row"], "down")
    (dh2, dwup), rx_down = dgrad_wgrad_cols([dup], wup4, h2, "up_bwd", ride=exchange_ride(cs_down))
    cs_up = chip_sums([dwup], ["col"], "up")
    up_rows = lambda r0, nr, into=None: exchange_ride(cs_up, items=[(0, r0, nr)], into=into)
    (dx2, dmix, dg3, dg2), rx_up = norms_mid_bwd(dh2, x2, dy, mix, g3, g2, ride=up_rows(0, 96))
    (da, db, dwout, dgates), rx_up = mix_out_bwd(dmix, wout, merged, a, b, proj, ride=up_rows(96, 160, rx_up))
    (dconv, dwcb, dws), rx_up = mix_conv_bwd(da, wcb, proj, q, small["conv_short_w"], ride=up_rows(256, 160, rx_up))
    cs_mid = chip_sums([dwout, dwcb], ["row", "row"], "mid")
    (dlru, dwlb, dwa, dwx, dba, dbx, dwl, dbl, dlam), rx_mid = mix_lru_bwd(
        db, wlb, proj, xl, r, gi, h, small["lru_conv_w"], small["lru_wa"].astype(BF16), small["lru_wx"].astype(BF16),
        w["lru_lambda"], ride=exchange_ride(cs_up + cs_mid, items=[(0, 416, 96), (1, 0, 128), (2, 0, 128)],
                                            into=rx_up + [None, None]))
    rx_up, rx_mid = rx_mid[:1], rx_mid[1:]
    grads = dict(norm_mix_post=dg2, norm_ffn_pre=dg3, norm_ffn_post=dg4, conv_short_w=dws, lru_conv_w=dwl,
                 lru_conv_b=dbl, lru_wa=dwa, lru_ba=dba, lru_wx=dwx, lru_bx=dbx, lru_lambda=dlam,
                 ffn_conv_w=jnp.concatenate([dfw[0], dfw[1]], axis=1), ffn_conv_b=jnp.concatenate([dfb[0], dfb[1]], axis=1))
    cs_late = chip_sums([dwlb, _split_small(grads)], ["row", "lead"], "late")
    (dh1, dwin), rx_late = dgrad_wgrad_cols([dconv, dlru, dgates], win4, h1, "proj_bwd", ride=exchange_ride(cs_late))
    dx, grads["norm_mix_pre"] = norm_in_bwd(dh1, xs, dx2, g1)
    cs_in = chip_sums([dwin], ["col"], "in")
    rep_part = _pack_repl(grads, loss[0, 0])
    rx_in, rep_all = run_ride(exchange_ride(cs_in, rep=rep_part), "exchange_last")

    order = (("w_in", cs_in[0], rx_in), ("ffn_w_up", cs_up[0], rx_up[0]), ("w_conv_branch", cs_mid[1], rx_mid[1]),
             ("w_lru_branch", cs_late[0], rx_late[0]), ("w_out", cs_mid[0], rx_mid[0]),
             ("ffn_w_down", cs_down[0], rx_down[0]), ("small", cs_late[1], rx_late[1]))
    halves = [sum_chips(rx, cs, chip, "chip_sum_" + n) for n, cs, rx in order]
    me = 4 * xi + 2 * yi + ci
    rep_grad = sum_lead(_own_slot(rep_all, rep_part, me), "device_sum")
    others = pair_swap(halves)

    g_out, d_out, m_out, v_out = {}, {}, {}, {}
    for n, gm, go in zip(BIG, halves[:-1], others[:-1]):
        g, d, nm, nv = adamw_halves(w[n][0], gm, go, m[n][0], v[n][0], core, "adamw_" + n)
        g_out[n], d_out[n], m_out[n], v_out[n] = g[None], d[None], nm[None], nv[None]
    bufs = adamw_halves(small_shard, halves[-1], others[-1], _pack_small_shard(m), _pack_small_shard(v),
                        core, "adamw_small")
    for dst, buf in zip((g_out, d_out, m_out, v_out), bufs):
        dst.update(_unpack_small_shard(buf))
    d, nm, nv = adamw(_pack_repl(w), rep_grad, _pack_repl(m), _pack_repl(v), "adamw_repl")
    for dst, buf in ((g_out, rep_grad), (d_out, d), (m_out, nm), (v_out, nv)):
        dst.update(_unpack_repl(buf))

    return (rep_grad[LOSS_ROW, 0], dx[None], *[g_out[n] for n in WEIGHTS], *[d_out[n] for n in WEIGHTS],
            *[m_out[n] for n in WEIGHTS], *[v_out[n] for n in WEIGHTS])
```

```python
import functools
import math

import jax
import jax.numpy as jnp
from jax import lax
from jax.experimental import pallas as pl
from jax.experimental.pallas import tpu as pltpu

F32 = jnp.float32
BF16 = jnp.bfloat16

D_MODEL = 1024
N_CHIPS = 4
N_SEG = 7
D_FF = 3 * D_MODEL
LRU_HEADS = 4
HEAD_DIM = D_MODEL // LRU_HEADS
LRU_C = 8.0
RMS_EPS = 1e-6
CW = 256
FW = 256
SUBLANES = 8
VMEM_LIMIT = 58 * 1024 * 1024

ADAM_LR = 0.001
ADAM_B1 = 0.9
ADAM_B2 = 0.999
ADAM_EPS = 1e-08
ADAM_WD = 0.01
ADAM_STEP = 10

_GELU_C = math.sqrt(2.0 / math.pi)
_GELU_K = 0.044715


def _params(**kw):
    return pltpu.CompilerParams(vmem_limit_bytes=VMEM_LIMIT, **kw)


def _sigmoid(x):
    return 1.0 / (1.0 + jnp.exp(-x))


def _gelu(x):
    t = jnp.tanh(_GELU_C * (x + _GELU_K * x * x * x))
    return 0.5 * x * (1.0 + t)


def _gelu_and_grad(x):
    x2 = x * x
    t = jnp.tanh(_GELU_C * (x + _GELU_K * x * x2))
    g = 0.5 * x * (1.0 + t)
    dg = 0.5 * (1.0 + t) + 0.5 * x * (1.0 - t * t) * _GELU_C * (1.0 + 3.0 * _GELU_K * x2)
    return g, dg


def _log_sigmoid(x):
    e = jnp.exp(-jnp.abs(x))
    u = 1.0 + e
    l1p = jnp.where(u == 1.0, e, jnp.log(u) * e / (u - 1.0))
    return jnp.minimum(x, 0.0) - l1p


def _neg_expm1(z):
    series = -z * (1.0 + z * (0.5 + z * (1.0 / 6.0 + z * (1.0 / 24.0 + z * (1.0 / 120.0 + z * (1.0 / 720.0))))))
    return jnp.where(z > -0.2, series, 1.0 - jnp.exp(z))


def _rows(shape):
    return lax.broadcasted_iota(jnp.int32, shape, 0)


def _shift_down(x, k):
    return jnp.where(_rows(x.shape) >= k, pltpu.roll(x, k, 0), 0.0)


def _shift_up(x, k):
    n = x.shape[0]
    return jnp.where(_rows(x.shape) < n - k, pltpu.roll(x, n - k, 0), 0.0)


def _delays(x, k_width):
    return [x] + [_shift_down(x, j) for j in range(1, k_width)]


def _advances(dy, k_width):
    return [dy] + [_shift_up(dy, j) for j in range(1, k_width)]


def _taps_sum(shifted, w_ref, b=None):
    k_width = w_ref.shape[0]
    y = w_ref[k_width - 1:k_width, :] * shifted[0]
    for j in range(1, k_width):
        y = y + w_ref[k_width - 1 - j:k_width - j, :] * shifted[j]
    if b is not None:
        y = y + b
    return y


def _causal_conv(x, w_ref, b=None):
    return _taps_sum(_delays(x, w_ref.shape[0]), w_ref, b)


def _conv_wgrad(advanced, x):
    k_width = len(advanced)
    rows = [jnp.sum(advanced[k_width - 1 - k] * x, axis=0, keepdims=True) for k in range(k_width)]
    return jnp.concatenate(rows, axis=0)


def _dot(a, b):
    return jnp.dot(a, b, preferred_element_type=F32)


def _dot_nt(a, b):
    return lax.dot_general(a, b, (((1,), (1,)), ((), ())), preferred_element_type=F32)


def _dot_tn(a, b):
    return lax.dot_general(a, b, (((0,), (0,)), ((), ())), preferred_element_type=F32)


def _rms_stats(x):
    r = lax.rsqrt(jnp.mean(x * x, axis=-1, keepdims=True) + RMS_EPS)
    return x * r, r


def _rms_bwd(n, r, g, dy):
    dn = dy * g
    dx = r * (dn - n * jnp.mean(dn * n, axis=-1, keepdims=True))
    return dx, dy * n


def _scan_forward(a_ref, b_ref, h_ref):
    n, c = a_ref.shape
    row = lax.broadcasted_iota(jnp.int32, (SUBLANES, c), 0)

    def group(g, carry):
        r0 = pl.multiple_of(g * SUBLANES, SUBLANES)
        a = a_ref[pl.ds(r0, SUBLANES), :]
        b = b_ref[pl.ds(r0, SUBLANES), :]
        for k in (1, 2, 4):
            ap = jnp.where(row >= k, pltpu.roll(a, k, 0), 1.0)
            bp = jnp.where(row >= k, pltpu.roll(b, k, 0), 0.0)
            b = a * bp + b
            a = a * ap
        h = a * carry + b
        h_ref[pl.ds(r0, SUBLANES), :] = h
        return h[SUBLANES - 1:SUBLANES, :]

    lax.fori_loop(0, n // SUBLANES, group, jnp.zeros((1, c), F32))


def _scan_backward(c_ref, b_ref, g_ref):
    n, ch = c_ref.shape
    row = lax.broadcasted_iota(jnp.int32, (SUBLANES, ch), 0)
    n_groups = n // SUBLANES

    def group(i, carry):
        r0 = pl.multiple_of((n_groups - 1 - i) * SUBLANES, SUBLANES)
        a = c_ref[pl.ds(r0, SUBLANES), :]
        b = b_ref[pl.ds(r0, SUBLANES), :]
        for k in (1, 2, 4):
            keep = row < SUBLANES - k
            ap = jnp.where(keep, pltpu.roll(a, SUBLANES - k, 0), 1.0)
            bp = jnp.where(keep, pltpu.roll(b, SUBLANES - k, 0), 0.0)
            b = a * bp + b
            a = a * ap
        g = a * carry + b
        g_ref[pl.ds(r0, SUBLANES), :] = g
        return g[0:1, :]

    lax.fori_loop(0, n_groups, group, jnp.zeros((1, ch), F32))


MESH = pl.DeviceIdType.MESH
_HBM = pl.BlockSpec(memory_space=pltpu.HBM)
_OTHER_CHIPS = ((1, 0), (0, 1), (1, 1))
_OTHER_DEVICES = tuple((dx, dy, dc) for dx in (0, 1) for dy in (0, 1) for dc in (0, 1) if dx or dy or dc)
N_DEVICES = 8


def _position():
    return lax.axis_index("x"), lax.axis_index("y"), lax.axis_index("c")


def _flip(v, d):
    return 1 - v if d else v


def _chip(x, y, p):
    px, py = _flip(x, _OTHER_CHIPS[p][0]), _flip(y, _OTHER_CHIPS[p][1])
    return px, py, 2 * px + py


class _Ride:
    def __init__(self, srcs, bufs, scratch, plan):
        self.srcs, self.bufs, self.scratch, self.plan = list(srcs), list(bufs), list(scratch), plan


def _call(body, *, name, grid, in_specs, out_specs, out_shape, operands, scratch_shapes=(), ride=None):
    in_specs, out_specs, out_shape = list(in_specs), list(out_specs), list(out_shape)
    scratch_shapes = list(scratch_shapes)
    if ride is None:
        return pl.pallas_call(body, name=name, grid=grid, in_specs=in_specs, out_specs=out_specs, out_shape=out_shape,
                              scratch_shapes=scratch_shapes, compiler_params=_params())(*operands)
    n_in, n_out, n_scr = len(in_specs), len(out_shape), len(scratch_shapes)
    old = [i for i, b in enumerate(ride.bufs) if not isinstance(b, jax.ShapeDtypeStruct)]
    n_src, n_old, n_buf = len(ride.srcs), len(old), len(ride.bufs)

    def full_body(*refs):
        o0 = n_in + n_src + n_old
        s0 = o0 + n_out + n_buf
        start, relay, finish = ride.plan(refs[n_in:n_in + n_src], refs[o0 + n_out:s0], refs[s0 + n_scr:])
        ids = [pl.program_id(i) for i in range(len(grid))]
        first = functools.reduce(jnp.logical_and, [i == 0 for i in ids])
        last = functools.reduce(jnp.logical_and, [i == g - 1 for i, g in zip(ids, grid)])
        pl.when(first)(start)
        pl.when(last)(relay)
        body(*refs[:n_in], *refs[o0:o0 + n_out], *refs[s0:s0 + n_scr])
        pl.when(last)(finish)

    shapes = [jax.ShapeDtypeStruct(b.shape, b.dtype) for b in ride.bufs]
    res = pl.pallas_call(
        full_body, name=name, grid=grid,
        in_specs=in_specs + [_HBM] * (n_src + n_old), out_specs=out_specs + [_HBM] * n_buf,
        out_shape=out_shape + shapes, scratch_shapes=scratch_shapes + ride.scratch,
        input_output_aliases={n_in + n_src + k: n_out + i for k, i in enumerate(old)},
        compiler_params=_params(),
    )(*operands, *ride.srcs, *[ride.bufs[i] for i in old])
    return list(res[:n_out]), list(res[n_out:])


def run_ride(ride, name):
    def body():
        pass

    return _call(body, name=name, grid=(1,), in_specs=[], out_specs=[], out_shape=[], operands=[], ride=ride)[1]


def gather_ride(shards, items=None, into=None):
    items = items or [(a, 0, s.shape[0]) for a, s in enumerate(shards)]
    bufs = into or [jax.ShapeDtypeStruct((N_CHIPS,) + s.shape, s.dtype) for s in shards]
    nrel = len(_OTHER_CHIPS)

    def plan(srcs, dsts, sems):
        ici_send, ici_recv, sib_send, sib_recv = sems
        x, y, c = _position()
        j = 2 * x + y

        def rows(ref, it, h):
            return ref.at[pl.ds(it[1] + h * (it[2] // 2), it[2] // 2), :]

        def ici(i, p, slot):
            it = items[i]
            px, py, _ = _chip(x, y, p)
            return pltpu.make_async_remote_copy(
                src_ref=rows(srcs[it[0]], it, c), dst_ref=rows(dsts[it[0]].at[slot], it, c),
                send_sem=ici_send.at[i * nrel + p], recv_sem=ici_recv.at[i * nrel + p],
                device_id=(px, py, c), device_id_type=MESH)

        def sib(i, p, h):
            it = items[i]
            part = rows(dsts[it[0]].at[_chip(x, y, p)[2]], it, h)
            return pltpu.make_async_remote_copy(
                src_ref=part, dst_ref=part, send_sem=sib_send.at[i * nrel + p], recv_sem=sib_recv.at[i * nrel + p],
                device_id=(x, y, 1 - c), device_id_type=MESH)

        pairs = [(i, p) for i in range(len(items)) for p in range(nrel)]

        def start():
            for i, p in pairs:
                ici(i, p, j).start()

        def relay():
            for i, p in pairs:
                ici(i, p, _chip(x, y, p)[2]).wait_recv()
                sib(i, p, c).start()

        def finish():
            for i, p in pairs:
                sib(i, p, 1 - c).wait_recv()
            for i, p in pairs:
                ici(i, p, j).wait_send()
                sib(i, p, c).wait_send()

        return start, relay, finish

    return _Ride(shards, bufs, [pltpu.SemaphoreType.DMA((len(items) * nrel,))] * 4, plan)


def exchange_ride(sums, items=None, into=None, rep=None):
    items = items or [(a, 0, s.shape[1]) for a, s in enumerate(sums)]
    into = into or [None] * len(sums)
    bufs = [jax.ShapeDtypeStruct(s.shape, s.dtype) if b is None else b for s, b in zip(sums, into)]
    srcs = list(sums)
    scratch = [pltpu.SemaphoreType.DMA((len(items) * len(_OTHER_CHIPS),))] * 2
    if rep is not None:
        srcs.append(rep)
        bufs.append(jax.ShapeDtypeStruct((N_DEVICES,) + rep.shape, rep.dtype))
        scratch += [pltpu.SemaphoreType.DMA((len(_OTHER_DEVICES),))] * 2
    nrel = len(_OTHER_CHIPS)

    def plan(src_refs, dst_refs, sems):
        x, y, c = _position()
        j = 2 * x + y
        me = 4 * x + 2 * y + c

        def part(i, p, src_slot, dst_slot):
            a, r0, nr = items[i]
            px, py, _ = _chip(x, y, p)
            return pltpu.make_async_remote_copy(
                src_ref=src_refs[a].at[src_slot, pl.ds(r0, nr), :], dst_ref=dst_refs[a].at[dst_slot, pl.ds(r0, nr), :],
                send_sem=sems[0].at[i * nrel + p], recv_sem=sems[1].at[i * nrel + p],
                device_id=(px, py, c), device_id_type=MESH)

        def device(q):
            dx, dy, dc = _OTHER_DEVICES[q]
            return _flip(x, dx), _flip(y, dy), _flip(c, dc)

        def rep_copy(q, slot):
            return pltpu.make_async_remote_copy(
                src_ref=src_refs[-1], dst_ref=dst_refs[-1].at[slot], send_sem=sems[2].at[q], recv_sem=sems[3].at[q],
                device_id=device(q), device_id_type=MESH)

        pairs = [(i, p) for i in range(len(items)) for p in range(nrel)]
        others = range(len(_OTHER_DEVICES)) if rep is not None else ()

        def start():
            for i, p in pairs:
                part(i, p, _chip(x, y, p)[2], j).start()
            for q in others:
                rep_copy(q, me).start()

        def finish():
            for i, p in pairs:
                k = _chip(x, y, p)[2]
                part(i, p, k, k).wait_recv()
            for q in others:
                px, py, pc = device(q)
                rep_copy(q, 4 * px + 2 * py + pc).wait_recv()
            for i, p in pairs:
                part(i, p, _chip(x, y, p)[2], j).wait_send()
            for q in others:
                rep_copy(q, me).wait_send()

        return start, lambda: None, finish

    return _Ride(srcs, bufs, scratch, plan)


def _own_slot(buf, own, index):
    return lax.dynamic_update_slice(buf, own[None], (index,) + (0,) * own.ndim)


def _token_tile(s):
    return min(s, 512)


def norm_in(x, g):
    s, d = x.shape
    t = _token_tile(s)

    def body(x_ref, g_ref, o_ref):
        n, _ = _rms_stats(x_ref[...])
        o_ref[...] = (n * g_ref[...]).astype(BF16)

    return pl.pallas_call(
        body, name="norm_in", grid=(s // t,),
        in_specs=[pl.BlockSpec((t, d), lambda i: (i, 0)), pl.BlockSpec((1, d), lambda i: (0, 0))],
        out_specs=pl.BlockSpec((t, d), lambda i: (i, 0)),
        out_shape=jax.ShapeDtypeStruct((s, d), BF16),
        compiler_params=_params(),
    )(x, g)


def matmul_cols(a, w4, name, ride=None):
    m, k = a.shape
    nj, _, ns = w4.shape
    nb = ns // CW

    def body(a_ref, w_ref, o_ref):
        o_ref[...] = _dot(a_ref[...], w_ref[0])

    return _call(
        body, name=name, grid=(nj, nb),
        in_specs=[pl.BlockSpec((m, k), lambda j, b: (0, 0)),
                  pl.BlockSpec((1, k, CW), lambda j, b: (j, 0, b))],
        out_specs=[pl.BlockSpec((m, CW), lambda j, b: (0, j * nb + b))],
        out_shape=[jax.ShapeDtypeStruct((m, nj * ns), F32)],
        operands=(a, w4), ride=ride)


def mix_conv_fwd(proj, ws, ride=None):
    s = proj.shape[0]
    nblk = D_MODEL // CW

    def body(cb_ref, cc_ref, cx_ref, ws_ref, q_ref, ya_ref):
        q = _causal_conv(cc_ref[...] * cx_ref[...], ws_ref)
        q_ref[...] = q
        ya_ref[...] = (cb_ref[...] * q).astype(BF16)

    seg = lambda k: pl.BlockSpec((s, CW), lambda c, k=k: (0, k * nblk + c))
    return _call(
        body, name="mix_conv_fwd", grid=(nblk,),
        in_specs=[seg(0), seg(1), seg(2), pl.BlockSpec((3, CW), lambda c: (0, c))],
        out_specs=[pl.BlockSpec((s, CW), lambda c: (0, c))] * 2,
        out_shape=[jax.ShapeDtypeStruct((s, D_MODEL), F32), jax.ShapeDtypeStruct((s, D_MODEL), BF16)],
        operands=(proj, proj, proj, ws), ride=ride)


def _lru_gates(r, ls):
    log_a = LRU_C * r * ls
    a = jnp.exp(log_a)
    mult = jnp.sqrt(_neg_expm1(2.0 * log_a))
    mult = jnp.where(_rows(r.shape) == 0, 1.0, mult)
    return a, mult


def mix_lru_fwd(proj, wl, bl, wa, ba, wx, bx, lam, ride=None):
    s = proj.shape[0]
    nblk = D_MODEL // CW

    def body(lx_ref, ly_ref, wl_ref, bl_ref, wa_ref, ba_ref, wx_ref, bx_ref, lam_ref,
             xl_ref, r_ref, i_ref, h_ref, yb_ref, a_scr, u_scr):
        xl = _causal_conv(lx_ref[...], wl_ref, bl_ref[...])
        xlb = xl.astype(BF16)
        xl_ref[...] = xlb
        r = _sigmoid(_dot(xlb, wa_ref[0]) + ba_ref[...])
        i = _sigmoid(_dot(xlb, wx_ref[0]) + bx_ref[...])
        r_ref[...] = r.astype(BF16)
        i_ref[...] = i.astype(BF16)
        a, mult = _lru_gates(r, _log_sigmoid(lam_ref[...]))
        a_scr[...] = a
        u_scr[...] = mult * i * xl
        _scan_forward(a_scr, u_scr, h_ref)
        yb_ref[...] = (h_ref[...] * _gelu(ly_ref[...])).astype(BF16)

    blk = lambda k: pl.BlockSpec((s, CW), lambda c, k=k: (0, k * nblk + c))
    vec = pl.BlockSpec((1, CW), lambda c: (0, c))
    mat = pl.BlockSpec((1, CW, CW), lambda c: (c, 0, 0))
    out = pl.BlockSpec((s, CW), lambda c: (0, c))
    f = jax.ShapeDtypeStruct((s, D_MODEL), F32)
    hb = jax.ShapeDtypeStruct((s, D_MODEL), BF16)
    return _call(
        body, name="mix_lru_fwd", grid=(nblk,),
        in_specs=[blk(3), blk(4), pl.BlockSpec((4, CW), lambda c: (0, c)), vec, mat, vec, mat, vec, vec],
        out_specs=[out] * 5,
        out_shape=[hb, hb, hb, f, hb],
        scratch_shapes=[pltpu.VMEM((s, CW), F32), pltpu.VMEM((s, CW), F32)],
        operands=(proj, proj, wl, bl, wa, ba, wx, bx, lam), ride=ride)


def branch_merge_fwd(ya, yb, wcb, wlb, proj, ride=None):
    s = ya.shape[0]
    nblk = D_MODEL // CW

    def body(ya_ref, yb_ref, wcb_ref, wlb_ref, gc_ref, gl_ref, a_ref, b_ref, m_ref):
        a = _dot(ya_ref[...], wcb_ref[...])
        b = _dot(yb_ref[...], wlb_ref[...])
        a_ref[...] = a
        b_ref[...] = b
        m_ref[...] = (_sigmoid(gc_ref[...]) * a + _sigmoid(gl_ref[...]) * b).astype(BF16)

    res = pl.BlockSpec((s, D_MODEL), lambda n: (0, 0))
    wcol = pl.BlockSpec((D_MODEL, CW), lambda n: (0, n))
    blk = lambda k: pl.BlockSpec((s, CW), lambda n, k=k: (0, k * nblk + n))
    out = pl.BlockSpec((s, CW), lambda n: (0, n))
    f = jax.ShapeDtypeStruct((s, D_MODEL), F32)
    return _call(
        body, name="branch_merge_fwd", grid=(nblk,),
        in_specs=[res, res, wcol, wcol, blk(5), blk(6)],
        out_specs=[out] * 3,
        out_shape=[f, f, jax.ShapeDtypeStruct((s, D_MODEL), BF16)],
        operands=(ya, yb, wcb, wlb, proj, proj), ride=ride)


def mix_out_fwd(merged, wout, x, g2, g3, ride=None):
    s, d = x.shape
    t = _token_tile(s)

    def body(m_ref, w_ref, x_ref, g2_ref, g3_ref, mix_ref, x2_ref, h2_ref):
        mix = _dot(m_ref[...], w_ref[...])
        mix_ref[...] = mix
        n, _ = _rms_stats(mix)
        x2 = x_ref[...] + n * g2_ref[...]
        x2_ref[...] = x2
        n2, _ = _rms_stats(x2)
        h2_ref[...] = (n2 * g3_ref[...]).astype(BF16)

    tile = pl.BlockSpec((t, d), lambda i: (i, 0))
    vec = pl.BlockSpec((1, d), lambda i: (0, 0))
    f = jax.ShapeDtypeStruct((s, d), F32)
    return _call(
        body, name="mix_out_fwd", grid=(s // t,),
        in_specs=[tile, pl.BlockSpec((d, d), lambda i: (0, 0)), tile, vec, vec],
        out_specs=[tile] * 3,
        out_shape=[f, f, jax.ShapeDtypeStruct((s, d), BF16)],
        operands=(merged, wout, x, g2, g3), ride=ride)


def ffn_up_act_fwd(h2, wup4, fw, fb, ride=None):
    s, k = h2.shape
    ns = wup4.shape[2]
    per_chip = ns // CW
    nblk = D_FF // CW

    def body(h_ref, wg_ref, wv_ref, cg_ref, cv_ref, bg_ref, bv_ref, up_ref, act_ref, f_ref):
        h = h_ref[...]
        ug = _dot(h, wg_ref[0])
        uv = _dot(h, wv_ref[0])
        up_ref[0] = ug
        up_ref[1] = uv
        gate = _causal_conv(ug, cg_ref, bg_ref[...])
        val = _causal_conv(uv, cv_ref, bv_ref[...])
        act_ref[0] = gate.astype(BF16)
        act_ref[1] = val.astype(BF16)
        f_ref[...] = (_gelu(gate) * val).astype(BF16)

    wcols = lambda h: pl.BlockSpec((1, k, CW), lambda n, h=h: (n // per_chip + 2 * h, 0, n % per_chip))
    half = lambda h, rows: pl.BlockSpec((rows, CW), lambda n, h=h: (0, h * nblk + n))
    both = pl.BlockSpec((2, s, CW), lambda n: (0, 0, n))
    return _call(
        body, name="ffn_up_act_fwd", grid=(nblk,),
        in_specs=[pl.BlockSpec((s, k), lambda n: (0, 0)), wcols(0), wcols(1),
                  half(0, 3), half(1, 3), half(0, 1), half(1, 1)],
        out_specs=[both, both, pl.BlockSpec((s, CW), lambda n: (0, n))],
        out_shape=[jax.ShapeDtypeStruct((2, s, D_FF), F32), jax.ShapeDtypeStruct((2, s, D_FF), BF16),
                   jax.ShapeDtypeStruct((s, D_FF), BF16)],
        operands=(h2, wup4, wup4, fw, fw, fb, fb), ride=ride)


def ffn_down_loss(f, wdown, x2, target, g4):
    s, d = x2.shape
    t = _token_tile(s)

    def body(f_ref, w_ref, x2_ref, tg_ref, g4_ref, dy_ref, dout_ref, loss_ref, dg4_ref):
        @pl.when(pl.program_id(0) == 0)
        def _():
            loss_ref[...] = jnp.zeros_like(loss_ref)
            dg4_ref[...] = jnp.zeros_like(dg4_ref)

        out = _dot(f_ref[...], w_ref[...])
        n, r = _rms_stats(out)
        err = x2_ref[...] + n * g4_ref[...] - tg_ref[...]
        loss_ref[...] += jnp.full(loss_ref.shape, (0.5 / d) * jnp.sum(err * err), F32)
        dy = err * (1.0 / d)
        dy_ref[...] = dy
        dout, dg = _rms_bwd(n, r, g4_ref[...], dy)
        dout_ref[...] = dout.astype(BF16)
        dg4_ref[...] += jnp.sum(dg, axis=0, keepdims=True)

    tile = pl.BlockSpec((t, d), lambda i: (i, 0))
    vec = pl.BlockSpec((1, d), lambda i: (0, 0))
    return pl.pallas_call(
        body, name="ffn_down_loss", grid=(s // t,),
        in_specs=[pl.BlockSpec((t, D_FF), lambda i: (i, 0)), pl.BlockSpec((D_FF, d), lambda i: (0, 0)), tile, tile, vec],
        out_specs=[tile, tile, pl.BlockSpec((1, 128), lambda i: (0, 0)), vec],
        out_shape=[jax.ShapeDtypeStruct((s, d), F32), jax.ShapeDtypeStruct((s, d), BF16),
                   jax.ShapeDtypeStruct((1, 128), F32), jax.ShapeDtypeStruct((1, d), F32)],
        compiler_params=_params(),
    )(f, wdown, x2, target, g4)


def ffn_bwd(dout, wdown, up, act, f, fw):
    s = up.shape[1]
    nblk = D_FF // FW

    def body(do_ref, wd_ref, up_ref, act_ref, f_ref, wg_ref, wv_ref, dup_ref, dwd_ref, dw_ref, db_ref):
        do = do_ref[...]
        df = _dot_nt(do, wd_ref[...])
        dwd_ref[...] = _dot_tn(f_ref[...], do).astype(BF16)
        val = act_ref[1].astype(F32)
        ge, dge = _gelu_and_grad(act_ref[0].astype(F32))
        dgate = _advances(df * val * dge, 3)
        dval = _advances(df * ge, 3)
        dup_ref[0] = _taps_sum(dgate, wg_ref).astype(BF16)
        dup_ref[1] = _taps_sum(dval, wv_ref).astype(BF16)
        dw_ref[0] = _conv_wgrad(dgate, up_ref[0])
        dw_ref[1] = _conv_wgrad(dval, up_ref[1])
        db_ref[0] = jnp.sum(dgate[0], axis=0, keepdims=True)
        db_ref[1] = jnp.sum(dval[0], axis=0, keepdims=True)

    half = lambda h, rows: pl.BlockSpec((rows, FW), lambda n, h=h: (0, h * nblk + n))
    both = lambda rows: pl.BlockSpec((2, rows, FW), lambda n: (0, 0, n))
    return pl.pallas_call(
        body, name="ffn_bwd", grid=(nblk,),
        in_specs=[pl.BlockSpec((s, D_MODEL), lambda n: (0, 0)), pl.BlockSpec((FW, D_MODEL), lambda n: (n, 0)),
                  both(s), both(s), pl.BlockSpec((s, FW), lambda n: (0, n)), half(0, 3), half(1, 3)],
        out_specs=[both(s), pl.BlockSpec((FW, D_MODEL), lambda n: (n, 0)), both(3), both(1)],
        out_shape=[jax.ShapeDtypeStruct((2, s, D_FF), BF16), jax.ShapeDtypeStruct((D_FF, D_MODEL), BF16),
                   jax.ShapeDtypeStruct((2, 3, D_FF), F32), jax.ShapeDtypeStruct((2, 1, D_FF), F32)],
        compiler_params=_params(),
    )(dout, wdown, up, act, f, fw, fw)


def ffn_up_bwd(dout, wdown, up, act, f, fw, wup4, h2, ride=None):
    s, k = h2.shape
    nblk = D_FF // FW
    per_chip = wup4.shape[2] // FW

    def body(do_ref, wd_ref, up_ref, act_ref, f_ref, cg_ref, cv_ref, wg_ref, wv_ref, h_ref,
             dh_ref, dwu_ref, dwd_ref, dw_ref, db_ref, dup_scr):
        @pl.when(pl.program_id(0) == 0)
        def _():
            dup_scr[...] = jnp.zeros_like(dup_scr)
            dh_ref[...] = jnp.zeros_like(dh_ref)

        dg = dup_scr[0]
        dv = dup_scr[1]
        h = h_ref[...]
        dh_ref[...] += _dot_nt(dg, wg_ref[0]) + _dot_nt(dv, wv_ref[0])
        dwu_ref[0] = _dot_tn(h, dg).astype(BF16)
        dwu_ref[1] = _dot_tn(h, dv).astype(BF16)
        do = do_ref[...]
        df = _dot_nt(do, wd_ref[...])
        dwd_ref[...] = _dot_tn(f_ref[...], do).astype(BF16)
        val = act_ref[1].astype(F32)
        ge, dge = _gelu_and_grad(act_ref[0].astype(F32))
        dgate = _advances(df * val * dge, 3)
        dval = _advances(df * ge, 3)
        dw_ref[0] = _conv_wgrad(dgate, up_ref[0])
        dw_ref[1] = _conv_wgrad(dval, up_ref[1])
        db_ref[0] = jnp.sum(dgate[0], axis=0, keepdims=True)
        db_ref[1] = jnp.sum(dval[0], axis=0, keepdims=True)
        dup_scr[0] = _taps_sum(dgate, cg_ref).astype(BF16)
        dup_scr[1] = _taps_sum(dval, cv_ref).astype(BF16)

    cur = lambda n: jnp.minimum(n, nblk - 1)
    prev = lambda n: jnp.maximum(n - 1, 0)
    once = pl.Buffered(1)
    both = lambda rows: pl.BlockSpec((2, rows, FW), lambda n: (0, 0, cur(n)))
    taps = lambda h: pl.BlockSpec((3, FW), lambda n, h=h: (0, h * nblk + cur(n)))
    wcols = lambda h: pl.BlockSpec((1, k, FW), lambda n, h=h: (prev(n) // per_chip + 2 * h, 0, prev(n) % per_chip))
    return _call(
        body, name="ffn_up_bwd", grid=(nblk + 1,),
        in_specs=[pl.BlockSpec((s, D_MODEL), lambda n: (0, 0), pipeline_mode=once),
                  pl.BlockSpec((FW, D_MODEL), lambda n: (cur(n), 0)), both(s), both(s),
                  pl.BlockSpec((s, FW), lambda n: (0, cur(n))), taps(0), taps(1), wcols(0), wcols(1),
                  pl.BlockSpec((s, k), lambda n: (0, 0), pipeline_mode=once)],
        out_specs=[pl.BlockSpec((s, k), lambda n: (0, 0), pipeline_mode=once),
                   pl.BlockSpec((2, k, FW), lambda n: (0, 0, prev(n))),
                   pl.BlockSpec((FW, D_MODEL), lambda n: (cur(n), 0)), both(3), both(1)],
        out_shape=[jax.ShapeDtypeStruct((s, k), F32), jax.ShapeDtypeStruct((2, k, D_FF), BF16),
                   jax.ShapeDtypeStruct((D_FF, D_MODEL), BF16),
                   jax.ShapeDtypeStruct((2, 3, D_FF), F32), jax.ShapeDtypeStruct((2, 1, D_FF), F32)],
        scratch_shapes=[pltpu.VMEM((2, s, FW), BF16)],
        operands=(dout, wdown, up, act, f, fw, fw, wup4, wup4, h2), ride=ride)


def dgrad_wgrad_cols(dy, w4, a, name, ride=None):
    m, k = a.shape
    nj, _, ns = w4.shape
    nb = ns // CW
    per_seg = dy[0].shape[2] // CW
    first = [sum(d.shape[0] for d in dy[:i]) for i in range(len(dy))]

    def segment(j, b):
        return (j * nb + b) // per_seg, (j * nb + b) % per_seg

    def body(*refs):
        dy_refs, (w_ref, a_ref, da_ref, dw_ref) = refs[:len(dy)], refs[len(dy):]

        @pl.when((pl.program_id(0) == 0) & (pl.program_id(1) == 0))
        def _():
            da_ref[...] = jnp.zeros_like(da_ref)

        seg, _ = segment(pl.program_id(0), pl.program_id(1))
        dyb = dy_refs[-1][0]
        for i in range(len(dy) - 2, -1, -1):
            dyb = jnp.where(seg < first[i + 1], dy_refs[i][0], dyb)
        da_ref[...] += _dot_nt(dyb, w_ref[0])
        dw_ref[...] = _dot_tn(a_ref[...], dyb).astype(BF16)

    def dy_spec(i):
        nseg = dy[i].shape[0]

        def index(j, b):
            seg, col = segment(j, b)
            local = seg - first[i]
            return (jnp.clip(local, 0, nseg - 1), 0,
                    jnp.where(local < 0, 0, jnp.where(local >= nseg, per_seg - 1, col)))

        return pl.BlockSpec((1, m, CW), index)

    return _call(
        body, name=name, grid=(nj, nb),
        in_specs=[dy_spec(i) for i in range(len(dy))]
        + [pl.BlockSpec((1, k, CW), lambda j, b: (j, 0, b)), pl.BlockSpec((m, k), lambda j, b: (0, 0))],
        out_specs=[pl.BlockSpec((m, k), lambda j, b: (0, 0)),
                   pl.BlockSpec((k, CW), lambda j, b: (0, j * nb + b))],
        out_shape=[jax.ShapeDtypeStruct((m, k), F32), jax.ShapeDtypeStruct((k, nj * ns), BF16)],
        operands=(*dy, w4, a), ride=ride)


def norms_mid_bwd(dh2, x2, dy, mix, g3, g2, ride=None):
    s, d = x2.shape
    t = _token_tile(s)

    def body(dh2_ref, x2_ref, dy_ref, mix_ref, g3_ref, g2_ref, dx2_ref, dmix_ref, dg3_ref, dg2_ref):
        @pl.when(pl.program_id(0) == 0)
        def _():
            dg3_ref[...] = jnp.zeros_like(dg3_ref)
            dg2_ref[...] = jnp.zeros_like(dg2_ref)

        n3, r3 = _rms_stats(x2_ref[...])
        dx, dg3 = _rms_bwd(n3, r3, g3_ref[...], dh2_ref[...])
        dx2 = dy_ref[...] + dx
        dx2_ref[...] = dx2
        dg3_ref[...] += jnp.sum(dg3, axis=0, keepdims=True)
        n2, r2 = _rms_stats(mix_ref[...])
        dmix, dg2 = _rms_bwd(n2, r2, g2_ref[...], dx2)
        dmix_ref[...] = dmix.astype(BF16)
        dg2_ref[...] += jnp.sum(dg2, axis=0, keepdims=True)

    tile = pl.BlockSpec((t, d), lambda i: (i, 0))
    vec = pl.BlockSpec((1, d), lambda i: (0, 0))
    v = jax.ShapeDtypeStruct((1, d), F32)
    return _call(
        body, name="norms_mid_bwd", grid=(s // t,),
        in_specs=[tile, tile, tile, tile, vec, vec],
        out_specs=[tile, tile, vec, vec],
        out_shape=[jax.ShapeDtypeStruct((s, d), F32), jax.ShapeDtypeStruct((s, d), BF16), v, v],
        operands=(dh2, x2, dy, mix, g3, g2), ride=ride)


def mix_out_bwd(dmix, wout, merged, a, b, proj, ride=None):
    s = dmix.shape[0]
    nblk = D_MODEL // CW

    def body(dm_ref, w_ref, mg_ref, a_ref, b_ref, gc_ref, gl_ref, da_ref, db_ref, dw_ref, dg_ref):
        dm = dm_ref[...]
        dmerged = _dot_nt(dm, w_ref[...])
        dw_ref[...] = _dot_tn(mg_ref[...], dm).astype(BF16)
        sc = _sigmoid(gc_ref[...])
        sl = _sigmoid(gl_ref[...])
        da_ref[...] = (dmerged * sc).astype(BF16)
        db_ref[...] = (dmerged * sl).astype(BF16)
        dg_ref[0] = (dmerged * a_ref[...] * sc * (1.0 - sc)).astype(BF16)
        dg_ref[1] = (dmerged * b_ref[...] * sl * (1.0 - sl)).astype(BF16)

    res = pl.BlockSpec((s, D_MODEL), lambda n: (0, 0))
    rows = pl.BlockSpec((CW, D_MODEL), lambda n: (n, 0))
    col = pl.BlockSpec((s, CW), lambda n: (0, n))
    blk = lambda k: pl.BlockSpec((s, CW), lambda n, k=k: (0, k * nblk + n))
    hb = jax.ShapeDtypeStruct((s, D_MODEL), BF16)
    return _call(
        body, name="mix_out_bwd", grid=(nblk,),
        in_specs=[res, rows, col, col, col, blk(5), blk(6)],
        out_specs=[col, col, rows, pl.BlockSpec((2, s, CW), lambda n: (0, 0, n))],
        out_shape=[hb, hb, jax.ShapeDtypeStruct((D_MODEL, D_MODEL), BF16), jax.ShapeDtypeStruct((2, s, D_MODEL), BF16)],
        operands=(dmix, wout, merged, a, b, proj, proj), ride=ride)


def mix_conv_bwd(da, wcb, proj, q, ws, ride=None):
    s = da.shape[0]
    nblk = D_MODEL // CW

    def body(da_ref, w_ref, cb_ref, cc_ref, cx_ref, q_ref, ws_ref, dc_ref, dw_ref, dws_ref):
        dab = da_ref[...]
        dya = _dot_nt(dab, w_ref[...])
        cb = cb_ref[...]
        cc = cc_ref[...]
        cx = cx_ref[...]
        q = q_ref[...]
        dw_ref[...] = _dot_tn((cb * q).astype(BF16), dab).astype(BF16)
        dc_ref[0] = (dya * q).astype(BF16)
        dq = _advances(dya * cb, 3)
        dp = _taps_sum(dq, ws_ref)
        dws_ref[...] = _conv_wgrad(dq, cc * cx)
        dc_ref[1] = (dp * cx).astype(BF16)
        dc_ref[2] = (dp * cc).astype(BF16)

    res = pl.BlockSpec((s, D_MODEL), lambda n: (0, 0))
    rows = pl.BlockSpec((CW, D_MODEL), lambda n: (n, 0))
    col = pl.BlockSpec((s, CW), lambda n: (0, n))
    blk = lambda k: pl.BlockSpec((s, CW), lambda n, k=k: (0, k * nblk + n))
    taps = pl.BlockSpec((3, CW), lambda n: (0, n))
    hb = jax.ShapeDtypeStruct((s, D_MODEL), BF16)
    return _call(
        body, name="mix_conv_bwd", grid=(nblk,),
        in_specs=[res, rows, blk(0), blk(1), blk(2), col, taps],
        out_specs=[pl.BlockSpec((3, s, CW), lambda n: (0, 0, n)), rows, taps],
        out_shape=[jax.ShapeDtypeStruct((3, s, D_MODEL), BF16), jax.ShapeDtypeStruct((D_MODEL, D_MODEL), BF16),
                   jax.ShapeDtypeStruct((3, D_MODEL), F32)],
        operands=(da, wcb, proj, proj, proj, q, ws), ride=ride)


def mix_lru_bwd(db, wlb, proj, xl, r, i, h, wl, wa, wx, lam, ride=None):
    s = db.shape[0]
    nblk = D_MODEL // CW

    def body(db_ref, w_ref, lx_ref, ly_ref, xl_ref, r_ref, i_ref, h_ref, wl_ref, wa_ref, wx_ref, lam_ref,
             dl_ref, dw_ref, dwa_ref, dwx_ref, dba_ref, dbx_ref, dwl_ref, dbl_ref, dlam_ref,
             c_scr, g_scr):
        dbb = db_ref[...]
        dyb = _dot_nt(dbb, w_ref[...])
        h = h_ref[...]
        ge, dge = _gelu_and_grad(ly_ref[...])
        dw_ref[...] = _dot_tn((h * ge).astype(BF16), dbb).astype(BF16)
        dl_ref[1] = (dyb * h * dge).astype(BF16)
        r = r_ref[...].astype(F32)
        gi = i_ref[...].astype(F32)
        xlb = xl_ref[...]
        xl = xlb.astype(F32)
        lam = lam_ref[...]
        ls = _log_sigmoid(lam)
        a, mult = _lru_gates(r, ls)
        c_scr[...] = _shift_up(a, 1)
        g_scr[...] = dyb * ge
        _scan_backward(c_scr, g_scr, g_scr)
        du = g_scr[...]
        da = du * _shift_down(h, 1)
        dmult = du * gi * xl
        di = du * mult * xl
        dxl = du * mult * gi
        first = _rows(a.shape) == 0
        dlog_a = da * a - jnp.where(first, 0.0, dmult * a * a / mult)
        dr = dlog_a * (LRU_C * ls)
        dlam_ref[...] = jnp.sum(dlog_a * r, axis=0, keepdims=True) * (LRU_C * (1.0 - _sigmoid(lam)))
        dzr = dr * r * (1.0 - r)
        dzi = di * gi * (1.0 - gi)
        dba_ref[...] = jnp.sum(dzr, axis=0, keepdims=True)
        dbx_ref[...] = jnp.sum(dzi, axis=0, keepdims=True)
        dzrb = dzr.astype(BF16)
        dzib = dzi.astype(BF16)
        dwa_ref[0] = _dot_tn(xlb, dzrb)
        dwx_ref[0] = _dot_tn(xlb, dzib)
        dxl = _advances(dxl + _dot_nt(dzrb, wa_ref[0]) + _dot_nt(dzib, wx_ref[0]), 4)
        dl_ref[0] = _taps_sum(dxl, wl_ref).astype(BF16)
        dwl_ref[...] = _conv_wgrad(dxl, lx_ref[...])
        dbl_ref[...] = jnp.sum(dxl[0], axis=0, keepdims=True)

    res = pl.BlockSpec((s, D_MODEL), lambda n: (0, 0), pipeline_mode=pl.Buffered(1))
    rows = pl.BlockSpec((CW, D_MODEL), lambda n: (n, 0))
    col = pl.BlockSpec((s, CW), lambda n: (0, n))
    blk = lambda k: pl.BlockSpec((s, CW), lambda n, k=k: (0, k * nblk + n))
    taps = pl.BlockSpec((4, CW), lambda n: (0, n))
    vec = pl.BlockSpec((1, CW), lambda n: (0, n))
    mat = pl.BlockSpec((1, CW, CW), lambda n: (n, 0, 0))
    hb = jax.ShapeDtypeStruct((s, D_MODEL), BF16)
    v = jax.ShapeDtypeStruct((1, D_MODEL), F32)
    m = jax.ShapeDtypeStruct((LRU_HEADS, HEAD_DIM, HEAD_DIM), F32)
    scr = pltpu.VMEM((s, CW), F32)
    return _call(
        body, name="mix_lru_bwd", grid=(nblk,),
        in_specs=[res, rows, blk(3), blk(4), col, col, col, col, taps, mat, mat, vec],
        out_specs=[pl.BlockSpec((2, s, CW), lambda n: (0, 0, n)), rows, mat, mat, vec, vec, taps, vec, vec],
        out_shape=[jax.ShapeDtypeStruct((2, s, D_MODEL), BF16), jax.ShapeDtypeStruct((D_MODEL, D_MODEL), BF16), m, m, v, v,
                   jax.ShapeDtypeStruct((4, D_MODEL), F32), v, v],
        scratch_shapes=[scr, scr],
        operands=(db, wlb, proj, proj, xl, r, i, h, wl, wa, wx, lam), ride=ride)


def norm_in_bwd(dh1, x, dx2, g1):
    s, d = x.shape
    t = _token_tile(s)

    def body(dh_ref, x_ref, dx2_ref, g_ref, dx_ref, dg_ref):
        @pl.when(pl.program_id(0) == 0)
        def _():
            dg_ref[...] = jnp.zeros_like(dg_ref)

        n, r = _rms_stats(x_ref[...])
        dx, dg = _rms_bwd(n, r, g_ref[...], dh_ref[...])
        dx_ref[...] = dx2_ref[...] + dx
        dg_ref[...] += jnp.sum(dg, axis=0, keepdims=True)

    tile = pl.BlockSpec((t, d), lambda i: (i, 0))
    vec = pl.BlockSpec((1, d), lambda i: (0, 0))
    return pl.pallas_call(
        body, name="norm_in_bwd", grid=(s // t,),
        in_specs=[tile, tile, tile, vec],
        out_specs=[tile, vec],
        out_shape=[jax.ShapeDtypeStruct((s, d), F32), jax.ShapeDtypeStruct((1, d), F32)],
        compiler_params=_params(),
    )(dh1, x, dx2, g1)


def local_step(x, target, g1, g2, g3, g4, win4, ws, wcb, wl, bl, wa, ba, wx, bx, lam, wlb, wout, wup4, fw, fb, wdown):
    h1 = norm_in(x, g1)
    proj = matmul_cols(h1, win4, "proj_fwd")
    q, ya = mix_conv_fwd(proj, ws)
    xl, r, gi, h, yb = mix_lru_fwd(proj, wl, bl, wa, ba, wx, bx, lam)
    a, b, merged = branch_merge_fwd(ya, yb, wcb, wlb, proj)
    mix, x2, h2 = mix_out_fwd(merged, wout, x, g2, g3)
    up = matmul_cols(h2, wup4, "up_fwd")
    f = ffn_act_fwd(up, fw, fb)
    dy, dout, loss, dg4 = ffn_down_loss(f, wdown, x2, target, g4)

    dug, duv, dwdown, dfw_g, dfw_v, dfb_g, dfb_v = ffn_bwd(dout, wdown, up, fw, fb)
    dup = jnp.concatenate([dug, duv], axis=1)
    dfw = jnp.concatenate([dfw_g, dfw_v], axis=1)
    dfb = jnp.concatenate([dfb_g, dfb_v], axis=1)
    dh2, dwup = dgrad_wgrad_cols(dup, wup4, h2, "up_bwd")
    dx2, dmix, dg3, dg2 = norms_mid_bwd(dh2, x2, dy, mix, g3, g2)
    da, db, dwout, dgc, dgl = mix_out_bwd(dmix, wout, merged, a, b, proj)
    dcb, dcc, dcx, dwcb, dws = mix_conv_bwd(da, wcb, proj, q, ws)
    dlx, dly, dwlb, dwa, dwx, dba, dbx, dwl, dbl, dlam = mix_lru_bwd(db, wlb, proj, xl, r, gi, h, wl, wa, wx, lam)
    dproj = jnp.concatenate([dcb, dcc, dcx, dlx, dly, dgc[:, 5 * D_MODEL:6 * D_MODEL], dgl[:, 6 * D_MODEL:]], axis=1)
    dh1, dwin = dgrad_wgrad_cols(dproj, win4, h1, "proj_bwd")
    dx, dg1 = norm_in_bwd(dh1, x, dx2, g1)
    grads = dict(norm_mix_pre=dg1, norm_mix_post=dg2, norm_ffn_pre=dg3, norm_ffn_post=dg4,
                 w_in=dwin, conv_short_w=dws, w_conv_branch=dwcb, lru_conv_w=dwl, lru_conv_b=dbl,
                 lru_wa=dwa, lru_ba=dba, lru_wx=dwx, lru_bx=dbx, lru_lambda=dlam,
                 w_lru_branch=dwlb, w_out=dwout, ffn_w_up=dwup, ffn_conv_w=dfw, ffn_conv_b=dfb,
                 ffn_w_down=dwdown)
    return loss[0, 0], dx, grads


MESH = pl.DeviceIdType.MESH
_HBM = pl.BlockSpec(memory_space=pltpu.HBM)
_OTHER_CHIPS = ((1, 0), (0, 1), (1, 1))
_OTHER_DEVICES = tuple((dx, dy, dc) for dx in (0, 1) for dy in (0, 1) for dc in (0, 1) if dx or dy or dc)
N_DEVICES = 8


def _position():
    return lax.axis_index("x"), lax.axis_index("y"), lax.axis_index("c")


def _flip(v, d):
    return 1 - v if d else v


def _half_rows(ref, h, hr):
    return ref.at[pl.ds(h * hr, hr), :]


def gather_chips(shards):
    n = len(shards)
    nrel = len(_OTHER_CHIPS)

    def body(*refs):
        ins, outs = refs[:n], refs[n:2 * n]
        ici_send, ici_recv, sib_send, sib_recv = refs[2 * n:]
        x, y, c = _position()
        j = 2 * x + y
        hr = [s.shape[0] // 2 for s in shards]

        def chip(p):
            px, py = _flip(x, _OTHER_CHIPS[p][0]), _flip(y, _OTHER_CHIPS[p][1])
            return px, py, 2 * px + py

        def ici(a, p, slot):
            px, py, _ = chip(p)
            return pltpu.make_async_remote_copy(
                src_ref=_half_rows(ins[a], c, hr[a]), dst_ref=_half_rows(outs[a].at[slot], c, hr[a]),
                send_sem=ici_send.at[a * nrel + p], recv_sem=ici_recv.at[a * nrel + p],
                device_id=(px, py, c), device_id_type=MESH)

        def sib(a, p, h):
            _, _, k = chip(p)
            part = _half_rows(outs[a].at[k], h, hr[a])
            return pltpu.make_async_remote_copy(
                src_ref=part, dst_ref=part, send_sem=sib_send.at[a * nrel + p], recv_sem=sib_recv.at[a * nrel + p],
                device_id=(x, y, 1 - c), device_id_type=MESH)

        pairs = [(a, p) for a in range(n) for p in range(nrel)]
        for a, p in pairs:
            ici(a, p, j).start()
        for a, p in pairs:
            ici(a, p, chip(p)[2]).wait_recv()
            sib(a, p, c).start()
        for a, p in pairs:
            sib(a, p, 1 - c).wait_recv()
        for a, p in pairs:
            ici(a, p, j).wait_send()
            sib(a, p, c).wait_send()

    got = pl.pallas_call(
        body, name="gather_chips",
        in_specs=[_HBM] * n, out_specs=[_HBM] * n,
        out_shape=[jax.ShapeDtypeStruct((N_CHIPS,) + s.shape, s.dtype) for s in shards],
        scratch_shapes=[pltpu.SemaphoreType.DMA((n * nrel,))] * 4,
    )(*shards)
    j = 2 * lax.axis_index("x") + lax.axis_index("y")
    return [lax.dynamic_update_slice(g, s[None], (j, 0, 0)) for g, s in zip(got, shards)]


def _owned_part(ref, kind, k, h, hr):
    if kind == "col":
        ns = ref.shape[1] // N_CHIPS
        return ref.at[pl.ds(h * hr, hr), pl.ds(k * ns, ns)]
    if kind == "row":
        return ref.at[pl.ds(k * 2 * hr + h * hr, hr), :]
    if kind == "col2":
        ns = ref.shape[2] // 2
        return ref.at[k // 2, pl.ds(h * hr, hr), pl.ds((k % 2) * ns, ns)]
    return ref.at[k, pl.ds(h * hr, hr), :]


def _part_shape(g, kind):
    if kind == "col2":
        return g.shape[1] // 2, g.shape[2] // 2
    if kind == "col":
        return g.shape[0] // 2, g.shape[1] // N_CHIPS
    if kind == "row":
        return g.shape[0] // (2 * N_CHIPS), g.shape[1]
    return g.shape[1] // 2, g.shape[2]


def pair_split(grads, kinds, name):
    n = len(grads)
    shapes = [_part_shape(g, k) for g, k in zip(grads, kinds)]

    def body(*refs):
        ins, theirs = refs[:n], refs[n:2 * n]
        send_sem, recv_sem = refs[2 * n:]
        x, y, c = _position()
        copies = []
        for a in range(n):
            hr = shapes[a][0]
            for k in range(N_CHIPS):
                s = a * N_CHIPS + k
                copies.append(pltpu.make_async_remote_copy(
                    src_ref=_owned_part(ins[a], kinds[a], k, 1 - c, hr), dst_ref=theirs[a].at[k],
                    send_sem=send_sem.at[s], recv_sem=recv_sem.at[s], device_id=(x, y, 1 - c), device_id_type=MESH))
        for cp in copies:
            cp.start()
        for cp in copies:
            cp.wait()

    return pl.pallas_call(
        body, name=name,
        in_specs=[_HBM] * n, out_specs=[_HBM] * n,
        out_shape=[jax.ShapeDtypeStruct((N_CHIPS,) + shp, g.dtype) for shp, g in zip(shapes, grads)],
        scratch_shapes=[pltpu.SemaphoreType.DMA((n * N_CHIPS,))] * 2,
    )(*grads)


def chip_exchange(sums, rep):
    n = len(sums)
    nrel = len(_OTHER_CHIPS)
    ndev = len(_OTHER_DEVICES)

    def body(*refs):
        ins, rep_ref = refs[:n], refs[n]
        outs, rep_out = refs[n + 1:2 * n + 1], refs[2 * n + 1]
        loc_sem, send_sem, recv_sem, rep_send, rep_recv = refs[2 * n + 2:]
        x, y, c = _position()
        j = 2 * x + y
        me = 4 * x + 2 * y + c

        def chip(p):
            px, py = _flip(x, _OTHER_CHIPS[p][0]), _flip(y, _OTHER_CHIPS[p][1])
            return px, py, 2 * px + py

        def part(a, p, src_slot, dst_slot):
            px, py, _ = chip(p)
            return pltpu.make_async_remote_copy(
                src_ref=ins[a].at[src_slot], dst_ref=outs[a].at[dst_slot],
                send_sem=send_sem.at[a * nrel + p], recv_sem=recv_sem.at[a * nrel + p],
                device_id=(px, py, c), device_id_type=MESH)

        def device(q):
            dx, dy, dc = _OTHER_DEVICES[q]
            return _flip(x, dx), _flip(y, dy), _flip(c, dc)

        def rep_copy(q, slot):
            return pltpu.make_async_remote_copy(
                src_ref=rep_ref, dst_ref=rep_out.at[slot], send_sem=rep_send.at[q], recv_sem=rep_recv.at[q],
                device_id=device(q), device_id_type=MESH)

        own = [pltpu.make_async_copy(ins[a].at[j], outs[a].at[j], loc_sem.at[a]) for a in range(n)]
        own.append(pltpu.make_async_copy(rep_ref, rep_out.at[me], loc_sem.at[n]))
        for cp in own:
            cp.start()
        pairs = [(a, p) for a in range(n) for p in range(nrel)]
        for a, p in pairs:
            part(a, p, chip(p)[2], j).start()
        for q in range(ndev):
            rep_copy(q, me).start()
        for a, p in pairs:
            part(a, p, chip(p)[2], chip(p)[2]).wait_recv()
        for q in range(ndev):
            px, py, pc = device(q)
            rep_copy(q, 4 * px + 2 * py + pc).wait_recv()
        for a, p in pairs:
            part(a, p, chip(p)[2], j).wait_send()
        for q in range(ndev):
            rep_copy(q, me).wait_send()
        for cp in own:
            cp.wait()

    return pl.pallas_call(
        body, name="chip_exchange",
        in_specs=[_HBM] * (n + 1), out_specs=[_HBM] * (n + 1),
        out_shape=[jax.ShapeDtypeStruct(s.shape, s.dtype) for s in sums]
        + [jax.ShapeDtypeStruct((N_DEVICES,) + rep.shape, rep.dtype)],
        scratch_shapes=[pltpu.SemaphoreType.DMA((n + 1,)), pltpu.SemaphoreType.DMA((n * nrel,)),
                        pltpu.SemaphoreType.DMA((n * nrel,)), pltpu.SemaphoreType.DMA((ndev,)),
                        pltpu.SemaphoreType.DMA((ndev,))],
    )(*sums, rep)


def pair_swap(halves):
    n = len(halves)

    def body(*refs):
        ins, outs = refs[:n], refs[n:2 * n]
        send_sem, recv_sem = refs[2 * n:]
        x, y, c = _position()
        copies = [pltpu.make_async_remote_copy(
            src_ref=ins[a], dst_ref=outs[a], send_sem=send_sem.at[a], recv_sem=recv_sem.at[a],
            device_id=(x, y, 1 - c), device_id_type=MESH) for a in range(n)]
        for cp in copies:
            cp.start()
        for cp in copies:
            cp.wait()

    return pl.pallas_call(
        body, name="pair_swap",
        in_specs=[_HBM] * n, out_specs=[_HBM] * n,
        out_shape=[jax.ShapeDtypeStruct(h.shape, h.dtype) for h in halves],
        scratch_shapes=[pltpu.SemaphoreType.DMA((n,))] * 2,
    )(*halves)


def _row_tile(rows, cols, limit_bytes=1 << 20):
    best = None
    for t in range(SUBLANES, rows + 1, SUBLANES):
        if rows % t == 0 and t * cols * 4 <= limit_bytes:
            best = t
    return best or rows


def add_pair(g, kind, theirs, core, name):
    nc, rows, cols = theirs.shape
    t = _row_tile(rows, cols, 4 << 20)
    nt = rows // t

    def body(core_ref, g_ref, b_ref, o_ref):
        mine = g_ref[...].reshape(t, cols)
        o_ref[0] = (mine.astype(F32) + b_ref[0].astype(F32)).astype(o_ref.dtype)

    if kind == "col":
        own = pl.BlockSpec((t, cols), lambda k, i, c: (c[0] * nt + i, k))
    elif kind == "col2":
        own = pl.BlockSpec((1, t, cols), lambda k, i, c: (k // 2, c[0] * nt + i, k % 2))
    elif kind == "row":
        own = pl.BlockSpec((t, cols), lambda k, i, c: ((2 * k + c[0]) * nt + i, 0))
    else:
        own = pl.BlockSpec((1, t, cols), lambda k, i, c: (k, c[0] * nt + i, 0))
    spec = pl.BlockSpec((1, t, cols), lambda k, i, c: (k, i, 0))
    return pl.pallas_call(
        body, name=name,
        grid_spec=pltpu.PrefetchScalarGridSpec(num_scalar_prefetch=1, grid=(nc, nt), in_specs=[own, spec], out_specs=spec),
        out_shape=jax.ShapeDtypeStruct(theirs.shape, theirs.dtype), compiler_params=_params(),
    )(core, g, theirs)


def sum_lead(a, name):
    nl, rows, cols = a.shape
    t = _row_tile(rows, cols, (1 << 20) // 2)

    def body(a_ref, o_ref):
        acc = a_ref[0].astype(F32)
        for s in range(1, nl):
            acc = acc + a_ref[s].astype(F32)
        o_ref[...] = acc

    return pl.pallas_call(
        body, name=name, grid=(rows // t,),
        in_specs=[pl.BlockSpec((nl, t, cols), lambda i: (0, i, 0))],
        out_specs=pl.BlockSpec((t, cols), lambda i: (i, 0)),
        out_shape=jax.ShapeDtypeStruct((rows, cols), F32), compiler_params=_params(),
    )(a)


def sum_chips(rx, csum, chip, name):
    nc, rows, cols = rx.shape
    t = _row_tile(rows, cols, 2 << 20)

    def body(chip_ref, r0, r1, r2, r3, own_ref, o_ref):
        acc = None
        for s, ref in enumerate((r0, r1, r2, r3)):
            term = jnp.where(chip_ref[0] == s, own_ref[0], ref[0]).astype(F32)
            acc = term if acc is None else acc + term
        o_ref[...] = acc

    def slot(s):
        return pl.BlockSpec((1, t, cols), lambda i, c, s=s: (jnp.where(c[0] == s, c[0] ^ 1, s), i, 0))

    return pl.pallas_call(
        body, name=name,
        grid_spec=pltpu.PrefetchScalarGridSpec(
            num_scalar_prefetch=1, grid=(rows // t,),
            in_specs=[slot(s) for s in range(nc)] + [pl.BlockSpec((1, t, cols), lambda i, c: (c[0], i, 0))],
            out_specs=pl.BlockSpec((t, cols), lambda i, c: (i, 0))),
        out_shape=jax.ShapeDtypeStruct((rows, cols), F32), compiler_params=_params(),
    )(chip, rx, rx, rx, rx, csum)


def _adamw_update(w, g, m, v):
    nm = ADAM_B1 * m + (1.0 - ADAM_B1) * g
    nv = ADAM_B2 * v + (1.0 - ADAM_B2) * (g * g)
    m_hat = nm * (1.0 / (1.0 - ADAM_B1 ** ADAM_STEP))
    v_hat = nv * (1.0 / (1.0 - ADAM_B2 ** ADAM_STEP))
    return -ADAM_LR * (m_hat / (jnp.sqrt(v_hat) + ADAM_EPS) + ADAM_WD * w), nm, nv


def adamw(w, g, m, v, name):
    rows, cols = w.shape
    t = _row_tile(rows, cols)

    def body(w_ref, g_ref, m_ref, v_ref, d_ref, nm_ref, nv_ref):
        d_ref[...], nm_ref[...], nv_ref[...] = _adamw_update(w_ref[...], g_ref[...], m_ref[...], v_ref[...])

    spec = pl.BlockSpec((t, cols), lambda i: (i, 0))
    shp = jax.ShapeDtypeStruct((rows, cols), F32)
    return pl.pallas_call(
        body, name=name, grid=(rows // t,), in_specs=[spec] * 4, out_specs=[spec] * 3,
        out_shape=[shp, shp, shp], compiler_params=_params(),
    )(w, g, m, v)


def adamw_halves(w, g_mine, g_other, m, v, core, name):
    rows, cols = w.shape
    hr = rows // 2
    t = _row_tile(hr, cols)
    nt = hr // t

    def body(core_ref, w_ref, gm_ref, go_ref, m_ref, v_ref, g_ref, d_ref, nm_ref, nv_ref):
        g = jnp.where(pl.program_id(0) // nt == core_ref[0], gm_ref[...], go_ref[...])
        g_ref[...] = g
        d_ref[...], nm_ref[...], nv_ref[...] = _adamw_update(w_ref[...], g, m_ref[...], v_ref[...])

    spec = pl.BlockSpec((t, cols), lambda i, c: (i, 0))
    half = pl.BlockSpec((t, cols), lambda i, c: (i % nt, 0))
    shp = jax.ShapeDtypeStruct((rows, cols), F32)
    return pl.pallas_call(
        body, name=name,
        grid_spec=pltpu.PrefetchScalarGridSpec(num_scalar_prefetch=1, grid=(2 * nt,),
                                               in_specs=[spec, half, half, spec, spec], out_specs=[spec] * 4),
        out_shape=[shp] * 4, compiler_params=_params(),
    )(core, w, g_mine, g_other, m, v)


WEIGHTS = ("norm_mix_pre", "norm_mix_post", "norm_ffn_pre", "norm_ffn_post", "w_in", "conv_short_w",
           "w_conv_branch", "lru_conv_w", "lru_conv_b", "lru_wa", "lru_ba", "lru_wx", "lru_bx", "lru_lambda",
           "w_lru_branch", "w_out", "ffn_w_up", "ffn_conv_w", "ffn_conv_b", "ffn_w_down")
BIG = ("w_in", "ffn_w_up", "w_conv_branch", "w_lru_branch", "w_out", "ffn_w_down")
BIG_KIND = ("col", "col", "row", "row", "row", "row")
SMALL = ("conv_short_w", "lru_conv_w", "lru_wa", "lru_ba", "lru_wx", "lru_bx", "ffn_conv_w")
REPL = ("norm_mix_pre", "norm_mix_post", "norm_ffn_pre", "norm_ffn_post", "lru_conv_b", "lru_lambda", "ffn_conv_b")
PACK_W = 256
SMALL_ROWS = 576
REPL_ROWS = 16
LOSS_ROW = 12
FFN_SHARD = 2 * D_FF // N_CHIPS
QUARTER = HEAD_DIM // N_CHIPS
SMALL_PARTS = (("conv_short_w", 3, (1, 3, PACK_W)), ("lru_conv_w", 4, (1, 4, PACK_W)),
               ("lru_wa", LRU_HEADS * QUARTER, (1, LRU_HEADS, QUARTER, HEAD_DIM)), ("lru_ba", 1, (1, LRU_HEADS, QUARTER)),
               ("lru_wx", LRU_HEADS * QUARTER, (1, LRU_HEADS, QUARTER, HEAD_DIM)), ("lru_bx", 1, (1, LRU_HEADS, QUARTER)),
               ("ffn_conv_w", 3 * FFN_SHARD // PACK_W, (1, 3, FFN_SHARD)))


def _pad8(nr):
    return -(-nr // SUBLANES) * SUBLANES


def _pack_small_shard(p):
    rows = [jnp.pad(p[name].reshape(nr, PACK_W), ((0, _pad8(nr) - nr), (0, 0))) for name, nr, _ in SMALL_PARTS]
    used = sum(r.shape[0] for r in rows)
    return jnp.concatenate(rows + [jnp.zeros((SMALL_ROWS - used, PACK_W), F32)], axis=0)


def _unpack_small_shard(buf):
    out, r = {}, 0
    for name, nr, shape in SMALL_PARTS:
        out[name] = buf[r:r + nr].reshape(shape)
        r += _pad8(nr)
    return out


def _full_small(g4):
    per = [_unpack_small_shard(g4[k]) for k in range(N_CHIPS)]
    cat = lambda name, axis: jnp.concatenate([per[k][name][0] for k in range(N_CHIPS)], axis=axis)
    return dict(conv_short_w=cat("conv_short_w", 1), lru_conv_w=cat("lru_conv_w", 1),
                lru_wa=cat("lru_wa", 1), lru_ba=cat("lru_ba", 1).reshape(1, D_MODEL),
                lru_wx=cat("lru_wx", 1), lru_bx=cat("lru_bx", 1).reshape(1, D_MODEL),
                ffn_conv_w=cat("ffn_conv_w", 1))


def _split_small(full):
    shards = []
    for k in range(N_CHIPS):
        cols = lambda a, w: a[:, k * w:(k + 1) * w]
        q = slice(k * QUARTER, (k + 1) * QUARTER)
        shards.append(_pack_small_shard(dict(
            conv_short_w=cols(full["conv_short_w"], PACK_W), lru_conv_w=cols(full["lru_conv_w"], PACK_W),
            lru_wa=full["lru_wa"][:, q, :], lru_ba=full["lru_ba"].reshape(LRU_HEADS, HEAD_DIM)[:, q],
            lru_wx=full["lru_wx"][:, q, :], lru_bx=full["lru_bx"].reshape(LRU_HEADS, HEAD_DIM)[:, q],
            ffn_conv_w=cols(full["ffn_conv_w"], FFN_SHARD))))
    return jnp.stack(shards)


def _pack_repl(p, loss=None):
    rows = [p[n].reshape(-1, D_MODEL) for n in REPL]
    if loss is not None:
        rows.append(jnp.broadcast_to(loss.reshape(1, 1), (1, D_MODEL)))
    used = sum(r.shape[0] for r in rows)
    return jnp.concatenate(rows + [jnp.zeros((REPL_ROWS - used, D_MODEL), F32)], axis=0)


def _unpack_repl(buf):
    out, r = {}, 0
    for n in REPL:
        nr = (2 * D_FF // D_MODEL) if n == "ffn_conv_b" else 1
        out[n] = buf[r:r + nr].reshape(1, nr * D_MODEL)
        r += nr
    return out


def kernel(x, norm_mix_pre, norm_mix_post, norm_ffn_pre, norm_ffn_post, w_in, conv_short_w, w_conv_branch, lru_conv_w, lru_conv_b, lru_wa, lru_ba, lru_wx, lru_bx, lru_lambda, w_lru_branch, w_out, ffn_w_up, ffn_conv_w, ffn_conv_b, ffn_w_down, loss_target, m_norm_mix_pre, m_norm_mix_post, m_norm_ffn_pre, m_norm_ffn_post, m_w_in, m_conv_short_w, m_w_conv_branch, m_lru_conv_w, m_lru_conv_b, m_lru_wa, m_lru_ba, m_lru_wx, m_lru_bx, m_lru_lambda, m_w_lru_branch, m_w_out, m_ffn_w_up, m_ffn_conv_w, m_ffn_conv_b, m_ffn_w_down, v_norm_mix_pre, v_norm_mix_post, v_norm_ffn_pre, v_norm_ffn_post, v_w_in, v_conv_short_w, v_w_conv_branch, v_lru_conv_w, v_lru_conv_b, v_lru_wa, v_lru_ba, v_lru_wx, v_lru_bx, v_lru_lambda, v_w_lru_branch, v_w_out, v_ffn_w_up, v_ffn_conv_w, v_ffn_conv_b, v_ffn_w_down):
    given = dict(locals())
    w = {n: given[n] for n in WEIGHTS}
    m = {n: given["m_" + n] for n in WEIGHTS}
    v = {n: given["v_" + n] for n in WEIGHTS}

    xi, yi, ci = _position()
    chip_i = 2 * xi + yi
    chip = chip_i.astype(jnp.int32).reshape(1)
    core = ci.astype(jnp.int32).reshape(1)
    xs, target = x[0], loss_target[0]
    g1, g2, g3, g4 = w["norm_mix_pre"], w["norm_mix_post"], w["norm_ffn_pre"], w["norm_ffn_post"]
    shard = {n: w[n][0].astype(BF16) for n in BIG}
    small_shard = _pack_small_shard(w)

    def gathered(bufs, names):
        return [_own_slot(b, small_shard if n == "small" else shard[n], chip_i) for b, n in zip(bufs, names)]

    def chip_sums(arrays, kinds, tag):
        theirs = pair_split(arrays, kinds, "pair_split_" + tag)
        return [add_pair(g, k, t, core, "pair_add_%s_%d" % (tag, i)) for i, (g, k, t) in enumerate(zip(arrays, kinds, theirs))]

    h1 = norm_in(xs, g1)
    win4, small4 = gathered(run_ride(gather_ride([shard["w_in"], small_shard]), "gather_first"), ("w_in", "small"))
    small = _full_small(small4)
    (proj,), got = matmul_cols(h1, win4, "proj_fwd",
                               ride=gather_ride([shard["w_conv_branch"], shard["w_lru_branch"], shard["w_out"]]))
    wcb, wlb, wout = [g.reshape(-1, D_MODEL) for g in gathered(got, ("w_conv_branch", "w_lru_branch", "w_out"))]
    up_piece = lambda r0, nr, into=None: gather_ride([shard["ffn_w_up"]], items=[(0, r0, nr)], into=into)
    down_piece = lambda r0, nr, into=None: gather_ride([shard["ffn_w_down"]], items=[(0, r0, nr)], into=into)
    (q, ya), got = mix_conv_fwd(proj, small["conv_short_w"], ride=up_piece(0, 160))
    (xl, r, gi, h, yb), got = mix_lru_fwd(
        proj, small["lru_conv_w"], w["lru_conv_b"], small["lru_wa"].astype(BF16), small["lru_ba"],
        small["lru_wx"].astype(BF16), small["lru_bx"], w["lru_lambda"], ride=up_piece(160, 512, got))
    (a, b, merged), got = branch_merge_fwd(ya, yb, wcb, wlb, proj, ride=up_piece(672, 352, got))
    (wup4,) = gathered(got, ("ffn_w_up",))
    (mix, x2, h2), got = mix_out_fwd(merged, wout, xs, g2, g3, ride=down_piece(0, 256))
    (up, act, f), got = ffn_up_act_fwd(h2, wup4, small["ffn_conv_w"], w["ffn_conv_b"], ride=down_piece(256, 512, got))
    wdown = gathered(got, ("ffn_w_down",))[0].reshape(-1, D_MODEL)
    dy, dout, loss, dg4 = ffn_down_loss(f, wdown, x2, target, g4)

    dh2, dwup, dwdown, dfw, dfb = ffn_up_bwd(dout, wdown, up, act, f, small["ffn_conv_w"], wup4, h2)
    cs_down, cs_up = chip_sums([dwdown, dwup], ["row", "col2"], "ffn")
    down_rows = lambda r0, nr, into=None: exchange_ride([cs_down], items=[(0, r0, nr)], into=into)
    up_rows = lambda r0, nr, into=None: exchange_ride([cs_up], items=[(0, r0, nr)], into=into)
    (dx2, dmix, dg3, dg2), rx_down = norms_mid_bwd(dh2, x2, dy, mix, g3, g2, ride=down_rows(0, 128))
    (da, db, dwout, dgates), rx_down = mix_out_bwd(dmix, wout, merged, a, b, proj, ride=down_rows(128, 256, rx_down))
    (dconv, dwcb, dws), rx_up = mix_conv_bwd(da, wcb, proj, q, small["conv_short_w"], ride=up_rows(0, 176))
    cs_mid = chip_sums([dwout, dwcb], ["row", "row"], "mid")
    (dlru, dwlb, dwa, dwx, dba, dbx, dwl, dbl, dlam), rx_up = mix_lru_bwd(
        db, wlb, proj, xl, r, gi, h, small["lru_conv_w"], small["lru_wa"].astype(BF16), small["lru_wx"].astype(BF16),
        w["lru_lambda"], ride=up_rows(176, 336, rx_up))
    grads = dict(norm_mix_post=dg2, norm_ffn_pre=dg3, norm_ffn_post=dg4, conv_short_w=dws, lru_conv_w=dwl,
                 lru_conv_b=dbl, lru_wa=dwa, lru_ba=dba, lru_wx=dwx, lru_bx=dbx, lru_lambda=dlam,
                 ffn_conv_w=jnp.concatenate([dfw[0], dfw[1]], axis=1), ffn_conv_b=jnp.concatenate([dfb[0], dfb[1]], axis=1))
    cs_late = chip_sums([dwlb, _split_small(grads)], ["row", "lead"], "late")
    (dh1, dwin), rx_all = dgrad_wgrad_cols([dconv, dlru, dgates], win4, h1, "proj_bwd", ride=exchange_ride(cs_mid + cs_late))
    rx_mid, rx_late = rx_all[:2], rx_all[2:]
    dx, grads["norm_mix_pre"] = norm_in_bwd(dh1, xs, dx2, g1)
    cs_in = chip_sums([dwin], ["col"], "in")
    rep_part = _pack_repl(grads, loss[0, 0])
    rx_in, rep_all = run_ride(exchange_ride(cs_in, rep=rep_part), "exchange_last")

    order = (("w_in", cs_in[0], rx_in), ("ffn_w_up", cs_up, rx_up[0]), ("w_conv_branch", cs_mid[1], rx_mid[1]),
             ("w_lru_branch", cs_late[0], rx_late[0]), ("w_out", cs_mid[0], rx_mid[0]),
             ("ffn_w_down", cs_down, rx_down[0]), ("small", cs_late[1], rx_late[1]))
    halves = [sum_chips(rx, cs, chip, "chip_sum_" + n) for n, cs, rx in order]
    me = 4 * xi + 2 * yi + ci
    rep_grad = sum_lead(_own_slot(rep_all, rep_part, me), "device_sum")
    others = pair_swap(halves)

    g_out, d_out, m_out, v_out = {}, {}, {}, {}
    for n, gm, go in zip(BIG, halves[:-1], others[:-1]):
        g, d, nm, nv = adamw_halves(w[n][0], gm, go, m[n][0], v[n][0], core, "adamw_" + n)
        g_out[n], d_out[n], m_out[n], v_out[n] = g[None], d[None], nm[None], nv[None]
    bufs = adamw_halves(small_shard, halves[-1], others[-1], _pack_small_shard(m), _pack_small_shard(v),
                        core, "adamw_small")
    for dst, buf in zip((g_out, d_out, m_out, v_out), bufs):
        dst.update(_unpack_small_shard(buf))
    d, nm, nv = adamw(_pack_repl(w), rep_grad, _pack_repl(m), _pack_repl(v), "adamw_repl")
    for dst, buf in ((g_out, rep_grad), (d_out, d), (m_out, nm), (v_out, nv)):
        dst.update(_unpack_repl(buf))

    return (rep_grad[LOSS_ROW, 0], dx[None], *[g_out[n] for n in WEIGHTS], *[d_out[n] for n in WEIGHTS],
            *[m_out[n] for n in WEIGHTS], *[v_out[n] for n in WEIGHTS])
```

```python
import functools
import math

import jax
import jax.numpy as jnp
from jax import lax
from jax.experimental import pallas as pl
from jax.experimental.pallas import tpu as pltpu

F32 = jnp.float32
BF16 = jnp.bfloat16

D_MODEL = 1024
N_CHIPS = 4
N_SEG = 7
D_FF = 3 * D_MODEL
LRU_HEADS = 4
HEAD_DIM = D_MODEL // LRU_HEADS
LRU_C = 8.0
RMS_EPS = 1e-6
CW = 256
FW = 256
SUBLANES = 8
VMEM_LIMIT = 58 * 1024 * 1024

ADAM_LR = 0.001
ADAM_B1 = 0.9
ADAM_B2 = 0.999
ADAM_EPS = 1e-08
ADAM_WD = 0.01
ADAM_STEP = 10

_GELU_C = math.sqrt(2.0 / math.pi)
_GELU_K = 0.044715


def _params(**kw):
    return pltpu.CompilerParams(vmem_limit_bytes=VMEM_LIMIT, **kw)


def _sigmoid(x):
    return 1.0 / (1.0 + jnp.exp(-x))


def _gelu(x):
    t = jnp.tanh(_GELU_C * (x + _GELU_K * x * x * x))
    return 0.5 * x * (1.0 + t)


def _gelu_and_grad(x):
    x2 = x * x
    t = jnp.tanh(_GELU_C * (x + _GELU_K * x * x2))
    g = 0.5 * x * (1.0 + t)
    dg = 0.5 * (1.0 + t) + 0.5 * x * (1.0 - t * t) * _GELU_C * (1.0 + 3.0 * _GELU_K * x2)
    return g, dg


def _log_sigmoid(x):
    e = jnp.exp(-jnp.abs(x))
    u = 1.0 + e
    l1p = jnp.where(u == 1.0, e, jnp.log(u) * e / (u - 1.0))
    return jnp.minimum(x, 0.0) - l1p


def _neg_expm1(z):
    series = -z * (1.0 + z * (0.5 + z * (1.0 / 6.0 + z * (1.0 / 24.0 + z * (1.0 / 120.0 + z * (1.0 / 720.0))))))
    return jnp.where(z > -0.2, series, 1.0 - jnp.exp(z))


def _rows(shape):
    return lax.broadcasted_iota(jnp.int32, shape, 0)


def _shift_down(x, k):
    return jnp.where(_rows(x.shape) >= k, pltpu.roll(x, k, 0), 0.0)


def _shift_up(x, k):
    n = x.shape[0]
    return jnp.where(_rows(x.shape) < n - k, pltpu.roll(x, n - k, 0), 0.0)


def _delays(x, k_width):
    return [x] + [_shift_down(x, j) for j in range(1, k_width)]


def _advances(dy, k_width):
    return [dy] + [_shift_up(dy, j) for j in range(1, k_width)]


def _taps_sum(shifted, w_ref, b=None):
    k_width = w_ref.shape[0]
    y = w_ref[k_width - 1:k_width, :] * shifted[0]
    for j in range(1, k_width):
        y = y + w_ref[k_width - 1 - j:k_width - j, :] * shifted[j]
    if b is not None:
        y = y + b
    return y


def _causal_conv(x, w_ref, b=None):
    return _taps_sum(_delays(x, w_ref.shape[0]), w_ref, b)


def _conv_wgrad(advanced, x):
    k_width = len(advanced)
    rows = [jnp.sum(advanced[k_width - 1 - k] * x, axis=0, keepdims=True) for k in range(k_width)]
    return jnp.concatenate(rows, axis=0)


def _dot(a, b):
    return jnp.dot(a, b, preferred_element_type=F32)


def _dot_nt(a, b):
    return lax.dot_general(a, b, (((1,), (1,)), ((), ())), preferred_element_type=F32)


def _dot_tn(a, b):
    return lax.dot_general(a, b, (((0,), (0,)), ((), ())), preferred_element_type=F32)


def _rms_stats(x):
    r = lax.rsqrt(jnp.mean(x * x, axis=-1, keepdims=True) + RMS_EPS)
    return x * r, r


def _rms_bwd(n, r, g, dy):
    dn = dy * g
    dx = r * (dn - n * jnp.mean(dn * n, axis=-1, keepdims=True))
    return dx, dy * n


def _scan_forward(a_ref, b_ref, h_ref):
    n, c = a_ref.shape
    row = lax.broadcasted_iota(jnp.int32, (SUBLANES, c), 0)

    def group(g, carry):
        r0 = pl.multiple_of(g * SUBLANES, SUBLANES)
        a = a_ref[pl.ds(r0, SUBLANES), :]
        b = b_ref[pl.ds(r0, SUBLANES), :]
        for k in (1, 2, 4):
            ap = jnp.where(row >= k, pltpu.roll(a, k, 0), 1.0)
            bp = jnp.where(row >= k, pltpu.roll(b, k, 0), 0.0)
            b = a * bp + b
            a = a * ap
        h = a * carry + b
        h_ref[pl.ds(r0, SUBLANES), :] = h
        return h[SUBLANES - 1:SUBLANES, :]

    lax.fori_loop(0, n // SUBLANES, group, jnp.zeros((1, c), F32))


def _scan_backward(c_ref, b_ref, g_ref):
    n, ch = c_ref.shape
    row = lax.broadcasted_iota(jnp.int32, (SUBLANES, ch), 0)
    n_groups = n // SUBLANES

    def group(i, carry):
        r0 = pl.multiple_of((n_groups - 1 - i) * SUBLANES, SUBLANES)
        a = c_ref[pl.ds(r0, SUBLANES), :]
        b = b_ref[pl.ds(r0, SUBLANES), :]
        for k in (1, 2, 4):
            keep = row < SUBLANES - k
            ap = jnp.where(keep, pltpu.roll(a, SUBLANES - k, 0), 1.0)
            bp = jnp.where(keep, pltpu.roll(b, SUBLANES - k, 0), 0.0)
            b = a * bp + b
            a = a * ap
        g = a * carry + b
        g_ref[pl.ds(r0, SUBLANES), :] = g
        return g[0:1, :]

    lax.fori_loop(0, n_groups, group, jnp.zeros((1, ch), F32))


MESH = pl.DeviceIdType.MESH
_HBM = pl.BlockSpec(memory_space=pltpu.HBM)
_OTHER_CHIPS = ((1, 0), (0, 1), (1, 1))
_OTHER_DEVICES = tuple((dx, dy, dc) for dx in (0, 1) for dy in (0, 1) for dc in (0, 1) if dx or dy or dc)
N_DEVICES = 8


def _position():
    return lax.axis_index("x"), lax.axis_index("y"), lax.axis_index("c")


def _flip(v, d):
    return 1 - v if d else v


def _chip(x, y, p):
    px, py = _flip(x, _OTHER_CHIPS[p][0]), _flip(y, _OTHER_CHIPS[p][1])
    return px, py, 2 * px + py


class _Ride:
    def __init__(self, srcs, bufs, scratch, plan):
        self.srcs, self.bufs, self.scratch, self.plan = list(srcs), list(bufs), list(scratch), plan


def _call(body, *, name, grid, in_specs, out_specs, out_shape, operands, scratch_shapes=(), ride=None):
    in_specs, out_specs, out_shape = list(in_specs), list(out_specs), list(out_shape)
    scratch_shapes = list(scratch_shapes)
    if ride is None:
        return pl.pallas_call(body, name=name, grid=grid, in_specs=in_specs, out_specs=out_specs, out_shape=out_shape,
                              scratch_shapes=scratch_shapes, compiler_params=_params())(*operands)
    n_in, n_out, n_scr = len(in_specs), len(out_shape), len(scratch_shapes)
    old = [i for i, b in enumerate(ride.bufs) if not isinstance(b, jax.ShapeDtypeStruct)]
    n_src, n_old, n_buf = len(ride.srcs), len(old), len(ride.bufs)

    def full_body(*refs):
        o0 = n_in + n_src + n_old
        s0 = o0 + n_out + n_buf
        start, relay, finish = ride.plan(refs[n_in:n_in + n_src], refs[o0 + n_out:s0], refs[s0 + n_scr:])
        ids = [pl.program_id(i) for i in range(len(grid))]
        first = functools.reduce(jnp.logical_and, [i == 0 for i in ids])
        last = functools.reduce(jnp.logical_and, [i == g - 1 for i, g in zip(ids, grid)])
        pl.when(first)(start)
        pl.when(last)(relay)
        body(*refs[:n_in], *refs[o0:o0 + n_out], *refs[s0:s0 + n_scr])
        pl.when(last)(finish)

    shapes = [jax.ShapeDtypeStruct(b.shape, b.dtype) for b in ride.bufs]
    res = pl.pallas_call(
        full_body, name=name, grid=grid,
        in_specs=in_specs + [_HBM] * (n_src + n_old), out_specs=out_specs + [_HBM] * n_buf,
        out_shape=out_shape + shapes, scratch_shapes=scratch_shapes + ride.scratch,
        input_output_aliases={n_in + n_src + k: n_out + i for k, i in enumerate(old)},
        compiler_params=_params(),
    )(*operands, *ride.srcs, *[ride.bufs[i] for i in old])
    return list(res[:n_out]), list(res[n_out:])


def run_ride(ride, name):
    def body():
        pass

    return _call(body, name=name, grid=(1,), in_specs=[], out_specs=[], out_shape=[], operands=[], ride=ride)[1]


def gather_ride(shards, items=None, into=None):
    items = items or [(a, 0, s.shape[0]) for a, s in enumerate(shards)]
    bufs = into or [jax.ShapeDtypeStruct((N_CHIPS,) + s.shape, s.dtype) for s in shards]
    nrel = len(_OTHER_CHIPS)

    def plan(srcs, dsts, sems):
        ici_send, ici_recv, hop_send, hop_recv, sib_send, sib_recv = sems
        x, y, c = _position()
        j = 2 * x + y

        def rows(ref, it, h, q=None):
            half = it[2] // 2
            if q is None:
                return ref.at[pl.ds(it[1] + h * half, half), :]
            return ref.at[pl.ds(it[1] + h * half + q * (half // 2), half // 2), :]

        def ici(i, p, slot):
            it = items[i]
            px, py, _ = _chip(x, y, p)
            return pltpu.make_async_remote_copy(
                src_ref=rows(srcs[it[0]], it, c), dst_ref=rows(dsts[it[0]].at[slot], it, c),
                send_sem=ici_send.at[i * nrel + p], recv_sem=ici_recv.at[i * nrel + p],
                device_id=(px, py, c), device_id_type=MESH)

        def hop(i, p, slot):
            it = items[i]
            part = rows(dsts[it[0]].at[slot], it, c, p)
            px, py, _ = _chip(x, y, 1 - p)
            return pltpu.make_async_remote_copy(
                src_ref=part, dst_ref=part, send_sem=hop_send.at[i * 2 + p], recv_sem=hop_recv.at[i * 2 + p],
                device_id=(px, py, c), device_id_type=MESH)

        def sib(i, p, h):
            it = items[i]
            part = rows(dsts[it[0]].at[_chip(x, y, p)[2]], it, h)
            return pltpu.make_async_remote_copy(
                src_ref=part, dst_ref=part, send_sem=sib_send.at[i * nrel + p], recv_sem=sib_recv.at[i * nrel + p],
                device_id=(x, y, 1 - c), device_id_type=MESH)

        every = range(len(items))
        diag = _chip(x, y, 2)[2]

        def start():
            for i in every:
                for p in (0, 1):
                    ici(i, p, j).start()

        def relay():
            for i in every:
                for p in (0, 1):
                    k = _chip(x, y, p)[2]
                    ici(i, p, k).wait_recv()
                    hop(i, p, k).start()
                    sib(i, p, c).start()
            for i in every:
                for p in (0, 1):
                    hop(i, p, diag).wait_recv()
                sib(i, 2, c).start()

        def finish():
            for i in every:
                for p in range(nrel):
                    sib(i, p, 1 - c).wait_recv()
            for i in every:
                for p in (0, 1):
                    ici(i, p, j).wait_send()
                    hop(i, p, _chip(x, y, p)[2]).wait_send()
                for p in range(nrel):
                    sib(i, p, c).wait_send()

        return start, relay, finish

    n = len(items)
    sems = [pltpu.SemaphoreType.DMA((n * nrel,))] * 2 + [pltpu.SemaphoreType.DMA((n * 2,))] * 2 \
        + [pltpu.SemaphoreType.DMA((n * nrel,))] * 2
    return _Ride(shards, bufs, sems, plan)


def exchange_ride(sums, items=None, into=None, rep=None):
    items = items or [(a, 0, s.shape[1]) for a, s in enumerate(sums)]
    into = into or [None] * len(sums)
    bufs = [jax.ShapeDtypeStruct(s.shape, s.dtype) if b is None else b for s, b in zip(sums, into)]
    srcs = list(sums)
    scratch = [pltpu.SemaphoreType.DMA((len(items) * len(_OTHER_CHIPS),))] * 2
    if rep is not None:
        srcs.append(rep)
        bufs.append(jax.ShapeDtypeStruct((N_DEVICES,) + rep.shape, rep.dtype))
        scratch += [pltpu.SemaphoreType.DMA((len(_OTHER_DEVICES),))] * 2
    nrel = len(_OTHER_CHIPS)

    def plan(src_refs, dst_refs, sems):
        x, y, c = _position()
        j = 2 * x + y
        me = 4 * x + 2 * y + c

        def part(i, p, src_slot, dst_slot):
            a, r0, nr = items[i]
            px, py, _ = _chip(x, y, p)
            return pltpu.make_async_remote_copy(
                src_ref=src_refs[a].at[src_slot, pl.ds(r0, nr), :], dst_ref=dst_refs[a].at[dst_slot, pl.ds(r0, nr), :],
                send_sem=sems[0].at[i * nrel + p], recv_sem=sems[1].at[i * nrel + p],
                device_id=(px, py, c), device_id_type=MESH)

        def device(q):
            dx, dy, dc = _OTHER_DEVICES[q]
            return _flip(x, dx), _flip(y, dy), _flip(c, dc)

        def rep_copy(q, slot):
            return pltpu.make_async_remote_copy(
                src_ref=src_refs[-1], dst_ref=dst_refs[-1].at[slot], send_sem=sems[2].at[q], recv_sem=sems[3].at[q],
                device_id=device(q), device_id_type=MESH)

        pairs = [(i, p) for i in range(len(items)) for p in range(nrel)]
        others = range(len(_OTHER_DEVICES)) if rep is not None else ()

        def start():
            for i, p in pairs:
                part(i, p, _chip(x, y, p)[2], j).start()
            for q in others:
                rep_copy(q, me).start()

        def finish():
            for i, p in pairs:
                k = _chip(x, y, p)[2]
                part(i, p, k, k).wait_recv()
            for q in others:
                px, py, pc = device(q)
                rep_copy(q, 4 * px + 2 * py + pc).wait_recv()
            for i, p in pairs:
                part(i, p, _chip(x, y, p)[2], j).wait_send()
            for q in others:
                rep_copy(q, me).wait_send()

        return start, lambda: None, finish

    return _Ride(srcs, bufs, scratch, plan)


def _own_slot(buf, own, index):
    return lax.dynamic_update_slice(buf, own[None], (index,) + (0,) * own.ndim)


def _token_tile(s):
    return min(s, 512)


def norm_in(x, g):
    s, d = x.shape
    t = _token_tile(s)

    def body(x_ref, g_ref, o_ref):
        n, _ = _rms_stats(x_ref[...])
        o_ref[...] = (n * g_ref[...]).astype(BF16)

    return pl.pallas_call(
        body, name="norm_in", grid=(s // t,),
        in_specs=[pl.BlockSpec((t, d), lambda i: (i, 0)), pl.BlockSpec((1, d), lambda i: (0, 0))],
        out_specs=pl.BlockSpec((t, d), lambda i: (i, 0)),
        out_shape=jax.ShapeDtypeStruct((s, d), BF16),
        compiler_params=_params(),
    )(x, g)


def matmul_cols(a, w4, name, ride=None):
    m, k = a.shape
    nj, _, ns = w4.shape
    nb = ns // CW

    def body(a_ref, w_ref, o_ref):
        o_ref[...] = _dot(a_ref[...], w_ref[0])

    return _call(
        body, name=name, grid=(nj, nb),
        in_specs=[pl.BlockSpec((m, k), lambda j, b: (0, 0)),
                  pl.BlockSpec((1, k, CW), lambda j, b: (j, 0, b))],
        out_specs=[pl.BlockSpec((m, CW), lambda j, b: (0, j * nb + b))],
        out_shape=[jax.ShapeDtypeStruct((m, nj * ns), F32)],
        operands=(a, w4), ride=ride)


def mix_conv_fwd(proj, ws, ride=None):
    s = proj.shape[0]
    nblk = D_MODEL // CW

    def body(cb_ref, cc_ref, cx_ref, ws_ref, q_ref, ya_ref):
        q = _causal_conv(cc_ref[...] * cx_ref[...], ws_ref)
        q_ref[...] = q
        ya_ref[...] = (cb_ref[...] * q).astype(BF16)

    seg = lambda k: pl.BlockSpec((s, CW), lambda c, k=k: (0, k * nblk + c))
    return _call(
        body, name="mix_conv_fwd", grid=(nblk,),
        in_specs=[seg(0), seg(1), seg(2), pl.BlockSpec((3, CW), lambda c: (0, c))],
        out_specs=[pl.BlockSpec((s, CW), lambda c: (0, c))] * 2,
        out_shape=[jax.ShapeDtypeStruct((s, D_MODEL), F32), jax.ShapeDtypeStruct((s, D_MODEL), BF16)],
        operands=(proj, proj, proj, ws), ride=ride)


def _lru_gates(r, ls):
    log_a = LRU_C * r * ls
    a = jnp.exp(log_a)
    mult = jnp.sqrt(_neg_expm1(2.0 * log_a))
    mult = jnp.where(_rows(r.shape) == 0, 1.0, mult)
    return a, mult


def mix_lru_fwd(proj, wl, bl, wa, ba, wx, bx, lam, ride=None):
    s = proj.shape[0]
    nblk = D_MODEL // CW

    def body(lx_ref, ly_ref, wl_ref, bl_ref, wa_ref, ba_ref, wx_ref, bx_ref, lam_ref,
             xl_ref, r_ref, i_ref, h_ref, yb_ref, a_scr, u_scr):
        xl = _causal_conv(lx_ref[...], wl_ref, bl_ref[...])
        xlb = xl.astype(BF16)
        xl_ref[...] = xlb
        r = _sigmoid(_dot(xlb, wa_ref[0]) + ba_ref[...])
        i = _sigmoid(_dot(xlb, wx_ref[0]) + bx_ref[...])
        r_ref[...] = r.astype(BF16)
        i_ref[...] = i.astype(BF16)
        a, mult = _lru_gates(r, _log_sigmoid(lam_ref[...]))
        a_scr[...] = a
        u_scr[...] = mult * i * xl
        _scan_forward(a_scr, u_scr, h_ref)
        yb_ref[...] = (h_ref[...] * _gelu(ly_ref[...])).astype(BF16)

    blk = lambda k: pl.BlockSpec((s, CW), lambda c, k=k: (0, k * nblk + c))
    vec = pl.BlockSpec((1, CW), lambda c: (0, c))
    mat = pl.BlockSpec((1, CW, CW), lambda c: (c, 0, 0))
    out = pl.BlockSpec((s, CW), lambda c: (0, c))
    f = jax.ShapeDtypeStruct((s, D_MODEL), F32)
    hb = jax.ShapeDtypeStruct((s, D_MODEL), BF16)
    return _call(
        body, name="mix_lru_fwd", grid=(nblk,),
        in_specs=[blk(3), blk(4), pl.BlockSpec((4, CW), lambda c: (0, c)), vec, mat, vec, mat, vec, vec],
        out_specs=[out] * 5,
        out_shape=[hb, hb, hb, f, hb],
        scratch_shapes=[pltpu.VMEM((s, CW), F32), pltpu.VMEM((s, CW), F32)],
        operands=(proj, proj, wl, bl, wa, ba, wx, bx, lam), ride=ride)


def branch_merge_fwd(ya, yb, wcb, wlb, proj, ride=None):
    s = ya.shape[0]
    nblk = D_MODEL // CW

    def body(ya_ref, yb_ref, wcb_ref, wlb_ref, gc_ref, gl_ref, a_ref, b_ref, m_ref):
        a = _dot(ya_ref[...], wcb_ref[...])
        b = _dot(yb_ref[...], wlb_ref[...])
        a_ref[...] = a
        b_ref[...] = b
        m_ref[...] = (_sigmoid(gc_ref[...]) * a + _sigmoid(gl_ref[...]) * b).astype(BF16)

    res = pl.BlockSpec((s, D_MODEL), lambda n: (0, 0))
    wcol = pl.BlockSpec((D_MODEL, CW), lambda n: (0, n))
    blk = lambda k: pl.BlockSpec((s, CW), lambda n, k=k: (0, k * nblk + n))
    out = pl.BlockSpec((s, CW), lambda n: (0, n))
    f = jax.ShapeDtypeStruct((s, D_MODEL), F32)
    return _call(
        body, name="branch_merge_fwd", grid=(nblk,),
        in_specs=[res, res, wcol, wcol, blk(5), blk(6)],
        out_specs=[out] * 3,
        out_shape=[f, f, jax.ShapeDtypeStruct((s, D_MODEL), BF16)],
        operands=(ya, yb, wcb, wlb, proj, proj), ride=ride)


def mix_out_fwd(merged, wout, x, g2, g3, ride=None):
    s, d = x.shape
    t = _token_tile(s)

    def body(m_ref, w_ref, x_ref, g2_ref, g3_ref, mix_ref, x2_ref, h2_ref):
        mix = _dot(m_ref[...], w_ref[...])
        mix_ref[...] = mix
        n, _ = _rms_stats(mix)
        x2 = x_ref[...] + n * g2_ref[...]
        x2_ref[...] = x2
        n2, _ = _rms_stats(x2)
        h2_ref[...] = (n2 * g3_ref[...]).astype(BF16)

    tile = pl.BlockSpec((t, d), lambda i: (i, 0))
    vec = pl.BlockSpec((1, d), lambda i: (0, 0))
    f = jax.ShapeDtypeStruct((s, d), F32)
    return _call(
        body, name="mix_out_fwd", grid=(s // t,),
        in_specs=[tile, pl.BlockSpec((d, d), lambda i: (0, 0)), tile, vec, vec],
        out_specs=[tile] * 3,
        out_shape=[f, f, jax.ShapeDtypeStruct((s, d), BF16)],
        operands=(merged, wout, x, g2, g3), ride=ride)


def ffn_up_act_fwd(h2, wup4, fw, fb, ride=None):
    s, k = h2.shape
    ns = wup4.shape[2]
    per_chip = ns // CW
    nblk = D_FF // CW

    def body(h_ref, wg_ref, wv_ref, cg_ref, cv_ref, bg_ref, bv_ref, up_ref, act_ref, f_ref):
        h = h_ref[...]
        ug = _dot(h, wg_ref[0])
        uv = _dot(h, wv_ref[0])
        up_ref[0] = ug
        up_ref[1] = uv
        gate = _causal_conv(ug, cg_ref, bg_ref[...])
        val = _causal_conv(uv, cv_ref, bv_ref[...])
        act_ref[0] = gate.astype(BF16)
        act_ref[1] = val.astype(BF16)
        f_ref[...] = (_gelu(gate) * val).astype(BF16)

    wcols = lambda h: pl.BlockSpec((1, k, CW), lambda n, h=h: (n // per_chip + 2 * h, 0, n % per_chip))
    half = lambda h, rows: pl.BlockSpec((rows, CW), lambda n, h=h: (0, h * nblk + n))
    both = pl.BlockSpec((2, s, CW), lambda n: (0, 0, n))
    return _call(
        body, name="ffn_up_act_fwd", grid=(nblk,),
        in_specs=[pl.BlockSpec((s, k), lambda n: (0, 0)), wcols(0), wcols(1),
                  half(0, 3), half(1, 3), half(0, 1), half(1, 1)],
        out_specs=[both, both, pl.BlockSpec((s, CW), lambda n: (0, n))],
        out_shape=[jax.ShapeDtypeStruct((2, s, D_FF), F32), jax.ShapeDtypeStruct((2, s, D_FF), BF16),
                   jax.ShapeDtypeStruct((s, D_FF), BF16)],
        operands=(h2, wup4, wup4, fw, fw, fb, fb), ride=ride)


def ffn_down_loss(f, wdown, x2, target, g4):
    s, d = x2.shape
    t = _token_tile(s)

    def body(f_ref, w_ref, x2_ref, tg_ref, g4_ref, dy_ref, dout_ref, loss_ref, dg4_ref):
        @pl.when(pl.program_id(0) == 0)
        def _():
            loss_ref[...] = jnp.zeros_like(loss_ref)
            dg4_ref[...] = jnp.zeros_like(dg4_ref)

        out = _dot(f_ref[...], w_ref[...])
        n, r = _rms_stats(out)
        err = x2_ref[...] + n * g4_ref[...] - tg_ref[...]
        loss_ref[...] += jnp.full(loss_ref.shape, (0.5 / d) * jnp.sum(err * err), F32)
        dy = err * (1.0 / d)
        dy_ref[...] = dy
        dout, dg = _rms_bwd(n, r, g4_ref[...], dy)
        dout_ref[...] = dout.astype(BF16)
        dg4_ref[...] += jnp.sum(dg, axis=0, keepdims=True)

    tile = pl.BlockSpec((t, d), lambda i: (i, 0))
    vec = pl.BlockSpec((1, d), lambda i: (0, 0))
    return pl.pallas_call(
        body, name="ffn_down_loss", grid=(s // t,),
        in_specs=[pl.BlockSpec((t, D_FF), lambda i: (i, 0)), pl.BlockSpec((D_FF, d), lambda i: (0, 0)), tile, tile, vec],
        out_specs=[tile, tile, pl.BlockSpec((1, 128), lambda i: (0, 0)), vec],
        out_shape=[jax.ShapeDtypeStruct((s, d), F32), jax.ShapeDtypeStruct((s, d), BF16),
                   jax.ShapeDtypeStruct((1, 128), F32), jax.ShapeDtypeStruct((1, d), F32)],
        compiler_params=_params(),
    )(f, wdown, x2, target, g4)


def ffn_bwd(dout, wdown, up, act, f, fw):
    s = up.shape[1]
    nblk = D_FF // FW

    def body(do_ref, wd_ref, up_ref, act_ref, f_ref, wg_ref, wv_ref, dup_ref, dwd_ref, dw_ref, db_ref):
        do = do_ref[...]
        df = _dot_nt(do, wd_ref[...])
        dwd_ref[...] = _dot_tn(f_ref[...], do).astype(BF16)
        val = act_ref[1].astype(F32)
        ge, dge = _gelu_and_grad(act_ref[0].astype(F32))
        dgate = _advances(df * val * dge, 3)
        dval = _advances(df * ge, 3)
        dup_ref[0] = _taps_sum(dgate, wg_ref).astype(BF16)
        dup_ref[1] = _taps_sum(dval, wv_ref).astype(BF16)
        dw_ref[0] = _conv_wgrad(dgate, up_ref[0])
        dw_ref[1] = _conv_wgrad(dval, up_ref[1])
        db_ref[0] = jnp.sum(dgate[0], axis=0, keepdims=True)
        db_ref[1] = jnp.sum(dval[0], axis=0, keepdims=True)

    half = lambda h, rows: pl.BlockSpec((rows, FW), lambda n, h=h: (0, h * nblk + n))
    both = lambda rows: pl.BlockSpec((2, rows, FW), lambda n: (0, 0, n))
    return pl.pallas_call(
        body, name="ffn_bwd", grid=(nblk,),
        in_specs=[pl.BlockSpec((s, D_MODEL), lambda n: (0, 0)), pl.BlockSpec((FW, D_MODEL), lambda n: (n, 0)),
                  both(s), both(s), pl.BlockSpec((s, FW), lambda n: (0, n)), half(0, 3), half(1, 3)],
        out_specs=[both(s), pl.BlockSpec((FW, D_MODEL), lambda n: (n, 0)), both(3), both(1)],
        out_shape=[jax.ShapeDtypeStruct((2, s, D_FF), BF16), jax.ShapeDtypeStruct((D_FF, D_MODEL), BF16),
                   jax.ShapeDtypeStruct((2, 3, D_FF), F32), jax.ShapeDtypeStruct((2, 1, D_FF), F32)],
        compiler_params=_params(),
    )(dout, wdown, up, act, f, fw, fw)


def ffn_up_bwd(dout, wdown, up, act, f, fw, wup4, h2, ride=None):
    s, k = h2.shape
    nblk = D_FF // FW
    per_chip = wup4.shape[2] // FW

    def body(do_ref, wd_ref, up_ref, act_ref, f_ref, cg_ref, cv_ref, wg_ref, wv_ref, h_ref,
             dh_ref, dwu_ref, dwd_ref, dw_ref, db_ref, dup_scr):
        @pl.when(pl.program_id(0) == 0)
        def _():
            dup_scr[...] = jnp.zeros_like(dup_scr)
            dh_ref[...] = jnp.zeros_like(dh_ref)

        do = do_ref[...]
        df = _dot_nt(do, wd_ref[...])
        dg = dup_scr[0]
        dv = dup_scr[1]
        h = h_ref[...]
        dh_ref[...] += _dot_nt(dg, wg_ref[0]) + _dot_nt(dv, wv_ref[0])
        dwu_ref[0] = _dot_tn(h, dg).astype(BF16)
        dwu_ref[1] = _dot_tn(h, dv).astype(BF16)
        dwd_ref[...] = _dot_tn(f_ref[...], do).astype(BF16)
        val = act_ref[1].astype(F32)
        ge, dge = _gelu_and_grad(act_ref[0].astype(F32))
        dgate = _advances(df * val * dge, 3)
        dval = _advances(df * ge, 3)
        dw_ref[0] = _conv_wgrad(dgate, up_ref[0])
        dw_ref[1] = _conv_wgrad(dval, up_ref[1])
        db_ref[0] = jnp.sum(dgate[0], axis=0, keepdims=True)
        db_ref[1] = jnp.sum(dval[0], axis=0, keepdims=True)
        dup_scr[0] = _taps_sum(dgate, cg_ref).astype(BF16)
        dup_scr[1] = _taps_sum(dval, cv_ref).astype(BF16)

    cur = lambda n: jnp.minimum(n, nblk - 1)
    prev = lambda n: jnp.maximum(n - 1, 0)
    once = pl.Buffered(1)
    both = lambda rows: pl.BlockSpec((2, rows, FW), lambda n: (0, 0, cur(n)))
    taps = lambda h: pl.BlockSpec((3, FW), lambda n, h=h: (0, h * nblk + cur(n)))
    wcols = lambda h: pl.BlockSpec((1, k, FW), lambda n, h=h: (prev(n) // per_chip + 2 * h, 0, prev(n) % per_chip))
    return _call(
        body, name="ffn_up_bwd", grid=(nblk + 1,),
        in_specs=[pl.BlockSpec((s, D_MODEL), lambda n: (0, 0), pipeline_mode=once),
                  pl.BlockSpec((FW, D_MODEL), lambda n: (cur(n), 0)), both(s), both(s),
                  pl.BlockSpec((s, FW), lambda n: (0, cur(n))), taps(0), taps(1), wcols(0), wcols(1),
                  pl.BlockSpec((s, k), lambda n: (0, 0), pipeline_mode=once)],
        out_specs=[pl.BlockSpec((s, k), lambda n: (0, 0), pipeline_mode=once),
                   pl.BlockSpec((2, k, FW), lambda n: (0, 0, prev(n))),
                   pl.BlockSpec((FW, D_MODEL), lambda n: (cur(n), 0)), both(3), both(1)],
        out_shape=[jax.ShapeDtypeStruct((s, k), F32), jax.ShapeDtypeStruct((2, k, D_FF), BF16),
                   jax.ShapeDtypeStruct((D_FF, D_MODEL), BF16),
                   jax.ShapeDtypeStruct((2, 3, D_FF), F32), jax.ShapeDtypeStruct((2, 1, D_FF), F32)],
        scratch_shapes=[pltpu.VMEM((2, s, FW), BF16)],
        operands=(dout, wdown, up, act, f, fw, fw, wup4, wup4, h2), ride=ride)


def dgrad_wgrad_cols(dy, w4, a, name, ride=None):
    m, k = a.shape
    nj, _, ns = w4.shape
    nb = ns // CW
    per_seg = dy[0].shape[2] // CW
    first = [sum(d.shape[0] for d in dy[:i]) for i in range(len(dy))]

    def segment(j, b):
        return (j * nb + b) // per_seg, (j * nb + b) % per_seg

    def body(*refs):
        dy_refs, (w_ref, a_ref, da_ref, dw_ref) = refs[:len(dy)], refs[len(dy):]

        @pl.when((pl.program_id(0) == 0) & (pl.program_id(1) == 0))
        def _():
            da_ref[...] = jnp.zeros_like(da_ref)

        seg, _ = segment(pl.program_id(0), pl.program_id(1))
        dyb = dy_refs[-1][0]
        for i in range(len(dy) - 2, -1, -1):
            dyb = jnp.where(seg < first[i + 1], dy_refs[i][0], dyb)
        da_ref[...] += _dot_nt(dyb, w_ref[0])
        dw_ref[...] = _dot_tn(a_ref[...], dyb).astype(BF16)

    def dy_spec(i):
        nseg = dy[i].shape[0]

        def index(j, b):
            seg, col = segment(j, b)
            local = seg - first[i]
            return (jnp.clip(local, 0, nseg - 1), 0,
                    jnp.where(local < 0, 0, jnp.where(local >= nseg, per_seg - 1, col)))

        return pl.BlockSpec((1, m, CW), index)

    return _call(
        body, name=name, grid=(nj, nb),
        in_specs=[dy_spec(i) for i in range(len(dy))]
        + [pl.BlockSpec((1, k, CW), lambda j, b: (j, 0, b)), pl.BlockSpec((m, k), lambda j, b: (0, 0))],
        out_specs=[pl.BlockSpec((m, k), lambda j, b: (0, 0)),
                   pl.BlockSpec((k, CW), lambda j, b: (0, j * nb + b))],
        out_shape=[jax.ShapeDtypeStruct((m, k), F32), jax.ShapeDtypeStruct((k, nj * ns), BF16)],
        operands=(*dy, w4, a), ride=ride)


def norms_mid_bwd(dh2, x2, dy, mix, g3, g2, ride=None):
    s, d = x2.shape
    t = _token_tile(s)

    def body(dh2_ref, x2_ref, dy_ref, mix_ref, g3_ref, g2_ref, dx2_ref, dmix_ref, dg3_ref, dg2_ref):
        @pl.when(pl.program_id(0) == 0)
        def _():
            dg3_ref[...] = jnp.zeros_like(dg3_ref)
            dg2_ref[...] = jnp.zeros_like(dg2_ref)

        n3, r3 = _rms_stats(x2_ref[...])
        dx, dg3 = _rms_bwd(n3, r3, g3_ref[...], dh2_ref[...])
        dx2 = dy_ref[...] + dx
        dx2_ref[...] = dx2
        dg3_ref[...] += jnp.sum(dg3, axis=0, keepdims=True)
        n2, r2 = _rms_stats(mix_ref[...])
        dmix, dg2 = _rms_bwd(n2, r2, g2_ref[...], dx2)
        dmix_ref[...] = dmix.astype(BF16)
        dg2_ref[...] += jnp.sum(dg2, axis=0, keepdims=True)

    tile = pl.BlockSpec((t, d), lambda i: (i, 0))
    vec = pl.BlockSpec((1, d), lambda i: (0, 0))
    v = jax.ShapeDtypeStruct((1, d), F32)
    return _call(
        body, name="norms_mid_bwd", grid=(s // t,),
        in_specs=[tile, tile, tile, tile, vec, vec],
        out_specs=[tile, tile, vec, vec],
        out_shape=[jax.ShapeDtypeStruct((s, d), F32), jax.ShapeDtypeStruct((s, d), BF16), v, v],
        operands=(dh2, x2, dy, mix, g3, g2), ride=ride)


def mix_out_bwd(dmix, wout, merged, a, b, proj, ride=None):
    s = dmix.shape[0]
    nblk = D_MODEL // CW

    def body(dm_ref, w_ref, mg_ref, a_ref, b_ref, gc_ref, gl_ref, da_ref, db_ref, dw_ref, dg_ref):
        dm = dm_ref[...]
        dmerged = _dot_nt(dm, w_ref[...])
        dw_ref[...] = _dot_tn(mg_ref[...], dm).astype(BF16)
        sc = _sigmoid(gc_ref[...])
        sl = _sigmoid(gl_ref[...])
        da_ref[...] = (dmerged * sc).astype(BF16)
        db_ref[...] = (dmerged * sl).astype(BF16)
        dg_ref[0] = (dmerged * a_ref[...] * sc * (1.0 - sc)).astype(BF16)
        dg_ref[1] = (dmerged * b_ref[...] * sl * (1.0 - sl)).astype(BF16)

    res = pl.BlockSpec((s, D_MODEL), lambda n: (0, 0))
    rows = pl.BlockSpec((CW, D_MODEL), lambda n: (n, 0))
    col = pl.BlockSpec((s, CW), lambda n: (0, n))
    blk = lambda k: pl.BlockSpec((s, CW), lambda n, k=k: (0, k * nblk + n))
    hb = jax.ShapeDtypeStruct((s, D_MODEL), BF16)
    return _call(
        body, name="mix_out_bwd", grid=(nblk,),
        in_specs=[res, rows, col, col, col, blk(5), blk(6)],
        out_specs=[col, col, rows, pl.BlockSpec((2, s, CW), lambda n: (0, 0, n))],
        out_shape=[hb, hb, jax.ShapeDtypeStruct((D_MODEL, D_MODEL), BF16), jax.ShapeDtypeStruct((2, s, D_MODEL), BF16)],
        operands=(dmix, wout, merged, a, b, proj, proj), ride=ride)


def mix_conv_bwd(da, wcb, proj, q, ws, ride=None):
    s = da.shape[0]
    nblk = D_MODEL // CW

    def body(da_ref, w_ref, cb_ref, cc_ref, cx_ref, q_ref, ws_ref, dc_ref, dw_ref, dws_ref):
        dab = da_ref[...]
        dya = _dot_nt(dab, w_ref[...])
        cb = cb_ref[...]
        cc = cc_ref[...]
        cx = cx_ref[...]
        q = q_ref[...]
        dw_ref[...] = _dot_tn((cb * q).astype(BF16), dab).astype(BF16)
        dc_ref[0] = (dya * q).astype(BF16)
        dq = _advances(dya * cb, 3)
        dp = _taps_sum(dq, ws_ref)
        dws_ref[...] = _conv_wgrad(dq, cc * cx)
        dc_ref[1] = (dp * cx).astype(BF16)
        dc_ref[2] = (dp * cc).astype(BF16)

    res = pl.BlockSpec((s, D_MODEL), lambda n: (0, 0))
    rows = pl.BlockSpec((CW, D_MODEL), lambda n: (n, 0))
    col = pl.BlockSpec((s, CW), lambda n: (0, n))
    blk = lambda k: pl.BlockSpec((s, CW), lambda n, k=k: (0, k * nblk + n))
    taps = pl.BlockSpec((3, CW), lambda n: (0, n))
    hb = jax.ShapeDtypeStruct((s, D_MODEL), BF16)
    return _call(
        body, name="mix_conv_bwd", grid=(nblk,),
        in_specs=[res, rows, blk(0), blk(1), blk(2), col, taps],
        out_specs=[pl.BlockSpec((3, s, CW), lambda n: (0, 0, n)), rows, taps],
        out_shape=[jax.ShapeDtypeStruct((3, s, D_MODEL), BF16), jax.ShapeDtypeStruct((D_MODEL, D_MODEL), BF16),
                   jax.ShapeDtypeStruct((3, D_MODEL), F32)],
        operands=(da, wcb, proj, proj, proj, q, ws), ride=ride)


def mix_lru_bwd(db, wlb, proj, xl, r, i, h, wl, wa, wx, lam, ride=None):
    s = db.shape[0]
    nblk = D_MODEL // CW

    def body(db_ref, w_ref, lx_ref, ly_ref, xl_ref, r_ref, i_ref, h_ref, wl_ref, wa_ref, wx_ref, lam_ref,
             dl_ref, dw_ref, dwa_ref, dwx_ref, dba_ref, dbx_ref, dwl_ref, dbl_ref, dlam_ref,
             c_scr, g_scr):
        dbb = db_ref[...]
        dyb = _dot_nt(dbb, w_ref[...])
        h = h_ref[...]
        ge, dge = _gelu_and_grad(ly_ref[...])
        dw_ref[...] = _dot_tn((h * ge).astype(BF16), dbb).astype(BF16)
        dl_ref[1] = (dyb * h * dge).astype(BF16)
        r = r_ref[...].astype(F32)
        gi = i_ref[...].astype(F32)
        xlb = xl_ref[...]
        xl = xlb.astype(F32)
        lam = lam_ref[...]
        ls = _log_sigmoid(lam)
        a, mult = _lru_gates(r, ls)
        c_scr[...] = _shift_up(a, 1)
        g_scr[...] = dyb * ge
        _scan_backward(c_scr, g_scr, g_scr)
        du = g_scr[...]
        da = du * _shift_down(h, 1)
        dmult = du * gi * xl
        di = du * mult * xl
        dxl = du * mult * gi
        first = _rows(a.shape) == 0
        dlog_a = da * a - jnp.where(first, 0.0, dmult * a * a / mult)
        dr = dlog_a * (LRU_C * ls)
        dlam_ref[...] = jnp.sum(dlog_a * r, axis=0, keepdims=True) * (LRU_C * (1.0 - _sigmoid(lam)))
        dzr = dr * r * (1.0 - r)
        dzi = di * gi * (1.0 - gi)
        dba_ref[...] = jnp.sum(dzr, axis=0, keepdims=True)
        dbx_ref[...] = jnp.sum(dzi, axis=0, keepdims=True)
        dzrb = dzr.astype(BF16)
        dzib = dzi.astype(BF16)
        dwa_ref[0] = _dot_tn(xlb, dzrb)
        dwx_ref[0] = _dot_tn(xlb, dzib)
        dxl = _advances(dxl + _dot_nt(dzrb, wa_ref[0]) + _dot_nt(dzib, wx_ref[0]), 4)
        dl_ref[0] = _taps_sum(dxl, wl_ref).astype(BF16)
        dwl_ref[...] = _conv_wgrad(dxl, lx_ref[...])
        dbl_ref[...] = jnp.sum(dxl[0], axis=0, keepdims=True)

    res = pl.BlockSpec((s, D_MODEL), lambda n: (0, 0), pipeline_mode=pl.Buffered(1))
    rows = pl.BlockSpec((CW, D_MODEL), lambda n: (n, 0))
    col = pl.BlockSpec((s, CW), lambda n: (0, n))
    blk = lambda k: pl.BlockSpec((s, CW), lambda n, k=k: (0, k * nblk + n))
    taps = pl.BlockSpec((4, CW), lambda n: (0, n))
    vec = pl.BlockSpec((1, CW), lambda n: (0, n))
    mat = pl.BlockSpec((1, CW, CW), lambda n: (n, 0, 0))
    hb = jax.ShapeDtypeStruct((s, D_MODEL), BF16)
    v = jax.ShapeDtypeStruct((1, D_MODEL), F32)
    m = jax.ShapeDtypeStruct((LRU_HEADS, HEAD_DIM, HEAD_DIM), F32)
    scr = pltpu.VMEM((s, CW), F32)
    return _call(
        body, name="mix_lru_bwd", grid=(nblk,),
        in_specs=[res, rows, blk(3), blk(4), col, col, col, col, taps, mat, mat, vec],
        out_specs=[pl.BlockSpec((2, s, CW), lambda n: (0, 0, n)), rows, mat, mat, vec, vec, taps, vec, vec],
        out_shape=[jax.ShapeDtypeStruct((2, s, D_MODEL), BF16), jax.ShapeDtypeStruct((D_MODEL, D_MODEL), BF16), m, m, v, v,
                   jax.ShapeDtypeStruct((4, D_MODEL), F32), v, v],
        scratch_shapes=[scr, scr],
        operands=(db, wlb, proj, proj, xl, r, i, h, wl, wa, wx, lam), ride=ride)


def norm_in_bwd(dh1, x, dx2, g1):
    s, d = x.shape
    t = _token_tile(s)

    def body(dh_ref, x_ref, dx2_ref, g_ref, dx_ref, dg_ref):
        @pl.when(pl.program_id(0) == 0)
        def _():
            dg_ref[...] = jnp.zeros_like(dg_ref)

        n, r = _rms_stats(x_ref[...])
        dx, dg = _rms_bwd(n, r, g_ref[...], dh_ref[...])
        dx_ref[...] = dx2_ref[...] + dx
        dg_ref[...] += jnp.sum(dg, axis=0, keepdims=True)

    tile = pl.BlockSpec((t, d), lambda i: (i, 0))
    vec = pl.BlockSpec((1, d), lambda i: (0, 0))
    return pl.pallas_call(
        body, name="norm_in_bwd", grid=(s // t,),
        in_specs=[tile, tile, tile, vec],
        out_specs=[tile, vec],
        out_shape=[jax.ShapeDtypeStruct((s, d), F32), jax.ShapeDtypeStruct((1, d), F32)],
        compiler_params=_params(),
    )(dh1, x, dx2, g1)


def local_step(x, target, g1, g2, g3, g4, win4, ws, wcb, wl, bl, wa, ba, wx, bx, lam, wlb, wout, wup4, fw, fb, wdown):
    h1 = norm_in(x, g1)
    proj = matmul_cols(h1, win4, "proj_fwd")
    q, ya = mix_conv_fwd(proj, ws)
    xl, r, gi, h, yb = mix_lru_fwd(proj, wl, bl, wa, ba, wx, bx, lam)
    a, b, merged = branch_merge_fwd(ya, yb, wcb, wlb, proj)
    mix, x2, h2 = mix_out_fwd(merged, wout, x, g2, g3)
    up = matmul_cols(h2, wup4, "up_fwd")
    f = ffn_act_fwd(up, fw, fb)
    dy, dout, loss, dg4 = ffn_down_loss(f, wdown, x2, target, g4)

    dug, duv, dwdown, dfw_g, dfw_v, dfb_g, dfb_v = ffn_bwd(dout, wdown, up, fw, fb)
    dup = jnp.concatenate([dug, duv], axis=1)
    dfw = jnp.concatenate([dfw_g, dfw_v], axis=1)
    dfb = jnp.concatenate([dfb_g, dfb_v], axis=1)
    dh2, dwup = dgrad_wgrad_cols(dup, wup4, h2, "up_bwd")
    dx2, dmix, dg3, dg2 = norms_mid_bwd(dh2, x2, dy, mix, g3, g2)
    da, db, dwout, dgc, dgl = mix_out_bwd(dmix, wout, merged, a, b, proj)
    dcb, dcc, dcx, dwcb, dws = mix_conv_bwd(da, wcb, proj, q, ws)
    dlx, dly, dwlb, dwa, dwx, dba, dbx, dwl, dbl, dlam = mix_lru_bwd(db, wlb, proj, xl, r, gi, h, wl, wa, wx, lam)
    dproj = jnp.concatenate([dcb, dcc, dcx, dlx, dly, dgc[:, 5 * D_MODEL:6 * D_MODEL], dgl[:, 6 * D_MODEL:]], axis=1)
    dh1, dwin = dgrad_wgrad_cols(dproj, win4, h1, "proj_bwd")
    dx, dg1 = norm_in_bwd(dh1, x, dx2, g1)
    grads = dict(norm_mix_pre=dg1, norm_mix_post=dg2, norm_ffn_pre=dg3, norm_ffn_post=dg4,
                 w_in=dwin, conv_short_w=dws, w_conv_branch=dwcb, lru_conv_w=dwl, lru_conv_b=dbl,
                 lru_wa=dwa, lru_ba=dba, lru_wx=dwx, lru_bx=dbx, lru_lambda=dlam,
                 w_lru_branch=dwlb, w_out=dwout, ffn_w_up=dwup, ffn_conv_w=dfw, ffn_conv_b=dfb,
                 ffn_w_down=dwdown)
    return loss[0, 0], dx, grads


MESH = pl.DeviceIdType.MESH
_HBM = pl.BlockSpec(memory_space=pltpu.HBM)
_OTHER_CHIPS = ((1, 0), (0, 1), (1, 1))
_OTHER_DEVICES = tuple((dx, dy, dc) for dx in (0, 1) for dy in (0, 1) for dc in (0, 1) if dx or dy or dc)
N_DEVICES = 8


def _position():
    return lax.axis_index("x"), lax.axis_index("y"), lax.axis_index("c")


def _flip(v, d):
    return 1 - v if d else v


def _half_rows(ref, h, hr):
    return ref.at[pl.ds(h * hr, hr), :]


def gather_chips(shards):
    n = len(shards)
    nrel = len(_OTHER_CHIPS)

    def body(*refs):
        ins, outs = refs[:n], refs[n:2 * n]
        ici_send, ici_recv, sib_send, sib_recv = refs[2 * n:]
        x, y, c = _position()
        j = 2 * x + y
        hr = [s.shape[0] // 2 for s in shards]

        def chip(p):
            px, py = _flip(x, _OTHER_CHIPS[p][0]), _flip(y, _OTHER_CHIPS[p][1])
            return px, py, 2 * px + py

        def ici(a, p, slot):
            px, py, _ = chip(p)
            return pltpu.make_async_remote_copy(
                src_ref=_half_rows(ins[a], c, hr[a]), dst_ref=_half_rows(outs[a].at[slot], c, hr[a]),
                send_sem=ici_send.at[a * nrel + p], recv_sem=ici_recv.at[a * nrel + p],
                device_id=(px, py, c), device_id_type=MESH)

        def sib(a, p, h):
            _, _, k = chip(p)
            part = _half_rows(outs[a].at[k], h, hr[a])
            return pltpu.make_async_remote_copy(
                src_ref=part, dst_ref=part, send_sem=sib_send.at[a * nrel + p], recv_sem=sib_recv.at[a * nrel + p],
                device_id=(x, y, 1 - c), device_id_type=MESH)

        pairs = [(a, p) for a in range(n) for p in range(nrel)]
        for a, p in pairs:
            ici(a, p, j).start()
        for a, p in pairs:
            ici(a, p, chip(p)[2]).wait_recv()
            sib(a, p, c).start()
        for a, p in pairs:
            sib(a, p, 1 - c).wait_recv()
        for a, p in pairs:
            ici(a, p, j).wait_send()
            sib(a, p, c).wait_send()

    got = pl.pallas_call(
        body, name="gather_chips",
        in_specs=[_HBM] * n, out_specs=[_HBM] * n,
        out_shape=[jax.ShapeDtypeStruct((N_CHIPS,) + s.shape, s.dtype) for s in shards],
        scratch_shapes=[pltpu.SemaphoreType.DMA((n * nrel,))] * 4,
    )(*shards)
    j = 2 * lax.axis_index("x") + lax.axis_index("y")
    return [lax.dynamic_update_slice(g, s[None], (j, 0, 0)) for g, s in zip(got, shards)]


def _owned_part(ref, kind, k, h, hr):
    if kind == "col":
        ns = ref.shape[1] // N_CHIPS
        return ref.at[pl.ds(h * hr, hr), pl.ds(k * ns, ns)]
    if kind == "row":
        return ref.at[pl.ds(k * 2 * hr + h * hr, hr), :]
    if kind == "col2":
        ns = ref.shape[2] // 2
        return ref.at[k // 2, pl.ds(h * hr, hr), pl.ds((k % 2) * ns, ns)]
    return ref.at[k, pl.ds(h * hr, hr), :]


def _part_shape(g, kind):
    if kind == "col2":
        return g.shape[1] // 2, g.shape[2] // 2
    if kind == "col":
        return g.shape[0] // 2, g.shape[1] // N_CHIPS
    if kind == "row":
        return g.shape[0] // (2 * N_CHIPS), g.shape[1]
    return g.shape[1] // 2, g.shape[2]


def pair_split(grads, kinds, name):
    n = len(grads)
    shapes = [_part_shape(g, k) for g, k in zip(grads, kinds)]

    def body(*refs):
        ins, theirs = refs[:n], refs[n:2 * n]
        send_sem, recv_sem = refs[2 * n:]
        x, y, c = _position()
        copies = []
        for a in range(n):
            hr = shapes[a][0]
            for k in range(N_CHIPS):
                s = a * N_CHIPS + k
                copies.append(pltpu.make_async_remote_copy(
                    src_ref=_owned_part(ins[a], kinds[a], k, 1 - c, hr), dst_ref=theirs[a].at[k],
                    send_sem=send_sem.at[s], recv_sem=recv_sem.at[s], device_id=(x, y, 1 - c), device_id_type=MESH))
        for cp in copies:
            cp.start()
        for cp in copies:
            cp.wait()

    return pl.pallas_call(
        body, name=name,
        in_specs=[_HBM] * n, out_specs=[_HBM] * n,
        out_shape=[jax.ShapeDtypeStruct((N_CHIPS,) + shp, g.dtype) for shp, g in zip(shapes, grads)],
        scratch_shapes=[pltpu.SemaphoreType.DMA((n * N_CHIPS,))] * 2,
    )(*grads)


def chip_exchange(sums, rep):
    n = len(sums)
    nrel = len(_OTHER_CHIPS)
    ndev = len(_OTHER_DEVICES)

    def body(*refs):
        ins, rep_ref = refs[:n], refs[n]
        outs, rep_out = refs[n + 1:2 * n + 1], refs[2 * n + 1]
        loc_sem, send_sem, recv_sem, rep_send, rep_recv = refs[2 * n + 2:]
        x, y, c = _position()
        j = 2 * x + y
        me = 4 * x + 2 * y + c

        def chip(p):
            px, py = _flip(x, _OTHER_CHIPS[p][0]), _flip(y, _OTHER_CHIPS[p][1])
            return px, py, 2 * px + py

        def part(a, p, src_slot, dst_slot):
            px, py, _ = chip(p)
            return pltpu.make_async_remote_copy(
                src_ref=ins[a].at[src_slot], dst_ref=outs[a].at[dst_slot],
                send_sem=send_sem.at[a * nrel + p], recv_sem=recv_sem.at[a * nrel + p],
                device_id=(px, py, c), device_id_type=MESH)

        def device(q):
            dx, dy, dc = _OTHER_DEVICES[q]
            return _flip(x, dx), _flip(y, dy), _flip(c, dc)

        def rep_copy(q, slot):
            return pltpu.make_async_remote_copy(
                src_ref=rep_ref, dst_ref=rep_out.at[slot], send_sem=rep_send.at[q], recv_sem=rep_recv.at[q],
                device_id=device(q), device_id_type=MESH)

        own = [pltpu.make_async_copy(ins[a].at[j], outs[a].at[j], loc_sem.at[a]) for a in range(n)]
        own.append(pltpu.make_async_copy(rep_ref, rep_out.at[me], loc_sem.at[n]))
        for cp in own:
            cp.start()
        pairs = [(a, p) for a in range(n) for p in range(nrel)]
        for a, p in pairs:
            part(a, p, chip(p)[2], j).start()
        for q in range(ndev):
            rep_copy(q, me).start()
        for a, p in pairs:
            part(a, p, chip(p)[2], chip(p)[2]).wait_recv()
        for q in range(ndev):
            px, py, pc = device(q)
            rep_copy(q, 4 * px + 2 * py + pc).wait_recv()
        for a, p in pairs:
            part(a, p, chip(p)[2], j).wait_send()
        for q in range(ndev):
            rep_copy(q, me).wait_send()
        for cp in own:
            cp.wait()

    return pl.pallas_call(
        body, name="chip_exchange",
        in_specs=[_HBM] * (n + 1), out_specs=[_HBM] * (n + 1),
        out_shape=[jax.ShapeDtypeStruct(s.shape, s.dtype) for s in sums]
        + [jax.ShapeDtypeStruct((N_DEVICES,) + rep.shape, rep.dtype)],
        scratch_shapes=[pltpu.SemaphoreType.DMA((n + 1,)), pltpu.SemaphoreType.DMA((n * nrel,)),
                        pltpu.SemaphoreType.DMA((n * nrel,)), pltpu.SemaphoreType.DMA((ndev,)),
                        pltpu.SemaphoreType.DMA((ndev,))],
    )(*sums, rep)


def pair_swap(halves):
    n = len(halves)

    def body(*refs):
        ins, outs = refs[:n], refs[n:2 * n]
        send_sem, recv_sem = refs[2 * n:]
        x, y, c = _position()
        copies = [pltpu.make_async_remote_copy(
            src_ref=ins[a], dst_ref=outs[a], send_sem=send_sem.at[a], recv_sem=recv_sem.at[a],
            device_id=(x, y, 1 - c), device_id_type=MESH) for a in range(n)]
        for cp in copies:
            cp.start()
        for cp in copies:
            cp.wait()

    return pl.pallas_call(
        body, name="pair_swap",
        in_specs=[_HBM] * n, out_specs=[_HBM] * n,
        out_shape=[jax.ShapeDtypeStruct(h.shape, h.dtype) for h in halves],
        scratch_shapes=[pltpu.SemaphoreType.DMA((n,))] * 2,
    )(*halves)


def _row_tile(rows, cols, limit_bytes=1 << 20):
    best = None
    for t in range(SUBLANES, rows + 1, SUBLANES):
        if rows % t == 0 and t * cols * 4 <= limit_bytes:
            best = t
    return best or rows


def add_pair(g, kind, theirs, core, name):
    nc, rows, cols = theirs.shape
    t = _row_tile(rows, cols, 4 << 20)
    nt = rows // t

    def body(core_ref, g_ref, b_ref, o_ref):
        mine = g_ref[...].reshape(t, cols)
        o_ref[0] = (mine.astype(F32) + b_ref[0].astype(F32)).astype(o_ref.dtype)

    if kind == "col":
        own = pl.BlockSpec((t, cols), lambda k, i, c: (c[0] * nt + i, k))
    elif kind == "col2":
        own = pl.BlockSpec((1, t, cols), lambda k, i, c: (k // 2, c[0] * nt + i, k % 2))
    elif kind == "row":
        own = pl.BlockSpec((t, cols), lambda k, i, c: ((2 * k + c[0]) * nt + i, 0))
    else:
        own = pl.BlockSpec((1, t, cols), lambda k, i, c: (k, c[0] * nt + i, 0))
    spec = pl.BlockSpec((1, t, cols), lambda k, i, c: (k, i, 0))
    return pl.pallas_call(
        body, name=name,
        grid_spec=pltpu.PrefetchScalarGridSpec(num_scalar_prefetch=1, grid=(nc, nt), in_specs=[own, spec], out_specs=spec),
        out_shape=jax.ShapeDtypeStruct(theirs.shape, theirs.dtype), compiler_params=_params(),
    )(core, g, theirs)


def sum_lead(a, name):
    nl, rows, cols = a.shape
    t = _row_tile(rows, cols, (1 << 20) // 2)

    def body(a_ref, o_ref):
        acc = a_ref[0].astype(F32)
        for s in range(1, nl):
            acc = acc + a_ref[s].astype(F32)
        o_ref[...] = acc

    return pl.pallas_call(
        body, name=name, grid=(rows // t,),
        in_specs=[pl.BlockSpec((nl, t, cols), lambda i: (0, i, 0))],
        out_specs=pl.BlockSpec((t, cols), lambda i: (i, 0)),
        out_shape=jax.ShapeDtypeStruct((rows, cols), F32), compiler_params=_params(),
    )(a)


def sum_chips(rx, csum, chip, name):
    nc, rows, cols = rx.shape
    t = _row_tile(rows, cols, 2 << 20)

    def body(chip_ref, r0, r1, r2, r3, own_ref, o_ref):
        acc = None
        for s, ref in enumerate((r0, r1, r2, r3)):
            term = jnp.where(chip_ref[0] == s, own_ref[0], ref[0]).astype(F32)
            acc = term if acc is None else acc + term
        o_ref[...] = acc

    def slot(s):
        return pl.BlockSpec((1, t, cols), lambda i, c, s=s: (jnp.where(c[0] == s, c[0] ^ 1, s), i, 0))

    return pl.pallas_call(
        body, name=name,
        grid_spec=pltpu.PrefetchScalarGridSpec(
            num_scalar_prefetch=1, grid=(rows // t,),
            in_specs=[slot(s) for s in range(nc)] + [pl.BlockSpec((1, t, cols), lambda i, c: (c[0], i, 0))],
            out_specs=pl.BlockSpec((t, cols), lambda i, c: (i, 0))),
        out_shape=jax.ShapeDtypeStruct((rows, cols), F32), compiler_params=_params(),
    )(chip, rx, rx, rx, rx, csum)


def _adamw_update(w, g, m, v):
    nm = ADAM_B1 * m + (1.0 - ADAM_B1) * g
    nv = ADAM_B2 * v + (1.0 - ADAM_B2) * (g * g)
    m_hat = nm * (1.0 / (1.0 - ADAM_B1 ** ADAM_STEP))
    v_hat = nv * (1.0 / (1.0 - ADAM_B2 ** ADAM_STEP))
    return -ADAM_LR * (m_hat / (jnp.sqrt(v_hat) + ADAM_EPS) + ADAM_WD * w), nm, nv


def adamw(w, g, m, v, name):
    rows, cols = w.shape
    t = _row_tile(rows, cols)

    def body(w_ref, g_ref, m_ref, v_ref, d_ref, nm_ref, nv_ref):
        d_ref[...], nm_ref[...], nv_ref[...] = _adamw_update(w_ref[...], g_ref[...], m_ref[...], v_ref[...])

    spec = pl.BlockSpec((t, cols), lambda i: (i, 0))
    shp = jax.ShapeDtypeStruct((rows, cols), F32)
    return pl.pallas_call(
        body, name=name, grid=(rows // t,), in_specs=[spec] * 4, out_specs=[spec] * 3,
        out_shape=[shp, shp, shp], compiler_params=_params(),
    )(w, g, m, v)


def adamw_halves(w, g_mine, g_other, m, v, core, name):
    rows, cols = w.shape
    hr = rows // 2
    t = _row_tile(hr, cols)
    nt = hr // t

    def body(core_ref, w_ref, gm_ref, go_ref, m_ref, v_ref, g_ref, d_ref, nm_ref, nv_ref):
        g = jnp.where(pl.program_id(0) // nt == core_ref[0], gm_ref[...], go_ref[...])
        g_ref[...] = g
        d_ref[...], nm_ref[...], nv_ref[...] = _adamw_update(w_ref[...], g, m_ref[...], v_ref[...])

    spec = pl.BlockSpec((t, cols), lambda i, c: (i, 0))
    half = pl.BlockSpec((t, cols), lambda i, c: (i % nt, 0))
    shp = jax.ShapeDtypeStruct((rows, cols), F32)
    return pl.pallas_call(
        body, name=name,
        grid_spec=pltpu.PrefetchScalarGridSpec(num_scalar_prefetch=1, grid=(2 * nt,),
                                               in_specs=[spec, half, half, spec, spec], out_specs=[spec] * 4),
        out_shape=[shp] * 4, compiler_params=_params(),
    )(core, w, g_mine, g_other, m, v)


WEIGHTS = ("norm_mix_pre", "norm_mix_post", "norm_ffn_pre", "norm_ffn_post", "w_in", "conv_short_w",
           "w_conv_branch", "lru_conv_w", "lru_conv_b", "lru_wa", "lru_ba", "lru_wx", "lru_bx", "lru_lambda",
           "w_lru_branch", "w_out", "ffn_w_up", "ffn_conv_w", "ffn_conv_b", "ffn_w_down")
BIG = ("w_in", "ffn_w_up", "w_conv_branch", "w_lru_branch", "w_out", "ffn_w_down")
BIG_KIND = ("col", "col", "row", "row", "row", "row")
SMALL = ("conv_short_w", "lru_conv_w", "lru_wa", "lru_ba", "lru_wx", "lru_bx", "ffn_conv_w")
REPL = ("norm_mix_pre", "norm_mix_post", "norm_ffn_pre", "norm_ffn_post", "lru_conv_b", "lru_lambda", "ffn_conv_b")
PACK_W = 256
SMALL_ROWS = 576
REPL_ROWS = 16
LOSS_ROW = 12
FFN_SHARD = 2 * D_FF // N_CHIPS
QUARTER = HEAD_DIM // N_CHIPS
SMALL_PARTS = (("conv_short_w", 3, (1, 3, PACK_W)), ("lru_conv_w", 4, (1, 4, PACK_W)),
               ("lru_wa", LRU_HEADS * QUARTER, (1, LRU_HEADS, QUARTER, HEAD_DIM)), ("lru_ba", 1, (1, LRU_HEADS, QUARTER)),
               ("lru_wx", LRU_HEADS * QUARTER, (1, LRU_HEADS, QUARTER, HEAD_DIM)), ("lru_bx", 1, (1, LRU_HEADS, QUARTER)),
               ("ffn_conv_w", 3 * FFN_SHARD // PACK_W, (1, 3, FFN_SHARD)))


def _pad8(nr):
    return -(-nr // SUBLANES) * SUBLANES


def _pack_small_shard(p):
    rows = [jnp.pad(p[name].reshape(nr, PACK_W), ((0, _pad8(nr) - nr), (0, 0))) for name, nr, _ in SMALL_PARTS]
    used = sum(r.shape[0] for r in rows)
    return jnp.concatenate(rows + [jnp.zeros((SMALL_ROWS - used, PACK_W), F32)], axis=0)


def _unpack_small_shard(buf):
    out, r = {}, 0
    for name, nr, shape in SMALL_PARTS:
        out[name] = buf[r:r + nr].reshape(shape)
        r += _pad8(nr)
    return out


def _full_small(g4):
    per = [_unpack_small_shard(g4[k]) for k in range(N_CHIPS)]
    cat = lambda name, axis: jnp.concatenate([per[k][name][0] for k in range(N_CHIPS)], axis=axis)
    return dict(conv_short_w=cat("conv_short_w", 1), lru_conv_w=cat("lru_conv_w", 1),
                lru_wa=cat("lru_wa", 1), lru_ba=cat("lru_ba", 1).reshape(1, D_MODEL),
                lru_wx=cat("lru_wx", 1), lru_bx=cat("lru_bx", 1).reshape(1, D_MODEL),
                ffn_conv_w=cat("ffn_conv_w", 1))


def _split_small(full):
    shards = []
    for k in range(N_CHIPS):
        cols = lambda a, w: a[:, k * w:(k + 1) * w]
        q = slice(k * QUARTER, (k + 1) * QUARTER)
        shards.append(_pack_small_shard(dict(
            conv_short_w=cols(full["conv_short_w"], PACK_W), lru_conv_w=cols(full["lru_conv_w"], PACK_W),
            lru_wa=full["lru_wa"][:, q, :], lru_ba=full["lru_ba"].reshape(LRU_HEADS, HEAD_DIM)[:, q],
            lru_wx=full["lru_wx"][:, q, :], lru_bx=full["lru_bx"].reshape(LRU_HEADS, HEAD_DIM)[:, q],
            ffn_conv_w=cols(full["ffn_conv_w"], FFN_SHARD))))
    return jnp.stack(shards)


def _pack_repl(p, loss=None):
    rows = [p[n].reshape(-1, D_MODEL) for n in REPL]
    if loss is not None:
        rows.append(jnp.broadcast_to(loss.reshape(1, 1), (1, D_MODEL)))
    used = sum(r.shape[0] for r in rows)
    return jnp.concatenate(rows + [jnp.zeros((REPL_ROWS - used, D_MODEL), F32)], axis=0)


def _unpack_repl(buf):
    out, r = {}, 0
    for n in REPL:
        nr = (2 * D_FF // D_MODEL) if n == "ffn_conv_b" else 1
        out[n] = buf[r:r + nr].reshape(1, nr * D_MODEL)
        r += nr
    return out


def kernel(x, norm_mix_pre, norm_mix_post, norm_ffn_pre, norm_ffn_post, w_in, conv_short_w, w_conv_branch, lru_conv_w, lru_conv_b, lru_wa, lru_ba, lru_wx, lru_bx, lru_lambda, w_lru_branch, w_out, ffn_w_up, ffn_conv_w, ffn_conv_b, ffn_w_down, loss_target, m_norm_mix_pre, m_norm_mix_post, m_norm_ffn_pre, m_norm_ffn_post, m_w_in, m_conv_short_w, m_w_conv_branch, m_lru_conv_w, m_lru_conv_b, m_lru_wa, m_lru_ba, m_lru_wx, m_lru_bx, m_lru_lambda, m_w_lru_branch, m_w_out, m_ffn_w_up, m_ffn_conv_w, m_ffn_conv_b, m_ffn_w_down, v_norm_mix_pre, v_norm_mix_post, v_norm_ffn_pre, v_norm_ffn_post, v_w_in, v_conv_short_w, v_w_conv_branch, v_lru_conv_w, v_lru_conv_b, v_lru_wa, v_lru_ba, v_lru_wx, v_lru_bx, v_lru_lambda, v_w_lru_branch, v_w_out, v_ffn_w_up, v_ffn_conv_w, v_ffn_conv_b, v_ffn_w_down):
    given = dict(locals())
    w = {n: given[n] for n in WEIGHTS}
    m = {n: given["m_" + n] for n in WEIGHTS}
    v = {n: given["v_" + n] for n in WEIGHTS}

    xi, yi, ci = _position()
    chip_i = 2 * xi + yi
    chip = chip_i.astype(jnp.int32).reshape(1)
    core = ci.astype(jnp.int32).reshape(1)
    xs, target = x[0], loss_target[0]
    g1, g2, g3, g4 = w["norm_mix_pre"], w["norm_mix_post"], w["norm_ffn_pre"], w["norm_ffn_post"]
    shard = {n: w[n][0].astype(BF16) for n in BIG}
    small_shard = _pack_small_shard(w)

    def gathered(bufs, names):
        return [_own_slot(b, small_shard if n == "small" else shard[n], chip_i) for b, n in zip(bufs, names)]

    def chip_sums(arrays, kinds, tag):
        theirs = pair_split(arrays, kinds, "pair_split_" + tag)
        return [add_pair(g, k, t, core, "pair_add_%s_%d" % (tag, i)) for i, (g, k, t) in enumerate(zip(arrays, kinds, theirs))]

    h1 = norm_in(xs, g1)
    win4, small4 = gathered(run_ride(gather_ride([shard["w_in"], small_shard]), "gather_first"), ("w_in", "small"))
    small = _full_small(small4)
    (proj,), got = matmul_cols(h1, win4, "proj_fwd",
                               ride=gather_ride([shard["w_conv_branch"], shard["w_lru_branch"], shard["w_out"]]))
    wcb, wlb, wout = [g.reshape(-1, D_MODEL) for g in gathered(got, ("w_conv_branch", "w_lru_branch", "w_out"))]
    up_piece = lambda r0, nr, into=None: gather_ride([shard["ffn_w_up"]], items=[(0, r0, nr)], into=into)
    down_piece = lambda r0, nr, into=None: gather_ride([shard["ffn_w_down"]], items=[(0, r0, nr)], into=into)
    (q, ya), got = mix_conv_fwd(proj, small["conv_short_w"], ride=up_piece(0, 128))
    (xl, r, gi, h, yb), got = mix_lru_fwd(
        proj, small["lru_conv_w"], w["lru_conv_b"], small["lru_wa"].astype(BF16), small["lru_ba"],
        small["lru_wx"].astype(BF16), small["lru_bx"], w["lru_lambda"], ride=up_piece(128, 512, got))
    (a, b, merged), got = branch_merge_fwd(ya, yb, wcb, wlb, proj, ride=up_piece(640, 384, got))
    (wup4,) = gathered(got, ("ffn_w_up",))
    (mix, x2, h2), got = mix_out_fwd(merged, wout, xs, g2, g3, ride=down_piece(0, 256))
    (up, act, f), got = ffn_up_act_fwd(h2, wup4, small["ffn_conv_w"], w["ffn_conv_b"], ride=down_piece(256, 512, got))
    wdown = gathered(got, ("ffn_w_down",))[0].reshape(-1, D_MODEL)
    dy, dout, loss, dg4 = ffn_down_loss(f, wdown, x2, target, g4)

    dh2, dwup, dwdown, dfw, dfb = ffn_up_bwd(dout, wdown, up, act, f, small["ffn_conv_w"], wup4, h2)
    cs_down, cs_up = chip_sums([dwdown, dwup], ["row", "col2"], "ffn")
    down_rows = lambda r0, nr, into=None: exchange_ride([cs_down], items=[(0, r0, nr)], into=into)
    up_rows = lambda r0, nr, into=None: exchange_ride([cs_up], items=[(0, r0, nr)], into=into)
    (dx2, dmix, dg3, dg2), rx_down = norms_mid_bwd(dh2, x2, dy, mix, g3, g2, ride=down_rows(0, 128))
    (da, db, dwout, dgates), rx_down = mix_out_bwd(dmix, wout, merged, a, b, proj, ride=down_rows(128, 256, rx_down))
    (dconv, dwcb, dws), rx_up = mix_conv_bwd(da, wcb, proj, q, small["conv_short_w"], ride=up_rows(0, 176))
    cs_mid = chip_sums([dwout, dwcb], ["row", "row"], "mid")
    (dlru, dwlb, dwa, dwx, dba, dbx, dwl, dbl, dlam), rx_up = mix_lru_bwd(
        db, wlb, proj, xl, r, gi, h, small["lru_conv_w"], small["lru_wa"].astype(BF16), small["lru_wx"].astype(BF16),
        w["lru_lambda"], ride=up_rows(176, 336, rx_up))
    grads = dict(norm_mix_post=dg2, norm_ffn_pre=dg3, norm_ffn_post=dg4, conv_short_w=dws, lru_conv_w=dwl,
                 lru_conv_b=dbl, lru_wa=dwa, lru_ba=dba, lru_wx=dwx, lru_bx=dbx, lru_lambda=dlam,
                 ffn_conv_w=jnp.concatenate([dfw[0], dfw[1]], axis=1), ffn_conv_b=jnp.concatenate([dfb[0], dfb[1]], axis=1))
    cs_late = chip_sums([dwlb, _split_small(grads)], ["row", "lead"], "late")
    (dh1, dwin), rx_all = dgrad_wgrad_cols([dconv, dlru, dgates], win4, h1, "proj_bwd", ride=exchange_ride(cs_mid + cs_late))
    rx_mid, rx_late = rx_all[:2], rx_all[2:]
    dx, grads["norm_mix_pre"] = norm_in_bwd(dh1, xs, dx2, g1)
    cs_in = chip_sums([dwin], ["col"], "in")
    rep_part = _pack_repl(grads, loss[0, 0])
    rx_in, rep_all = run_ride(exchange_ride(cs_in, rep=rep_part), "exchange_last")

    order = (("w_in", cs_in[0], rx_in), ("ffn_w_up", cs_up, rx_up[0]), ("w_conv_branch", cs_mid[1], rx_mid[1]),
             ("w_lru_branch", cs_late[0], rx_late[0]), ("w_out", cs_mid[0], rx_mid[0]),
             ("ffn_w_down", cs_down, rx_down[0]), ("small", cs_late[1], rx_late[1]))
    halves = [sum_chips(rx, cs, chip, "chip_sum_" + n) for n, cs, rx in order]
    me = 4 * xi + 2 * yi + ci
    rep_grad = sum_lead(_own_slot(rep_all, rep_part, me), "device_sum")
    others = pair_swap(halves)

    g_out, d_out, m_out, v_out = {}, {}, {}, {}
    for n, gm, go in zip(BIG, halves[:-1], others[:-1]):
        g, d, nm, nv = adamw_halves(w[n][0], gm, go, m[n][0], v[n][0], core, "adamw_" + n)
        g_out[n], d_out[n], m_out[n], v_out[n] = g[None], d[None], nm[None], nv[None]
    bufs = adamw_halves(small_shard, halves[-1], others[-1], _pack_small_shard(m), _pack_small_shard(v),
                        core, "adamw_small")
    for dst, buf in zip((g_out, d_out, m_out, v_out), bufs):
        dst.update(_unpack_small_shard(buf))
    d, nm, nv = adamw(_pack_repl(w), rep_grad, _pack_repl(m), _pack_repl(v), "adamw_repl")
    for dst, buf in ((g_out, rep_grad), (d_out, d), (m_out, nm), (v_out, nv)):
        dst.update(_unpack_repl(buf))

    return (rep_grad[LOSS_ROW, 0], dx[None], *[g_out[n] for n in WEIGHTS], *[d_out[n] for n in WEIGHTS],
            *[m_out[n] for n in WEIGHTS], *[v_out[n] for n in WEIGHTS])
```

```python
import functools
import math

import jax
import jax.numpy as jnp
from jax import lax
from jax.experimental import pallas as pl
from jax.experimental.pallas import tpu as pltpu

F32 = jnp.float32
BF16 = jnp.bfloat16

D_MODEL = 1024
N_CHIPS = 4
N_SEG = 7
D_FF = 3 * D_MODEL
LRU_HEADS = 4
HEAD_DIM = D_MODEL // LRU_HEADS
LRU_C = 8.0
RMS_EPS = 1e-6
CW = 256
FW = 256
SUBLANES = 8
VMEM_LIMIT = 58 * 1024 * 1024

ADAM_LR = 0.001
ADAM_B1 = 0.9
ADAM_B2 = 0.999
ADAM_EPS = 1e-08
ADAM_WD = 0.01
ADAM_STEP = 10

_GELU_C = math.sqrt(2.0 / math.pi)
_GELU_K = 0.044715


def _params(**kw):
    return pltpu.CompilerParams(vmem_limit_bytes=VMEM_LIMIT, **kw)


def _sigmoid(x):
    return 1.0 / (1.0 + jnp.exp(-x))


def _gelu(x):
    t = jnp.tanh(_GELU_C * (x + _GELU_K * x * x * x))
    return 0.5 * x * (1.0 + t)


def _gelu_and_grad(x):
    x2 = x * x
    t = jnp.tanh(_GELU_C * (x + _GELU_K * x * x2))
    g = 0.5 * x * (1.0 + t)
    dg = 0.5 * (1.0 + t) + 0.5 * x * (1.0 - t * t) * _GELU_C * (1.0 + 3.0 * _GELU_K * x2)
    return g, dg


def _log_sigmoid(x):
    e = jnp.exp(-jnp.abs(x))
    u = 1.0 + e
    l1p = jnp.where(u == 1.0, e, jnp.log(u) * e / (u - 1.0))
    return jnp.minimum(x, 0.0) - l1p


def _neg_expm1(z):
    series = -z * (1.0 + z * (0.5 + z * (1.0 / 6.0 + z * (1.0 / 24.0 + z * (1.0 / 120.0 + z * (1.0 / 720.0))))))
    return jnp.where(z > -0.2, series, 1.0 - jnp.exp(z))


def _rows(shape):
    return lax.broadcasted_iota(jnp.int32, shape, 0)


def _shift_down(x, k):
    return jnp.where(_rows(x.shape) >= k, pltpu.roll(x, k, 0), 0.0)


def _shift_up(x, k):
    n = x.shape[0]
    return jnp.where(_rows(x.shape) < n - k, pltpu.roll(x, n - k, 0), 0.0)


def _delays(x, k_width):
    return [x] + [_shift_down(x, j) for j in range(1, k_width)]


def _advances(dy, k_width):
    return [dy] + [_shift_up(dy, j) for j in range(1, k_width)]


def _taps_sum(shifted, w_ref, b=None):
    k_width = w_ref.shape[0]
    y = w_ref[k_width - 1:k_width, :] * shifted[0]
    for j in range(1, k_width):
        y = y + w_ref[k_width - 1 - j:k_width - j, :] * shifted[j]
    if b is not None:
        y = y + b
    return y


def _causal_conv(x, w_ref, b=None):
    return _taps_sum(_delays(x, w_ref.shape[0]), w_ref, b)


def _conv_wgrad(advanced, x):
    k_width = len(advanced)
    rows = [jnp.sum(advanced[k_width - 1 - k] * x, axis=0, keepdims=True) for k in range(k_width)]
    return jnp.concatenate(rows, axis=0)


def _dot(a, b):
    return jnp.dot(a, b, preferred_element_type=F32)


def _dot_nt(a, b):
    return lax.dot_general(a, b, (((1,), (1,)), ((), ())), preferred_element_type=F32)


def _dot_tn(a, b):
    return lax.dot_general(a, b, (((0,), (0,)), ((), ())), preferred_element_type=F32)


def _rms_stats(x):
    r = lax.rsqrt(jnp.mean(x * x, axis=-1, keepdims=True) + RMS_EPS)
    return x * r, r


def _rms_bwd(n, r, g, dy):
    dn = dy * g
    dx = r * (dn - n * jnp.mean(dn * n, axis=-1, keepdims=True))
    return dx, dy * n


def _scan_forward(a_ref, b_ref, h_ref):
    n, c = a_ref.shape
    row = lax.broadcasted_iota(jnp.int32, (SUBLANES, c), 0)

    def group(g, carry):
        r0 = pl.multiple_of(g * SUBLANES, SUBLANES)
        a = a_ref[pl.ds(r0, SUBLANES), :]
        b = b_ref[pl.ds(r0, SUBLANES), :]
        for k in (1, 2, 4):
            ap = jnp.where(row >= k, pltpu.roll(a, k, 0), 1.0)
            bp = jnp.where(row >= k, pltpu.roll(b, k, 0), 0.0)
            b = a * bp + b
            a = a * ap
        h = a * carry + b
        h_ref[pl.ds(r0, SUBLANES), :] = h
        return h[SUBLANES - 1:SUBLANES, :]

    lax.fori_loop(0, n // SUBLANES, group, jnp.zeros((1, c), F32))


def _scan_backward(c_ref, b_ref, g_ref):
    n, ch = c_ref.shape
    row = lax.broadcasted_iota(jnp.int32, (SUBLANES, ch), 0)
    n_groups = n // SUBLANES

    def group(i, carry):
        r0 = pl.multiple_of((n_groups - 1 - i) * SUBLANES, SUBLANES)
        a = c_ref[pl.ds(r0, SUBLANES), :]
        b = b_ref[pl.ds(r0, SUBLANES), :]
        for k in (1, 2, 4):
            keep = row < SUBLANES - k
            ap = jnp.where(keep, pltpu.roll(a, SUBLANES - k, 0), 1.0)
            bp = jnp.where(keep, pltpu.roll(b, SUBLANES - k, 0), 0.0)
            b = a * bp + b
            a = a * ap
        g = a * carry + b
        g_ref[pl.ds(r0, SUBLANES), :] = g
        return g[0:1, :]

    lax.fori_loop(0, n_groups, group, jnp.zeros((1, ch), F32))


MESH = pl.DeviceIdType.MESH
_HBM = pl.BlockSpec(memory_space=pltpu.HBM)
_OTHER_CHIPS = ((1, 0), (0, 1), (1, 1))
_OTHER_DEVICES = tuple((dx, dy, dc) for dx in (0, 1) for dy in (0, 1) for dc in (0, 1) if dx or dy or dc)
N_DEVICES = 8


def _position():
    return lax.axis_index("x"), lax.axis_index("y"), lax.axis_index("c")


def _flip(v, d):
    return 1 - v if d else v


def _chip(x, y, p):
    px, py = _flip(x, _OTHER_CHIPS[p][0]), _flip(y, _OTHER_CHIPS[p][1])
    return px, py, 2 * px + py


class _Ride:
    def __init__(self, srcs, bufs, scratch, plan):
        self.srcs, self.bufs, self.scratch, self.plan = list(srcs), list(bufs), list(scratch), plan


def _call(body, *, name, grid, in_specs, out_specs, out_shape, operands, scratch_shapes=(), ride=None):
    in_specs, out_specs, out_shape = list(in_specs), list(out_specs), list(out_shape)
    scratch_shapes = list(scratch_shapes)
    if ride is None:
        return pl.pallas_call(body, name=name, grid=grid, in_specs=in_specs, out_specs=out_specs, out_shape=out_shape,
                              scratch_shapes=scratch_shapes, compiler_params=_params())(*operands)
    n_in, n_out, n_scr = len(in_specs), len(out_shape), len(scratch_shapes)
    old = [i for i, b in enumerate(ride.bufs) if not isinstance(b, jax.ShapeDtypeStruct)]
    n_src, n_old, n_buf = len(ride.srcs), len(old), len(ride.bufs)

    def full_body(*refs):
        o0 = n_in + n_src + n_old
        s0 = o0 + n_out + n_buf
        start, relay, relay_on, finish = ride.plan(refs[n_in:n_in + n_src], refs[o0 + n_out:s0], refs[s0 + n_scr:])
        ids = [pl.program_id(i) for i in range(len(grid))]
        first = functools.reduce(jnp.logical_and, [i == 0 for i in ids])
        middle = functools.reduce(jnp.logical_and, [ids[0] == grid[0] // 2] + [i == 0 for i in ids[1:]])
        last = functools.reduce(jnp.logical_and, [i == g - 1 for i, g in zip(ids, grid)])
        pl.when(first)(start)
        pl.when(middle)(relay)
        pl.when(last)(relay_on)
        body(*refs[:n_in], *refs[o0:o0 + n_out], *refs[s0:s0 + n_scr])
        pl.when(last)(finish)

    shapes = [jax.ShapeDtypeStruct(b.shape, b.dtype) for b in ride.bufs]
    res = pl.pallas_call(
        full_body, name=name, grid=grid,
        in_specs=in_specs + [_HBM] * (n_src + n_old), out_specs=out_specs + [_HBM] * n_buf,
        out_shape=out_shape + shapes, scratch_shapes=scratch_shapes + ride.scratch,
        input_output_aliases={n_in + n_src + k: n_out + i for k, i in enumerate(old)},
        compiler_params=_params(),
    )(*operands, *ride.srcs, *[ride.bufs[i] for i in old])
    return list(res[:n_out]), list(res[n_out:])


def run_ride(ride, name):
    def body():
        pass

    return _call(body, name=name, grid=(1,), in_specs=[], out_specs=[], out_shape=[], operands=[], ride=ride)[1]


def gather_ride(shards, items=None, into=None):
    items = items or [(a, 0, s.shape[0]) for a, s in enumerate(shards)]
    bufs = into or [jax.ShapeDtypeStruct((N_CHIPS,) + s.shape, s.dtype) for s in shards]
    nrel = len(_OTHER_CHIPS)

    def plan(srcs, dsts, sems):
        ici_send, ici_recv, hop_send, hop_recv, sib_send, sib_recv = sems
        x, y, c = _position()
        j = 2 * x + y

        def rows(ref, it, h, q=None):
            half = it[2] // 2
            if q is None:
                return ref.at[pl.ds(it[1] + h * half, half), :]
            return ref.at[pl.ds(it[1] + h * half + q * (half // 2), half // 2), :]

        def ici(i, p, slot):
            it = items[i]
            px, py, _ = _chip(x, y, p)
            return pltpu.make_async_remote_copy(
                src_ref=rows(srcs[it[0]], it, c), dst_ref=rows(dsts[it[0]].at[slot], it, c),
                send_sem=ici_send.at[i * nrel + p], recv_sem=ici_recv.at[i * nrel + p],
                device_id=(px, py, c), device_id_type=MESH)

        def hop(i, p, slot):
            it = items[i]
            part = rows(dsts[it[0]].at[slot], it, c, p)
            px, py, _ = _chip(x, y, 1 - p)
            return pltpu.make_async_remote_copy(
                src_ref=part, dst_ref=part, send_sem=hop_send.at[i * 2 + p], recv_sem=hop_recv.at[i * 2 + p],
                device_id=(px, py, c), device_id_type=MESH)

        def sib(i, p, h):
            it = items[i]
            part = rows(dsts[it[0]].at[_chip(x, y, p)[2]], it, h)
            return pltpu.make_async_remote_copy(
                src_ref=part, dst_ref=part, send_sem=sib_send.at[i * nrel + p], recv_sem=sib_recv.at[i * nrel + p],
                device_id=(x, y, 1 - c), device_id_type=MESH)

        every = range(len(items))
        diag = _chip(x, y, 2)[2]

        def start():
            for i in every:
                for p in (0, 1):
                    ici(i, p, j).start()

        def relay():
            for i in every:
                for p in (0, 1):
                    k = _chip(x, y, p)[2]
                    ici(i, p, k).wait_recv()
                    hop(i, p, k).start()
                    sib(i, p, c).start()

        def relay_on():
            for i in every:
                for p in (0, 1):
                    hop(i, p, diag).wait_recv()
                sib(i, 2, c).start()

        def finish():
            for i in every:
                for p in range(nrel):
                    sib(i, p, 1 - c).wait_recv()
            for i in every:
                for p in (0, 1):
                    ici(i, p, j).wait_send()
                    hop(i, p, _chip(x, y, p)[2]).wait_send()
                for p in range(nrel):
                    sib(i, p, c).wait_send()

        return start, relay, relay_on, finish

    n = len(items)
    sems = [pltpu.SemaphoreType.DMA((n * nrel,))] * 2 + [pltpu.SemaphoreType.DMA((n * 2,))] * 2 \
        + [pltpu.SemaphoreType.DMA((n * nrel,))] * 2
    return _Ride(shards, bufs, sems, plan)


def exchange_ride(sums, items=None, into=None, rep=None):
    items = items or [(a, 0, s.shape[1]) for a, s in enumerate(sums)]
    into = into or [None] * len(sums)
    bufs = [jax.ShapeDtypeStruct(s.shape, s.dtype) if b is None else b for s, b in zip(sums, into)]
    srcs = list(sums)
    scratch = [pltpu.SemaphoreType.DMA((len(items) * len(_OTHER_CHIPS),))] * 2
    if rep is not None:
        srcs.append(rep)
        bufs.append(jax.ShapeDtypeStruct((N_DEVICES,) + rep.shape, rep.dtype))
        scratch += [pltpu.SemaphoreType.DMA((len(_OTHER_DEVICES),))] * 2
    nrel = len(_OTHER_CHIPS)

    def plan(src_refs, dst_refs, sems):
        x, y, c = _position()
        j = 2 * x + y
        me = 4 * x + 2 * y + c

        def part(i, p, src_slot, dst_slot):
            a, r0, nr = items[i]
            px, py, _ = _chip(x, y, p)
            return pltpu.make_async_remote_copy(
                src_ref=src_refs[a].at[src_slot, pl.ds(r0, nr), :], dst_ref=dst_refs[a].at[dst_slot, pl.ds(r0, nr), :],
                send_sem=sems[0].at[i * nrel + p], recv_sem=sems[1].at[i * nrel + p],
                device_id=(px, py, c), device_id_type=MESH)

        def device(q):
            dx, dy, dc = _OTHER_DEVICES[q]
            return _flip(x, dx), _flip(y, dy), _flip(c, dc)

        def rep_copy(q, slot):
            return pltpu.make_async_remote_copy(
                src_ref=src_refs[-1], dst_ref=dst_refs[-1].at[slot], send_sem=sems[2].at[q], recv_sem=sems[3].at[q],
                device_id=device(q), device_id_type=MESH)

        pairs = [(i, p) for i in range(len(items)) for p in range(nrel)]
        others = range(len(_OTHER_DEVICES)) if rep is not None else ()

        def start():
            for i, p in pairs:
                part(i, p, _chip(x, y, p)[2], j).start()
            for q in others:
                rep_copy(q, me).start()

        def finish():
            for i, p in pairs:
                k = _chip(x, y, p)[2]
                part(i, p, k, k).wait_recv()
            for q in others:
                px, py, pc = device(q)
                rep_copy(q, 4 * px + 2 * py + pc).wait_recv()
            for i, p in pairs:
                part(i, p, _chip(x, y, p)[2], j).wait_send()
            for q in others:
                rep_copy(q, me).wait_send()

        return start, lambda: None, lambda: None, finish

    return _Ride(srcs, bufs, scratch, plan)


def _own_slot(buf, own, index):
    return lax.dynamic_update_slice(buf, own[None], (index,) + (0,) * own.ndim)


def _token_tile(s):
    return min(s, 512)


def norm_in(x, g):
    s, d = x.shape
    t = _token_tile(s)

    def body(x_ref, g_ref, o_ref):
        n, _ = _rms_stats(x_ref[...])
        o_ref[...] = (n * g_ref[...]).astype(BF16)

    return pl.pallas_call(
        body, name="norm_in", grid=(s // t,),
        in_specs=[pl.BlockSpec((t, d), lambda i: (i, 0)), pl.BlockSpec((1, d), lambda i: (0, 0))],
        out_specs=pl.BlockSpec((t, d), lambda i: (i, 0)),
        out_shape=jax.ShapeDtypeStruct((s, d), BF16),
        compiler_params=_params(),
    )(x, g)


def matmul_cols(a, w4, name, ride=None):
    m, k = a.shape
    nj, _, ns = w4.shape
    nb = ns // CW

    def body(a_ref, w_ref, o_ref):
        o_ref[...] = _dot(a_ref[...], w_ref[0])

    return _call(
        body, name=name, grid=(nj, nb),
        in_specs=[pl.BlockSpec((m, k), lambda j, b: (0, 0)),
                  pl.BlockSpec((1, k, CW), lambda j, b: (j, 0, b))],
        out_specs=[pl.BlockSpec((m, CW), lambda j, b: (0, j * nb + b))],
        out_shape=[jax.ShapeDtypeStruct((m, nj * ns), F32)],
        operands=(a, w4), ride=ride)


def mix_conv_fwd(proj, ws, ride=None):
    s = proj.shape[0]
    nblk = D_MODEL // CW

    def body(cb_ref, cc_ref, cx_ref, ws_ref, q_ref, ya_ref):
        q = _causal_conv(cc_ref[...] * cx_ref[...], ws_ref)
        q_ref[...] = q
        ya_ref[...] = (cb_ref[...] * q).astype(BF16)

    seg = lambda k: pl.BlockSpec((s, CW), lambda c, k=k: (0, k * nblk + c))
    return _call(
        body, name="mix_conv_fwd", grid=(nblk,),
        in_specs=[seg(0), seg(1), seg(2), pl.BlockSpec((3, CW), lambda c: (0, c))],
        out_specs=[pl.BlockSpec((s, CW), lambda c: (0, c))] * 2,
        out_shape=[jax.ShapeDtypeStruct((s, D_MODEL), F32), jax.ShapeDtypeStruct((s, D_MODEL), BF16)],
        operands=(proj, proj, proj, ws), ride=ride)


def _lru_gates(r, ls):
    log_a = LRU_C * r * ls
    a = jnp.exp(log_a)
    mult = jnp.sqrt(_neg_expm1(2.0 * log_a))
    mult = jnp.where(_rows(r.shape) == 0, 1.0, mult)
    return a, mult


def mix_lru_fwd(proj, wl, bl, wa, ba, wx, bx, lam, ride=None):
    s = proj.shape[0]
    nblk = D_MODEL // CW

    def body(lx_ref, ly_ref, wl_ref, bl_ref, wa_ref, ba_ref, wx_ref, bx_ref, lam_ref,
             xl_ref, r_ref, i_ref, h_ref, yb_ref, a_scr, u_scr):
        xl = _causal_conv(lx_ref[...], wl_ref, bl_ref[...])
        xlb = xl.astype(BF16)
        xl_ref[...] = xlb
        r = _sigmoid(_dot(xlb, wa_ref[0]) + ba_ref[...])
        i = _sigmoid(_dot(xlb, wx_ref[0]) + bx_ref[...])
        r_ref[...] = r.astype(BF16)
        i_ref[...] = i.astype(BF16)
        a, mult = _lru_gates(r, _log_sigmoid(lam_ref[...]))
        a_scr[...] = a
        u_scr[...] = mult * i * xl
        _scan_forward(a_scr, u_scr, h_ref)
        yb_ref[...] = (h_ref[...] * _gelu(ly_ref[...])).astype(BF16)

    blk = lambda k: pl.BlockSpec((s, CW), lambda c, k=k: (0, k * nblk + c))
    vec = pl.BlockSpec((1, CW), lambda c: (0, c))
    mat = pl.BlockSpec((1, CW, CW), lambda c: (c, 0, 0))
    out = pl.BlockSpec((s, CW), lambda c: (0, c))
    f = jax.ShapeDtypeStruct((s, D_MODEL), F32)
    hb = jax.ShapeDtypeStruct((s, D_MODEL), BF16)
    return _call(
        body, name="mix_lru_fwd", grid=(nblk,),
        in_specs=[blk(3), blk(4), pl.BlockSpec((4, CW), lambda c: (0, c)), vec, mat, vec, mat, vec, vec],
        out_specs=[out] * 5,
        out_shape=[hb, hb, hb, f, hb],
        scratch_shapes=[pltpu.VMEM((s, CW), F32), pltpu.VMEM((s, CW), F32)],
        operands=(proj, proj, wl, bl, wa, ba, wx, bx, lam), ride=ride)


def branch_merge_fwd(ya, yb, wcb, wlb, proj, ride=None):
    s = ya.shape[0]
    nblk = D_MODEL // CW

    def body(ya_ref, yb_ref, wcb_ref, wlb_ref, gc_ref, gl_ref, a_ref, b_ref, m_ref):
        a = _dot(ya_ref[...], wcb_ref[...])
        b = _dot(yb_ref[...], wlb_ref[...])
        a_ref[...] = a
        b_ref[...] = b
        m_ref[...] = (_sigmoid(gc_ref[...]) * a + _sigmoid(gl_ref[...]) * b).astype(BF16)

    res = pl.BlockSpec((s, D_MODEL), lambda n: (0, 0))
    wcol = pl.BlockSpec((D_MODEL, CW), lambda n: (0, n))
    blk = lambda k: pl.BlockSpec((s, CW), lambda n, k=k: (0, k * nblk + n))
    out = pl.BlockSpec((s, CW), lambda n: (0, n))
    f = jax.ShapeDtypeStruct((s, D_MODEL), F32)
    return _call(
        body, name="branch_merge_fwd", grid=(nblk,),
        in_specs=[res, res, wcol, wcol, blk(5), blk(6)],
        out_specs=[out] * 3,
        out_shape=[f, f, jax.ShapeDtypeStruct((s, D_MODEL), BF16)],
        operands=(ya, yb, wcb, wlb, proj, proj), ride=ride)


def mix_out_fwd(merged, wout, x, g2, g3, ride=None):
    s, d = x.shape
    t = _token_tile(s)

    def body(m_ref, w_ref, x_ref, g2_ref, g3_ref, mix_ref, x2_ref, h2_ref):
        mix = _dot(m_ref[...], w_ref[...])
        mix_ref[...] = mix
        n, _ = _rms_stats(mix)
        x2 = x_ref[...] + n * g2_ref[...]
        x2_ref[...] = x2
        n2, _ = _rms_stats(x2)
        h2_ref[...] = (n2 * g3_ref[...]).astype(BF16)

    tile = pl.BlockSpec((t, d), lambda i: (i, 0))
    vec = pl.BlockSpec((1, d), lambda i: (0, 0))
    f = jax.ShapeDtypeStruct((s, d), F32)
    return _call(
        body, name="mix_out_fwd", grid=(s // t,),
        in_specs=[tile, pl.BlockSpec((d, d), lambda i: (0, 0)), tile, vec, vec],
        out_specs=[tile] * 3,
        out_shape=[f, f, jax.ShapeDtypeStruct((s, d), BF16)],
        operands=(merged, wout, x, g2, g3), ride=ride)


def ffn_up_act_fwd(h2, wup4, fw, fb, ride=None):
    s, k = h2.shape
    ns = wup4.shape[2]
    per_chip = ns // CW
    nblk = D_FF // CW

    def body(h_ref, wg_ref, wv_ref, cg_ref, cv_ref, bg_ref, bv_ref, up_ref, act_ref, f_ref):
        h = h_ref[...]
        ug = _dot(h, wg_ref[0])
        uv = _dot(h, wv_ref[0])
        up_ref[0] = ug
        up_ref[1] = uv
        gate = _causal_conv(ug, cg_ref, bg_ref[...])
        val = _causal_conv(uv, cv_ref, bv_ref[...])
        act_ref[0] = gate.astype(BF16)
        act_ref[1] = val.astype(BF16)
        f_ref[...] = (_gelu(gate) * val).astype(BF16)

    wcols = lambda h: pl.BlockSpec((1, k, CW), lambda n, h=h: (n // per_chip + 2 * h, 0, n % per_chip))
    half = lambda h, rows: pl.BlockSpec((rows, CW), lambda n, h=h: (0, h * nblk + n))
    both = pl.BlockSpec((2, s, CW), lambda n: (0, 0, n))
    return _call(
        body, name="ffn_up_act_fwd", grid=(nblk,),
        in_specs=[pl.BlockSpec((s, k), lambda n: (0, 0)), wcols(0), wcols(1),
                  half(0, 3), half(1, 3), half(0, 1), half(1, 1)],
        out_specs=[both, both, pl.BlockSpec((s, CW), lambda n: (0, n))],
        out_shape=[jax.ShapeDtypeStruct((2, s, D_FF), F32), jax.ShapeDtypeStruct((2, s, D_FF), BF16),
                   jax.ShapeDtypeStruct((s, D_FF), BF16)],
        operands=(h2, wup4, wup4, fw, fw, fb, fb), ride=ride)


def ffn_down_loss(f, wdown, x2, target, g4):
    s, d = x2.shape
    t = _token_tile(s)

    def body(f_ref, w_ref, x2_ref, tg_ref, g4_ref, dy_ref, dout_ref, loss_ref, dg4_ref):
        @pl.when(pl.program_id(0) == 0)
        def _():
            loss_ref[...] = jnp.zeros_like(loss_ref)
            dg4_ref[...] = jnp.zeros_like(dg4_ref)

        out = _dot(f_ref[...], w_ref[...])
        n, r = _rms_stats(out)
        err = x2_ref[...] + n * g4_ref[...] - tg_ref[...]
        loss_ref[...] += jnp.full(loss_ref.shape, (0.5 / d) * jnp.sum(err * err), F32)
        dy = err * (1.0 / d)
        dy_ref[...] = dy
        dout, dg = _rms_bwd(n, r, g4_ref[...], dy)
        dout_ref[...] = dout.astype(BF16)
        dg4_ref[...] += jnp.sum(dg, axis=0, keepdims=True)

    tile = pl.BlockSpec((t, d), lambda i: (i, 0))
    vec = pl.BlockSpec((1, d), lambda i: (0, 0))
    return pl.pallas_call(
        body, name="ffn_down_loss", grid=(s // t,),
        in_specs=[pl.BlockSpec((t, D_FF), lambda i: (i, 0)), pl.BlockSpec((D_FF, d), lambda i: (0, 0)), tile, tile, vec],
        out_specs=[tile, tile, pl.BlockSpec((1, 128), lambda i: (0, 0)), vec],
        out_shape=[jax.ShapeDtypeStruct((s, d), F32), jax.ShapeDtypeStruct((s, d), BF16),
                   jax.ShapeDtypeStruct((1, 128), F32), jax.ShapeDtypeStruct((1, d), F32)],
        compiler_params=_params(),
    )(f, wdown, x2, target, g4)


def ffn_bwd(dout, wdown, up, act, f, fw):
    s = up.shape[1]
    nblk = D_FF // FW

    def body(do_ref, wd_ref, up_ref, act_ref, f_ref, wg_ref, wv_ref, dup_ref, dwd_ref, dw_ref, db_ref):
        do = do_ref[...]
        df = _dot_nt(do, wd_ref[...])
        dwd_ref[...] = _dot_tn(f_ref[...], do).astype(BF16)
        val = act_ref[1].astype(F32)
        ge, dge = _gelu_and_grad(act_ref[0].astype(F32))
        dgate = _advances(df * val * dge, 3)
        dval = _advances(df * ge, 3)
        dup_ref[0] = _taps_sum(dgate, wg_ref).astype(BF16)
        dup_ref[1] = _taps_sum(dval, wv_ref).astype(BF16)
        dw_ref[0] = _conv_wgrad(dgate, up_ref[0])
        dw_ref[1] = _conv_wgrad(dval, up_ref[1])
        db_ref[0] = jnp.sum(dgate[0], axis=0, keepdims=True)
        db_ref[1] = jnp.sum(dval[0], axis=0, keepdims=True)

    half = lambda h, rows: pl.BlockSpec((rows, FW), lambda n, h=h: (0, h * nblk + n))
    both = lambda rows: pl.BlockSpec((2, rows, FW), lambda n: (0, 0, n))
    return pl.pallas_call(
        body, name="ffn_bwd", grid=(nblk,),
        in_specs=[pl.BlockSpec((s, D_MODEL), lambda n: (0, 0)), pl.BlockSpec((FW, D_MODEL), lambda n: (n, 0)),
                  both(s), both(s), pl.BlockSpec((s, FW), lambda n: (0, n)), half(0, 3), half(1, 3)],
        out_specs=[both(s), pl.BlockSpec((FW, D_MODEL), lambda n: (n, 0)), both(3), both(1)],
        out_shape=[jax.ShapeDtypeStruct((2, s, D_FF), BF16), jax.ShapeDtypeStruct((D_FF, D_MODEL), BF16),
                   jax.ShapeDtypeStruct((2, 3, D_FF), F32), jax.ShapeDtypeStruct((2, 1, D_FF), F32)],
        compiler_params=_params(),
    )(dout, wdown, up, act, f, fw, fw)


def ffn_up_bwd(dout, wdown, up, act, f, fw, wup4, h2, ride=None):
    s, k = h2.shape
    nblk = D_FF // FW
    per_chip = wup4.shape[2] // FW

    def body(do_ref, wd_ref, up_ref, act_ref, f_ref, cg_ref, cv_ref, wg_ref, wv_ref, h_ref,
             dh_ref, dwu_ref, dwd_ref, dw_ref, db_ref, dup_scr):
        @pl.when(pl.program_id(0) == 0)
        def _():
            dup_scr[...] = jnp.zeros_like(dup_scr)
            dh_ref[...] = jnp.zeros_like(dh_ref)

        do = do_ref[...]
        df = _dot_nt(do, wd_ref[...])
        dg = dup_scr[0]
        dv = dup_scr[1]
        h = h_ref[...]
        dh_ref[...] += _dot_nt(dg, wg_ref[0]) + _dot_nt(dv, wv_ref[0])
        dwu_ref[0] = _dot_tn(h, dg).astype(BF16)
        dwu_ref[1] = _dot_tn(h, dv).astype(BF16)
        dwd_ref[...] = _dot_tn(f_ref[...], do).astype(BF16)
        val = act_ref[1].astype(F32)
        ge, dge = _gelu_and_grad(act_ref[0].astype(F32))
        dgate = _advances(df * val * dge, 3)
        dval = _advances(df * ge, 3)
        dw_ref[0] = _conv_wgrad(dgate, up_ref[0])
        dw_ref[1] = _conv_wgrad(dval, up_ref[1])
        db_ref[0] = jnp.sum(dgate[0], axis=0, keepdims=True)
        db_ref[1] = jnp.sum(dval[0], axis=0, keepdims=True)
        dup_scr[0] = _taps_sum(dgate, cg_ref).astype(BF16)
        dup_scr[1] = _taps_sum(dval, cv_ref).astype(BF16)

    cur = lambda n: jnp.minimum(n, nblk - 1)
    prev = lambda n: jnp.maximum(n - 1, 0)
    once = pl.Buffered(1)
    both = lambda rows: pl.BlockSpec((2, rows, FW), lambda n: (0, 0, cur(n)))
    taps = lambda h: pl.BlockSpec((3, FW), lambda n, h=h: (0, h * nblk + cur(n)))
    wcols = lambda h: pl.BlockSpec((1, k, FW), lambda n, h=h: (prev(n) // per_chip + 2 * h, 0, prev(n) % per_chip))
    return _call(
        body, name="ffn_up_bwd", grid=(nblk + 1,),
        in_specs=[pl.BlockSpec((s, D_MODEL), lambda n: (0, 0), pipeline_mode=once),
                  pl.BlockSpec((FW, D_MODEL), lambda n: (cur(n), 0)), both(s), both(s),
                  pl.BlockSpec((s, FW), lambda n: (0, cur(n))), taps(0), taps(1), wcols(0), wcols(1),
                  pl.BlockSpec((s, k), lambda n: (0, 0), pipeline_mode=once)],
        out_specs=[pl.BlockSpec((s, k), lambda n: (0, 0), pipeline_mode=once),
                   pl.BlockSpec((2, k, FW), lambda n: (0, 0, prev(n))),
                   pl.BlockSpec((FW, D_MODEL), lambda n: (cur(n), 0)), both(3), both(1)],
        out_shape=[jax.ShapeDtypeStruct((s, k), F32), jax.ShapeDtypeStruct((2, k, D_FF), BF16),
                   jax.ShapeDtypeStruct((D_FF, D_MODEL), BF16),
                   jax.ShapeDtypeStruct((2, 3, D_FF), F32), jax.ShapeDtypeStruct((2, 1, D_FF), F32)],
        scratch_shapes=[pltpu.VMEM((2, s, FW), BF16)],
        operands=(dout, wdown, up, act, f, fw, fw, wup4, wup4, h2), ride=ride)


def dgrad_wgrad_cols(dy, w4, a, name, ride=None):
    m, k = a.shape
    nj, _, ns = w4.shape
    nb = ns // CW
    per_seg = dy[0].shape[2] // CW
    first = [sum(d.shape[0] for d in dy[:i]) for i in range(len(dy))]

    def segment(j, b):
        return (j * nb + b) // per_seg, (j * nb + b) % per_seg

    def body(*refs):
        dy_refs, (w_ref, a_ref, da_ref, dw_ref) = refs[:len(dy)], refs[len(dy):]

        @pl.when((pl.program_id(0) == 0) & (pl.program_id(1) == 0))
        def _():
            da_ref[...] = jnp.zeros_like(da_ref)

        seg, _ = segment(pl.program_id(0), pl.program_id(1))
        dyb = dy_refs[-1][0]
        for i in range(len(dy) - 2, -1, -1):
            dyb = jnp.where(seg < first[i + 1], dy_refs[i][0], dyb)
        da_ref[...] += _dot_nt(dyb, w_ref[0])
        dw_ref[...] = _dot_tn(a_ref[...], dyb).astype(BF16)

    def dy_spec(i):
        nseg = dy[i].shape[0]

        def index(j, b):
            seg, col = segment(j, b)
            local = seg - first[i]
            return (jnp.clip(local, 0, nseg - 1), 0,
                    jnp.where(local < 0, 0, jnp.where(local >= nseg, per_seg - 1, col)))

        return pl.BlockSpec((1, m, CW), index)

    return _call(
        body, name=name, grid=(nj, nb),
        in_specs=[dy_spec(i) for i in range(len(dy))]
        + [pl.BlockSpec((1, k, CW), lambda j, b: (j, 0, b)), pl.BlockSpec((m, k), lambda j, b: (0, 0))],
        out_specs=[pl.BlockSpec((m, k), lambda j, b: (0, 0)),
                   pl.BlockSpec((k, CW), lambda j, b: (0, j * nb + b))],
        out_shape=[jax.ShapeDtypeStruct((m, k), F32), jax.ShapeDtypeStruct((k, nj * ns), BF16)],
        operands=(*dy, w4, a), ride=ride)


def norms_mid_bwd(dh2, x2, dy, mix, g3, g2, ride=None):
    s, d = x2.shape
    t = _token_tile(s)

    def body(dh2_ref, x2_ref, dy_ref, mix_ref, g3_ref, g2_ref, dx2_ref, dmix_ref, dg3_ref, dg2_ref):
        @pl.when(pl.program_id(0) == 0)
        def _():
            dg3_ref[...] = jnp.zeros_like(dg3_ref)
            dg2_ref[...] = jnp.zeros_like(dg2_ref)

        n3, r3 = _rms_stats(x2_ref[...])
        dx, dg3 = _rms_bwd(n3, r3, g3_ref[...], dh2_ref[...])
        dx2 = dy_ref[...] + dx
        dx2_ref[...] = dx2
        dg3_ref[...] += jnp.sum(dg3, axis=0, keepdims=True)
        n2, r2 = _rms_stats(mix_ref[...])
        dmix, dg2 = _rms_bwd(n2, r2, g2_ref[...], dx2)
        dmix_ref[...] = dmix.astype(BF16)
        dg2_ref[...] += jnp.sum(dg2, axis=0, keepdims=True)

    tile = pl.BlockSpec((t, d), lambda i: (i, 0))
    vec = pl.BlockSpec((1, d), lambda i: (0, 0))
    v = jax.ShapeDtypeStruct((1, d), F32)
    return _call(
        body, name="norms_mid_bwd", grid=(s // t,),
        in_specs=[tile, tile, tile, tile, vec, vec],
        out_specs=[tile, tile, vec, vec],
        out_shape=[jax.ShapeDtypeStruct((s, d), F32), jax.ShapeDtypeStruct((s, d), BF16), v, v],
        operands=(dh2, x2, dy, mix, g3, g2), ride=ride)


def mix_out_bwd(dmix, wout, merged, a, b, proj, ride=None):
    s = dmix.shape[0]
    nblk = D_MODEL // CW

    def body(dm_ref, w_ref, mg_ref, a_ref, b_ref, gc_ref, gl_ref, da_ref, db_ref, dw_ref, dg_ref):
        dm = dm_ref[...]
        dmerged = _dot_nt(dm, w_ref[...])
        dw_ref[...] = _dot_tn(mg_ref[...], dm).astype(BF16)
        sc = _sigmoid(gc_ref[...])
        sl = _sigmoid(gl_ref[...])
        da_ref[...] = (dmerged * sc).astype(BF16)
        db_ref[...] = (dmerged * sl).astype(BF16)
        dg_ref[0] = (dmerged * a_ref[...] * sc * (1.0 - sc)).astype(BF16)
        dg_ref[1] = (dmerged * b_ref[...] * sl * (1.0 - sl)).astype(BF16)

    res = pl.BlockSpec((s, D_MODEL), lambda n: (0, 0))
    rows = pl.BlockSpec((CW, D_MODEL), lambda n: (n, 0))
    col = pl.BlockSpec((s, CW), lambda n: (0, n))
    blk = lambda k: pl.BlockSpec((s, CW), lambda n, k=k: (0, k * nblk + n))
    hb = jax.ShapeDtypeStruct((s, D_MODEL), BF16)
    return _call(
        body, name="mix_out_bwd", grid=(nblk,),
        in_specs=[res, rows, col, col, col, blk(5), blk(6)],
        out_specs=[col, col, rows, pl.BlockSpec((2, s, CW), lambda n: (0, 0, n))],
        out_shape=[hb, hb, jax.ShapeDtypeStruct((D_MODEL, D_MODEL), BF16), jax.ShapeDtypeStruct((2, s, D_MODEL), BF16)],
        operands=(dmix, wout, merged, a, b, proj, proj), ride=ride)


def mix_conv_bwd(da, wcb, proj, q, ws, ride=None):
    s = da.shape[0]
    nblk = D_MODEL // CW

    def body(da_ref, w_ref, cb_ref, cc_ref, cx_ref, q_ref, ws_ref, dc_ref, dw_ref, dws_ref):
        dab = da_ref[...]
        dya = _dot_nt(dab, w_ref[...])
        cb = cb_ref[...]
        cc = cc_ref[...]
        cx = cx_ref[...]
        q = q_ref[...]
        dw_ref[...] = _dot_tn((cb * q).astype(BF16), dab).astype(BF16)
        dc_ref[0] = (dya * q).astype(BF16)
        dq = _advances(dya * cb, 3)
        dp = _taps_sum(dq, ws_ref)
        dws_ref[...] = _conv_wgrad(dq, cc * cx)
        dc_ref[1] = (dp * cx).astype(BF16)
        dc_ref[2] = (dp * cc).astype(BF16)

    res = pl.BlockSpec((s, D_MODEL), lambda n: (0, 0))
    rows = pl.BlockSpec((CW, D_MODEL), lambda n: (n, 0))
    col = pl.BlockSpec((s, CW), lambda n: (0, n))
    blk = lambda k: pl.BlockSpec((s, CW), lambda n, k=k: (0, k * nblk + n))
    taps = pl.BlockSpec((3, CW), lambda n: (0, n))
    hb = jax.ShapeDtypeStruct((s, D_MODEL), BF16)
    return _call(
        body, name="mix_conv_bwd", grid=(nblk,),
        in_specs=[res, rows, blk(0), blk(1), blk(2), col, taps],
        out_specs=[pl.BlockSpec((3, s, CW), lambda n: (0, 0, n)), rows, taps],
        out_shape=[jax.ShapeDtypeStruct((3, s, D_MODEL), BF16), jax.ShapeDtypeStruct((D_MODEL, D_MODEL), BF16),
                   jax.ShapeDtypeStruct((3, D_MODEL), F32)],
        operands=(da, wcb, proj, proj, proj, q, ws), ride=ride)


def mix_lru_bwd(db, wlb, proj, xl, r, i, h, wl, wa, wx, lam, ride=None):
    s = db.shape[0]
    nblk = D_MODEL // CW

    def body(db_ref, w_ref, lx_ref, ly_ref, xl_ref, r_ref, i_ref, h_ref, wl_ref, wa_ref, wx_ref, lam_ref,
             dl_ref, dw_ref, dwa_ref, dwx_ref, dba_ref, dbx_ref, dwl_ref, dbl_ref, dlam_ref,
             c_scr, g_scr):
        dbb = db_ref[...]
        dyb = _dot_nt(dbb, w_ref[...])
        h = h_ref[...]
        ge, dge = _gelu_and_grad(ly_ref[...])
        dw_ref[...] = _dot_tn((h * ge).astype(BF16), dbb).astype(BF16)
        dl_ref[1] = (dyb * h * dge).astype(BF16)
        r = r_ref[...].astype(F32)
        gi = i_ref[...].astype(F32)
        xlb = xl_ref[...]
        xl = xlb.astype(F32)
        lam = lam_ref[...]
        ls = _log_sigmoid(lam)
        a, mult = _lru_gates(r, ls)
        c_scr[...] = _shift_up(a, 1)
        g_scr[...] = dyb * ge
        _scan_backward(c_scr, g_scr, g_scr)
        du = g_scr[...]
        da = du * _shift_down(h, 1)
        dmult = du * gi * xl
        di = du * mult * xl
        dxl = du * mult * gi
        first = _rows(a.shape) == 0
        dlog_a = da * a - jnp.where(first, 0.0, dmult * a * a / mult)
        dr = dlog_a * (LRU_C * ls)
        dlam_ref[...] = jnp.sum(dlog_a * r, axis=0, keepdims=True) * (LRU_C * (1.0 - _sigmoid(lam)))
        dzr = dr * r * (1.0 - r)
        dzi = di * gi * (1.0 - gi)
        dba_ref[...] = jnp.sum(dzr, axis=0, keepdims=True)
        dbx_ref[...] = jnp.sum(dzi, axis=0, keepdims=True)
        dzrb = dzr.astype(BF16)
        dzib = dzi.astype(BF16)
        dwa_ref[0] = _dot_tn(xlb, dzrb)
        dwx_ref[0] = _dot_tn(xlb, dzib)
        dxl = _advances(dxl + _dot_nt(dzrb, wa_ref[0]) + _dot_nt(dzib, wx_ref[0]), 4)
        dl_ref[0] = _taps_sum(dxl, wl_ref).astype(BF16)
        dwl_ref[...] = _conv_wgrad(dxl, lx_ref[...])
        dbl_ref[...] = jnp.sum(dxl[0], axis=0, keepdims=True)

    res = pl.BlockSpec((s, D_MODEL), lambda n: (0, 0), pipeline_mode=pl.Buffered(1))
    rows = pl.BlockSpec((CW, D_MODEL), lambda n: (n, 0))
    col = pl.BlockSpec((s, CW), lambda n: (0, n))
    blk = lambda k: pl.BlockSpec((s, CW), lambda n, k=k: (0, k * nblk + n))
    taps = pl.BlockSpec((4, CW), lambda n: (0, n))
    vec = pl.BlockSpec((1, CW), lambda n: (0, n))
    mat = pl.BlockSpec((1, CW, CW), lambda n: (n, 0, 0))
    hb = jax.ShapeDtypeStruct((s, D_MODEL), BF16)
    v = jax.ShapeDtypeStruct((1, D_MODEL), F32)
    m = jax.ShapeDtypeStruct((LRU_HEADS, HEAD_DIM, HEAD_DIM), F32)
    scr = pltpu.VMEM((s, CW), F32)
    return _call(
        body, name="mix_lru_bwd", grid=(nblk,),
        in_specs=[res, rows, blk(3), blk(4), col, col, col, col, taps, mat, mat, vec],
        out_specs=[pl.BlockSpec((2, s, CW), lambda n: (0, 0, n)), rows, mat, mat, vec, vec, taps, vec, vec],
        out_shape=[jax.ShapeDtypeStruct((2, s, D_MODEL), BF16), jax.ShapeDtypeStruct((D_MODEL, D_MODEL), BF16), m, m, v, v,
                   jax.ShapeDtypeStruct((4, D_MODEL), F32), v, v],
        scratch_shapes=[scr, scr],
        operands=(db, wlb, proj, proj, xl, r, i, h, wl, wa, wx, lam), ride=ride)


def norm_in_bwd(dh1, x, dx2, g1):
    s, d = x.shape
    t = _token_tile(s)

    def body(dh_ref, x_ref, dx2_ref, g_ref, dx_ref, dg_ref):
        @pl.when(pl.program_id(0) == 0)
        def _():
            dg_ref[...] = jnp.zeros_like(dg_ref)

        n, r = _rms_stats(x_ref[...])
        dx, dg = _rms_bwd(n, r, g_ref[...], dh_ref[...])
        dx_ref[...] = dx2_ref[...] + dx
        dg_ref[...] += jnp.sum(dg, axis=0, keepdims=True)

    tile = pl.BlockSpec((t, d), lambda i: (i, 0))
    vec = pl.BlockSpec((1, d), lambda i: (0, 0))
    return pl.pallas_call(
        body, name="norm_in_bwd", grid=(s // t,),
        in_specs=[tile, tile, tile, vec],
        out_specs=[tile, vec],
        out_shape=[jax.ShapeDtypeStruct((s, d), F32), jax.ShapeDtypeStruct((1, d), F32)],
        compiler_params=_params(),
    )(dh1, x, dx2, g1)


def local_step(x, target, g1, g2, g3, g4, win4, ws, wcb, wl, bl, wa, ba, wx, bx, lam, wlb, wout, wup4, fw, fb, wdown):
    h1 = norm_in(x, g1)
    proj = matmul_cols(h1, win4, "proj_fwd")
    q, ya = mix_conv_fwd(proj, ws)
    xl, r, gi, h, yb = mix_lru_fwd(proj, wl, bl, wa, ba, wx, bx, lam)
    a, b, merged = branch_merge_fwd(ya, yb, wcb, wlb, proj)
    mix, x2, h2 = mix_out_fwd(merged, wout, x, g2, g3)
    up = matmul_cols(h2, wup4, "up_fwd")
    f = ffn_act_fwd(up, fw, fb)
    dy, dout, loss, dg4 = ffn_down_loss(f, wdown, x2, target, g4)

    dug, duv, dwdown, dfw_g, dfw_v, dfb_g, dfb_v = ffn_bwd(dout, wdown, up, fw, fb)
    dup = jnp.concatenate([dug, duv], axis=1)
    dfw = jnp.concatenate([dfw_g, dfw_v], axis=1)
    dfb = jnp.concatenate([dfb_g, dfb_v], axis=1)
    dh2, dwup = dgrad_wgrad_cols(dup, wup4, h2, "up_bwd")
    dx2, dmix, dg3, dg2 = norms_mid_bwd(dh2, x2, dy, mix, g3, g2)
    da, db, dwout, dgc, dgl = mix_out_bwd(dmix, wout, merged, a, b, proj)
    dcb, dcc, dcx, dwcb, dws = mix_conv_bwd(da, wcb, proj, q, ws)
    dlx, dly, dwlb, dwa, dwx, dba, dbx, dwl, dbl, dlam = mix_lru_bwd(db, wlb, proj, xl, r, gi, h, wl, wa, wx, lam)
    dproj = jnp.concatenate([dcb, dcc, dcx, dlx, dly, dgc[:, 5 * D_MODEL:6 * D_MODEL], dgl[:, 6 * D_MODEL:]], axis=1)
    dh1, dwin = dgrad_wgrad_cols(dproj, win4, h1, "proj_bwd")
    dx, dg1 = norm_in_bwd(dh1, x, dx2, g1)
    grads = dict(norm_mix_pre=dg1, norm_mix_post=dg2, norm_ffn_pre=dg3, norm_ffn_post=dg4,
                 w_in=dwin, conv_short_w=dws, w_conv_branch=dwcb, lru_conv_w=dwl, lru_conv_b=dbl,
                 lru_wa=dwa, lru_ba=dba, lru_wx=dwx, lru_bx=dbx, lru_lambda=dlam,
                 w_lru_branch=dwlb, w_out=dwout, ffn_w_up=dwup, ffn_conv_w=dfw, ffn_conv_b=dfb,
                 ffn_w_down=dwdown)
    return loss[0, 0], dx, grads


MESH = pl.DeviceIdType.MESH
_HBM = pl.BlockSpec(memory_space=pltpu.HBM)
_OTHER_CHIPS = ((1, 0), (0, 1), (1, 1))
_OTHER_DEVICES = tuple((dx, dy, dc) for dx in (0, 1) for dy in (0, 1) for dc in (0, 1) if dx or dy or dc)
N_DEVICES = 8


def _position():
    return lax.axis_index("x"), lax.axis_index("y"), lax.axis_index("c")


def _flip(v, d):
    return 1 - v if d else v


def _half_rows(ref, h, hr):
    return ref.at[pl.ds(h * hr, hr), :]


def gather_chips(shards):
    n = len(shards)
    nrel = len(_OTHER_CHIPS)

    def body(*refs):
        ins, outs = refs[:n], refs[n:2 * n]
        ici_send, ici_recv, sib_send, sib_recv = refs[2 * n:]
        x, y, c = _position()
        j = 2 * x + y
        hr = [s.shape[0] // 2 for s in shards]

        def chip(p):
            px, py = _flip(x, _OTHER_CHIPS[p][0]), _flip(y, _OTHER_CHIPS[p][1])
            return px, py, 2 * px + py

        def ici(a, p, slot):
            px, py, _ = chip(p)
            return pltpu.make_async_remote_copy(
                src_ref=_half_rows(ins[a], c, hr[a]), dst_ref=_half_rows(outs[a].at[slot], c, hr[a]),
                send_sem=ici_send.at[a * nrel + p], recv_sem=ici_recv.at[a * nrel + p],
                device_id=(px, py, c), device_id_type=MESH)

        def sib(a, p, h):
            _, _, k = chip(p)
            part = _half_rows(outs[a].at[k], h, hr[a])
            return pltpu.make_async_remote_copy(
                src_ref=part, dst_ref=part, send_sem=sib_send.at[a * nrel + p], recv_sem=sib_recv.at[a * nrel + p],
                device_id=(x, y, 1 - c), device_id_type=MESH)

        pairs = [(a, p) for a in range(n) for p in range(nrel)]
        for a, p in pairs:
            ici(a, p, j).start()
        for a, p in pairs:
            ici(a, p, chip(p)[2]).wait_recv()
            sib(a, p, c).start()
        for a, p in pairs:
            sib(a, p, 1 - c).wait_recv()
        for a, p in pairs:
            ici(a, p, j).wait_send()
            sib(a, p, c).wait_send()

    got = pl.pallas_call(
        body, name="gather_chips",
        in_specs=[_HBM] * n, out_specs=[_HBM] * n,
        out_shape=[jax.ShapeDtypeStruct((N_CHIPS,) + s.shape, s.dtype) for s in shards],
        scratch_shapes=[pltpu.SemaphoreType.DMA((n * nrel,))] * 4,
    )(*shards)
    j = 2 * lax.axis_index("x") + lax.axis_index("y")
    return [lax.dynamic_update_slice(g, s[None], (j, 0, 0)) for g, s in zip(got, shards)]


def _owned_part(ref, kind, k, h, hr):
    if kind == "col":
        ns = ref.shape[1] // N_CHIPS
        return ref.at[pl.ds(h * hr, hr), pl.ds(k * ns, ns)]
    if kind == "row":
        return ref.at[pl.ds(k * 2 * hr + h * hr, hr), :]
    if kind == "col2":
        ns = ref.shape[2] // 2
        return ref.at[k // 2, pl.ds(h * hr, hr), pl.ds((k % 2) * ns, ns)]
    return ref.at[k, pl.ds(h * hr, hr), :]


def _part_shape(g, kind):
    if kind == "col2":
        return g.shape[1] // 2, g.shape[2] // 2
    if kind == "col":
        return g.shape[0] // 2, g.shape[1] // N_CHIPS
    if kind == "row":
        return g.shape[0] // (2 * N_CHIPS), g.shape[1]
    return g.shape[1] // 2, g.shape[2]


def pair_split(grads, kinds, name):
    n = len(grads)
    shapes = [_part_shape(g, k) for g, k in zip(grads, kinds)]

    def body(*refs):
        ins, theirs = refs[:n], refs[n:2 * n]
        send_sem, recv_sem = refs[2 * n:]
        x, y, c = _position()
        copies = []
        for a in range(n):
            hr = shapes[a][0]
            for k in range(N_CHIPS):
                s = a * N_CHIPS + k
                copies.append(pltpu.make_async_remote_copy(
                    src_ref=_owned_part(ins[a], kinds[a], k, 1 - c, hr), dst_ref=theirs[a].at[k],
                    send_sem=send_sem.at[s], recv_sem=recv_sem.at[s], device_id=(x, y, 1 - c), device_id_type=MESH))
        for cp in copies:
            cp.start()
        for cp in copies:
            cp.wait()

    return pl.pallas_call(
        body, name=name,
        in_specs=[_HBM] * n, out_specs=[_HBM] * n,
        out_shape=[jax.ShapeDtypeStruct((N_CHIPS,) + shp, g.dtype) for shp, g in zip(shapes, grads)],
        scratch_shapes=[pltpu.SemaphoreType.DMA((n * N_CHIPS,))] * 2,
    )(*grads)


def chip_exchange(sums, rep):
    n = len(sums)
    nrel = len(_OTHER_CHIPS)
    ndev = len(_OTHER_DEVICES)

    def body(*refs):
        ins, rep_ref = refs[:n], refs[n]
        outs, rep_out = refs[n + 1:2 * n + 1], refs[2 * n + 1]
        loc_sem, send_sem, recv_sem, rep_send, rep_recv = refs[2 * n + 2:]
        x, y, c = _position()
        j = 2 * x + y
        me = 4 * x + 2 * y + c

        def chip(p):
            px, py = _flip(x, _OTHER_CHIPS[p][0]), _flip(y, _OTHER_CHIPS[p][1])
            return px, py, 2 * px + py

        def part(a, p, src_slot, dst_slot):
            px, py, _ = chip(p)
            return pltpu.make_async_remote_copy(
                src_ref=ins[a].at[src_slot], dst_ref=outs[a].at[dst_slot],
                send_sem=send_sem.at[a * nrel + p], recv_sem=recv_sem.at[a * nrel + p],
                device_id=(px, py, c), device_id_type=MESH)

        def device(q):
            dx, dy, dc = _OTHER_DEVICES[q]
            return _flip(x, dx), _flip(y, dy), _flip(c, dc)

        def rep_copy(q, slot):
            return pltpu.make_async_remote_copy(
                src_ref=rep_ref, dst_ref=rep_out.at[slot], send_sem=rep_send.at[q], recv_sem=rep_recv.at[q],
                device_id=device(q), device_id_type=MESH)

        own = [pltpu.make_async_copy(ins[a].at[j], outs[a].at[j], loc_sem.at[a]) for a in range(n)]
        own.append(pltpu.make_async_copy(rep_ref, rep_out.at[me], loc_sem.at[n]))
        for cp in own:
            cp.start()
        pairs = [(a, p) for a in range(n) for p in range(nrel)]
        for a, p in pairs:
            part(a, p, chip(p)[2], j).start()
        for q in range(ndev):
            rep_copy(q, me).start()
        for a, p in pairs:
            part(a, p, chip(p)[2], chip(p)[2]).wait_recv()
        for q in range(ndev):
            px, py, pc = device(q)
            rep_copy(q, 4 * px + 2 * py + pc).wait_recv()
        for a, p in pairs:
            part(a, p, chip(p)[2], j).wait_send()
        for q in range(ndev):
            rep_copy(q, me).wait_send()
        for cp in own:
            cp.wait()

    return pl.pallas_call(
        body, name="chip_exchange",
        in_specs=[_HBM] * (n + 1), out_specs=[_HBM] * (n + 1),
        out_shape=[jax.ShapeDtypeStruct(s.shape, s.dtype) for s in sums]
        + [jax.ShapeDtypeStruct((N_DEVICES,) + rep.shape, rep.dtype)],
        scratch_shapes=[pltpu.SemaphoreType.DMA((n + 1,)), pltpu.SemaphoreType.DMA((n * nrel,)),
                        pltpu.SemaphoreType.DMA((n * nrel,)), pltpu.SemaphoreType.DMA((ndev,)),
                        pltpu.SemaphoreType.DMA((ndev,))],
    )(*sums, rep)


def pair_swap(halves):
    n = len(halves)

    def body(*refs):
        ins, outs = refs[:n], refs[n:2 * n]
        send_sem, recv_sem = refs[2 * n:]
        x, y, c = _position()
        copies = [pltpu.make_async_remote_copy(
            src_ref=ins[a], dst_ref=outs[a], send_sem=send_sem.at[a], recv_sem=recv_sem.at[a],
            device_id=(x, y, 1 - c), device_id_type=MESH) for a in range(n)]
        for cp in copies:
            cp.start()
        for cp in copies:
            cp.wait()

    return pl.pallas_call(
        body, name="pair_swap",
        in_specs=[_HBM] * n, out_specs=[_HBM] * n,
        out_shape=[jax.ShapeDtypeStruct(h.shape, h.dtype) for h in halves],
        scratch_shapes=[pltpu.SemaphoreType.DMA((n,))] * 2,
    )(*halves)


def _row_tile(rows, cols, limit_bytes=1 << 20):
    best = None
    for t in range(SUBLANES, rows + 1, SUBLANES):
        if rows % t == 0 and t * cols * 4 <= limit_bytes:
            best = t
    return best or rows


def add_pair(g, kind, theirs, core, name):
    nc, rows, cols = theirs.shape
    t = _row_tile(rows, cols, 4 << 20)
    nt = rows // t

    def body(core_ref, g_ref, b_ref, o_ref):
        mine = g_ref[...].reshape(t, cols)
        o_ref[0] = (mine.astype(F32) + b_ref[0].astype(F32)).astype(o_ref.dtype)

    if kind == "col":
        own = pl.BlockSpec((t, cols), lambda k, i, c: (c[0] * nt + i, k))
    elif kind == "col2":
        own = pl.BlockSpec((1, t, cols), lambda k, i, c: (k // 2, c[0] * nt + i, k % 2))
    elif kind == "row":
        own = pl.BlockSpec((t, cols), lambda k, i, c: ((2 * k + c[0]) * nt + i, 0))
    else:
        own = pl.BlockSpec((1, t, cols), lambda k, i, c: (k, c[0] * nt + i, 0))
    spec = pl.BlockSpec((1, t, cols), lambda k, i, c: (k, i, 0))
    return pl.pallas_call(
        body, name=name,
        grid_spec=pltpu.PrefetchScalarGridSpec(num_scalar_prefetch=1, grid=(nc, nt), in_specs=[own, spec], out_specs=spec),
        out_shape=jax.ShapeDtypeStruct(theirs.shape, theirs.dtype), compiler_params=_params(),
    )(core, g, theirs)


def sum_lead(a, name):
    nl, rows, cols = a.shape
    t = _row_tile(rows, cols, (1 << 20) // 2)

    def body(a_ref, o_ref):
        acc = a_ref[0].astype(F32)
        for s in range(1, nl):
            acc = acc + a_ref[s].astype(F32)
        o_ref[...] = acc

    return pl.pallas_call(
        body, name=name, grid=(rows // t,),
        in_specs=[pl.BlockSpec((nl, t, cols), lambda i: (0, i, 0))],
        out_specs=pl.BlockSpec((t, cols), lambda i: (i, 0)),
        out_shape=jax.ShapeDtypeStruct((rows, cols), F32), compiler_params=_params(),
    )(a)


def sum_chips(rx, csum, chip, name):
    nc, rows, cols = rx.shape
    t = _row_tile(rows, cols, 2 << 20)

    def body(chip_ref, r0, r1, r2, r3, own_ref, o_ref):
        acc = None
        for s, ref in enumerate((r0, r1, r2, r3)):
            term = jnp.where(chip_ref[0] == s, own_ref[0], ref[0]).astype(F32)
            acc = term if acc is None else acc + term
        o_ref[...] = acc

    def slot(s):
        return pl.BlockSpec((1, t, cols), lambda i, c, s=s: (jnp.where(c[0] == s, c[0] ^ 1, s), i, 0))

    return pl.pallas_call(
        body, name=name,
        grid_spec=pltpu.PrefetchScalarGridSpec(
            num_scalar_prefetch=1, grid=(rows // t,),
            in_specs=[slot(s) for s in range(nc)] + [pl.BlockSpec((1, t, cols), lambda i, c: (c[0], i, 0))],
            out_specs=pl.BlockSpec((t, cols), lambda i, c: (i, 0))),
        out_shape=jax.ShapeDtypeStruct((rows, cols), F32), compiler_params=_params(),
    )(chip, rx, rx, rx, rx, csum)


def _adamw_update(w, g, m, v):
    nm = ADAM_B1 * m + (1.0 - ADAM_B1) * g
    nv = ADAM_B2 * v + (1.0 - ADAM_B2) * (g * g)
    m_hat = nm * (1.0 / (1.0 - ADAM_B1 ** ADAM_STEP))
    v_hat = nv * (1.0 / (1.0 - ADAM_B2 ** ADAM_STEP))
    return -ADAM_LR * (m_hat / (jnp.sqrt(v_hat) + ADAM_EPS) + ADAM_WD * w), nm, nv


def adamw(w, g, m, v, name):
    rows, cols = w.shape
    t = _row_tile(rows, cols)

    def body(w_ref, g_ref, m_ref, v_ref, d_ref, nm_ref, nv_ref):
        d_ref[...], nm_ref[...], nv_ref[...] = _adamw_update(w_ref[...], g_ref[...], m_ref[...], v_ref[...])

    spec = pl.BlockSpec((t, cols), lambda i: (i, 0))
    shp = jax.ShapeDtypeStruct((rows, cols), F32)
    return pl.pallas_call(
        body, name=name, grid=(rows // t,), in_specs=[spec] * 4, out_specs=[spec] * 3,
        out_shape=[shp, shp, shp], compiler_params=_params(),
    )(w, g, m, v)


def adamw_halves(w, g_mine, g_other, m, v, core, name):
    rows, cols = w.shape
    hr = rows // 2
    t = _row_tile(hr, cols)
    nt = hr // t

    def body(core_ref, w_ref, gm_ref, go_ref, m_ref, v_ref, g_ref, d_ref, nm_ref, nv_ref):
        g = jnp.where(pl.program_id(0) // nt == core_ref[0], gm_ref[...], go_ref[...])
        g_ref[...] = g
        d_ref[...], nm_ref[...], nv_ref[...] = _adamw_update(w_ref[...], g, m_ref[...], v_ref[...])

    spec = pl.BlockSpec((t, cols), lambda i, c: (i, 0))
    half = pl.BlockSpec((t, cols), lambda i, c: (i % nt, 0))
    shp = jax.ShapeDtypeStruct((rows, cols), F32)
    return pl.pallas_call(
        body, name=name,
        grid_spec=pltpu.PrefetchScalarGridSpec(num_scalar_prefetch=1, grid=(2 * nt,),
                                               in_specs=[spec, half, half, spec, spec], out_specs=[spec] * 4),
        out_shape=[shp] * 4, compiler_params=_params(),
    )(core, w, g_mine, g_other, m, v)


WEIGHTS = ("norm_mix_pre", "norm_mix_post", "norm_ffn_pre", "norm_ffn_post", "w_in", "conv_short_w",
           "w_conv_branch", "lru_conv_w", "lru_conv_b", "lru_wa", "lru_ba", "lru_wx", "lru_bx", "lru_lambda",
           "w_lru_branch", "w_out", "ffn_w_up", "ffn_conv_w", "ffn_conv_b", "ffn_w_down")
BIG = ("w_in", "ffn_w_up", "w_conv_branch", "w_lru_branch", "w_out", "ffn_w_down")
BIG_KIND = ("col", "col", "row", "row", "row", "row")
SMALL = ("conv_short_w", "lru_conv_w", "lru_wa", "lru_ba", "lru_wx", "lru_bx", "ffn_conv_w")
REPL = ("norm_mix_pre", "norm_mix_post", "norm_ffn_pre", "norm_ffn_post", "lru_conv_b", "lru_lambda", "ffn_conv_b")
PACK_W = 256
SMALL_ROWS = 576
REPL_ROWS = 16
LOSS_ROW = 12
FFN_SHARD = 2 * D_FF // N_CHIPS
QUARTER = HEAD_DIM // N_CHIPS
SMALL_PARTS = (("conv_short_w", 3, (1, 3, PACK_W)), ("lru_conv_w", 4, (1, 4, PACK_W)),
               ("lru_wa", LRU_HEADS * QUARTER, (1, LRU_HEADS, QUARTER, HEAD_DIM)), ("lru_ba", 1, (1, LRU_HEADS, QUARTER)),
               ("lru_wx", LRU_HEADS * QUARTER, (1, LRU_HEADS, QUARTER, HEAD_DIM)), ("lru_bx", 1, (1, LRU_HEADS, QUARTER)),
               ("ffn_conv_w", 3 * FFN_SHARD // PACK_W, (1, 3, FFN_SHARD)))


def _pad8(nr):
    return -(-nr // SUBLANES) * SUBLANES


def _pack_small_shard(p):
    rows = [jnp.pad(p[name].reshape(nr, PACK_W), ((0, _pad8(nr) - nr), (0, 0))) for name, nr, _ in SMALL_PARTS]
    used = sum(r.shape[0] for r in rows)
    return jnp.concatenate(rows + [jnp.zeros((SMALL_ROWS - used, PACK_W), F32)], axis=0)


def _unpack_small_shard(buf):
    out, r = {}, 0
    for name, nr, shape in SMALL_PARTS:
        out[name] = buf[r:r + nr].reshape(shape)
        r += _pad8(nr)
    return out


def _full_small(g4):
    per = [_unpack_small_shard(g4[k]) for k in range(N_CHIPS)]
    cat = lambda name, axis: jnp.concatenate([per[k][name][0] for k in range(N_CHIPS)], axis=axis)
    return dict(conv_short_w=cat("conv_short_w", 1), lru_conv_w=cat("lru_conv_w", 1),
                lru_wa=cat("lru_wa", 1), lru_ba=cat("lru_ba", 1).reshape(1, D_MODEL),
                lru_wx=cat("lru_wx", 1), lru_bx=cat("lru_bx", 1).reshape(1, D_MODEL),
                ffn_conv_w=cat("ffn_conv_w", 1))


def _split_small(full):
    shards = []
    for k in range(N_CHIPS):
        cols = lambda a, w: a[:, k * w:(k + 1) * w]
        q = slice(k * QUARTER, (k + 1) * QUARTER)
        shards.append(_pack_small_shard(dict(
            conv_short_w=cols(full["conv_short_w"], PACK_W), lru_conv_w=cols(full["lru_conv_w"], PACK_W),
            lru_wa=full["lru_wa"][:, q, :], lru_ba=full["lru_ba"].reshape(LRU_HEADS, HEAD_DIM)[:, q],
            lru_wx=full["lru_wx"][:, q, :], lru_bx=full["lru_bx"].reshape(LRU_HEADS, HEAD_DIM)[:, q],
            ffn_conv_w=cols(full["ffn_conv_w"], FFN_SHARD))))
    return jnp.stack(shards)


def _pack_repl(p, loss=None):
    rows = [p[n].reshape(-1, D_MODEL) for n in REPL]
    if loss is not None:
        rows.append(jnp.broadcast_to(loss.reshape(1, 1), (1, D_MODEL)))
    used = sum(r.shape[0] for r in rows)
    return jnp.concatenate(rows + [jnp.zeros((REPL_ROWS - used, D_MODEL), F32)], axis=0)


def _unpack_repl(buf):
    out, r = {}, 0
    for n in REPL:
        nr = (2 * D_FF // D_MODEL) if n == "ffn_conv_b" else 1
        out[n] = buf[r:r + nr].reshape(1, nr * D_MODEL)
        r += nr
    return out


def kernel(x, norm_mix_pre, norm_mix_post, norm_ffn_pre, norm_ffn_post, w_in, conv_short_w, w_conv_branch, lru_conv_w, lru_conv_b, lru_wa, lru_ba, lru_wx, lru_bx, lru_lambda, w_lru_branch, w_out, ffn_w_up, ffn_conv_w, ffn_conv_b, ffn_w_down, loss_target, m_norm_mix_pre, m_norm_mix_post, m_norm_ffn_pre, m_norm_ffn_post, m_w_in, m_conv_short_w, m_w_conv_branch, m_lru_conv_w, m_lru_conv_b, m_lru_wa, m_lru_ba, m_lru_wx, m_lru_bx, m_lru_lambda, m_w_lru_branch, m_w_out, m_ffn_w_up, m_ffn_conv_w, m_ffn_conv_b, m_ffn_w_down, v_norm_mix_pre, v_norm_mix_post, v_norm_ffn_pre, v_norm_ffn_post, v_w_in, v_conv_short_w, v_w_conv_branch, v_lru_conv_w, v_lru_conv_b, v_lru_wa, v_lru_ba, v_lru_wx, v_lru_bx, v_lru_lambda, v_w_lru_branch, v_w_out, v_ffn_w_up, v_ffn_conv_w, v_ffn_conv_b, v_ffn_w_down):
    given = dict(locals())
    w = {n: given[n] for n in WEIGHTS}
    m = {n: given["m_" + n] for n in WEIGHTS}
    v = {n: given["v_" + n] for n in WEIGHTS}

    xi, yi, ci = _position()
    chip_i = 2 * xi + yi
    chip = chip_i.astype(jnp.int32).reshape(1)
    core = ci.astype(jnp.int32).reshape(1)
    xs, target = x[0], loss_target[0]
    g1, g2, g3, g4 = w["norm_mix_pre"], w["norm_mix_post"], w["norm_ffn_pre"], w["norm_ffn_post"]
    shard = {n: w[n][0].astype(BF16) for n in BIG}
    small_shard = _pack_small_shard(w)

    def gathered(bufs, names):
        return [_own_slot(b, small_shard if n == "small" else shard[n], chip_i) for b, n in zip(bufs, names)]

    def chip_sums(arrays, kinds, tag):
        theirs = pair_split(arrays, kinds, "pair_split_" + tag)
        return [add_pair(g, k, t, core, "pair_add_%s_%d" % (tag, i)) for i, (g, k, t) in enumerate(zip(arrays, kinds, theirs))]

    h1 = norm_in(xs, g1)
    win4, small4 = gathered(run_ride(gather_ride([shard["w_in"], small_shard]), "gather_first"), ("w_in", "small"))
    small = _full_small(small4)
    (proj,), got = matmul_cols(h1, win4, "proj_fwd",
                               ride=gather_ride([shard["w_conv_branch"], shard["w_lru_branch"], shard["w_out"]]))
    wcb, wlb, wout = [g.reshape(-1, D_MODEL) for g in gathered(got, ("w_conv_branch", "w_lru_branch", "w_out"))]
    up_piece = lambda r0, nr, into=None: gather_ride([shard["ffn_w_up"]], items=[(0, r0, nr)], into=into)
    down_piece = lambda r0, nr, into=None: gather_ride([shard["ffn_w_down"]], items=[(0, r0, nr)], into=into)
    (q, ya), got = mix_conv_fwd(proj, small["conv_short_w"], ride=up_piece(0, 128))
    (xl, r, gi, h, yb), got = mix_lru_fwd(
        proj, small["lru_conv_w"], w["lru_conv_b"], small["lru_wa"].astype(BF16), small["lru_ba"],
        small["lru_wx"].astype(BF16), small["lru_bx"], w["lru_lambda"], ride=up_piece(128, 512, got))
    (a, b, merged), got = branch_merge_fwd(ya, yb, wcb, wlb, proj, ride=up_piece(640, 384, got))
    (wup4,) = gathered(got, ("ffn_w_up",))
    (mix, x2, h2), got = mix_out_fwd(merged, wout, xs, g2, g3, ride=down_piece(0, 256))
    (up, act, f), got = ffn_up_act_fwd(h2, wup4, small["ffn_conv_w"], w["ffn_conv_b"], ride=down_piece(256, 512, got))
    wdown = gathered(got, ("ffn_w_down",))[0].reshape(-1, D_MODEL)
    dy, dout, loss, dg4 = ffn_down_loss(f, wdown, x2, target, g4)

    dh2, dwup, dwdown, dfw, dfb = ffn_up_bwd(dout, wdown, up, act, f, small["ffn_conv_w"], wup4, h2)
    cs_down, cs_up = chip_sums([dwdown, dwup], ["row", "col2"], "ffn")
    down_rows = lambda r0, nr, into=None: exchange_ride([cs_down], items=[(0, r0, nr)], into=into)
    up_rows = lambda r0, nr, into=None: exchange_ride([cs_up], items=[(0, r0, nr)], into=into)
    (dx2, dmix, dg3, dg2), rx_down = norms_mid_bwd(dh2, x2, dy, mix, g3, g2, ride=down_rows(0, 128))
    (da, db, dwout, dgates), rx_down = mix_out_bwd(dmix, wout, merged, a, b, proj, ride=down_rows(128, 256, rx_down))
    (dconv, dwcb, dws), rx_up = mix_conv_bwd(da, wcb, proj, q, small["conv_short_w"], ride=up_rows(0, 176))
    cs_mid = chip_sums([dwout, dwcb], ["row", "row"], "mid")
    (dlru, dwlb, dwa, dwx, dba, dbx, dwl, dbl, dlam), rx_up = mix_lru_bwd(
        db, wlb, proj, xl, r, gi, h, small["lru_conv_w"], small["lru_wa"].astype(BF16), small["lru_wx"].astype(BF16),
        w["lru_lambda"], ride=up_rows(176, 336, rx_up))
    grads = dict(norm_mix_post=dg2, norm_ffn_pre=dg3, norm_ffn_post=dg4, conv_short_w=dws, lru_conv_w=dwl,
                 lru_conv_b=dbl, lru_wa=dwa, lru_ba=dba, lru_wx=dwx, lru_bx=dbx, lru_lambda=dlam,
                 ffn_conv_w=jnp.concatenate([dfw[0], dfw[1]], axis=1), ffn_conv_b=jnp.concatenate([dfb[0], dfb[1]], axis=1))
    cs_late = chip_sums([dwlb, _split_small(grads)], ["row", "lead"], "late")
    (dh1, dwin), rx_all = dgrad_wgrad_cols([dconv, dlru, dgates], win4, h1, "proj_bwd", ride=exchange_ride(cs_mid + cs_late))
    rx_mid, rx_late = rx_all[:2], rx_all[2:]
    dx, grads["norm_mix_pre"] = norm_in_bwd(dh1, xs, dx2, g1)
    cs_in = chip_sums([dwin], ["col"], "in")
    rep_part = _pack_repl(grads, loss[0, 0])
    rx_in, rep_all = run_ride(exchange_ride(cs_in, rep=rep_part), "exchange_last")

    order = (("w_in", cs_in[0], rx_in), ("ffn_w_up", cs_up, rx_up[0]), ("w_conv_branch", cs_mid[1], rx_mid[1]),
             ("w_lru_branch", cs_late[0], rx_late[0]), ("w_out", cs_mid[0], rx_mid[0]),
             ("ffn_w_down", cs_down, rx_down[0]), ("small", cs_late[1], rx_late[1]))
    halves = [sum_chips(rx, cs, chip, "chip_sum_" + n) for n, cs, rx in order]
    me = 4 * xi + 2 * yi + ci
    rep_grad = sum_lead(_own_slot(rep_all, rep_part, me), "device_sum")
    others = pair_swap(halves)

    g_out, d_out, m_out, v_out = {}, {}, {}, {}
    for n, gm, go in zip(BIG, halves[:-1], others[:-1]):
        g, d, nm, nv = adamw_halves(w[n][0], gm, go, m[n][0], v[n][0], core, "adamw_" + n)
        g_out[n], d_out[n], m_out[n], v_out[n] = g[None], d[None], nm[None], nv[None]
    bufs = adamw_halves(small_shard, halves[-1], others[-1], _pack_small_shard(m), _pack_small_shard(v),
                        core, "adamw_small")
    for dst, buf in zip((g_out, d_out, m_out, v_out), bufs):
        dst.update(_unpack_small_shard(buf))
    d, nm, nv = adamw(_pack_repl(w), rep_grad, _pack_repl(m), _pack_repl(v), "adamw_repl")
    for dst, buf in ((g_out, rep_grad), (d_out, d), (m_out, nm), (v_out, nv)):
        dst.update(_unpack_repl(buf))

    return (rep_grad[LOSS_ROW, 0], dx[None], *[g_out[n] for n in WEIGHTS], *[d_out[n] for n in WEIGHTS],
            *[m_out[n] for n in WEIGHTS], *[v_out[n] for n in WEIGHTS])
```

```python
import functools
import math

import jax
import jax.numpy as jnp
from jax import lax
from jax.experimental import pallas as pl
from jax.experimental.pallas import tpu as pltpu

F32 = jnp.float32
BF16 = jnp.bfloat16

D_MODEL = 1024
N_CHIPS = 4
N_SEG = 7
D_FF = 3 * D_MODEL
LRU_HEADS = 4
HEAD_DIM = D_MODEL // LRU_HEADS
LRU_C = 8.0
RMS_EPS = 1e-6
CW = 256
FW = 256
SUBLANES = 8
VMEM_LIMIT = 58 * 1024 * 1024

ADAM_LR = 0.001
ADAM_B1 = 0.9
ADAM_B2 = 0.999
ADAM_EPS = 1e-08
ADAM_WD = 0.01
ADAM_STEP = 10

_GELU_C = math.sqrt(2.0 / math.pi)
_GELU_K = 0.044715


def _params(**kw):
    return pltpu.CompilerParams(vmem_limit_bytes=VMEM_LIMIT, **kw)


def _sigmoid(x):
    return 1.0 / (1.0 + jnp.exp(-x))


def _gelu(x):
    t = jnp.tanh(_GELU_C * (x + _GELU_K * x * x * x))
    return 0.5 * x * (1.0 + t)


def _gelu_and_grad(x):
    x2 = x * x
    t = jnp.tanh(_GELU_C * (x + _GELU_K * x * x2))
    g = 0.5 * x * (1.0 + t)
    dg = 0.5 * (1.0 + t) + 0.5 * x * (1.0 - t * t) * _GELU_C * (1.0 + 3.0 * _GELU_K * x2)
    return g, dg


def _log_sigmoid(x):
    e = jnp.exp(-jnp.abs(x))
    u = 1.0 + e
    l1p = jnp.where(u == 1.0, e, jnp.log(u) * e / (u - 1.0))
    return jnp.minimum(x, 0.0) - l1p


def _neg_expm1(z):
    series = -z * (1.0 + z * (0.5 + z * (1.0 / 6.0 + z * (1.0 / 24.0 + z * (1.0 / 120.0 + z * (1.0 / 720.0))))))
    return jnp.where(z > -0.2, series, 1.0 - jnp.exp(z))


def _rows(shape):
    return lax.broadcasted_iota(jnp.int32, shape, 0)


def _shift_down(x, k):
    return jnp.where(_rows(x.shape) >= k, pltpu.roll(x, k, 0), 0.0)


def _shift_up(x, k):
    n = x.shape[0]
    return jnp.where(_rows(x.shape) < n - k, pltpu.roll(x, n - k, 0), 0.0)


def _delays(x, k_width):
    return [x] + [_shift_down(x, j) for j in range(1, k_width)]


def _advances(dy, k_width):
    return [dy] + [_shift_up(dy, j) for j in range(1, k_width)]


def _taps_sum(shifted, w_ref, b=None):
    k_width = w_ref.shape[0]
    y = w_ref[k_width - 1:k_width, :] * shifted[0]
    for j in range(1, k_width):
        y = y + w_ref[k_width - 1 - j:k_width - j, :] * shifted[j]
    if b is not None:
        y = y + b
    return y


def _causal_conv(x, w_ref, b=None):
    return _taps_sum(_delays(x, w_ref.shape[0]), w_ref, b)


def _conv_wgrad(advanced, x):
    k_width = len(advanced)
    rows = [jnp.sum(advanced[k_width - 1 - k] * x, axis=0, keepdims=True) for k in range(k_width)]
    return jnp.concatenate(rows, axis=0)


def _dot(a, b):
    return jnp.dot(a, b, preferred_element_type=F32)


def _dot_nt(a, b):
    return lax.dot_general(a, b, (((1,), (1,)), ((), ())), preferred_element_type=F32)


def _dot_tn(a, b):
    return lax.dot_general(a, b, (((0,), (0,)), ((), ())), preferred_element_type=F32)


def _rms_stats(x):
    r = lax.rsqrt(jnp.mean(x * x, axis=-1, keepdims=True) + RMS_EPS)
    return x * r, r


def _rms_bwd(n, r, g, dy):
    dn = dy * g
    dx = r * (dn - n * jnp.mean(dn * n, axis=-1, keepdims=True))
    return dx, dy * n


def _scan_forward(a_ref, b_ref, h_ref):
    n, c = a_ref.shape
    row = lax.broadcasted_iota(jnp.int32, (SUBLANES, c), 0)

    def group(g, carry):
        r0 = pl.multiple_of(g * SUBLANES, SUBLANES)
        a = a_ref[pl.ds(r0, SUBLANES), :]
        b = b_ref[pl.ds(r0, SUBLANES), :]
        for k in (1, 2, 4):
            ap = jnp.where(row >= k, pltpu.roll(a, k, 0), 1.0)
            bp = jnp.where(row >= k, pltpu.roll(b, k, 0), 0.0)
            b = a * bp + b
            a = a * ap
        h = a * carry + b
        h_ref[pl.ds(r0, SUBLANES), :] = h
        return h[SUBLANES - 1:SUBLANES, :]

    lax.fori_loop(0, n // SUBLANES, group, jnp.zeros((1, c), F32))


def _scan_backward(c_ref, b_ref, g_ref):
    n, ch = c_ref.shape
    row = lax.broadcasted_iota(jnp.int32, (SUBLANES, ch), 0)
    n_groups = n // SUBLANES

    def group(i, carry):
        r0 = pl.multiple_of((n_groups - 1 - i) * SUBLANES, SUBLANES)
        a = c_ref[pl.ds(r0, SUBLANES), :]
        b = b_ref[pl.ds(r0, SUBLANES), :]
        for k in (1, 2, 4):
            keep = row < SUBLANES - k
            ap = jnp.where(keep, pltpu.roll(a, SUBLANES - k, 0), 1.0)
            bp = jnp.where(keep, pltpu.roll(b, SUBLANES - k, 0), 0.0)
            b = a * bp + b
            a = a * ap
        g = a * carry + b
        g_ref[pl.ds(r0, SUBLANES), :] = g
        return g[0:1, :]

    lax.fori_loop(0, n_groups, group, jnp.zeros((1, ch), F32))


MESH = pl.DeviceIdType.MESH
_HBM = pl.BlockSpec(memory_space=pltpu.HBM)
_OTHER_CHIPS = ((1, 0), (0, 1), (1, 1))
_OTHER_DEVICES = tuple((dx, dy, dc) for dx in (0, 1) for dy in (0, 1) for dc in (0, 1) if dx or dy or dc)
N_DEVICES = 8


def _position():
    return lax.axis_index("x"), lax.axis_index("y"), lax.axis_index("c")


def _flip(v, d):
    return 1 - v if d else v


def _chip(x, y, p):
    px, py = _flip(x, _OTHER_CHIPS[p][0]), _flip(y, _OTHER_CHIPS[p][1])
    return px, py, 2 * px + py


class _Ride:
    def __init__(self, srcs, bufs, scratch, plan):
        self.srcs, self.bufs, self.scratch, self.plan = list(srcs), list(bufs), list(scratch), plan


def _call(body, *, name, grid, in_specs, out_specs, out_shape, operands, scratch_shapes=(), ride=None):
    in_specs, out_specs, out_shape = list(in_specs), list(out_specs), list(out_shape)
    scratch_shapes = list(scratch_shapes)
    if ride is None:
        return pl.pallas_call(body, name=name, grid=grid, in_specs=in_specs, out_specs=out_specs, out_shape=out_shape,
                              scratch_shapes=scratch_shapes, compiler_params=_params())(*operands)
    n_in, n_out, n_scr = len(in_specs), len(out_shape), len(scratch_shapes)
    old = [i for i, b in enumerate(ride.bufs) if not isinstance(b, jax.ShapeDtypeStruct)]
    n_src, n_old, n_buf = len(ride.srcs), len(old), len(ride.bufs)

    def full_body(*refs):
        o0 = n_in + n_src + n_old
        s0 = o0 + n_out + n_buf
        start, relay, relay_on, finish = ride.plan(refs[n_in:n_in + n_src], refs[o0 + n_out:s0], refs[s0 + n_scr:])
        ids = [pl.program_id(i) for i in range(len(grid))]
        first = functools.reduce(jnp.logical_and, [i == 0 for i in ids])
        middle = functools.reduce(jnp.logical_and, [ids[0] == grid[0] // 2] + [i == 0 for i in ids[1:]])
        last = functools.reduce(jnp.logical_and, [i == g - 1 for i, g in zip(ids, grid)])
        pl.when(first)(start)
        pl.when(middle)(relay)
        pl.when(last)(relay_on)
        body(*refs[:n_in], *refs[o0:o0 + n_out], *refs[s0:s0 + n_scr])
        pl.when(last)(finish)

    shapes = [jax.ShapeDtypeStruct(b.shape, b.dtype) for b in ride.bufs]
    res = pl.pallas_call(
        full_body, name=name, grid=grid,
        in_specs=in_specs + [_HBM] * (n_src + n_old), out_specs=out_specs + [_HBM] * n_buf,
        out_shape=out_shape + shapes, scratch_shapes=scratch_shapes + ride.scratch,
        input_output_aliases={n_in + n_src + k: n_out + i for k, i in enumerate(old)},
        compiler_params=_params(),
    )(*operands, *ride.srcs, *[ride.bufs[i] for i in old])
    return list(res[:n_out]), list(res[n_out:])


def run_ride(ride, name):
    def body():
        pass

    return _call(body, name=name, grid=(1,), in_specs=[], out_specs=[], out_shape=[], operands=[], ride=ride)[1]


def gather_ride(shards, items=None, into=None):
    items = items or [(a, 0, s.shape[0]) for a, s in enumerate(shards)]
    bufs = into or [jax.ShapeDtypeStruct((N_CHIPS,) + s.shape, s.dtype) for s in shards]
    nrel = len(_OTHER_CHIPS)

    def plan(srcs, dsts, sems):
        ici_send, ici_recv, hop_send, hop_recv, sib_send, sib_recv = sems
        x, y, c = _position()
        j = 2 * x + y

        def rows(ref, it, h, q=None):
            half = it[2] // 2
            if q is None:
                return ref.at[pl.ds(it[1] + h * half, half), :]
            return ref.at[pl.ds(it[1] + h * half + q * (half // 2), half // 2), :]

        def ici(i, p, slot):
            it = items[i]
            px, py, _ = _chip(x, y, p)
            return pltpu.make_async_remote_copy(
                src_ref=rows(srcs[it[0]], it, c), dst_ref=rows(dsts[it[0]].at[slot], it, c),
                send_sem=ici_send.at[i * nrel + p], recv_sem=ici_recv.at[i * nrel + p],
                device_id=(px, py, c), device_id_type=MESH)

        def hop(i, p, slot):
            it = items[i]
            part = rows(dsts[it[0]].at[slot], it, c, p)
            px, py, _ = _chip(x, y, 1 - p)
            return pltpu.make_async_remote_copy(
                src_ref=part, dst_ref=part, send_sem=hop_send.at[i * 2 + p], recv_sem=hop_recv.at[i * 2 + p],
                device_id=(px, py, c), device_id_type=MESH)

        def sib(i, p, h):
            it = items[i]
            part = rows(dsts[it[0]].at[_chip(x, y, p)[2]], it, h)
            return pltpu.make_async_remote_copy(
                src_ref=part, dst_ref=part, send_sem=sib_send.at[i * nrel + p], recv_sem=sib_recv.at[i * nrel + p],
                device_id=(x, y, 1 - c), device_id_type=MESH)

        every = range(len(items))
        diag = _chip(x, y, 2)[2]

        def start():
            for i in every:
                for p in (0, 1):
                    ici(i, p, j).start()

        def relay():
            for i in every:
                for p in (0, 1):
                    k = _chip(x, y, p)[2]
                    ici(i, p, k).wait_recv()
                    hop(i, p, k).start()
                    sib(i, p, c).start()

        def relay_on():
            for i in every:
                for p in (0, 1):
                    hop(i, p, diag).wait_recv()
                sib(i, 2, c).start()

        def finish():
            for i in every:
                for p in range(nrel):
                    sib(i, p, 1 - c).wait_recv()
            for i in every:
                for p in (0, 1):
                    ici(i, p, j).wait_send()
                    hop(i, p, _chip(x, y, p)[2]).wait_send()
                for p in range(nrel):
                    sib(i, p, c).wait_send()

        return start, relay, relay_on, finish

    n = len(items)
    sems = [pltpu.SemaphoreType.DMA((n * nrel,))] * 2 + [pltpu.SemaphoreType.DMA((n * 2,))] * 2 \
        + [pltpu.SemaphoreType.DMA((n * nrel,))] * 2
    return _Ride(shards, bufs, sems, plan)


def exchange_ride(sums, items=None, into=None, rep=None):
    items = [(a, 0, s.shape[1]) for a, s in enumerate(sums)] if items is None else items
    into = into or [None] * len(sums)
    bufs = [jax.ShapeDtypeStruct(s.shape, s.dtype) if b is None else b for s, b in zip(sums, into)]
    srcs = list(sums)
    scratch = [pltpu.SemaphoreType.DMA((max(len(items), 1) * len(_OTHER_CHIPS),))] * 2
    if rep is not None:
        srcs.append(rep)
        bufs.append(jax.ShapeDtypeStruct((N_DEVICES,) + rep.shape, rep.dtype))
        scratch += [pltpu.SemaphoreType.DMA((len(_OTHER_DEVICES),))] * 2
    nrel = len(_OTHER_CHIPS)

    def plan(src_refs, dst_refs, sems):
        x, y, c = _position()
        j = 2 * x + y
        me = 4 * x + 2 * y + c

        def part(i, p, src_slot, dst_slot):
            a, r0, nr = items[i]
            px, py, _ = _chip(x, y, p)
            return pltpu.make_async_remote_copy(
                src_ref=src_refs[a].at[src_slot, pl.ds(r0, nr), :], dst_ref=dst_refs[a].at[dst_slot, pl.ds(r0, nr), :],
                send_sem=sems[0].at[i * nrel + p], recv_sem=sems[1].at[i * nrel + p],
                device_id=(px, py, c), device_id_type=MESH)

        def device(q):
            dx, dy, dc = _OTHER_DEVICES[q]
            return _flip(x, dx), _flip(y, dy), _flip(c, dc)

        def rep_copy(q, slot):
            return pltpu.make_async_remote_copy(
                src_ref=src_refs[-1], dst_ref=dst_refs[-1].at[slot], send_sem=sems[2].at[q], recv_sem=sems[3].at[q],
                device_id=device(q), device_id_type=MESH)

        pairs = [(i, p) for i in range(len(items)) for p in range(nrel)]
        others = range(len(_OTHER_DEVICES)) if rep is not None else ()

        def start():
            for i, p in pairs:
                part(i, p, _chip(x, y, p)[2], j).start()
            for q in others:
                rep_copy(q, me).start()

        def finish():
            for i, p in pairs:
                k = _chip(x, y, p)[2]
                part(i, p, k, k).wait_recv()
            for q in others:
                px, py, pc = device(q)
                rep_copy(q, 4 * px + 2 * py + pc).wait_recv()
            for i, p in pairs:
                part(i, p, _chip(x, y, p)[2], j).wait_send()
            for q in others:
                rep_copy(q, me).wait_send()

        return start, lambda: None, lambda: None, finish

    return _Ride(srcs, bufs, scratch, plan)


def _own_slot(buf, own, index):
    return lax.dynamic_update_slice(buf, own[None], (index,) + (0,) * own.ndim)


def _token_tile(s):
    return min(s, 512)


def norm_in(x, g):
    s, d = x.shape
    t = _token_tile(s)

    def body(x_ref, g_ref, o_ref, ot_ref):
        n, _ = _rms_stats(x_ref[...])
        h = n * g_ref[...]
        o_ref[...] = h.astype(BF16)
        ot_ref[...] = h.T.astype(BF16)

    return pl.pallas_call(
        body, name="norm_in", grid=(s // t,),
        in_specs=[pl.BlockSpec((t, d), lambda i: (i, 0)), pl.BlockSpec((1, d), lambda i: (0, 0))],
        out_specs=[pl.BlockSpec((t, d), lambda i: (i, 0)), pl.BlockSpec((d, t), lambda i: (0, i))],
        out_shape=[jax.ShapeDtypeStruct((s, d), BF16), jax.ShapeDtypeStruct((d, s), BF16)],
        compiler_params=_params(),
    )(x, g)


def matmul_cols(a, w4, name, ride=None):
    m, k = a.shape
    nj, _, ns = w4.shape
    nb = ns // CW

    def body(a_ref, w_ref, o_ref):
        o_ref[...] = _dot(a_ref[...], w_ref[0])

    return _call(
        body, name=name, grid=(nj, nb),
        in_specs=[pl.BlockSpec((m, k), lambda j, b: (0, 0)),
                  pl.BlockSpec((1, k, CW), lambda j, b: (j, 0, b))],
        out_specs=[pl.BlockSpec((m, CW), lambda j, b: (0, j * nb + b))],
        out_shape=[jax.ShapeDtypeStruct((m, nj * ns), F32)],
        operands=(a, w4), ride=ride)


def mix_conv_fwd(proj, ws, ride=None):
    s = proj.shape[0]
    nblk = D_MODEL // CW

    def body(cb_ref, cc_ref, cx_ref, ws_ref, q_ref, ya_ref):
        q = _causal_conv(cc_ref[...] * cx_ref[...], ws_ref)
        q_ref[...] = q
        ya_ref[...] = (cb_ref[...] * q).astype(BF16)

    seg = lambda k: pl.BlockSpec((s, CW), lambda c, k=k: (0, k * nblk + c))
    return _call(
        body, name="mix_conv_fwd", grid=(nblk,),
        in_specs=[seg(0), seg(1), seg(2), pl.BlockSpec((3, CW), lambda c: (0, c))],
        out_specs=[pl.BlockSpec((s, CW), lambda c: (0, c))] * 2,
        out_shape=[jax.ShapeDtypeStruct((s, D_MODEL), F32), jax.ShapeDtypeStruct((s, D_MODEL), BF16)],
        operands=(proj, proj, proj, ws), ride=ride)


def _lru_gates(r, ls):
    log_a = LRU_C * r * ls
    a = jnp.exp(log_a)
    mult = jnp.sqrt(_neg_expm1(2.0 * log_a))
    mult = jnp.where(_rows(r.shape) == 0, 1.0, mult)
    return a, mult


def mix_lru_fwd(proj, wl, bl, wa, ba, wx, bx, lam, ride=None):
    s = proj.shape[0]
    nblk = D_MODEL // CW

    def body(lx_ref, ly_ref, wl_ref, bl_ref, wa_ref, ba_ref, wx_ref, bx_ref, lam_ref,
             xl_ref, r_ref, i_ref, h_ref, yb_ref, a_scr, u_scr):
        xl = _causal_conv(lx_ref[...], wl_ref, bl_ref[...])
        xlb = xl.astype(BF16)
        xl_ref[...] = xlb
        r = _sigmoid(_dot(xlb, wa_ref[0]) + ba_ref[...])
        i = _sigmoid(_dot(xlb, wx_ref[0]) + bx_ref[...])
        r_ref[...] = r.astype(BF16)
        i_ref[...] = i.astype(BF16)
        a, mult = _lru_gates(r, _log_sigmoid(lam_ref[...]))
        a_scr[...] = a
        u_scr[...] = mult * i * xl
        _scan_forward(a_scr, u_scr, h_ref)
        yb_ref[...] = (h_ref[...] * _gelu(ly_ref[...])).astype(BF16)

    blk = lambda k: pl.BlockSpec((s, CW), lambda c, k=k: (0, k * nblk + c))
    vec = pl.BlockSpec((1, CW), lambda c: (0, c))
    mat = pl.BlockSpec((1, CW, CW), lambda c: (c, 0, 0))
    out = pl.BlockSpec((s, CW), lambda c: (0, c))
    f = jax.ShapeDtypeStruct((s, D_MODEL), F32)
    hb = jax.ShapeDtypeStruct((s, D_MODEL), BF16)
    return _call(
        body, name="mix_lru_fwd", grid=(nblk,),
        in_specs=[blk(3), blk(4), pl.BlockSpec((4, CW), lambda c: (0, c)), vec, mat, vec, mat, vec, vec],
        out_specs=[out] * 5,
        out_shape=[hb, hb, hb, f, hb],
        scratch_shapes=[pltpu.VMEM((s, CW), F32), pltpu.VMEM((s, CW), F32)],
        operands=(proj, proj, wl, bl, wa, ba, wx, bx, lam), ride=ride)


def branch_merge_fwd(ya, yb, wcb, wlb, proj, ride=None):
    s = ya.shape[0]
    nblk = D_MODEL // CW

    def body(ya_ref, yb_ref, wcb_ref, wlb_ref, gc_ref, gl_ref, a_ref, b_ref, m_ref):
        a = _dot(ya_ref[...], wcb_ref[...])
        b = _dot(yb_ref[...], wlb_ref[...])
        a_ref[...] = a
        b_ref[...] = b
        m_ref[...] = (_sigmoid(gc_ref[...]) * a + _sigmoid(gl_ref[...]) * b).astype(BF16)

    res = pl.BlockSpec((s, D_MODEL), lambda n: (0, 0))
    wcol = pl.BlockSpec((D_MODEL, CW), lambda n: (0, n))
    blk = lambda k: pl.BlockSpec((s, CW), lambda n, k=k: (0, k * nblk + n))
    out = pl.BlockSpec((s, CW), lambda n: (0, n))
    f = jax.ShapeDtypeStruct((s, D_MODEL), F32)
    return _call(
        body, name="branch_merge_fwd", grid=(nblk,),
        in_specs=[res, res, wcol, wcol, blk(5), blk(6)],
        out_specs=[out] * 3,
        out_shape=[f, f, jax.ShapeDtypeStruct((s, D_MODEL), BF16)],
        operands=(ya, yb, wcb, wlb, proj, proj), ride=ride)


def mix_out_fwd(merged, wout, x, g2, g3, ride=None):
    s, d = x.shape
    t = _token_tile(s)

    def body(m_ref, w_ref, x_ref, g2_ref, g3_ref, mix_ref, x2_ref, h2_ref, h2t_ref):
        mix = _dot(m_ref[...], w_ref[...])
        mix_ref[...] = mix
        n, _ = _rms_stats(mix)
        x2 = x_ref[...] + n * g2_ref[...]
        x2_ref[...] = x2
        n2, _ = _rms_stats(x2)
        h2 = n2 * g3_ref[...]
        h2_ref[...] = h2.astype(BF16)
        h2t_ref[...] = h2.T.astype(BF16)

    tile = pl.BlockSpec((t, d), lambda i: (i, 0))
    vec = pl.BlockSpec((1, d), lambda i: (0, 0))
    f = jax.ShapeDtypeStruct((s, d), F32)
    return _call(
        body, name="mix_out_fwd", grid=(s // t,),
        in_specs=[tile, pl.BlockSpec((d, d), lambda i: (0, 0)), tile, vec, vec],
        out_specs=[tile] * 3 + [pl.BlockSpec((d, t), lambda i: (0, i))],
        out_shape=[f, f, jax.ShapeDtypeStruct((s, d), BF16), jax.ShapeDtypeStruct((d, s), BF16)],
        operands=(merged, wout, x, g2, g3), ride=ride)


def ffn_up_act_fwd(h2, wup4, fw, fb, ride=None):
    s, k = h2.shape
    ns = wup4.shape[2]
    per_chip = ns // CW
    nblk = D_FF // CW

    def body(h_ref, wg_ref, wv_ref, cg_ref, cv_ref, bg_ref, bv_ref, up_ref, act_ref, f_ref):
        h = h_ref[...]
        ug = _dot(h, wg_ref[0])
        uv = _dot(h, wv_ref[0])
        up_ref[0] = ug
        up_ref[1] = uv
        gate = _causal_conv(ug, cg_ref, bg_ref[...])
        val = _causal_conv(uv, cv_ref, bv_ref[...])
        act_ref[0] = gate.astype(BF16)
        act_ref[1] = val.astype(BF16)
        f_ref[...] = (_gelu(gate) * val).astype(BF16)

    wcols = lambda h: pl.BlockSpec((1, k, CW), lambda n, h=h: (n // per_chip + 2 * h, 0, n % per_chip))
    half = lambda h, rows: pl.BlockSpec((rows, CW), lambda n, h=h: (0, h * nblk + n))
    both = pl.BlockSpec((2, s, CW), lambda n: (0, 0, n))
    return _call(
        body, name="ffn_up_act_fwd", grid=(nblk,),
        in_specs=[pl.BlockSpec((s, k), lambda n: (0, 0)), wcols(0), wcols(1),
                  half(0, 3), half(1, 3), half(0, 1), half(1, 1)],
        out_specs=[both, both, pl.BlockSpec((s, CW), lambda n: (0, n))],
        out_shape=[jax.ShapeDtypeStruct((2, s, D_FF), F32), jax.ShapeDtypeStruct((2, s, D_FF), BF16),
                   jax.ShapeDtypeStruct((s, D_FF), BF16)],
        operands=(h2, wup4, wup4, fw, fw, fb, fb), ride=ride)


def ffn_down_loss(f, wdown, x2, target, g4):
    s, d = x2.shape
    t = _token_tile(s)

    def body(f_ref, w_ref, x2_ref, tg_ref, g4_ref, dy_ref, dout_ref, loss_ref, dg4_ref):
        @pl.when(pl.program_id(0) == 0)
        def _():
            loss_ref[...] = jnp.zeros_like(loss_ref)
            dg4_ref[...] = jnp.zeros_like(dg4_ref)

        out = _dot(f_ref[...], w_ref[...])
        n, r = _rms_stats(out)
        err = x2_ref[...] + n * g4_ref[...] - tg_ref[...]
        loss_ref[...] += jnp.full(loss_ref.shape, (0.5 / d) * jnp.sum(err * err), F32)
        dy = err * (1.0 / d)
        dy_ref[...] = dy
        dout, dg = _rms_bwd(n, r, g4_ref[...], dy)
        dout_ref[...] = dout.astype(BF16)
        dg4_ref[...] += jnp.sum(dg, axis=0, keepdims=True)

    tile = pl.BlockSpec((t, d), lambda i: (i, 0))
    vec = pl.BlockSpec((1, d), lambda i: (0, 0))
    return pl.pallas_call(
        body, name="ffn_down_loss", grid=(s // t,),
        in_specs=[pl.BlockSpec((t, D_FF), lambda i: (i, 0)), pl.BlockSpec((D_FF, d), lambda i: (0, 0)), tile, tile, vec],
        out_specs=[tile, tile, pl.BlockSpec((1, 128), lambda i: (0, 0)), vec],
        out_shape=[jax.ShapeDtypeStruct((s, d), F32), jax.ShapeDtypeStruct((s, d), BF16),
                   jax.ShapeDtypeStruct((1, 128), F32), jax.ShapeDtypeStruct((1, d), F32)],
        compiler_params=_params(),
    )(f, wdown, x2, target, g4)


def ffn_bwd(dout, wdown, up, act, f, fw):
    s = up.shape[1]
    nblk = D_FF // FW

    def body(do_ref, wd_ref, up_ref, act_ref, f_ref, wg_ref, wv_ref, dup_ref, dwd_ref, dw_ref, db_ref):
        do = do_ref[...]
        df = _dot_nt(do, wd_ref[...])
        dwd_ref[...] = _dot_tn(f_ref[...], do).astype(BF16)
        val = act_ref[1].astype(F32)
        ge, dge = _gelu_and_grad(act_ref[0].astype(F32))
        dgate = _advances(df * val * dge, 3)
        dval = _advances(df * ge, 3)
        dup_ref[0] = _taps_sum(dgate, wg_ref).astype(BF16)
        dup_ref[1] = _taps_sum(dval, wv_ref).astype(BF16)
        dw_ref[0] = _conv_wgrad(dgate, up_ref[0])
        dw_ref[1] = _conv_wgrad(dval, up_ref[1])
        db_ref[0] = jnp.sum(dgate[0], axis=0, keepdims=True)
        db_ref[1] = jnp.sum(dval[0], axis=0, keepdims=True)

    half = lambda h, rows: pl.BlockSpec((rows, FW), lambda n, h=h: (0, h * nblk + n))
    both = lambda rows: pl.BlockSpec((2, rows, FW), lambda n: (0, 0, n))
    return pl.pallas_call(
        body, name="ffn_bwd", grid=(nblk,),
        in_specs=[pl.BlockSpec((s, D_MODEL), lambda n: (0, 0)), pl.BlockSpec((FW, D_MODEL), lambda n: (n, 0)),
                  both(s), both(s), pl.BlockSpec((s, FW), lambda n: (0, n)), half(0, 3), half(1, 3)],
        out_specs=[both(s), pl.BlockSpec((FW, D_MODEL), lambda n: (n, 0)), both(3), both(1)],
        out_shape=[jax.ShapeDtypeStruct((2, s, D_FF), BF16), jax.ShapeDtypeStruct((D_FF, D_MODEL), BF16),
                   jax.ShapeDtypeStruct((2, 3, D_FF), F32), jax.ShapeDtypeStruct((2, 1, D_FF), F32)],
        compiler_params=_params(),
    )(dout, wdown, up, act, f, fw, fw)


def ffn_up_bwd(dout, wdown, up, act, f, fw, wup4, h2t, ride=None):
    k, s = h2t.shape
    nblk = D_FF // FW
    per_chip = wup4.shape[2] // FW

    def body(do_ref, wd_ref, up_ref, act_ref, f_ref, cg_ref, cv_ref, wg_ref, wv_ref, h_ref,
             dh_ref, dwu_ref, dwd_ref, dw_ref, db_ref, dup_scr):
        @pl.when(pl.program_id(0) == 0)
        def _():
            dup_scr[...] = jnp.zeros_like(dup_scr)
            dh_ref[...] = jnp.zeros_like(dh_ref)

        do = do_ref[...]
        df = _dot_nt(do, wd_ref[...])
        dg = dup_scr[0]
        dv = dup_scr[1]
        ht = h_ref[...]
        dh_ref[...] += _dot_nt(dg, wg_ref[0]) + _dot_nt(dv, wv_ref[0])
        dwu_ref[0] = _dot(ht, dg).astype(BF16)
        dwu_ref[1] = _dot(ht, dv).astype(BF16)
        dwd_ref[...] = _dot_tn(f_ref[...], do).astype(BF16)
        val = act_ref[1].astype(F32)
        ge, dge = _gelu_and_grad(act_ref[0].astype(F32))
        dgate = _advances(df * val * dge, 3)
        dval = _advances(df * ge, 3)
        dw_ref[0] = _conv_wgrad(dgate, up_ref[0])
        dw_ref[1] = _conv_wgrad(dval, up_ref[1])
        db_ref[0] = jnp.sum(dgate[0], axis=0, keepdims=True)
        db_ref[1] = jnp.sum(dval[0], axis=0, keepdims=True)
        dup_scr[0] = _taps_sum(dgate, cg_ref).astype(BF16)
        dup_scr[1] = _taps_sum(dval, cv_ref).astype(BF16)

    cur = lambda n: jnp.minimum(n, nblk - 1)
    prev = lambda n: jnp.maximum(n - 1, 0)
    once = pl.Buffered(1)
    both = lambda rows: pl.BlockSpec((2, rows, FW), lambda n: (0, 0, cur(n)))
    taps = lambda h: pl.BlockSpec((3, FW), lambda n, h=h: (0, h * nblk + cur(n)))
    wcols = lambda h: pl.BlockSpec((1, k, FW), lambda n, h=h: (prev(n) // per_chip + 2 * h, 0, prev(n) % per_chip))
    return _call(
        body, name="ffn_up_bwd", grid=(nblk + 1,),
        in_specs=[pl.BlockSpec((s, D_MODEL), lambda n: (0, 0), pipeline_mode=once),
                  pl.BlockSpec((FW, D_MODEL), lambda n: (cur(n), 0)), both(s), both(s),
                  pl.BlockSpec((s, FW), lambda n: (0, cur(n))), taps(0), taps(1), wcols(0), wcols(1),
                  pl.BlockSpec((k, s), lambda n: (0, 0), pipeline_mode=once)],
        out_specs=[pl.BlockSpec((s, k), lambda n: (0, 0), pipeline_mode=once),
                   pl.BlockSpec((2, k, FW), lambda n: (0, 0, prev(n))),
                   pl.BlockSpec((FW, D_MODEL), lambda n: (cur(n), 0)), both(3), both(1)],
        out_shape=[jax.ShapeDtypeStruct((s, k), F32), jax.ShapeDtypeStruct((2, k, D_FF), BF16),
                   jax.ShapeDtypeStruct((D_FF, D_MODEL), BF16),
                   jax.ShapeDtypeStruct((2, 3, D_FF), F32), jax.ShapeDtypeStruct((2, 1, D_FF), F32)],
        scratch_shapes=[pltpu.VMEM((2, s, FW), BF16)],
        operands=(dout, wdown, up, act, f, fw, fw, wup4, wup4, h2t), ride=ride)


def matmul_cols_bwd(dy, other, name, wgrad, ride=None):
    m = dy[0].shape[1]
    if wgrad:
        k = other.shape[0]
        nj, nb = N_CHIPS, sum(d.shape[0] * d.shape[2] for d in dy) // (N_CHIPS * CW)
    else:
        nj, k, ns = other.shape
        nb = ns // CW
    per_seg = dy[0].shape[2] // CW
    first = [sum(d.shape[0] for d in dy[:i]) for i in range(len(dy))]

    def segment(j, b):
        return (j * nb + b) // per_seg, (j * nb + b) % per_seg

    def body(*refs):
        dy_refs, (o_ref, r_ref) = refs[:len(dy)], refs[len(dy):]
        seg, _ = segment(pl.program_id(0), pl.program_id(1))
        dyb = dy_refs[-1][0]
        for i in range(len(dy) - 2, -1, -1):
            dyb = jnp.where(seg < first[i + 1], dy_refs[i][0], dyb)
        if wgrad:
            r_ref[...] = _dot(o_ref[...], dyb).astype(BF16)
        else:
            @pl.when((pl.program_id(0) == 0) & (pl.program_id(1) == 0))
            def _():
                r_ref[...] = jnp.zeros_like(r_ref)

            r_ref[...] += _dot_nt(dyb, o_ref[0])

    def dy_spec(i):
        nseg = dy[i].shape[0]

        def index(j, b):
            seg, col = segment(j, b)
            local = seg - first[i]
            return (jnp.clip(local, 0, nseg - 1), 0,
                    jnp.where(local < 0, 0, jnp.where(local >= nseg, per_seg - 1, col)))

        return pl.BlockSpec((1, m, CW), index)

    if wgrad:
        other_spec = pl.BlockSpec((k, m), lambda j, b: (0, 0))
        out_spec = pl.BlockSpec((k, CW), lambda j, b: (0, j * nb + b))
        out_shape = jax.ShapeDtypeStruct((k, nj * nb * CW), BF16)
    else:
        other_spec = pl.BlockSpec((1, k, CW), lambda j, b: (j, 0, b))
        out_spec = pl.BlockSpec((m, k), lambda j, b: (0, 0))
        out_shape = jax.ShapeDtypeStruct((m, k), F32)
    return _call(
        body, name=name, grid=(nj, nb), in_specs=[dy_spec(i) for i in range(len(dy))] + [other_spec],
        out_specs=[out_spec], out_shape=[out_shape], operands=(*dy, other), ride=ride)


def norms_mid_bwd(dh2, x2, dy, mix, g3, g2, ride=None):
    s, d = x2.shape
    t = _token_tile(s)

    def body(dh2_ref, x2_ref, dy_ref, mix_ref, g3_ref, g2_ref, dx2_ref, dmix_ref, dg3_ref, dg2_ref):
        @pl.when(pl.program_id(0) == 0)
        def _():
            dg3_ref[...] = jnp.zeros_like(dg3_ref)
            dg2_ref[...] = jnp.zeros_like(dg2_ref)

        n3, r3 = _rms_stats(x2_ref[...])
        dx, dg3 = _rms_bwd(n3, r3, g3_ref[...], dh2_ref[...])
        dx2 = dy_ref[...] + dx
        dx2_ref[...] = dx2
        dg3_ref[...] += jnp.sum(dg3, axis=0, keepdims=True)
        n2, r2 = _rms_stats(mix_ref[...])
        dmix, dg2 = _rms_bwd(n2, r2, g2_ref[...], dx2)
        dmix_ref[...] = dmix.astype(BF16)
        dg2_ref[...] += jnp.sum(dg2, axis=0, keepdims=True)

    tile = pl.BlockSpec((t, d), lambda i: (i, 0))
    vec = pl.BlockSpec((1, d), lambda i: (0, 0))
    v = jax.ShapeDtypeStruct((1, d), F32)
    return _call(
        body, name="norms_mid_bwd", grid=(s // t,),
        in_specs=[tile, tile, tile, tile, vec, vec],
        out_specs=[tile, tile, vec, vec],
        out_shape=[jax.ShapeDtypeStruct((s, d), F32), jax.ShapeDtypeStruct((s, d), BF16), v, v],
        operands=(dh2, x2, dy, mix, g3, g2), ride=ride)


def mix_out_bwd(dmix, wout, merged, a, b, proj, ride=None):
    s = dmix.shape[0]
    nblk = D_MODEL // CW

    def body(dm_ref, w_ref, mg_ref, a_ref, b_ref, gc_ref, gl_ref, da_ref, db_ref, dw_ref, dg_ref):
        dm = dm_ref[...]
        dmerged = _dot_nt(dm, w_ref[...])
        dw_ref[...] = _dot_tn(mg_ref[...], dm).astype(BF16)
        sc = _sigmoid(gc_ref[...])
        sl = _sigmoid(gl_ref[...])
        da_ref[...] = (dmerged * sc).astype(BF16)
        db_ref[...] = (dmerged * sl).astype(BF16)
        dg_ref[0] = (dmerged * a_ref[...] * sc * (1.0 - sc)).astype(BF16)
        dg_ref[1] = (dmerged * b_ref[...] * sl * (1.0 - sl)).astype(BF16)

    res = pl.BlockSpec((s, D_MODEL), lambda n: (0, 0))
    rows = pl.BlockSpec((CW, D_MODEL), lambda n: (n, 0))
    col = pl.BlockSpec((s, CW), lambda n: (0, n))
    blk = lambda k: pl.BlockSpec((s, CW), lambda n, k=k: (0, k * nblk + n))
    hb = jax.ShapeDtypeStruct((s, D_MODEL), BF16)
    return _call(
        body, name="mix_out_bwd", grid=(nblk,),
        in_specs=[res, rows, col, col, col, blk(5), blk(6)],
        out_specs=[col, col, rows, pl.BlockSpec((2, s, CW), lambda n: (0, 0, n))],
        out_shape=[hb, hb, jax.ShapeDtypeStruct((D_MODEL, D_MODEL), BF16), jax.ShapeDtypeStruct((2, s, D_MODEL), BF16)],
        operands=(dmix, wout, merged, a, b, proj, proj), ride=ride)


def mix_conv_bwd(da, wcb, proj, q, ws, ride=None):
    s = da.shape[0]
    nblk = D_MODEL // CW

    def body(da_ref, w_ref, cb_ref, cc_ref, cx_ref, q_ref, ws_ref, dc_ref, dw_ref, dws_ref):
        dab = da_ref[...]
        dya = _dot_nt(dab, w_ref[...])
        cb = cb_ref[...]
        cc = cc_ref[...]
        cx = cx_ref[...]
        q = q_ref[...]
        dw_ref[...] = _dot_tn((cb * q).astype(BF16), dab).astype(BF16)
        dc_ref[0] = (dya * q).astype(BF16)
        dq = _advances(dya * cb, 3)
        dp = _taps_sum(dq, ws_ref)
        dws_ref[...] = _conv_wgrad(dq, cc * cx)
        dc_ref[1] = (dp * cx).astype(BF16)
        dc_ref[2] = (dp * cc).astype(BF16)

    res = pl.BlockSpec((s, D_MODEL), lambda n: (0, 0))
    rows = pl.BlockSpec((CW, D_MODEL), lambda n: (n, 0))
    col = pl.BlockSpec((s, CW), lambda n: (0, n))
    blk = lambda k: pl.BlockSpec((s, CW), lambda n, k=k: (0, k * nblk + n))
    taps = pl.BlockSpec((3, CW), lambda n: (0, n))
    hb = jax.ShapeDtypeStruct((s, D_MODEL), BF16)
    return _call(
        body, name="mix_conv_bwd", grid=(nblk,),
        in_specs=[res, rows, blk(0), blk(1), blk(2), col, taps],
        out_specs=[pl.BlockSpec((3, s, CW), lambda n: (0, 0, n)), rows, taps],
        out_shape=[jax.ShapeDtypeStruct((3, s, D_MODEL), BF16), jax.ShapeDtypeStruct((D_MODEL, D_MODEL), BF16),
                   jax.ShapeDtypeStruct((3, D_MODEL), F32)],
        operands=(da, wcb, proj, proj, proj, q, ws), ride=ride)


def mix_lru_bwd(db, wlb, proj, xl, r, i, h, wl, wa, wx, lam, ride=None):
    s = db.shape[0]
    nblk = D_MODEL // CW

    def body(db_ref, w_ref, lx_ref, ly_ref, xl_ref, r_ref, i_ref, h_ref, wl_ref, wa_ref, wx_ref, lam_ref,
             dl_ref, dw_ref, dwa_ref, dwx_ref, dba_ref, dbx_ref, dwl_ref, dbl_ref, dlam_ref,
             c_scr, g_scr):
        dbb = db_ref[...]
        dyb = _dot_nt(dbb, w_ref[...])
        h = h_ref[...]
        ge, dge = _gelu_and_grad(ly_ref[...])
        dw_ref[...] = _dot_tn((h * ge).astype(BF16), dbb).astype(BF16)
        dl_ref[1] = (dyb * h * dge).astype(BF16)
        r = r_ref[...].astype(F32)
        gi = i_ref[...].astype(F32)
        xlb = xl_ref[...]
        xl = xlb.astype(F32)
        lam = lam_ref[...]
        ls = _log_sigmoid(lam)
        a, mult = _lru_gates(r, ls)
        c_scr[...] = _shift_up(a, 1)
        g_scr[...] = dyb * ge
        _scan_backward(c_scr, g_scr, g_scr)
        du = g_scr[...]
        da = du * _shift_down(h, 1)
        dmult = du * gi * xl
        di = du * mult * xl
        dxl = du * mult * gi
        first = _rows(a.shape) == 0
        dlog_a = da * a - jnp.where(first, 0.0, dmult * a * a / mult)
        dr = dlog_a * (LRU_C * ls)
        dlam_ref[...] = jnp.sum(dlog_a * r, axis=0, keepdims=True) * (LRU_C * (1.0 - _sigmoid(lam)))
        dzr = dr * r * (1.0 - r)
        dzi = di * gi * (1.0 - gi)
        dba_ref[...] = jnp.sum(dzr, axis=0, keepdims=True)
        dbx_ref[...] = jnp.sum(dzi, axis=0, keepdims=True)
        dzrb = dzr.astype(BF16)
        dzib = dzi.astype(BF16)
        dwa_ref[0] = _dot_tn(xlb, dzrb)
        dwx_ref[0] = _dot_tn(xlb, dzib)
        dxl = _advances(dxl + _dot_nt(dzrb, wa_ref[0]) + _dot_nt(dzib, wx_ref[0]), 4)
        dl_ref[0] = _taps_sum(dxl, wl_ref).astype(BF16)
        dwl_ref[...] = _conv_wgrad(dxl, lx_ref[...])
        dbl_ref[...] = jnp.sum(dxl[0], axis=0, keepdims=True)

    res = pl.BlockSpec((s, D_MODEL), lambda n: (0, 0), pipeline_mode=pl.Buffered(1))
    rows = pl.BlockSpec((CW, D_MODEL), lambda n: (n, 0))
    col = pl.BlockSpec((s, CW), lambda n: (0, n))
    blk = lambda k: pl.BlockSpec((s, CW), lambda n, k=k: (0, k * nblk + n))
    taps = pl.BlockSpec((4, CW), lambda n: (0, n))
    vec = pl.BlockSpec((1, CW), lambda n: (0, n))
    mat = pl.BlockSpec((1, CW, CW), lambda n: (n, 0, 0))
    hb = jax.ShapeDtypeStruct((s, D_MODEL), BF16)
    v = jax.ShapeDtypeStruct((1, D_MODEL), F32)
    m = jax.ShapeDtypeStruct((LRU_HEADS, HEAD_DIM, HEAD_DIM), F32)
    scr = pltpu.VMEM((s, CW), F32)
    return _call(
        body, name="mix_lru_bwd", grid=(nblk,),
        in_specs=[res, rows, blk(3), blk(4), col, col, col, col, taps, mat, mat, vec],
        out_specs=[pl.BlockSpec((2, s, CW), lambda n: (0, 0, n)), rows, mat, mat, vec, vec, taps, vec, vec],
        out_shape=[jax.ShapeDtypeStruct((2, s, D_MODEL), BF16), jax.ShapeDtypeStruct((D_MODEL, D_MODEL), BF16), m, m, v, v,
                   jax.ShapeDtypeStruct((4, D_MODEL), F32), v, v],
        scratch_shapes=[scr, scr],
        operands=(db, wlb, proj, proj, xl, r, i, h, wl, wa, wx, lam), ride=ride)


def norm_in_bwd(dh1, x, dx2, g1, ride=None):
    s, d = x.shape
    t = _token_tile(s)

    def body(dh_ref, x_ref, dx2_ref, g_ref, dx_ref, dg_ref):
        @pl.when(pl.program_id(0) == 0)
        def _():
            dg_ref[...] = jnp.zeros_like(dg_ref)

        n, r = _rms_stats(x_ref[...])
        dx, dg = _rms_bwd(n, r, g_ref[...], dh_ref[...])
        dx_ref[...] = dx2_ref[...] + dx
        dg_ref[...] += jnp.sum(dg, axis=0, keepdims=True)

    tile = pl.BlockSpec((t, d), lambda i: (i, 0))
    vec = pl.BlockSpec((1, d), lambda i: (0, 0))
    return _call(
        body, name="norm_in_bwd", grid=(s // t,),
        in_specs=[tile, tile, tile, vec],
        out_specs=[tile, vec],
        out_shape=[jax.ShapeDtypeStruct((s, d), F32), jax.ShapeDtypeStruct((1, d), F32)],
        operands=(dh1, x, dx2, g1), ride=ride)


def local_step(x, target, g1, g2, g3, g4, win4, ws, wcb, wl, bl, wa, ba, wx, bx, lam, wlb, wout, wup4, fw, fb, wdown):
    h1 = norm_in(x, g1)
    proj = matmul_cols(h1, win4, "proj_fwd")
    q, ya = mix_conv_fwd(proj, ws)
    xl, r, gi, h, yb = mix_lru_fwd(proj, wl, bl, wa, ba, wx, bx, lam)
    a, b, merged = branch_merge_fwd(ya, yb, wcb, wlb, proj)
    mix, x2, h2 = mix_out_fwd(merged, wout, x, g2, g3)
    up = matmul_cols(h2, wup4, "up_fwd")
    f = ffn_act_fwd(up, fw, fb)
    dy, dout, loss, dg4 = ffn_down_loss(f, wdown, x2, target, g4)

    dug, duv, dwdown, dfw_g, dfw_v, dfb_g, dfb_v = ffn_bwd(dout, wdown, up, fw, fb)
    dup = jnp.concatenate([dug, duv], axis=1)
    dfw = jnp.concatenate([dfw_g, dfw_v], axis=1)
    dfb = jnp.concatenate([dfb_g, dfb_v], axis=1)
    dh2, dwup = dgrad_wgrad_cols(dup, wup4, h2, "up_bwd")
    dx2, dmix, dg3, dg2 = norms_mid_bwd(dh2, x2, dy, mix, g3, g2)
    da, db, dwout, dgc, dgl = mix_out_bwd(dmix, wout, merged, a, b, proj)
    dcb, dcc, dcx, dwcb, dws = mix_conv_bwd(da, wcb, proj, q, ws)
    dlx, dly, dwlb, dwa, dwx, dba, dbx, dwl, dbl, dlam = mix_lru_bwd(db, wlb, proj, xl, r, gi, h, wl, wa, wx, lam)
    dproj = jnp.concatenate([dcb, dcc, dcx, dlx, dly, dgc[:, 5 * D_MODEL:6 * D_MODEL], dgl[:, 6 * D_MODEL:]], axis=1)
    dh1, dwin = dgrad_wgrad_cols(dproj, win4, h1, "proj_bwd")
    dx, dg1 = norm_in_bwd(dh1, x, dx2, g1)
    grads = dict(norm_mix_pre=dg1, norm_mix_post=dg2, norm_ffn_pre=dg3, norm_ffn_post=dg4,
                 w_in=dwin, conv_short_w=dws, w_conv_branch=dwcb, lru_conv_w=dwl, lru_conv_b=dbl,
                 lru_wa=dwa, lru_ba=dba, lru_wx=dwx, lru_bx=dbx, lru_lambda=dlam,
                 w_lru_branch=dwlb, w_out=dwout, ffn_w_up=dwup, ffn_conv_w=dfw, ffn_conv_b=dfb,
                 ffn_w_down=dwdown)
    return loss[0, 0], dx, grads


MESH = pl.DeviceIdType.MESH
_HBM = pl.BlockSpec(memory_space=pltpu.HBM)
_OTHER_CHIPS = ((1, 0), (0, 1), (1, 1))
_OTHER_DEVICES = tuple((dx, dy, dc) for dx in (0, 1) for dy in (0, 1) for dc in (0, 1) if dx or dy or dc)
N_DEVICES = 8


def _position():
    return lax.axis_index("x"), lax.axis_index("y"), lax.axis_index("c")


def _flip(v, d):
    return 1 - v if d else v


def _half_rows(ref, h, hr):
    return ref.at[pl.ds(h * hr, hr), :]


def gather_chips(shards):
    n = len(shards)
    nrel = len(_OTHER_CHIPS)

    def body(*refs):
        ins, outs = refs[:n], refs[n:2 * n]
        ici_send, ici_recv, sib_send, sib_recv = refs[2 * n:]
        x, y, c = _position()
        j = 2 * x + y
        hr = [s.shape[0] // 2 for s in shards]

        def chip(p):
            px, py = _flip(x, _OTHER_CHIPS[p][0]), _flip(y, _OTHER_CHIPS[p][1])
            return px, py, 2 * px + py

        def ici(a, p, slot):
            px, py, _ = chip(p)
            return pltpu.make_async_remote_copy(
                src_ref=_half_rows(ins[a], c, hr[a]), dst_ref=_half_rows(outs[a].at[slot], c, hr[a]),
                send_sem=ici_send.at[a * nrel + p], recv_sem=ici_recv.at[a * nrel + p],
                device_id=(px, py, c), device_id_type=MESH)

        def sib(a, p, h):
            _, _, k = chip(p)
            part = _half_rows(outs[a].at[k], h, hr[a])
            return pltpu.make_async_remote_copy(
                src_ref=part, dst_ref=part, send_sem=sib_send.at[a * nrel + p], recv_sem=sib_recv.at[a * nrel + p],
                device_id=(x, y, 1 - c), device_id_type=MESH)

        pairs = [(a, p) for a in range(n) for p in range(nrel)]
        for a, p in pairs:
            ici(a, p, j).start()
        for a, p in pairs:
            ici(a, p, chip(p)[2]).wait_recv()
            sib(a, p, c).start()
        for a, p in pairs:
            sib(a, p, 1 - c).wait_recv()
        for a, p in pairs:
            ici(a, p, j).wait_send()
            sib(a, p, c).wait_send()

    got = pl.pallas_call(
        body, name="gather_chips",
        in_specs=[_HBM] * n, out_specs=[_HBM] * n,
        out_shape=[jax.ShapeDtypeStruct((N_CHIPS,) + s.shape, s.dtype) for s in shards],
        scratch_shapes=[pltpu.SemaphoreType.DMA((n * nrel,))] * 4,
    )(*shards)
    j = 2 * lax.axis_index("x") + lax.axis_index("y")
    return [lax.dynamic_update_slice(g, s[None], (j, 0, 0)) for g, s in zip(got, shards)]


def _owned_part(ref, kind, k, h, hr):
    if kind == "col":
        ns = ref.shape[1] // N_CHIPS
        return ref.at[pl.ds(h * hr, hr), pl.ds(k * ns, ns)]
    if kind == "row":
        return ref.at[pl.ds(k * 2 * hr + h * hr, hr), :]
    if kind == "col2":
        ns = ref.shape[2] // 2
        return ref.at[k // 2, pl.ds(h * hr, hr), pl.ds((k % 2) * ns, ns)]
    return ref.at[k, pl.ds(h * hr, hr), :]


def _part_shape(g, kind):
    if kind == "col2":
        return g.shape[1] // 2, g.shape[2] // 2
    if kind == "col":
        return g.shape[0] // 2, g.shape[1] // N_CHIPS
    if kind == "row":
        return g.shape[0] // (2 * N_CHIPS), g.shape[1]
    return g.shape[1] // 2, g.shape[2]


def pair_split(grads, kinds, name):
    n = len(grads)
    shapes = [_part_shape(g, k) for g, k in zip(grads, kinds)]

    def body(*refs):
        ins, theirs = refs[:n], refs[n:2 * n]
        send_sem, recv_sem = refs[2 * n:]
        x, y, c = _position()
        copies = []
        for a in range(n):
            hr = shapes[a][0]
            for k in range(N_CHIPS):
                s = a * N_CHIPS + k
                copies.append(pltpu.make_async_remote_copy(
                    src_ref=_owned_part(ins[a], kinds[a], k, 1 - c, hr), dst_ref=theirs[a].at[k],
                    send_sem=send_sem.at[s], recv_sem=recv_sem.at[s], device_id=(x, y, 1 - c), device_id_type=MESH))
        for cp in copies:
            cp.start()
        for cp in copies:
            cp.wait()

    return pl.pallas_call(
        body, name=name,
        in_specs=[_HBM] * n, out_specs=[_HBM] * n,
        out_shape=[jax.ShapeDtypeStruct((N_CHIPS,) + shp, g.dtype) for shp, g in zip(shapes, grads)],
        scratch_shapes=[pltpu.SemaphoreType.DMA((n * N_CHIPS,))] * 2,
    )(*grads)


def chip_exchange(sums, rep):
    n = len(sums)
    nrel = len(_OTHER_CHIPS)
    ndev = len(_OTHER_DEVICES)

    def body(*refs):
        ins, rep_ref = refs[:n], refs[n]
        outs, rep_out = refs[n + 1:2 * n + 1], refs[2 * n + 1]
        loc_sem, send_sem, recv_sem, rep_send, rep_recv = refs[2 * n + 2:]
        x, y, c = _position()
        j = 2 * x + y
        me = 4 * x + 2 * y + c

        def chip(p):
            px, py = _flip(x, _OTHER_CHIPS[p][0]), _flip(y, _OTHER_CHIPS[p][1])
            return px, py, 2 * px + py

        def part(a, p, src_slot, dst_slot):
            px, py, _ = chip(p)
            return pltpu.make_async_remote_copy(
                src_ref=ins[a].at[src_slot], dst_ref=outs[a].at[dst_slot],
                send_sem=send_sem.at[a * nrel + p], recv_sem=recv_sem.at[a * nrel + p],
                device_id=(px, py, c), device_id_type=MESH)

        def device(q):
            dx, dy, dc = _OTHER_DEVICES[q]
            return _flip(x, dx), _flip(y, dy), _flip(c, dc)

        def rep_copy(q, slot):
            return pltpu.make_async_remote_copy(
                src_ref=rep_ref, dst_ref=rep_out.at[slot], send_sem=rep_send.at[q], recv_sem=rep_recv.at[q],
                device_id=device(q), device_id_type=MESH)

        own = [pltpu.make_async_copy(ins[a].at[j], outs[a].at[j], loc_sem.at[a]) for a in range(n)]
        own.append(pltpu.make_async_copy(rep_ref, rep_out.at[me], loc_sem.at[n]))
        for cp in own:
            cp.start()
        pairs = [(a, p) for a in range(n) for p in range(nrel)]
        for a, p in pairs:
            part(a, p, chip(p)[2], j).start()
        for q in range(ndev):
            rep_copy(q, me).start()
        for a, p in pairs:
            part(a, p, chip(p)[2], chip(p)[2]).wait_recv()
        for q in range(ndev):
            px, py, pc = device(q)
            rep_copy(q, 4 * px + 2 * py + pc).wait_recv()
        for a, p in pairs:
            part(a, p, chip(p)[2], j).wait_send()
        for q in range(ndev):
            rep_copy(q, me).wait_send()
        for cp in own:
            cp.wait()

    return pl.pallas_call(
        body, name="chip_exchange",
        in_specs=[_HBM] * (n + 1), out_specs=[_HBM] * (n + 1),
        out_shape=[jax.ShapeDtypeStruct(s.shape, s.dtype) for s in sums]
        + [jax.ShapeDtypeStruct((N_DEVICES,) + rep.shape, rep.dtype)],
        scratch_shapes=[pltpu.SemaphoreType.DMA((n + 1,)), pltpu.SemaphoreType.DMA((n * nrel,)),
                        pltpu.SemaphoreType.DMA((n * nrel,)), pltpu.SemaphoreType.DMA((ndev,)),
                        pltpu.SemaphoreType.DMA((ndev,))],
    )(*sums, rep)


def pair_swap(halves):
    n = len(halves)

    def body(*refs):
        ins, outs = refs[:n], refs[n:2 * n]
        send_sem, recv_sem = refs[2 * n:]
        x, y, c = _position()
        copies = [pltpu.make_async_remote_copy(
            src_ref=ins[a], dst_ref=outs[a], send_sem=send_sem.at[a], recv_sem=recv_sem.at[a],
            device_id=(x, y, 1 - c), device_id_type=MESH) for a in range(n)]
        for cp in copies:
            cp.start()
        for cp in copies:
            cp.wait()

    return pl.pallas_call(
        body, name="pair_swap",
        in_specs=[_HBM] * n, out_specs=[_HBM] * n,
        out_shape=[jax.ShapeDtypeStruct(h.shape, h.dtype) for h in halves],
        scratch_shapes=[pltpu.SemaphoreType.DMA((n,))] * 2,
    )(*halves)


def _row_tile(rows, cols, limit_bytes=1 << 20):
    best = None
    for t in range(SUBLANES, rows + 1, SUBLANES):
        if rows % t == 0 and t * cols * 4 <= limit_bytes:
            best = t
    return best or rows


def add_pair(g, kind, theirs, core, name):
    nc, rows, cols = theirs.shape
    t = _row_tile(rows, cols, 4 << 20)
    nt = rows // t

    def body(core_ref, g_ref, b_ref, o_ref):
        mine = g_ref[...].reshape(t, cols)
        o_ref[0] = (mine.astype(F32) + b_ref[0].astype(F32)).astype(o_ref.dtype)

    if kind == "col":
        own = pl.BlockSpec((t, cols), lambda k, i, c: (c[0] * nt + i, k))
    elif kind == "col2":
        own = pl.BlockSpec((1, t, cols), lambda k, i, c: (k // 2, c[0] * nt + i, k % 2))
    elif kind == "row":
        own = pl.BlockSpec((t, cols), lambda k, i, c: ((2 * k + c[0]) * nt + i, 0))
    else:
        own = pl.BlockSpec((1, t, cols), lambda k, i, c: (k, c[0] * nt + i, 0))
    spec = pl.BlockSpec((1, t, cols), lambda k, i, c: (k, i, 0))
    return pl.pallas_call(
        body, name=name,
        grid_spec=pltpu.PrefetchScalarGridSpec(num_scalar_prefetch=1, grid=(nc, nt), in_specs=[own, spec], out_specs=spec),
        out_shape=jax.ShapeDtypeStruct(theirs.shape, theirs.dtype), compiler_params=_params(),
    )(core, g, theirs)


def sum_lead(a, name):
    nl, rows, cols = a.shape
    t = _row_tile(rows, cols, (1 << 20) // 2)

    def body(a_ref, o_ref):
        acc = a_ref[0].astype(F32)
        for s in range(1, nl):
            acc = acc + a_ref[s].astype(F32)
        o_ref[...] = acc

    return pl.pallas_call(
        body, name=name, grid=(rows // t,),
        in_specs=[pl.BlockSpec((nl, t, cols), lambda i: (0, i, 0))],
        out_specs=pl.BlockSpec((t, cols), lambda i: (i, 0)),
        out_shape=jax.ShapeDtypeStruct((rows, cols), F32), compiler_params=_params(),
    )(a)


def sum_chips(rx, csum, chip, name):
    nc, rows, cols = rx.shape
    t = _row_tile(rows, cols, 2 << 20)

    def body(chip_ref, r0, r1, r2, r3, own_ref, o_ref):
        acc = None
        for s, ref in enumerate((r0, r1, r2, r3)):
            term = jnp.where(chip_ref[0] == s, own_ref[0], ref[0]).astype(F32)
            acc = term if acc is None else acc + term
        o_ref[...] = acc

    def slot(s):
        return pl.BlockSpec((1, t, cols), lambda i, c, s=s: (jnp.where(c[0] == s, c[0] ^ 1, s), i, 0))

    return pl.pallas_call(
        body, name=name,
        grid_spec=pltpu.PrefetchScalarGridSpec(
            num_scalar_prefetch=1, grid=(rows // t,),
            in_specs=[slot(s) for s in range(nc)] + [pl.BlockSpec((1, t, cols), lambda i, c: (c[0], i, 0))],
            out_specs=pl.BlockSpec((t, cols), lambda i, c: (i, 0))),
        out_shape=jax.ShapeDtypeStruct((rows, cols), F32), compiler_params=_params(),
    )(chip, rx, rx, rx, rx, csum)


def _adamw_update(w, g, m, v):
    nm = ADAM_B1 * m + (1.0 - ADAM_B1) * g
    nv = ADAM_B2 * v + (1.0 - ADAM_B2) * (g * g)
    m_hat = nm * (1.0 / (1.0 - ADAM_B1 ** ADAM_STEP))
    v_hat = nv * (1.0 / (1.0 - ADAM_B2 ** ADAM_STEP))
    return -ADAM_LR * (m_hat / (jnp.sqrt(v_hat) + ADAM_EPS) + ADAM_WD * w), nm, nv


def adamw(w, g, m, v, name):
    rows, cols = w.shape
    t = _row_tile(rows, cols)

    def body(w_ref, g_ref, m_ref, v_ref, d_ref, nm_ref, nv_ref):
        d_ref[...], nm_ref[...], nv_ref[...] = _adamw_update(w_ref[...], g_ref[...], m_ref[...], v_ref[...])

    spec = pl.BlockSpec((t, cols), lambda i: (i, 0))
    shp = jax.ShapeDtypeStruct((rows, cols), F32)
    return pl.pallas_call(
        body, name=name, grid=(rows // t,), in_specs=[spec] * 4, out_specs=[spec] * 3,
        out_shape=[shp, shp, shp], compiler_params=_params(),
    )(w, g, m, v)


def adamw_halves(w, g_mine, g_other, m, v, core, name):
    rows, cols = w.shape
    hr = rows // 2
    t = _row_tile(hr, cols)
    nt = hr // t

    def body(core_ref, w_ref, gm_ref, go_ref, m_ref, v_ref, g_ref, d_ref, nm_ref, nv_ref):
        g = jnp.where(pl.program_id(0) // nt == core_ref[0], gm_ref[...], go_ref[...])
        g_ref[...] = g
        d_ref[...], nm_ref[...], nv_ref[...] = _adamw_update(w_ref[...], g, m_ref[...], v_ref[...])

    spec = pl.BlockSpec((t, cols), lambda i, c: (i, 0))
    half = pl.BlockSpec((t, cols), lambda i, c: (i % nt, 0))
    shp = jax.ShapeDtypeStruct((rows, cols), F32)
    return pl.pallas_call(
        body, name=name,
        grid_spec=pltpu.PrefetchScalarGridSpec(num_scalar_prefetch=1, grid=(2 * nt,),
                                               in_specs=[spec, half, half, spec, spec], out_specs=[spec] * 4),
        out_shape=[shp] * 4, compiler_params=_params(),
    )(core, w, g_mine, g_other, m, v)


WEIGHTS = ("norm_mix_pre", "norm_mix_post", "norm_ffn_pre", "norm_ffn_post", "w_in", "conv_short_w",
           "w_conv_branch", "lru_conv_w", "lru_conv_b", "lru_wa", "lru_ba", "lru_wx", "lru_bx", "lru_lambda",
           "w_lru_branch", "w_out", "ffn_w_up", "ffn_conv_w", "ffn_conv_b", "ffn_w_down")
BIG = ("w_in", "ffn_w_up", "w_conv_branch", "w_lru_branch", "w_out", "ffn_w_down")
BIG_KIND = ("col", "col", "row", "row", "row", "row")
SMALL = ("conv_short_w", "lru_conv_w", "lru_wa", "lru_ba", "lru_wx", "lru_bx", "ffn_conv_w")
REPL = ("norm_mix_pre", "norm_mix_post", "norm_ffn_pre", "norm_ffn_post", "lru_conv_b", "lru_lambda", "ffn_conv_b")
PACK_W = 256
SMALL_ROWS = 576
REPL_ROWS = 16
LOSS_ROW = 12
FFN_SHARD = 2 * D_FF // N_CHIPS
QUARTER = HEAD_DIM // N_CHIPS
SMALL_PARTS = (("conv_short_w", 3, (1, 3, PACK_W)), ("lru_conv_w", 4, (1, 4, PACK_W)),
               ("lru_wa", LRU_HEADS * QUARTER, (1, LRU_HEADS, QUARTER, HEAD_DIM)), ("lru_ba", 1, (1, LRU_HEADS, QUARTER)),
               ("lru_wx", LRU_HEADS * QUARTER, (1, LRU_HEADS, QUARTER, HEAD_DIM)), ("lru_bx", 1, (1, LRU_HEADS, QUARTER)),
               ("ffn_conv_w", 3 * FFN_SHARD // PACK_W, (1, 3, FFN_SHARD)))


def _pad8(nr):
    return -(-nr // SUBLANES) * SUBLANES


def _pack_small_shard(p):
    rows = [jnp.pad(p[name].reshape(nr, PACK_W), ((0, _pad8(nr) - nr), (0, 0))) for name, nr, _ in SMALL_PARTS]
    used = sum(r.shape[0] for r in rows)
    return jnp.concatenate(rows + [jnp.zeros((SMALL_ROWS - used, PACK_W), F32)], axis=0)


def _unpack_small_shard(buf):
    out, r = {}, 0
    for name, nr, shape in SMALL_PARTS:
        out[name] = buf[r:r + nr].reshape(shape)
        r += _pad8(nr)
    return out


def _full_small(g4):
    per = [_unpack_small_shard(g4[k]) for k in range(N_CHIPS)]
    cat = lambda name, axis: jnp.concatenate([per[k][name][0] for k in range(N_CHIPS)], axis=axis)
    return dict(conv_short_w=cat("conv_short_w", 1), lru_conv_w=cat("lru_conv_w", 1),
                lru_wa=cat("lru_wa", 1), lru_ba=cat("lru_ba", 1).reshape(1, D_MODEL),
                lru_wx=cat("lru_wx", 1), lru_bx=cat("lru_bx", 1).reshape(1, D_MODEL),
                ffn_conv_w=cat("ffn_conv_w", 1))


def _split_small(full):
    shards = []
    for k in range(N_CHIPS):
        cols = lambda a, w: a[:, k * w:(k + 1) * w]
        q = slice(k * QUARTER, (k + 1) * QUARTER)
        shards.append(_pack_small_shard(dict(
            conv_short_w=cols(full["conv_short_w"], PACK_W), lru_conv_w=cols(full["lru_conv_w"], PACK_W),
            lru_wa=full["lru_wa"][:, q, :], lru_ba=full["lru_ba"].reshape(LRU_HEADS, HEAD_DIM)[:, q],
            lru_wx=full["lru_wx"][:, q, :], lru_bx=full["lru_bx"].reshape(LRU_HEADS, HEAD_DIM)[:, q],
            ffn_conv_w=cols(full["ffn_conv_w"], FFN_SHARD))))
    return jnp.stack(shards)


def _pack_repl(p, loss=None):
    rows = [p[n].reshape(-1, D_MODEL) for n in REPL]
    if loss is not None:
        rows.append(jnp.broadcast_to(loss.reshape(1, 1), (1, D_MODEL)))
    used = sum(r.shape[0] for r in rows)
    return jnp.concatenate(rows + [jnp.zeros((REPL_ROWS - used, D_MODEL), F32)], axis=0)


def _unpack_repl(buf):
    out, r = {}, 0
    for n in REPL:
        nr = (2 * D_FF // D_MODEL) if n == "ffn_conv_b" else 1
        out[n] = buf[r:r + nr].reshape(1, nr * D_MODEL)
        r += nr
    return out


def kernel(x, norm_mix_pre, norm_mix_post, norm_ffn_pre, norm_ffn_post, w_in, conv_short_w, w_conv_branch, lru_conv_w, lru_conv_b, lru_wa, lru_ba, lru_wx, lru_bx, lru_lambda, w_lru_branch, w_out, ffn_w_up, ffn_conv_w, ffn_conv_b, ffn_w_down, loss_target, m_norm_mix_pre, m_norm_mix_post, m_norm_ffn_pre, m_norm_ffn_post, m_w_in, m_conv_short_w, m_w_conv_branch, m_lru_conv_w, m_lru_conv_b, m_lru_wa, m_lru_ba, m_lru_wx, m_lru_bx, m_lru_lambda, m_w_lru_branch, m_w_out, m_ffn_w_up, m_ffn_conv_w, m_ffn_conv_b, m_ffn_w_down, v_norm_mix_pre, v_norm_mix_post, v_norm_ffn_pre, v_norm_ffn_post, v_w_in, v_conv_short_w, v_w_conv_branch, v_lru_conv_w, v_lru_conv_b, v_lru_wa, v_lru_ba, v_lru_wx, v_lru_bx, v_lru_lambda, v_w_lru_branch, v_w_out, v_ffn_w_up, v_ffn_conv_w, v_ffn_conv_b, v_ffn_w_down):
    given = dict(locals())
    w = {n: given[n] for n in WEIGHTS}
    m = {n: given["m_" + n] for n in WEIGHTS}
    v = {n: given["v_" + n] for n in WEIGHTS}

    xi, yi, ci = _position()
    chip_i = 2 * xi + yi
    chip = chip_i.astype(jnp.int32).reshape(1)
    core = ci.astype(jnp.int32).reshape(1)
    xs, target = x[0], loss_target[0]
    g1, g2, g3, g4 = w["norm_mix_pre"], w["norm_mix_post"], w["norm_ffn_pre"], w["norm_ffn_post"]
    shard = {n: w[n][0].astype(BF16) for n in BIG}
    small_shard = _pack_small_shard(w)

    def gathered(bufs, names):
        return [_own_slot(b, small_shard if n == "small" else shard[n], chip_i) for b, n in zip(bufs, names)]

    def chip_sums(arrays, kinds, tag):
        theirs = pair_split(arrays, kinds, "pair_split_" + tag)
        return [add_pair(g, k, t, core, "pair_add_%s_%d" % (tag, i)) for i, (g, k, t) in enumerate(zip(arrays, kinds, theirs))]

    h1, h1t = norm_in(xs, g1)
    win4, small4 = gathered(run_ride(gather_ride([shard["w_in"], small_shard]), "gather_first"), ("w_in", "small"))
    small = _full_small(small4)
    (proj,), got = matmul_cols(h1, win4, "proj_fwd",
                               ride=gather_ride([shard["w_conv_branch"], shard["w_lru_branch"], shard["w_out"]]))
    wcb, wlb, wout = [g.reshape(-1, D_MODEL) for g in gathered(got, ("w_conv_branch", "w_lru_branch", "w_out"))]
    up_piece = lambda r0, nr, into=None: gather_ride([shard["ffn_w_up"]], items=[(0, r0, nr)], into=into)
    down_piece = lambda r0, nr, into=None: gather_ride([shard["ffn_w_down"]], items=[(0, r0, nr)], into=into)
    (q, ya), got = mix_conv_fwd(proj, small["conv_short_w"], ride=up_piece(0, 128))
    (xl, r, gi, h, yb), got = mix_lru_fwd(
        proj, small["lru_conv_w"], w["lru_conv_b"], small["lru_wa"].astype(BF16), small["lru_ba"],
        small["lru_wx"].astype(BF16), small["lru_bx"], w["lru_lambda"], ride=up_piece(128, 512, got))
    (a, b, merged), got = branch_merge_fwd(ya, yb, wcb, wlb, proj, ride=up_piece(640, 384, got))
    (wup4,) = gathered(got, ("ffn_w_up",))
    (mix, x2, h2, h2t), got = mix_out_fwd(merged, wout, xs, g2, g3, ride=down_piece(0, 256))
    (up, act, f), got = ffn_up_act_fwd(h2, wup4, small["ffn_conv_w"], w["ffn_conv_b"], ride=down_piece(256, 512, got))
    wdown = gathered(got, ("ffn_w_down",))[0].reshape(-1, D_MODEL)
    dy, dout, loss, dg4 = ffn_down_loss(f, wdown, x2, target, g4)

    dh2, dwup, dwdown, dfw, dfb = ffn_up_bwd(dout, wdown, up, act, f, small["ffn_conv_w"], wup4, h2t)
    cs_down, cs_up = chip_sums([dwdown, dwup], ["row", "col2"], "ffn")
    down_rows = lambda r0, nr, into=None: exchange_ride([cs_down], items=[(0, r0, nr)], into=into)
    up_rows = lambda r0, nr, into=None: exchange_ride([cs_up], items=[(0, r0, nr)], into=into)
    (dx2, dmix, dg3, dg2), rx_down = norms_mid_bwd(dh2, x2, dy, mix, g3, g2, ride=down_rows(0, 128))
    (da, db, dwout, dgates), rx_down = mix_out_bwd(dmix, wout, merged, a, b, proj, ride=down_rows(128, 256, rx_down))
    (dconv, dwcb, dws), rx_up = mix_conv_bwd(da, wcb, proj, q, small["conv_short_w"], ride=up_rows(0, 176))
    cs_mid = chip_sums([dwout, dwcb], ["row", "row"], "mid")
    (dlru, dwlb, dwa, dwx, dba, dbx, dwl, dbl, dlam), rx_up = mix_lru_bwd(
        db, wlb, proj, xl, r, gi, h, small["lru_conv_w"], small["lru_wa"].astype(BF16), small["lru_wx"].astype(BF16),
        w["lru_lambda"], ride=up_rows(176, 336, rx_up))
    grads = dict(norm_mix_post=dg2, norm_ffn_pre=dg3, norm_ffn_post=dg4, conv_short_w=dws, lru_conv_w=dwl,
                 lru_conv_b=dbl, lru_wa=dwa, lru_ba=dba, lru_wx=dwx, lru_bx=dbx, lru_lambda=dlam,
                 ffn_conv_w=jnp.concatenate([dfw[0], dfw[1]], axis=1), ffn_conv_b=jnp.concatenate([dfb[0], dfb[1]], axis=1))
    cs_late = chip_sums([dwlb, _split_small(grads)], ["row", "lead"], "late")
    dproj = [dconv, dlru, dgates]
    (dwin,), rx_all = matmul_cols_bwd(dproj, h1t, "proj_wgrad", True, ride=exchange_ride(cs_mid + cs_late))
    rx_mid, rx_late = rx_all[:2], rx_all[2:]
    cs_in = chip_sums([dwin], ["col"], "in")
    in_rows = lambda r0, nr, into=None: exchange_ride(cs_in, items=[(0, r0, nr)], into=into)
    (dh1,), rx_in = matmul_cols_bwd(dproj, win4, "proj_dgrad", False, ride=in_rows(0, 384))
    (dx, grads["norm_mix_pre"]), rx_in = norm_in_bwd(dh1, xs, dx2, g1, ride=in_rows(384, 128, rx_in))
    rx_in = rx_in[0]
    rep_part = _pack_repl(grads, loss[0, 0])
    (rep_all,) = run_ride(exchange_ride([], rep=rep_part), "exchange_repl")

    order = (("w_in", cs_in[0], rx_in), ("ffn_w_up", cs_up, rx_up[0]), ("w_conv_branch", cs_mid[1], rx_mid[1]),
             ("w_lru_branch", cs_late[0], rx_late[0]), ("w_out", cs_mid[0], rx_mid[0]),
             ("ffn_w_down", cs_down, rx_down[0]), ("small", cs_late[1], rx_late[1]))
    halves = [sum_chips(rx, cs, chip, "chip_sum_" + n) for n, cs, rx in order]
    me = 4 * xi + 2 * yi + ci
    rep_grad = sum_lead(_own_slot(rep_all, rep_part, me), "device_sum")
    others = pair_swap(halves)

    g_out, d_out, m_out, v_out = {}, {}, {}, {}
    for n, gm, go in zip(BIG, halves[:-1], others[:-1]):
        g, d, nm, nv = adamw_halves(w[n][0], gm, go, m[n][0], v[n][0], core, "adamw_" + n)
        g_out[n], d_out[n], m_out[n], v_out[n] = g[None], d[None], nm[None], nv[None]
    bufs = adamw_halves(small_shard, halves[-1], others[-1], _pack_small_shard(m), _pack_small_shard(v),
                        core, "adamw_small")
    for dst, buf in zip((g_out, d_out, m_out, v_out), bufs):
        dst.update(_unpack_small_shard(buf))
    d, nm, nv = adamw(_pack_repl(w), rep_grad, _pack_repl(m), _pack_repl(v), "adamw_repl")
    for dst, buf in ((g_out, rep_grad), (d_out, d), (m_out, nm), (v_out, nv)):
        dst.update(_unpack_repl(buf))

    return (rep_grad[LOSS_ROW, 0], dx[None], *[g_out[n] for n in WEIGHTS], *[d_out[n] for n in WEIGHTS],
            *[m_out[n] for n in WEIGHTS], *[v_out[n] for n in WEIGHTS])
```

```python
import functools
import math

import jax
import jax.numpy as jnp
from jax import lax
from jax.experimental import pallas as pl
from jax.experimental.pallas import tpu as pltpu

F32 = jnp.float32
BF16 = jnp.bfloat16

D_MODEL = 1024
N_CHIPS = 4
N_SEG = 7
D_FF = 3 * D_MODEL
LRU_HEADS = 4
HEAD_DIM = D_MODEL // LRU_HEADS
LRU_C = 8.0
RMS_EPS = 1e-6
CW = 256
FW = 256
SUBLANES = 8
VMEM_LIMIT = 58 * 1024 * 1024

ADAM_LR = 0.001
ADAM_B1 = 0.9
ADAM_B2 = 0.999
ADAM_EPS = 1e-08
ADAM_WD = 0.01
ADAM_STEP = 10

_GELU_C = math.sqrt(2.0 / math.pi)
_GELU_K = 0.044715


def _params(**kw):
    return pltpu.CompilerParams(vmem_limit_bytes=VMEM_LIMIT, **kw)


def _sigmoid(x):
    return 1.0 / (1.0 + jnp.exp(-x))


def _gelu(x):
    t = jnp.tanh(_GELU_C * (x + _GELU_K * x * x * x))
    return 0.5 * x * (1.0 + t)


def _gelu_and_grad(x):
    x2 = x * x
    t = jnp.tanh(_GELU_C * (x + _GELU_K * x * x2))
    g = 0.5 * x * (1.0 + t)
    dg = 0.5 * (1.0 + t) + 0.5 * x * (1.0 - t * t) * _GELU_C * (1.0 + 3.0 * _GELU_K * x2)
    return g, dg


def _log_sigmoid(x):
    e = jnp.exp(-jnp.abs(x))
    u = 1.0 + e
    l1p = jnp.where(u == 1.0, e, jnp.log(u) * e / (u - 1.0))
    return jnp.minimum(x, 0.0) - l1p


def _neg_expm1(z):
    series = -z * (1.0 + z * (0.5 + z * (1.0 / 6.0 + z * (1.0 / 24.0 + z * (1.0 / 120.0 + z * (1.0 / 720.0))))))
    return jnp.where(z > -0.2, series, 1.0 - jnp.exp(z))


def _rows(shape):
    return lax.broadcasted_iota(jnp.int32, shape, 0)


def _shift_down(x, k):
    return jnp.where(_rows(x.shape) >= k, pltpu.roll(x, k, 0), 0.0)


def _shift_up(x, k):
    n = x.shape[0]
    return jnp.where(_rows(x.shape) < n - k, pltpu.roll(x, n - k, 0), 0.0)


def _delays(x, k_width):
    return [x] + [_shift_down(x, j) for j in range(1, k_width)]


def _advances(dy, k_width):
    return [dy] + [_shift_up(dy, j) for j in range(1, k_width)]


def _taps_sum(shifted, w_ref, b=None):
    k_width = w_ref.shape[0]
    y = w_ref[k_width - 1:k_width, :] * shifted[0]
    for j in range(1, k_width):
        y = y + w_ref[k_width - 1 - j:k_width - j, :] * shifted[j]
    if b is not None:
        y = y + b
    return y


def _causal_conv(x, w_ref, b=None):
    return _taps_sum(_delays(x, w_ref.shape[0]), w_ref, b)


def _conv_wgrad(advanced, x):
    k_width = len(advanced)
    rows = [jnp.sum(advanced[k_width - 1 - k] * x, axis=0, keepdims=True) for k in range(k_width)]
    return jnp.concatenate(rows, axis=0)


def _dot(a, b):
    return jnp.dot(a, b, preferred_element_type=F32)


def _dot_nt(a, b):
    return lax.dot_general(a, b, (((1,), (1,)), ((), ())), preferred_element_type=F32)


def _dot_tn(a, b):
    return lax.dot_general(a, b, (((0,), (0,)), ((), ())), preferred_element_type=F32)


def _rms_stats(x):
    r = lax.rsqrt(jnp.mean(x * x, axis=-1, keepdims=True) + RMS_EPS)
    return x * r, r


def _rms_bwd(n, r, g, dy):
    dn = dy * g
    dx = r * (dn - n * jnp.mean(dn * n, axis=-1, keepdims=True))
    return dx, dy * n


def _scan_forward(a_ref, b_ref, h_ref):
    n, c = a_ref.shape
    row = lax.broadcasted_iota(jnp.int32, (SUBLANES, c), 0)

    def group(g, carry):
        r0 = pl.multiple_of(g * SUBLANES, SUBLANES)
        a = a_ref[pl.ds(r0, SUBLANES), :]
        b = b_ref[pl.ds(r0, SUBLANES), :]
        for k in (1, 2, 4):
            ap = jnp.where(row >= k, pltpu.roll(a, k, 0), 1.0)
            bp = jnp.where(row >= k, pltpu.roll(b, k, 0), 0.0)
            b = a * bp + b
            a = a * ap
        h = a * carry + b
        h_ref[pl.ds(r0, SUBLANES), :] = h
        return h[SUBLANES - 1:SUBLANES, :]

    lax.fori_loop(0, n // SUBLANES, group, jnp.zeros((1, c), F32))


def _scan_backward(c_ref, b_ref, g_ref):
    n, ch = c_ref.shape
    row = lax.broadcasted_iota(jnp.int32, (SUBLANES, ch), 0)
    n_groups = n // SUBLANES

    def group(i, carry):
        r0 = pl.multiple_of((n_groups - 1 - i) * SUBLANES, SUBLANES)
        a = c_ref[pl.ds(r0, SUBLANES), :]
        b = b_ref[pl.ds(r0, SUBLANES), :]
        for k in (1, 2, 4):
            keep = row < SUBLANES - k
            ap = jnp.where(keep, pltpu.roll(a, SUBLANES - k, 0), 1.0)
            bp = jnp.where(keep, pltpu.roll(b, SUBLANES - k, 0), 0.0)
            b = a * bp + b
            a = a * ap
        g = a * carry + b
        g_ref[pl.ds(r0, SUBLANES), :] = g
        return g[0:1, :]

    lax.fori_loop(0, n_groups, group, jnp.zeros((1, ch), F32))


MESH = pl.DeviceIdType.MESH
_HBM = pl.BlockSpec(memory_space=pltpu.HBM)
_OTHER_CHIPS = ((1, 0), (0, 1), (1, 1))
_OTHER_DEVICES = tuple((dx, dy, dc) for dx in (0, 1) for dy in (0, 1) for dc in (0, 1) if dx or dy or dc)
N_DEVICES = 8


def _position():
    return lax.axis_index("x"), lax.axis_index("y"), lax.axis_index("c")


def _flip(v, d):
    return 1 - v if d else v


def _chip(x, y, p):
    px, py = _flip(x, _OTHER_CHIPS[p][0]), _flip(y, _OTHER_CHIPS[p][1])
    return px, py, 2 * px + py


class _Ride:
    def __init__(self, srcs, bufs, scratch, plan, collective_id):
        self.srcs, self.bufs, self.scratch, self.plan = list(srcs), list(bufs), list(scratch), plan
        self.collective_id = collective_id


NEIGHBOURS_AND_SIBLING = 1
OTHER_CHIPS_SAME_CORE = 2
ALL_DEVICES = 3


def _handshake(peers):
    barrier = pltpu.get_barrier_semaphore()
    for peer in peers:
        pl.semaphore_signal(barrier, inc=1, device_id=peer, device_id_type=MESH)
    pl.semaphore_wait(barrier, len(peers))


def _call(body, *, name, grid, in_specs, out_specs, out_shape, operands, scratch_shapes=(), ride=None):
    in_specs, out_specs, out_shape = list(in_specs), list(out_specs), list(out_shape)
    scratch_shapes = list(scratch_shapes)
    if ride is None:
        return pl.pallas_call(body, name=name, grid=grid, in_specs=in_specs, out_specs=out_specs, out_shape=out_shape,
                              scratch_shapes=scratch_shapes, compiler_params=_params())(*operands)
    n_in, n_out, n_scr = len(in_specs), len(out_shape), len(scratch_shapes)
    old = [i for i, b in enumerate(ride.bufs) if not isinstance(b, jax.ShapeDtypeStruct)]
    n_src, n_old, n_buf = len(ride.srcs), len(old), len(ride.bufs)

    def full_body(*refs):
        o0 = n_in + n_src + n_old
        s0 = o0 + n_out + n_buf
        start, relay, relay_on, finish = ride.plan(refs[n_in:n_in + n_src], refs[o0 + n_out:s0], refs[s0 + n_scr:])
        ids = [pl.program_id(i) for i in range(len(grid))]
        first = functools.reduce(jnp.logical_and, [i == 0 for i in ids])
        middle = functools.reduce(jnp.logical_and, [ids[0] == grid[0] // 2] + [i == 0 for i in ids[1:]])
        last = functools.reduce(jnp.logical_and, [i == g - 1 for i, g in zip(ids, grid)])
        pl.when(first)(start)
        pl.when(middle)(relay)
        pl.when(last)(relay_on)
        body(*refs[:n_in], *refs[o0:o0 + n_out], *refs[s0:s0 + n_scr])
        pl.when(last)(finish)

    shapes = [jax.ShapeDtypeStruct(b.shape, b.dtype) for b in ride.bufs]
    res = pl.pallas_call(
        full_body, name=name, grid=grid,
        in_specs=in_specs + [_HBM] * (n_src + n_old), out_specs=out_specs + [_HBM] * n_buf,
        out_shape=out_shape + shapes, scratch_shapes=scratch_shapes + ride.scratch,
        input_output_aliases={n_in + n_src + k: n_out + i for k, i in enumerate(old)},
        compiler_params=_params(collective_id=ride.collective_id),
    )(*operands, *ride.srcs, *[ride.bufs[i] for i in old])
    return list(res[:n_out]), list(res[n_out:])


def run_ride(ride, name):
    def body():
        pass

    return _call(body, name=name, grid=(1,), in_specs=[], out_specs=[], out_shape=[], operands=[], ride=ride)[1]


def gather_ride(shards, items=None, into=None):
    items = items or [(a, 0, s.shape[0]) for a, s in enumerate(shards)]
    bufs = into or [jax.ShapeDtypeStruct((N_CHIPS,) + s.shape, s.dtype) for s in shards]
    nrel = len(_OTHER_CHIPS)

    def plan(srcs, dsts, sems):
        ici_send, ici_recv, hop_send, hop_recv, sib_send, sib_recv = sems
        x, y, c = _position()
        j = 2 * x + y

        def rows(ref, it, h, q=None):
            half = it[2] // 2
            if q is None:
                return ref.at[pl.ds(it[1] + h * half, half), :]
            return ref.at[pl.ds(it[1] + h * half + q * (half // 2), half // 2), :]

        def ici(i, p, slot):
            it = items[i]
            px, py, _ = _chip(x, y, p)
            return pltpu.make_async_remote_copy(
                src_ref=rows(srcs[it[0]], it, c), dst_ref=rows(dsts[it[0]].at[slot], it, c),
                send_sem=ici_send.at[i * nrel + p], recv_sem=ici_recv.at[i * nrel + p],
                device_id=(px, py, c), device_id_type=MESH)

        def hop(i, p, slot):
            it = items[i]
            part = rows(dsts[it[0]].at[slot], it, c, p)
            px, py, _ = _chip(x, y, 1 - p)
            return pltpu.make_async_remote_copy(
                src_ref=part, dst_ref=part, send_sem=hop_send.at[i * 2 + p], recv_sem=hop_recv.at[i * 2 + p],
                device_id=(px, py, c), device_id_type=MESH)

        def sib(i, p, h):
            it = items[i]
            part = rows(dsts[it[0]].at[_chip(x, y, p)[2]], it, h)
            return pltpu.make_async_remote_copy(
                src_ref=part, dst_ref=part, send_sem=sib_send.at[i * nrel + p], recv_sem=sib_recv.at[i * nrel + p],
                device_id=(x, y, 1 - c), device_id_type=MESH)

        every = range(len(items))
        diag = _chip(x, y, 2)[2]

        def start():
            _handshake([_chip(x, y, 0)[:2] + (c,), _chip(x, y, 1)[:2] + (c,), (x, y, 1 - c)])
            for i in every:
                for p in (0, 1):
                    ici(i, p, j).start()

        def relay():
            for i in every:
                for p in (0, 1):
                    k = _chip(x, y, p)[2]
                    ici(i, p, k).wait_recv()
                    hop(i, p, k).start()
                    sib(i, p, c).start()

        def relay_on():
            for i in every:
                for p in (0, 1):
                    hop(i, p, diag).wait_recv()
                sib(i, 2, c).start()

        def finish():
            for i in every:
                for p in range(nrel):
                    sib(i, p, 1 - c).wait_recv()
            for i in every:
                for p in (0, 1):
                    ici(i, p, j).wait_send()
                    hop(i, p, _chip(x, y, p)[2]).wait_send()
                for p in range(nrel):
                    sib(i, p, c).wait_send()

        return start, relay, relay_on, finish

    n = len(items)
    sems = [pltpu.SemaphoreType.DMA((n * nrel,))] * 2 + [pltpu.SemaphoreType.DMA((n * 2,))] * 2 \
        + [pltpu.SemaphoreType.DMA((n * nrel,))] * 2
    return _Ride(shards, bufs, sems, plan, NEIGHBOURS_AND_SIBLING)


def exchange_ride(sums, items=None, into=None, rep=None):
    items = [(a, 0, s.shape[1]) for a, s in enumerate(sums)] if items is None else items
    into = into or [None] * len(sums)
    bufs = [jax.ShapeDtypeStruct(s.shape, s.dtype) if b is None else b for s, b in zip(sums, into)]
    srcs = list(sums)
    scratch = [pltpu.SemaphoreType.DMA((max(len(items), 1) * len(_OTHER_CHIPS),))] * 2
    if rep is not None:
        srcs.append(rep)
        bufs.append(jax.ShapeDtypeStruct((N_DEVICES,) + rep.shape, rep.dtype))
        scratch += [pltpu.SemaphoreType.DMA((len(_OTHER_DEVICES),))] * 2
    nrel = len(_OTHER_CHIPS)

    def plan(src_refs, dst_refs, sems):
        x, y, c = _position()
        j = 2 * x + y
        me = 4 * x + 2 * y + c

        def part(i, p, src_slot, dst_slot):
            a, r0, nr = items[i]
            px, py, _ = _chip(x, y, p)
            return pltpu.make_async_remote_copy(
                src_ref=src_refs[a].at[src_slot, pl.ds(r0, nr), :], dst_ref=dst_refs[a].at[dst_slot, pl.ds(r0, nr), :],
                send_sem=sems[0].at[i * nrel + p], recv_sem=sems[1].at[i * nrel + p],
                device_id=(px, py, c), device_id_type=MESH)

        def device(q):
            dx, dy, dc = _OTHER_DEVICES[q]
            return _flip(x, dx), _flip(y, dy), _flip(c, dc)

        def rep_copy(q, slot):
            return pltpu.make_async_remote_copy(
                src_ref=src_refs[-1], dst_ref=dst_refs[-1].at[slot], send_sem=sems[2].at[q], recv_sem=sems[3].at[q],
                device_id=device(q), device_id_type=MESH)

        pairs = [(i, p) for i in range(len(items)) for p in range(nrel)]
        others = range(len(_OTHER_DEVICES)) if rep is not None else ()

        def start():
            if rep is None:
                _handshake([_chip(x, y, p)[:2] + (c,) for p in range(nrel)])
            else:
                _handshake([device(q) for q in others])
            for i, p in pairs:
                part(i, p, _chip(x, y, p)[2], j).start()
            for q in others:
                rep_copy(q, me).start()

        def finish():
            for i, p in pairs:
                k = _chip(x, y, p)[2]
                part(i, p, k, k).wait_recv()
            for q in others:
                px, py, pc = device(q)
                rep_copy(q, 4 * px + 2 * py + pc).wait_recv()
            for i, p in pairs:
                part(i, p, _chip(x, y, p)[2], j).wait_send()
            for q in others:
                rep_copy(q, me).wait_send()

        return start, lambda: None, lambda: None, finish

    return _Ride(srcs, bufs, scratch, plan, OTHER_CHIPS_SAME_CORE if rep is None else ALL_DEVICES)


def _own_slot(buf, own, index):
    return lax.dynamic_update_slice(buf, own[None], (index,) + (0,) * own.ndim)


def _token_tile(s):
    return min(s, 512)


def norm_in(x, g):
    s, d = x.shape
    t = _token_tile(s)

    def body(x_ref, g_ref, o_ref, ot_ref):
        n, _ = _rms_stats(x_ref[...])
        h = n * g_ref[...]
        o_ref[...] = h.astype(BF16)
        ot_ref[...] = h.T.astype(BF16)

    return pl.pallas_call(
        body, name="norm_in", grid=(s // t,),
        in_specs=[pl.BlockSpec((t, d), lambda i: (i, 0)), pl.BlockSpec((1, d), lambda i: (0, 0))],
        out_specs=[pl.BlockSpec((t, d), lambda i: (i, 0)), pl.BlockSpec((d, t), lambda i: (0, i))],
        out_shape=[jax.ShapeDtypeStruct((s, d), BF16), jax.ShapeDtypeStruct((d, s), BF16)],
        compiler_params=_params(),
    )(x, g)


def matmul_cols(a, w4, name, ride=None):
    m, k = a.shape
    nj, _, ns = w4.shape
    nb = ns // CW

    def body(a_ref, w_ref, o_ref):
        o_ref[...] = _dot(a_ref[...], w_ref[0])

    return _call(
        body, name=name, grid=(nj, nb),
        in_specs=[pl.BlockSpec((m, k), lambda j, b: (0, 0)),
                  pl.BlockSpec((1, k, CW), lambda j, b: (j, 0, b))],
        out_specs=[pl.BlockSpec((m, CW), lambda j, b: (0, j * nb + b))],
        out_shape=[jax.ShapeDtypeStruct((m, nj * ns), F32)],
        operands=(a, w4), ride=ride)


def mix_conv_fwd(proj, ws, ride=None):
    s = proj.shape[0]
    nblk = D_MODEL // CW

    def body(cb_ref, cc_ref, cx_ref, ws_ref, q_ref, ya_ref):
        q = _causal_conv(cc_ref[...] * cx_ref[...], ws_ref)
        q_ref[...] = q
        ya_ref[...] = (cb_ref[...] * q).astype(BF16)

    seg = lambda k: pl.BlockSpec((s, CW), lambda c, k=k: (0, k * nblk + c))
    return _call(
        body, name="mix_conv_fwd", grid=(nblk,),
        in_specs=[seg(0), seg(1), seg(2), pl.BlockSpec((3, CW), lambda c: (0, c))],
        out_specs=[pl.BlockSpec((s, CW), lambda c: (0, c))] * 2,
        out_shape=[jax.ShapeDtypeStruct((s, D_MODEL), F32), jax.ShapeDtypeStruct((s, D_MODEL), BF16)],
        operands=(proj, proj, proj, ws), ride=ride)


def _lru_gates(r, ls):
    log_a = LRU_C * r * ls
    a = jnp.exp(log_a)
    mult = jnp.sqrt(_neg_expm1(2.0 * log_a))
    mult = jnp.where(_rows(r.shape) == 0, 1.0, mult)
    return a, mult


def mix_lru_fwd(proj, wl, bl, wa, ba, wx, bx, lam, ride=None):
    s = proj.shape[0]
    nblk = D_MODEL // CW

    def body(lx_ref, ly_ref, wl_ref, bl_ref, wa_ref, ba_ref, wx_ref, bx_ref, lam_ref,
             xl_ref, r_ref, i_ref, h_ref, yb_ref, a_scr, u_scr):
        xl = _causal_conv(lx_ref[...], wl_ref, bl_ref[...])
        xlb = xl.astype(BF16)
        xl_ref[...] = xlb
        r = _sigmoid(_dot(xlb, wa_ref[0]) + ba_ref[...])
        i = _sigmoid(_dot(xlb, wx_ref[0]) + bx_ref[...])
        r_ref[...] = r.astype(BF16)
        i_ref[...] = i.astype(BF16)
        a, mult = _lru_gates(r, _log_sigmoid(lam_ref[...]))
        a_scr[...] = a
        u_scr[...] = mult * i * xl
        _scan_forward(a_scr, u_scr, h_ref)
        yb_ref[...] = (h_ref[...] * _gelu(ly_ref[...])).astype(BF16)

    blk = lambda k: pl.BlockSpec((s, CW), lambda c, k=k: (0, k * nblk + c))
    vec = pl.BlockSpec((1, CW), lambda c: (0, c))
    mat = pl.BlockSpec((1, CW, CW), lambda c: (c, 0, 0))
    out = pl.BlockSpec((s, CW), lambda c: (0, c))
    f = jax.ShapeDtypeStruct((s, D_MODEL), F32)
    hb = jax.ShapeDtypeStruct((s, D_MODEL), BF16)
    return _call(
        body, name="mix_lru_fwd", grid=(nblk,),
        in_specs=[blk(3), blk(4), pl.BlockSpec((4, CW), lambda c: (0, c)), vec, mat, vec, mat, vec, vec],
        out_specs=[out] * 5,
        out_shape=[hb, hb, hb, f, hb],
        scratch_shapes=[pltpu.VMEM((s, CW), F32), pltpu.VMEM((s, CW), F32)],
        operands=(proj, proj, wl, bl, wa, ba, wx, bx, lam), ride=ride)


def branch_merge_fwd(ya, yb, wcb, wlb, proj, ride=None):
    s = ya.shape[0]
    nblk = D_MODEL // CW

    def body(ya_ref, yb_ref, wcb_ref, wlb_ref, gc_ref, gl_ref, a_ref, b_ref, m_ref):
        a = _dot(ya_ref[...], wcb_ref[...])
        b = _dot(yb_ref[...], wlb_ref[...])
        a_ref[...] = a
        b_ref[...] = b
        m_ref[...] = (_sigmoid(gc_ref[...]) * a + _sigmoid(gl_ref[...]) * b).astype(BF16)

    res = pl.BlockSpec((s, D_MODEL), lambda n: (0, 0))
    wcol = pl.BlockSpec((D_MODEL, CW), lambda n: (0, n))
    blk = lambda k: pl.BlockSpec((s, CW), lambda n, k=k: (0, k * nblk + n))
    out = pl.BlockSpec((s, CW), lambda n: (0, n))
    f = jax.ShapeDtypeStruct((s, D_MODEL), F32)
    return _call(
        body, name="branch_merge_fwd", grid=(nblk,),
        in_specs=[res, res, wcol, wcol, blk(5), blk(6)],
        out_specs=[out] * 3,
        out_shape=[f, f, jax.ShapeDtypeStruct((s, D_MODEL), BF16)],
        operands=(ya, yb, wcb, wlb, proj, proj), ride=ride)


def mix_out_fwd(merged, wout, x, g2, g3, ride=None):
    s, d = x.shape
    t = _token_tile(s)

    def body(m_ref, w_ref, x_ref, g2_ref, g3_ref, mix_ref, x2_ref, h2_ref, h2t_ref):
        mix = _dot(m_ref[...], w_ref[...])
        mix_ref[...] = mix
        n, _ = _rms_stats(mix)
        x2 = x_ref[...] + n * g2_ref[...]
        x2_ref[...] = x2
        n2, _ = _rms_stats(x2)
        h2 = n2 * g3_ref[...]
        h2_ref[...] = h2.astype(BF16)
        h2t_ref[...] = h2.T.astype(BF16)

    tile = pl.BlockSpec((t, d), lambda i: (i, 0))
    vec = pl.BlockSpec((1, d), lambda i: (0, 0))
    f = jax.ShapeDtypeStruct((s, d), F32)
    return _call(
        body, name="mix_out_fwd", grid=(s // t,),
        in_specs=[tile, pl.BlockSpec((d, d), lambda i: (0, 0)), tile, vec, vec],
        out_specs=[tile] * 3 + [pl.BlockSpec((d, t), lambda i: (0, i))],
        out_shape=[f, f, jax.ShapeDtypeStruct((s, d), BF16), jax.ShapeDtypeStruct((d, s), BF16)],
        operands=(merged, wout, x, g2, g3), ride=ride)


def ffn_up_act_fwd(h2, wup4, fw, fb, ride=None):
    s, k = h2.shape
    ns = wup4.shape[2]
    per_chip = ns // CW
    nblk = D_FF // CW

    def body(h_ref, wg_ref, wv_ref, cg_ref, cv_ref, bg_ref, bv_ref, up_ref, act_ref, f_ref):
        h = h_ref[...]
        ug = _dot(h, wg_ref[0])
        uv = _dot(h, wv_ref[0])
        up_ref[0] = ug
        up_ref[1] = uv
        gate = _causal_conv(ug, cg_ref, bg_ref[...])
        val = _causal_conv(uv, cv_ref, bv_ref[...])
        act_ref[0] = gate.astype(BF16)
        act_ref[1] = val.astype(BF16)
        f_ref[...] = (_gelu(gate) * val).astype(BF16)

    wcols = lambda h: pl.BlockSpec((1, k, CW), lambda n, h=h: (n // per_chip + 2 * h, 0, n % per_chip))
    half = lambda h, rows: pl.BlockSpec((rows, CW), lambda n, h=h: (0, h * nblk + n))
    both = pl.BlockSpec((2, s, CW), lambda n: (0, 0, n))
    return _call(
        body, name="ffn_up_act_fwd", grid=(nblk,),
        in_specs=[pl.BlockSpec((s, k), lambda n: (0, 0)), wcols(0), wcols(1),
                  half(0, 3), half(1, 3), half(0, 1), half(1, 1)],
        out_specs=[both, both, pl.BlockSpec((s, CW), lambda n: (0, n))],
        out_shape=[jax.ShapeDtypeStruct((2, s, D_FF), F32), jax.ShapeDtypeStruct((2, s, D_FF), BF16),
                   jax.ShapeDtypeStruct((s, D_FF), BF16)],
        operands=(h2, wup4, wup4, fw, fw, fb, fb), ride=ride)


def ffn_down_loss(f, wdown, x2, target, g4):
    s, d = x2.shape
    t = _token_tile(s)

    def body(f_ref, w_ref, x2_ref, tg_ref, g4_ref, dy_ref, dout_ref, loss_ref, dg4_ref):
        @pl.when(pl.program_id(0) == 0)
        def _():
            loss_ref[...] = jnp.zeros_like(loss_ref)
            dg4_ref[...] = jnp.zeros_like(dg4_ref)

        out = _dot(f_ref[...], w_ref[...])
        n, r = _rms_stats(out)
        err = x2_ref[...] + n * g4_ref[...] - tg_ref[...]
        loss_ref[...] += jnp.full(loss_ref.shape, (0.5 / d) * jnp.sum(err * err), F32)
        dy = err * (1.0 / d)
        dy_ref[...] = dy
        dout, dg = _rms_bwd(n, r, g4_ref[...], dy)
        dout_ref[...] = dout.astype(BF16)
        dg4_ref[...] += jnp.sum(dg, axis=0, keepdims=True)

    tile = pl.BlockSpec((t, d), lambda i: (i, 0))
    vec = pl.BlockSpec((1, d), lambda i: (0, 0))
    return pl.pallas_call(
        body, name="ffn_down_loss", grid=(s // t,),
        in_specs=[pl.BlockSpec((t, D_FF), lambda i: (i, 0)), pl.BlockSpec((D_FF, d), lambda i: (0, 0)), tile, tile, vec],
        out_specs=[tile, tile, pl.BlockSpec((1, 128), lambda i: (0, 0)), vec],
        out_shape=[jax.ShapeDtypeStruct((s, d), F32), jax.ShapeDtypeStruct((s, d), BF16),
                   jax.ShapeDtypeStruct((1, 128), F32), jax.ShapeDtypeStruct((1, d), F32)],
        compiler_params=_params(),
    )(f, wdown, x2, target, g4)


def ffn_bwd(dout, wdown, up, act, f, fw):
    s = up.shape[1]
    nblk = D_FF // FW

    def body(do_ref, wd_ref, up_ref, act_ref, f_ref, wg_ref, wv_ref, dup_ref, dwd_ref, dw_ref, db_ref):
        do = do_ref[...]
        df = _dot_nt(do, wd_ref[...])
        dwd_ref[...] = _dot_tn(f_ref[...], do).astype(BF16)
        val = act_ref[1].astype(F32)
        ge, dge = _gelu_and_grad(act_ref[0].astype(F32))
        dgate = _advances(df * val * dge, 3)
        dval = _advances(df * ge, 3)
        dup_ref[0] = _taps_sum(dgate, wg_ref).astype(BF16)
        dup_ref[1] = _taps_sum(dval, wv_ref).astype(BF16)
        dw_ref[0] = _conv_wgrad(dgate, up_ref[0])
        dw_ref[1] = _conv_wgrad(dval, up_ref[1])
        db_ref[0] = jnp.sum(dgate[0], axis=0, keepdims=True)
        db_ref[1] = jnp.sum(dval[0], axis=0, keepdims=True)

    half = lambda h, rows: pl.BlockSpec((rows, FW), lambda n, h=h: (0, h * nblk + n))
    both = lambda rows: pl.BlockSpec((2, rows, FW), lambda n: (0, 0, n))
    return pl.pallas_call(
        body, name="ffn_bwd", grid=(nblk,),
        in_specs=[pl.BlockSpec((s, D_MODEL), lambda n: (0, 0)), pl.BlockSpec((FW, D_MODEL), lambda n: (n, 0)),
                  both(s), both(s), pl.BlockSpec((s, FW), lambda n: (0, n)), half(0, 3), half(1, 3)],
        out_specs=[both(s), pl.BlockSpec((FW, D_MODEL), lambda n: (n, 0)), both(3), both(1)],
        out_shape=[jax.ShapeDtypeStruct((2, s, D_FF), BF16), jax.ShapeDtypeStruct((D_FF, D_MODEL), BF16),
                   jax.ShapeDtypeStruct((2, 3, D_FF), F32), jax.ShapeDtypeStruct((2, 1, D_FF), F32)],
        compiler_params=_params(),
    )(dout, wdown, up, act, f, fw, fw)


def ffn_up_bwd(dout, wdown, up, act, f, fw, wup4, h2t, ride=None):
    k, s = h2t.shape
    nblk = D_FF // FW
    per_chip = wup4.shape[2] // FW

    def body(do_ref, wd_ref, up_ref, act_ref, f_ref, cg_ref, cv_ref, wg_ref, wv_ref, h_ref,
             dh_ref, dwu_ref, dwd_ref, dw_ref, db_ref, dup_scr):
        @pl.when(pl.program_id(0) == 0)
        def _():
            dup_scr[...] = jnp.zeros_like(dup_scr)
            dh_ref[...] = jnp.zeros_like(dh_ref)

        do = do_ref[...]
        df = _dot_nt(do, wd_ref[...])
        dg = dup_scr[0]
        dv = dup_scr[1]
        ht = h_ref[...]
        dh_ref[...] += _dot_nt(dg, wg_ref[0]) + _dot_nt(dv, wv_ref[0])
        dwu_ref[0] = _dot(ht, dg).astype(BF16)
        dwu_ref[1] = _dot(ht, dv).astype(BF16)
        dwd_ref[...] = _dot_tn(f_ref[...], do).astype(BF16)
        val = act_ref[1].astype(F32)
        ge, dge = _gelu_and_grad(act_ref[0].astype(F32))
        dgate = _advances(df * val * dge, 3)
        dval = _advances(df * ge, 3)
        dw_ref[0] = _conv_wgrad(dgate, up_ref[0])
        dw_ref[1] = _conv_wgrad(dval, up_ref[1])
        db_ref[0] = jnp.sum(dgate[0], axis=0, keepdims=True)
        db_ref[1] = jnp.sum(dval[0], axis=0, keepdims=True)
        dup_scr[0] = _taps_sum(dgate, cg_ref).astype(BF16)
        dup_scr[1] = _taps_sum(dval, cv_ref).astype(BF16)

    cur = lambda n: jnp.minimum(n, nblk - 1)
    prev = lambda n: jnp.maximum(n - 1, 0)
    once = pl.Buffered(1)
    both = lambda rows: pl.BlockSpec((2, rows, FW), lambda n: (0, 0, cur(n)))
    taps = lambda h: pl.BlockSpec((3, FW), lambda n, h=h: (0, h * nblk + cur(n)))
    wcols = lambda h: pl.BlockSpec((1, k, FW), lambda n, h=h: (prev(n) // per_chip + 2 * h, 0, prev(n) % per_chip))
    return _call(
        body, name="ffn_up_bwd", grid=(nblk + 1,),
        in_specs=[pl.BlockSpec((s, D_MODEL), lambda n: (0, 0), pipeline_mode=once),
                  pl.BlockSpec((FW, D_MODEL), lambda n: (cur(n), 0)), both(s), both(s),
                  pl.BlockSpec((s, FW), lambda n: (0, cur(n))), taps(0), taps(1), wcols(0), wcols(1),
                  pl.BlockSpec((k, s), lambda n: (0, 0), pipeline_mode=once)],
        out_specs=[pl.BlockSpec((s, k), lambda n: (0, 0), pipeline_mode=once),
                   pl.BlockSpec((2, k, FW), lambda n: (0, 0, prev(n))),
                   pl.BlockSpec((FW, D_MODEL), lambda n: (cur(n), 0)), both(3), both(1)],
        out_shape=[jax.ShapeDtypeStruct((s, k), F32), jax.ShapeDtypeStruct((2, k, D_FF), BF16),
                   jax.ShapeDtypeStruct((D_FF, D_MODEL), BF16),
                   jax.ShapeDtypeStruct((2, 3, D_FF), F32), jax.ShapeDtypeStruct((2, 1, D_FF), F32)],
        scratch_shapes=[pltpu.VMEM((2, s, FW), BF16)],
        operands=(dout, wdown, up, act, f, fw, fw, wup4, wup4, h2t), ride=ride)


def matmul_cols_bwd(dy, other, name, wgrad, ride=None):
    m = dy[0].shape[1]
    if wgrad:
        k = other.shape[0]
        nj, nb = N_CHIPS, sum(d.shape[0] * d.shape[2] for d in dy) // (N_CHIPS * CW)
    else:
        nj, k, ns = other.shape
        nb = ns // CW
    per_seg = dy[0].shape[2] // CW
    first = [sum(d.shape[0] for d in dy[:i]) for i in range(len(dy))]

    def segment(j, b):
        return (j * nb + b) // per_seg, (j * nb + b) % per_seg

    def body(*refs):
        dy_refs, (o_ref, r_ref) = refs[:len(dy)], refs[len(dy):]
        seg, _ = segment(pl.program_id(0), pl.program_id(1))
        dyb = dy_refs[-1][0]
        for i in range(len(dy) - 2, -1, -1):
            dyb = jnp.where(seg < first[i + 1], dy_refs[i][0], dyb)
        if wgrad:
            r_ref[...] = _dot(o_ref[...], dyb).astype(BF16)
        else:
            @pl.when((pl.program_id(0) == 0) & (pl.program_id(1) == 0))
            def _():
                r_ref[...] = jnp.zeros_like(r_ref)

            r_ref[...] += _dot_nt(dyb, o_ref[0])

    def dy_spec(i):
        nseg = dy[i].shape[0]

        def index(j, b):
            seg, col = segment(j, b)
            local = seg - first[i]
            return (jnp.clip(local, 0, nseg - 1), 0,
                    jnp.where(local < 0, 0, jnp.where(local >= nseg, per_seg - 1, col)))

        return pl.BlockSpec((1, m, CW), index)

    if wgrad:
        other_spec = pl.BlockSpec((k, m), lambda j, b: (0, 0))
        out_spec = pl.BlockSpec((k, CW), lambda j, b: (0, j * nb + b))
        out_shape = jax.ShapeDtypeStruct((k, nj * nb * CW), BF16)
    else:
        other_spec = pl.BlockSpec((1, k, CW), lambda j, b: (j, 0, b))
        out_spec = pl.BlockSpec((m, k), lambda j, b: (0, 0))
        out_shape = jax.ShapeDtypeStruct((m, k), F32)
    return _call(
        body, name=name, grid=(nj, nb), in_specs=[dy_spec(i) for i in range(len(dy))] + [other_spec],
        out_specs=[out_spec], out_shape=[out_shape], operands=(*dy, other), ride=ride)


def norms_mid_bwd(dh2, x2, dy, mix, g3, g2, ride=None):
    s, d = x2.shape
    t = _token_tile(s)

    def body(dh2_ref, x2_ref, dy_ref, mix_ref, g3_ref, g2_ref, dx2_ref, dmix_ref, dg3_ref, dg2_ref):
        @pl.when(pl.program_id(0) == 0)
        def _():
            dg3_ref[...] = jnp.zeros_like(dg3_ref)
            dg2_ref[...] = jnp.zeros_like(dg2_ref)

        n3, r3 = _rms_stats(x2_ref[...])
        dx, dg3 = _rms_bwd(n3, r3, g3_ref[...], dh2_ref[...])
        dx2 = dy_ref[...] + dx
        dx2_ref[...] = dx2
        dg3_ref[...] += jnp.sum(dg3, axis=0, keepdims=True)
        n2, r2 = _rms_stats(mix_ref[...])
        dmix, dg2 = _rms_bwd(n2, r2, g2_ref[...], dx2)
        dmix_ref[...] = dmix.astype(BF16)
        dg2_ref[...] += jnp.sum(dg2, axis=0, keepdims=True)

    tile = pl.BlockSpec((t, d), lambda i: (i, 0))
    vec = pl.BlockSpec((1, d), lambda i: (0, 0))
    v = jax.ShapeDtypeStruct((1, d), F32)
    return _call(
        body, name="norms_mid_bwd", grid=(s // t,),
        in_specs=[tile, tile, tile, tile, vec, vec],
        out_specs=[tile, tile, vec, vec],
        out_shape=[jax.ShapeDtypeStruct((s, d), F32), jax.ShapeDtypeStruct((s, d), BF16), v, v],
        operands=(dh2, x2, dy, mix, g3, g2), ride=ride)


def mix_out_bwd(dmix, wout, merged, a, b, proj, ride=None):
    s = dmix.shape[0]
    nblk = D_MODEL // CW

    def body(dm_ref, w_ref, mg_ref, a_ref, b_ref, gc_ref, gl_ref, da_ref, db_ref, dw_ref, dg_ref):
        dm = dm_ref[...]
        dmerged = _dot_nt(dm, w_ref[...])
        dw_ref[...] = _dot_tn(mg_ref[...], dm).astype(BF16)
        sc = _sigmoid(gc_ref[...])
        sl = _sigmoid(gl_ref[...])
        da_ref[...] = (dmerged * sc).astype(BF16)
        db_ref[...] = (dmerged * sl).astype(BF16)
        dg_ref[0] = (dmerged * a_ref[...] * sc * (1.0 - sc)).astype(BF16)
        dg_ref[1] = (dmerged * b_ref[...] * sl * (1.0 - sl)).astype(BF16)

    res = pl.BlockSpec((s, D_MODEL), lambda n: (0, 0))
    rows = pl.BlockSpec((CW, D_MODEL), lambda n: (n, 0))
    col = pl.BlockSpec((s, CW), lambda n: (0, n))
    blk = lambda k: pl.BlockSpec((s, CW), lambda n, k=k: (0, k * nblk + n))
    hb = jax.ShapeDtypeStruct((s, D_MODEL), BF16)
    return _call(
        body, name="mix_out_bwd", grid=(nblk,),
        in_specs=[res, rows, col, col, col, blk(5), blk(6)],
        out_specs=[col, col, rows, pl.BlockSpec((2, s, CW), lambda n: (0, 0, n))],
        out_shape=[hb, hb, jax.ShapeDtypeStruct((D_MODEL, D_MODEL), BF16), jax.ShapeDtypeStruct((2, s, D_MODEL), BF16)],
        operands=(dmix, wout, merged, a, b, proj, proj), ride=ride)


def mix_conv_bwd(da, wcb, proj, q, ws, ride=None):
    s = da.shape[0]
    nblk = D_MODEL // CW

    def body(da_ref, w_ref, cb_ref, cc_ref, cx_ref, q_ref, ws_ref, dc_ref, dw_ref, dws_ref):
        dab = da_ref[...]
        dya = _dot_nt(dab, w_ref[...])
        cb = cb_ref[...]
        cc = cc_ref[...]
        cx = cx_ref[...]
        q = q_ref[...]
        dw_ref[...] = _dot_tn((cb * q).astype(BF16), dab).astype(BF16)
        dc_ref[0] = (dya * q).astype(BF16)
        dq = _advances(dya * cb, 3)
        dp = _taps_sum(dq, ws_ref)
        dws_ref[...] = _conv_wgrad(dq, cc * cx)
        dc_ref[1] = (dp * cx).astype(BF16)
        dc_ref[2] = (dp * cc).astype(BF16)

    res = pl.BlockSpec((s, D_MODEL), lambda n: (0, 0))
    rows = pl.BlockSpec((CW, D_MODEL), lambda n: (n, 0))
    col = pl.BlockSpec((s, CW), lambda n: (0, n))
    blk = lambda k: pl.BlockSpec((s, CW), lambda n, k=k: (0, k * nblk + n))
    taps = pl.BlockSpec((3, CW), lambda n: (0, n))
    hb = jax.ShapeDtypeStruct((s, D_MODEL), BF16)
    return _call(
        body, name="mix_conv_bwd", grid=(nblk,),
        in_specs=[res, rows, blk(0), blk(1), blk(2), col, taps],
        out_specs=[pl.BlockSpec((3, s, CW), lambda n: (0, 0, n)), rows, taps],
        out_shape=[jax.ShapeDtypeStruct((3, s, D_MODEL), BF16), jax.ShapeDtypeStruct((D_MODEL, D_MODEL), BF16),
                   jax.ShapeDtypeStruct((3, D_MODEL), F32)],
        operands=(da, wcb, proj, proj, proj, q, ws), ride=ride)


def mix_lru_bwd(db, wlb, proj, xl, r, i, h, wl, wa, wx, lam, ride=None):
    s = db.shape[0]
    nblk = D_MODEL // CW

    def body(db_ref, w_ref, lx_ref, ly_ref, xl_ref, r_ref, i_ref, h_ref, wl_ref, wa_ref, wx_ref, lam_ref,
             dl_ref, dw_ref, dwa_ref, dwx_ref, dba_ref, dbx_ref, dwl_ref, dbl_ref, dlam_ref,
             c_scr, g_scr):
        dbb = db_ref[...]
        dyb = _dot_nt(dbb, w_ref[...])
        h = h_ref[...]
        ge, dge = _gelu_and_grad(ly_ref[...])
        dw_ref[...] = _dot_tn((h * ge).astype(BF16), dbb).astype(BF16)
        dl_ref[1] = (dyb * h * dge).astype(BF16)
        r = r_ref[...].astype(F32)
        gi = i_ref[...].astype(F32)
        xlb = xl_ref[...]
        xl = xlb.astype(F32)
        lam = lam_ref[...]
        ls = _log_sigmoid(lam)
        a, mult = _lru_gates(r, ls)
        c_scr[...] = _shift_up(a, 1)
        g_scr[...] = dyb * ge
        _scan_backward(c_scr, g_scr, g_scr)
        du = g_scr[...]
        da = du * _shift_down(h, 1)
        dmult = du * gi * xl
        di = du * mult * xl
        dxl = du * mult * gi
        first = _rows(a.shape) == 0
        dlog_a = da * a - jnp.where(first, 0.0, dmult * a * a / mult)
        dr = dlog_a * (LRU_C * ls)
        dlam_ref[...] = jnp.sum(dlog_a * r, axis=0, keepdims=True) * (LRU_C * (1.0 - _sigmoid(lam)))
        dzr = dr * r * (1.0 - r)
        dzi = di * gi * (1.0 - gi)
        dba_ref[...] = jnp.sum(dzr, axis=0, keepdims=True)
        dbx_ref[...] = jnp.sum(dzi, axis=0, keepdims=True)
        dzrb = dzr.astype(BF16)
        dzib = dzi.astype(BF16)
        dwa_ref[0] = _dot_tn(xlb, dzrb)
        dwx_ref[0] = _dot_tn(xlb, dzib)
        dxl = _advances(dxl + _dot_nt(dzrb, wa_ref[0]) + _dot_nt(dzib, wx_ref[0]), 4)
        dl_ref[0] = _taps_sum(dxl, wl_ref).astype(BF16)
        dwl_ref[...] = _conv_wgrad(dxl, lx_ref[...])
        dbl_ref[...] = jnp.sum(dxl[0], axis=0, keepdims=True)

    res = pl.BlockSpec((s, D_MODEL), lambda n: (0, 0), pipeline_mode=pl.Buffered(1))
    rows = pl.BlockSpec((CW, D_MODEL), lambda n: (n, 0))
    col = pl.BlockSpec((s, CW), lambda n: (0, n))
    blk = lambda k: pl.BlockSpec((s, CW), lambda n, k=k: (0, k * nblk + n))
    taps = pl.BlockSpec((4, CW), lambda n: (0, n))
    vec = pl.BlockSpec((1, CW), lambda n: (0, n))
    mat = pl.BlockSpec((1, CW, CW), lambda n: (n, 0, 0))
    hb = jax.ShapeDtypeStruct((s, D_MODEL), BF16)
    v = jax.ShapeDtypeStruct((1, D_MODEL), F32)
    m = jax.ShapeDtypeStruct((LRU_HEADS, HEAD_DIM, HEAD_DIM), F32)
    scr = pltpu.VMEM((s, CW), F32)
    return _call(
        body, name="mix_lru_bwd", grid=(nblk,),
        in_specs=[res, rows, blk(3), blk(4), col, col, col, col, taps, mat, mat, vec],
        out_specs=[pl.BlockSpec((2, s, CW), lambda n: (0, 0, n)), rows, mat, mat, vec, vec, taps, vec, vec],
        out_shape=[jax.ShapeDtypeStruct((2, s, D_MODEL), BF16), jax.ShapeDtypeStruct((D_MODEL, D_MODEL), BF16), m, m, v, v,
                   jax.ShapeDtypeStruct((4, D_MODEL), F32), v, v],
        scratch_shapes=[scr, scr],
        operands=(db, wlb, proj, proj, xl, r, i, h, wl, wa, wx, lam), ride=ride)


def norm_in_bwd(dh1, x, dx2, g1, ride=None):
    s, d = x.shape
    t = _token_tile(s)

    def body(dh_ref, x_ref, dx2_ref, g_ref, dx_ref, dg_ref):
        @pl.when(pl.program_id(0) == 0)
        def _():
            dg_ref[...] = jnp.zeros_like(dg_ref)

        n, r = _rms_stats(x_ref[...])
        dx, dg = _rms_bwd(n, r, g_ref[...], dh_ref[...])
        dx_ref[...] = dx2_ref[...] + dx
        dg_ref[...] += jnp.sum(dg, axis=0, keepdims=True)

    tile = pl.BlockSpec((t, d), lambda i: (i, 0))
    vec = pl.BlockSpec((1, d), lambda i: (0, 0))
    return _call(
        body, name="norm_in_bwd", grid=(s // t,),
        in_specs=[tile, tile, tile, vec],
        out_specs=[tile, vec],
        out_shape=[jax.ShapeDtypeStruct((s, d), F32), jax.ShapeDtypeStruct((1, d), F32)],
        operands=(dh1, x, dx2, g1), ride=ride)


def local_step(x, target, g1, g2, g3, g4, win4, ws, wcb, wl, bl, wa, ba, wx, bx, lam, wlb, wout, wup4, fw, fb, wdown):
    h1 = norm_in(x, g1)
    proj = matmul_cols(h1, win4, "proj_fwd")
    q, ya = mix_conv_fwd(proj, ws)
    xl, r, gi, h, yb = mix_lru_fwd(proj, wl, bl, wa, ba, wx, bx, lam)
    a, b, merged = branch_merge_fwd(ya, yb, wcb, wlb, proj)
    mix, x2, h2 = mix_out_fwd(merged, wout, x, g2, g3)
    up = matmul_cols(h2, wup4, "up_fwd")
    f = ffn_act_fwd(up, fw, fb)
    dy, dout, loss, dg4 = ffn_down_loss(f, wdown, x2, target, g4)

    dug, duv, dwdown, dfw_g, dfw_v, dfb_g, dfb_v = ffn_bwd(dout, wdown, up, fw, fb)
    dup = jnp.concatenate([dug, duv], axis=1)
    dfw = jnp.concatenate([dfw_g, dfw_v], axis=1)
    dfb = jnp.concatenate([dfb_g, dfb_v], axis=1)
    dh2, dwup = dgrad_wgrad_cols(dup, wup4, h2, "up_bwd")
    dx2, dmix, dg3, dg2 = norms_mid_bwd(dh2, x2, dy, mix, g3, g2)
    da, db, dwout, dgc, dgl = mix_out_bwd(dmix, wout, merged, a, b, proj)
    dcb, dcc, dcx, dwcb, dws = mix_conv_bwd(da, wcb, proj, q, ws)
    dlx, dly, dwlb, dwa, dwx, dba, dbx, dwl, dbl, dlam = mix_lru_bwd(db, wlb, proj, xl, r, gi, h, wl, wa, wx, lam)
    dproj = jnp.concatenate([dcb, dcc, dcx, dlx, dly, dgc[:, 5 * D_MODEL:6 * D_MODEL], dgl[:, 6 * D_MODEL:]], axis=1)
    dh1, dwin = dgrad_wgrad_cols(dproj, win4, h1, "proj_bwd")
    dx, dg1 = norm_in_bwd(dh1, x, dx2, g1)
    grads = dict(norm_mix_pre=dg1, norm_mix_post=dg2, norm_ffn_pre=dg3, norm_ffn_post=dg4,
                 w_in=dwin, conv_short_w=dws, w_conv_branch=dwcb, lru_conv_w=dwl, lru_conv_b=dbl,
                 lru_wa=dwa, lru_ba=dba, lru_wx=dwx, lru_bx=dbx, lru_lambda=dlam,
                 w_lru_branch=dwlb, w_out=dwout, ffn_w_up=dwup, ffn_conv_w=dfw, ffn_conv_b=dfb,
                 ffn_w_down=dwdown)
    return loss[0, 0], dx, grads


MESH = pl.DeviceIdType.MESH
_HBM = pl.BlockSpec(memory_space=pltpu.HBM)
_OTHER_CHIPS = ((1, 0), (0, 1), (1, 1))
_OTHER_DEVICES = tuple((dx, dy, dc) for dx in (0, 1) for dy in (0, 1) for dc in (0, 1) if dx or dy or dc)
N_DEVICES = 8


def _position():
    return lax.axis_index("x"), lax.axis_index("y"), lax.axis_index("c")


def _flip(v, d):
    return 1 - v if d else v


def _half_rows(ref, h, hr):
    return ref.at[pl.ds(h * hr, hr), :]


def gather_chips(shards):
    n = len(shards)
    nrel = len(_OTHER_CHIPS)

    def body(*refs):
        ins, outs = refs[:n], refs[n:2 * n]
        ici_send, ici_recv, sib_send, sib_recv = refs[2 * n:]
        x, y, c = _position()
        j = 2 * x + y
        hr = [s.shape[0] // 2 for s in shards]

        def chip(p):
            px, py = _flip(x, _OTHER_CHIPS[p][0]), _flip(y, _OTHER_CHIPS[p][1])
            return px, py, 2 * px + py

        def ici(a, p, slot):
            px, py, _ = chip(p)
            return pltpu.make_async_remote_copy(
                src_ref=_half_rows(ins[a], c, hr[a]), dst_ref=_half_rows(outs[a].at[slot], c, hr[a]),
                send_sem=ici_send.at[a * nrel + p], recv_sem=ici_recv.at[a * nrel + p],
                device_id=(px, py, c), device_id_type=MESH)

        def sib(a, p, h):
            _, _, k = chip(p)
            part = _half_rows(outs[a].at[k], h, hr[a])
            return pltpu.make_async_remote_copy(
                src_ref=part, dst_ref=part, send_sem=sib_send.at[a * nrel + p], recv_sem=sib_recv.at[a * nrel + p],
                device_id=(x, y, 1 - c), device_id_type=MESH)

        pairs = [(a, p) for a in range(n) for p in range(nrel)]
        for a, p in pairs:
            ici(a, p, j).start()
        for a, p in pairs:
            ici(a, p, chip(p)[2]).wait_recv()
            sib(a, p, c).start()
        for a, p in pairs:
            sib(a, p, 1 - c).wait_recv()
        for a, p in pairs:
            ici(a, p, j).wait_send()
            sib(a, p, c).wait_send()

    got = pl.pallas_call(
        body, name="gather_chips",
        in_specs=[_HBM] * n, out_specs=[_HBM] * n,
        out_shape=[jax.ShapeDtypeStruct((N_CHIPS,) + s.shape, s.dtype) for s in shards],
        scratch_shapes=[pltpu.SemaphoreType.DMA((n * nrel,))] * 4,
    )(*shards)
    j = 2 * lax.axis_index("x") + lax.axis_index("y")
    return [lax.dynamic_update_slice(g, s[None], (j, 0, 0)) for g, s in zip(got, shards)]


def _owned_part(ref, kind, k, h, hr):
    if kind == "col":
        ns = ref.shape[1] // N_CHIPS
        return ref.at[pl.ds(h * hr, hr), pl.ds(k * ns, ns)]
    if kind == "row":
        return ref.at[pl.ds(k * 2 * hr + h * hr, hr), :]
    if kind == "col2":
        ns = ref.shape[2] // 2
        return ref.at[k // 2, pl.ds(h * hr, hr), pl.ds((k % 2) * ns, ns)]
    return ref.at[k, pl.ds(h * hr, hr), :]


def _part_shape(g, kind):
    if kind == "col2":
        return g.shape[1] // 2, g.shape[2] // 2
    if kind == "col":
        return g.shape[0] // 2, g.shape[1] // N_CHIPS
    if kind == "row":
        return g.shape[0] // (2 * N_CHIPS), g.shape[1]
    return g.shape[1] // 2, g.shape[2]


def pair_split(grads, kinds, name):
    n = len(grads)
    shapes = [_part_shape(g, k) for g, k in zip(grads, kinds)]

    def body(*refs):
        ins, theirs = refs[:n], refs[n:2 * n]
        send_sem, recv_sem = refs[2 * n:]
        x, y, c = _position()
        copies = []
        for a in range(n):
            hr = shapes[a][0]
            for k in range(N_CHIPS):
                s = a * N_CHIPS + k
                copies.append(pltpu.make_async_remote_copy(
                    src_ref=_owned_part(ins[a], kinds[a], k, 1 - c, hr), dst_ref=theirs[a].at[k],
                    send_sem=send_sem.at[s], recv_sem=recv_sem.at[s], device_id=(x, y, 1 - c), device_id_type=MESH))
        for cp in copies:
            cp.start()
        for cp in copies:
            cp.wait()

    return pl.pallas_call(
        body, name=name,
        in_specs=[_HBM] * n, out_specs=[_HBM] * n,
        out_shape=[jax.ShapeDtypeStruct((N_CHIPS,) + shp, g.dtype) for shp, g in zip(shapes, grads)],
        scratch_shapes=[pltpu.SemaphoreType.DMA((n * N_CHIPS,))] * 2,
    )(*grads)


def chip_exchange(sums, rep):
    n = len(sums)
    nrel = len(_OTHER_CHIPS)
    ndev = len(_OTHER_DEVICES)

    def body(*refs):
        ins, rep_ref = refs[:n], refs[n]
        outs, rep_out = refs[n + 1:2 * n + 1], refs[2 * n + 1]
        loc_sem, send_sem, recv_sem, rep_send, rep_recv = refs[2 * n + 2:]
        x, y, c = _position()
        j = 2 * x + y
        me = 4 * x + 2 * y + c

        def chip(p):
            px, py = _flip(x, _OTHER_CHIPS[p][0]), _flip(y, _OTHER_CHIPS[p][1])
            return px, py, 2 * px + py

        def part(a, p, src_slot, dst_slot):
            px, py, _ = chip(p)
            return pltpu.make_async_remote_copy(
                src_ref=ins[a].at[src_slot], dst_ref=outs[a].at[dst_slot],
                send_sem=send_sem.at[a * nrel + p], recv_sem=recv_sem.at[a * nrel + p],
                device_id=(px, py, c), device_id_type=MESH)

        def device(q):
            dx, dy, dc = _OTHER_DEVICES[q]
            return _flip(x, dx), _flip(y, dy), _flip(c, dc)

        def rep_copy(q, slot):
            return pltpu.make_async_remote_copy(
                src_ref=rep_ref, dst_ref=rep_out.at[slot], send_sem=rep_send.at[q], recv_sem=rep_recv.at[q],
                device_id=device(q), device_id_type=MESH)

        own = [pltpu.make_async_copy(ins[a].at[j], outs[a].at[j], loc_sem.at[a]) for a in range(n)]
        own.append(pltpu.make_async_copy(rep_ref, rep_out.at[me], loc_sem.at[n]))
        for cp in own:
            cp.start()
        pairs = [(a, p) for a in range(n) for p in range(nrel)]
        for a, p in pairs:
            part(a, p, chip(p)[2], j).start()
        for q in range(ndev):
            rep_copy(q, me).start()
        for a, p in pairs:
            part(a, p, chip(p)[2], chip(p)[2]).wait_recv()
        for q in range(ndev):
            px, py, pc = device(q)
            rep_copy(q, 4 * px + 2 * py + pc).wait_recv()
        for a, p in pairs:
            part(a, p, chip(p)[2], j).wait_send()
        for q in range(ndev):
            rep_copy(q, me).wait_send()
        for cp in own:
            cp.wait()

    return pl.pallas_call(
        body, name="chip_exchange",
        in_specs=[_HBM] * (n + 1), out_specs=[_HBM] * (n + 1),
        out_shape=[jax.ShapeDtypeStruct(s.shape, s.dtype) for s in sums]
        + [jax.ShapeDtypeStruct((N_DEVICES,) + rep.shape, rep.dtype)],
        scratch_shapes=[pltpu.SemaphoreType.DMA((n + 1,)), pltpu.SemaphoreType.DMA((n * nrel,)),
                        pltpu.SemaphoreType.DMA((n * nrel,)), pltpu.SemaphoreType.DMA((ndev,)),
                        pltpu.SemaphoreType.DMA((ndev,))],
    )(*sums, rep)


def pair_swap(halves):
    n = len(halves)

    def body(*refs):
        ins, outs = refs[:n], refs[n:2 * n]
        send_sem, recv_sem = refs[2 * n:]
        x, y, c = _position()
        copies = [pltpu.make_async_remote_copy(
            src_ref=ins[a], dst_ref=outs[a], send_sem=send_sem.at[a], recv_sem=recv_sem.at[a],
            device_id=(x, y, 1 - c), device_id_type=MESH) for a in range(n)]
        for cp in copies:
            cp.start()
        for cp in copies:
            cp.wait()

    return pl.pallas_call(
        body, name="pair_swap",
        in_specs=[_HBM] * n, out_specs=[_HBM] * n,
        out_shape=[jax.ShapeDtypeStruct(h.shape, h.dtype) for h in halves],
        scratch_shapes=[pltpu.SemaphoreType.DMA((n,))] * 2,
    )(*halves)


def _row_tile(rows, cols, limit_bytes=1 << 20):
    best = None
    for t in range(SUBLANES, rows + 1, SUBLANES):
        if rows % t == 0 and t * cols * 4 <= limit_bytes:
            best = t
    return best or rows


def add_pair(g, kind, theirs, core, name):
    nc, rows, cols = theirs.shape
    t = _row_tile(rows, cols, 4 << 20)
    nt = rows // t

    def body(core_ref, g_ref, b_ref, o_ref):
        mine = g_ref[...].reshape(t, cols)
        o_ref[0] = (mine.astype(F32) + b_ref[0].astype(F32)).astype(o_ref.dtype)

    if kind == "col":
        own = pl.BlockSpec((t, cols), lambda k, i, c: (c[0] * nt + i, k))
    elif kind == "col2":
        own = pl.BlockSpec((1, t, cols), lambda k, i, c: (k // 2, c[0] * nt + i, k % 2))
    elif kind == "row":
        own = pl.BlockSpec((t, cols), lambda k, i, c: ((2 * k + c[0]) * nt + i, 0))
    else:
        own = pl.BlockSpec((1, t, cols), lambda k, i, c: (k, c[0] * nt + i, 0))
    spec = pl.BlockSpec((1, t, cols), lambda k, i, c: (k, i, 0))
    return pl.pallas_call(
        body, name=name,
        grid_spec=pltpu.PrefetchScalarGridSpec(num_scalar_prefetch=1, grid=(nc, nt), in_specs=[own, spec], out_specs=spec),
        out_shape=jax.ShapeDtypeStruct(theirs.shape, theirs.dtype), compiler_params=_params(),
    )(core, g, theirs)


def sum_lead(a, name):
    nl, rows, cols = a.shape
    t = _row_tile(rows, cols, (1 << 20) // 2)

    def body(a_ref, o_ref):
        acc = a_ref[0].astype(F32)
        for s in range(1, nl):
            acc = acc + a_ref[s].astype(F32)
        o_ref[...] = acc

    return pl.pallas_call(
        body, name=name, grid=(rows // t,),
        in_specs=[pl.BlockSpec((nl, t, cols), lambda i: (0, i, 0))],
        out_specs=pl.BlockSpec((t, cols), lambda i: (i, 0)),
        out_shape=jax.ShapeDtypeStruct((rows, cols), F32), compiler_params=_params(),
    )(a)


def sum_chips(rx, csum, chip, name):
    nc, rows, cols = rx.shape
    t = _row_tile(rows, cols, 2 << 20)

    def body(chip_ref, r0, r1, r2, r3, own_ref, o_ref):
        acc = None
        for s, ref in enumerate((r0, r1, r2, r3)):
            term = jnp.where(chip_ref[0] == s, own_ref[0], ref[0]).astype(F32)
            acc = term if acc is None else acc + term
        o_ref[...] = acc

    def slot(s):
        return pl.BlockSpec((1, t, cols), lambda i, c, s=s: (jnp.where(c[0] == s, c[0] ^ 1, s), i, 0))

    return pl.pallas_call(
        body, name=name,
        grid_spec=pltpu.PrefetchScalarGridSpec(
            num_scalar_prefetch=1, grid=(rows // t,),
            in_specs=[slot(s) for s in range(nc)] + [pl.BlockSpec((1, t, cols), lambda i, c: (c[0], i, 0))],
            out_specs=pl.BlockSpec((t, cols), lambda i, c: (i, 0))),
        out_shape=jax.ShapeDtypeStruct((rows, cols), F32), compiler_params=_params(),
    )(chip, rx, rx, rx, rx, csum)


def _adamw_update(w, g, m, v):
    nm = ADAM_B1 * m + (1.0 - ADAM_B1) * g
    nv = ADAM_B2 * v + (1.0 - ADAM_B2) * (g * g)
    m_hat = nm * (1.0 / (1.0 - ADAM_B1 ** ADAM_STEP))
    v_hat = nv * (1.0 / (1.0 - ADAM_B2 ** ADAM_STEP))
    return -ADAM_LR * (m_hat / (jnp.sqrt(v_hat) + ADAM_EPS) + ADAM_WD * w), nm, nv


def adamw(w, g, m, v, name):
    rows, cols = w.shape
    t = _row_tile(rows, cols)

    def body(w_ref, g_ref, m_ref, v_ref, d_ref, nm_ref, nv_ref):
        d_ref[...], nm_ref[...], nv_ref[...] = _adamw_update(w_ref[...], g_ref[...], m_ref[...], v_ref[...])

    spec = pl.BlockSpec((t, cols), lambda i: (i, 0))
    shp = jax.ShapeDtypeStruct((rows, cols), F32)
    return pl.pallas_call(
        body, name=name, grid=(rows // t,), in_specs=[spec] * 4, out_specs=[spec] * 3,
        out_shape=[shp, shp, shp], compiler_params=_params(),
    )(w, g, m, v)


def adamw_halves(w, g_mine, g_other, m, v, core, name):
    rows, cols = w.shape
    hr = rows // 2
    t = _row_tile(hr, cols)
    nt = hr // t

    def body(core_ref, w_ref, gm_ref, go_ref, m_ref, v_ref, g_ref, d_ref, nm_ref, nv_ref):
        g = jnp.where(pl.program_id(0) // nt == core_ref[0], gm_ref[...], go_ref[...])
        g_ref[...] = g
        d_ref[...], nm_ref[...], nv_ref[...] = _adamw_update(w_ref[...], g, m_ref[...], v_ref[...])

    spec = pl.BlockSpec((t, cols), lambda i, c: (i, 0))
    half = pl.BlockSpec((t, cols), lambda i, c: (i % nt, 0))
    shp = jax.ShapeDtypeStruct((rows, cols), F32)
    return pl.pallas_call(
        body, name=name,
        grid_spec=pltpu.PrefetchScalarGridSpec(num_scalar_prefetch=1, grid=(2 * nt,),
                                               in_specs=[spec, half, half, spec, spec], out_specs=[spec] * 4),
        out_shape=[shp] * 4, compiler_params=_params(),
    )(core, w, g_mine, g_other, m, v)


WEIGHTS = ("norm_mix_pre", "norm_mix_post", "norm_ffn_pre", "norm_ffn_post", "w_in", "conv_short_w",
           "w_conv_branch", "lru_conv_w", "lru_conv_b", "lru_wa", "lru_ba", "lru_wx", "lru_bx", "lru_lambda",
           "w_lru_branch", "w_out", "ffn_w_up", "ffn_conv_w", "ffn_conv_b", "ffn_w_down")
BIG = ("w_in", "ffn_w_up", "w_conv_branch", "w_lru_branch", "w_out", "ffn_w_down")
BIG_KIND = ("col", "col", "row", "row", "row", "row")
SMALL = ("conv_short_w", "lru_conv_w", "lru_wa", "lru_ba", "lru_wx", "lru_bx", "ffn_conv_w")
REPL = ("norm_mix_pre", "norm_mix_post", "norm_ffn_pre", "norm_ffn_post", "lru_conv_b", "lru_lambda", "ffn_conv_b")
PACK_W = 256
SMALL_ROWS = 576
REPL_ROWS = 16
LOSS_ROW = 12
FFN_SHARD = 2 * D_FF // N_CHIPS
QUARTER = HEAD_DIM // N_CHIPS
SMALL_PARTS = (("conv_short_w", 3, (1, 3, PACK_W)), ("lru_conv_w", 4, (1, 4, PACK_W)),
               ("lru_wa", LRU_HEADS * QUARTER, (1, LRU_HEADS, QUARTER, HEAD_DIM)), ("lru_ba", 1, (1, LRU_HEADS, QUARTER)),
               ("lru_wx", LRU_HEADS * QUARTER, (1, LRU_HEADS, QUARTER, HEAD_DIM)), ("lru_bx", 1, (1, LRU_HEADS, QUARTER)),
               ("ffn_conv_w", 3 * FFN_SHARD // PACK_W, (1, 3, FFN_SHARD)))


def _pad8(nr):
    return -(-nr // SUBLANES) * SUBLANES


def _pack_small_shard(p):
    rows = [jnp.pad(p[name].reshape(nr, PACK_W), ((0, _pad8(nr) - nr), (0, 0))) for name, nr, _ in SMALL_PARTS]
    used = sum(r.shape[0] for r in rows)
    return jnp.concatenate(rows + [jnp.zeros((SMALL_ROWS - used, PACK_W), F32)], axis=0)


def _unpack_small_shard(buf):
    out, r = {}, 0
    for name, nr, shape in SMALL_PARTS:
        out[name] = buf[r:r + nr].reshape(shape)
        r += _pad8(nr)
    return out


def _full_small(g4):
    per = [_unpack_small_shard(g4[k]) for k in range(N_CHIPS)]
    cat = lambda name, axis: jnp.concatenate([per[k][name][0] for k in range(N_CHIPS)], axis=axis)
    return dict(conv_short_w=cat("conv_short_w", 1), lru_conv_w=cat("lru_conv_w", 1),
                lru_wa=cat("lru_wa", 1), lru_ba=cat("lru_ba", 1).reshape(1, D_MODEL),
                lru_wx=cat("lru_wx", 1), lru_bx=cat("lru_bx", 1).reshape(1, D_MODEL),
                ffn_conv_w=cat("ffn_conv_w", 1))


def _split_small(full):
    shards = []
    for k in range(N_CHIPS):
        cols = lambda a, w: a[:, k * w:(k + 1) * w]
        q = slice(k * QUARTER, (k + 1) * QUARTER)
        shards.append(_pack_small_shard(dict(
            conv_short_w=cols(full["conv_short_w"], PACK_W), lru_conv_w=cols(full["lru_conv_w"], PACK_W),
            lru_wa=full["lru_wa"][:, q, :], lru_ba=full["lru_ba"].reshape(LRU_HEADS, HEAD_DIM)[:, q],
            lru_wx=full["lru_wx"][:, q, :], lru_bx=full["lru_bx"].reshape(LRU_HEADS, HEAD_DIM)[:, q],
            ffn_conv_w=cols(full["ffn_conv_w"], FFN_SHARD))))
    return jnp.stack(shards)


def _pack_repl(p, loss=None):
    rows = [p[n].reshape(-1, D_MODEL) for n in REPL]
    if loss is not None:
        rows.append(jnp.broadcast_to(loss.reshape(1, 1), (1, D_MODEL)))
    used = sum(r.shape[0] for r in rows)
    return jnp.concatenate(rows + [jnp.zeros((REPL_ROWS - used, D_MODEL), F32)], axis=0)


def _unpack_repl(buf):
    out, r = {}, 0
    for n in REPL:
        nr = (2 * D_FF // D_MODEL) if n == "ffn_conv_b" else 1
        out[n] = buf[r:r + nr].reshape(1, nr * D_MODEL)
        r += nr
    return out


def kernel(x, norm_mix_pre, norm_mix_post, norm_ffn_pre, norm_ffn_post, w_in, conv_short_w, w_conv_branch, lru_conv_w, lru_conv_b, lru_wa, lru_ba, lru_wx, lru_bx, lru_lambda, w_lru_branch, w_out, ffn_w_up, ffn_conv_w, ffn_conv_b, ffn_w_down, loss_target, m_norm_mix_pre, m_norm_mix_post, m_norm_ffn_pre, m_norm_ffn_post, m_w_in, m_conv_short_w, m_w_conv_branch, m_lru_conv_w, m_lru_conv_b, m_lru_wa, m_lru_ba, m_lru_wx, m_lru_bx, m_lru_lambda, m_w_lru_branch, m_w_out, m_ffn_w_up, m_ffn_conv_w, m_ffn_conv_b, m_ffn_w_down, v_norm_mix_pre, v_norm_mix_post, v_norm_ffn_pre, v_norm_ffn_post, v_w_in, v_conv_short_w, v_w_conv_branch, v_lru_conv_w, v_lru_conv_b, v_lru_wa, v_lru_ba, v_lru_wx, v_lru_bx, v_lru_lambda, v_w_lru_branch, v_w_out, v_ffn_w_up, v_ffn_conv_w, v_ffn_conv_b, v_ffn_w_down):
    given = dict(locals())
    w = {n: given[n] for n in WEIGHTS}
    m = {n: given["m_" + n] for n in WEIGHTS}
    v = {n: given["v_" + n] for n in WEIGHTS}

    xi, yi, ci = _position()
    chip_i = 2 * xi + yi
    chip = chip_i.astype(jnp.int32).reshape(1)
    core = ci.astype(jnp.int32).reshape(1)
    xs, target = x[0], loss_target[0]
    g1, g2, g3, g4 = w["norm_mix_pre"], w["norm_mix_post"], w["norm_ffn_pre"], w["norm_ffn_post"]
    shard = {n: w[n][0].astype(BF16) for n in BIG}
    small_shard = _pack_small_shard(w)

    def gathered(bufs, names):
        return [_own_slot(b, small_shard if n == "small" else shard[n], chip_i) for b, n in zip(bufs, names)]

    def chip_sums(arrays, kinds, tag):
        theirs = pair_split(arrays, kinds, "pair_split_" + tag)
        return [add_pair(g, k, t, core, "pair_add_%s_%d" % (tag, i)) for i, (g, k, t) in enumerate(zip(arrays, kinds, theirs))]

    h1, h1t = norm_in(xs, g1)
    win4, small4 = gathered(run_ride(gather_ride([shard["w_in"], small_shard]), "gather_first"), ("w_in", "small"))
    small = _full_small(small4)
    (proj,), got = matmul_cols(h1, win4, "proj_fwd",
                               ride=gather_ride([shard["w_conv_branch"], shard["w_lru_branch"], shard["w_out"]]))
    wcb, wlb, wout = [g.reshape(-1, D_MODEL) for g in gathered(got, ("w_conv_branch", "w_lru_branch", "w_out"))]
    up_piece = lambda r0, nr, into=None: gather_ride([shard["ffn_w_up"]], items=[(0, r0, nr)], into=into)
    down_piece = lambda r0, nr, into=None: gather_ride([shard["ffn_w_down"]], items=[(0, r0, nr)], into=into)
    (q, ya), got = mix_conv_fwd(proj, small["conv_short_w"], ride=up_piece(0, 128))
    (xl, r, gi, h, yb), got = mix_lru_fwd(
        proj, small["lru_conv_w"], w["lru_conv_b"], small["lru_wa"].astype(BF16), small["lru_ba"],
        small["lru_wx"].astype(BF16), small["lru_bx"], w["lru_lambda"], ride=up_piece(128, 512, got))
    (a, b, merged), got = branch_merge_fwd(ya, yb, wcb, wlb, proj, ride=up_piece(640, 384, got))
    (wup4,) = gathered(got, ("ffn_w_up",))
    (mix, x2, h2, h2t), got = mix_out_fwd(merged, wout, xs, g2, g3, ride=down_piece(0, 256))
    (up, act, f), got = ffn_up_act_fwd(h2, wup4, small["ffn_conv_w"], w["ffn_conv_b"], ride=down_piece(256, 512, got))
    wdown = gathered(got, ("ffn_w_down",))[0].reshape(-1, D_MODEL)
    dy, dout, loss, dg4 = ffn_down_loss(f, wdown, x2, target, g4)

    dh2, dwup, dwdown, dfw, dfb = ffn_up_bwd(dout, wdown, up, act, f, small["ffn_conv_w"], wup4, h2t)
    cs_down, cs_up = chip_sums([dwdown, dwup], ["row", "col2"], "ffn")
    down_rows = lambda r0, nr, into=None: exchange_ride([cs_down], items=[(0, r0, nr)], into=into)
    up_rows = lambda r0, nr, into=None: exchange_ride([cs_up], items=[(0, r0, nr)], into=into)
    (dx2, dmix, dg3, dg2), rx_down = norms_mid_bwd(dh2, x2, dy, mix, g3, g2, ride=down_rows(0, 128))
    (da, db, dwout, dgates), rx_down = mix_out_bwd(dmix, wout, merged, a, b, proj, ride=down_rows(128, 256, rx_down))
    (dconv, dwcb, dws), rx_up = mix_conv_bwd(da, wcb, proj, q, small["conv_short_w"], ride=up_rows(0, 176))
    cs_mid = chip_sums([dwout, dwcb], ["row", "row"], "mid")
    (dlru, dwlb, dwa, dwx, dba, dbx, dwl, dbl, dlam), rx_up = mix_lru_bwd(
        db, wlb, proj, xl, r, gi, h, small["lru_conv_w"], small["lru_wa"].astype(BF16), small["lru_wx"].astype(BF16),
        w["lru_lambda"], ride=up_rows(176, 336, rx_up))
    grads = dict(norm_mix_post=dg2, norm_ffn_pre=dg3, norm_ffn_post=dg4, conv_short_w=dws, lru_conv_w=dwl,
                 lru_conv_b=dbl, lru_wa=dwa, lru_ba=dba, lru_wx=dwx, lru_bx=dbx, lru_lambda=dlam,
                 ffn_conv_w=jnp.concatenate([dfw[0], dfw[1]], axis=1), ffn_conv_b=jnp.concatenate([dfb[0], dfb[1]], axis=1))
    cs_late = chip_sums([dwlb, _split_small(grads)], ["row", "lead"], "late")
    dproj = [dconv, dlru, dgates]
    (dwin,), rx_all = matmul_cols_bwd(dproj, h1t, "proj_wgrad", True, ride=exchange_ride(cs_mid + cs_late))
    rx_mid, rx_late = rx_all[:2], rx_all[2:]
    cs_in = chip_sums([dwin], ["col"], "in")
    in_rows = lambda r0, nr, into=None: exchange_ride(cs_in, items=[(0, r0, nr)], into=into)
    (dh1,), rx_in = matmul_cols_bwd(dproj, win4, "proj_dgrad", False, ride=in_rows(0, 384))
    (dx, grads["norm_mix_pre"]), rx_in = norm_in_bwd(dh1, xs, dx2, g1, ride=in_rows(384, 128, rx_in))
    rx_in = rx_in[0]
    rep_part = _pack_repl(grads, loss[0, 0])
    (rep_all,) = run_ride(exchange_ride([], rep=rep_part), "exchange_repl")

    order = (("w_in", cs_in[0], rx_in), ("ffn_w_up", cs_up, rx_up[0]), ("w_conv_branch", cs_mid[1], rx_mid[1]),
             ("w_lru_branch", cs_late[0], rx_late[0]), ("w_out", cs_mid[0], rx_mid[0]),
             ("ffn_w_down", cs_down, rx_down[0]), ("small", cs_late[1], rx_late[1]))
    halves = [sum_chips(rx, cs, chip, "chip_sum_" + n) for n, cs, rx in order]
    me = 4 * xi + 2 * yi + ci
    rep_grad = sum_lead(_own_slot(rep_all, rep_part, me), "device_sum")
    others = pair_swap(halves)

    g_out, d_out, m_out, v_out = {}, {}, {}, {}
    for n, gm, go in zip(BIG, halves[:-1], others[:-1]):
        g, d, nm, nv = adamw_halves(w[n][0], gm, go, m[n][0], v[n][0], core, "adamw_" + n)
        g_out[n], d_out[n], m_out[n], v_out[n] = g[None], d[None], nm[None], nv[None]
    bufs = adamw_halves(small_shard, halves[-1], others[-1], _pack_small_shard(m), _pack_small_shard(v),
                        core, "adamw_small")
    for dst, buf in zip((g_out, d_out, m_out, v_out), bufs):
        dst.update(_unpack_small_shard(buf))
    d, nm, nv = adamw(_pack_repl(w), rep_grad, _pack_repl(m), _pack_repl(v), "adamw_repl")
    for dst, buf in ((g_out, rep_grad), (d_out, d), (m_out, nm), (v_out, nv)):
        dst.update(_unpack_repl(buf))

    return (rep_grad[LOSS_ROW, 0], dx[None], *[g_out[n] for n in WEIGHTS], *[d_out[n] for n in WEIGHTS],
            *[m_out[n] for n in WEIGHTS], *[v_out[n] for n in WEIGHTS])
```

```python
import functools
import math

import jax
import jax.numpy as jnp
from jax import lax
from jax.experimental import pallas as pl
from jax.experimental.pallas import tpu as pltpu

F32 = jnp.float32
BF16 = jnp.bfloat16

D_MODEL = 1024
N_CHIPS = 4
N_SEG = 7
D_FF = 3 * D_MODEL
LRU_HEADS = 4
HEAD_DIM = D_MODEL // LRU_HEADS
LRU_C = 8.0
RMS_EPS = 1e-6
CW = 256
FW = 256
SUBLANES = 8
VMEM_LIMIT = 58 * 1024 * 1024

ADAM_LR = 0.001
ADAM_B1 = 0.9
ADAM_B2 = 0.999
ADAM_EPS = 1e-08
ADAM_WD = 0.01
ADAM_STEP = 10

_GELU_C = math.sqrt(2.0 / math.pi)
_GELU_K = 0.044715


def _params(**kw):
    return pltpu.CompilerParams(vmem_limit_bytes=VMEM_LIMIT, **kw)


def _sigmoid(x):
    return 1.0 / (1.0 + jnp.exp(-x))


def _gelu(x):
    t = jnp.tanh(_GELU_C * (x + _GELU_K * x * x * x))
    return 0.5 * x * (1.0 + t)


def _gelu_and_grad(x):
    x2 = x * x
    t = jnp.tanh(_GELU_C * (x + _GELU_K * x * x2))
    g = 0.5 * x * (1.0 + t)
    dg = 0.5 * (1.0 + t) + 0.5 * x * (1.0 - t * t) * _GELU_C * (1.0 + 3.0 * _GELU_K * x2)
    return g, dg


def _log_sigmoid(x):
    e = jnp.exp(-jnp.abs(x))
    u = 1.0 + e
    l1p = jnp.where(u == 1.0, e, jnp.log(u) * e / (u - 1.0))
    return jnp.minimum(x, 0.0) - l1p


def _neg_expm1(z):
    series = -z * (1.0 + z * (0.5 + z * (1.0 / 6.0 + z * (1.0 / 24.0 + z * (1.0 / 120.0 + z * (1.0 / 720.0))))))
    return jnp.where(z > -0.2, series, 1.0 - jnp.exp(z))


def _rows(shape):
    return lax.broadcasted_iota(jnp.int32, shape, 0)


def _shift_down(x, k):
    return jnp.where(_rows(x.shape) >= k, pltpu.roll(x, k, 0), 0.0)


def _shift_up(x, k):
    n = x.shape[0]
    return jnp.where(_rows(x.shape) < n - k, pltpu.roll(x, n - k, 0), 0.0)


def _delays(x, k_width):
    return [x] + [_shift_down(x, j) for j in range(1, k_width)]


def _advances(dy, k_width):
    return [dy] + [_shift_up(dy, j) for j in range(1, k_width)]


def _taps_sum(shifted, w_ref, b=None):
    k_width = w_ref.shape[0]
    y = w_ref[k_width - 1:k_width, :] * shifted[0]
    for j in range(1, k_width):
        y = y + w_ref[k_width - 1 - j:k_width - j, :] * shifted[j]
    if b is not None:
        y = y + b
    return y


def _causal_conv(x, w_ref, b=None):
    return _taps_sum(_delays(x, w_ref.shape[0]), w_ref, b)


def _conv_wgrad(advanced, x):
    k_width = len(advanced)
    rows = [jnp.sum(advanced[k_width - 1 - k] * x, axis=0, keepdims=True) for k in range(k_width)]
    return jnp.concatenate(rows, axis=0)


def _dot(a, b):
    return jnp.dot(a, b, preferred_element_type=F32)


def _dot_nt(a, b):
    return lax.dot_general(a, b, (((1,), (1,)), ((), ())), preferred_element_type=F32)


def _dot_tn(a, b):
    return lax.dot_general(a, b, (((0,), (0,)), ((), ())), preferred_element_type=F32)


def _rms_stats(x):
    r = lax.rsqrt(jnp.mean(x * x, axis=-1, keepdims=True) + RMS_EPS)
    return x * r, r


def _rms_bwd(n, r, g, dy):
    dn = dy * g
    dx = r * (dn - n * jnp.mean(dn * n, axis=-1, keepdims=True))
    return dx, dy * n


def _scan_forward(a_ref, b_ref, h_ref):
    n, c = a_ref.shape
    row = lax.broadcasted_iota(jnp.int32, (SUBLANES, c), 0)

    def group(g, carry):
        r0 = pl.multiple_of(g * SUBLANES, SUBLANES)
        a = a_ref[pl.ds(r0, SUBLANES), :]
        b = b_ref[pl.ds(r0, SUBLANES), :]
        for k in (1, 2, 4):
            ap = jnp.where(row >= k, pltpu.roll(a, k, 0), 1.0)
            bp = jnp.where(row >= k, pltpu.roll(b, k, 0), 0.0)
            b = a * bp + b
            a = a * ap
        h = a * carry + b
        h_ref[pl.ds(r0, SUBLANES), :] = h
        return h[SUBLANES - 1:SUBLANES, :]

    lax.fori_loop(0, n // SUBLANES, group, jnp.zeros((1, c), F32))


def _scan_backward(c_ref, b_ref, g_ref):
    n, ch = c_ref.shape
    row = lax.broadcasted_iota(jnp.int32, (SUBLANES, ch), 0)
    n_groups = n // SUBLANES

    def group(i, carry):
        r0 = pl.multiple_of((n_groups - 1 - i) * SUBLANES, SUBLANES)
        a = c_ref[pl.ds(r0, SUBLANES), :]
        b = b_ref[pl.ds(r0, SUBLANES), :]
        for k in (1, 2, 4):
            keep = row < SUBLANES - k
            ap = jnp.where(keep, pltpu.roll(a, SUBLANES - k, 0), 1.0)
            bp = jnp.where(keep, pltpu.roll(b, SUBLANES - k, 0), 0.0)
            b = a * bp + b
            a = a * ap
        g = a * carry + b
        g_ref[pl.ds(r0, SUBLANES), :] = g
        return g[0:1, :]

    lax.fori_loop(0, n_groups, group, jnp.zeros((1, ch), F32))


MESH = pl.DeviceIdType.MESH
_HBM = pl.BlockSpec(memory_space=pltpu.HBM)
_OTHER_CHIPS = ((1, 0), (0, 1), (1, 1))
_OTHER_DEVICES = tuple((dx, dy, dc) for dx in (0, 1) for dy in (0, 1) for dc in (0, 1) if dx or dy or dc)
N_DEVICES = 8


def _position():
    return lax.axis_index("x"), lax.axis_index("y"), lax.axis_index("c")


def _flip(v, d):
    return 1 - v if d else v


def _chip(x, y, p):
    px, py = _flip(x, _OTHER_CHIPS[p][0]), _flip(y, _OTHER_CHIPS[p][1])
    return px, py, 2 * px + py


class _Ride:
    def __init__(self, srcs, bufs, scratch, plan, collective_id):
        self.srcs, self.bufs, self.scratch, self.plan = list(srcs), list(bufs), list(scratch), plan
        self.collective_id = collective_id


NEIGHBOURS_AND_SIBLING = 1
OTHER_CHIPS_SAME_CORE = 2
ALL_DEVICES = 3
SIBLING = 4


def _handshake(peers):
    barrier = pltpu.get_barrier_semaphore()
    for peer in peers:
        pl.semaphore_signal(barrier, inc=1, device_id=peer, device_id_type=MESH)
    pl.semaphore_wait(barrier, len(peers))


def _call(body, *, name, grid, in_specs, out_specs, out_shape, operands, scratch_shapes=(), ride=None):
    in_specs, out_specs, out_shape = list(in_specs), list(out_specs), list(out_shape)
    scratch_shapes = list(scratch_shapes)
    if ride is None:
        return pl.pallas_call(body, name=name, grid=grid, in_specs=in_specs, out_specs=out_specs, out_shape=out_shape,
                              scratch_shapes=scratch_shapes, compiler_params=_params())(*operands)
    n_in, n_out, n_scr = len(in_specs), len(out_shape), len(scratch_shapes)
    old = [i for i, b in enumerate(ride.bufs) if not isinstance(b, jax.ShapeDtypeStruct)]
    n_src, n_old, n_buf = len(ride.srcs), len(old), len(ride.bufs)

    def full_body(*refs):
        o0 = n_in + n_src + n_old
        s0 = o0 + n_out + n_buf
        start, relay, relay_on, finish = ride.plan(refs[n_in:n_in + n_src], refs[o0 + n_out:s0], refs[s0 + n_scr:])
        ids = [pl.program_id(i) for i in range(len(grid))]
        first = functools.reduce(jnp.logical_and, [i == 0 for i in ids])
        middle = functools.reduce(jnp.logical_and, [ids[0] == grid[0] // 2] + [i == 0 for i in ids[1:]])
        last = functools.reduce(jnp.logical_and, [i == g - 1 for i, g in zip(ids, grid)])
        pl.when(first)(start)
        pl.when(middle)(relay)
        pl.when(last)(relay_on)
        body(*refs[:n_in], *refs[o0:o0 + n_out], *refs[s0:s0 + n_scr])
        pl.when(last)(finish)

    shapes = [jax.ShapeDtypeStruct(b.shape, b.dtype) for b in ride.bufs]
    res = pl.pallas_call(
        full_body, name=name, grid=grid,
        in_specs=in_specs + [_HBM] * (n_src + n_old), out_specs=out_specs + [_HBM] * n_buf,
        out_shape=out_shape + shapes, scratch_shapes=scratch_shapes + ride.scratch,
        input_output_aliases={n_in + n_src + k: n_out + i for k, i in enumerate(old)},
        compiler_params=_params(collective_id=ride.collective_id),
    )(*operands, *ride.srcs, *[ride.bufs[i] for i in old])
    return list(res[:n_out]), list(res[n_out:])


def run_ride(ride, name):
    def body():
        pass

    return _call(body, name=name, grid=(1,), in_specs=[], out_specs=[], out_shape=[], operands=[], ride=ride)[1]


def gather_ride(shards, items=None, into=None):
    items = items or [(a, 0, s.shape[0]) for a, s in enumerate(shards)]
    bufs = into or [jax.ShapeDtypeStruct((N_CHIPS,) + s.shape, s.dtype) for s in shards]
    nrel = len(_OTHER_CHIPS)

    def plan(srcs, dsts, sems):
        ici_send, ici_recv, hop_send, hop_recv, sib_send, sib_recv = sems
        x, y, c = _position()
        j = 2 * x + y

        def rows(ref, it, h, q=None):
            half = it[2] // 2
            if q is None:
                return ref.at[pl.ds(it[1] + h * half, half), :]
            return ref.at[pl.ds(it[1] + h * half + q * (half // 2), half // 2), :]

        def ici(i, p, slot):
            it = items[i]
            px, py, _ = _chip(x, y, p)
            return pltpu.make_async_remote_copy(
                src_ref=rows(srcs[it[0]], it, c), dst_ref=rows(dsts[it[0]].at[slot], it, c),
                send_sem=ici_send.at[i * nrel + p], recv_sem=ici_recv.at[i * nrel + p],
                device_id=(px, py, c), device_id_type=MESH)

        def hop(i, p, slot):
            it = items[i]
            part = rows(dsts[it[0]].at[slot], it, c, p)
            px, py, _ = _chip(x, y, 1 - p)
            return pltpu.make_async_remote_copy(
                src_ref=part, dst_ref=part, send_sem=hop_send.at[i * 2 + p], recv_sem=hop_recv.at[i * 2 + p],
                device_id=(px, py, c), device_id_type=MESH)

        def sib(i, p, h):
            it = items[i]
            part = rows(dsts[it[0]].at[_chip(x, y, p)[2]], it, h)
            return pltpu.make_async_remote_copy(
                src_ref=part, dst_ref=part, send_sem=sib_send.at[i * nrel + p], recv_sem=sib_recv.at[i * nrel + p],
                device_id=(x, y, 1 - c), device_id_type=MESH)

        every = range(len(items))
        diag = _chip(x, y, 2)[2]

        def start():
            _handshake([_chip(x, y, 0)[:2] + (c,), _chip(x, y, 1)[:2] + (c,), (x, y, 1 - c)])
            for i in every:
                for p in (0, 1):
                    ici(i, p, j).start()

        def relay():
            for i in every:
                for p in (0, 1):
                    k = _chip(x, y, p)[2]
                    ici(i, p, k).wait_recv()
                    hop(i, p, k).start()
                    sib(i, p, c).start()

        def relay_on():
            for i in every:
                for p in (0, 1):
                    hop(i, p, diag).wait_recv()
                sib(i, 2, c).start()

        def finish():
            for i in every:
                for p in range(nrel):
                    sib(i, p, 1 - c).wait_recv()
            for i in every:
                for p in (0, 1):
                    ici(i, p, j).wait_send()
                    hop(i, p, _chip(x, y, p)[2]).wait_send()
                for p in range(nrel):
                    sib(i, p, c).wait_send()

        return start, relay, relay_on, finish

    n = len(items)
    sems = [pltpu.SemaphoreType.DMA((n * nrel,))] * 2 + [pltpu.SemaphoreType.DMA((n * 2,))] * 2 \
        + [pltpu.SemaphoreType.DMA((n * nrel,))] * 2
    return _Ride(shards, bufs, sems, plan, NEIGHBOURS_AND_SIBLING)


def exchange_ride(sums, items=None, into=None, rep=None):
    items = [(a, 0, s.shape[1]) for a, s in enumerate(sums)] if items is None else items
    into = into or [None] * len(sums)
    bufs = [jax.ShapeDtypeStruct(s.shape, s.dtype) if b is None else b for s, b in zip(sums, into)]
    srcs = list(sums)
    scratch = [pltpu.SemaphoreType.DMA((max(len(items), 1) * len(_OTHER_CHIPS),))] * 2
    if rep is not None:
        srcs.append(rep)
        bufs.append(jax.ShapeDtypeStruct((N_DEVICES,) + rep.shape, rep.dtype))
        scratch += [pltpu.SemaphoreType.DMA((len(_OTHER_DEVICES),))] * 2
    nrel = len(_OTHER_CHIPS)

    def plan(src_refs, dst_refs, sems):
        x, y, c = _position()
        j = 2 * x + y
        me = 4 * x + 2 * y + c

        def part(i, p, src_slot, dst_slot):
            a, r0, nr = items[i]
            px, py, _ = _chip(x, y, p)
            return pltpu.make_async_remote_copy(
                src_ref=src_refs[a].at[src_slot, pl.ds(r0, nr), :], dst_ref=dst_refs[a].at[dst_slot, pl.ds(r0, nr), :],
                send_sem=sems[0].at[i * nrel + p], recv_sem=sems[1].at[i * nrel + p],
                device_id=(px, py, c), device_id_type=MESH)

        def device(q):
            dx, dy, dc = _OTHER_DEVICES[q]
            return _flip(x, dx), _flip(y, dy), _flip(c, dc)

        def rep_copy(q, slot):
            return pltpu.make_async_remote_copy(
                src_ref=src_refs[-1], dst_ref=dst_refs[-1].at[slot], send_sem=sems[2].at[q], recv_sem=sems[3].at[q],
                device_id=device(q), device_id_type=MESH)

        pairs = [(i, p) for i in range(len(items)) for p in range(nrel)]
        others = range(len(_OTHER_DEVICES)) if rep is not None else ()

        def start():
            if rep is None:
                _handshake([_chip(x, y, p)[:2] + (c,) for p in range(nrel)])
            else:
                _handshake([device(q) for q in others])
            for i, p in pairs:
                part(i, p, _chip(x, y, p)[2], j).start()
            for q in others:
                rep_copy(q, me).start()

        def finish():
            for i, p in pairs:
                k = _chip(x, y, p)[2]
                part(i, p, k, k).wait_recv()
            for q in others:
                px, py, pc = device(q)
                rep_copy(q, 4 * px + 2 * py + pc).wait_recv()
            for i, p in pairs:
                part(i, p, _chip(x, y, p)[2], j).wait_send()
            for q in others:
                rep_copy(q, me).wait_send()

        return start, lambda: None, lambda: None, finish

    return _Ride(srcs, bufs, scratch, plan, OTHER_CHIPS_SAME_CORE if rep is None else ALL_DEVICES)


def _own_slot(buf, own, index):
    return lax.dynamic_update_slice(buf, own[None], (index,) + (0,) * own.ndim)


def _token_tile(s):
    return min(s, 512)


def norm_in(x, g):
    s, d = x.shape
    t = _token_tile(s)

    def body(x_ref, g_ref, o_ref, ot_ref):
        n, _ = _rms_stats(x_ref[...])
        h = n * g_ref[...]
        o_ref[...] = h.astype(BF16)
        ot_ref[...] = h.T.astype(BF16)

    return pl.pallas_call(
        body, name="norm_in", grid=(s // t,),
        in_specs=[pl.BlockSpec((t, d), lambda i: (i, 0)), pl.BlockSpec((1, d), lambda i: (0, 0))],
        out_specs=[pl.BlockSpec((t, d), lambda i: (i, 0)), pl.BlockSpec((d, t), lambda i: (0, i))],
        out_shape=[jax.ShapeDtypeStruct((s, d), BF16), jax.ShapeDtypeStruct((d, s), BF16)],
        compiler_params=_params(),
    )(x, g)


def matmul_cols(a, w4, name, ride=None):
    m, k = a.shape
    nj, _, ns = w4.shape
    nb = ns // CW

    def body(a_ref, w_ref, o_ref):
        o_ref[...] = _dot(a_ref[...], w_ref[0])

    return _call(
        body, name=name, grid=(nj, nb),
        in_specs=[pl.BlockSpec((m, k), lambda j, b: (0, 0)),
                  pl.BlockSpec((1, k, CW), lambda j, b: (j, 0, b))],
        out_specs=[pl.BlockSpec((m, CW), lambda j, b: (0, j * nb + b))],
        out_shape=[jax.ShapeDtypeStruct((m, nj * ns), F32)],
        operands=(a, w4), ride=ride)


def mix_conv_fwd(proj, ws, ride=None):
    s = proj.shape[0]
    nblk = D_MODEL // CW

    def body(cb_ref, cc_ref, cx_ref, ws_ref, q_ref, ya_ref):
        q = _causal_conv(cc_ref[...] * cx_ref[...], ws_ref)
        q_ref[...] = q
        ya_ref[...] = (cb_ref[...] * q).astype(BF16)

    seg = lambda k: pl.BlockSpec((s, CW), lambda c, k=k: (0, k * nblk + c))
    return _call(
        body, name="mix_conv_fwd", grid=(nblk,),
        in_specs=[seg(0), seg(1), seg(2), pl.BlockSpec((3, CW), lambda c: (0, c))],
        out_specs=[pl.BlockSpec((s, CW), lambda c: (0, c))] * 2,
        out_shape=[jax.ShapeDtypeStruct((s, D_MODEL), F32), jax.ShapeDtypeStruct((s, D_MODEL), BF16)],
        operands=(proj, proj, proj, ws), ride=ride)


def _lru_gates(r, ls):
    log_a = LRU_C * r * ls
    a = jnp.exp(log_a)
    mult = jnp.sqrt(_neg_expm1(2.0 * log_a))
    mult = jnp.where(_rows(r.shape) == 0, 1.0, mult)
    return a, mult


def mix_lru_fwd(proj, wl, bl, wa, ba, wx, bx, lam, ride=None):
    s = proj.shape[0]
    nblk = D_MODEL // CW

    def body(lx_ref, ly_ref, wl_ref, bl_ref, wa_ref, ba_ref, wx_ref, bx_ref, lam_ref,
             xl_ref, r_ref, i_ref, h_ref, yb_ref, a_scr, u_scr):
        xl = _causal_conv(lx_ref[...], wl_ref, bl_ref[...])
        xlb = xl.astype(BF16)
        xl_ref[...] = xlb
        r = _sigmoid(_dot(xlb, wa_ref[0]) + ba_ref[...])
        i = _sigmoid(_dot(xlb, wx_ref[0]) + bx_ref[...])
        r_ref[...] = r.astype(BF16)
        i_ref[...] = i.astype(BF16)
        a, mult = _lru_gates(r, _log_sigmoid(lam_ref[...]))
        a_scr[...] = a
        u_scr[...] = mult * i * xl
        _scan_forward(a_scr, u_scr, h_ref)
        yb_ref[...] = (h_ref[...] * _gelu(ly_ref[...])).astype(BF16)

    blk = lambda k: pl.BlockSpec((s, CW), lambda c, k=k: (0, k * nblk + c))
    vec = pl.BlockSpec((1, CW), lambda c: (0, c))
    mat = pl.BlockSpec((1, CW, CW), lambda c: (c, 0, 0))
    out = pl.BlockSpec((s, CW), lambda c: (0, c))
    f = jax.ShapeDtypeStruct((s, D_MODEL), F32)
    hb = jax.ShapeDtypeStruct((s, D_MODEL), BF16)
    return _call(
        body, name="mix_lru_fwd", grid=(nblk,),
        in_specs=[blk(3), blk(4), pl.BlockSpec((4, CW), lambda c: (0, c)), vec, mat, vec, mat, vec, vec],
        out_specs=[out] * 5,
        out_shape=[hb, hb, hb, f, hb],
        scratch_shapes=[pltpu.VMEM((s, CW), F32), pltpu.VMEM((s, CW), F32)],
        operands=(proj, proj, wl, bl, wa, ba, wx, bx, lam), ride=ride)


def branch_merge_fwd(ya, yb, wcb, wlb, proj, ride=None):
    s = ya.shape[0]
    nblk = D_MODEL // CW

    def body(ya_ref, yb_ref, wcb_ref, wlb_ref, gc_ref, gl_ref, a_ref, b_ref, m_ref):
        a = _dot(ya_ref[...], wcb_ref[...])
        b = _dot(yb_ref[...], wlb_ref[...])
        a_ref[...] = a
        b_ref[...] = b
        m_ref[...] = (_sigmoid(gc_ref[...]) * a + _sigmoid(gl_ref[...]) * b).astype(BF16)

    res = pl.BlockSpec((s, D_MODEL), lambda n: (0, 0))
    wcol = pl.BlockSpec((D_MODEL, CW), lambda n: (0, n))
    blk = lambda k: pl.BlockSpec((s, CW), lambda n, k=k: (0, k * nblk + n))
    out = pl.BlockSpec((s, CW), lambda n: (0, n))
    f = jax.ShapeDtypeStruct((s, D_MODEL), F32)
    return _call(
        body, name="branch_merge_fwd", grid=(nblk,),
        in_specs=[res, res, wcol, wcol, blk(5), blk(6)],
        out_specs=[out] * 3,
        out_shape=[f, f, jax.ShapeDtypeStruct((s, D_MODEL), BF16)],
        operands=(ya, yb, wcb, wlb, proj, proj), ride=ride)


def mix_out_fwd(merged, wout, x, g2, g3, ride=None):
    s, d = x.shape
    t = _token_tile(s)

    def body(m_ref, w_ref, x_ref, g2_ref, g3_ref, mix_ref, x2_ref, h2_ref, h2t_ref):
        mix = _dot(m_ref[...], w_ref[...])
        mix_ref[...] = mix
        n, _ = _rms_stats(mix)
        x2 = x_ref[...] + n * g2_ref[...]
        x2_ref[...] = x2
        n2, _ = _rms_stats(x2)
        h2 = n2 * g3_ref[...]
        h2_ref[...] = h2.astype(BF16)
        h2t_ref[...] = h2.T.astype(BF16)

    tile = pl.BlockSpec((t, d), lambda i: (i, 0))
    vec = pl.BlockSpec((1, d), lambda i: (0, 0))
    f = jax.ShapeDtypeStruct((s, d), F32)
    return _call(
        body, name="mix_out_fwd", grid=(s // t,),
        in_specs=[tile, pl.BlockSpec((d, d), lambda i: (0, 0)), tile, vec, vec],
        out_specs=[tile] * 3 + [pl.BlockSpec((d, t), lambda i: (0, i))],
        out_shape=[f, f, jax.ShapeDtypeStruct((s, d), BF16), jax.ShapeDtypeStruct((d, s), BF16)],
        operands=(merged, wout, x, g2, g3), ride=ride)


def ffn_up_act_fwd(h2, wup4, fw, fb, ride=None):
    s, k = h2.shape
    ns = wup4.shape[2]
    per_chip = ns // CW
    nblk = D_FF // CW

    def body(h_ref, wg_ref, wv_ref, cg_ref, cv_ref, bg_ref, bv_ref, up_ref, act_ref, f_ref):
        h = h_ref[...]
        ug = _dot(h, wg_ref[0])
        uv = _dot(h, wv_ref[0])
        up_ref[0] = ug
        up_ref[1] = uv
        gate = _causal_conv(ug, cg_ref, bg_ref[...])
        val = _causal_conv(uv, cv_ref, bv_ref[...])
        act_ref[0] = gate.astype(BF16)
        act_ref[1] = val.astype(BF16)
        f_ref[...] = (_gelu(gate) * val).astype(BF16)

    wcols = lambda h: pl.BlockSpec((1, k, CW), lambda n, h=h: (n // per_chip + 2 * h, 0, n % per_chip))
    half = lambda h, rows: pl.BlockSpec((rows, CW), lambda n, h=h: (0, h * nblk + n))
    both = pl.BlockSpec((2, s, CW), lambda n: (0, 0, n))
    return _call(
        body, name="ffn_up_act_fwd", grid=(nblk,),
        in_specs=[pl.BlockSpec((s, k), lambda n: (0, 0)), wcols(0), wcols(1),
                  half(0, 3), half(1, 3), half(0, 1), half(1, 1)],
        out_specs=[both, both, pl.BlockSpec((s, CW), lambda n: (0, n))],
        out_shape=[jax.ShapeDtypeStruct((2, s, D_FF), F32), jax.ShapeDtypeStruct((2, s, D_FF), BF16),
                   jax.ShapeDtypeStruct((s, D_FF), BF16)],
        operands=(h2, wup4, wup4, fw, fw, fb, fb), ride=ride)


def ffn_down_loss(f, wdown, x2, target, g4):
    s, d = x2.shape
    t = _token_tile(s)

    def body(f_ref, w_ref, x2_ref, tg_ref, g4_ref, dy_ref, dout_ref, loss_ref, dg4_ref):
        @pl.when(pl.program_id(0) == 0)
        def _():
            loss_ref[...] = jnp.zeros_like(loss_ref)
            dg4_ref[...] = jnp.zeros_like(dg4_ref)

        out = _dot(f_ref[...], w_ref[...])
        n, r = _rms_stats(out)
        err = x2_ref[...] + n * g4_ref[...] - tg_ref[...]
        loss_ref[...] += jnp.full(loss_ref.shape, (0.5 / d) * jnp.sum(err * err), F32)
        dy = err * (1.0 / d)
        dy_ref[...] = dy
        dout, dg = _rms_bwd(n, r, g4_ref[...], dy)
        dout_ref[...] = dout.astype(BF16)
        dg4_ref[...] += jnp.sum(dg, axis=0, keepdims=True)

    tile = pl.BlockSpec((t, d), lambda i: (i, 0))
    vec = pl.BlockSpec((1, d), lambda i: (0, 0))
    return pl.pallas_call(
        body, name="ffn_down_loss", grid=(s // t,),
        in_specs=[pl.BlockSpec((t, D_FF), lambda i: (i, 0)), pl.BlockSpec((D_FF, d), lambda i: (0, 0)), tile, tile, vec],
        out_specs=[tile, tile, pl.BlockSpec((1, 128), lambda i: (0, 0)), vec],
        out_shape=[jax.ShapeDtypeStruct((s, d), F32), jax.ShapeDtypeStruct((s, d), BF16),
                   jax.ShapeDtypeStruct((1, 128), F32), jax.ShapeDtypeStruct((1, d), F32)],
        compiler_params=_params(),
    )(f, wdown, x2, target, g4)


def ffn_up_bwd(dout, wdown, up, act, f, fw, wup4, h2t, ride=None):
    k, s = h2t.shape
    nblk = D_FF // FW
    per_chip = wup4.shape[2] // FW

    def body(do_ref, wd_ref, up_ref, act_ref, f_ref, cg_ref, cv_ref, wg_ref, wv_ref, h_ref,
             dh_ref, dwu_ref, dwd_ref, dw_ref, db_ref, dup_scr):
        @pl.when(pl.program_id(0) == 0)
        def _():
            dup_scr[...] = jnp.zeros_like(dup_scr)
            dh_ref[...] = jnp.zeros_like(dh_ref)

        do = do_ref[...]
        df = _dot_nt(do, wd_ref[...])
        dg = dup_scr[0]
        dv = dup_scr[1]
        ht = h_ref[...]
        dh_ref[...] += _dot_nt(dg, wg_ref[0]) + _dot_nt(dv, wv_ref[0])
        dwu_ref[0] = _dot(ht, dg).astype(BF16)
        dwu_ref[1] = _dot(ht, dv).astype(BF16)
        dwd_ref[...] = _dot_tn(f_ref[...], do).astype(BF16)
        val = act_ref[1].astype(F32)
        ge, dge = _gelu_and_grad(act_ref[0].astype(F32))
        dgate = _advances(df * val * dge, 3)
        dval = _advances(df * ge, 3)
        dw_ref[0] = _conv_wgrad(dgate, up_ref[0])
        dw_ref[1] = _conv_wgrad(dval, up_ref[1])
        db_ref[0] = jnp.sum(dgate[0], axis=0, keepdims=True)
        db_ref[1] = jnp.sum(dval[0], axis=0, keepdims=True)
        dup_scr[0] = _taps_sum(dgate, cg_ref).astype(BF16)
        dup_scr[1] = _taps_sum(dval, cv_ref).astype(BF16)

    cur = lambda n: jnp.minimum(n, nblk - 1)
    prev = lambda n: jnp.maximum(n - 1, 0)
    once = pl.Buffered(1)
    both = lambda rows: pl.BlockSpec((2, rows, FW), lambda n: (0, 0, cur(n)))
    taps = lambda h: pl.BlockSpec((3, FW), lambda n, h=h: (0, h * nblk + cur(n)))
    wcols = lambda h: pl.BlockSpec((1, k, FW), lambda n, h=h: (prev(n) // per_chip + 2 * h, 0, prev(n) % per_chip))
    return _call(
        body, name="ffn_up_bwd", grid=(nblk + 1,),
        in_specs=[pl.BlockSpec((s, D_MODEL), lambda n: (0, 0), pipeline_mode=once),
                  pl.BlockSpec((FW, D_MODEL), lambda n: (cur(n), 0)), both(s), both(s),
                  pl.BlockSpec((s, FW), lambda n: (0, cur(n))), taps(0), taps(1), wcols(0), wcols(1),
                  pl.BlockSpec((k, s), lambda n: (0, 0), pipeline_mode=once)],
        out_specs=[pl.BlockSpec((s, k), lambda n: (0, 0), pipeline_mode=once),
                   pl.BlockSpec((2, k, FW), lambda n: (0, 0, prev(n))),
                   pl.BlockSpec((FW, D_MODEL), lambda n: (cur(n), 0)), both(3), both(1)],
        out_shape=[jax.ShapeDtypeStruct((s, k), F32), jax.ShapeDtypeStruct((2, k, D_FF), BF16),
                   jax.ShapeDtypeStruct((D_FF, D_MODEL), BF16),
                   jax.ShapeDtypeStruct((2, 3, D_FF), F32), jax.ShapeDtypeStruct((2, 1, D_FF), F32)],
        scratch_shapes=[pltpu.VMEM((2, s, FW), BF16)],
        operands=(dout, wdown, up, act, f, fw, fw, wup4, wup4, h2t), ride=ride)


def matmul_cols_bwd(dy, other, name, wgrad, ride=None):
    m = dy[0].shape[1]
    if wgrad:
        k = other.shape[0]
        nj, nb = N_CHIPS, sum(d.shape[0] * d.shape[2] for d in dy) // (N_CHIPS * CW)
    else:
        nj, k, ns = other.shape
        nb = ns // CW
    per_seg = dy[0].shape[2] // CW
    first = [sum(d.shape[0] for d in dy[:i]) for i in range(len(dy))]

    def segment(j, b):
        return (j * nb + b) // per_seg, (j * nb + b) % per_seg

    def body(*refs):
        dy_refs, (o_ref, r_ref) = refs[:len(dy)], refs[len(dy):]
        seg, _ = segment(pl.program_id(0), pl.program_id(1))
        dyb = dy_refs[-1][0]
        for i in range(len(dy) - 2, -1, -1):
            dyb = jnp.where(seg < first[i + 1], dy_refs[i][0], dyb)
        if wgrad:
            r_ref[...] = _dot(o_ref[...], dyb).astype(BF16)
        else:
            @pl.when((pl.program_id(0) == 0) & (pl.program_id(1) == 0))
            def _():
                r_ref[...] = jnp.zeros_like(r_ref)

            r_ref[...] += _dot_nt(dyb, o_ref[0])

    def dy_spec(i):
        nseg = dy[i].shape[0]

        def index(j, b):
            seg, col = segment(j, b)
            local = seg - first[i]
            return (jnp.clip(local, 0, nseg - 1), 0,
                    jnp.where(local < 0, 0, jnp.where(local >= nseg, per_seg - 1, col)))

        return pl.BlockSpec((1, m, CW), index)

    if wgrad:
        other_spec = pl.BlockSpec((k, m), lambda j, b: (0, 0))
        out_spec = pl.BlockSpec((k, CW), lambda j, b: (0, j * nb + b))
        out_shape = jax.ShapeDtypeStruct((k, nj * nb * CW), BF16)
    else:
        other_spec = pl.BlockSpec((1, k, CW), lambda j, b: (j, 0, b))
        out_spec = pl.BlockSpec((m, k), lambda j, b: (0, 0))
        out_shape = jax.ShapeDtypeStruct((m, k), F32)
    return _call(
        body, name=name, grid=(nj, nb), in_specs=[dy_spec(i) for i in range(len(dy))] + [other_spec],
        out_specs=[out_spec], out_shape=[out_shape], operands=(*dy, other), ride=ride)


def norms_mid_bwd(dh2, x2, dy, mix, g3, g2, ride=None):
    s, d = x2.shape
    t = _token_tile(s)

    def body(dh2_ref, x2_ref, dy_ref, mix_ref, g3_ref, g2_ref, dx2_ref, dmix_ref, dg3_ref, dg2_ref):
        @pl.when(pl.program_id(0) == 0)
        def _():
            dg3_ref[...] = jnp.zeros_like(dg3_ref)
            dg2_ref[...] = jnp.zeros_like(dg2_ref)

        n3, r3 = _rms_stats(x2_ref[...])
        dx, dg3 = _rms_bwd(n3, r3, g3_ref[...], dh2_ref[...])
        dx2 = dy_ref[...] + dx
        dx2_ref[...] = dx2
        dg3_ref[...] += jnp.sum(dg3, axis=0, keepdims=True)
        n2, r2 = _rms_stats(mix_ref[...])
        dmix, dg2 = _rms_bwd(n2, r2, g2_ref[...], dx2)
        dmix_ref[...] = dmix.astype(BF16)
        dg2_ref[...] += jnp.sum(dg2, axis=0, keepdims=True)

    tile = pl.BlockSpec((t, d), lambda i: (i, 0))
    vec = pl.BlockSpec((1, d), lambda i: (0, 0))
    v = jax.ShapeDtypeStruct((1, d), F32)
    return _call(
        body, name="norms_mid_bwd", grid=(s // t,),
        in_specs=[tile, tile, tile, tile, vec, vec],
        out_specs=[tile, tile, vec, vec],
        out_shape=[jax.ShapeDtypeStruct((s, d), F32), jax.ShapeDtypeStruct((s, d), BF16), v, v],
        operands=(dh2, x2, dy, mix, g3, g2), ride=ride)


def mix_out_bwd(dmix, wout, merged, a, b, proj, ride=None):
    s = dmix.shape[0]
    nblk = D_MODEL // CW

    def body(dm_ref, w_ref, mg_ref, a_ref, b_ref, gc_ref, gl_ref, da_ref, db_ref, dw_ref, dg_ref):
        dm = dm_ref[...]
        dmerged = _dot_nt(dm, w_ref[...])
        dw_ref[...] = _dot_tn(mg_ref[...], dm).astype(BF16)
        sc = _sigmoid(gc_ref[...])
        sl = _sigmoid(gl_ref[...])
        da_ref[...] = (dmerged * sc).astype(BF16)
        db_ref[...] = (dmerged * sl).astype(BF16)
        dg_ref[0] = (dmerged * a_ref[...] * sc * (1.0 - sc)).astype(BF16)
        dg_ref[1] = (dmerged * b_ref[...] * sl * (1.0 - sl)).astype(BF16)

    res = pl.BlockSpec((s, D_MODEL), lambda n: (0, 0))
    rows = pl.BlockSpec((CW, D_MODEL), lambda n: (n, 0))
    col = pl.BlockSpec((s, CW), lambda n: (0, n))
    blk = lambda k: pl.BlockSpec((s, CW), lambda n, k=k: (0, k * nblk + n))
    hb = jax.ShapeDtypeStruct((s, D_MODEL), BF16)
    return _call(
        body, name="mix_out_bwd", grid=(nblk,),
        in_specs=[res, rows, col, col, col, blk(5), blk(6)],
        out_specs=[col, col, rows, pl.BlockSpec((2, s, CW), lambda n: (0, 0, n))],
        out_shape=[hb, hb, jax.ShapeDtypeStruct((D_MODEL, D_MODEL), BF16), jax.ShapeDtypeStruct((2, s, D_MODEL), BF16)],
        operands=(dmix, wout, merged, a, b, proj, proj), ride=ride)


def mix_conv_bwd(da, wcb, proj, q, ws, ride=None):
    s = da.shape[0]
    nblk = D_MODEL // CW

    def body(da_ref, w_ref, cb_ref, cc_ref, cx_ref, q_ref, ws_ref, dc_ref, dw_ref, dws_ref):
        dab = da_ref[...]
        dya = _dot_nt(dab, w_ref[...])
        cb = cb_ref[...]
        cc = cc_ref[...]
        cx = cx_ref[...]
        q = q_ref[...]
        dw_ref[...] = _dot_tn((cb * q).astype(BF16), dab).astype(BF16)
        dc_ref[0] = (dya * q).astype(BF16)
        dq = _advances(dya * cb, 3)
        dp = _taps_sum(dq, ws_ref)
        dws_ref[...] = _conv_wgrad(dq, cc * cx)
        dc_ref[1] = (dp * cx).astype(BF16)
        dc_ref[2] = (dp * cc).astype(BF16)

    res = pl.BlockSpec((s, D_MODEL), lambda n: (0, 0))
    rows = pl.BlockSpec((CW, D_MODEL), lambda n: (n, 0))
    col = pl.BlockSpec((s, CW), lambda n: (0, n))
    blk = lambda k: pl.BlockSpec((s, CW), lambda n, k=k: (0, k * nblk + n))
    taps = pl.BlockSpec((3, CW), lambda n: (0, n))
    hb = jax.ShapeDtypeStruct((s, D_MODEL), BF16)
    return _call(
        body, name="mix_conv_bwd", grid=(nblk,),
        in_specs=[res, rows, blk(0), blk(1), blk(2), col, taps],
        out_specs=[pl.BlockSpec((3, s, CW), lambda n: (0, 0, n)), rows, taps],
        out_shape=[jax.ShapeDtypeStruct((3, s, D_MODEL), BF16), jax.ShapeDtypeStruct((D_MODEL, D_MODEL), BF16),
                   jax.ShapeDtypeStruct((3, D_MODEL), F32)],
        operands=(da, wcb, proj, proj, proj, q, ws), ride=ride)


def mix_lru_bwd(db, wlb, proj, xl, r, i, h, wl, wa, wx, lam, ride=None):
    s = db.shape[0]
    nblk = D_MODEL // CW

    def body(db_ref, w_ref, lx_ref, ly_ref, xl_ref, r_ref, i_ref, h_ref, wl_ref, wa_ref, wx_ref, lam_ref,
             dl_ref, dw_ref, dwa_ref, dwx_ref, dba_ref, dbx_ref, dwl_ref, dbl_ref, dlam_ref,
             c_scr, g_scr):
        dbb = db_ref[...]
        dyb = _dot_nt(dbb, w_ref[...])
        h = h_ref[...]
        ge, dge = _gelu_and_grad(ly_ref[...])
        dw_ref[...] = _dot_tn((h * ge).astype(BF16), dbb).astype(BF16)
        dl_ref[1] = (dyb * h * dge).astype(BF16)
        r = r_ref[...].astype(F32)
        gi = i_ref[...].astype(F32)
        xlb = xl_ref[...]
        xl = xlb.astype(F32)
        lam = lam_ref[...]
        ls = _log_sigmoid(lam)
        a, mult = _lru_gates(r, ls)
        c_scr[...] = _shift_up(a, 1)
        g_scr[...] = dyb * ge
        _scan_backward(c_scr, g_scr, g_scr)
        du = g_scr[...]
        da = du * _shift_down(h, 1)
        dmult = du * gi * xl
        di = du * mult * xl
        dxl = du * mult * gi
        first = _rows(a.shape) == 0
        dlog_a = da * a - jnp.where(first, 0.0, dmult * a * a / mult)
        dr = dlog_a * (LRU_C * ls)
        dlam_ref[...] = jnp.sum(dlog_a * r, axis=0, keepdims=True) * (LRU_C * (1.0 - _sigmoid(lam)))
        dzr = dr * r * (1.0 - r)
        dzi = di * gi * (1.0 - gi)
        dba_ref[...] = jnp.sum(dzr, axis=0, keepdims=True)
        dbx_ref[...] = jnp.sum(dzi, axis=0, keepdims=True)
        dzrb = dzr.astype(BF16)
        dzib = dzi.astype(BF16)
        dwa_ref[0] = _dot_tn(xlb, dzrb)
        dwx_ref[0] = _dot_tn(xlb, dzib)
        dxl = _advances(dxl + _dot_nt(dzrb, wa_ref[0]) + _dot_nt(dzib, wx_ref[0]), 4)
        dl_ref[0] = _taps_sum(dxl, wl_ref).astype(BF16)
        dwl_ref[...] = _conv_wgrad(dxl, lx_ref[...])
        dbl_ref[...] = jnp.sum(dxl[0], axis=0, keepdims=True)

    res = pl.BlockSpec((s, D_MODEL), lambda n: (0, 0))
    rows = pl.BlockSpec((CW, D_MODEL), lambda n: (n, 0))
    col = pl.BlockSpec((s, CW), lambda n: (0, n))
    blk = lambda k: pl.BlockSpec((s, CW), lambda n, k=k: (0, k * nblk + n))
    taps = pl.BlockSpec((4, CW), lambda n: (0, n))
    vec = pl.BlockSpec((1, CW), lambda n: (0, n))
    mat = pl.BlockSpec((1, CW, CW), lambda n: (n, 0, 0))
    hb = jax.ShapeDtypeStruct((s, D_MODEL), BF16)
    v = jax.ShapeDtypeStruct((1, D_MODEL), F32)
    m = jax.ShapeDtypeStruct((LRU_HEADS, HEAD_DIM, HEAD_DIM), F32)
    scr = pltpu.VMEM((s, CW), F32)
    return _call(
        body, name="mix_lru_bwd", grid=(nblk,),
        in_specs=[res, rows, blk(3), blk(4), col, col, col, col, taps, mat, mat, vec],
        out_specs=[pl.BlockSpec((2, s, CW), lambda n: (0, 0, n)), rows, mat, mat, vec, vec, taps, vec, vec],
        out_shape=[jax.ShapeDtypeStruct((2, s, D_MODEL), BF16), jax.ShapeDtypeStruct((D_MODEL, D_MODEL), BF16), m, m, v, v,
                   jax.ShapeDtypeStruct((4, D_MODEL), F32), v, v],
        scratch_shapes=[scr, scr],
        operands=(db, wlb, proj, proj, xl, r, i, h, wl, wa, wx, lam), ride=ride)


def norm_in_bwd(dh1, x, dx2, g1, ride=None):
    s, d = x.shape
    t = _token_tile(s)

    def body(dh_ref, x_ref, dx2_ref, g_ref, dx_ref, dg_ref):
        @pl.when(pl.program_id(0) == 0)
        def _():
            dg_ref[...] = jnp.zeros_like(dg_ref)

        n, r = _rms_stats(x_ref[...])
        dx, dg = _rms_bwd(n, r, g_ref[...], dh_ref[...])
        dx_ref[...] = dx2_ref[...] + dx
        dg_ref[...] += jnp.sum(dg, axis=0, keepdims=True)

    tile = pl.BlockSpec((t, d), lambda i: (i, 0))
    vec = pl.BlockSpec((1, d), lambda i: (0, 0))
    return _call(
        body, name="norm_in_bwd", grid=(s // t,),
        in_specs=[tile, tile, tile, vec],
        out_specs=[tile, vec],
        out_shape=[jax.ShapeDtypeStruct((s, d), F32), jax.ShapeDtypeStruct((1, d), F32)],
        operands=(dh1, x, dx2, g1), ride=ride)


def _owned_part(ref, kind, k, h, hr):
    if kind == "col":
        ns = ref.shape[1] // N_CHIPS
        return ref.at[pl.ds(h * hr, hr), pl.ds(k * ns, ns)]
    if kind == "row":
        return ref.at[pl.ds(k * 2 * hr + h * hr, hr), :]
    if kind == "col2":
        ns = ref.shape[2] // 2
        return ref.at[k // 2, pl.ds(h * hr, hr), pl.ds((k % 2) * ns, ns)]
    return ref.at[k, pl.ds(h * hr, hr), :]


def _part_shape(g, kind):
    if kind == "col2":
        return g.shape[1] // 2, g.shape[2] // 2
    if kind == "col":
        return g.shape[0] // 2, g.shape[1] // N_CHIPS
    if kind == "row":
        return g.shape[0] // (2 * N_CHIPS), g.shape[1]
    return g.shape[1] // 2, g.shape[2]


def pair_split(grads, kinds, name):
    n = len(grads)
    shapes = [_part_shape(g, k) for g, k in zip(grads, kinds)]

    def body(*refs):
        ins, theirs = refs[:n], refs[n:2 * n]
        send_sem, recv_sem = refs[2 * n:]
        x, y, c = _position()
        copies = []
        for a in range(n):
            hr = shapes[a][0]
            for k in range(N_CHIPS):
                s = a * N_CHIPS + k
                copies.append(pltpu.make_async_remote_copy(
                    src_ref=_owned_part(ins[a], kinds[a], k, 1 - c, hr), dst_ref=theirs[a].at[k],
                    send_sem=send_sem.at[s], recv_sem=recv_sem.at[s], device_id=(x, y, 1 - c), device_id_type=MESH))
        _handshake([(x, y, 1 - c)])
        for cp in copies:
            cp.start()
        for cp in copies:
            cp.wait()

    return pl.pallas_call(
        body, name=name,
        in_specs=[_HBM] * n, out_specs=[_HBM] * n,
        out_shape=[jax.ShapeDtypeStruct((N_CHIPS,) + shp, g.dtype) for shp, g in zip(shapes, grads)],
        scratch_shapes=[pltpu.SemaphoreType.DMA((n * N_CHIPS,))] * 2,
        compiler_params=pltpu.CompilerParams(collective_id=SIBLING),
    )(*grads)


def pair_swap(halves):
    n = len(halves)

    def body(*refs):
        ins, outs = refs[:n], refs[n:2 * n]
        send_sem, recv_sem = refs[2 * n:]
        x, y, c = _position()
        copies = [pltpu.make_async_remote_copy(
            src_ref=ins[a], dst_ref=outs[a], send_sem=send_sem.at[a], recv_sem=recv_sem.at[a],
            device_id=(x, y, 1 - c), device_id_type=MESH) for a in range(n)]
        _handshake([(x, y, 1 - c)])
        for cp in copies:
            cp.start()
        for cp in copies:
            cp.wait()

    return pl.pallas_call(
        body, name="pair_swap",
        in_specs=[_HBM] * n, out_specs=[_HBM] * n,
        out_shape=[jax.ShapeDtypeStruct(h.shape, h.dtype) for h in halves],
        scratch_shapes=[pltpu.SemaphoreType.DMA((n,))] * 2,
        compiler_params=pltpu.CompilerParams(collective_id=SIBLING),
    )(*halves)


def _row_tile(rows, cols, limit_bytes=1 << 20):
    best = None
    for t in range(SUBLANES, rows + 1, SUBLANES):
        if rows % t == 0 and t * cols * 4 <= limit_bytes:
            best = t
    return best or rows


def add_pair(g, kind, theirs, core, name):
    nc, rows, cols = theirs.shape
    t = _row_tile(rows, cols, 4 << 20)
    nt = rows // t

    def body(core_ref, g_ref, b_ref, o_ref):
        mine = g_ref[...].reshape(t, cols)
        o_ref[0] = (mine.astype(F32) + b_ref[0].astype(F32)).astype(o_ref.dtype)

    if kind == "col":
        own = pl.BlockSpec((t, cols), lambda k, i, c: (c[0] * nt + i, k))
    elif kind == "col2":
        own = pl.BlockSpec((1, t, cols), lambda k, i, c: (k // 2, c[0] * nt + i, k % 2))
    elif kind == "row":
        own = pl.BlockSpec((t, cols), lambda k, i, c: ((2 * k + c[0]) * nt + i, 0))
    else:
        own = pl.BlockSpec((1, t, cols), lambda k, i, c: (k, c[0] * nt + i, 0))
    spec = pl.BlockSpec((1, t, cols), lambda k, i, c: (k, i, 0))
    return pl.pallas_call(
        body, name=name,
        grid_spec=pltpu.PrefetchScalarGridSpec(num_scalar_prefetch=1, grid=(nc, nt), in_specs=[own, spec], out_specs=spec),
        out_shape=jax.ShapeDtypeStruct(theirs.shape, theirs.dtype), compiler_params=_params(),
    )(core, g, theirs)


def sum_lead(a, name):
    nl, rows, cols = a.shape
    t = _row_tile(rows, cols, (1 << 20) // 2)

    def body(a_ref, o_ref):
        acc = a_ref[0].astype(F32)
        for s in range(1, nl):
            acc = acc + a_ref[s].astype(F32)
        o_ref[...] = acc

    return pl.pallas_call(
        body, name=name, grid=(rows // t,),
        in_specs=[pl.BlockSpec((nl, t, cols), lambda i: (0, i, 0))],
        out_specs=pl.BlockSpec((t, cols), lambda i: (i, 0)),
        out_shape=jax.ShapeDtypeStruct((rows, cols), F32), compiler_params=_params(),
    )(a)


def sum_chips(rx, csum, chip, name):
    nc, rows, cols = rx.shape
    t = _row_tile(rows, cols, 2 << 20)

    def body(chip_ref, r0, r1, r2, r3, own_ref, o_ref):
        acc = None
        for s, ref in enumerate((r0, r1, r2, r3)):
            term = jnp.where(chip_ref[0] == s, own_ref[0], ref[0]).astype(F32)
            acc = term if acc is None else acc + term
        o_ref[...] = acc

    def slot(s):
        return pl.BlockSpec((1, t, cols), lambda i, c, s=s: (jnp.where(c[0] == s, c[0] ^ 1, s), i, 0))

    return pl.pallas_call(
        body, name=name,
        grid_spec=pltpu.PrefetchScalarGridSpec(
            num_scalar_prefetch=1, grid=(rows // t,),
            in_specs=[slot(s) for s in range(nc)] + [pl.BlockSpec((1, t, cols), lambda i, c: (c[0], i, 0))],
            out_specs=pl.BlockSpec((t, cols), lambda i, c: (i, 0))),
        out_shape=jax.ShapeDtypeStruct((rows, cols), F32), compiler_params=_params(),
    )(chip, rx, rx, rx, rx, csum)


def _adamw_update(w, g, m, v):
    nm = ADAM_B1 * m + (1.0 - ADAM_B1) * g
    nv = ADAM_B2 * v + (1.0 - ADAM_B2) * (g * g)
    m_hat = nm * (1.0 / (1.0 - ADAM_B1 ** ADAM_STEP))
    v_hat = nv * (1.0 / (1.0 - ADAM_B2 ** ADAM_STEP))
    return -ADAM_LR * (m_hat / (jnp.sqrt(v_hat) + ADAM_EPS) + ADAM_WD * w), nm, nv


def adamw(w, g, m, v, name):
    rows, cols = w.shape
    t = _row_tile(rows, cols)

    def body(w_ref, g_ref, m_ref, v_ref, d_ref, nm_ref, nv_ref):
        d_ref[...], nm_ref[...], nv_ref[...] = _adamw_update(w_ref[...], g_ref[...], m_ref[...], v_ref[...])

    spec = pl.BlockSpec((t, cols), lambda i: (i, 0))
    shp = jax.ShapeDtypeStruct((rows, cols), F32)
    return pl.pallas_call(
        body, name=name, grid=(rows // t,), in_specs=[spec] * 4, out_specs=[spec] * 3,
        out_shape=[shp, shp, shp], compiler_params=_params(),
    )(w, g, m, v)


def adamw_halves(w, g_mine, g_other, m, v, core, name):
    rows, cols = w.shape
    hr = rows // 2
    t = _row_tile(hr, cols)
    nt = hr // t

    def body(core_ref, w_ref, gm_ref, go_ref, m_ref, v_ref, g_ref, d_ref, nm_ref, nv_ref):
        g = jnp.where(pl.program_id(0) // nt == core_ref[0], gm_ref[...], go_ref[...])
        g_ref[...] = g
        d_ref[...], nm_ref[...], nv_ref[...] = _adamw_update(w_ref[...], g, m_ref[...], v_ref[...])

    spec = pl.BlockSpec((t, cols), lambda i, c: (i, 0))
    half = pl.BlockSpec((t, cols), lambda i, c: (i % nt, 0))
    shp = jax.ShapeDtypeStruct((rows, cols), F32)
    return pl.pallas_call(
        body, name=name,
        grid_spec=pltpu.PrefetchScalarGridSpec(num_scalar_prefetch=1, grid=(2 * nt,),
                                               in_specs=[spec, half, half, spec, spec], out_specs=[spec] * 4),
        out_shape=[shp] * 4, compiler_params=_params(),
    )(core, w, g_mine, g_other, m, v)


WEIGHTS = ("norm_mix_pre", "norm_mix_post", "norm_ffn_pre", "norm_ffn_post", "w_in", "conv_short_w",
           "w_conv_branch", "lru_conv_w", "lru_conv_b", "lru_wa", "lru_ba", "lru_wx", "lru_bx", "lru_lambda",
           "w_lru_branch", "w_out", "ffn_w_up", "ffn_conv_w", "ffn_conv_b", "ffn_w_down")
BIG = ("w_in", "ffn_w_up", "w_conv_branch", "w_lru_branch", "w_out", "ffn_w_down")
BIG_KIND = ("col", "col", "row", "row", "row", "row")
SMALL = ("conv_short_w", "lru_conv_w", "lru_wa", "lru_ba", "lru_wx", "lru_bx", "ffn_conv_w")
REPL = ("norm_mix_pre", "norm_mix_post", "norm_ffn_pre", "norm_ffn_post", "lru_conv_b", "lru_lambda", "ffn_conv_b")
PACK_W = 256
SMALL_ROWS = 576
REPL_ROWS = 16
LOSS_ROW = 12
FFN_SHARD = 2 * D_FF // N_CHIPS
QUARTER = HEAD_DIM // N_CHIPS
SMALL_PARTS = (("conv_short_w", 3, (1, 3, PACK_W)), ("lru_conv_w", 4, (1, 4, PACK_W)),
               ("lru_wa", LRU_HEADS * QUARTER, (1, LRU_HEADS, QUARTER, HEAD_DIM)), ("lru_ba", 1, (1, LRU_HEADS, QUARTER)),
               ("lru_wx", LRU_HEADS * QUARTER, (1, LRU_HEADS, QUARTER, HEAD_DIM)), ("lru_bx", 1, (1, LRU_HEADS, QUARTER)),
               ("ffn_conv_w", 3 * FFN_SHARD // PACK_W, (1, 3, FFN_SHARD)))


def _pad8(nr):
    return -(-nr // SUBLANES) * SUBLANES


def _pack_small_shard(p):
    rows = [jnp.pad(p[name].reshape(nr, PACK_W), ((0, _pad8(nr) - nr), (0, 0))) for name, nr, _ in SMALL_PARTS]
    used = sum(r.shape[0] for r in rows)
    return jnp.concatenate(rows + [jnp.zeros((SMALL_ROWS - used, PACK_W), F32)], axis=0)


def _unpack_small_shard(buf):
    out, r = {}, 0
    for name, nr, shape in SMALL_PARTS:
        out[name] = buf[r:r + nr].reshape(shape)
        r += _pad8(nr)
    return out


def _full_small(g4):
    per = [_unpack_small_shard(g4[k]) for k in range(N_CHIPS)]
    cat = lambda name, axis: jnp.concatenate([per[k][name][0] for k in range(N_CHIPS)], axis=axis)
    return dict(conv_short_w=cat("conv_short_w", 1), lru_conv_w=cat("lru_conv_w", 1),
                lru_wa=cat("lru_wa", 1), lru_ba=cat("lru_ba", 1).reshape(1, D_MODEL),
                lru_wx=cat("lru_wx", 1), lru_bx=cat("lru_bx", 1).reshape(1, D_MODEL),
                ffn_conv_w=cat("ffn_conv_w", 1))


def _split_small(full):
    shards = []
    for k in range(N_CHIPS):
        cols = lambda a, w: a[:, k * w:(k + 1) * w]
        q = slice(k * QUARTER, (k + 1) * QUARTER)
        shards.append(_pack_small_shard(dict(
            conv_short_w=cols(full["conv_short_w"], PACK_W), lru_conv_w=cols(full["lru_conv_w"], PACK_W),
            lru_wa=full["lru_wa"][:, q, :], lru_ba=full["lru_ba"].reshape(LRU_HEADS, HEAD_DIM)[:, q],
            lru_wx=full["lru_wx"][:, q, :], lru_bx=full["lru_bx"].reshape(LRU_HEADS, HEAD_DIM)[:, q],
            ffn_conv_w=cols(full["ffn_conv_w"], FFN_SHARD))))
    return jnp.stack(shards)


def _pack_repl(p, loss=None):
    rows = [p[n].reshape(-1, D_MODEL) for n in REPL]
    if loss is not None:
        rows.append(jnp.broadcast_to(loss.reshape(1, 1), (1, D_MODEL)))
    used = sum(r.shape[0] for r in rows)
    return jnp.concatenate(rows + [jnp.zeros((REPL_ROWS - used, D_MODEL), F32)], axis=0)


def _unpack_repl(buf):
    out, r = {}, 0
    for n in REPL:
        nr = (2 * D_FF // D_MODEL) if n == "ffn_conv_b" else 1
        out[n] = buf[r:r + nr].reshape(1, nr * D_MODEL)
        r += nr
    return out


def kernel(x, norm_mix_pre, norm_mix_post, norm_ffn_pre, norm_ffn_post, w_in, conv_short_w, w_conv_branch, lru_conv_w, lru_conv_b, lru_wa, lru_ba, lru_wx, lru_bx, lru_lambda, w_lru_branch, w_out, ffn_w_up, ffn_conv_w, ffn_conv_b, ffn_w_down, loss_target, m_norm_mix_pre, m_norm_mix_post, m_norm_ffn_pre, m_norm_ffn_post, m_w_in, m_conv_short_w, m_w_conv_branch, m_lru_conv_w, m_lru_conv_b, m_lru_wa, m_lru_ba, m_lru_wx, m_lru_bx, m_lru_lambda, m_w_lru_branch, m_w_out, m_ffn_w_up, m_ffn_conv_w, m_ffn_conv_b, m_ffn_w_down, v_norm_mix_pre, v_norm_mix_post, v_norm_ffn_pre, v_norm_ffn_post, v_w_in, v_conv_short_w, v_w_conv_branch, v_lru_conv_w, v_lru_conv_b, v_lru_wa, v_lru_ba, v_lru_wx, v_lru_bx, v_lru_lambda, v_w_lru_branch, v_w_out, v_ffn_w_up, v_ffn_conv_w, v_ffn_conv_b, v_ffn_w_down):
    given = dict(locals())
    w = {n: given[n] for n in WEIGHTS}
    m = {n: given["m_" + n] for n in WEIGHTS}
    v = {n: given["v_" + n] for n in WEIGHTS}

    xi, yi, ci = _position()
    chip_i = 2 * xi + yi
    chip = chip_i.astype(jnp.int32).reshape(1)
    core = ci.astype(jnp.int32).reshape(1)
    xs, target = x[0], loss_target[0]
    g1, g2, g3, g4 = w["norm_mix_pre"], w["norm_mix_post"], w["norm_ffn_pre"], w["norm_ffn_post"]
    shard = {n: w[n][0].astype(BF16) for n in BIG}
    small_shard = _pack_small_shard(w)

    def gathered(bufs, names):
        return [_own_slot(b, small_shard if n == "small" else shard[n], chip_i) for b, n in zip(bufs, names)]

    def chip_sums(arrays, kinds, tag):
        theirs = pair_split(arrays, kinds, "pair_split_" + tag)
        return [add_pair(g, k, t, core, "pair_add_%s_%d" % (tag, i)) for i, (g, k, t) in enumerate(zip(arrays, kinds, theirs))]

    h1, h1t = norm_in(xs, g1)
    win4, small4 = gathered(run_ride(gather_ride([shard["w_in"], small_shard]), "gather_first"), ("w_in", "small"))
    small = _full_small(small4)
    (proj,), got = matmul_cols(h1, win4, "proj_fwd",
                               ride=gather_ride([shard["w_conv_branch"], shard["w_lru_branch"], shard["w_out"]]))
    wcb, wlb, wout = [g.reshape(-1, D_MODEL) for g in gathered(got, ("w_conv_branch", "w_lru_branch", "w_out"))]
    up_piece = lambda r0, nr, into=None: gather_ride([shard["ffn_w_up"]], items=[(0, r0, nr)], into=into)
    down_piece = lambda r0, nr, into=None: gather_ride([shard["ffn_w_down"]], items=[(0, r0, nr)], into=into)
    (q, ya), got = mix_conv_fwd(proj, small["conv_short_w"], ride=up_piece(0, 128))
    (xl, r, gi, h, yb), got = mix_lru_fwd(
        proj, small["lru_conv_w"], w["lru_conv_b"], small["lru_wa"].astype(BF16), small["lru_ba"],
        small["lru_wx"].astype(BF16), small["lru_bx"], w["lru_lambda"], ride=up_piece(128, 512, got))
    (a, b, merged), got = branch_merge_fwd(ya, yb, wcb, wlb, proj, ride=up_piece(640, 384, got))
    (wup4,) = gathered(got, ("ffn_w_up",))
    (mix, x2, h2, h2t), got = mix_out_fwd(merged, wout, xs, g2, g3, ride=down_piece(0, 256))
    (up, act, f), got = ffn_up_act_fwd(h2, wup4, small["ffn_conv_w"], w["ffn_conv_b"], ride=down_piece(256, 512, got))
    wdown = gathered(got, ("ffn_w_down",))[0].reshape(-1, D_MODEL)
    dy, dout, loss, dg4 = ffn_down_loss(f, wdown, x2, target, g4)

    dh2, dwup, dwdown, dfw, dfb = ffn_up_bwd(dout, wdown, up, act, f, small["ffn_conv_w"], wup4, h2t)
    cs_down, cs_up = chip_sums([dwdown, dwup], ["row", "col2"], "ffn")
    down_rows = lambda r0, nr, into=None: exchange_ride([cs_down], items=[(0, r0, nr)], into=into)
    up_rows = lambda r0, nr, into=None: exchange_ride([cs_up], items=[(0, r0, nr)], into=into)
    (dx2, dmix, dg3, dg2), rx_down = norms_mid_bwd(dh2, x2, dy, mix, g3, g2, ride=down_rows(0, 128))
    (da, db, dwout, dgates), rx_down = mix_out_bwd(dmix, wout, merged, a, b, proj, ride=down_rows(128, 256, rx_down))
    (dconv, dwcb, dws), rx_up = mix_conv_bwd(da, wcb, proj, q, small["conv_short_w"], ride=up_rows(0, 176))
    cs_mid = chip_sums([dwout, dwcb], ["row", "row"], "mid")
    (dlru, dwlb, dwa, dwx, dba, dbx, dwl, dbl, dlam), rx_up = mix_lru_bwd(
        db, wlb, proj, xl, r, gi, h, small["lru_conv_w"], small["lru_wa"].astype(BF16), small["lru_wx"].astype(BF16),
        w["lru_lambda"], ride=up_rows(176, 336, rx_up))
    grads = dict(norm_mix_post=dg2, norm_ffn_pre=dg3, norm_ffn_post=dg4, conv_short_w=dws, lru_conv_w=dwl,
                 lru_conv_b=dbl, lru_wa=dwa, lru_ba=dba, lru_wx=dwx, lru_bx=dbx, lru_lambda=dlam,
                 ffn_conv_w=jnp.concatenate([dfw[0], dfw[1]], axis=1), ffn_conv_b=jnp.concatenate([dfb[0], dfb[1]], axis=1))
    cs_late = chip_sums([dwlb, _split_small(grads)], ["row", "lead"], "late")
    dproj = [dconv, dlru, dgates]
    (dwin,), rx_all = matmul_cols_bwd(dproj, h1t, "proj_wgrad", True, ride=exchange_ride(cs_mid + cs_late))
    rx_mid, rx_late = rx_all[:2], rx_all[2:]
    cs_in = chip_sums([dwin], ["col"], "in")
    in_rows = lambda r0, nr, into=None: exchange_ride(cs_in, items=[(0, r0, nr)], into=into)
    (dh1,), rx_in = matmul_cols_bwd(dproj, win4, "proj_dgrad", False, ride=in_rows(0, 384))
    (dx, grads["norm_mix_pre"]), rx_in = norm_in_bwd(dh1, xs, dx2, g1, ride=in_rows(384, 128, rx_in))
    rx_in = rx_in[0]
    rep_part = _pack_repl(grads, loss[0, 0])
    (rep_all,) = run_ride(exchange_ride([], rep=rep_part), "exchange_repl")

    order = (("w_in", cs_in[0], rx_in), ("ffn_w_up", cs_up, rx_up[0]), ("w_conv_branch", cs_mid[1], rx_mid[1]),
             ("w_lru_branch", cs_late[0], rx_late[0]), ("w_out", cs_mid[0], rx_mid[0]),
             ("ffn_w_down", cs_down, rx_down[0]), ("small", cs_late[1], rx_late[1]))
    halves = [sum_chips(rx, cs, chip, "chip_sum_" + n) for n, cs, rx in order]
    me = 4 * xi + 2 * yi + ci
    rep_grad = sum_lead(_own_slot(rep_all, rep_part, me), "device_sum")
    others = pair_swap(halves)

    g_out, d_out, m_out, v_out = {}, {}, {}, {}
    for n, gm, go in zip(BIG, halves[:-1], others[:-1]):
        g, d, nm, nv = adamw_halves(w[n][0], gm, go, m[n][0], v[n][0], core, "adamw_" + n)
        g_out[n], d_out[n], m_out[n], v_out[n] = g[None], d[None], nm[None], nv[None]
    bufs = adamw_halves(small_shard, halves[-1], others[-1], _pack_small_shard(m), _pack_small_shard(v),
                        core, "adamw_small")
    for dst, buf in zip((g_out, d_out, m_out, v_out), bufs):
        dst.update(_unpack_small_shard(buf))
    d, nm, nv = adamw(_pack_repl(w), rep_grad, _pack_repl(m), _pack_repl(v), "adamw_repl")
    for dst, buf in ((g_out, rep_grad), (d_out, d), (m_out, nm), (v_out, nv)):
        dst.update(_unpack_repl(buf))

    return (rep_grad[LOSS_ROW, 0], dx[None], *[g_out[n] for n in WEIGHTS], *[d_out[n] for n in WEIGHTS],
            *[m_out[n] for n in WEIGHTS], *[v_out[n] for n in WEIGHTS])
```

```python
import functools
import math

import jax
import jax.numpy as jnp
from jax import lax
from jax.experimental import pallas as pl
from jax.experimental.pallas import tpu as pltpu

F32 = jnp.float32
BF16 = jnp.bfloat16

D_MODEL = 1024
N_CHIPS = 4
N_SEG = 7
D_FF = 3 * D_MODEL
LRU_HEADS = 4
HEAD_DIM = D_MODEL // LRU_HEADS
LRU_C = 8.0
RMS_EPS = 1e-6
CW = 256
FW = 256
SUBLANES = 8
VMEM_LIMIT = 58 * 1024 * 1024

ADAM_LR = 0.001
ADAM_B1 = 0.9
ADAM_B2 = 0.999
ADAM_EPS = 1e-08
ADAM_WD = 0.01
ADAM_STEP = 10

_GELU_C = math.sqrt(2.0 / math.pi)
_GELU_K = 0.044715


def _params(**kw):
    return pltpu.CompilerParams(vmem_limit_bytes=VMEM_LIMIT, **kw)


def _sigmoid(x):
    return 1.0 / (1.0 + jnp.exp(-x))


def _gelu(x):
    t = jnp.tanh(_GELU_C * (x + _GELU_K * x * x * x))
    return 0.5 * x * (1.0 + t)


def _gelu_and_grad(x):
    x2 = x * x
    t = jnp.tanh(_GELU_C * (x + _GELU_K * x * x2))
    g = 0.5 * x * (1.0 + t)
    dg = 0.5 * (1.0 + t) + 0.5 * x * (1.0 - t * t) * _GELU_C * (1.0 + 3.0 * _GELU_K * x2)
    return g, dg


def _log_sigmoid(x):
    e = jnp.exp(-jnp.abs(x))
    u = 1.0 + e
    l1p = jnp.where(u == 1.0, e, jnp.log(u) * e / (u - 1.0))
    return jnp.minimum(x, 0.0) - l1p


def _neg_expm1(z):
    series = -z * (1.0 + z * (0.5 + z * (1.0 / 6.0 + z * (1.0 / 24.0 + z * (1.0 / 120.0 + z * (1.0 / 720.0))))))
    return jnp.where(z > -0.2, series, 1.0 - jnp.exp(z))


def _rows(shape):
    return lax.broadcasted_iota(jnp.int32, shape, 0)


def _shift_down(x, k):
    return jnp.where(_rows(x.shape) >= k, pltpu.roll(x, k, 0), 0.0)


def _shift_up(x, k):
    n = x.shape[0]
    return jnp.where(_rows(x.shape) < n - k, pltpu.roll(x, n - k, 0), 0.0)


def _delays(x, k_width):
    return [x] + [_shift_down(x, j) for j in range(1, k_width)]


def _advances(dy, k_width):
    return [dy] + [_shift_up(dy, j) for j in range(1, k_width)]


def _taps_sum(shifted, w_ref, b=None):
    k_width = w_ref.shape[0]
    y = w_ref[k_width - 1:k_width, :] * shifted[0]
    for j in range(1, k_width):
        y = y + w_ref[k_width - 1 - j:k_width - j, :] * shifted[j]
    if b is not None:
        y = y + b
    return y


def _causal_conv(x, w_ref, b=None):
    return _taps_sum(_delays(x, w_ref.shape[0]), w_ref, b)


def _conv_wgrad(advanced, x):
    k_width = len(advanced)
    rows = [jnp.sum(advanced[k_width - 1 - k] * x, axis=0, keepdims=True) for k in range(k_width)]
    return jnp.concatenate(rows, axis=0)


def _dot(a, b):
    return jnp.dot(a, b, preferred_element_type=F32)


def _dot_nt(a, b):
    return lax.dot_general(a, b, (((1,), (1,)), ((), ())), preferred_element_type=F32)


def _dot_tn(a, b):
    return lax.dot_general(a, b, (((0,), (0,)), ((), ())), preferred_element_type=F32)


def _rms_stats(x):
    r = lax.rsqrt(jnp.mean(x * x, axis=-1, keepdims=True) + RMS_EPS)
    return x * r, r


def _rms_bwd(n, r, g, dy):
    dn = dy * g
    dx = r * (dn - n * jnp.mean(dn * n, axis=-1, keepdims=True))
    return dx, dy * n


def _scan_forward(a_ref, b_ref, h_ref):
    n, c = a_ref.shape
    row = lax.broadcasted_iota(jnp.int32, (SUBLANES, c), 0)

    def group(g, carry):
        r0 = pl.multiple_of(g * SUBLANES, SUBLANES)
        a = a_ref[pl.ds(r0, SUBLANES), :]
        b = b_ref[pl.ds(r0, SUBLANES), :]
        for k in (1, 2, 4):
            ap = jnp.where(row >= k, pltpu.roll(a, k, 0), 1.0)
            bp = jnp.where(row >= k, pltpu.roll(b, k, 0), 0.0)
            b = a * bp + b
            a = a * ap
        h = a * carry + b
        h_ref[pl.ds(r0, SUBLANES), :] = h
        return h[SUBLANES - 1:SUBLANES, :]

    lax.fori_loop(0, n // SUBLANES, group, jnp.zeros((1, c), F32))


def _scan_backward(c_ref, b_ref, g_ref):
    n, ch = c_ref.shape
    row = lax.broadcasted_iota(jnp.int32, (SUBLANES, ch), 0)
    n_groups = n // SUBLANES

    def group(i, carry):
        r0 = pl.multiple_of((n_groups - 1 - i) * SUBLANES, SUBLANES)
        a = c_ref[pl.ds(r0, SUBLANES), :]
        b = b_ref[pl.ds(r0, SUBLANES), :]
        for k in (1, 2, 4):
            keep = row < SUBLANES - k
            ap = jnp.where(keep, pltpu.roll(a, SUBLANES - k, 0), 1.0)
            bp = jnp.where(keep, pltpu.roll(b, SUBLANES - k, 0), 0.0)
            b = a * bp + b
            a = a * ap
        g = a * carry + b
        g_ref[pl.ds(r0, SUBLANES), :] = g
        return g[0:1, :]

    lax.fori_loop(0, n_groups, group, jnp.zeros((1, ch), F32))


MESH = pl.DeviceIdType.MESH
_HBM = pl.BlockSpec(memory_space=pltpu.HBM)
_OTHER_CHIPS = ((1, 0), (0, 1), (1, 1))
_OTHER_DEVICES = tuple((dx, dy, dc) for dx in (0, 1) for dy in (0, 1) for dc in (0, 1) if dx or dy or dc)
N_DEVICES = 8


def _position():
    return lax.axis_index("x"), lax.axis_index("y"), lax.axis_index("c")


def _flip(v, d):
    return 1 - v if d else v


def _chip(x, y, p):
    px, py = _flip(x, _OTHER_CHIPS[p][0]), _flip(y, _OTHER_CHIPS[p][1])
    return px, py, 2 * px + py


class _Ride:
    def __init__(self, srcs, bufs, scratch, plan, collective_id):
        self.srcs, self.bufs, self.scratch, self.plan = list(srcs), list(bufs), list(scratch), plan
        self.collective_id = collective_id


NEIGHBOURS_AND_SIBLING = 1
OTHER_CHIPS_SAME_CORE = 2
ALL_DEVICES = 3
SIBLING = 4


def _handshake(peers):
    barrier = pltpu.get_barrier_semaphore()
    for peer in peers:
        pl.semaphore_signal(barrier, inc=1, device_id=peer, device_id_type=MESH)
    pl.semaphore_wait(barrier, len(peers))


def _call(body, *, name, grid, in_specs, out_specs, out_shape, operands, scratch_shapes=(), ride=None):
    in_specs, out_specs, out_shape = list(in_specs), list(out_specs), list(out_shape)
    scratch_shapes = list(scratch_shapes)
    if ride is None:
        return pl.pallas_call(body, name=name, grid=grid, in_specs=in_specs, out_specs=out_specs, out_shape=out_shape,
                              scratch_shapes=scratch_shapes, compiler_params=_params())(*operands)
    n_in, n_out, n_scr = len(in_specs), len(out_shape), len(scratch_shapes)
    old = [i for i, b in enumerate(ride.bufs) if not isinstance(b, jax.ShapeDtypeStruct)]
    n_src, n_old, n_buf = len(ride.srcs), len(old), len(ride.bufs)

    def full_body(*refs):
        o0 = n_in + n_src + n_old
        s0 = o0 + n_out + n_buf
        start, relay, relay_on, finish = ride.plan(refs[n_in:n_in + n_src], refs[o0 + n_out:s0], refs[s0 + n_scr:])
        ids = [pl.program_id(i) for i in range(len(grid))]
        first = functools.reduce(jnp.logical_and, [i == 0 for i in ids])
        middle = functools.reduce(jnp.logical_and, [ids[0] == grid[0] // 2] + [i == 0 for i in ids[1:]])
        last = functools.reduce(jnp.logical_and, [i == g - 1 for i, g in zip(ids, grid)])
        pl.when(first)(start)
        pl.when(middle)(relay)
        pl.when(last)(relay_on)
        body(*refs[:n_in], *refs[o0:o0 + n_out], *refs[s0:s0 + n_scr])
        pl.when(last)(finish)

    shapes = [jax.ShapeDtypeStruct(b.shape, b.dtype) for b in ride.bufs]
    res = pl.pallas_call(
        full_body, name=name, grid=grid,
        in_specs=in_specs + [_HBM] * (n_src + n_old), out_specs=out_specs + [_HBM] * n_buf,
        out_shape=out_shape + shapes, scratch_shapes=scratch_shapes + ride.scratch,
        input_output_aliases={n_in + n_src + k: n_out + i for k, i in enumerate(old)},
        compiler_params=_params(collective_id=ride.collective_id),
    )(*operands, *ride.srcs, *[ride.bufs[i] for i in old])
    return list(res[:n_out]), list(res[n_out:])


def run_ride(ride, name):
    def body():
        pass

    return _call(body, name=name, grid=(1,), in_specs=[], out_specs=[], out_shape=[], operands=[], ride=ride)[1]


def gather_ride(shards, items=None, into=None):
    items = items or [(a, 0, s.shape[0]) for a, s in enumerate(shards)]
    bufs = into or [jax.ShapeDtypeStruct((N_CHIPS,) + s.shape, s.dtype) for s in shards]
    nrel = len(_OTHER_CHIPS)

    def plan(srcs, dsts, sems):
        ici_send, ici_recv, hop_send, hop_recv, sib_send, sib_recv = sems
        x, y, c = _position()
        j = 2 * x + y

        def rows(ref, it, h, q=None):
            half = it[2] // 2
            if q is None:
                return ref.at[pl.ds(it[1] + h * half, half), :]
            return ref.at[pl.ds(it[1] + h * half + q * (half // 2), half // 2), :]

        def ici(i, p, slot):
            it = items[i]
            px, py, _ = _chip(x, y, p)
            return pltpu.make_async_remote_copy(
                src_ref=rows(srcs[it[0]], it, c), dst_ref=rows(dsts[it[0]].at[slot], it, c),
                send_sem=ici_send.at[i * nrel + p], recv_sem=ici_recv.at[i * nrel + p],
                device_id=(px, py, c), device_id_type=MESH)

        def hop(i, p, slot):
            it = items[i]
            part = rows(dsts[it[0]].at[slot], it, c, p)
            px, py, _ = _chip(x, y, 1 - p)
            return pltpu.make_async_remote_copy(
                src_ref=part, dst_ref=part, send_sem=hop_send.at[i * 2 + p], recv_sem=hop_recv.at[i * 2 + p],
                device_id=(px, py, c), device_id_type=MESH)

        def sib(i, p, h):
            it = items[i]
            part = rows(dsts[it[0]].at[_chip(x, y, p)[2]], it, h)
            return pltpu.make_async_remote_copy(
                src_ref=part, dst_ref=part, send_sem=sib_send.at[i * nrel + p], recv_sem=sib_recv.at[i * nrel + p],
                device_id=(x, y, 1 - c), device_id_type=MESH)

        every = range(len(items))
        diag = _chip(x, y, 2)[2]

        def start():
            _handshake([_chip(x, y, 0)[:2] + (c,), _chip(x, y, 1)[:2] + (c,), (x, y, 1 - c)])
            for i in every:
                for p in (0, 1):
                    ici(i, p, j).start()

        def relay():
            for i in every:
                for p in (0, 1):
                    k = _chip(x, y, p)[2]
                    ici(i, p, k).wait_recv()
                    hop(i, p, k).start()
                    sib(i, p, c).start()

        def relay_on():
            for i in every:
                for p in (0, 1):
                    hop(i, p, diag).wait_recv()
                sib(i, 2, c).start()

        def finish():
            for i in every:
                for p in range(nrel):
                    sib(i, p, 1 - c).wait_recv()
            for i in every:
                for p in (0, 1):
                    ici(i, p, j).wait_send()
                    hop(i, p, _chip(x, y, p)[2]).wait_send()
                for p in range(nrel):
                    sib(i, p, c).wait_send()

        return start, relay, relay_on, finish

    n = len(items)
    sems = [pltpu.SemaphoreType.DMA((n * nrel,))] * 2 + [pltpu.SemaphoreType.DMA((n * 2,))] * 2 \
        + [pltpu.SemaphoreType.DMA((n * nrel,))] * 2
    return _Ride(shards, bufs, sems, plan, NEIGHBOURS_AND_SIBLING)


def exchange_ride(sums, items=None, into=None, rep=None):
    items = [(a, 0, s.shape[1]) for a, s in enumerate(sums)] if items is None else items
    into = into or [None] * len(sums)
    bufs = [jax.ShapeDtypeStruct(s.shape, s.dtype) if b is None else b for s, b in zip(sums, into)]
    srcs = list(sums)
    scratch = [pltpu.SemaphoreType.DMA((max(len(items), 1) * len(_OTHER_CHIPS),))] * 2
    if rep is not None:
        srcs.append(rep)
        bufs.append(jax.ShapeDtypeStruct((N_DEVICES,) + rep.shape, rep.dtype))
        scratch += [pltpu.SemaphoreType.DMA((len(_OTHER_DEVICES),))] * 2
    nrel = len(_OTHER_CHIPS)

    def plan(src_refs, dst_refs, sems):
        x, y, c = _position()
        j = 2 * x + y
        me = 4 * x + 2 * y + c

        def part(i, p, src_slot, dst_slot):
            a, r0, nr = items[i]
            px, py, _ = _chip(x, y, p)
            return pltpu.make_async_remote_copy(
                src_ref=src_refs[a].at[src_slot, pl.ds(r0, nr), :], dst_ref=dst_refs[a].at[dst_slot, pl.ds(r0, nr), :],
                send_sem=sems[0].at[i * nrel + p], recv_sem=sems[1].at[i * nrel + p],
                device_id=(px, py, c), device_id_type=MESH)

        def device(q):
            dx, dy, dc = _OTHER_DEVICES[q]
            return _flip(x, dx), _flip(y, dy), _flip(c, dc)

        def rep_copy(q, slot):
            return pltpu.make_async_remote_copy(
                src_ref=src_refs[-1], dst_ref=dst_refs[-1].at[slot], send_sem=sems[2].at[q], recv_sem=sems[3].at[q],
                device_id=device(q), device_id_type=MESH)

        pairs = [(i, p) for i in range(len(items)) for p in range(nrel)]
        others = range(len(_OTHER_DEVICES)) if rep is not None else ()

        def start():
            if rep is None:
                _handshake([_chip(x, y, p)[:2] + (c,) for p in range(nrel)])
            else:
                _handshake([device(q) for q in others])
            for i, p in pairs:
                part(i, p, _chip(x, y, p)[2], j).start()
            for q in others:
                rep_copy(q, me).start()

        def finish():
            for i, p in pairs:
                k = _chip(x, y, p)[2]
                part(i, p, k, k).wait_recv()
            for q in others:
                px, py, pc = device(q)
                rep_copy(q, 4 * px + 2 * py + pc).wait_recv()
            for i, p in pairs:
                part(i, p, _chip(x, y, p)[2], j).wait_send()
            for q in others:
                rep_copy(q, me).wait_send()

        return start, lambda: None, lambda: None, finish

    return _Ride(srcs, bufs, scratch, plan, OTHER_CHIPS_SAME_CORE if rep is None else ALL_DEVICES)


def _own_slot(buf, own, index):
    return lax.dynamic_update_slice(buf, own[None], (index,) + (0,) * own.ndim)


def _token_tile(s):
    return min(s, 512)


def norm_in(x, g):
    s, d = x.shape
    t = _token_tile(s)

    def body(x_ref, g_ref, o_ref, ot_ref):
        n, _ = _rms_stats(x_ref[...])
        h = n * g_ref[...]
        o_ref[...] = h.astype(BF16)
        ot_ref[...] = h.T.astype(BF16)

    return pl.pallas_call(
        body, name="norm_in", grid=(s // t,),
        in_specs=[pl.BlockSpec((t, d), lambda i: (i, 0)), pl.BlockSpec((1, d), lambda i: (0, 0))],
        out_specs=[pl.BlockSpec((t, d), lambda i: (i, 0)), pl.BlockSpec((d, t), lambda i: (0, i))],
        out_shape=[jax.ShapeDtypeStruct((s, d), BF16), jax.ShapeDtypeStruct((d, s), BF16)],
        compiler_params=_params(),
    )(x, g)


def matmul_cols(a, w4, name, ride=None):
    m, k = a.shape
    nj, _, ns = w4.shape
    nb = ns // CW

    def body(a_ref, w_ref, o_ref):
        o_ref[...] = _dot(a_ref[...], w_ref[0])

    return _call(
        body, name=name, grid=(nj, nb),
        in_specs=[pl.BlockSpec((m, k), lambda j, b: (0, 0)),
                  pl.BlockSpec((1, k, CW), lambda j, b: (j, 0, b))],
        out_specs=[pl.BlockSpec((m, CW), lambda j, b: (0, j * nb + b))],
        out_shape=[jax.ShapeDtypeStruct((m, nj * ns), F32)],
        operands=(a, w4), ride=ride)


def mix_conv_fwd(proj, ws, ride=None):
    s = proj.shape[0]
    nblk = D_MODEL // CW

    def body(cb_ref, cc_ref, cx_ref, ws_ref, q_ref, ya_ref):
        q = _causal_conv(cc_ref[...] * cx_ref[...], ws_ref)
        q_ref[...] = q
        ya_ref[...] = (cb_ref[...] * q).astype(BF16)

    seg = lambda k: pl.BlockSpec((s, CW), lambda c, k=k: (0, k * nblk + c))
    return _call(
        body, name="mix_conv_fwd", grid=(nblk,),
        in_specs=[seg(0), seg(1), seg(2), pl.BlockSpec((3, CW), lambda c: (0, c))],
        out_specs=[pl.BlockSpec((s, CW), lambda c: (0, c))] * 2,
        out_shape=[jax.ShapeDtypeStruct((s, D_MODEL), F32), jax.ShapeDtypeStruct((s, D_MODEL), BF16)],
        operands=(proj, proj, proj, ws), ride=ride)


def _lru_gates(r, ls):
    log_a = LRU_C * r * ls
    a = jnp.exp(log_a)
    mult = jnp.sqrt(_neg_expm1(2.0 * log_a))
    mult = jnp.where(_rows(r.shape) == 0, 1.0, mult)
    return a, mult


def mix_lru_fwd(proj, wl, bl, wa, ba, wx, bx, lam, ride=None):
    s = proj.shape[0]
    nblk = D_MODEL // CW

    def body(lx_ref, ly_ref, wl_ref, bl_ref, wa_ref, ba_ref, wx_ref, bx_ref, lam_ref,
             xl_ref, r_ref, i_ref, h_ref, yb_ref, a_scr, u_scr):
        xl = _causal_conv(lx_ref[...], wl_ref, bl_ref[...])
        xlb = xl.astype(BF16)
        xl_ref[...] = xlb
        r = _sigmoid(_dot(xlb, wa_ref[0]) + ba_ref[...])
        i = _sigmoid(_dot(xlb, wx_ref[0]) + bx_ref[...])
        r_ref[...] = r.astype(BF16)
        i_ref[...] = i.astype(BF16)
        a, mult = _lru_gates(r, _log_sigmoid(lam_ref[...]))
        a_scr[...] = a
        u_scr[...] = mult * i * xl
        _scan_forward(a_scr, u_scr, h_ref)
        yb_ref[...] = (h_ref[...] * _gelu(ly_ref[...])).astype(BF16)

    blk = lambda k: pl.BlockSpec((s, CW), lambda c, k=k: (0, k * nblk + c))
    vec = pl.BlockSpec((1, CW), lambda c: (0, c))
    mat = pl.BlockSpec((1, CW, CW), lambda c: (c, 0, 0))
    out = pl.BlockSpec((s, CW), lambda c: (0, c))
    f = jax.ShapeDtypeStruct((s, D_MODEL), F32)
    hb = jax.ShapeDtypeStruct((s, D_MODEL), BF16)
    return _call(
        body, name="mix_lru_fwd", grid=(nblk,),
        in_specs=[blk(3), blk(4), pl.BlockSpec((4, CW), lambda c: (0, c)), vec, mat, vec, mat, vec, vec],
        out_specs=[out] * 5,
        out_shape=[hb, hb, hb, f, hb],
        scratch_shapes=[pltpu.VMEM((s, CW), F32), pltpu.VMEM((s, CW), F32)],
        operands=(proj, proj, wl, bl, wa, ba, wx, bx, lam), ride=ride)


def branch_merge_fwd(ya, yb, wcb, wlb, proj, ride=None):
    s = ya.shape[0]
    nblk = D_MODEL // CW

    def body(ya_ref, yb_ref, wcb_ref, wlb_ref, gc_ref, gl_ref, a_ref, b_ref, m_ref):
        a = _dot(ya_ref[...], wcb_ref[...])
        b = _dot(yb_ref[...], wlb_ref[...])
        a_ref[...] = a
        b_ref[...] = b
        m_ref[...] = (_sigmoid(gc_ref[...]) * a + _sigmoid(gl_ref[...]) * b).astype(BF16)

    res = pl.BlockSpec((s, D_MODEL), lambda n: (0, 0))
    wcol = pl.BlockSpec((D_MODEL, CW), lambda n: (0, n))
    blk = lambda k: pl.BlockSpec((s, CW), lambda n, k=k: (0, k * nblk + n))
    out = pl.BlockSpec((s, CW), lambda n: (0, n))
    f = jax.ShapeDtypeStruct((s, D_MODEL), F32)
    return _call(
        body, name="branch_merge_fwd", grid=(nblk,),
        in_specs=[res, res, wcol, wcol, blk(5), blk(6)],
        out_specs=[out] * 3,
        out_shape=[f, f, jax.ShapeDtypeStruct((s, D_MODEL), BF16)],
        operands=(ya, yb, wcb, wlb, proj, proj), ride=ride)


def mix_out_fwd(merged, wout, x, g2, g3, ride=None):
    s, d = x.shape
    t = _token_tile(s)

    def body(m_ref, w_ref, x_ref, g2_ref, g3_ref, mix_ref, x2_ref, h2_ref, h2t_ref):
        mix = _dot(m_ref[...], w_ref[...])
        mix_ref[...] = mix
        n, _ = _rms_stats(mix)
        x2 = x_ref[...] + n * g2_ref[...]
        x2_ref[...] = x2
        n2, _ = _rms_stats(x2)
        h2 = n2 * g3_ref[...]
        h2_ref[...] = h2.astype(BF16)
        h2t_ref[...] = h2.T.astype(BF16)

    tile = pl.BlockSpec((t, d), lambda i: (i, 0))
    vec = pl.BlockSpec((1, d), lambda i: (0, 0))
    f = jax.ShapeDtypeStruct((s, d), F32)
    return _call(
        body, name="mix_out_fwd", grid=(s // t,),
        in_specs=[tile, pl.BlockSpec((d, d), lambda i: (0, 0)), tile, vec, vec],
        out_specs=[tile] * 3 + [pl.BlockSpec((d, t), lambda i: (0, i))],
        out_shape=[f, f, jax.ShapeDtypeStruct((s, d), BF16), jax.ShapeDtypeStruct((d, s), BF16)],
        operands=(merged, wout, x, g2, g3), ride=ride)


def ffn_up_act_fwd(h2, wup4, fw, fb, ride=None):
    s, k = h2.shape
    ns = wup4.shape[2]
    per_chip = ns // CW
    nblk = D_FF // CW

    def body(h_ref, wg_ref, wv_ref, cg_ref, cv_ref, bg_ref, bv_ref, up_ref, act_ref, f_ref):
        h = h_ref[...]
        ug = _dot(h, wg_ref[0])
        uv = _dot(h, wv_ref[0])
        up_ref[0] = ug
        up_ref[1] = uv
        gate = _causal_conv(ug, cg_ref, bg_ref[...])
        val = _causal_conv(uv, cv_ref, bv_ref[...])
        act_ref[0] = gate.astype(BF16)
        act_ref[1] = val.astype(BF16)
        f_ref[...] = (_gelu(gate) * val).astype(BF16)

    wcols = lambda h: pl.BlockSpec((1, k, CW), lambda n, h=h: (n // per_chip + 2 * h, 0, n % per_chip))
    half = lambda h, rows: pl.BlockSpec((rows, CW), lambda n, h=h: (0, h * nblk + n))
    both = pl.BlockSpec((2, s, CW), lambda n: (0, 0, n))
    return _call(
        body, name="ffn_up_act_fwd", grid=(nblk,),
        in_specs=[pl.BlockSpec((s, k), lambda n: (0, 0)), wcols(0), wcols(1),
                  half(0, 3), half(1, 3), half(0, 1), half(1, 1)],
        out_specs=[both, both, pl.BlockSpec((s, CW), lambda n: (0, n))],
        out_shape=[jax.ShapeDtypeStruct((2, s, D_FF), F32), jax.ShapeDtypeStruct((2, s, D_FF), BF16),
                   jax.ShapeDtypeStruct((s, D_FF), BF16)],
        operands=(h2, wup4, wup4, fw, fw, fb, fb), ride=ride)


def ffn_down_loss(f, wdown, x2, target, g4):
    s, d = x2.shape
    t = _token_tile(s)

    def body(f_ref, w_ref, x2_ref, tg_ref, g4_ref, dy_ref, dout_ref, loss_ref, dg4_ref):
        @pl.when(pl.program_id(0) == 0)
        def _():
            loss_ref[...] = jnp.zeros_like(loss_ref)
            dg4_ref[...] = jnp.zeros_like(dg4_ref)

        out = _dot(f_ref[...], w_ref[...])
        n, r = _rms_stats(out)
        err = x2_ref[...] + n * g4_ref[...] - tg_ref[...]
        loss_ref[...] += jnp.full(loss_ref.shape, (0.5 / d) * jnp.sum(err * err), F32)
        dy = err * (1.0 / d)
        dy_ref[...] = dy
        dout, dg = _rms_bwd(n, r, g4_ref[...], dy)
        dout_ref[...] = dout.astype(BF16)
        dg4_ref[...] += jnp.sum(dg, axis=0, keepdims=True)

    tile = pl.BlockSpec((t, d), lambda i: (i, 0))
    vec = pl.BlockSpec((1, d), lambda i: (0, 0))
    return pl.pallas_call(
        body, name="ffn_down_loss", grid=(s // t,),
        in_specs=[pl.BlockSpec((t, D_FF), lambda i: (i, 0)), pl.BlockSpec((D_FF, d), lambda i: (0, 0)), tile, tile, vec],
        out_specs=[tile, tile, pl.BlockSpec((1, 128), lambda i: (0, 0)), vec],
        out_shape=[jax.ShapeDtypeStruct((s, d), F32), jax.ShapeDtypeStruct((s, d), BF16),
                   jax.ShapeDtypeStruct((1, 128), F32), jax.ShapeDtypeStruct((1, d), F32)],
        compiler_params=_params(),
    )(f, wdown, x2, target, g4)


def ffn_up_bwd(dout, wdown, up, act, f, fw, wup4, h2t, ride=None):
    k, s = h2t.shape
    nblk = D_FF // FW
    per_chip = wup4.shape[2] // FW

    def body(do_ref, wd_ref, up_ref, act_ref, f_ref, cg_ref, cv_ref, wg_ref, wv_ref, h_ref,
             dh_ref, dwu_ref, dwd_ref, dw_ref, db_ref, dup_scr):
        @pl.when(pl.program_id(0) == 0)
        def _():
            dup_scr[...] = jnp.zeros_like(dup_scr)
            dh_ref[...] = jnp.zeros_like(dh_ref)

        do = do_ref[...]
        df = _dot_nt(do, wd_ref[...])
        dg = dup_scr[0]
        dv = dup_scr[1]
        ht = h_ref[...]
        dh_ref[...] += _dot_nt(dg, wg_ref[0]) + _dot_nt(dv, wv_ref[0])
        dwu_ref[0] = _dot(ht, dg).astype(BF16)
        dwu_ref[1] = _dot(ht, dv).astype(BF16)
        dwd_ref[...] = _dot_tn(f_ref[...], do).astype(BF16)
        val = act_ref[1].astype(F32)
        ge, dge = _gelu_and_grad(act_ref[0].astype(F32))
        dgate = _advances(df * val * dge, 3)
        dval = _advances(df * ge, 3)
        dw_ref[0] = _conv_wgrad(dgate, up_ref[0])
        dw_ref[1] = _conv_wgrad(dval, up_ref[1])
        db_ref[0] = jnp.sum(dgate[0], axis=0, keepdims=True)
        db_ref[1] = jnp.sum(dval[0], axis=0, keepdims=True)
        dup_scr[0] = _taps_sum(dgate, cg_ref).astype(BF16)
        dup_scr[1] = _taps_sum(dval, cv_ref).astype(BF16)

    cur = lambda n: jnp.minimum(n, nblk - 1)
    prev = lambda n: jnp.maximum(n - 1, 0)
    once = pl.Buffered(1)
    both = lambda rows: pl.BlockSpec((2, rows, FW), lambda n: (0, 0, cur(n)))
    taps = lambda h: pl.BlockSpec((3, FW), lambda n, h=h: (0, h * nblk + cur(n)))
    wcols = lambda h: pl.BlockSpec((1, k, FW), lambda n, h=h: (prev(n) // per_chip + 2 * h, 0, prev(n) % per_chip))
    return _call(
        body, name="ffn_up_bwd", grid=(nblk + 1,),
        in_specs=[pl.BlockSpec((s, D_MODEL), lambda n: (0, 0), pipeline_mode=once),
                  pl.BlockSpec((FW, D_MODEL), lambda n: (cur(n), 0)), both(s), both(s),
                  pl.BlockSpec((s, FW), lambda n: (0, cur(n))), taps(0), taps(1), wcols(0), wcols(1),
                  pl.BlockSpec((k, s), lambda n: (0, 0), pipeline_mode=once)],
        out_specs=[pl.BlockSpec((s, k), lambda n: (0, 0), pipeline_mode=once),
                   pl.BlockSpec((2, k, FW), lambda n: (0, 0, prev(n))),
                   pl.BlockSpec((FW, D_MODEL), lambda n: (cur(n), 0)), both(3), both(1)],
        out_shape=[jax.ShapeDtypeStruct((s, k), F32), jax.ShapeDtypeStruct((2, k, D_FF), BF16),
                   jax.ShapeDtypeStruct((D_FF, D_MODEL), BF16),
                   jax.ShapeDtypeStruct((2, 3, D_FF), F32), jax.ShapeDtypeStruct((2, 1, D_FF), F32)],
        scratch_shapes=[pltpu.VMEM((2, s, FW), BF16)],
        operands=(dout, wdown, up, act, f, fw, fw, wup4, wup4, h2t), ride=ride)


def matmul_cols_bwd(dy, other, name, wgrad, ride=None):
    m = dy[0].shape[1]
    if wgrad:
        k = other.shape[0]
        nj, nb = N_CHIPS, sum(d.shape[0] * d.shape[2] for d in dy) // (N_CHIPS * CW)
    else:
        nj, k, ns = other.shape
        nb = ns // CW
    per_seg = dy[0].shape[2] // CW
    first = [sum(d.shape[0] for d in dy[:i]) for i in range(len(dy))]

    def segment(j, b):
        return (j * nb + b) // per_seg, (j * nb + b) % per_seg

    def body(*refs):
        dy_refs, (o_ref, r_ref) = refs[:len(dy)], refs[len(dy):]
        seg, _ = segment(pl.program_id(0), pl.program_id(1))
        dyb = dy_refs[-1][0]
        for i in range(len(dy) - 2, -1, -1):
            dyb = jnp.where(seg < first[i + 1], dy_refs[i][0], dyb)
        if wgrad:
            r_ref[...] = _dot(o_ref[...], dyb).astype(BF16)
        else:
            @pl.when((pl.program_id(0) == 0) & (pl.program_id(1) == 0))
            def _():
                r_ref[...] = jnp.zeros_like(r_ref)

            r_ref[...] += _dot_nt(dyb, o_ref[0])

    def dy_spec(i):
        nseg = dy[i].shape[0]

        def index(j, b):
            seg, col = segment(j, b)
            local = seg - first[i]
            return (jnp.clip(local, 0, nseg - 1), 0,
                    jnp.where(local < 0, 0, jnp.where(local >= nseg, per_seg - 1, col)))

        return pl.BlockSpec((1, m, CW), index)

    if wgrad:
        other_spec = pl.BlockSpec((k, m), lambda j, b: (0, 0))
        out_spec = pl.BlockSpec((k, CW), lambda j, b: (0, j * nb + b))
        out_shape = jax.ShapeDtypeStruct((k, nj * nb * CW), BF16)
    else:
        other_spec = pl.BlockSpec((1, k, CW), lambda j, b: (j, 0, b))
        out_spec = pl.BlockSpec((m, k), lambda j, b: (0, 0))
        out_shape = jax.ShapeDtypeStruct((m, k), F32)
    return _call(
        body, name=name, grid=(nj, nb), in_specs=[dy_spec(i) for i in range(len(dy))] + [other_spec],
        out_specs=[out_spec], out_shape=[out_shape], operands=(*dy, other), ride=ride)


def norms_mid_bwd(dh2, x2, dy, mix, g3, g2, ride=None):
    s, d = x2.shape
    t = _token_tile(s)

    def body(dh2_ref, x2_ref, dy_ref, mix_ref, g3_ref, g2_ref, dx2_ref, dmix_ref, dg3_ref, dg2_ref):
        @pl.when(pl.program_id(0) == 0)
        def _():
            dg3_ref[...] = jnp.zeros_like(dg3_ref)
            dg2_ref[...] = jnp.zeros_like(dg2_ref)

        n3, r3 = _rms_stats(x2_ref[...])
        dx, dg3 = _rms_bwd(n3, r3, g3_ref[...], dh2_ref[...])
        dx2 = dy_ref[...] + dx
        dx2_ref[...] = dx2
        dg3_ref[...] += jnp.sum(dg3, axis=0, keepdims=True)
        n2, r2 = _rms_stats(mix_ref[...])
        dmix, dg2 = _rms_bwd(n2, r2, g2_ref[...], dx2)
        dmix_ref[...] = dmix.astype(BF16)
        dg2_ref[...] += jnp.sum(dg2, axis=0, keepdims=True)

    tile = pl.BlockSpec((t, d), lambda i: (i, 0))
    vec = pl.BlockSpec((1, d), lambda i: (0, 0))
    v = jax.ShapeDtypeStruct((1, d), F32)
    return _call(
        body, name="norms_mid_bwd", grid=(s // t,),
        in_specs=[tile, tile, tile, tile, vec, vec],
        out_specs=[tile, tile, vec, vec],
        out_shape=[jax.ShapeDtypeStruct((s, d), F32), jax.ShapeDtypeStruct((s, d), BF16), v, v],
        operands=(dh2, x2, dy, mix, g3, g2), ride=ride)


def mix_out_bwd(dmix, wout, merged, a, b, proj, ride=None):
    s = dmix.shape[0]
    nblk = D_MODEL // CW

    def body(dm_ref, w_ref, mg_ref, a_ref, b_ref, gc_ref, gl_ref, da_ref, db_ref, dw_ref, dg_ref):
        dm = dm_ref[...]
        dmerged = _dot_nt(dm, w_ref[...])
        dw_ref[...] = _dot_tn(mg_ref[...], dm).astype(BF16)
        sc = _sigmoid(gc_ref[...])
        sl = _sigmoid(gl_ref[...])
        da_ref[...] = (dmerged * sc).astype(BF16)
        db_ref[...] = (dmerged * sl).astype(BF16)
        dg_ref[0] = (dmerged * a_ref[...] * sc * (1.0 - sc)).astype(BF16)
        dg_ref[1] = (dmerged * b_ref[...] * sl * (1.0 - sl)).astype(BF16)

    res = pl.BlockSpec((s, D_MODEL), lambda n: (0, 0))
    rows = pl.BlockSpec((CW, D_MODEL), lambda n: (n, 0))
    col = pl.BlockSpec((s, CW), lambda n: (0, n))
    blk = lambda k: pl.BlockSpec((s, CW), lambda n, k=k: (0, k * nblk + n))
    hb = jax.ShapeDtypeStruct((s, D_MODEL), BF16)
    return _call(
        body, name="mix_out_bwd", grid=(nblk,),
        in_specs=[res, rows, col, col, col, blk(5), blk(6)],
        out_specs=[col, col, rows, pl.BlockSpec((2, s, CW), lambda n: (0, 0, n))],
        out_shape=[hb, hb, jax.ShapeDtypeStruct((D_MODEL, D_MODEL), BF16), jax.ShapeDtypeStruct((2, s, D_MODEL), BF16)],
        operands=(dmix, wout, merged, a, b, proj, proj), ride=ride)


def mix_conv_bwd(da, wcb, proj, q, ws, ride=None):
    s = da.shape[0]
    nblk = D_MODEL // CW

    def body(da_ref, w_ref, cb_ref, cc_ref, cx_ref, q_ref, ws_ref, dc_ref, dw_ref, dws_ref):
        dab = da_ref[...]
        dya = _dot_nt(dab, w_ref[...])
        cb = cb_ref[...]
        cc = cc_ref[...]
        cx = cx_ref[...]
        q = q_ref[...]
        dw_ref[...] = _dot_tn((cb * q).astype(BF16), dab).astype(BF16)
        dc_ref[0] = (dya * q).astype(BF16)
        dq = _advances(dya * cb, 3)
        dp = _taps_sum(dq, ws_ref)
        dws_ref[...] = _conv_wgrad(dq, cc * cx)
        dc_ref[1] = (dp * cx).astype(BF16)
        dc_ref[2] = (dp * cc).astype(BF16)

    res = pl.BlockSpec((s, D_MODEL), lambda n: (0, 0))
    rows = pl.BlockSpec((CW, D_MODEL), lambda n: (n, 0))
    col = pl.BlockSpec((s, CW), lambda n: (0, n))
    blk = lambda k: pl.BlockSpec((s, CW), lambda n, k=k: (0, k * nblk + n))
    taps = pl.BlockSpec((3, CW), lambda n: (0, n))
    hb = jax.ShapeDtypeStruct((s, D_MODEL), BF16)
    return _call(
        body, name="mix_conv_bwd", grid=(nblk,),
        in_specs=[res, rows, blk(0), blk(1), blk(2), col, taps],
        out_specs=[pl.BlockSpec((3, s, CW), lambda n: (0, 0, n)), rows, taps],
        out_shape=[jax.ShapeDtypeStruct((3, s, D_MODEL), BF16), jax.ShapeDtypeStruct((D_MODEL, D_MODEL), BF16),
                   jax.ShapeDtypeStruct((3, D_MODEL), F32)],
        operands=(da, wcb, proj, proj, proj, q, ws), ride=ride)


def mix_lru_bwd(db, wlb, proj, xl, r, i, h, wl, wa, wx, lam, ride=None):
    s = db.shape[0]
    nblk = D_MODEL // CW

    def body(db_ref, w_ref, lx_ref, ly_ref, xl_ref, r_ref, i_ref, h_ref, wl_ref, wa_ref, wx_ref, lam_ref,
             dl_ref, dw_ref, dwa_ref, dwx_ref, dba_ref, dbx_ref, dwl_ref, dbl_ref, dlam_ref,
             c_scr, g_scr):
        dbb = db_ref[...]
        dyb = _dot_nt(dbb, w_ref[...])
        h = h_ref[...]
        ge, dge = _gelu_and_grad(ly_ref[...])
        dw_ref[...] = _dot_tn((h * ge).astype(BF16), dbb).astype(BF16)
        dl_ref[1] = (dyb * h * dge).astype(BF16)
        r = r_ref[...].astype(F32)
        gi = i_ref[...].astype(F32)
        xlb = xl_ref[...]
        xl = xlb.astype(F32)
        lam = lam_ref[...]
        ls = _log_sigmoid(lam)
        a, mult = _lru_gates(r, ls)
        c_scr[...] = _shift_up(a, 1)
        g_scr[...] = dyb * ge
        _scan_backward(c_scr, g_scr, g_scr)
        du = g_scr[...]
        da = du * _shift_down(h, 1)
        dmult = du * gi * xl
        di = du * mult * xl
        dxl = du * mult * gi
        first = _rows(a.shape) == 0
        dlog_a = da * a - jnp.where(first, 0.0, dmult * a * a / mult)
        dr = dlog_a * (LRU_C * ls)
        dlam_ref[...] = jnp.sum(dlog_a * r, axis=0, keepdims=True) * (LRU_C * (1.0 - _sigmoid(lam)))
        dzr = dr * r * (1.0 - r)
        dzi = di * gi * (1.0 - gi)
        dba_ref[...] = jnp.sum(dzr, axis=0, keepdims=True)
        dbx_ref[...] = jnp.sum(dzi, axis=0, keepdims=True)
        dzrb = dzr.astype(BF16)
        dzib = dzi.astype(BF16)
        dwa_ref[0] = _dot_tn(xlb, dzrb)
        dwx_ref[0] = _dot_tn(xlb, dzib)
        dxl = _advances(dxl + _dot_nt(dzrb, wa_ref[0]) + _dot_nt(dzib, wx_ref[0]), 4)
        dl_ref[0] = _taps_sum(dxl, wl_ref).astype(BF16)
        dwl_ref[...] = _conv_wgrad(dxl, lx_ref[...])
        dbl_ref[...] = jnp.sum(dxl[0], axis=0, keepdims=True)

    res = pl.BlockSpec((s, D_MODEL), lambda n: (0, 0))
    rows = pl.BlockSpec((CW, D_MODEL), lambda n: (n, 0))
    col = pl.BlockSpec((s, CW), lambda n: (0, n))
    blk = lambda k: pl.BlockSpec((s, CW), lambda n, k=k: (0, k * nblk + n))
    taps = pl.BlockSpec((4, CW), lambda n: (0, n))
    vec = pl.BlockSpec((1, CW), lambda n: (0, n))
    mat = pl.BlockSpec((1, CW, CW), lambda n: (n, 0, 0))
    hb = jax.ShapeDtypeStruct((s, D_MODEL), BF16)
    v = jax.ShapeDtypeStruct((1, D_MODEL), F32)
    m = jax.ShapeDtypeStruct((LRU_HEADS, HEAD_DIM, HEAD_DIM), F32)
    scr = pltpu.VMEM((s, CW), F32)
    return _call(
        body, name="mix_lru_bwd", grid=(nblk,),
        in_specs=[res, rows, blk(3), blk(4), col, col, col, col, taps, mat, mat, vec],
        out_specs=[pl.BlockSpec((2, s, CW), lambda n: (0, 0, n)), rows, mat, mat, vec, vec, taps, vec, vec],
        out_shape=[jax.ShapeDtypeStruct((2, s, D_MODEL), BF16), jax.ShapeDtypeStruct((D_MODEL, D_MODEL), BF16), m, m, v, v,
                   jax.ShapeDtypeStruct((4, D_MODEL), F32), v, v],
        scratch_shapes=[scr, scr],
        operands=(db, wlb, proj, proj, xl, r, i, h, wl, wa, wx, lam), ride=ride)


def norm_in_bwd(dh1, x, dx2, g1, ride=None):
    s, d = x.shape
    t = _token_tile(s)

    def body(dh_ref, x_ref, dx2_ref, g_ref, dx_ref, dg_ref):
        @pl.when(pl.program_id(0) == 0)
        def _():
            dg_ref[...] = jnp.zeros_like(dg_ref)

        n, r = _rms_stats(x_ref[...])
        dx, dg = _rms_bwd(n, r, g_ref[...], dh_ref[...])
        dx_ref[...] = dx2_ref[...] + dx
        dg_ref[...] += jnp.sum(dg, axis=0, keepdims=True)

    tile = pl.BlockSpec((t, d), lambda i: (i, 0))
    vec = pl.BlockSpec((1, d), lambda i: (0, 0))
    return _call(
        body, name="norm_in_bwd", grid=(s // t,),
        in_specs=[tile, tile, tile, vec],
        out_specs=[tile, vec],
        out_shape=[jax.ShapeDtypeStruct((s, d), F32), jax.ShapeDtypeStruct((1, d), F32)],
        operands=(dh1, x, dx2, g1), ride=ride)


def _owned_part(ref, kind, k, h, hr):
    if kind == "col":
        ns = ref.shape[1] // N_CHIPS
        return ref.at[pl.ds(h * hr, hr), pl.ds(k * ns, ns)]
    if kind == "row":
        return ref.at[pl.ds(k * 2 * hr + h * hr, hr), :]
    if kind == "col2":
        ns = ref.shape[2] // 2
        return ref.at[k // 2, pl.ds(h * hr, hr), pl.ds((k % 2) * ns, ns)]
    return ref.at[k, pl.ds(h * hr, hr), :]


def _part_shape(g, kind):
    if kind == "col2":
        return g.shape[1] // 2, g.shape[2] // 2
    if kind == "col":
        return g.shape[0] // 2, g.shape[1] // N_CHIPS
    if kind == "row":
        return g.shape[0] // (2 * N_CHIPS), g.shape[1]
    return g.shape[1] // 2, g.shape[2]


def pair_split(grads, kinds, name):
    n = len(grads)
    shapes = [_part_shape(g, k) for g, k in zip(grads, kinds)]

    def body(*refs):
        ins, theirs = refs[:n], refs[n:2 * n]
        send_sem, recv_sem = refs[2 * n:]
        x, y, c = _position()
        copies = []
        for a in range(n):
            hr = shapes[a][0]
            for k in range(N_CHIPS):
                s = a * N_CHIPS + k
                copies.append(pltpu.make_async_remote_copy(
                    src_ref=_owned_part(ins[a], kinds[a], k, 1 - c, hr), dst_ref=theirs[a].at[k],
                    send_sem=send_sem.at[s], recv_sem=recv_sem.at[s], device_id=(x, y, 1 - c), device_id_type=MESH))
        _handshake([(x, y, 1 - c)])
        for cp in copies:
            cp.start()
        for cp in copies:
            cp.wait()

    return pl.pallas_call(
        body, name=name,
        in_specs=[_HBM] * n, out_specs=[_HBM] * n,
        out_shape=[jax.ShapeDtypeStruct((N_CHIPS,) + shp, g.dtype) for shp, g in zip(shapes, grads)],
        scratch_shapes=[pltpu.SemaphoreType.DMA((n * N_CHIPS,))] * 2,
        compiler_params=pltpu.CompilerParams(collective_id=SIBLING),
    )(*grads)


def pair_swap(halves):
    n = len(halves)

    def body(*refs):
        ins, outs = refs[:n], refs[n:2 * n]
        send_sem, recv_sem = refs[2 * n:]
        x, y, c = _position()
        copies = [pltpu.make_async_remote_copy(
            src_ref=ins[a], dst_ref=outs[a], send_sem=send_sem.at[a], recv_sem=recv_sem.at[a],
            device_id=(x, y, 1 - c), device_id_type=MESH) for a in range(n)]
        _handshake([(x, y, 1 - c)])
        for cp in copies:
            cp.start()
        for cp in copies:
            cp.wait()

    return pl.pallas_call(
        body, name="pair_swap",
        in_specs=[_HBM] * n, out_specs=[_HBM] * n,
        out_shape=[jax.ShapeDtypeStruct(h.shape, h.dtype) for h in halves],
        scratch_shapes=[pltpu.SemaphoreType.DMA((n,))] * 2,
        compiler_params=pltpu.CompilerParams(collective_id=SIBLING),
    )(*halves)


def _row_tile(rows, cols, limit_bytes=1 << 20):
    best = None
    for t in range(SUBLANES, rows + 1, SUBLANES):
        if rows % t == 0 and t * cols * 4 <= limit_bytes:
            best = t
    return best or rows


def add_pair(g, kind, theirs, core, name):
    nc, rows, cols = theirs.shape
    t = _row_tile(rows, cols, 4 << 20)
    nt = rows // t

    def body(core_ref, g_ref, b_ref, o_ref):
        mine = g_ref[...].reshape(t, cols)
        o_ref[0] = (mine.astype(F32) + b_ref[0].astype(F32)).astype(o_ref.dtype)

    if kind == "col":
        own = pl.BlockSpec((t, cols), lambda k, i, c: (c[0] * nt + i, k))
    elif kind == "col2":
        own = pl.BlockSpec((1, t, cols), lambda k, i, c: (k // 2, c[0] * nt + i, k % 2))
    elif kind == "row":
        own = pl.BlockSpec((t, cols), lambda k, i, c: ((2 * k + c[0]) * nt + i, 0))
    else:
        own = pl.BlockSpec((1, t, cols), lambda k, i, c: (k, c[0] * nt + i, 0))
    spec = pl.BlockSpec((1, t, cols), lambda k, i, c: (k, i, 0))
    return pl.pallas_call(
        body, name=name,
        grid_spec=pltpu.PrefetchScalarGridSpec(num_scalar_prefetch=1, grid=(nc, nt), in_specs=[own, spec], out_specs=spec),
        out_shape=jax.ShapeDtypeStruct(theirs.shape, theirs.dtype), compiler_params=_params(),
    )(core, g, theirs)


def sum_lead(a, name):
    nl, rows, cols = a.shape
    t = _row_tile(rows, cols, (1 << 20) // 2)

    def body(a_ref, o_ref):
        acc = a_ref[0].astype(F32)
        for s in range(1, nl):
            acc = acc + a_ref[s].astype(F32)
        o_ref[...] = acc

    return pl.pallas_call(
        body, name=name, grid=(rows // t,),
        in_specs=[pl.BlockSpec((nl, t, cols), lambda i: (0, i, 0))],
        out_specs=pl.BlockSpec((t, cols), lambda i: (i, 0)),
        out_shape=jax.ShapeDtypeStruct((rows, cols), F32), compiler_params=_params(),
    )(a)


def sum_chips(rx, csum, chip, name):
    nc, rows, cols = rx.shape
    t = _row_tile(rows, cols, 2 << 20)

    def body(chip_ref, r0, r1, r2, r3, own_ref, o_ref):
        acc = None
        for s, ref in enumerate((r0, r1, r2, r3)):
            term = jnp.where(chip_ref[0] == s, own_ref[0], ref[0]).astype(F32)
            acc = term if acc is None else acc + term
        o_ref[...] = acc

    def slot(s):
        return pl.BlockSpec((1, t, cols), lambda i, c, s=s: (jnp.where(c[0] == s, c[0] ^ 1, s), i, 0))

    return pl.pallas_call(
        body, name=name,
        grid_spec=pltpu.PrefetchScalarGridSpec(
            num_scalar_prefetch=1, grid=(rows // t,),
            in_specs=[slot(s) for s in range(nc)] + [pl.BlockSpec((1, t, cols), lambda i, c: (c[0], i, 0))],
            out_specs=pl.BlockSpec((t, cols), lambda i, c: (i, 0))),
        out_shape=jax.ShapeDtypeStruct((rows, cols), F32), compiler_params=_params(),
    )(chip, rx, rx, rx, rx, csum)


def _adamw_update(w, g, m, v):
    nm = ADAM_B1 * m + (1.0 - ADAM_B1) * g
    nv = ADAM_B2 * v + (1.0 - ADAM_B2) * (g * g)
    m_hat = nm * (1.0 / (1.0 - ADAM_B1 ** ADAM_STEP))
    v_hat = nv * (1.0 / (1.0 - ADAM_B2 ** ADAM_STEP))
    return -ADAM_LR * (m_hat / (jnp.sqrt(v_hat) + ADAM_EPS) + ADAM_WD * w), nm, nv


def adamw(w, g, m, v, name):
    rows, cols = w.shape
    t = _row_tile(rows, cols)

    def body(w_ref, g_ref, m_ref, v_ref, d_ref, nm_ref, nv_ref):
        d_ref[...], nm_ref[...], nv_ref[...] = _adamw_update(w_ref[...], g_ref[...], m_ref[...], v_ref[...])

    spec = pl.BlockSpec((t, cols), lambda i: (i, 0))
    shp = jax.ShapeDtypeStruct((rows, cols), F32)
    return pl.pallas_call(
        body, name=name, grid=(rows // t,), in_specs=[spec] * 4, out_specs=[spec] * 3,
        out_shape=[shp, shp, shp], compiler_params=_params(),
    )(w, g, m, v)


def adamw_halves(w, g_mine, g_other, m, v, core, name):
    rows, cols = w.shape
    hr = rows // 2
    t = _row_tile(hr, cols)
    nt = hr // t

    def body(core_ref, w_ref, gm_ref, go_ref, m_ref, v_ref, g_ref, d_ref, nm_ref, nv_ref):
        g = jnp.where(pl.program_id(0) // nt == core_ref[0], gm_ref[...], go_ref[...])
        g_ref[...] = g
        d_ref[...], nm_ref[...], nv_ref[...] = _adamw_update(w_ref[...], g, m_ref[...], v_ref[...])

    spec = pl.BlockSpec((t, cols), lambda i, c: (i, 0))
    half = pl.BlockSpec((t, cols), lambda i, c: (i % nt, 0))
    shp = jax.ShapeDtypeStruct((rows, cols), F32)
    return pl.pallas_call(
        body, name=name,
        grid_spec=pltpu.PrefetchScalarGridSpec(num_scalar_prefetch=1, grid=(2 * nt,),
                                               in_specs=[spec, half, half, spec, spec], out_specs=[spec] * 4),
        out_shape=[shp] * 4, compiler_params=_params(),
    )(core, w, g_mine, g_other, m, v)


WEIGHTS = ("norm_mix_pre", "norm_mix_post", "norm_ffn_pre", "norm_ffn_post", "w_in", "conv_short_w",
           "w_conv_branch", "lru_conv_w", "lru_conv_b", "lru_wa", "lru_ba", "lru_wx", "lru_bx", "lru_lambda",
           "w_lru_branch", "w_out", "ffn_w_up", "ffn_conv_w", "ffn_conv_b", "ffn_w_down")
BIG = ("w_in", "ffn_w_up", "w_conv_branch", "w_lru_branch", "w_out", "ffn_w_down")
BIG_KIND = ("col", "col", "row", "row", "row", "row")
SMALL = ("conv_short_w", "lru_conv_w", "lru_wa", "lru_ba", "lru_wx", "lru_bx", "ffn_conv_w")
REPL = ("norm_mix_pre", "norm_mix_post", "norm_ffn_pre", "norm_ffn_post", "lru_conv_b", "lru_lambda", "ffn_conv_b")
PACK_W = 256
SMALL_ROWS = 576
REPL_ROWS = 16
LOSS_ROW = 12
FFN_SHARD = 2 * D_FF // N_CHIPS
QUARTER = HEAD_DIM // N_CHIPS
SMALL_PARTS = (("conv_short_w", 3, (1, 3, PACK_W)), ("lru_conv_w", 4, (1, 4, PACK_W)),
               ("lru_wa", LRU_HEADS * QUARTER, (1, LRU_HEADS, QUARTER, HEAD_DIM)), ("lru_ba", 1, (1, LRU_HEADS, QUARTER)),
               ("lru_wx", LRU_HEADS * QUARTER, (1, LRU_HEADS, QUARTER, HEAD_DIM)), ("lru_bx", 1, (1, LRU_HEADS, QUARTER)),
               ("ffn_conv_w", 3 * FFN_SHARD // PACK_W, (1, 3, FFN_SHARD)))


def _pad8(nr):
    return -(-nr // SUBLANES) * SUBLANES


SMALL_OFFSET = {}
for _name, _nr, _ in SMALL_PARTS:
    SMALL_OFFSET[_name] = sum(_pad8(nr) for n, nr, _ in SMALL_PARTS[:len(SMALL_OFFSET)])
FFN_ROWS = FFN_SHARD // PACK_W
BIASES = ("lru_ba", "lru_bx")


def pack_small(dicts):
    names = [n for n, _, _ in SMALL_PARTS]
    operands = [d[n].reshape(1, PACK_W) if n in BIASES else d[n] for d in dicts for n in names]

    def body(*refs):
        ins, outs = refs[:len(operands)], refs[len(operands):]
        for i, o in enumerate(outs):
            o[...] = jnp.zeros_like(o)
            for (name, nr, shape), p in zip(SMALL_PARTS, ins[i * len(names):(i + 1) * len(names)]):
                r0 = SMALL_OFFSET[name]
                if name in BIASES:
                    o[r0:r0 + 1, :] = p[...]
                elif name == "ffn_conv_w":
                    for k in range(shape[1]):
                        for s in range(FFN_ROWS):
                            o[r0 + FFN_ROWS * k + s:r0 + FFN_ROWS * k + s + 1, :] = p[0, k:k + 1, s * PACK_W:(s + 1) * PACK_W]
                else:
                    o[r0:r0 + nr, :] = p[0].reshape(nr, PACK_W)

    shape = jax.ShapeDtypeStruct((SMALL_ROWS, PACK_W), F32)
    return pl.pallas_call(body, name="pack_small", out_shape=[shape] * len(dicts), compiler_params=_params())(*operands)


def full_small(g4):
    def body(p, csw, lcw, wa, wx, fcw):
        chips = range(N_CHIPS)
        r0 = SMALL_OFFSET["conv_short_w"]
        csw[...] = jnp.concatenate([p[c, r0:r0 + 3, :] for c in chips], axis=1)
        r0 = SMALL_OFFSET["lru_conv_w"]
        lcw[...] = jnp.concatenate([p[c, r0:r0 + 4, :] for c in chips], axis=1)
        for name, o in (("lru_wa", wa), ("lru_wx", wx)):
            r0 = SMALL_OFFSET[name]
            for h in range(LRU_HEADS):
                for c in chips:
                    o[h, c * QUARTER:(c + 1) * QUARTER, :] = p[c, r0 + h * QUARTER:r0 + (h + 1) * QUARTER, :].astype(BF16)
        r0 = SMALL_OFFSET["ffn_conv_w"]
        for k in range(3):
            fcw[k:k + 1, :] = jnp.concatenate(
                [p[c, r0 + FFN_ROWS * k + s:r0 + FFN_ROWS * k + s + 1, :] for c in chips for s in range(FFN_ROWS)], axis=1)

    mat = jax.ShapeDtypeStruct((LRU_HEADS, HEAD_DIM, HEAD_DIM), BF16)
    csw, lcw, wa, wx, fcw = pl.pallas_call(
        body, name="full_small",
        out_shape=[jax.ShapeDtypeStruct((3, D_MODEL), F32), jax.ShapeDtypeStruct((4, D_MODEL), F32), mat, mat,
                   jax.ShapeDtypeStruct((3, 2 * D_FF), F32)],
        compiler_params=_params())(g4)

    def bias(name):
        rows = g4[:, SMALL_OFFSET[name], :].reshape(N_CHIPS, LRU_HEADS, QUARTER)
        return rows.transpose(1, 0, 2).reshape(1, D_MODEL)

    return dict(conv_short_w=csw, lru_conv_w=lcw, lru_wa=wa, lru_wx=wx, ffn_conv_w=fcw,
                lru_ba=bias("lru_ba"), lru_bx=bias("lru_bx"))


def split_small(full):
    def bias(name):
        return full[name].reshape(LRU_HEADS, N_CHIPS, QUARTER).transpose(1, 0, 2).reshape(N_CHIPS, PACK_W)

    def body(csw, lcw, wa, wx, fcw, ba, bx, o):
        o[...] = jnp.zeros_like(o)
        for c in range(N_CHIPS):
            cols = slice(c * PACK_W, (c + 1) * PACK_W)
            r0 = SMALL_OFFSET["conv_short_w"]
            o[c, r0:r0 + 3, :] = csw[:, cols]
            r0 = SMALL_OFFSET["lru_conv_w"]
            o[c, r0:r0 + 4, :] = lcw[:, cols]
            for name, p in (("lru_wa", wa), ("lru_wx", wx)):
                r0 = SMALL_OFFSET[name]
                for h in range(LRU_HEADS):
                    o[c, r0 + h * QUARTER:r0 + (h + 1) * QUARTER, :] = p[h, c * QUARTER:(c + 1) * QUARTER, :]
            for name, p in (("lru_ba", ba), ("lru_bx", bx)):
                r0 = SMALL_OFFSET[name]
                o[c, r0:r0 + 1, :] = p[c:c + 1, :]
            r0 = SMALL_OFFSET["ffn_conv_w"]
            for k in range(3):
                for s in range(FFN_ROWS):
                    lo = c * FFN_SHARD + s * PACK_W
                    o[c, r0 + FFN_ROWS * k + s:r0 + FFN_ROWS * k + s + 1, :] = fcw[k:k + 1, lo:lo + PACK_W]

    return pl.pallas_call(
        body, name="split_small", out_shape=jax.ShapeDtypeStruct((N_CHIPS, SMALL_ROWS, PACK_W), F32),
        compiler_params=_params(),
    )(full["conv_short_w"], full["lru_conv_w"], full["lru_wa"], full["lru_wx"], full["ffn_conv_w"],
      bias("lru_ba"), bias("lru_bx"))


def pack_repl(dicts, loss=None):
    operands = [d[n] for d in dicts for n in REPL] + ([loss] if loss is not None else [])

    def body(*refs):
        ins, outs = refs[:len(operands)], refs[len(operands):]
        for i, o in enumerate(outs):
            o[...] = jnp.zeros_like(o)
            r0 = 0
            for p in ins[i * len(REPL):(i + 1) * len(REPL)]:
                for s in range(p.shape[1] // D_MODEL):
                    o[r0:r0 + 1, :] = p[:, s * D_MODEL:(s + 1) * D_MODEL]
                    r0 += 1
        if loss is not None:
            outs[-1][LOSS_ROW:LOSS_ROW + 1, :] = jnp.tile(ins[-1][...], (1, D_MODEL // 128))

    shape = jax.ShapeDtypeStruct((REPL_ROWS, D_MODEL), F32)
    return pl.pallas_call(body, name="pack_repl" + ("_loss" if loss is not None else ""),
                          out_shape=[shape] * len(dicts), compiler_params=_params())(*operands)


def _lane_concat(ref, r0, n):
    return jnp.concatenate([ref[r0 + s:r0 + s + 1, :] for s in range(n)], axis=1)


def unpack_small(packs):
    names = [n for n, _, _ in SMALL_PARTS]
    biases = ("lru_ba", "lru_bx")
    per_row = FFN_SHARD // PACK_W

    def body(*refs):
        ins, outs = refs[:len(packs)], refs[len(packs):]
        for i, p in enumerate(ins):
            r0 = 0
            for (name, nr, shape), o in zip(SMALL_PARTS, outs[i * len(names):(i + 1) * len(names)]):
                if name in biases:
                    o[...] = p[r0:r0 + 1, :]
                elif name == "ffn_conv_w":
                    for k in range(shape[1]):
                        o[0, k:k + 1, :] = _lane_concat(p, r0 + per_row * k, per_row)
                else:
                    o[0] = p[r0:r0 + nr, :].reshape(shape[1:])
                r0 += _pad8(nr)

    shapes = [jax.ShapeDtypeStruct((1, PACK_W) if n in biases else s, F32) for n, _, s in SMALL_PARTS]
    res = pl.pallas_call(body, name="unpack_small", out_shape=shapes * len(packs), compiler_params=_params())(*packs)
    out = []
    for i in range(len(packs)):
        d = dict(zip(names, res[i * len(names):(i + 1) * len(names)]))
        for n in biases:
            d[n] = d[n].reshape(1, LRU_HEADS, QUARTER)
        out.append(d)
    return out


def unpack_repl(packs):
    rows = [(2 * D_FF // D_MODEL) if n == "ffn_conv_b" else 1 for n in REPL]

    def body(*refs):
        ins, outs = refs[:len(packs)], refs[len(packs):]
        for i, p in enumerate(ins):
            r0 = 0
            for nr, o in zip(rows, outs[i * len(REPL):(i + 1) * len(REPL)]):
                o[...] = _lane_concat(p, r0, nr)
                r0 += nr

    shapes = [jax.ShapeDtypeStruct((1, nr * D_MODEL), F32) for nr in rows]
    res = pl.pallas_call(body, name="unpack_repl", out_shape=shapes * len(packs), compiler_params=_params())(*packs)
    return [dict(zip(REPL, res[i * len(REPL):(i + 1) * len(REPL)])) for i in range(len(packs))]


def kernel(x, norm_mix_pre, norm_mix_post, norm_ffn_pre, norm_ffn_post, w_in, conv_short_w, w_conv_branch, lru_conv_w, lru_conv_b, lru_wa, lru_ba, lru_wx, lru_bx, lru_lambda, w_lru_branch, w_out, ffn_w_up, ffn_conv_w, ffn_conv_b, ffn_w_down, loss_target, m_norm_mix_pre, m_norm_mix_post, m_norm_ffn_pre, m_norm_ffn_post, m_w_in, m_conv_short_w, m_w_conv_branch, m_lru_conv_w, m_lru_conv_b, m_lru_wa, m_lru_ba, m_lru_wx, m_lru_bx, m_lru_lambda, m_w_lru_branch, m_w_out, m_ffn_w_up, m_ffn_conv_w, m_ffn_conv_b, m_ffn_w_down, v_norm_mix_pre, v_norm_mix_post, v_norm_ffn_pre, v_norm_ffn_post, v_w_in, v_conv_short_w, v_w_conv_branch, v_lru_conv_w, v_lru_conv_b, v_lru_wa, v_lru_ba, v_lru_wx, v_lru_bx, v_lru_lambda, v_w_lru_branch, v_w_out, v_ffn_w_up, v_ffn_conv_w, v_ffn_conv_b, v_ffn_w_down):
    given = dict(locals())
    w = {n: given[n] for n in WEIGHTS}
    m = {n: given["m_" + n] for n in WEIGHTS}
    v = {n: given["v_" + n] for n in WEIGHTS}

    xi, yi, ci = _position()
    chip_i = 2 * xi + yi
    chip = chip_i.astype(jnp.int32).reshape(1)
    core = ci.astype(jnp.int32).reshape(1)
    xs, target = x[0], loss_target[0]
    g1, g2, g3, g4 = w["norm_mix_pre"], w["norm_mix_post"], w["norm_ffn_pre"], w["norm_ffn_post"]
    shard = {n: w[n][0].astype(BF16) for n in BIG}
    small_shard, m_small, v_small = pack_small([w, m, v])

    def gathered(bufs, names):
        return [_own_slot(b, small_shard if n == "small" else shard[n], chip_i) for b, n in zip(bufs, names)]

    def chip_sums(arrays, kinds, tag):
        theirs = pair_split(arrays, kinds, "pair_split_" + tag)
        return [add_pair(g, k, t, core, "pair_add_%s_%d" % (tag, i)) for i, (g, k, t) in enumerate(zip(arrays, kinds, theirs))]

    h1, h1t = norm_in(xs, g1)
    win4, small4 = gathered(run_ride(gather_ride([shard["w_in"], small_shard]), "gather_first"), ("w_in", "small"))
    small = full_small(small4)
    (proj,), got = matmul_cols(h1, win4, "proj_fwd",
                               ride=gather_ride([shard["w_conv_branch"], shard["w_lru_branch"], shard["w_out"]]))
    wcb, wlb, wout = [g.reshape(-1, D_MODEL) for g in gathered(got, ("w_conv_branch", "w_lru_branch", "w_out"))]
    up_piece = lambda r0, nr, into=None: gather_ride([shard["ffn_w_up"]], items=[(0, r0, nr)], into=into)
    down_piece = lambda r0, nr, into=None: gather_ride([shard["ffn_w_down"]], items=[(0, r0, nr)], into=into)
    (q, ya), got = mix_conv_fwd(proj, small["conv_short_w"], ride=up_piece(0, 128))
    (xl, r, gi, h, yb), got = mix_lru_fwd(
        proj, small["lru_conv_w"], w["lru_conv_b"], small["lru_wa"], small["lru_ba"],
        small["lru_wx"], small["lru_bx"], w["lru_lambda"], ride=up_piece(128, 512, got))
    (a, b, merged), got = branch_merge_fwd(ya, yb, wcb, wlb, proj, ride=up_piece(640, 384, got))
    (wup4,) = gathered(got, ("ffn_w_up",))
    (mix, x2, h2, h2t), got = mix_out_fwd(merged, wout, xs, g2, g3, ride=down_piece(0, 256))
    (up, act, f), got = ffn_up_act_fwd(h2, wup4, small["ffn_conv_w"], w["ffn_conv_b"], ride=down_piece(256, 512, got))
    wdown = gathered(got, ("ffn_w_down",))[0].reshape(-1, D_MODEL)
    dy, dout, loss, dg4 = ffn_down_loss(f, wdown, x2, target, g4)

    dh2, dwup, dwdown, dfw, dfb = ffn_up_bwd(dout, wdown, up, act, f, small["ffn_conv_w"], wup4, h2t)
    cs_down, cs_up = chip_sums([dwdown, dwup], ["row", "col2"], "ffn")
    down_rows = lambda r0, nr, into=None: exchange_ride([cs_down], items=[(0, r0, nr)], into=into)
    up_rows = lambda r0, nr, into=None: exchange_ride([cs_up], items=[(0, r0, nr)], into=into)
    (dx2, dmix, dg3, dg2), rx_down = norms_mid_bwd(dh2, x2, dy, mix, g3, g2, ride=down_rows(0, 128))
    (da, db, dwout, dgates), rx_down = mix_out_bwd(dmix, wout, merged, a, b, proj, ride=down_rows(128, 256, rx_down))
    (dconv, dwcb, dws), rx_up = mix_conv_bwd(da, wcb, proj, q, small["conv_short_w"], ride=up_rows(0, 176))
    cs_mid = chip_sums([dwout, dwcb], ["row", "row"], "mid")
    (dlru, dwlb, dwa, dwx, dba, dbx, dwl, dbl, dlam), rx_up = mix_lru_bwd(
        db, wlb, proj, xl, r, gi, h, small["lru_conv_w"], small["lru_wa"], small["lru_wx"],
        w["lru_lambda"], ride=up_rows(176, 336, rx_up))
    grads = dict(norm_mix_post=dg2, norm_ffn_pre=dg3, norm_ffn_post=dg4, conv_short_w=dws, lru_conv_w=dwl,
                 lru_conv_b=dbl, lru_wa=dwa, lru_ba=dba, lru_wx=dwx, lru_bx=dbx, lru_lambda=dlam,
                 ffn_conv_w=jnp.concatenate([dfw[0], dfw[1]], axis=1), ffn_conv_b=jnp.concatenate([dfb[0], dfb[1]], axis=1))
    cs_late = chip_sums([dwlb, split_small(grads)], ["row", "lead"], "late")
    dproj = [dconv, dlru, dgates]
    (dwin,), rx_all = matmul_cols_bwd(dproj, h1t, "proj_wgrad", True, ride=exchange_ride(cs_mid + cs_late))
    rx_mid, rx_late = rx_all[:2], rx_all[2:]
    cs_in = chip_sums([dwin], ["col"], "in")
    in_rows = lambda r0, nr, into=None: exchange_ride(cs_in, items=[(0, r0, nr)], into=into)
    (dh1,), rx_in = matmul_cols_bwd(dproj, win4, "proj_dgrad", False, ride=in_rows(0, 384))
    (dx, grads["norm_mix_pre"]), rx_in = norm_in_bwd(dh1, xs, dx2, g1, ride=in_rows(384, 128, rx_in))
    rx_in = rx_in[0]
    (rep_part,) = pack_repl([grads], loss)
    (rep_all,) = run_ride(exchange_ride([], rep=rep_part), "exchange_repl")

    order = (("w_in", cs_in[0], rx_in), ("ffn_w_up", cs_up, rx_up[0]), ("w_conv_branch", cs_mid[1], rx_mid[1]),
             ("w_lru_branch", cs_late[0], rx_late[0]), ("w_out", cs_mid[0], rx_mid[0]),
             ("ffn_w_down", cs_down, rx_down[0]), ("small", cs_late[1], rx_late[1]))
    halves = [sum_chips(rx, cs, chip, "chip_sum_" + n) for n, cs, rx in order]
    me = 4 * xi + 2 * yi + ci
    rep_grad = sum_lead(_own_slot(rep_all, rep_part, me), "device_sum")
    others = pair_swap(halves)

    g_out, d_out, m_out, v_out = {}, {}, {}, {}
    for n, gm, go in zip(BIG, halves[:-1], others[:-1]):
        g, d, nm, nv = adamw_halves(w[n][0], gm, go, m[n][0], v[n][0], core, "adamw_" + n)
        g_out[n], d_out[n], m_out[n], v_out[n] = g[None], d[None], nm[None], nv[None]
    bufs = adamw_halves(small_shard, halves[-1], others[-1], m_small, v_small, core, "adamw_small")
    for dst, part in zip((g_out, d_out, m_out, v_out), unpack_small(bufs)):
        dst.update(part)
    w_rep, m_rep, v_rep = pack_repl([w, m, v])
    d, nm, nv = adamw(w_rep, rep_grad, m_rep, v_rep, "adamw_repl")
    for dst, part in zip((g_out, d_out, m_out, v_out), unpack_repl([rep_grad, d, nm, nv])):
        dst.update(part)

    return (rep_grad[LOSS_ROW, 0], dx[None], *[g_out[n] for n in WEIGHTS], *[d_out[n] for n in WEIGHTS],
            *[m_out[n] for n in WEIGHTS], *[v_out[n] for n in WEIGHTS])
```

```python
import functools
import math

import jax
import jax.numpy as jnp
from jax import lax
from jax.experimental import pallas as pl
from jax.experimental.pallas import tpu as pltpu

F32 = jnp.float32
BF16 = jnp.bfloat16

D_MODEL = 1024
N_CHIPS = 4
N_SEG = 7
D_FF = 3 * D_MODEL
LRU_HEADS = 4
HEAD_DIM = D_MODEL // LRU_HEADS
LRU_C = 8.0
RMS_EPS = 1e-6
CW = 256
FW = 256
SUBLANES = 8
VMEM_LIMIT = 58 * 1024 * 1024

ADAM_LR = 0.001
ADAM_B1 = 0.9
ADAM_B2 = 0.999
ADAM_EPS = 1e-08
ADAM_WD = 0.01
ADAM_STEP = 10

_GELU_C = math.sqrt(2.0 / math.pi)
_GELU_K = 0.044715


def _params(**kw):
    return pltpu.CompilerParams(vmem_limit_bytes=VMEM_LIMIT, **kw)


def _sigmoid(x):
    return 1.0 / (1.0 + jnp.exp(-x))


def _gelu(x):
    t = jnp.tanh(_GELU_C * (x + _GELU_K * x * x * x))
    return 0.5 * x * (1.0 + t)


def _gelu_and_grad(x):
    x2 = x * x
    t = jnp.tanh(_GELU_C * (x + _GELU_K * x * x2))
    g = 0.5 * x * (1.0 + t)
    dg = 0.5 * (1.0 + t) + 0.5 * x * (1.0 - t * t) * _GELU_C * (1.0 + 3.0 * _GELU_K * x2)
    return g, dg


def _log_sigmoid(x):
    e = jnp.exp(-jnp.abs(x))
    u = 1.0 + e
    l1p = jnp.where(u == 1.0, e, jnp.log(u) * e / (u - 1.0))
    return jnp.minimum(x, 0.0) - l1p


def _neg_expm1(z):
    series = -z * (1.0 + z * (0.5 + z * (1.0 / 6.0 + z * (1.0 / 24.0 + z * (1.0 / 120.0 + z * (1.0 / 720.0))))))
    return jnp.where(z > -0.2, series, 1.0 - jnp.exp(z))


def _rows(shape):
    return lax.broadcasted_iota(jnp.int32, shape, 0)


def _shift_down(x, k):
    return jnp.where(_rows(x.shape) >= k, pltpu.roll(x, k, 0), 0.0)


def _shift_up(x, k):
    n = x.shape[0]
    return jnp.where(_rows(x.shape) < n - k, pltpu.roll(x, n - k, 0), 0.0)


def _delays(x, k_width):
    return [x] + [_shift_down(x, j) for j in range(1, k_width)]


def _advances(dy, k_width):
    return [dy] + [_shift_up(dy, j) for j in range(1, k_width)]


def _taps_sum(shifted, w_ref, b=None):
    k_width = w_ref.shape[0]
    y = w_ref[k_width - 1:k_width, :] * shifted[0]
    for j in range(1, k_width):
        y = y + w_ref[k_width - 1 - j:k_width - j, :] * shifted[j]
    if b is not None:
        y = y + b
    return y


def _causal_conv(x, w_ref, b=None):
    return _taps_sum(_delays(x, w_ref.shape[0]), w_ref, b)


def _conv_wgrad(advanced, x):
    k_width = len(advanced)
    rows = [jnp.sum(advanced[k_width - 1 - k] * x, axis=0, keepdims=True) for k in range(k_width)]
    return jnp.concatenate(rows, axis=0)


def _dot(a, b):
    return jnp.dot(a, b, preferred_element_type=F32)


def _dot_nt(a, b):
    return lax.dot_general(a, b, (((1,), (1,)), ((), ())), preferred_element_type=F32)


def _dot_tn(a, b):
    return lax.dot_general(a, b, (((0,), (0,)), ((), ())), preferred_element_type=F32)


def _rms_stats(x):
    r = lax.rsqrt(jnp.mean(x * x, axis=-1, keepdims=True) + RMS_EPS)
    return x * r, r


def _rms_bwd(n, r, g, dy):
    dn = dy * g
    dx = r * (dn - n * jnp.mean(dn * n, axis=-1, keepdims=True))
    return dx, dy * n


def _scan_forward(a_ref, b_ref, h_ref):
    n, c = a_ref.shape
    row = lax.broadcasted_iota(jnp.int32, (SUBLANES, c), 0)

    def group(g, carry):
        r0 = pl.multiple_of(g * SUBLANES, SUBLANES)
        a = a_ref[pl.ds(r0, SUBLANES), :]
        b = b_ref[pl.ds(r0, SUBLANES), :]
        for k in (1, 2, 4):
            ap = jnp.where(row >= k, pltpu.roll(a, k, 0), 1.0)
            bp = jnp.where(row >= k, pltpu.roll(b, k, 0), 0.0)
            b = a * bp + b
            a = a * ap
        h = a * carry + b
        h_ref[pl.ds(r0, SUBLANES), :] = h
        return h[SUBLANES - 1:SUBLANES, :]

    lax.fori_loop(0, n // SUBLANES, group, jnp.zeros((1, c), F32))


def _scan_backward(c_ref, b_ref, g_ref):
    n, ch = c_ref.shape
    row = lax.broadcasted_iota(jnp.int32, (SUBLANES, ch), 0)
    n_groups = n // SUBLANES

    def group(i, carry):
        r0 = pl.multiple_of((n_groups - 1 - i) * SUBLANES, SUBLANES)
        a = c_ref[pl.ds(r0, SUBLANES), :]
        b = b_ref[pl.ds(r0, SUBLANES), :]
        for k in (1, 2, 4):
            keep = row < SUBLANES - k
            ap = jnp.where(keep, pltpu.roll(a, SUBLANES - k, 0), 1.0)
            bp = jnp.where(keep, pltpu.roll(b, SUBLANES - k, 0), 0.0)
            b = a * bp + b
            a = a * ap
        g = a * carry + b
        g_ref[pl.ds(r0, SUBLANES), :] = g
        return g[0:1, :]

    lax.fori_loop(0, n_groups, group, jnp.zeros((1, ch), F32))


MESH = pl.DeviceIdType.MESH
_HBM = pl.BlockSpec(memory_space=pltpu.HBM)
_OTHER_CHIPS = ((1, 0), (0, 1), (1, 1))
_OTHER_DEVICES = tuple((dx, dy, dc) for dx in (0, 1) for dy in (0, 1) for dc in (0, 1) if dx or dy or dc)
N_DEVICES = 8


def _position():
    return lax.axis_index("x"), lax.axis_index("y"), lax.axis_index("c")


def _flip(v, d):
    return 1 - v if d else v


def _chip(x, y, p):
    px, py = _flip(x, _OTHER_CHIPS[p][0]), _flip(y, _OTHER_CHIPS[p][1])
    return px, py, 2 * px + py


class _Ride:
    def __init__(self, srcs, bufs, scratch, plan, collective_id):
        self.srcs, self.bufs, self.scratch, self.plan = list(srcs), list(bufs), list(scratch), plan
        self.collective_id = collective_id


NEIGHBOURS_AND_SIBLING = 1
OTHER_CHIPS_SAME_CORE = 2
ALL_DEVICES = 3
SIBLING = 4


def _handshake(peers):
    barrier = pltpu.get_barrier_semaphore()
    for peer in peers:
        pl.semaphore_signal(barrier, inc=1, device_id=peer, device_id_type=MESH)
    pl.semaphore_wait(barrier, len(peers))


def _call(body, *, name, grid, in_specs, out_specs, out_shape, operands, scratch_shapes=(), ride=None):
    in_specs, out_specs, out_shape = list(in_specs), list(out_specs), list(out_shape)
    scratch_shapes = list(scratch_shapes)
    if ride is None:
        return pl.pallas_call(body, name=name, grid=grid, in_specs=in_specs, out_specs=out_specs, out_shape=out_shape,
                              scratch_shapes=scratch_shapes, compiler_params=_params())(*operands)
    n_in, n_out, n_scr = len(in_specs), len(out_shape), len(scratch_shapes)
    old = [i for i, b in enumerate(ride.bufs) if not isinstance(b, jax.ShapeDtypeStruct)]
    n_src, n_old, n_buf = len(ride.srcs), len(old), len(ride.bufs)

    def full_body(*refs):
        o0 = n_in + n_src + n_old
        s0 = o0 + n_out + n_buf
        start, relay, relay_on, finish = ride.plan(refs[n_in:n_in + n_src], refs[o0 + n_out:s0], refs[s0 + n_scr:])
        ids = [pl.program_id(i) for i in range(len(grid))]
        first = functools.reduce(jnp.logical_and, [i == 0 for i in ids])
        middle = functools.reduce(jnp.logical_and, [ids[0] == grid[0] // 2] + [i == 0 for i in ids[1:]])
        last = functools.reduce(jnp.logical_and, [i == g - 1 for i, g in zip(ids, grid)])
        pl.when(first)(start)
        pl.when(middle)(relay)
        pl.when(last)(relay_on)
        body(*refs[:n_in], *refs[o0:o0 + n_out], *refs[s0:s0 + n_scr])
        pl.when(last)(finish)

    shapes = [jax.ShapeDtypeStruct(b.shape, b.dtype) for b in ride.bufs]
    res = pl.pallas_call(
        full_body, name=name, grid=grid,
        in_specs=in_specs + [_HBM] * (n_src + n_old), out_specs=out_specs + [_HBM] * n_buf,
        out_shape=out_shape + shapes, scratch_shapes=scratch_shapes + ride.scratch,
        input_output_aliases={n_in + n_src + k: n_out + i for k, i in enumerate(old)},
        compiler_params=_params(collective_id=ride.collective_id),
    )(*operands, *ride.srcs, *[ride.bufs[i] for i in old])
    return list(res[:n_out]), list(res[n_out:])


def run_ride(ride, name):
    def body():
        pass

    return _call(body, name=name, grid=(1,), in_specs=[], out_specs=[], out_shape=[], operands=[], ride=ride)[1]


def gather_ride(shards, items=None, into=None):
    items = items or [(a, 0, s.shape[0]) for a, s in enumerate(shards)]
    bufs = into or [jax.ShapeDtypeStruct((N_CHIPS,) + s.shape, s.dtype) for s in shards]
    nrel = len(_OTHER_CHIPS)

    def plan(srcs, dsts, sems):
        ici_send, ici_recv, hop_send, hop_recv, sib_send, sib_recv = sems
        x, y, c = _position()
        j = 2 * x + y

        def rows(ref, it, h, q=None):
            half = it[2] // 2
            if q is None:
                return ref.at[pl.ds(it[1] + h * half, half), :]
            return ref.at[pl.ds(it[1] + h * half + q * (half // 2), half // 2), :]

        def ici(i, p, slot):
            it = items[i]
            px, py, _ = _chip(x, y, p)
            return pltpu.make_async_remote_copy(
                src_ref=rows(srcs[it[0]], it, c), dst_ref=rows(dsts[it[0]].at[slot], it, c),
                send_sem=ici_send.at[i * nrel + p], recv_sem=ici_recv.at[i * nrel + p],
                device_id=(px, py, c), device_id_type=MESH)

        def hop(i, p, slot):
            it = items[i]
            part = rows(dsts[it[0]].at[slot], it, c, p)
            px, py, _ = _chip(x, y, 1 - p)
            return pltpu.make_async_remote_copy(
                src_ref=part, dst_ref=part, send_sem=hop_send.at[i * 2 + p], recv_sem=hop_recv.at[i * 2 + p],
                device_id=(px, py, c), device_id_type=MESH)

        def sib(i, p, h):
            it = items[i]
            part = rows(dsts[it[0]].at[_chip(x, y, p)[2]], it, h)
            return pltpu.make_async_remote_copy(
                src_ref=part, dst_ref=part, send_sem=sib_send.at[i * nrel + p], recv_sem=sib_recv.at[i * nrel + p],
                device_id=(x, y, 1 - c), device_id_type=MESH)

        every = range(len(items))
        diag = _chip(x, y, 2)[2]

        def start():
            _handshake([_chip(x, y, 0)[:2] + (c,), _chip(x, y, 1)[:2] + (c,), (x, y, 1 - c)])
            for i in every:
                for p in (0, 1):
                    ici(i, p, j).start()

        def relay():
            for i in every:
                for p in (0, 1):
                    k = _chip(x, y, p)[2]
                    ici(i, p, k).wait_recv()
                    hop(i, p, k).start()
                    sib(i, p, c).start()

        def relay_on():
            for i in every:
                for p in (0, 1):
                    hop(i, p, diag).wait_recv()
                sib(i, 2, c).start()

        def finish():
            for i in every:
                for p in range(nrel):
                    sib(i, p, 1 - c).wait_recv()
            for i in every:
                for p in (0, 1):
                    ici(i, p, j).wait_send()
                    hop(i, p, _chip(x, y, p)[2]).wait_send()
                for p in range(nrel):
                    sib(i, p, c).wait_send()

        return start, relay, relay_on, finish

    n = len(items)
    sems = [pltpu.SemaphoreType.DMA((n * nrel,))] * 2 + [pltpu.SemaphoreType.DMA((n * 2,))] * 2 \
        + [pltpu.SemaphoreType.DMA((n * nrel,))] * 2
    return _Ride(shards, bufs, sems, plan, NEIGHBOURS_AND_SIBLING)


def exchange_ride(sums, items=None, into=None, rep=None):
    items = [(a, 0, s.shape[1]) for a, s in enumerate(sums)] if items is None else items
    into = into or [None] * len(sums)
    bufs = [jax.ShapeDtypeStruct(s.shape, s.dtype) if b is None else b for s, b in zip(sums, into)]
    srcs = list(sums)
    scratch = [pltpu.SemaphoreType.DMA((max(len(items), 1) * len(_OTHER_CHIPS),))] * 2
    if rep is not None:
        srcs.append(rep)
        bufs.append(jax.ShapeDtypeStruct((N_DEVICES,) + rep.shape, rep.dtype))
        scratch += [pltpu.SemaphoreType.DMA((len(_OTHER_DEVICES),))] * 2
    nrel = len(_OTHER_CHIPS)

    def plan(src_refs, dst_refs, sems):
        x, y, c = _position()
        j = 2 * x + y
        me = 4 * x + 2 * y + c

        def part(i, p, src_slot, dst_slot):
            a, r0, nr = items[i]
            px, py, _ = _chip(x, y, p)
            return pltpu.make_async_remote_copy(
                src_ref=src_refs[a].at[src_slot, pl.ds(r0, nr), :], dst_ref=dst_refs[a].at[dst_slot, pl.ds(r0, nr), :],
                send_sem=sems[0].at[i * nrel + p], recv_sem=sems[1].at[i * nrel + p],
                device_id=(px, py, c), device_id_type=MESH)

        def device(q):
            dx, dy, dc = _OTHER_DEVICES[q]
            return _flip(x, dx), _flip(y, dy), _flip(c, dc)

        def rep_copy(q, slot):
            return pltpu.make_async_remote_copy(
                src_ref=src_refs[-1], dst_ref=dst_refs[-1].at[slot], send_sem=sems[2].at[q], recv_sem=sems[3].at[q],
                device_id=device(q), device_id_type=MESH)

        pairs = [(i, p) for i in range(len(items)) for p in range(nrel)]
        others = range(len(_OTHER_DEVICES)) if rep is not None else ()

        def start():
            if rep is None:
                _handshake([_chip(x, y, p)[:2] + (c,) for p in range(nrel)])
            else:
                _handshake([device(q) for q in others])
            for i, p in pairs:
                part(i, p, _chip(x, y, p)[2], j).start()
            for q in others:
                rep_copy(q, me).start()

        def finish():
            for i, p in pairs:
                k = _chip(x, y, p)[2]
                part(i, p, k, k).wait_recv()
            for q in others:
                px, py, pc = device(q)
                rep_copy(q, 4 * px + 2 * py + pc).wait_recv()
            for i, p in pairs:
                part(i, p, _chip(x, y, p)[2], j).wait_send()
            for q in others:
                rep_copy(q, me).wait_send()

        return start, lambda: None, lambda: None, finish

    return _Ride(srcs, bufs, scratch, plan, OTHER_CHIPS_SAME_CORE if rep is None else ALL_DEVICES)


def _own_slot(buf, own, index):
    return lax.dynamic_update_slice(buf, own[None], (index,) + (0,) * own.ndim)


def _token_tile(s):
    return min(s, 512)


def norm_in(x, g):
    s, d = x.shape
    t = _token_tile(s)

    def body(x_ref, g_ref, o_ref, ot_ref):
        n, _ = _rms_stats(x_ref[...])
        h = n * g_ref[...]
        o_ref[...] = h.astype(BF16)
        ot_ref[...] = h.T.astype(BF16)

    return pl.pallas_call(
        body, name="norm_in", grid=(s // t,),
        in_specs=[pl.BlockSpec((t, d), lambda i: (i, 0)), pl.BlockSpec((1, d), lambda i: (0, 0))],
        out_specs=[pl.BlockSpec((t, d), lambda i: (i, 0)), pl.BlockSpec((d, t), lambda i: (0, i))],
        out_shape=[jax.ShapeDtypeStruct((s, d), BF16), jax.ShapeDtypeStruct((d, s), BF16)],
        compiler_params=_params(),
    )(x, g)


def matmul_cols(a, w4, name, ride=None):
    m, k = a.shape
    nj, _, ns = w4.shape
    nb = ns // CW

    def body(a_ref, w_ref, o_ref):
        o_ref[...] = _dot(a_ref[...], w_ref[0])

    return _call(
        body, name=name, grid=(nj, nb),
        in_specs=[pl.BlockSpec((m, k), lambda j, b: (0, 0)),
                  pl.BlockSpec((1, k, CW), lambda j, b: (j, 0, b))],
        out_specs=[pl.BlockSpec((m, CW), lambda j, b: (0, j * nb + b))],
        out_shape=[jax.ShapeDtypeStruct((m, nj * ns), F32)],
        operands=(a, w4), ride=ride)


def mix_conv_fwd(proj, ws, ride=None):
    s = proj.shape[0]
    nblk = D_MODEL // CW

    def body(cb_ref, cc_ref, cx_ref, ws_ref, q_ref, ya_ref):
        q = _causal_conv(cc_ref[...] * cx_ref[...], ws_ref)
        q_ref[...] = q
        ya_ref[...] = (cb_ref[...] * q).astype(BF16)

    seg = lambda k: pl.BlockSpec((s, CW), lambda c, k=k: (0, k * nblk + c))
    return _call(
        body, name="mix_conv_fwd", grid=(nblk,),
        in_specs=[seg(0), seg(1), seg(2), pl.BlockSpec((3, CW), lambda c: (0, c))],
        out_specs=[pl.BlockSpec((s, CW), lambda c: (0, c))] * 2,
        out_shape=[jax.ShapeDtypeStruct((s, D_MODEL), F32), jax.ShapeDtypeStruct((s, D_MODEL), BF16)],
        operands=(proj, proj, proj, ws), ride=ride)


def _lru_gates(r, ls):
    log_a = LRU_C * r * ls
    a = jnp.exp(log_a)
    mult = jnp.sqrt(_neg_expm1(2.0 * log_a))
    mult = jnp.where(_rows(r.shape) == 0, 1.0, mult)
    return a, mult


def mix_lru_fwd(proj, wl, bl, wa, ba, wx, bx, lam, ride=None):
    s = proj.shape[0]
    nblk = D_MODEL // CW

    def body(lx_ref, ly_ref, wl_ref, bl_ref, wa_ref, ba_ref, wx_ref, bx_ref, lam_ref,
             xl_ref, r_ref, i_ref, h_ref, yb_ref, a_scr, u_scr):
        xl = _causal_conv(lx_ref[...], wl_ref, bl_ref[...])
        xlb = xl.astype(BF16)
        xl_ref[...] = xlb
        r = _sigmoid(_dot(xlb, wa_ref[0]) + ba_ref[...])
        i = _sigmoid(_dot(xlb, wx_ref[0]) + bx_ref[...])
        r_ref[...] = r.astype(BF16)
        i_ref[...] = i.astype(BF16)
        a, mult = _lru_gates(r, _log_sigmoid(lam_ref[...]))
        a_scr[...] = a
        u_scr[...] = mult * i * xl
        _scan_forward(a_scr, u_scr, h_ref)
        yb_ref[...] = (h_ref[...] * _gelu(ly_ref[...])).astype(BF16)

    blk = lambda k: pl.BlockSpec((s, CW), lambda c, k=k: (0, k * nblk + c))
    vec = pl.BlockSpec((1, CW), lambda c: (0, c))
    mat = pl.BlockSpec((1, CW, CW), lambda c: (c, 0, 0))
    out = pl.BlockSpec((s, CW), lambda c: (0, c))
    f = jax.ShapeDtypeStruct((s, D_MODEL), F32)
    hb = jax.ShapeDtypeStruct((s, D_MODEL), BF16)
    return _call(
        body, name="mix_lru_fwd", grid=(nblk,),
        in_specs=[blk(3), blk(4), pl.BlockSpec((4, CW), lambda c: (0, c)), vec, mat, vec, mat, vec, vec],
        out_specs=[out] * 5,
        out_shape=[hb, hb, hb, f, hb],
        scratch_shapes=[pltpu.VMEM((s, CW), F32), pltpu.VMEM((s, CW), F32)],
        operands=(proj, proj, wl, bl, wa, ba, wx, bx, lam), ride=ride)


def branch_merge_fwd(ya, yb, wcb, wlb, proj, ride=None):
    s = ya.shape[0]
    nblk = D_MODEL // CW

    def body(ya_ref, yb_ref, wcb_ref, wlb_ref, gc_ref, gl_ref, a_ref, b_ref, m_ref):
        a = _dot(ya_ref[...], wcb_ref[...])
        b = _dot(yb_ref[...], wlb_ref[...])
        a_ref[...] = a
        b_ref[...] = b
        m_ref[...] = (_sigmoid(gc_ref[...]) * a + _sigmoid(gl_ref[...]) * b).astype(BF16)

    res = pl.BlockSpec((s, D_MODEL), lambda n: (0, 0))
    wcol = pl.BlockSpec((D_MODEL, CW), lambda n: (0, n))
    blk = lambda k: pl.BlockSpec((s, CW), lambda n, k=k: (0, k * nblk + n))
    out = pl.BlockSpec((s, CW), lambda n: (0, n))
    f = jax.ShapeDtypeStruct((s, D_MODEL), F32)
    return _call(
        body, name="branch_merge_fwd", grid=(nblk,),
        in_specs=[res, res, wcol, wcol, blk(5), blk(6)],
        out_specs=[out] * 3,
        out_shape=[f, f, jax.ShapeDtypeStruct((s, D_MODEL), BF16)],
        operands=(ya, yb, wcb, wlb, proj, proj), ride=ride)


def mix_out_fwd(merged, wout, x, g2, g3, ride=None):
    s, d = x.shape
    t = _token_tile(s)

    def body(m_ref, w_ref, x_ref, g2_ref, g3_ref, mix_ref, x2_ref, h2_ref, h2t_ref):
        mix = _dot(m_ref[...], w_ref[...])
        mix_ref[...] = mix
        n, _ = _rms_stats(mix)
        x2 = x_ref[...] + n * g2_ref[...]
        x2_ref[...] = x2
        n2, _ = _rms_stats(x2)
        h2 = n2 * g3_ref[...]
        h2_ref[...] = h2.astype(BF16)
        h2t_ref[...] = h2.T.astype(BF16)

    tile = pl.BlockSpec((t, d), lambda i: (i, 0))
    vec = pl.BlockSpec((1, d), lambda i: (0, 0))
    f = jax.ShapeDtypeStruct((s, d), F32)
    return _call(
        body, name="mix_out_fwd", grid=(s // t,),
        in_specs=[tile, pl.BlockSpec((d, d), lambda i: (0, 0)), tile, vec, vec],
        out_specs=[tile] * 3 + [pl.BlockSpec((d, t), lambda i: (0, i))],
        out_shape=[f, f, jax.ShapeDtypeStruct((s, d), BF16), jax.ShapeDtypeStruct((d, s), BF16)],
        operands=(merged, wout, x, g2, g3), ride=ride)


def ffn_up_act_fwd(h2, wup4, fw, fb, ride=None):
    s, k = h2.shape
    ns = wup4.shape[2]
    per_chip = ns // CW
    nblk = D_FF // CW

    def body(h_ref, wg_ref, wv_ref, cg_ref, cv_ref, bg_ref, bv_ref, up_ref, act_ref, f_ref):
        h = h_ref[...]
        ug = _dot(h, wg_ref[0])
        uv = _dot(h, wv_ref[0])
        up_ref[0] = ug
        up_ref[1] = uv
        gate = _causal_conv(ug, cg_ref, bg_ref[...])
        val = _causal_conv(uv, cv_ref, bv_ref[...])
        act_ref[0] = gate.astype(BF16)
        act_ref[1] = val.astype(BF16)
        f_ref[...] = (_gelu(gate) * val).astype(BF16)

    wcols = lambda h: pl.BlockSpec((1, k, CW), lambda n, h=h: (n // per_chip + 2 * h, 0, n % per_chip))
    half = lambda h, rows: pl.BlockSpec((rows, CW), lambda n, h=h: (0, h * nblk + n))
    both = pl.BlockSpec((2, s, CW), lambda n: (0, 0, n))
    return _call(
        body, name="ffn_up_act_fwd", grid=(nblk,),
        in_specs=[pl.BlockSpec((s, k), lambda n: (0, 0)), wcols(0), wcols(1),
                  half(0, 3), half(1, 3), half(0, 1), half(1, 1)],
        out_specs=[both, both, pl.BlockSpec((s, CW), lambda n: (0, n))],
        out_shape=[jax.ShapeDtypeStruct((2, s, D_FF), F32), jax.ShapeDtypeStruct((2, s, D_FF), BF16),
                   jax.ShapeDtypeStruct((s, D_FF), BF16)],
        operands=(h2, wup4, wup4, fw, fw, fb, fb), ride=ride)


def ffn_down_loss(f, wdown, x2, target, g4):
    s, d = x2.shape
    t = _token_tile(s)

    def body(f_ref, w_ref, x2_ref, tg_ref, g4_ref, dy_ref, dout_ref, loss_ref, dg4_ref):
        @pl.when(pl.program_id(0) == 0)
        def _():
            loss_ref[...] = jnp.zeros_like(loss_ref)
            dg4_ref[...] = jnp.zeros_like(dg4_ref)

        out = _dot(f_ref[...], w_ref[...])
        n, r = _rms_stats(out)
        err = x2_ref[...] + n * g4_ref[...] - tg_ref[...]
        loss_ref[...] += jnp.full(loss_ref.shape, (0.5 / d) * jnp.sum(err * err), F32)
        dy = err * (1.0 / d)
        dy_ref[...] = dy
        dout, dg = _rms_bwd(n, r, g4_ref[...], dy)
        dout_ref[...] = dout.astype(BF16)
        dg4_ref[...] += jnp.sum(dg, axis=0, keepdims=True)

    tile = pl.BlockSpec((t, d), lambda i: (i, 0))
    vec = pl.BlockSpec((1, d), lambda i: (0, 0))
    return pl.pallas_call(
        body, name="ffn_down_loss", grid=(s // t,),
        in_specs=[pl.BlockSpec((t, D_FF), lambda i: (i, 0)), pl.BlockSpec((D_FF, d), lambda i: (0, 0)), tile, tile, vec],
        out_specs=[tile, tile, pl.BlockSpec((1, 128), lambda i: (0, 0)), vec],
        out_shape=[jax.ShapeDtypeStruct((s, d), F32), jax.ShapeDtypeStruct((s, d), BF16),
                   jax.ShapeDtypeStruct((1, 128), F32), jax.ShapeDtypeStruct((1, d), F32)],
        compiler_params=_params(),
    )(f, wdown, x2, target, g4)


def ffn_up_bwd(dout, wdown, up, act, f, fw, wup4, h2t, ride=None):
    k, s = h2t.shape
    nblk = D_FF // FW
    per_chip = wup4.shape[2] // FW

    def body(do_ref, wd_ref, up_ref, act_ref, f_ref, cg_ref, cv_ref, wg_ref, wv_ref, h_ref,
             dh_ref, dwu_ref, dwd_ref, dw_ref, db_ref, dup_scr):
        @pl.when(pl.program_id(0) == 0)
        def _():
            dup_scr[...] = jnp.zeros_like(dup_scr)
            dh_ref[...] = jnp.zeros_like(dh_ref)

        do = do_ref[...]
        df = _dot_nt(do, wd_ref[...])
        dg = dup_scr[0]
        dv = dup_scr[1]
        ht = h_ref[...]
        dh_ref[...] += _dot_nt(dg, wg_ref[0]) + _dot_nt(dv, wv_ref[0])
        dwu_ref[0] = _dot(ht, dg).astype(BF16)
        dwu_ref[1] = _dot(ht, dv).astype(BF16)
        dwd_ref[...] = _dot_tn(f_ref[...], do).astype(BF16)
        val = act_ref[1].astype(F32)
        ge, dge = _gelu_and_grad(act_ref[0].astype(F32))
        dgate = _advances(df * val * dge, 3)
        dval = _advances(df * ge, 3)
        dw_ref[0] = _conv_wgrad(dgate, up_ref[0])
        dw_ref[1] = _conv_wgrad(dval, up_ref[1])
        db_ref[0] = jnp.sum(dgate[0], axis=0, keepdims=True)
        db_ref[1] = jnp.sum(dval[0], axis=0, keepdims=True)
        dup_scr[0] = _taps_sum(dgate, cg_ref).astype(BF16)
        dup_scr[1] = _taps_sum(dval, cv_ref).astype(BF16)

    cur = lambda n: jnp.minimum(n, nblk - 1)
    prev = lambda n: jnp.maximum(n - 1, 0)
    once = pl.Buffered(1)
    both = lambda rows: pl.BlockSpec((2, rows, FW), lambda n: (0, 0, cur(n)))
    taps = lambda h: pl.BlockSpec((3, FW), lambda n, h=h: (0, h * nblk + cur(n)))
    wcols = lambda h: pl.BlockSpec((1, k, FW), lambda n, h=h: (prev(n) // per_chip + 2 * h, 0, prev(n) % per_chip))
    return _call(
        body, name="ffn_up_bwd", grid=(nblk + 1,),
        in_specs=[pl.BlockSpec((s, D_MODEL), lambda n: (0, 0), pipeline_mode=once),
                  pl.BlockSpec((FW, D_MODEL), lambda n: (cur(n), 0)), both(s), both(s),
                  pl.BlockSpec((s, FW), lambda n: (0, cur(n))), taps(0), taps(1), wcols(0), wcols(1),
                  pl.BlockSpec((k, s), lambda n: (0, 0), pipeline_mode=once)],
        out_specs=[pl.BlockSpec((s, k), lambda n: (0, 0), pipeline_mode=once),
                   pl.BlockSpec((2, k, FW), lambda n: (0, 0, prev(n))),
                   pl.BlockSpec((FW, D_MODEL), lambda n: (cur(n), 0)), both(3), both(1)],
        out_shape=[jax.ShapeDtypeStruct((s, k), F32), jax.ShapeDtypeStruct((2, k, D_FF), BF16),
                   jax.ShapeDtypeStruct((D_FF, D_MODEL), BF16),
                   jax.ShapeDtypeStruct((2, 3, D_FF), F32), jax.ShapeDtypeStruct((2, 1, D_FF), F32)],
        scratch_shapes=[pltpu.VMEM((2, s, FW), BF16)],
        operands=(dout, wdown, up, act, f, fw, fw, wup4, wup4, h2t), ride=ride)


def matmul_cols_bwd(dy, other, name, wgrad, ride=None):
    m = dy[0].shape[1]
    if wgrad:
        k = other.shape[0]
        nj, nb = N_CHIPS, sum(d.shape[0] * d.shape[2] for d in dy) // (N_CHIPS * CW)
    else:
        nj, k, ns = other.shape
        nb = ns // CW
    per_seg = dy[0].shape[2] // CW
    first = [sum(d.shape[0] for d in dy[:i]) for i in range(len(dy))]

    def segment(j, b):
        return (j * nb + b) // per_seg, (j * nb + b) % per_seg

    def body(*refs):
        dy_refs, (o_ref, r_ref) = refs[:len(dy)], refs[len(dy):]
        seg, _ = segment(pl.program_id(0), pl.program_id(1))
        dyb = dy_refs[-1][0]
        for i in range(len(dy) - 2, -1, -1):
            dyb = jnp.where(seg < first[i + 1], dy_refs[i][0], dyb)
        if wgrad:
            r_ref[...] = _dot(o_ref[...], dyb).astype(BF16)
        else:
            @pl.when((pl.program_id(0) == 0) & (pl.program_id(1) == 0))
            def _():
                r_ref[...] = jnp.zeros_like(r_ref)

            r_ref[...] += _dot_nt(dyb, o_ref[0])

    def dy_spec(i):
        nseg = dy[i].shape[0]

        def index(j, b):
            seg, col = segment(j, b)
            local = seg - first[i]
            return (jnp.clip(local, 0, nseg - 1), 0,
                    jnp.where(local < 0, 0, jnp.where(local >= nseg, per_seg - 1, col)))

        return pl.BlockSpec((1, m, CW), index)

    if wgrad:
        other_spec = pl.BlockSpec((k, m), lambda j, b: (0, 0))
        out_spec = pl.BlockSpec((k, CW), lambda j, b: (0, j * nb + b))
        out_shape = jax.ShapeDtypeStruct((k, nj * nb * CW), BF16)
    else:
        other_spec = pl.BlockSpec((1, k, CW), lambda j, b: (j, 0, b))
        out_spec = pl.BlockSpec((m, k), lambda j, b: (0, 0))
        out_shape = jax.ShapeDtypeStruct((m, k), F32)
    return _call(
        body, name=name, grid=(nj, nb), in_specs=[dy_spec(i) for i in range(len(dy))] + [other_spec],
        out_specs=[out_spec], out_shape=[out_shape], operands=(*dy, other), ride=ride)


def norms_mid_bwd(dh2, x2, dy, mix, g3, g2, ride=None):
    s, d = x2.shape
    t = _token_tile(s)

    def body(dh2_ref, x2_ref, dy_ref, mix_ref, g3_ref, g2_ref, dx2_ref, dmix_ref, dg3_ref, dg2_ref):
        @pl.when(pl.program_id(0) == 0)
        def _():
            dg3_ref[...] = jnp.zeros_like(dg3_ref)
            dg2_ref[...] = jnp.zeros_like(dg2_ref)

        n3, r3 = _rms_stats(x2_ref[...])
        dx, dg3 = _rms_bwd(n3, r3, g3_ref[...], dh2_ref[...])
        dx2 = dy_ref[...] + dx
        dx2_ref[...] = dx2
        dg3_ref[...] += jnp.sum(dg3, axis=0, keepdims=True)
        n2, r2 = _rms_stats(mix_ref[...])
        dmix, dg2 = _rms_bwd(n2, r2, g2_ref[...], dx2)
        dmix_ref[...] = dmix.astype(BF16)
        dg2_ref[...] += jnp.sum(dg2, axis=0, keepdims=True)

    tile = pl.BlockSpec((t, d), lambda i: (i, 0))
    vec = pl.BlockSpec((1, d), lambda i: (0, 0))
    v = jax.ShapeDtypeStruct((1, d), F32)
    return _call(
        body, name="norms_mid_bwd", grid=(s // t,),
        in_specs=[tile, tile, tile, tile, vec, vec],
        out_specs=[tile, tile, vec, vec],
        out_shape=[jax.ShapeDtypeStruct((s, d), F32), jax.ShapeDtypeStruct((s, d), BF16), v, v],
        operands=(dh2, x2, dy, mix, g3, g2), ride=ride)


def mix_out_bwd(dmix, wout, merged, a, b, proj, ride=None):
    s = dmix.shape[0]
    nblk = D_MODEL // CW

    def body(dm_ref, w_ref, mg_ref, a_ref, b_ref, gc_ref, gl_ref, da_ref, db_ref, dw_ref, dg_ref):
        dm = dm_ref[...]
        dmerged = _dot_nt(dm, w_ref[...])
        dw_ref[...] = _dot_tn(mg_ref[...], dm).astype(BF16)
        sc = _sigmoid(gc_ref[...])
        sl = _sigmoid(gl_ref[...])
        da_ref[...] = (dmerged * sc).astype(BF16)
        db_ref[...] = (dmerged * sl).astype(BF16)
        dg_ref[0] = (dmerged * a_ref[...] * sc * (1.0 - sc)).astype(BF16)
        dg_ref[1] = (dmerged * b_ref[...] * sl * (1.0 - sl)).astype(BF16)

    res = pl.BlockSpec((s, D_MODEL), lambda n: (0, 0))
    rows = pl.BlockSpec((CW, D_MODEL), lambda n: (n, 0))
    col = pl.BlockSpec((s, CW), lambda n: (0, n))
    blk = lambda k: pl.BlockSpec((s, CW), lambda n, k=k: (0, k * nblk + n))
    hb = jax.ShapeDtypeStruct((s, D_MODEL), BF16)
    return _call(
        body, name="mix_out_bwd", grid=(nblk,),
        in_specs=[res, rows, col, col, col, blk(5), blk(6)],
        out_specs=[col, col, rows, pl.BlockSpec((2, s, CW), lambda n: (0, 0, n))],
        out_shape=[hb, hb, jax.ShapeDtypeStruct((D_MODEL, D_MODEL), BF16), jax.ShapeDtypeStruct((2, s, D_MODEL), BF16)],
        operands=(dmix, wout, merged, a, b, proj, proj), ride=ride)


def mix_conv_bwd(da, wcb, proj, q, ws, ride=None):
    s = da.shape[0]
    nblk = D_MODEL // CW

    def body(da_ref, w_ref, cb_ref, cc_ref, cx_ref, q_ref, ws_ref, dc_ref, dw_ref, dws_ref):
        dab = da_ref[...]
        dya = _dot_nt(dab, w_ref[...])
        cb = cb_ref[...]
        cc = cc_ref[...]
        cx = cx_ref[...]
        q = q_ref[...]
        dw_ref[...] = _dot_tn((cb * q).astype(BF16), dab).astype(BF16)
        dc_ref[0] = (dya * q).astype(BF16)
        dq = _advances(dya * cb, 3)
        dp = _taps_sum(dq, ws_ref)
        dws_ref[...] = _conv_wgrad(dq, cc * cx)
        dc_ref[1] = (dp * cx).astype(BF16)
        dc_ref[2] = (dp * cc).astype(BF16)

    res = pl.BlockSpec((s, D_MODEL), lambda n: (0, 0))
    rows = pl.BlockSpec((CW, D_MODEL), lambda n: (n, 0))
    col = pl.BlockSpec((s, CW), lambda n: (0, n))
    blk = lambda k: pl.BlockSpec((s, CW), lambda n, k=k: (0, k * nblk + n))
    taps = pl.BlockSpec((3, CW), lambda n: (0, n))
    hb = jax.ShapeDtypeStruct((s, D_MODEL), BF16)
    return _call(
        body, name="mix_conv_bwd", grid=(nblk,),
        in_specs=[res, rows, blk(0), blk(1), blk(2), col, taps],
        out_specs=[pl.BlockSpec((3, s, CW), lambda n: (0, 0, n)), rows, taps],
        out_shape=[jax.ShapeDtypeStruct((3, s, D_MODEL), BF16), jax.ShapeDtypeStruct((D_MODEL, D_MODEL), BF16),
                   jax.ShapeDtypeStruct((3, D_MODEL), F32)],
        operands=(da, wcb, proj, proj, proj, q, ws), ride=ride)


def mix_lru_bwd(db, wlb, proj, xl, r, i, h, wl, wa, wx, lam, ride=None):
    s = db.shape[0]
    nblk = D_MODEL // CW

    def body(db_ref, w_ref, lx_ref, ly_ref, xl_ref, r_ref, i_ref, h_ref, wl_ref, wa_ref, wx_ref, lam_ref,
             dl_ref, dw_ref, dwa_ref, dwx_ref, dba_ref, dbx_ref, dwl_ref, dbl_ref, dlam_ref,
             c_scr, g_scr):
        dbb = db_ref[...]
        dyb = _dot_nt(dbb, w_ref[...])
        h = h_ref[...]
        ge, dge = _gelu_and_grad(ly_ref[...])
        dw_ref[...] = _dot_tn((h * ge).astype(BF16), dbb).astype(BF16)
        dl_ref[1] = (dyb * h * dge).astype(BF16)
        r = r_ref[...].astype(F32)
        gi = i_ref[...].astype(F32)
        xlb = xl_ref[...]
        xl = xlb.astype(F32)
        lam = lam_ref[...]
        ls = _log_sigmoid(lam)
        a, mult = _lru_gates(r, ls)
        c_scr[...] = _shift_up(a, 1)
        g_scr[...] = dyb * ge
        _scan_backward(c_scr, g_scr, g_scr)
        du = g_scr[...]
        da = du * _shift_down(h, 1)
        dmult = du * gi * xl
        di = du * mult * xl
        dxl = du * mult * gi
        first = _rows(a.shape) == 0
        dlog_a = da * a - jnp.where(first, 0.0, dmult * a * a / mult)
        dr = dlog_a * (LRU_C * ls)
        dlam_ref[...] = jnp.sum(dlog_a * r, axis=0, keepdims=True) * (LRU_C * (1.0 - _sigmoid(lam)))
        dzr = dr * r * (1.0 - r)
        dzi = di * gi * (1.0 - gi)
        dba_ref[...] = jnp.sum(dzr, axis=0, keepdims=True)
        dbx_ref[...] = jnp.sum(dzi, axis=0, keepdims=True)
        dzrb = dzr.astype(BF16)
        dzib = dzi.astype(BF16)
        dwa_ref[0] = _dot_tn(xlb, dzrb)
        dwx_ref[0] = _dot_tn(xlb, dzib)
        dxl = _advances(dxl + _dot_nt(dzrb, wa_ref[0]) + _dot_nt(dzib, wx_ref[0]), 4)
        dl_ref[0] = _taps_sum(dxl, wl_ref).astype(BF16)
        dwl_ref[...] = _conv_wgrad(dxl, lx_ref[...])
        dbl_ref[...] = jnp.sum(dxl[0], axis=0, keepdims=True)

    res = pl.BlockSpec((s, D_MODEL), lambda n: (0, 0))
    rows = pl.BlockSpec((CW, D_MODEL), lambda n: (n, 0))
    col = pl.BlockSpec((s, CW), lambda n: (0, n))
    blk = lambda k: pl.BlockSpec((s, CW), lambda n, k=k: (0, k * nblk + n))
    taps = pl.BlockSpec((4, CW), lambda n: (0, n))
    vec = pl.BlockSpec((1, CW), lambda n: (0, n))
    mat = pl.BlockSpec((1, CW, CW), lambda n: (n, 0, 0))
    hb = jax.ShapeDtypeStruct((s, D_MODEL), BF16)
    v = jax.ShapeDtypeStruct((1, D_MODEL), F32)
    m = jax.ShapeDtypeStruct((LRU_HEADS, HEAD_DIM, HEAD_DIM), F32)
    scr = pltpu.VMEM((s, CW), F32)
    return _call(
        body, name="mix_lru_bwd", grid=(nblk,),
        in_specs=[res, rows, blk(3), blk(4), col, col, col, col, taps, mat, mat, vec],
        out_specs=[pl.BlockSpec((2, s, CW), lambda n: (0, 0, n)), rows, mat, mat, vec, vec, taps, vec, vec],
        out_shape=[jax.ShapeDtypeStruct((2, s, D_MODEL), BF16), jax.ShapeDtypeStruct((D_MODEL, D_MODEL), BF16), m, m, v, v,
                   jax.ShapeDtypeStruct((4, D_MODEL), F32), v, v],
        scratch_shapes=[scr, scr],
        operands=(db, wlb, proj, proj, xl, r, i, h, wl, wa, wx, lam), ride=ride)


def norm_in_bwd(dh1, x, dx2, g1, ride=None):
    s, d = x.shape
    t = _token_tile(s)

    def body(dh_ref, x_ref, dx2_ref, g_ref, dx_ref, dg_ref):
        @pl.when(pl.program_id(0) == 0)
        def _():
            dg_ref[...] = jnp.zeros_like(dg_ref)

        n, r = _rms_stats(x_ref[...])
        dx, dg = _rms_bwd(n, r, g_ref[...], dh_ref[...])
        dx_ref[...] = dx2_ref[...] + dx
        dg_ref[...] += jnp.sum(dg, axis=0, keepdims=True)

    tile = pl.BlockSpec((t, d), lambda i: (i, 0))
    vec = pl.BlockSpec((1, d), lambda i: (0, 0))
    return _call(
        body, name="norm_in_bwd", grid=(s // t,),
        in_specs=[tile, tile, tile, vec],
        out_specs=[tile, vec],
        out_shape=[jax.ShapeDtypeStruct((s, d), F32), jax.ShapeDtypeStruct((1, d), F32)],
        operands=(dh1, x, dx2, g1), ride=ride)


def _owned_part(ref, kind, k, h, hr):
    if kind == "col":
        ns = ref.shape[1] // N_CHIPS
        return ref.at[pl.ds(h * hr, hr), pl.ds(k * ns, ns)]
    if kind == "row":
        return ref.at[pl.ds(k * 2 * hr + h * hr, hr), :]
    if kind == "col2":
        ns = ref.shape[2] // 2
        return ref.at[k // 2, pl.ds(h * hr, hr), pl.ds((k % 2) * ns, ns)]
    return ref.at[k, pl.ds(h * hr, hr), :]


def _part_shape(g, kind):
    if kind == "col2":
        return g.shape[1] // 2, g.shape[2] // 2
    if kind == "col":
        return g.shape[0] // 2, g.shape[1] // N_CHIPS
    if kind == "row":
        return g.shape[0] // (2 * N_CHIPS), g.shape[1]
    return g.shape[1] // 2, g.shape[2]


def pair_split(grads, kinds, name):
    n = len(grads)
    shapes = [_part_shape(g, k) for g, k in zip(grads, kinds)]

    def body(*refs):
        ins, theirs = refs[:n], refs[n:2 * n]
        send_sem, recv_sem = refs[2 * n:]
        x, y, c = _position()
        copies = []
        for a in range(n):
            hr = shapes[a][0]
            for k in range(N_CHIPS):
                s = a * N_CHIPS + k
                copies.append(pltpu.make_async_remote_copy(
                    src_ref=_owned_part(ins[a], kinds[a], k, 1 - c, hr), dst_ref=theirs[a].at[k],
                    send_sem=send_sem.at[s], recv_sem=recv_sem.at[s], device_id=(x, y, 1 - c), device_id_type=MESH))
        _handshake([(x, y, 1 - c)])
        for cp in copies:
            cp.start()
        for cp in copies:
            cp.wait()

    return pl.pallas_call(
        body, name=name,
        in_specs=[_HBM] * n, out_specs=[_HBM] * n,
        out_shape=[jax.ShapeDtypeStruct((N_CHIPS,) + shp, g.dtype) for shp, g in zip(shapes, grads)],
        scratch_shapes=[pltpu.SemaphoreType.DMA((n * N_CHIPS,))] * 2,
        compiler_params=pltpu.CompilerParams(collective_id=SIBLING),
    )(*grads)


def pair_swap(halves):
    n = len(halves)

    def body(*refs):
        ins, outs = refs[:n], refs[n:2 * n]
        send_sem, recv_sem = refs[2 * n:]
        x, y, c = _position()
        copies = [pltpu.make_async_remote_copy(
            src_ref=ins[a], dst_ref=outs[a], send_sem=send_sem.at[a], recv_sem=recv_sem.at[a],
            device_id=(x, y, 1 - c), device_id_type=MESH) for a in range(n)]
        _handshake([(x, y, 1 - c)])
        for cp in copies:
            cp.start()
        for cp in copies:
            cp.wait()

    return pl.pallas_call(
        body, name="pair_swap",
        in_specs=[_HBM] * n, out_specs=[_HBM] * n,
        out_shape=[jax.ShapeDtypeStruct(h.shape, h.dtype) for h in halves],
        scratch_shapes=[pltpu.SemaphoreType.DMA((n,))] * 2,
        compiler_params=pltpu.CompilerParams(collective_id=SIBLING),
    )(*halves)


def _row_tile(rows, cols, limit_bytes=1 << 20):
    best = None
    for t in range(SUBLANES, rows + 1, SUBLANES):
        if rows % t == 0 and t * cols * 4 <= limit_bytes:
            best = t
    return best or rows


def add_pair(g, kind, theirs, core, name):
    nc, rows, cols = theirs.shape
    t = _row_tile(rows, cols, 4 << 20)
    nt = rows // t

    def body(core_ref, g_ref, b_ref, o_ref):
        mine = g_ref[...].reshape(t, cols)
        o_ref[0] = (mine.astype(F32) + b_ref[0].astype(F32)).astype(o_ref.dtype)

    if kind == "col":
        own = pl.BlockSpec((t, cols), lambda k, i, c: (c[0] * nt + i, k))
    elif kind == "col2":
        own = pl.BlockSpec((1, t, cols), lambda k, i, c: (k // 2, c[0] * nt + i, k % 2))
    elif kind == "row":
        own = pl.BlockSpec((t, cols), lambda k, i, c: ((2 * k + c[0]) * nt + i, 0))
    else:
        own = pl.BlockSpec((1, t, cols), lambda k, i, c: (k, c[0] * nt + i, 0))
    spec = pl.BlockSpec((1, t, cols), lambda k, i, c: (k, i, 0))
    return pl.pallas_call(
        body, name=name,
        grid_spec=pltpu.PrefetchScalarGridSpec(num_scalar_prefetch=1, grid=(nc, nt), in_specs=[own, spec], out_specs=spec),
        out_shape=jax.ShapeDtypeStruct(theirs.shape, theirs.dtype), compiler_params=_params(),
    )(core, g, theirs)


def sum_lead(a, name):
    nl, rows, cols = a.shape
    t = _row_tile(rows, cols, (1 << 20) // 2)

    def body(a_ref, o_ref):
        acc = a_ref[0].astype(F32)
        for s in range(1, nl):
            acc = acc + a_ref[s].astype(F32)
        o_ref[...] = acc

    return pl.pallas_call(
        body, name=name, grid=(rows // t,),
        in_specs=[pl.BlockSpec((nl, t, cols), lambda i: (0, i, 0))],
        out_specs=pl.BlockSpec((t, cols), lambda i: (i, 0)),
        out_shape=jax.ShapeDtypeStruct((rows, cols), F32), compiler_params=_params(),
    )(a)


def sum_chips(rx, csum, chip, name):
    nc, rows, cols = rx.shape
    t = _row_tile(rows, cols, 2 << 20)

    def body(chip_ref, r0, r1, r2, r3, own_ref, o_ref):
        acc = None
        for s, ref in enumerate((r0, r1, r2, r3)):
            term = jnp.where(chip_ref[0] == s, own_ref[0], ref[0]).astype(F32)
            acc = term if acc is None else acc + term
        o_ref[...] = acc

    def slot(s):
        return pl.BlockSpec((1, t, cols), lambda i, c, s=s: (jnp.where(c[0] == s, c[0] ^ 1, s), i, 0))

    return pl.pallas_call(
        body, name=name,
        grid_spec=pltpu.PrefetchScalarGridSpec(
            num_scalar_prefetch=1, grid=(rows // t,),
            in_specs=[slot(s) for s in range(nc)] + [pl.BlockSpec((1, t, cols), lambda i, c: (c[0], i, 0))],
            out_specs=pl.BlockSpec((t, cols), lambda i, c: (i, 0))),
        out_shape=jax.ShapeDtypeStruct((rows, cols), F32), compiler_params=_params(),
    )(chip, rx, rx, rx, rx, csum)


def cast_bf16(a, name):
    rows, cols = a.shape
    t = _row_tile(rows, cols, 2 << 20)

    def body(i_ref, o_ref):
        o_ref[...] = i_ref[...].astype(BF16)

    spec = pl.BlockSpec((t, cols), lambda i: (i, 0))
    return pl.pallas_call(body, name=name, grid=(rows // t,), in_specs=[spec], out_specs=spec,
                          out_shape=jax.ShapeDtypeStruct((rows, cols), BF16), compiler_params=_params())(a)


def _adamw_update(w, g, m, v):
    nm = ADAM_B1 * m + (1.0 - ADAM_B1) * g
    nv = ADAM_B2 * v + (1.0 - ADAM_B2) * (g * g)
    m_hat = nm * (1.0 / (1.0 - ADAM_B1 ** ADAM_STEP))
    v_hat = nv * (1.0 / (1.0 - ADAM_B2 ** ADAM_STEP))
    return -ADAM_LR * (m_hat / (jnp.sqrt(v_hat) + ADAM_EPS) + ADAM_WD * w), nm, nv


def adamw(w, g, m, v, name):
    rows, cols = w.shape
    t = _row_tile(rows, cols)

    def body(w_ref, g_ref, m_ref, v_ref, d_ref, nm_ref, nv_ref):
        d_ref[...], nm_ref[...], nv_ref[...] = _adamw_update(w_ref[...], g_ref[...], m_ref[...], v_ref[...])

    spec = pl.BlockSpec((t, cols), lambda i: (i, 0))
    shp = jax.ShapeDtypeStruct((rows, cols), F32)
    return pl.pallas_call(
        body, name=name, grid=(rows // t,), in_specs=[spec] * 4, out_specs=[spec] * 3,
        out_shape=[shp, shp, shp], compiler_params=_params(),
    )(w, g, m, v)


def adamw_halves(w, g_mine, g_other, m, v, core, name):
    rows, cols = w.shape
    hr = rows // 2
    t = _row_tile(hr, cols)
    nt = hr // t

    def body(core_ref, w_ref, gm_ref, go_ref, m_ref, v_ref, g_ref, d_ref, nm_ref, nv_ref):
        g = jnp.where(pl.program_id(0) // nt == core_ref[0], gm_ref[...], go_ref[...])
        g_ref[...] = g
        d_ref[...], nm_ref[...], nv_ref[...] = _adamw_update(w_ref[...], g, m_ref[...], v_ref[...])

    spec = pl.BlockSpec((t, cols), lambda i, c: (i, 0))
    half = pl.BlockSpec((t, cols), lambda i, c: (i % nt, 0))
    shp = jax.ShapeDtypeStruct((rows, cols), F32)
    return pl.pallas_call(
        body, name=name,
        grid_spec=pltpu.PrefetchScalarGridSpec(num_scalar_prefetch=1, grid=(2 * nt,),
                                               in_specs=[spec, half, half, spec, spec], out_specs=[spec] * 4),
        out_shape=[shp] * 4, compiler_params=_params(),
    )(core, w, g_mine, g_other, m, v)


WEIGHTS = ("norm_mix_pre", "norm_mix_post", "norm_ffn_pre", "norm_ffn_post", "w_in", "conv_short_w",
           "w_conv_branch", "lru_conv_w", "lru_conv_b", "lru_wa", "lru_ba", "lru_wx", "lru_bx", "lru_lambda",
           "w_lru_branch", "w_out", "ffn_w_up", "ffn_conv_w", "ffn_conv_b", "ffn_w_down")
BIG = ("w_in", "ffn_w_up", "w_conv_branch", "w_lru_branch", "w_out", "ffn_w_down")
BIG_KIND = ("col", "col", "row", "row", "row", "row")
SMALL = ("conv_short_w", "lru_conv_w", "lru_wa", "lru_ba", "lru_wx", "lru_bx", "ffn_conv_w")
REPL = ("norm_mix_pre", "norm_mix_post", "norm_ffn_pre", "norm_ffn_post", "lru_conv_b", "lru_lambda", "ffn_conv_b")
PACK_W = 256
SMALL_ROWS = 576
REPL_ROWS = 16
LOSS_ROW = 12
FFN_SHARD = 2 * D_FF // N_CHIPS
QUARTER = HEAD_DIM // N_CHIPS
SMALL_PARTS = (("conv_short_w", 3, (1, 3, PACK_W)), ("lru_conv_w", 4, (1, 4, PACK_W)),
               ("lru_wa", LRU_HEADS * QUARTER, (1, LRU_HEADS, QUARTER, HEAD_DIM)), ("lru_ba", LRU_HEADS, (1, LRU_HEADS, QUARTER)),
               ("lru_wx", LRU_HEADS * QUARTER, (1, LRU_HEADS, QUARTER, HEAD_DIM)), ("lru_bx", LRU_HEADS, (1, LRU_HEADS, QUARTER)),
               ("ffn_conv_w", 3 * FFN_SHARD // PACK_W, (1, 3, FFN_SHARD)))


def _pad8(nr):
    return -(-nr // SUBLANES) * SUBLANES


SMALL_OFFSET = {}
for _name, _nr, _ in SMALL_PARTS:
    SMALL_OFFSET[_name] = sum(_pad8(nr) for n, nr, _ in SMALL_PARTS[:len(SMALL_OFFSET)])
FFN_ROWS = FFN_SHARD // PACK_W
BIASES = ("lru_ba", "lru_bx")
TAPS3 = ("conv_short_w", "ffn_conv_w")


def pack_small(dicts):
    names = [n for n, _, _ in SMALL_PARTS]
    operands = [d[n].transpose(1, 0, 2) if n in TAPS3 else d[n] for d in dicts for n in names]

    def body(*refs):
        ins, outs = refs[:len(operands)], refs[len(operands):]
        for i, o in enumerate(outs):
            o[...] = jnp.zeros_like(o)
            for (name, nr, shape), p in zip(SMALL_PARTS, ins[i * len(names):(i + 1) * len(names)]):
                r0 = SMALL_OFFSET[name]
                if name in BIASES:
                    o[r0:r0 + nr, 0:QUARTER] = p[0]
                elif name == "ffn_conv_w":
                    for k in range(shape[1]):
                        for s in range(FFN_ROWS):
                            o[r0 + FFN_ROWS * k + s:r0 + FFN_ROWS * k + s + 1, :] = p[k, :, s * PACK_W:(s + 1) * PACK_W]
                elif name == "conv_short_w":
                    for k in range(nr):
                        o[r0 + k:r0 + k + 1, :] = p[k]
                else:
                    o[r0:r0 + nr, :] = p[0].reshape(nr, PACK_W)

    shape = jax.ShapeDtypeStruct((SMALL_ROWS, PACK_W), F32)
    return pl.pallas_call(body, name="pack_small", out_shape=[shape] * len(dicts), compiler_params=_params())(*operands)


def full_small(g4):
    def body(p, csw, lcw, wa, wx, fcw):
        chips = range(N_CHIPS)
        r0 = SMALL_OFFSET["conv_short_w"]
        csw[...] = jnp.concatenate([p[c, r0:r0 + 3, :] for c in chips], axis=1)
        r0 = SMALL_OFFSET["lru_conv_w"]
        lcw[...] = jnp.concatenate([p[c, r0:r0 + 4, :] for c in chips], axis=1)
        for name, o in (("lru_wa", wa), ("lru_wx", wx)):
            r0 = SMALL_OFFSET[name]
            for h in range(LRU_HEADS):
                for c in chips:
                    o[h, c * QUARTER:(c + 1) * QUARTER, :] = p[c, r0 + h * QUARTER:r0 + (h + 1) * QUARTER, :].astype(BF16)
        r0 = SMALL_OFFSET["ffn_conv_w"]
        for k in range(3):
            fcw[k:k + 1, :] = jnp.concatenate(
                [p[c, r0 + FFN_ROWS * k + s:r0 + FFN_ROWS * k + s + 1, :] for c in chips for s in range(FFN_ROWS)], axis=1)

    mat = jax.ShapeDtypeStruct((LRU_HEADS, HEAD_DIM, HEAD_DIM), BF16)
    csw, lcw, wa, wx, fcw = pl.pallas_call(
        body, name="full_small",
        out_shape=[jax.ShapeDtypeStruct((3, D_MODEL), F32), jax.ShapeDtypeStruct((4, D_MODEL), F32), mat, mat,
                   jax.ShapeDtypeStruct((3, 2 * D_FF), F32)],
        compiler_params=_params())(g4)

    def bias(name):
        r0 = SMALL_OFFSET[name]
        return g4[:, r0:r0 + LRU_HEADS, :QUARTER].transpose(1, 0, 2).reshape(1, D_MODEL)

    return dict(conv_short_w=csw, lru_conv_w=lcw, lru_wa=wa, lru_wx=wx, ffn_conv_w=fcw,
                lru_ba=bias("lru_ba"), lru_bx=bias("lru_bx"))


def split_small(full):
    def bias(name):
        return full[name].reshape(LRU_HEADS, N_CHIPS, QUARTER).transpose(1, 0, 2)

    def body(csw, lcw, wa, wx, fcw, ba, bx, o):
        o[...] = jnp.zeros_like(o)
        for c in range(N_CHIPS):
            cols = slice(c * PACK_W, (c + 1) * PACK_W)
            r0 = SMALL_OFFSET["conv_short_w"]
            o[c, r0:r0 + 3, :] = csw[:, cols]
            r0 = SMALL_OFFSET["lru_conv_w"]
            o[c, r0:r0 + 4, :] = lcw[:, cols]
            for name, p in (("lru_wa", wa), ("lru_wx", wx)):
                r0 = SMALL_OFFSET[name]
                for h in range(LRU_HEADS):
                    o[c, r0 + h * QUARTER:r0 + (h + 1) * QUARTER, :] = p[h, c * QUARTER:(c + 1) * QUARTER, :]
            for name, p in (("lru_ba", ba), ("lru_bx", bx)):
                r0 = SMALL_OFFSET[name]
                o[c, r0:r0 + LRU_HEADS, 0:QUARTER] = p[c]
            r0 = SMALL_OFFSET["ffn_conv_w"]
            for k in range(3):
                for s in range(FFN_ROWS):
                    lo = c * FFN_SHARD + s * PACK_W
                    o[c, r0 + FFN_ROWS * k + s:r0 + FFN_ROWS * k + s + 1, :] = fcw[k:k + 1, lo:lo + PACK_W]

    return pl.pallas_call(
        body, name="split_small", out_shape=jax.ShapeDtypeStruct((N_CHIPS, SMALL_ROWS, PACK_W), F32),
        compiler_params=_params(),
    )(full["conv_short_w"], full["lru_conv_w"], full["lru_wa"], full["lru_wx"], full["ffn_conv_w"],
      bias("lru_ba"), bias("lru_bx"))


def pack_repl(dicts, loss=None):
    operands = [d[n] for d in dicts for n in REPL] + ([loss] if loss is not None else [])

    def body(*refs):
        ins, outs = refs[:len(operands)], refs[len(operands):]
        for i, o in enumerate(outs):
            o[...] = jnp.zeros_like(o)
            r0 = 0
            for p in ins[i * len(REPL):(i + 1) * len(REPL)]:
                for s in range(p.shape[1] // D_MODEL):
                    o[r0:r0 + 1, :] = p[:, s * D_MODEL:(s + 1) * D_MODEL]
                    r0 += 1
        if loss is not None:
            outs[-1][LOSS_ROW:LOSS_ROW + 1, :] = jnp.tile(ins[-1][...], (1, D_MODEL // 128))

    shape = jax.ShapeDtypeStruct((REPL_ROWS, D_MODEL), F32)
    return pl.pallas_call(body, name="pack_repl" + ("_loss" if loss is not None else ""),
                          out_shape=[shape] * len(dicts), compiler_params=_params())(*operands)


def _lane_concat(ref, r0, n):
    return jnp.concatenate([ref[r0 + s:r0 + s + 1, :] for s in range(n)], axis=1)


def unpack_small(packs):
    names = [n for n, _, _ in SMALL_PARTS]

    def body(*refs):
        ins, outs = refs[:len(packs)], refs[len(packs):]
        for i, p in enumerate(ins):
            for (name, nr, shape), o in zip(SMALL_PARTS, outs[i * len(names):(i + 1) * len(names)]):
                r0 = SMALL_OFFSET[name]
                if name in BIASES:
                    o[0] = p[r0:r0 + nr, 0:QUARTER]
                elif name == "ffn_conv_w":
                    for k in range(shape[1]):
                        o[k] = _lane_concat(p, r0 + FFN_ROWS * k, FFN_ROWS)
                elif name == "conv_short_w":
                    for k in range(nr):
                        o[k] = p[r0 + k:r0 + k + 1, :]
                else:
                    o[0] = p[r0:r0 + nr, :].reshape(shape[1:])

    shapes = [jax.ShapeDtypeStruct((s[1], 1, s[2]) if n in TAPS3 else s, F32) for n, _, s in SMALL_PARTS]
    res = pl.pallas_call(body, name="unpack_small", out_shape=shapes * len(packs), compiler_params=_params())(*packs)
    out = []
    for i in range(len(packs)):
        d = dict(zip(names, res[i * len(names):(i + 1) * len(names)]))
        for n in TAPS3:
            d[n] = d[n].transpose(1, 0, 2)
        out.append(d)
    return out


def unpack_repl(packs):
    rows = [(2 * D_FF // D_MODEL) if n == "ffn_conv_b" else 1 for n in REPL]

    def body(*refs):
        ins, outs = refs[:len(packs)], refs[len(packs):]
        for i, p in enumerate(ins):
            r0 = 0
            for nr, o in zip(rows, outs[i * len(REPL):(i + 1) * len(REPL)]):
                o[...] = _lane_concat(p, r0, nr)
                r0 += nr

    shapes = [jax.ShapeDtypeStruct((1, nr * D_MODEL), F32) for nr in rows]
    res = pl.pallas_call(body, name="unpack_repl", out_shape=shapes * len(packs), compiler_params=_params())(*packs)
    return [dict(zip(REPL, res[i * len(REPL):(i + 1) * len(REPL)])) for i in range(len(packs))]


def kernel(x, norm_mix_pre, norm_mix_post, norm_ffn_pre, norm_ffn_post, w_in, conv_short_w, w_conv_branch, lru_conv_w, lru_conv_b, lru_wa, lru_ba, lru_wx, lru_bx, lru_lambda, w_lru_branch, w_out, ffn_w_up, ffn_conv_w, ffn_conv_b, ffn_w_down, loss_target, m_norm_mix_pre, m_norm_mix_post, m_norm_ffn_pre, m_norm_ffn_post, m_w_in, m_conv_short_w, m_w_conv_branch, m_lru_conv_w, m_lru_conv_b, m_lru_wa, m_lru_ba, m_lru_wx, m_lru_bx, m_lru_lambda, m_w_lru_branch, m_w_out, m_ffn_w_up, m_ffn_conv_w, m_ffn_conv_b, m_ffn_w_down, v_norm_mix_pre, v_norm_mix_post, v_norm_ffn_pre, v_norm_ffn_post, v_w_in, v_conv_short_w, v_w_conv_branch, v_lru_conv_w, v_lru_conv_b, v_lru_wa, v_lru_ba, v_lru_wx, v_lru_bx, v_lru_lambda, v_w_lru_branch, v_w_out, v_ffn_w_up, v_ffn_conv_w, v_ffn_conv_b, v_ffn_w_down):
    given = dict(locals())
    w = {n: given[n] for n in WEIGHTS}
    m = {n: given["m_" + n] for n in WEIGHTS}
    v = {n: given["v_" + n] for n in WEIGHTS}

    xi, yi, ci = _position()
    chip_i = 2 * xi + yi
    chip = chip_i.astype(jnp.int32).reshape(1)
    core = ci.astype(jnp.int32).reshape(1)
    xs, target = x[0], loss_target[0]
    g1, g2, g3, g4 = w["norm_mix_pre"], w["norm_mix_post"], w["norm_ffn_pre"], w["norm_ffn_post"]
    shard = {n: cast_bf16(w[n][0], "cast_" + n) for n in BIG}
    small_shard, m_small, v_small = pack_small([w, m, v])

    def gathered(bufs, names):
        return [_own_slot(b, small_shard if n == "small" else shard[n], chip_i) for b, n in zip(bufs, names)]

    def chip_sums(arrays, kinds, tag):
        theirs = pair_split(arrays, kinds, "pair_split_" + tag)
        return [add_pair(g, k, t, core, "pair_add_%s_%d" % (tag, i)) for i, (g, k, t) in enumerate(zip(arrays, kinds, theirs))]

    h1, h1t = norm_in(xs, g1)
    win4, small4 = gathered(run_ride(gather_ride([shard["w_in"], small_shard]), "gather_first"), ("w_in", "small"))
    small = full_small(small4)
    (proj,), got = matmul_cols(h1, win4, "proj_fwd",
                               ride=gather_ride([shard["w_conv_branch"], shard["w_lru_branch"], shard["w_out"]]))
    wcb, wlb, wout = [g.reshape(-1, D_MODEL) for g in gathered(got, ("w_conv_branch", "w_lru_branch", "w_out"))]
    up_piece = lambda r0, nr, into=None: gather_ride([shard["ffn_w_up"]], items=[(0, r0, nr)], into=into)
    down_piece = lambda r0, nr, into=None: gather_ride([shard["ffn_w_down"]], items=[(0, r0, nr)], into=into)
    (q, ya), got = mix_conv_fwd(proj, small["conv_short_w"], ride=up_piece(0, 128))
    (xl, r, gi, h, yb), got = mix_lru_fwd(
        proj, small["lru_conv_w"], w["lru_conv_b"], small["lru_wa"], small["lru_ba"],
        small["lru_wx"], small["lru_bx"], w["lru_lambda"], ride=up_piece(128, 512, got))
    (a, b, merged), got = branch_merge_fwd(ya, yb, wcb, wlb, proj, ride=up_piece(640, 384, got))
    (wup4,) = gathered(got, ("ffn_w_up",))
    (mix, x2, h2, h2t), got = mix_out_fwd(merged, wout, xs, g2, g3, ride=down_piece(0, 256))
    (up, act, f), got = ffn_up_act_fwd(h2, wup4, small["ffn_conv_w"], w["ffn_conv_b"], ride=down_piece(256, 512, got))
    wdown = gathered(got, ("ffn_w_down",))[0].reshape(-1, D_MODEL)
    dy, dout, loss, dg4 = ffn_down_loss(f, wdown, x2, target, g4)

    dh2, dwup, dwdown, dfw, dfb = ffn_up_bwd(dout, wdown, up, act, f, small["ffn_conv_w"], wup4, h2t)
    cs_down, cs_up = chip_sums([dwdown, dwup], ["row", "col2"], "ffn")
    down_rows = lambda r0, nr, into=None: exchange_ride([cs_down], items=[(0, r0, nr)], into=into)
    up_rows = lambda r0, nr, into=None: exchange_ride([cs_up], items=[(0, r0, nr)], into=into)
    (dx2, dmix, dg3, dg2), rx_down = norms_mid_bwd(dh2, x2, dy, mix, g3, g2, ride=down_rows(0, 128))
    (da, db, dwout, dgates), rx_down = mix_out_bwd(dmix, wout, merged, a, b, proj, ride=down_rows(128, 256, rx_down))
    (dconv, dwcb, dws), rx_up = mix_conv_bwd(da, wcb, proj, q, small["conv_short_w"], ride=up_rows(0, 176))
    cs_mid = chip_sums([dwout, dwcb], ["row", "row"], "mid")
    (dlru, dwlb, dwa, dwx, dba, dbx, dwl, dbl, dlam), rx_up = mix_lru_bwd(
        db, wlb, proj, xl, r, gi, h, small["lru_conv_w"], small["lru_wa"], small["lru_wx"],
        w["lru_lambda"], ride=up_rows(176, 336, rx_up))
    grads = dict(norm_mix_post=dg2, norm_ffn_pre=dg3, norm_ffn_post=dg4, conv_short_w=dws, lru_conv_w=dwl,
                 lru_conv_b=dbl, lru_wa=dwa, lru_ba=dba, lru_wx=dwx, lru_bx=dbx, lru_lambda=dlam,
                 ffn_conv_w=jnp.concatenate([dfw[0], dfw[1]], axis=1), ffn_conv_b=jnp.concatenate([dfb[0], dfb[1]], axis=1))
    cs_late = chip_sums([dwlb, split_small(grads)], ["row", "lead"], "late")
    dproj = [dconv, dlru, dgates]
    (dwin,), rx_all = matmul_cols_bwd(dproj, h1t, "proj_wgrad", True, ride=exchange_ride(cs_mid + cs_late))
    rx_mid, rx_late = rx_all[:2], rx_all[2:]
    cs_in = chip_sums([dwin], ["col"], "in")
    in_rows = lambda r0, nr, into=None: exchange_ride(cs_in, items=[(0, r0, nr)], into=into)
    (dh1,), rx_in = matmul_cols_bwd(dproj, win4, "proj_dgrad", False, ride=in_rows(0, 384))
    (dx, grads["norm_mix_pre"]), rx_in = norm_in_bwd(dh1, xs, dx2, g1, ride=in_rows(384, 128, rx_in))
    rx_in = rx_in[0]
    (rep_part,) = pack_repl([grads], loss)
    (rep_all,) = run_ride(exchange_ride([], rep=rep_part), "exchange_repl")

    order = (("w_in", cs_in[0], rx_in), ("ffn_w_up", cs_up, rx_up[0]), ("w_conv_branch", cs_mid[1], rx_mid[1]),
             ("w_lru_branch", cs_late[0], rx_late[0]), ("w_out", cs_mid[0], rx_mid[0]),
             ("ffn_w_down", cs_down, rx_down[0]), ("small", cs_late[1], rx_late[1]))
    halves = [sum_chips(rx, cs, chip, "chip_sum_" + n) for n, cs, rx in order]
    me = 4 * xi + 2 * yi + ci
    rep_grad = sum_lead(_own_slot(rep_all, rep_part, me), "device_sum")
    others = pair_swap(halves)

    g_out, d_out, m_out, v_out = {}, {}, {}, {}
    for n, gm, go in zip(BIG, halves[:-1], others[:-1]):
        g, d, nm, nv = adamw_halves(w[n][0], gm, go, m[n][0], v[n][0], core, "adamw_" + n)
        g_out[n], d_out[n], m_out[n], v_out[n] = g[None], d[None], nm[None], nv[None]
    bufs = adamw_halves(small_shard, halves[-1], others[-1], m_small, v_small, core, "adamw_small")
    for dst, part in zip((g_out, d_out, m_out, v_out), unpack_small(bufs)):
        dst.update(part)
    w_rep, m_rep, v_rep = pack_repl([w, m, v])
    d, nm, nv = adamw(w_rep, rep_grad, m_rep, v_rep, "adamw_repl")
    for dst, part in zip((g_out, d_out, m_out, v_out), unpack_repl([rep_grad, d, nm, nv])):
        dst.update(part)

    return (rep_grad[LOSS_ROW, 0], dx[None], *[g_out[n] for n in WEIGHTS], *[d_out[n] for n in WEIGHTS],
            *[m_out[n] for n in WEIGHTS], *[v_out[n] for n in WEIGHTS])
```

```python
import functools
import math

import jax
import jax.numpy as jnp
from jax import lax
from jax.experimental import pallas as pl
from jax.experimental.pallas import tpu as pltpu

F32 = jnp.float32
BF16 = jnp.bfloat16

D_MODEL = 1024
N_CHIPS = 4
N_SEG = 7
D_FF = 3 * D_MODEL
LRU_HEADS = 4
HEAD_DIM = D_MODEL // LRU_HEADS
LRU_C = 8.0
RMS_EPS = 1e-6
CW = 256
FW = 256
SUBLANES = 8
VMEM_LIMIT = 58 * 1024 * 1024

ADAM_LR = 0.001
ADAM_B1 = 0.9
ADAM_B2 = 0.999
ADAM_EPS = 1e-08
ADAM_WD = 0.01
ADAM_STEP = 10

_GELU_C = math.sqrt(2.0 / math.pi)
_GELU_K = 0.044715


def _params(**kw):
    return pltpu.CompilerParams(vmem_limit_bytes=VMEM_LIMIT, **kw)


def _sigmoid(x):
    return 1.0 / (1.0 + jnp.exp(-x))


def _gelu(x):
    t = jnp.tanh(_GELU_C * (x + _GELU_K * x * x * x))
    return 0.5 * x * (1.0 + t)


def _gelu_and_grad(x):
    x2 = x * x
    t = jnp.tanh(_GELU_C * (x + _GELU_K * x * x2))
    g = 0.5 * x * (1.0 + t)
    dg = 0.5 * (1.0 + t) + 0.5 * x * (1.0 - t * t) * _GELU_C * (1.0 + 3.0 * _GELU_K * x2)
    return g, dg


def _log_sigmoid(x):
    e = jnp.exp(-jnp.abs(x))
    u = 1.0 + e
    l1p = jnp.where(u == 1.0, e, jnp.log(u) * e / (u - 1.0))
    return jnp.minimum(x, 0.0) - l1p


def _neg_expm1(z):
    series = -z * (1.0 + z * (0.5 + z * (1.0 / 6.0 + z * (1.0 / 24.0 + z * (1.0 / 120.0 + z * (1.0 / 720.0))))))
    return jnp.where(z > -0.2, series, 1.0 - jnp.exp(z))


def _rows(shape):
    return lax.broadcasted_iota(jnp.int32, shape, 0)


def _shift_down(x, k):
    return jnp.where(_rows(x.shape) >= k, pltpu.roll(x, k, 0), 0.0)


def _shift_up(x, k):
    n = x.shape[0]
    return jnp.where(_rows(x.shape) < n - k, pltpu.roll(x, n - k, 0), 0.0)


def _delays(x, k_width):
    return [x] + [_shift_down(x, j) for j in range(1, k_width)]


def _advances(dy, k_width):
    return [dy] + [_shift_up(dy, j) for j in range(1, k_width)]


def _taps_sum(shifted, w_ref, b=None):
    k_width = w_ref.shape[0]
    y = w_ref[k_width - 1:k_width, :] * shifted[0]
    for j in range(1, k_width):
        y = y + w_ref[k_width - 1 - j:k_width - j, :] * shifted[j]
    if b is not None:
        y = y + b
    return y


def _causal_conv(x, w_ref, b=None):
    return _taps_sum(_delays(x, w_ref.shape[0]), w_ref, b)


def _conv_wgrad(advanced, x):
    k_width = len(advanced)
    rows = [jnp.sum(advanced[k_width - 1 - k] * x, axis=0, keepdims=True) for k in range(k_width)]
    return jnp.concatenate(rows, axis=0)


def _dot(a, b):
    return jnp.dot(a, b, preferred_element_type=F32)


def _dot_nt(a, b):
    return lax.dot_general(a, b, (((1,), (1,)), ((), ())), preferred_element_type=F32)


def _dot_tn(a, b):
    return lax.dot_general(a, b, (((0,), (0,)), ((), ())), preferred_element_type=F32)


def _rms_stats(x):
    r = lax.rsqrt(jnp.mean(x * x, axis=-1, keepdims=True) + RMS_EPS)
    return x * r, r


def _rms_bwd(n, r, g, dy):
    dn = dy * g
    dx = r * (dn - n * jnp.mean(dn * n, axis=-1, keepdims=True))
    return dx, dy * n


def _scan_forward(a_ref, b_ref, h_ref):
    n, c = a_ref.shape
    row = lax.broadcasted_iota(jnp.int32, (SUBLANES, c), 0)

    def group(g, carry):
        r0 = pl.multiple_of(g * SUBLANES, SUBLANES)
        a = a_ref[pl.ds(r0, SUBLANES), :]
        b = b_ref[pl.ds(r0, SUBLANES), :]
        for k in (1, 2, 4):
            ap = jnp.where(row >= k, pltpu.roll(a, k, 0), 1.0)
            bp = jnp.where(row >= k, pltpu.roll(b, k, 0), 0.0)
            b = a * bp + b
            a = a * ap
        h = a * carry + b
        h_ref[pl.ds(r0, SUBLANES), :] = h
        return h[SUBLANES - 1:SUBLANES, :]

    lax.fori_loop(0, n // SUBLANES, group, jnp.zeros((1, c), F32))


def _scan_backward(c_ref, b_ref, g_ref):
    n, ch = c_ref.shape
    row = lax.broadcasted_iota(jnp.int32, (SUBLANES, ch), 0)
    n_groups = n // SUBLANES

    def group(i, carry):
        r0 = pl.multiple_of((n_groups - 1 - i) * SUBLANES, SUBLANES)
        a = c_ref[pl.ds(r0, SUBLANES), :]
        b = b_ref[pl.ds(r0, SUBLANES), :]
        for k in (1, 2, 4):
            keep = row < SUBLANES - k
            ap = jnp.where(keep, pltpu.roll(a, SUBLANES - k, 0), 1.0)
            bp = jnp.where(keep, pltpu.roll(b, SUBLANES - k, 0), 0.0)
            b = a * bp + b
            a = a * ap
        g = a * carry + b
        g_ref[pl.ds(r0, SUBLANES), :] = g
        return g[0:1, :]

    lax.fori_loop(0, n_groups, group, jnp.zeros((1, ch), F32))


MESH = pl.DeviceIdType.MESH
_HBM = pl.BlockSpec(memory_space=pltpu.HBM)
_OTHER_CHIPS = ((1, 0), (0, 1), (1, 1))
_OTHER_DEVICES = tuple((dx, dy, dc) for dx in (0, 1) for dy in (0, 1) for dc in (0, 1) if dx or dy or dc)
N_DEVICES = 8


def _position():
    return lax.axis_index("x"), lax.axis_index("y"), lax.axis_index("c")


def _flip(v, d):
    return 1 - v if d else v


def _chip(x, y, p):
    px, py = _flip(x, _OTHER_CHIPS[p][0]), _flip(y, _OTHER_CHIPS[p][1])
    return px, py, 2 * px + py


class _Ride:
    def __init__(self, srcs, bufs, scratch, plan, collective_id):
        self.srcs, self.bufs, self.scratch, self.plan = list(srcs), list(bufs), list(scratch), plan
        self.collective_id = collective_id


NEIGHBOURS_AND_SIBLING = 1
OTHER_CHIPS_SAME_CORE = 2
ALL_DEVICES = 3
SIBLING = 4


def _handshake(peers):
    barrier = pltpu.get_barrier_semaphore()
    for peer in peers:
        pl.semaphore_signal(barrier, inc=1, device_id=peer, device_id_type=MESH)
    pl.semaphore_wait(barrier, len(peers))


def _call(body, *, name, grid, in_specs, out_specs, out_shape, operands, scratch_shapes=(), ride=None):
    in_specs, out_specs, out_shape = list(in_specs), list(out_specs), list(out_shape)
    scratch_shapes = list(scratch_shapes)
    if ride is None:
        return pl.pallas_call(body, name=name, grid=grid, in_specs=in_specs, out_specs=out_specs, out_shape=out_shape,
                              scratch_shapes=scratch_shapes, compiler_params=_params())(*operands)
    n_in, n_out, n_scr = len(in_specs), len(out_shape), len(scratch_shapes)
    old = [i for i, b in enumerate(ride.bufs) if not isinstance(b, jax.ShapeDtypeStruct)]
    n_src, n_old, n_buf = len(ride.srcs), len(old), len(ride.bufs)

    def full_body(*refs):
        o0 = n_in + n_src + n_old
        s0 = o0 + n_out + n_buf
        start, relay, relay_on, finish = ride.plan(refs[n_in:n_in + n_src], refs[o0 + n_out:s0], refs[s0 + n_scr:])
        ids = [pl.program_id(i) for i in range(len(grid))]
        first = functools.reduce(jnp.logical_and, [i == 0 for i in ids])
        middle = functools.reduce(jnp.logical_and, [ids[0] == grid[0] // 2] + [i == 0 for i in ids[1:]])
        last = functools.reduce(jnp.logical_and, [i == g - 1 for i, g in zip(ids, grid)])
        pl.when(first)(start)
        pl.when(middle)(relay)
        pl.when(last)(relay_on)
        body(*refs[:n_in], *refs[o0:o0 + n_out], *refs[s0:s0 + n_scr])
        pl.when(last)(finish)

    shapes = [jax.ShapeDtypeStruct(b.shape, b.dtype) for b in ride.bufs]
    res = pl.pallas_call(
        full_body, name=name, grid=grid,
        in_specs=in_specs + [_HBM] * (n_src + n_old), out_specs=out_specs + [_HBM] * n_buf,
        out_shape=out_shape + shapes, scratch_shapes=scratch_shapes + ride.scratch,
        input_output_aliases={n_in + n_src + k: n_out + i for k, i in enumerate(old)},
        compiler_params=_params(collective_id=ride.collective_id),
    )(*operands, *ride.srcs, *[ride.bufs[i] for i in old])
    return list(res[:n_out]), list(res[n_out:])


def run_ride(ride, name):
    def body():
        pass

    return _call(body, name=name, grid=(1,), in_specs=[], out_specs=[], out_shape=[], operands=[], ride=ride)[1]


def gather_ride(shards, items=None, into=None):
    items = items or [(a, 0, s.shape[0]) for a, s in enumerate(shards)]
    bufs = into or [jax.ShapeDtypeStruct((N_CHIPS,) + s.shape, s.dtype) for s in shards]
    nrel = len(_OTHER_CHIPS)

    def plan(srcs, dsts, sems):
        ici_send, ici_recv, hop_send, hop_recv, sib_send, sib_recv = sems
        x, y, c = _position()
        j = 2 * x + y

        def rows(ref, it, h, q=None):
            half = it[2] // 2
            if q is None:
                return ref.at[pl.ds(it[1] + h * half, half), :]
            return ref.at[pl.ds(it[1] + h * half + q * (half // 2), half // 2), :]

        def ici(i, p, slot):
            it = items[i]
            px, py, _ = _chip(x, y, p)
            return pltpu.make_async_remote_copy(
                src_ref=rows(srcs[it[0]], it, c), dst_ref=rows(dsts[it[0]].at[slot], it, c),
                send_sem=ici_send.at[i * nrel + p], recv_sem=ici_recv.at[i * nrel + p],
                device_id=(px, py, c), device_id_type=MESH)

        def hop(i, p, slot):
            it = items[i]
            part = rows(dsts[it[0]].at[slot], it, c, p)
            px, py, _ = _chip(x, y, 1 - p)
            return pltpu.make_async_remote_copy(
                src_ref=part, dst_ref=part, send_sem=hop_send.at[i * 2 + p], recv_sem=hop_recv.at[i * 2 + p],
                device_id=(px, py, c), device_id_type=MESH)

        def sib(i, p, h):
            it = items[i]
            part = rows(dsts[it[0]].at[_chip(x, y, p)[2]], it, h)
            return pltpu.make_async_remote_copy(
                src_ref=part, dst_ref=part, send_sem=sib_send.at[i * nrel + p], recv_sem=sib_recv.at[i * nrel + p],
                device_id=(x, y, 1 - c), device_id_type=MESH)

        every = range(len(items))
        diag = _chip(x, y, 2)[2]

        def start():
            _handshake([_chip(x, y, 0)[:2] + (c,), _chip(x, y, 1)[:2] + (c,), (x, y, 1 - c)])
            for i in every:
                for p in (0, 1):
                    ici(i, p, j).start()

        def relay():
            for i in every:
                for p in (0, 1):
                    k = _chip(x, y, p)[2]
                    ici(i, p, k).wait_recv()
                    hop(i, p, k).start()
                    sib(i, p, c).start()

        def relay_on():
            for i in every:
                for p in (0, 1):
                    hop(i, p, diag).wait_recv()
                sib(i, 2, c).start()

        def finish():
            for i in every:
                for p in range(nrel):
                    sib(i, p, 1 - c).wait_recv()
            for i in every:
                for p in (0, 1):
                    ici(i, p, j).wait_send()
                    hop(i, p, _chip(x, y, p)[2]).wait_send()
                for p in range(nrel):
                    sib(i, p, c).wait_send()

        return start, relay, relay_on, finish

    n = len(items)
    sems = [pltpu.SemaphoreType.DMA((n * nrel,))] * 2 + [pltpu.SemaphoreType.DMA((n * 2,))] * 2 \
        + [pltpu.SemaphoreType.DMA((n * nrel,))] * 2
    return _Ride(shards, bufs, sems, plan, NEIGHBOURS_AND_SIBLING)


def exchange_ride(sums, items=None, into=None, rep=None):
    items = [(a, 0, s.shape[1]) for a, s in enumerate(sums)] if items is None else items
    into = into or [None] * len(sums)
    bufs = [jax.ShapeDtypeStruct(s.shape, s.dtype) if b is None else b for s, b in zip(sums, into)]
    srcs = list(sums)
    scratch = [pltpu.SemaphoreType.DMA((max(len(items), 1) * len(_OTHER_CHIPS),))] * 2
    if rep is not None:
        srcs.append(rep)
        bufs.append(jax.ShapeDtypeStruct((N_DEVICES,) + rep.shape, rep.dtype))
        scratch += [pltpu.SemaphoreType.DMA((len(_OTHER_DEVICES),))] * 2
    nrel = len(_OTHER_CHIPS)

    def plan(src_refs, dst_refs, sems):
        x, y, c = _position()
        j = 2 * x + y
        me = 4 * x + 2 * y + c

        def part(i, p, src_slot, dst_slot):
            a, r0, nr = items[i]
            px, py, _ = _chip(x, y, p)
            return pltpu.make_async_remote_copy(
                src_ref=src_refs[a].at[src_slot, pl.ds(r0, nr), :], dst_ref=dst_refs[a].at[dst_slot, pl.ds(r0, nr), :],
                send_sem=sems[0].at[i * nrel + p], recv_sem=sems[1].at[i * nrel + p],
                device_id=(px, py, c), device_id_type=MESH)

        def device(q):
            dx, dy, dc = _OTHER_DEVICES[q]
            return _flip(x, dx), _flip(y, dy), _flip(c, dc)

        def rep_copy(q, slot):
            return pltpu.make_async_remote_copy(
                src_ref=src_refs[-1], dst_ref=dst_refs[-1].at[slot], send_sem=sems[2].at[q], recv_sem=sems[3].at[q],
                device_id=device(q), device_id_type=MESH)

        pairs = [(i, p) for i in range(len(items)) for p in range(nrel)]
        others = range(len(_OTHER_DEVICES)) if rep is not None else ()

        def start():
            if rep is None:
                _handshake([_chip(x, y, p)[:2] + (c,) for p in range(nrel)])
            else:
                _handshake([device(q) for q in others])
            for i, p in pairs:
                part(i, p, _chip(x, y, p)[2], j).start()
            for q in others:
                rep_copy(q, me).start()

        def finish():
            for i, p in pairs:
                k = _chip(x, y, p)[2]
                part(i, p, k, k).wait_recv()
            for q in others:
                px, py, pc = device(q)
                rep_copy(q, 4 * px + 2 * py + pc).wait_recv()
            for i, p in pairs:
                part(i, p, _chip(x, y, p)[2], j).wait_send()
            for q in others:
                rep_copy(q, me).wait_send()

        return start, lambda: None, lambda: None, finish

    return _Ride(srcs, bufs, scratch, plan, OTHER_CHIPS_SAME_CORE if rep is None else ALL_DEVICES)


def _own_slot(buf, own, index):
    return lax.dynamic_update_slice(buf, own[None], (index,) + (0,) * own.ndim)


def _token_tile(s):
    return min(s, 512)


def norm_in(x, g):
    s, d = x.shape
    t = _token_tile(s)

    def body(x_ref, g_ref, o_ref, ot_ref):
        n, _ = _rms_stats(x_ref[...])
        h = n * g_ref[...]
        o_ref[...] = h.astype(BF16)
        ot_ref[...] = h.T.astype(BF16)

    return pl.pallas_call(
        body, name="norm_in", grid=(s // t,),
        in_specs=[pl.BlockSpec((t, d), lambda i: (i, 0)), pl.BlockSpec((1, d), lambda i: (0, 0))],
        out_specs=[pl.BlockSpec((t, d), lambda i: (i, 0)), pl.BlockSpec((d, t), lambda i: (0, i))],
        out_shape=[jax.ShapeDtypeStruct((s, d), BF16), jax.ShapeDtypeStruct((d, s), BF16)],
        compiler_params=_params(),
    )(x, g)


def matmul_cols(a, w4, name, ride=None):
    m, k = a.shape
    nj, _, ns = w4.shape
    nb = ns // CW

    def body(a_ref, w_ref, o_ref):
        o_ref[...] = _dot(a_ref[...], w_ref[0])

    return _call(
        body, name=name, grid=(nj, nb),
        in_specs=[pl.BlockSpec((m, k), lambda j, b: (0, 0)),
                  pl.BlockSpec((1, k, CW), lambda j, b: (j, 0, b))],
        out_specs=[pl.BlockSpec((m, CW), lambda j, b: (0, j * nb + b))],
        out_shape=[jax.ShapeDtypeStruct((m, nj * ns), F32)],
        operands=(a, w4), ride=ride)


def mix_conv_fwd(proj, ws, ride=None):
    s = proj.shape[0]
    nblk = D_MODEL // CW

    def body(cb_ref, cc_ref, cx_ref, ws_ref, q_ref, ya_ref):
        q = _causal_conv(cc_ref[...] * cx_ref[...], ws_ref)
        q_ref[...] = q
        ya_ref[...] = (cb_ref[...] * q).astype(BF16)

    seg = lambda k: pl.BlockSpec((s, CW), lambda c, k=k: (0, k * nblk + c))
    return _call(
        body, name="mix_conv_fwd", grid=(nblk,),
        in_specs=[seg(0), seg(1), seg(2), pl.BlockSpec((3, CW), lambda c: (0, c))],
        out_specs=[pl.BlockSpec((s, CW), lambda c: (0, c))] * 2,
        out_shape=[jax.ShapeDtypeStruct((s, D_MODEL), F32), jax.ShapeDtypeStruct((s, D_MODEL), BF16)],
        operands=(proj, proj, proj, ws), ride=ride)


def _lru_gates(r, ls):
    log_a = LRU_C * r * ls
    a = jnp.exp(log_a)
    mult = jnp.sqrt(_neg_expm1(2.0 * log_a))
    mult = jnp.where(_rows(r.shape) == 0, 1.0, mult)
    return a, mult


def mix_lru_fwd(proj, wl, bl, wa, ba, wx, bx, lam, ride=None):
    s = proj.shape[0]
    nblk = D_MODEL // CW

    def body(lx_ref, ly_ref, wl_ref, bl_ref, wa_ref, ba_ref, wx_ref, bx_ref, lam_ref,
             xl_ref, r_ref, i_ref, h_ref, yb_ref, a_scr, u_scr):
        xl = _causal_conv(lx_ref[...], wl_ref, bl_ref[...])
        xlb = xl.astype(BF16)
        xl_ref[...] = xlb
        r = _sigmoid(_dot(xlb, wa_ref[0]) + ba_ref[...])
        i = _sigmoid(_dot(xlb, wx_ref[0]) + bx_ref[...])
        r_ref[...] = r.astype(BF16)
        i_ref[...] = i.astype(BF16)
        a, mult = _lru_gates(r, _log_sigmoid(lam_ref[...]))
        a_scr[...] = a
        u_scr[...] = mult * i * xl
        _scan_forward(a_scr, u_scr, h_ref)
        yb_ref[...] = (h_ref[...] * _gelu(ly_ref[...])).astype(BF16)

    blk = lambda k: pl.BlockSpec((s, CW), lambda c, k=k: (0, k * nblk + c))
    vec = pl.BlockSpec((1, CW), lambda c: (0, c))
    mat = pl.BlockSpec((1, CW, CW), lambda c: (c, 0, 0))
    out = pl.BlockSpec((s, CW), lambda c: (0, c))
    f = jax.ShapeDtypeStruct((s, D_MODEL), F32)
    hb = jax.ShapeDtypeStruct((s, D_MODEL), BF16)
    return _call(
        body, name="mix_lru_fwd", grid=(nblk,),
        in_specs=[blk(3), blk(4), pl.BlockSpec((4, CW), lambda c: (0, c)), vec, mat, vec, mat, vec, vec],
        out_specs=[out] * 5,
        out_shape=[hb, hb, hb, f, hb],
        scratch_shapes=[pltpu.VMEM((s, CW), F32), pltpu.VMEM((s, CW), F32)],
        operands=(proj, proj, wl, bl, wa, ba, wx, bx, lam), ride=ride)


def branch_merge_fwd(ya, yb, wcb, wlb, proj, ride=None):
    s = ya.shape[0]
    nblk = D_MODEL // CW

    def body(ya_ref, yb_ref, wcb_ref, wlb_ref, gc_ref, gl_ref, a_ref, b_ref, m_ref):
        a = _dot(ya_ref[...], wcb_ref[...])
        b = _dot(yb_ref[...], wlb_ref[...])
        a_ref[...] = a
        b_ref[...] = b
        m_ref[...] = (_sigmoid(gc_ref[...]) * a + _sigmoid(gl_ref[...]) * b).astype(BF16)

    res = pl.BlockSpec((s, D_MODEL), lambda n: (0, 0))
    wcol = pl.BlockSpec((D_MODEL, CW), lambda n: (0, n))
    blk = lambda k: pl.BlockSpec((s, CW), lambda n, k=k: (0, k * nblk + n))
    out = pl.BlockSpec((s, CW), lambda n: (0, n))
    f = jax.ShapeDtypeStruct((s, D_MODEL), F32)
    return _call(
        body, name="branch_merge_fwd", grid=(nblk,),
        in_specs=[res, res, wcol, wcol, blk(5), blk(6)],
        out_specs=[out] * 3,
        out_shape=[f, f, jax.ShapeDtypeStruct((s, D_MODEL), BF16)],
        operands=(ya, yb, wcb, wlb, proj, proj), ride=ride)


def mix_out_fwd(merged, wout, x, g2, g3, ride=None):
    s, d = x.shape
    t = _token_tile(s)

    def body(m_ref, w_ref, x_ref, g2_ref, g3_ref, mix_ref, x2_ref, h2_ref, h2t_ref):
        mix = _dot(m_ref[...], w_ref[...])
        mix_ref[...] = mix
        n, _ = _rms_stats(mix)
        x2 = x_ref[...] + n * g2_ref[...]
        x2_ref[...] = x2
        n2, _ = _rms_stats(x2)
        h2 = n2 * g3_ref[...]
        h2_ref[...] = h2.astype(BF16)
        h2t_ref[...] = h2.T.astype(BF16)

    tile = pl.BlockSpec((t, d), lambda i: (i, 0))
    vec = pl.BlockSpec((1, d), lambda i: (0, 0))
    f = jax.ShapeDtypeStruct((s, d), F32)
    return _call(
        body, name="mix_out_fwd", grid=(s // t,),
        in_specs=[tile, pl.BlockSpec((d, d), lambda i: (0, 0)), tile, vec, vec],
        out_specs=[tile] * 3 + [pl.BlockSpec((d, t), lambda i: (0, i))],
        out_shape=[f, f, jax.ShapeDtypeStruct((s, d), BF16), jax.ShapeDtypeStruct((d, s), BF16)],
        operands=(merged, wout, x, g2, g3), ride=ride)


def ffn_up_act_fwd(h2, wup4, fw, fb, ride=None):
    s, k = h2.shape
    ns = wup4.shape[2]
    per_chip = ns // CW
    nblk = D_FF // CW

    def body(h_ref, wg_ref, wv_ref, cg_ref, cv_ref, bg_ref, bv_ref, up_ref, act_ref, f_ref):
        h = h_ref[...]
        ug = _dot(h, wg_ref[0])
        uv = _dot(h, wv_ref[0])
        up_ref[0] = ug
        up_ref[1] = uv
        gate = _causal_conv(ug, cg_ref, bg_ref[...])
        val = _causal_conv(uv, cv_ref, bv_ref[...])
        act_ref[0] = gate.astype(BF16)
        act_ref[1] = val.astype(BF16)
        f_ref[...] = (_gelu(gate) * val).astype(BF16)

    wcols = lambda h: pl.BlockSpec((1, k, CW), lambda n, h=h: (n // per_chip + 2 * h, 0, n % per_chip))
    half = lambda h, rows: pl.BlockSpec((rows, CW), lambda n, h=h: (0, h * nblk + n))
    both = pl.BlockSpec((2, s, CW), lambda n: (0, 0, n))
    return _call(
        body, name="ffn_up_act_fwd", grid=(nblk,),
        in_specs=[pl.BlockSpec((s, k), lambda n: (0, 0)), wcols(0), wcols(1),
                  half(0, 3), half(1, 3), half(0, 1), half(1, 1)],
        out_specs=[both, both, pl.BlockSpec((s, CW), lambda n: (0, n))],
        out_shape=[jax.ShapeDtypeStruct((2, s, D_FF), F32), jax.ShapeDtypeStruct((2, s, D_FF), BF16),
                   jax.ShapeDtypeStruct((s, D_FF), BF16)],
        operands=(h2, wup4, wup4, fw, fw, fb, fb), ride=ride)


def ffn_down_loss(f, wdown, x2, target, g4):
    s, d = x2.shape
    t = _token_tile(s)

    def body(f_ref, w_ref, x2_ref, tg_ref, g4_ref, dy_ref, dout_ref, loss_ref, dg4_ref):
        @pl.when(pl.program_id(0) == 0)
        def _():
            loss_ref[...] = jnp.zeros_like(loss_ref)
            dg4_ref[...] = jnp.zeros_like(dg4_ref)

        out = _dot(f_ref[...], w_ref[...])
        n, r = _rms_stats(out)
        err = x2_ref[...] + n * g4_ref[...] - tg_ref[...]
        loss_ref[...] += jnp.full(loss_ref.shape, (0.5 / d) * jnp.sum(err * err), F32)
        dy = err * (1.0 / d)
        dy_ref[...] = dy
        dout, dg = _rms_bwd(n, r, g4_ref[...], dy)
        dout_ref[...] = dout.astype(BF16)
        dg4_ref[...] += jnp.sum(dg, axis=0, keepdims=True)

    tile = pl.BlockSpec((t, d), lambda i: (i, 0))
    vec = pl.BlockSpec((1, d), lambda i: (0, 0))
    return pl.pallas_call(
        body, name="ffn_down_loss", grid=(s // t,),
        in_specs=[pl.BlockSpec((t, D_FF), lambda i: (i, 0)), pl.BlockSpec((D_FF, d), lambda i: (0, 0)), tile, tile, vec],
        out_specs=[tile, tile, pl.BlockSpec((1, 128), lambda i: (0, 0)), vec],
        out_shape=[jax.ShapeDtypeStruct((s, d), F32), jax.ShapeDtypeStruct((s, d), BF16),
                   jax.ShapeDtypeStruct((1, 128), F32), jax.ShapeDtypeStruct((1, d), F32)],
        compiler_params=_params(),
    )(f, wdown, x2, target, g4)


def ffn_up_bwd(dout, wdown, up, act, f, fw, wup4, h2t, ride=None):
    k, s = h2t.shape
    nblk = D_FF // FW
    per_chip = wup4.shape[2] // FW

    def body(do_ref, wd_ref, up_ref, act_ref, f_ref, cg_ref, cv_ref, wg_ref, wv_ref, h_ref,
             dh_ref, dwu_ref, dwd_ref, dw_ref, db_ref, dup_scr):
        @pl.when(pl.program_id(0) == 0)
        def _():
            dup_scr[...] = jnp.zeros_like(dup_scr)
            dh_ref[...] = jnp.zeros_like(dh_ref)

        do = do_ref[...]
        df = _dot_nt(do, wd_ref[...])
        dg = dup_scr[0]
        dv = dup_scr[1]
        ht = h_ref[...]
        dh_ref[...] += _dot_nt(dg, wg_ref[0]) + _dot_nt(dv, wv_ref[0])
        dwu_ref[0] = _dot(ht, dg).astype(BF16)
        dwu_ref[1] = _dot(ht, dv).astype(BF16)
        dwd_ref[...] = _dot_tn(f_ref[...], do).astype(BF16)
        val = act_ref[1].astype(F32)
        ge, dge = _gelu_and_grad(act_ref[0].astype(F32))
        dgate = _advances(df * val * dge, 3)
        dval = _advances(df * ge, 3)
        dw_ref[0] = _conv_wgrad(dgate, up_ref[0])
        dw_ref[1] = _conv_wgrad(dval, up_ref[1])
        db_ref[0] = jnp.sum(dgate[0], axis=0, keepdims=True)
        db_ref[1] = jnp.sum(dval[0], axis=0, keepdims=True)
        dup_scr[0] = _taps_sum(dgate, cg_ref).astype(BF16)
        dup_scr[1] = _taps_sum(dval, cv_ref).astype(BF16)

    cur = lambda n: jnp.minimum(n, nblk - 1)
    prev = lambda n: jnp.maximum(n - 1, 0)
    once = pl.Buffered(1)
    both = lambda rows: pl.BlockSpec((2, rows, FW), lambda n: (0, 0, cur(n)))
    taps = lambda h: pl.BlockSpec((3, FW), lambda n, h=h: (0, h * nblk + cur(n)))
    wcols = lambda h: pl.BlockSpec((1, k, FW), lambda n, h=h: (prev(n) // per_chip + 2 * h, 0, prev(n) % per_chip))
    return _call(
        body, name="ffn_up_bwd", grid=(nblk + 1,),
        in_specs=[pl.BlockSpec((s, D_MODEL), lambda n: (0, 0), pipeline_mode=once),
                  pl.BlockSpec((FW, D_MODEL), lambda n: (cur(n), 0)), both(s), both(s),
                  pl.BlockSpec((s, FW), lambda n: (0, cur(n))), taps(0), taps(1), wcols(0), wcols(1),
                  pl.BlockSpec((k, s), lambda n: (0, 0), pipeline_mode=once)],
        out_specs=[pl.BlockSpec((s, k), lambda n: (0, 0), pipeline_mode=once),
                   pl.BlockSpec((2, k, FW), lambda n: (0, 0, prev(n))),
                   pl.BlockSpec((FW, D_MODEL), lambda n: (cur(n), 0)), both(3), both(1)],
        out_shape=[jax.ShapeDtypeStruct((s, k), F32), jax.ShapeDtypeStruct((2, k, D_FF), BF16),
                   jax.ShapeDtypeStruct((D_FF, D_MODEL), BF16),
                   jax.ShapeDtypeStruct((2, 3, D_FF), F32), jax.ShapeDtypeStruct((2, 1, D_FF), F32)],
        scratch_shapes=[pltpu.VMEM((2, s, FW), BF16)],
        operands=(dout, wdown, up, act, f, fw, fw, wup4, wup4, h2t), ride=ride)


def matmul_cols_bwd(dy, other, name, wgrad, ride=None):
    m = dy[0].shape[1]
    if wgrad:
        k = other.shape[0]
        nj, nb = N_CHIPS, sum(d.shape[0] * d.shape[2] for d in dy) // (N_CHIPS * CW)
    else:
        nj, k, ns = other.shape
        nb = ns // CW
    per_seg = dy[0].shape[2] // CW
    first = [sum(d.shape[0] for d in dy[:i]) for i in range(len(dy))]

    def segment(j, b):
        return (j * nb + b) // per_seg, (j * nb + b) % per_seg

    def body(*refs):
        dy_refs, (o_ref, r_ref) = refs[:len(dy)], refs[len(dy):]
        seg, _ = segment(pl.program_id(0), pl.program_id(1))
        dyb = dy_refs[-1][0]
        for i in range(len(dy) - 2, -1, -1):
            dyb = jnp.where(seg < first[i + 1], dy_refs[i][0], dyb)
        if wgrad:
            r_ref[...] = _dot(o_ref[...], dyb).astype(BF16)
        else:
            @pl.when((pl.program_id(0) == 0) & (pl.program_id(1) == 0))
            def _():
                r_ref[...] = jnp.zeros_like(r_ref)

            r_ref[...] += _dot_nt(dyb, o_ref[0])

    def dy_spec(i):
        nseg = dy[i].shape[0]

        def index(j, b):
            seg, col = segment(j, b)
            local = seg - first[i]
            return (jnp.clip(local, 0, nseg - 1), 0,
                    jnp.where(local < 0, 0, jnp.where(local >= nseg, per_seg - 1, col)))

        return pl.BlockSpec((1, m, CW), index)

    if wgrad:
        other_spec = pl.BlockSpec((k, m), lambda j, b: (0, 0))
        out_spec = pl.BlockSpec((k, CW), lambda j, b: (0, j * nb + b))
        out_shape = jax.ShapeDtypeStruct((k, nj * nb * CW), BF16)
    else:
        other_spec = pl.BlockSpec((1, k, CW), lambda j, b: (j, 0, b))
        out_spec = pl.BlockSpec((m, k), lambda j, b: (0, 0))
        out_shape = jax.ShapeDtypeStruct((m, k), F32)
    return _call(
        body, name=name, grid=(nj, nb), in_specs=[dy_spec(i) for i in range(len(dy))] + [other_spec],
        out_specs=[out_spec], out_shape=[out_shape], operands=(*dy, other), ride=ride)


def norms_mid_bwd(dh2, x2, dy, mix, g3, g2, ride=None):
    s, d = x2.shape
    t = _token_tile(s)

    def body(dh2_ref, x2_ref, dy_ref, mix_ref, g3_ref, g2_ref, dx2_ref, dmix_ref, dg3_ref, dg2_ref):
        @pl.when(pl.program_id(0) == 0)
        def _():
            dg3_ref[...] = jnp.zeros_like(dg3_ref)
            dg2_ref[...] = jnp.zeros_like(dg2_ref)

        n3, r3 = _rms_stats(x2_ref[...])
        dx, dg3 = _rms_bwd(n3, r3, g3_ref[...], dh2_ref[...])
        dx2 = dy_ref[...] + dx
        dx2_ref[...] = dx2
        dg3_ref[...] += jnp.sum(dg3, axis=0, keepdims=True)
        n2, r2 = _rms_stats(mix_ref[...])
        dmix, dg2 = _rms_bwd(n2, r2, g2_ref[...], dx2)
        dmix_ref[...] = dmix.astype(BF16)
        dg2_ref[...] += jnp.sum(dg2, axis=0, keepdims=True)

    tile = pl.BlockSpec((t, d), lambda i: (i, 0))
    vec = pl.BlockSpec((1, d), lambda i: (0, 0))
    v = jax.ShapeDtypeStruct((1, d), F32)
    return _call(
        body, name="norms_mid_bwd", grid=(s // t,),
        in_specs=[tile, tile, tile, tile, vec, vec],
        out_specs=[tile, tile, vec, vec],
        out_shape=[jax.ShapeDtypeStruct((s, d), F32), jax.ShapeDtypeStruct((s, d), BF16), v, v],
        operands=(dh2, x2, dy, mix, g3, g2), ride=ride)


def mix_out_bwd(dmix, wout, merged, a, b, proj, ride=None):
    s = dmix.shape[0]
    nblk = D_MODEL // CW

    def body(dm_ref, w_ref, mg_ref, a_ref, b_ref, gc_ref, gl_ref, da_ref, db_ref, dw_ref, dg_ref):
        dm = dm_ref[...]
        dmerged = _dot_nt(dm, w_ref[...])
        dw_ref[...] = _dot_tn(mg_ref[...], dm).astype(BF16)
        sc = _sigmoid(gc_ref[...])
        sl = _sigmoid(gl_ref[...])
        da_ref[...] = (dmerged * sc).astype(BF16)
        db_ref[...] = (dmerged * sl).astype(BF16)
        dg_ref[0] = (dmerged * a_ref[...] * sc * (1.0 - sc)).astype(BF16)
        dg_ref[1] = (dmerged * b_ref[...] * sl * (1.0 - sl)).astype(BF16)

    res = pl.BlockSpec((s, D_MODEL), lambda n: (0, 0))
    rows = pl.BlockSpec((CW, D_MODEL), lambda n: (n, 0))
    col = pl.BlockSpec((s, CW), lambda n: (0, n))
    blk = lambda k: pl.BlockSpec((s, CW), lambda n, k=k: (0, k * nblk + n))
    hb = jax.ShapeDtypeStruct((s, D_MODEL), BF16)
    return _call(
        body, name="mix_out_bwd", grid=(nblk,),
        in_specs=[res, rows, col, col, col, blk(5), blk(6)],
        out_specs=[col, col, rows, pl.BlockSpec((2, s, CW), lambda n: (0, 0, n))],
        out_shape=[hb, hb, jax.ShapeDtypeStruct((D_MODEL, D_MODEL), BF16), jax.ShapeDtypeStruct((2, s, D_MODEL), BF16)],
        operands=(dmix, wout, merged, a, b, proj, proj), ride=ride)


def mix_conv_bwd(da, wcb, proj, q, ws, ride=None):
    s = da.shape[0]
    nblk = D_MODEL // CW

    def body(da_ref, w_ref, cb_ref, cc_ref, cx_ref, q_ref, ws_ref, dc_ref, dw_ref, dws_ref):
        dab = da_ref[...]
        dya = _dot_nt(dab, w_ref[...])
        cb = cb_ref[...]
        cc = cc_ref[...]
        cx = cx_ref[...]
        q = q_ref[...]
        dw_ref[...] = _dot_tn((cb * q).astype(BF16), dab).astype(BF16)
        dc_ref[0] = (dya * q).astype(BF16)
        dq = _advances(dya * cb, 3)
        dp = _taps_sum(dq, ws_ref)
        dws_ref[...] = _conv_wgrad(dq, cc * cx)
        dc_ref[1] = (dp * cx).astype(BF16)
        dc_ref[2] = (dp * cc).astype(BF16)

    res = pl.BlockSpec((s, D_MODEL), lambda n: (0, 0))
    rows = pl.BlockSpec((CW, D_MODEL), lambda n: (n, 0))
    col = pl.BlockSpec((s, CW), lambda n: (0, n))
    blk = lambda k: pl.BlockSpec((s, CW), lambda n, k=k: (0, k * nblk + n))
    taps = pl.BlockSpec((3, CW), lambda n: (0, n))
    hb = jax.ShapeDtypeStruct((s, D_MODEL), BF16)
    return _call(
        body, name="mix_conv_bwd", grid=(nblk,),
        in_specs=[res, rows, blk(0), blk(1), blk(2), col, taps],
        out_specs=[pl.BlockSpec((3, s, CW), lambda n: (0, 0, n)), rows, taps],
        out_shape=[jax.ShapeDtypeStruct((3, s, D_MODEL), BF16), jax.ShapeDtypeStruct((D_MODEL, D_MODEL), BF16),
                   jax.ShapeDtypeStruct((3, D_MODEL), F32)],
        operands=(da, wcb, proj, proj, proj, q, ws), ride=ride)


def mix_lru_bwd(db, wlb, proj, xl, r, i, h, wl, wa, wx, lam, ride=None):
    s = db.shape[0]
    nblk = D_MODEL // CW

    def body(db_ref, w_ref, lx_ref, ly_ref, xl_ref, r_ref, i_ref, h_ref, wl_ref, wa_ref, wx_ref, lam_ref,
             dl_ref, dw_ref, dwa_ref, dwx_ref, dba_ref, dbx_ref, dwl_ref, dbl_ref, dlam_ref,
             c_scr, g_scr):
        dbb = db_ref[...]
        dyb = _dot_nt(dbb, w_ref[...])
        h = h_ref[...]
        ge, dge = _gelu_and_grad(ly_ref[...])
        dw_ref[...] = _dot_tn((h * ge).astype(BF16), dbb).astype(BF16)
        dl_ref[1] = (dyb * h * dge).astype(BF16)
        r = r_ref[...].astype(F32)
        gi = i_ref[...].astype(F32)
        xlb = xl_ref[...]
        xl = xlb.astype(F32)
        lam = lam_ref[...]
        ls = _log_sigmoid(lam)
        a, mult = _lru_gates(r, ls)
        c_scr[...] = _shift_up(a, 1)
        g_scr[...] = dyb * ge
        _scan_backward(c_scr, g_scr, g_scr)
        du = g_scr[...]
        da = du * _shift_down(h, 1)
        dmult = du * gi * xl
        di = du * mult * xl
        dxl = du * mult * gi
        first = _rows(a.shape) == 0
        dlog_a = da * a - jnp.where(first, 0.0, dmult * a * a / mult)
        dr = dlog_a * (LRU_C * ls)
        dlam_ref[...] = jnp.sum(dlog_a * r, axis=0, keepdims=True) * (LRU_C * (1.0 - _sigmoid(lam)))
        dzr = dr * r * (1.0 - r)
        dzi = di * gi * (1.0 - gi)
        dba_ref[...] = jnp.sum(dzr, axis=0, keepdims=True)
        dbx_ref[...] = jnp.sum(dzi, axis=0, keepdims=True)
        dzrb = dzr.astype(BF16)
        dzib = dzi.astype(BF16)
        dwa_ref[0] = _dot_tn(xlb, dzrb)
        dwx_ref[0] = _dot_tn(xlb, dzib)
        dxl = _advances(dxl + _dot_nt(dzrb, wa_ref[0]) + _dot_nt(dzib, wx_ref[0]), 4)
        dl_ref[0] = _taps_sum(dxl, wl_ref).astype(BF16)
        dwl_ref[...] = _conv_wgrad(dxl, lx_ref[...])
        dbl_ref[...] = jnp.sum(dxl[0], axis=0, keepdims=True)

    res = pl.BlockSpec((s, D_MODEL), lambda n: (0, 0))
    rows = pl.BlockSpec((CW, D_MODEL), lambda n: (n, 0))
    col = pl.BlockSpec((s, CW), lambda n: (0, n))
    blk = lambda k: pl.BlockSpec((s, CW), lambda n, k=k: (0, k * nblk + n))
    taps = pl.BlockSpec((4, CW), lambda n: (0, n))
    vec = pl.BlockSpec((1, CW), lambda n: (0, n))
    mat = pl.BlockSpec((1, CW, CW), lambda n: (n, 0, 0))
    hb = jax.ShapeDtypeStruct((s, D_MODEL), BF16)
    v = jax.ShapeDtypeStruct((1, D_MODEL), F32)
    m = jax.ShapeDtypeStruct((LRU_HEADS, HEAD_DIM, HEAD_DIM), F32)
    scr = pltpu.VMEM((s, CW), F32)
    return _call(
        body, name="mix_lru_bwd", grid=(nblk,),
        in_specs=[res, rows, blk(3), blk(4), col, col, col, col, taps, mat, mat, vec],
        out_specs=[pl.BlockSpec((2, s, CW), lambda n: (0, 0, n)), rows, mat, mat, vec, vec, taps, vec, vec],
        out_shape=[jax.ShapeDtypeStruct((2, s, D_MODEL), BF16), jax.ShapeDtypeStruct((D_MODEL, D_MODEL), BF16), m, m, v, v,
                   jax.ShapeDtypeStruct((4, D_MODEL), F32), v, v],
        scratch_shapes=[scr, scr],
        operands=(db, wlb, proj, proj, xl, r, i, h, wl, wa, wx, lam), ride=ride)


def norm_in_bwd(dh1, x, dx2, g1, ride=None):
    s, d = x.shape
    t = _token_tile(s)

    def body(dh_ref, x_ref, dx2_ref, g_ref, dx_ref, dg_ref):
        @pl.when(pl.program_id(0) == 0)
        def _():
            dg_ref[...] = jnp.zeros_like(dg_ref)

        n, r = _rms_stats(x_ref[...])
        dx, dg = _rms_bwd(n, r, g_ref[...], dh_ref[...])
        dx_ref[...] = dx2_ref[...] + dx
        dg_ref[...] += jnp.sum(dg, axis=0, keepdims=True)

    tile = pl.BlockSpec((t, d), lambda i: (i, 0))
    vec = pl.BlockSpec((1, d), lambda i: (0, 0))
    return _call(
        body, name="norm_in_bwd", grid=(s // t,),
        in_specs=[tile, tile, tile, vec],
        out_specs=[tile, vec],
        out_shape=[jax.ShapeDtypeStruct((s, d), F32), jax.ShapeDtypeStruct((1, d), F32)],
        operands=(dh1, x, dx2, g1), ride=ride)


def _owned_part(ref, kind, k, h, hr):
    if kind == "col":
        ns = ref.shape[1] // N_CHIPS
        return ref.at[pl.ds(h * hr, hr), pl.ds(k * ns, ns)]
    if kind == "row":
        return ref.at[pl.ds(k * 2 * hr + h * hr, hr), :]
    if kind == "col2":
        ns = ref.shape[2] // 2
        return ref.at[k // 2, pl.ds(h * hr, hr), pl.ds((k % 2) * ns, ns)]
    return ref.at[k, pl.ds(h * hr, hr), :]


def _part_shape(g, kind):
    if kind == "col2":
        return g.shape[1] // 2, g.shape[2] // 2
    if kind == "col":
        return g.shape[0] // 2, g.shape[1] // N_CHIPS
    if kind == "row":
        return g.shape[0] // (2 * N_CHIPS), g.shape[1]
    return g.shape[1] // 2, g.shape[2]


def pair_split(grads, kinds, name):
    n = len(grads)
    shapes = [_part_shape(g, k) for g, k in zip(grads, kinds)]

    def body(*refs):
        ins, theirs = refs[:n], refs[n:2 * n]
        send_sem, recv_sem = refs[2 * n:]
        x, y, c = _position()
        copies = []
        for a in range(n):
            hr = shapes[a][0]
            for k in range(N_CHIPS):
                s = a * N_CHIPS + k
                copies.append(pltpu.make_async_remote_copy(
                    src_ref=_owned_part(ins[a], kinds[a], k, 1 - c, hr), dst_ref=theirs[a].at[k],
                    send_sem=send_sem.at[s], recv_sem=recv_sem.at[s], device_id=(x, y, 1 - c), device_id_type=MESH))
        _handshake([(x, y, 1 - c)])
        for cp in copies:
            cp.start()
        for cp in copies:
            cp.wait()

    return pl.pallas_call(
        body, name=name,
        in_specs=[_HBM] * n, out_specs=[_HBM] * n,
        out_shape=[jax.ShapeDtypeStruct((N_CHIPS,) + shp, g.dtype) for shp, g in zip(shapes, grads)],
        scratch_shapes=[pltpu.SemaphoreType.DMA((n * N_CHIPS,))] * 2,
        compiler_params=pltpu.CompilerParams(collective_id=SIBLING),
    )(*grads)


def pair_swap(halves):
    n = len(halves)

    def body(*refs):
        ins, outs = refs[:n], refs[n:2 * n]
        send_sem, recv_sem = refs[2 * n:]
        x, y, c = _position()
        copies = [pltpu.make_async_remote_copy(
            src_ref=ins[a], dst_ref=outs[a], send_sem=send_sem.at[a], recv_sem=recv_sem.at[a],
            device_id=(x, y, 1 - c), device_id_type=MESH) for a in range(n)]
        _handshake([(x, y, 1 - c)])
        for cp in copies:
            cp.start()
        for cp in copies:
            cp.wait()

    return pl.pallas_call(
        body, name="pair_swap",
        in_specs=[_HBM] * n, out_specs=[_HBM] * n,
        out_shape=[jax.ShapeDtypeStruct(h.shape, h.dtype) for h in halves],
        scratch_shapes=[pltpu.SemaphoreType.DMA((n,))] * 2,
        compiler_params=pltpu.CompilerParams(collective_id=SIBLING),
    )(*halves)


def _row_tile(rows, cols, limit_bytes=1 << 20):
    best = None
    for t in range(SUBLANES, rows + 1, SUBLANES):
        if rows % t == 0 and t * cols * 4 <= limit_bytes:
            best = t
    return best or rows


def add_pair(g, kind, theirs, core, name):
    nc, rows, cols = theirs.shape
    t = _row_tile(rows, cols, 4 << 20)
    nt = rows // t

    def body(core_ref, g_ref, b_ref, o_ref):
        mine = g_ref[...].reshape(t, cols)
        o_ref[0] = (mine.astype(F32) + b_ref[0].astype(F32)).astype(o_ref.dtype)

    if kind == "col":
        own = pl.BlockSpec((t, cols), lambda k, i, c: (c[0] * nt + i, k))
    elif kind == "col2":
        own = pl.BlockSpec((1, t, cols), lambda k, i, c: (k // 2, c[0] * nt + i, k % 2))
    elif kind == "row":
        own = pl.BlockSpec((t, cols), lambda k, i, c: ((2 * k + c[0]) * nt + i, 0))
    else:
        own = pl.BlockSpec((1, t, cols), lambda k, i, c: (k, c[0] * nt + i, 0))
    spec = pl.BlockSpec((1, t, cols), lambda k, i, c: (k, i, 0))
    return pl.pallas_call(
        body, name=name,
        grid_spec=pltpu.PrefetchScalarGridSpec(num_scalar_prefetch=1, grid=(nc, nt), in_specs=[own, spec], out_specs=spec),
        out_shape=jax.ShapeDtypeStruct(theirs.shape, theirs.dtype), compiler_params=_params(),
    )(core, g, theirs)


def sum_lead(a, name):
    nl, rows, cols = a.shape
    t = _row_tile(rows, cols, (1 << 20) // 2)

    def body(a_ref, o_ref):
        acc = a_ref[0].astype(F32)
        for s in range(1, nl):
            acc = acc + a_ref[s].astype(F32)
        o_ref[...] = acc

    return pl.pallas_call(
        body, name=name, grid=(rows // t,),
        in_specs=[pl.BlockSpec((nl, t, cols), lambda i: (0, i, 0))],
        out_specs=pl.BlockSpec((t, cols), lambda i: (i, 0)),
        out_shape=jax.ShapeDtypeStruct((rows, cols), F32), compiler_params=_params(),
    )(a)


def sum_chips(rx, csum, chip, name):
    nc, rows, cols = rx.shape
    t = _row_tile(rows, cols, 2 << 20)

    def body(chip_ref, r0, r1, r2, r3, own_ref, o_ref):
        acc = None
        for s, ref in enumerate((r0, r1, r2, r3)):
            term = jnp.where(chip_ref[0] == s, own_ref[0], ref[0]).astype(F32)
            acc = term if acc is None else acc + term
        o_ref[...] = acc

    def slot(s):
        return pl.BlockSpec((1, t, cols), lambda i, c, s=s: (jnp.where(c[0] == s, c[0] ^ 1, s), i, 0))

    return pl.pallas_call(
        body, name=name,
        grid_spec=pltpu.PrefetchScalarGridSpec(
            num_scalar_prefetch=1, grid=(rows // t,),
            in_specs=[slot(s) for s in range(nc)] + [pl.BlockSpec((1, t, cols), lambda i, c: (c[0], i, 0))],
            out_specs=pl.BlockSpec((t, cols), lambda i, c: (i, 0))),
        out_shape=jax.ShapeDtypeStruct((rows, cols), F32), compiler_params=_params(),
    )(chip, rx, rx, rx, rx, csum)


def cast_bf16(a, name):
    rows, cols = a.shape
    t = _row_tile(rows, cols, 2 << 20)

    def body(i_ref, o_ref):
        o_ref[...] = i_ref[...].astype(BF16)

    spec = pl.BlockSpec((t, cols), lambda i: (i, 0))
    return pl.pallas_call(body, name=name, grid=(rows // t,), in_specs=[spec], out_specs=spec,
                          out_shape=jax.ShapeDtypeStruct((rows, cols), BF16), compiler_params=_params())(a)


def _adamw_update(w, g, m, v):
    nm = ADAM_B1 * m + (1.0 - ADAM_B1) * g
    nv = ADAM_B2 * v + (1.0 - ADAM_B2) * (g * g)
    m_hat = nm * (1.0 / (1.0 - ADAM_B1 ** ADAM_STEP))
    v_hat = nv * (1.0 / (1.0 - ADAM_B2 ** ADAM_STEP))
    return -ADAM_LR * (m_hat / (jnp.sqrt(v_hat) + ADAM_EPS) + ADAM_WD * w), nm, nv


def adamw(w, g, m, v, name):
    rows, cols = w.shape
    t = _row_tile(rows, cols)

    def body(w_ref, g_ref, m_ref, v_ref, d_ref, nm_ref, nv_ref):
        d_ref[...], nm_ref[...], nv_ref[...] = _adamw_update(w_ref[...], g_ref[...], m_ref[...], v_ref[...])

    spec = pl.BlockSpec((t, cols), lambda i: (i, 0))
    shp = jax.ShapeDtypeStruct((rows, cols), F32)
    return pl.pallas_call(
        body, name=name, grid=(rows // t,), in_specs=[spec] * 4, out_specs=[spec] * 3,
        out_shape=[shp, shp, shp], compiler_params=_params(),
    )(w, g, m, v)


def adamw_halves(w, g_mine, g_other, m, v, core, name):
    rows, cols = w.shape
    hr = rows // 2
    t = _row_tile(hr, cols)
    nt = hr // t

    def body(core_ref, w_ref, gm_ref, go_ref, m_ref, v_ref, g_ref, d_ref, nm_ref, nv_ref):
        g = jnp.where(pl.program_id(0) // nt == core_ref[0], gm_ref[...], go_ref[...])
        g_ref[...] = g
        d_ref[...], nm_ref[...], nv_ref[...] = _adamw_update(w_ref[...], g, m_ref[...], v_ref[...])

    spec = pl.BlockSpec((t, cols), lambda i, c: (i, 0))
    half = pl.BlockSpec((t, cols), lambda i, c: (i % nt, 0))
    shp = jax.ShapeDtypeStruct((rows, cols), F32)
    return pl.pallas_call(
        body, name=name,
        grid_spec=pltpu.PrefetchScalarGridSpec(num_scalar_prefetch=1, grid=(2 * nt,),
                                               in_specs=[spec, half, half, spec, spec], out_specs=[spec] * 4),
        out_shape=[shp] * 4, compiler_params=_params(),
    )(core, w, g_mine, g_other, m, v)


WEIGHTS = ("norm_mix_pre", "norm_mix_post", "norm_ffn_pre", "norm_ffn_post", "w_in", "conv_short_w",
           "w_conv_branch", "lru_conv_w", "lru_conv_b", "lru_wa", "lru_ba", "lru_wx", "lru_bx", "lru_lambda",
           "w_lru_branch", "w_out", "ffn_w_up", "ffn_conv_w", "ffn_conv_b", "ffn_w_down")
BIG = ("w_in", "ffn_w_up", "w_conv_branch", "w_lru_branch", "w_out", "ffn_w_down")
BIG_KIND = ("col", "col", "row", "row", "row", "row")
SMALL = ("conv_short_w", "lru_conv_w", "lru_wa", "lru_ba", "lru_wx", "lru_bx", "ffn_conv_w")
REPL = ("norm_mix_pre", "norm_mix_post", "norm_ffn_pre", "norm_ffn_post", "lru_conv_b", "lru_lambda", "ffn_conv_b")
PACK_W = 256
SMALL_ROWS = 576
REPL_ROWS = 16
LOSS_ROW = 12
FFN_SHARD = 2 * D_FF // N_CHIPS
QUARTER = HEAD_DIM // N_CHIPS
SMALL_PARTS = (("conv_short_w", 3, (1, 3, PACK_W)), ("lru_conv_w", 4, (1, 4, PACK_W)),
               ("lru_wa", LRU_HEADS * QUARTER, (1, LRU_HEADS, QUARTER, HEAD_DIM)), ("lru_ba", LRU_HEADS, (1, LRU_HEADS, QUARTER)),
               ("lru_wx", LRU_HEADS * QUARTER, (1, LRU_HEADS, QUARTER, HEAD_DIM)), ("lru_bx", LRU_HEADS, (1, LRU_HEADS, QUARTER)),
               ("ffn_conv_w", 3 * FFN_SHARD // PACK_W, (1, 3, FFN_SHARD)))


def _pad8(nr):
    return -(-nr // SUBLANES) * SUBLANES


SMALL_OFFSET = {}
for _name, _nr, _ in SMALL_PARTS:
    SMALL_OFFSET[_name] = sum(_pad8(nr) for n, nr, _ in SMALL_PARTS[:len(SMALL_OFFSET)])
FFN_ROWS = FFN_SHARD // PACK_W
BIASES = ("lru_ba", "lru_bx")
TAPS3 = ("conv_short_w", "ffn_conv_w")


def pack_small(dicts):
    names = [n for n, _, _ in SMALL_PARTS]
    operands = [d[n].transpose(1, 0, 2) if n in TAPS3 else d[n] for d in dicts for n in names]

    def body(*refs):
        ins, outs = refs[:len(operands)], refs[len(operands):]
        for i, o in enumerate(outs):
            o[...] = jnp.zeros_like(o)
            for (name, nr, shape), p in zip(SMALL_PARTS, ins[i * len(names):(i + 1) * len(names)]):
                r0 = SMALL_OFFSET[name]
                if name in BIASES:
                    o[r0:r0 + nr, 0:QUARTER] = p[0]
                elif name == "ffn_conv_w":
                    for k in range(shape[1]):
                        for s in range(FFN_ROWS):
                            o[r0 + FFN_ROWS * k + s:r0 + FFN_ROWS * k + s + 1, :] = p[k, :, s * PACK_W:(s + 1) * PACK_W]
                elif name == "conv_short_w":
                    for k in range(nr):
                        o[r0 + k:r0 + k + 1, :] = p[k]
                else:
                    o[r0:r0 + nr, :] = p[0].reshape(nr, PACK_W)

    shape = jax.ShapeDtypeStruct((SMALL_ROWS, PACK_W), F32)
    return pl.pallas_call(body, name="pack_small", out_shape=[shape] * len(dicts), compiler_params=_params())(*operands)


def full_small(g4):
    def body(p, csw, lcw, wa, wx, fcw):
        chips = range(N_CHIPS)
        r0 = SMALL_OFFSET["conv_short_w"]
        csw[...] = jnp.concatenate([p[c, r0:r0 + 3, :] for c in chips], axis=1)
        r0 = SMALL_OFFSET["lru_conv_w"]
        lcw[...] = jnp.concatenate([p[c, r0:r0 + 4, :] for c in chips], axis=1)
        for name, o in (("lru_wa", wa), ("lru_wx", wx)):
            r0 = SMALL_OFFSET[name]
            for h in range(LRU_HEADS):
                for c in chips:
                    o[h, c * QUARTER:(c + 1) * QUARTER, :] = p[c, r0 + h * QUARTER:r0 + (h + 1) * QUARTER, :].astype(BF16)
        r0 = SMALL_OFFSET["ffn_conv_w"]
        for k in range(3):
            fcw[k:k + 1, :] = jnp.concatenate(
                [p[c, r0 + FFN_ROWS * k + s:r0 + FFN_ROWS * k + s + 1, :] for c in chips for s in range(FFN_ROWS)], axis=1)

    mat = jax.ShapeDtypeStruct((LRU_HEADS, HEAD_DIM, HEAD_DIM), BF16)
    csw, lcw, wa, wx, fcw = pl.pallas_call(
        body, name="full_small",
        out_shape=[jax.ShapeDtypeStruct((3, D_MODEL), F32), jax.ShapeDtypeStruct((4, D_MODEL), F32), mat, mat,
                   jax.ShapeDtypeStruct((3, 2 * D_FF), F32)],
        compiler_params=_params())(g4)

    def bias(name):
        r0 = SMALL_OFFSET[name]
        return g4[:, r0:r0 + LRU_HEADS, :QUARTER].transpose(1, 0, 2).reshape(1, D_MODEL)

    return dict(conv_short_w=csw, lru_conv_w=lcw, lru_wa=wa, lru_wx=wx, ffn_conv_w=fcw,
                lru_ba=bias("lru_ba"), lru_bx=bias("lru_bx"))


def split_small(full):
    def bias(name):
        return full[name].reshape(LRU_HEADS, N_CHIPS, QUARTER).transpose(1, 0, 2)

    def body(csw, lcw, wa, wx, fcw, ba, bx, o):
        o[...] = jnp.zeros_like(o)
        for c in range(N_CHIPS):
            cols = slice(c * PACK_W, (c + 1) * PACK_W)
            r0 = SMALL_OFFSET["conv_short_w"]
            o[c, r0:r0 + 3, :] = csw[:, cols]
            r0 = SMALL_OFFSET["lru_conv_w"]
            o[c, r0:r0 + 4, :] = lcw[:, cols]
            for name, p in (("lru_wa", wa), ("lru_wx", wx)):
                r0 = SMALL_OFFSET[name]
                for h in range(LRU_HEADS):
                    o[c, r0 + h * QUARTER:r0 + (h + 1) * QUARTER, :] = p[h, c * QUARTER:(c + 1) * QUARTER, :]
            for name, p in (("lru_ba", ba), ("lru_bx", bx)):
                r0 = SMALL_OFFSET[name]
                o[c, r0:r0 + LRU_HEADS, 0:QUARTER] = p[c]
            r0 = SMALL_OFFSET["ffn_conv_w"]
            for k in range(3):
                for s in range(FFN_ROWS):
                    lo = c * FFN_SHARD + s * PACK_W
                    o[c, r0 + FFN_ROWS * k + s:r0 + FFN_ROWS * k + s + 1, :] = fcw[k:k + 1, lo:lo + PACK_W]

    return pl.pallas_call(
        body, name="split_small", out_shape=jax.ShapeDtypeStruct((N_CHIPS, SMALL_ROWS, PACK_W), F32),
        compiler_params=_params(),
    )(full["conv_short_w"], full["lru_conv_w"], full["lru_wa"], full["lru_wx"], full["ffn_conv_w"],
      bias("lru_ba"), bias("lru_bx"))


def pack_repl(dicts, loss=None):
    operands = [d[n] for d in dicts for n in REPL] + ([loss] if loss is not None else [])

    def body(*refs):
        ins, outs = refs[:len(operands)], refs[len(operands):]
        for i, o in enumerate(outs):
            o[...] = jnp.zeros_like(o)
            r0 = 0
            for p in ins[i * len(REPL):(i + 1) * len(REPL)]:
                for s in range(p.shape[1] // D_MODEL):
                    o[r0:r0 + 1, :] = p[:, s * D_MODEL:(s + 1) * D_MODEL]
                    r0 += 1
        if loss is not None:
            outs[-1][LOSS_ROW:LOSS_ROW + 1, :] = jnp.tile(ins[-1][...], (1, D_MODEL // 128))

    shape = jax.ShapeDtypeStruct((REPL_ROWS, D_MODEL), F32)
    return pl.pallas_call(body, name="pack_repl" + ("_loss" if loss is not None else ""),
                          out_shape=[shape] * len(dicts), compiler_params=_params())(*operands)


def _lane_concat(ref, r0, n):
    return jnp.concatenate([ref[r0 + s:r0 + s + 1, :] for s in range(n)], axis=1)


def unpack_small(packs):
    names = [n for n, _, _ in SMALL_PARTS]

    def body(*refs):
        ins, outs = refs[:len(packs)], refs[len(packs):]
        for i, p in enumerate(ins):
            for (name, nr, shape), o in zip(SMALL_PARTS, outs[i * len(names):(i + 1) * len(names)]):
                r0 = SMALL_OFFSET[name]
                if name in BIASES:
                    o[0] = p[r0:r0 + nr, 0:QUARTER]
                elif name == "ffn_conv_w":
                    for k in range(shape[1]):
                        o[k] = _lane_concat(p, r0 + FFN_ROWS * k, FFN_ROWS)
                elif name == "conv_short_w":
                    for k in range(nr):
                        o[k] = p[r0 + k:r0 + k + 1, :]
                else:
                    o[0] = p[r0:r0 + nr, :].reshape(shape[1:])

    shapes = [jax.ShapeDtypeStruct((s[1], 1, s[2]) if n in TAPS3 else s, F32) for n, _, s in SMALL_PARTS]
    res = pl.pallas_call(body, name="unpack_small", out_shape=shapes * len(packs), compiler_params=_params())(*packs)
    out = []
    for i in range(len(packs)):
        d = dict(zip(names, res[i * len(names):(i + 1) * len(names)]))
        for n in TAPS3:
            d[n] = d[n].transpose(1, 0, 2)
        out.append(d)
    return out


def unpack_repl(packs):
    rows = [(2 * D_FF // D_MODEL) if n == "ffn_conv_b" else 1 for n in REPL]

    def body(*refs):
        ins, outs = refs[:len(packs)], refs[len(packs):]
        for i, p in enumerate(ins):
            r0 = 0
            for nr, o in zip(rows, outs[i * len(REPL):(i + 1) * len(REPL)]):
                o[...] = _lane_concat(p, r0, nr)
                r0 += nr

    shapes = [jax.ShapeDtypeStruct((1, nr * D_MODEL), F32) for nr in rows]
    res = pl.pallas_call(body, name="unpack_repl", out_shape=shapes * len(packs), compiler_params=_params())(*packs)
    return [dict(zip(REPL, res[i * len(REPL):(i + 1) * len(REPL)])) for i in range(len(packs))]


def kernel(x, norm_mix_pre, norm_mix_post, norm_ffn_pre, norm_ffn_post, w_in, conv_short_w, w_conv_branch, lru_conv_w, lru_conv_b, lru_wa, lru_ba, lru_wx, lru_bx, lru_lambda, w_lru_branch, w_out, ffn_w_up, ffn_conv_w, ffn_conv_b, ffn_w_down, loss_target, m_norm_mix_pre, m_norm_mix_post, m_norm_ffn_pre, m_norm_ffn_post, m_w_in, m_conv_short_w, m_w_conv_branch, m_lru_conv_w, m_lru_conv_b, m_lru_wa, m_lru_ba, m_lru_wx, m_lru_bx, m_lru_lambda, m_w_lru_branch, m_w_out, m_ffn_w_up, m_ffn_conv_w, m_ffn_conv_b, m_ffn_w_down, v_norm_mix_pre, v_norm_mix_post, v_norm_ffn_pre, v_norm_ffn_post, v_w_in, v_conv_short_w, v_w_conv_branch, v_lru_conv_w, v_lru_conv_b, v_lru_wa, v_lru_ba, v_lru_wx, v_lru_bx, v_lru_lambda, v_w_lru_branch, v_w_out, v_ffn_w_up, v_ffn_conv_w, v_ffn_conv_b, v_ffn_w_down):
    given = dict(locals())
    w = {n: given[n] for n in WEIGHTS}
    m = {n: given["m_" + n] for n in WEIGHTS}
    v = {n: given["v_" + n] for n in WEIGHTS}

    xi, yi, ci = _position()
    chip_i = 2 * xi + yi
    chip = chip_i.astype(jnp.int32).reshape(1)
    core = ci.astype(jnp.int32).reshape(1)
    xs, target = x[0], loss_target[0]
    g1, g2, g3, g4 = w["norm_mix_pre"], w["norm_mix_post"], w["norm_ffn_pre"], w["norm_ffn_post"]
    shard = {n: cast_bf16(w[n][0], "cast_" + n) for n in BIG}
    small_shard, m_small, v_small = pack_small([w, m, v])

    def gathered(bufs, names):
        return [_own_slot(b, small_shard if n == "small" else shard[n], chip_i) for b, n in zip(bufs, names)]

    def chip_sums(arrays, kinds, tag):
        theirs = pair_split(arrays, kinds, "pair_split_" + tag)
        return [add_pair(g, k, t, core, "pair_add_%s_%d" % (tag, i)) for i, (g, k, t) in enumerate(zip(arrays, kinds, theirs))]

    h1, h1t = norm_in(xs, g1)
    win4, small4 = gathered(run_ride(gather_ride([shard["w_in"], small_shard]), "gather_first"), ("w_in", "small"))
    small = full_small(small4)
    first_up = 256
    (proj,), got = matmul_cols(
        h1, win4, "proj_fwd",
        ride=gather_ride([shard["w_conv_branch"], shard["w_lru_branch"], shard["w_out"], shard["ffn_w_up"]],
                         items=[(0, 0, 256), (1, 0, 256), (2, 0, 256), (3, 0, first_up)]))
    wcb, wlb, wout = [g.reshape(-1, D_MODEL) for g in gathered(got[:3], ("w_conv_branch", "w_lru_branch", "w_out"))]
    got = got[3:]
    up_piece = lambda r0, nr, into=None: gather_ride([shard["ffn_w_up"]], items=[(0, r0, nr)], into=into)
    down_piece = lambda r0, nr, into=None: gather_ride([shard["ffn_w_down"]], items=[(0, r0, nr)], into=into)
    q, ya = mix_conv_fwd(proj, small["conv_short_w"])
    (xl, r, gi, h, yb), got = mix_lru_fwd(
        proj, small["lru_conv_w"], w["lru_conv_b"], small["lru_wa"], small["lru_ba"],
        small["lru_wx"], small["lru_bx"], w["lru_lambda"], ride=up_piece(first_up, 640, got))
    (a, b, merged), got = branch_merge_fwd(ya, yb, wcb, wlb, proj, ride=up_piece(first_up + 640, 128, got))
    (wup4,) = gathered(got, ("ffn_w_up",))
    (mix, x2, h2, h2t), got = mix_out_fwd(merged, wout, xs, g2, g3, ride=down_piece(0, 256))
    (up, act, f), got = ffn_up_act_fwd(h2, wup4, small["ffn_conv_w"], w["ffn_conv_b"], ride=down_piece(256, 512, got))
    wdown = gathered(got, ("ffn_w_down",))[0].reshape(-1, D_MODEL)
    dy, dout, loss, dg4 = ffn_down_loss(f, wdown, x2, target, g4)

    dh2, dwup, dwdown, dfw, dfb = ffn_up_bwd(dout, wdown, up, act, f, small["ffn_conv_w"], wup4, h2t)
    cs_down, cs_up = chip_sums([dwdown, dwup], ["row", "col2"], "ffn")
    down_rows = lambda r0, nr, into=None: exchange_ride([cs_down], items=[(0, r0, nr)], into=into)
    up_rows = lambda r0, nr, into=None: exchange_ride([cs_up], items=[(0, r0, nr)], into=into)
    (dx2, dmix, dg3, dg2), rx_down = norms_mid_bwd(dh2, x2, dy, mix, g3, g2, ride=down_rows(0, 128))
    (da, db, dwout, dgates), rx_down = mix_out_bwd(dmix, wout, merged, a, b, proj, ride=down_rows(128, 256, rx_down))
    (dconv, dwcb, dws), rx_up = mix_conv_bwd(da, wcb, proj, q, small["conv_short_w"], ride=up_rows(0, 176))
    cs_mid = chip_sums([dwout, dwcb], ["row", "row"], "mid")
    (dlru, dwlb, dwa, dwx, dba, dbx, dwl, dbl, dlam), rx_all = mix_lru_bwd(
        db, wlb, proj, xl, r, gi, h, small["lru_conv_w"], small["lru_wa"], small["lru_wx"], w["lru_lambda"],
        ride=exchange_ride([cs_up] + cs_mid, items=[(0, 176, 336), (1, 0, 128), (2, 0, 128)], into=rx_up + [None, None]))
    rx_up, rx_mid = rx_all[:1], rx_all[1:]
    grads = dict(norm_mix_post=dg2, norm_ffn_pre=dg3, norm_ffn_post=dg4, conv_short_w=dws, lru_conv_w=dwl,
                 lru_conv_b=dbl, lru_wa=dwa, lru_ba=dba, lru_wx=dwx, lru_bx=dbx, lru_lambda=dlam,
                 ffn_conv_w=jnp.concatenate([dfw[0], dfw[1]], axis=1), ffn_conv_b=jnp.concatenate([dfb[0], dfb[1]], axis=1))
    cs_late = chip_sums([dwlb, split_small(grads)], ["row", "lead"], "late")
    dproj = [dconv, dlru, dgates]
    (dwin,), rx_late = matmul_cols_bwd(dproj, h1t, "proj_wgrad", True, ride=exchange_ride(cs_late))
    cs_in = chip_sums([dwin], ["col"], "in")
    in_rows = lambda r0, nr, into=None: exchange_ride(cs_in, items=[(0, r0, nr)], into=into)
    (dh1,), rx_in = matmul_cols_bwd(dproj, win4, "proj_dgrad", False, ride=in_rows(0, 384))
    (dx, grads["norm_mix_pre"]), rx_in = norm_in_bwd(dh1, xs, dx2, g1, ride=in_rows(384, 128, rx_in))
    rx_in = rx_in[0]
    (rep_part,) = pack_repl([grads], loss)
    (rep_all,) = run_ride(exchange_ride([], rep=rep_part), "exchange_repl")

    order = (("w_in", cs_in[0], rx_in), ("ffn_w_up", cs_up, rx_up[0]), ("w_conv_branch", cs_mid[1], rx_mid[1]),
             ("w_lru_branch", cs_late[0], rx_late[0]), ("w_out", cs_mid[0], rx_mid[0]),
             ("ffn_w_down", cs_down, rx_down[0]), ("small", cs_late[1], rx_late[1]))
    halves = [sum_chips(rx, cs, chip, "chip_sum_" + n) for n, cs, rx in order]
    me = 4 * xi + 2 * yi + ci
    rep_grad = sum_lead(_own_slot(rep_all, rep_part, me), "device_sum")
    others = pair_swap(halves)

    g_out, d_out, m_out, v_out = {}, {}, {}, {}
    for n, gm, go in zip(BIG, halves[:-1], others[:-1]):
        g, d, nm, nv = adamw_halves(w[n][0], gm, go, m[n][0], v[n][0], core, "adamw_" + n)
        g_out[n], d_out[n], m_out[n], v_out[n] = g[None], d[None], nm[None], nv[None]
    bufs = adamw_halves(small_shard, halves[-1], others[-1], m_small, v_small, core, "adamw_small")
    for dst, part in zip((g_out, d_out, m_out, v_out), unpack_small(bufs)):
        dst.update(part)
    w_rep, m_rep, v_rep = pack_repl([w, m, v])
    d, nm, nv = adamw(w_rep, rep_grad, m_rep, v_rep, "adamw_repl")
    for dst, part in zip((g_out, d_out, m_out, v_out), unpack_repl([rep_grad, d, nm, nv])):
        dst.update(part)

    return (rep_grad[LOSS_ROW, 0], dx[None], *[g_out[n] for n in WEIGHTS], *[d_out[n] for n in WEIGHTS],
            *[m_out[n] for n in WEIGHTS], *[v_out[n] for n in WEIGHTS])
```

```python
import functools
import math

import jax
import jax.numpy as jnp
from jax import lax
from jax.experimental import pallas as pl
from jax.experimental.pallas import tpu as pltpu

F32 = jnp.float32
BF16 = jnp.bfloat16

D_MODEL = 1024
N_CHIPS = 4
N_SEG = 7
D_FF = 3 * D_MODEL
LRU_HEADS = 4
HEAD_DIM = D_MODEL // LRU_HEADS
LRU_C = 8.0
RMS_EPS = 1e-6
CW = 256
FW = 256
SUBLANES = 8
SCAN_UNROLL = 8
VMEM_LIMIT = 58 * 1024 * 1024

ADAM_LR = 0.001
ADAM_B1 = 0.9
ADAM_B2 = 0.999
ADAM_EPS = 1e-08
ADAM_WD = 0.01
ADAM_STEP = 10

_GELU_C = math.sqrt(2.0 / math.pi)
_GELU_K = 0.044715


def _params(**kw):
    return pltpu.CompilerParams(vmem_limit_bytes=VMEM_LIMIT, **kw)


def _sigmoid(x):
    return 1.0 / (1.0 + jnp.exp(-x))


def _gelu(x):
    t = jnp.tanh(_GELU_C * (x + _GELU_K * x * x * x))
    return 0.5 * x * (1.0 + t)


def _gelu_and_grad(x):
    x2 = x * x
    t = jnp.tanh(_GELU_C * (x + _GELU_K * x * x2))
    g = 0.5 * x * (1.0 + t)
    dg = 0.5 * (1.0 + t) + 0.5 * x * (1.0 - t * t) * _GELU_C * (1.0 + 3.0 * _GELU_K * x2)
    return g, dg


def _log_sigmoid(x):
    e = jnp.exp(-jnp.abs(x))
    u = 1.0 + e
    l1p = jnp.where(u == 1.0, e, jnp.log(u) * e / (u - 1.0))
    return jnp.minimum(x, 0.0) - l1p


def _neg_expm1(z):
    series = -z * (1.0 + z * (0.5 + z * (1.0 / 6.0 + z * (1.0 / 24.0 + z * (1.0 / 120.0 + z * (1.0 / 720.0))))))
    return jnp.where(z > -0.2, series, 1.0 - jnp.exp(z))


def _rows(shape):
    return lax.broadcasted_iota(jnp.int32, shape, 0)


def _shift_down(x, k):
    return jnp.where(_rows(x.shape) >= k, pltpu.roll(x, k, 0), 0.0)


def _shift_up(x, k):
    n = x.shape[0]
    return jnp.where(_rows(x.shape) < n - k, pltpu.roll(x, n - k, 0), 0.0)


def _delays(x, k_width):
    return [x] + [_shift_down(x, j) for j in range(1, k_width)]


def _advances(dy, k_width):
    return [dy] + [_shift_up(dy, j) for j in range(1, k_width)]


def _taps_sum(shifted, w_ref, b=None):
    k_width = w_ref.shape[0]
    y = w_ref[k_width - 1:k_width, :] * shifted[0]
    for j in range(1, k_width):
        y = y + w_ref[k_width - 1 - j:k_width - j, :] * shifted[j]
    if b is not None:
        y = y + b
    return y


def _causal_conv(x, w_ref, b=None):
    return _taps_sum(_delays(x, w_ref.shape[0]), w_ref, b)


def _conv_wgrad(advanced, x):
    k_width = len(advanced)
    rows = [jnp.sum(advanced[k_width - 1 - k] * x, axis=0, keepdims=True) for k in range(k_width)]
    return jnp.concatenate(rows, axis=0)


def _dot(a, b):
    return jnp.dot(a, b, preferred_element_type=F32)


def _dot_nt(a, b):
    return lax.dot_general(a, b, (((1,), (1,)), ((), ())), preferred_element_type=F32)


def _dot_tn(a, b):
    return lax.dot_general(a, b, (((0,), (0,)), ((), ())), preferred_element_type=F32)


def _rms_stats(x):
    r = lax.rsqrt(jnp.mean(x * x, axis=-1, keepdims=True) + RMS_EPS)
    return x * r, r


def _rms_bwd(n, r, g, dy):
    dn = dy * g
    dx = r * (dn - n * jnp.mean(dn * n, axis=-1, keepdims=True))
    return dx, dy * n


def _scan(a_ref, b_ref, h_ref, reverse):
    n, c = a_ref.shape
    row = lax.broadcasted_iota(jnp.int32, (SUBLANES, c), 0)
    span = SCAN_UNROLL * SUBLANES
    n_trips = n // span

    def within(a, b):
        for k in (1, 2, 4):
            if reverse:
                keep, shift = row < SUBLANES - k, SUBLANES - k
            else:
                keep, shift = row >= k, k
            ap = jnp.where(keep, pltpu.roll(a, shift, 0), 1.0)
            bp = jnp.where(keep, pltpu.roll(b, shift, 0), 0.0)
            b = a * bp + b
            a = a * ap
        return a, b

    def trip(t, carry):
        base = pl.multiple_of((n_trips - 1 - t if reverse else t) * span, span)
        order = list(reversed(range(SCAN_UNROLL))) if reverse else list(range(SCAN_UNROLL))
        loaded = [(a_ref[pl.ds(base + u * SUBLANES, SUBLANES), :], b_ref[pl.ds(base + u * SUBLANES, SUBLANES), :])
                  for u in order]
        out = []
        for a, b in [within(a, b) for a, b in loaded]:
            h = a * carry + b
            out.append(h)
            carry = h[0:1, :] if reverse else h[SUBLANES - 1:SUBLANES, :]
        for u, h in zip(order, out):
            h_ref[pl.ds(base + u * SUBLANES, SUBLANES), :] = h
        return carry

    lax.fori_loop(0, n_trips, trip, jnp.zeros((1, c), F32))


def _scan_forward(a_ref, b_ref, h_ref):
    _scan(a_ref, b_ref, h_ref, False)


def _scan_backward(c_ref, b_ref, g_ref):
    _scan(c_ref, b_ref, g_ref, True)


MESH = pl.DeviceIdType.MESH
_HBM = pl.BlockSpec(memory_space=pltpu.HBM)
_OTHER_CHIPS = ((1, 0), (0, 1), (1, 1))
_OTHER_DEVICES = tuple((dx, dy, dc) for dx in (0, 1) for dy in (0, 1) for dc in (0, 1) if dx or dy or dc)
N_DEVICES = 8


def _position():
    return lax.axis_index("x"), lax.axis_index("y"), lax.axis_index("c")


def _flip(v, d):
    return 1 - v if d else v


def _chip(x, y, p):
    px, py = _flip(x, _OTHER_CHIPS[p][0]), _flip(y, _OTHER_CHIPS[p][1])
    return px, py, 2 * px + py


class _Ride:
    def __init__(self, srcs, bufs, scratch, plan, collective_id):
        self.srcs, self.bufs, self.scratch, self.plan = list(srcs), list(bufs), list(scratch), plan
        self.collective_id = collective_id


NEIGHBOURS_AND_SIBLING = 1
OTHER_CHIPS_SAME_CORE = 2
ALL_DEVICES = 3
SIBLING = 4


def _handshake(peers):
    barrier = pltpu.get_barrier_semaphore()
    for peer in peers:
        pl.semaphore_signal(barrier, inc=1, device_id=peer, device_id_type=MESH)
    pl.semaphore_wait(barrier, len(peers))


def _call(body, *, name, grid, in_specs, out_specs, out_shape, operands, scratch_shapes=(), ride=None):
    in_specs, out_specs, out_shape = list(in_specs), list(out_specs), list(out_shape)
    scratch_shapes = list(scratch_shapes)
    if ride is None:
        return pl.pallas_call(body, name=name, grid=grid, in_specs=in_specs, out_specs=out_specs, out_shape=out_shape,
                              scratch_shapes=scratch_shapes, compiler_params=_params())(*operands)
    n_in, n_out, n_scr = len(in_specs), len(out_shape), len(scratch_shapes)
    old = [i for i, b in enumerate(ride.bufs) if not isinstance(b, jax.ShapeDtypeStruct)]
    n_src, n_old, n_buf = len(ride.srcs), len(old), len(ride.bufs)

    def full_body(*refs):
        o0 = n_in + n_src + n_old
        s0 = o0 + n_out + n_buf
        start, relay, relay_on, finish = ride.plan(refs[n_in:n_in + n_src], refs[o0 + n_out:s0], refs[s0 + n_scr:])
        ids = [pl.program_id(i) for i in range(len(grid))]
        first = functools.reduce(jnp.logical_and, [i == 0 for i in ids])
        middle = functools.reduce(jnp.logical_and, [ids[0] == grid[0] // 2] + [i == 0 for i in ids[1:]])
        last = functools.reduce(jnp.logical_and, [i == g - 1 for i, g in zip(ids, grid)])
        pl.when(first)(start)
        pl.when(middle)(relay)
        pl.when(last)(relay_on)
        body(*refs[:n_in], *refs[o0:o0 + n_out], *refs[s0:s0 + n_scr])
        pl.when(last)(finish)

    shapes = [jax.ShapeDtypeStruct(b.shape, b.dtype) for b in ride.bufs]
    res = pl.pallas_call(
        full_body, name=name, grid=grid,
        in_specs=in_specs + [_HBM] * (n_src + n_old), out_specs=out_specs + [_HBM] * n_buf,
        out_shape=out_shape + shapes, scratch_shapes=scratch_shapes + ride.scratch,
        input_output_aliases={n_in + n_src + k: n_out + i for k, i in enumerate(old)},
        compiler_params=_params(collective_id=ride.collective_id),
    )(*operands, *ride.srcs, *[ride.bufs[i] for i in old])
    return list(res[:n_out]), list(res[n_out:])


def run_ride(ride, name):
    def body():
        pass

    return _call(body, name=name, grid=(1,), in_specs=[], out_specs=[], out_shape=[], operands=[], ride=ride)[1]


def gather_ride(shards, items=None, into=None):
    items = items or [(a, 0, s.shape[0]) for a, s in enumerate(shards)]
    bufs = into or [jax.ShapeDtypeStruct((N_CHIPS,) + s.shape, s.dtype) for s in shards]
    nrel = len(_OTHER_CHIPS)

    def plan(srcs, dsts, sems):
        ici_send, ici_recv, hop_send, hop_recv, sib_send, sib_recv = sems
        x, y, c = _position()
        j = 2 * x + y

        def rows(ref, it, h, q=None):
            half = it[2] // 2
            if q is None:
                return ref.at[pl.ds(it[1] + h * half, half), :]
            return ref.at[pl.ds(it[1] + h * half + q * (half // 2), half // 2), :]

        def ici(i, p, slot):
            it = items[i]
            px, py, _ = _chip(x, y, p)
            return pltpu.make_async_remote_copy(
                src_ref=rows(srcs[it[0]], it, c), dst_ref=rows(dsts[it[0]].at[slot], it, c),
                send_sem=ici_send.at[i * nrel + p], recv_sem=ici_recv.at[i * nrel + p],
                device_id=(px, py, c), device_id_type=MESH)

        def hop(i, p, slot):
            it = items[i]
            part = rows(dsts[it[0]].at[slot], it, c, p)
            px, py, _ = _chip(x, y, 1 - p)
            return pltpu.make_async_remote_copy(
                src_ref=part, dst_ref=part, send_sem=hop_send.at[i * 2 + p], recv_sem=hop_recv.at[i * 2 + p],
                device_id=(px, py, c), device_id_type=MESH)

        def sib(i, p, h):
            it = items[i]
            part = rows(dsts[it[0]].at[_chip(x, y, p)[2]], it, h)
            return pltpu.make_async_remote_copy(
                src_ref=part, dst_ref=part, send_sem=sib_send.at[i * nrel + p], recv_sem=sib_recv.at[i * nrel + p],
                device_id=(x, y, 1 - c), device_id_type=MESH)

        every = range(len(items))
        diag = _chip(x, y, 2)[2]

        def start():
            _handshake([_chip(x, y, 0)[:2] + (c,), _chip(x, y, 1)[:2] + (c,), (x, y, 1 - c)])
            for i in every:
                for p in (0, 1):
                    ici(i, p, j).start()

        def relay():
            for i in every:
                for p in (0, 1):
                    k = _chip(x, y, p)[2]
                    ici(i, p, k).wait_recv()
                    hop(i, p, k).start()
                    sib(i, p, c).start()

        def relay_on():
            for i in every:
                for p in (0, 1):
                    hop(i, p, diag).wait_recv()
                sib(i, 2, c).start()

        def finish():
            for i in every:
                for p in range(nrel):
                    sib(i, p, 1 - c).wait_recv()
            for i in every:
                for p in (0, 1):
                    ici(i, p, j).wait_send()
                    hop(i, p, _chip(x, y, p)[2]).wait_send()
                for p in range(nrel):
                    sib(i, p, c).wait_send()

        return start, relay, relay_on, finish

    n = len(items)
    sems = [pltpu.SemaphoreType.DMA((n * nrel,))] * 2 + [pltpu.SemaphoreType.DMA((n * 2,))] * 2 \
        + [pltpu.SemaphoreType.DMA((n * nrel,))] * 2
    return _Ride(shards, bufs, sems, plan, NEIGHBOURS_AND_SIBLING)


def exchange_ride(sums, items=None, into=None, rep=None):
    items = [(a, 0, s.shape[1]) for a, s in enumerate(sums)] if items is None else items
    into = into or [None] * len(sums)
    bufs = [jax.ShapeDtypeStruct(s.shape, s.dtype) if b is None else b for s, b in zip(sums, into)]
    srcs = list(sums)
    scratch = [pltpu.SemaphoreType.DMA((max(len(items), 1) * len(_OTHER_CHIPS),))] * 2
    if rep is not None:
        srcs.append(rep)
        bufs.append(jax.ShapeDtypeStruct((N_DEVICES,) + rep.shape, rep.dtype))
        scratch += [pltpu.SemaphoreType.DMA((len(_OTHER_DEVICES),))] * 2
    nrel = len(_OTHER_CHIPS)

    def plan(src_refs, dst_refs, sems):
        x, y, c = _position()
        j = 2 * x + y
        me = 4 * x + 2 * y + c

        def part(i, p, src_slot, dst_slot):
            a, r0, nr = items[i]
            px, py, _ = _chip(x, y, p)
            return pltpu.make_async_remote_copy(
                src_ref=src_refs[a].at[src_slot, pl.ds(r0, nr), :], dst_ref=dst_refs[a].at[dst_slot, pl.ds(r0, nr), :],
                send_sem=sems[0].at[i * nrel + p], recv_sem=sems[1].at[i * nrel + p],
                device_id=(px, py, c), device_id_type=MESH)

        def device(q):
            dx, dy, dc = _OTHER_DEVICES[q]
            return _flip(x, dx), _flip(y, dy), _flip(c, dc)

        def rep_copy(q, slot):
            return pltpu.make_async_remote_copy(
                src_ref=src_refs[-1], dst_ref=dst_refs[-1].at[slot], send_sem=sems[2].at[q], recv_sem=sems[3].at[q],
                device_id=device(q), device_id_type=MESH)

        pairs = [(i, p) for i in range(len(items)) for p in range(nrel)]
        others = range(len(_OTHER_DEVICES)) if rep is not None else ()

        def start():
            if rep is None:
                _handshake([_chip(x, y, p)[:2] + (c,) for p in range(nrel)])
            else:
                _handshake([device(q) for q in others])
            for i, p in pairs:
                part(i, p, _chip(x, y, p)[2], j).start()
            for q in others:
                rep_copy(q, me).start()

        def finish():
            for i, p in pairs:
                k = _chip(x, y, p)[2]
                part(i, p, k, k).wait_recv()
            for q in others:
                px, py, pc = device(q)
                rep_copy(q, 4 * px + 2 * py + pc).wait_recv()
            for i, p in pairs:
                part(i, p, _chip(x, y, p)[2], j).wait_send()
            for q in others:
                rep_copy(q, me).wait_send()

        return start, lambda: None, lambda: None, finish

    return _Ride(srcs, bufs, scratch, plan, OTHER_CHIPS_SAME_CORE if rep is None else ALL_DEVICES)


def _own_slot(buf, own, index):
    return lax.dynamic_update_slice(buf, own[None], (index,) + (0,) * own.ndim)


def _token_tile(s):
    return min(s, 512)


def norm_in(x, g):
    s, d = x.shape
    t = _token_tile(s)

    def body(x_ref, g_ref, o_ref, ot_ref):
        n, _ = _rms_stats(x_ref[...])
        h = n * g_ref[...]
        o_ref[...] = h.astype(BF16)
        ot_ref[...] = h.T.astype(BF16)

    return pl.pallas_call(
        body, name="norm_in", grid=(s // t,),
        in_specs=[pl.BlockSpec((t, d), lambda i: (i, 0)), pl.BlockSpec((1, d), lambda i: (0, 0))],
        out_specs=[pl.BlockSpec((t, d), lambda i: (i, 0)), pl.BlockSpec((d, t), lambda i: (0, i))],
        out_shape=[jax.ShapeDtypeStruct((s, d), BF16), jax.ShapeDtypeStruct((d, s), BF16)],
        compiler_params=_params(),
    )(x, g)


def matmul_cols(a, w4, name, ride=None):
    m, k = a.shape
    nj, _, ns = w4.shape
    nb = ns // CW

    def body(a_ref, w_ref, o_ref):
        o_ref[...] = _dot(a_ref[...], w_ref[0])

    return _call(
        body, name=name, grid=(nj, nb),
        in_specs=[pl.BlockSpec((m, k), lambda j, b: (0, 0)),
                  pl.BlockSpec((1, k, CW), lambda j, b: (j, 0, b))],
        out_specs=[pl.BlockSpec((m, CW), lambda j, b: (0, j * nb + b))],
        out_shape=[jax.ShapeDtypeStruct((m, nj * ns), F32)],
        operands=(a, w4), ride=ride)


def mix_conv_fwd(proj, ws, ride=None):
    s = proj.shape[0]
    nblk = D_MODEL // CW

    def body(cb_ref, cc_ref, cx_ref, ws_ref, q_ref, ya_ref):
        q = _causal_conv(cc_ref[...] * cx_ref[...], ws_ref)
        q_ref[...] = q
        ya_ref[...] = (cb_ref[...] * q).astype(BF16)

    seg = lambda k: pl.BlockSpec((s, CW), lambda c, k=k: (0, k * nblk + c))
    return _call(
        body, name="mix_conv_fwd", grid=(nblk,),
        in_specs=[seg(0), seg(1), seg(2), pl.BlockSpec((3, CW), lambda c: (0, c))],
        out_specs=[pl.BlockSpec((s, CW), lambda c: (0, c))] * 2,
        out_shape=[jax.ShapeDtypeStruct((s, D_MODEL), F32), jax.ShapeDtypeStruct((s, D_MODEL), BF16)],
        operands=(proj, proj, proj, ws), ride=ride)


def _lru_gates(r, ls):
    log_a = LRU_C * r * ls
    a = jnp.exp(log_a)
    mult = jnp.sqrt(_neg_expm1(2.0 * log_a))
    mult = jnp.where(_rows(r.shape) == 0, 1.0, mult)
    return a, mult


def mix_lru_fwd(proj, wl, bl, wa, ba, wx, bx, lam, ride=None):
    s = proj.shape[0]
    nblk = D_MODEL // CW

    def body(lx_ref, ly_ref, wl_ref, bl_ref, wa_ref, ba_ref, wx_ref, bx_ref, lam_ref,
             xl_ref, r_ref, i_ref, h_ref, yb_ref, a_scr, u_scr):
        xl = _causal_conv(lx_ref[...], wl_ref, bl_ref[...])
        xlb = xl.astype(BF16)
        xl_ref[...] = xlb
        r = _sigmoid(_dot(xlb, wa_ref[0]) + ba_ref[...])
        i = _sigmoid(_dot(xlb, wx_ref[0]) + bx_ref[...])
        r_ref[...] = r.astype(BF16)
        i_ref[...] = i.astype(BF16)
        a, mult = _lru_gates(r, _log_sigmoid(lam_ref[...]))
        a_scr[...] = a
        u_scr[...] = mult * i * xl
        _scan_forward(a_scr, u_scr, h_ref)
        yb_ref[...] = (h_ref[...] * _gelu(ly_ref[...])).astype(BF16)

    blk = lambda k: pl.BlockSpec((s, CW), lambda c, k=k: (0, k * nblk + c))
    vec = pl.BlockSpec((1, CW), lambda c: (0, c))
    mat = pl.BlockSpec((1, CW, CW), lambda c: (c, 0, 0))
    out = pl.BlockSpec((s, CW), lambda c: (0, c))
    f = jax.ShapeDtypeStruct((s, D_MODEL), F32)
    hb = jax.ShapeDtypeStruct((s, D_MODEL), BF16)
    return _call(
        body, name="mix_lru_fwd", grid=(nblk,),
        in_specs=[blk(3), blk(4), pl.BlockSpec((4, CW), lambda c: (0, c)), vec, mat, vec, mat, vec, vec],
        out_specs=[out] * 5,
        out_shape=[hb, hb, hb, f, hb],
        scratch_shapes=[pltpu.VMEM((s, CW), F32), pltpu.VMEM((s, CW), F32)],
        operands=(proj, proj, wl, bl, wa, ba, wx, bx, lam), ride=ride)


def branch_merge_fwd(ya, yb, wcb, wlb, proj, ride=None):
    s = ya.shape[0]
    nblk = D_MODEL // CW

    def body(ya_ref, yb_ref, wcb_ref, wlb_ref, gc_ref, gl_ref, a_ref, b_ref, m_ref):
        a = _dot(ya_ref[...], wcb_ref[...])
        b = _dot(yb_ref[...], wlb_ref[...])
        a_ref[...] = a
        b_ref[...] = b
        m_ref[...] = (_sigmoid(gc_ref[...]) * a + _sigmoid(gl_ref[...]) * b).astype(BF16)

    res = pl.BlockSpec((s, D_MODEL), lambda n: (0, 0))
    wcol = pl.BlockSpec((D_MODEL, CW), lambda n: (0, n))
    blk = lambda k: pl.BlockSpec((s, CW), lambda n, k=k: (0, k * nblk + n))
    out = pl.BlockSpec((s, CW), lambda n: (0, n))
    f = jax.ShapeDtypeStruct((s, D_MODEL), F32)
    return _call(
        body, name="branch_merge_fwd", grid=(nblk,),
        in_specs=[res, res, wcol, wcol, blk(5), blk(6)],
        out_specs=[out] * 3,
        out_shape=[f, f, jax.ShapeDtypeStruct((s, D_MODEL), BF16)],
        operands=(ya, yb, wcb, wlb, proj, proj), ride=ride)


def mix_out_fwd(merged, wout, x, g2, g3, ride=None):
    s, d = x.shape
    t = _token_tile(s)

    def body(m_ref, w_ref, x_ref, g2_ref, g3_ref, mix_ref, x2_ref, h2_ref, h2t_ref):
        mix = _dot(m_ref[...], w_ref[...])
        mix_ref[...] = mix
        n, _ = _rms_stats(mix)
        x2 = x_ref[...] + n * g2_ref[...]
        x2_ref[...] = x2
        n2, _ = _rms_stats(x2)
        h2 = n2 * g3_ref[...]
        h2_ref[...] = h2.astype(BF16)
        h2t_ref[...] = h2.T.astype(BF16)

    tile = pl.BlockSpec((t, d), lambda i: (i, 0))
    vec = pl.BlockSpec((1, d), lambda i: (0, 0))
    f = jax.ShapeDtypeStruct((s, d), F32)
    return _call(
        body, name="mix_out_fwd", grid=(s // t,),
        in_specs=[tile, pl.BlockSpec((d, d), lambda i: (0, 0)), tile, vec, vec],
        out_specs=[tile] * 3 + [pl.BlockSpec((d, t), lambda i: (0, i))],
        out_shape=[f, f, jax.ShapeDtypeStruct((s, d), BF16), jax.ShapeDtypeStruct((d, s), BF16)],
        operands=(merged, wout, x, g2, g3), ride=ride)


def ffn_up_act_fwd(h2, wup4, fw, fb, ride=None):
    s, k = h2.shape
    ns = wup4.shape[2]
    per_chip = ns // CW
    nblk = D_FF // CW

    def body(h_ref, wg_ref, wv_ref, cg_ref, cv_ref, bg_ref, bv_ref, up_ref, act_ref, f_ref):
        h = h_ref[...]
        ug = _dot(h, wg_ref[0])
        uv = _dot(h, wv_ref[0])
        up_ref[0] = ug
        up_ref[1] = uv
        gate = _causal_conv(ug, cg_ref, bg_ref[...])
        val = _causal_conv(uv, cv_ref, bv_ref[...])
        act_ref[0] = gate.astype(BF16)
        act_ref[1] = val.astype(BF16)
        f_ref[...] = (_gelu(gate) * val).astype(BF16)

    wcols = lambda h: pl.BlockSpec((1, k, CW), lambda n, h=h: (n // per_chip + 2 * h, 0, n % per_chip))
    half = lambda h, rows: pl.BlockSpec((rows, CW), lambda n, h=h: (0, h * nblk + n))
    both = pl.BlockSpec((2, s, CW), lambda n: (0, 0, n))
    return _call(
        body, name="ffn_up_act_fwd", grid=(nblk,),
        in_specs=[pl.BlockSpec((s, k), lambda n: (0, 0)), wcols(0), wcols(1),
                  half(0, 3), half(1, 3), half(0, 1), half(1, 1)],
        out_specs=[both, both, pl.BlockSpec((s, CW), lambda n: (0, n))],
        out_shape=[jax.ShapeDtypeStruct((2, s, D_FF), F32), jax.ShapeDtypeStruct((2, s, D_FF), BF16),
                   jax.ShapeDtypeStruct((s, D_FF), BF16)],
        operands=(h2, wup4, wup4, fw, fw, fb, fb), ride=ride)


def ffn_down_loss(f, wdown, x2, target, g4):
    s, d = x2.shape
    t = _token_tile(s)

    def body(f_ref, w_ref, x2_ref, tg_ref, g4_ref, dy_ref, dout_ref, loss_ref, dg4_ref):
        @pl.when(pl.program_id(0) == 0)
        def _():
            loss_ref[...] = jnp.zeros_like(loss_ref)
            dg4_ref[...] = jnp.zeros_like(dg4_ref)

        out = _dot(f_ref[...], w_ref[...])
        n, r = _rms_stats(out)
        err = x2_ref[...] + n * g4_ref[...] - tg_ref[...]
        loss_ref[...] += jnp.full(loss_ref.shape, (0.5 / d) * jnp.sum(err * err), F32)
        dy = err * (1.0 / d)
        dy_ref[...] = dy
        dout, dg = _rms_bwd(n, r, g4_ref[...], dy)
        dout_ref[...] = dout.astype(BF16)
        dg4_ref[...] += jnp.sum(dg, axis=0, keepdims=True)

    tile = pl.BlockSpec((t, d), lambda i: (i, 0))
    vec = pl.BlockSpec((1, d), lambda i: (0, 0))
    return pl.pallas_call(
        body, name="ffn_down_loss", grid=(s // t,),
        in_specs=[pl.BlockSpec((t, D_FF), lambda i: (i, 0)), pl.BlockSpec((D_FF, d), lambda i: (0, 0)), tile, tile, vec],
        out_specs=[tile, tile, pl.BlockSpec((1, 128), lambda i: (0, 0)), vec],
        out_shape=[jax.ShapeDtypeStruct((s, d), F32), jax.ShapeDtypeStruct((s, d), BF16),
                   jax.ShapeDtypeStruct((1, 128), F32), jax.ShapeDtypeStruct((1, d), F32)],
        compiler_params=_params(),
    )(f, wdown, x2, target, g4)


def ffn_up_bwd(dout, wdown, up, act, f, fw, wup4, h2t, ride=None):
    k, s = h2t.shape
    nblk = D_FF // FW
    per_chip = wup4.shape[2] // FW

    def body(do_ref, wd_ref, up_ref, act_ref, f_ref, cg_ref, cv_ref, wg_ref, wv_ref, h_ref,
             dh_ref, dwu_ref, dwd_ref, dw_ref, db_ref, dup_scr):
        @pl.when(pl.program_id(0) == 0)
        def _():
            dup_scr[...] = jnp.zeros_like(dup_scr)
            dh_ref[...] = jnp.zeros_like(dh_ref)

        do = do_ref[...]
        df = _dot_nt(do, wd_ref[...])
        dg = dup_scr[0]
        dv = dup_scr[1]
        ht = h_ref[...]
        dh_ref[...] += _dot_nt(dg, wg_ref[0]) + _dot_nt(dv, wv_ref[0])
        dwu_ref[0] = _dot(ht, dg).astype(BF16)
        dwu_ref[1] = _dot(ht, dv).astype(BF16)
        dwd_ref[...] = _dot_tn(f_ref[...], do).astype(BF16)
        val = act_ref[1].astype(F32)
        ge, dge = _gelu_and_grad(act_ref[0].astype(F32))
        dgate = _advances(df * val * dge, 3)
        dval = _advances(df * ge, 3)
        dw_ref[0] = _conv_wgrad(dgate, up_ref[0])
        dw_ref[1] = _conv_wgrad(dval, up_ref[1])
        db_ref[0] = jnp.sum(dgate[0], axis=0, keepdims=True)
        db_ref[1] = jnp.sum(dval[0], axis=0, keepdims=True)
        dup_scr[0] = _taps_sum(dgate, cg_ref).astype(BF16)
        dup_scr[1] = _taps_sum(dval, cv_ref).astype(BF16)

    cur = lambda n: jnp.minimum(n, nblk - 1)
    prev = lambda n: jnp.maximum(n - 1, 0)
    once = pl.Buffered(1)
    both = lambda rows: pl.BlockSpec((2, rows, FW), lambda n: (0, 0, cur(n)))
    taps = lambda h: pl.BlockSpec((3, FW), lambda n, h=h: (0, h * nblk + cur(n)))
    wcols = lambda h: pl.BlockSpec((1, k, FW), lambda n, h=h: (prev(n) // per_chip + 2 * h, 0, prev(n) % per_chip))
    return _call(
        body, name="ffn_up_bwd", grid=(nblk + 1,),
        in_specs=[pl.BlockSpec((s, D_MODEL), lambda n: (0, 0), pipeline_mode=once),
                  pl.BlockSpec((FW, D_MODEL), lambda n: (cur(n), 0)), both(s), both(s),
                  pl.BlockSpec((s, FW), lambda n: (0, cur(n))), taps(0), taps(1), wcols(0), wcols(1),
                  pl.BlockSpec((k, s), lambda n: (0, 0), pipeline_mode=once)],
        out_specs=[pl.BlockSpec((s, k), lambda n: (0, 0), pipeline_mode=once),
                   pl.BlockSpec((2, k, FW), lambda n: (0, 0, prev(n))),
                   pl.BlockSpec((FW, D_MODEL), lambda n: (cur(n), 0)), both(3), both(1)],
        out_shape=[jax.ShapeDtypeStruct((s, k), F32), jax.ShapeDtypeStruct((2, k, D_FF), BF16),
                   jax.ShapeDtypeStruct((D_FF, D_MODEL), BF16),
                   jax.ShapeDtypeStruct((2, 3, D_FF), F32), jax.ShapeDtypeStruct((2, 1, D_FF), F32)],
        scratch_shapes=[pltpu.VMEM((2, s, FW), BF16)],
        operands=(dout, wdown, up, act, f, fw, fw, wup4, wup4, h2t), ride=ride)


def matmul_cols_bwd(dy, other, name, wgrad, ride=None):
    m = dy[0].shape[1]
    if wgrad:
        k = other.shape[0]
        nj, nb = N_CHIPS, sum(d.shape[0] * d.shape[2] for d in dy) // (N_CHIPS * CW)
    else:
        nj, k, ns = other.shape
        nb = ns // CW
    per_seg = dy[0].shape[2] // CW
    first = [sum(d.shape[0] for d in dy[:i]) for i in range(len(dy))]

    def segment(j, b):
        return (j * nb + b) // per_seg, (j * nb + b) % per_seg

    def body(*refs):
        dy_refs, (o_ref, r_ref) = refs[:len(dy)], refs[len(dy):]
        seg, _ = segment(pl.program_id(0), pl.program_id(1))
        dyb = dy_refs[-1][0]
        for i in range(len(dy) - 2, -1, -1):
            dyb = jnp.where(seg < first[i + 1], dy_refs[i][0], dyb)
        if wgrad:
            r_ref[...] = _dot(o_ref[...], dyb).astype(BF16)
        else:
            @pl.when((pl.program_id(0) == 0) & (pl.program_id(1) == 0))
            def _():
                r_ref[...] = jnp.zeros_like(r_ref)

            r_ref[...] += _dot_nt(dyb, o_ref[0])

    def dy_spec(i):
        nseg = dy[i].shape[0]

        def index(j, b):
            seg, col = segment(j, b)
            local = seg - first[i]
            return (jnp.clip(local, 0, nseg - 1), 0,
                    jnp.where(local < 0, 0, jnp.where(local >= nseg, per_seg - 1, col)))

        return pl.BlockSpec((1, m, CW), index)

    if wgrad:
        other_spec = pl.BlockSpec((k, m), lambda j, b: (0, 0))
        out_spec = pl.BlockSpec((k, CW), lambda j, b: (0, j * nb + b))
        out_shape = jax.ShapeDtypeStruct((k, nj * nb * CW), BF16)
    else:
        other_spec = pl.BlockSpec((1, k, CW), lambda j, b: (j, 0, b))
        out_spec = pl.BlockSpec((m, k), lambda j, b: (0, 0))
        out_shape = jax.ShapeDtypeStruct((m, k), F32)
    return _call(
        body, name=name, grid=(nj, nb), in_specs=[dy_spec(i) for i in range(len(dy))] + [other_spec],
        out_specs=[out_spec], out_shape=[out_shape], operands=(*dy, other), ride=ride)


def norms_mid_bwd(dh2, x2, dy, mix, g3, g2, ride=None):
    s, d = x2.shape
    t = _token_tile(s)

    def body(dh2_ref, x2_ref, dy_ref, mix_ref, g3_ref, g2_ref, dx2_ref, dmix_ref, dg3_ref, dg2_ref):
        @pl.when(pl.program_id(0) == 0)
        def _():
            dg3_ref[...] = jnp.zeros_like(dg3_ref)
            dg2_ref[...] = jnp.zeros_like(dg2_ref)

        n3, r3 = _rms_stats(x2_ref[...])
        dx, dg3 = _rms_bwd(n3, r3, g3_ref[...], dh2_ref[...])
        dx2 = dy_ref[...] + dx
        dx2_ref[...] = dx2
        dg3_ref[...] += jnp.sum(dg3, axis=0, keepdims=True)
        n2, r2 = _rms_stats(mix_ref[...])
        dmix, dg2 = _rms_bwd(n2, r2, g2_ref[...], dx2)
        dmix_ref[...] = dmix.astype(BF16)
        dg2_ref[...] += jnp.sum(dg2, axis=0, keepdims=True)

    tile = pl.BlockSpec((t, d), lambda i: (i, 0))
    vec = pl.BlockSpec((1, d), lambda i: (0, 0))
    v = jax.ShapeDtypeStruct((1, d), F32)
    return _call(
        body, name="norms_mid_bwd", grid=(s // t,),
        in_specs=[tile, tile, tile, tile, vec, vec],
        out_specs=[tile, tile, vec, vec],
        out_shape=[jax.ShapeDtypeStruct((s, d), F32), jax.ShapeDtypeStruct((s, d), BF16), v, v],
        operands=(dh2, x2, dy, mix, g3, g2), ride=ride)


def mix_out_bwd(dmix, wout, merged, a, b, proj, ride=None):
    s = dmix.shape[0]
    nblk = D_MODEL // CW

    def body(dm_ref, w_ref, mg_ref, a_ref, b_ref, gc_ref, gl_ref, da_ref, db_ref, dw_ref, dg_ref):
        dm = dm_ref[...]
        dmerged = _dot_nt(dm, w_ref[...])
        dw_ref[...] = _dot_tn(mg_ref[...], dm).astype(BF16)
        sc = _sigmoid(gc_ref[...])
        sl = _sigmoid(gl_ref[...])
        da_ref[...] = (dmerged * sc).astype(BF16)
        db_ref[...] = (dmerged * sl).astype(BF16)
        dg_ref[0] = (dmerged * a_ref[...] * sc * (1.0 - sc)).astype(BF16)
        dg_ref[1] = (dmerged * b_ref[...] * sl * (1.0 - sl)).astype(BF16)

    res = pl.BlockSpec((s, D_MODEL), lambda n: (0, 0))
    rows = pl.BlockSpec((CW, D_MODEL), lambda n: (n, 0))
    col = pl.BlockSpec((s, CW), lambda n: (0, n))
    blk = lambda k: pl.BlockSpec((s, CW), lambda n, k=k: (0, k * nblk + n))
    hb = jax.ShapeDtypeStruct((s, D_MODEL), BF16)
    return _call(
        body, name="mix_out_bwd", grid=(nblk,),
        in_specs=[res, rows, col, col, col, blk(5), blk(6)],
        out_specs=[col, col, rows, pl.BlockSpec((2, s, CW), lambda n: (0, 0, n))],
        out_shape=[hb, hb, jax.ShapeDtypeStruct((D_MODEL, D_MODEL), BF16), jax.ShapeDtypeStruct((2, s, D_MODEL), BF16)],
        operands=(dmix, wout, merged, a, b, proj, proj), ride=ride)


def mix_conv_bwd(da, wcb, proj, q, ws, ride=None):
    s = da.shape[0]
    nblk = D_MODEL // CW

    def body(da_ref, w_ref, cb_ref, cc_ref, cx_ref, q_ref, ws_ref, dc_ref, dw_ref, dws_ref):
        dab = da_ref[...]
        dya = _dot_nt(dab, w_ref[...])
        cb = cb_ref[...]
        cc = cc_ref[...]
        cx = cx_ref[...]
        q = q_ref[...]
        dw_ref[...] = _dot_tn((cb * q).astype(BF16), dab).astype(BF16)
        dc_ref[0] = (dya * q).astype(BF16)
        dq = _advances(dya * cb, 3)
        dp = _taps_sum(dq, ws_ref)
        dws_ref[...] = _conv_wgrad(dq, cc * cx)
        dc_ref[1] = (dp * cx).astype(BF16)
        dc_ref[2] = (dp * cc).astype(BF16)

    res = pl.BlockSpec((s, D_MODEL), lambda n: (0, 0))
    rows = pl.BlockSpec((CW, D_MODEL), lambda n: (n, 0))
    col = pl.BlockSpec((s, CW), lambda n: (0, n))
    blk = lambda k: pl.BlockSpec((s, CW), lambda n, k=k: (0, k * nblk + n))
    taps = pl.BlockSpec((3, CW), lambda n: (0, n))
    hb = jax.ShapeDtypeStruct((s, D_MODEL), BF16)
    return _call(
        body, name="mix_conv_bwd", grid=(nblk,),
        in_specs=[res, rows, blk(0), blk(1), blk(2), col, taps],
        out_specs=[pl.BlockSpec((3, s, CW), lambda n: (0, 0, n)), rows, taps],
        out_shape=[jax.ShapeDtypeStruct((3, s, D_MODEL), BF16), jax.ShapeDtypeStruct((D_MODEL, D_MODEL), BF16),
                   jax.ShapeDtypeStruct((3, D_MODEL), F32)],
        operands=(da, wcb, proj, proj, proj, q, ws), ride=ride)


def mix_lru_bwd(db, wlb, proj, xl, r, i, h, wl, wa, wx, lam, ride=None):
    s = db.shape[0]
    nblk = D_MODEL // CW

    def body(db_ref, w_ref, lx_ref, ly_ref, xl_ref, r_ref, i_ref, h_ref, wl_ref, wa_ref, wx_ref, lam_ref,
             dl_ref, dw_ref, dwa_ref, dwx_ref, dba_ref, dbx_ref, dwl_ref, dbl_ref, dlam_ref,
             c_scr, g_scr):
        dbb = db_ref[...]
        dyb = _dot_nt(dbb, w_ref[...])
        h = h_ref[...]
        ge, dge = _gelu_and_grad(ly_ref[...])
        dw_ref[...] = _dot_tn((h * ge).astype(BF16), dbb).astype(BF16)
        dl_ref[1] = (dyb * h * dge).astype(BF16)
        r = r_ref[...].astype(F32)
        gi = i_ref[...].astype(F32)
        xlb = xl_ref[...]
        xl = xlb.astype(F32)
        lam = lam_ref[...]
        ls = _log_sigmoid(lam)
        a, mult = _lru_gates(r, ls)
        c_scr[...] = _shift_up(a, 1)
        g_scr[...] = dyb * ge
        _scan_backward(c_scr, g_scr, g_scr)
        du = g_scr[...]
        da = du * _shift_down(h, 1)
        dmult = du * gi * xl
        di = du * mult * xl
        dxl = du * mult * gi
        first = _rows(a.shape) == 0
        dlog_a = da * a - jnp.where(first, 0.0, dmult * a * a / mult)
        dr = dlog_a * (LRU_C * ls)
        dlam_ref[...] = jnp.sum(dlog_a * r, axis=0, keepdims=True) * (LRU_C * (1.0 - _sigmoid(lam)))
        dzr = dr * r * (1.0 - r)
        dzi = di * gi * (1.0 - gi)
        dba_ref[...] = jnp.sum(dzr, axis=0, keepdims=True)
        dbx_ref[...] = jnp.sum(dzi, axis=0, keepdims=True)
        dzrb = dzr.astype(BF16)
        dzib = dzi.astype(BF16)
        dwa_ref[0] = _dot_tn(xlb, dzrb)
        dwx_ref[0] = _dot_tn(xlb, dzib)
        dxl = _advances(dxl + _dot_nt(dzrb, wa_ref[0]) + _dot_nt(dzib, wx_ref[0]), 4)
        dl_ref[0] = _taps_sum(dxl, wl_ref).astype(BF16)
        dwl_ref[...] = _conv_wgrad(dxl, lx_ref[...])
        dbl_ref[...] = jnp.sum(dxl[0], axis=0, keepdims=True)

    res = pl.BlockSpec((s, D_MODEL), lambda n: (0, 0))
    rows = pl.BlockSpec((CW, D_MODEL), lambda n: (n, 0))
    col = pl.BlockSpec((s, CW), lambda n: (0, n))
    blk = lambda k: pl.BlockSpec((s, CW), lambda n, k=k: (0, k * nblk + n))
    taps = pl.BlockSpec((4, CW), lambda n: (0, n))
    vec = pl.BlockSpec((1, CW), lambda n: (0, n))
    mat = pl.BlockSpec((1, CW, CW), lambda n: (n, 0, 0))
    hb = jax.ShapeDtypeStruct((s, D_MODEL), BF16)
    v = jax.ShapeDtypeStruct((1, D_MODEL), F32)
    m = jax.ShapeDtypeStruct((LRU_HEADS, HEAD_DIM, HEAD_DIM), F32)
    scr = pltpu.VMEM((s, CW), F32)
    return _call(
        body, name="mix_lru_bwd", grid=(nblk,),
        in_specs=[res, rows, blk(3), blk(4), col, col, col, col, taps, mat, mat, vec],
        out_specs=[pl.BlockSpec((2, s, CW), lambda n: (0, 0, n)), rows, mat, mat, vec, vec, taps, vec, vec],
        out_shape=[jax.ShapeDtypeStruct((2, s, D_MODEL), BF16), jax.ShapeDtypeStruct((D_MODEL, D_MODEL), BF16), m, m, v, v,
                   jax.ShapeDtypeStruct((4, D_MODEL), F32), v, v],
        scratch_shapes=[scr, scr],
        operands=(db, wlb, proj, proj, xl, r, i, h, wl, wa, wx, lam), ride=ride)


def norm_in_bwd(dh1, x, dx2, g1, ride=None):
    s, d = x.shape
    t = _token_tile(s)

    def body(dh_ref, x_ref, dx2_ref, g_ref, dx_ref, dg_ref):
        @pl.when(pl.program_id(0) == 0)
        def _():
            dg_ref[...] = jnp.zeros_like(dg_ref)

        n, r = _rms_stats(x_ref[...])
        dx, dg = _rms_bwd(n, r, g_ref[...], dh_ref[...])
        dx_ref[...] = dx2_ref[...] + dx
        dg_ref[...] += jnp.sum(dg, axis=0, keepdims=True)

    tile = pl.BlockSpec((t, d), lambda i: (i, 0))
    vec = pl.BlockSpec((1, d), lambda i: (0, 0))
    return _call(
        body, name="norm_in_bwd", grid=(s // t,),
        in_specs=[tile, tile, tile, vec],
        out_specs=[tile, vec],
        out_shape=[jax.ShapeDtypeStruct((s, d), F32), jax.ShapeDtypeStruct((1, d), F32)],
        operands=(dh1, x, dx2, g1), ride=ride)


def _owned_part(ref, kind, k, h, hr):
    if kind == "col":
        ns = ref.shape[1] // N_CHIPS
        return ref.at[pl.ds(h * hr, hr), pl.ds(k * ns, ns)]
    if kind == "row":
        return ref.at[pl.ds(k * 2 * hr + h * hr, hr), :]
    if kind == "col2":
        ns = ref.shape[2] // 2
        return ref.at[k // 2, pl.ds(h * hr, hr), pl.ds((k % 2) * ns, ns)]
    return ref.at[k, pl.ds(h * hr, hr), :]


def _part_shape(g, kind):
    if kind == "col2":
        return g.shape[1] // 2, g.shape[2] // 2
    if kind == "col":
        return g.shape[0] // 2, g.shape[1] // N_CHIPS
    if kind == "row":
        return g.shape[0] // (2 * N_CHIPS), g.shape[1]
    return g.shape[1] // 2, g.shape[2]


def pair_split(grads, kinds, name):
    n = len(grads)
    shapes = [_part_shape(g, k) for g, k in zip(grads, kinds)]

    def body(*refs):
        ins, theirs = refs[:n], refs[n:2 * n]
        send_sem, recv_sem = refs[2 * n:]
        x, y, c = _position()
        copies = []
        for a in range(n):
            hr = shapes[a][0]
            for k in range(N_CHIPS):
                s = a * N_CHIPS + k
                copies.append(pltpu.make_async_remote_copy(
                    src_ref=_owned_part(ins[a], kinds[a], k, 1 - c, hr), dst_ref=theirs[a].at[k],
                    send_sem=send_sem.at[s], recv_sem=recv_sem.at[s], device_id=(x, y, 1 - c), device_id_type=MESH))
        _handshake([(x, y, 1 - c)])
        for cp in copies:
            cp.start()
        for cp in copies:
            cp.wait()

    return pl.pallas_call(
        body, name=name,
        in_specs=[_HBM] * n, out_specs=[_HBM] * n,
        out_shape=[jax.ShapeDtypeStruct((N_CHIPS,) + shp, g.dtype) for shp, g in zip(shapes, grads)],
        scratch_shapes=[pltpu.SemaphoreType.DMA((n * N_CHIPS,))] * 2,
        compiler_params=pltpu.CompilerParams(collective_id=SIBLING),
    )(*grads)


def pair_swap(halves):
    n = len(halves)

    def body(*refs):
        ins, outs = refs[:n], refs[n:2 * n]
        send_sem, recv_sem = refs[2 * n:]
        x, y, c = _position()
        copies = [pltpu.make_async_remote_copy(
            src_ref=ins[a], dst_ref=outs[a], send_sem=send_sem.at[a], recv_sem=recv_sem.at[a],
            device_id=(x, y, 1 - c), device_id_type=MESH) for a in range(n)]
        _handshake([(x, y, 1 - c)])
        for cp in copies:
            cp.start()
        for cp in copies:
            cp.wait()

    return pl.pallas_call(
        body, name="pair_swap",
        in_specs=[_HBM] * n, out_specs=[_HBM] * n,
        out_shape=[jax.ShapeDtypeStruct(h.shape, h.dtype) for h in halves],
        scratch_shapes=[pltpu.SemaphoreType.DMA((n,))] * 2,
        compiler_params=pltpu.CompilerParams(collective_id=SIBLING),
    )(*halves)


def _row_tile(rows, cols, limit_bytes=1 << 20):
    best = None
    for t in range(SUBLANES, rows + 1, SUBLANES):
        if rows % t == 0 and t * cols * 4 <= limit_bytes:
            best = t
    return best or rows


def add_pair(g, kind, theirs, core, name):
    nc, rows, cols = theirs.shape
    t = _row_tile(rows, cols, 4 << 20)
    nt = rows // t

    def body(core_ref, g_ref, b_ref, o_ref):
        mine = g_ref[...].reshape(t, cols)
        o_ref[0] = (mine.astype(F32) + b_ref[0].astype(F32)).astype(o_ref.dtype)

    if kind == "col":
        own = pl.BlockSpec((t, cols), lambda k, i, c: (c[0] * nt + i, k))
    elif kind == "col2":
        own = pl.BlockSpec((1, t, cols), lambda k, i, c: (k // 2, c[0] * nt + i, k % 2))
    elif kind == "row":
        own = pl.BlockSpec((t, cols), lambda k, i, c: ((2 * k + c[0]) * nt + i, 0))
    else:
        own = pl.BlockSpec((1, t, cols), lambda k, i, c: (k, c[0] * nt + i, 0))
    spec = pl.BlockSpec((1, t, cols), lambda k, i, c: (k, i, 0))
    return pl.pallas_call(
        body, name=name,
        grid_spec=pltpu.PrefetchScalarGridSpec(num_scalar_prefetch=1, grid=(nc, nt), in_specs=[own, spec], out_specs=spec),
        out_shape=jax.ShapeDtypeStruct(theirs.shape, theirs.dtype), compiler_params=_params(),
    )(core, g, theirs)


def sum_lead(a, name):
    nl, rows, cols = a.shape
    t = _row_tile(rows, cols, (1 << 20) // 2)

    def body(a_ref, o_ref):
        acc = a_ref[0].astype(F32)
        for s in range(1, nl):
            acc = acc + a_ref[s].astype(F32)
        o_ref[...] = acc

    return pl.pallas_call(
        body, name=name, grid=(rows // t,),
        in_specs=[pl.BlockSpec((nl, t, cols), lambda i: (0, i, 0))],
        out_specs=pl.BlockSpec((t, cols), lambda i: (i, 0)),
        out_shape=jax.ShapeDtypeStruct((rows, cols), F32), compiler_params=_params(),
    )(a)


def sum_chips(rx, csum, chip, name):
    nc, rows, cols = rx.shape
    t = _row_tile(rows, cols, 2 << 20)

    def body(chip_ref, r0, r1, r2, r3, own_ref, o_ref):
        acc = None
        for s, ref in enumerate((r0, r1, r2, r3)):
            term = jnp.where(chip_ref[0] == s, own_ref[0], ref[0]).astype(F32)
            acc = term if acc is None else acc + term
        o_ref[...] = acc

    def slot(s):
        return pl.BlockSpec((1, t, cols), lambda i, c, s=s: (jnp.where(c[0] == s, c[0] ^ 1, s), i, 0))

    return pl.pallas_call(
        body, name=name,
        grid_spec=pltpu.PrefetchScalarGridSpec(
            num_scalar_prefetch=1, grid=(rows // t,),
            in_specs=[slot(s) for s in range(nc)] + [pl.BlockSpec((1, t, cols), lambda i, c: (c[0], i, 0))],
            out_specs=pl.BlockSpec((t, cols), lambda i, c: (i, 0))),
        out_shape=jax.ShapeDtypeStruct((rows, cols), F32), compiler_params=_params(),
    )(chip, rx, rx, rx, rx, csum)


def cast_bf16(a, name):
    rows, cols = a.shape
    t = _row_tile(rows, cols, 2 << 20)

    def body(i_ref, o_ref):
        o_ref[...] = i_ref[...].astype(BF16)

    spec = pl.BlockSpec((t, cols), lambda i: (i, 0))
    return pl.pallas_call(body, name=name, grid=(rows // t,), in_specs=[spec], out_specs=spec,
                          out_shape=jax.ShapeDtypeStruct((rows, cols), BF16), compiler_params=_params())(a)


def _adamw_update(w, g, m, v):
    nm = ADAM_B1 * m + (1.0 - ADAM_B1) * g
    nv = ADAM_B2 * v + (1.0 - ADAM_B2) * (g * g)
    m_hat = nm * (1.0 / (1.0 - ADAM_B1 ** ADAM_STEP))
    v_hat = nv * (1.0 / (1.0 - ADAM_B2 ** ADAM_STEP))
    return -ADAM_LR * (m_hat / (jnp.sqrt(v_hat) + ADAM_EPS) + ADAM_WD * w), nm, nv


def adamw(w, g, m, v, name):
    rows, cols = w.shape
    t = _row_tile(rows, cols)

    def body(w_ref, g_ref, m_ref, v_ref, d_ref, nm_ref, nv_ref):
        d_ref[...], nm_ref[...], nv_ref[...] = _adamw_update(w_ref[...], g_ref[...], m_ref[...], v_ref[...])

    spec = pl.BlockSpec((t, cols), lambda i: (i, 0))
    shp = jax.ShapeDtypeStruct((rows, cols), F32)
    return pl.pallas_call(
        body, name=name, grid=(rows // t,), in_specs=[spec] * 4, out_specs=[spec] * 3,
        out_shape=[shp, shp, shp], compiler_params=_params(),
    )(w, g, m, v)


def adamw_halves(w, g_mine, g_other, m, v, core, name):
    rows, cols = w.shape
    hr = rows // 2
    t = _row_tile(hr, cols)
    nt = hr // t

    def body(core_ref, w_ref, gm_ref, go_ref, m_ref, v_ref, g_ref, d_ref, nm_ref, nv_ref):
        g = jnp.where(pl.program_id(0) // nt == core_ref[0], gm_ref[...], go_ref[...])
        g_ref[...] = g
        d_ref[...], nm_ref[...], nv_ref[...] = _adamw_update(w_ref[...], g, m_ref[...], v_ref[...])

    spec = pl.BlockSpec((t, cols), lambda i, c: (i, 0))
    half = pl.BlockSpec((t, cols), lambda i, c: (i % nt, 0))
    shp = jax.ShapeDtypeStruct((rows, cols), F32)
    return pl.pallas_call(
        body, name=name,
        grid_spec=pltpu.PrefetchScalarGridSpec(num_scalar_prefetch=1, grid=(2 * nt,),
                                               in_specs=[spec, half, half, spec, spec], out_specs=[spec] * 4),
        out_shape=[shp] * 4, compiler_params=_params(),
    )(core, w, g_mine, g_other, m, v)


WEIGHTS = ("norm_mix_pre", "norm_mix_post", "norm_ffn_pre", "norm_ffn_post", "w_in", "conv_short_w",
           "w_conv_branch", "lru_conv_w", "lru_conv_b", "lru_wa", "lru_ba", "lru_wx", "lru_bx", "lru_lambda",
           "w_lru_branch", "w_out", "ffn_w_up", "ffn_conv_w", "ffn_conv_b", "ffn_w_down")
BIG = ("w_in", "ffn_w_up", "w_conv_branch", "w_lru_branch", "w_out", "ffn_w_down")
BIG_KIND = ("col", "col", "row", "row", "row", "row")
SMALL = ("conv_short_w", "lru_conv_w", "lru_wa", "lru_ba", "lru_wx", "lru_bx", "ffn_conv_w")
REPL = ("norm_mix_pre", "norm_mix_post", "norm_ffn_pre", "norm_ffn_post", "lru_conv_b", "lru_lambda", "ffn_conv_b")
PACK_W = 256
SMALL_ROWS = 576
REPL_ROWS = 16
LOSS_ROW = 12
FFN_SHARD = 2 * D_FF // N_CHIPS
QUARTER = HEAD_DIM // N_CHIPS
SMALL_PARTS = (("conv_short_w", 3, (1, 3, PACK_W)), ("lru_conv_w", 4, (1, 4, PACK_W)),
               ("lru_wa", LRU_HEADS * QUARTER, (1, LRU_HEADS, QUARTER, HEAD_DIM)), ("lru_ba", LRU_HEADS, (1, LRU_HEADS, QUARTER)),
               ("lru_wx", LRU_HEADS * QUARTER, (1, LRU_HEADS, QUARTER, HEAD_DIM)), ("lru_bx", LRU_HEADS, (1, LRU_HEADS, QUARTER)),
               ("ffn_conv_w", 3 * FFN_SHARD // PACK_W, (1, 3, FFN_SHARD)))


def _pad8(nr):
    return -(-nr // SUBLANES) * SUBLANES


SMALL_OFFSET = {}
for _name, _nr, _ in SMALL_PARTS:
    SMALL_OFFSET[_name] = sum(_pad8(nr) for n, nr, _ in SMALL_PARTS[:len(SMALL_OFFSET)])
FFN_ROWS = FFN_SHARD // PACK_W
BIASES = ("lru_ba", "lru_bx")
TAPS3 = ("conv_short_w", "ffn_conv_w")


def pack_small(dicts):
    names = [n for n, _, _ in SMALL_PARTS]
    operands = [d[n].transpose(1, 0, 2) if n in TAPS3 else d[n] for d in dicts for n in names]

    def body(*refs):
        ins, outs = refs[:len(operands)], refs[len(operands):]
        for i, o in enumerate(outs):
            o[...] = jnp.zeros_like(o)
            for (name, nr, shape), p in zip(SMALL_PARTS, ins[i * len(names):(i + 1) * len(names)]):
                r0 = SMALL_OFFSET[name]
                if name in BIASES:
                    o[r0:r0 + nr, 0:QUARTER] = p[0]
                elif name == "ffn_conv_w":
                    for k in range(shape[1]):
                        for s in range(FFN_ROWS):
                            o[r0 + FFN_ROWS * k + s:r0 + FFN_ROWS * k + s + 1, :] = p[k, :, s * PACK_W:(s + 1) * PACK_W]
                elif name == "conv_short_w":
                    for k in range(nr):
                        o[r0 + k:r0 + k + 1, :] = p[k]
                else:
                    o[r0:r0 + nr, :] = p[0].reshape(nr, PACK_W)

    shape = jax.ShapeDtypeStruct((SMALL_ROWS, PACK_W), F32)
    return pl.pallas_call(body, name="pack_small", out_shape=[shape] * len(dicts), compiler_params=_params())(*operands)


def full_small(g4):
    def body(p, csw, lcw, wa, wx, fcw):
        chips = range(N_CHIPS)
        r0 = SMALL_OFFSET["conv_short_w"]
        csw[...] = jnp.concatenate([p[c, r0:r0 + 3, :] for c in chips], axis=1)
        r0 = SMALL_OFFSET["lru_conv_w"]
        lcw[...] = jnp.concatenate([p[c, r0:r0 + 4, :] for c in chips], axis=1)
        for name, o in (("lru_wa", wa), ("lru_wx", wx)):
            r0 = SMALL_OFFSET[name]
            for h in range(LRU_HEADS):
                for c in chips:
                    o[h, c * QUARTER:(c + 1) * QUARTER, :] = p[c, r0 + h * QUARTER:r0 + (h + 1) * QUARTER, :].astype(BF16)
        r0 = SMALL_OFFSET["ffn_conv_w"]
        for k in range(3):
            fcw[k:k + 1, :] = jnp.concatenate(
                [p[c, r0 + FFN_ROWS * k + s:r0 + FFN_ROWS * k + s + 1, :] for c in chips for s in range(FFN_ROWS)], axis=1)

    mat = jax.ShapeDtypeStruct((LRU_HEADS, HEAD_DIM, HEAD_DIM), BF16)
    csw, lcw, wa, wx, fcw = pl.pallas_call(
        body, name="full_small",
        out_shape=[jax.ShapeDtypeStruct((3, D_MODEL), F32), jax.ShapeDtypeStruct((4, D_MODEL), F32), mat, mat,
                   jax.ShapeDtypeStruct((3, 2 * D_FF), F32)],
        compiler_params=_params())(g4)

    def bias(name):
        r0 = SMALL_OFFSET[name]
        return g4[:, r0:r0 + LRU_HEADS, :QUARTER].transpose(1, 0, 2).reshape(1, D_MODEL)

    return dict(conv_short_w=csw, lru_conv_w=lcw, lru_wa=wa, lru_wx=wx, ffn_conv_w=fcw,
                lru_ba=bias("lru_ba"), lru_bx=bias("lru_bx"))


def split_small(full):
    def bias(name):
        return full[name].reshape(LRU_HEADS, N_CHIPS, QUARTER).transpose(1, 0, 2)

    def body(csw, lcw, wa, wx, fcw, ba, bx, o):
        o[...] = jnp.zeros_like(o)
        for c in range(N_CHIPS):
            cols = slice(c * PACK_W, (c + 1) * PACK_W)
            r0 = SMALL_OFFSET["conv_short_w"]
            o[c, r0:r0 + 3, :] = csw[:, cols]
            r0 = SMALL_OFFSET["lru_conv_w"]
            o[c, r0:r0 + 4, :] = lcw[:, cols]
            for name, p in (("lru_wa", wa), ("lru_wx", wx)):
                r0 = SMALL_OFFSET[name]
                for h in range(LRU_HEADS):
                    o[c, r0 + h * QUARTER:r0 + (h + 1) * QUARTER, :] = p[h, c * QUARTER:(c + 1) * QUARTER, :]
            for name, p in (("lru_ba", ba), ("lru_bx", bx)):
                r0 = SMALL_OFFSET[name]
                o[c, r0:r0 + LRU_HEADS, 0:QUARTER] = p[c]
            r0 = SMALL_OFFSET["ffn_conv_w"]
            for k in range(3):
                for s in range(FFN_ROWS):
                    lo = c * FFN_SHARD + s * PACK_W
                    o[c, r0 + FFN_ROWS * k + s:r0 + FFN_ROWS * k + s + 1, :] = fcw[k:k + 1, lo:lo + PACK_W]

    return pl.pallas_call(
        body, name="split_small", out_shape=jax.ShapeDtypeStruct((N_CHIPS, SMALL_ROWS, PACK_W), F32),
        compiler_params=_params(),
    )(full["conv_short_w"], full["lru_conv_w"], full["lru_wa"], full["lru_wx"], full["ffn_conv_w"],
      bias("lru_ba"), bias("lru_bx"))


def pack_repl(dicts, loss=None):
    operands = [d[n] for d in dicts for n in REPL] + ([loss] if loss is not None else [])

    def body(*refs):
        ins, outs = refs[:len(operands)], refs[len(operands):]
        for i, o in enumerate(outs):
            o[...] = jnp.zeros_like(o)
            r0 = 0
            for p in ins[i * len(REPL):(i + 1) * len(REPL)]:
                for s in range(p.shape[1] // D_MODEL):
                    o[r0:r0 + 1, :] = p[:, s * D_MODEL:(s + 1) * D_MODEL]
                    r0 += 1
        if loss is not None:
            outs[-1][LOSS_ROW:LOSS_ROW + 1, :] = jnp.tile(ins[-1][...], (1, D_MODEL // 128))

    shape = jax.ShapeDtypeStruct((REPL_ROWS, D_MODEL), F32)
    return pl.pallas_call(body, name="pack_repl" + ("_loss" if loss is not None else ""),
                          out_shape=[shape] * len(dicts), compiler_params=_params())(*operands)


def _lane_concat(ref, r0, n):
    return jnp.concatenate([ref[r0 + s:r0 + s + 1, :] for s in range(n)], axis=1)


def unpack_small(packs):
    names = [n for n, _, _ in SMALL_PARTS]

    def body(*refs):
        ins, outs = refs[:len(packs)], refs[len(packs):]
        for i, p in enumerate(ins):
            for (name, nr, shape), o in zip(SMALL_PARTS, outs[i * len(names):(i + 1) * len(names)]):
                r0 = SMALL_OFFSET[name]
                if name in BIASES:
                    o[0] = p[r0:r0 + nr, 0:QUARTER]
                elif name == "ffn_conv_w":
                    for k in range(shape[1]):
                        o[k] = _lane_concat(p, r0 + FFN_ROWS * k, FFN_ROWS)
                elif name == "conv_short_w":
                    for k in range(nr):
                        o[k] = p[r0 + k:r0 + k + 1, :]
                else:
                    o[0] = p[r0:r0 + nr, :].reshape(shape[1:])

    shapes = [jax.ShapeDtypeStruct((s[1], 1, s[2]) if n in TAPS3 else s, F32) for n, _, s in SMALL_PARTS]
    res = pl.pallas_call(body, name="unpack_small", out_shape=shapes * len(packs), compiler_params=_params())(*packs)
    out = []
    for i in range(len(packs)):
        d = dict(zip(names, res[i * len(names):(i + 1) * len(names)]))
        for n in TAPS3:
            d[n] = d[n].transpose(1, 0, 2)
        out.append(d)
    return out


def unpack_repl(packs):
    rows = [(2 * D_FF // D_MODEL) if n == "ffn_conv_b" else 1 for n in REPL]

    def body(*refs):
        ins, outs = refs[:len(packs)], refs[len(packs):]
        for i, p in enumerate(ins):
            r0 = 0
            for nr, o in zip(rows, outs[i * len(REPL):(i + 1) * len(REPL)]):
                o[...] = _lane_concat(p, r0, nr)
                r0 += nr

    shapes = [jax.ShapeDtypeStruct((1, nr * D_MODEL), F32) for nr in rows]
    res = pl.pallas_call(body, name="unpack_repl", out_shape=shapes * len(packs), compiler_params=_params())(*packs)
    return [dict(zip(REPL, res[i * len(REPL):(i + 1) * len(REPL)])) for i in range(len(packs))]


def kernel(x, norm_mix_pre, norm_mix_post, norm_ffn_pre, norm_ffn_post, w_in, conv_short_w, w_conv_branch, lru_conv_w, lru_conv_b, lru_wa, lru_ba, lru_wx, lru_bx, lru_lambda, w_lru_branch, w_out, ffn_w_up, ffn_conv_w, ffn_conv_b, ffn_w_down, loss_target, m_norm_mix_pre, m_norm_mix_post, m_norm_ffn_pre, m_norm_ffn_post, m_w_in, m_conv_short_w, m_w_conv_branch, m_lru_conv_w, m_lru_conv_b, m_lru_wa, m_lru_ba, m_lru_wx, m_lru_bx, m_lru_lambda, m_w_lru_branch, m_w_out, m_ffn_w_up, m_ffn_conv_w, m_ffn_conv_b, m_ffn_w_down, v_norm_mix_pre, v_norm_mix_post, v_norm_ffn_pre, v_norm_ffn_post, v_w_in, v_conv_short_w, v_w_conv_branch, v_lru_conv_w, v_lru_conv_b, v_lru_wa, v_lru_ba, v_lru_wx, v_lru_bx, v_lru_lambda, v_w_lru_branch, v_w_out, v_ffn_w_up, v_ffn_conv_w, v_ffn_conv_b, v_ffn_w_down):
    given = dict(locals())
    w = {n: given[n] for n in WEIGHTS}
    m = {n: given["m_" + n] for n in WEIGHTS}
    v = {n: given["v_" + n] for n in WEIGHTS}

    xi, yi, ci = _position()
    chip_i = 2 * xi + yi
    chip = chip_i.astype(jnp.int32).reshape(1)
    core = ci.astype(jnp.int32).reshape(1)
    xs, target = x[0], loss_target[0]
    g1, g2, g3, g4 = w["norm_mix_pre"], w["norm_mix_post"], w["norm_ffn_pre"], w["norm_ffn_post"]
    shard = {n: cast_bf16(w[n][0], "cast_" + n) for n in BIG}
    small_shard, m_small, v_small = pack_small([w, m, v])

    def gathered(bufs, names):
        return [_own_slot(b, small_shard if n == "small" else shard[n], chip_i) for b, n in zip(bufs, names)]

    def chip_sums(arrays, kinds, tag):
        theirs = pair_split(arrays, kinds, "pair_split_" + tag)
        return [add_pair(g, k, t, core, "pair_add_%s_%d" % (tag, i)) for i, (g, k, t) in enumerate(zip(arrays, kinds, theirs))]

    h1, h1t = norm_in(xs, g1)
    win4, small4 = gathered(run_ride(gather_ride([shard["w_in"], small_shard]), "gather_first"), ("w_in", "small"))
    small = full_small(small4)
    first_up = 256
    (proj,), got = matmul_cols(
        h1, win4, "proj_fwd",
        ride=gather_ride([shard["w_conv_branch"], shard["w_lru_branch"], shard["w_out"], shard["ffn_w_up"]],
                         items=[(0, 0, 256), (1, 0, 256), (2, 0, 256), (3, 0, first_up)]))
    wcb, wlb, wout = [g.reshape(-1, D_MODEL) for g in gathered(got[:3], ("w_conv_branch", "w_lru_branch", "w_out"))]
    got = got[3:]
    up_piece = lambda r0, nr, into=None: gather_ride([shard["ffn_w_up"]], items=[(0, r0, nr)], into=into)
    down_piece = lambda r0, nr, into=None: gather_ride([shard["ffn_w_down"]], items=[(0, r0, nr)], into=into)
    q, ya = mix_conv_fwd(proj, small["conv_short_w"])
    (xl, r, gi, h, yb), got = mix_lru_fwd(
        proj, small["lru_conv_w"], w["lru_conv_b"], small["lru_wa"], small["lru_ba"],
        small["lru_wx"], small["lru_bx"], w["lru_lambda"], ride=up_piece(first_up, 640, got))
    (a, b, merged), got = branch_merge_fwd(ya, yb, wcb, wlb, proj, ride=up_piece(first_up + 640, 128, got))
    (wup4,) = gathered(got, ("ffn_w_up",))
    (mix, x2, h2, h2t), got = mix_out_fwd(merged, wout, xs, g2, g3, ride=down_piece(0, 256))
    (up, act, f), got = ffn_up_act_fwd(h2, wup4, small["ffn_conv_w"], w["ffn_conv_b"], ride=down_piece(256, 512, got))
    wdown = gathered(got, ("ffn_w_down",))[0].reshape(-1, D_MODEL)
    dy, dout, loss, dg4 = ffn_down_loss(f, wdown, x2, target, g4)

    dh2, dwup, dwdown, dfw, dfb = ffn_up_bwd(dout, wdown, up, act, f, small["ffn_conv_w"], wup4, h2t)
    cs_down, cs_up = chip_sums([dwdown, dwup], ["row", "col2"], "ffn")
    down_rows = lambda r0, nr, into=None: exchange_ride([cs_down], items=[(0, r0, nr)], into=into)
    up_rows = lambda r0, nr, into=None: exchange_ride([cs_up], items=[(0, r0, nr)], into=into)
    (dx2, dmix, dg3, dg2), rx_down = norms_mid_bwd(dh2, x2, dy, mix, g3, g2, ride=down_rows(0, 128))
    (da, db, dwout, dgates), rx_down = mix_out_bwd(dmix, wout, merged, a, b, proj, ride=down_rows(128, 256, rx_down))
    (dconv, dwcb, dws), rx_up = mix_conv_bwd(da, wcb, proj, q, small["conv_short_w"], ride=up_rows(0, 176))
    cs_mid = chip_sums([dwout, dwcb], ["row", "row"], "mid")
    (dlru, dwlb, dwa, dwx, dba, dbx, dwl, dbl, dlam), rx_all = mix_lru_bwd(
        db, wlb, proj, xl, r, gi, h, small["lru_conv_w"], small["lru_wa"], small["lru_wx"], w["lru_lambda"],
        ride=exchange_ride([cs_up] + cs_mid, items=[(0, 176, 336), (1, 0, 128), (2, 0, 128)], into=rx_up + [None, None]))
    rx_up, rx_mid = rx_all[:1], rx_all[1:]
    grads = dict(norm_mix_post=dg2, norm_ffn_pre=dg3, norm_ffn_post=dg4, conv_short_w=dws, lru_conv_w=dwl,
                 lru_conv_b=dbl, lru_wa=dwa, lru_ba=dba, lru_wx=dwx, lru_bx=dbx, lru_lambda=dlam,
                 ffn_conv_w=jnp.concatenate([dfw[0], dfw[1]], axis=1), ffn_conv_b=jnp.concatenate([dfb[0], dfb[1]], axis=1))
    cs_late = chip_sums([dwlb, split_small(grads)], ["row", "lead"], "late")
    dproj = [dconv, dlru, dgates]
    (dwin,), rx_late = matmul_cols_bwd(dproj, h1t, "proj_wgrad", True, ride=exchange_ride(cs_late))
    cs_in = chip_sums([dwin], ["col"], "in")
    in_rows = lambda r0, nr, into=None: exchange_ride(cs_in, items=[(0, r0, nr)], into=into)
    (dh1,), rx_in = matmul_cols_bwd(dproj, win4, "proj_dgrad", False, ride=in_rows(0, 384))
    (dx, grads["norm_mix_pre"]), rx_in = norm_in_bwd(dh1, xs, dx2, g1, ride=in_rows(384, 128, rx_in))
    rx_in = rx_in[0]
    (rep_part,) = pack_repl([grads], loss)
    (rep_all,) = run_ride(exchange_ride([], rep=rep_part), "exchange_repl")

    order = (("w_in", cs_in[0], rx_in), ("ffn_w_up", cs_up, rx_up[0]), ("w_conv_branch", cs_mid[1], rx_mid[1]),
             ("w_lru_branch", cs_late[0], rx_late[0]), ("w_out", cs_mid[0], rx_mid[0]),
             ("ffn_w_down", cs_down, rx_down[0]), ("small", cs_late[1], rx_late[1]))
    halves = [sum_chips(rx, cs, chip, "chip_sum_" + n) for n, cs, rx in order]
    me = 4 * xi + 2 * yi + ci
    rep_grad = sum_lead(_own_slot(rep_all, rep_part, me), "device_sum")
    others = pair_swap(halves)

    g_out, d_out, m_out, v_out = {}, {}, {}, {}
    for n, gm, go in zip(BIG, halves[:-1], others[:-1]):
        g, d, nm, nv = adamw_halves(w[n][0], gm, go, m[n][0], v[n][0], core, "adamw_" + n)
        g_out[n], d_out[n], m_out[n], v_out[n] = g[None], d[None], nm[None], nv[None]
    bufs = adamw_halves(small_shard, halves[-1], others[-1], m_small, v_small, core, "adamw_small")
    for dst, part in zip((g_out, d_out, m_out, v_out), unpack_small(bufs)):
        dst.update(part)
    w_rep, m_rep, v_rep = pack_repl([w, m, v])
    d, nm, nv = adamw(w_rep, rep_grad, m_rep, v_rep, "adamw_repl")
    for dst, part in zip((g_out, d_out, m_out, v_out), unpack_repl([rep_grad, d, nm, nv])):
        dst.update(part)

    return (rep_grad[LOSS_ROW, 0], dx[None], *[g_out[n] for n in WEIGHTS], *[d_out[n] for n in WEIGHTS],
            *[m_out[n] for n in WEIGHTS], *[v_out[n] for n in WEIGHTS])
```

```python
import functools
import math

import jax
import jax.numpy as jnp
from jax import lax
from jax.experimental import pallas as pl
from jax.experimental.pallas import tpu as pltpu

F32 = jnp.float32
BF16 = jnp.bfloat16

D_MODEL = 1024
N_CHIPS = 4
N_SEG = 7
D_FF = 3 * D_MODEL
LRU_HEADS = 4
HEAD_DIM = D_MODEL // LRU_HEADS
LRU_C = 8.0
RMS_EPS = 1e-6
CW = 256
FW = 256
SUBLANES = 8
SCAN_UNROLL = 8
VMEM_LIMIT = 58 * 1024 * 1024

ADAM_LR = 0.001
ADAM_B1 = 0.9
ADAM_B2 = 0.999
ADAM_EPS = 1e-08
ADAM_WD = 0.01
ADAM_STEP = 10

_GELU_C = math.sqrt(2.0 / math.pi)
_GELU_K = 0.044715


def _params(**kw):
    return pltpu.CompilerParams(vmem_limit_bytes=VMEM_LIMIT, **kw)


def _sigmoid(x):
    return 1.0 / (1.0 + jnp.exp(-x))


def _gelu(x):
    t = jnp.tanh(_GELU_C * (x + _GELU_K * x * x * x))
    return 0.5 * x * (1.0 + t)


def _gelu_and_grad(x):
    x2 = x * x
    t = jnp.tanh(_GELU_C * (x + _GELU_K * x * x2))
    g = 0.5 * x * (1.0 + t)
    dg = 0.5 * (1.0 + t) + 0.5 * x * (1.0 - t * t) * _GELU_C * (1.0 + 3.0 * _GELU_K * x2)
    return g, dg


def _log_sigmoid(x):
    e = jnp.exp(-jnp.abs(x))
    u = 1.0 + e
    l1p = jnp.where(u == 1.0, e, jnp.log(u) * e / (u - 1.0))
    return jnp.minimum(x, 0.0) - l1p


def _neg_expm1(z):
    series = -z * (1.0 + z * (0.5 + z * (1.0 / 6.0 + z * (1.0 / 24.0 + z * (1.0 / 120.0 + z * (1.0 / 720.0))))))
    return jnp.where(z > -0.2, series, 1.0 - jnp.exp(z))


def _rows(shape):
    return lax.broadcasted_iota(jnp.int32, shape, 0)


def _shift_down(x, k):
    return jnp.where(_rows(x.shape) >= k, pltpu.roll(x, k, 0), 0.0)


def _shift_up(x, k):
    n = x.shape[0]
    return jnp.where(_rows(x.shape) < n - k, pltpu.roll(x, n - k, 0), 0.0)


def _delays(x, k_width):
    return [x] + [_shift_down(x, j) for j in range(1, k_width)]


def _advances(dy, k_width):
    return [dy] + [_shift_up(dy, j) for j in range(1, k_width)]


def _taps_sum(shifted, w_ref, b=None):
    k_width = w_ref.shape[0]
    y = w_ref[k_width - 1:k_width, :] * shifted[0]
    for j in range(1, k_width):
        y = y + w_ref[k_width - 1 - j:k_width - j, :] * shifted[j]
    if b is not None:
        y = y + b
    return y


def _causal_conv(x, w_ref, b=None):
    return _taps_sum(_delays(x, w_ref.shape[0]), w_ref, b)


def _conv_wgrad(advanced, x):
    k_width = len(advanced)
    rows = [jnp.sum(advanced[k_width - 1 - k] * x, axis=0, keepdims=True) for k in range(k_width)]
    return jnp.concatenate(rows, axis=0)


def _dot(a, b):
    return jnp.dot(a, b, preferred_element_type=F32)


def _dot_nt(a, b):
    return lax.dot_general(a, b, (((1,), (1,)), ((), ())), preferred_element_type=F32)


def _dot_tn(a, b):
    return lax.dot_general(a, b, (((0,), (0,)), ((), ())), preferred_element_type=F32)


def _rms_stats(x):
    r = lax.rsqrt(jnp.mean(x * x, axis=-1, keepdims=True) + RMS_EPS)
    return x * r, r


def _rms_bwd(n, r, g, dy):
    dn = dy * g
    dx = r * (dn - n * jnp.mean(dn * n, axis=-1, keepdims=True))
    return dx, dy * n


def _scan(a_ref, b_ref, h_ref, reverse):
    n, c = a_ref.shape
    row = lax.broadcasted_iota(jnp.int32, (SUBLANES, c), 0)
    span = SCAN_UNROLL * SUBLANES
    n_trips = n // span

    def within(a, b):
        for k in (1, 2, 4):
            if reverse:
                keep, shift = row < SUBLANES - k, SUBLANES - k
            else:
                keep, shift = row >= k, k
            ap = jnp.where(keep, pltpu.roll(a, shift, 0), 1.0)
            bp = jnp.where(keep, pltpu.roll(b, shift, 0), 0.0)
            b = a * bp + b
            a = a * ap
        return a, b

    def trip(t, carry):
        base = pl.multiple_of((n_trips - 1 - t if reverse else t) * span, span)
        order = list(reversed(range(SCAN_UNROLL))) if reverse else list(range(SCAN_UNROLL))
        loaded = [(a_ref[pl.ds(base + u * SUBLANES, SUBLANES), :], b_ref[pl.ds(base + u * SUBLANES, SUBLANES), :])
                  for u in order]
        out = []
        for a, b in [within(a, b) for a, b in loaded]:
            h = a * carry + b
            out.append(h)
            carry = h[0:1, :] if reverse else h[SUBLANES - 1:SUBLANES, :]
        for u, h in zip(order, out):
            h_ref[pl.ds(base + u * SUBLANES, SUBLANES), :] = h
        return carry

    lax.fori_loop(0, n_trips, trip, jnp.zeros((1, c), F32))


def _scan_forward(a_ref, b_ref, h_ref):
    _scan(a_ref, b_ref, h_ref, False)


def _scan_backward(c_ref, b_ref, g_ref):
    _scan(c_ref, b_ref, g_ref, True)


MESH = pl.DeviceIdType.MESH
_HBM = pl.BlockSpec(memory_space=pltpu.HBM)
_OTHER_CHIPS = ((1, 0), (0, 1), (1, 1))
_OTHER_DEVICES = tuple((dx, dy, dc) for dx in (0, 1) for dy in (0, 1) for dc in (0, 1) if dx or dy or dc)
N_DEVICES = 8


def _position():
    return lax.axis_index("x"), lax.axis_index("y"), lax.axis_index("c")


def _flip(v, d):
    return 1 - v if d else v


def _chip(x, y, p):
    px, py = _flip(x, _OTHER_CHIPS[p][0]), _flip(y, _OTHER_CHIPS[p][1])
    return px, py, 2 * px + py


class _Ride:
    def __init__(self, srcs, bufs, scratch, plan, collective_id):
        self.srcs, self.bufs, self.scratch, self.plan = list(srcs), list(bufs), list(scratch), plan
        self.collective_id = collective_id


NEIGHBOURS_AND_SIBLING = 1
OTHER_CHIPS_SAME_CORE = 2
ALL_DEVICES = 3
SIBLING = 4


def _handshake(peers):
    barrier = pltpu.get_barrier_semaphore()
    for peer in peers:
        pl.semaphore_signal(barrier, inc=1, device_id=peer, device_id_type=MESH)
    pl.semaphore_wait(barrier, len(peers))


def _call(body, *, name, grid, in_specs, out_specs, out_shape, operands, scratch_shapes=(), ride=None):
    in_specs, out_specs, out_shape = list(in_specs), list(out_specs), list(out_shape)
    scratch_shapes = list(scratch_shapes)
    if ride is None:
        return pl.pallas_call(body, name=name, grid=grid, in_specs=in_specs, out_specs=out_specs, out_shape=out_shape,
                              scratch_shapes=scratch_shapes, compiler_params=_params())(*operands)
    n_in, n_out, n_scr = len(in_specs), len(out_shape), len(scratch_shapes)
    old = [i for i, b in enumerate(ride.bufs) if not isinstance(b, jax.ShapeDtypeStruct)]
    n_src, n_old, n_buf = len(ride.srcs), len(old), len(ride.bufs)

    def full_body(*refs):
        o0 = n_in + n_src + n_old
        s0 = o0 + n_out + n_buf
        start, relay, relay_on, finish = ride.plan(refs[n_in:n_in + n_src], refs[o0 + n_out:s0], refs[s0 + n_scr:])
        ids = [pl.program_id(i) for i in range(len(grid))]
        first = functools.reduce(jnp.logical_and, [i == 0 for i in ids])
        middle = functools.reduce(jnp.logical_and, [ids[0] == grid[0] // 2] + [i == 0 for i in ids[1:]])
        last = functools.reduce(jnp.logical_and, [i == g - 1 for i, g in zip(ids, grid)])
        pl.when(first)(start)
        pl.when(middle)(relay)
        pl.when(last)(relay_on)
        body(*refs[:n_in], *refs[o0:o0 + n_out], *refs[s0:s0 + n_scr])
        pl.when(last)(finish)

    shapes = [jax.ShapeDtypeStruct(b.shape, b.dtype) for b in ride.bufs]
    res = pl.pallas_call(
        full_body, name=name, grid=grid,
        in_specs=in_specs + [_HBM] * (n_src + n_old), out_specs=out_specs + [_HBM] * n_buf,
        out_shape=out_shape + shapes, scratch_shapes=scratch_shapes + ride.scratch,
        input_output_aliases={n_in + n_src + k: n_out + i for k, i in enumerate(old)},
        compiler_params=_params(collective_id=ride.collective_id),
    )(*operands, *ride.srcs, *[ride.bufs[i] for i in old])
    return list(res[:n_out]), list(res[n_out:])


def run_ride(ride, name):
    def body():
        pass

    return _call(body, name=name, grid=(1,), in_specs=[], out_specs=[], out_shape=[], operands=[], ride=ride)[1]


def gather_ride(shards, items=None, into=None):
    items = items or [(a, 0, s.shape[0]) for a, s in enumerate(shards)]
    bufs = into or [jax.ShapeDtypeStruct((N_CHIPS,) + s.shape, s.dtype) for s in shards]
    nrel = len(_OTHER_CHIPS)

    def plan(srcs, dsts, sems):
        ici_send, ici_recv, hop_send, hop_recv, sib_send, sib_recv = sems
        x, y, c = _position()
        j = 2 * x + y

        def rows(ref, it, h, q=None):
            half = it[2] // 2
            if q is None:
                return ref.at[pl.ds(it[1] + h * half, half), :]
            return ref.at[pl.ds(it[1] + h * half + q * (half // 2), half // 2), :]

        def ici(i, p, slot):
            it = items[i]
            px, py, _ = _chip(x, y, p)
            return pltpu.make_async_remote_copy(
                src_ref=rows(srcs[it[0]], it, c), dst_ref=rows(dsts[it[0]].at[slot], it, c),
                send_sem=ici_send.at[i * nrel + p], recv_sem=ici_recv.at[i * nrel + p],
                device_id=(px, py, c), device_id_type=MESH)

        def hop(i, p, slot):
            it = items[i]
            part = rows(dsts[it[0]].at[slot], it, c, p)
            px, py, _ = _chip(x, y, 1 - p)
            return pltpu.make_async_remote_copy(
                src_ref=part, dst_ref=part, send_sem=hop_send.at[i * 2 + p], recv_sem=hop_recv.at[i * 2 + p],
                device_id=(px, py, c), device_id_type=MESH)

        def sib(i, p, h):
            it = items[i]
            part = rows(dsts[it[0]].at[_chip(x, y, p)[2]], it, h)
            return pltpu.make_async_remote_copy(
                src_ref=part, dst_ref=part, send_sem=sib_send.at[i * nrel + p], recv_sem=sib_recv.at[i * nrel + p],
                device_id=(x, y, 1 - c), device_id_type=MESH)

        every = range(len(items))
        diag = _chip(x, y, 2)[2]

        def start():
            _handshake([_chip(x, y, 0)[:2] + (c,), _chip(x, y, 1)[:2] + (c,), (x, y, 1 - c)])
            for i in every:
                for p in (0, 1):
                    ici(i, p, j).start()

        def relay():
            for i in every:
                for p in (0, 1):
                    k = _chip(x, y, p)[2]
                    ici(i, p, k).wait_recv()
                    hop(i, p, k).start()
                    sib(i, p, c).start()

        def relay_on():
            for i in every:
                for p in (0, 1):
                    hop(i, p, diag).wait_recv()
                sib(i, 2, c).start()

        def finish():
            for i in every:
                for p in range(nrel):
                    sib(i, p, 1 - c).wait_recv()
            for i in every:
                for p in (0, 1):
                    ici(i, p, j).wait_send()
                    hop(i, p, _chip(x, y, p)[2]).wait_send()
                for p in range(nrel):
                    sib(i, p, c).wait_send()

        return start, relay, relay_on, finish

    n = len(items)
    sems = [pltpu.SemaphoreType.DMA((n * nrel,))] * 2 + [pltpu.SemaphoreType.DMA((n * 2,))] * 2 \
        + [pltpu.SemaphoreType.DMA((n * nrel,))] * 2
    return _Ride(shards, bufs, sems, plan, NEIGHBOURS_AND_SIBLING)


def exchange_ride(sums, items=None, into=None, rep=None):
    items = [(a, 0, s.shape[1]) for a, s in enumerate(sums)] if items is None else items
    into = into or [None] * len(sums)
    bufs = [jax.ShapeDtypeStruct(s.shape, s.dtype) if b is None else b for s, b in zip(sums, into)]
    srcs = list(sums)
    scratch = [pltpu.SemaphoreType.DMA((max(len(items), 1) * len(_OTHER_CHIPS),))] * 2
    if rep is not None:
        srcs.append(rep)
        bufs.append(jax.ShapeDtypeStruct((N_DEVICES,) + rep.shape, rep.dtype))
        scratch += [pltpu.SemaphoreType.DMA((len(_OTHER_DEVICES),))] * 2
    nrel = len(_OTHER_CHIPS)

    def plan(src_refs, dst_refs, sems):
        x, y, c = _position()
        j = 2 * x + y
        me = 4 * x + 2 * y + c

        def part(i, p, src_slot, dst_slot):
            a, r0, nr = items[i]
            px, py, _ = _chip(x, y, p)
            return pltpu.make_async_remote_copy(
                src_ref=src_refs[a].at[src_slot, pl.ds(r0, nr), :], dst_ref=dst_refs[a].at[dst_slot, pl.ds(r0, nr), :],
                send_sem=sems[0].at[i * nrel + p], recv_sem=sems[1].at[i * nrel + p],
                device_id=(px, py, c), device_id_type=MESH)

        def device(q):
            dx, dy, dc = _OTHER_DEVICES[q]
            return _flip(x, dx), _flip(y, dy), _flip(c, dc)

        def rep_copy(q, slot):
            return pltpu.make_async_remote_copy(
                src_ref=src_refs[-1], dst_ref=dst_refs[-1].at[slot], send_sem=sems[2].at[q], recv_sem=sems[3].at[q],
                device_id=device(q), device_id_type=MESH)

        pairs = [(i, p) for i in range(len(items)) for p in range(nrel)]
        others = range(len(_OTHER_DEVICES)) if rep is not None else ()

        def start():
            if rep is None:
                _handshake([_chip(x, y, p)[:2] + (c,) for p in range(nrel)])
            else:
                _handshake([device(q) for q in others])
            for i, p in pairs:
                part(i, p, _chip(x, y, p)[2], j).start()
            for q in others:
                rep_copy(q, me).start()

        def finish():
            for i, p in pairs:
                k = _chip(x, y, p)[2]
                part(i, p, k, k).wait_recv()
            for q in others:
                px, py, pc = device(q)
                rep_copy(q, 4 * px + 2 * py + pc).wait_recv()
            for i, p in pairs:
                part(i, p, _chip(x, y, p)[2], j).wait_send()
            for q in others:
                rep_copy(q, me).wait_send()

        return start, lambda: None, lambda: None, finish

    return _Ride(srcs, bufs, scratch, plan, OTHER_CHIPS_SAME_CORE if rep is None else ALL_DEVICES)


def _own_slot(buf, own, index):
    return lax.dynamic_update_slice(buf, own[None], (index,) + (0,) * own.ndim)


def _token_tile(s):
    return min(s, 512)


def norm_in(x, g):
    s, d = x.shape
    t = _token_tile(s)

    def body(x_ref, g_ref, o_ref, ot_ref):
        n, _ = _rms_stats(x_ref[...])
        h = n * g_ref[...]
        o_ref[...] = h.astype(BF16)
        ot_ref[...] = h.T.astype(BF16)

    return pl.pallas_call(
        body, name="norm_in", grid=(s // t,),
        in_specs=[pl.BlockSpec((t, d), lambda i: (i, 0)), pl.BlockSpec((1, d), lambda i: (0, 0))],
        out_specs=[pl.BlockSpec((t, d), lambda i: (i, 0)), pl.BlockSpec((d, t), lambda i: (0, i))],
        out_shape=[jax.ShapeDtypeStruct((s, d), BF16), jax.ShapeDtypeStruct((d, s), BF16)],
        compiler_params=_params(),
    )(x, g)


def matmul_cols(a, w4, name, ride=None):
    m, k = a.shape
    nj, _, ns = w4.shape
    nb = ns // CW

    def body(a_ref, w_ref, o_ref):
        o_ref[...] = _dot(a_ref[...], w_ref[0])

    return _call(
        body, name=name, grid=(nj, nb),
        in_specs=[pl.BlockSpec((m, k), lambda j, b: (0, 0)),
                  pl.BlockSpec((1, k, CW), lambda j, b: (j, 0, b))],
        out_specs=[pl.BlockSpec((m, CW), lambda j, b: (0, j * nb + b))],
        out_shape=[jax.ShapeDtypeStruct((m, nj * ns), F32)],
        operands=(a, w4), ride=ride)


def mix_conv_fwd(proj, ws, ride=None):
    s = proj.shape[0]
    nblk = D_MODEL // CW

    def body(cb_ref, cc_ref, cx_ref, ws_ref, q_ref, ya_ref):
        q = _causal_conv(cc_ref[...] * cx_ref[...], ws_ref)
        q_ref[...] = q
        ya_ref[...] = (cb_ref[...] * q).astype(BF16)

    seg = lambda k: pl.BlockSpec((s, CW), lambda c, k=k: (0, k * nblk + c))
    return _call(
        body, name="mix_conv_fwd", grid=(nblk,),
        in_specs=[seg(0), seg(1), seg(2), pl.BlockSpec((3, CW), lambda c: (0, c))],
        out_specs=[pl.BlockSpec((s, CW), lambda c: (0, c))] * 2,
        out_shape=[jax.ShapeDtypeStruct((s, D_MODEL), F32), jax.ShapeDtypeStruct((s, D_MODEL), BF16)],
        operands=(proj, proj, proj, ws), ride=ride)


def _lru_gates(r, ls):
    log_a = LRU_C * r * ls
    a = jnp.exp(log_a)
    mult = jnp.sqrt(_neg_expm1(2.0 * log_a))
    mult = jnp.where(_rows(r.shape) == 0, 1.0, mult)
    return a, mult


def mix_lru_fwd(proj, wl, bl, wa, ba, wx, bx, lam, ride=None):
    s = proj.shape[0]
    nblk = D_MODEL // CW

    def body(lx_ref, ly_ref, wl_ref, bl_ref, wa_ref, ba_ref, wx_ref, bx_ref, lam_ref,
             xl_ref, r_ref, i_ref, h_ref, yb_ref, a_scr, u_scr):
        xl = _causal_conv(lx_ref[...], wl_ref, bl_ref[...])
        xlb = xl.astype(BF16)
        xl_ref[...] = xlb
        r = _sigmoid(_dot(xlb, wa_ref[0]) + ba_ref[...])
        i = _sigmoid(_dot(xlb, wx_ref[0]) + bx_ref[...])
        r_ref[...] = r.astype(BF16)
        i_ref[...] = i.astype(BF16)
        a, mult = _lru_gates(r, _log_sigmoid(lam_ref[...]))
        a_scr[...] = a
        u_scr[...] = mult * i * xl
        _scan_forward(a_scr, u_scr, h_ref)
        yb_ref[...] = (h_ref[...] * _gelu(ly_ref[...])).astype(BF16)

    blk = lambda k: pl.BlockSpec((s, CW), lambda c, k=k: (0, k * nblk + c))
    vec = pl.BlockSpec((1, CW), lambda c: (0, c))
    mat = pl.BlockSpec((1, CW, CW), lambda c: (c, 0, 0))
    out = pl.BlockSpec((s, CW), lambda c: (0, c))
    f = jax.ShapeDtypeStruct((s, D_MODEL), F32)
    hb = jax.ShapeDtypeStruct((s, D_MODEL), BF16)
    return _call(
        body, name="mix_lru_fwd", grid=(nblk,),
        in_specs=[blk(3), blk(4), pl.BlockSpec((4, CW), lambda c: (0, c)), vec, mat, vec, mat, vec, vec],
        out_specs=[out] * 5,
        out_shape=[hb, hb, hb, f, hb],
        scratch_shapes=[pltpu.VMEM((s, CW), F32), pltpu.VMEM((s, CW), F32)],
        operands=(proj, proj, wl, bl, wa, ba, wx, bx, lam), ride=ride)


def branch_merge_fwd(ya, yb, wcb, wlb, proj, ride=None):
    s = ya.shape[0]
    nblk = D_MODEL // CW

    def body(ya_ref, yb_ref, wcb_ref, wlb_ref, gc_ref, gl_ref, a_ref, b_ref, m_ref):
        a = _dot(ya_ref[...], wcb_ref[...])
        b = _dot(yb_ref[...], wlb_ref[...])
        a_ref[...] = a
        b_ref[...] = b
        m_ref[...] = (_sigmoid(gc_ref[...]) * a + _sigmoid(gl_ref[...]) * b).astype(BF16)

    res = pl.BlockSpec((s, D_MODEL), lambda n: (0, 0))
    wcol = pl.BlockSpec((D_MODEL, CW), lambda n: (0, n))
    blk = lambda k: pl.BlockSpec((s, CW), lambda n, k=k: (0, k * nblk + n))
    out = pl.BlockSpec((s, CW), lambda n: (0, n))
    f = jax.ShapeDtypeStruct((s, D_MODEL), F32)
    return _call(
        body, name="branch_merge_fwd", grid=(nblk,),
        in_specs=[res, res, wcol, wcol, blk(5), blk(6)],
        out_specs=[out] * 3,
        out_shape=[f, f, jax.ShapeDtypeStruct((s, D_MODEL), BF16)],
        operands=(ya, yb, wcb, wlb, proj, proj), ride=ride)


def mix_out_fwd(merged, wout, x, g2, g3, ride=None):
    s, d = x.shape
    t = _token_tile(s)

    def body(m_ref, w_ref, x_ref, g2_ref, g3_ref, mix_ref, x2_ref, h2_ref, h2t_ref):
        mix = _dot(m_ref[...], w_ref[...])
        mix_ref[...] = mix
        n, _ = _rms_stats(mix)
        x2 = x_ref[...] + n * g2_ref[...]
        x2_ref[...] = x2
        n2, _ = _rms_stats(x2)
        h2 = n2 * g3_ref[...]
        h2_ref[...] = h2.astype(BF16)
        h2t_ref[...] = h2.T.astype(BF16)

    tile = pl.BlockSpec((t, d), lambda i: (i, 0))
    vec = pl.BlockSpec((1, d), lambda i: (0, 0))
    f = jax.ShapeDtypeStruct((s, d), F32)
    return _call(
        body, name="mix_out_fwd", grid=(s // t,),
        in_specs=[tile, pl.BlockSpec((d, d), lambda i: (0, 0)), tile, vec, vec],
        out_specs=[tile] * 3 + [pl.BlockSpec((d, t), lambda i: (0, i))],
        out_shape=[f, f, jax.ShapeDtypeStruct((s, d), BF16), jax.ShapeDtypeStruct((d, s), BF16)],
        operands=(merged, wout, x, g2, g3), ride=ride)


def ffn_up_act_fwd(h2, wup4, fw, fb, ride=None):
    s, k = h2.shape
    ns = wup4.shape[2]
    per_chip = ns // CW
    nblk = D_FF // CW

    def body(h_ref, wg_ref, wv_ref, cg_ref, cv_ref, bg_ref, bv_ref, up_ref, act_ref, f_ref):
        h = h_ref[...]
        ug = _dot(h, wg_ref[0])
        uv = _dot(h, wv_ref[0])
        up_ref[0] = ug
        up_ref[1] = uv
        gate = _causal_conv(ug, cg_ref, bg_ref[...])
        val = _causal_conv(uv, cv_ref, bv_ref[...])
        act_ref[0] = gate.astype(BF16)
        act_ref[1] = val.astype(BF16)
        f_ref[...] = (_gelu(gate) * val).astype(BF16)

    wcols = lambda h: pl.BlockSpec((1, k, CW), lambda n, h=h: (n // per_chip + 2 * h, 0, n % per_chip))
    half = lambda h, rows: pl.BlockSpec((rows, CW), lambda n, h=h: (0, h * nblk + n))
    both = pl.BlockSpec((2, s, CW), lambda n: (0, 0, n))
    return _call(
        body, name="ffn_up_act_fwd", grid=(nblk,),
        in_specs=[pl.BlockSpec((s, k), lambda n: (0, 0)), wcols(0), wcols(1),
                  half(0, 3), half(1, 3), half(0, 1), half(1, 1)],
        out_specs=[both, both, pl.BlockSpec((s, CW), lambda n: (0, n))],
        out_shape=[jax.ShapeDtypeStruct((2, s, D_FF), F32), jax.ShapeDtypeStruct((2, s, D_FF), BF16),
                   jax.ShapeDtypeStruct((s, D_FF), BF16)],
        operands=(h2, wup4, wup4, fw, fw, fb, fb), ride=ride)


def ffn_down_loss(f, wdown, x2, target, g4):
    s, d = x2.shape
    t = _token_tile(s)

    def body(f_ref, w_ref, x2_ref, tg_ref, g4_ref, dy_ref, dout_ref, loss_ref, dg4_ref):
        @pl.when(pl.program_id(0) == 0)
        def _():
            loss_ref[...] = jnp.zeros_like(loss_ref)
            dg4_ref[...] = jnp.zeros_like(dg4_ref)

        out = _dot(f_ref[...], w_ref[...])
        n, r = _rms_stats(out)
        err = x2_ref[...] + n * g4_ref[...] - tg_ref[...]
        loss_ref[...] += jnp.full(loss_ref.shape, (0.5 / d) * jnp.sum(err * err), F32)
        dy = err * (1.0 / d)
        dy_ref[...] = dy
        dout, dg = _rms_bwd(n, r, g4_ref[...], dy)
        dout_ref[...] = dout.astype(BF16)
        dg4_ref[...] += jnp.sum(dg, axis=0, keepdims=True)

    tile = pl.BlockSpec((t, d), lambda i: (i, 0))
    vec = pl.BlockSpec((1, d), lambda i: (0, 0))
    return pl.pallas_call(
        body, name="ffn_down_loss", grid=(s // t,),
        in_specs=[pl.BlockSpec((t, D_FF), lambda i: (i, 0)), pl.BlockSpec((D_FF, d), lambda i: (0, 0)), tile, tile, vec],
        out_specs=[tile, tile, pl.BlockSpec((1, 128), lambda i: (0, 0)), vec],
        out_shape=[jax.ShapeDtypeStruct((s, d), F32), jax.ShapeDtypeStruct((s, d), BF16),
                   jax.ShapeDtypeStruct((1, 128), F32), jax.ShapeDtypeStruct((1, d), F32)],
        compiler_params=_params(),
    )(f, wdown, x2, target, g4)


def ffn_up_bwd(dout, wdown, up, act, f, fw, wup4, h2t, ride=None):
    k, s = h2t.shape
    nblk = D_FF // FW
    per_chip = wup4.shape[2] // FW

    def body(do_ref, wd_ref, up_ref, act_ref, f_ref, cg_ref, cv_ref, wg_ref, wv_ref, h_ref,
             dh_ref, dwu_ref, dwd_ref, dw_ref, db_ref, dup_scr):
        @pl.when(pl.program_id(0) == 0)
        def _():
            dup_scr[...] = jnp.zeros_like(dup_scr)
            dh_ref[...] = jnp.zeros_like(dh_ref)

        do = do_ref[...]
        df = _dot_nt(do, wd_ref[...])
        dg = dup_scr[0]
        dv = dup_scr[1]
        ht = h_ref[...]
        dh_ref[...] += _dot_nt(dg, wg_ref[0]) + _dot_nt(dv, wv_ref[0])
        dwu_ref[0] = _dot(ht, dg).astype(BF16)
        dwu_ref[1] = _dot(ht, dv).astype(BF16)
        dwd_ref[...] = _dot_tn(f_ref[...], do).astype(BF16)
        val = act_ref[1].astype(F32)
        ge, dge = _gelu_and_grad(act_ref[0].astype(F32))
        dgate = _advances(df * val * dge, 3)
        dval = _advances(df * ge, 3)
        dw_ref[0] = _conv_wgrad(dgate, up_ref[0])
        dw_ref[1] = _conv_wgrad(dval, up_ref[1])
        db_ref[0] = jnp.sum(dgate[0], axis=0, keepdims=True)
        db_ref[1] = jnp.sum(dval[0], axis=0, keepdims=True)
        dup_scr[0] = _taps_sum(dgate, cg_ref).astype(BF16)
        dup_scr[1] = _taps_sum(dval, cv_ref).astype(BF16)

    cur = lambda n: jnp.minimum(n, nblk - 1)
    prev = lambda n: jnp.maximum(n - 1, 0)
    once = pl.Buffered(1)
    both = lambda rows: pl.BlockSpec((2, rows, FW), lambda n: (0, 0, cur(n)))
    taps = lambda h: pl.BlockSpec((3, FW), lambda n, h=h: (0, h * nblk + cur(n)))
    wcols = lambda h: pl.BlockSpec((1, k, FW), lambda n, h=h: (prev(n) // per_chip + 2 * h, 0, prev(n) % per_chip))
    return _call(
        body, name="ffn_up_bwd", grid=(nblk + 1,),
        in_specs=[pl.BlockSpec((s, D_MODEL), lambda n: (0, 0), pipeline_mode=once),
                  pl.BlockSpec((FW, D_MODEL), lambda n: (cur(n), 0)), both(s), both(s),
                  pl.BlockSpec((s, FW), lambda n: (0, cur(n))), taps(0), taps(1), wcols(0), wcols(1),
                  pl.BlockSpec((k, s), lambda n: (0, 0), pipeline_mode=once)],
        out_specs=[pl.BlockSpec((s, k), lambda n: (0, 0), pipeline_mode=once),
                   pl.BlockSpec((2, k, FW), lambda n: (0, 0, prev(n))),
                   pl.BlockSpec((FW, D_MODEL), lambda n: (cur(n), 0)), both(3), both(1)],
        out_shape=[jax.ShapeDtypeStruct((s, k), F32), jax.ShapeDtypeStruct((2, k, D_FF), BF16),
                   jax.ShapeDtypeStruct((D_FF, D_MODEL), BF16),
                   jax.ShapeDtypeStruct((2, 3, D_FF), F32), jax.ShapeDtypeStruct((2, 1, D_FF), F32)],
        scratch_shapes=[pltpu.VMEM((2, s, FW), BF16)],
        operands=(dout, wdown, up, act, f, fw, fw, wup4, wup4, h2t), ride=ride)


def matmul_cols_bwd(dy, other, name, wgrad, ride=None):
    m = dy[0].shape[1]
    if wgrad:
        k = other.shape[0]
        nj, nb = N_CHIPS, sum(d.shape[0] * d.shape[2] for d in dy) // (N_CHIPS * CW)
    else:
        nj, k, ns = other.shape
        nb = ns // CW
    per_seg = dy[0].shape[2] // CW
    first = [sum(d.shape[0] for d in dy[:i]) for i in range(len(dy))]

    def segment(j, b):
        return (j * nb + b) // per_seg, (j * nb + b) % per_seg

    def body(*refs):
        dy_refs, (o_ref, r_ref) = refs[:len(dy)], refs[len(dy):]
        seg, _ = segment(pl.program_id(0), pl.program_id(1))
        dyb = dy_refs[-1][0]
        for i in range(len(dy) - 2, -1, -1):
            dyb = jnp.where(seg < first[i + 1], dy_refs[i][0], dyb)
        if wgrad:
            r_ref[...] = _dot(o_ref[...], dyb).astype(BF16)
        else:
            @pl.when((pl.program_id(0) == 0) & (pl.program_id(1) == 0))
            def _():
                r_ref[...] = jnp.zeros_like(r_ref)

            r_ref[...] += _dot_nt(dyb, o_ref[0])

    def dy_spec(i):
        nseg = dy[i].shape[0]

        def index(j, b):
            seg, col = segment(j, b)
            local = seg - first[i]
            return (jnp.clip(local, 0, nseg - 1), 0,
                    jnp.where(local < 0, 0, jnp.where(local >= nseg, per_seg - 1, col)))

        return pl.BlockSpec((1, m, CW), index)

    if wgrad:
        other_spec = pl.BlockSpec((k, m), lambda j, b: (0, 0))
        out_spec = pl.BlockSpec((k, CW), lambda j, b: (0, j * nb + b))
        out_shape = jax.ShapeDtypeStruct((k, nj * nb * CW), BF16)
    else:
        other_spec = pl.BlockSpec((1, k, CW), lambda j, b: (j, 0, b))
        out_spec = pl.BlockSpec((m, k), lambda j, b: (0, 0))
        out_shape = jax.ShapeDtypeStruct((m, k), F32)
    return _call(
        body, name=name, grid=(nj, nb), in_specs=[dy_spec(i) for i in range(len(dy))] + [other_spec],
        out_specs=[out_spec], out_shape=[out_shape], operands=(*dy, other), ride=ride)


def norms_mid_bwd(dh2, x2, dy, mix, g3, g2, ride=None):
    s, d = x2.shape
    t = _token_tile(s)

    def body(dh2_ref, x2_ref, dy_ref, mix_ref, g3_ref, g2_ref, dx2_ref, dmix_ref, dg3_ref, dg2_ref):
        @pl.when(pl.program_id(0) == 0)
        def _():
            dg3_ref[...] = jnp.zeros_like(dg3_ref)
            dg2_ref[...] = jnp.zeros_like(dg2_ref)

        n3, r3 = _rms_stats(x2_ref[...])
        dx, dg3 = _rms_bwd(n3, r3, g3_ref[...], dh2_ref[...])
        dx2 = dy_ref[...] + dx
        dx2_ref[...] = dx2
        dg3_ref[...] += jnp.sum(dg3, axis=0, keepdims=True)
        n2, r2 = _rms_stats(mix_ref[...])
        dmix, dg2 = _rms_bwd(n2, r2, g2_ref[...], dx2)
        dmix_ref[...] = dmix.astype(BF16)
        dg2_ref[...] += jnp.sum(dg2, axis=0, keepdims=True)

    tile = pl.BlockSpec((t, d), lambda i: (i, 0))
    vec = pl.BlockSpec((1, d), lambda i: (0, 0))
    v = jax.ShapeDtypeStruct((1, d), F32)
    return _call(
        body, name="norms_mid_bwd", grid=(s // t,),
        in_specs=[tile, tile, tile, tile, vec, vec],
        out_specs=[tile, tile, vec, vec],
        out_shape=[jax.ShapeDtypeStruct((s, d), F32), jax.ShapeDtypeStruct((s, d), BF16), v, v],
        operands=(dh2, x2, dy, mix, g3, g2), ride=ride)


def mix_out_bwd(dmix, wout, merged, a, b, proj, ride=None):
    s = dmix.shape[0]
    nblk = D_MODEL // CW

    def body(dm_ref, w_ref, mg_ref, a_ref, b_ref, gc_ref, gl_ref, da_ref, db_ref, dw_ref, dg_ref):
        dm = dm_ref[...]
        dmerged = _dot_nt(dm, w_ref[...])
        dw_ref[...] = _dot_tn(mg_ref[...], dm).astype(BF16)
        sc = _sigmoid(gc_ref[...])
        sl = _sigmoid(gl_ref[...])
        da_ref[...] = (dmerged * sc).astype(BF16)
        db_ref[...] = (dmerged * sl).astype(BF16)
        dg_ref[0] = (dmerged * a_ref[...] * sc * (1.0 - sc)).astype(BF16)
        dg_ref[1] = (dmerged * b_ref[...] * sl * (1.0 - sl)).astype(BF16)

    res = pl.BlockSpec((s, D_MODEL), lambda n: (0, 0))
    rows = pl.BlockSpec((CW, D_MODEL), lambda n: (n, 0))
    col = pl.BlockSpec((s, CW), lambda n: (0, n))
    blk = lambda k: pl.BlockSpec((s, CW), lambda n, k=k: (0, k * nblk + n))
    hb = jax.ShapeDtypeStruct((s, D_MODEL), BF16)
    return _call(
        body, name="mix_out_bwd", grid=(nblk,),
        in_specs=[res, rows, col, col, col, blk(5), blk(6)],
        out_specs=[col, col, rows, pl.BlockSpec((2, s, CW), lambda n: (0, 0, n))],
        out_shape=[hb, hb, jax.ShapeDtypeStruct((D_MODEL, D_MODEL), BF16), jax.ShapeDtypeStruct((2, s, D_MODEL), BF16)],
        operands=(dmix, wout, merged, a, b, proj, proj), ride=ride)


def mix_conv_bwd(da, wcb, proj, q, ws, ride=None):
    s = da.shape[0]
    nblk = D_MODEL // CW

    def body(da_ref, w_ref, cb_ref, cc_ref, cx_ref, q_ref, ws_ref, dc_ref, dw_ref, dws_ref):
        dab = da_ref[...]
        dya = _dot_nt(dab, w_ref[...])
        cb = cb_ref[...]
        cc = cc_ref[...]
        cx = cx_ref[...]
        q = q_ref[...]
        dw_ref[...] = _dot_tn((cb * q).astype(BF16), dab).astype(BF16)
        dc_ref[0] = (dya * q).astype(BF16)
        dq = _advances(dya * cb, 3)
        dp = _taps_sum(dq, ws_ref)
        dws_ref[...] = _conv_wgrad(dq, cc * cx)
        dc_ref[1] = (dp * cx).astype(BF16)
        dc_ref[2] = (dp * cc).astype(BF16)

    res = pl.BlockSpec((s, D_MODEL), lambda n: (0, 0))
    rows = pl.BlockSpec((CW, D_MODEL), lambda n: (n, 0))
    col = pl.BlockSpec((s, CW), lambda n: (0, n))
    blk = lambda k: pl.BlockSpec((s, CW), lambda n, k=k: (0, k * nblk + n))
    taps = pl.BlockSpec((3, CW), lambda n: (0, n))
    hb = jax.ShapeDtypeStruct((s, D_MODEL), BF16)
    return _call(
        body, name="mix_conv_bwd", grid=(nblk,),
        in_specs=[res, rows, blk(0), blk(1), blk(2), col, taps],
        out_specs=[pl.BlockSpec((3, s, CW), lambda n: (0, 0, n)), rows, taps],
        out_shape=[jax.ShapeDtypeStruct((3, s, D_MODEL), BF16), jax.ShapeDtypeStruct((D_MODEL, D_MODEL), BF16),
                   jax.ShapeDtypeStruct((3, D_MODEL), F32)],
        operands=(da, wcb, proj, proj, proj, q, ws), ride=ride)


def mix_lru_bwd(db, wlb, proj, xl, r, i, h, wl, wa, wx, lam, ride=None):
    s = db.shape[0]
    nblk = D_MODEL // CW

    def body(db_ref, w_ref, lx_ref, ly_ref, xl_ref, r_ref, i_ref, h_ref, wl_ref, wa_ref, wx_ref, lam_ref,
             dl_ref, dw_ref, dwa_ref, dwx_ref, dba_ref, dbx_ref, dwl_ref, dbl_ref, dlam_ref,
             c_scr, g_scr):
        dbb = db_ref[...]
        dyb = _dot_nt(dbb, w_ref[...])
        h = h_ref[...]
        ge, dge = _gelu_and_grad(ly_ref[...])
        dw_ref[...] = _dot_tn((h * ge).astype(BF16), dbb).astype(BF16)
        dl_ref[1] = (dyb * h * dge).astype(BF16)
        r = r_ref[...].astype(F32)
        gi = i_ref[...].astype(F32)
        xlb = xl_ref[...]
        xl = xlb.astype(F32)
        lam = lam_ref[...]
        ls = _log_sigmoid(lam)
        a, mult = _lru_gates(r, ls)
        c_scr[...] = _shift_up(a, 1)
        g_scr[...] = dyb * ge
        _scan_backward(c_scr, g_scr, g_scr)
        du = g_scr[...]
        da = du * _shift_down(h, 1)
        dmult = du * gi * xl
        di = du * mult * xl
        dxl = du * mult * gi
        first = _rows(a.shape) == 0
        dlog_a = da * a - jnp.where(first, 0.0, dmult * a * a / mult)
        dr = dlog_a * (LRU_C * ls)
        dlam_ref[...] = jnp.sum(dlog_a * r, axis=0, keepdims=True) * (LRU_C * (1.0 - _sigmoid(lam)))
        dzr = dr * r * (1.0 - r)
        dzi = di * gi * (1.0 - gi)
        dba_ref[...] = jnp.sum(dzr, axis=0, keepdims=True)
        dbx_ref[...] = jnp.sum(dzi, axis=0, keepdims=True)
        dzrb = dzr.astype(BF16)
        dzib = dzi.astype(BF16)
        dwa_ref[0] = _dot_tn(xlb, dzrb)
        dwx_ref[0] = _dot_tn(xlb, dzib)
        dxl = _advances(dxl + _dot_nt(dzrb, wa_ref[0]) + _dot_nt(dzib, wx_ref[0]), 4)
        dl_ref[0] = _taps_sum(dxl, wl_ref).astype(BF16)
        dwl_ref[...] = _conv_wgrad(dxl, lx_ref[...])
        dbl_ref[...] = jnp.sum(dxl[0], axis=0, keepdims=True)

    res = pl.BlockSpec((s, D_MODEL), lambda n: (0, 0))
    rows = pl.BlockSpec((CW, D_MODEL), lambda n: (n, 0))
    col = pl.BlockSpec((s, CW), lambda n: (0, n))
    blk = lambda k: pl.BlockSpec((s, CW), lambda n, k=k: (0, k * nblk + n))
    taps = pl.BlockSpec((4, CW), lambda n: (0, n))
    vec = pl.BlockSpec((1, CW), lambda n: (0, n))
    mat = pl.BlockSpec((1, CW, CW), lambda n: (n, 0, 0))
    hb = jax.ShapeDtypeStruct((s, D_MODEL), BF16)
    v = jax.ShapeDtypeStruct((1, D_MODEL), F32)
    m = jax.ShapeDtypeStruct((LRU_HEADS, HEAD_DIM, HEAD_DIM), F32)
    scr = pltpu.VMEM((s, CW), F32)
    return _call(
        body, name="mix_lru_bwd", grid=(nblk,),
        in_specs=[res, rows, blk(3), blk(4), col, col, col, col, taps, mat, mat, vec],
        out_specs=[pl.BlockSpec((2, s, CW), lambda n: (0, 0, n)), rows, mat, mat, vec, vec, taps, vec, vec],
        out_shape=[jax.ShapeDtypeStruct((2, s, D_MODEL), BF16), jax.ShapeDtypeStruct((D_MODEL, D_MODEL), BF16), m, m, v, v,
                   jax.ShapeDtypeStruct((4, D_MODEL), F32), v, v],
        scratch_shapes=[scr, scr],
        operands=(db, wlb, proj, proj, xl, r, i, h, wl, wa, wx, lam), ride=ride)


def norm_in_bwd(dh1, x, dx2, g1, ride=None):
    s, d = x.shape
    t = _token_tile(s)

    def body(dh_ref, x_ref, dx2_ref, g_ref, dx_ref, dg_ref):
        @pl.when(pl.program_id(0) == 0)
        def _():
            dg_ref[...] = jnp.zeros_like(dg_ref)

        n, r = _rms_stats(x_ref[...])
        dx, dg = _rms_bwd(n, r, g_ref[...], dh_ref[...])
        dx_ref[...] = dx2_ref[...] + dx
        dg_ref[...] += jnp.sum(dg, axis=0, keepdims=True)

    tile = pl.BlockSpec((t, d), lambda i: (i, 0))
    vec = pl.BlockSpec((1, d), lambda i: (0, 0))
    return _call(
        body, name="norm_in_bwd", grid=(s // t,),
        in_specs=[tile, tile, tile, vec],
        out_specs=[tile, vec],
        out_shape=[jax.ShapeDtypeStruct((s, d), F32), jax.ShapeDtypeStruct((1, d), F32)],
        operands=(dh1, x, dx2, g1), ride=ride)


def _owned_part(ref, kind, k, h, hr):
    if kind == "col":
        ns = ref.shape[1] // N_CHIPS
        return ref.at[pl.ds(h * hr, hr), pl.ds(k * ns, ns)]
    if kind == "row":
        return ref.at[pl.ds(k * 2 * hr + h * hr, hr), :]
    if kind == "col2":
        ns = ref.shape[2] // 2
        return ref.at[k // 2, pl.ds(h * hr, hr), pl.ds((k % 2) * ns, ns)]
    return ref.at[k, pl.ds(h * hr, hr), :]


def _part_shape(g, kind):
    if kind == "col2":
        return g.shape[1] // 2, g.shape[2] // 2
    if kind == "col":
        return g.shape[0] // 2, g.shape[1] // N_CHIPS
    if kind == "row":
        return g.shape[0] // (2 * N_CHIPS), g.shape[1]
    return g.shape[1] // 2, g.shape[2]


def pair_split(grads, kinds, name):
    n = len(grads)
    shapes = [_part_shape(g, k) for g, k in zip(grads, kinds)]

    def body(*refs):
        ins, theirs = refs[:n], refs[n:2 * n]
        send_sem, recv_sem = refs[2 * n:]
        x, y, c = _position()
        copies = []
        for a in range(n):
            hr = shapes[a][0]
            for k in range(N_CHIPS):
                s = a * N_CHIPS + k
                copies.append(pltpu.make_async_remote_copy(
                    src_ref=_owned_part(ins[a], kinds[a], k, 1 - c, hr), dst_ref=theirs[a].at[k],
                    send_sem=send_sem.at[s], recv_sem=recv_sem.at[s], device_id=(x, y, 1 - c), device_id_type=MESH))
        _handshake([(x, y, 1 - c)])
        for cp in copies:
            cp.start()
        for cp in copies:
            cp.wait()

    return pl.pallas_call(
        body, name=name,
        in_specs=[_HBM] * n, out_specs=[_HBM] * n,
        out_shape=[jax.ShapeDtypeStruct((N_CHIPS,) + shp, g.dtype) for shp, g in zip(shapes, grads)],
        scratch_shapes=[pltpu.SemaphoreType.DMA((n * N_CHIPS,))] * 2,
        compiler_params=pltpu.CompilerParams(collective_id=SIBLING),
    )(*grads)


def pair_swap(halves):
    n = len(halves)

    def body(*refs):
        ins, outs = refs[:n], refs[n:2 * n]
        send_sem, recv_sem = refs[2 * n:]
        x, y, c = _position()
        copies = [pltpu.make_async_remote_copy(
            src_ref=ins[a], dst_ref=outs[a], send_sem=send_sem.at[a], recv_sem=recv_sem.at[a],
            device_id=(x, y, 1 - c), device_id_type=MESH) for a in range(n)]
        _handshake([(x, y, 1 - c)])
        for cp in copies:
            cp.start()
        for cp in copies:
            cp.wait()

    return pl.pallas_call(
        body, name="pair_swap",
        in_specs=[_HBM] * n, out_specs=[_HBM] * n,
        out_shape=[jax.ShapeDtypeStruct(h.shape, h.dtype) for h in halves],
        scratch_shapes=[pltpu.SemaphoreType.DMA((n,))] * 2,
        compiler_params=pltpu.CompilerParams(collective_id=SIBLING),
    )(*halves)


def _row_tile(rows, cols, limit_bytes=1 << 20):
    best = None
    for t in range(SUBLANES, rows + 1, SUBLANES):
        if rows % t == 0 and t * cols * 4 <= limit_bytes:
            best = t
    return best or rows


def add_pair(g, kind, theirs, core, name):
    nc, rows, cols = theirs.shape
    t = _row_tile(rows, cols, 4 << 20)
    nt = rows // t

    def body(core_ref, g_ref, b_ref, o_ref):
        mine = g_ref[...].reshape(t, cols)
        o_ref[0] = (mine.astype(F32) + b_ref[0].astype(F32)).astype(o_ref.dtype)

    if kind == "col":
        own = pl.BlockSpec((t, cols), lambda k, i, c: (c[0] * nt + i, k))
    elif kind == "col2":
        own = pl.BlockSpec((1, t, cols), lambda k, i, c: (k // 2, c[0] * nt + i, k % 2))
    elif kind == "row":
        own = pl.BlockSpec((t, cols), lambda k, i, c: ((2 * k + c[0]) * nt + i, 0))
    else:
        own = pl.BlockSpec((1, t, cols), lambda k, i, c: (k, c[0] * nt + i, 0))
    spec = pl.BlockSpec((1, t, cols), lambda k, i, c: (k, i, 0))
    return pl.pallas_call(
        body, name=name,
        grid_spec=pltpu.PrefetchScalarGridSpec(num_scalar_prefetch=1, grid=(nc, nt), in_specs=[own, spec], out_specs=spec),
        out_shape=jax.ShapeDtypeStruct(theirs.shape, theirs.dtype), compiler_params=_params(),
    )(core, g, theirs)


def sum_lead(a, name):
    nl, rows, cols = a.shape
    t = _row_tile(rows, cols, (1 << 20) // 2)

    def body(a_ref, o_ref):
        acc = a_ref[0].astype(F32)
        for s in range(1, nl):
            acc = acc + a_ref[s].astype(F32)
        o_ref[...] = acc

    return pl.pallas_call(
        body, name=name, grid=(rows // t,),
        in_specs=[pl.BlockSpec((nl, t, cols), lambda i: (0, i, 0))],
        out_specs=pl.BlockSpec((t, cols), lambda i: (i, 0)),
        out_shape=jax.ShapeDtypeStruct((rows, cols), F32), compiler_params=_params(),
    )(a)


def sum_chips(rx, csum, chip, name):
    nc, rows, cols = rx.shape
    t = _row_tile(rows, cols, 2 << 20)

    def body(chip_ref, r0, r1, r2, r3, own_ref, o_ref):
        acc = None
        for s, ref in enumerate((r0, r1, r2, r3)):
            term = jnp.where(chip_ref[0] == s, own_ref[0], ref[0]).astype(F32)
            acc = term if acc is None else acc + term
        o_ref[...] = acc

    def slot(s):
        return pl.BlockSpec((1, t, cols), lambda i, c, s=s: (jnp.where(c[0] == s, c[0] ^ 1, s), i, 0))

    return pl.pallas_call(
        body, name=name,
        grid_spec=pltpu.PrefetchScalarGridSpec(
            num_scalar_prefetch=1, grid=(rows // t,),
            in_specs=[slot(s) for s in range(nc)] + [pl.BlockSpec((1, t, cols), lambda i, c: (c[0], i, 0))],
            out_specs=pl.BlockSpec((t, cols), lambda i, c: (i, 0))),
        out_shape=jax.ShapeDtypeStruct((rows, cols), F32), compiler_params=_params(),
    )(chip, rx, rx, rx, rx, csum)


def cast_bf16(a, name):
    rows, cols = a.shape
    t = _row_tile(rows, cols, 2 << 20)

    def body(i_ref, o_ref):
        o_ref[...] = i_ref[...].astype(BF16)

    spec = pl.BlockSpec((t, cols), lambda i: (i, 0))
    return pl.pallas_call(body, name=name, grid=(rows // t,), in_specs=[spec], out_specs=spec,
                          out_shape=jax.ShapeDtypeStruct((rows, cols), BF16), compiler_params=_params())(a)


def _adamw_update(w, g, m, v):
    nm = ADAM_B1 * m + (1.0 - ADAM_B1) * g
    nv = ADAM_B2 * v + (1.0 - ADAM_B2) * (g * g)
    m_hat = nm * (1.0 / (1.0 - ADAM_B1 ** ADAM_STEP))
    v_hat = nv * (1.0 / (1.0 - ADAM_B2 ** ADAM_STEP))
    return -ADAM_LR * (m_hat / (jnp.sqrt(v_hat) + ADAM_EPS) + ADAM_WD * w), nm, nv


def adamw(w, g, m, v, name):
    rows, cols = w.shape
    t = _row_tile(rows, cols)

    def body(w_ref, g_ref, m_ref, v_ref, d_ref, nm_ref, nv_ref):
        d_ref[...], nm_ref[...], nv_ref[...] = _adamw_update(w_ref[...], g_ref[...], m_ref[...], v_ref[...])

    spec = pl.BlockSpec((t, cols), lambda i: (i, 0))
    shp = jax.ShapeDtypeStruct((rows, cols), F32)
    return pl.pallas_call(
        body, name=name, grid=(rows // t,), in_specs=[spec] * 4, out_specs=[spec] * 3,
        out_shape=[shp, shp, shp], compiler_params=_params(),
    )(w, g, m, v)


def adamw_halves(w, g_mine, g_other, m, v, core, name):
    rows, cols = w.shape
    hr = rows // 2
    t = _row_tile(hr, cols)
    nt = hr // t

    def body(core_ref, w_ref, gm_ref, go_ref, m_ref, v_ref, g_ref, d_ref, nm_ref, nv_ref):
        g = jnp.where(pl.program_id(0) // nt == core_ref[0], gm_ref[...], go_ref[...])
        g_ref[...] = g
        d_ref[...], nm_ref[...], nv_ref[...] = _adamw_update(w_ref[...], g, m_ref[...], v_ref[...])

    spec = pl.BlockSpec((t, cols), lambda i, c: (i, 0))
    half = pl.BlockSpec((t, cols), lambda i, c: (i % nt, 0))
    shp = jax.ShapeDtypeStruct((rows, cols), F32)
    return pl.pallas_call(
        body, name=name,
        grid_spec=pltpu.PrefetchScalarGridSpec(num_scalar_prefetch=1, grid=(2 * nt,),
                                               in_specs=[spec, half, half, spec, spec], out_specs=[spec] * 4),
        out_shape=[shp] * 4, compiler_params=_params(),
    )(core, w, g_mine, g_other, m, v)


WEIGHTS = ("norm_mix_pre", "norm_mix_post", "norm_ffn_pre", "norm_ffn_post", "w_in", "conv_short_w",
           "w_conv_branch", "lru_conv_w", "lru_conv_b", "lru_wa", "lru_ba", "lru_wx", "lru_bx", "lru_lambda",
           "w_lru_branch", "w_out", "ffn_w_up", "ffn_conv_w", "ffn_conv_b", "ffn_w_down")
BIG = ("w_in", "ffn_w_up", "w_conv_branch", "w_lru_branch", "w_out", "ffn_w_down")
BIG_KIND = ("col", "col", "row", "row", "row", "row")
SMALL = ("conv_short_w", "lru_conv_w", "lru_wa", "lru_ba", "lru_wx", "lru_bx", "ffn_conv_w")
REPL = ("norm_mix_pre", "norm_mix_post", "norm_ffn_pre", "norm_ffn_post", "lru_conv_b", "lru_lambda", "ffn_conv_b")
PACK_W = 256
SMALL_ROWS = 576
REPL_ROWS = 16
LOSS_ROW = 12
FFN_SHARD = 2 * D_FF // N_CHIPS
QUARTER = HEAD_DIM // N_CHIPS
SMALL_PARTS = (("conv_short_w", 3, (1, 3, PACK_W)), ("lru_conv_w", 4, (1, 4, PACK_W)),
               ("lru_wa", LRU_HEADS * QUARTER, (1, LRU_HEADS, QUARTER, HEAD_DIM)), ("lru_ba", LRU_HEADS, (1, LRU_HEADS, QUARTER)),
               ("lru_wx", LRU_HEADS * QUARTER, (1, LRU_HEADS, QUARTER, HEAD_DIM)), ("lru_bx", LRU_HEADS, (1, LRU_HEADS, QUARTER)),
               ("ffn_conv_w", 3 * FFN_SHARD // PACK_W, (1, 3, FFN_SHARD)))


def _pad8(nr):
    return -(-nr // SUBLANES) * SUBLANES


SMALL_OFFSET = {}
for _name, _nr, _ in SMALL_PARTS:
    SMALL_OFFSET[_name] = sum(_pad8(nr) for n, nr, _ in SMALL_PARTS[:len(SMALL_OFFSET)])
FFN_ROWS = FFN_SHARD // PACK_W
BIASES = ("lru_ba", "lru_bx")
TAPS3 = ("conv_short_w", "ffn_conv_w")


def pack_small(dicts):
    names = [n for n, _, _ in SMALL_PARTS]
    operands = [d[n].transpose(1, 0, 2) if n in TAPS3 else d[n] for d in dicts for n in names]

    def body(*refs):
        ins, outs = refs[:len(operands)], refs[len(operands):]
        for i, o in enumerate(outs):
            o[...] = jnp.zeros_like(o)
            for (name, nr, shape), p in zip(SMALL_PARTS, ins[i * len(names):(i + 1) * len(names)]):
                r0 = SMALL_OFFSET[name]
                if name in BIASES:
                    o[r0:r0 + nr, 0:QUARTER] = p[0]
                elif name == "ffn_conv_w":
                    for k in range(shape[1]):
                        for s in range(FFN_ROWS):
                            o[r0 + FFN_ROWS * k + s:r0 + FFN_ROWS * k + s + 1, :] = p[k, :, s * PACK_W:(s + 1) * PACK_W]
                elif name == "conv_short_w":
                    for k in range(nr):
                        o[r0 + k:r0 + k + 1, :] = p[k]
                else:
                    o[r0:r0 + nr, :] = p[0].reshape(nr, PACK_W)

    shape = jax.ShapeDtypeStruct((SMALL_ROWS, PACK_W), F32)
    return pl.pallas_call(body, name="pack_small", out_shape=[shape] * len(dicts), compiler_params=_params())(*operands)


def full_small(g4):
    def body(p, csw, lcw, wa, wx, fcw):
        chips = range(N_CHIPS)
        r0 = SMALL_OFFSET["conv_short_w"]
        csw[...] = jnp.concatenate([p[c, r0:r0 + 3, :] for c in chips], axis=1)
        r0 = SMALL_OFFSET["lru_conv_w"]
        lcw[...] = jnp.concatenate([p[c, r0:r0 + 4, :] for c in chips], axis=1)
        for name, o in (("lru_wa", wa), ("lru_wx", wx)):
            r0 = SMALL_OFFSET[name]
            for h in range(LRU_HEADS):
                for c in chips:
                    o[h, c * QUARTER:(c + 1) * QUARTER, :] = p[c, r0 + h * QUARTER:r0 + (h + 1) * QUARTER, :].astype(BF16)
        r0 = SMALL_OFFSET["ffn_conv_w"]
        for k in range(3):
            fcw[k:k + 1, :] = jnp.concatenate(
                [p[c, r0 + FFN_ROWS * k + s:r0 + FFN_ROWS * k + s + 1, :] for c in chips for s in range(FFN_ROWS)], axis=1)

    mat = jax.ShapeDtypeStruct((LRU_HEADS, HEAD_DIM, HEAD_DIM), BF16)
    csw, lcw, wa, wx, fcw = pl.pallas_call(
        body, name="full_small",
        out_shape=[jax.ShapeDtypeStruct((3, D_MODEL), F32), jax.ShapeDtypeStruct((4, D_MODEL), F32), mat, mat,
                   jax.ShapeDtypeStruct((3, 2 * D_FF), F32)],
        compiler_params=_params())(g4)

    def bias(name):
        r0 = SMALL_OFFSET[name]
        return g4[:, r0:r0 + LRU_HEADS, :QUARTER].transpose(1, 0, 2).reshape(1, D_MODEL)

    return dict(conv_short_w=csw, lru_conv_w=lcw, lru_wa=wa, lru_wx=wx, ffn_conv_w=fcw,
                lru_ba=bias("lru_ba"), lru_bx=bias("lru_bx"))


def split_small(full):
    def bias(name):
        return full[name].reshape(LRU_HEADS, N_CHIPS, QUARTER).transpose(1, 0, 2)

    def body(csw, lcw, wa, wx, fcw, ba, bx, o):
        o[...] = jnp.zeros_like(o)
        for c in range(N_CHIPS):
            cols = slice(c * PACK_W, (c + 1) * PACK_W)
            r0 = SMALL_OFFSET["conv_short_w"]
            o[c, r0:r0 + 3, :] = csw[:, cols]
            r0 = SMALL_OFFSET["lru_conv_w"]
            o[c, r0:r0 + 4, :] = lcw[:, cols]
            for name, p in (("lru_wa", wa), ("lru_wx", wx)):
                r0 = SMALL_OFFSET[name]
                for h in range(LRU_HEADS):
                    o[c, r0 + h * QUARTER:r0 + (h + 1) * QUARTER, :] = p[h, c * QUARTER:(c + 1) * QUARTER, :]
            for name, p in (("lru_ba", ba), ("lru_bx", bx)):
                r0 = SMALL_OFFSET[name]
                o[c, r0:r0 + LRU_HEADS, 0:QUARTER] = p[c]
            r0 = SMALL_OFFSET["ffn_conv_w"]
            for k in range(3):
                for s in range(FFN_ROWS):
                    lo = c * FFN_SHARD + s * PACK_W
                    o[c, r0 + FFN_ROWS * k + s:r0 + FFN_ROWS * k + s + 1, :] = fcw[k:k + 1, lo:lo + PACK_W]

    return pl.pallas_call(
        body, name="split_small", out_shape=jax.ShapeDtypeStruct((N_CHIPS, SMALL_ROWS, PACK_W), F32),
        compiler_params=_params(),
    )(full["conv_short_w"], full["lru_conv_w"], full["lru_wa"], full["lru_wx"], full["ffn_conv_w"],
      bias("lru_ba"), bias("lru_bx"))


def pack_repl(dicts, loss=None):
    operands = [d[n] for d in dicts for n in REPL] + ([loss] if loss is not None else [])

    def body(*refs):
        ins, outs = refs[:len(operands)], refs[len(operands):]
        for i, o in enumerate(outs):
            o[...] = jnp.zeros_like(o)
            r0 = 0
            for p in ins[i * len(REPL):(i + 1) * len(REPL)]:
                for s in range(p.shape[1] // D_MODEL):
                    o[r0:r0 + 1, :] = p[:, s * D_MODEL:(s + 1) * D_MODEL]
                    r0 += 1
        if loss is not None:
            outs[-1][LOSS_ROW:LOSS_ROW + 1, :] = jnp.tile(ins[-1][...], (1, D_MODEL // 128))

    shape = jax.ShapeDtypeStruct((REPL_ROWS, D_MODEL), F32)
    return pl.pallas_call(body, name="pack_repl" + ("_loss" if loss is not None else ""),
                          out_shape=[shape] * len(dicts), compiler_params=_params())(*operands)


def _lane_concat(ref, r0, n):
    return jnp.concatenate([ref[r0 + s:r0 + s + 1, :] for s in range(n)], axis=1)


def unpack_small(packs):
    names = [n for n, _, _ in SMALL_PARTS]

    def body(*refs):
        ins, outs = refs[:len(packs)], refs[len(packs):]
        for i, p in enumerate(ins):
            for (name, nr, shape), o in zip(SMALL_PARTS, outs[i * len(names):(i + 1) * len(names)]):
                r0 = SMALL_OFFSET[name]
                if name in BIASES:
                    o[0] = p[r0:r0 + nr, 0:QUARTER]
                elif name == "ffn_conv_w":
                    for k in range(shape[1]):
                        o[k] = _lane_concat(p, r0 + FFN_ROWS * k, FFN_ROWS)
                elif name == "conv_short_w":
                    for k in range(nr):
                        o[k] = p[r0 + k:r0 + k + 1, :]
                else:
                    o[0] = p[r0:r0 + nr, :].reshape(shape[1:])

    shapes = [jax.ShapeDtypeStruct((s[1], 1, s[2]) if n in TAPS3 else s, F32) for n, _, s in SMALL_PARTS]
    res = pl.pallas_call(body, name="unpack_small", out_shape=shapes * len(packs), compiler_params=_params())(*packs)
    out = []
    for i in range(len(packs)):
        d = dict(zip(names, res[i * len(names):(i + 1) * len(names)]))
        for n in TAPS3:
            d[n] = d[n].transpose(1, 0, 2)
        out.append(d)
    return out


def unpack_repl(packs):
    rows = [(2 * D_FF // D_MODEL) if n == "ffn_conv_b" else 1 for n in REPL]

    def body(*refs):
        ins, outs = refs[:len(packs)], refs[len(packs):]
        for i, p in enumerate(ins):
            r0 = 0
            for nr, o in zip(rows, outs[i * len(REPL):(i + 1) * len(REPL)]):
                o[...] = _lane_concat(p, r0, nr)
                r0 += nr

    shapes = [jax.ShapeDtypeStruct((1, nr * D_MODEL), F32) for nr in rows]
    res = pl.pallas_call(body, name="unpack_repl", out_shape=shapes * len(packs), compiler_params=_params())(*packs)
    return [dict(zip(REPL, res[i * len(REPL):(i + 1) * len(REPL)])) for i in range(len(packs))]


def kernel(x, norm_mix_pre, norm_mix_post, norm_ffn_pre, norm_ffn_post, w_in, conv_short_w, w_conv_branch, lru_conv_w, lru_conv_b, lru_wa, lru_ba, lru_wx, lru_bx, lru_lambda, w_lru_branch, w_out, ffn_w_up, ffn_conv_w, ffn_conv_b, ffn_w_down, loss_target, m_norm_mix_pre, m_norm_mix_post, m_norm_ffn_pre, m_norm_ffn_post, m_w_in, m_conv_short_w, m_w_conv_branch, m_lru_conv_w, m_lru_conv_b, m_lru_wa, m_lru_ba, m_lru_wx, m_lru_bx, m_lru_lambda, m_w_lru_branch, m_w_out, m_ffn_w_up, m_ffn_conv_w, m_ffn_conv_b, m_ffn_w_down, v_norm_mix_pre, v_norm_mix_post, v_norm_ffn_pre, v_norm_ffn_post, v_w_in, v_conv_short_w, v_w_conv_branch, v_lru_conv_w, v_lru_conv_b, v_lru_wa, v_lru_ba, v_lru_wx, v_lru_bx, v_lru_lambda, v_w_lru_branch, v_w_out, v_ffn_w_up, v_ffn_conv_w, v_ffn_conv_b, v_ffn_w_down):
    given = dict(locals())
    w = {n: given[n] for n in WEIGHTS}
    m = {n: given["m_" + n] for n in WEIGHTS}
    v = {n: given["v_" + n] for n in WEIGHTS}

    xi, yi, ci = _position()
    chip_i = 2 * xi + yi
    chip = chip_i.astype(jnp.int32).reshape(1)
    core = ci.astype(jnp.int32).reshape(1)
    xs, target = x[0], loss_target[0]
    g1, g2, g3, g4 = w["norm_mix_pre"], w["norm_mix_post"], w["norm_ffn_pre"], w["norm_ffn_post"]
    shard = {n: cast_bf16(w[n][0], "cast_" + n) for n in BIG}
    small_shard, m_small, v_small = pack_small([w, m, v])

    def gathered(bufs, names):
        return [_own_slot(b, small_shard if n == "small" else shard[n], chip_i) for b, n in zip(bufs, names)]

    def chip_sums(arrays, kinds, tag):
        theirs = pair_split(arrays, kinds, "pair_split_" + tag)
        return [add_pair(g, k, t, core, "pair_add_%s_%d" % (tag, i)) for i, (g, k, t) in enumerate(zip(arrays, kinds, theirs))]

    h1, h1t = norm_in(xs, g1)
    win4, small4 = gathered(run_ride(gather_ride([shard["w_in"], small_shard]), "gather_first"), ("w_in", "small"))
    small = full_small(small4)
    first_up = 256
    (proj,), got = matmul_cols(
        h1, win4, "proj_fwd",
        ride=gather_ride([shard["w_conv_branch"], shard["w_lru_branch"], shard["w_out"], shard["ffn_w_up"]],
                         items=[(0, 0, 256), (1, 0, 256), (2, 0, 256), (3, 0, first_up)]))
    wcb, wlb, wout = [g.reshape(-1, D_MODEL) for g in gathered(got[:3], ("w_conv_branch", "w_lru_branch", "w_out"))]
    got = got[3:]
    up_piece = lambda r0, nr, into=None: gather_ride([shard["ffn_w_up"]], items=[(0, r0, nr)], into=into)
    down_piece = lambda r0, nr, into=None: gather_ride([shard["ffn_w_down"]], items=[(0, r0, nr)], into=into)
    q, ya = mix_conv_fwd(proj, small["conv_short_w"])
    (xl, r, gi, h, yb), got = mix_lru_fwd(
        proj, small["lru_conv_w"], w["lru_conv_b"], small["lru_wa"], small["lru_ba"],
        small["lru_wx"], small["lru_bx"], w["lru_lambda"], ride=up_piece(first_up, 512, got))
    (a, b, merged), got = branch_merge_fwd(ya, yb, wcb, wlb, proj, ride=up_piece(first_up + 512, 256, got))
    (wup4,) = gathered(got, ("ffn_w_up",))
    (mix, x2, h2, h2t), got = mix_out_fwd(merged, wout, xs, g2, g3, ride=down_piece(0, 256))
    (up, act, f), got = ffn_up_act_fwd(h2, wup4, small["ffn_conv_w"], w["ffn_conv_b"], ride=down_piece(256, 512, got))
    wdown = gathered(got, ("ffn_w_down",))[0].reshape(-1, D_MODEL)
    dy, dout, loss, dg4 = ffn_down_loss(f, wdown, x2, target, g4)

    dh2, dwup, dwdown, dfw, dfb = ffn_up_bwd(dout, wdown, up, act, f, small["ffn_conv_w"], wup4, h2t)
    cs_down, cs_up = chip_sums([dwdown, dwup], ["row", "col2"], "ffn")
    down_rows = lambda r0, nr, into=None: exchange_ride([cs_down], items=[(0, r0, nr)], into=into)
    up_rows = lambda r0, nr, into=None: exchange_ride([cs_up], items=[(0, r0, nr)], into=into)
    (dx2, dmix, dg3, dg2), rx_down = norms_mid_bwd(dh2, x2, dy, mix, g3, g2, ride=down_rows(0, 128))
    (da, db, dwout, dgates), rx_down = mix_out_bwd(dmix, wout, merged, a, b, proj, ride=down_rows(128, 256, rx_down))
    (dconv, dwcb, dws), rx_up = mix_conv_bwd(da, wcb, proj, q, small["conv_short_w"], ride=up_rows(0, 176))
    cs_mid = chip_sums([dwout, dwcb], ["row", "row"], "mid")
    (dlru, dwlb, dwa, dwx, dba, dbx, dwl, dbl, dlam), rx_up = mix_lru_bwd(
        db, wlb, proj, xl, r, gi, h, small["lru_conv_w"], small["lru_wa"], small["lru_wx"], w["lru_lambda"],
        ride=up_rows(176, 336, rx_up))
    grads = dict(norm_mix_post=dg2, norm_ffn_pre=dg3, norm_ffn_post=dg4, conv_short_w=dws, lru_conv_w=dwl,
                 lru_conv_b=dbl, lru_wa=dwa, lru_ba=dba, lru_wx=dwx, lru_bx=dbx, lru_lambda=dlam,
                 ffn_conv_w=jnp.concatenate([dfw[0], dfw[1]], axis=1), ffn_conv_b=jnp.concatenate([dfb[0], dfb[1]], axis=1))
    cs_late = chip_sums([dwlb, split_small(grads)], ["row", "lead"], "late")
    dproj = [dconv, dlru, dgates]
    (dwin,), rx_all = matmul_cols_bwd(dproj, h1t, "proj_wgrad", True, ride=exchange_ride(cs_mid + cs_late))
    rx_mid, rx_late = rx_all[:2], rx_all[2:]
    cs_in = chip_sums([dwin], ["col"], "in")
    in_rows = lambda r0, nr, into=None: exchange_ride(cs_in, items=[(0, r0, nr)], into=into)
    (dh1,), rx_in = matmul_cols_bwd(dproj, win4, "proj_dgrad", False, ride=in_rows(0, 384))
    (dx, grads["norm_mix_pre"]), rx_in = norm_in_bwd(dh1, xs, dx2, g1, ride=in_rows(384, 128, rx_in))
    rx_in = rx_in[0]
    (rep_part,) = pack_repl([grads], loss)
    (rep_all,) = run_ride(exchange_ride([], rep=rep_part), "exchange_repl")

    order = (("w_in", cs_in[0], rx_in), ("ffn_w_up", cs_up, rx_up[0]), ("w_conv_branch", cs_mid[1], rx_mid[1]),
             ("w_lru_branch", cs_late[0], rx_late[0]), ("w_out", cs_mid[0], rx_mid[0]),
             ("ffn_w_down", cs_down, rx_down[0]), ("small", cs_late[1], rx_late[1]))
    halves = [sum_chips(rx, cs, chip, "chip_sum_" + n) for n, cs, rx in order]
    me = 4 * xi + 2 * yi + ci
    rep_grad = sum_lead(_own_slot(rep_all, rep_part, me), "device_sum")
    others = pair_swap(halves)

    g_out, d_out, m_out, v_out = {}, {}, {}, {}
    for n, gm, go in zip(BIG, halves[:-1], others[:-1]):
        g, d, nm, nv = adamw_halves(w[n][0], gm, go, m[n][0], v[n][0], core, "adamw_" + n)
        g_out[n], d_out[n], m_out[n], v_out[n] = g[None], d[None], nm[None], nv[None]
    bufs = adamw_halves(small_shard, halves[-1], others[-1], m_small, v_small, core, "adamw_small")
    for dst, part in zip((g_out, d_out, m_out, v_out), unpack_small(bufs)):
        dst.update(part)
    w_rep, m_rep, v_rep = pack_repl([w, m, v])
    d, nm, nv = adamw(w_rep, rep_grad, m_rep, v_rep, "adamw_repl")
    for dst, part in zip((g_out, d_out, m_out, v_out), unpack_repl([rep_grad, d, nm, nv])):
        dst.update(part)

    return (rep_grad[LOSS_ROW, 0], dx[None], *[g_out[n] for n in WEIGHTS], *[d_out[n] for n in WEIGHTS],
            *[m_out[n] for n in WEIGHTS], *[v_out[n] for n in WEIGHTS])
```

```python
import functools
import math

import jax
import jax.numpy as jnp
from jax import lax
from jax.experimental import pallas as pl
from jax.experimental.pallas import tpu as pltpu

F32 = jnp.float32
BF16 = jnp.bfloat16

D_MODEL = 1024
N_CHIPS = 4
N_SEG = 7
D_FF = 3 * D_MODEL
LRU_HEADS = 4
HEAD_DIM = D_MODEL // LRU_HEADS
LRU_C = 8.0
RMS_EPS = 1e-6
CW = 256
FW = 256
SUBLANES = 8
SCAN_UNROLL = 8
VMEM_LIMIT = 58 * 1024 * 1024

ADAM_LR = 0.001
ADAM_B1 = 0.9
ADAM_B2 = 0.999
ADAM_EPS = 1e-08
ADAM_WD = 0.01
ADAM_STEP = 10

_GELU_C = math.sqrt(2.0 / math.pi)
_GELU_K = 0.044715


def _params(**kw):
    return pltpu.CompilerParams(vmem_limit_bytes=VMEM_LIMIT, **kw)


def _sigmoid(x):
    return 1.0 / (1.0 + jnp.exp(-x))


def _gelu(x):
    t = jnp.tanh(_GELU_C * (x + _GELU_K * x * x * x))
    return 0.5 * x * (1.0 + t)


def _gelu_and_grad(x):
    x2 = x * x
    t = jnp.tanh(_GELU_C * (x + _GELU_K * x * x2))
    g = 0.5 * x * (1.0 + t)
    dg = 0.5 * (1.0 + t) + 0.5 * x * (1.0 - t * t) * _GELU_C * (1.0 + 3.0 * _GELU_K * x2)
    return g, dg


def _log_sigmoid(x):
    e = jnp.exp(-jnp.abs(x))
    u = 1.0 + e
    l1p = jnp.where(u == 1.0, e, jnp.log(u) * e / (u - 1.0))
    return jnp.minimum(x, 0.0) - l1p


def _neg_expm1(z):
    series = -z * (1.0 + z * (0.5 + z * (1.0 / 6.0 + z * (1.0 / 24.0 + z * (1.0 / 120.0 + z * (1.0 / 720.0))))))
    return jnp.where(z > -0.2, series, 1.0 - jnp.exp(z))


def _rows(shape):
    return lax.broadcasted_iota(jnp.int32, shape, 0)


def _shift_down(x, k):
    return jnp.where(_rows(x.shape) >= k, pltpu.roll(x, k, 0), 0.0)


def _shift_up(x, k):
    n = x.shape[0]
    return jnp.where(_rows(x.shape) < n - k, pltpu.roll(x, n - k, 0), 0.0)


def _delays(x, k_width):
    return [x] + [_shift_down(x, j) for j in range(1, k_width)]


def _advances(dy, k_width):
    return [dy] + [_shift_up(dy, j) for j in range(1, k_width)]


def _taps_sum(shifted, w_ref, b=None):
    k_width = w_ref.shape[0]
    y = w_ref[k_width - 1:k_width, :] * shifted[0]
    for j in range(1, k_width):
        y = y + w_ref[k_width - 1 - j:k_width - j, :] * shifted[j]
    if b is not None:
        y = y + b
    return y


def _causal_conv(x, w_ref, b=None):
    return _taps_sum(_delays(x, w_ref.shape[0]), w_ref, b)


def _conv_wgrad(advanced, x):
    k_width = len(advanced)
    rows = [jnp.sum(advanced[k_width - 1 - k] * x, axis=0, keepdims=True) for k in range(k_width)]
    return jnp.concatenate(rows, axis=0)


def _dot(a, b):
    return jnp.dot(a, b, preferred_element_type=F32)


def _dot_nt(a, b):
    return lax.dot_general(a, b, (((1,), (1,)), ((), ())), preferred_element_type=F32)


def _dot_tn(a, b):
    return lax.dot_general(a, b, (((0,), (0,)), ((), ())), preferred_element_type=F32)


def _rms_stats(x):
    r = lax.rsqrt(jnp.mean(x * x, axis=-1, keepdims=True) + RMS_EPS)
    return x * r, r


def _rms_bwd(n, r, g, dy):
    dn = dy * g
    dx = r * (dn - n * jnp.mean(dn * n, axis=-1, keepdims=True))
    return dx, dy * n


def _scan(a_ref, b_ref, h_ref, reverse):
    n, c = a_ref.shape
    row = lax.broadcasted_iota(jnp.int32, (SUBLANES, c), 0)
    span = SCAN_UNROLL * SUBLANES
    n_trips = n // span

    def within(a, b):
        for k in (1, 2, 4):
            if reverse:
                keep, shift = row < SUBLANES - k, SUBLANES - k
            else:
                keep, shift = row >= k, k
            ap = jnp.where(keep, pltpu.roll(a, shift, 0), 1.0)
            bp = jnp.where(keep, pltpu.roll(b, shift, 0), 0.0)
            b = a * bp + b
            a = a * ap
        return a, b

    def trip(t, carry):
        base = pl.multiple_of((n_trips - 1 - t if reverse else t) * span, span)
        order = list(reversed(range(SCAN_UNROLL))) if reverse else list(range(SCAN_UNROLL))
        loaded = [(a_ref[pl.ds(base + u * SUBLANES, SUBLANES), :], b_ref[pl.ds(base + u * SUBLANES, SUBLANES), :])
                  for u in order]
        out = []
        for a, b in [within(a, b) for a, b in loaded]:
            h = a * carry + b
            out.append(h)
            carry = h[0:1, :] if reverse else h[SUBLANES - 1:SUBLANES, :]
        for u, h in zip(order, out):
            h_ref[pl.ds(base + u * SUBLANES, SUBLANES), :] = h
        return carry

    lax.fori_loop(0, n_trips, trip, jnp.zeros((1, c), F32))


def _scan_forward(a_ref, b_ref, h_ref):
    _scan(a_ref, b_ref, h_ref, False)


def _scan_backward(c_ref, b_ref, g_ref):
    _scan(c_ref, b_ref, g_ref, True)


MESH = pl.DeviceIdType.MESH
_HBM = pl.BlockSpec(memory_space=pltpu.HBM)
_OTHER_CHIPS = ((1, 0), (0, 1), (1, 1))
_OTHER_DEVICES = tuple((dx, dy, dc) for dx in (0, 1) for dy in (0, 1) for dc in (0, 1) if dx or dy or dc)
N_DEVICES = 8


def _position():
    return lax.axis_index("x"), lax.axis_index("y"), lax.axis_index("c")


def _flip(v, d):
    return 1 - v if d else v


def _chip(x, y, p):
    px, py = _flip(x, _OTHER_CHIPS[p][0]), _flip(y, _OTHER_CHIPS[p][1])
    return px, py, 2 * px + py


class _Ride:
    def __init__(self, srcs, bufs, scratch, plan, collective_id):
        self.srcs, self.bufs, self.scratch, self.plan = list(srcs), list(bufs), list(scratch), plan
        self.collective_id = collective_id


NEIGHBOURS_AND_SIBLING = 1
OTHER_CHIPS_SAME_CORE = 2
ALL_DEVICES = 3
SIBLING = 4


def _handshake(peers):
    barrier = pltpu.get_barrier_semaphore()
    for peer in peers:
        pl.semaphore_signal(barrier, inc=1, device_id=peer, device_id_type=MESH)
    pl.semaphore_wait(barrier, len(peers))


def _call(body, *, name, grid, in_specs, out_specs, out_shape, operands, scratch_shapes=(), ride=None):
    in_specs, out_specs, out_shape = list(in_specs), list(out_specs), list(out_shape)
    scratch_shapes = list(scratch_shapes)
    if ride is None:
        return pl.pallas_call(body, name=name, grid=grid, in_specs=in_specs, out_specs=out_specs, out_shape=out_shape,
                              scratch_shapes=scratch_shapes, compiler_params=_params())(*operands)
    n_in, n_out, n_scr = len(in_specs), len(out_shape), len(scratch_shapes)
    old = [i for i, b in enumerate(ride.bufs) if not isinstance(b, jax.ShapeDtypeStruct)]
    n_src, n_old, n_buf = len(ride.srcs), len(old), len(ride.bufs)

    def full_body(*refs):
        o0 = n_in + n_src + n_old
        s0 = o0 + n_out + n_buf
        start, relay, relay_on, finish = ride.plan(refs[n_in:n_in + n_src], refs[o0 + n_out:s0], refs[s0 + n_scr:])
        ids = [pl.program_id(i) for i in range(len(grid))]
        first = functools.reduce(jnp.logical_and, [i == 0 for i in ids])
        middle = functools.reduce(jnp.logical_and, [ids[0] == grid[0] // 2] + [i == 0 for i in ids[1:]])
        last = functools.reduce(jnp.logical_and, [i == g - 1 for i, g in zip(ids, grid)])
        pl.when(first)(start)
        pl.when(middle)(relay)
        pl.when(last)(relay_on)
        body(*refs[:n_in], *refs[o0:o0 + n_out], *refs[s0:s0 + n_scr])
        pl.when(last)(finish)

    shapes = [jax.ShapeDtypeStruct(b.shape, b.dtype) for b in ride.bufs]
    res = pl.pallas_call(
        full_body, name=name, grid=grid,
        in_specs=in_specs + [_HBM] * (n_src + n_old), out_specs=out_specs + [_HBM] * n_buf,
        out_shape=out_shape + shapes, scratch_shapes=scratch_shapes + ride.scratch,
        input_output_aliases={n_in + n_src + k: n_out + i for k, i in enumerate(old)},
        compiler_params=_params(collective_id=ride.collective_id),
    )(*operands, *ride.srcs, *[ride.bufs[i] for i in old])
    return list(res[:n_out]), list(res[n_out:])


def run_ride(ride, name):
    def body():
        pass

    return _call(body, name=name, grid=(1,), in_specs=[], out_specs=[], out_shape=[], operands=[], ride=ride)[1]


def gather_ride(shards, items=None, into=None):
    items = items or [(a, 0, s.shape[0]) for a, s in enumerate(shards)]
    bufs = into or [jax.ShapeDtypeStruct((N_CHIPS,) + s.shape, s.dtype) for s in shards]
    nrel = len(_OTHER_CHIPS)

    def plan(srcs, dsts, sems):
        ici_send, ici_recv, hop_send, hop_recv, sib_send, sib_recv = sems
        x, y, c = _position()
        j = 2 * x + y

        def rows(ref, it, h, q=None):
            half = it[2] // 2
            if q is None:
                return ref.at[pl.ds(it[1] + h * half, half), :]
            return ref.at[pl.ds(it[1] + h * half + q * (half // 2), half // 2), :]

        def ici(i, p, slot):
            it = items[i]
            px, py, _ = _chip(x, y, p)
            return pltpu.make_async_remote_copy(
                src_ref=rows(srcs[it[0]], it, c), dst_ref=rows(dsts[it[0]].at[slot], it, c),
                send_sem=ici_send.at[i * nrel + p], recv_sem=ici_recv.at[i * nrel + p],
                device_id=(px, py, c), device_id_type=MESH)

        def hop(i, p, slot):
            it = items[i]
            part = rows(dsts[it[0]].at[slot], it, c, p)
            px, py, _ = _chip(x, y, 1 - p)
            return pltpu.make_async_remote_copy(
                src_ref=part, dst_ref=part, send_sem=hop_send.at[i * 2 + p], recv_sem=hop_recv.at[i * 2 + p],
                device_id=(px, py, c), device_id_type=MESH)

        def sib(i, p, h):
            it = items[i]
            part = rows(dsts[it[0]].at[_chip(x, y, p)[2]], it, h)
            return pltpu.make_async_remote_copy(
                src_ref=part, dst_ref=part, send_sem=sib_send.at[i * nrel + p], recv_sem=sib_recv.at[i * nrel + p],
                device_id=(x, y, 1 - c), device_id_type=MESH)

        every = range(len(items))
        diag = _chip(x, y, 2)[2]

        def start():
            _handshake([_chip(x, y, 0)[:2] + (c,), _chip(x, y, 1)[:2] + (c,), (x, y, 1 - c)])
            for i in every:
                for p in (0, 1):
                    ici(i, p, j).start()

        def relay():
            for i in every:
                for p in (0, 1):
                    k = _chip(x, y, p)[2]
                    ici(i, p, k).wait_recv()
                    hop(i, p, k).start()
                    sib(i, p, c).start()

        def relay_on():
            for i in every:
                for p in (0, 1):
                    hop(i, p, diag).wait_recv()
                sib(i, 2, c).start()

        def finish():
            for i in every:
                for p in range(nrel):
                    sib(i, p, 1 - c).wait_recv()
            for i in every:
                for p in (0, 1):
                    ici(i, p, j).wait_send()
                    hop(i, p, _chip(x, y, p)[2]).wait_send()
                for p in range(nrel):
                    sib(i, p, c).wait_send()

        return start, relay, relay_on, finish

    n = len(items)
    sems = [pltpu.SemaphoreType.DMA((n * nrel,))] * 2 + [pltpu.SemaphoreType.DMA((n * 2,))] * 2 \
        + [pltpu.SemaphoreType.DMA((n * nrel,))] * 2
    return _Ride(shards, bufs, sems, plan, NEIGHBOURS_AND_SIBLING)


def exchange_ride(sums, items=None, into=None, rep=None):
    items = [(a, 0, s.shape[1]) for a, s in enumerate(sums)] if items is None else items
    into = into or [None] * len(sums)
    bufs = [jax.ShapeDtypeStruct(s.shape, s.dtype) if b is None else b for s, b in zip(sums, into)]
    srcs = list(sums)
    scratch = [pltpu.SemaphoreType.DMA((max(len(items), 1) * len(_OTHER_CHIPS),))] * 2
    if rep is not None:
        srcs.append(rep)
        bufs.append(jax.ShapeDtypeStruct((N_DEVICES,) + rep.shape, rep.dtype))
        scratch += [pltpu.SemaphoreType.DMA((len(_OTHER_DEVICES),))] * 2
    nrel = len(_OTHER_CHIPS)

    def plan(src_refs, dst_refs, sems):
        x, y, c = _position()
        j = 2 * x + y
        me = 4 * x + 2 * y + c

        def part(i, p, src_slot, dst_slot):
            a, r0, nr = items[i]
            px, py, _ = _chip(x, y, p)
            return pltpu.make_async_remote_copy(
                src_ref=src_refs[a].at[src_slot, pl.ds(r0, nr), :], dst_ref=dst_refs[a].at[dst_slot, pl.ds(r0, nr), :],
                send_sem=sems[0].at[i * nrel + p], recv_sem=sems[1].at[i * nrel + p],
                device_id=(px, py, c), device_id_type=MESH)

        def device(q):
            dx, dy, dc = _OTHER_DEVICES[q]
            return _flip(x, dx), _flip(y, dy), _flip(c, dc)

        def rep_copy(q, slot):
            return pltpu.make_async_remote_copy(
                src_ref=src_refs[-1], dst_ref=dst_refs[-1].at[slot], send_sem=sems[2].at[q], recv_sem=sems[3].at[q],
                device_id=device(q), device_id_type=MESH)

        pairs = [(i, p) for i in range(len(items)) for p in range(nrel)]
        others = range(len(_OTHER_DEVICES)) if rep is not None else ()

        def start():
            if rep is None:
                _handshake([_chip(x, y, p)[:2] + (c,) for p in range(nrel)])
            else:
                _handshake([device(q) for q in others])
            for i, p in pairs:
                part(i, p, _chip(x, y, p)[2], j).start()
            for q in others:
                rep_copy(q, me).start()

        def finish():
            for i, p in pairs:
                k = _chip(x, y, p)[2]
                part(i, p, k, k).wait_recv()
            for q in others:
                px, py, pc = device(q)
                rep_copy(q, 4 * px + 2 * py + pc).wait_recv()
            for i, p in pairs:
                part(i, p, _chip(x, y, p)[2], j).wait_send()
            for q in others:
                rep_copy(q, me).wait_send()

        return start, lambda: None, lambda: None, finish

    return _Ride(srcs, bufs, scratch, plan, OTHER_CHIPS_SAME_CORE if rep is None else ALL_DEVICES)


def _own_slot(buf, own, index):
    return lax.dynamic_update_slice(buf, own[None], (index,) + (0,) * own.ndim)


def _token_tile(s):
    return min(s, 512)


def norm_in(x, g):
    s, d = x.shape
    t = _token_tile(s)

    def body(x_ref, g_ref, o_ref, ot_ref):
        n, _ = _rms_stats(x_ref[...])
        h = n * g_ref[...]
        o_ref[...] = h.astype(BF16)
        ot_ref[...] = h.T.astype(BF16)

    return pl.pallas_call(
        body, name="norm_in", grid=(s // t,),
        in_specs=[pl.BlockSpec((t, d), lambda i: (i, 0)), pl.BlockSpec((1, d), lambda i: (0, 0))],
        out_specs=[pl.BlockSpec((t, d), lambda i: (i, 0)), pl.BlockSpec((d, t), lambda i: (0, i))],
        out_shape=[jax.ShapeDtypeStruct((s, d), BF16), jax.ShapeDtypeStruct((d, s), BF16)],
        compiler_params=_params(),
    )(x, g)


def matmul_cols(a, w4, name, ride=None):
    m, k = a.shape
    nj, _, ns = w4.shape
    nb = ns // CW

    def body(a_ref, w_ref, o_ref):
        o_ref[...] = _dot(a_ref[...], w_ref[0])

    return _call(
        body, name=name, grid=(nj, nb),
        in_specs=[pl.BlockSpec((m, k), lambda j, b: (0, 0)),
                  pl.BlockSpec((1, k, CW), lambda j, b: (j, 0, b))],
        out_specs=[pl.BlockSpec((m, CW), lambda j, b: (0, j * nb + b))],
        out_shape=[jax.ShapeDtypeStruct((m, nj * ns), F32)],
        operands=(a, w4), ride=ride)


def mix_conv_fwd(proj, ws, ride=None):
    s = proj.shape[0]
    nblk = D_MODEL // CW

    def body(cb_ref, cc_ref, cx_ref, ws_ref, q_ref, ya_ref):
        q = _causal_conv(cc_ref[...] * cx_ref[...], ws_ref)
        q_ref[...] = q
        ya_ref[...] = (cb_ref[...] * q).astype(BF16)

    seg = lambda k: pl.BlockSpec((s, CW), lambda c, k=k: (0, k * nblk + c))
    return _call(
        body, name="mix_conv_fwd", grid=(nblk,),
        in_specs=[seg(0), seg(1), seg(2), pl.BlockSpec((3, CW), lambda c: (0, c))],
        out_specs=[pl.BlockSpec((s, CW), lambda c: (0, c))] * 2,
        out_shape=[jax.ShapeDtypeStruct((s, D_MODEL), F32), jax.ShapeDtypeStruct((s, D_MODEL), BF16)],
        operands=(proj, proj, proj, ws), ride=ride)


def _lru_gates(r, ls):
    log_a = LRU_C * r * ls
    a = jnp.exp(log_a)
    mult = jnp.sqrt(_neg_expm1(2.0 * log_a))
    mult = jnp.where(_rows(r.shape) == 0, 1.0, mult)
    return a, mult


def mix_lru_fwd(proj, wl, bl, wa, ba, wx, bx, lam, ride=None):
    s = proj.shape[0]
    nblk = D_MODEL // CW

    def body(lx_ref, ly_ref, wl_ref, bl_ref, wa_ref, ba_ref, wx_ref, bx_ref, lam_ref,
             xl_ref, r_ref, i_ref, h_ref, yb_ref, a_scr, u_scr):
        xl = _causal_conv(lx_ref[...], wl_ref, bl_ref[...])
        xlb = xl.astype(BF16)
        xl_ref[...] = xlb
        r = _sigmoid(_dot(xlb, wa_ref[0]) + ba_ref[...])
        i = _sigmoid(_dot(xlb, wx_ref[0]) + bx_ref[...])
        r_ref[...] = r.astype(BF16)
        i_ref[...] = i.astype(BF16)
        a, mult = _lru_gates(r, _log_sigmoid(lam_ref[...]))
        a_scr[...] = a
        u_scr[...] = mult * i * xl
        _scan_forward(a_scr, u_scr, h_ref)
        yb_ref[...] = (h_ref[...] * _gelu(ly_ref[...])).astype(BF16)

    blk = lambda k: pl.BlockSpec((s, CW), lambda c, k=k: (0, k * nblk + c))
    vec = pl.BlockSpec((1, CW), lambda c: (0, c))
    mat = pl.BlockSpec((1, CW, CW), lambda c: (c, 0, 0))
    out = pl.BlockSpec((s, CW), lambda c: (0, c))
    f = jax.ShapeDtypeStruct((s, D_MODEL), F32)
    hb = jax.ShapeDtypeStruct((s, D_MODEL), BF16)
    return _call(
        body, name="mix_lru_fwd", grid=(nblk,),
        in_specs=[blk(3), blk(4), pl.BlockSpec((4, CW), lambda c: (0, c)), vec, mat, vec, mat, vec, vec],
        out_specs=[out] * 5,
        out_shape=[hb, hb, hb, f, hb],
        scratch_shapes=[pltpu.VMEM((s, CW), F32), pltpu.VMEM((s, CW), F32)],
        operands=(proj, proj, wl, bl, wa, ba, wx, bx, lam), ride=ride)


def branch_merge_fwd(ya, yb, wcb, wlb, proj, ride=None):
    s = ya.shape[0]
    nblk = D_MODEL // CW

    def body(ya_ref, yb_ref, wcb_ref, wlb_ref, gc_ref, gl_ref, a_ref, b_ref, m_ref):
        a = _dot(ya_ref[...], wcb_ref[...])
        b = _dot(yb_ref[...], wlb_ref[...])
        a_ref[...] = a
        b_ref[...] = b
        m_ref[...] = (_sigmoid(gc_ref[...]) * a + _sigmoid(gl_ref[...]) * b).astype(BF16)

    res = pl.BlockSpec((s, D_MODEL), lambda n: (0, 0))
    wcol = pl.BlockSpec((D_MODEL, CW), lambda n: (0, n))
    blk = lambda k: pl.BlockSpec((s, CW), lambda n, k=k: (0, k * nblk + n))
    out = pl.BlockSpec((s, CW), lambda n: (0, n))
    f = jax.ShapeDtypeStruct((s, D_MODEL), F32)
    return _call(
        body, name="branch_merge_fwd", grid=(nblk,),
        in_specs=[res, res, wcol, wcol, blk(5), blk(6)],
        out_specs=[out] * 3,
        out_shape=[f, f, jax.ShapeDtypeStruct((s, D_MODEL), BF16)],
        operands=(ya, yb, wcb, wlb, proj, proj), ride=ride)


def mix_out_fwd(merged, wout, x, g2, g3, ride=None):
    s, d = x.shape
    t = _token_tile(s)

    def body(m_ref, w_ref, x_ref, g2_ref, g3_ref, mix_ref, x2_ref, h2_ref, h2t_ref):
        mix = _dot(m_ref[...], w_ref[...])
        mix_ref[...] = mix
        n, _ = _rms_stats(mix)
        x2 = x_ref[...] + n * g2_ref[...]
        x2_ref[...] = x2
        n2, _ = _rms_stats(x2)
        h2 = n2 * g3_ref[...]
        h2_ref[...] = h2.astype(BF16)
        h2t_ref[...] = h2.T.astype(BF16)

    tile = pl.BlockSpec((t, d), lambda i: (i, 0))
    vec = pl.BlockSpec((1, d), lambda i: (0, 0))
    f = jax.ShapeDtypeStruct((s, d), F32)
    return _call(
        body, name="mix_out_fwd", grid=(s // t,),
        in_specs=[tile, pl.BlockSpec((d, d), lambda i: (0, 0)), tile, vec, vec],
        out_specs=[tile] * 3 + [pl.BlockSpec((d, t), lambda i: (0, i))],
        out_shape=[f, f, jax.ShapeDtypeStruct((s, d), BF16), jax.ShapeDtypeStruct((d, s), BF16)],
        operands=(merged, wout, x, g2, g3), ride=ride)


def ffn_up_act_fwd(h2, wup4, fw, fb, ride=None):
    s, k = h2.shape
    ns = wup4.shape[2]
    per_chip = ns // CW
    nblk = D_FF // CW

    def body(h_ref, wg_ref, wv_ref, cg_ref, cv_ref, bg_ref, bv_ref, up_ref, act_ref, f_ref):
        h = h_ref[...]
        ug = _dot(h, wg_ref[0])
        uv = _dot(h, wv_ref[0])
        up_ref[0] = ug
        up_ref[1] = uv
        gate = _causal_conv(ug, cg_ref, bg_ref[...])
        val = _causal_conv(uv, cv_ref, bv_ref[...])
        act_ref[0] = gate.astype(BF16)
        act_ref[1] = val.astype(BF16)
        f_ref[...] = (_gelu(gate) * val).astype(BF16)

    wcols = lambda h: pl.BlockSpec((1, k, CW), lambda n, h=h: (n // per_chip + 2 * h, 0, n % per_chip))
    half = lambda h, rows: pl.BlockSpec((rows, CW), lambda n, h=h: (0, h * nblk + n))
    both = pl.BlockSpec((2, s, CW), lambda n: (0, 0, n))
    return _call(
        body, name="ffn_up_act_fwd", grid=(nblk,),
        in_specs=[pl.BlockSpec((s, k), lambda n: (0, 0)), wcols(0), wcols(1),
                  half(0, 3), half(1, 3), half(0, 1), half(1, 1)],
        out_specs=[both, both, pl.BlockSpec((s, CW), lambda n: (0, n))],
        out_shape=[jax.ShapeDtypeStruct((2, s, D_FF), F32), jax.ShapeDtypeStruct((2, s, D_FF), BF16),
                   jax.ShapeDtypeStruct((s, D_FF), BF16)],
        operands=(h2, wup4, wup4, fw, fw, fb, fb), ride=ride)


def ffn_down_loss(f, wdown, x2, target, g4):
    s, d = x2.shape
    t = _token_tile(s)

    def body(f_ref, w_ref, x2_ref, tg_ref, g4_ref, dy_ref, dout_ref, loss_ref, dg4_ref):
        @pl.when(pl.program_id(0) == 0)
        def _():
            loss_ref[...] = jnp.zeros_like(loss_ref)
            dg4_ref[...] = jnp.zeros_like(dg4_ref)

        out = _dot(f_ref[...], w_ref[...])
        n, r = _rms_stats(out)
        err = x2_ref[...] + n * g4_ref[...] - tg_ref[...]
        loss_ref[...] += jnp.full(loss_ref.shape, (0.5 / d) * jnp.sum(err * err), F32)
        dy = err * (1.0 / d)
        dy_ref[...] = dy
        dout, dg = _rms_bwd(n, r, g4_ref[...], dy)
        dout_ref[...] = dout.astype(BF16)
        dg4_ref[...] += jnp.sum(dg, axis=0, keepdims=True)

    tile = pl.BlockSpec((t, d), lambda i: (i, 0))
    vec = pl.BlockSpec((1, d), lambda i: (0, 0))
    return pl.pallas_call(
        body, name="ffn_down_loss", grid=(s // t,),
        in_specs=[pl.BlockSpec((t, D_FF), lambda i: (i, 0)), pl.BlockSpec((D_FF, d), lambda i: (0, 0)), tile, tile, vec],
        out_specs=[tile, tile, pl.BlockSpec((1, 128), lambda i: (0, 0)), vec],
        out_shape=[jax.ShapeDtypeStruct((s, d), F32), jax.ShapeDtypeStruct((s, d), BF16),
                   jax.ShapeDtypeStruct((1, 128), F32), jax.ShapeDtypeStruct((1, d), F32)],
        compiler_params=_params(),
    )(f, wdown, x2, target, g4)


def ffn_up_bwd(dout, wdown, up, act, f, fw, wup4, h2t, ride=None):
    k, s = h2t.shape
    nblk = D_FF // FW
    per_chip = wup4.shape[2] // FW

    def body(do_ref, wd_ref, up_ref, act_ref, f_ref, cg_ref, cv_ref, wg_ref, wv_ref, h_ref,
             dh_ref, dwu_ref, dwd_ref, dw_ref, db_ref, dup_scr):
        @pl.when(pl.program_id(0) == 0)
        def _():
            dup_scr[...] = jnp.zeros_like(dup_scr)
            dh_ref[...] = jnp.zeros_like(dh_ref)

        do = do_ref[...]
        df = _dot_nt(do, wd_ref[...])
        dg = dup_scr[0]
        dv = dup_scr[1]
        ht = h_ref[...]
        dh_ref[...] += _dot_nt(dg, wg_ref[0]) + _dot_nt(dv, wv_ref[0])
        dwu_ref[0] = _dot(ht, dg).astype(BF16)
        dwu_ref[1] = _dot(ht, dv).astype(BF16)
        dwd_ref[...] = _dot_tn(f_ref[...], do).astype(BF16)
        val = act_ref[1].astype(F32)
        ge, dge = _gelu_and_grad(act_ref[0].astype(F32))
        dgate = _advances(df * val * dge, 3)
        dval = _advances(df * ge, 3)
        dw_ref[0] = _conv_wgrad(dgate, up_ref[0])
        dw_ref[1] = _conv_wgrad(dval, up_ref[1])
        db_ref[0] = jnp.sum(dgate[0], axis=0, keepdims=True)
        db_ref[1] = jnp.sum(dval[0], axis=0, keepdims=True)
        dup_scr[0] = _taps_sum(dgate, cg_ref).astype(BF16)
        dup_scr[1] = _taps_sum(dval, cv_ref).astype(BF16)

    cur = lambda n: jnp.minimum(n, nblk - 1)
    prev = lambda n: jnp.maximum(n - 1, 0)
    once = pl.Buffered(1)
    both = lambda rows: pl.BlockSpec((2, rows, FW), lambda n: (0, 0, cur(n)))
    taps = lambda h: pl.BlockSpec((3, FW), lambda n, h=h: (0, h * nblk + cur(n)))
    wcols = lambda h: pl.BlockSpec((1, k, FW), lambda n, h=h: (prev(n) // per_chip + 2 * h, 0, prev(n) % per_chip))
    return _call(
        body, name="ffn_up_bwd", grid=(nblk + 1,),
        in_specs=[pl.BlockSpec((s, D_MODEL), lambda n: (0, 0), pipeline_mode=once),
                  pl.BlockSpec((FW, D_MODEL), lambda n: (cur(n), 0)), both(s), both(s),
                  pl.BlockSpec((s, FW), lambda n: (0, cur(n))), taps(0), taps(1), wcols(0), wcols(1),
                  pl.BlockSpec((k, s), lambda n: (0, 0), pipeline_mode=once)],
        out_specs=[pl.BlockSpec((s, k), lambda n: (0, 0), pipeline_mode=once),
                   pl.BlockSpec((2, k, FW), lambda n: (0, 0, prev(n))),
                   pl.BlockSpec((FW, D_MODEL), lambda n: (cur(n), 0)), both(3), both(1)],
        out_shape=[jax.ShapeDtypeStruct((s, k), F32), jax.ShapeDtypeStruct((2, k, D_FF), BF16),
                   jax.ShapeDtypeStruct((D_FF, D_MODEL), BF16),
                   jax.ShapeDtypeStruct((2, 3, D_FF), F32), jax.ShapeDtypeStruct((2, 1, D_FF), F32)],
        scratch_shapes=[pltpu.VMEM((2, s, FW), BF16)],
        operands=(dout, wdown, up, act, f, fw, fw, wup4, wup4, h2t), ride=ride)


def matmul_cols_bwd(dy, other, name, wgrad, ride=None):
    m = dy[0].shape[1]
    if wgrad:
        k = other.shape[0]
        nj, nb = N_CHIPS, sum(d.shape[0] * d.shape[2] for d in dy) // (N_CHIPS * CW)
    else:
        nj, k, ns = other.shape
        nb = ns // CW
    per_seg = dy[0].shape[2] // CW
    first = [sum(d.shape[0] for d in dy[:i]) for i in range(len(dy))]

    def segment(j, b):
        return (j * nb + b) // per_seg, (j * nb + b) % per_seg

    def body(*refs):
        dy_refs, (o_ref, r_ref) = refs[:len(dy)], refs[len(dy):]
        seg, _ = segment(pl.program_id(0), pl.program_id(1))
        dyb = dy_refs[-1][0]
        for i in range(len(dy) - 2, -1, -1):
            dyb = jnp.where(seg < first[i + 1], dy_refs[i][0], dyb)
        if wgrad:
            r_ref[...] = _dot(o_ref[...], dyb).astype(BF16)
        else:
            @pl.when((pl.program_id(0) == 0) & (pl.program_id(1) == 0))
            def _():
                r_ref[...] = jnp.zeros_like(r_ref)

            r_ref[...] += _dot_nt(dyb, o_ref[0])

    def dy_spec(i):
        nseg = dy[i].shape[0]

        def index(j, b):
            seg, col = segment(j, b)
            local = seg - first[i]
            return (jnp.clip(local, 0, nseg - 1), 0,
                    jnp.where(local < 0, 0, jnp.where(local >= nseg, per_seg - 1, col)))

        return pl.BlockSpec((1, m, CW), index)

    if wgrad:
        other_spec = pl.BlockSpec((k, m), lambda j, b: (0, 0))
        out_spec = pl.BlockSpec((k, CW), lambda j, b: (0, j * nb + b))
        out_shape = jax.ShapeDtypeStruct((k, nj * nb * CW), BF16)
    else:
        other_spec = pl.BlockSpec((1, k, CW), lambda j, b: (j, 0, b))
        out_spec = pl.BlockSpec((m, k), lambda j, b: (0, 0))
        out_shape = jax.ShapeDtypeStruct((m, k), F32)
    return _call(
        body, name=name, grid=(nj, nb), in_specs=[dy_spec(i) for i in range(len(dy))] + [other_spec],
        out_specs=[out_spec], out_shape=[out_shape], operands=(*dy, other), ride=ride)


def norms_mid_bwd(dh2, x2, dy, mix, g3, g2, ride=None):
    s, d = x2.shape
    t = _token_tile(s)

    def body(dh2_ref, x2_ref, dy_ref, mix_ref, g3_ref, g2_ref, dx2_ref, dmix_ref, dg3_ref, dg2_ref):
        @pl.when(pl.program_id(0) == 0)
        def _():
            dg3_ref[...] = jnp.zeros_like(dg3_ref)
            dg2_ref[...] = jnp.zeros_like(dg2_ref)

        n3, r3 = _rms_stats(x2_ref[...])
        dx, dg3 = _rms_bwd(n3, r3, g3_ref[...], dh2_ref[...])
        dx2 = dy_ref[...] + dx
        dx2_ref[...] = dx2
        dg3_ref[...] += jnp.sum(dg3, axis=0, keepdims=True)
        n2, r2 = _rms_stats(mix_ref[...])
        dmix, dg2 = _rms_bwd(n2, r2, g2_ref[...], dx2)
        dmix_ref[...] = dmix.astype(BF16)
        dg2_ref[...] += jnp.sum(dg2, axis=0, keepdims=True)

    tile = pl.BlockSpec((t, d), lambda i: (i, 0))
    vec = pl.BlockSpec((1, d), lambda i: (0, 0))
    v = jax.ShapeDtypeStruct((1, d), F32)
    return _call(
        body, name="norms_mid_bwd", grid=(s // t,),
        in_specs=[tile, tile, tile, tile, vec, vec],
        out_specs=[tile, tile, vec, vec],
        out_shape=[jax.ShapeDtypeStruct((s, d), F32), jax.ShapeDtypeStruct((s, d), BF16), v, v],
        operands=(dh2, x2, dy, mix, g3, g2), ride=ride)


def mix_out_bwd(dmix, wout, merged, a, b, proj, ride=None):
    s = dmix.shape[0]
    nblk = D_MODEL // CW

    def body(dm_ref, w_ref, mg_ref, a_ref, b_ref, gc_ref, gl_ref, da_ref, db_ref, dw_ref, dg_ref):
        dm = dm_ref[...]
        dmerged = _dot_nt(dm, w_ref[...])
        dw_ref[...] = _dot_tn(mg_ref[...], dm).astype(BF16)
        sc = _sigmoid(gc_ref[...])
        sl = _sigmoid(gl_ref[...])
        da_ref[...] = (dmerged * sc).astype(BF16)
        db_ref[...] = (dmerged * sl).astype(BF16)
        dg_ref[0] = (dmerged * a_ref[...] * sc * (1.0 - sc)).astype(BF16)
        dg_ref[1] = (dmerged * b_ref[...] * sl * (1.0 - sl)).astype(BF16)

    res = pl.BlockSpec((s, D_MODEL), lambda n: (0, 0))
    rows = pl.BlockSpec((CW, D_MODEL), lambda n: (n, 0))
    col = pl.BlockSpec((s, CW), lambda n: (0, n))
    blk = lambda k: pl.BlockSpec((s, CW), lambda n, k=k: (0, k * nblk + n))
    hb = jax.ShapeDtypeStruct((s, D_MODEL), BF16)
    return _call(
        body, name="mix_out_bwd", grid=(nblk,),
        in_specs=[res, rows, col, col, col, blk(5), blk(6)],
        out_specs=[col, col, rows, pl.BlockSpec((2, s, CW), lambda n: (0, 0, n))],
        out_shape=[hb, hb, jax.ShapeDtypeStruct((D_MODEL, D_MODEL), BF16), jax.ShapeDtypeStruct((2, s, D_MODEL), BF16)],
        operands=(dmix, wout, merged, a, b, proj, proj), ride=ride)


def mix_conv_bwd(da, wcb, proj, q, ws, ride=None):
    s = da.shape[0]
    nblk = D_MODEL // CW

    def body(da_ref, w_ref, cb_ref, cc_ref, cx_ref, q_ref, ws_ref, dc_ref, dw_ref, dws_ref):
        dab = da_ref[...]
        dya = _dot_nt(dab, w_ref[...])
        cb = cb_ref[...]
        cc = cc_ref[...]
        cx = cx_ref[...]
        q = q_ref[...]
        dw_ref[...] = _dot_tn((cb * q).astype(BF16), dab).astype(BF16)
        dc_ref[0] = (dya * q).astype(BF16)
        dq = _advances(dya * cb, 3)
        dp = _taps_sum(dq, ws_ref)
        dws_ref[...] = _conv_wgrad(dq, cc * cx)
        dc_ref[1] = (dp * cx).astype(BF16)
        dc_ref[2] = (dp * cc).astype(BF16)

    res = pl.BlockSpec((s, D_MODEL), lambda n: (0, 0))
    rows = pl.BlockSpec((CW, D_MODEL), lambda n: (n, 0))
    col = pl.BlockSpec((s, CW), lambda n: (0, n))
    blk = lambda k: pl.BlockSpec((s, CW), lambda n, k=k: (0, k * nblk + n))
    taps = pl.BlockSpec((3, CW), lambda n: (0, n))
    hb = jax.ShapeDtypeStruct((s, D_MODEL), BF16)
    return _call(
        body, name="mix_conv_bwd", grid=(nblk,),
        in_specs=[res, rows, blk(0), blk(1), blk(2), col, taps],
        out_specs=[pl.BlockSpec((3, s, CW), lambda n: (0, 0, n)), rows, taps],
        out_shape=[jax.ShapeDtypeStruct((3, s, D_MODEL), BF16), jax.ShapeDtypeStruct((D_MODEL, D_MODEL), BF16),
                   jax.ShapeDtypeStruct((3, D_MODEL), F32)],
        operands=(da, wcb, proj, proj, proj, q, ws), ride=ride)


def mix_lru_bwd(db, wlb, proj, xl, r, i, h, wl, wa, wx, lam, ride=None):
    s = db.shape[0]
    nblk = D_MODEL // CW

    def body(db_ref, w_ref, lx_ref, ly_ref, xl_ref, r_ref, i_ref, h_ref, wl_ref, wa_ref, wx_ref, lam_ref,
             dl_ref, dw_ref, dwa_ref, dwx_ref, dba_ref, dbx_ref, dwl_ref, dbl_ref, dlam_ref,
             c_scr, g_scr):
        dbb = db_ref[...]
        dyb = _dot_nt(dbb, w_ref[...])
        h = h_ref[...]
        ge, dge = _gelu_and_grad(ly_ref[...])
        dw_ref[...] = _dot_tn((h * ge).astype(BF16), dbb).astype(BF16)
        dl_ref[1] = (dyb * h * dge).astype(BF16)
        r = r_ref[...].astype(F32)
        gi = i_ref[...].astype(F32)
        xlb = xl_ref[...]
        xl = xlb.astype(F32)
        lam = lam_ref[...]
        ls = _log_sigmoid(lam)
        a, mult = _lru_gates(r, ls)
        c_scr[...] = _shift_up(a, 1)
        g_scr[...] = dyb * ge
        _scan_backward(c_scr, g_scr, g_scr)
        du = g_scr[...]
        da = du * _shift_down(h, 1)
        dmult = du * gi * xl
        di = du * mult * xl
        dxl = du * mult * gi
        first = _rows(a.shape) == 0
        dlog_a = da * a - jnp.where(first, 0.0, dmult * a * a / mult)
        dr = dlog_a * (LRU_C * ls)
        dlam_ref[...] = jnp.sum(dlog_a * r, axis=0, keepdims=True) * (LRU_C * (1.0 - _sigmoid(lam)))
        dzr = dr * r * (1.0 - r)
        dzi = di * gi * (1.0 - gi)
        dba_ref[...] = jnp.sum(dzr, axis=0, keepdims=True)
        dbx_ref[...] = jnp.sum(dzi, axis=0, keepdims=True)
        dzrb = dzr.astype(BF16)
        dzib = dzi.astype(BF16)
        dwa_ref[0] = _dot_tn(xlb, dzrb)
        dwx_ref[0] = _dot_tn(xlb, dzib)
        dxl = _advances(dxl + _dot_nt(dzrb, wa_ref[0]) + _dot_nt(dzib, wx_ref[0]), 4)
        dl_ref[0] = _taps_sum(dxl, wl_ref).astype(BF16)
        dwl_ref[...] = _conv_wgrad(dxl, lx_ref[...])
        dbl_ref[...] = jnp.sum(dxl[0], axis=0, keepdims=True)

    res = pl.BlockSpec((s, D_MODEL), lambda n: (0, 0))
    rows = pl.BlockSpec((CW, D_MODEL), lambda n: (n, 0))
    col = pl.BlockSpec((s, CW), lambda n: (0, n))
    blk = lambda k: pl.BlockSpec((s, CW), lambda n, k=k: (0, k * nblk + n))
    taps = pl.BlockSpec((4, CW), lambda n: (0, n))
    vec = pl.BlockSpec((1, CW), lambda n: (0, n))
    mat = pl.BlockSpec((1, CW, CW), lambda n: (n, 0, 0))
    hb = jax.ShapeDtypeStruct((s, D_MODEL), BF16)
    v = jax.ShapeDtypeStruct((1, D_MODEL), F32)
    m = jax.ShapeDtypeStruct((LRU_HEADS, HEAD_DIM, HEAD_DIM), F32)
    scr = pltpu.VMEM((s, CW), F32)
    return _call(
        body, name="mix_lru_bwd", grid=(nblk,),
        in_specs=[res, rows, blk(3), blk(4), col, col, col, col, taps, mat, mat, vec],
        out_specs=[pl.BlockSpec((2, s, CW), lambda n: (0, 0, n)), rows, mat, mat, vec, vec, taps, vec, vec],
        out_shape=[jax.ShapeDtypeStruct((2, s, D_MODEL), BF16), jax.ShapeDtypeStruct((D_MODEL, D_MODEL), BF16), m, m, v, v,
                   jax.ShapeDtypeStruct((4, D_MODEL), F32), v, v],
        scratch_shapes=[scr, scr],
        operands=(db, wlb, proj, proj, xl, r, i, h, wl, wa, wx, lam), ride=ride)


def norm_in_bwd(dh1, x, dx2, g1, ride=None):
    s, d = x.shape
    t = _token_tile(s)

    def body(dh_ref, x_ref, dx2_ref, g_ref, dx_ref, dg_ref):
        @pl.when(pl.program_id(0) == 0)
        def _():
            dg_ref[...] = jnp.zeros_like(dg_ref)

        n, r = _rms_stats(x_ref[...])
        dx, dg = _rms_bwd(n, r, g_ref[...], dh_ref[...])
        dx_ref[...] = dx2_ref[...] + dx
        dg_ref[...] += jnp.sum(dg, axis=0, keepdims=True)

    tile = pl.BlockSpec((t, d), lambda i: (i, 0))
    vec = pl.BlockSpec((1, d), lambda i: (0, 0))
    return _call(
        body, name="norm_in_bwd", grid=(s // t,),
        in_specs=[tile, tile, tile, vec],
        out_specs=[tile, vec],
        out_shape=[jax.ShapeDtypeStruct((s, d), F32), jax.ShapeDtypeStruct((1, d), F32)],
        operands=(dh1, x, dx2, g1), ride=ride)


def _owned_part(ref, kind, k, h, hr):
    if kind == "col":
        ns = ref.shape[1] // N_CHIPS
        return ref.at[pl.ds(h * hr, hr), pl.ds(k * ns, ns)]
    if kind == "row":
        return ref.at[pl.ds(k * 2 * hr + h * hr, hr), :]
    if kind == "col2":
        ns = ref.shape[2] // 2
        return ref.at[k // 2, pl.ds(h * hr, hr), pl.ds((k % 2) * ns, ns)]
    return ref.at[k, pl.ds(h * hr, hr), :]


def _part_shape(g, kind):
    if kind == "col2":
        return g.shape[1] // 2, g.shape[2] // 2
    if kind == "col":
        return g.shape[0] // 2, g.shape[1] // N_CHIPS
    if kind == "row":
        return g.shape[0] // (2 * N_CHIPS), g.shape[1]
    return g.shape[1] // 2, g.shape[2]


def pair_split(grads, kinds, name):
    n = len(grads)
    shapes = [_part_shape(g, k) for g, k in zip(grads, kinds)]

    def body(*refs):
        ins, theirs = refs[:n], refs[n:2 * n]
        send_sem, recv_sem = refs[2 * n:]
        x, y, c = _position()
        copies = []
        for a in range(n):
            hr = shapes[a][0]
            for k in range(N_CHIPS):
                s = a * N_CHIPS + k
                copies.append(pltpu.make_async_remote_copy(
                    src_ref=_owned_part(ins[a], kinds[a], k, 1 - c, hr), dst_ref=theirs[a].at[k],
                    send_sem=send_sem.at[s], recv_sem=recv_sem.at[s], device_id=(x, y, 1 - c), device_id_type=MESH))
        _handshake([(x, y, 1 - c)])
        for cp in copies:
            cp.start()
        for cp in copies:
            cp.wait()

    return pl.pallas_call(
        body, name=name,
        in_specs=[_HBM] * n, out_specs=[_HBM] * n,
        out_shape=[jax.ShapeDtypeStruct((N_CHIPS,) + shp, g.dtype) for shp, g in zip(shapes, grads)],
        scratch_shapes=[pltpu.SemaphoreType.DMA((n * N_CHIPS,))] * 2,
        compiler_params=pltpu.CompilerParams(collective_id=SIBLING),
    )(*grads)


def pair_swap(halves):
    n = len(halves)

    def body(*refs):
        ins, outs = refs[:n], refs[n:2 * n]
        send_sem, recv_sem = refs[2 * n:]
        x, y, c = _position()
        copies = [pltpu.make_async_remote_copy(
            src_ref=ins[a], dst_ref=outs[a], send_sem=send_sem.at[a], recv_sem=recv_sem.at[a],
            device_id=(x, y, 1 - c), device_id_type=MESH) for a in range(n)]
        _handshake([(x, y, 1 - c)])
        for cp in copies:
            cp.start()
        for cp in copies:
            cp.wait()

    return pl.pallas_call(
        body, name="pair_swap",
        in_specs=[_HBM] * n, out_specs=[_HBM] * n,
        out_shape=[jax.ShapeDtypeStruct(h.shape, h.dtype) for h in halves],
        scratch_shapes=[pltpu.SemaphoreType.DMA((n,))] * 2,
        compiler_params=pltpu.CompilerParams(collective_id=SIBLING),
    )(*halves)


def _row_tile(rows, cols, limit_bytes=1 << 20):
    best = None
    for t in range(SUBLANES, rows + 1, SUBLANES):
        if rows % t == 0 and t * cols * 4 <= limit_bytes:
            best = t
    return best or rows


def add_pair(g, kind, theirs, core, name):
    nc, rows, cols = theirs.shape
    t = _row_tile(rows, cols, 4 << 20)
    nt = rows // t

    def body(core_ref, g_ref, b_ref, o_ref):
        mine = g_ref[...].reshape(t, cols)
        o_ref[0] = (mine.astype(F32) + b_ref[0].astype(F32)).astype(o_ref.dtype)

    if kind == "col":
        own = pl.BlockSpec((t, cols), lambda k, i, c: (c[0] * nt + i, k))
    elif kind == "col2":
        own = pl.BlockSpec((1, t, cols), lambda k, i, c: (k // 2, c[0] * nt + i, k % 2))
    elif kind == "row":
        own = pl.BlockSpec((t, cols), lambda k, i, c: ((2 * k + c[0]) * nt + i, 0))
    else:
        own = pl.BlockSpec((1, t, cols), lambda k, i, c: (k, c[0] * nt + i, 0))
    spec = pl.BlockSpec((1, t, cols), lambda k, i, c: (k, i, 0))
    return pl.pallas_call(
        body, name=name,
        grid_spec=pltpu.PrefetchScalarGridSpec(num_scalar_prefetch=1, grid=(nc, nt), in_specs=[own, spec], out_specs=spec),
        out_shape=jax.ShapeDtypeStruct(theirs.shape, theirs.dtype), compiler_params=_params(),
    )(core, g, theirs)


def sum_lead(a, name):
    nl, rows, cols = a.shape
    t = _row_tile(rows, cols, (1 << 20) // 2)

    def body(a_ref, o_ref):
        acc = a_ref[0].astype(F32)
        for s in range(1, nl):
            acc = acc + a_ref[s].astype(F32)
        o_ref[...] = acc

    return pl.pallas_call(
        body, name=name, grid=(rows // t,),
        in_specs=[pl.BlockSpec((nl, t, cols), lambda i: (0, i, 0))],
        out_specs=pl.BlockSpec((t, cols), lambda i: (i, 0)),
        out_shape=jax.ShapeDtypeStruct((rows, cols), F32), compiler_params=_params(),
    )(a)


def sum_chips(rx, csum, chip, name):
    nc, rows, cols = rx.shape
    t = _row_tile(rows, cols, 2 << 20)

    def body(chip_ref, r0, r1, r2, r3, own_ref, o_ref):
        acc = None
        for s, ref in enumerate((r0, r1, r2, r3)):
            term = jnp.where(chip_ref[0] == s, own_ref[0], ref[0]).astype(F32)
            acc = term if acc is None else acc + term
        o_ref[...] = acc

    def slot(s):
        return pl.BlockSpec((1, t, cols), lambda i, c, s=s: (jnp.where(c[0] == s, c[0] ^ 1, s), i, 0))

    return pl.pallas_call(
        body, name=name,
        grid_spec=pltpu.PrefetchScalarGridSpec(
            num_scalar_prefetch=1, grid=(rows // t,),
            in_specs=[slot(s) for s in range(nc)] + [pl.BlockSpec((1, t, cols), lambda i, c: (c[0], i, 0))],
            out_specs=pl.BlockSpec((t, cols), lambda i, c: (i, 0))),
        out_shape=jax.ShapeDtypeStruct((rows, cols), F32), compiler_params=_params(),
    )(chip, rx, rx, rx, rx, csum)


def cast_bf16(a, name):
    rows, cols = a.shape
    t = _row_tile(rows, cols, 2 << 20)

    def body(i_ref, o_ref):
        o_ref[...] = i_ref[...].astype(BF16)

    spec = pl.BlockSpec((t, cols), lambda i: (i, 0))
    return pl.pallas_call(body, name=name, grid=(rows // t,), in_specs=[spec], out_specs=spec,
                          out_shape=jax.ShapeDtypeStruct((rows, cols), BF16), compiler_params=_params())(a)


def _adamw_update(w, g, m, v):
    nm = ADAM_B1 * m + (1.0 - ADAM_B1) * g
    nv = ADAM_B2 * v + (1.0 - ADAM_B2) * (g * g)
    m_hat = nm * (1.0 / (1.0 - ADAM_B1 ** ADAM_STEP))
    v_hat = nv * (1.0 / (1.0 - ADAM_B2 ** ADAM_STEP))
    return -ADAM_LR * (m_hat / (jnp.sqrt(v_hat) + ADAM_EPS) + ADAM_WD * w), nm, nv


def adamw(w, g, m, v, name):
    rows, cols = w.shape
    t = _row_tile(rows, cols)

    def body(w_ref, g_ref, m_ref, v_ref, d_ref, nm_ref, nv_ref):
        d_ref[...], nm_ref[...], nv_ref[...] = _adamw_update(w_ref[...], g_ref[...], m_ref[...], v_ref[...])

    spec = pl.BlockSpec((t, cols), lambda i: (i, 0))
    shp = jax.ShapeDtypeStruct((rows, cols), F32)
    return pl.pallas_call(
        body, name=name, grid=(rows // t,), in_specs=[spec] * 4, out_specs=[spec] * 3,
        out_shape=[shp, shp, shp], compiler_params=_params(),
    )(w, g, m, v)


def adamw_halves(w, g_mine, g_other, m, v, core, name):
    rows, cols = w.shape
    hr = rows // 2
    t = _row_tile(hr, cols)
    nt = hr // t

    def body(core_ref, w_ref, gm_ref, go_ref, m_ref, v_ref, g_ref, d_ref, nm_ref, nv_ref):
        g = jnp.where(pl.program_id(0) // nt == core_ref[0], gm_ref[...], go_ref[...])
        g_ref[...] = g
        d_ref[...], nm_ref[...], nv_ref[...] = _adamw_update(w_ref[...], g, m_ref[...], v_ref[...])

    spec = pl.BlockSpec((t, cols), lambda i, c: (i, 0))
    half = pl.BlockSpec((t, cols), lambda i, c: (i % nt, 0))
    shp = jax.ShapeDtypeStruct((rows, cols), F32)
    return pl.pallas_call(
        body, name=name,
        grid_spec=pltpu.PrefetchScalarGridSpec(num_scalar_prefetch=1, grid=(2 * nt,),
                                               in_specs=[spec, half, half, spec, spec], out_specs=[spec] * 4),
        out_shape=[shp] * 4, compiler_params=_params(),
    )(core, w, g_mine, g_other, m, v)


WEIGHTS = ("norm_mix_pre", "norm_mix_post", "norm_ffn_pre", "norm_ffn_post", "w_in", "conv_short_w",
           "w_conv_branch", "lru_conv_w", "lru_conv_b", "lru_wa", "lru_ba", "lru_wx", "lru_bx", "lru_lambda",
           "w_lru_branch", "w_out", "ffn_w_up", "ffn_conv_w", "ffn_conv_b", "ffn_w_down")
BIG = ("w_in", "ffn_w_up", "w_conv_branch", "w_lru_branch", "w_out", "ffn_w_down")
BIG_KIND = ("col", "col", "row", "row", "row", "row")
SMALL = ("conv_short_w", "lru_conv_w", "lru_wa", "lru_ba", "lru_wx", "lru_bx", "ffn_conv_w")
REPL = ("norm_mix_pre", "norm_mix_post", "norm_ffn_pre", "norm_ffn_post", "lru_conv_b", "lru_lambda", "ffn_conv_b")
PACK_W = 256
SMALL_ROWS = 576
REPL_ROWS = 16
LOSS_ROW = 12
FFN_SHARD = 2 * D_FF // N_CHIPS
QUARTER = HEAD_DIM // N_CHIPS
SMALL_PARTS = (("conv_short_w", 3, (1, 3, PACK_W)), ("lru_conv_w", 4, (1, 4, PACK_W)),
               ("lru_wa", LRU_HEADS * QUARTER, (1, LRU_HEADS, QUARTER, HEAD_DIM)), ("lru_ba", LRU_HEADS, (1, LRU_HEADS, QUARTER)),
               ("lru_wx", LRU_HEADS * QUARTER, (1, LRU_HEADS, QUARTER, HEAD_DIM)), ("lru_bx", LRU_HEADS, (1, LRU_HEADS, QUARTER)),
               ("ffn_conv_w", 3 * FFN_SHARD // PACK_W, (1, 3, FFN_SHARD)))


def _pad8(nr):
    return -(-nr // SUBLANES) * SUBLANES


SMALL_OFFSET = {}
for _name, _nr, _ in SMALL_PARTS:
    SMALL_OFFSET[_name] = sum(_pad8(nr) for n, nr, _ in SMALL_PARTS[:len(SMALL_OFFSET)])
FFN_ROWS = FFN_SHARD // PACK_W
BIASES = ("lru_ba", "lru_bx")
TAPS3 = ("conv_short_w", "ffn_conv_w")


def pack_small(dicts):
    names = [n for n, _, _ in SMALL_PARTS]
    operands = [d[n].transpose(1, 0, 2) if n in TAPS3 else d[n] for d in dicts for n in names]

    def body(*refs):
        ins, outs = refs[:len(operands)], refs[len(operands):]
        for i, o in enumerate(outs):
            o[...] = jnp.zeros_like(o)
            for (name, nr, shape), p in zip(SMALL_PARTS, ins[i * len(names):(i + 1) * len(names)]):
                r0 = SMALL_OFFSET[name]
                if name in BIASES:
                    o[r0:r0 + nr, 0:QUARTER] = p[0]
                elif name == "ffn_conv_w":
                    for k in range(shape[1]):
                        for s in range(FFN_ROWS):
                            o[r0 + FFN_ROWS * k + s:r0 + FFN_ROWS * k + s + 1, :] = p[k, :, s * PACK_W:(s + 1) * PACK_W]
                elif name == "conv_short_w":
                    for k in range(nr):
                        o[r0 + k:r0 + k + 1, :] = p[k]
                else:
                    o[r0:r0 + nr, :] = p[0].reshape(nr, PACK_W)

    shape = jax.ShapeDtypeStruct((SMALL_ROWS, PACK_W), F32)
    return pl.pallas_call(body, name="pack_small", out_shape=[shape] * len(dicts), compiler_params=_params())(*operands)


def full_small(g4):
    def body(p, csw, lcw, wa, wx, fcw):
        chips = range(N_CHIPS)
        r0 = SMALL_OFFSET["conv_short_w"]
        csw[...] = jnp.concatenate([p[c, r0:r0 + 3, :] for c in chips], axis=1)
        r0 = SMALL_OFFSET["lru_conv_w"]
        lcw[...] = jnp.concatenate([p[c, r0:r0 + 4, :] for c in chips], axis=1)
        for name, o in (("lru_wa", wa), ("lru_wx", wx)):
            r0 = SMALL_OFFSET[name]
            for h in range(LRU_HEADS):
                for c in chips:
                    o[h, c * QUARTER:(c + 1) * QUARTER, :] = p[c, r0 + h * QUARTER:r0 + (h + 1) * QUARTER, :].astype(BF16)
        r0 = SMALL_OFFSET["ffn_conv_w"]
        for k in range(3):
            fcw[k:k + 1, :] = jnp.concatenate(
                [p[c, r0 + FFN_ROWS * k + s:r0 + FFN_ROWS * k + s + 1, :] for c in chips for s in range(FFN_ROWS)], axis=1)

    mat = jax.ShapeDtypeStruct((LRU_HEADS, HEAD_DIM, HEAD_DIM), BF16)
    csw, lcw, wa, wx, fcw = pl.pallas_call(
        body, name="full_small",
        out_shape=[jax.ShapeDtypeStruct((3, D_MODEL), F32), jax.ShapeDtypeStruct((4, D_MODEL), F32), mat, mat,
                   jax.ShapeDtypeStruct((3, 2 * D_FF), F32)],
        compiler_params=_params())(g4)

    def bias(name):
        r0 = SMALL_OFFSET[name]
        return g4[:, r0:r0 + LRU_HEADS, :QUARTER].transpose(1, 0, 2).reshape(1, D_MODEL)

    return dict(conv_short_w=csw, lru_conv_w=lcw, lru_wa=wa, lru_wx=wx, ffn_conv_w=fcw,
                lru_ba=bias("lru_ba"), lru_bx=bias("lru_bx"))


def split_small(full):
    def bias(name):
        return full[name].reshape(LRU_HEADS, N_CHIPS, QUARTER).transpose(1, 0, 2)

    def body(csw, lcw, wa, wx, fcw, ba, bx, o):
        o[...] = jnp.zeros_like(o)
        for c in range(N_CHIPS):
            cols = slice(c * PACK_W, (c + 1) * PACK_W)
            r0 = SMALL_OFFSET["conv_short_w"]
            o[c, r0:r0 + 3, :] = csw[:, cols]
            r0 = SMALL_OFFSET["lru_conv_w"]
            o[c, r0:r0 + 4, :] = lcw[:, cols]
            for name, p in (("lru_wa", wa), ("lru_wx", wx)):
                r0 = SMALL_OFFSET[name]
                for h in range(LRU_HEADS):
                    o[c, r0 + h * QUARTER:r0 + (h + 1) * QUARTER, :] = p[h, c * QUARTER:(c + 1) * QUARTER, :]
            for name, p in (("lru_ba", ba), ("lru_bx", bx)):
                r0 = SMALL_OFFSET[name]
                o[c, r0:r0 + LRU_HEADS, 0:QUARTER] = p[c]
            r0 = SMALL_OFFSET["ffn_conv_w"]
            for k in range(3):
                for s in range(FFN_ROWS):
                    lo = c * FFN_SHARD + s * PACK_W
                    o[c, r0 + FFN_ROWS * k + s:r0 + FFN_ROWS * k + s + 1, :] = fcw[k:k + 1, lo:lo + PACK_W]

    return pl.pallas_call(
        body, name="split_small", out_shape=jax.ShapeDtypeStruct((N_CHIPS, SMALL_ROWS, PACK_W), F32),
        compiler_params=_params(),
    )(full["conv_short_w"], full["lru_conv_w"], full["lru_wa"], full["lru_wx"], full["ffn_conv_w"],
      bias("lru_ba"), bias("lru_bx"))


def pack_repl(dicts, loss=None):
    operands = [d[n] for d in dicts for n in REPL] + ([loss] if loss is not None else [])

    def body(*refs):
        ins, outs = refs[:len(operands)], refs[len(operands):]
        for i, o in enumerate(outs):
            o[...] = jnp.zeros_like(o)
            r0 = 0
            for p in ins[i * len(REPL):(i + 1) * len(REPL)]:
                for s in range(p.shape[1] // D_MODEL):
                    o[r0:r0 + 1, :] = p[:, s * D_MODEL:(s + 1) * D_MODEL]
                    r0 += 1
        if loss is not None:
            outs[-1][LOSS_ROW:LOSS_ROW + 1, :] = jnp.tile(ins[-1][...], (1, D_MODEL // 128))

    shape = jax.ShapeDtypeStruct((REPL_ROWS, D_MODEL), F32)
    return pl.pallas_call(body, name="pack_repl" + ("_loss" if loss is not None else ""),
                          out_shape=[shape] * len(dicts), compiler_params=_params())(*operands)


def _lane_concat(ref, r0, n):
    return jnp.concatenate([ref[r0 + s:r0 + s + 1, :] for s in range(n)], axis=1)


def unpack_small(packs):
    names = [n for n, _, _ in SMALL_PARTS]

    def body(*refs):
        ins, outs = refs[:len(packs)], refs[len(packs):]
        for i, p in enumerate(ins):
            for (name, nr, shape), o in zip(SMALL_PARTS, outs[i * len(names):(i + 1) * len(names)]):
                r0 = SMALL_OFFSET[name]
                if name in BIASES:
                    o[0] = p[r0:r0 + nr, 0:QUARTER]
                elif name == "ffn_conv_w":
                    for k in range(shape[1]):
                        o[k] = _lane_concat(p, r0 + FFN_ROWS * k, FFN_ROWS)
                elif name == "conv_short_w":
                    for k in range(nr):
                        o[k] = p[r0 + k:r0 + k + 1, :]
                else:
                    o[0] = p[r0:r0 + nr, :].reshape(shape[1:])

    shapes = [jax.ShapeDtypeStruct((s[1], 1, s[2]) if n in TAPS3 else s, F32) for n, _, s in SMALL_PARTS]
    res = pl.pallas_call(body, name="unpack_small", out_shape=shapes * len(packs), compiler_params=_params())(*packs)
    out = []
    for i in range(len(packs)):
        d = dict(zip(names, res[i * len(names):(i + 1) * len(names)]))
        for n in TAPS3:
            d[n] = d[n].transpose(1, 0, 2)
        out.append(d)
    return out


def unpack_repl(packs):
    rows = [(2 * D_FF // D_MODEL) if n == "ffn_conv_b" else 1 for n in REPL]

    def body(*refs):
        ins, outs = refs[:len(packs)], refs[len(packs):]
        for i, p in enumerate(ins):
            r0 = 0
            for nr, o in zip(rows, outs[i * len(REPL):(i + 1) * len(REPL)]):
                o[...] = _lane_concat(p, r0, nr)
                r0 += nr

    shapes = [jax.ShapeDtypeStruct((1, nr * D_MODEL), F32) for nr in rows]
    res = pl.pallas_call(body, name="unpack_repl", out_shape=shapes * len(packs), compiler_params=_params())(*packs)
    return [dict(zip(REPL, res[i * len(REPL):(i + 1) * len(REPL)])) for i in range(len(packs))]


def kernel(x, norm_mix_pre, norm_mix_post, norm_ffn_pre, norm_ffn_post, w_in, conv_short_w, w_conv_branch, lru_conv_w, lru_conv_b, lru_wa, lru_ba, lru_wx, lru_bx, lru_lambda, w_lru_branch, w_out, ffn_w_up, ffn_conv_w, ffn_conv_b, ffn_w_down, loss_target, m_norm_mix_pre, m_norm_mix_post, m_norm_ffn_pre, m_norm_ffn_post, m_w_in, m_conv_short_w, m_w_conv_branch, m_lru_conv_w, m_lru_conv_b, m_lru_wa, m_lru_ba, m_lru_wx, m_lru_bx, m_lru_lambda, m_w_lru_branch, m_w_out, m_ffn_w_up, m_ffn_conv_w, m_ffn_conv_b, m_ffn_w_down, v_norm_mix_pre, v_norm_mix_post, v_norm_ffn_pre, v_norm_ffn_post, v_w_in, v_conv_short_w, v_w_conv_branch, v_lru_conv_w, v_lru_conv_b, v_lru_wa, v_lru_ba, v_lru_wx, v_lru_bx, v_lru_lambda, v_w_lru_branch, v_w_out, v_ffn_w_up, v_ffn_conv_w, v_ffn_conv_b, v_ffn_w_down):
    given = dict(locals())
    w = {n: given[n] for n in WEIGHTS}
    m = {n: given["m_" + n] for n in WEIGHTS}
    v = {n: given["v_" + n] for n in WEIGHTS}

    xi, yi, ci = _position()
    chip_i = 2 * xi + yi
    chip = chip_i.astype(jnp.int32).reshape(1)
    core = ci.astype(jnp.int32).reshape(1)
    xs, target = x[0], loss_target[0]
    g1, g2, g3, g4 = w["norm_mix_pre"], w["norm_mix_post"], w["norm_ffn_pre"], w["norm_ffn_post"]
    shard = {n: cast_bf16(w[n][0], "cast_" + n) for n in BIG}
    small_shard, m_small, v_small = pack_small([w, m, v])

    def gathered(bufs, names):
        return [_own_slot(b, small_shard if n == "small" else shard[n], chip_i) for b, n in zip(bufs, names)]

    def chip_sums(arrays, kinds, tag):
        theirs = pair_split(arrays, kinds, "pair_split_" + tag)
        return [add_pair(g, k, t, core, "pair_add_%s_%d" % (tag, i)) for i, (g, k, t) in enumerate(zip(arrays, kinds, theirs))]

    h1, h1t = norm_in(xs, g1)
    (win4,) = gathered(run_ride(gather_ride([shard["w_in"]]), "gather_first"), ("w_in",))
    first_up = 256
    (proj,), got = matmul_cols(
        h1, win4, "proj_fwd",
        ride=gather_ride([small_shard, shard["w_conv_branch"], shard["w_lru_branch"], shard["w_out"], shard["ffn_w_up"]],
                         items=[(0, 0, SMALL_ROWS), (1, 0, 256), (2, 0, 256), (3, 0, 256), (4, 0, first_up)]))
    small4, wcb, wlb, wout = gathered(got[:4], ("small", "w_conv_branch", "w_lru_branch", "w_out"))
    small = full_small(small4)
    wcb, wlb, wout = [g.reshape(-1, D_MODEL) for g in (wcb, wlb, wout)]
    got = got[4:]
    up_piece = lambda r0, nr, into=None: gather_ride([shard["ffn_w_up"]], items=[(0, r0, nr)], into=into)
    down_piece = lambda r0, nr, into=None: gather_ride([shard["ffn_w_down"]], items=[(0, r0, nr)], into=into)
    q, ya = mix_conv_fwd(proj, small["conv_short_w"])
    (xl, r, gi, h, yb), got = mix_lru_fwd(
        proj, small["lru_conv_w"], w["lru_conv_b"], small["lru_wa"], small["lru_ba"],
        small["lru_wx"], small["lru_bx"], w["lru_lambda"], ride=up_piece(first_up, 512, got))
    (a, b, merged), got = branch_merge_fwd(ya, yb, wcb, wlb, proj, ride=up_piece(first_up + 512, 256, got))
    (wup4,) = gathered(got, ("ffn_w_up",))
    (mix, x2, h2, h2t), got = mix_out_fwd(merged, wout, xs, g2, g3, ride=down_piece(0, 256))
    (up, act, f), got = ffn_up_act_fwd(h2, wup4, small["ffn_conv_w"], w["ffn_conv_b"], ride=down_piece(256, 512, got))
    wdown = gathered(got, ("ffn_w_down",))[0].reshape(-1, D_MODEL)
    dy, dout, loss, dg4 = ffn_down_loss(f, wdown, x2, target, g4)

    dh2, dwup, dwdown, dfw, dfb = ffn_up_bwd(dout, wdown, up, act, f, small["ffn_conv_w"], wup4, h2t)
    cs_down, cs_up = chip_sums([dwdown, dwup], ["row", "col2"], "ffn")
    down_rows = lambda r0, nr, into=None: exchange_ride([cs_down], items=[(0, r0, nr)], into=into)
    up_rows = lambda r0, nr, into=None: exchange_ride([cs_up], items=[(0, r0, nr)], into=into)
    (dx2, dmix, dg3, dg2), rx_down = norms_mid_bwd(dh2, x2, dy, mix, g3, g2, ride=down_rows(0, 128))
    (da, db, dwout, dgates), rx_down = mix_out_bwd(dmix, wout, merged, a, b, proj, ride=down_rows(128, 256, rx_down))
    (dconv, dwcb, dws), rx_up = mix_conv_bwd(da, wcb, proj, q, small["conv_short_w"], ride=up_rows(0, 176))
    cs_mid = chip_sums([dwout, dwcb], ["row", "row"], "mid")
    (dlru, dwlb, dwa, dwx, dba, dbx, dwl, dbl, dlam), rx_up = mix_lru_bwd(
        db, wlb, proj, xl, r, gi, h, small["lru_conv_w"], small["lru_wa"], small["lru_wx"], w["lru_lambda"],
        ride=up_rows(176, 336, rx_up))
    grads = dict(norm_mix_post=dg2, norm_ffn_pre=dg3, norm_ffn_post=dg4, conv_short_w=dws, lru_conv_w=dwl,
                 lru_conv_b=dbl, lru_wa=dwa, lru_ba=dba, lru_wx=dwx, lru_bx=dbx, lru_lambda=dlam,
                 ffn_conv_w=jnp.concatenate([dfw[0], dfw[1]], axis=1), ffn_conv_b=jnp.concatenate([dfb[0], dfb[1]], axis=1))
    cs_late = chip_sums([dwlb, split_small(grads)], ["row", "lead"], "late")
    dproj = [dconv, dlru, dgates]
    (dwin,), rx_all = matmul_cols_bwd(dproj, h1t, "proj_wgrad", True, ride=exchange_ride(cs_mid + cs_late))
    rx_mid, rx_late = rx_all[:2], rx_all[2:]
    cs_in = chip_sums([dwin], ["col"], "in")
    in_rows = lambda r0, nr, into=None: exchange_ride(cs_in, items=[(0, r0, nr)], into=into)
    (dh1,), rx_in = matmul_cols_bwd(dproj, win4, "proj_dgrad", False, ride=in_rows(0, 384))
    (dx, grads["norm_mix_pre"]), rx_in = norm_in_bwd(dh1, xs, dx2, g1, ride=in_rows(384, 128, rx_in))
    rx_in = rx_in[0]
    (rep_part,) = pack_repl([grads], loss)
    (rep_all,) = run_ride(exchange_ride([], rep=rep_part), "exchange_repl")

    order = (("w_in", cs_in[0], rx_in), ("ffn_w_up", cs_up, rx_up[0]), ("w_conv_branch", cs_mid[1], rx_mid[1]),
             ("w_lru_branch", cs_late[0], rx_late[0]), ("w_out", cs_mid[0], rx_mid[0]),
             ("ffn_w_down", cs_down, rx_down[0]), ("small", cs_late[1], rx_late[1]))
    halves = [sum_chips(rx, cs, chip, "chip_sum_" + n) for n, cs, rx in order]
    me = 4 * xi + 2 * yi + ci
    rep_grad = sum_lead(_own_slot(rep_all, rep_part, me), "device_sum")
    others = pair_swap(halves)

    g_out, d_out, m_out, v_out = {}, {}, {}, {}
    for n, gm, go in zip(BIG, halves[:-1], others[:-1]):
        g, d, nm, nv = adamw_halves(w[n][0], gm, go, m[n][0], v[n][0], core, "adamw_" + n)
        g_out[n], d_out[n], m_out[n], v_out[n] = g[None], d[None], nm[None], nv[None]
    bufs = adamw_halves(small_shard, halves[-1], others[-1], m_small, v_small, core, "adamw_small")
    for dst, part in zip((g_out, d_out, m_out, v_out), unpack_small(bufs)):
        dst.update(part)
    w_rep, m_rep, v_rep = pack_repl([w, m, v])
    d, nm, nv = adamw(w_rep, rep_grad, m_rep, v_rep, "adamw_repl")
    for dst, part in zip((g_out, d_out, m_out, v_out), unpack_repl([rep_grad, d, nm, nv])):
        dst.update(part)

    return (rep_grad[LOSS_ROW, 0], dx[None], *[g_out[n] for n in WEIGHTS], *[d_out[n] for n in WEIGHTS],
            *[m_out[n] for n in WEIGHTS], *[v_out[n] for n in WEIGHTS])
```

```python
import functools
import math

import jax
import jax.numpy as jnp
from jax import lax
from jax.experimental import pallas as pl
from jax.experimental.pallas import tpu as pltpu

F32 = jnp.float32
BF16 = jnp.bfloat16

D_MODEL = 1024
N_CHIPS = 4
N_SEG = 7
D_FF = 3 * D_MODEL
LRU_HEADS = 4
HEAD_DIM = D_MODEL // LRU_HEADS
LRU_C = 8.0
RMS_EPS = 1e-6
CW = 256
FW = 256
SUBLANES = 8
SCAN_UNROLL = 8
VMEM_LIMIT = 58 * 1024 * 1024

ADAM_LR = 0.001
ADAM_B1 = 0.9
ADAM_B2 = 0.999
ADAM_EPS = 1e-08
ADAM_WD = 0.01
ADAM_STEP = 10

_GELU_C = math.sqrt(2.0 / math.pi)
_GELU_K = 0.044715


def _params(**kw):
    return pltpu.CompilerParams(vmem_limit_bytes=VMEM_LIMIT, **kw)


def _sigmoid(x):
    return 1.0 / (1.0 + jnp.exp(-x))


def _gelu(x):
    t = jnp.tanh(_GELU_C * (x + _GELU_K * x * x * x))
    return 0.5 * x * (1.0 + t)


def _gelu_and_grad(x):
    x2 = x * x
    t = jnp.tanh(_GELU_C * (x + _GELU_K * x * x2))
    g = 0.5 * x * (1.0 + t)
    dg = 0.5 * (1.0 + t) + 0.5 * x * (1.0 - t * t) * _GELU_C * (1.0 + 3.0 * _GELU_K * x2)
    return g, dg


def _log_sigmoid(x):
    e = jnp.exp(-jnp.abs(x))
    u = 1.0 + e
    l1p = jnp.where(u == 1.0, e, jnp.log(u) * e / (u - 1.0))
    return jnp.minimum(x, 0.0) - l1p


def _neg_expm1(z):
    series = -z * (1.0 + z * (0.5 + z * (1.0 / 6.0 + z * (1.0 / 24.0 + z * (1.0 / 120.0 + z * (1.0 / 720.0))))))
    return jnp.where(z > -0.2, series, 1.0 - jnp.exp(z))


def _rows(shape):
    return lax.broadcasted_iota(jnp.int32, shape, 0)


def _shift_down(x, k):
    return jnp.where(_rows(x.shape) >= k, pltpu.roll(x, k, 0), 0.0)


def _shift_up(x, k):
    n = x.shape[0]
    return jnp.where(_rows(x.shape) < n - k, pltpu.roll(x, n - k, 0), 0.0)


def _delays(x, k_width):
    return [x] + [_shift_down(x, j) for j in range(1, k_width)]


def _advances(dy, k_width):
    return [dy] + [_shift_up(dy, j) for j in range(1, k_width)]


def _taps_sum(shifted, w_ref, b=None):
    k_width = w_ref.shape[0]
    y = w_ref[k_width - 1:k_width, :] * shifted[0]
    for j in range(1, k_width):
        y = y + w_ref[k_width - 1 - j:k_width - j, :] * shifted[j]
    if b is not None:
        y = y + b
    return y


def _causal_conv(x, w_ref, b=None):
    return _taps_sum(_delays(x, w_ref.shape[0]), w_ref, b)


def _conv_wgrad(advanced, x):
    k_width = len(advanced)
    rows = [jnp.sum(advanced[k_width - 1 - k] * x, axis=0, keepdims=True) for k in range(k_width)]
    return jnp.concatenate(rows, axis=0)


def _dot(a, b):
    return jnp.dot(a, b, preferred_element_type=F32)


def _dot_nt(a, b):
    return lax.dot_general(a, b, (((1,), (1,)), ((), ())), preferred_element_type=F32)


def _dot_tn(a, b):
    return lax.dot_general(a, b, (((0,), (0,)), ((), ())), preferred_element_type=F32)


def _rms_stats(x):
    r = lax.rsqrt(jnp.mean(x * x, axis=-1, keepdims=True) + RMS_EPS)
    return x * r, r


def _rms_bwd(n, r, g, dy):
    dn = dy * g
    dx = r * (dn - n * jnp.mean(dn * n, axis=-1, keepdims=True))
    return dx, dy * n


def _scan(a_ref, b_ref, h_ref, reverse):
    n, c = a_ref.shape
    row = lax.broadcasted_iota(jnp.int32, (SUBLANES, c), 0)
    span = SCAN_UNROLL * SUBLANES
    n_trips = n // span

    def within(a, b):
        for k in (1, 2, 4):
            if reverse:
                keep, shift = row < SUBLANES - k, SUBLANES - k
            else:
                keep, shift = row >= k, k
            ap = jnp.where(keep, pltpu.roll(a, shift, 0), 1.0)
            bp = jnp.where(keep, pltpu.roll(b, shift, 0), 0.0)
            b = a * bp + b
            a = a * ap
        return a, b

    def trip(t, carry):
        base = pl.multiple_of((n_trips - 1 - t if reverse else t) * span, span)
        order = list(reversed(range(SCAN_UNROLL))) if reverse else list(range(SCAN_UNROLL))
        loaded = [(a_ref[pl.ds(base + u * SUBLANES, SUBLANES), :], b_ref[pl.ds(base + u * SUBLANES, SUBLANES), :])
                  for u in order]
        out = []
        for a, b in [within(a, b) for a, b in loaded]:
            h = a * carry + b
            out.append(h)
            carry = h[0:1, :] if reverse else h[SUBLANES - 1:SUBLANES, :]
        for u, h in zip(order, out):
            h_ref[pl.ds(base + u * SUBLANES, SUBLANES), :] = h
        return carry

    lax.fori_loop(0, n_trips, trip, jnp.zeros((1, c), F32))


def _scan_forward(a_ref, b_ref, h_ref):
    _scan(a_ref, b_ref, h_ref, False)


def _scan_backward(c_ref, b_ref, g_ref):
    _scan(c_ref, b_ref, g_ref, True)


MESH = pl.DeviceIdType.MESH
_HBM = pl.BlockSpec(memory_space=pltpu.HBM)
_OTHER_CHIPS = ((1, 0), (0, 1), (1, 1))
_OTHER_DEVICES = tuple((dx, dy, dc) for dx in (0, 1) for dy in (0, 1) for dc in (0, 1) if dx or dy or dc)
N_DEVICES = 8


def _position():
    return lax.axis_index("x"), lax.axis_index("y"), lax.axis_index("c")


def _flip(v, d):
    return 1 - v if d else v


def _chip(x, y, p):
    px, py = _flip(x, _OTHER_CHIPS[p][0]), _flip(y, _OTHER_CHIPS[p][1])
    return px, py, 2 * px + py


class _Ride:
    def __init__(self, srcs, bufs, scratch, plan, collective_id):
        self.srcs, self.bufs, self.scratch, self.plan = list(srcs), list(bufs), list(scratch), plan
        self.collective_id = collective_id


NEIGHBOURS_AND_SIBLING = 1
OTHER_CHIPS_SAME_CORE = 2
ALL_DEVICES = 3
SIBLING = 4


def _handshake(peers):
    barrier = pltpu.get_barrier_semaphore()
    for peer in peers:
        pl.semaphore_signal(barrier, inc=1, device_id=peer, device_id_type=MESH)
    pl.semaphore_wait(barrier, len(peers))


def _call(body, *, name, grid, in_specs, out_specs, out_shape, operands, scratch_shapes=(), ride=None):
    in_specs, out_specs, out_shape = list(in_specs), list(out_specs), list(out_shape)
    scratch_shapes = list(scratch_shapes)
    if ride is None:
        return pl.pallas_call(body, name=name, grid=grid, in_specs=in_specs, out_specs=out_specs, out_shape=out_shape,
                              scratch_shapes=scratch_shapes, compiler_params=_params())(*operands)
    n_in, n_out, n_scr = len(in_specs), len(out_shape), len(scratch_shapes)
    old = [i for i, b in enumerate(ride.bufs) if not isinstance(b, jax.ShapeDtypeStruct)]
    n_src, n_old, n_buf = len(ride.srcs), len(old), len(ride.bufs)

    def full_body(*refs):
        o0 = n_in + n_src + n_old
        s0 = o0 + n_out + n_buf
        start, relay, relay_on, finish = ride.plan(refs[n_in:n_in + n_src], refs[o0 + n_out:s0], refs[s0 + n_scr:])
        ids = [pl.program_id(i) for i in range(len(grid))]
        first = functools.reduce(jnp.logical_and, [i == 0 for i in ids])
        middle = functools.reduce(jnp.logical_and, [ids[0] == grid[0] // 2] + [i == 0 for i in ids[1:]])
        last = functools.reduce(jnp.logical_and, [i == g - 1 for i, g in zip(ids, grid)])
        pl.when(first)(start)
        pl.when(middle)(relay)
        pl.when(last)(relay_on)
        body(*refs[:n_in], *refs[o0:o0 + n_out], *refs[s0:s0 + n_scr])
        pl.when(last)(finish)

    shapes = [jax.ShapeDtypeStruct(b.shape, b.dtype) for b in ride.bufs]
    res = pl.pallas_call(
        full_body, name=name, grid=grid,
        in_specs=in_specs + [_HBM] * (n_src + n_old), out_specs=out_specs + [_HBM] * n_buf,
        out_shape=out_shape + shapes, scratch_shapes=scratch_shapes + ride.scratch,
        input_output_aliases={n_in + n_src + k: n_out + i for k, i in enumerate(old)},
        compiler_params=_params(collective_id=ride.collective_id),
    )(*operands, *ride.srcs, *[ride.bufs[i] for i in old])
    return list(res[:n_out]), list(res[n_out:])


def run_ride(ride, name):
    def body():
        pass

    return _call(body, name=name, grid=(1,), in_specs=[], out_specs=[], out_shape=[], operands=[], ride=ride)[1]


def gather_ride(shards, items=None, into=None):
    items = items or [(a, 0, s.shape[0]) for a, s in enumerate(shards)]
    bufs = into or [jax.ShapeDtypeStruct((N_CHIPS,) + s.shape, s.dtype) for s in shards]
    nrel = len(_OTHER_CHIPS)

    def plan(srcs, dsts, sems):
        ici_send, ici_recv, hop_send, hop_recv, sib_send, sib_recv = sems
        x, y, c = _position()
        j = 2 * x + y

        def rows(ref, it, h, q=None):
            half = it[2] // 2
            if q is None:
                return ref.at[pl.ds(it[1] + h * half, half), :]
            return ref.at[pl.ds(it[1] + h * half + q * (half // 2), half // 2), :]

        def ici(i, p, slot):
            it = items[i]
            px, py, _ = _chip(x, y, p)
            return pltpu.make_async_remote_copy(
                src_ref=rows(srcs[it[0]], it, c), dst_ref=rows(dsts[it[0]].at[slot], it, c),
                send_sem=ici_send.at[i * nrel + p], recv_sem=ici_recv.at[i * nrel + p],
                device_id=(px, py, c), device_id_type=MESH)

        def hop(i, p, slot):
            it = items[i]
            part = rows(dsts[it[0]].at[slot], it, c, p)
            px, py, _ = _chip(x, y, 1 - p)
            return pltpu.make_async_remote_copy(
                src_ref=part, dst_ref=part, send_sem=hop_send.at[i * 2 + p], recv_sem=hop_recv.at[i * 2 + p],
                device_id=(px, py, c), device_id_type=MESH)

        def sib(i, p, h):
            it = items[i]
            part = rows(dsts[it[0]].at[_chip(x, y, p)[2]], it, h)
            return pltpu.make_async_remote_copy(
                src_ref=part, dst_ref=part, send_sem=sib_send.at[i * nrel + p], recv_sem=sib_recv.at[i * nrel + p],
                device_id=(x, y, 1 - c), device_id_type=MESH)

        every = range(len(items))
        diag = _chip(x, y, 2)[2]

        def start():
            _handshake([_chip(x, y, 0)[:2] + (c,), _chip(x, y, 1)[:2] + (c,), (x, y, 1 - c)])
            for i in every:
                for p in (0, 1):
                    ici(i, p, j).start()

        def relay():
            for i in every:
                for p in (0, 1):
                    k = _chip(x, y, p)[2]
                    ici(i, p, k).wait_recv()
                    hop(i, p, k).start()
                    sib(i, p, c).start()

        def relay_on():
            for i in every:
                for p in (0, 1):
                    hop(i, p, diag).wait_recv()
                sib(i, 2, c).start()

        def finish():
            for i in every:
                for p in range(nrel):
                    sib(i, p, 1 - c).wait_recv()
            for i in every:
                for p in (0, 1):
                    ici(i, p, j).wait_send()
                    hop(i, p, _chip(x, y, p)[2]).wait_send()
                for p in range(nrel):
                    sib(i, p, c).wait_send()

        return start, relay, relay_on, finish

    n = len(items)
    sems = [pltpu.SemaphoreType.DMA((n * nrel,))] * 2 + [pltpu.SemaphoreType.DMA((n * 2,))] * 2 \
        + [pltpu.SemaphoreType.DMA((n * nrel,))] * 2
    return _Ride(shards, bufs, sems, plan, NEIGHBOURS_AND_SIBLING)


def exchange_ride(sums, items=None, into=None, rep=None):
    items = [(a, 0, s.shape[1]) for a, s in enumerate(sums)] if items is None else items
    into = into or [None] * len(sums)
    bufs = [jax.ShapeDtypeStruct(s.shape, s.dtype) if b is None else b for s, b in zip(sums, into)]
    srcs = list(sums)
    scratch = [pltpu.SemaphoreType.DMA((max(len(items), 1) * len(_OTHER_CHIPS),))] * 2
    if rep is not None:
        srcs.append(rep)
        bufs.append(jax.ShapeDtypeStruct((N_DEVICES,) + rep.shape, rep.dtype))
        scratch += [pltpu.SemaphoreType.DMA((len(_OTHER_DEVICES),))] * 2
    nrel = len(_OTHER_CHIPS)

    def plan(src_refs, dst_refs, sems):
        x, y, c = _position()
        j = 2 * x + y
        me = 4 * x + 2 * y + c

        def part(i, p, src_slot, dst_slot):
            a, r0, nr = items[i]
            px, py, _ = _chip(x, y, p)
            return pltpu.make_async_remote_copy(
                src_ref=src_refs[a].at[src_slot, pl.ds(r0, nr), :], dst_ref=dst_refs[a].at[dst_slot, pl.ds(r0, nr), :],
                send_sem=sems[0].at[i * nrel + p], recv_sem=sems[1].at[i * nrel + p],
                device_id=(px, py, c), device_id_type=MESH)

        def device(q):
            dx, dy, dc = _OTHER_DEVICES[q]
            return _flip(x, dx), _flip(y, dy), _flip(c, dc)

        def rep_copy(q, slot):
            return pltpu.make_async_remote_copy(
                src_ref=src_refs[-1], dst_ref=dst_refs[-1].at[slot], send_sem=sems[2].at[q], recv_sem=sems[3].at[q],
                device_id=device(q), device_id_type=MESH)

        pairs = [(i, p) for i in range(len(items)) for p in range(nrel)]
        others = range(len(_OTHER_DEVICES)) if rep is not None else ()

        def start():
            if rep is None:
                _handshake([_chip(x, y, p)[:2] + (c,) for p in range(nrel)])
            else:
                _handshake([device(q) for q in others])
            for i, p in pairs:
                part(i, p, _chip(x, y, p)[2], j).start()
            for q in others:
                rep_copy(q, me).start()

        def finish():
            for i, p in pairs:
                k = _chip(x, y, p)[2]
                part(i, p, k, k).wait_recv()
            for q in others:
                px, py, pc = device(q)
                rep_copy(q, 4 * px + 2 * py + pc).wait_recv()
            for i, p in pairs:
                part(i, p, _chip(x, y, p)[2], j).wait_send()
            for q in others:
                rep_copy(q, me).wait_send()

        return start, lambda: None, lambda: None, finish

    return _Ride(srcs, bufs, scratch, plan, OTHER_CHIPS_SAME_CORE if rep is None else ALL_DEVICES)


def _own_slot(buf, own, index):
    return lax.dynamic_update_slice(buf, own[None], (index,) + (0,) * own.ndim)


def _token_tile(s):
    return min(s, 512)


def norm_in(x, g):
    s, d = x.shape
    t = _token_tile(s)

    def body(x_ref, g_ref, o_ref, ot_ref):
        n, _ = _rms_stats(x_ref[...])
        h = n * g_ref[...]
        o_ref[...] = h.astype(BF16)
        ot_ref[...] = h.T.astype(BF16)

    return pl.pallas_call(
        body, name="norm_in", grid=(s // t,),
        in_specs=[pl.BlockSpec((t, d), lambda i: (i, 0)), pl.BlockSpec((1, d), lambda i: (0, 0))],
        out_specs=[pl.BlockSpec((t, d), lambda i: (i, 0)), pl.BlockSpec((d, t), lambda i: (0, i))],
        out_shape=[jax.ShapeDtypeStruct((s, d), BF16), jax.ShapeDtypeStruct((d, s), BF16)],
        compiler_params=_params(),
    )(x, g)


def matmul_cols(a, w4, name, ride=None):
    m, k = a.shape
    nj, _, ns = w4.shape
    nb = ns // CW

    def body(a_ref, w_ref, o_ref):
        o_ref[...] = _dot(a_ref[...], w_ref[0])

    return _call(
        body, name=name, grid=(nj, nb),
        in_specs=[pl.BlockSpec((m, k), lambda j, b: (0, 0)),
                  pl.BlockSpec((1, k, CW), lambda j, b: (j, 0, b))],
        out_specs=[pl.BlockSpec((m, CW), lambda j, b: (0, j * nb + b))],
        out_shape=[jax.ShapeDtypeStruct((m, nj * ns), F32)],
        operands=(a, w4), ride=ride)


def mix_conv_fwd(proj, ws, ride=None):
    s = proj.shape[0]
    nblk = D_MODEL // CW

    def body(cb_ref, cc_ref, cx_ref, ws_ref, q_ref, ya_ref):
        q = _causal_conv(cc_ref[...] * cx_ref[...], ws_ref)
        q_ref[...] = q
        ya_ref[...] = (cb_ref[...] * q).astype(BF16)

    seg = lambda k: pl.BlockSpec((s, CW), lambda c, k=k: (0, k * nblk + c))
    return _call(
        body, name="mix_conv_fwd", grid=(nblk,),
        in_specs=[seg(0), seg(1), seg(2), pl.BlockSpec((3, CW), lambda c: (0, c))],
        out_specs=[pl.BlockSpec((s, CW), lambda c: (0, c))] * 2,
        out_shape=[jax.ShapeDtypeStruct((s, D_MODEL), F32), jax.ShapeDtypeStruct((s, D_MODEL), BF16)],
        operands=(proj, proj, proj, ws), ride=ride)


def _lru_gates(r, ls):
    log_a = LRU_C * r * ls
    a = jnp.exp(log_a)
    mult = jnp.sqrt(_neg_expm1(2.0 * log_a))
    mult = jnp.where(_rows(r.shape) == 0, 1.0, mult)
    return a, mult


def mix_lru_fwd(proj, wl, bl, wa, ba, wx, bx, lam, ride=None):
    s = proj.shape[0]
    nblk = D_MODEL // CW

    def body(lx_ref, ly_ref, wl_ref, bl_ref, wa_ref, ba_ref, wx_ref, bx_ref, lam_ref,
             xl_ref, r_ref, i_ref, h_ref, yb_ref, a_scr, u_scr):
        xl = _causal_conv(lx_ref[...], wl_ref, bl_ref[...])
        xlb = xl.astype(BF16)
        xl_ref[...] = xlb
        r = _sigmoid(_dot(xlb, wa_ref[0]) + ba_ref[...])
        i = _sigmoid(_dot(xlb, wx_ref[0]) + bx_ref[...])
        r_ref[...] = r.astype(BF16)
        i_ref[...] = i.astype(BF16)
        a, mult = _lru_gates(r, _log_sigmoid(lam_ref[...]))
        a_scr[...] = a
        u_scr[...] = mult * i * xl
        _scan_forward(a_scr, u_scr, h_ref)
        yb_ref[...] = (h_ref[...] * _gelu(ly_ref[...])).astype(BF16)

    blk = lambda k: pl.BlockSpec((s, CW), lambda c, k=k: (0, k * nblk + c))
    vec = pl.BlockSpec((1, CW), lambda c: (0, c))
    mat = pl.BlockSpec((1, CW, CW), lambda c: (c, 0, 0))
    out = pl.BlockSpec((s, CW), lambda c: (0, c))
    f = jax.ShapeDtypeStruct((s, D_MODEL), F32)
    hb = jax.ShapeDtypeStruct((s, D_MODEL), BF16)
    return _call(
        body, name="mix_lru_fwd", grid=(nblk,),
        in_specs=[blk(3), blk(4), pl.BlockSpec((4, CW), lambda c: (0, c)), vec, mat, vec, mat, vec, vec],
        out_specs=[out] * 5,
        out_shape=[hb, hb, hb, f, hb],
        scratch_shapes=[pltpu.VMEM((s, CW), F32), pltpu.VMEM((s, CW), F32)],
        operands=(proj, proj, wl, bl, wa, ba, wx, bx, lam), ride=ride)


def branch_merge_fwd(ya, yb, wcb, wlb, proj, ride=None):
    s = ya.shape[0]
    nblk = D_MODEL // CW

    def body(ya_ref, yb_ref, wcb_ref, wlb_ref, gc_ref, gl_ref, a_ref, b_ref, m_ref):
        a = _dot(ya_ref[...], wcb_ref[...])
        b = _dot(yb_ref[...], wlb_ref[...])
        a_ref[...] = a
        b_ref[...] = b
        m_ref[...] = (_sigmoid(gc_ref[...]) * a + _sigmoid(gl_ref[...]) * b).astype(BF16)

    res = pl.BlockSpec((s, D_MODEL), lambda n: (0, 0))
    wcol = pl.BlockSpec((D_MODEL, CW), lambda n: (0, n))
    blk = lambda k: pl.BlockSpec((s, CW), lambda n, k=k: (0, k * nblk + n))
    out = pl.BlockSpec((s, CW), lambda n: (0, n))
    f = jax.ShapeDtypeStruct((s, D_MODEL), F32)
    return _call(
        body, name="branch_merge_fwd", grid=(nblk,),
        in_specs=[res, res, wcol, wcol, blk(5), blk(6)],
        out_specs=[out] * 3,
        out_shape=[f, f, jax.ShapeDtypeStruct((s, D_MODEL), BF16)],
        operands=(ya, yb, wcb, wlb, proj, proj), ride=ride)


def mix_out_fwd(merged, wout, x, g2, g3, ride=None):
    s, d = x.shape
    t = _token_tile(s)

    def body(m_ref, w_ref, x_ref, g2_ref, g3_ref, mix_ref, x2_ref, h2_ref, h2t_ref):
        mix = _dot(m_ref[...], w_ref[...])
        mix_ref[...] = mix
        n, _ = _rms_stats(mix)
        x2 = x_ref[...] + n * g2_ref[...]
        x2_ref[...] = x2
        n2, _ = _rms_stats(x2)
        h2 = n2 * g3_ref[...]
        h2_ref[...] = h2.astype(BF16)
        h2t_ref[...] = h2.T.astype(BF16)

    tile = pl.BlockSpec((t, d), lambda i: (i, 0))
    vec = pl.BlockSpec((1, d), lambda i: (0, 0))
    f = jax.ShapeDtypeStruct((s, d), F32)
    return _call(
        body, name="mix_out_fwd", grid=(s // t,),
        in_specs=[tile, pl.BlockSpec((d, d), lambda i: (0, 0)), tile, vec, vec],
        out_specs=[tile] * 3 + [pl.BlockSpec((d, t), lambda i: (0, i))],
        out_shape=[f, f, jax.ShapeDtypeStruct((s, d), BF16), jax.ShapeDtypeStruct((d, s), BF16)],
        operands=(merged, wout, x, g2, g3), ride=ride)


def ffn_up_act_fwd(h2, wup4, fw, fb, ride=None):
    s, k = h2.shape
    ns = wup4.shape[2]
    per_chip = ns // CW
    nblk = D_FF // CW

    def body(h_ref, wg_ref, wv_ref, cg_ref, cv_ref, bg_ref, bv_ref, up_ref, act_ref, f_ref):
        h = h_ref[...]
        ug = _dot(h, wg_ref[0])
        uv = _dot(h, wv_ref[0])
        up_ref[0] = ug
        up_ref[1] = uv
        gate = _causal_conv(ug, cg_ref, bg_ref[...])
        val = _causal_conv(uv, cv_ref, bv_ref[...])
        act_ref[0] = gate.astype(BF16)
        act_ref[1] = val.astype(BF16)
        f_ref[...] = (_gelu(gate) * val).astype(BF16)

    wcols = lambda h: pl.BlockSpec((1, k, CW), lambda n, h=h: (n // per_chip + 2 * h, 0, n % per_chip))
    half = lambda h, rows: pl.BlockSpec((rows, CW), lambda n, h=h: (0, h * nblk + n))
    both = pl.BlockSpec((2, s, CW), lambda n: (0, 0, n))
    return _call(
        body, name="ffn_up_act_fwd", grid=(nblk,),
        in_specs=[pl.BlockSpec((s, k), lambda n: (0, 0)), wcols(0), wcols(1),
                  half(0, 3), half(1, 3), half(0, 1), half(1, 1)],
        out_specs=[both, both, pl.BlockSpec((s, CW), lambda n: (0, n))],
        out_shape=[jax.ShapeDtypeStruct((2, s, D_FF), F32), jax.ShapeDtypeStruct((2, s, D_FF), BF16),
                   jax.ShapeDtypeStruct((s, D_FF), BF16)],
        operands=(h2, wup4, wup4, fw, fw, fb, fb), ride=ride)


def ffn_down_loss(f, wdown, x2, target, g4):
    s, d = x2.shape
    t = _token_tile(s)

    def body(f_ref, w_ref, x2_ref, tg_ref, g4_ref, dy_ref, dout_ref, loss_ref, dg4_ref):
        @pl.when(pl.program_id(0) == 0)
        def _():
            loss_ref[...] = jnp.zeros_like(loss_ref)
            dg4_ref[...] = jnp.zeros_like(dg4_ref)

        out = _dot(f_ref[...], w_ref[...])
        n, r = _rms_stats(out)
        err = x2_ref[...] + n * g4_ref[...] - tg_ref[...]
        loss_ref[...] += jnp.full(loss_ref.shape, (0.5 / d) * jnp.sum(err * err), F32)
        dy = err * (1.0 / d)
        dy_ref[...] = dy
        dout, dg = _rms_bwd(n, r, g4_ref[...], dy)
        dout_ref[...] = dout.astype(BF16)
        dg4_ref[...] += jnp.sum(dg, axis=0, keepdims=True)

    tile = pl.BlockSpec((t, d), lambda i: (i, 0))
    vec = pl.BlockSpec((1, d), lambda i: (0, 0))
    return pl.pallas_call(
        body, name="ffn_down_loss", grid=(s // t,),
        in_specs=[pl.BlockSpec((t, D_FF), lambda i: (i, 0)), pl.BlockSpec((D_FF, d), lambda i: (0, 0)), tile, tile, vec],
        out_specs=[tile, tile, pl.BlockSpec((1, 128), lambda i: (0, 0)), vec],
        out_shape=[jax.ShapeDtypeStruct((s, d), F32), jax.ShapeDtypeStruct((s, d), BF16),
                   jax.ShapeDtypeStruct((1, 128), F32), jax.ShapeDtypeStruct((1, d), F32)],
        compiler_params=_params(),
    )(f, wdown, x2, target, g4)


def ffn_up_bwd(dout, wdown, up, act, f, fw, wup4, h2t, ride=None):
    k, s = h2t.shape
    nblk = D_FF // FW
    per_chip = wup4.shape[2] // FW

    def body(do_ref, wd_ref, up_ref, act_ref, f_ref, cg_ref, cv_ref, wg_ref, wv_ref, h_ref,
             dh_ref, dwu_ref, dwd_ref, dw_ref, db_ref, dup_scr):
        step = pl.program_id(0)

        @pl.when(step == 0)
        def _():
            dh_ref[...] = jnp.zeros_like(dh_ref)

        @pl.when(step > 0)
        def _():
            dg = dup_scr[0]
            dv = dup_scr[1]
            ht = h_ref[...]
            dh_ref[...] += _dot_nt(dg, wg_ref[0]) + _dot_nt(dv, wv_ref[0])
            dwu_ref[0] = _dot(ht, dg).astype(BF16)
            dwu_ref[1] = _dot(ht, dv).astype(BF16)

        @pl.when(step < nblk)
        def _():
            do = do_ref[...]
            df = _dot_nt(do, wd_ref[...])
            dwd_ref[...] = _dot_tn(f_ref[...], do).astype(BF16)
            val = act_ref[1].astype(F32)
            ge, dge = _gelu_and_grad(act_ref[0].astype(F32))
            dgate = _advances(df * val * dge, 3)
            dval = _advances(df * ge, 3)
            dw_ref[0] = _conv_wgrad(dgate, up_ref[0])
            dw_ref[1] = _conv_wgrad(dval, up_ref[1])
            db_ref[0] = jnp.sum(dgate[0], axis=0, keepdims=True)
            db_ref[1] = jnp.sum(dval[0], axis=0, keepdims=True)
            dup_scr[0] = _taps_sum(dgate, cg_ref).astype(BF16)
            dup_scr[1] = _taps_sum(dval, cv_ref).astype(BF16)

    cur = lambda n: jnp.minimum(n, nblk - 1)
    prev = lambda n: jnp.maximum(n - 1, 0)
    once = pl.Buffered(1)
    both = lambda rows: pl.BlockSpec((2, rows, FW), lambda n: (0, 0, cur(n)))
    taps = lambda h: pl.BlockSpec((3, FW), lambda n, h=h: (0, h * nblk + cur(n)))
    wcols = lambda h: pl.BlockSpec((1, k, FW), lambda n, h=h: (prev(n) // per_chip + 2 * h, 0, prev(n) % per_chip))
    return _call(
        body, name="ffn_up_bwd", grid=(nblk + 1,),
        in_specs=[pl.BlockSpec((s, D_MODEL), lambda n: (0, 0), pipeline_mode=once),
                  pl.BlockSpec((FW, D_MODEL), lambda n: (cur(n), 0)), both(s), both(s),
                  pl.BlockSpec((s, FW), lambda n: (0, cur(n))), taps(0), taps(1), wcols(0), wcols(1),
                  pl.BlockSpec((k, s), lambda n: (0, 0), pipeline_mode=once)],
        out_specs=[pl.BlockSpec((s, k), lambda n: (0, 0), pipeline_mode=once),
                   pl.BlockSpec((2, k, FW), lambda n: (0, 0, prev(n))),
                   pl.BlockSpec((FW, D_MODEL), lambda n: (cur(n), 0)), both(3), both(1)],
        out_shape=[jax.ShapeDtypeStruct((s, k), F32), jax.ShapeDtypeStruct((2, k, D_FF), BF16),
                   jax.ShapeDtypeStruct((D_FF, D_MODEL), BF16),
                   jax.ShapeDtypeStruct((2, 3, D_FF), F32), jax.ShapeDtypeStruct((2, 1, D_FF), F32)],
        scratch_shapes=[pltpu.VMEM((2, s, FW), BF16)],
        operands=(dout, wdown, up, act, f, fw, fw, wup4, wup4, h2t), ride=ride)


def matmul_cols_bwd(dy, other, name, wgrad, ride=None):
    m = dy[0].shape[1]
    if wgrad:
        k = other.shape[0]
        nj, nb = N_CHIPS, sum(d.shape[0] * d.shape[2] for d in dy) // (N_CHIPS * CW)
    else:
        nj, k, ns = other.shape
        nb = ns // CW
    per_seg = dy[0].shape[2] // CW
    first = [sum(d.shape[0] for d in dy[:i]) for i in range(len(dy))]

    def segment(j, b):
        return (j * nb + b) // per_seg, (j * nb + b) % per_seg

    def body(*refs):
        dy_refs, (o_ref, r_ref) = refs[:len(dy)], refs[len(dy):]
        seg, _ = segment(pl.program_id(0), pl.program_id(1))
        dyb = dy_refs[-1][0]
        for i in range(len(dy) - 2, -1, -1):
            dyb = jnp.where(seg < first[i + 1], dy_refs[i][0], dyb)
        if wgrad:
            r_ref[...] = _dot(o_ref[...], dyb).astype(BF16)
        else:
            @pl.when((pl.program_id(0) == 0) & (pl.program_id(1) == 0))
            def _():
                r_ref[...] = jnp.zeros_like(r_ref)

            r_ref[...] += _dot_nt(dyb, o_ref[0])

    def dy_spec(i):
        nseg = dy[i].shape[0]

        def index(j, b):
            seg, col = segment(j, b)
            local = seg - first[i]
            return (jnp.clip(local, 0, nseg - 1), 0,
                    jnp.where(local < 0, 0, jnp.where(local >= nseg, per_seg - 1, col)))

        return pl.BlockSpec((1, m, CW), index)

    if wgrad:
        other_spec = pl.BlockSpec((k, m), lambda j, b: (0, 0))
        out_spec = pl.BlockSpec((k, CW), lambda j, b: (0, j * nb + b))
        out_shape = jax.ShapeDtypeStruct((k, nj * nb * CW), BF16)
    else:
        other_spec = pl.BlockSpec((1, k, CW), lambda j, b: (j, 0, b))
        out_spec = pl.BlockSpec((m, k), lambda j, b: (0, 0))
        out_shape = jax.ShapeDtypeStruct((m, k), F32)
    return _call(
        body, name=name, grid=(nj, nb), in_specs=[dy_spec(i) for i in range(len(dy))] + [other_spec],
        out_specs=[out_spec], out_shape=[out_shape], operands=(*dy, other), ride=ride)


def norms_mid_bwd(dh2, x2, dy, mix, g3, g2, ride=None):
    s, d = x2.shape
    t = _token_tile(s)

    def body(dh2_ref, x2_ref, dy_ref, mix_ref, g3_ref, g2_ref, dx2_ref, dmix_ref, dg3_ref, dg2_ref):
        @pl.when(pl.program_id(0) == 0)
        def _():
            dg3_ref[...] = jnp.zeros_like(dg3_ref)
            dg2_ref[...] = jnp.zeros_like(dg2_ref)

        n3, r3 = _rms_stats(x2_ref[...])
        dx, dg3 = _rms_bwd(n3, r3, g3_ref[...], dh2_ref[...])
        dx2 = dy_ref[...] + dx
        dx2_ref[...] = dx2
        dg3_ref[...] += jnp.sum(dg3, axis=0, keepdims=True)
        n2, r2 = _rms_stats(mix_ref[...])
        dmix, dg2 = _rms_bwd(n2, r2, g2_ref[...], dx2)
        dmix_ref[...] = dmix.astype(BF16)
        dg2_ref[...] += jnp.sum(dg2, axis=0, keepdims=True)

    tile = pl.BlockSpec((t, d), lambda i: (i, 0))
    vec = pl.BlockSpec((1, d), lambda i: (0, 0))
    v = jax.ShapeDtypeStruct((1, d), F32)
    return _call(
        body, name="norms_mid_bwd", grid=(s // t,),
        in_specs=[tile, tile, tile, tile, vec, vec],
        out_specs=[tile, tile, vec, vec],
        out_shape=[jax.ShapeDtypeStruct((s, d), F32), jax.ShapeDtypeStruct((s, d), BF16), v, v],
        operands=(dh2, x2, dy, mix, g3, g2), ride=ride)


def mix_out_bwd(dmix, wout, merged, a, b, proj, ride=None):
    s = dmix.shape[0]
    nblk = D_MODEL // CW

    def body(dm_ref, w_ref, mg_ref, a_ref, b_ref, gc_ref, gl_ref, da_ref, db_ref, dw_ref, dg_ref):
        dm = dm_ref[...]
        dmerged = _dot_nt(dm, w_ref[...])
        dw_ref[...] = _dot_tn(mg_ref[...], dm).astype(BF16)
        sc = _sigmoid(gc_ref[...])
        sl = _sigmoid(gl_ref[...])
        da_ref[...] = (dmerged * sc).astype(BF16)
        db_ref[...] = (dmerged * sl).astype(BF16)
        dg_ref[0] = (dmerged * a_ref[...] * sc * (1.0 - sc)).astype(BF16)
        dg_ref[1] = (dmerged * b_ref[...] * sl * (1.0 - sl)).astype(BF16)

    res = pl.BlockSpec((s, D_MODEL), lambda n: (0, 0))
    rows = pl.BlockSpec((CW, D_MODEL), lambda n: (n, 0))
    col = pl.BlockSpec((s, CW), lambda n: (0, n))
    blk = lambda k: pl.BlockSpec((s, CW), lambda n, k=k: (0, k * nblk + n))
    hb = jax.ShapeDtypeStruct((s, D_MODEL), BF16)
    return _call(
        body, name="mix_out_bwd", grid=(nblk,),
        in_specs=[res, rows, col, col, col, blk(5), blk(6)],
        out_specs=[col, col, rows, pl.BlockSpec((2, s, CW), lambda n: (0, 0, n))],
        out_shape=[hb, hb, jax.ShapeDtypeStruct((D_MODEL, D_MODEL), BF16), jax.ShapeDtypeStruct((2, s, D_MODEL), BF16)],
        operands=(dmix, wout, merged, a, b, proj, proj), ride=ride)


def mix_conv_bwd(da, wcb, proj, q, ws, ride=None):
    s = da.shape[0]
    nblk = D_MODEL // CW

    def body(da_ref, w_ref, cb_ref, cc_ref, cx_ref, q_ref, ws_ref, dc_ref, dw_ref, dws_ref):
        dab = da_ref[...]
        dya = _dot_nt(dab, w_ref[...])
        cb = cb_ref[...]
        cc = cc_ref[...]
        cx = cx_ref[...]
        q = q_ref[...]
        dw_ref[...] = _dot_tn((cb * q).astype(BF16), dab).astype(BF16)
        dc_ref[0] = (dya * q).astype(BF16)
        dq = _advances(dya * cb, 3)
        dp = _taps_sum(dq, ws_ref)
        dws_ref[...] = _conv_wgrad(dq, cc * cx)
        dc_ref[1] = (dp * cx).astype(BF16)
        dc_ref[2] = (dp * cc).astype(BF16)

    res = pl.BlockSpec((s, D_MODEL), lambda n: (0, 0))
    rows = pl.BlockSpec((CW, D_MODEL), lambda n: (n, 0))
    col = pl.BlockSpec((s, CW), lambda n: (0, n))
    blk = lambda k: pl.BlockSpec((s, CW), lambda n, k=k: (0, k * nblk + n))
    taps = pl.BlockSpec((3, CW), lambda n: (0, n))
    hb = jax.ShapeDtypeStruct((s, D_MODEL), BF16)
    return _call(
        body, name="mix_conv_bwd", grid=(nblk,),
        in_specs=[res, rows, blk(0), blk(1), blk(2), col, taps],
        out_specs=[pl.BlockSpec((3, s, CW), lambda n: (0, 0, n)), rows, taps],
        out_shape=[jax.ShapeDtypeStruct((3, s, D_MODEL), BF16), jax.ShapeDtypeStruct((D_MODEL, D_MODEL), BF16),
                   jax.ShapeDtypeStruct((3, D_MODEL), F32)],
        operands=(da, wcb, proj, proj, proj, q, ws), ride=ride)


def mix_lru_bwd(db, wlb, proj, xl, r, i, h, wl, wa, wx, lam, ride=None):
    s = db.shape[0]
    nblk = D_MODEL // CW

    def body(db_ref, w_ref, lx_ref, ly_ref, xl_ref, r_ref, i_ref, h_ref, wl_ref, wa_ref, wx_ref, lam_ref,
             dl_ref, dw_ref, dwa_ref, dwx_ref, dba_ref, dbx_ref, dwl_ref, dbl_ref, dlam_ref,
             c_scr, g_scr):
        dbb = db_ref[...]
        dyb = _dot_nt(dbb, w_ref[...])
        h = h_ref[...]
        ge, dge = _gelu_and_grad(ly_ref[...])
        dw_ref[...] = _dot_tn((h * ge).astype(BF16), dbb).astype(BF16)
        dl_ref[1] = (dyb * h * dge).astype(BF16)
        r = r_ref[...].astype(F32)
        gi = i_ref[...].astype(F32)
        xlb = xl_ref[...]
        xl = xlb.astype(F32)
        lam = lam_ref[...]
        ls = _log_sigmoid(lam)
        a, mult = _lru_gates(r, ls)
        c_scr[...] = _shift_up(a, 1)
        g_scr[...] = dyb * ge
        _scan_backward(c_scr, g_scr, g_scr)
        du = g_scr[...]
        da = du * _shift_down(h, 1)
        dmult = du * gi * xl
        di = du * mult * xl
        dxl = du * mult * gi
        first = _rows(a.shape) == 0
        dlog_a = da * a - jnp.where(first, 0.0, dmult * a * a / mult)
        dr = dlog_a * (LRU_C * ls)
        dlam_ref[...] = jnp.sum(dlog_a * r, axis=0, keepdims=True) * (LRU_C * (1.0 - _sigmoid(lam)))
        dzr = dr * r * (1.0 - r)
        dzi = di * gi * (1.0 - gi)
        dba_ref[...] = jnp.sum(dzr, axis=0, keepdims=True)
        dbx_ref[...] = jnp.sum(dzi, axis=0, keepdims=True)
        dzrb = dzr.astype(BF16)
        dzib = dzi.astype(BF16)
        dwa_ref[0] = _dot_tn(xlb, dzrb)
        dwx_ref[0] = _dot_tn(xlb, dzib)
        dxl = _advances(dxl + _dot_nt(dzrb, wa_ref[0]) + _dot_nt(dzib, wx_ref[0]), 4)
        dl_ref[0] = _taps_sum(dxl, wl_ref).astype(BF16)
        dwl_ref[...] = _conv_wgrad(dxl, lx_ref[...])
        dbl_ref[...] = jnp.sum(dxl[0], axis=0, keepdims=True)

    res = pl.BlockSpec((s, D_MODEL), lambda n: (0, 0))
    rows = pl.BlockSpec((CW, D_MODEL), lambda n: (n, 0))
    col = pl.BlockSpec((s, CW), lambda n: (0, n))
    blk = lambda k: pl.BlockSpec((s, CW), lambda n, k=k: (0, k * nblk + n))
    taps = pl.BlockSpec((4, CW), lambda n: (0, n))
    vec = pl.BlockSpec((1, CW), lambda n: (0, n))
    mat = pl.BlockSpec((1, CW, CW), lambda n: (n, 0, 0))
    hb = jax.ShapeDtypeStruct((s, D_MODEL), BF16)
    v = jax.ShapeDtypeStruct((1, D_MODEL), F32)
    m = jax.ShapeDtypeStruct((LRU_HEADS, HEAD_DIM, HEAD_DIM), F32)
    scr = pltpu.VMEM((s, CW), F32)
    return _call(
        body, name="mix_lru_bwd", grid=(nblk,),
        in_specs=[res, rows, blk(3), blk(4), col, col, col, col, taps, mat, mat, vec],
        out_specs=[pl.BlockSpec((2, s, CW), lambda n: (0, 0, n)), rows, mat, mat, vec, vec, taps, vec, vec],
        out_shape=[jax.ShapeDtypeStruct((2, s, D_MODEL), BF16), jax.ShapeDtypeStruct((D_MODEL, D_MODEL), BF16), m, m, v, v,
                   jax.ShapeDtypeStruct((4, D_MODEL), F32), v, v],
        scratch_shapes=[scr, scr],
        operands=(db, wlb, proj, proj, xl, r, i, h, wl, wa, wx, lam), ride=ride)


def norm_in_bwd(dh1, x, dx2, g1, ride=None):
    s, d = x.shape
    t = _token_tile(s)

    def body(dh_ref, x_ref, dx2_ref, g_ref, dx_ref, dg_ref):
        @pl.when(pl.program_id(0) == 0)
        def _():
            dg_ref[...] = jnp.zeros_like(dg_ref)

        n, r = _rms_stats(x_ref[...])
        dx, dg = _rms_bwd(n, r, g_ref[...], dh_ref[...])
        dx_ref[...] = dx2_ref[...] + dx
        dg_ref[...] += jnp.sum(dg, axis=0, keepdims=True)

    tile = pl.BlockSpec((t, d), lambda i: (i, 0))
    vec = pl.BlockSpec((1, d), lambda i: (0, 0))
    return _call(
        body, name="norm_in_bwd", grid=(s // t,),
        in_specs=[tile, tile, tile, vec],
        out_specs=[tile, vec],
        out_shape=[jax.ShapeDtypeStruct((s, d), F32), jax.ShapeDtypeStruct((1, d), F32)],
        operands=(dh1, x, dx2, g1), ride=ride)


def _owned_part(ref, kind, k, h, hr):
    if kind == "col":
        ns = ref.shape[1] // N_CHIPS
        return ref.at[pl.ds(h * hr, hr), pl.ds(k * ns, ns)]
    if kind == "row":
        return ref.at[pl.ds(k * 2 * hr + h * hr, hr), :]
    if kind == "col2":
        ns = ref.shape[2] // 2
        return ref.at[k // 2, pl.ds(h * hr, hr), pl.ds((k % 2) * ns, ns)]
    return ref.at[k, pl.ds(h * hr, hr), :]


def _part_shape(g, kind):
    if kind == "col2":
        return g.shape[1] // 2, g.shape[2] // 2
    if kind == "col":
        return g.shape[0] // 2, g.shape[1] // N_CHIPS
    if kind == "row":
        return g.shape[0] // (2 * N_CHIPS), g.shape[1]
    return g.shape[1] // 2, g.shape[2]


def pair_split(grads, kinds, name):
    n = len(grads)
    shapes = [_part_shape(g, k) for g, k in zip(grads, kinds)]

    def body(*refs):
        ins, theirs = refs[:n], refs[n:2 * n]
        send_sem, recv_sem = refs[2 * n:]
        x, y, c = _position()
        copies = []
        for a in range(n):
            hr = shapes[a][0]
            for k in range(N_CHIPS):
                s = a * N_CHIPS + k
                copies.append(pltpu.make_async_remote_copy(
                    src_ref=_owned_part(ins[a], kinds[a], k, 1 - c, hr), dst_ref=theirs[a].at[k],
                    send_sem=send_sem.at[s], recv_sem=recv_sem.at[s], device_id=(x, y, 1 - c), device_id_type=MESH))
        _handshake([(x, y, 1 - c)])
        for cp in copies:
            cp.start()
        for cp in copies:
            cp.wait()

    return pl.pallas_call(
        body, name=name,
        in_specs=[_HBM] * n, out_specs=[_HBM] * n,
        out_shape=[jax.ShapeDtypeStruct((N_CHIPS,) + shp, g.dtype) for shp, g in zip(shapes, grads)],
        scratch_shapes=[pltpu.SemaphoreType.DMA((n * N_CHIPS,))] * 2,
        compiler_params=pltpu.CompilerParams(collective_id=SIBLING),
    )(*grads)


def pair_swap(halves):
    n = len(halves)

    def body(*refs):
        ins, outs = refs[:n], refs[n:2 * n]
        send_sem, recv_sem = refs[2 * n:]
        x, y, c = _position()
        copies = [pltpu.make_async_remote_copy(
            src_ref=ins[a], dst_ref=outs[a], send_sem=send_sem.at[a], recv_sem=recv_sem.at[a],
            device_id=(x, y, 1 - c), device_id_type=MESH) for a in range(n)]
        _handshake([(x, y, 1 - c)])
        for cp in copies:
            cp.start()
        for cp in copies:
            cp.wait()

    return pl.pallas_call(
        body, name="pair_swap",
        in_specs=[_HBM] * n, out_specs=[_HBM] * n,
        out_shape=[jax.ShapeDtypeStruct(h.shape, h.dtype) for h in halves],
        scratch_shapes=[pltpu.SemaphoreType.DMA((n,))] * 2,
        compiler_params=pltpu.CompilerParams(collective_id=SIBLING),
    )(*halves)


def _row_tile(rows, cols, limit_bytes=1 << 20):
    best = None
    for t in range(SUBLANES, rows + 1, SUBLANES):
        if rows % t == 0 and t * cols * 4 <= limit_bytes:
            best = t
    return best or rows


def add_pair(g, kind, theirs, core, name):
    nc, rows, cols = theirs.shape
    t = _row_tile(rows, cols, 4 << 20)
    nt = rows // t

    def body(core_ref, g_ref, b_ref, o_ref):
        mine = g_ref[...].reshape(t, cols)
        o_ref[0] = (mine.astype(F32) + b_ref[0].astype(F32)).astype(o_ref.dtype)

    if kind == "col":
        own = pl.BlockSpec((t, cols), lambda k, i, c: (c[0] * nt + i, k))
    elif kind == "col2":
        own = pl.BlockSpec((1, t, cols), lambda k, i, c: (k // 2, c[0] * nt + i, k % 2))
    elif kind == "row":
        own = pl.BlockSpec((t, cols), lambda k, i, c: ((2 * k + c[0]) * nt + i, 0))
    else:
        own = pl.BlockSpec((1, t, cols), lambda k, i, c: (k, c[0] * nt + i, 0))
    spec = pl.BlockSpec((1, t, cols), lambda k, i, c: (k, i, 0))
    return pl.pallas_call(
        body, name=name,
        grid_spec=pltpu.PrefetchScalarGridSpec(num_scalar_prefetch=1, grid=(nc, nt), in_specs=[own, spec], out_specs=spec),
        out_shape=jax.ShapeDtypeStruct(theirs.shape, theirs.dtype), compiler_params=_params(),
    )(core, g, theirs)


def sum_lead(a, name):
    nl, rows, cols = a.shape
    t = _row_tile(rows, cols, (1 << 20) // 2)

    def body(a_ref, o_ref):
        acc = a_ref[0].astype(F32)
        for s in range(1, nl):
            acc = acc + a_ref[s].astype(F32)
        o_ref[...] = acc

    return pl.pallas_call(
        body, name=name, grid=(rows // t,),
        in_specs=[pl.BlockSpec((nl, t, cols), lambda i: (0, i, 0))],
        out_specs=pl.BlockSpec((t, cols), lambda i: (i, 0)),
        out_shape=jax.ShapeDtypeStruct((rows, cols), F32), compiler_params=_params(),
    )(a)


def sum_chips(rx, csum, chip, name):
    nc, rows, cols = rx.shape
    t = _row_tile(rows, cols, 2 << 20)

    def body(chip_ref, r0, r1, r2, r3, own_ref, o_ref):
        acc = None
        for s, ref in enumerate((r0, r1, r2, r3)):
            term = jnp.where(chip_ref[0] == s, own_ref[0], ref[0]).astype(F32)
            acc = term if acc is None else acc + term
        o_ref[...] = acc

    def slot(s):
        return pl.BlockSpec((1, t, cols), lambda i, c, s=s: (jnp.where(c[0] == s, c[0] ^ 1, s), i, 0))

    return pl.pallas_call(
        body, name=name,
        grid_spec=pltpu.PrefetchScalarGridSpec(
            num_scalar_prefetch=1, grid=(rows // t,),
            in_specs=[slot(s) for s in range(nc)] + [pl.BlockSpec((1, t, cols), lambda i, c: (c[0], i, 0))],
            out_specs=pl.BlockSpec((t, cols), lambda i, c: (i, 0))),
        out_shape=jax.ShapeDtypeStruct((rows, cols), F32), compiler_params=_params(),
    )(chip, rx, rx, rx, rx, csum)


def cast_bf16(a, name):
    rows, cols = a.shape
    t = _row_tile(rows, cols, 2 << 20)

    def body(i_ref, o_ref):
        o_ref[...] = i_ref[...].astype(BF16)

    spec = pl.BlockSpec((t, cols), lambda i: (i, 0))
    return pl.pallas_call(body, name=name, grid=(rows // t,), in_specs=[spec], out_specs=spec,
                          out_shape=jax.ShapeDtypeStruct((rows, cols), BF16), compiler_params=_params())(a)


def _adamw_update(w, g, m, v):
    nm = ADAM_B1 * m + (1.0 - ADAM_B1) * g
    nv = ADAM_B2 * v + (1.0 - ADAM_B2) * (g * g)
    m_hat = nm * (1.0 / (1.0 - ADAM_B1 ** ADAM_STEP))
    v_hat = nv * (1.0 / (1.0 - ADAM_B2 ** ADAM_STEP))
    return -ADAM_LR * (m_hat / (jnp.sqrt(v_hat) + ADAM_EPS) + ADAM_WD * w), nm, nv


def adamw(w, g, m, v, name):
    rows, cols = w.shape
    t = _row_tile(rows, cols)

    def body(w_ref, g_ref, m_ref, v_ref, d_ref, nm_ref, nv_ref):
        d_ref[...], nm_ref[...], nv_ref[...] = _adamw_update(w_ref[...], g_ref[...], m_ref[...], v_ref[...])

    spec = pl.BlockSpec((t, cols), lambda i: (i, 0))
    shp = jax.ShapeDtypeStruct((rows, cols), F32)
    return pl.pallas_call(
        body, name=name, grid=(rows // t,), in_specs=[spec] * 4, out_specs=[spec] * 3,
        out_shape=[shp, shp, shp], compiler_params=_params(),
    )(w, g, m, v)


def adamw_halves(w, g_mine, g_other, m, v, core, name):
    rows, cols = w.shape
    hr = rows // 2
    t = _row_tile(hr, cols)
    nt = hr // t

    def body(core_ref, w_ref, gm_ref, go_ref, m_ref, v_ref, g_ref, d_ref, nm_ref, nv_ref):
        g = jnp.where(pl.program_id(0) // nt == core_ref[0], gm_ref[...], go_ref[...])
        g_ref[...] = g
        d_ref[...], nm_ref[...], nv_ref[...] = _adamw_update(w_ref[...], g, m_ref[...], v_ref[...])

    spec = pl.BlockSpec((t, cols), lambda i, c: (i, 0))
    half = pl.BlockSpec((t, cols), lambda i, c: (i % nt, 0))
    shp = jax.ShapeDtypeStruct((rows, cols), F32)
    return pl.pallas_call(
        body, name=name,
        grid_spec=pltpu.PrefetchScalarGridSpec(num_scalar_prefetch=1, grid=(2 * nt,),
                                               in_specs=[spec, half, half, spec, spec], out_specs=[spec] * 4),
        out_shape=[shp] * 4, compiler_params=_params(),
    )(core, w, g_mine, g_other, m, v)


WEIGHTS = ("norm_mix_pre", "norm_mix_post", "norm_ffn_pre", "norm_ffn_post", "w_in", "conv_short_w",
           "w_conv_branch", "lru_conv_w", "lru_conv_b", "lru_wa", "lru_ba", "lru_wx", "lru_bx", "lru_lambda",
           "w_lru_branch", "w_out", "ffn_w_up", "ffn_conv_w", "ffn_conv_b", "ffn_w_down")
BIG = ("w_in", "ffn_w_up", "w_conv_branch", "w_lru_branch", "w_out", "ffn_w_down")
BIG_KIND = ("col", "col", "row", "row", "row", "row")
SMALL = ("conv_short_w", "lru_conv_w", "lru_wa", "lru_ba", "lru_wx", "lru_bx", "ffn_conv_w")
REPL = ("norm_mix_pre", "norm_mix_post", "norm_ffn_pre", "norm_ffn_post", "lru_conv_b", "lru_lambda", "ffn_conv_b")
PACK_W = 256
SMALL_ROWS = 576
REPL_ROWS = 16
LOSS_ROW = 12
FFN_SHARD = 2 * D_FF // N_CHIPS
QUARTER = HEAD_DIM // N_CHIPS
SMALL_PARTS = (("conv_short_w", 3, (1, 3, PACK_W)), ("lru_conv_w", 4, (1, 4, PACK_W)),
               ("lru_wa", LRU_HEADS * QUARTER, (1, LRU_HEADS, QUARTER, HEAD_DIM)), ("lru_ba", LRU_HEADS, (1, LRU_HEADS, QUARTER)),
               ("lru_wx", LRU_HEADS * QUARTER, (1, LRU_HEADS, QUARTER, HEAD_DIM)), ("lru_bx", LRU_HEADS, (1, LRU_HEADS, QUARTER)),
               ("ffn_conv_w", 3 * FFN_SHARD // PACK_W, (1, 3, FFN_SHARD)))


def _pad8(nr):
    return -(-nr // SUBLANES) * SUBLANES


SMALL_OFFSET = {}
for _name, _nr, _ in SMALL_PARTS:
    SMALL_OFFSET[_name] = sum(_pad8(nr) for n, nr, _ in SMALL_PARTS[:len(SMALL_OFFSET)])
FFN_ROWS = FFN_SHARD // PACK_W
BIASES = ("lru_ba", "lru_bx")
TAPS3 = ("conv_short_w", "ffn_conv_w")


def pack_small(dicts):
    names = [n for n, _, _ in SMALL_PARTS]
    operands = [d[n].transpose(1, 0, 2) if n in TAPS3 else d[n] for d in dicts for n in names]

    def body(*refs):
        ins, outs = refs[:len(operands)], refs[len(operands):]
        for i, o in enumerate(outs):
            o[...] = jnp.zeros_like(o)
            for (name, nr, shape), p in zip(SMALL_PARTS, ins[i * len(names):(i + 1) * len(names)]):
                r0 = SMALL_OFFSET[name]
                if name in BIASES:
                    o[r0:r0 + nr, 0:QUARTER] = p[0]
                elif name == "ffn_conv_w":
                    for k in range(shape[1]):
                        for s in range(FFN_ROWS):
                            o[r0 + FFN_ROWS * k + s:r0 + FFN_ROWS * k + s + 1, :] = p[k, :, s * PACK_W:(s + 1) * PACK_W]
                elif name == "conv_short_w":
                    for k in range(nr):
                        o[r0 + k:r0 + k + 1, :] = p[k]
                else:
                    o[r0:r0 + nr, :] = p[0].reshape(nr, PACK_W)

    shape = jax.ShapeDtypeStruct((SMALL_ROWS, PACK_W), F32)
    return pl.pallas_call(body, name="pack_small", out_shape=[shape] * len(dicts), compiler_params=_params())(*operands)


def full_small(g4):
    def body(p, csw, lcw, wa, wx, fcw):
        chips = range(N_CHIPS)
        r0 = SMALL_OFFSET["conv_short_w"]
        csw[...] = jnp.concatenate([p[c, r0:r0 + 3, :] for c in chips], axis=1)
        r0 = SMALL_OFFSET["lru_conv_w"]
        lcw[...] = jnp.concatenate([p[c, r0:r0 + 4, :] for c in chips], axis=1)
        for name, o in (("lru_wa", wa), ("lru_wx", wx)):
            r0 = SMALL_OFFSET[name]
            for h in range(LRU_HEADS):
                for c in chips:
                    o[h, c * QUARTER:(c + 1) * QUARTER, :] = p[c, r0 + h * QUARTER:r0 + (h + 1) * QUARTER, :].astype(BF16)
        r0 = SMALL_OFFSET["ffn_conv_w"]
        for k in range(3):
            fcw[k:k + 1, :] = jnp.concatenate(
                [p[c, r0 + FFN_ROWS * k + s:r0 + FFN_ROWS * k + s + 1, :] for c in chips for s in range(FFN_ROWS)], axis=1)

    mat = jax.ShapeDtypeStruct((LRU_HEADS, HEAD_DIM, HEAD_DIM), BF16)
    csw, lcw, wa, wx, fcw = pl.pallas_call(
        body, name="full_small",
        out_shape=[jax.ShapeDtypeStruct((3, D_MODEL), F32), jax.ShapeDtypeStruct((4, D_MODEL), F32), mat, mat,
                   jax.ShapeDtypeStruct((3, 2 * D_FF), F32)],
        compiler_params=_params())(g4)

    def bias(name):
        r0 = SMALL_OFFSET[name]
        return g4[:, r0:r0 + LRU_HEADS, :QUARTER].transpose(1, 0, 2).reshape(1, D_MODEL)

    return dict(conv_short_w=csw, lru_conv_w=lcw, lru_wa=wa, lru_wx=wx, ffn_conv_w=fcw,
                lru_ba=bias("lru_ba"), lru_bx=bias("lru_bx"))


def split_small(full):
    def bias(name):
        return full[name].reshape(LRU_HEADS, N_CHIPS, QUARTER).transpose(1, 0, 2)

    def body(csw, lcw, wa, wx, fcw, ba, bx, o):
        o[...] = jnp.zeros_like(o)
        for c in range(N_CHIPS):
            cols = slice(c * PACK_W, (c + 1) * PACK_W)
            r0 = SMALL_OFFSET["conv_short_w"]
            o[c, r0:r0 + 3, :] = csw[:, cols]
            r0 = SMALL_OFFSET["lru_conv_w"]
            o[c, r0:r0 + 4, :] = lcw[:, cols]
            for name, p in (("lru_wa", wa), ("lru_wx", wx)):
                r0 = SMALL_OFFSET[name]
                for h in range(LRU_HEADS):
                    o[c, r0 + h * QUARTER:r0 + (h + 1) * QUARTER, :] = p[h, c * QUARTER:(c + 1) * QUARTER, :]
            for name, p in (("lru_ba", ba), ("lru_bx", bx)):
                r0 = SMALL_OFFSET[name]
                o[c, r0:r0 + LRU_HEADS, 0:QUARTER] = p[c]
            r0 = SMALL_OFFSET["ffn_conv_w"]
            for k in range(3):
                for s in range(FFN_ROWS):
                    lo = c * FFN_SHARD + s * PACK_W
                    o[c, r0 + FFN_ROWS * k + s:r0 + FFN_ROWS * k + s + 1, :] = fcw[k:k + 1, lo:lo + PACK_W]

    return pl.pallas_call(
        body, name="split_small", out_shape=jax.ShapeDtypeStruct((N_CHIPS, SMALL_ROWS, PACK_W), F32),
        compiler_params=_params(),
    )(full["conv_short_w"], full["lru_conv_w"], full["lru_wa"], full["lru_wx"], full["ffn_conv_w"],
      bias("lru_ba"), bias("lru_bx"))


def pack_repl(dicts, loss=None):
    operands = [d[n] for d in dicts for n in REPL] + ([loss] if loss is not None else [])

    def body(*refs):
        ins, outs = refs[:len(operands)], refs[len(operands):]
        for i, o in enumerate(outs):
            o[...] = jnp.zeros_like(o)
            r0 = 0
            for p in ins[i * len(REPL):(i + 1) * len(REPL)]:
                for s in range(p.shape[1] // D_MODEL):
                    o[r0:r0 + 1, :] = p[:, s * D_MODEL:(s + 1) * D_MODEL]
                    r0 += 1
        if loss is not None:
            outs[-1][LOSS_ROW:LOSS_ROW + 1, :] = jnp.tile(ins[-1][...], (1, D_MODEL // 128))

    shape = jax.ShapeDtypeStruct((REPL_ROWS, D_MODEL), F32)
    return pl.pallas_call(body, name="pack_repl" + ("_loss" if loss is not None else ""),
                          out_shape=[shape] * len(dicts), compiler_params=_params())(*operands)


def _lane_concat(ref, r0, n):
    return jnp.concatenate([ref[r0 + s:r0 + s + 1, :] for s in range(n)], axis=1)


def unpack_small(packs):
    names = [n for n, _, _ in SMALL_PARTS]

    def body(*refs):
        ins, outs = refs[:len(packs)], refs[len(packs):]
        for i, p in enumerate(ins):
            for (name, nr, shape), o in zip(SMALL_PARTS, outs[i * len(names):(i + 1) * len(names)]):
                r0 = SMALL_OFFSET[name]
                if name in BIASES:
                    o[0] = p[r0:r0 + nr, 0:QUARTER]
                elif name == "ffn_conv_w":
                    for k in range(shape[1]):
                        o[k] = _lane_concat(p, r0 + FFN_ROWS * k, FFN_ROWS)
                elif name == "conv_short_w":
                    for k in range(nr):
                        o[k] = p[r0 + k:r0 + k + 1, :]
                else:
                    o[0] = p[r0:r0 + nr, :].reshape(shape[1:])

    shapes = [jax.ShapeDtypeStruct((s[1], 1, s[2]) if n in TAPS3 else s, F32) for n, _, s in SMALL_PARTS]
    res = pl.pallas_call(body, name="unpack_small", out_shape=shapes * len(packs), compiler_params=_params())(*packs)
    out = []
    for i in range(len(packs)):
        d = dict(zip(names, res[i * len(names):(i + 1) * len(names)]))
        for n in TAPS3:
            d[n] = d[n].transpose(1, 0, 2)
        out.append(d)
    return out


def unpack_repl(packs):
    rows = [(2 * D_FF // D_MODEL) if n == "ffn_conv_b" else 1 for n in REPL]

    def body(*refs):
        ins, outs = refs[:len(packs)], refs[len(packs):]
        for i, p in enumerate(ins):
            r0 = 0
            for nr, o in zip(rows, outs[i * len(REPL):(i + 1) * len(REPL)]):
                o[...] = _lane_concat(p, r0, nr)
                r0 += nr

    shapes = [jax.ShapeDtypeStruct((1, nr * D_MODEL), F32) for nr in rows]
    res = pl.pallas_call(body, name="unpack_repl", out_shape=shapes * len(packs), compiler_params=_params())(*packs)
    return [dict(zip(REPL, res[i * len(REPL):(i + 1) * len(REPL)])) for i in range(len(packs))]


def kernel(x, norm_mix_pre, norm_mix_post, norm_ffn_pre, norm_ffn_post, w_in, conv_short_w, w_conv_branch, lru_conv_w, lru_conv_b, lru_wa, lru_ba, lru_wx, lru_bx, lru_lambda, w_lru_branch, w_out, ffn_w_up, ffn_conv_w, ffn_conv_b, ffn_w_down, loss_target, m_norm_mix_pre, m_norm_mix_post, m_norm_ffn_pre, m_norm_ffn_post, m_w_in, m_conv_short_w, m_w_conv_branch, m_lru_conv_w, m_lru_conv_b, m_lru_wa, m_lru_ba, m_lru_wx, m_lru_bx, m_lru_lambda, m_w_lru_branch, m_w_out, m_ffn_w_up, m_ffn_conv_w, m_ffn_conv_b, m_ffn_w_down, v_norm_mix_pre, v_norm_mix_post, v_norm_ffn_pre, v_norm_ffn_post, v_w_in, v_conv_short_w, v_w_conv_branch, v_lru_conv_w, v_lru_conv_b, v_lru_wa, v_lru_ba, v_lru_wx, v_lru_bx, v_lru_lambda, v_w_lru_branch, v_w_out, v_ffn_w_up, v_ffn_conv_w, v_ffn_conv_b, v_ffn_w_down):
    given = dict(locals())
    w = {n: given[n] for n in WEIGHTS}
    m = {n: given["m_" + n] for n in WEIGHTS}
    v = {n: given["v_" + n] for n in WEIGHTS}

    xi, yi, ci = _position()
    chip_i = 2 * xi + yi
    chip = chip_i.astype(jnp.int32).reshape(1)
    core = ci.astype(jnp.int32).reshape(1)
    xs, target = x[0], loss_target[0]
    g1, g2, g3, g4 = w["norm_mix_pre"], w["norm_mix_post"], w["norm_ffn_pre"], w["norm_ffn_post"]
    shard = {n: cast_bf16(w[n][0], "cast_" + n) for n in BIG}
    small_shard, m_small, v_small = pack_small([w, m, v])

    def gathered(bufs, names):
        return [_own_slot(b, small_shard if n == "small" else shard[n], chip_i) for b, n in zip(bufs, names)]

    def chip_sums(arrays, kinds, tag):
        theirs = pair_split(arrays, kinds, "pair_split_" + tag)
        return [add_pair(g, k, t, core, "pair_add_%s_%d" % (tag, i)) for i, (g, k, t) in enumerate(zip(arrays, kinds, theirs))]

    h1, h1t = norm_in(xs, g1)
    win4, small4 = gathered(run_ride(gather_ride([shard["w_in"], small_shard]), "gather_first"), ("w_in", "small"))
    small = full_small(small4)
    first_up = 256
    (proj,), got = matmul_cols(
        h1, win4, "proj_fwd",
        ride=gather_ride([shard["w_conv_branch"], shard["w_lru_branch"], shard["w_out"], shard["ffn_w_up"]],
                         items=[(0, 0, 256), (1, 0, 256), (2, 0, 256), (3, 0, first_up)]))
    wcb, wlb, wout = [g.reshape(-1, D_MODEL) for g in gathered(got[:3], ("w_conv_branch", "w_lru_branch", "w_out"))]
    got = got[3:]
    up_piece = lambda r0, nr, into=None: gather_ride([shard["ffn_w_up"]], items=[(0, r0, nr)], into=into)
    down_piece = lambda r0, nr, into=None: gather_ride([shard["ffn_w_down"]], items=[(0, r0, nr)], into=into)
    q, ya = mix_conv_fwd(proj, small["conv_short_w"])
    (xl, r, gi, h, yb), got = mix_lru_fwd(
        proj, small["lru_conv_w"], w["lru_conv_b"], small["lru_wa"], small["lru_ba"],
        small["lru_wx"], small["lru_bx"], w["lru_lambda"], ride=up_piece(first_up, 512, got))
    (a, b, merged), got = branch_merge_fwd(ya, yb, wcb, wlb, proj, ride=up_piece(first_up + 512, 256, got))
    (wup4,) = gathered(got, ("ffn_w_up",))
    (mix, x2, h2, h2t), got = mix_out_fwd(merged, wout, xs, g2, g3, ride=down_piece(0, 256))
    (up, act, f), got = ffn_up_act_fwd(h2, wup4, small["ffn_conv_w"], w["ffn_conv_b"], ride=down_piece(256, 512, got))
    wdown = gathered(got, ("ffn_w_down",))[0].reshape(-1, D_MODEL)
    dy, dout, loss, dg4 = ffn_down_loss(f, wdown, x2, target, g4)

    dh2, dwup, dwdown, dfw, dfb = ffn_up_bwd(dout, wdown, up, act, f, small["ffn_conv_w"], wup4, h2t)
    cs_down, cs_up = chip_sums([dwdown, dwup], ["row", "col2"], "ffn")
    down_rows = lambda r0, nr, into=None: exchange_ride([cs_down], items=[(0, r0, nr)], into=into)
    up_rows = lambda r0, nr, into=None: exchange_ride([cs_up], items=[(0, r0, nr)], into=into)
    (dx2, dmix, dg3, dg2), rx_down = norms_mid_bwd(dh2, x2, dy, mix, g3, g2, ride=down_rows(0, 128))
    (da, db, dwout, dgates), rx_down = mix_out_bwd(dmix, wout, merged, a, b, proj, ride=down_rows(128, 256, rx_down))
    (dconv, dwcb, dws), rx_up = mix_conv_bwd(da, wcb, proj, q, small["conv_short_w"], ride=up_rows(0, 176))
    cs_mid = chip_sums([dwout, dwcb], ["row", "row"], "mid")
    (dlru, dwlb, dwa, dwx, dba, dbx, dwl, dbl, dlam), rx_up = mix_lru_bwd(
        db, wlb, proj, xl, r, gi, h, small["lru_conv_w"], small["lru_wa"], small["lru_wx"], w["lru_lambda"],
        ride=up_rows(176, 336, rx_up))
    grads = dict(norm_mix_post=dg2, norm_ffn_pre=dg3, norm_ffn_post=dg4, conv_short_w=dws, lru_conv_w=dwl,
                 lru_conv_b=dbl, lru_wa=dwa, lru_ba=dba, lru_wx=dwx, lru_bx=dbx, lru_lambda=dlam,
                 ffn_conv_w=jnp.concatenate([dfw[0], dfw[1]], axis=1), ffn_conv_b=jnp.concatenate([dfb[0], dfb[1]], axis=1))
    cs_late = chip_sums([dwlb, split_small(grads)], ["row", "lead"], "late")
    dproj = [dconv, dlru, dgates]
    (dwin,), rx_all = matmul_cols_bwd(dproj, h1t, "proj_wgrad", True, ride=exchange_ride(cs_mid + cs_late))
    rx_mid, rx_late = rx_all[:2], rx_all[2:]
    cs_in = chip_sums([dwin], ["col"], "in")
    in_rows = lambda r0, nr, into=None: exchange_ride(cs_in, items=[(0, r0, nr)], into=into)
    (dh1,), rx_in = matmul_cols_bwd(dproj, win4, "proj_dgrad", False, ride=in_rows(0, 384))
    (dx, grads["norm_mix_pre"]), rx_in = norm_in_bwd(dh1, xs, dx2, g1, ride=in_rows(384, 128, rx_in))
    rx_in = rx_in[0]
    (rep_part,) = pack_repl([grads], loss)
    (rep_all,) = run_ride(exchange_ride([], rep=rep_part), "exchange_repl")

    order = (("w_in", cs_in[0], rx_in), ("ffn_w_up", cs_up, rx_up[0]), ("w_conv_branch", cs_mid[1], rx_mid[1]),
             ("w_lru_branch", cs_late[0], rx_late[0]), ("w_out", cs_mid[0], rx_mid[0]),
             ("ffn_w_down", cs_down, rx_down[0]), ("small", cs_late[1], rx_late[1]))
    halves = [sum_chips(rx, cs, chip, "chip_sum_" + n) for n, cs, rx in order]
    me = 4 * xi + 2 * yi + ci
    rep_grad = sum_lead(_own_slot(rep_all, rep_part, me), "device_sum")
    others = pair_swap(halves)

    g_out, d_out, m_out, v_out = {}, {}, {}, {}
    for n, gm, go in zip(BIG, halves[:-1], others[:-1]):
        g, d, nm, nv = adamw_halves(w[n][0], gm, go, m[n][0], v[n][0], core, "adamw_" + n)
        g_out[n], d_out[n], m_out[n], v_out[n] = g[None], d[None], nm[None], nv[None]
    bufs = adamw_halves(small_shard, halves[-1], others[-1], m_small, v_small, core, "adamw_small")
    for dst, part in zip((g_out, d_out, m_out, v_out), unpack_small(bufs)):
        dst.update(part)
    w_rep, m_rep, v_rep = pack_repl([w, m, v])
    d, nm, nv = adamw(w_rep, rep_grad, m_rep, v_rep, "adamw_repl")
    for dst, part in zip((g_out, d_out, m_out, v_out), unpack_repl([rep_grad, d, nm, nv])):
        dst.update(part)

    return (rep_grad[LOSS_ROW, 0], dx[None], *[g_out[n] for n in WEIGHTS], *[d_out[n] for n in WEIGHTS],
            *[m_out[n] for n in WEIGHTS], *[v_out[n] for n in WEIGHTS])
```

```python
import functools
import math

import jax
import jax.numpy as jnp
from jax import lax
from jax.experimental import pallas as pl
from jax.experimental.pallas import tpu as pltpu

F32 = jnp.float32
BF16 = jnp.bfloat16

D_MODEL = 1024
N_CHIPS = 4
N_SEG = 7
D_FF = 3 * D_MODEL
LRU_HEADS = 4
HEAD_DIM = D_MODEL // LRU_HEADS
LRU_C = 8.0
RMS_EPS = 1e-6
CW = 256
FW = 256
SUBLANES = 8
SCAN_UNROLL = 8
VMEM_LIMIT = 58 * 1024 * 1024

ADAM_LR = 0.001
ADAM_B1 = 0.9
ADAM_B2 = 0.999
ADAM_EPS = 1e-08
ADAM_WD = 0.01
ADAM_STEP = 10

_GELU_C = math.sqrt(2.0 / math.pi)
_GELU_K = 0.044715


def _params(**kw):
    return pltpu.CompilerParams(vmem_limit_bytes=VMEM_LIMIT, **kw)


def _sigmoid(x):
    return 1.0 / (1.0 + jnp.exp(-x))


def _gelu(x):
    t = jnp.tanh(_GELU_C * (x + _GELU_K * x * x * x))
    return 0.5 * x * (1.0 + t)


def _gelu_and_grad(x):
    x2 = x * x
    t = jnp.tanh(_GELU_C * (x + _GELU_K * x * x2))
    g = 0.5 * x * (1.0 + t)
    dg = 0.5 * (1.0 + t) + 0.5 * x * (1.0 - t * t) * _GELU_C * (1.0 + 3.0 * _GELU_K * x2)
    return g, dg


def _log_sigmoid(x):
    e = jnp.exp(-jnp.abs(x))
    u = 1.0 + e
    l1p = jnp.where(u == 1.0, e, jnp.log(u) * e / (u - 1.0))
    return jnp.minimum(x, 0.0) - l1p


def _neg_expm1(z):
    series = -z * (1.0 + z * (0.5 + z * (1.0 / 6.0 + z * (1.0 / 24.0 + z * (1.0 / 120.0 + z * (1.0 / 720.0))))))
    return jnp.where(z > -0.2, series, 1.0 - jnp.exp(z))


def _rows(shape):
    return lax.broadcasted_iota(jnp.int32, shape, 0)


def _shift_down(x, k):
    return jnp.where(_rows(x.shape) >= k, pltpu.roll(x, k, 0), 0.0)


def _shift_up(x, k):
    n = x.shape[0]
    return jnp.where(_rows(x.shape) < n - k, pltpu.roll(x, n - k, 0), 0.0)


def _delays(x, k_width):
    return [x] + [_shift_down(x, j) for j in range(1, k_width)]


def _advances(dy, k_width):
    return [dy] + [_shift_up(dy, j) for j in range(1, k_width)]


def _taps_sum(shifted, w_ref, b=None):
    k_width = w_ref.shape[0]
    y = w_ref[k_width - 1:k_width, :] * shifted[0]
    for j in range(1, k_width):
        y = y + w_ref[k_width - 1 - j:k_width - j, :] * shifted[j]
    if b is not None:
        y = y + b
    return y


def _causal_conv(x, w_ref, b=None):
    return _taps_sum(_delays(x, w_ref.shape[0]), w_ref, b)


def _conv_wgrad(advanced, x):
    k_width = len(advanced)
    rows = [jnp.sum(advanced[k_width - 1 - k] * x, axis=0, keepdims=True) for k in range(k_width)]
    return jnp.concatenate(rows, axis=0)


def _dot(a, b):
    return jnp.dot(a, b, preferred_element_type=F32)


def _dot_nt(a, b):
    return lax.dot_general(a, b, (((1,), (1,)), ((), ())), preferred_element_type=F32)


def _dot_tn(a, b):
    return lax.dot_general(a, b, (((0,), (0,)), ((), ())), preferred_element_type=F32)


def _rms_stats(x):
    r = lax.rsqrt(jnp.mean(x * x, axis=-1, keepdims=True) + RMS_EPS)
    return x * r, r


def _rms_bwd(n, r, g, dy):
    dn = dy * g
    dx = r * (dn - n * jnp.mean(dn * n, axis=-1, keepdims=True))
    return dx, dy * n


def _scan(a_ref, b_ref, h_ref, reverse):
    n, c = a_ref.shape
    row = lax.broadcasted_iota(jnp.int32, (SUBLANES, c), 0)
    span = SCAN_UNROLL * SUBLANES
    n_trips = n // span

    def within(a, b):
        for k in (1, 2, 4):
            if reverse:
                keep, shift = row < SUBLANES - k, SUBLANES - k
            else:
                keep, shift = row >= k, k
            ap = jnp.where(keep, pltpu.roll(a, shift, 0), 1.0)
            bp = jnp.where(keep, pltpu.roll(b, shift, 0), 0.0)
            b = a * bp + b
            a = a * ap
        return a, b

    def trip(t, carry):
        base = pl.multiple_of((n_trips - 1 - t if reverse else t) * span, span)
        order = list(reversed(range(SCAN_UNROLL))) if reverse else list(range(SCAN_UNROLL))
        loaded = [(a_ref[pl.ds(base + u * SUBLANES, SUBLANES), :], b_ref[pl.ds(base + u * SUBLANES, SUBLANES), :])
                  for u in order]
        out = []
        for a, b in [within(a, b) for a, b in loaded]:
            h = a * carry + b
            out.append(h)
            carry = h[0:1, :] if reverse else h[SUBLANES - 1:SUBLANES, :]
        for u, h in zip(order, out):
            h_ref[pl.ds(base + u * SUBLANES, SUBLANES), :] = h
        return carry

    lax.fori_loop(0, n_trips, trip, jnp.zeros((1, c), F32))


def _scan_forward(a_ref, b_ref, h_ref):
    _scan(a_ref, b_ref, h_ref, False)


def _scan_backward(c_ref, b_ref, g_ref):
    _scan(c_ref, b_ref, g_ref, True)


MESH = pl.DeviceIdType.MESH
_HBM = pl.BlockSpec(memory_space=pltpu.HBM)
_OTHER_CHIPS = ((1, 0), (0, 1), (1, 1))
_OTHER_DEVICES = tuple((dx, dy, dc) for dx in (0, 1) for dy in (0, 1) for dc in (0, 1) if dx or dy or dc)
N_DEVICES = 8


def _position():
    return lax.axis_index("x"), lax.axis_index("y"), lax.axis_index("c")


def _flip(v, d):
    return 1 - v if d else v


def _chip(x, y, p):
    px, py = _flip(x, _OTHER_CHIPS[p][0]), _flip(y, _OTHER_CHIPS[p][1])
    return px, py, 2 * px + py


class _Ride:
    def __init__(self, srcs, bufs, scratch, plan, collective_id):
        self.srcs, self.bufs, self.scratch, self.plan = list(srcs), list(bufs), list(scratch), plan
        self.collective_id = collective_id


NEIGHBOURS_AND_SIBLING = 1
OTHER_CHIPS_SAME_CORE = 2
ALL_DEVICES = 3
SIBLING = 4


def _handshake(peers):
    barrier = pltpu.get_barrier_semaphore()
    for peer in peers:
        pl.semaphore_signal(barrier, inc=1, device_id=peer, device_id_type=MESH)
    pl.semaphore_wait(barrier, len(peers))


def _call(body, *, name, grid, in_specs, out_specs, out_shape, operands, scratch_shapes=(), ride=None):
    in_specs, out_specs, out_shape = list(in_specs), list(out_specs), list(out_shape)
    scratch_shapes = list(scratch_shapes)
    if ride is None:
        return pl.pallas_call(body, name=name, grid=grid, in_specs=in_specs, out_specs=out_specs, out_shape=out_shape,
                              scratch_shapes=scratch_shapes, compiler_params=_params())(*operands)
    n_in, n_out, n_scr = len(in_specs), len(out_shape), len(scratch_shapes)
    old = [i for i, b in enumerate(ride.bufs) if not isinstance(b, jax.ShapeDtypeStruct)]
    n_src, n_old, n_buf = len(ride.srcs), len(old), len(ride.bufs)

    def full_body(*refs):
        o0 = n_in + n_src + n_old
        s0 = o0 + n_out + n_buf
        start, relay, relay_on, finish = ride.plan(refs[n_in:n_in + n_src], refs[o0 + n_out:s0], refs[s0 + n_scr:])
        ids = [pl.program_id(i) for i in range(len(grid))]
        first = functools.reduce(jnp.logical_and, [i == 0 for i in ids])
        middle = functools.reduce(jnp.logical_and, [ids[0] == grid[0] // 2] + [i == 0 for i in ids[1:]])
        last = functools.reduce(jnp.logical_and, [i == g - 1 for i, g in zip(ids, grid)])
        pl.when(first)(start)
        pl.when(middle)(relay)
        pl.when(last)(relay_on)
        body(*refs[:n_in], *refs[o0:o0 + n_out], *refs[s0:s0 + n_scr])
        pl.when(last)(finish)

    shapes = [jax.ShapeDtypeStruct(b.shape, b.dtype) for b in ride.bufs]
    res = pl.pallas_call(
        full_body, name=name, grid=grid,
        in_specs=in_specs + [_HBM] * (n_src + n_old), out_specs=out_specs + [_HBM] * n_buf,
        out_shape=out_shape + shapes, scratch_shapes=scratch_shapes + ride.scratch,
        input_output_aliases={n_in + n_src + k: n_out + i for k, i in enumerate(old)},
        compiler_params=_params(collective_id=ride.collective_id),
    )(*operands, *ride.srcs, *[ride.bufs[i] for i in old])
    return list(res[:n_out]), list(res[n_out:])


def run_ride(ride, name):
    def body():
        pass

    return _call(body, name=name, grid=(1,), in_specs=[], out_specs=[], out_shape=[], operands=[], ride=ride)[1]


def gather_ride(shards, items=None, into=None):
    items = items or [(a, 0, s.shape[0]) for a, s in enumerate(shards)]
    bufs = into or [jax.ShapeDtypeStruct((N_CHIPS,) + s.shape, s.dtype) for s in shards]
    nrel = len(_OTHER_CHIPS)

    def plan(srcs, dsts, sems):
        ici_send, ici_recv, hop_send, hop_recv, sib_send, sib_recv = sems
        x, y, c = _position()
        j = 2 * x + y

        def rows(ref, it, h, q=None):
            half = it[2] // 2
            if q is None:
                return ref.at[pl.ds(it[1] + h * half, half), :]
            return ref.at[pl.ds(it[1] + h * half + q * (half // 2), half // 2), :]

        def ici(i, p, slot):
            it = items[i]
            px, py, _ = _chip(x, y, p)
            return pltpu.make_async_remote_copy(
                src_ref=rows(srcs[it[0]], it, c), dst_ref=rows(dsts[it[0]].at[slot], it, c),
                send_sem=ici_send.at[i * nrel + p], recv_sem=ici_recv.at[i * nrel + p],
                device_id=(px, py, c), device_id_type=MESH)

        def hop(i, p, slot):
            it = items[i]
            part = rows(dsts[it[0]].at[slot], it, c, p)
            px, py, _ = _chip(x, y, 1 - p)
            return pltpu.make_async_remote_copy(
                src_ref=part, dst_ref=part, send_sem=hop_send.at[i * 2 + p], recv_sem=hop_recv.at[i * 2 + p],
                device_id=(px, py, c), device_id_type=MESH)

        def sib(i, p, h):
            it = items[i]
            part = rows(dsts[it[0]].at[_chip(x, y, p)[2]], it, h)
            return pltpu.make_async_remote_copy(
                src_ref=part, dst_ref=part, send_sem=sib_send.at[i * nrel + p], recv_sem=sib_recv.at[i * nrel + p],
                device_id=(x, y, 1 - c), device_id_type=MESH)

        every = range(len(items))
        diag = _chip(x, y, 2)[2]

        def start():
            _handshake([_chip(x, y, 0)[:2] + (c,), _chip(x, y, 1)[:2] + (c,), (x, y, 1 - c)])
            for i in every:
                for p in (0, 1):
                    ici(i, p, j).start()

        def relay():
            for i in every:
                for p in (0, 1):
                    k = _chip(x, y, p)[2]
                    ici(i, p, k).wait_recv()
                    hop(i, p, k).start()
                    sib(i, p, c).start()

        def relay_on():
            for i in every:
                for p in (0, 1):
                    hop(i, p, diag).wait_recv()
                sib(i, 2, c).start()

        def finish():
            for i in every:
                for p in range(nrel):
                    sib(i, p, 1 - c).wait_recv()
            for i in every:
                for p in (0, 1):
                    ici(i, p, j).wait_send()
                    hop(i, p, _chip(x, y, p)[2]).wait_send()
                for p in range(nrel):
                    sib(i, p, c).wait_send()

        return start, relay, relay_on, finish

    n = len(items)
    sems = [pltpu.SemaphoreType.DMA((n * nrel,))] * 2 + [pltpu.SemaphoreType.DMA((n * 2,))] * 2 \
        + [pltpu.SemaphoreType.DMA((n * nrel,))] * 2
    return _Ride(shards, bufs, sems, plan, NEIGHBOURS_AND_SIBLING)


def exchange_ride(sums, items=None, into=None, rep=None):
    items = [(a, 0, s.shape[1]) for a, s in enumerate(sums)] if items is None else items
    into = into or [None] * len(sums)
    bufs = [jax.ShapeDtypeStruct(s.shape, s.dtype) if b is None else b for s, b in zip(sums, into)]
    srcs = list(sums)
    scratch = [pltpu.SemaphoreType.DMA((max(len(items), 1) * len(_OTHER_CHIPS),))] * 2
    if rep is not None:
        srcs.append(rep)
        bufs.append(jax.ShapeDtypeStruct((N_DEVICES,) + rep.shape, rep.dtype))
        scratch += [pltpu.SemaphoreType.DMA((len(_OTHER_DEVICES),))] * 2
    nrel = len(_OTHER_CHIPS)

    def plan(src_refs, dst_refs, sems):
        x, y, c = _position()
        j = 2 * x + y
        me = 4 * x + 2 * y + c

        def part(i, p, src_slot, dst_slot):
            a, r0, nr = items[i]
            px, py, _ = _chip(x, y, p)
            return pltpu.make_async_remote_copy(
                src_ref=src_refs[a].at[src_slot, pl.ds(r0, nr), :], dst_ref=dst_refs[a].at[dst_slot, pl.ds(r0, nr), :],
                send_sem=sems[0].at[i * nrel + p], recv_sem=sems[1].at[i * nrel + p],
                device_id=(px, py, c), device_id_type=MESH)

        def device(q):
            dx, dy, dc = _OTHER_DEVICES[q]
            return _flip(x, dx), _flip(y, dy), _flip(c, dc)

        def rep_copy(q, slot):
            return pltpu.make_async_remote_copy(
                src_ref=src_refs[-1], dst_ref=dst_refs[-1].at[slot], send_sem=sems[2].at[q], recv_sem=sems[3].at[q],
                device_id=device(q), device_id_type=MESH)

        pairs = [(i, p) for i in range(len(items)) for p in range(nrel)]
        others = range(len(_OTHER_DEVICES)) if rep is not None else ()

        def start():
            if rep is None:
                _handshake([_chip(x, y, p)[:2] + (c,) for p in range(nrel)])
            else:
                _handshake([device(q) for q in others])
            for i, p in pairs:
                part(i, p, _chip(x, y, p)[2], j).start()
            for q in others:
                rep_copy(q, me).start()

        def finish():
            for i, p in pairs:
                k = _chip(x, y, p)[2]
                part(i, p, k, k).wait_recv()
            for q in others:
                px, py, pc = device(q)
                rep_copy(q, 4 * px + 2 * py + pc).wait_recv()
            for i, p in pairs:
                part(i, p, _chip(x, y, p)[2], j).wait_send()
            for q in others:
                rep_copy(q, me).wait_send()

        return start, lambda: None, lambda: None, finish

    return _Ride(srcs, bufs, scratch, plan, OTHER_CHIPS_SAME_CORE if rep is None else ALL_DEVICES)


def _own_slot(buf, own, index):
    return lax.dynamic_update_slice(buf, own[None], (index,) + (0,) * own.ndim)


def _token_tile(s):
    return min(s, 512)


def norm_in(x, g):
    s, d = x.shape
    t = _token_tile(s)

    def body(x_ref, g_ref, o_ref, ot_ref):
        n, _ = _rms_stats(x_ref[...])
        h = n * g_ref[...]
        o_ref[...] = h.astype(BF16)
        ot_ref[...] = h.T.astype(BF16)

    return pl.pallas_call(
        body, name="norm_in", grid=(s // t,),
        in_specs=[pl.BlockSpec((t, d), lambda i: (i, 0)), pl.BlockSpec((1, d), lambda i: (0, 0))],
        out_specs=[pl.BlockSpec((t, d), lambda i: (i, 0)), pl.BlockSpec((d, t), lambda i: (0, i))],
        out_shape=[jax.ShapeDtypeStruct((s, d), BF16), jax.ShapeDtypeStruct((d, s), BF16)],
        compiler_params=_params(),
    )(x, g)


def matmul_cols(a, w4, name, ride=None):
    m, k = a.shape
    nj, _, ns = w4.shape
    nb = ns // CW

    def body(a_ref, w_ref, o_ref):
        o_ref[...] = _dot(a_ref[...], w_ref[0])

    return _call(
        body, name=name, grid=(nj, nb),
        in_specs=[pl.BlockSpec((m, k), lambda j, b: (0, 0)),
                  pl.BlockSpec((1, k, CW), lambda j, b: (j, 0, b))],
        out_specs=[pl.BlockSpec((m, CW), lambda j, b: (0, j * nb + b))],
        out_shape=[jax.ShapeDtypeStruct((m, nj * ns), F32)],
        operands=(a, w4), ride=ride)


def mix_conv_fwd(proj, ws, ride=None):
    s = proj.shape[0]
    nblk = D_MODEL // CW

    def body(cb_ref, cc_ref, cx_ref, ws_ref, q_ref, ya_ref):
        q = _causal_conv(cc_ref[...] * cx_ref[...], ws_ref)
        q_ref[...] = q
        ya_ref[...] = (cb_ref[...] * q).astype(BF16)

    seg = lambda k: pl.BlockSpec((s, CW), lambda c, k=k: (0, k * nblk + c))
    return _call(
        body, name="mix_conv_fwd", grid=(nblk,),
        in_specs=[seg(0), seg(1), seg(2), pl.BlockSpec((3, CW), lambda c: (0, c))],
        out_specs=[pl.BlockSpec((s, CW), lambda c: (0, c))] * 2,
        out_shape=[jax.ShapeDtypeStruct((s, D_MODEL), F32), jax.ShapeDtypeStruct((s, D_MODEL), BF16)],
        operands=(proj, proj, proj, ws), ride=ride)


def _lru_gates(r, ls):
    log_a = LRU_C * r * ls
    a = jnp.exp(log_a)
    mult = jnp.sqrt(_neg_expm1(2.0 * log_a))
    mult = jnp.where(_rows(r.shape) == 0, 1.0, mult)
    return a, mult


def mix_lru_fwd(proj, wl, bl, wa, ba, wx, bx, lam, ride=None):
    s = proj.shape[0]
    nblk = D_MODEL // CW

    def body(lx_ref, ly_ref, wl_ref, bl_ref, wa_ref, ba_ref, wx_ref, bx_ref, lam_ref,
             xl_ref, r_ref, i_ref, h_ref, yb_ref, a_scr, u_scr):
        xl = _causal_conv(lx_ref[...], wl_ref, bl_ref[...])
        xlb = xl.astype(BF16)
        xl_ref[...] = xlb
        r = _sigmoid(_dot(xlb, wa_ref[0]) + ba_ref[...])
        i = _sigmoid(_dot(xlb, wx_ref[0]) + bx_ref[...])
        r_ref[...] = r.astype(BF16)
        i_ref[...] = i.astype(BF16)
        a, mult = _lru_gates(r, _log_sigmoid(lam_ref[...]))
        a_scr[...] = a
        u_scr[...] = mult * i * xl
        _scan_forward(a_scr, u_scr, h_ref)
        yb_ref[...] = (h_ref[...] * _gelu(ly_ref[...])).astype(BF16)

    blk = lambda k: pl.BlockSpec((s, CW), lambda c, k=k: (0, k * nblk + c))
    vec = pl.BlockSpec((1, CW), lambda c: (0, c))
    mat = pl.BlockSpec((1, CW, CW), lambda c: (c, 0, 0))
    out = pl.BlockSpec((s, CW), lambda c: (0, c))
    f = jax.ShapeDtypeStruct((s, D_MODEL), F32)
    hb = jax.ShapeDtypeStruct((s, D_MODEL), BF16)
    return _call(
        body, name="mix_lru_fwd", grid=(nblk,),
        in_specs=[blk(3), blk(4), pl.BlockSpec((4, CW), lambda c: (0, c)), vec, mat, vec, mat, vec, vec],
        out_specs=[out] * 5,
        out_shape=[hb, hb, hb, f, hb],
        scratch_shapes=[pltpu.VMEM((s, CW), F32), pltpu.VMEM((s, CW), F32)],
        operands=(proj, proj, wl, bl, wa, ba, wx, bx, lam), ride=ride)


def branch_merge_fwd(ya, yb, wcb, wlb, proj, ride=None):
    s = ya.shape[0]
    nblk = D_MODEL // CW

    def body(ya_ref, yb_ref, wcb_ref, wlb_ref, gc_ref, gl_ref, a_ref, b_ref, m_ref):
        a = _dot(ya_ref[...], wcb_ref[...])
        b = _dot(yb_ref[...], wlb_ref[...])
        a_ref[...] = a
        b_ref[...] = b
        m_ref[...] = (_sigmoid(gc_ref[...]) * a + _sigmoid(gl_ref[...]) * b).astype(BF16)

    res = pl.BlockSpec((s, D_MODEL), lambda n: (0, 0))
    wcol = pl.BlockSpec((D_MODEL, CW), lambda n: (0, n))
    blk = lambda k: pl.BlockSpec((s, CW), lambda n, k=k: (0, k * nblk + n))
    out = pl.BlockSpec((s, CW), lambda n: (0, n))
    f = jax.ShapeDtypeStruct((s, D_MODEL), F32)
    return _call(
        body, name="branch_merge_fwd", grid=(nblk,),
        in_specs=[res, res, wcol, wcol, blk(5), blk(6)],
        out_specs=[out] * 3,
        out_shape=[f, f, jax.ShapeDtypeStruct((s, D_MODEL), BF16)],
        operands=(ya, yb, wcb, wlb, proj, proj), ride=ride)


def mix_out_fwd(merged, wout, x, g2, g3, ride=None):
    s, d = x.shape
    t = _token_tile(s)

    def body(m_ref, w_ref, x_ref, g2_ref, g3_ref, mix_ref, x2_ref, h2_ref, h2t_ref):
        mix = _dot(m_ref[...], w_ref[...])
        mix_ref[...] = mix
        n, _ = _rms_stats(mix)
        x2 = x_ref[...] + n * g2_ref[...]
        x2_ref[...] = x2
        n2, _ = _rms_stats(x2)
        h2 = n2 * g3_ref[...]
        h2_ref[...] = h2.astype(BF16)
        h2t_ref[...] = h2.T.astype(BF16)

    tile = pl.BlockSpec((t, d), lambda i: (i, 0))
    vec = pl.BlockSpec((1, d), lambda i: (0, 0))
    f = jax.ShapeDtypeStruct((s, d), F32)
    return _call(
        body, name="mix_out_fwd", grid=(s // t,),
        in_specs=[tile, pl.BlockSpec((d, d), lambda i: (0, 0)), tile, vec, vec],
        out_specs=[tile] * 3 + [pl.BlockSpec((d, t), lambda i: (0, i))],
        out_shape=[f, f, jax.ShapeDtypeStruct((s, d), BF16), jax.ShapeDtypeStruct((d, s), BF16)],
        operands=(merged, wout, x, g2, g3), ride=ride)


def ffn_up_act_fwd(h2, wup4, fw, fb, ride=None):
    s, k = h2.shape
    ns = wup4.shape[2]
    per_chip = ns // CW
    nblk = D_FF // CW

    def body(h_ref, wg_ref, wv_ref, cg_ref, cv_ref, bg_ref, bv_ref, up_ref, act_ref, f_ref):
        h = h_ref[...]
        ug = _dot(h, wg_ref[0])
        uv = _dot(h, wv_ref[0])
        up_ref[0] = ug
        up_ref[1] = uv
        gate = _causal_conv(ug, cg_ref, bg_ref[...])
        val = _causal_conv(uv, cv_ref, bv_ref[...])
        act_ref[0] = gate.astype(BF16)
        act_ref[1] = val.astype(BF16)
        f_ref[...] = (_gelu(gate) * val).astype(BF16)

    wcols = lambda h: pl.BlockSpec((1, k, CW), lambda n, h=h: (n // per_chip + 2 * h, 0, n % per_chip))
    half = lambda h, rows: pl.BlockSpec((rows, CW), lambda n, h=h: (0, h * nblk + n))
    both = pl.BlockSpec((2, s, CW), lambda n: (0, 0, n))
    return _call(
        body, name="ffn_up_act_fwd", grid=(nblk,),
        in_specs=[pl.BlockSpec((s, k), lambda n: (0, 0)), wcols(0), wcols(1),
                  half(0, 3), half(1, 3), half(0, 1), half(1, 1)],
        out_specs=[both, both, pl.BlockSpec((s, CW), lambda n: (0, n))],
        out_shape=[jax.ShapeDtypeStruct((2, s, D_FF), F32), jax.ShapeDtypeStruct((2, s, D_FF), BF16),
                   jax.ShapeDtypeStruct((s, D_FF), BF16)],
        operands=(h2, wup4, wup4, fw, fw, fb, fb), ride=ride)


def ffn_down_loss(f, wdown, x2, target, g4):
    s, d = x2.shape
    t = _token_tile(s)

    def body(f_ref, w_ref, x2_ref, tg_ref, g4_ref, dy_ref, dout_ref, loss_ref, dg4_ref):
        @pl.when(pl.program_id(0) == 0)
        def _():
            loss_ref[...] = jnp.zeros_like(loss_ref)
            dg4_ref[...] = jnp.zeros_like(dg4_ref)

        out = _dot(f_ref[...], w_ref[...])
        n, r = _rms_stats(out)
        err = x2_ref[...] + n * g4_ref[...] - tg_ref[...]
        loss_ref[...] += jnp.full(loss_ref.shape, (0.5 / d) * jnp.sum(err * err), F32)
        dy = err * (1.0 / d)
        dy_ref[...] = dy
        dout, dg = _rms_bwd(n, r, g4_ref[...], dy)
        dout_ref[...] = dout.astype(BF16)
        dg4_ref[...] += jnp.sum(dg, axis=0, keepdims=True)

    tile = pl.BlockSpec((t, d), lambda i: (i, 0))
    vec = pl.BlockSpec((1, d), lambda i: (0, 0))
    return pl.pallas_call(
        body, name="ffn_down_loss", grid=(s // t,),
        in_specs=[pl.BlockSpec((t, D_FF), lambda i: (i, 0)), pl.BlockSpec((D_FF, d), lambda i: (0, 0)), tile, tile, vec],
        out_specs=[tile, tile, pl.BlockSpec((1, 128), lambda i: (0, 0)), vec],
        out_shape=[jax.ShapeDtypeStruct((s, d), F32), jax.ShapeDtypeStruct((s, d), BF16),
                   jax.ShapeDtypeStruct((1, 128), F32), jax.ShapeDtypeStruct((1, d), F32)],
        compiler_params=_params(),
    )(f, wdown, x2, target, g4)


def ffn_up_bwd(dout, wdown, up, act, f, fw, wup4, h2t, ride=None):
    k, s = h2t.shape
    nblk = D_FF // FW
    per_chip = wup4.shape[2] // FW

    def body(do_ref, wd_ref, up_ref, act_ref, f_ref, cg_ref, cv_ref, wg_ref, wv_ref, h_ref,
             dh_ref, dwu_ref, dwd_ref, dw_ref, db_ref, dup_scr):
        @pl.when(pl.program_id(0) == 0)
        def _():
            dup_scr[...] = jnp.zeros_like(dup_scr)
            dh_ref[...] = jnp.zeros_like(dh_ref)

        do = do_ref[...]
        df = _dot_nt(do, wd_ref[...])
        dg = dup_scr[0]
        dv = dup_scr[1]
        ht = h_ref[...]
        dh_ref[...] += _dot_nt(dg, wg_ref[0]) + _dot_nt(dv, wv_ref[0])
        dwu_ref[0] = _dot(ht, dg).astype(BF16)
        dwu_ref[1] = _dot(ht, dv).astype(BF16)
        dwd_ref[...] = _dot_tn(f_ref[...], do).astype(BF16)
        val = act_ref[1].astype(F32)
        ge, dge = _gelu_and_grad(act_ref[0].astype(F32))
        dgate = _advances(df * val * dge, 3)
        dval = _advances(df * ge, 3)
        dw_ref[0] = _conv_wgrad(dgate, up_ref[0])
        dw_ref[1] = _conv_wgrad(dval, up_ref[1])
        db_ref[0] = jnp.sum(dgate[0], axis=0, keepdims=True)
        db_ref[1] = jnp.sum(dval[0], axis=0, keepdims=True)
        dup_scr[0] = _taps_sum(dgate, cg_ref).astype(BF16)
        dup_scr[1] = _taps_sum(dval, cv_ref).astype(BF16)

    cur = lambda n: jnp.minimum(n, nblk - 1)
    prev = lambda n: jnp.maximum(n - 1, 0)
    once = pl.Buffered(1)
    both = lambda rows: pl.BlockSpec((2, rows, FW), lambda n: (0, 0, cur(n)))
    taps = lambda h: pl.BlockSpec((3, FW), lambda n, h=h: (0, h * nblk + cur(n)))
    wcols = lambda h: pl.BlockSpec((1, k, FW), lambda n, h=h: (prev(n) // per_chip + 2 * h, 0, prev(n) % per_chip))
    return _call(
        body, name="ffn_up_bwd", grid=(nblk + 1,),
        in_specs=[pl.BlockSpec((s, D_MODEL), lambda n: (0, 0), pipeline_mode=once),
                  pl.BlockSpec((FW, D_MODEL), lambda n: (cur(n), 0)), both(s), both(s),
                  pl.BlockSpec((s, FW), lambda n: (0, cur(n))), taps(0), taps(1), wcols(0), wcols(1),
                  pl.BlockSpec((k, s), lambda n: (0, 0), pipeline_mode=once)],
        out_specs=[pl.BlockSpec((s, k), lambda n: (0, 0), pipeline_mode=once),
                   pl.BlockSpec((2, k, FW), lambda n: (0, 0, prev(n))),
                   pl.BlockSpec((FW, D_MODEL), lambda n: (cur(n), 0)), both(3), both(1)],
        out_shape=[jax.ShapeDtypeStruct((s, k), F32), jax.ShapeDtypeStruct((2, k, D_FF), BF16),
                   jax.ShapeDtypeStruct((D_FF, D_MODEL), BF16),
                   jax.ShapeDtypeStruct((2, 3, D_FF), F32), jax.ShapeDtypeStruct((2, 1, D_FF), F32)],
        scratch_shapes=[pltpu.VMEM((2, s, FW), BF16)],
        operands=(dout, wdown, up, act, f, fw, fw, wup4, wup4, h2t), ride=ride)


def matmul_cols_bwd(dy, other, name, wgrad, ride=None):
    m = dy[0].shape[1]
    if wgrad:
        k = other.shape[0]
        nj, nb = N_CHIPS, sum(d.shape[0] * d.shape[2] for d in dy) // (N_CHIPS * CW)
    else:
        nj, k, ns = other.shape
        nb = ns // CW
    per_seg = dy[0].shape[2] // CW
    first = [sum(d.shape[0] for d in dy[:i]) for i in range(len(dy))]

    def segment(j, b):
        return (j * nb + b) // per_seg, (j * nb + b) % per_seg

    def body(*refs):
        dy_refs, (o_ref, r_ref) = refs[:len(dy)], refs[len(dy):]
        seg, _ = segment(pl.program_id(0), pl.program_id(1))
        dyb = dy_refs[-1][0]
        for i in range(len(dy) - 2, -1, -1):
            dyb = jnp.where(seg < first[i + 1], dy_refs[i][0], dyb)
        if wgrad:
            r_ref[...] = _dot(o_ref[...], dyb).astype(BF16)
        else:
            @pl.when((pl.program_id(0) == 0) & (pl.program_id(1) == 0))
            def _():
                r_ref[...] = jnp.zeros_like(r_ref)

            r_ref[...] += _dot_nt(dyb, o_ref[0])

    def dy_spec(i):
        nseg = dy[i].shape[0]

        def index(j, b):
            seg, col = segment(j, b)
            local = seg - first[i]
            return (jnp.clip(local, 0, nseg - 1), 0,
                    jnp.where(local < 0, 0, jnp.where(local >= nseg, per_seg - 1, col)))

        return pl.BlockSpec((1, m, CW), index)

    if wgrad:
        other_spec = pl.BlockSpec((k, m), lambda j, b: (0, 0))
        out_spec = pl.BlockSpec((k, CW), lambda j, b: (0, j * nb + b))
        out_shape = jax.ShapeDtypeStruct((k, nj * nb * CW), BF16)
    else:
        other_spec = pl.BlockSpec((1, k, CW), lambda j, b: (j, 0, b))
        out_spec = pl.BlockSpec((m, k), lambda j, b: (0, 0))
        out_shape = jax.ShapeDtypeStruct((m, k), F32)
    return _call(
        body, name=name, grid=(nj, nb), in_specs=[dy_spec(i) for i in range(len(dy))] + [other_spec],
        out_specs=[out_spec], out_shape=[out_shape], operands=(*dy, other), ride=ride)


def norms_mid_bwd(dh2, x2, dy, mix, g3, g2, ride=None):
    s, d = x2.shape
    t = _token_tile(s)

    def body(dh2_ref, x2_ref, dy_ref, mix_ref, g3_ref, g2_ref, dx2_ref, dmix_ref, dg3_ref, dg2_ref):
        @pl.when(pl.program_id(0) == 0)
        def _():
            dg3_ref[...] = jnp.zeros_like(dg3_ref)
            dg2_ref[...] = jnp.zeros_like(dg2_ref)

        n3, r3 = _rms_stats(x2_ref[...])
        dx, dg3 = _rms_bwd(n3, r3, g3_ref[...], dh2_ref[...])
        dx2 = dy_ref[...] + dx
        dx2_ref[...] = dx2
        dg3_ref[...] += jnp.sum(dg3, axis=0, keepdims=True)
        n2, r2 = _rms_stats(mix_ref[...])
        dmix, dg2 = _rms_bwd(n2, r2, g2_ref[...], dx2)
        dmix_ref[...] = dmix.astype(BF16)
        dg2_ref[...] += jnp.sum(dg2, axis=0, keepdims=True)

    tile = pl.BlockSpec((t, d), lambda i: (i, 0))
    vec = pl.BlockSpec((1, d), lambda i: (0, 0))
    v = jax.ShapeDtypeStruct((1, d), F32)
    return _call(
        body, name="norms_mid_bwd", grid=(s // t,),
        in_specs=[tile, tile, tile, tile, vec, vec],
        out_specs=[tile, tile, vec, vec],
        out_shape=[jax.ShapeDtypeStruct((s, d), F32), jax.ShapeDtypeStruct((s, d), BF16), v, v],
        operands=(dh2, x2, dy, mix, g3, g2), ride=ride)


def mix_out_bwd(dmix, wout, merged, a, b, proj, ride=None):
    s = dmix.shape[0]
    nblk = D_MODEL // CW

    def body(dm_ref, w_ref, mg_ref, a_ref, b_ref, gc_ref, gl_ref, da_ref, db_ref, dw_ref, dg_ref):
        dm = dm_ref[...]
        dmerged = _dot_nt(dm, w_ref[...])
        dw_ref[...] = _dot_tn(mg_ref[...], dm).astype(BF16)
        sc = _sigmoid(gc_ref[...])
        sl = _sigmoid(gl_ref[...])
        da_ref[...] = (dmerged * sc).astype(BF16)
        db_ref[...] = (dmerged * sl).astype(BF16)
        dg_ref[0] = (dmerged * a_ref[...] * sc * (1.0 - sc)).astype(BF16)
        dg_ref[1] = (dmerged * b_ref[...] * sl * (1.0 - sl)).astype(BF16)

    res = pl.BlockSpec((s, D_MODEL), lambda n: (0, 0))
    rows = pl.BlockSpec((CW, D_MODEL), lambda n: (n, 0))
    col = pl.BlockSpec((s, CW), lambda n: (0, n))
    blk = lambda k: pl.BlockSpec((s, CW), lambda n, k=k: (0, k * nblk + n))
    hb = jax.ShapeDtypeStruct((s, D_MODEL), BF16)
    return _call(
        body, name="mix_out_bwd", grid=(nblk,),
        in_specs=[res, rows, col, col, col, blk(5), blk(6)],
        out_specs=[col, col, rows, pl.BlockSpec((2, s, CW), lambda n: (0, 0, n))],
        out_shape=[hb, hb, jax.ShapeDtypeStruct((D_MODEL, D_MODEL), BF16), jax.ShapeDtypeStruct((2, s, D_MODEL), BF16)],
        operands=(dmix, wout, merged, a, b, proj, proj), ride=ride)


def mix_conv_bwd(da, wcb, proj, q, ws, ride=None):
    s = da.shape[0]
    nblk = D_MODEL // CW

    def body(da_ref, w_ref, cb_ref, cc_ref, cx_ref, q_ref, ws_ref, dc_ref, dw_ref, dws_ref):
        dab = da_ref[...]
        dya = _dot_nt(dab, w_ref[...])
        cb = cb_ref[...]
        cc = cc_ref[...]
        cx = cx_ref[...]
        q = q_ref[...]
        dw_ref[...] = _dot_tn((cb * q).astype(BF16), dab).astype(BF16)
        dc_ref[0] = (dya * q).astype(BF16)
        dq = _advances(dya * cb, 3)
        dp = _taps_sum(dq, ws_ref)
        dws_ref[...] = _conv_wgrad(dq, cc * cx)
        dc_ref[1] = (dp * cx).astype(BF16)
        dc_ref[2] = (dp * cc).astype(BF16)

    res = pl.BlockSpec((s, D_MODEL), lambda n: (0, 0))
    rows = pl.BlockSpec((CW, D_MODEL), lambda n: (n, 0))
    col = pl.BlockSpec((s, CW), lambda n: (0, n))
    blk = lambda k: pl.BlockSpec((s, CW), lambda n, k=k: (0, k * nblk + n))
    taps = pl.BlockSpec((3, CW), lambda n: (0, n))
    hb = jax.ShapeDtypeStruct((s, D_MODEL), BF16)
    return _call(
        body, name="mix_conv_bwd", grid=(nblk,),
        in_specs=[res, rows, blk(0), blk(1), blk(2), col, taps],
        out_specs=[pl.BlockSpec((3, s, CW), lambda n: (0, 0, n)), rows, taps],
        out_shape=[jax.ShapeDtypeStruct((3, s, D_MODEL), BF16), jax.ShapeDtypeStruct((D_MODEL, D_MODEL), BF16),
                   jax.ShapeDtypeStruct((3, D_MODEL), F32)],
        operands=(da, wcb, proj, proj, proj, q, ws), ride=ride)


def mix_lru_bwd(db, wlb, proj, xl, r, i, h, wl, wa, wx, lam, ride=None):
    s = db.shape[0]
    nblk = D_MODEL // CW

    def body(db_ref, w_ref, lx_ref, ly_ref, xl_ref, r_ref, i_ref, h_ref, wl_ref, wa_ref, wx_ref, lam_ref,
             dl_ref, dw_ref, dwa_ref, dwx_ref, dba_ref, dbx_ref, dwl_ref, dbl_ref, dlam_ref,
             c_scr, g_scr):
        dbb = db_ref[...]
        dyb = _dot_nt(dbb, w_ref[...])
        h = h_ref[...]
        ge, dge = _gelu_and_grad(ly_ref[...])
        dw_ref[...] = _dot_tn((h * ge).astype(BF16), dbb).astype(BF16)
        dl_ref[1] = (dyb * h * dge).astype(BF16)
        r = r_ref[...].astype(F32)
        gi = i_ref[...].astype(F32)
        xlb = xl_ref[...]
        xl = xlb.astype(F32)
        lam = lam_ref[...]
        ls = _log_sigmoid(lam)
        a, mult = _lru_gates(r, ls)
        c_scr[...] = _shift_up(a, 1)
        g_scr[...] = dyb * ge
        _scan_backward(c_scr, g_scr, g_scr)
        du = g_scr[...]
        da = du * _shift_down(h, 1)
        dmult = du * gi * xl
        di = du * mult * xl
        dxl = du * mult * gi
        first = _rows(a.shape) == 0
        dlog_a = da * a - jnp.where(first, 0.0, dmult * a * a / mult)
        dr = dlog_a * (LRU_C * ls)
        dlam_ref[...] = jnp.sum(dlog_a * r, axis=0, keepdims=True) * (LRU_C * (1.0 - _sigmoid(lam)))
        dzr = dr * r * (1.0 - r)
        dzi = di * gi * (1.0 - gi)
        dba_ref[...] = jnp.sum(dzr, axis=0, keepdims=True)
        dbx_ref[...] = jnp.sum(dzi, axis=0, keepdims=True)
        dzrb = dzr.astype(BF16)
        dzib = dzi.astype(BF16)
        dwa_ref[0] = _dot_tn(xlb, dzrb)
        dwx_ref[0] = _dot_tn(xlb, dzib)
        dxl = _advances(dxl + _dot_nt(dzrb, wa_ref[0]) + _dot_nt(dzib, wx_ref[0]), 4)
        dl_ref[0] = _taps_sum(dxl, wl_ref).astype(BF16)
        dwl_ref[...] = _conv_wgrad(dxl, lx_ref[...])
        dbl_ref[...] = jnp.sum(dxl[0], axis=0, keepdims=True)

    res = pl.BlockSpec((s, D_MODEL), lambda n: (0, 0))
    rows = pl.BlockSpec((CW, D_MODEL), lambda n: (n, 0))
    col = pl.BlockSpec((s, CW), lambda n: (0, n))
    blk = lambda k: pl.BlockSpec((s, CW), lambda n, k=k: (0, k * nblk + n))
    taps = pl.BlockSpec((4, CW), lambda n: (0, n))
    vec = pl.BlockSpec((1, CW), lambda n: (0, n))
    mat = pl.BlockSpec((1, CW, CW), lambda n: (n, 0, 0))
    hb = jax.ShapeDtypeStruct((s, D_MODEL), BF16)
    v = jax.ShapeDtypeStruct((1, D_MODEL), F32)
    m = jax.ShapeDtypeStruct((LRU_HEADS, HEAD_DIM, HEAD_DIM), F32)
    scr = pltpu.VMEM((s, CW), F32)
    return _call(
        body, name="mix_lru_bwd", grid=(nblk,),
        in_specs=[res, rows, blk(3), blk(4), col, col, col, col, taps, mat, mat, vec],
        out_specs=[pl.BlockSpec((2, s, CW), lambda n: (0, 0, n)), rows, mat, mat, vec, vec, taps, vec, vec],
        out_shape=[jax.ShapeDtypeStruct((2, s, D_MODEL), BF16), jax.ShapeDtypeStruct((D_MODEL, D_MODEL), BF16), m, m, v, v,
                   jax.ShapeDtypeStruct((4, D_MODEL), F32), v, v],
        scratch_shapes=[scr, scr],
        operands=(db, wlb, proj, proj, xl, r, i, h, wl, wa, wx, lam), ride=ride)


def norm_in_bwd(dh1, x, dx2, g1, ride=None):
    s, d = x.shape
    t = _token_tile(s)

    def body(dh_ref, x_ref, dx2_ref, g_ref, dx_ref, dg_ref):
        @pl.when(pl.program_id(0) == 0)
        def _():
            dg_ref[...] = jnp.zeros_like(dg_ref)

        n, r = _rms_stats(x_ref[...])
        dx, dg = _rms_bwd(n, r, g_ref[...], dh_ref[...])
        dx_ref[...] = dx2_ref[...] + dx
        dg_ref[...] += jnp.sum(dg, axis=0, keepdims=True)

    tile = pl.BlockSpec((t, d), lambda i: (i, 0))
    vec = pl.BlockSpec((1, d), lambda i: (0, 0))
    return _call(
        body, name="norm_in_bwd", grid=(s // t,),
        in_specs=[tile, tile, tile, vec],
        out_specs=[tile, vec],
        out_shape=[jax.ShapeDtypeStruct((s, d), F32), jax.ShapeDtypeStruct((1, d), F32)],
        operands=(dh1, x, dx2, g1), ride=ride)


def _owned_part(ref, kind, k, h, hr):
    if kind == "col":
        ns = ref.shape[1] // N_CHIPS
        return ref.at[pl.ds(h * hr, hr), pl.ds(k * ns, ns)]
    if kind == "row":
        return ref.at[pl.ds(k * 2 * hr + h * hr, hr), :]
    if kind == "col2":
        ns = ref.shape[2] // 2
        return ref.at[k // 2, pl.ds(h * hr, hr), pl.ds((k % 2) * ns, ns)]
    return ref.at[k, pl.ds(h * hr, hr), :]


def _part_shape(g, kind):
    if kind == "col2":
        return g.shape[1] // 2, g.shape[2] // 2
    if kind == "col":
        return g.shape[0] // 2, g.shape[1] // N_CHIPS
    if kind == "row":
        return g.shape[0] // (2 * N_CHIPS), g.shape[1]
    return g.shape[1] // 2, g.shape[2]


def pair_split(grads, kinds, name):
    n = len(grads)
    shapes = [_part_shape(g, k) for g, k in zip(grads, kinds)]

    def body(*refs):
        ins, theirs = refs[:n], refs[n:2 * n]
        send_sem, recv_sem = refs[2 * n:]
        x, y, c = _position()
        copies = []
        for a in range(n):
            hr = shapes[a][0]
            for k in range(N_CHIPS):
                s = a * N_CHIPS + k
                copies.append(pltpu.make_async_remote_copy(
                    src_ref=_owned_part(ins[a], kinds[a], k, 1 - c, hr), dst_ref=theirs[a].at[k],
                    send_sem=send_sem.at[s], recv_sem=recv_sem.at[s], device_id=(x, y, 1 - c), device_id_type=MESH))
        _handshake([(x, y, 1 - c)])
        for cp in copies:
            cp.start()
        for cp in copies:
            cp.wait()

    return pl.pallas_call(
        body, name=name,
        in_specs=[_HBM] * n, out_specs=[_HBM] * n,
        out_shape=[jax.ShapeDtypeStruct((N_CHIPS,) + shp, g.dtype) for shp, g in zip(shapes, grads)],
        scratch_shapes=[pltpu.SemaphoreType.DMA((n * N_CHIPS,))] * 2,
        compiler_params=pltpu.CompilerParams(collective_id=SIBLING),
    )(*grads)


def pair_swap(halves):
    n = len(halves)

    def body(*refs):
        ins, outs = refs[:n], refs[n:2 * n]
        send_sem, recv_sem = refs[2 * n:]
        x, y, c = _position()
        copies = [pltpu.make_async_remote_copy(
            src_ref=ins[a], dst_ref=outs[a], send_sem=send_sem.at[a], recv_sem=recv_sem.at[a],
            device_id=(x, y, 1 - c), device_id_type=MESH) for a in range(n)]
        _handshake([(x, y, 1 - c)])
        for cp in copies:
            cp.start()
        for cp in copies:
            cp.wait()

    return pl.pallas_call(
        body, name="pair_swap",
        in_specs=[_HBM] * n, out_specs=[_HBM] * n,
        out_shape=[jax.ShapeDtypeStruct(h.shape, h.dtype) for h in halves],
        scratch_shapes=[pltpu.SemaphoreType.DMA((n,))] * 2,
        compiler_params=pltpu.CompilerParams(collective_id=SIBLING),
    )(*halves)


def _row_tile(rows, cols, limit_bytes=1 << 20):
    best = None
    for t in range(SUBLANES, rows + 1, SUBLANES):
        if rows % t == 0 and t * cols * 4 <= limit_bytes:
            best = t
    return best or rows


def add_pair(g, kind, theirs, core, name):
    nc, rows, cols = theirs.shape
    t = _row_tile(rows, cols, 4 << 20)
    nt = rows // t

    def body(core_ref, g_ref, b_ref, o_ref):
        mine = g_ref[...].reshape(t, cols)
        o_ref[0] = (mine.astype(F32) + b_ref[0].astype(F32)).astype(o_ref.dtype)

    if kind == "col":
        own = pl.BlockSpec((t, cols), lambda k, i, c: (c[0] * nt + i, k))
    elif kind == "col2":
        own = pl.BlockSpec((1, t, cols), lambda k, i, c: (k // 2, c[0] * nt + i, k % 2))
    elif kind == "row":
        own = pl.BlockSpec((t, cols), lambda k, i, c: ((2 * k + c[0]) * nt + i, 0))
    else:
        own = pl.BlockSpec((1, t, cols), lambda k, i, c: (k, c[0] * nt + i, 0))
    spec = pl.BlockSpec((1, t, cols), lambda k, i, c: (k, i, 0))
    return pl.pallas_call(
        body, name=name,
        grid_spec=pltpu.PrefetchScalarGridSpec(num_scalar_prefetch=1, grid=(nc, nt), in_specs=[own, spec], out_specs=spec),
        out_shape=jax.ShapeDtypeStruct(theirs.shape, theirs.dtype), compiler_params=_params(),
    )(core, g, theirs)


def sum_lead(a, name):
    nl, rows, cols = a.shape
    t = _row_tile(rows, cols, (1 << 20) // 2)

    def body(a_ref, o_ref):
        acc = a_ref[0].astype(F32)
        for s in range(1, nl):
            acc = acc + a_ref[s].astype(F32)
        o_ref[...] = acc

    return pl.pallas_call(
        body, name=name, grid=(rows // t,),
        in_specs=[pl.BlockSpec((nl, t, cols), lambda i: (0, i, 0))],
        out_specs=pl.BlockSpec((t, cols), lambda i: (i, 0)),
        out_shape=jax.ShapeDtypeStruct((rows, cols), F32), compiler_params=_params(),
    )(a)


def sum_chips(rx, csum, chip, name):
    nc, rows, cols = rx.shape
    t = _row_tile(rows, cols, 2 << 20)

    def body(chip_ref, r0, r1, r2, r3, own_ref, o_ref):
        acc = None
        for s, ref in enumerate((r0, r1, r2, r3)):
            term = jnp.where(chip_ref[0] == s, own_ref[0], ref[0]).astype(F32)
            acc = term if acc is None else acc + term
        o_ref[...] = acc

    def slot(s):
        return pl.BlockSpec((1, t, cols), lambda i, c, s=s: (jnp.where(c[0] == s, c[0] ^ 1, s), i, 0))

    return pl.pallas_call(
        body, name=name,
        grid_spec=pltpu.PrefetchScalarGridSpec(
            num_scalar_prefetch=1, grid=(rows // t,),
            in_specs=[slot(s) for s in range(nc)] + [pl.BlockSpec((1, t, cols), lambda i, c: (c[0], i, 0))],
            out_specs=pl.BlockSpec((t, cols), lambda i, c: (i, 0))),
        out_shape=jax.ShapeDtypeStruct((rows, cols), F32), compiler_params=_params(),
    )(chip, rx, rx, rx, rx, csum)


def cast_bf16(a, name):
    rows, cols = a.shape
    t = _row_tile(rows, cols, 2 << 20)

    def body(i_ref, o_ref):
        o_ref[...] = i_ref[...].astype(BF16)

    spec = pl.BlockSpec((t, cols), lambda i: (i, 0))
    return pl.pallas_call(body, name=name, grid=(rows // t,), in_specs=[spec], out_specs=spec,
                          out_shape=jax.ShapeDtypeStruct((rows, cols), BF16), compiler_params=_params())(a)


def _adamw_update(w, g, m, v):
    nm = ADAM_B1 * m + (1.0 - ADAM_B1) * g
    nv = ADAM_B2 * v + (1.0 - ADAM_B2) * (g * g)
    m_hat = nm * (1.0 / (1.0 - ADAM_B1 ** ADAM_STEP))
    v_hat = nv * (1.0 / (1.0 - ADAM_B2 ** ADAM_STEP))
    return -ADAM_LR * (m_hat / (jnp.sqrt(v_hat) + ADAM_EPS) + ADAM_WD * w), nm, nv


def adamw(w, g, m, v, name):
    rows, cols = w.shape
    t = _row_tile(rows, cols)

    def body(w_ref, g_ref, m_ref, v_ref, d_ref, nm_ref, nv_ref):
        d_ref[...], nm_ref[...], nv_ref[...] = _adamw_update(w_ref[...], g_ref[...], m_ref[...], v_ref[...])

    spec = pl.BlockSpec((t, cols), lambda i: (i, 0))
    shp = jax.ShapeDtypeStruct((rows, cols), F32)
    return pl.pallas_call(
        body, name=name, grid=(rows // t,), in_specs=[spec] * 4, out_specs=[spec] * 3,
        out_shape=[shp, shp, shp], compiler_params=_params(),
    )(w, g, m, v)


def adamw_halves(w, g_mine, g_other, m, v, core, name):
    rows, cols = w.shape
    hr = rows // 2
    t = _row_tile(hr, cols)
    nt = hr // t

    def body(core_ref, w_ref, gm_ref, go_ref, m_ref, v_ref, g_ref, d_ref, nm_ref, nv_ref):
        g = jnp.where(pl.program_id(0) // nt == core_ref[0], gm_ref[...], go_ref[...])
        g_ref[...] = g
        d_ref[...], nm_ref[...], nv_ref[...] = _adamw_update(w_ref[...], g, m_ref[...], v_ref[...])

    spec = pl.BlockSpec((t, cols), lambda i, c: (i, 0))
    half = pl.BlockSpec((t, cols), lambda i, c: (i % nt, 0))
    shp = jax.ShapeDtypeStruct((rows, cols), F32)
    return pl.pallas_call(
        body, name=name,
        grid_spec=pltpu.PrefetchScalarGridSpec(num_scalar_prefetch=1, grid=(2 * nt,),
                                               in_specs=[spec, half, half, spec, spec], out_specs=[spec] * 4),
        out_shape=[shp] * 4, compiler_params=_params(),
    )(core, w, g_mine, g_other, m, v)


WEIGHTS = ("norm_mix_pre", "norm_mix_post", "norm_ffn_pre", "norm_ffn_post", "w_in", "conv_short_w",
           "w_conv_branch", "lru_conv_w", "lru_conv_b", "lru_wa", "lru_ba", "lru_wx", "lru_bx", "lru_lambda",
           "w_lru_branch", "w_out", "ffn_w_up", "ffn_conv_w", "ffn_conv_b", "ffn_w_down")
BIG = ("w_in", "ffn_w_up", "w_conv_branch", "w_lru_branch", "w_out", "ffn_w_down")
BIG_KIND = ("col", "col", "row", "row", "row", "row")
SMALL = ("conv_short_w", "lru_conv_w", "lru_wa", "lru_ba", "lru_wx", "lru_bx", "ffn_conv_w")
REPL = ("norm_mix_pre", "norm_mix_post", "norm_ffn_pre", "norm_ffn_post", "lru_conv_b", "lru_lambda", "ffn_conv_b")
PACK_W = 256
SMALL_ROWS = 576
REPL_ROWS = 16
LOSS_ROW = 12
FFN_SHARD = 2 * D_FF // N_CHIPS
QUARTER = HEAD_DIM // N_CHIPS
SMALL_PARTS = (("conv_short_w", 3, (1, 3, PACK_W)), ("lru_conv_w", 4, (1, 4, PACK_W)),
               ("lru_wa", LRU_HEADS * QUARTER, (1, LRU_HEADS, QUARTER, HEAD_DIM)), ("lru_ba", LRU_HEADS, (1, LRU_HEADS, QUARTER)),
               ("lru_wx", LRU_HEADS * QUARTER, (1, LRU_HEADS, QUARTER, HEAD_DIM)), ("lru_bx", LRU_HEADS, (1, LRU_HEADS, QUARTER)),
               ("ffn_conv_w", 3 * FFN_SHARD // PACK_W, (1, 3, FFN_SHARD)))


def _pad8(nr):
    return -(-nr // SUBLANES) * SUBLANES


SMALL_OFFSET = {}
for _name, _nr, _ in SMALL_PARTS:
    SMALL_OFFSET[_name] = sum(_pad8(nr) for n, nr, _ in SMALL_PARTS[:len(SMALL_OFFSET)])
FFN_ROWS = FFN_SHARD // PACK_W
BIASES = ("lru_ba", "lru_bx")
TAPS3 = ("conv_short_w", "ffn_conv_w")


def pack_small(dicts):
    names = [n for n, _, _ in SMALL_PARTS]
    operands = [d[n].transpose(1, 0, 2) if n in TAPS3 else d[n] for d in dicts for n in names]

    def body(*refs):
        ins, outs = refs[:len(operands)], refs[len(operands):]
        for i, o in enumerate(outs):
            o[...] = jnp.zeros_like(o)
            for (name, nr, shape), p in zip(SMALL_PARTS, ins[i * len(names):(i + 1) * len(names)]):
                r0 = SMALL_OFFSET[name]
                if name in BIASES:
                    o[r0:r0 + nr, 0:QUARTER] = p[0]
                elif name == "ffn_conv_w":
                    for k in range(shape[1]):
                        for s in range(FFN_ROWS):
                            o[r0 + FFN_ROWS * k + s:r0 + FFN_ROWS * k + s + 1, :] = p[k, :, s * PACK_W:(s + 1) * PACK_W]
                elif name == "conv_short_w":
                    for k in range(nr):
                        o[r0 + k:r0 + k + 1, :] = p[k]
                else:
                    o[r0:r0 + nr, :] = p[0].reshape(nr, PACK_W)

    shape = jax.ShapeDtypeStruct((SMALL_ROWS, PACK_W), F32)
    return pl.pallas_call(body, name="pack_small", out_shape=[shape] * len(dicts), compiler_params=_params())(*operands)


def full_small(g4):
    def body(p, csw, lcw, wa, wx, fcw):
        chips = range(N_CHIPS)
        r0 = SMALL_OFFSET["conv_short_w"]
        csw[...] = jnp.concatenate([p[c, r0:r0 + 3, :] for c in chips], axis=1)
        r0 = SMALL_OFFSET["lru_conv_w"]
        lcw[...] = jnp.concatenate([p[c, r0:r0 + 4, :] for c in chips], axis=1)
        for name, o in (("lru_wa", wa), ("lru_wx", wx)):
            r0 = SMALL_OFFSET[name]
            for h in range(LRU_HEADS):
                for c in chips:
                    o[h, c * QUARTER:(c + 1) * QUARTER, :] = p[c, r0 + h * QUARTER:r0 + (h + 1) * QUARTER, :].astype(BF16)
        r0 = SMALL_OFFSET["ffn_conv_w"]
        for k in range(3):
            fcw[k:k + 1, :] = jnp.concatenate(
                [p[c, r0 + FFN_ROWS * k + s:r0 + FFN_ROWS * k + s + 1, :] for c in chips for s in range(FFN_ROWS)], axis=1)

    mat = jax.ShapeDtypeStruct((LRU_HEADS, HEAD_DIM, HEAD_DIM), BF16)
    csw, lcw, wa, wx, fcw = pl.pallas_call(
        body, name="full_small",
        out_shape=[jax.ShapeDtypeStruct((3, D_MODEL), F32), jax.ShapeDtypeStruct((4, D_MODEL), F32), mat, mat,
                   jax.ShapeDtypeStruct((3, 2 * D_FF), F32)],
        compiler_params=_params())(g4)

    def bias(name):
        r0 = SMALL_OFFSET[name]
        return g4[:, r0:r0 + LRU_HEADS, :QUARTER].transpose(1, 0, 2).reshape(1, D_MODEL)

    return dict(conv_short_w=csw, lru_conv_w=lcw, lru_wa=wa, lru_wx=wx, ffn_conv_w=fcw,
                lru_ba=bias("lru_ba"), lru_bx=bias("lru_bx"))


def split_small(full):
    def bias(name):
        return full[name].reshape(LRU_HEADS, N_CHIPS, QUARTER).transpose(1, 0, 2)

    def body(csw, lcw, wa, wx, fcw, ba, bx, o):
        o[...] = jnp.zeros_like(o)
        for c in range(N_CHIPS):
            cols = slice(c * PACK_W, (c + 1) * PACK_W)
            r0 = SMALL_OFFSET["conv_short_w"]
            o[c, r0:r0 + 3, :] = csw[:, cols]
            r0 = SMALL_OFFSET["lru_conv_w"]
            o[c, r0:r0 + 4, :] = lcw[:, cols]
            for name, p in (("lru_wa", wa), ("lru_wx", wx)):
                r0 = SMALL_OFFSET[name]
                for h in range(LRU_HEADS):
                    o[c, r0 + h * QUARTER:r0 + (h + 1) * QUARTER, :] = p[h, c * QUARTER:(c + 1) * QUARTER, :]
            for name, p in (("lru_ba", ba), ("lru_bx", bx)):
                r0 = SMALL_OFFSET[name]
                o[c, r0:r0 + LRU_HEADS, 0:QUARTER] = p[c]
            r0 = SMALL_OFFSET["ffn_conv_w"]
            for k in range(3):
                for s in range(FFN_ROWS):
                    lo = c * FFN_SHARD + s * PACK_W
                    o[c, r0 + FFN_ROWS * k + s:r0 + FFN_ROWS * k + s + 1, :] = fcw[k:k + 1, lo:lo + PACK_W]

    return pl.pallas_call(
        body, name="split_small", out_shape=jax.ShapeDtypeStruct((N_CHIPS, SMALL_ROWS, PACK_W), F32),
        compiler_params=_params(),
    )(full["conv_short_w"], full["lru_conv_w"], full["lru_wa"], full["lru_wx"], full["ffn_conv_w"],
      bias("lru_ba"), bias("lru_bx"))


def pack_repl(dicts, loss=None):
    operands = [d[n] for d in dicts for n in REPL] + ([loss] if loss is not None else [])

    def body(*refs):
        ins, outs = refs[:len(operands)], refs[len(operands):]
        for i, o in enumerate(outs):
            o[...] = jnp.zeros_like(o)
            r0 = 0
            for p in ins[i * len(REPL):(i + 1) * len(REPL)]:
                for s in range(p.shape[1] // D_MODEL):
                    o[r0:r0 + 1, :] = p[:, s * D_MODEL:(s + 1) * D_MODEL]
                    r0 += 1
        if loss is not None:
            outs[-1][LOSS_ROW:LOSS_ROW + 1, :] = jnp.tile(ins[-1][...], (1, D_MODEL // 128))

    shape = jax.ShapeDtypeStruct((REPL_ROWS, D_MODEL), F32)
    return pl.pallas_call(body, name="pack_repl" + ("_loss" if loss is not None else ""),
                          out_shape=[shape] * len(dicts), compiler_params=_params())(*operands)


def _lane_concat(ref, r0, n):
    return jnp.concatenate([ref[r0 + s:r0 + s + 1, :] for s in range(n)], axis=1)


def unpack_small(packs):
    names = [n for n, _, _ in SMALL_PARTS]

    def body(*refs):
        ins, outs = refs[:len(packs)], refs[len(packs):]
        for i, p in enumerate(ins):
            for (name, nr, shape), o in zip(SMALL_PARTS, outs[i * len(names):(i + 1) * len(names)]):
                r0 = SMALL_OFFSET[name]
                if name in BIASES:
                    o[0] = p[r0:r0 + nr, 0:QUARTER]
                elif name == "ffn_conv_w":
                    for k in range(shape[1]):
                        o[k] = _lane_concat(p, r0 + FFN_ROWS * k, FFN_ROWS)
                elif name == "conv_short_w":
                    for k in range(nr):
                        o[k] = p[r0 + k:r0 + k + 1, :]
                else:
                    o[0] = p[r0:r0 + nr, :].reshape(shape[1:])

    shapes = [jax.ShapeDtypeStruct((s[1], 1, s[2]) if n in TAPS3 else s, F32) for n, _, s in SMALL_PARTS]
    res = pl.pallas_call(body, name="unpack_small", out_shape=shapes * len(packs), compiler_params=_params())(*packs)
    out = []
    for i in range(len(packs)):
        d = dict(zip(names, res[i * len(names):(i + 1) * len(names)]))
        for n in TAPS3:
            d[n] = d[n].transpose(1, 0, 2)
        out.append(d)
    return out


def unpack_repl(packs):
    rows = [(2 * D_FF // D_MODEL) if n == "ffn_conv_b" else 1 for n in REPL]

    def body(*refs):
        ins, outs = refs[:len(packs)], refs[len(packs):]
        for i, p in enumerate(ins):
            r0 = 0
            for nr, o in zip(rows, outs[i * len(REPL):(i + 1) * len(REPL)]):
                o[...] = _lane_concat(p, r0, nr)
                r0 += nr

    shapes = [jax.ShapeDtypeStruct((1, nr * D_MODEL), F32) for nr in rows]
    res = pl.pallas_call(body, name="unpack_repl", out_shape=shapes * len(packs), compiler_params=_params())(*packs)
    return [dict(zip(REPL, res[i * len(REPL):(i + 1) * len(REPL)])) for i in range(len(packs))]


def kernel(x, norm_mix_pre, norm_mix_post, norm_ffn_pre, norm_ffn_post, w_in, conv_short_w, w_conv_branch, lru_conv_w, lru_conv_b, lru_wa, lru_ba, lru_wx, lru_bx, lru_lambda, w_lru_branch, w_out, ffn_w_up, ffn_conv_w, ffn_conv_b, ffn_w_down, loss_target, m_norm_mix_pre, m_norm_mix_post, m_norm_ffn_pre, m_norm_ffn_post, m_w_in, m_conv_short_w, m_w_conv_branch, m_lru_conv_w, m_lru_conv_b, m_lru_wa, m_lru_ba, m_lru_wx, m_lru_bx, m_lru_lambda, m_w_lru_branch, m_w_out, m_ffn_w_up, m_ffn_conv_w, m_ffn_conv_b, m_ffn_w_down, v_norm_mix_pre, v_norm_mix_post, v_norm_ffn_pre, v_norm_ffn_post, v_w_in, v_conv_short_w, v_w_conv_branch, v_lru_conv_w, v_lru_conv_b, v_lru_wa, v_lru_ba, v_lru_wx, v_lru_bx, v_lru_lambda, v_w_lru_branch, v_w_out, v_ffn_w_up, v_ffn_conv_w, v_ffn_conv_b, v_ffn_w_down):
    given = dict(locals())
    w = {n: given[n] for n in WEIGHTS}
    m = {n: given["m_" + n] for n in WEIGHTS}
    v = {n: given["v_" + n] for n in WEIGHTS}

    xi, yi, ci = _position()
    chip_i = 2 * xi + yi
    chip = chip_i.astype(jnp.int32).reshape(1)
    core = ci.astype(jnp.int32).reshape(1)
    xs, target = x[0], loss_target[0]
    g1, g2, g3, g4 = w["norm_mix_pre"], w["norm_mix_post"], w["norm_ffn_pre"], w["norm_ffn_post"]
    shard = {n: cast_bf16(w[n][0], "cast_" + n) for n in BIG}
    small_shard, m_small, v_small = pack_small([w, m, v])

    def gathered(bufs, names):
        return [_own_slot(b, small_shard if n == "small" else shard[n], chip_i) for b, n in zip(bufs, names)]

    def chip_sums(arrays, kinds, tag):
        theirs = pair_split(arrays, kinds, "pair_split_" + tag)
        return [add_pair(g, k, t, core, "pair_add_%s_%d" % (tag, i)) for i, (g, k, t) in enumerate(zip(arrays, kinds, theirs))]

    h1, h1t = norm_in(xs, g1)
    win4, small4 = gathered(run_ride(gather_ride([shard["w_in"], small_shard]), "gather_first"), ("w_in", "small"))
    small = full_small(small4)
    first_up = 256
    (proj,), got = matmul_cols(
        h1, win4, "proj_fwd",
        ride=gather_ride([shard["w_conv_branch"], shard["w_lru_branch"], shard["w_out"], shard["ffn_w_up"]],
                         items=[(0, 0, 256), (1, 0, 256), (2, 0, 256), (3, 0, first_up)]))
    wcb, wlb, wout = [g.reshape(-1, D_MODEL) for g in gathered(got[:3], ("w_conv_branch", "w_lru_branch", "w_out"))]
    got = got[3:]
    up_piece = lambda r0, nr, into=None: gather_ride([shard["ffn_w_up"]], items=[(0, r0, nr)], into=into)
    down_piece = lambda r0, nr, into=None: gather_ride([shard["ffn_w_down"]], items=[(0, r0, nr)], into=into)
    q, ya = mix_conv_fwd(proj, small["conv_short_w"])
    (xl, r, gi, h, yb), got = mix_lru_fwd(
        proj, small["lru_conv_w"], w["lru_conv_b"], small["lru_wa"], small["lru_ba"],
        small["lru_wx"], small["lru_bx"], w["lru_lambda"], ride=up_piece(first_up, 512, got))
    (a, b, merged), got = branch_merge_fwd(ya, yb, wcb, wlb, proj, ride=up_piece(first_up + 512, 256, got))
    (wup4,) = gathered(got, ("ffn_w_up",))
    (mix, x2, h2, h2t), got = mix_out_fwd(merged, wout, xs, g2, g3, ride=down_piece(0, 256))
    (up, act, f), got = ffn_up_act_fwd(h2, wup4, small["ffn_conv_w"], w["ffn_conv_b"], ride=down_piece(256, 512, got))
    wdown = gathered(got, ("ffn_w_down",))[0].reshape(-1, D_MODEL)
    dy, dout, loss, dg4 = ffn_down_loss(f, wdown, x2, target, g4)

    dh2, dwup, dwdown, dfw, dfb = ffn_up_bwd(dout, wdown, up, act, f, small["ffn_conv_w"], wup4, h2t)
    cs_down, cs_up = chip_sums([dwdown, dwup], ["row", "col2"], "ffn")
    down_rows = lambda r0, nr, into=None: exchange_ride([cs_down], items=[(0, r0, nr)], into=into)
    up_rows = lambda r0, nr, into=None: exchange_ride([cs_up], items=[(0, r0, nr)], into=into)
    (dx2, dmix, dg3, dg2), rx_down = norms_mid_bwd(dh2, x2, dy, mix, g3, g2, ride=down_rows(0, 128))
    (da, db, dwout, dgates), rx_down = mix_out_bwd(dmix, wout, merged, a, b, proj, ride=down_rows(128, 256, rx_down))
    (dconv, dwcb, dws), rx_up = mix_conv_bwd(da, wcb, proj, q, small["conv_short_w"], ride=up_rows(0, 176))
    cs_mid = chip_sums([dwout, dwcb], ["row", "row"], "mid")
    (dlru, dwlb, dwa, dwx, dba, dbx, dwl, dbl, dlam), rx_up = mix_lru_bwd(
        db, wlb, proj, xl, r, gi, h, small["lru_conv_w"], small["lru_wa"], small["lru_wx"], w["lru_lambda"],
        ride=up_rows(176, 336, rx_up))
    grads = dict(norm_mix_post=dg2, norm_ffn_pre=dg3, norm_ffn_post=dg4, conv_short_w=dws, lru_conv_w=dwl,
                 lru_conv_b=dbl, lru_wa=dwa, lru_ba=dba, lru_wx=dwx, lru_bx=dbx, lru_lambda=dlam,
                 ffn_conv_w=jnp.concatenate([dfw[0], dfw[1]], axis=1), ffn_conv_b=jnp.concatenate([dfb[0], dfb[1]], axis=1))
    cs_late = chip_sums([dwlb, split_small(grads)], ["row", "lead"], "late")
    dproj = [dconv, dlru, dgates]
    (dwin,), rx_all = matmul_cols_bwd(dproj, h1t, "proj_wgrad", True, ride=exchange_ride(cs_mid + cs_late))
    rx_mid, rx_late = rx_all[:2], rx_all[2:]
    cs_in = chip_sums([dwin], ["col"], "in")
    in_rows = lambda r0, nr, into=None: exchange_ride(cs_in, items=[(0, r0, nr)], into=into)
    (dh1,), rx_in = matmul_cols_bwd(dproj, win4, "proj_dgrad", False, ride=in_rows(0, 384))
    dx, grads["norm_mix_pre"] = norm_in_bwd(dh1, xs, dx2, g1)
    (rep_part,) = pack_repl([grads], loss)
    rx_in, rep_all = run_ride(exchange_ride(cs_in, items=[(0, 384, 128)], into=rx_in, rep=rep_part), "exchange_last")

    order = (("w_in", cs_in[0], rx_in), ("ffn_w_up", cs_up, rx_up[0]), ("w_conv_branch", cs_mid[1], rx_mid[1]),
             ("w_lru_branch", cs_late[0], rx_late[0]), ("w_out", cs_mid[0], rx_mid[0]),
             ("ffn_w_down", cs_down, rx_down[0]), ("small", cs_late[1], rx_late[1]))
    halves = [sum_chips(rx, cs, chip, "chip_sum_" + n) for n, cs, rx in order]
    me = 4 * xi + 2 * yi + ci
    rep_grad = sum_lead(_own_slot(rep_all, rep_part, me), "device_sum")
    others = pair_swap(halves)

    g_out, d_out, m_out, v_out = {}, {}, {}, {}
    for n, gm, go in zip(BIG, halves[:-1], others[:-1]):
        g, d, nm, nv = adamw_halves(w[n][0], gm, go, m[n][0], v[n][0], core, "adamw_" + n)
        g_out[n], d_out[n], m_out[n], v_out[n] = g[None], d[None], nm[None], nv[None]
    bufs = adamw_halves(small_shard, halves[-1], others[-1], m_small, v_small, core, "adamw_small")
    for dst, part in zip((g_out, d_out, m_out, v_out), unpack_small(bufs)):
        dst.update(part)
    w_rep, m_rep, v_rep = pack_repl([w, m, v])
    d, nm, nv = adamw(w_rep, rep_grad, m_rep, v_rep, "adamw_repl")
    for dst, part in zip((g_out, d_out, m_out, v_out), unpack_repl([rep_grad, d, nm, nv])):
        dst.update(part)

    return (rep_grad[LOSS_ROW, 0], dx[None], *[g_out[n] for n in WEIGHTS], *[d_out[n] for n in WEIGHTS],
            *[m_out[n] for n in WEIGHTS], *[v_out[n] for n in WEIGHTS])
```

```python
import functools
import math

import jax
import jax.numpy as jnp
from jax import lax
from jax.experimental import pallas as pl
from jax.experimental.pallas import tpu as pltpu

F32 = jnp.float32
BF16 = jnp.bfloat16

D_MODEL = 1024
N_CHIPS = 4
N_SEG = 7
D_FF = 3 * D_MODEL
LRU_HEADS = 4
HEAD_DIM = D_MODEL // LRU_HEADS
LRU_C = 8.0
RMS_EPS = 1e-6
CW = 256
FW = 256
UP_W = 512
SUBLANES = 8
SCAN_UNROLL = 8
VMEM_LIMIT = 58 * 1024 * 1024

ADAM_LR = 0.001
ADAM_B1 = 0.9
ADAM_B2 = 0.999
ADAM_EPS = 1e-08
ADAM_WD = 0.01
ADAM_STEP = 10

_GELU_C = math.sqrt(2.0 / math.pi)
_GELU_K = 0.044715


def _params(**kw):
    return pltpu.CompilerParams(vmem_limit_bytes=VMEM_LIMIT, **kw)


def _sigmoid(x):
    return 1.0 / (1.0 + jnp.exp(-x))


def _gelu(x):
    t = jnp.tanh(_GELU_C * (x + _GELU_K * x * x * x))
    return 0.5 * x * (1.0 + t)


def _gelu_and_grad(x):
    x2 = x * x
    t = jnp.tanh(_GELU_C * (x + _GELU_K * x * x2))
    g = 0.5 * x * (1.0 + t)
    dg = 0.5 * (1.0 + t) + 0.5 * x * (1.0 - t * t) * _GELU_C * (1.0 + 3.0 * _GELU_K * x2)
    return g, dg


def _log_sigmoid(x):
    e = jnp.exp(-jnp.abs(x))
    u = 1.0 + e
    l1p = jnp.where(u == 1.0, e, jnp.log(u) * e / (u - 1.0))
    return jnp.minimum(x, 0.0) - l1p


def _neg_expm1(z):
    series = -z * (1.0 + z * (0.5 + z * (1.0 / 6.0 + z * (1.0 / 24.0 + z * (1.0 / 120.0 + z * (1.0 / 720.0))))))
    return jnp.where(z > -0.2, series, 1.0 - jnp.exp(z))


def _rows(shape):
    return lax.broadcasted_iota(jnp.int32, shape, 0)


def _shift_down(x, k):
    return jnp.where(_rows(x.shape) >= k, pltpu.roll(x, k, 0), 0.0)


def _shift_up(x, k):
    n = x.shape[0]
    return jnp.where(_rows(x.shape) < n - k, pltpu.roll(x, n - k, 0), 0.0)


def _delays(x, k_width):
    return [x] + [_shift_down(x, j) for j in range(1, k_width)]


def _advances(dy, k_width):
    return [dy] + [_shift_up(dy, j) for j in range(1, k_width)]


def _taps_sum(shifted, w_ref, b=None):
    k_width = w_ref.shape[0]
    y = w_ref[k_width - 1:k_width, :] * shifted[0]
    for j in range(1, k_width):
        y = y + w_ref[k_width - 1 - j:k_width - j, :] * shifted[j]
    if b is not None:
        y = y + b
    return y


def _causal_conv(x, w_ref, b=None):
    return _taps_sum(_delays(x, w_ref.shape[0]), w_ref, b)


def _conv_wgrad(advanced, x):
    k_width = len(advanced)
    rows = [jnp.sum(advanced[k_width - 1 - k] * x, axis=0, keepdims=True) for k in range(k_width)]
    return jnp.concatenate(rows, axis=0)


def _dot(a, b):
    return jnp.dot(a, b, preferred_element_type=F32)


def _dot_nt(a, b):
    return lax.dot_general(a, b, (((1,), (1,)), ((), ())), preferred_element_type=F32)


def _dot_tn(a, b):
    return lax.dot_general(a, b, (((0,), (0,)), ((), ())), preferred_element_type=F32)


def _rms_stats(x):
    r = lax.rsqrt(jnp.mean(x * x, axis=-1, keepdims=True) + RMS_EPS)
    return x * r, r


def _rms_bwd(n, r, g, dy):
    dn = dy * g
    dx = r * (dn - n * jnp.mean(dn * n, axis=-1, keepdims=True))
    return dx, dy * n


def _scan(a_ref, b_ref, h_ref, reverse):
    n, c = a_ref.shape
    row = lax.broadcasted_iota(jnp.int32, (SUBLANES, c), 0)
    span = SCAN_UNROLL * SUBLANES
    n_trips = n // span

    def within(a, b):
        for k in (1, 2, 4):
            if reverse:
                keep, shift = row < SUBLANES - k, SUBLANES - k
            else:
                keep, shift = row >= k, k
            ap = jnp.where(keep, pltpu.roll(a, shift, 0), 1.0)
            bp = jnp.where(keep, pltpu.roll(b, shift, 0), 0.0)
            b = a * bp + b
            a = a * ap
        return a, b

    def trip(t, carry):
        base = pl.multiple_of((n_trips - 1 - t if reverse else t) * span, span)
        order = list(reversed(range(SCAN_UNROLL))) if reverse else list(range(SCAN_UNROLL))
        loaded = [(a_ref[pl.ds(base + u * SUBLANES, SUBLANES), :], b_ref[pl.ds(base + u * SUBLANES, SUBLANES), :])
                  for u in order]
        out = []
        for a, b in [within(a, b) for a, b in loaded]:
            h = a * carry + b
            out.append(h)
            carry = h[0:1, :] if reverse else h[SUBLANES - 1:SUBLANES, :]
        for u, h in zip(order, out):
            h_ref[pl.ds(base + u * SUBLANES, SUBLANES), :] = h
        return carry

    lax.fori_loop(0, n_trips, trip, jnp.zeros((1, c), F32))


def _scan_forward(a_ref, b_ref, h_ref):
    _scan(a_ref, b_ref, h_ref, False)


def _scan_backward(c_ref, b_ref, g_ref):
    _scan(c_ref, b_ref, g_ref, True)


MESH = pl.DeviceIdType.MESH
_HBM = pl.BlockSpec(memory_space=pltpu.HBM)
_OTHER_CHIPS = ((1, 0), (0, 1), (1, 1))
_OTHER_DEVICES = tuple((dx, dy, dc) for dx in (0, 1) for dy in (0, 1) for dc in (0, 1) if dx or dy or dc)
N_DEVICES = 8


def _position():
    return lax.axis_index("x"), lax.axis_index("y"), lax.axis_index("c")


def _flip(v, d):
    return 1 - v if d else v


def _chip(x, y, p):
    px, py = _flip(x, _OTHER_CHIPS[p][0]), _flip(y, _OTHER_CHIPS[p][1])
    return px, py, 2 * px + py


class _Ride:
    def __init__(self, srcs, bufs, scratch, plan, collective_id):
        self.srcs, self.bufs, self.scratch, self.plan = list(srcs), list(bufs), list(scratch), plan
        self.collective_id = collective_id


NEIGHBOURS_AND_SIBLING = 1
OTHER_CHIPS_SAME_CORE = 2
ALL_DEVICES = 3
SIBLING = 4


def _handshake(peers):
    barrier = pltpu.get_barrier_semaphore()
    for peer in peers:
        pl.semaphore_signal(barrier, inc=1, device_id=peer, device_id_type=MESH)
    pl.semaphore_wait(barrier, len(peers))


def _call(body, *, name, grid, in_specs, out_specs, out_shape, operands, scratch_shapes=(), ride=None):
    in_specs, out_specs, out_shape = list(in_specs), list(out_specs), list(out_shape)
    scratch_shapes = list(scratch_shapes)
    if ride is None:
        return pl.pallas_call(body, name=name, grid=grid, in_specs=in_specs, out_specs=out_specs, out_shape=out_shape,
                              scratch_shapes=scratch_shapes, compiler_params=_params())(*operands)
    n_in, n_out, n_scr = len(in_specs), len(out_shape), len(scratch_shapes)
    old = [i for i, b in enumerate(ride.bufs) if not isinstance(b, jax.ShapeDtypeStruct)]
    n_src, n_old, n_buf = len(ride.srcs), len(old), len(ride.bufs)

    def full_body(*refs):
        o0 = n_in + n_src + n_old
        s0 = o0 + n_out + n_buf
        start, relay, relay_on, finish = ride.plan(refs[n_in:n_in + n_src], refs[o0 + n_out:s0], refs[s0 + n_scr:])
        ids = [pl.program_id(i) for i in range(len(grid))]
        first = functools.reduce(jnp.logical_and, [i == 0 for i in ids])
        middle = functools.reduce(jnp.logical_and, [ids[0] == grid[0] // 2] + [i == 0 for i in ids[1:]])
        last = functools.reduce(jnp.logical_and, [i == g - 1 for i, g in zip(ids, grid)])
        pl.when(first)(start)
        pl.when(middle)(relay)
        pl.when(last)(relay_on)
        body(*refs[:n_in], *refs[o0:o0 + n_out], *refs[s0:s0 + n_scr])
        pl.when(last)(finish)

    shapes = [jax.ShapeDtypeStruct(b.shape, b.dtype) for b in ride.bufs]
    res = pl.pallas_call(
        full_body, name=name, grid=grid,
        in_specs=in_specs + [_HBM] * (n_src + n_old), out_specs=out_specs + [_HBM] * n_buf,
        out_shape=out_shape + shapes, scratch_shapes=scratch_shapes + ride.scratch,
        input_output_aliases={n_in + n_src + k: n_out + i for k, i in enumerate(old)},
        compiler_params=_params(collective_id=ride.collective_id),
    )(*operands, *ride.srcs, *[ride.bufs[i] for i in old])
    return list(res[:n_out]), list(res[n_out:])


def run_ride(ride, name):
    def body():
        pass

    return _call(body, name=name, grid=(1,), in_specs=[], out_specs=[], out_shape=[], operands=[], ride=ride)[1]


def gather_ride(shards, items=None, into=None):
    items = items or [(a, 0, s.shape[0]) for a, s in enumerate(shards)]
    bufs = into or [jax.ShapeDtypeStruct((N_CHIPS,) + s.shape, s.dtype) for s in shards]
    nrel = len(_OTHER_CHIPS)

    def plan(srcs, dsts, sems):
        ici_send, ici_recv, hop_send, hop_recv, sib_send, sib_recv = sems
        x, y, c = _position()
        j = 2 * x + y

        def rows(ref, it, h, q=None):
            half = it[2] // 2
            if q is None:
                return ref.at[pl.ds(it[1] + h * half, half), :]
            return ref.at[pl.ds(it[1] + h * half + q * (half // 2), half // 2), :]

        def ici(i, p, slot):
            it = items[i]
            px, py, _ = _chip(x, y, p)
            return pltpu.make_async_remote_copy(
                src_ref=rows(srcs[it[0]], it, c), dst_ref=rows(dsts[it[0]].at[slot], it, c),
                send_sem=ici_send.at[i * nrel + p], recv_sem=ici_recv.at[i * nrel + p],
                device_id=(px, py, c), device_id_type=MESH)

        def hop(i, p, slot):
            it = items[i]
            part = rows(dsts[it[0]].at[slot], it, c, p)
            px, py, _ = _chip(x, y, 1 - p)
            return pltpu.make_async_remote_copy(
                src_ref=part, dst_ref=part, send_sem=hop_send.at[i * 2 + p], recv_sem=hop_recv.at[i * 2 + p],
                device_id=(px, py, c), device_id_type=MESH)

        def sib(i, p, h):
            it = items[i]
            part = rows(dsts[it[0]].at[_chip(x, y, p)[2]], it, h)
            return pltpu.make_async_remote_copy(
                src_ref=part, dst_ref=part, send_sem=sib_send.at[i * nrel + p], recv_sem=sib_recv.at[i * nrel + p],
                device_id=(x, y, 1 - c), device_id_type=MESH)

        every = range(len(items))
        diag = _chip(x, y, 2)[2]

        def start():
            _handshake([_chip(x, y, 0)[:2] + (c,), _chip(x, y, 1)[:2] + (c,), (x, y, 1 - c)])
            for i in every:
                for p in (0, 1):
                    ici(i, p, j).start()

        def relay():
            for i in every:
                for p in (0, 1):
                    k = _chip(x, y, p)[2]
                    ici(i, p, k).wait_recv()
                    hop(i, p, k).start()
                    sib(i, p, c).start()

        def relay_on():
            for i in every:
                for p in (0, 1):
                    hop(i, p, diag).wait_recv()
                sib(i, 2, c).start()

        def finish():
            for i in every:
                for p in range(nrel):
                    sib(i, p, 1 - c).wait_recv()
            for i in every:
                for p in (0, 1):
                    ici(i, p, j).wait_send()
                    hop(i, p, _chip(x, y, p)[2]).wait_send()
                for p in range(nrel):
                    sib(i, p, c).wait_send()

        return start, relay, relay_on, finish

    n = len(items)
    sems = [pltpu.SemaphoreType.DMA((n * nrel,))] * 2 + [pltpu.SemaphoreType.DMA((n * 2,))] * 2 \
        + [pltpu.SemaphoreType.DMA((n * nrel,))] * 2
    return _Ride(shards, bufs, sems, plan, NEIGHBOURS_AND_SIBLING)


def exchange_ride(sums, items=None, into=None, rep=None):
    items = [(a, 0, s.shape[1]) for a, s in enumerate(sums)] if items is None else items
    into = into or [None] * len(sums)
    bufs = [jax.ShapeDtypeStruct(s.shape, s.dtype) if b is None else b for s, b in zip(sums, into)]
    srcs = list(sums)
    scratch = [pltpu.SemaphoreType.DMA((max(len(items), 1) * len(_OTHER_CHIPS),))] * 2
    if rep is not None:
        srcs.append(rep)
        bufs.append(jax.ShapeDtypeStruct((N_DEVICES,) + rep.shape, rep.dtype))
        scratch += [pltpu.SemaphoreType.DMA((len(_OTHER_DEVICES),))] * 2
    nrel = len(_OTHER_CHIPS)

    def plan(src_refs, dst_refs, sems):
        x, y, c = _position()
        j = 2 * x + y
        me = 4 * x + 2 * y + c

        def part(i, p, src_slot, dst_slot):
            a, r0, nr = items[i]
            px, py, _ = _chip(x, y, p)
            return pltpu.make_async_remote_copy(
                src_ref=src_refs[a].at[src_slot, pl.ds(r0, nr), :], dst_ref=dst_refs[a].at[dst_slot, pl.ds(r0, nr), :],
                send_sem=sems[0].at[i * nrel + p], recv_sem=sems[1].at[i * nrel + p],
                device_id=(px, py, c), device_id_type=MESH)

        def device(q):
            dx, dy, dc = _OTHER_DEVICES[q]
            return _flip(x, dx), _flip(y, dy), _flip(c, dc)

        def rep_copy(q, slot):
            return pltpu.make_async_remote_copy(
                src_ref=src_refs[-1], dst_ref=dst_refs[-1].at[slot], send_sem=sems[2].at[q], recv_sem=sems[3].at[q],
                device_id=device(q), device_id_type=MESH)

        pairs = [(i, p) for i in range(len(items)) for p in range(nrel)]
        others = range(len(_OTHER_DEVICES)) if rep is not None else ()

        def start():
            if rep is None:
                _handshake([_chip(x, y, p)[:2] + (c,) for p in range(nrel)])
            else:
                _handshake([device(q) for q in others])
            for i, p in pairs:
                part(i, p, _chip(x, y, p)[2], j).start()
            for q in others:
                rep_copy(q, me).start()

        def finish():
            for i, p in pairs:
                k = _chip(x, y, p)[2]
                part(i, p, k, k).wait_recv()
            for q in others:
                px, py, pc = device(q)
                rep_copy(q, 4 * px + 2 * py + pc).wait_recv()
            for i, p in pairs:
                part(i, p, _chip(x, y, p)[2], j).wait_send()
            for q in others:
                rep_copy(q, me).wait_send()

        return start, lambda: None, lambda: None, finish

    return _Ride(srcs, bufs, scratch, plan, OTHER_CHIPS_SAME_CORE if rep is None else ALL_DEVICES)


def _own_slot(buf, own, index):
    return lax.dynamic_update_slice(buf, own[None], (index,) + (0,) * own.ndim)


def _token_tile(s):
    return min(s, 512)


def norm_in(x, g):
    s, d = x.shape
    t = _token_tile(s)

    def body(x_ref, g_ref, o_ref, ot_ref):
        n, _ = _rms_stats(x_ref[...])
        h = n * g_ref[...]
        o_ref[...] = h.astype(BF16)
        ot_ref[...] = h.T.astype(BF16)

    return pl.pallas_call(
        body, name="norm_in", grid=(s // t,),
        in_specs=[pl.BlockSpec((t, d), lambda i: (i, 0)), pl.BlockSpec((1, d), lambda i: (0, 0))],
        out_specs=[pl.BlockSpec((t, d), lambda i: (i, 0)), pl.BlockSpec((d, t), lambda i: (0, i))],
        out_shape=[jax.ShapeDtypeStruct((s, d), BF16), jax.ShapeDtypeStruct((d, s), BF16)],
        compiler_params=_params(),
    )(x, g)


def matmul_cols(a, w4, name, ride=None):
    m, k = a.shape
    nj, _, ns = w4.shape
    nb = ns // CW

    def body(a_ref, w_ref, o_ref):
        o_ref[...] = _dot(a_ref[...], w_ref[0])

    return _call(
        body, name=name, grid=(nj, nb),
        in_specs=[pl.BlockSpec((m, k), lambda j, b: (0, 0)),
                  pl.BlockSpec((1, k, CW), lambda j, b: (j, 0, b))],
        out_specs=[pl.BlockSpec((m, CW), lambda j, b: (0, j * nb + b))],
        out_shape=[jax.ShapeDtypeStruct((m, nj * ns), F32)],
        operands=(a, w4), ride=ride)


def mix_conv_fwd(proj, ws, ride=None):
    s = proj.shape[0]
    nblk = D_MODEL // CW

    def body(cb_ref, cc_ref, cx_ref, ws_ref, q_ref, ya_ref):
        q = _causal_conv(cc_ref[...] * cx_ref[...], ws_ref)
        q_ref[...] = q
        ya_ref[...] = (cb_ref[...] * q).astype(BF16)

    seg = lambda k: pl.BlockSpec((s, CW), lambda c, k=k: (0, k * nblk + c))
    return _call(
        body, name="mix_conv_fwd", grid=(nblk,),
        in_specs=[seg(0), seg(1), seg(2), pl.BlockSpec((3, CW), lambda c: (0, c))],
        out_specs=[pl.BlockSpec((s, CW), lambda c: (0, c))] * 2,
        out_shape=[jax.ShapeDtypeStruct((s, D_MODEL), F32), jax.ShapeDtypeStruct((s, D_MODEL), BF16)],
        operands=(proj, proj, proj, ws), ride=ride)


def _lru_gates(r, ls):
    log_a = LRU_C * r * ls
    a = jnp.exp(log_a)
    mult = jnp.sqrt(_neg_expm1(2.0 * log_a))
    mult = jnp.where(_rows(r.shape) == 0, 1.0, mult)
    return a, mult


def mix_lru_fwd(proj, wl, bl, wa, ba, wx, bx, lam, ride=None):
    s = proj.shape[0]
    nblk = D_MODEL // CW

    def body(lx_ref, ly_ref, wl_ref, bl_ref, wa_ref, ba_ref, wx_ref, bx_ref, lam_ref,
             xl_ref, r_ref, i_ref, h_ref, yb_ref, a_scr, u_scr):
        xl = _causal_conv(lx_ref[...], wl_ref, bl_ref[...])
        xlb = xl.astype(BF16)
        xl_ref[...] = xlb
        r = _sigmoid(_dot(xlb, wa_ref[0]) + ba_ref[...])
        i = _sigmoid(_dot(xlb, wx_ref[0]) + bx_ref[...])
        r_ref[...] = r.astype(BF16)
        i_ref[...] = i.astype(BF16)
        a, mult = _lru_gates(r, _log_sigmoid(lam_ref[...]))
        a_scr[...] = a
        u_scr[...] = mult * i * xl
        _scan_forward(a_scr, u_scr, h_ref)
        yb_ref[...] = (h_ref[...] * _gelu(ly_ref[...])).astype(BF16)

    blk = lambda k: pl.BlockSpec((s, CW), lambda c, k=k: (0, k * nblk + c))
    vec = pl.BlockSpec((1, CW), lambda c: (0, c))
    mat = pl.BlockSpec((1, CW, CW), lambda c: (c, 0, 0))
    out = pl.BlockSpec((s, CW), lambda c: (0, c))
    f = jax.ShapeDtypeStruct((s, D_MODEL), F32)
    hb = jax.ShapeDtypeStruct((s, D_MODEL), BF16)
    return _call(
        body, name="mix_lru_fwd", grid=(nblk,),
        in_specs=[blk(3), blk(4), pl.BlockSpec((4, CW), lambda c: (0, c)), vec, mat, vec, mat, vec, vec],
        out_specs=[out] * 5,
        out_shape=[hb, hb, hb, f, hb],
        scratch_shapes=[pltpu.VMEM((s, CW), F32), pltpu.VMEM((s, CW), F32)],
        operands=(proj, proj, wl, bl, wa, ba, wx, bx, lam), ride=ride)


def branch_merge_fwd(ya, yb, wcb, wlb, proj, ride=None):
    s = ya.shape[0]
    nblk = D_MODEL // CW

    def body(ya_ref, yb_ref, wcb_ref, wlb_ref, gc_ref, gl_ref, a_ref, b_ref, m_ref):
        a = _dot(ya_ref[...], wcb_ref[...])
        b = _dot(yb_ref[...], wlb_ref[...])
        a_ref[...] = a
        b_ref[...] = b
        m_ref[...] = (_sigmoid(gc_ref[...]) * a + _sigmoid(gl_ref[...]) * b).astype(BF16)

    res = pl.BlockSpec((s, D_MODEL), lambda n: (0, 0))
    wcol = pl.BlockSpec((D_MODEL, CW), lambda n: (0, n))
    blk = lambda k: pl.BlockSpec((s, CW), lambda n, k=k: (0, k * nblk + n))
    out = pl.BlockSpec((s, CW), lambda n: (0, n))
    f = jax.ShapeDtypeStruct((s, D_MODEL), F32)
    return _call(
        body, name="branch_merge_fwd", grid=(nblk,),
        in_specs=[res, res, wcol, wcol, blk(5), blk(6)],
        out_specs=[out] * 3,
        out_shape=[f, f, jax.ShapeDtypeStruct((s, D_MODEL), BF16)],
        operands=(ya, yb, wcb, wlb, proj, proj), ride=ride)


def mix_out_fwd(merged, wout, x, g2, g3, ride=None):
    s, d = x.shape
    t = _token_tile(s)

    def body(m_ref, w_ref, x_ref, g2_ref, g3_ref, mix_ref, x2_ref, h2_ref, h2t_ref):
        mix = _dot(m_ref[...], w_ref[...])
        mix_ref[...] = mix
        n, _ = _rms_stats(mix)
        x2 = x_ref[...] + n * g2_ref[...]
        x2_ref[...] = x2
        n2, _ = _rms_stats(x2)
        h2 = n2 * g3_ref[...]
        h2_ref[...] = h2.astype(BF16)
        h2t_ref[...] = h2.T.astype(BF16)

    tile = pl.BlockSpec((t, d), lambda i: (i, 0))
    vec = pl.BlockSpec((1, d), lambda i: (0, 0))
    f = jax.ShapeDtypeStruct((s, d), F32)
    return _call(
        body, name="mix_out_fwd", grid=(s // t,),
        in_specs=[tile, pl.BlockSpec((d, d), lambda i: (0, 0)), tile, vec, vec],
        out_specs=[tile] * 3 + [pl.BlockSpec((d, t), lambda i: (0, i))],
        out_shape=[f, f, jax.ShapeDtypeStruct((s, d), BF16), jax.ShapeDtypeStruct((d, s), BF16)],
        operands=(merged, wout, x, g2, g3), ride=ride)


def ffn_up_act_fwd(h2, wup4, fw, fb, ride=None):
    s, k = h2.shape
    ns = wup4.shape[2]
    bw = UP_W
    per_chip = ns // bw
    nblk = D_FF // bw

    def body(h_ref, wg_ref, wv_ref, cg_ref, cv_ref, bg_ref, bv_ref, up_ref, act_ref, f_ref):
        h = h_ref[...]
        ug = _dot(h, wg_ref[0])
        uv = _dot(h, wv_ref[0])
        up_ref[0] = ug
        up_ref[1] = uv
        gate = _causal_conv(ug, cg_ref, bg_ref[...])
        val = _causal_conv(uv, cv_ref, bv_ref[...])
        act_ref[0] = gate.astype(BF16)
        act_ref[1] = val.astype(BF16)
        f_ref[...] = (_gelu(gate) * val).astype(BF16)

    wcols = lambda h: pl.BlockSpec((1, k, bw), lambda n, h=h: (n // per_chip + 2 * h, 0, n % per_chip))
    half = lambda h, rows: pl.BlockSpec((rows, bw), lambda n, h=h: (0, h * nblk + n))
    both = pl.BlockSpec((2, s, bw), lambda n: (0, 0, n))
    return _call(
        body, name="ffn_up_act_fwd", grid=(nblk,),
        in_specs=[pl.BlockSpec((s, k), lambda n: (0, 0)), wcols(0), wcols(1),
                  half(0, 3), half(1, 3), half(0, 1), half(1, 1)],
        out_specs=[both, both, pl.BlockSpec((s, bw), lambda n: (0, n))],
        out_shape=[jax.ShapeDtypeStruct((2, s, D_FF), F32), jax.ShapeDtypeStruct((2, s, D_FF), BF16),
                   jax.ShapeDtypeStruct((s, D_FF), BF16)],
        operands=(h2, wup4, wup4, fw, fw, fb, fb), ride=ride)


def ffn_down_loss(f, wdown, x2, target, g4):
    s, d = x2.shape
    t = _token_tile(s)

    def body(f_ref, w_ref, x2_ref, tg_ref, g4_ref, dy_ref, dout_ref, loss_ref, dg4_ref):
        @pl.when(pl.program_id(0) == 0)
        def _():
            loss_ref[...] = jnp.zeros_like(loss_ref)
            dg4_ref[...] = jnp.zeros_like(dg4_ref)

        out = _dot(f_ref[...], w_ref[...])
        n, r = _rms_stats(out)
        err = x2_ref[...] + n * g4_ref[...] - tg_ref[...]
        loss_ref[...] += jnp.full(loss_ref.shape, (0.5 / d) * jnp.sum(err * err), F32)
        dy = err * (1.0 / d)
        dy_ref[...] = dy
        dout, dg = _rms_bwd(n, r, g4_ref[...], dy)
        dout_ref[...] = dout.astype(BF16)
        dg4_ref[...] += jnp.sum(dg, axis=0, keepdims=True)

    tile = pl.BlockSpec((t, d), lambda i: (i, 0))
    vec = pl.BlockSpec((1, d), lambda i: (0, 0))
    return pl.pallas_call(
        body, name="ffn_down_loss", grid=(s // t,),
        in_specs=[pl.BlockSpec((t, D_FF), lambda i: (i, 0)), pl.BlockSpec((D_FF, d), lambda i: (0, 0)), tile, tile, vec],
        out_specs=[tile, tile, pl.BlockSpec((1, 128), lambda i: (0, 0)), vec],
        out_shape=[jax.ShapeDtypeStruct((s, d), F32), jax.ShapeDtypeStruct((s, d), BF16),
                   jax.ShapeDtypeStruct((1, 128), F32), jax.ShapeDtypeStruct((1, d), F32)],
        compiler_params=_params(),
    )(f, wdown, x2, target, g4)


def ffn_up_bwd(dout, wdown, up, act, f, fw, wup4, h2t, ride=None):
    k, s = h2t.shape
    nblk = D_FF // FW
    per_chip = wup4.shape[2] // FW

    def body(do_ref, wd_ref, up_ref, act_ref, f_ref, cg_ref, cv_ref, wg_ref, wv_ref, h_ref,
             dh_ref, dwu_ref, dwd_ref, dw_ref, db_ref, dup_scr):
        @pl.when(pl.program_id(0) == 0)
        def _():
            dup_scr[...] = jnp.zeros_like(dup_scr)
            dh_ref[...] = jnp.zeros_like(dh_ref)

        do = do_ref[...]
        df = _dot_nt(do, wd_ref[...])
        dg = dup_scr[0]
        dv = dup_scr[1]
        ht = h_ref[...]
        dh_ref[...] += _dot_nt(dg, wg_ref[0]) + _dot_nt(dv, wv_ref[0])
        dwu_ref[0] = _dot(ht, dg).astype(BF16)
        dwu_ref[1] = _dot(ht, dv).astype(BF16)
        dwd_ref[...] = _dot_tn(f_ref[...], do).astype(BF16)
        val = act_ref[1].astype(F32)
        ge, dge = _gelu_and_grad(act_ref[0].astype(F32))
        dgate = _advances(df * val * dge, 3)
        dval = _advances(df * ge, 3)
        dw_ref[0] = _conv_wgrad(dgate, up_ref[0])
        dw_ref[1] = _conv_wgrad(dval, up_ref[1])
        db_ref[0] = jnp.sum(dgate[0], axis=0, keepdims=True)
        db_ref[1] = jnp.sum(dval[0], axis=0, keepdims=True)
        dup_scr[0] = _taps_sum(dgate, cg_ref).astype(BF16)
        dup_scr[1] = _taps_sum(dval, cv_ref).astype(BF16)

    cur = lambda n: jnp.minimum(n, nblk - 1)
    prev = lambda n: jnp.maximum(n - 1, 0)
    once = pl.Buffered(1)
    both = lambda rows: pl.BlockSpec((2, rows, FW), lambda n: (0, 0, cur(n)))
    taps = lambda h: pl.BlockSpec((3, FW), lambda n, h=h: (0, h * nblk + cur(n)))
    wcols = lambda h: pl.BlockSpec((1, k, FW), lambda n, h=h: (prev(n) // per_chip + 2 * h, 0, prev(n) % per_chip))
    return _call(
        body, name="ffn_up_bwd", grid=(nblk + 1,),
        in_specs=[pl.BlockSpec((s, D_MODEL), lambda n: (0, 0), pipeline_mode=once),
                  pl.BlockSpec((FW, D_MODEL), lambda n: (cur(n), 0)), both(s), both(s),
                  pl.BlockSpec((s, FW), lambda n: (0, cur(n))), taps(0), taps(1), wcols(0), wcols(1),
                  pl.BlockSpec((k, s), lambda n: (0, 0), pipeline_mode=once)],
        out_specs=[pl.BlockSpec((s, k), lambda n: (0, 0), pipeline_mode=once),
                   pl.BlockSpec((2, k, FW), lambda n: (0, 0, prev(n))),
                   pl.BlockSpec((FW, D_MODEL), lambda n: (cur(n), 0)), both(3), both(1)],
        out_shape=[jax.ShapeDtypeStruct((s, k), F32), jax.ShapeDtypeStruct((2, k, D_FF), BF16),
                   jax.ShapeDtypeStruct((D_FF, D_MODEL), BF16),
                   jax.ShapeDtypeStruct((2, 3, D_FF), F32), jax.ShapeDtypeStruct((2, 1, D_FF), F32)],
        scratch_shapes=[pltpu.VMEM((2, s, FW), BF16)],
        operands=(dout, wdown, up, act, f, fw, fw, wup4, wup4, h2t), ride=ride)


def matmul_cols_bwd(dy, other, name, wgrad, ride=None):
    m = dy[0].shape[1]
    if wgrad:
        k = other.shape[0]
        nj, nb = N_CHIPS, sum(d.shape[0] * d.shape[2] for d in dy) // (N_CHIPS * CW)
    else:
        nj, k, ns = other.shape
        nb = ns // CW
    per_seg = dy[0].shape[2] // CW
    first = [sum(d.shape[0] for d in dy[:i]) for i in range(len(dy))]

    def segment(j, b):
        return (j * nb + b) // per_seg, (j * nb + b) % per_seg

    def body(*refs):
        dy_refs, (o_ref, r_ref) = refs[:len(dy)], refs[len(dy):]
        seg, _ = segment(pl.program_id(0), pl.program_id(1))
        dyb = dy_refs[-1][0]
        for i in range(len(dy) - 2, -1, -1):
            dyb = jnp.where(seg < first[i + 1], dy_refs[i][0], dyb)
        if wgrad:
            r_ref[...] = _dot(o_ref[...], dyb).astype(BF16)
        else:
            @pl.when((pl.program_id(0) == 0) & (pl.program_id(1) == 0))
            def _():
                r_ref[...] = jnp.zeros_like(r_ref)

            r_ref[...] += _dot_nt(dyb, o_ref[0])

    def dy_spec(i):
        nseg = dy[i].shape[0]

        def index(j, b):
            seg, col = segment(j, b)
            local = seg - first[i]
            return (jnp.clip(local, 0, nseg - 1), 0,
                    jnp.where(local < 0, 0, jnp.where(local >= nseg, per_seg - 1, col)))

        return pl.BlockSpec((1, m, CW), index)

    if wgrad:
        other_spec = pl.BlockSpec((k, m), lambda j, b: (0, 0))
        out_spec = pl.BlockSpec((k, CW), lambda j, b: (0, j * nb + b))
        out_shape = jax.ShapeDtypeStruct((k, nj * nb * CW), BF16)
    else:
        other_spec = pl.BlockSpec((1, k, CW), lambda j, b: (j, 0, b))
        out_spec = pl.BlockSpec((m, k), lambda j, b: (0, 0))
        out_shape = jax.ShapeDtypeStruct((m, k), F32)
    return _call(
        body, name=name, grid=(nj, nb), in_specs=[dy_spec(i) for i in range(len(dy))] + [other_spec],
        out_specs=[out_spec], out_shape=[out_shape], operands=(*dy, other), ride=ride)


def norms_mid_bwd(dh2, x2, dy, mix, g3, g2, ride=None):
    s, d = x2.shape
    t = _token_tile(s)

    def body(dh2_ref, x2_ref, dy_ref, mix_ref, g3_ref, g2_ref, dx2_ref, dmix_ref, dg3_ref, dg2_ref):
        @pl.when(pl.program_id(0) == 0)
        def _():
            dg3_ref[...] = jnp.zeros_like(dg3_ref)
            dg2_ref[...] = jnp.zeros_like(dg2_ref)

        n3, r3 = _rms_stats(x2_ref[...])
        dx, dg3 = _rms_bwd(n3, r3, g3_ref[...], dh2_ref[...])
        dx2 = dy_ref[...] + dx
        dx2_ref[...] = dx2
        dg3_ref[...] += jnp.sum(dg3, axis=0, keepdims=True)
        n2, r2 = _rms_stats(mix_ref[...])
        dmix, dg2 = _rms_bwd(n2, r2, g2_ref[...], dx2)
        dmix_ref[...] = dmix.astype(BF16)
        dg2_ref[...] += jnp.sum(dg2, axis=0, keepdims=True)

    tile = pl.BlockSpec((t, d), lambda i: (i, 0))
    vec = pl.BlockSpec((1, d), lambda i: (0, 0))
    v = jax.ShapeDtypeStruct((1, d), F32)
    return _call(
        body, name="norms_mid_bwd", grid=(s // t,),
        in_specs=[tile, tile, tile, tile, vec, vec],
        out_specs=[tile, tile, vec, vec],
        out_shape=[jax.ShapeDtypeStruct((s, d), F32), jax.ShapeDtypeStruct((s, d), BF16), v, v],
        operands=(dh2, x2, dy, mix, g3, g2), ride=ride)


def mix_out_bwd(dmix, wout, merged, a, b, proj, ride=None):
    s = dmix.shape[0]
    nblk = D_MODEL // CW

    def body(dm_ref, w_ref, mg_ref, a_ref, b_ref, gc_ref, gl_ref, da_ref, db_ref, dw_ref, dg_ref):
        dm = dm_ref[...]
        dmerged = _dot_nt(dm, w_ref[...])
        dw_ref[...] = _dot_tn(mg_ref[...], dm).astype(BF16)
        sc = _sigmoid(gc_ref[...])
        sl = _sigmoid(gl_ref[...])
        da_ref[...] = (dmerged * sc).astype(BF16)
        db_ref[...] = (dmerged * sl).astype(BF16)
        dg_ref[0] = (dmerged * a_ref[...] * sc * (1.0 - sc)).astype(BF16)
        dg_ref[1] = (dmerged * b_ref[...] * sl * (1.0 - sl)).astype(BF16)

    res = pl.BlockSpec((s, D_MODEL), lambda n: (0, 0))
    rows = pl.BlockSpec((CW, D_MODEL), lambda n: (n, 0))
    col = pl.BlockSpec((s, CW), lambda n: (0, n))
    blk = lambda k: pl.BlockSpec((s, CW), lambda n, k=k: (0, k * nblk + n))
    hb = jax.ShapeDtypeStruct((s, D_MODEL), BF16)
    return _call(
        body, name="mix_out_bwd", grid=(nblk,),
        in_specs=[res, rows, col, col, col, blk(5), blk(6)],
        out_specs=[col, col, rows, pl.BlockSpec((2, s, CW), lambda n: (0, 0, n))],
        out_shape=[hb, hb, jax.ShapeDtypeStruct((D_MODEL, D_MODEL), BF16), jax.ShapeDtypeStruct((2, s, D_MODEL), BF16)],
        operands=(dmix, wout, merged, a, b, proj, proj), ride=ride)


def mix_conv_bwd(da, wcb, proj, q, ws, ride=None):
    s = da.shape[0]
    nblk = D_MODEL // CW

    def body(da_ref, w_ref, cb_ref, cc_ref, cx_ref, q_ref, ws_ref, dc_ref, dw_ref, dws_ref):
        dab = da_ref[...]
        dya = _dot_nt(dab, w_ref[...])
        cb = cb_ref[...]
        cc = cc_ref[...]
        cx = cx_ref[...]
        q = q_ref[...]
        dw_ref[...] = _dot_tn((cb * q).astype(BF16), dab).astype(BF16)
        dc_ref[0] = (dya * q).astype(BF16)
        dq = _advances(dya * cb, 3)
        dp = _taps_sum(dq, ws_ref)
        dws_ref[...] = _conv_wgrad(dq, cc * cx)
        dc_ref[1] = (dp * cx).astype(BF16)
        dc_ref[2] = (dp * cc).astype(BF16)

    res = pl.BlockSpec((s, D_MODEL), lambda n: (0, 0))
    rows = pl.BlockSpec((CW, D_MODEL), lambda n: (n, 0))
    col = pl.BlockSpec((s, CW), lambda n: (0, n))
    blk = lambda k: pl.BlockSpec((s, CW), lambda n, k=k: (0, k * nblk + n))
    taps = pl.BlockSpec((3, CW), lambda n: (0, n))
    hb = jax.ShapeDtypeStruct((s, D_MODEL), BF16)
    return _call(
        body, name="mix_conv_bwd", grid=(nblk,),
        in_specs=[res, rows, blk(0), blk(1), blk(2), col, taps],
        out_specs=[pl.BlockSpec((3, s, CW), lambda n: (0, 0, n)), rows, taps],
        out_shape=[jax.ShapeDtypeStruct((3, s, D_MODEL), BF16), jax.ShapeDtypeStruct((D_MODEL, D_MODEL), BF16),
                   jax.ShapeDtypeStruct((3, D_MODEL), F32)],
        operands=(da, wcb, proj, proj, proj, q, ws), ride=ride)


def mix_lru_bwd(db, wlb, proj, xl, r, i, h, wl, wa, wx, lam, ride=None):
    s = db.shape[0]
    nblk = D_MODEL // CW

    def body(db_ref, w_ref, lx_ref, ly_ref, xl_ref, r_ref, i_ref, h_ref, wl_ref, wa_ref, wx_ref, lam_ref,
             dl_ref, dw_ref, dwa_ref, dwx_ref, dba_ref, dbx_ref, dwl_ref, dbl_ref, dlam_ref,
             c_scr, g_scr):
        dbb = db_ref[...]
        dyb = _dot_nt(dbb, w_ref[...])
        h = h_ref[...]
        ge, dge = _gelu_and_grad(ly_ref[...])
        dw_ref[...] = _dot_tn((h * ge).astype(BF16), dbb).astype(BF16)
        dl_ref[1] = (dyb * h * dge).astype(BF16)
        r = r_ref[...].astype(F32)
        gi = i_ref[...].astype(F32)
        xlb = xl_ref[...]
        xl = xlb.astype(F32)
        lam = lam_ref[...]
        ls = _log_sigmoid(lam)
        a, mult = _lru_gates(r, ls)
        c_scr[...] = _shift_up(a, 1)
        g_scr[...] = dyb * ge
        _scan_backward(c_scr, g_scr, g_scr)
        du = g_scr[...]
        da = du * _shift_down(h, 1)
        dmult = du * gi * xl
        di = du * mult * xl
        dxl = du * mult * gi
        first = _rows(a.shape) == 0
        dlog_a = da * a - jnp.where(first, 0.0, dmult * a * a / mult)
        dr = dlog_a * (LRU_C * ls)
        dlam_ref[...] = jnp.sum(dlog_a * r, axis=0, keepdims=True) * (LRU_C * (1.0 - _sigmoid(lam)))
        dzr = dr * r * (1.0 - r)
        dzi = di * gi * (1.0 - gi)
        dba_ref[...] = jnp.sum(dzr, axis=0, keepdims=True)
        dbx_ref[...] = jnp.sum(dzi, axis=0, keepdims=True)
        dzrb = dzr.astype(BF16)
        dzib = dzi.astype(BF16)
        dwa_ref[0] = _dot_tn(xlb, dzrb)
        dwx_ref[0] = _dot_tn(xlb, dzib)
        dxl = _advances(dxl + _dot_nt(dzrb, wa_ref[0]) + _dot_nt(dzib, wx_ref[0]), 4)
        dl_ref[0] = _taps_sum(dxl, wl_ref).astype(BF16)
        dwl_ref[...] = _conv_wgrad(dxl, lx_ref[...])
        dbl_ref[...] = jnp.sum(dxl[0], axis=0, keepdims=True)

    res = pl.BlockSpec((s, D_MODEL), lambda n: (0, 0))
    rows = pl.BlockSpec((CW, D_MODEL), lambda n: (n, 0))
    col = pl.BlockSpec((s, CW), lambda n: (0, n))
    blk = lambda k: pl.BlockSpec((s, CW), lambda n, k=k: (0, k * nblk + n))
    taps = pl.BlockSpec((4, CW), lambda n: (0, n))
    vec = pl.BlockSpec((1, CW), lambda n: (0, n))
    mat = pl.BlockSpec((1, CW, CW), lambda n: (n, 0, 0))
    hb = jax.ShapeDtypeStruct((s, D_MODEL), BF16)
    v = jax.ShapeDtypeStruct((1, D_MODEL), F32)
    m = jax.ShapeDtypeStruct((LRU_HEADS, HEAD_DIM, HEAD_DIM), F32)
    scr = pltpu.VMEM((s, CW), F32)
    return _call(
        body, name="mix_lru_bwd", grid=(nblk,),
        in_specs=[res, rows, blk(3), blk(4), col, col, col, col, taps, mat, mat, vec],
        out_specs=[pl.BlockSpec((2, s, CW), lambda n: (0, 0, n)), rows, mat, mat, vec, vec, taps, vec, vec],
        out_shape=[jax.ShapeDtypeStruct((2, s, D_MODEL), BF16), jax.ShapeDtypeStruct((D_MODEL, D_MODEL), BF16), m, m, v, v,
                   jax.ShapeDtypeStruct((4, D_MODEL), F32), v, v],
        scratch_shapes=[scr, scr],
        operands=(db, wlb, proj, proj, xl, r, i, h, wl, wa, wx, lam), ride=ride)


def norm_in_bwd(dh1, x, dx2, g1, ride=None):
    s, d = x.shape
    t = _token_tile(s)

    def body(dh_ref, x_ref, dx2_ref, g_ref, dx_ref, dg_ref):
        @pl.when(pl.program_id(0) == 0)
        def _():
            dg_ref[...] = jnp.zeros_like(dg_ref)

        n, r = _rms_stats(x_ref[...])
        dx, dg = _rms_bwd(n, r, g_ref[...], dh_ref[...])
        dx_ref[...] = dx2_ref[...] + dx
        dg_ref[...] += jnp.sum(dg, axis=0, keepdims=True)

    tile = pl.BlockSpec((t, d), lambda i: (i, 0))
    vec = pl.BlockSpec((1, d), lambda i: (0, 0))
    return _call(
        body, name="norm_in_bwd", grid=(s // t,),
        in_specs=[tile, tile, tile, vec],
        out_specs=[tile, vec],
        out_shape=[jax.ShapeDtypeStruct((s, d), F32), jax.ShapeDtypeStruct((1, d), F32)],
        operands=(dh1, x, dx2, g1), ride=ride)


def _owned_part(ref, kind, k, h, hr):
    if kind == "col":
        ns = ref.shape[1] // N_CHIPS
        return ref.at[pl.ds(h * hr, hr), pl.ds(k * ns, ns)]
    if kind == "row":
        return ref.at[pl.ds(k * 2 * hr + h * hr, hr), :]
    if kind == "col2":
        ns = ref.shape[2] // 2
        return ref.at[k // 2, pl.ds(h * hr, hr), pl.ds((k % 2) * ns, ns)]
    return ref.at[k, pl.ds(h * hr, hr), :]


def _part_shape(g, kind):
    if kind == "col2":
        return g.shape[1] // 2, g.shape[2] // 2
    if kind == "col":
        return g.shape[0] // 2, g.shape[1] // N_CHIPS
    if kind == "row":
        return g.shape[0] // (2 * N_CHIPS), g.shape[1]
    return g.shape[1] // 2, g.shape[2]


def pair_split(grads, kinds, name):
    n = len(grads)
    shapes = [_part_shape(g, k) for g, k in zip(grads, kinds)]

    def body(*refs):
        ins, theirs = refs[:n], refs[n:2 * n]
        send_sem, recv_sem = refs[2 * n:]
        x, y, c = _position()
        copies = []
        for a in range(n):
            hr = shapes[a][0]
            for k in range(N_CHIPS):
                s = a * N_CHIPS + k
                copies.append(pltpu.make_async_remote_copy(
                    src_ref=_owned_part(ins[a], kinds[a], k, 1 - c, hr), dst_ref=theirs[a].at[k],
                    send_sem=send_sem.at[s], recv_sem=recv_sem.at[s], device_id=(x, y, 1 - c), device_id_type=MESH))
        _handshake([(x, y, 1 - c)])
        for cp in copies:
            cp.start()
        for cp in copies:
            cp.wait()

    return pl.pallas_call(
        body, name=name,
        in_specs=[_HBM] * n, out_specs=[_HBM] * n,
        out_shape=[jax.ShapeDtypeStruct((N_CHIPS,) + shp, g.dtype) for shp, g in zip(shapes, grads)],
        scratch_shapes=[pltpu.SemaphoreType.DMA((n * N_CHIPS,))] * 2,
        compiler_params=pltpu.CompilerParams(collective_id=SIBLING),
    )(*grads)


def pair_swap(halves):
    n = len(halves)

    def body(*refs):
        ins, outs = refs[:n], refs[n:2 * n]
        send_sem, recv_sem = refs[2 * n:]
        x, y, c = _position()
        copies = [pltpu.make_async_remote_copy(
            src_ref=ins[a], dst_ref=outs[a], send_sem=send_sem.at[a], recv_sem=recv_sem.at[a],
            device_id=(x, y, 1 - c), device_id_type=MESH) for a in range(n)]
        _handshake([(x, y, 1 - c)])
        for cp in copies:
            cp.start()
        for cp in copies:
            cp.wait()

    return pl.pallas_call(
        body, name="pair_swap",
        in_specs=[_HBM] * n, out_specs=[_HBM] * n,
        out_shape=[jax.ShapeDtypeStruct(h.shape, h.dtype) for h in halves],
        scratch_shapes=[pltpu.SemaphoreType.DMA((n,))] * 2,
        compiler_params=pltpu.CompilerParams(collective_id=SIBLING),
    )(*halves)


def _row_tile(rows, cols, limit_bytes=1 << 20):
    best = None
    for t in range(SUBLANES, rows + 1, SUBLANES):
        if rows % t == 0 and t * cols * 4 <= limit_bytes:
            best = t
    return best or rows


def add_pair(g, kind, theirs, core, name):
    nc, rows, cols = theirs.shape
    t = _row_tile(rows, cols, 4 << 20)
    nt = rows // t

    def body(core_ref, g_ref, b_ref, o_ref):
        mine = g_ref[...].reshape(t, cols)
        o_ref[0] = (mine.astype(F32) + b_ref[0].astype(F32)).astype(o_ref.dtype)

    if kind == "col":
        own = pl.BlockSpec((t, cols), lambda k, i, c: (c[0] * nt + i, k))
    elif kind == "col2":
        own = pl.BlockSpec((1, t, cols), lambda k, i, c: (k // 2, c[0] * nt + i, k % 2))
    elif kind == "row":
        own = pl.BlockSpec((t, cols), lambda k, i, c: ((2 * k + c[0]) * nt + i, 0))
    else:
        own = pl.BlockSpec((1, t, cols), lambda k, i, c: (k, c[0] * nt + i, 0))
    spec = pl.BlockSpec((1, t, cols), lambda k, i, c: (k, i, 0))
    return pl.pallas_call(
        body, name=name,
        grid_spec=pltpu.PrefetchScalarGridSpec(num_scalar_prefetch=1, grid=(nc, nt), in_specs=[own, spec], out_specs=spec),
        out_shape=jax.ShapeDtypeStruct(theirs.shape, theirs.dtype), compiler_params=_params(),
    )(core, g, theirs)


def sum_lead(a, name):
    nl, rows, cols = a.shape
    t = _row_tile(rows, cols, (1 << 20) // 2)

    def body(a_ref, o_ref):
        acc = a_ref[0].astype(F32)
        for s in range(1, nl):
            acc = acc + a_ref[s].astype(F32)
        o_ref[...] = acc

    return pl.pallas_call(
        body, name=name, grid=(rows // t,),
        in_specs=[pl.BlockSpec((nl, t, cols), lambda i: (0, i, 0))],
        out_specs=pl.BlockSpec((t, cols), lambda i: (i, 0)),
        out_shape=jax.ShapeDtypeStruct((rows, cols), F32), compiler_params=_params(),
    )(a)


def sum_chips(rx, csum, chip, name):
    nc, rows, cols = rx.shape
    t = _row_tile(rows, cols, 2 << 20)

    def body(chip_ref, r0, r1, r2, r3, own_ref, o_ref):
        acc = None
        for s, ref in enumerate((r0, r1, r2, r3)):
            term = jnp.where(chip_ref[0] == s, own_ref[0], ref[0]).astype(F32)
            acc = term if acc is None else acc + term
        o_ref[...] = acc

    def slot(s):
        return pl.BlockSpec((1, t, cols), lambda i, c, s=s: (jnp.where(c[0] == s, c[0] ^ 1, s), i, 0))

    return pl.pallas_call(
        body, name=name,
        grid_spec=pltpu.PrefetchScalarGridSpec(
            num_scalar_prefetch=1, grid=(rows // t,),
            in_specs=[slot(s) for s in range(nc)] + [pl.BlockSpec((1, t, cols), lambda i, c: (c[0], i, 0))],
            out_specs=pl.BlockSpec((t, cols), lambda i, c: (i, 0))),
        out_shape=jax.ShapeDtypeStruct((rows, cols), F32), compiler_params=_params(),
    )(chip, rx, rx, rx, rx, csum)


def cast_bf16(a, name):
    rows, cols = a.shape
    t = _row_tile(rows, cols, 2 << 20)

    def body(i_ref, o_ref):
        o_ref[...] = i_ref[...].astype(BF16)

    spec = pl.BlockSpec((t, cols), lambda i: (i, 0))
    return pl.pallas_call(body, name=name, grid=(rows // t,), in_specs=[spec], out_specs=spec,
                          out_shape=jax.ShapeDtypeStruct((rows, cols), BF16), compiler_params=_params())(a)


def _adamw_update(w, g, m, v):
    nm = ADAM_B1 * m + (1.0 - ADAM_B1) * g
    nv = ADAM_B2 * v + (1.0 - ADAM_B2) * (g * g)
    m_hat = nm * (1.0 / (1.0 - ADAM_B1 ** ADAM_STEP))
    v_hat = nv * (1.0 / (1.0 - ADAM_B2 ** ADAM_STEP))
    return -ADAM_LR * (m_hat / (jnp.sqrt(v_hat) + ADAM_EPS) + ADAM_WD * w), nm, nv


def adamw(w, g, m, v, name):
    rows, cols = w.shape
    t = _row_tile(rows, cols)

    def body(w_ref, g_ref, m_ref, v_ref, d_ref, nm_ref, nv_ref):
        d_ref[...], nm_ref[...], nv_ref[...] = _adamw_update(w_ref[...], g_ref[...], m_ref[...], v_ref[...])

    spec = pl.BlockSpec((t, cols), lambda i: (i, 0))
    shp = jax.ShapeDtypeStruct((rows, cols), F32)
    return pl.pallas_call(
        body, name=name, grid=(rows // t,), in_specs=[spec] * 4, out_specs=[spec] * 3,
        out_shape=[shp, shp, shp], compiler_params=_params(),
    )(w, g, m, v)


def adamw_halves(w, g_mine, g_other, m, v, core, name):
    rows, cols = w.shape
    hr = rows // 2
    t = _row_tile(hr, cols)
    nt = hr // t

    def body(core_ref, w_ref, gm_ref, go_ref, m_ref, v_ref, g_ref, d_ref, nm_ref, nv_ref):
        g = jnp.where(pl.program_id(0) // nt == core_ref[0], gm_ref[...], go_ref[...])
        g_ref[...] = g
        d_ref[...], nm_ref[...], nv_ref[...] = _adamw_update(w_ref[...], g, m_ref[...], v_ref[...])

    spec = pl.BlockSpec((t, cols), lambda i, c: (i, 0))
    half = pl.BlockSpec((t, cols), lambda i, c: (i % nt, 0))
    shp = jax.ShapeDtypeStruct((rows, cols), F32)
    return pl.pallas_call(
        body, name=name,
        grid_spec=pltpu.PrefetchScalarGridSpec(num_scalar_prefetch=1, grid=(2 * nt,),
                                               in_specs=[spec, half, half, spec, spec], out_specs=[spec] * 4),
        out_shape=[shp] * 4, compiler_params=_params(),
    )(core, w, g_mine, g_other, m, v)


WEIGHTS = ("norm_mix_pre", "norm_mix_post", "norm_ffn_pre", "norm_ffn_post", "w_in", "conv_short_w",
           "w_conv_branch", "lru_conv_w", "lru_conv_b", "lru_wa", "lru_ba", "lru_wx", "lru_bx", "lru_lambda",
           "w_lru_branch", "w_out", "ffn_w_up", "ffn_conv_w", "ffn_conv_b", "ffn_w_down")
BIG = ("w_in", "ffn_w_up", "w_conv_branch", "w_lru_branch", "w_out", "ffn_w_down")
BIG_KIND = ("col", "col", "row", "row", "row", "row")
SMALL = ("conv_short_w", "lru_conv_w", "lru_wa", "lru_ba", "lru_wx", "lru_bx", "ffn_conv_w")
REPL = ("norm_mix_pre", "norm_mix_post", "norm_ffn_pre", "norm_ffn_post", "lru_conv_b", "lru_lambda", "ffn_conv_b")
PACK_W = 256
SMALL_ROWS = 576
REPL_ROWS = 16
LOSS_ROW = 12
FFN_SHARD = 2 * D_FF // N_CHIPS
QUARTER = HEAD_DIM // N_CHIPS
SMALL_PARTS = (("conv_short_w", 3, (1, 3, PACK_W)), ("lru_conv_w", 4, (1, 4, PACK_W)),
               ("lru_wa", LRU_HEADS * QUARTER, (1, LRU_HEADS, QUARTER, HEAD_DIM)), ("lru_ba", LRU_HEADS, (1, LRU_HEADS, QUARTER)),
               ("lru_wx", LRU_HEADS * QUARTER, (1, LRU_HEADS, QUARTER, HEAD_DIM)), ("lru_bx", LRU_HEADS, (1, LRU_HEADS, QUARTER)),
               ("ffn_conv_w", 3 * FFN_SHARD // PACK_W, (1, 3, FFN_SHARD)))


def _pad8(nr):
    return -(-nr // SUBLANES) * SUBLANES


SMALL_OFFSET = {}
for _name, _nr, _ in SMALL_PARTS:
    SMALL_OFFSET[_name] = sum(_pad8(nr) for n, nr, _ in SMALL_PARTS[:len(SMALL_OFFSET)])
FFN_ROWS = FFN_SHARD // PACK_W
BIASES = ("lru_ba", "lru_bx")
TAPS3 = ("conv_short_w", "ffn_conv_w")


def pack_small(dicts):
    names = [n for n, _, _ in SMALL_PARTS]
    operands = [d[n].transpose(1, 0, 2) if n in TAPS3 else d[n] for d in dicts for n in names]

    def body(*refs):
        ins, outs = refs[:len(operands)], refs[len(operands):]
        for i, o in enumerate(outs):
            o[...] = jnp.zeros_like(o)
            for (name, nr, shape), p in zip(SMALL_PARTS, ins[i * len(names):(i + 1) * len(names)]):
                r0 = SMALL_OFFSET[name]
                if name in BIASES:
                    o[r0:r0 + nr, 0:QUARTER] = p[0]
                elif name == "ffn_conv_w":
                    for k in range(shape[1]):
                        for s in range(FFN_ROWS):
                            o[r0 + FFN_ROWS * k + s:r0 + FFN_ROWS * k + s + 1, :] = p[k, :, s * PACK_W:(s + 1) * PACK_W]
                elif name == "conv_short_w":
                    for k in range(nr):
                        o[r0 + k:r0 + k + 1, :] = p[k]
                else:
                    o[r0:r0 + nr, :] = p[0].reshape(nr, PACK_W)

    shape = jax.ShapeDtypeStruct((SMALL_ROWS, PACK_W), F32)
    return pl.pallas_call(body, name="pack_small", out_shape=[shape] * len(dicts), compiler_params=_params())(*operands)


def full_small(g4):
    def body(p, csw, lcw, wa, wx, fcw):
        chips = range(N_CHIPS)
        r0 = SMALL_OFFSET["conv_short_w"]
        csw[...] = jnp.concatenate([p[c, r0:r0 + 3, :] for c in chips], axis=1)
        r0 = SMALL_OFFSET["lru_conv_w"]
        lcw[...] = jnp.concatenate([p[c, r0:r0 + 4, :] for c in chips], axis=1)
        for name, o in (("lru_wa", wa), ("lru_wx", wx)):
            r0 = SMALL_OFFSET[name]
            for h in range(LRU_HEADS):
                for c in chips:
                    o[h, c * QUARTER:(c + 1) * QUARTER, :] = p[c, r0 + h * QUARTER:r0 + (h + 1) * QUARTER, :].astype(BF16)
        r0 = SMALL_OFFSET["ffn_conv_w"]
        for k in range(3):
            fcw[k:k + 1, :] = jnp.concatenate(
                [p[c, r0 + FFN_ROWS * k + s:r0 + FFN_ROWS * k + s + 1, :] for c in chips for s in range(FFN_ROWS)], axis=1)

    mat = jax.ShapeDtypeStruct((LRU_HEADS, HEAD_DIM, HEAD_DIM), BF16)
    csw, lcw, wa, wx, fcw = pl.pallas_call(
        body, name="full_small",
        out_shape=[jax.ShapeDtypeStruct((3, D_MODEL), F32), jax.ShapeDtypeStruct((4, D_MODEL), F32), mat, mat,
                   jax.ShapeDtypeStruct((3, 2 * D_FF), F32)],
        compiler_params=_params())(g4)

    def bias(name):
        r0 = SMALL_OFFSET[name]
        return g4[:, r0:r0 + LRU_HEADS, :QUARTER].transpose(1, 0, 2).reshape(1, D_MODEL)

    return dict(conv_short_w=csw, lru_conv_w=lcw, lru_wa=wa, lru_wx=wx, ffn_conv_w=fcw,
                lru_ba=bias("lru_ba"), lru_bx=bias("lru_bx"))


def split_small(full):
    def bias(name):
        return full[name].reshape(LRU_HEADS, N_CHIPS, QUARTER).transpose(1, 0, 2)

    def body(csw, lcw, wa, wx, fcw, ba, bx, o):
        o[...] = jnp.zeros_like(o)
        for c in range(N_CHIPS):
            cols = slice(c * PACK_W, (c + 1) * PACK_W)
            r0 = SMALL_OFFSET["conv_short_w"]
            o[c, r0:r0 + 3, :] = csw[:, cols]
            r0 = SMALL_OFFSET["lru_conv_w"]
            o[c, r0:r0 + 4, :] = lcw[:, cols]
            for name, p in (("lru_wa", wa), ("lru_wx", wx)):
                r0 = SMALL_OFFSET[name]
                for h in range(LRU_HEADS):
                    o[c, r0 + h * QUARTER:r0 + (h + 1) * QUARTER, :] = p[h, c * QUARTER:(c + 1) * QUARTER, :]
            for name, p in (("lru_ba", ba), ("lru_bx", bx)):
                r0 = SMALL_OFFSET[name]
                o[c, r0:r0 + LRU_HEADS, 0:QUARTER] = p[c]
            r0 = SMALL_OFFSET["ffn_conv_w"]
            for k in range(3):
                for s in range(FFN_ROWS):
                    lo = c * FFN_SHARD + s * PACK_W
                    o[c, r0 + FFN_ROWS * k + s:r0 + FFN_ROWS * k + s + 1, :] = fcw[k:k + 1, lo:lo + PACK_W]

    return pl.pallas_call(
        body, name="split_small", out_shape=jax.ShapeDtypeStruct((N_CHIPS, SMALL_ROWS, PACK_W), F32),
        compiler_params=_params(),
    )(full["conv_short_w"], full["lru_conv_w"], full["lru_wa"], full["lru_wx"], full["ffn_conv_w"],
      bias("lru_ba"), bias("lru_bx"))


def pack_repl(dicts, loss=None):
    operands = [d[n] for d in dicts for n in REPL] + ([loss] if loss is not None else [])

    def body(*refs):
        ins, outs = refs[:len(operands)], refs[len(operands):]
        for i, o in enumerate(outs):
            o[...] = jnp.zeros_like(o)
            r0 = 0
            for p in ins[i * len(REPL):(i + 1) * len(REPL)]:
                for s in range(p.shape[1] // D_MODEL):
                    o[r0:r0 + 1, :] = p[:, s * D_MODEL:(s + 1) * D_MODEL]
                    r0 += 1
        if loss is not None:
            outs[-1][LOSS_ROW:LOSS_ROW + 1, :] = jnp.tile(ins[-1][...], (1, D_MODEL // 128))

    shape = jax.ShapeDtypeStruct((REPL_ROWS, D_MODEL), F32)
    return pl.pallas_call(body, name="pack_repl" + ("_loss" if loss is not None else ""),
                          out_shape=[shape] * len(dicts), compiler_params=_params())(*operands)


def _lane_concat(ref, r0, n):
    return jnp.concatenate([ref[r0 + s:r0 + s + 1, :] for s in range(n)], axis=1)


def unpack_small(packs):
    names = [n for n, _, _ in SMALL_PARTS]

    def body(*refs):
        ins, outs = refs[:len(packs)], refs[len(packs):]
        for i, p in enumerate(ins):
            for (name, nr, shape), o in zip(SMALL_PARTS, outs[i * len(names):(i + 1) * len(names)]):
                r0 = SMALL_OFFSET[name]
                if name in BIASES:
                    o[0] = p[r0:r0 + nr, 0:QUARTER]
                elif name == "ffn_conv_w":
                    for k in range(shape[1]):
                        o[k] = _lane_concat(p, r0 + FFN_ROWS * k, FFN_ROWS)
                elif name == "conv_short_w":
                    for k in range(nr):
                        o[k] = p[r0 + k:r0 + k + 1, :]
                else:
                    o[0] = p[r0:r0 + nr, :].reshape(shape[1:])

    shapes = [jax.ShapeDtypeStruct((s[1], 1, s[2]) if n in TAPS3 else s, F32) for n, _, s in SMALL_PARTS]
    res = pl.pallas_call(body, name="unpack_small", out_shape=shapes * len(packs), compiler_params=_params())(*packs)
    out = []
    for i in range(len(packs)):
        d = dict(zip(names, res[i * len(names):(i + 1) * len(names)]))
        for n in TAPS3:
            d[n] = d[n].transpose(1, 0, 2)
        out.append(d)
    return out


def unpack_repl(packs):
    rows = [(2 * D_FF // D_MODEL) if n == "ffn_conv_b" else 1 for n in REPL]

    def body(*refs):
        ins, outs = refs[:len(packs)], refs[len(packs):]
        for i, p in enumerate(ins):
            r0 = 0
            for nr, o in zip(rows, outs[i * len(REPL):(i + 1) * len(REPL)]):
                o[...] = _lane_concat(p, r0, nr)
                r0 += nr

    shapes = [jax.ShapeDtypeStruct((1, nr * D_MODEL), F32) for nr in rows]
    res = pl.pallas_call(body, name="unpack_repl", out_shape=shapes * len(packs), compiler_params=_params())(*packs)
    return [dict(zip(REPL, res[i * len(REPL):(i + 1) * len(REPL)])) for i in range(len(packs))]


def kernel(x, norm_mix_pre, norm_mix_post, norm_ffn_pre, norm_ffn_post, w_in, conv_short_w, w_conv_branch, lru_conv_w, lru_conv_b, lru_wa, lru_ba, lru_wx, lru_bx, lru_lambda, w_lru_branch, w_out, ffn_w_up, ffn_conv_w, ffn_conv_b, ffn_w_down, loss_target, m_norm_mix_pre, m_norm_mix_post, m_norm_ffn_pre, m_norm_ffn_post, m_w_in, m_conv_short_w, m_w_conv_branch, m_lru_conv_w, m_lru_conv_b, m_lru_wa, m_lru_ba, m_lru_wx, m_lru_bx, m_lru_lambda, m_w_lru_branch, m_w_out, m_ffn_w_up, m_ffn_conv_w, m_ffn_conv_b, m_ffn_w_down, v_norm_mix_pre, v_norm_mix_post, v_norm_ffn_pre, v_norm_ffn_post, v_w_in, v_conv_short_w, v_w_conv_branch, v_lru_conv_w, v_lru_conv_b, v_lru_wa, v_lru_ba, v_lru_wx, v_lru_bx, v_lru_lambda, v_w_lru_branch, v_w_out, v_ffn_w_up, v_ffn_conv_w, v_ffn_conv_b, v_ffn_w_down):
    given = dict(locals())
    w = {n: given[n] for n in WEIGHTS}
    m = {n: given["m_" + n] for n in WEIGHTS}
    v = {n: given["v_" + n] for n in WEIGHTS}

    xi, yi, ci = _position()
    chip_i = 2 * xi + yi
    chip = chip_i.astype(jnp.int32).reshape(1)
    core = ci.astype(jnp.int32).reshape(1)
    xs, target = x[0], loss_target[0]
    g1, g2, g3, g4 = w["norm_mix_pre"], w["norm_mix_post"], w["norm_ffn_pre"], w["norm_ffn_post"]
    shard = {n: cast_bf16(w[n][0], "cast_" + n) for n in BIG}
    small_shard, m_small, v_small = pack_small([w, m, v])

    def gathered(bufs, names):
        return [_own_slot(b, small_shard if n == "small" else shard[n], chip_i) for b, n in zip(bufs, names)]

    def chip_sums(arrays, kinds, tag):
        theirs = pair_split(arrays, kinds, "pair_split_" + tag)
        return [add_pair(g, k, t, core, "pair_add_%s_%d" % (tag, i)) for i, (g, k, t) in enumerate(zip(arrays, kinds, theirs))]

    h1, h1t = norm_in(xs, g1)
    win4, small4 = gathered(run_ride(gather_ride([shard["w_in"], small_shard]), "gather_first"), ("w_in", "small"))
    small = full_small(small4)
    first_up = 256
    (proj,), got = matmul_cols(
        h1, win4, "proj_fwd",
        ride=gather_ride([shard["w_conv_branch"], shard["w_lru_branch"], shard["w_out"], shard["ffn_w_up"]],
                         items=[(0, 0, 256), (1, 0, 256), (2, 0, 256), (3, 0, first_up)]))
    wcb, wlb, wout = [g.reshape(-1, D_MODEL) for g in gathered(got[:3], ("w_conv_branch", "w_lru_branch", "w_out"))]
    got = got[3:]
    up_piece = lambda r0, nr, into=None: gather_ride([shard["ffn_w_up"]], items=[(0, r0, nr)], into=into)
    down_piece = lambda r0, nr, into=None: gather_ride([shard["ffn_w_down"]], items=[(0, r0, nr)], into=into)
    q, ya = mix_conv_fwd(proj, small["conv_short_w"])
    (xl, r, gi, h, yb), got = mix_lru_fwd(
        proj, small["lru_conv_w"], w["lru_conv_b"], small["lru_wa"], small["lru_ba"],
        small["lru_wx"], small["lru_bx"], w["lru_lambda"], ride=up_piece(first_up, 512, got))
    (a, b, merged), got = branch_merge_fwd(ya, yb, wcb, wlb, proj, ride=up_piece(first_up + 512, 256, got))
    (wup4,) = gathered(got, ("ffn_w_up",))
    (mix, x2, h2, h2t), got = mix_out_fwd(merged, wout, xs, g2, g3, ride=down_piece(0, 256))
    (up, act, f), got = ffn_up_act_fwd(h2, wup4, small["ffn_conv_w"], w["ffn_conv_b"], ride=down_piece(256, 512, got))
    wdown = gathered(got, ("ffn_w_down",))[0].reshape(-1, D_MODEL)
    dy, dout, loss, dg4 = ffn_down_loss(f, wdown, x2, target, g4)

    dh2, dwup, dwdown, dfw, dfb = ffn_up_bwd(dout, wdown, up, act, f, small["ffn_conv_w"], wup4, h2t)
    cs_down, cs_up = chip_sums([dwdown, dwup], ["row", "col2"], "ffn")
    down_rows = lambda r0, nr, into=None: exchange_ride([cs_down], items=[(0, r0, nr)], into=into)
    up_rows = lambda r0, nr, into=None: exchange_ride([cs_up], items=[(0, r0, nr)], into=into)
    (dx2, dmix, dg3, dg2), rx_down = norms_mid_bwd(dh2, x2, dy, mix, g3, g2, ride=down_rows(0, 128))
    (da, db, dwout, dgates), rx_down = mix_out_bwd(dmix, wout, merged, a, b, proj, ride=down_rows(128, 256, rx_down))
    (dconv, dwcb, dws), rx_up = mix_conv_bwd(da, wcb, proj, q, small["conv_short_w"], ride=up_rows(0, 176))
    cs_mid = chip_sums([dwout, dwcb], ["row", "row"], "mid")
    (dlru, dwlb, dwa, dwx, dba, dbx, dwl, dbl, dlam), rx_up = mix_lru_bwd(
        db, wlb, proj, xl, r, gi, h, small["lru_conv_w"], small["lru_wa"], small["lru_wx"], w["lru_lambda"],
        ride=up_rows(176, 336, rx_up))
    grads = dict(norm_mix_post=dg2, norm_ffn_pre=dg3, norm_ffn_post=dg4, conv_short_w=dws, lru_conv_w=dwl,
                 lru_conv_b=dbl, lru_wa=dwa, lru_ba=dba, lru_wx=dwx, lru_bx=dbx, lru_lambda=dlam,
                 ffn_conv_w=jnp.concatenate([dfw[0], dfw[1]], axis=1), ffn_conv_b=jnp.concatenate([dfb[0], dfb[1]], axis=1))
    cs_late = chip_sums([dwlb, split_small(grads)], ["row", "lead"], "late")
    dproj = [dconv, dlru, dgates]
    (dwin,), rx_all = matmul_cols_bwd(dproj, h1t, "proj_wgrad", True, ride=exchange_ride(cs_mid + cs_late))
    rx_mid, rx_late = rx_all[:2], rx_all[2:]
    cs_in = chip_sums([dwin], ["col"], "in")
    in_rows = lambda r0, nr, into=None: exchange_ride(cs_in, items=[(0, r0, nr)], into=into)
    (dh1,), rx_in = matmul_cols_bwd(dproj, win4, "proj_dgrad", False, ride=in_rows(0, 384))
    dx, grads["norm_mix_pre"] = norm_in_bwd(dh1, xs, dx2, g1)
    (rep_part,) = pack_repl([grads], loss)
    rx_in, rep_all = run_ride(exchange_ride(cs_in, items=[(0, 384, 128)], into=rx_in, rep=rep_part), "exchange_last")

    order = (("w_in", cs_in[0], rx_in), ("ffn_w_up", cs_up, rx_up[0]), ("w_conv_branch", cs_mid[1], rx_mid[1]),
             ("w_lru_branch", cs_late[0], rx_late[0]), ("w_out", cs_mid[0], rx_mid[0]),
             ("ffn_w_down", cs_down, rx_down[0]), ("small", cs_late[1], rx_late[1]))
    halves = [sum_chips(rx, cs, chip, "chip_sum_" + n) for n, cs, rx in order]
    me = 4 * xi + 2 * yi + ci
    rep_grad = sum_lead(_own_slot(rep_all, rep_part, me), "device_sum")
    others = pair_swap(halves)

    g_out, d_out, m_out, v_out = {}, {}, {}, {}
    for n, gm, go in zip(BIG, halves[:-1], others[:-1]):
        g, d, nm, nv = adamw_halves(w[n][0], gm, go, m[n][0], v[n][0], core, "adamw_" + n)
        g_out[n], d_out[n], m_out[n], v_out[n] = g[None], d[None], nm[None], nv[None]
    bufs = adamw_halves(small_shard, halves[-1], others[-1], m_small, v_small, core, "adamw_small")
    for dst, part in zip((g_out, d_out, m_out, v_out), unpack_small(bufs)):
        dst.update(part)
    w_rep, m_rep, v_rep = pack_repl([w, m, v])
    d, nm, nv = adamw(w_rep, rep_grad, m_rep, v_rep, "adamw_repl")
    for dst, part in zip((g_out, d_out, m_out, v_out), unpack_repl([rep_grad, d, nm, nv])):
        dst.update(part)

    return (rep_grad[LOSS_ROW, 0], dx[None], *[g_out[n] for n in WEIGHTS], *[d_out[n] for n in WEIGHTS],
            *[m_out[n] for n in WEIGHTS], *[v_out[n] for n in WEIGHTS])
```

```python
import functools
import math

import jax
import jax.numpy as jnp
from jax import lax
from jax.experimental import pallas as pl
from jax.experimental.pallas import tpu as pltpu

F32 = jnp.float32
BF16 = jnp.bfloat16

D_MODEL = 1024
N_CHIPS = 4
N_SEG = 7
D_FF = 3 * D_MODEL
LRU_HEADS = 4
HEAD_DIM = D_MODEL // LRU_HEADS
LRU_C = 8.0
RMS_EPS = 1e-6
CW = 256
FW = 256
SUBLANES = 8
SCAN_UNROLL = 8
VMEM_LIMIT = 58 * 1024 * 1024

ADAM_LR = 0.001
ADAM_B1 = 0.9
ADAM_B2 = 0.999
ADAM_EPS = 1e-08
ADAM_WD = 0.01
ADAM_STEP = 10

_GELU_C = math.sqrt(2.0 / math.pi)
_GELU_K = 0.044715


def _params(**kw):
    return pltpu.CompilerParams(vmem_limit_bytes=VMEM_LIMIT, **kw)


def _sigmoid(x):
    return 1.0 / (1.0 + jnp.exp(-x))


def _gelu(x):
    t = jnp.tanh(_GELU_C * (x + _GELU_K * x * x * x))
    return 0.5 * x * (1.0 + t)


def _gelu_and_grad(x):
    x2 = x * x
    t = jnp.tanh(_GELU_C * (x + _GELU_K * x * x2))
    g = 0.5 * x * (1.0 + t)
    dg = 0.5 * (1.0 + t) + 0.5 * x * (1.0 - t * t) * _GELU_C * (1.0 + 3.0 * _GELU_K * x2)
    return g, dg


def _log_sigmoid(x):
    e = jnp.exp(-jnp.abs(x))
    u = 1.0 + e
    l1p = jnp.where(u == 1.0, e, jnp.log(u) * e / (u - 1.0))
    return jnp.minimum(x, 0.0) - l1p


def _neg_expm1(z):
    series = -z * (1.0 + z * (0.5 + z * (1.0 / 6.0 + z * (1.0 / 24.0 + z * (1.0 / 120.0 + z * (1.0 / 720.0))))))
    return jnp.where(z > -0.2, series, 1.0 - jnp.exp(z))


def _rows(shape):
    return lax.broadcasted_iota(jnp.int32, shape, 0)


def _shift_down(x, k):
    return jnp.where(_rows(x.shape) >= k, pltpu.roll(x, k, 0), 0.0)


def _shift_up(x, k):
    n = x.shape[0]
    return jnp.where(_rows(x.shape) < n - k, pltpu.roll(x, n - k, 0), 0.0)


def _delays(x, k_width):
    return [x] + [_shift_down(x, j) for j in range(1, k_width)]


def _advances(dy, k_width):
    return [dy] + [_shift_up(dy, j) for j in range(1, k_width)]


def _taps_sum(shifted, w_ref, b=None):
    k_width = w_ref.shape[0]
    y = w_ref[k_width - 1:k_width, :] * shifted[0]
    for j in range(1, k_width):
        y = y + w_ref[k_width - 1 - j:k_width - j, :] * shifted[j]
    if b is not None:
        y = y + b
    return y


def _causal_conv(x, w_ref, b=None):
    return _taps_sum(_delays(x, w_ref.shape[0]), w_ref, b)


def _conv_wgrad(advanced, x):
    k_width = len(advanced)
    rows = [jnp.sum(advanced[k_width - 1 - k] * x, axis=0, keepdims=True) for k in range(k_width)]
    return jnp.concatenate(rows, axis=0)


def _dot(a, b):
    return jnp.dot(a, b, preferred_element_type=F32)


def _dot_nt(a, b):
    return lax.dot_general(a, b, (((1,), (1,)), ((), ())), preferred_element_type=F32)


def _dot_tn(a, b):
    return lax.dot_general(a, b, (((0,), (0,)), ((), ())), preferred_element_type=F32)


def _rms_stats(x):
    r = lax.rsqrt(jnp.mean(x * x, axis=-1, keepdims=True) + RMS_EPS)
    return x * r, r


def _rms_bwd(n, r, g, dy):
    dn = dy * g
    dx = r * (dn - n * jnp.mean(dn * n, axis=-1, keepdims=True))
    return dx, dy * n


def _scan(a_ref, b_ref, h_ref, reverse):
    n, c = a_ref.shape
    row = lax.broadcasted_iota(jnp.int32, (SUBLANES, c), 0)
    span = SCAN_UNROLL * SUBLANES
    n_trips = n // span

    def within(a, b):
        for k in (1, 2, 4):
            if reverse:
                keep, shift = row < SUBLANES - k, SUBLANES - k
            else:
                keep, shift = row >= k, k
            ap = jnp.where(keep, pltpu.roll(a, shift, 0), 1.0)
            bp = jnp.where(keep, pltpu.roll(b, shift, 0), 0.0)
            b = a * bp + b
            a = a * ap
        return a, b

    def trip(t, carry):
        base = pl.multiple_of((n_trips - 1 - t if reverse else t) * span, span)
        order = list(reversed(range(SCAN_UNROLL))) if reverse else list(range(SCAN_UNROLL))
        loaded = [(a_ref[pl.ds(base + u * SUBLANES, SUBLANES), :], b_ref[pl.ds(base + u * SUBLANES, SUBLANES), :])
                  for u in order]
        out = []
        for a, b in [within(a, b) for a, b in loaded]:
            h = a * carry + b
            out.append(h)
            carry = h[0:1, :] if reverse else h[SUBLANES - 1:SUBLANES, :]
        for u, h in zip(order, out):
            h_ref[pl.ds(base + u * SUBLANES, SUBLANES), :] = h
        return carry

    lax.fori_loop(0, n_trips, trip, jnp.zeros((1, c), F32))


def _scan_forward(a_ref, b_ref, h_ref):
    _scan(a_ref, b_ref, h_ref, False)


def _scan_backward(c_ref, b_ref, g_ref):
    _scan(c_ref, b_ref, g_ref, True)


MESH = pl.DeviceIdType.MESH
_HBM = pl.BlockSpec(memory_space=pltpu.HBM)
_OTHER_CHIPS = ((1, 0), (0, 1), (1, 1))
_OTHER_DEVICES = tuple((dx, dy, dc) for dx in (0, 1) for dy in (0, 1) for dc in (0, 1) if dx or dy or dc)
N_DEVICES = 8


def _position():
    return lax.axis_index("x"), lax.axis_index("y"), lax.axis_index("c")


def _flip(v, d):
    return 1 - v if d else v


def _chip(x, y, p):
    px, py = _flip(x, _OTHER_CHIPS[p][0]), _flip(y, _OTHER_CHIPS[p][1])
    return px, py, 2 * px + py


class _Ride:
    def __init__(self, srcs, bufs, scratch, plan, collective_id):
        self.srcs, self.bufs, self.scratch, self.plan = list(srcs), list(bufs), list(scratch), plan
        self.collective_id = collective_id


NEIGHBOURS_AND_SIBLING = 1
OTHER_CHIPS_SAME_CORE = 2
ALL_DEVICES = 3
SIBLING = 4


def _handshake(peers):
    barrier = pltpu.get_barrier_semaphore()
    for peer in peers:
        pl.semaphore_signal(barrier, inc=1, device_id=peer, device_id_type=MESH)
    pl.semaphore_wait(barrier, len(peers))


def _call(body, *, name, grid, in_specs, out_specs, out_shape, operands, scratch_shapes=(), ride=None):
    in_specs, out_specs, out_shape = list(in_specs), list(out_specs), list(out_shape)
    scratch_shapes = list(scratch_shapes)
    if ride is None:
        return pl.pallas_call(body, name=name, grid=grid, in_specs=in_specs, out_specs=out_specs, out_shape=out_shape,
                              scratch_shapes=scratch_shapes, compiler_params=_params())(*operands)
    n_in, n_out, n_scr = len(in_specs), len(out_shape), len(scratch_shapes)
    old = [i for i, b in enumerate(ride.bufs) if not isinstance(b, jax.ShapeDtypeStruct)]
    n_src, n_old, n_buf = len(ride.srcs), len(old), len(ride.bufs)

    def full_body(*refs):
        o0 = n_in + n_src + n_old
        s0 = o0 + n_out + n_buf
        start, relay, relay_on, finish = ride.plan(refs[n_in:n_in + n_src], refs[o0 + n_out:s0], refs[s0 + n_scr:])
        ids = [pl.program_id(i) for i in range(len(grid))]
        first = functools.reduce(jnp.logical_and, [i == 0 for i in ids])
        middle = functools.reduce(jnp.logical_and, [ids[0] == grid[0] // 2] + [i == 0 for i in ids[1:]])
        last = functools.reduce(jnp.logical_and, [i == g - 1 for i, g in zip(ids, grid)])
        pl.when(first)(start)
        pl.when(middle)(relay)
        pl.when(last)(relay_on)
        body(*refs[:n_in], *refs[o0:o0 + n_out], *refs[s0:s0 + n_scr])
        pl.when(last)(finish)

    shapes = [jax.ShapeDtypeStruct(b.shape, b.dtype) for b in ride.bufs]
    res = pl.pallas_call(
        full_body, name=name, grid=grid,
        in_specs=in_specs + [_HBM] * (n_src + n_old), out_specs=out_specs + [_HBM] * n_buf,
        out_shape=out_shape + shapes, scratch_shapes=scratch_shapes + ride.scratch,
        input_output_aliases={n_in + n_src + k: n_out + i for k, i in enumerate(old)},
        compiler_params=_params(collective_id=ride.collective_id),
    )(*operands, *ride.srcs, *[ride.bufs[i] for i in old])
    return list(res[:n_out]), list(res[n_out:])


def run_ride(ride, name):
    def body():
        pass

    return _call(body, name=name, grid=(1,), in_specs=[], out_specs=[], out_shape=[], operands=[], ride=ride)[1]


def gather_ride(shards, items=None, into=None):
    items = items or [(a, 0, s.shape[0]) for a, s in enumerate(shards)]
    bufs = into or [jax.ShapeDtypeStruct((N_CHIPS,) + s.shape, s.dtype) for s in shards]
    nrel = len(_OTHER_CHIPS)

    def plan(srcs, dsts, sems):
        ici_send, ici_recv, hop_send, hop_recv, sib_send, sib_recv = sems
        x, y, c = _position()
        j = 2 * x + y

        def rows(ref, it, h, q=None):
            half = it[2] // 2
            if q is None:
                return ref.at[pl.ds(it[1] + h * half, half), :]
            return ref.at[pl.ds(it[1] + h * half + q * (half // 2), half // 2), :]

        def ici(i, p, slot):
            it = items[i]
            px, py, _ = _chip(x, y, p)
            return pltpu.make_async_remote_copy(
                src_ref=rows(srcs[it[0]], it, c), dst_ref=rows(dsts[it[0]].at[slot], it, c),
                send_sem=ici_send.at[i * nrel + p], recv_sem=ici_recv.at[i * nrel + p],
                device_id=(px, py, c), device_id_type=MESH)

        def hop(i, p, slot):
            it = items[i]
            part = rows(dsts[it[0]].at[slot], it, c, p)
            px, py, _ = _chip(x, y, 1 - p)
            return pltpu.make_async_remote_copy(
                src_ref=part, dst_ref=part, send_sem=hop_send.at[i * 2 + p], recv_sem=hop_recv.at[i * 2 + p],
                device_id=(px, py, c), device_id_type=MESH)

        def sib(i, p, h):
            it = items[i]
            part = rows(dsts[it[0]].at[_chip(x, y, p)[2]], it, h)
            return pltpu.make_async_remote_copy(
                src_ref=part, dst_ref=part, send_sem=sib_send.at[i * nrel + p], recv_sem=sib_recv.at[i * nrel + p],
                device_id=(x, y, 1 - c), device_id_type=MESH)

        every = range(len(items))
        diag = _chip(x, y, 2)[2]

        def start():
            _handshake([_chip(x, y, 0)[:2] + (c,), _chip(x, y, 1)[:2] + (c,), (x, y, 1 - c)])
            for i in every:
                for p in (0, 1):
                    ici(i, p, j).start()

        def relay():
            for i in every:
                for p in (0, 1):
                    k = _chip(x, y, p)[2]
                    ici(i, p, k).wait_recv()
                    hop(i, p, k).start()
                    sib(i, p, c).start()

        def relay_on():
            for i in every:
                for p in (0, 1):
                    hop(i, p, diag).wait_recv()
                sib(i, 2, c).start()

        def finish():
            for i in every:
                for p in range(nrel):
                    sib(i, p, 1 - c).wait_recv()
            for i in every:
                for p in (0, 1):
                    ici(i, p, j).wait_send()
                    hop(i, p, _chip(x, y, p)[2]).wait_send()
                for p in range(nrel):
                    sib(i, p, c).wait_send()

        return start, relay, relay_on, finish

    n = len(items)
    sems = [pltpu.SemaphoreType.DMA((n * nrel,))] * 2 + [pltpu.SemaphoreType.DMA((n * 2,))] * 2 \
        + [pltpu.SemaphoreType.DMA((n * nrel,))] * 2
    return _Ride(shards, bufs, sems, plan, NEIGHBOURS_AND_SIBLING)


def exchange_ride(sums, items=None, into=None, rep=None):
    items = [(a, 0, s.shape[1]) for a, s in enumerate(sums)] if items is None else items
    into = into or [None] * len(sums)
    bufs = [jax.ShapeDtypeStruct(s.shape, s.dtype) if b is None else b for s, b in zip(sums, into)]
    srcs = list(sums)
    scratch = [pltpu.SemaphoreType.DMA((max(len(items), 1) * len(_OTHER_CHIPS),))] * 2
    if rep is not None:
        srcs.append(rep)
        bufs.append(jax.ShapeDtypeStruct((N_DEVICES,) + rep.shape, rep.dtype))
        scratch += [pltpu.SemaphoreType.DMA((len(_OTHER_DEVICES),))] * 2
    nrel = len(_OTHER_CHIPS)

    def plan(src_refs, dst_refs, sems):
        x, y, c = _position()
        j = 2 * x + y
        me = 4 * x + 2 * y + c

        def part(i, p, src_slot, dst_slot):
            a, r0, nr = items[i]
            px, py, _ = _chip(x, y, p)
            return pltpu.make_async_remote_copy(
                src_ref=src_refs[a].at[src_slot, pl.ds(r0, nr), :], dst_ref=dst_refs[a].at[dst_slot, pl.ds(r0, nr), :],
                send_sem=sems[0].at[i * nrel + p], recv_sem=sems[1].at[i * nrel + p],
                device_id=(px, py, c), device_id_type=MESH)

        def device(q):
            dx, dy, dc = _OTHER_DEVICES[q]
            return _flip(x, dx), _flip(y, dy), _flip(c, dc)

        def rep_copy(q, slot):
            return pltpu.make_async_remote_copy(
                src_ref=src_refs[-1], dst_ref=dst_refs[-1].at[slot], send_sem=sems[2].at[q], recv_sem=sems[3].at[q],
                device_id=device(q), device_id_type=MESH)

        pairs = [(i, p) for i in range(len(items)) for p in range(nrel)]
        others = range(len(_OTHER_DEVICES)) if rep is not None else ()

        def start():
            if rep is None:
                _handshake([_chip(x, y, p)[:2] + (c,) for p in range(nrel)])
            else:
                _handshake([device(q) for q in others])
            for i, p in pairs:
                part(i, p, _chip(x, y, p)[2], j).start()
            for q in others:
                rep_copy(q, me).start()

        def finish():
            for i, p in pairs:
                k = _chip(x, y, p)[2]
                part(i, p, k, k).wait_recv()
            for q in others:
                px, py, pc = device(q)
                rep_copy(q, 4 * px + 2 * py + pc).wait_recv()
            for i, p in pairs:
                part(i, p, _chip(x, y, p)[2], j).wait_send()
            for q in others:
                rep_copy(q, me).wait_send()

        return start, lambda: None, lambda: None, finish

    return _Ride(srcs, bufs, scratch, plan, OTHER_CHIPS_SAME_CORE if rep is None else ALL_DEVICES)


def _own_slot(buf, own, index):
    return lax.dynamic_update_slice(buf, own[None], (index,) + (0,) * own.ndim)


def _token_tile(s):
    return min(s, 512)


def norm_in(x, g):
    s, d = x.shape
    t = _token_tile(s)

    def body(x_ref, g_ref, o_ref, ot_ref):
        n, _ = _rms_stats(x_ref[...])
        h = n * g_ref[...]
        o_ref[...] = h.astype(BF16)
        ot_ref[...] = h.T.astype(BF16)

    return pl.pallas_call(
        body, name="norm_in", grid=(s // t,),
        in_specs=[pl.BlockSpec((t, d), lambda i: (i, 0)), pl.BlockSpec((1, d), lambda i: (0, 0))],
        out_specs=[pl.BlockSpec((t, d), lambda i: (i, 0)), pl.BlockSpec((d, t), lambda i: (0, i))],
        out_shape=[jax.ShapeDtypeStruct((s, d), BF16), jax.ShapeDtypeStruct((d, s), BF16)],
        compiler_params=_params(),
    )(x, g)


def matmul_cols(a, w4, name, ride=None):
    m, k = a.shape
    nj, _, ns = w4.shape
    nb = ns // CW

    def body(a_ref, w_ref, o_ref):
        o_ref[...] = _dot(a_ref[...], w_ref[0])

    return _call(
        body, name=name, grid=(nj, nb),
        in_specs=[pl.BlockSpec((m, k), lambda j, b: (0, 0)),
                  pl.BlockSpec((1, k, CW), lambda j, b: (j, 0, b))],
        out_specs=[pl.BlockSpec((m, CW), lambda j, b: (0, j * nb + b))],
        out_shape=[jax.ShapeDtypeStruct((m, nj * ns), F32)],
        operands=(a, w4), ride=ride)


def mix_conv_fwd(proj, ws, ride=None):
    s = proj.shape[0]
    nblk = D_MODEL // CW

    def body(cb_ref, cc_ref, cx_ref, ws_ref, q_ref, ya_ref):
        q = _causal_conv(cc_ref[...] * cx_ref[...], ws_ref)
        q_ref[...] = q
        ya_ref[...] = (cb_ref[...] * q).astype(BF16)

    seg = lambda k: pl.BlockSpec((s, CW), lambda c, k=k: (0, k * nblk + c))
    return _call(
        body, name="mix_conv_fwd", grid=(nblk,),
        in_specs=[seg(0), seg(1), seg(2), pl.BlockSpec((3, CW), lambda c: (0, c))],
        out_specs=[pl.BlockSpec((s, CW), lambda c: (0, c))] * 2,
        out_shape=[jax.ShapeDtypeStruct((s, D_MODEL), F32), jax.ShapeDtypeStruct((s, D_MODEL), BF16)],
        operands=(proj, proj, proj, ws), ride=ride)


def _lru_gates(r, ls):
    log_a = LRU_C * r * ls
    a = jnp.exp(log_a)
    mult = jnp.sqrt(_neg_expm1(2.0 * log_a))
    mult = jnp.where(_rows(r.shape) == 0, 1.0, mult)
    return a, mult


def mix_lru_fwd(proj, wl, bl, wa, ba, wx, bx, lam, ride=None):
    s = proj.shape[0]
    nblk = D_MODEL // CW

    def body(lx_ref, ly_ref, wl_ref, bl_ref, wa_ref, ba_ref, wx_ref, bx_ref, lam_ref,
             xl_ref, r_ref, i_ref, h_ref, yb_ref, a_scr, u_scr):
        xl = _causal_conv(lx_ref[...], wl_ref, bl_ref[...])
        xlb = xl.astype(BF16)
        xl_ref[...] = xlb
        r = _sigmoid(_dot(xlb, wa_ref[0]) + ba_ref[...])
        i = _sigmoid(_dot(xlb, wx_ref[0]) + bx_ref[...])
        r_ref[...] = r.astype(BF16)
        i_ref[...] = i.astype(BF16)
        a, mult = _lru_gates(r, _log_sigmoid(lam_ref[...]))
        a_scr[...] = a
        u_scr[...] = mult * i * xl
        _scan_forward(a_scr, u_scr, h_ref)
        yb_ref[...] = (h_ref[...] * _gelu(ly_ref[...])).astype(BF16)

    blk = lambda k: pl.BlockSpec((s, CW), lambda c, k=k: (0, k * nblk + c))
    vec = pl.BlockSpec((1, CW), lambda c: (0, c))
    mat = pl.BlockSpec((1, CW, CW), lambda c: (c, 0, 0))
    out = pl.BlockSpec((s, CW), lambda c: (0, c))
    f = jax.ShapeDtypeStruct((s, D_MODEL), F32)
    hb = jax.ShapeDtypeStruct((s, D_MODEL), BF16)
    return _call(
        body, name="mix_lru_fwd", grid=(nblk,),
        in_specs=[blk(3), blk(4), pl.BlockSpec((4, CW), lambda c: (0, c)), vec, mat, vec, mat, vec, vec],
        out_specs=[out] * 5,
        out_shape=[hb, hb, hb, f, hb],
        scratch_shapes=[pltpu.VMEM((s, CW), F32), pltpu.VMEM((s, CW), F32)],
        operands=(proj, proj, wl, bl, wa, ba, wx, bx, lam), ride=ride)


def branch_merge_fwd(ya, yb, wcb, wlb, proj, ride=None):
    s = ya.shape[0]
    nblk = D_MODEL // CW

    def body(ya_ref, yb_ref, wcb_ref, wlb_ref, gc_ref, gl_ref, a_ref, b_ref, m_ref):
        a = _dot(ya_ref[...], wcb_ref[...])
        b = _dot(yb_ref[...], wlb_ref[...])
        a_ref[...] = a
        b_ref[...] = b
        m_ref[...] = (_sigmoid(gc_ref[...]) * a + _sigmoid(gl_ref[...]) * b).astype(BF16)

    res = pl.BlockSpec((s, D_MODEL), lambda n: (0, 0))
    wcol = pl.BlockSpec((D_MODEL, CW), lambda n: (0, n))
    blk = lambda k: pl.BlockSpec((s, CW), lambda n, k=k: (0, k * nblk + n))
    out = pl.BlockSpec((s, CW), lambda n: (0, n))
    f = jax.ShapeDtypeStruct((s, D_MODEL), F32)
    return _call(
        body, name="branch_merge_fwd", grid=(nblk,),
        in_specs=[res, res, wcol, wcol, blk(5), blk(6)],
        out_specs=[out] * 3,
        out_shape=[f, f, jax.ShapeDtypeStruct((s, D_MODEL), BF16)],
        operands=(ya, yb, wcb, wlb, proj, proj), ride=ride)


def mix_out_fwd(merged, wout, x, g2, g3, ride=None):
    s, d = x.shape
    t = _token_tile(s)

    def body(m_ref, w_ref, x_ref, g2_ref, g3_ref, mix_ref, x2_ref, h2_ref, h2t_ref):
        mix = _dot(m_ref[...], w_ref[...])
        mix_ref[...] = mix
        n, _ = _rms_stats(mix)
        x2 = x_ref[...] + n * g2_ref[...]
        x2_ref[...] = x2
        n2, _ = _rms_stats(x2)
        h2 = n2 * g3_ref[...]
        h2_ref[...] = h2.astype(BF16)
        h2t_ref[...] = h2.T.astype(BF16)

    tile = pl.BlockSpec((t, d), lambda i: (i, 0))
    vec = pl.BlockSpec((1, d), lambda i: (0, 0))
    f = jax.ShapeDtypeStruct((s, d), F32)
    return _call(
        body, name="mix_out_fwd", grid=(s // t,),
        in_specs=[tile, pl.BlockSpec((d, d), lambda i: (0, 0)), tile, vec, vec],
        out_specs=[tile] * 3 + [pl.BlockSpec((d, t), lambda i: (0, i))],
        out_shape=[f, f, jax.ShapeDtypeStruct((s, d), BF16), jax.ShapeDtypeStruct((d, s), BF16)],
        operands=(merged, wout, x, g2, g3), ride=ride)


def ffn_up_act_fwd(h2, wup4, fw, fb, ride=None):
    s, k = h2.shape
    ns = wup4.shape[2]
    per_chip = ns // CW
    nblk = D_FF // CW

    def body(h_ref, wg_ref, wv_ref, cg_ref, cv_ref, bg_ref, bv_ref, up_ref, act_ref, f_ref):
        h = h_ref[...]
        ug = _dot(h, wg_ref[0])
        uv = _dot(h, wv_ref[0])
        up_ref[0] = ug
        up_ref[1] = uv
        gate = _causal_conv(ug, cg_ref, bg_ref[...])
        val = _causal_conv(uv, cv_ref, bv_ref[...])
        act_ref[0] = gate.astype(BF16)
        act_ref[1] = val.astype(BF16)
        f_ref[...] = (_gelu(gate) * val).astype(BF16)

    wcols = lambda h: pl.BlockSpec((1, k, CW), lambda n, h=h: (n // per_chip + 2 * h, 0, n % per_chip))
    half = lambda h, rows: pl.BlockSpec((rows, CW), lambda n, h=h: (0, h * nblk + n))
    both = pl.BlockSpec((2, s, CW), lambda n: (0, 0, n))
    return _call(
        body, name="ffn_up_act_fwd", grid=(nblk,),
        in_specs=[pl.BlockSpec((s, k), lambda n: (0, 0)), wcols(0), wcols(1),
                  half(0, 3), half(1, 3), half(0, 1), half(1, 1)],
        out_specs=[both, both, pl.BlockSpec((s, CW), lambda n: (0, n))],
        out_shape=[jax.ShapeDtypeStruct((2, s, D_FF), F32), jax.ShapeDtypeStruct((2, s, D_FF), BF16),
                   jax.ShapeDtypeStruct((s, D_FF), BF16)],
        operands=(h2, wup4, wup4, fw, fw, fb, fb), ride=ride)


def ffn_down_loss(f, wdown, x2, target, g4):
    s, d = x2.shape
    t = _token_tile(s)

    def body(f_ref, w_ref, x2_ref, tg_ref, g4_ref, dy_ref, dout_ref, loss_ref, dg4_ref):
        @pl.when(pl.program_id(0) == 0)
        def _():
            loss_ref[...] = jnp.zeros_like(loss_ref)
            dg4_ref[...] = jnp.zeros_like(dg4_ref)

        out = _dot(f_ref[...], w_ref[...])
        n, r = _rms_stats(out)
        err = x2_ref[...] + n * g4_ref[...] - tg_ref[...]
        loss_ref[...] += jnp.full(loss_ref.shape, (0.5 / d) * jnp.sum(err * err), F32)
        dy = err * (1.0 / d)
        dy_ref[...] = dy
        dout, dg = _rms_bwd(n, r, g4_ref[...], dy)
        dout_ref[...] = dout.astype(BF16)
        dg4_ref[...] += jnp.sum(dg, axis=0, keepdims=True)

    tile = pl.BlockSpec((t, d), lambda i: (i, 0))
    vec = pl.BlockSpec((1, d), lambda i: (0, 0))
    return pl.pallas_call(
        body, name="ffn_down_loss", grid=(s // t,),
        in_specs=[pl.BlockSpec((t, D_FF), lambda i: (i, 0)), pl.BlockSpec((D_FF, d), lambda i: (0, 0)), tile, tile, vec],
        out_specs=[tile, tile, pl.BlockSpec((1, 128), lambda i: (0, 0)), vec],
        out_shape=[jax.ShapeDtypeStruct((s, d), F32), jax.ShapeDtypeStruct((s, d), BF16),
                   jax.ShapeDtypeStruct((1, 128), F32), jax.ShapeDtypeStruct((1, d), F32)],
        compiler_params=_params(),
    )(f, wdown, x2, target, g4)


def ffn_up_bwd(dout, wdown, up, act, f, fw, wup4, h2t, ride=None):
    k, s = h2t.shape
    nblk = D_FF // FW
    per_chip = wup4.shape[2] // FW

    def body(do_ref, wd_ref, up_ref, act_ref, f_ref, cg_ref, cv_ref, wg_ref, wv_ref, h_ref,
             dh_ref, dwu_ref, dwd_ref, dw_ref, db_ref, dup_scr):
        @pl.when(pl.program_id(0) == 0)
        def _():
            dup_scr[...] = jnp.zeros_like(dup_scr)
            dh_ref[...] = jnp.zeros_like(dh_ref)

        do = do_ref[...]
        df = _dot_nt(do, wd_ref[...])
        dg = dup_scr[0]
        dv = dup_scr[1]
        ht = h_ref[...]
        dh_ref[...] += _dot_nt(dg, wg_ref[0]) + _dot_nt(dv, wv_ref[0])
        dwu_ref[0] = _dot(ht, dg).astype(BF16)
        dwu_ref[1] = _dot(ht, dv).astype(BF16)
        dwd_ref[...] = _dot_tn(f_ref[...], do).astype(BF16)
        val = act_ref[1].astype(F32)
        ge, dge = _gelu_and_grad(act_ref[0].astype(F32))
        dgate = _advances(df * val * dge, 3)
        dval = _advances(df * ge, 3)
        dw_ref[0] = _conv_wgrad(dgate, up_ref[0])
        dw_ref[1] = _conv_wgrad(dval, up_ref[1])
        db_ref[0] = jnp.sum(dgate[0], axis=0, keepdims=True)
        db_ref[1] = jnp.sum(dval[0], axis=0, keepdims=True)
        dup_scr[0] = _taps_sum(dgate, cg_ref).astype(BF16)
        dup_scr[1] = _taps_sum(dval, cv_ref).astype(BF16)

    cur = lambda n: jnp.minimum(n, nblk - 1)
    prev = lambda n: jnp.maximum(n - 1, 0)
    once = pl.Buffered(1)
    both = lambda rows: pl.BlockSpec((2, rows, FW), lambda n: (0, 0, cur(n)))
    taps = lambda h: pl.BlockSpec((3, FW), lambda n, h=h: (0, h * nblk + cur(n)))
    wcols = lambda h: pl.BlockSpec((1, k, FW), lambda n, h=h: (prev(n) // per_chip + 2 * h, 0, prev(n) % per_chip))
    return _call(
        body, name="ffn_up_bwd", grid=(nblk + 1,),
        in_specs=[pl.BlockSpec((s, D_MODEL), lambda n: (0, 0), pipeline_mode=once),
                  pl.BlockSpec((FW, D_MODEL), lambda n: (cur(n), 0)), both(s), both(s),
                  pl.BlockSpec((s, FW), lambda n: (0, cur(n))), taps(0), taps(1), wcols(0), wcols(1),
                  pl.BlockSpec((k, s), lambda n: (0, 0), pipeline_mode=once)],
        out_specs=[pl.BlockSpec((s, k), lambda n: (0, 0), pipeline_mode=once),
                   pl.BlockSpec((2, k, FW), lambda n: (0, 0, prev(n))),
                   pl.BlockSpec((FW, D_MODEL), lambda n: (cur(n), 0)), both(3), both(1)],
        out_shape=[jax.ShapeDtypeStruct((s, k), F32), jax.ShapeDtypeStruct((2, k, D_FF), BF16),
                   jax.ShapeDtypeStruct((D_FF, D_MODEL), BF16),
                   jax.ShapeDtypeStruct((2, 3, D_FF), F32), jax.ShapeDtypeStruct((2, 1, D_FF), F32)],
        scratch_shapes=[pltpu.VMEM((2, s, FW), BF16)],
        operands=(dout, wdown, up, act, f, fw, fw, wup4, wup4, h2t), ride=ride)


def matmul_cols_bwd(dy, other, name, wgrad, ride=None):
    m = dy[0].shape[1]
    if wgrad:
        k = other.shape[0]
        nj, nb = N_CHIPS, sum(d.shape[0] * d.shape[2] for d in dy) // (N_CHIPS * CW)
    else:
        nj, k, ns = other.shape
        nb = ns // CW
    per_seg = dy[0].shape[2] // CW
    first = [sum(d.shape[0] for d in dy[:i]) for i in range(len(dy))]

    def segment(j, b):
        return (j * nb + b) // per_seg, (j * nb + b) % per_seg

    def body(*refs):
        dy_refs, (o_ref, r_ref) = refs[:len(dy)], refs[len(dy):]
        seg, _ = segment(pl.program_id(0), pl.program_id(1))
        dyb = dy_refs[-1][0]
        for i in range(len(dy) - 2, -1, -1):
            dyb = jnp.where(seg < first[i + 1], dy_refs[i][0], dyb)
        if wgrad:
            r_ref[...] = _dot(o_ref[...], dyb).astype(BF16)
        else:
            @pl.when((pl.program_id(0) == 0) & (pl.program_id(1) == 0))
            def _():
                r_ref[...] = jnp.zeros_like(r_ref)

            r_ref[...] += _dot_nt(dyb, o_ref[0])

    def dy_spec(i):
        nseg = dy[i].shape[0]

        def index(j, b):
            seg, col = segment(j, b)
            local = seg - first[i]
            return (jnp.clip(local, 0, nseg - 1), 0,
                    jnp.where(local < 0, 0, jnp.where(local >= nseg, per_seg - 1, col)))

        return pl.BlockSpec((1, m, CW), index)

    if wgrad:
        other_spec = pl.BlockSpec((k, m), lambda j, b: (0, 0))
        out_spec = pl.BlockSpec((k, CW), lambda j, b: (0, j * nb + b))
        out_shape = jax.ShapeDtypeStruct((k, nj * nb * CW), BF16)
    else:
        other_spec = pl.BlockSpec((1, k, CW), lambda j, b: (j, 0, b))
        out_spec = pl.BlockSpec((m, k), lambda j, b: (0, 0))
        out_shape = jax.ShapeDtypeStruct((m, k), F32)
    return _call(
        body, name=name, grid=(nj, nb), in_specs=[dy_spec(i) for i in range(len(dy))] + [other_spec],
        out_specs=[out_spec], out_shape=[out_shape], operands=(*dy, other), ride=ride)


def norms_mid_bwd(dh2, x2, dy, mix, g3, g2, ride=None):
    s, d = x2.shape
    t = _token_tile(s)

    def body(dh2_ref, x2_ref, dy_ref, mix_ref, g3_ref, g2_ref, dx2_ref, dmix_ref, dg3_ref, dg2_ref):
        @pl.when(pl.program_id(0) == 0)
        def _():
            dg3_ref[...] = jnp.zeros_like(dg3_ref)
            dg2_ref[...] = jnp.zeros_like(dg2_ref)

        n3, r3 = _rms_stats(x2_ref[...])
        dx, dg3 = _rms_bwd(n3, r3, g3_ref[...], dh2_ref[...])
        dx2 = dy_ref[...] + dx
        dx2_ref[...] = dx2
        dg3_ref[...] += jnp.sum(dg3, axis=0, keepdims=True)
        n2, r2 = _rms_stats(mix_ref[...])
        dmix, dg2 = _rms_bwd(n2, r2, g2_ref[...], dx2)
        dmix_ref[...] = dmix.astype(BF16)
        dg2_ref[...] += jnp.sum(dg2, axis=0, keepdims=True)

    tile = pl.BlockSpec((t, d), lambda i: (i, 0))
    vec = pl.BlockSpec((1, d), lambda i: (0, 0))
    v = jax.ShapeDtypeStruct((1, d), F32)
    return _call(
        body, name="norms_mid_bwd", grid=(s // t,),
        in_specs=[tile, tile, tile, tile, vec, vec],
        out_specs=[tile, tile, vec, vec],
        out_shape=[jax.ShapeDtypeStruct((s, d), F32), jax.ShapeDtypeStruct((s, d), BF16), v, v],
        operands=(dh2, x2, dy, mix, g3, g2), ride=ride)


def mix_out_bwd(dmix, wout, merged, a, b, proj, ride=None):
    s = dmix.shape[0]
    nblk = D_MODEL // CW

    def body(dm_ref, w_ref, mg_ref, a_ref, b_ref, gc_ref, gl_ref, da_ref, db_ref, dw_ref, dg_ref):
        dm = dm_ref[...]
        dmerged = _dot_nt(dm, w_ref[...])
        dw_ref[...] = _dot_tn(mg_ref[...], dm).astype(BF16)
        sc = _sigmoid(gc_ref[...])
        sl = _sigmoid(gl_ref[...])
        da_ref[...] = (dmerged * sc).astype(BF16)
        db_ref[...] = (dmerged * sl).astype(BF16)
        dg_ref[0] = (dmerged * a_ref[...] * sc * (1.0 - sc)).astype(BF16)
        dg_ref[1] = (dmerged * b_ref[...] * sl * (1.0 - sl)).astype(BF16)

    res = pl.BlockSpec((s, D_MODEL), lambda n: (0, 0))
    rows = pl.BlockSpec((CW, D_MODEL), lambda n: (n, 0))
    col = pl.BlockSpec((s, CW), lambda n: (0, n))
    blk = lambda k: pl.BlockSpec((s, CW), lambda n, k=k: (0, k * nblk + n))
    hb = jax.ShapeDtypeStruct((s, D_MODEL), BF16)
    return _call(
        body, name="mix_out_bwd", grid=(nblk,),
        in_specs=[res, rows, col, col, col, blk(5), blk(6)],
        out_specs=[col, col, rows, pl.BlockSpec((2, s, CW), lambda n: (0, 0, n))],
        out_shape=[hb, hb, jax.ShapeDtypeStruct((D_MODEL, D_MODEL), BF16), jax.ShapeDtypeStruct((2, s, D_MODEL), BF16)],
        operands=(dmix, wout, merged, a, b, proj, proj), ride=ride)


def mix_conv_bwd(da, wcb, proj, q, ws, ride=None):
    s = da.shape[0]
    nblk = D_MODEL // CW

    def body(da_ref, w_ref, cb_ref, cc_ref, cx_ref, q_ref, ws_ref, dc_ref, dw_ref, dws_ref):
        dab = da_ref[...]
        dya = _dot_nt(dab, w_ref[...])
        cb = cb_ref[...]
        cc = cc_ref[...]
        cx = cx_ref[...]
        q = q_ref[...]
        dw_ref[...] = _dot_tn((cb * q).astype(BF16), dab).astype(BF16)
        dc_ref[0] = (dya * q).astype(BF16)
        dq = _advances(dya * cb, 3)
        dp = _taps_sum(dq, ws_ref)
        dws_ref[...] = _conv_wgrad(dq, cc * cx)
        dc_ref[1] = (dp * cx).astype(BF16)
        dc_ref[2] = (dp * cc).astype(BF16)

    res = pl.BlockSpec((s, D_MODEL), lambda n: (0, 0))
    rows = pl.BlockSpec((CW, D_MODEL), lambda n: (n, 0))
    col = pl.BlockSpec((s, CW), lambda n: (0, n))
    blk = lambda k: pl.BlockSpec((s, CW), lambda n, k=k: (0, k * nblk + n))
    taps = pl.BlockSpec((3, CW), lambda n: (0, n))
    hb = jax.ShapeDtypeStruct((s, D_MODEL), BF16)
    return _call(
        body, name="mix_conv_bwd", grid=(nblk,),
        in_specs=[res, rows, blk(0), blk(1), blk(2), col, taps],
        out_specs=[pl.BlockSpec((3, s, CW), lambda n: (0, 0, n)), rows, taps],
        out_shape=[jax.ShapeDtypeStruct((3, s, D_MODEL), BF16), jax.ShapeDtypeStruct((D_MODEL, D_MODEL), BF16),
                   jax.ShapeDtypeStruct((3, D_MODEL), F32)],
        operands=(da, wcb, proj, proj, proj, q, ws), ride=ride)


def mix_lru_bwd(db, wlb, proj, xl, r, i, h, wl, wa, wx, lam, ride=None):
    s = db.shape[0]
    nblk = D_MODEL // CW

    def body(db_ref, w_ref, lx_ref, ly_ref, xl_ref, r_ref, i_ref, h_ref, wl_ref, wa_ref, wx_ref, lam_ref,
             dl_ref, dw_ref, dwa_ref, dwx_ref, dba_ref, dbx_ref, dwl_ref, dbl_ref, dlam_ref,
             c_scr, g_scr):
        dbb = db_ref[...]
        dyb = _dot_nt(dbb, w_ref[...])
        h = h_ref[...]
        ge, dge = _gelu_and_grad(ly_ref[...])
        dw_ref[...] = _dot_tn((h * ge).astype(BF16), dbb).astype(BF16)
        dl_ref[1] = (dyb * h * dge).astype(BF16)
        r = r_ref[...].astype(F32)
        gi = i_ref[...].astype(F32)
        xlb = xl_ref[...]
        xl = xlb.astype(F32)
        lam = lam_ref[...]
        ls = _log_sigmoid(lam)
        a, mult = _lru_gates(r, ls)
        c_scr[...] = _shift_up(a, 1)
        g_scr[...] = dyb * ge
        _scan_backward(c_scr, g_scr, g_scr)
        du = g_scr[...]
        da = du * _shift_down(h, 1)
        dmult = du * gi * xl
        di = du * mult * xl
        dxl = du * mult * gi
        first = _rows(a.shape) == 0
        dlog_a = da * a - jnp.where(first, 0.0, dmult * a * a / mult)
        dr = dlog_a * (LRU_C * ls)
        dlam_ref[...] = jnp.sum(dlog_a * r, axis=0, keepdims=True) * (LRU_C * (1.0 - _sigmoid(lam)))
        dzr = dr * r * (1.0 - r)
        dzi = di * gi * (1.0 - gi)
        dba_ref[...] = jnp.sum(dzr, axis=0, keepdims=True)
        dbx_ref[...] = jnp.sum(dzi, axis=0, keepdims=True)
        dzrb = dzr.astype(BF16)
        dzib = dzi.astype(BF16)
        dwa_ref[0] = _dot_tn(xlb, dzrb)
        dwx_ref[0] = _dot_tn(xlb, dzib)
        dxl = _advances(dxl + _dot_nt(dzrb, wa_ref[0]) + _dot_nt(dzib, wx_ref[0]), 4)
        dl_ref[0] = _taps_sum(dxl, wl_ref).astype(BF16)
        dwl_ref[...] = _conv_wgrad(dxl, lx_ref[...])
        dbl_ref[...] = jnp.sum(dxl[0], axis=0, keepdims=True)

    res = pl.BlockSpec((s, D_MODEL), lambda n: (0, 0))
    rows = pl.BlockSpec((CW, D_MODEL), lambda n: (n, 0))
    col = pl.BlockSpec((s, CW), lambda n: (0, n))
    blk = lambda k: pl.BlockSpec((s, CW), lambda n, k=k: (0, k * nblk + n))
    taps = pl.BlockSpec((4, CW), lambda n: (0, n))
    vec = pl.BlockSpec((1, CW), lambda n: (0, n))
    mat = pl.BlockSpec((1, CW, CW), lambda n: (n, 0, 0))
    hb = jax.ShapeDtypeStruct((s, D_MODEL), BF16)
    v = jax.ShapeDtypeStruct((1, D_MODEL), F32)
    m = jax.ShapeDtypeStruct((LRU_HEADS, HEAD_DIM, HEAD_DIM), F32)
    scr = pltpu.VMEM((s, CW), F32)
    return _call(
        body, name="mix_lru_bwd", grid=(nblk,),
        in_specs=[res, rows, blk(3), blk(4), col, col, col, col, taps, mat, mat, vec],
        out_specs=[pl.BlockSpec((2, s, CW), lambda n: (0, 0, n)), rows, mat, mat, vec, vec, taps, vec, vec],
        out_shape=[jax.ShapeDtypeStruct((2, s, D_MODEL), BF16), jax.ShapeDtypeStruct((D_MODEL, D_MODEL), BF16), m, m, v, v,
                   jax.ShapeDtypeStruct((4, D_MODEL), F32), v, v],
        scratch_shapes=[scr, scr],
        operands=(db, wlb, proj, proj, xl, r, i, h, wl, wa, wx, lam), ride=ride)


def norm_in_bwd(dh1, x, dx2, g1, ride=None):
    s, d = x.shape
    t = _token_tile(s)

    def body(dh_ref, x_ref, dx2_ref, g_ref, dx_ref, dg_ref):
        @pl.when(pl.program_id(0) == 0)
        def _():
            dg_ref[...] = jnp.zeros_like(dg_ref)

        n, r = _rms_stats(x_ref[...])
        dx, dg = _rms_bwd(n, r, g_ref[...], dh_ref[...])
        dx_ref[...] = dx2_ref[...] + dx
        dg_ref[...] += jnp.sum(dg, axis=0, keepdims=True)

    tile = pl.BlockSpec((t, d), lambda i: (i, 0))
    vec = pl.BlockSpec((1, d), lambda i: (0, 0))
    return _call(
        body, name="norm_in_bwd", grid=(s // t,),
        in_specs=[tile, tile, tile, vec],
        out_specs=[tile, vec],
        out_shape=[jax.ShapeDtypeStruct((s, d), F32), jax.ShapeDtypeStruct((1, d), F32)],
        operands=(dh1, x, dx2, g1), ride=ride)


def _owned_part(ref, kind, k, h, hr):
    if kind == "col":
        ns = ref.shape[1] // N_CHIPS
        return ref.at[pl.ds(h * hr, hr), pl.ds(k * ns, ns)]
    if kind == "row":
        return ref.at[pl.ds(k * 2 * hr + h * hr, hr), :]
    if kind == "col2":
        ns = ref.shape[2] // 2
        return ref.at[k // 2, pl.ds(h * hr, hr), pl.ds((k % 2) * ns, ns)]
    return ref.at[k, pl.ds(h * hr, hr), :]


def _part_shape(g, kind):
    if kind == "col2":
        return g.shape[1] // 2, g.shape[2] // 2
    if kind == "col":
        return g.shape[0] // 2, g.shape[1] // N_CHIPS
    if kind == "row":
        return g.shape[0] // (2 * N_CHIPS), g.shape[1]
    return g.shape[1] // 2, g.shape[2]


def pair_split(grads, kinds, name):
    n = len(grads)
    shapes = [_part_shape(g, k) for g, k in zip(grads, kinds)]

    def body(*refs):
        ins, theirs = refs[:n], refs[n:2 * n]
        send_sem, recv_sem = refs[2 * n:]
        x, y, c = _position()
        copies = []
        for a in range(n):
            hr = shapes[a][0]
            for k in range(N_CHIPS):
                s = a * N_CHIPS + k
                copies.append(pltpu.make_async_remote_copy(
                    src_ref=_owned_part(ins[a], kinds[a], k, 1 - c, hr), dst_ref=theirs[a].at[k],
                    send_sem=send_sem.at[s], recv_sem=recv_sem.at[s], device_id=(x, y, 1 - c), device_id_type=MESH))
        _handshake([(x, y, 1 - c)])
        for cp in copies:
            cp.start()
        for cp in copies:
            cp.wait()

    return pl.pallas_call(
        body, name=name,
        in_specs=[_HBM] * n, out_specs=[_HBM] * n,
        out_shape=[jax.ShapeDtypeStruct((N_CHIPS,) + shp, g.dtype) for shp, g in zip(shapes, grads)],
        scratch_shapes=[pltpu.SemaphoreType.DMA((n * N_CHIPS,))] * 2,
        compiler_params=pltpu.CompilerParams(collective_id=SIBLING),
    )(*grads)


def pair_swap(halves):
    n = len(halves)

    def body(*refs):
        ins, outs = refs[:n], refs[n:2 * n]
        send_sem, recv_sem = refs[2 * n:]
        x, y, c = _position()
        copies = [pltpu.make_async_remote_copy(
            src_ref=ins[a], dst_ref=outs[a], send_sem=send_sem.at[a], recv_sem=recv_sem.at[a],
            device_id=(x, y, 1 - c), device_id_type=MESH) for a in range(n)]
        _handshake([(x, y, 1 - c)])
        for cp in copies:
            cp.start()
        for cp in copies:
            cp.wait()

    return pl.pallas_call(
        body, name="pair_swap",
        in_specs=[_HBM] * n, out_specs=[_HBM] * n,
        out_shape=[jax.ShapeDtypeStruct(h.shape, h.dtype) for h in halves],
        scratch_shapes=[pltpu.SemaphoreType.DMA((n,))] * 2,
        compiler_params=pltpu.CompilerParams(collective_id=SIBLING),
    )(*halves)


def _row_tile(rows, cols, limit_bytes=1 << 20):
    best = None
    for t in range(SUBLANES, rows + 1, SUBLANES):
        if rows % t == 0 and t * cols * 4 <= limit_bytes:
            best = t
    return best or rows


def add_pair(g, kind, theirs, core, name):
    nc, rows, cols = theirs.shape
    t = _row_tile(rows, cols, 4 << 20)
    nt = rows // t

    def body(core_ref, g_ref, b_ref, o_ref):
        mine = g_ref[...].reshape(t, cols)
        o_ref[0] = (mine.astype(F32) + b_ref[0].astype(F32)).astype(o_ref.dtype)

    if kind == "col":
        own = pl.BlockSpec((t, cols), lambda k, i, c: (c[0] * nt + i, k))
    elif kind == "col2":
        own = pl.BlockSpec((1, t, cols), lambda k, i, c: (k // 2, c[0] * nt + i, k % 2))
    elif kind == "row":
        own = pl.BlockSpec((t, cols), lambda k, i, c: ((2 * k + c[0]) * nt + i, 0))
    else:
        own = pl.BlockSpec((1, t, cols), lambda k, i, c: (k, c[0] * nt + i, 0))
    spec = pl.BlockSpec((1, t, cols), lambda k, i, c: (k, i, 0))
    return pl.pallas_call(
        body, name=name,
        grid_spec=pltpu.PrefetchScalarGridSpec(num_scalar_prefetch=1, grid=(nc, nt), in_specs=[own, spec], out_specs=spec),
        out_shape=jax.ShapeDtypeStruct(theirs.shape, theirs.dtype), compiler_params=_params(),
    )(core, g, theirs)


def sum_lead(a, name):
    nl, rows, cols = a.shape
    t = _row_tile(rows, cols, (1 << 20) // 2)

    def body(a_ref, o_ref):
        acc = a_ref[0].astype(F32)
        for s in range(1, nl):
            acc = acc + a_ref[s].astype(F32)
        o_ref[...] = acc

    return pl.pallas_call(
        body, name=name, grid=(rows // t,),
        in_specs=[pl.BlockSpec((nl, t, cols), lambda i: (0, i, 0))],
        out_specs=pl.BlockSpec((t, cols), lambda i: (i, 0)),
        out_shape=jax.ShapeDtypeStruct((rows, cols), F32), compiler_params=_params(),
    )(a)


def sum_chips(rx, csum, chip, name):
    nc, rows, cols = rx.shape
    t = _row_tile(rows, cols, 2 << 20)

    def body(chip_ref, r0, r1, r2, r3, own_ref, o_ref):
        acc = None
        for s, ref in enumerate((r0, r1, r2, r3)):
            term = jnp.where(chip_ref[0] == s, own_ref[0], ref[0]).astype(F32)
            acc = term if acc is None else acc + term
        o_ref[...] = acc

    def slot(s):
        return pl.BlockSpec((1, t, cols), lambda i, c, s=s: (jnp.where(c[0] == s, c[0] ^ 1, s), i, 0))

    return pl.pallas_call(
        body, name=name,
        grid_spec=pltpu.PrefetchScalarGridSpec(
            num_scalar_prefetch=1, grid=(rows // t,),
            in_specs=[slot(s) for s in range(nc)] + [pl.BlockSpec((1, t, cols), lambda i, c: (c[0], i, 0))],
            out_specs=pl.BlockSpec((t, cols), lambda i, c: (i, 0))),
        out_shape=jax.ShapeDtypeStruct((rows, cols), F32), compiler_params=_params(),
    )(chip, rx, rx, rx, rx, csum)


def cast_bf16(a, name):
    rows, cols = a.shape
    t = _row_tile(rows, cols, 2 << 20)

    def body(i_ref, o_ref):
        o_ref[...] = i_ref[...].astype(BF16)

    spec = pl.BlockSpec((t, cols), lambda i: (i, 0))
    return pl.pallas_call(body, name=name, grid=(rows // t,), in_specs=[spec], out_specs=spec,
                          out_shape=jax.ShapeDtypeStruct((rows, cols), BF16), compiler_params=_params())(a)


def _adamw_update(w, g, m, v):
    nm = ADAM_B1 * m + (1.0 - ADAM_B1) * g
    nv = ADAM_B2 * v + (1.0 - ADAM_B2) * (g * g)
    m_hat = nm * (1.0 / (1.0 - ADAM_B1 ** ADAM_STEP))
    v_hat = nv * (1.0 / (1.0 - ADAM_B2 ** ADAM_STEP))
    return -ADAM_LR * (m_hat / (jnp.sqrt(v_hat) + ADAM_EPS) + ADAM_WD * w), nm, nv


def adamw(w, g, m, v, name):
    rows, cols = w.shape
    t = _row_tile(rows, cols)

    def body(w_ref, g_ref, m_ref, v_ref, d_ref, nm_ref, nv_ref):
        d_ref[...], nm_ref[...], nv_ref[...] = _adamw_update(w_ref[...], g_ref[...], m_ref[...], v_ref[...])

    spec = pl.BlockSpec((t, cols), lambda i: (i, 0))
    shp = jax.ShapeDtypeStruct((rows, cols), F32)
    return pl.pallas_call(
        body, name=name, grid=(rows // t,), in_specs=[spec] * 4, out_specs=[spec] * 3,
        out_shape=[shp, shp, shp], compiler_params=_params(),
    )(w, g, m, v)


def adamw_halves(w, g_mine, g_other, m, v, core, name):
    rows, cols = w.shape
    hr = rows // 2
    t = _row_tile(hr, cols)
    nt = hr // t

    def body(core_ref, w_ref, gm_ref, go_ref, m_ref, v_ref, g_ref, d_ref, nm_ref, nv_ref):
        g = jnp.where(pl.program_id(0) // nt == core_ref[0], gm_ref[...], go_ref[...])
        g_ref[...] = g
        d_ref[...], nm_ref[...], nv_ref[...] = _adamw_update(w_ref[...], g, m_ref[...], v_ref[...])

    spec = pl.BlockSpec((t, cols), lambda i, c: (i, 0))
    half = pl.BlockSpec((t, cols), lambda i, c: (i % nt, 0))
    shp = jax.ShapeDtypeStruct((rows, cols), F32)
    return pl.pallas_call(
        body, name=name,
        grid_spec=pltpu.PrefetchScalarGridSpec(num_scalar_prefetch=1, grid=(2 * nt,),
                                               in_specs=[spec, half, half, spec, spec], out_specs=[spec] * 4),
        out_shape=[shp] * 4, compiler_params=_params(),
    )(core, w, g_mine, g_other, m, v)


WEIGHTS = ("norm_mix_pre", "norm_mix_post", "norm_ffn_pre", "norm_ffn_post", "w_in", "conv_short_w",
           "w_conv_branch", "lru_conv_w", "lru_conv_b", "lru_wa", "lru_ba", "lru_wx", "lru_bx", "lru_lambda",
           "w_lru_branch", "w_out", "ffn_w_up", "ffn_conv_w", "ffn_conv_b", "ffn_w_down")
BIG = ("w_in", "ffn_w_up", "w_conv_branch", "w_lru_branch", "w_out", "ffn_w_down")
BIG_KIND = ("col", "col", "row", "row", "row", "row")
SMALL = ("conv_short_w", "lru_conv_w", "lru_wa", "lru_ba", "lru_wx", "lru_bx", "ffn_conv_w")
REPL = ("norm_mix_pre", "norm_mix_post", "norm_ffn_pre", "norm_ffn_post", "lru_conv_b", "lru_lambda", "ffn_conv_b")
PACK_W = 256
SMALL_ROWS = 576
REPL_ROWS = 16
LOSS_ROW = 12
FFN_SHARD = 2 * D_FF // N_CHIPS
QUARTER = HEAD_DIM // N_CHIPS
SMALL_PARTS = (("conv_short_w", 3, (1, 3, PACK_W)), ("lru_conv_w", 4, (1, 4, PACK_W)),
               ("lru_wa", LRU_HEADS * QUARTER, (1, LRU_HEADS, QUARTER, HEAD_DIM)), ("lru_ba", LRU_HEADS, (1, LRU_HEADS, QUARTER)),
               ("lru_wx", LRU_HEADS * QUARTER, (1, LRU_HEADS, QUARTER, HEAD_DIM)), ("lru_bx", LRU_HEADS, (1, LRU_HEADS, QUARTER)),
               ("ffn_conv_w", 3 * FFN_SHARD // PACK_W, (1, 3, FFN_SHARD)))


def _pad8(nr):
    return -(-nr // SUBLANES) * SUBLANES


SMALL_OFFSET = {}
for _name, _nr, _ in SMALL_PARTS:
    SMALL_OFFSET[_name] = sum(_pad8(nr) for n, nr, _ in SMALL_PARTS[:len(SMALL_OFFSET)])
FFN_ROWS = FFN_SHARD // PACK_W
BIASES = ("lru_ba", "lru_bx")
TAPS3 = ("conv_short_w", "ffn_conv_w")


def pack_small(dicts):
    names = [n for n, _, _ in SMALL_PARTS]
    operands = [d[n].transpose(1, 0, 2) if n in TAPS3 else d[n] for d in dicts for n in names]

    def body(*refs):
        ins, outs = refs[:len(operands)], refs[len(operands):]
        for i, o in enumerate(outs):
            o[...] = jnp.zeros_like(o)
            for (name, nr, shape), p in zip(SMALL_PARTS, ins[i * len(names):(i + 1) * len(names)]):
                r0 = SMALL_OFFSET[name]
                if name in BIASES:
                    o[r0:r0 + nr, 0:QUARTER] = p[0]
                elif name == "ffn_conv_w":
                    for k in range(shape[1]):
                        for s in range(FFN_ROWS):
                            o[r0 + FFN_ROWS * k + s:r0 + FFN_ROWS * k + s + 1, :] = p[k, :, s * PACK_W:(s + 1) * PACK_W]
                elif name == "conv_short_w":
                    for k in range(nr):
                        o[r0 + k:r0 + k + 1, :] = p[k]
                else:
                    o[r0:r0 + nr, :] = p[0].reshape(nr, PACK_W)

    shape = jax.ShapeDtypeStruct((SMALL_ROWS, PACK_W), F32)
    return pl.pallas_call(body, name="pack_small", out_shape=[shape] * len(dicts), compiler_params=_params())(*operands)


def full_small(g4):
    def body(p, csw, lcw, wa, wx, fcw):
        chips = range(N_CHIPS)
        r0 = SMALL_OFFSET["conv_short_w"]
        csw[...] = jnp.concatenate([p[c, r0:r0 + 3, :] for c in chips], axis=1)
        r0 = SMALL_OFFSET["lru_conv_w"]
        lcw[...] = jnp.concatenate([p[c, r0:r0 + 4, :] for c in chips], axis=1)
        for name, o in (("lru_wa", wa), ("lru_wx", wx)):
            r0 = SMALL_OFFSET[name]
            for h in range(LRU_HEADS):
                for c in chips:
                    o[h, c * QUARTER:(c + 1) * QUARTER, :] = p[c, r0 + h * QUARTER:r0 + (h + 1) * QUARTER, :].astype(BF16)
        r0 = SMALL_OFFSET["ffn_conv_w"]
        for k in range(3):
            fcw[k:k + 1, :] = jnp.concatenate(
                [p[c, r0 + FFN_ROWS * k + s:r0 + FFN_ROWS * k + s + 1, :] for c in chips for s in range(FFN_ROWS)], axis=1)

    mat = jax.ShapeDtypeStruct((LRU_HEADS, HEAD_DIM, HEAD_DIM), BF16)
    csw, lcw, wa, wx, fcw = pl.pallas_call(
        body, name="full_small",
        out_shape=[jax.ShapeDtypeStruct((3, D_MODEL), F32), jax.ShapeDtypeStruct((4, D_MODEL), F32), mat, mat,
                   jax.ShapeDtypeStruct((3, 2 * D_FF), F32)],
        compiler_params=_params())(g4)

    def bias(name):
        r0 = SMALL_OFFSET[name]
        return g4[:, r0:r0 + LRU_HEADS, :QUARTER].transpose(1, 0, 2).reshape(1, D_MODEL)

    return dict(conv_short_w=csw, lru_conv_w=lcw, lru_wa=wa, lru_wx=wx, ffn_conv_w=fcw,
                lru_ba=bias("lru_ba"), lru_bx=bias("lru_bx"))


def split_small(full):
    def bias(name):
        return full[name].reshape(LRU_HEADS, N_CHIPS, QUARTER).transpose(1, 0, 2)

    def body(csw, lcw, wa, wx, fcw, ba, bx, o):
        o[...] = jnp.zeros_like(o)
        for c in range(N_CHIPS):
            cols = slice(c * PACK_W, (c + 1) * PACK_W)
            r0 = SMALL_OFFSET["conv_short_w"]
            o[c, r0:r0 + 3, :] = csw[:, cols]
            r0 = SMALL_OFFSET["lru_conv_w"]
            o[c, r0:r0 + 4, :] = lcw[:, cols]
            for name, p in (("lru_wa", wa), ("lru_wx", wx)):
                r0 = SMALL_OFFSET[name]
                for h in range(LRU_HEADS):
                    o[c, r0 + h * QUARTER:r0 + (h + 1) * QUARTER, :] = p[h, c * QUARTER:(c + 1) * QUARTER, :]
            for name, p in (("lru_ba", ba), ("lru_bx", bx)):
                r0 = SMALL_OFFSET[name]
                o[c, r0:r0 + LRU_HEADS, 0:QUARTER] = p[c]
            r0 = SMALL_OFFSET["ffn_conv_w"]
            for k in range(3):
                for s in range(FFN_ROWS):
                    lo = c * FFN_SHARD + s * PACK_W
                    o[c, r0 + FFN_ROWS * k + s:r0 + FFN_ROWS * k + s + 1, :] = fcw[k:k + 1, lo:lo + PACK_W]

    return pl.pallas_call(
        body, name="split_small", out_shape=jax.ShapeDtypeStruct((N_CHIPS, SMALL_ROWS, PACK_W), F32),
        compiler_params=_params(),
    )(full["conv_short_w"], full["lru_conv_w"], full["lru_wa"], full["lru_wx"], full["ffn_conv_w"],
      bias("lru_ba"), bias("lru_bx"))


def pack_repl(dicts, loss=None):
    operands = [d[n] for d in dicts for n in REPL] + ([loss] if loss is not None else [])

    def body(*refs):
        ins, outs = refs[:len(operands)], refs[len(operands):]
        for i, o in enumerate(outs):
            o[...] = jnp.zeros_like(o)
            r0 = 0
            for p in ins[i * len(REPL):(i + 1) * len(REPL)]:
                for s in range(p.shape[1] // D_MODEL):
                    o[r0:r0 + 1, :] = p[:, s * D_MODEL:(s + 1) * D_MODEL]
                    r0 += 1
        if loss is not None:
            outs[-1][LOSS_ROW:LOSS_ROW + 1, :] = jnp.tile(ins[-1][...], (1, D_MODEL // 128))

    shape = jax.ShapeDtypeStruct((REPL_ROWS, D_MODEL), F32)
    return pl.pallas_call(body, name="pack_repl" + ("_loss" if loss is not None else ""),
                          out_shape=[shape] * len(dicts), compiler_params=_params())(*operands)


def _lane_concat(ref, r0, n):
    return jnp.concatenate([ref[r0 + s:r0 + s + 1, :] for s in range(n)], axis=1)


def unpack_small(packs):
    names = [n for n, _, _ in SMALL_PARTS]

    def body(*refs):
        ins, outs = refs[:len(packs)], refs[len(packs):]
        for i, p in enumerate(ins):
            for (name, nr, shape), o in zip(SMALL_PARTS, outs[i * len(names):(i + 1) * len(names)]):
                r0 = SMALL_OFFSET[name]
                if name in BIASES:
                    o[0] = p[r0:r0 + nr, 0:QUARTER]
                elif name == "ffn_conv_w":
                    for k in range(shape[1]):
                        o[k] = _lane_concat(p, r0 + FFN_ROWS * k, FFN_ROWS)
                elif name == "conv_short_w":
                    for k in range(nr):
                        o[k] = p[r0 + k:r0 + k + 1, :]
                else:
                    o[0] = p[r0:r0 + nr, :].reshape(shape[1:])

    shapes = [jax.ShapeDtypeStruct((s[1], 1, s[2]) if n in TAPS3 else s, F32) for n, _, s in SMALL_PARTS]
    res = pl.pallas_call(body, name="unpack_small", out_shape=shapes * len(packs), compiler_params=_params())(*packs)
    out = []
    for i in range(len(packs)):
        d = dict(zip(names, res[i * len(names):(i + 1) * len(names)]))
        for n in TAPS3:
            d[n] = d[n].transpose(1, 0, 2)
        out.append(d)
    return out


def unpack_repl(packs):
    rows = [(2 * D_FF // D_MODEL) if n == "ffn_conv_b" else 1 for n in REPL]

    def body(*refs):
        ins, outs = refs[:len(packs)], refs[len(packs):]
        for i, p in enumerate(ins):
            r0 = 0
            for nr, o in zip(rows, outs[i * len(REPL):(i + 1) * len(REPL)]):
                o[...] = _lane_concat(p, r0, nr)
                r0 += nr

    shapes = [jax.ShapeDtypeStruct((1, nr * D_MODEL), F32) for nr in rows]
    res = pl.pallas_call(body, name="unpack_repl", out_shape=shapes * len(packs), compiler_params=_params())(*packs)
    return [dict(zip(REPL, res[i * len(REPL):(i + 1) * len(REPL)])) for i in range(len(packs))]


def kernel(x, norm_mix_pre, norm_mix_post, norm_ffn_pre, norm_ffn_post, w_in, conv_short_w, w_conv_branch, lru_conv_w, lru_conv_b, lru_wa, lru_ba, lru_wx, lru_bx, lru_lambda, w_lru_branch, w_out, ffn_w_up, ffn_conv_w, ffn_conv_b, ffn_w_down, loss_target, m_norm_mix_pre, m_norm_mix_post, m_norm_ffn_pre, m_norm_ffn_post, m_w_in, m_conv_short_w, m_w_conv_branch, m_lru_conv_w, m_lru_conv_b, m_lru_wa, m_lru_ba, m_lru_wx, m_lru_bx, m_lru_lambda, m_w_lru_branch, m_w_out, m_ffn_w_up, m_ffn_conv_w, m_ffn_conv_b, m_ffn_w_down, v_norm_mix_pre, v_norm_mix_post, v_norm_ffn_pre, v_norm_ffn_post, v_w_in, v_conv_short_w, v_w_conv_branch, v_lru_conv_w, v_lru_conv_b, v_lru_wa, v_lru_ba, v_lru_wx, v_lru_bx, v_lru_lambda, v_w_lru_branch, v_w_out, v_ffn_w_up, v_ffn_conv_w, v_ffn_conv_b, v_ffn_w_down):
    given = dict(locals())
    w = {n: given[n] for n in WEIGHTS}
    m = {n: given["m_" + n] for n in WEIGHTS}
    v = {n: given["v_" + n] for n in WEIGHTS}

    xi, yi, ci = _position()
    chip_i = 2 * xi + yi
    chip = chip_i.astype(jnp.int32).reshape(1)
    core = ci.astype(jnp.int32).reshape(1)
    xs, target = x[0], loss_target[0]
    g1, g2, g3, g4 = w["norm_mix_pre"], w["norm_mix_post"], w["norm_ffn_pre"], w["norm_ffn_post"]
    shard = {n: cast_bf16(w[n][0], "cast_" + n) for n in BIG}
    small_shard, m_small, v_small = pack_small([w, m, v])

    def gathered(bufs, names):
        return [_own_slot(b, small_shard if n == "small" else shard[n], chip_i) for b, n in zip(bufs, names)]

    def chip_sums(arrays, kinds, tag):
        theirs = pair_split(arrays, kinds, "pair_split_" + tag)
        return [add_pair(g, k, t, core, "pair_add_%s_%d" % (tag, i)) for i, (g, k, t) in enumerate(zip(arrays, kinds, theirs))]

    h1, h1t = norm_in(xs, g1)
    half_in = D_MODEL // 2
    first = gather_ride([shard["w_in"], small_shard], items=[(0, 0, half_in), (0, half_in, half_in), (1, 0, SMALL_ROWS)])
    win4, small4 = gathered(run_ride(first, "gather_first"), ("w_in", "small"))
    small = full_small(small4)
    first_up = 256
    (proj,), got = matmul_cols(
        h1, win4, "proj_fwd",
        ride=gather_ride([shard["w_conv_branch"], shard["w_lru_branch"], shard["w_out"], shard["ffn_w_up"]],
                         items=[(0, 0, 256), (1, 0, 256), (2, 0, 256), (3, 0, first_up)]))
    wcb, wlb, wout = [g.reshape(-1, D_MODEL) for g in gathered(got[:3], ("w_conv_branch", "w_lru_branch", "w_out"))]
    got = got[3:]
    up_piece = lambda r0, nr, into=None: gather_ride([shard["ffn_w_up"]], items=[(0, r0, nr)], into=into)
    down_piece = lambda r0, nr, into=None: gather_ride([shard["ffn_w_down"]], items=[(0, r0, nr)], into=into)
    q, ya = mix_conv_fwd(proj, small["conv_short_w"])
    (xl, r, gi, h, yb), got = mix_lru_fwd(
        proj, small["lru_conv_w"], w["lru_conv_b"], small["lru_wa"], small["lru_ba"],
        small["lru_wx"], small["lru_bx"], w["lru_lambda"], ride=up_piece(first_up, 512, got))
    (a, b, merged), got = branch_merge_fwd(ya, yb, wcb, wlb, proj, ride=up_piece(first_up + 512, 256, got))
    (wup4,) = gathered(got, ("ffn_w_up",))
    (mix, x2, h2, h2t), got = mix_out_fwd(merged, wout, xs, g2, g3, ride=down_piece(0, 256))
    (up, act, f), got = ffn_up_act_fwd(h2, wup4, small["ffn_conv_w"], w["ffn_conv_b"], ride=down_piece(256, 512, got))
    wdown = gathered(got, ("ffn_w_down",))[0].reshape(-1, D_MODEL)
    dy, dout, loss, dg4 = ffn_down_loss(f, wdown, x2, target, g4)

    dh2, dwup, dwdown, dfw, dfb = ffn_up_bwd(dout, wdown, up, act, f, small["ffn_conv_w"], wup4, h2t)
    cs_down, cs_up = chip_sums([dwdown, dwup], ["row", "col2"], "ffn")
    down_rows = lambda r0, nr, into=None: exchange_ride([cs_down], items=[(0, r0, nr)], into=into)
    up_rows = lambda r0, nr, into=None: exchange_ride([cs_up], items=[(0, r0, nr)], into=into)
    (dx2, dmix, dg3, dg2), rx_down = norms_mid_bwd(dh2, x2, dy, mix, g3, g2, ride=down_rows(0, 128))
    (da, db, dwout, dgates), rx_down = mix_out_bwd(dmix, wout, merged, a, b, proj, ride=down_rows(128, 256, rx_down))
    (dconv, dwcb, dws), rx_up = mix_conv_bwd(da, wcb, proj, q, small["conv_short_w"], ride=up_rows(0, 176))
    cs_mid = chip_sums([dwout, dwcb], ["row", "row"], "mid")
    (dlru, dwlb, dwa, dwx, dba, dbx, dwl, dbl, dlam), rx_up = mix_lru_bwd(
        db, wlb, proj, xl, r, gi, h, small["lru_conv_w"], small["lru_wa"], small["lru_wx"], w["lru_lambda"],
        ride=up_rows(176, 336, rx_up))
    grads = dict(norm_mix_post=dg2, norm_ffn_pre=dg3, norm_ffn_post=dg4, conv_short_w=dws, lru_conv_w=dwl,
                 lru_conv_b=dbl, lru_wa=dwa, lru_ba=dba, lru_wx=dwx, lru_bx=dbx, lru_lambda=dlam,
                 ffn_conv_w=jnp.concatenate([dfw[0], dfw[1]], axis=1), ffn_conv_b=jnp.concatenate([dfb[0], dfb[1]], axis=1))
    cs_late = chip_sums([dwlb, split_small(grads)], ["row", "lead"], "late")
    dproj = [dconv, dlru, dgates]
    (dwin,), rx_all = matmul_cols_bwd(dproj, h1t, "proj_wgrad", True, ride=exchange_ride(cs_mid + cs_late))
    rx_mid, rx_late = rx_all[:2], rx_all[2:]
    cs_in = chip_sums([dwin], ["col"], "in")
    in_rows = lambda r0, nr, into=None: exchange_ride(cs_in, items=[(0, r0, nr)], into=into)
    (dh1,), rx_in = matmul_cols_bwd(dproj, win4, "proj_dgrad", False, ride=in_rows(0, 384))
    dx, grads["norm_mix_pre"] = norm_in_bwd(dh1, xs, dx2, g1)
    (rep_part,) = pack_repl([grads], loss)
    rx_in, rep_all = run_ride(exchange_ride(cs_in, items=[(0, 384, 128)], into=rx_in, rep=rep_part), "exchange_last")

    order = (("w_in", cs_in[0], rx_in), ("ffn_w_up", cs_up, rx_up[0]), ("w_conv_branch", cs_mid[1], rx_mid[1]),
             ("w_lru_branch", cs_late[0], rx_late[0]), ("w_out", cs_mid[0], rx_mid[0]),
             ("ffn_w_down", cs_down, rx_down[0]), ("small", cs_late[1], rx_late[1]))
    halves = [sum_chips(rx, cs, chip, "chip_sum_" + n) for n, cs, rx in order]
    me = 4 * xi + 2 * yi + ci
    rep_grad = sum_lead(_own_slot(rep_all, rep_part, me), "device_sum")
    others = pair_swap(halves)

    g_out, d_out, m_out, v_out = {}, {}, {}, {}
    for n, gm, go in zip(BIG, halves[:-1], others[:-1]):
        g, d, nm, nv = adamw_halves(w[n][0], gm, go, m[n][0], v[n][0], core, "adamw_" + n)
        g_out[n], d_out[n], m_out[n], v_out[n] = g[None], d[None], nm[None], nv[None]
    bufs = adamw_halves(small_shard, halves[-1], others[-1], m_small, v_small, core, "adamw_small")
    for dst, part in zip((g_out, d_out, m_out, v_out), unpack_small(bufs)):
        dst.update(part)
    w_rep, m_rep, v_rep = pack_repl([w, m, v])
    d, nm, nv = adamw(w_rep, rep_grad, m_rep, v_rep, "adamw_repl")
    for dst, part in zip((g_out, d_out, m_out, v_out), unpack_repl([rep_grad, d, nm, nv])):
        dst.update(part)

    return (rep_grad[LOSS_ROW, 0], dx[None], *[g_out[n] for n in WEIGHTS], *[d_out[n] for n in WEIGHTS],
            *[m_out[n] for n in WEIGHTS], *[v_out[n] for n in WEIGHTS])
```

```python
import functools
import math

import jax
import jax.numpy as jnp
from jax import lax
from jax.experimental import pallas as pl
from jax.experimental.pallas import tpu as pltpu

F32 = jnp.float32
BF16 = jnp.bfloat16

D_MODEL = 1024
N_CHIPS = 4
N_SEG = 7
D_FF = 3 * D_MODEL
LRU_HEADS = 4
HEAD_DIM = D_MODEL // LRU_HEADS
LRU_C = 8.0
RMS_EPS = 1e-6
CW = 256
FW = 256
UP_W = 512
SUBLANES = 8
SCAN_UNROLL = 8
VMEM_LIMIT = 58 * 1024 * 1024

ADAM_LR = 0.001
ADAM_B1 = 0.9
ADAM_B2 = 0.999
ADAM_EPS = 1e-08
ADAM_WD = 0.01
ADAM_STEP = 10

_GELU_C = math.sqrt(2.0 / math.pi)
_GELU_K = 0.044715


def _params(**kw):
    return pltpu.CompilerParams(vmem_limit_bytes=VMEM_LIMIT, **kw)


def _sigmoid(x):
    return 1.0 / (1.0 + jnp.exp(-x))


def _gelu(x):
    t = jnp.tanh(_GELU_C * (x + _GELU_K * x * x * x))
    return 0.5 * x * (1.0 + t)


def _gelu_and_grad(x):
    x2 = x * x
    t = jnp.tanh(_GELU_C * (x + _GELU_K * x * x2))
    g = 0.5 * x * (1.0 + t)
    dg = 0.5 * (1.0 + t) + 0.5 * x * (1.0 - t * t) * _GELU_C * (1.0 + 3.0 * _GELU_K * x2)
    return g, dg


def _log_sigmoid(x):
    e = jnp.exp(-jnp.abs(x))
    u = 1.0 + e
    l1p = jnp.where(u == 1.0, e, jnp.log(u) * e / (u - 1.0))
    return jnp.minimum(x, 0.0) - l1p


def _neg_expm1(z):
    series = -z * (1.0 + z * (0.5 + z * (1.0 / 6.0 + z * (1.0 / 24.0 + z * (1.0 / 120.0 + z * (1.0 / 720.0))))))
    return jnp.where(z > -0.2, series, 1.0 - jnp.exp(z))


def _rows(shape):
    return lax.broadcasted_iota(jnp.int32, shape, 0)


def _shift_down(x, k):
    return jnp.where(_rows(x.shape) >= k, pltpu.roll(x, k, 0), 0.0)


def _shift_up(x, k):
    n = x.shape[0]
    return jnp.where(_rows(x.shape) < n - k, pltpu.roll(x, n - k, 0), 0.0)


def _delays(x, k_width):
    return [x] + [_shift_down(x, j) for j in range(1, k_width)]


def _advances(dy, k_width):
    return [dy] + [_shift_up(dy, j) for j in range(1, k_width)]


def _taps_sum(shifted, w_ref, b=None):
    k_width = w_ref.shape[0]
    y = w_ref[k_width - 1:k_width, :] * shifted[0]
    for j in range(1, k_width):
        y = y + w_ref[k_width - 1 - j:k_width - j, :] * shifted[j]
    if b is not None:
        y = y + b
    return y


def _causal_conv(x, w_ref, b=None):
    return _taps_sum(_delays(x, w_ref.shape[0]), w_ref, b)


def _conv_wgrad(advanced, x):
    k_width = len(advanced)
    rows = [jnp.sum(advanced[k_width - 1 - k] * x, axis=0, keepdims=True) for k in range(k_width)]
    return jnp.concatenate(rows, axis=0)


def _dot(a, b):
    return jnp.dot(a, b, preferred_element_type=F32)


def _dot_nt(a, b):
    return lax.dot_general(a, b, (((1,), (1,)), ((), ())), preferred_element_type=F32)


def _dot_tn(a, b):
    return lax.dot_general(a, b, (((0,), (0,)), ((), ())), preferred_element_type=F32)


def _rms_stats(x):
    r = lax.rsqrt(jnp.mean(x * x, axis=-1, keepdims=True) + RMS_EPS)
    return x * r, r


def _rms_bwd(n, r, g, dy):
    dn = dy * g
    dx = r * (dn - n * jnp.mean(dn * n, axis=-1, keepdims=True))
    return dx, dy * n


def _scan(a_ref, b_ref, h_ref, reverse):
    n, c = a_ref.shape
    row = lax.broadcasted_iota(jnp.int32, (SUBLANES, c), 0)
    span = SCAN_UNROLL * SUBLANES
    n_trips = n // span

    def within(a, b):
        for k in (1, 2, 4):
            if reverse:
                keep, shift = row < SUBLANES - k, SUBLANES - k
            else:
                keep, shift = row >= k, k
            ap = jnp.where(keep, pltpu.roll(a, shift, 0), 1.0)
            bp = jnp.where(keep, pltpu.roll(b, shift, 0), 0.0)
            b = a * bp + b
            a = a * ap
        return a, b

    def trip(t, carry):
        base = pl.multiple_of((n_trips - 1 - t if reverse else t) * span, span)
        order = list(reversed(range(SCAN_UNROLL))) if reverse else list(range(SCAN_UNROLL))
        loaded = [(a_ref[pl.ds(base + u * SUBLANES, SUBLANES), :], b_ref[pl.ds(base + u * SUBLANES, SUBLANES), :])
                  for u in order]
        out = []
        for a, b in [within(a, b) for a, b in loaded]:
            h = a * carry + b
            out.append(h)
            carry = h[0:1, :] if reverse else h[SUBLANES - 1:SUBLANES, :]
        for u, h in zip(order, out):
            h_ref[pl.ds(base + u * SUBLANES, SUBLANES), :] = h
        return carry

    lax.fori_loop(0, n_trips, trip, jnp.zeros((1, c), F32))


def _scan_forward(a_ref, b_ref, h_ref):
    _scan(a_ref, b_ref, h_ref, False)


def _scan_backward(c_ref, b_ref, g_ref):
    _scan(c_ref, b_ref, g_ref, True)


MESH = pl.DeviceIdType.MESH
_HBM = pl.BlockSpec(memory_space=pltpu.HBM)
_OTHER_CHIPS = ((1, 0), (0, 1), (1, 1))
_OTHER_DEVICES = tuple((dx, dy, dc) for dx in (0, 1) for dy in (0, 1) for dc in (0, 1) if dx or dy or dc)
N_DEVICES = 8


def _position():
    return lax.axis_index("x"), lax.axis_index("y"), lax.axis_index("c")


def _flip(v, d):
    return 1 - v if d else v


def _chip(x, y, p):
    px, py = _flip(x, _OTHER_CHIPS[p][0]), _flip(y, _OTHER_CHIPS[p][1])
    return px, py, 2 * px + py


class _Ride:
    def __init__(self, srcs, bufs, scratch, plan, collective_id):
        self.srcs, self.bufs, self.scratch, self.plan = list(srcs), list(bufs), list(scratch), plan
        self.collective_id = collective_id


NEIGHBOURS_AND_SIBLING = 1
OTHER_CHIPS_SAME_CORE = 2
ALL_DEVICES = 3
SIBLING = 4


def _handshake(peers):
    barrier = pltpu.get_barrier_semaphore()
    for peer in peers:
        pl.semaphore_signal(barrier, inc=1, device_id=peer, device_id_type=MESH)
    pl.semaphore_wait(barrier, len(peers))


def _call(body, *, name, grid, in_specs, out_specs, out_shape, operands, scratch_shapes=(), ride=None):
    in_specs, out_specs, out_shape = list(in_specs), list(out_specs), list(out_shape)
    scratch_shapes = list(scratch_shapes)
    if ride is None:
        return pl.pallas_call(body, name=name, grid=grid, in_specs=in_specs, out_specs=out_specs, out_shape=out_shape,
                              scratch_shapes=scratch_shapes, compiler_params=_params())(*operands)
    n_in, n_out, n_scr = len(in_specs), len(out_shape), len(scratch_shapes)
    old = [i for i, b in enumerate(ride.bufs) if not isinstance(b, jax.ShapeDtypeStruct)]
    n_src, n_old, n_buf = len(ride.srcs), len(old), len(ride.bufs)

    def full_body(*refs):
        o0 = n_in + n_src + n_old
        s0 = o0 + n_out + n_buf
        start, relay, relay_on, finish = ride.plan(refs[n_in:n_in + n_src], refs[o0 + n_out:s0], refs[s0 + n_scr:])
        ids = [pl.program_id(i) for i in range(len(grid))]
        first = functools.reduce(jnp.logical_and, [i == 0 for i in ids])
        middle = functools.reduce(jnp.logical_and, [ids[0] == grid[0] // 2] + [i == 0 for i in ids[1:]])
        last = functools.reduce(jnp.logical_and, [i == g - 1 for i, g in zip(ids, grid)])
        pl.when(first)(start)
        pl.when(middle)(relay)
        pl.when(last)(relay_on)
        body(*refs[:n_in], *refs[o0:o0 + n_out], *refs[s0:s0 + n_scr])
        pl.when(last)(finish)

    shapes = [jax.ShapeDtypeStruct(b.shape, b.dtype) for b in ride.bufs]
    res = pl.pallas_call(
        full_body, name=name, grid=grid,
        in_specs=in_specs + [_HBM] * (n_src + n_old), out_specs=out_specs + [_HBM] * n_buf,
        out_shape=out_shape + shapes, scratch_shapes=scratch_shapes + ride.scratch,
        input_output_aliases={n_in + n_src + k: n_out + i for k, i in enumerate(old)},
        compiler_params=_params(collective_id=ride.collective_id),
    )(*operands, *ride.srcs, *[ride.bufs[i] for i in old])
    return list(res[:n_out]), list(res[n_out:])


def run_ride(ride, name):
    def body():
        pass

    return _call(body, name=name, grid=(1,), in_specs=[], out_specs=[], out_shape=[], operands=[], ride=ride)[1]


def gather_ride(shards, items=None, into=None):
    items = items or [(a, 0, s.shape[0]) for a, s in enumerate(shards)]
    bufs = into or [jax.ShapeDtypeStruct((N_CHIPS,) + s.shape, s.dtype) for s in shards]
    nrel = len(_OTHER_CHIPS)

    def plan(srcs, dsts, sems):
        ici_send, ici_recv, hop_send, hop_recv, sib_send, sib_recv = sems
        x, y, c = _position()
        j = 2 * x + y

        def rows(ref, it, h, q=None):
            half = it[2] // 2
            if q is None:
                return ref.at[pl.ds(it[1] + h * half, half), :]
            return ref.at[pl.ds(it[1] + h * half + q * (half // 2), half // 2), :]

        def ici(i, p, slot):
            it = items[i]
            px, py, _ = _chip(x, y, p)
            return pltpu.make_async_remote_copy(
                src_ref=rows(srcs[it[0]], it, c), dst_ref=rows(dsts[it[0]].at[slot], it, c),
                send_sem=ici_send.at[i * nrel + p], recv_sem=ici_recv.at[i * nrel + p],
                device_id=(px, py, c), device_id_type=MESH)

        def hop(i, p, slot):
            it = items[i]
            part = rows(dsts[it[0]].at[slot], it, c, p)
            px, py, _ = _chip(x, y, 1 - p)
            return pltpu.make_async_remote_copy(
                src_ref=part, dst_ref=part, send_sem=hop_send.at[i * 2 + p], recv_sem=hop_recv.at[i * 2 + p],
                device_id=(px, py, c), device_id_type=MESH)

        def sib(i, p, h):
            it = items[i]
            part = rows(dsts[it[0]].at[_chip(x, y, p)[2]], it, h)
            return pltpu.make_async_remote_copy(
                src_ref=part, dst_ref=part, send_sem=sib_send.at[i * nrel + p], recv_sem=sib_recv.at[i * nrel + p],
                device_id=(x, y, 1 - c), device_id_type=MESH)

        every = range(len(items))
        diag = _chip(x, y, 2)[2]

        def start():
            _handshake([_chip(x, y, 0)[:2] + (c,), _chip(x, y, 1)[:2] + (c,), (x, y, 1 - c)])
            for i in every:
                for p in (0, 1):
                    ici(i, p, j).start()

        def relay():
            for i in every:
                for p in (0, 1):
                    k = _chip(x, y, p)[2]
                    ici(i, p, k).wait_recv()
                    hop(i, p, k).start()
                    sib(i, p, c).start()

        def relay_on():
            for i in every:
                for p in (0, 1):
                    hop(i, p, diag).wait_recv()
                sib(i, 2, c).start()

        def finish():
            for i in every:
                for p in range(nrel):
                    sib(i, p, 1 - c).wait_recv()
            for i in every:
                for p in (0, 1):
                    ici(i, p, j).wait_send()
                    hop(i, p, _chip(x, y, p)[2]).wait_send()
                for p in range(nrel):
                    sib(i, p, c).wait_send()

        return start, relay, relay_on, finish

    n = len(items)
    sems = [pltpu.SemaphoreType.DMA((n * nrel,))] * 2 + [pltpu.SemaphoreType.DMA((n * 2,))] * 2 \
        + [pltpu.SemaphoreType.DMA((n * nrel,))] * 2
    return _Ride(shards, bufs, sems, plan, NEIGHBOURS_AND_SIBLING)


def exchange_ride(sums, items=None, into=None, rep=None):
    items = [(a, 0, s.shape[1]) for a, s in enumerate(sums)] if items is None else items
    into = into or [None] * len(sums)
    bufs = [jax.ShapeDtypeStruct(s.shape, s.dtype) if b is None else b for s, b in zip(sums, into)]
    srcs = list(sums)
    scratch = [pltpu.SemaphoreType.DMA((max(len(items), 1) * len(_OTHER_CHIPS),))] * 2
    if rep is not None:
        srcs.append(rep)
        bufs.append(jax.ShapeDtypeStruct((N_DEVICES,) + rep.shape, rep.dtype))
        scratch += [pltpu.SemaphoreType.DMA((len(_OTHER_DEVICES),))] * 2
    nrel = len(_OTHER_CHIPS)

    def plan(src_refs, dst_refs, sems):
        x, y, c = _position()
        j = 2 * x + y
        me = 4 * x + 2 * y + c

        def part(i, p, src_slot, dst_slot):
            a, r0, nr = items[i]
            px, py, _ = _chip(x, y, p)
            return pltpu.make_async_remote_copy(
                src_ref=src_refs[a].at[src_slot, pl.ds(r0, nr), :], dst_ref=dst_refs[a].at[dst_slot, pl.ds(r0, nr), :],
                send_sem=sems[0].at[i * nrel + p], recv_sem=sems[1].at[i * nrel + p],
                device_id=(px, py, c), device_id_type=MESH)

        def device(q):
            dx, dy, dc = _OTHER_DEVICES[q]
            return _flip(x, dx), _flip(y, dy), _flip(c, dc)

        def rep_copy(q, slot):
            return pltpu.make_async_remote_copy(
                src_ref=src_refs[-1], dst_ref=dst_refs[-1].at[slot], send_sem=sems[2].at[q], recv_sem=sems[3].at[q],
                device_id=device(q), device_id_type=MESH)

        pairs = [(i, p) for i in range(len(items)) for p in range(nrel)]
        others = range(len(_OTHER_DEVICES)) if rep is not None else ()

        def start():
            if rep is None:
                _handshake([_chip(x, y, p)[:2] + (c,) for p in range(nrel)])
            else:
                _handshake([device(q) for q in others])
            for i, p in pairs:
                part(i, p, _chip(x, y, p)[2], j).start()
            for q in others:
                rep_copy(q, me).start()

        def finish():
            for i, p in pairs:
                k = _chip(x, y, p)[2]
                part(i, p, k, k).wait_recv()
            for q in others:
                px, py, pc = device(q)
                rep_copy(q, 4 * px + 2 * py + pc).wait_recv()
            for i, p in pairs:
                part(i, p, _chip(x, y, p)[2], j).wait_send()
            for q in others:
                rep_copy(q, me).wait_send()

        return start, lambda: None, lambda: None, finish

    return _Ride(srcs, bufs, scratch, plan, OTHER_CHIPS_SAME_CORE if rep is None else ALL_DEVICES)


def _own_slot(buf, own, index):
    return lax.dynamic_update_slice(buf, own[None], (index,) + (0,) * own.ndim)


def _token_tile(s):
    return min(s, 512)


def norm_in(x, g):
    s, d = x.shape
    t = _token_tile(s)

    def body(x_ref, g_ref, o_ref, ot_ref):
        n, _ = _rms_stats(x_ref[...])
        h = n * g_ref[...]
        o_ref[...] = h.astype(BF16)
        ot_ref[...] = h.T.astype(BF16)

    return pl.pallas_call(
        body, name="norm_in", grid=(s // t,),
        in_specs=[pl.BlockSpec((t, d), lambda i: (i, 0)), pl.BlockSpec((1, d), lambda i: (0, 0))],
        out_specs=[pl.BlockSpec((t, d), lambda i: (i, 0)), pl.BlockSpec((d, t), lambda i: (0, i))],
        out_shape=[jax.ShapeDtypeStruct((s, d), BF16), jax.ShapeDtypeStruct((d, s), BF16)],
        compiler_params=_params(),
    )(x, g)


def matmul_cols(a, w4, name, ride=None):
    m, k = a.shape
    nj, _, ns = w4.shape
    nb = ns // CW

    def body(a_ref, w_ref, o_ref):
        o_ref[...] = _dot(a_ref[...], w_ref[0])

    return _call(
        body, name=name, grid=(nj, nb),
        in_specs=[pl.BlockSpec((m, k), lambda j, b: (0, 0)),
                  pl.BlockSpec((1, k, CW), lambda j, b: (j, 0, b))],
        out_specs=[pl.BlockSpec((m, CW), lambda j, b: (0, j * nb + b))],
        out_shape=[jax.ShapeDtypeStruct((m, nj * ns), F32)],
        operands=(a, w4), ride=ride)


def mix_conv_fwd(proj, ws, ride=None):
    s = proj.shape[0]
    nblk = D_MODEL // CW

    def body(cb_ref, cc_ref, cx_ref, ws_ref, q_ref, ya_ref):
        q = _causal_conv(cc_ref[...] * cx_ref[...], ws_ref)
        q_ref[...] = q
        ya_ref[...] = (cb_ref[...] * q).astype(BF16)

    seg = lambda k: pl.BlockSpec((s, CW), lambda c, k=k: (0, k * nblk + c))
    return _call(
        body, name="mix_conv_fwd", grid=(nblk,),
        in_specs=[seg(0), seg(1), seg(2), pl.BlockSpec((3, CW), lambda c: (0, c))],
        out_specs=[pl.BlockSpec((s, CW), lambda c: (0, c))] * 2,
        out_shape=[jax.ShapeDtypeStruct((s, D_MODEL), F32), jax.ShapeDtypeStruct((s, D_MODEL), BF16)],
        operands=(proj, proj, proj, ws), ride=ride)


def _lru_gates(r, ls):
    log_a = LRU_C * r * ls
    a = jnp.exp(log_a)
    mult = jnp.sqrt(_neg_expm1(2.0 * log_a))
    mult = jnp.where(_rows(r.shape) == 0, 1.0, mult)
    return a, mult


def mix_lru_fwd(proj, wl, bl, wa, ba, wx, bx, lam, ride=None):
    s = proj.shape[0]
    nblk = D_MODEL // CW

    def body(lx_ref, ly_ref, wl_ref, bl_ref, wa_ref, ba_ref, wx_ref, bx_ref, lam_ref,
             xl_ref, r_ref, i_ref, h_ref, yb_ref, a_scr, u_scr):
        xl = _causal_conv(lx_ref[...], wl_ref, bl_ref[...])
        xlb = xl.astype(BF16)
        xl_ref[...] = xlb
        r = _sigmoid(_dot(xlb, wa_ref[0]) + ba_ref[...])
        i = _sigmoid(_dot(xlb, wx_ref[0]) + bx_ref[...])
        r_ref[...] = r.astype(BF16)
        i_ref[...] = i.astype(BF16)
        a, mult = _lru_gates(r, _log_sigmoid(lam_ref[...]))
        a_scr[...] = a
        u_scr[...] = mult * i * xl
        _scan_forward(a_scr, u_scr, h_ref)
        yb_ref[...] = (h_ref[...] * _gelu(ly_ref[...])).astype(BF16)

    blk = lambda k: pl.BlockSpec((s, CW), lambda c, k=k: (0, k * nblk + c))
    vec = pl.BlockSpec((1, CW), lambda c: (0, c))
    mat = pl.BlockSpec((1, CW, CW), lambda c: (c, 0, 0))
    out = pl.BlockSpec((s, CW), lambda c: (0, c))
    f = jax.ShapeDtypeStruct((s, D_MODEL), F32)
    hb = jax.ShapeDtypeStruct((s, D_MODEL), BF16)
    return _call(
        body, name="mix_lru_fwd", grid=(nblk,),
        in_specs=[blk(3), blk(4), pl.BlockSpec((4, CW), lambda c: (0, c)), vec, mat, vec, mat, vec, vec],
        out_specs=[out] * 5,
        out_shape=[hb, hb, hb, f, hb],
        scratch_shapes=[pltpu.VMEM((s, CW), F32), pltpu.VMEM((s, CW), F32)],
        operands=(proj, proj, wl, bl, wa, ba, wx, bx, lam), ride=ride)


def branch_merge_fwd(ya, yb, wcb, wlb, proj, ride=None):
    s = ya.shape[0]
    nblk = D_MODEL // CW

    def body(ya_ref, yb_ref, wcb_ref, wlb_ref, gc_ref, gl_ref, a_ref, b_ref, m_ref):
        a = _dot(ya_ref[...], wcb_ref[...])
        b = _dot(yb_ref[...], wlb_ref[...])
        a_ref[...] = a
        b_ref[...] = b
        m_ref[...] = (_sigmoid(gc_ref[...]) * a + _sigmoid(gl_ref[...]) * b).astype(BF16)

    res = pl.BlockSpec((s, D_MODEL), lambda n: (0, 0))
    wcol = pl.BlockSpec((D_MODEL, CW), lambda n: (0, n))
    blk = lambda k: pl.BlockSpec((s, CW), lambda n, k=k: (0, k * nblk + n))
    out = pl.BlockSpec((s, CW), lambda n: (0, n))
    f = jax.ShapeDtypeStruct((s, D_MODEL), F32)
    return _call(
        body, name="branch_merge_fwd", grid=(nblk,),
        in_specs=[res, res, wcol, wcol, blk(5), blk(6)],
        out_specs=[out] * 3,
        out_shape=[f, f, jax.ShapeDtypeStruct((s, D_MODEL), BF16)],
        operands=(ya, yb, wcb, wlb, proj, proj), ride=ride)


def mix_out_fwd(merged, wout, x, g2, g3, ride=None):
    s, d = x.shape
    t = _token_tile(s)

    def body(m_ref, w_ref, x_ref, g2_ref, g3_ref, mix_ref, x2_ref, h2_ref, h2t_ref):
        mix = _dot(m_ref[...], w_ref[...])
        mix_ref[...] = mix
        n, _ = _rms_stats(mix)
        x2 = x_ref[...] + n * g2_ref[...]
        x2_ref[...] = x2
        n2, _ = _rms_stats(x2)
        h2 = n2 * g3_ref[...]
        h2_ref[...] = h2.astype(BF16)
        h2t_ref[...] = h2.T.astype(BF16)

    tile = pl.BlockSpec((t, d), lambda i: (i, 0))
    vec = pl.BlockSpec((1, d), lambda i: (0, 0))
    f = jax.ShapeDtypeStruct((s, d), F32)
    return _call(
        body, name="mix_out_fwd", grid=(s // t,),
        in_specs=[tile, pl.BlockSpec((d, d), lambda i: (0, 0)), tile, vec, vec],
        out_specs=[tile] * 3 + [pl.BlockSpec((d, t), lambda i: (0, i))],
        out_shape=[f, f, jax.ShapeDtypeStruct((s, d), BF16), jax.ShapeDtypeStruct((d, s), BF16)],
        operands=(merged, wout, x, g2, g3), ride=ride)


def ffn_up_act_fwd(h2, wup4, fw, fb, ride=None):
    s, k = h2.shape
    ns = wup4.shape[2]
    bw = UP_W
    per_chip = ns // bw
    nblk = D_FF // bw

    def body(h_ref, wg_ref, wv_ref, cg_ref, cv_ref, bg_ref, bv_ref, up_ref, act_ref, f_ref):
        h = h_ref[...]
        ug = _dot(h, wg_ref[0])
        uv = _dot(h, wv_ref[0])
        up_ref[0] = ug
        up_ref[1] = uv
        gate = _causal_conv(ug, cg_ref, bg_ref[...])
        val = _causal_conv(uv, cv_ref, bv_ref[...])
        act_ref[0] = gate.astype(BF16)
        act_ref[1] = val.astype(BF16)
        f_ref[...] = (_gelu(gate) * val).astype(BF16)

    wcols = lambda h: pl.BlockSpec((1, k, bw), lambda n, h=h: (n // per_chip + 2 * h, 0, n % per_chip))
    half = lambda h, rows: pl.BlockSpec((rows, bw), lambda n, h=h: (0, h * nblk + n))
    both = pl.BlockSpec((2, s, bw), lambda n: (0, 0, n))
    return _call(
        body, name="ffn_up_act_fwd", grid=(nblk,),
        in_specs=[pl.BlockSpec((s, k), lambda n: (0, 0)), wcols(0), wcols(1),
                  half(0, 3), half(1, 3), half(0, 1), half(1, 1)],
        out_specs=[both, both, pl.BlockSpec((s, bw), lambda n: (0, n))],
        out_shape=[jax.ShapeDtypeStruct((2, s, D_FF), F32), jax.ShapeDtypeStruct((2, s, D_FF), BF16),
                   jax.ShapeDtypeStruct((s, D_FF), BF16)],
        operands=(h2, wup4, wup4, fw, fw, fb, fb), ride=ride)


def ffn_down_loss(f, wdown, x2, target, g4):
    s, d = x2.shape
    t = _token_tile(s)

    def body(f_ref, w_ref, x2_ref, tg_ref, g4_ref, dy_ref, dout_ref, loss_ref, dg4_ref):
        @pl.when(pl.program_id(0) == 0)
        def _():
            loss_ref[...] = jnp.zeros_like(loss_ref)
            dg4_ref[...] = jnp.zeros_like(dg4_ref)

        out = _dot(f_ref[...], w_ref[...])
        n, r = _rms_stats(out)
        err = x2_ref[...] + n * g4_ref[...] - tg_ref[...]
        loss_ref[...] += jnp.full(loss_ref.shape, (0.5 / d) * jnp.sum(err * err), F32)
        dy = err * (1.0 / d)
        dy_ref[...] = dy
        dout, dg = _rms_bwd(n, r, g4_ref[...], dy)
        dout_ref[...] = dout.astype(BF16)
        dg4_ref[...] += jnp.sum(dg, axis=0, keepdims=True)

    tile = pl.BlockSpec((t, d), lambda i: (i, 0))
    vec = pl.BlockSpec((1, d), lambda i: (0, 0))
    return pl.pallas_call(
        body, name="ffn_down_loss", grid=(s // t,),
        in_specs=[pl.BlockSpec((t, D_FF), lambda i: (i, 0)), pl.BlockSpec((D_FF, d), lambda i: (0, 0)), tile, tile, vec],
        out_specs=[tile, tile, pl.BlockSpec((1, 128), lambda i: (0, 0)), vec],
        out_shape=[jax.ShapeDtypeStruct((s, d), F32), jax.ShapeDtypeStruct((s, d), BF16),
                   jax.ShapeDtypeStruct((1, 128), F32), jax.ShapeDtypeStruct((1, d), F32)],
        compiler_params=_params(),
    )(f, wdown, x2, target, g4)


def ffn_up_bwd(dout, wdown, up, act, f, fw, wup4, h2t, ride=None):
    k, s = h2t.shape
    nblk = D_FF // FW
    per_chip = wup4.shape[2] // FW

    def body(do_ref, wd_ref, up_ref, act_ref, f_ref, cg_ref, cv_ref, wg_ref, wv_ref, h_ref,
             dh_ref, dwu_ref, dwd_ref, dw_ref, db_ref, dup_scr):
        @pl.when(pl.program_id(0) == 0)
        def _():
            dup_scr[...] = jnp.zeros_like(dup_scr)
            dh_ref[...] = jnp.zeros_like(dh_ref)

        do = do_ref[...]
        df = _dot_nt(do, wd_ref[...])
        dg = dup_scr[0]
        dv = dup_scr[1]
        ht = h_ref[...]
        dh_ref[...] += _dot_nt(dg, wg_ref[0]) + _dot_nt(dv, wv_ref[0])
        dwu_ref[0] = _dot(ht, dg).astype(BF16)
        dwu_ref[1] = _dot(ht, dv).astype(BF16)
        dwd_ref[...] = _dot_tn(f_ref[...], do).astype(BF16)
        val = act_ref[1].astype(F32)
        ge, dge = _gelu_and_grad(act_ref[0].astype(F32))
        dgate = _advances(df * val * dge, 3)
        dval = _advances(df * ge, 3)
        dw_ref[0] = _conv_wgrad(dgate, up_ref[0])
        dw_ref[1] = _conv_wgrad(dval, up_ref[1])
        db_ref[0] = jnp.sum(dgate[0], axis=0, keepdims=True)
        db_ref[1] = jnp.sum(dval[0], axis=0, keepdims=True)
        dup_scr[0] = _taps_sum(dgate, cg_ref).astype(BF16)
        dup_scr[1] = _taps_sum(dval, cv_ref).astype(BF16)

    cur = lambda n: jnp.minimum(n, nblk - 1)
    prev = lambda n: jnp.maximum(n - 1, 0)
    once = pl.Buffered(1)
    both = lambda rows: pl.BlockSpec((2, rows, FW), lambda n: (0, 0, cur(n)))
    taps = lambda h: pl.BlockSpec((3, FW), lambda n, h=h: (0, h * nblk + cur(n)))
    wcols = lambda h: pl.BlockSpec((1, k, FW), lambda n, h=h: (prev(n) // per_chip + 2 * h, 0, prev(n) % per_chip))
    return _call(
        body, name="ffn_up_bwd", grid=(nblk + 1,),
        in_specs=[pl.BlockSpec((s, D_MODEL), lambda n: (0, 0), pipeline_mode=once),
                  pl.BlockSpec((FW, D_MODEL), lambda n: (cur(n), 0)), both(s), both(s),
                  pl.BlockSpec((s, FW), lambda n: (0, cur(n))), taps(0), taps(1), wcols(0), wcols(1),
                  pl.BlockSpec((k, s), lambda n: (0, 0), pipeline_mode=once)],
        out_specs=[pl.BlockSpec((s, k), lambda n: (0, 0), pipeline_mode=once),
                   pl.BlockSpec((2, k, FW), lambda n: (0, 0, prev(n))),
                   pl.BlockSpec((FW, D_MODEL), lambda n: (cur(n), 0)), both(3), both(1)],
        out_shape=[jax.ShapeDtypeStruct((s, k), F32), jax.ShapeDtypeStruct((2, k, D_FF), BF16),
                   jax.ShapeDtypeStruct((D_FF, D_MODEL), BF16),
                   jax.ShapeDtypeStruct((2, 3, D_FF), F32), jax.ShapeDtypeStruct((2, 1, D_FF), F32)],
        scratch_shapes=[pltpu.VMEM((2, s, FW), BF16)],
        operands=(dout, wdown, up, act, f, fw, fw, wup4, wup4, h2t), ride=ride)


def matmul_cols_bwd(dy, other, name, wgrad, ride=None):
    m = dy[0].shape[1]
    if wgrad:
        k = other.shape[0]
        nj, nb = N_CHIPS, sum(d.shape[0] * d.shape[2] for d in dy) // (N_CHIPS * CW)
    else:
        nj, k, ns = other.shape
        nb = ns // CW
    per_seg = dy[0].shape[2] // CW
    first = [sum(d.shape[0] for d in dy[:i]) for i in range(len(dy))]

    def segment(j, b):
        return (j * nb + b) // per_seg, (j * nb + b) % per_seg

    def body(*refs):
        dy_refs, (o_ref, r_ref) = refs[:len(dy)], refs[len(dy):]
        seg, _ = segment(pl.program_id(0), pl.program_id(1))
        dyb = dy_refs[-1][0]
        for i in range(len(dy) - 2, -1, -1):
            dyb = jnp.where(seg < first[i + 1], dy_refs[i][0], dyb)
        if wgrad:
            r_ref[...] = _dot(o_ref[...], dyb).astype(BF16)
        else:
            @pl.when((pl.program_id(0) == 0) & (pl.program_id(1) == 0))
            def _():
                r_ref[...] = jnp.zeros_like(r_ref)

            r_ref[...] += _dot_nt(dyb, o_ref[0])

    def dy_spec(i):
        nseg = dy[i].shape[0]

        def index(j, b):
            seg, col = segment(j, b)
            local = seg - first[i]
            return (jnp.clip(local, 0, nseg - 1), 0,
                    jnp.where(local < 0, 0, jnp.where(local >= nseg, per_seg - 1, col)))

        return pl.BlockSpec((1, m, CW), index)

    if wgrad:
        other_spec = pl.BlockSpec((k, m), lambda j, b: (0, 0))
        out_spec = pl.BlockSpec((k, CW), lambda j, b: (0, j * nb + b))
        out_shape = jax.ShapeDtypeStruct((k, nj * nb * CW), BF16)
    else:
        other_spec = pl.BlockSpec((1, k, CW), lambda j, b: (j, 0, b))
        out_spec = pl.BlockSpec((m, k), lambda j, b: (0, 0))
        out_shape = jax.ShapeDtypeStruct((m, k), F32)
    return _call(
        body, name=name, grid=(nj, nb), in_specs=[dy_spec(i) for i in range(len(dy))] + [other_spec],
        out_specs=[out_spec], out_shape=[out_shape], operands=(*dy, other), ride=ride)


def norms_mid_bwd(dh2, x2, dy, mix, g3, g2, ride=None):
    s, d = x2.shape
    t = _token_tile(s)

    def body(dh2_ref, x2_ref, dy_ref, mix_ref, g3_ref, g2_ref, dx2_ref, dmix_ref, dg3_ref, dg2_ref):
        @pl.when(pl.program_id(0) == 0)
        def _():
            dg3_ref[...] = jnp.zeros_like(dg3_ref)
            dg2_ref[...] = jnp.zeros_like(dg2_ref)

        n3, r3 = _rms_stats(x2_ref[...])
        dx, dg3 = _rms_bwd(n3, r3, g3_ref[...], dh2_ref[...])
        dx2 = dy_ref[...] + dx
        dx2_ref[...] = dx2
        dg3_ref[...] += jnp.sum(dg3, axis=0, keepdims=True)
        n2, r2 = _rms_stats(mix_ref[...])
        dmix, dg2 = _rms_bwd(n2, r2, g2_ref[...], dx2)
        dmix_ref[...] = dmix.astype(BF16)
        dg2_ref[...] += jnp.sum(dg2, axis=0, keepdims=True)

    tile = pl.BlockSpec((t, d), lambda i: (i, 0))
    vec = pl.BlockSpec((1, d), lambda i: (0, 0))
    v = jax.ShapeDtypeStruct((1, d), F32)
    return _call(
        body, name="norms_mid_bwd", grid=(s // t,),
        in_specs=[tile, tile, tile, tile, vec, vec],
        out_specs=[tile, tile, vec, vec],
        out_shape=[jax.ShapeDtypeStruct((s, d), F32), jax.ShapeDtypeStruct((s, d), BF16), v, v],
        operands=(dh2, x2, dy, mix, g3, g2), ride=ride)


def mix_out_bwd(dmix, wout, merged, a, b, proj, ride=None):
    s = dmix.shape[0]
    nblk = D_MODEL // CW

    def body(dm_ref, w_ref, mg_ref, a_ref, b_ref, gc_ref, gl_ref, da_ref, db_ref, dw_ref, dg_ref):
        dm = dm_ref[...]
        dmerged = _dot_nt(dm, w_ref[...])
        dw_ref[...] = _dot_tn(mg_ref[...], dm).astype(BF16)
        sc = _sigmoid(gc_ref[...])
        sl = _sigmoid(gl_ref[...])
        da_ref[...] = (dmerged * sc).astype(BF16)
        db_ref[...] = (dmerged * sl).astype(BF16)
        dg_ref[0] = (dmerged * a_ref[...] * sc * (1.0 - sc)).astype(BF16)
        dg_ref[1] = (dmerged * b_ref[...] * sl * (1.0 - sl)).astype(BF16)

    res = pl.BlockSpec((s, D_MODEL), lambda n: (0, 0))
    rows = pl.BlockSpec((CW, D_MODEL), lambda n: (n, 0))
    col = pl.BlockSpec((s, CW), lambda n: (0, n))
    blk = lambda k: pl.BlockSpec((s, CW), lambda n, k=k: (0, k * nblk + n))
    hb = jax.ShapeDtypeStruct((s, D_MODEL), BF16)
    return _call(
        body, name="mix_out_bwd", grid=(nblk,),
        in_specs=[res, rows, col, col, col, blk(5), blk(6)],
        out_specs=[col, col, rows, pl.BlockSpec((2, s, CW), lambda n: (0, 0, n))],
        out_shape=[hb, hb, jax.ShapeDtypeStruct((D_MODEL, D_MODEL), BF16), jax.ShapeDtypeStruct((2, s, D_MODEL), BF16)],
        operands=(dmix, wout, merged, a, b, proj, proj), ride=ride)


def mix_conv_bwd(da, wcb, proj, q, ws, ride=None):
    s = da.shape[0]
    nblk = D_MODEL // CW

    def body(da_ref, w_ref, cb_ref, cc_ref, cx_ref, q_ref, ws_ref, dc_ref, dw_ref, dws_ref):
        dab = da_ref[...]
        dya = _dot_nt(dab, w_ref[...])
        cb = cb_ref[...]
        cc = cc_ref[...]
        cx = cx_ref[...]
        q = q_ref[...]
        dw_ref[...] = _dot_tn((cb * q).astype(BF16), dab).astype(BF16)
        dc_ref[0] = (dya * q).astype(BF16)
        dq = _advances(dya * cb, 3)
        dp = _taps_sum(dq, ws_ref)
        dws_ref[...] = _conv_wgrad(dq, cc * cx)
        dc_ref[1] = (dp * cx).astype(BF16)
        dc_ref[2] = (dp * cc).astype(BF16)

    res = pl.BlockSpec((s, D_MODEL), lambda n: (0, 0))
    rows = pl.BlockSpec((CW, D_MODEL), lambda n: (n, 0))
    col = pl.BlockSpec((s, CW), lambda n: (0, n))
    blk = lambda k: pl.BlockSpec((s, CW), lambda n, k=k: (0, k * nblk + n))
    taps = pl.BlockSpec((3, CW), lambda n: (0, n))
    hb = jax.ShapeDtypeStruct((s, D_MODEL), BF16)
    return _call(
        body, name="mix_conv_bwd", grid=(nblk,),
        in_specs=[res, rows, blk(0), blk(1), blk(2), col, taps],
        out_specs=[pl.BlockSpec((3, s, CW), lambda n: (0, 0, n)), rows, taps],
        out_shape=[jax.ShapeDtypeStruct((3, s, D_MODEL), BF16), jax.ShapeDtypeStruct((D_MODEL, D_MODEL), BF16),
                   jax.ShapeDtypeStruct((3, D_MODEL), F32)],
        operands=(da, wcb, proj, proj, proj, q, ws), ride=ride)


def mix_lru_bwd(db, wlb, proj, xl, r, i, h, wl, wa, wx, lam, ride=None):
    s = db.shape[0]
    nblk = D_MODEL // CW

    def body(db_ref, w_ref, lx_ref, ly_ref, xl_ref, r_ref, i_ref, h_ref, wl_ref, wa_ref, wx_ref, lam_ref,
             dl_ref, dw_ref, dwa_ref, dwx_ref, dba_ref, dbx_ref, dwl_ref, dbl_ref, dlam_ref,
             c_scr, g_scr):
        dbb = db_ref[...]
        dyb = _dot_nt(dbb, w_ref[...])
        h = h_ref[...]
        ge, dge = _gelu_and_grad(ly_ref[...])
        dw_ref[...] = _dot_tn((h * ge).astype(BF16), dbb).astype(BF16)
        dl_ref[1] = (dyb * h * dge).astype(BF16)
        r = r_ref[...].astype(F32)
        gi = i_ref[...].astype(F32)
        xlb = xl_ref[...]
        xl = xlb.astype(F32)
        lam = lam_ref[...]
        ls = _log_sigmoid(lam)
        a, mult = _lru_gates(r, ls)
        c_scr[...] = _shift_up(a, 1)
        g_scr[...] = dyb * ge
        _scan_backward(c_scr, g_scr, g_scr)
        du = g_scr[...]
        da = du * _shift_down(h, 1)
        dmult = du * gi * xl
        di = du * mult * xl
        dxl = du * mult * gi
        first = _rows(a.shape) == 0
        dlog_a = da * a - jnp.where(first, 0.0, dmult * a * a / mult)
        dr = dlog_a * (LRU_C * ls)
        dlam_ref[...] = jnp.sum(dlog_a * r, axis=0, keepdims=True) * (LRU_C * (1.0 - _sigmoid(lam)))
        dzr = dr * r * (1.0 - r)
        dzi = di * gi * (1.0 - gi)
        dba_ref[...] = jnp.sum(dzr, axis=0, keepdims=True)
        dbx_ref[...] = jnp.sum(dzi, axis=0, keepdims=True)
        dzrb = dzr.astype(BF16)
        dzib = dzi.astype(BF16)
        dwa_ref[0] = _dot_tn(xlb, dzrb)
        dwx_ref[0] = _dot_tn(xlb, dzib)
        dxl = _advances(dxl + _dot_nt(dzrb, wa_ref[0]) + _dot_nt(dzib, wx_ref[0]), 4)
        dl_ref[0] = _taps_sum(dxl, wl_ref).astype(BF16)
        dwl_ref[...] = _conv_wgrad(dxl, lx_ref[...])
        dbl_ref[...] = jnp.sum(dxl[0], axis=0, keepdims=True)

    res = pl.BlockSpec((s, D_MODEL), lambda n: (0, 0))
    rows = pl.BlockSpec((CW, D_MODEL), lambda n: (n, 0))
    col = pl.BlockSpec((s, CW), lambda n: (0, n))
    blk = lambda k: pl.BlockSpec((s, CW), lambda n, k=k: (0, k * nblk + n))
    taps = pl.BlockSpec((4, CW), lambda n: (0, n))
    vec = pl.BlockSpec((1, CW), lambda n: (0, n))
    mat = pl.BlockSpec((1, CW, CW), lambda n: (n, 0, 0))
    hb = jax.ShapeDtypeStruct((s, D_MODEL), BF16)
    v = jax.ShapeDtypeStruct((1, D_MODEL), F32)
    m = jax.ShapeDtypeStruct((LRU_HEADS, HEAD_DIM, HEAD_DIM), F32)
    scr = pltpu.VMEM((s, CW), F32)
    return _call(
        body, name="mix_lru_bwd", grid=(nblk,),
        in_specs=[res, rows, blk(3), blk(4), col, col, col, col, taps, mat, mat, vec],
        out_specs=[pl.BlockSpec((2, s, CW), lambda n: (0, 0, n)), rows, mat, mat, vec, vec, taps, vec, vec],
        out_shape=[jax.ShapeDtypeStruct((2, s, D_MODEL), BF16), jax.ShapeDtypeStruct((D_MODEL, D_MODEL), BF16), m, m, v, v,
                   jax.ShapeDtypeStruct((4, D_MODEL), F32), v, v],
        scratch_shapes=[scr, scr],
        operands=(db, wlb, proj, proj, xl, r, i, h, wl, wa, wx, lam), ride=ride)


def norm_in_bwd(dh1, x, dx2, g1, ride=None):
    s, d = x.shape
    t = _token_tile(s)

    def body(dh_ref, x_ref, dx2_ref, g_ref, dx_ref, dg_ref):
        @pl.when(pl.program_id(0) == 0)
        def _():
            dg_ref[...] = jnp.zeros_like(dg_ref)

        n, r = _rms_stats(x_ref[...])
        dx, dg = _rms_bwd(n, r, g_ref[...], dh_ref[...])
        dx_ref[...] = dx2_ref[...] + dx
        dg_ref[...] += jnp.sum(dg, axis=0, keepdims=True)

    tile = pl.BlockSpec((t, d), lambda i: (i, 0))
    vec = pl.BlockSpec((1, d), lambda i: (0, 0))
    return _call(
        body, name="norm_in_bwd", grid=(s // t,),
        in_specs=[tile, tile, tile, vec],
        out_specs=[tile, vec],
        out_shape=[jax.ShapeDtypeStruct((s, d), F32), jax.ShapeDtypeStruct((1, d), F32)],
        operands=(dh1, x, dx2, g1), ride=ride)


def _owned_part(ref, kind, k, h, hr):
    if kind == "col":
        ns = ref.shape[1] // N_CHIPS
        return ref.at[pl.ds(h * hr, hr), pl.ds(k * ns, ns)]
    if kind == "row":
        return ref.at[pl.ds(k * 2 * hr + h * hr, hr), :]
    if kind == "col2":
        ns = ref.shape[2] // 2
        return ref.at[k // 2, pl.ds(h * hr, hr), pl.ds((k % 2) * ns, ns)]
    return ref.at[k, pl.ds(h * hr, hr), :]


def _part_shape(g, kind):
    if kind == "col2":
        return g.shape[1] // 2, g.shape[2] // 2
    if kind == "col":
        return g.shape[0] // 2, g.shape[1] // N_CHIPS
    if kind == "row":
        return g.shape[0] // (2 * N_CHIPS), g.shape[1]
    return g.shape[1] // 2, g.shape[2]


def pair_split(grads, kinds, name):
    n = len(grads)
    shapes = [_part_shape(g, k) for g, k in zip(grads, kinds)]

    def body(*refs):
        ins, theirs = refs[:n], refs[n:2 * n]
        send_sem, recv_sem = refs[2 * n:]
        x, y, c = _position()
        copies = []
        for a in range(n):
            hr = shapes[a][0]
            for k in range(N_CHIPS):
                s = a * N_CHIPS + k
                copies.append(pltpu.make_async_remote_copy(
                    src_ref=_owned_part(ins[a], kinds[a], k, 1 - c, hr), dst_ref=theirs[a].at[k],
                    send_sem=send_sem.at[s], recv_sem=recv_sem.at[s], device_id=(x, y, 1 - c), device_id_type=MESH))
        _handshake([(x, y, 1 - c)])
        for cp in copies:
            cp.start()
        for cp in copies:
            cp.wait()

    return pl.pallas_call(
        body, name=name,
        in_specs=[_HBM] * n, out_specs=[_HBM] * n,
        out_shape=[jax.ShapeDtypeStruct((N_CHIPS,) + shp, g.dtype) for shp, g in zip(shapes, grads)],
        scratch_shapes=[pltpu.SemaphoreType.DMA((n * N_CHIPS,))] * 2,
        compiler_params=pltpu.CompilerParams(collective_id=SIBLING),
    )(*grads)


def pair_swap(halves):
    n = len(halves)

    def body(*refs):
        ins, outs = refs[:n], refs[n:2 * n]
        send_sem, recv_sem = refs[2 * n:]
        x, y, c = _position()
        copies = [pltpu.make_async_remote_copy(
            src_ref=ins[a], dst_ref=outs[a], send_sem=send_sem.at[a], recv_sem=recv_sem.at[a],
            device_id=(x, y, 1 - c), device_id_type=MESH) for a in range(n)]
        _handshake([(x, y, 1 - c)])
        for cp in copies:
            cp.start()
        for cp in copies:
            cp.wait()

    return pl.pallas_call(
        body, name="pair_swap",
        in_specs=[_HBM] * n, out_specs=[_HBM] * n,
        out_shape=[jax.ShapeDtypeStruct(h.shape, h.dtype) for h in halves],
        scratch_shapes=[pltpu.SemaphoreType.DMA((n,))] * 2,
        compiler_params=pltpu.CompilerParams(collective_id=SIBLING),
    )(*halves)


def _row_tile(rows, cols, limit_bytes=1 << 20):
    best = None
    for t in range(SUBLANES, rows + 1, SUBLANES):
        if rows % t == 0 and t * cols * 4 <= limit_bytes:
            best = t
    return best or rows


def add_pair(g, kind, theirs, core, name):
    nc, rows, cols = theirs.shape
    t = _row_tile(rows, cols, 4 << 20)
    nt = rows // t

    def body(core_ref, g_ref, b_ref, o_ref):
        mine = g_ref[...].reshape(t, cols)
        o_ref[0] = (mine.astype(F32) + b_ref[0].astype(F32)).astype(o_ref.dtype)

    if kind == "col":
        own = pl.BlockSpec((t, cols), lambda k, i, c: (c[0] * nt + i, k))
    elif kind == "col2":
        own = pl.BlockSpec((1, t, cols), lambda k, i, c: (k // 2, c[0] * nt + i, k % 2))
    elif kind == "row":
        own = pl.BlockSpec((t, cols), lambda k, i, c: ((2 * k + c[0]) * nt + i, 0))
    else:
        own = pl.BlockSpec((1, t, cols), lambda k, i, c: (k, c[0] * nt + i, 0))
    spec = pl.BlockSpec((1, t, cols), lambda k, i, c: (k, i, 0))
    return pl.pallas_call(
        body, name=name,
        grid_spec=pltpu.PrefetchScalarGridSpec(num_scalar_prefetch=1, grid=(nc, nt), in_specs=[own, spec], out_specs=spec),
        out_shape=jax.ShapeDtypeStruct(theirs.shape, theirs.dtype), compiler_params=_params(),
    )(core, g, theirs)


def sum_lead(a, name):
    nl, rows, cols = a.shape
    t = _row_tile(rows, cols, (1 << 20) // 2)

    def body(a_ref, o_ref):
        acc = a_ref[0].astype(F32)
        for s in range(1, nl):
            acc = acc + a_ref[s].astype(F32)
        o_ref[...] = acc

    return pl.pallas_call(
        body, name=name, grid=(rows // t,),
        in_specs=[pl.BlockSpec((nl, t, cols), lambda i: (0, i, 0))],
        out_specs=pl.BlockSpec((t, cols), lambda i: (i, 0)),
        out_shape=jax.ShapeDtypeStruct((rows, cols), F32), compiler_params=_params(),
    )(a)


def sum_chips(rx, csum, chip, name):
    nc, rows, cols = rx.shape
    t = _row_tile(rows, cols, 2 << 20)

    def body(chip_ref, r0, r1, r2, r3, own_ref, o_ref):
        acc = None
        for s, ref in enumerate((r0, r1, r2, r3)):
            term = jnp.where(chip_ref[0] == s, own_ref[0], ref[0]).astype(F32)
            acc = term if acc is None else acc + term
        o_ref[...] = acc

    def slot(s):
        return pl.BlockSpec((1, t, cols), lambda i, c, s=s: (jnp.where(c[0] == s, c[0] ^ 1, s), i, 0))

    return pl.pallas_call(
        body, name=name,
        grid_spec=pltpu.PrefetchScalarGridSpec(
            num_scalar_prefetch=1, grid=(rows // t,),
            in_specs=[slot(s) for s in range(nc)] + [pl.BlockSpec((1, t, cols), lambda i, c: (c[0], i, 0))],
            out_specs=pl.BlockSpec((t, cols), lambda i, c: (i, 0))),
        out_shape=jax.ShapeDtypeStruct((rows, cols), F32), compiler_params=_params(),
    )(chip, rx, rx, rx, rx, csum)


def cast_bf16(a, name):
    rows, cols = a.shape
    t = _row_tile(rows, cols, 2 << 20)

    def body(i_ref, o_ref):
        o_ref[...] = i_ref[...].astype(BF16)

    spec = pl.BlockSpec((t, cols), lambda i: (i, 0))
    return pl.pallas_call(body, name=name, grid=(rows // t,), in_specs=[spec], out_specs=spec,
                          out_shape=jax.ShapeDtypeStruct((rows, cols), BF16), compiler_params=_params())(a)


def _adamw_update(w, g, m, v):
    nm = ADAM_B1 * m + (1.0 - ADAM_B1) * g
    nv = ADAM_B2 * v + (1.0 - ADAM_B2) * (g * g)
    m_hat = nm * (1.0 / (1.0 - ADAM_B1 ** ADAM_STEP))
    v_hat = nv * (1.0 / (1.0 - ADAM_B2 ** ADAM_STEP))
    return -ADAM_LR * (m_hat / (jnp.sqrt(v_hat) + ADAM_EPS) + ADAM_WD * w), nm, nv


def adamw(w, g, m, v, name):
    rows, cols = w.shape
    t = _row_tile(rows, cols)

    def body(w_ref, g_ref, m_ref, v_ref, d_ref, nm_ref, nv_ref):
        d_ref[...], nm_ref[...], nv_ref[...] = _adamw_update(w_ref[...], g_ref[...], m_ref[...], v_ref[...])

    spec = pl.BlockSpec((t, cols), lambda i: (i, 0))
    shp = jax.ShapeDtypeStruct((rows, cols), F32)
    return pl.pallas_call(
        body, name=name, grid=(rows // t,), in_specs=[spec] * 4, out_specs=[spec] * 3,
        out_shape=[shp, shp, shp], compiler_params=_params(),
    )(w, g, m, v)


def adamw_halves(w, g_mine, g_other, m, v, core, name):
    rows, cols = w.shape
    hr = rows // 2
    t = _row_tile(hr, cols)
    nt = hr // t

    def body(core_ref, w_ref, gm_ref, go_ref, m_ref, v_ref, g_ref, d_ref, nm_ref, nv_ref):
        g = jnp.where(pl.program_id(0) // nt == core_ref[0], gm_ref[...], go_ref[...])
        g_ref[...] = g
        d_ref[...], nm_ref[...], nv_ref[...] = _adamw_update(w_ref[...], g, m_ref[...], v_ref[...])

    spec = pl.BlockSpec((t, cols), lambda i, c: (i, 0))
    half = pl.BlockSpec((t, cols), lambda i, c: (i % nt, 0))
    shp = jax.ShapeDtypeStruct((rows, cols), F32)
    return pl.pallas_call(
        body, name=name,
        grid_spec=pltpu.PrefetchScalarGridSpec(num_scalar_prefetch=1, grid=(2 * nt,),
                                               in_specs=[spec, half, half, spec, spec], out_specs=[spec] * 4),
        out_shape=[shp] * 4, compiler_params=_params(),
    )(core, w, g_mine, g_other, m, v)


WEIGHTS = ("norm_mix_pre", "norm_mix_post", "norm_ffn_pre", "norm_ffn_post", "w_in", "conv_short_w",
           "w_conv_branch", "lru_conv_w", "lru_conv_b", "lru_wa", "lru_ba", "lru_wx", "lru_bx", "lru_lambda",
           "w_lru_branch", "w_out", "ffn_w_up", "ffn_conv_w", "ffn_conv_b", "ffn_w_down")
BIG = ("w_in", "ffn_w_up", "w_conv_branch", "w_lru_branch", "w_out", "ffn_w_down")
BIG_KIND = ("col", "col", "row", "row", "row", "row")
SMALL = ("conv_short_w", "lru_conv_w", "lru_wa", "lru_ba", "lru_wx", "lru_bx", "ffn_conv_w")
REPL = ("norm_mix_pre", "norm_mix_post", "norm_ffn_pre", "norm_ffn_post", "lru_conv_b", "lru_lambda", "ffn_conv_b")
PACK_W = 256
SMALL_ROWS = 576
REPL_ROWS = 16
LOSS_ROW = 12
FFN_SHARD = 2 * D_FF // N_CHIPS
QUARTER = HEAD_DIM // N_CHIPS
SMALL_PARTS = (("conv_short_w", 3, (1, 3, PACK_W)), ("lru_conv_w", 4, (1, 4, PACK_W)),
               ("lru_wa", LRU_HEADS * QUARTER, (1, LRU_HEADS, QUARTER, HEAD_DIM)), ("lru_ba", LRU_HEADS, (1, LRU_HEADS, QUARTER)),
               ("lru_wx", LRU_HEADS * QUARTER, (1, LRU_HEADS, QUARTER, HEAD_DIM)), ("lru_bx", LRU_HEADS, (1, LRU_HEADS, QUARTER)),
               ("ffn_conv_w", 3 * FFN_SHARD // PACK_W, (1, 3, FFN_SHARD)))


def _pad8(nr):
    return -(-nr // SUBLANES) * SUBLANES


SMALL_OFFSET = {}
for _name, _nr, _ in SMALL_PARTS:
    SMALL_OFFSET[_name] = sum(_pad8(nr) for n, nr, _ in SMALL_PARTS[:len(SMALL_OFFSET)])
FFN_ROWS = FFN_SHARD // PACK_W
BIASES = ("lru_ba", "lru_bx")
TAPS3 = ("conv_short_w", "ffn_conv_w")


def pack_small(dicts):
    names = [n for n, _, _ in SMALL_PARTS]
    operands = [d[n].transpose(1, 0, 2) if n in TAPS3 else d[n] for d in dicts for n in names]

    def body(*refs):
        ins, outs = refs[:len(operands)], refs[len(operands):]
        for i, o in enumerate(outs):
            o[...] = jnp.zeros_like(o)
            for (name, nr, shape), p in zip(SMALL_PARTS, ins[i * len(names):(i + 1) * len(names)]):
                r0 = SMALL_OFFSET[name]
                if name in BIASES:
                    o[r0:r0 + nr, 0:QUARTER] = p[0]
                elif name == "ffn_conv_w":
                    for k in range(shape[1]):
                        for s in range(FFN_ROWS):
                            o[r0 + FFN_ROWS * k + s:r0 + FFN_ROWS * k + s + 1, :] = p[k, :, s * PACK_W:(s + 1) * PACK_W]
                elif name == "conv_short_w":
                    for k in range(nr):
                        o[r0 + k:r0 + k + 1, :] = p[k]
                else:
                    o[r0:r0 + nr, :] = p[0].reshape(nr, PACK_W)

    shape = jax.ShapeDtypeStruct((SMALL_ROWS, PACK_W), F32)
    return pl.pallas_call(body, name="pack_small", out_shape=[shape] * len(dicts), compiler_params=_params())(*operands)


def full_small(g4):
    def body(p, csw, lcw, wa, wx, fcw):
        chips = range(N_CHIPS)
        r0 = SMALL_OFFSET["conv_short_w"]
        csw[...] = jnp.concatenate([p[c, r0:r0 + 3, :] for c in chips], axis=1)
        r0 = SMALL_OFFSET["lru_conv_w"]
        lcw[...] = jnp.concatenate([p[c, r0:r0 + 4, :] for c in chips], axis=1)
        for name, o in (("lru_wa", wa), ("lru_wx", wx)):
            r0 = SMALL_OFFSET[name]
            for h in range(LRU_HEADS):
                for c in chips:
                    o[h, c * QUARTER:(c + 1) * QUARTER, :] = p[c, r0 + h * QUARTER:r0 + (h + 1) * QUARTER, :].astype(BF16)
        r0 = SMALL_OFFSET["ffn_conv_w"]
        for k in range(3):
            fcw[k:k + 1, :] = jnp.concatenate(
                [p[c, r0 + FFN_ROWS * k + s:r0 + FFN_ROWS * k + s + 1, :] for c in chips for s in range(FFN_ROWS)], axis=1)

    mat = jax.ShapeDtypeStruct((LRU_HEADS, HEAD_DIM, HEAD_DIM), BF16)
    csw, lcw, wa, wx, fcw = pl.pallas_call(
        body, name="full_small",
        out_shape=[jax.ShapeDtypeStruct((3, D_MODEL), F32), jax.ShapeDtypeStruct((4, D_MODEL), F32), mat, mat,
                   jax.ShapeDtypeStruct((3, 2 * D_FF), F32)],
        compiler_params=_params())(g4)

    def bias(name):
        r0 = SMALL_OFFSET[name]
        return g4[:, r0:r0 + LRU_HEADS, :QUARTER].transpose(1, 0, 2).reshape(1, D_MODEL)

    return dict(conv_short_w=csw, lru_conv_w=lcw, lru_wa=wa, lru_wx=wx, ffn_conv_w=fcw,
                lru_ba=bias("lru_ba"), lru_bx=bias("lru_bx"))


def split_small(full):
    def bias(name):
        return full[name].reshape(LRU_HEADS, N_CHIPS, QUARTER).transpose(1, 0, 2)

    def body(csw, lcw, wa, wx, fcw, ba, bx, o):
        o[...] = jnp.zeros_like(o)
        for c in range(N_CHIPS):
            cols = slice(c * PACK_W, (c + 1) * PACK_W)
            r0 = SMALL_OFFSET["conv_short_w"]
            o[c, r0:r0 + 3, :] = csw[:, cols]
            r0 = SMALL_OFFSET["lru_conv_w"]
            o[c, r0:r0 + 4, :] = lcw[:, cols]
            for name, p in (("lru_wa", wa), ("lru_wx", wx)):
                r0 = SMALL_OFFSET[name]
                for h in range(LRU_HEADS):
                    o[c, r0 + h * QUARTER:r0 + (h + 1) * QUARTER, :] = p[h, c * QUARTER:(c + 1) * QUARTER, :]
            for name, p in (("lru_ba", ba), ("lru_bx", bx)):
                r0 = SMALL_OFFSET[name]
                o[c, r0:r0 + LRU_HEADS, 0:QUARTER] = p[c]
            r0 = SMALL_OFFSET["ffn_conv_w"]
            for k in range(3):
                for s in range(FFN_ROWS):
                    lo = c * FFN_SHARD + s * PACK_W
                    o[c, r0 + FFN_ROWS * k + s:r0 + FFN_ROWS * k + s + 1, :] = fcw[k:k + 1, lo:lo + PACK_W]

    return pl.pallas_call(
        body, name="split_small", out_shape=jax.ShapeDtypeStruct((N_CHIPS, SMALL_ROWS, PACK_W), F32),
        compiler_params=_params(),
    )(full["conv_short_w"], full["lru_conv_w"], full["lru_wa"], full["lru_wx"], full["ffn_conv_w"],
      bias("lru_ba"), bias("lru_bx"))


def pack_repl(dicts, loss=None):
    operands = [d[n] for d in dicts for n in REPL] + ([loss] if loss is not None else [])

    def body(*refs):
        ins, outs = refs[:len(operands)], refs[len(operands):]
        for i, o in enumerate(outs):
            o[...] = jnp.zeros_like(o)
            r0 = 0
            for p in ins[i * len(REPL):(i + 1) * len(REPL)]:
                for s in range(p.shape[1] // D_MODEL):
                    o[r0:r0 + 1, :] = p[:, s * D_MODEL:(s + 1) * D_MODEL]
                    r0 += 1
        if loss is not None:
            outs[-1][LOSS_ROW:LOSS_ROW + 1, :] = jnp.tile(ins[-1][...], (1, D_MODEL // 128))

    shape = jax.ShapeDtypeStruct((REPL_ROWS, D_MODEL), F32)
    return pl.pallas_call(body, name="pack_repl" + ("_loss" if loss is not None else ""),
                          out_shape=[shape] * len(dicts), compiler_params=_params())(*operands)


def _lane_concat(ref, r0, n):
    return jnp.concatenate([ref[r0 + s:r0 + s + 1, :] for s in range(n)], axis=1)


def unpack_small(packs):
    names = [n for n, _, _ in SMALL_PARTS]

    def body(*refs):
        ins, outs = refs[:len(packs)], refs[len(packs):]
        for i, p in enumerate(ins):
            for (name, nr, shape), o in zip(SMALL_PARTS, outs[i * len(names):(i + 1) * len(names)]):
                r0 = SMALL_OFFSET[name]
                if name in BIASES:
                    o[0] = p[r0:r0 + nr, 0:QUARTER]
                elif name == "ffn_conv_w":
                    for k in range(shape[1]):
                        o[k] = _lane_concat(p, r0 + FFN_ROWS * k, FFN_ROWS)
                elif name == "conv_short_w":
                    for k in range(nr):
                        o[k] = p[r0 + k:r0 + k + 1, :]
                else:
                    o[0] = p[r0:r0 + nr, :].reshape(shape[1:])

    shapes = [jax.ShapeDtypeStruct((s[1], 1, s[2]) if n in TAPS3 else s, F32) for n, _, s in SMALL_PARTS]
    res = pl.pallas_call(body, name="unpack_small", out_shape=shapes * len(packs), compiler_params=_params())(*packs)
    out = []
    for i in range(len(packs)):
        d = dict(zip(names, res[i * len(names):(i + 1) * len(names)]))
        for n in TAPS3:
            d[n] = d[n].transpose(1, 0, 2)
        out.append(d)
    return out


def unpack_repl(packs):
    rows = [(2 * D_FF // D_MODEL) if n == "ffn_conv_b" else 1 for n in REPL]

    def body(*refs):
        ins, outs = refs[:len(packs)], refs[len(packs):]
        for i, p in enumerate(ins):
            r0 = 0
            for nr, o in zip(rows, outs[i * len(REPL):(i + 1) * len(REPL)]):
                o[...] = _lane_concat(p, r0, nr)
                r0 += nr

    shapes = [jax.ShapeDtypeStruct((1, nr * D_MODEL), F32) for nr in rows]
    res = pl.pallas_call(body, name="unpack_repl", out_shape=shapes * len(packs), compiler_params=_params())(*packs)
    return [dict(zip(REPL, res[i * len(REPL):(i + 1) * len(REPL)])) for i in range(len(packs))]


def kernel(x, norm_mix_pre, norm_mix_post, norm_ffn_pre, norm_ffn_post, w_in, conv_short_w, w_conv_branch, lru_conv_w, lru_conv_b, lru_wa, lru_ba, lru_wx, lru_bx, lru_lambda, w_lru_branch, w_out, ffn_w_up, ffn_conv_w, ffn_conv_b, ffn_w_down, loss_target, m_norm_mix_pre, m_norm_mix_post, m_norm_ffn_pre, m_norm_ffn_post, m_w_in, m_conv_short_w, m_w_conv_branch, m_lru_conv_w, m_lru_conv_b, m_lru_wa, m_lru_ba, m_lru_wx, m_lru_bx, m_lru_lambda, m_w_lru_branch, m_w_out, m_ffn_w_up, m_ffn_conv_w, m_ffn_conv_b, m_ffn_w_down, v_norm_mix_pre, v_norm_mix_post, v_norm_ffn_pre, v_norm_ffn_post, v_w_in, v_conv_short_w, v_w_conv_branch, v_lru_conv_w, v_lru_conv_b, v_lru_wa, v_lru_ba, v_lru_wx, v_lru_bx, v_lru_lambda, v_w_lru_branch, v_w_out, v_ffn_w_up, v_ffn_conv_w, v_ffn_conv_b, v_ffn_w_down):
    given = dict(locals())
    w = {n: given[n] for n in WEIGHTS}
    m = {n: given["m_" + n] for n in WEIGHTS}
    v = {n: given["v_" + n] for n in WEIGHTS}

    xi, yi, ci = _position()
    chip_i = 2 * xi + yi
    chip = chip_i.astype(jnp.int32).reshape(1)
    core = ci.astype(jnp.int32).reshape(1)
    xs, target = x[0], loss_target[0]
    g1, g2, g3, g4 = w["norm_mix_pre"], w["norm_mix_post"], w["norm_ffn_pre"], w["norm_ffn_post"]
    shard = {n: cast_bf16(w[n][0], "cast_" + n) for n in BIG}
    small_shard, m_small, v_small = pack_small([w, m, v])

    def gathered(bufs, names):
        return [_own_slot(b, small_shard if n == "small" else shard[n], chip_i) for b, n in zip(bufs, names)]

    def chip_sums(arrays, kinds, tag):
        theirs = pair_split(arrays, kinds, "pair_split_" + tag)
        return [add_pair(g, k, t, core, "pair_add_%s_%d" % (tag, i)) for i, (g, k, t) in enumerate(zip(arrays, kinds, theirs))]

    h1, h1t = norm_in(xs, g1)
    half_in = D_MODEL // 2
    first = gather_ride([shard["w_in"], small_shard], items=[(0, 0, half_in), (0, half_in, half_in), (1, 0, SMALL_ROWS)])
    win4, small4 = gathered(run_ride(first, "gather_first"), ("w_in", "small"))
    small = full_small(small4)
    first_up = 256
    (proj,), got = matmul_cols(
        h1, win4, "proj_fwd",
        ride=gather_ride([shard["w_conv_branch"], shard["w_lru_branch"], shard["w_out"], shard["ffn_w_up"]],
                         items=[(0, 0, 256), (1, 0, 256), (2, 0, 256), (3, 0, first_up)]))
    wcb, wlb, wout = [g.reshape(-1, D_MODEL) for g in gathered(got[:3], ("w_conv_branch", "w_lru_branch", "w_out"))]
    got = got[3:]
    up_piece = lambda r0, nr, into=None: gather_ride([shard["ffn_w_up"]], items=[(0, r0, nr)], into=into)
    down_piece = lambda r0, nr, into=None: gather_ride([shard["ffn_w_down"]], items=[(0, r0, nr)], into=into)
    q, ya = mix_conv_fwd(proj, small["conv_short_w"])
    (xl, r, gi, h, yb), got = mix_lru_fwd(
        proj, small["lru_conv_w"], w["lru_conv_b"], small["lru_wa"], small["lru_ba"],
        small["lru_wx"], small["lru_bx"], w["lru_lambda"], ride=up_piece(first_up, 512, got))
    (a, b, merged), got = branch_merge_fwd(ya, yb, wcb, wlb, proj, ride=up_piece(first_up + 512, 256, got))
    (wup4,) = gathered(got, ("ffn_w_up",))
    (mix, x2, h2, h2t), got = mix_out_fwd(merged, wout, xs, g2, g3, ride=down_piece(0, 256))
    (up, act, f), got = ffn_up_act_fwd(h2, wup4, small["ffn_conv_w"], w["ffn_conv_b"], ride=down_piece(256, 512, got))
    wdown = gathered(got, ("ffn_w_down",))[0].reshape(-1, D_MODEL)
    dy, dout, loss, dg4 = ffn_down_loss(f, wdown, x2, target, g4)

    dh2, dwup, dwdown, dfw, dfb = ffn_up_bwd(dout, wdown, up, act, f, small["ffn_conv_w"], wup4, h2t)
    cs_down, cs_up = chip_sums([dwdown, dwup], ["row", "col2"], "ffn")
    down_rows = lambda r0, nr, into=None: exchange_ride([cs_down], items=[(0, r0, nr)], into=into)
    up_rows = lambda r0, nr, into=None: exchange_ride([cs_up], items=[(0, r0, nr)], into=into)
    (dx2, dmix, dg3, dg2), rx_down = norms_mid_bwd(dh2, x2, dy, mix, g3, g2, ride=down_rows(0, 128))
    (da, db, dwout, dgates), rx_down = mix_out_bwd(dmix, wout, merged, a, b, proj, ride=down_rows(128, 256, rx_down))
    (dconv, dwcb, dws), rx_up = mix_conv_bwd(da, wcb, proj, q, small["conv_short_w"], ride=up_rows(0, 176))
    cs_mid = chip_sums([dwout, dwcb], ["row", "row"], "mid")
    (dlru, dwlb, dwa, dwx, dba, dbx, dwl, dbl, dlam), rx_up = mix_lru_bwd(
        db, wlb, proj, xl, r, gi, h, small["lru_conv_w"], small["lru_wa"], small["lru_wx"], w["lru_lambda"],
        ride=up_rows(176, 336, rx_up))
    grads = dict(norm_mix_post=dg2, norm_ffn_pre=dg3, norm_ffn_post=dg4, conv_short_w=dws, lru_conv_w=dwl,
                 lru_conv_b=dbl, lru_wa=dwa, lru_ba=dba, lru_wx=dwx, lru_bx=dbx, lru_lambda=dlam,
                 ffn_conv_w=jnp.concatenate([dfw[0], dfw[1]], axis=1), ffn_conv_b=jnp.concatenate([dfb[0], dfb[1]], axis=1))
    cs_late = chip_sums([dwlb, split_small(grads)], ["row", "lead"], "late")
    dproj = [dconv, dlru, dgates]
    (dwin,), rx_all = matmul_cols_bwd(dproj, h1t, "proj_wgrad", True, ride=exchange_ride(cs_mid + cs_late))
    rx_mid, rx_late = rx_all[:2], rx_all[2:]
    cs_in = chip_sums([dwin], ["col"], "in")
    in_rows = lambda r0, nr, into=None: exchange_ride(cs_in, items=[(0, r0, nr)], into=into)
    (dh1,), rx_in = matmul_cols_bwd(dproj, win4, "proj_dgrad", False, ride=in_rows(0, 384))
    dx, grads["norm_mix_pre"] = norm_in_bwd(dh1, xs, dx2, g1)
    (rep_part,) = pack_repl([grads], loss)
    rx_in, rep_all = run_ride(exchange_ride(cs_in, items=[(0, 384, 128)], into=rx_in, rep=rep_part), "exchange_last")

    order = (("w_in", cs_in[0], rx_in), ("ffn_w_up", cs_up, rx_up[0]), ("w_conv_branch", cs_mid[1], rx_mid[1]),
             ("w_lru_branch", cs_late[0], rx_late[0]), ("w_out", cs_mid[0], rx_mid[0]),
             ("ffn_w_down", cs_down, rx_down[0]), ("small", cs_late[1], rx_late[1]))
    halves = [sum_chips(rx, cs, chip, "chip_sum_" + n) for n, cs, rx in order]
    me = 4 * xi + 2 * yi + ci
    rep_grad = sum_lead(_own_slot(rep_all, rep_part, me), "device_sum")
    others = pair_swap(halves)

    g_out, d_out, m_out, v_out = {}, {}, {}, {}
    for n, gm, go in zip(BIG, halves[:-1], others[:-1]):
        g, d, nm, nv = adamw_halves(w[n][0], gm, go, m[n][0], v[n][0], core, "adamw_" + n)
        g_out[n], d_out[n], m_out[n], v_out[n] = g[None], d[None], nm[None], nv[None]
    bufs = adamw_halves(small_shard, halves[-1], others[-1], m_small, v_small, core, "adamw_small")
    for dst, part in zip((g_out, d_out, m_out, v_out), unpack_small(bufs)):
        dst.update(part)
    w_rep, m_rep, v_rep = pack_repl([w, m, v])
    d, nm, nv = adamw(w_rep, rep_grad, m_rep, v_rep, "adamw_repl")
    for dst, part in zip((g_out, d_out, m_out, v_out), unpack_repl([rep_grad, d, nm, nv])):
        dst.update(part)

    return (rep_grad[LOSS_ROW, 0], dx[None], *[g_out[n] for n in WEIGHTS], *[d_out[n] for n in WEIGHTS],
            *[m_out[n] for n in WEIGHTS], *[v_out[n] for n in WEIGHTS])
```

```python
import functools
import math

import jax
import jax.numpy as jnp
from jax import lax
from jax.experimental import pallas as pl
from jax.experimental.pallas import tpu as pltpu

F32 = jnp.float32
BF16 = jnp.bfloat16

D_MODEL = 1024
N_CHIPS = 4
N_SEG = 7
D_FF = 3 * D_MODEL
LRU_HEADS = 4
HEAD_DIM = D_MODEL // LRU_HEADS
LRU_C = 8.0
RMS_EPS = 1e-6
CW = 256
FW = 256
SUBLANES = 8
SCAN_UNROLL = 8
VMEM_LIMIT = 58 * 1024 * 1024

ADAM_LR = 0.001
ADAM_B1 = 0.9
ADAM_B2 = 0.999
ADAM_EPS = 1e-08
ADAM_WD = 0.01
ADAM_STEP = 10

_GELU_C = math.sqrt(2.0 / math.pi)
_GELU_K = 0.044715


def _params(**kw):
    return pltpu.CompilerParams(vmem_limit_bytes=VMEM_LIMIT, **kw)


def _sigmoid(x):
    return 1.0 / (1.0 + jnp.exp(-x))


def _gelu(x):
    t = jnp.tanh(_GELU_C * (x + _GELU_K * x * x * x))
    return 0.5 * x * (1.0 + t)


def _gelu_and_grad(x):
    x2 = x * x
    t = jnp.tanh(_GELU_C * (x + _GELU_K * x * x2))
    g = 0.5 * x * (1.0 + t)
    dg = 0.5 * (1.0 + t) + 0.5 * x * (1.0 - t * t) * _GELU_C * (1.0 + 3.0 * _GELU_K * x2)
    return g, dg


def _log_sigmoid(x):
    e = jnp.exp(-jnp.abs(x))
    u = 1.0 + e
    l1p = jnp.where(u == 1.0, e, jnp.log(u) * e / (u - 1.0))
    return jnp.minimum(x, 0.0) - l1p


def _neg_expm1(z):
    series = -z * (1.0 + z * (0.5 + z * (1.0 / 6.0 + z * (1.0 / 24.0 + z * (1.0 / 120.0 + z * (1.0 / 720.0))))))
    return jnp.where(z > -0.2, series, 1.0 - jnp.exp(z))


def _rows(shape):
    return lax.broadcasted_iota(jnp.int32, shape, 0)


def _shift_down(x, k):
    return jnp.where(_rows(x.shape) >= k, pltpu.roll(x, k, 0), 0.0)


def _shift_up(x, k):
    n = x.shape[0]
    return jnp.where(_rows(x.shape) < n - k, pltpu.roll(x, n - k, 0), 0.0)


def _delays(x, k_width):
    return [x] + [_shift_down(x, j) for j in range(1, k_width)]


def _advances(dy, k_width):
    return [dy] + [_shift_up(dy, j) for j in range(1, k_width)]


def _taps_sum(shifted, w_ref, b=None):
    k_width = w_ref.shape[0]
    y = w_ref[k_width - 1:k_width, :] * shifted[0]
    for j in range(1, k_width):
        y = y + w_ref[k_width - 1 - j:k_width - j, :] * shifted[j]
    if b is not None:
        y = y + b
    return y


def _causal_conv(x, w_ref, b=None):
    return _taps_sum(_delays(x, w_ref.shape[0]), w_ref, b)


def _conv_wgrad(advanced, x):
    k_width = len(advanced)
    rows = [jnp.sum(advanced[k_width - 1 - k] * x, axis=0, keepdims=True) for k in range(k_width)]
    return jnp.concatenate(rows, axis=0)


def _dot(a, b):
    return jnp.dot(a, b, preferred_element_type=F32)


def _dot_nt(a, b):
    return lax.dot_general(a, b, (((1,), (1,)), ((), ())), preferred_element_type=F32)


def _dot_tn(a, b):
    return lax.dot_general(a, b, (((0,), (0,)), ((), ())), preferred_element_type=F32)


def _rms_stats(x):
    r = lax.rsqrt(jnp.mean(x * x, axis=-1, keepdims=True) + RMS_EPS)
    return x * r, r


def _rms_bwd(n, r, g, dy):
    dn = dy * g
    dx = r * (dn - n * jnp.mean(dn * n, axis=-1, keepdims=True))
    return dx, dy * n


def _scan(a_ref, b_ref, h_ref, reverse):
    n, c = a_ref.shape
    row = lax.broadcasted_iota(jnp.int32, (SUBLANES, c), 0)
    span = SCAN_UNROLL * SUBLANES
    n_trips = n // span

    def within(a, b):
        for k in (1, 2, 4):
            if reverse:
                keep, shift = row < SUBLANES - k, SUBLANES - k
            else:
                keep, shift = row >= k, k
            ap = jnp.where(keep, pltpu.roll(a, shift, 0), 1.0)
            bp = jnp.where(keep, pltpu.roll(b, shift, 0), 0.0)
            b = a * bp + b
            a = a * ap
        return a, b

    def trip(t, carry):
        base = pl.multiple_of((n_trips - 1 - t if reverse else t) * span, span)
        order = list(reversed(range(SCAN_UNROLL))) if reverse else list(range(SCAN_UNROLL))
        loaded = [(a_ref[pl.ds(base + u * SUBLANES, SUBLANES), :], b_ref[pl.ds(base + u * SUBLANES, SUBLANES), :])
                  for u in order]
        out = []
        for a, b in [within(a, b) for a, b in loaded]:
            h = a * carry + b
            out.append(h)
            carry = h[0:1, :] if reverse else h[SUBLANES - 1:SUBLANES, :]
        for u, h in zip(order, out):
            h_ref[pl.ds(base + u * SUBLANES, SUBLANES), :] = h
        return carry

    lax.fori_loop(0, n_trips, trip, jnp.zeros((1, c), F32))


def _scan_forward(a_ref, b_ref, h_ref):
    _scan(a_ref, b_ref, h_ref, False)


def _scan_backward(c_ref, b_ref, g_ref):
    _scan(c_ref, b_ref, g_ref, True)


MESH = pl.DeviceIdType.MESH
_HBM = pl.BlockSpec(memory_space=pltpu.HBM)
_OTHER_CHIPS = ((1, 0), (0, 1), (1, 1))
_OTHER_DEVICES = tuple((dx, dy, dc) for dx in (0, 1) for dy in (0, 1) for dc in (0, 1) if dx or dy or dc)
N_DEVICES = 8


def _position():
    return lax.axis_index("x"), lax.axis_index("y"), lax.axis_index("c")


def _flip(v, d):
    return 1 - v if d else v


def _chip(x, y, p):
    px, py = _flip(x, _OTHER_CHIPS[p][0]), _flip(y, _OTHER_CHIPS[p][1])
    return px, py, 2 * px + py


class _Ride:
    def __init__(self, srcs, bufs, scratch, plan, collective_id):
        self.srcs, self.bufs, self.scratch, self.plan = list(srcs), list(bufs), list(scratch), plan
        self.collective_id = collective_id


NEIGHBOURS_AND_SIBLING = 1
OTHER_CHIPS_SAME_CORE = 2
ALL_DEVICES = 3
SIBLING = 4


def _handshake(peers):
    barrier = pltpu.get_barrier_semaphore()
    for peer in peers:
        pl.semaphore_signal(barrier, inc=1, device_id=peer, device_id_type=MESH)
    pl.semaphore_wait(barrier, len(peers))


def _call(body, *, name, grid, in_specs, out_specs, out_shape, operands, scratch_shapes=(), ride=None):
    in_specs, out_specs, out_shape = list(in_specs), list(out_specs), list(out_shape)
    scratch_shapes = list(scratch_shapes)
    if ride is None:
        return pl.pallas_call(body, name=name, grid=grid, in_specs=in_specs, out_specs=out_specs, out_shape=out_shape,
                              scratch_shapes=scratch_shapes, compiler_params=_params())(*operands)
    n_in, n_out, n_scr = len(in_specs), len(out_shape), len(scratch_shapes)
    old = [i for i, b in enumerate(ride.bufs) if not isinstance(b, jax.ShapeDtypeStruct)]
    n_src, n_old, n_buf = len(ride.srcs), len(old), len(ride.bufs)

    def full_body(*refs):
        o0 = n_in + n_src + n_old
        s0 = o0 + n_out + n_buf
        start, relay, relay_on, finish = ride.plan(refs[n_in:n_in + n_src], refs[o0 + n_out:s0], refs[s0 + n_scr:])
        ids = [pl.program_id(i) for i in range(len(grid))]
        first = functools.reduce(jnp.logical_and, [i == 0 for i in ids])
        middle = functools.reduce(jnp.logical_and, [ids[0] == grid[0] // 2] + [i == 0 for i in ids[1:]])
        last = functools.reduce(jnp.logical_and, [i == g - 1 for i, g in zip(ids, grid)])
        pl.when(first)(start)
        pl.when(middle)(relay)
        pl.when(last)(relay_on)
        body(*refs[:n_in], *refs[o0:o0 + n_out], *refs[s0:s0 + n_scr])
        pl.when(last)(finish)

    shapes = [jax.ShapeDtypeStruct(b.shape, b.dtype) for b in ride.bufs]
    res = pl.pallas_call(
        full_body, name=name, grid=grid,
        in_specs=in_specs + [_HBM] * (n_src + n_old), out_specs=out_specs + [_HBM] * n_buf,
        out_shape=out_shape + shapes, scratch_shapes=scratch_shapes + ride.scratch,
        input_output_aliases={n_in + n_src + k: n_out + i for k, i in enumerate(old)},
        compiler_params=_params(collective_id=ride.collective_id),
    )(*operands, *ride.srcs, *[ride.bufs[i] for i in old])
    return list(res[:n_out]), list(res[n_out:])


def run_ride(ride, name):
    def body():
        pass

    return _call(body, name=name, grid=(1,), in_specs=[], out_specs=[], out_shape=[], operands=[], ride=ride)[1]


def gather_ride(shards, items=None, into=None):
    items = items or [(a, 0, s.shape[0]) for a, s in enumerate(shards)]
    bufs = into or [jax.ShapeDtypeStruct((N_CHIPS,) + s.shape, s.dtype) for s in shards]
    nrel = len(_OTHER_CHIPS)

    def plan(srcs, dsts, sems):
        ici_send, ici_recv, hop_send, hop_recv, sib_send, sib_recv = sems
        x, y, c = _position()
        j = 2 * x + y

        def rows(ref, it, h, q=None):
            half = it[2] // 2
            if q is None:
                return ref.at[pl.ds(it[1] + h * half, half), :]
            return ref.at[pl.ds(it[1] + h * half + q * (half // 2), half // 2), :]

        def ici(i, p, slot):
            it = items[i]
            px, py, _ = _chip(x, y, p)
            return pltpu.make_async_remote_copy(
                src_ref=rows(srcs[it[0]], it, c), dst_ref=rows(dsts[it[0]].at[slot], it, c),
                send_sem=ici_send.at[i * nrel + p], recv_sem=ici_recv.at[i * nrel + p],
                device_id=(px, py, c), device_id_type=MESH)

        def hop(i, p, slot):
            it = items[i]
            part = rows(dsts[it[0]].at[slot], it, c, p)
            px, py, _ = _chip(x, y, 1 - p)
            return pltpu.make_async_remote_copy(
                src_ref=part, dst_ref=part, send_sem=hop_send.at[i * 2 + p], recv_sem=hop_recv.at[i * 2 + p],
                device_id=(px, py, c), device_id_type=MESH)

        def sib(i, p, h):
            it = items[i]
            part = rows(dsts[it[0]].at[_chip(x, y, p)[2]], it, h)
            return pltpu.make_async_remote_copy(
                src_ref=part, dst_ref=part, send_sem=sib_send.at[i * nrel + p], recv_sem=sib_recv.at[i * nrel + p],
                device_id=(x, y, 1 - c), device_id_type=MESH)

        every = range(len(items))
        diag = _chip(x, y, 2)[2]

        def start():
            _handshake([_chip(x, y, 0)[:2] + (c,), _chip(x, y, 1)[:2] + (c,), (x, y, 1 - c)])
            for i in every:
                for p in (0, 1):
                    ici(i, p, j).start()

        def relay():
            for i in every:
                for p in (0, 1):
                    k = _chip(x, y, p)[2]
                    ici(i, p, k).wait_recv()
                    hop(i, p, k).start()
                    sib(i, p, c).start()

        def relay_on():
            for i in every:
                for p in (0, 1):
                    hop(i, p, diag).wait_recv()
                sib(i, 2, c).start()

        def finish():
            for i in every:
                for p in range(nrel):
                    sib(i, p, 1 - c).wait_recv()
            for i in every:
                for p in (0, 1):
                    ici(i, p, j).wait_send()
                    hop(i, p, _chip(x, y, p)[2]).wait_send()
                for p in range(nrel):
                    sib(i, p, c).wait_send()

        return start, relay, relay_on, finish

    n = len(items)
    sems = [pltpu.SemaphoreType.DMA((n * nrel,))] * 2 + [pltpu.SemaphoreType.DMA((n * 2,))] * 2 \
        + [pltpu.SemaphoreType.DMA((n * nrel,))] * 2
    return _Ride(shards, bufs, sems, plan, NEIGHBOURS_AND_SIBLING)


def exchange_ride(sums, items=None, into=None, rep=None):
    items = [(a, 0, s.shape[1]) for a, s in enumerate(sums)] if items is None else items
    into = into or [None] * len(sums)
    bufs = [jax.ShapeDtypeStruct(s.shape, s.dtype) if b is None else b for s, b in zip(sums, into)]
    srcs = list(sums)
    scratch = [pltpu.SemaphoreType.DMA((max(len(items), 1) * len(_OTHER_CHIPS),))] * 2
    if rep is not None:
        srcs.append(rep)
        bufs.append(jax.ShapeDtypeStruct((N_DEVICES,) + rep.shape, rep.dtype))
        scratch += [pltpu.SemaphoreType.DMA((len(_OTHER_DEVICES),))] * 2
    nrel = len(_OTHER_CHIPS)

    def plan(src_refs, dst_refs, sems):
        x, y, c = _position()
        j = 2 * x + y
        me = 4 * x + 2 * y + c

        def part(i, p, src_slot, dst_slot):
            a, r0, nr = items[i]
            px, py, _ = _chip(x, y, p)
            return pltpu.make_async_remote_copy(
                src_ref=src_refs[a].at[src_slot, pl.ds(r0, nr), :], dst_ref=dst_refs[a].at[dst_slot, pl.ds(r0, nr), :],
                send_sem=sems[0].at[i * nrel + p], recv_sem=sems[1].at[i * nrel + p],
                device_id=(px, py, c), device_id_type=MESH)

        def device(q):
            dx, dy, dc = _OTHER_DEVICES[q]
            return _flip(x, dx), _flip(y, dy), _flip(c, dc)

        def rep_copy(q, slot):
            return pltpu.make_async_remote_copy(
                src_ref=src_refs[-1], dst_ref=dst_refs[-1].at[slot], send_sem=sems[2].at[q], recv_sem=sems[3].at[q],
                device_id=device(q), device_id_type=MESH)

        pairs = [(i, p) for i in range(len(items)) for p in range(nrel)]
        others = range(len(_OTHER_DEVICES)) if rep is not None else ()

        def start():
            if rep is None:
                _handshake([_chip(x, y, p)[:2] + (c,) for p in range(nrel)])
            else:
                _handshake([device(q) for q in others])
            for i, p in pairs:
                part(i, p, _chip(x, y, p)[2], j).start()
            for q in others:
                rep_copy(q, me).start()

        def finish():
            for i, p in pairs:
                k = _chip(x, y, p)[2]
                part(i, p, k, k).wait_recv()
            for q in others:
                px, py, pc = device(q)
                rep_copy(q, 4 * px + 2 * py + pc).wait_recv()
            for i, p in pairs:
                part(i, p, _chip(x, y, p)[2], j).wait_send()
            for q in others:
                rep_copy(q, me).wait_send()

        return start, lambda: None, lambda: None, finish

    return _Ride(srcs, bufs, scratch, plan, OTHER_CHIPS_SAME_CORE if rep is None else ALL_DEVICES)


def _own_slot(buf, own, index):
    return lax.dynamic_update_slice(buf, own[None], (index,) + (0,) * own.ndim)


def _token_tile(s):
    return min(s, 512)


def norm_in(x, g):
    s, d = x.shape
    t = _token_tile(s)

    def body(x_ref, g_ref, o_ref, ot_ref):
        n, _ = _rms_stats(x_ref[...])
        h = n * g_ref[...]
        o_ref[...] = h.astype(BF16)
        ot_ref[...] = h.T.astype(BF16)

    return pl.pallas_call(
        body, name="norm_in", grid=(s // t,),
        in_specs=[pl.BlockSpec((t, d), lambda i: (i, 0)), pl.BlockSpec((1, d), lambda i: (0, 0))],
        out_specs=[pl.BlockSpec((t, d), lambda i: (i, 0)), pl.BlockSpec((d, t), lambda i: (0, i))],
        out_shape=[jax.ShapeDtypeStruct((s, d), BF16), jax.ShapeDtypeStruct((d, s), BF16)],
        compiler_params=_params(),
    )(x, g)


def matmul_cols(a, w4, name, ride=None):
    m, k = a.shape
    nj, _, ns = w4.shape
    nb = ns // CW

    def body(a_ref, w_ref, o_ref):
        o_ref[...] = _dot(a_ref[...], w_ref[0])

    return _call(
        body, name=name, grid=(nj, nb),
        in_specs=[pl.BlockSpec((m, k), lambda j, b: (0, 0)),
                  pl.BlockSpec((1, k, CW), lambda j, b: (j, 0, b))],
        out_specs=[pl.BlockSpec((m, CW), lambda j, b: (0, j * nb + b))],
        out_shape=[jax.ShapeDtypeStruct((m, nj * ns), F32)],
        operands=(a, w4), ride=ride)


def mix_conv_fwd(proj, ws, ride=None):
    s = proj.shape[0]
    nblk = D_MODEL // CW

    def body(cb_ref, cc_ref, cx_ref, ws_ref, q_ref, ya_ref):
        q = _causal_conv(cc_ref[...] * cx_ref[...], ws_ref)
        q_ref[...] = q
        ya_ref[...] = (cb_ref[...] * q).astype(BF16)

    seg = lambda k: pl.BlockSpec((s, CW), lambda c, k=k: (0, k * nblk + c))
    return _call(
        body, name="mix_conv_fwd", grid=(nblk,),
        in_specs=[seg(0), seg(1), seg(2), pl.BlockSpec((3, CW), lambda c: (0, c))],
        out_specs=[pl.BlockSpec((s, CW), lambda c: (0, c))] * 2,
        out_shape=[jax.ShapeDtypeStruct((s, D_MODEL), F32), jax.ShapeDtypeStruct((s, D_MODEL), BF16)],
        operands=(proj, proj, proj, ws), ride=ride)


def _lru_gates(r, ls):
    log_a = LRU_C * r * ls
    a = jnp.exp(log_a)
    mult = jnp.sqrt(_neg_expm1(2.0 * log_a))
    mult = jnp.where(_rows(r.shape) == 0, 1.0, mult)
    return a, mult


def mix_lru_fwd(proj, wl, bl, wa, ba, wx, bx, lam, ride=None):
    s = proj.shape[0]
    nblk = D_MODEL // CW

    def body(lx_ref, ly_ref, wl_ref, bl_ref, wa_ref, ba_ref, wx_ref, bx_ref, lam_ref,
             xl_ref, r_ref, i_ref, h_ref, yb_ref, a_scr, u_scr):
        xl = _causal_conv(lx_ref[...], wl_ref, bl_ref[...])
        xlb = xl.astype(BF16)
        xl_ref[...] = xlb
        r = _sigmoid(_dot(xlb, wa_ref[0]) + ba_ref[...])
        i = _sigmoid(_dot(xlb, wx_ref[0]) + bx_ref[...])
        r_ref[...] = r.astype(BF16)
        i_ref[...] = i.astype(BF16)
        a, mult = _lru_gates(r, _log_sigmoid(lam_ref[...]))
        a_scr[...] = a
        u_scr[...] = mult * i * xl
        _scan_forward(a_scr, u_scr, h_ref)
        yb_ref[...] = (h_ref[...] * _gelu(ly_ref[...])).astype(BF16)

    blk = lambda k: pl.BlockSpec((s, CW), lambda c, k=k: (0, k * nblk + c))
    vec = pl.BlockSpec((1, CW), lambda c: (0, c))
    mat = pl.BlockSpec((1, CW, CW), lambda c: (c, 0, 0))
    out = pl.BlockSpec((s, CW), lambda c: (0, c))
    f = jax.ShapeDtypeStruct((s, D_MODEL), F32)
    hb = jax.ShapeDtypeStruct((s, D_MODEL), BF16)
    return _call(
        body, name="mix_lru_fwd", grid=(nblk,),
        in_specs=[blk(3), blk(4), pl.BlockSpec((4, CW), lambda c: (0, c)), vec, mat, vec, mat, vec, vec],
        out_specs=[out] * 5,
        out_shape=[hb, hb, hb, f, hb],
        scratch_shapes=[pltpu.VMEM((s, CW), F32), pltpu.VMEM((s, CW), F32)],
        operands=(proj, proj, wl, bl, wa, ba, wx, bx, lam), ride=ride)


def branch_merge_fwd(ya, yb, wcb, wlb, proj, ride=None):
    s = ya.shape[0]
    nblk = D_MODEL // CW

    def body(ya_ref, yb_ref, wcb_ref, wlb_ref, gc_ref, gl_ref, a_ref, b_ref, m_ref):
        a = _dot(ya_ref[...], wcb_ref[...])
        b = _dot(yb_ref[...], wlb_ref[...])
        a_ref[...] = a
        b_ref[...] = b
        m_ref[...] = (_sigmoid(gc_ref[...]) * a + _sigmoid(gl_ref[...]) * b).astype(BF16)

    res = pl.BlockSpec((s, D_MODEL), lambda n: (0, 0))
    wcol = pl.BlockSpec((D_MODEL, CW), lambda n: (0, n))
    blk = lambda k: pl.BlockSpec((s, CW), lambda n, k=k: (0, k * nblk + n))
    out = pl.BlockSpec((s, CW), lambda n: (0, n))
    f = jax.ShapeDtypeStruct((s, D_MODEL), F32)
    return _call(
        body, name="branch_merge_fwd", grid=(nblk,),
        in_specs=[res, res, wcol, wcol, blk(5), blk(6)],
        out_specs=[out] * 3,
        out_shape=[f, f, jax.ShapeDtypeStruct((s, D_MODEL), BF16)],
        operands=(ya, yb, wcb, wlb, proj, proj), ride=ride)


def mix_out_fwd(merged, wout, x, g2, g3, ride=None):
    s, d = x.shape
    t = _token_tile(s)

    def body(m_ref, w_ref, x_ref, g2_ref, g3_ref, mix_ref, x2_ref, h2_ref, h2t_ref):
        mix = _dot(m_ref[...], w_ref[...])
        mix_ref[...] = mix
        n, _ = _rms_stats(mix)
        x2 = x_ref[...] + n * g2_ref[...]
        x2_ref[...] = x2
        n2, _ = _rms_stats(x2)
        h2 = n2 * g3_ref[...]
        h2_ref[...] = h2.astype(BF16)
        h2t_ref[...] = h2.T.astype(BF16)

    tile = pl.BlockSpec((t, d), lambda i: (i, 0))
    vec = pl.BlockSpec((1, d), lambda i: (0, 0))
    f = jax.ShapeDtypeStruct((s, d), F32)
    return _call(
        body, name="mix_out_fwd", grid=(s // t,),
        in_specs=[tile, pl.BlockSpec((d, d), lambda i: (0, 0)), tile, vec, vec],
        out_specs=[tile] * 3 + [pl.BlockSpec((d, t), lambda i: (0, i))],
        out_shape=[f, f, jax.ShapeDtypeStruct((s, d), BF16), jax.ShapeDtypeStruct((d, s), BF16)],
        operands=(merged, wout, x, g2, g3), ride=ride)


def ffn_up_act_fwd(h2, wup4, fw, fb, ride=None):
    s, k = h2.shape
    ns = wup4.shape[2]
    per_chip = ns // CW
    nblk = D_FF // CW

    def body(h_ref, wg_ref, wv_ref, cg_ref, cv_ref, bg_ref, bv_ref, up_ref, act_ref, f_ref):
        h = h_ref[...]
        ug = _dot(h, wg_ref[0])
        uv = _dot(h, wv_ref[0])
        up_ref[0] = ug
        up_ref[1] = uv
        gate = _causal_conv(ug, cg_ref, bg_ref[...])
        val = _causal_conv(uv, cv_ref, bv_ref[...])
        act_ref[0] = gate.astype(BF16)
        act_ref[1] = val.astype(BF16)
        f_ref[...] = (_gelu(gate) * val).astype(BF16)

    wcols = lambda h: pl.BlockSpec((1, k, CW), lambda n, h=h: (n // per_chip + 2 * h, 0, n % per_chip))
    half = lambda h, rows: pl.BlockSpec((rows, CW), lambda n, h=h: (0, h * nblk + n))
    both = pl.BlockSpec((2, s, CW), lambda n: (0, 0, n))
    return _call(
        body, name="ffn_up_act_fwd", grid=(nblk,),
        in_specs=[pl.BlockSpec((s, k), lambda n: (0, 0)), wcols(0), wcols(1),
                  half(0, 3), half(1, 3), half(0, 1), half(1, 1)],
        out_specs=[both, both, pl.BlockSpec((s, CW), lambda n: (0, n))],
        out_shape=[jax.ShapeDtypeStruct((2, s, D_FF), F32), jax.ShapeDtypeStruct((2, s, D_FF), BF16),
                   jax.ShapeDtypeStruct((s, D_FF), BF16)],
        operands=(h2, wup4, wup4, fw, fw, fb, fb), ride=ride)


def ffn_down_loss(f, wdown, x2, target, g4):
    s, d = x2.shape
    t = _token_tile(s)

    def body(f_ref, w_ref, x2_ref, tg_ref, g4_ref, dy_ref, dout_ref, loss_ref, dg4_ref):
        @pl.when(pl.program_id(0) == 0)
        def _():
            loss_ref[...] = jnp.zeros_like(loss_ref)
            dg4_ref[...] = jnp.zeros_like(dg4_ref)

        out = _dot(f_ref[...], w_ref[...])
        n, r = _rms_stats(out)
        err = x2_ref[...] + n * g4_ref[...] - tg_ref[...]
        loss_ref[...] += jnp.full(loss_ref.shape, (0.5 / d) * jnp.sum(err * err), F32)
        dy = err * (1.0 / d)
        dy_ref[...] = dy
        dout, dg = _rms_bwd(n, r, g4_ref[...], dy)
        dout_ref[...] = dout.astype(BF16)
        dg4_ref[...] += jnp.sum(dg, axis=0, keepdims=True)

    tile = pl.BlockSpec((t, d), lambda i: (i, 0))
    vec = pl.BlockSpec((1, d), lambda i: (0, 0))
    return pl.pallas_call(
        body, name="ffn_down_loss", grid=(s // t,),
        in_specs=[pl.BlockSpec((t, D_FF), lambda i: (i, 0)), pl.BlockSpec((D_FF, d), lambda i: (0, 0)), tile, tile, vec],
        out_specs=[tile, tile, pl.BlockSpec((1, 128), lambda i: (0, 0)), vec],
        out_shape=[jax.ShapeDtypeStruct((s, d), F32), jax.ShapeDtypeStruct((s, d), BF16),
                   jax.ShapeDtypeStruct((1, 128), F32), jax.ShapeDtypeStruct((1, d), F32)],
        compiler_params=_params(),
    )(f, wdown, x2, target, g4)


def ffn_up_bwd(dout, wdown, up, act, f, fw, wup4, h2t, ride=None):
    k, s = h2t.shape
    nblk = D_FF // FW
    per_chip = wup4.shape[2] // FW

    def body(do_ref, wd_ref, up_ref, act_ref, f_ref, cg_ref, cv_ref, wg_ref, wv_ref, h_ref,
             dh_ref, dwu_ref, dwd_ref, dw_ref, db_ref, dup_scr):
        @pl.when(pl.program_id(0) == 0)
        def _():
            dup_scr[...] = jnp.zeros_like(dup_scr)
            dh_ref[...] = jnp.zeros_like(dh_ref)

        do = do_ref[...]
        df = _dot_nt(do, wd_ref[...])
        dg = dup_scr[0]
        dv = dup_scr[1]
        ht = h_ref[...]
        dh_ref[...] += _dot_nt(dg, wg_ref[0]) + _dot_nt(dv, wv_ref[0])
        dwu_ref[0] = _dot(ht, dg).astype(BF16)
        dwu_ref[1] = _dot(ht, dv).astype(BF16)
        dwd_ref[...] = _dot_tn(f_ref[...], do).astype(BF16)
        val = act_ref[1].astype(F32)
        ge, dge = _gelu_and_grad(act_ref[0].astype(F32))
        dgate = _advances(df * val * dge, 3)
        dval = _advances(df * ge, 3)
        dw_ref[0] = _conv_wgrad(dgate, up_ref[0])
        dw_ref[1] = _conv_wgrad(dval, up_ref[1])
        db_ref[0] = jnp.sum(dgate[0], axis=0, keepdims=True)
        db_ref[1] = jnp.sum(dval[0], axis=0, keepdims=True)
        dup_scr[0] = _taps_sum(dgate, cg_ref).astype(BF16)
        dup_scr[1] = _taps_sum(dval, cv_ref).astype(BF16)

    cur = lambda n: jnp.minimum(n, nblk - 1)
    prev = lambda n: jnp.maximum(n - 1, 0)
    once = pl.Buffered(1)
    both = lambda rows: pl.BlockSpec((2, rows, FW), lambda n: (0, 0, cur(n)))
    taps = lambda h: pl.BlockSpec((3, FW), lambda n, h=h: (0, h * nblk + cur(n)))
    wcols = lambda h: pl.BlockSpec((1, k, FW), lambda n, h=h: (prev(n) // per_chip + 2 * h, 0, prev(n) % per_chip))
    return _call(
        body, name="ffn_up_bwd", grid=(nblk + 1,),
        in_specs=[pl.BlockSpec((s, D_MODEL), lambda n: (0, 0), pipeline_mode=once),
                  pl.BlockSpec((FW, D_MODEL), lambda n: (cur(n), 0)), both(s), both(s),
                  pl.BlockSpec((s, FW), lambda n: (0, cur(n))), taps(0), taps(1), wcols(0), wcols(1),
                  pl.BlockSpec((k, s), lambda n: (0, 0), pipeline_mode=once)],
        out_specs=[pl.BlockSpec((s, k), lambda n: (0, 0), pipeline_mode=once),
                   pl.BlockSpec((2, k, FW), lambda n: (0, 0, prev(n))),
                   pl.BlockSpec((FW, D_MODEL), lambda n: (cur(n), 0)), both(3), both(1)],
        out_shape=[jax.ShapeDtypeStruct((s, k), F32), jax.ShapeDtypeStruct((2, k, D_FF), BF16),
                   jax.ShapeDtypeStruct((D_FF, D_MODEL), BF16),
                   jax.ShapeDtypeStruct((2, 3, D_FF), F32), jax.ShapeDtypeStruct((2, 1, D_FF), F32)],
        scratch_shapes=[pltpu.VMEM((2, s, FW), BF16)],
        operands=(dout, wdown, up, act, f, fw, fw, wup4, wup4, h2t), ride=ride)


def matmul_cols_bwd(dy, other, name, wgrad, ride=None):
    m = dy[0].shape[1]
    if wgrad:
        k = other.shape[0]
        nj, nb = N_CHIPS, sum(d.shape[0] * d.shape[2] for d in dy) // (N_CHIPS * CW)
    else:
        nj, k, ns = other.shape
        nb = ns // CW
    per_seg = dy[0].shape[2] // CW
    first = [sum(d.shape[0] for d in dy[:i]) for i in range(len(dy))]

    def segment(j, b):
        return (j * nb + b) // per_seg, (j * nb + b) % per_seg

    def body(*refs):
        dy_refs, (o_ref, r_ref) = refs[:len(dy)], refs[len(dy):]
        seg, _ = segment(pl.program_id(0), pl.program_id(1))
        dyb = dy_refs[-1][0]
        for i in range(len(dy) - 2, -1, -1):
            dyb = jnp.where(seg < first[i + 1], dy_refs[i][0], dyb)
        if wgrad:
            r_ref[...] = _dot(o_ref[...], dyb).astype(BF16)
        else:
            @pl.when((pl.program_id(0) == 0) & (pl.program_id(1) == 0))
            def _():
                r_ref[...] = jnp.zeros_like(r_ref)

            r_ref[...] += _dot_nt(dyb, o_ref[0])

    def dy_spec(i):
        nseg = dy[i].shape[0]

        def index(j, b):
            seg, col = segment(j, b)
            local = seg - first[i]
            return (jnp.clip(local, 0, nseg - 1), 0,
                    jnp.where(local < 0, 0, jnp.where(local >= nseg, per_seg - 1, col)))

        return pl.BlockSpec((1, m, CW), index)

    if wgrad:
        other_spec = pl.BlockSpec((k, m), lambda j, b: (0, 0))
        out_spec = pl.BlockSpec((k, CW), lambda j, b: (0, j * nb + b))
        out_shape = jax.ShapeDtypeStruct((k, nj * nb * CW), BF16)
    else:
        other_spec = pl.BlockSpec((1, k, CW), lambda j, b: (j, 0, b))
        out_spec = pl.BlockSpec((m, k), lambda j, b: (0, 0))
        out_shape = jax.ShapeDtypeStruct((m, k), F32)
    return _call(
        body, name=name, grid=(nj, nb), in_specs=[dy_spec(i) for i in range(len(dy))] + [other_spec],
        out_specs=[out_spec], out_shape=[out_shape], operands=(*dy, other), ride=ride)


def norms_mid_bwd(dh2, x2, dy, mix, g3, g2, ride=None):
    s, d = x2.shape
    t = _token_tile(s)

    def body(dh2_ref, x2_ref, dy_ref, mix_ref, g3_ref, g2_ref, dx2_ref, dmix_ref, dg3_ref, dg2_ref):
        @pl.when(pl.program_id(0) == 0)
        def _():
            dg3_ref[...] = jnp.zeros_like(dg3_ref)
            dg2_ref[...] = jnp.zeros_like(dg2_ref)

        n3, r3 = _rms_stats(x2_ref[...])
        dx, dg3 = _rms_bwd(n3, r3, g3_ref[...], dh2_ref[...])
        dx2 = dy_ref[...] + dx
        dx2_ref[...] = dx2
        dg3_ref[...] += jnp.sum(dg3, axis=0, keepdims=True)
        n2, r2 = _rms_stats(mix_ref[...])
        dmix, dg2 = _rms_bwd(n2, r2, g2_ref[...], dx2)
        dmix_ref[...] = dmix.astype(BF16)
        dg2_ref[...] += jnp.sum(dg2, axis=0, keepdims=True)

    tile = pl.BlockSpec((t, d), lambda i: (i, 0))
    vec = pl.BlockSpec((1, d), lambda i: (0, 0))
    v = jax.ShapeDtypeStruct((1, d), F32)
    return _call(
        body, name="norms_mid_bwd", grid=(s // t,),
        in_specs=[tile, tile, tile, tile, vec, vec],
        out_specs=[tile, tile, vec, vec],
        out_shape=[jax.ShapeDtypeStruct((s, d), F32), jax.ShapeDtypeStruct((s, d), BF16), v, v],
        operands=(dh2, x2, dy, mix, g3, g2), ride=ride)


def mix_out_bwd(dmix, wout, merged, a, b, proj, ride=None):
    s = dmix.shape[0]
    nblk = D_MODEL // CW

    chunk = min(s, 128)

    def body(dm_ref, w_ref, mg_ref, a_ref, b_ref, gc_ref, gl_ref, da_ref, db_ref, dw_ref, dg_ref, dmerged_scr):
        dm = dm_ref[...]
        dmerged_scr[...] = _dot_nt(dm, w_ref[...])
        dw_ref[...] = _dot_tn(mg_ref[...], dm).astype(BF16)

        def rows_chunk(i, carry):
            rows = pl.ds(pl.multiple_of(i * chunk, chunk), chunk)
            dmerged = dmerged_scr[rows, :]
            sc = _sigmoid(gc_ref[rows, :])
            sl = _sigmoid(gl_ref[rows, :])
            da_ref[rows, :] = (dmerged * sc).astype(BF16)
            db_ref[rows, :] = (dmerged * sl).astype(BF16)
            dg_ref[0, rows, :] = (dmerged * a_ref[rows, :] * sc * (1.0 - sc)).astype(BF16)
            dg_ref[1, rows, :] = (dmerged * b_ref[rows, :] * sl * (1.0 - sl)).astype(BF16)
            return carry

        lax.fori_loop(0, s // chunk, rows_chunk, 0)

    res = pl.BlockSpec((s, D_MODEL), lambda n: (0, 0))
    rows = pl.BlockSpec((CW, D_MODEL), lambda n: (n, 0))
    col = pl.BlockSpec((s, CW), lambda n: (0, n))
    blk = lambda k: pl.BlockSpec((s, CW), lambda n, k=k: (0, k * nblk + n))
    hb = jax.ShapeDtypeStruct((s, D_MODEL), BF16)
    return _call(
        body, name="mix_out_bwd", grid=(nblk,),
        in_specs=[res, rows, col, col, col, blk(5), blk(6)],
        out_specs=[col, col, rows, pl.BlockSpec((2, s, CW), lambda n: (0, 0, n))],
        out_shape=[hb, hb, jax.ShapeDtypeStruct((D_MODEL, D_MODEL), BF16), jax.ShapeDtypeStruct((2, s, D_MODEL), BF16)],
        scratch_shapes=[pltpu.VMEM((s, CW), F32)],
        operands=(dmix, wout, merged, a, b, proj, proj), ride=ride)


def mix_conv_bwd(da, wcb, proj, q, ws, ride=None):
    s = da.shape[0]
    nblk = D_MODEL // CW

    def body(da_ref, w_ref, cb_ref, cc_ref, cx_ref, q_ref, ws_ref, dc_ref, dw_ref, dws_ref):
        dab = da_ref[...]
        dya = _dot_nt(dab, w_ref[...])
        cb = cb_ref[...]
        cc = cc_ref[...]
        cx = cx_ref[...]
        q = q_ref[...]
        dw_ref[...] = _dot_tn((cb * q).astype(BF16), dab).astype(BF16)
        dc_ref[0] = (dya * q).astype(BF16)
        dq = _advances(dya * cb, 3)
        dp = _taps_sum(dq, ws_ref)
        dws_ref[...] = _conv_wgrad(dq, cc * cx)
        dc_ref[1] = (dp * cx).astype(BF16)
        dc_ref[2] = (dp * cc).astype(BF16)

    res = pl.BlockSpec((s, D_MODEL), lambda n: (0, 0))
    rows = pl.BlockSpec((CW, D_MODEL), lambda n: (n, 0))
    col = pl.BlockSpec((s, CW), lambda n: (0, n))
    blk = lambda k: pl.BlockSpec((s, CW), lambda n, k=k: (0, k * nblk + n))
    taps = pl.BlockSpec((3, CW), lambda n: (0, n))
    hb = jax.ShapeDtypeStruct((s, D_MODEL), BF16)
    return _call(
        body, name="mix_conv_bwd", grid=(nblk,),
        in_specs=[res, rows, blk(0), blk(1), blk(2), col, taps],
        out_specs=[pl.BlockSpec((3, s, CW), lambda n: (0, 0, n)), rows, taps],
        out_shape=[jax.ShapeDtypeStruct((3, s, D_MODEL), BF16), jax.ShapeDtypeStruct((D_MODEL, D_MODEL), BF16),
                   jax.ShapeDtypeStruct((3, D_MODEL), F32)],
        operands=(da, wcb, proj, proj, proj, q, ws), ride=ride)


def mix_lru_bwd(db, wlb, proj, xl, r, i, h, wl, wa, wx, lam, ride=None):
    s = db.shape[0]
    nblk = D_MODEL // CW

    def body(db_ref, w_ref, lx_ref, ly_ref, xl_ref, r_ref, i_ref, h_ref, wl_ref, wa_ref, wx_ref, lam_ref,
             dl_ref, dw_ref, dwa_ref, dwx_ref, dba_ref, dbx_ref, dwl_ref, dbl_ref, dlam_ref,
             c_scr, g_scr):
        dbb = db_ref[...]
        dyb = _dot_nt(dbb, w_ref[...])
        h = h_ref[...]
        ge, dge = _gelu_and_grad(ly_ref[...])
        dw_ref[...] = _dot_tn((h * ge).astype(BF16), dbb).astype(BF16)
        dl_ref[1] = (dyb * h * dge).astype(BF16)
        r = r_ref[...].astype(F32)
        gi = i_ref[...].astype(F32)
        xlb = xl_ref[...]
        xl = xlb.astype(F32)
        lam = lam_ref[...]
        ls = _log_sigmoid(lam)
        a, mult = _lru_gates(r, ls)
        c_scr[...] = _shift_up(a, 1)
        g_scr[...] = dyb * ge
        _scan_backward(c_scr, g_scr, g_scr)
        du = g_scr[...]
        da = du * _shift_down(h, 1)
        dmult = du * gi * xl
        di = du * mult * xl
        dxl = du * mult * gi
        first = _rows(a.shape) == 0
        dlog_a = da * a - jnp.where(first, 0.0, dmult * a * a / mult)
        dr = dlog_a * (LRU_C * ls)
        dlam_ref[...] = jnp.sum(dlog_a * r, axis=0, keepdims=True) * (LRU_C * (1.0 - _sigmoid(lam)))
        dzr = dr * r * (1.0 - r)
        dzi = di * gi * (1.0 - gi)
        dba_ref[...] = jnp.sum(dzr, axis=0, keepdims=True)
        dbx_ref[...] = jnp.sum(dzi, axis=0, keepdims=True)
        dzrb = dzr.astype(BF16)
        dzib = dzi.astype(BF16)
        dwa_ref[0] = _dot_tn(xlb, dzrb)
        dwx_ref[0] = _dot_tn(xlb, dzib)
        dxl = _advances(dxl + _dot_nt(dzrb, wa_ref[0]) + _dot_nt(dzib, wx_ref[0]), 4)
        dl_ref[0] = _taps_sum(dxl, wl_ref).astype(BF16)
        dwl_ref[...] = _conv_wgrad(dxl, lx_ref[...])
        dbl_ref[...] = jnp.sum(dxl[0], axis=0, keepdims=True)

    res = pl.BlockSpec((s, D_MODEL), lambda n: (0, 0))
    rows = pl.BlockSpec((CW, D_MODEL), lambda n: (n, 0))
    col = pl.BlockSpec((s, CW), lambda n: (0, n))
    blk = lambda k: pl.BlockSpec((s, CW), lambda n, k=k: (0, k * nblk + n))
    taps = pl.BlockSpec((4, CW), lambda n: (0, n))
    vec = pl.BlockSpec((1, CW), lambda n: (0, n))
    mat = pl.BlockSpec((1, CW, CW), lambda n: (n, 0, 0))
    hb = jax.ShapeDtypeStruct((s, D_MODEL), BF16)
    v = jax.ShapeDtypeStruct((1, D_MODEL), F32)
    m = jax.ShapeDtypeStruct((LRU_HEADS, HEAD_DIM, HEAD_DIM), F32)
    scr = pltpu.VMEM((s, CW), F32)
    return _call(
        body, name="mix_lru_bwd", grid=(nblk,),
        in_specs=[res, rows, blk(3), blk(4), col, col, col, col, taps, mat, mat, vec],
        out_specs=[pl.BlockSpec((2, s, CW), lambda n: (0, 0, n)), rows, mat, mat, vec, vec, taps, vec, vec],
        out_shape=[jax.ShapeDtypeStruct((2, s, D_MODEL), BF16), jax.ShapeDtypeStruct((D_MODEL, D_MODEL), BF16), m, m, v, v,
                   jax.ShapeDtypeStruct((4, D_MODEL), F32), v, v],
        scratch_shapes=[scr, scr],
        operands=(db, wlb, proj, proj, xl, r, i, h, wl, wa, wx, lam), ride=ride)


def norm_in_bwd(dh1, x, dx2, g1, ride=None):
    s, d = x.shape
    t = _token_tile(s)

    def body(dh_ref, x_ref, dx2_ref, g_ref, dx_ref, dg_ref):
        @pl.when(pl.program_id(0) == 0)
        def _():
            dg_ref[...] = jnp.zeros_like(dg_ref)

        n, r = _rms_stats(x_ref[...])
        dx, dg = _rms_bwd(n, r, g_ref[...], dh_ref[...])
        dx_ref[...] = dx2_ref[...] + dx
        dg_ref[...] += jnp.sum(dg, axis=0, keepdims=True)

    tile = pl.BlockSpec((t, d), lambda i: (i, 0))
    vec = pl.BlockSpec((1, d), lambda i: (0, 0))
    return _call(
        body, name="norm_in_bwd", grid=(s // t,),
        in_specs=[tile, tile, tile, vec],
        out_specs=[tile, vec],
        out_shape=[jax.ShapeDtypeStruct((s, d), F32), jax.ShapeDtypeStruct((1, d), F32)],
        operands=(dh1, x, dx2, g1), ride=ride)


def _owned_part(ref, kind, k, h, hr):
    if kind == "col":
        ns = ref.shape[1] // N_CHIPS
        return ref.at[pl.ds(h * hr, hr), pl.ds(k * ns, ns)]
    if kind == "row":
        return ref.at[pl.ds(k * 2 * hr + h * hr, hr), :]
    if kind == "col2":
        ns = ref.shape[2] // 2
        return ref.at[k // 2, pl.ds(h * hr, hr), pl.ds((k % 2) * ns, ns)]
    return ref.at[k, pl.ds(h * hr, hr), :]


def _part_shape(g, kind):
    if kind == "col2":
        return g.shape[1] // 2, g.shape[2] // 2
    if kind == "col":
        return g.shape[0] // 2, g.shape[1] // N_CHIPS
    if kind == "row":
        return g.shape[0] // (2 * N_CHIPS), g.shape[1]
    return g.shape[1] // 2, g.shape[2]


def pair_split(grads, kinds, name):
    n = len(grads)
    shapes = [_part_shape(g, k) for g, k in zip(grads, kinds)]

    def body(*refs):
        ins, theirs = refs[:n], refs[n:2 * n]
        send_sem, recv_sem = refs[2 * n:]
        x, y, c = _position()
        copies = []
        for a in range(n):
            hr = shapes[a][0]
            for k in range(N_CHIPS):
                s = a * N_CHIPS + k
                copies.append(pltpu.make_async_remote_copy(
                    src_ref=_owned_part(ins[a], kinds[a], k, 1 - c, hr), dst_ref=theirs[a].at[k],
                    send_sem=send_sem.at[s], recv_sem=recv_sem.at[s], device_id=(x, y, 1 - c), device_id_type=MESH))
        _handshake([(x, y, 1 - c)])
        for cp in copies:
            cp.start()
        for cp in copies:
            cp.wait()

    return pl.pallas_call(
        body, name=name,
        in_specs=[_HBM] * n, out_specs=[_HBM] * n,
        out_shape=[jax.ShapeDtypeStruct((N_CHIPS,) + shp, g.dtype) for shp, g in zip(shapes, grads)],
        scratch_shapes=[pltpu.SemaphoreType.DMA((n * N_CHIPS,))] * 2,
        compiler_params=pltpu.CompilerParams(collective_id=SIBLING),
    )(*grads)


def pair_swap(halves):
    n = len(halves)

    def body(*refs):
        ins, outs = refs[:n], refs[n:2 * n]
        send_sem, recv_sem = refs[2 * n:]
        x, y, c = _position()
        copies = [pltpu.make_async_remote_copy(
            src_ref=ins[a], dst_ref=outs[a], send_sem=send_sem.at[a], recv_sem=recv_sem.at[a],
            device_id=(x, y, 1 - c), device_id_type=MESH) for a in range(n)]
        _handshake([(x, y, 1 - c)])
        for cp in copies:
            cp.start()
        for cp in copies:
            cp.wait()

    return pl.pallas_call(
        body, name="pair_swap",
        in_specs=[_HBM] * n, out_specs=[_HBM] * n,
        out_shape=[jax.ShapeDtypeStruct(h.shape, h.dtype) for h in halves],
        scratch_shapes=[pltpu.SemaphoreType.DMA((n,))] * 2,
        compiler_params=pltpu.CompilerParams(collective_id=SIBLING),
    )(*halves)


def _row_tile(rows, cols, limit_bytes=1 << 20):
    best = None
    for t in range(SUBLANES, rows + 1, SUBLANES):
        if rows % t == 0 and t * cols * 4 <= limit_bytes:
            best = t
    return best or rows


def add_pair(g, kind, theirs, core, name):
    nc, rows, cols = theirs.shape
    t = _row_tile(rows, cols, 4 << 20)
    nt = rows // t

    def body(core_ref, g_ref, b_ref, o_ref):
        mine = g_ref[...].reshape(t, cols)
        o_ref[0] = (mine.astype(F32) + b_ref[0].astype(F32)).astype(o_ref.dtype)

    if kind == "col":
        own = pl.BlockSpec((t, cols), lambda k, i, c: (c[0] * nt + i, k))
    elif kind == "col2":
        own = pl.BlockSpec((1, t, cols), lambda k, i, c: (k // 2, c[0] * nt + i, k % 2))
    elif kind == "row":
        own = pl.BlockSpec((t, cols), lambda k, i, c: ((2 * k + c[0]) * nt + i, 0))
    else:
        own = pl.BlockSpec((1, t, cols), lambda k, i, c: (k, c[0] * nt + i, 0))
    spec = pl.BlockSpec((1, t, cols), lambda k, i, c: (k, i, 0))
    return pl.pallas_call(
        body, name=name,
        grid_spec=pltpu.PrefetchScalarGridSpec(num_scalar_prefetch=1, grid=(nc, nt), in_specs=[own, spec], out_specs=spec),
        out_shape=jax.ShapeDtypeStruct(theirs.shape, theirs.dtype), compiler_params=_params(),
    )(core, g, theirs)


def sum_lead(a, name):
    nl, rows, cols = a.shape
    t = _row_tile(rows, cols, (1 << 20) // 2)

    def body(a_ref, o_ref):
        acc = a_ref[0].astype(F32)
        for s in range(1, nl):
            acc = acc + a_ref[s].astype(F32)
        o_ref[...] = acc

    return pl.pallas_call(
        body, name=name, grid=(rows // t,),
        in_specs=[pl.BlockSpec((nl, t, cols), lambda i: (0, i, 0))],
        out_specs=pl.BlockSpec((t, cols), lambda i: (i, 0)),
        out_shape=jax.ShapeDtypeStruct((rows, cols), F32), compiler_params=_params(),
    )(a)


def sum_chips(rx, csum, chip, name):
    nc, rows, cols = rx.shape
    t = _row_tile(rows, cols, 2 << 20)

    def body(chip_ref, r0, r1, r2, r3, own_ref, o_ref):
        acc = None
        for s, ref in enumerate((r0, r1, r2, r3)):
            term = jnp.where(chip_ref[0] == s, own_ref[0], ref[0]).astype(F32)
            acc = term if acc is None else acc + term
        o_ref[...] = acc

    def slot(s):
        return pl.BlockSpec((1, t, cols), lambda i, c, s=s: (jnp.where(c[0] == s, c[0] ^ 1, s), i, 0))

    return pl.pallas_call(
        body, name=name,
        grid_spec=pltpu.PrefetchScalarGridSpec(
            num_scalar_prefetch=1, grid=(rows // t,),
            in_specs=[slot(s) for s in range(nc)] + [pl.BlockSpec((1, t, cols), lambda i, c: (c[0], i, 0))],
            out_specs=pl.BlockSpec((t, cols), lambda i, c: (i, 0))),
        out_shape=jax.ShapeDtypeStruct((rows, cols), F32), compiler_params=_params(),
    )(chip, rx, rx, rx, rx, csum)


def cast_bf16(a, name):
    rows, cols = a.shape
    t = _row_tile(rows, cols, 2 << 20)

    def body(i_ref, o_ref):
        o_ref[...] = i_ref[...].astype(BF16)

    spec = pl.BlockSpec((t, cols), lambda i: (i, 0))
    return pl.pallas_call(body, name=name, grid=(rows // t,), in_specs=[spec], out_specs=spec,
                          out_shape=jax.ShapeDtypeStruct((rows, cols), BF16), compiler_params=_params())(a)


def _adamw_update(w, g, m, v):
    nm = ADAM_B1 * m + (1.0 - ADAM_B1) * g
    nv = ADAM_B2 * v + (1.0 - ADAM_B2) * (g * g)
    m_hat = nm * (1.0 / (1.0 - ADAM_B1 ** ADAM_STEP))
    v_hat = nv * (1.0 / (1.0 - ADAM_B2 ** ADAM_STEP))
    return -ADAM_LR * (m_hat / (jnp.sqrt(v_hat) + ADAM_EPS) + ADAM_WD * w), nm, nv


def adamw(w, g, m, v, name):
    rows, cols = w.shape
    t = _row_tile(rows, cols)

    def body(w_ref, g_ref, m_ref, v_ref, d_ref, nm_ref, nv_ref):
        d_ref[...], nm_ref[...], nv_ref[...] = _adamw_update(w_ref[...], g_ref[...], m_ref[...], v_ref[...])

    spec = pl.BlockSpec((t, cols), lambda i: (i, 0))
    shp = jax.ShapeDtypeStruct((rows, cols), F32)
    return pl.pallas_call(
        body, name=name, grid=(rows // t,), in_specs=[spec] * 4, out_specs=[spec] * 3,
        out_shape=[shp, shp, shp], compiler_params=_params(),
    )(w, g, m, v)


def adamw_halves(w, g_mine, g_other, m, v, core, name):
    rows, cols = w.shape
    hr = rows // 2
    t = _row_tile(hr, cols)
    nt = hr // t

    def body(core_ref, w_ref, gm_ref, go_ref, m_ref, v_ref, g_ref, d_ref, nm_ref, nv_ref):
        g = jnp.where(pl.program_id(0) // nt == core_ref[0], gm_ref[...], go_ref[...])
        g_ref[...] = g
        d_ref[...], nm_ref[...], nv_ref[...] = _adamw_update(w_ref[...], g, m_ref[...], v_ref[...])

    spec = pl.BlockSpec((t, cols), lambda i, c: (i, 0))
    half = pl.BlockSpec((t, cols), lambda i, c: (i % nt, 0))
    shp = jax.ShapeDtypeStruct((rows, cols), F32)
    return pl.pallas_call(
        body, name=name,
        grid_spec=pltpu.PrefetchScalarGridSpec(num_scalar_prefetch=1, grid=(2 * nt,),
                                               in_specs=[spec, half, half, spec, spec], out_specs=[spec] * 4),
        out_shape=[shp] * 4, compiler_params=_params(),
    )(core, w, g_mine, g_other, m, v)


WEIGHTS = ("norm_mix_pre", "norm_mix_post", "norm_ffn_pre", "norm_ffn_post", "w_in", "conv_short_w",
           "w_conv_branch", "lru_conv_w", "lru_conv_b", "lru_wa", "lru_ba", "lru_wx", "lru_bx", "lru_lambda",
           "w_lru_branch", "w_out", "ffn_w_up", "ffn_conv_w", "ffn_conv_b", "ffn_w_down")
BIG = ("w_in", "ffn_w_up", "w_conv_branch", "w_lru_branch", "w_out", "ffn_w_down")
BIG_KIND = ("col", "col", "row", "row", "row", "row")
SMALL = ("conv_short_w", "lru_conv_w", "lru_wa", "lru_ba", "lru_wx", "lru_bx", "ffn_conv_w")
REPL = ("norm_mix_pre", "norm_mix_post", "norm_ffn_pre", "norm_ffn_post", "lru_conv_b", "lru_lambda", "ffn_conv_b")
PACK_W = 256
SMALL_ROWS = 576
REPL_ROWS = 16
LOSS_ROW = 12
FFN_SHARD = 2 * D_FF // N_CHIPS
QUARTER = HEAD_DIM // N_CHIPS
SMALL_PARTS = (("conv_short_w", 3, (1, 3, PACK_W)), ("lru_conv_w", 4, (1, 4, PACK_W)),
               ("lru_wa", LRU_HEADS * QUARTER, (1, LRU_HEADS, QUARTER, HEAD_DIM)), ("lru_ba", LRU_HEADS, (1, LRU_HEADS, QUARTER)),
               ("lru_wx", LRU_HEADS * QUARTER, (1, LRU_HEADS, QUARTER, HEAD_DIM)), ("lru_bx", LRU_HEADS, (1, LRU_HEADS, QUARTER)),
               ("ffn_conv_w", 3 * FFN_SHARD // PACK_W, (1, 3, FFN_SHARD)))


def _pad8(nr):
    return -(-nr // SUBLANES) * SUBLANES


SMALL_OFFSET = {}
for _name, _nr, _ in SMALL_PARTS:
    SMALL_OFFSET[_name] = sum(_pad8(nr) for n, nr, _ in SMALL_PARTS[:len(SMALL_OFFSET)])
FFN_ROWS = FFN_SHARD // PACK_W
BIASES = ("lru_ba", "lru_bx")
TAPS3 = ("conv_short_w", "ffn_conv_w")


def pack_small(dicts):
    names = [n for n, _, _ in SMALL_PARTS]
    operands = [d[n].transpose(1, 0, 2) if n in TAPS3 else d[n] for d in dicts for n in names]

    def body(*refs):
        ins, outs = refs[:len(operands)], refs[len(operands):]
        for i, o in enumerate(outs):
            o[...] = jnp.zeros_like(o)
            for (name, nr, shape), p in zip(SMALL_PARTS, ins[i * len(names):(i + 1) * len(names)]):
                r0 = SMALL_OFFSET[name]
                if name in BIASES:
                    o[r0:r0 + nr, 0:QUARTER] = p[0]
                elif name == "ffn_conv_w":
                    for k in range(shape[1]):
                        for s in range(FFN_ROWS):
                            o[r0 + FFN_ROWS * k + s:r0 + FFN_ROWS * k + s + 1, :] = p[k, :, s * PACK_W:(s + 1) * PACK_W]
                elif name == "conv_short_w":
                    for k in range(nr):
                        o[r0 + k:r0 + k + 1, :] = p[k]
                else:
                    o[r0:r0 + nr, :] = p[0].reshape(nr, PACK_W)

    shape = jax.ShapeDtypeStruct((SMALL_ROWS, PACK_W), F32)
    return pl.pallas_call(body, name="pack_small", out_shape=[shape] * len(dicts), compiler_params=_params())(*operands)


def full_small(g4):
    def body(p, csw, lcw, wa, wx, fcw):
        chips = range(N_CHIPS)
        r0 = SMALL_OFFSET["conv_short_w"]
        csw[...] = jnp.concatenate([p[c, r0:r0 + 3, :] for c in chips], axis=1)
        r0 = SMALL_OFFSET["lru_conv_w"]
        lcw[...] = jnp.concatenate([p[c, r0:r0 + 4, :] for c in chips], axis=1)
        for name, o in (("lru_wa", wa), ("lru_wx", wx)):
            r0 = SMALL_OFFSET[name]
            for h in range(LRU_HEADS):
                for c in chips:
                    o[h, c * QUARTER:(c + 1) * QUARTER, :] = p[c, r0 + h * QUARTER:r0 + (h + 1) * QUARTER, :].astype(BF16)
        r0 = SMALL_OFFSET["ffn_conv_w"]
        for k in range(3):
            fcw[k:k + 1, :] = jnp.concatenate(
                [p[c, r0 + FFN_ROWS * k + s:r0 + FFN_ROWS * k + s + 1, :] for c in chips for s in range(FFN_ROWS)], axis=1)

    mat = jax.ShapeDtypeStruct((LRU_HEADS, HEAD_DIM, HEAD_DIM), BF16)
    csw, lcw, wa, wx, fcw = pl.pallas_call(
        body, name="full_small",
        out_shape=[jax.ShapeDtypeStruct((3, D_MODEL), F32), jax.ShapeDtypeStruct((4, D_MODEL), F32), mat, mat,
                   jax.ShapeDtypeStruct((3, 2 * D_FF), F32)],
        compiler_params=_params())(g4)

    def bias(name):
        r0 = SMALL_OFFSET[name]
        return g4[:, r0:r0 + LRU_HEADS, :QUARTER].transpose(1, 0, 2).reshape(1, D_MODEL)

    return dict(conv_short_w=csw, lru_conv_w=lcw, lru_wa=wa, lru_wx=wx, ffn_conv_w=fcw,
                lru_ba=bias("lru_ba"), lru_bx=bias("lru_bx"))


def split_small(full):
    def bias(name):
        return full[name].reshape(LRU_HEADS, N_CHIPS, QUARTER).transpose(1, 0, 2)

    def body(csw, lcw, wa, wx, fcw, ba, bx, o):
        o[...] = jnp.zeros_like(o)
        for c in range(N_CHIPS):
            cols = slice(c * PACK_W, (c + 1) * PACK_W)
            r0 = SMALL_OFFSET["conv_short_w"]
            o[c, r0:r0 + 3, :] = csw[:, cols]
            r0 = SMALL_OFFSET["lru_conv_w"]
            o[c, r0:r0 + 4, :] = lcw[:, cols]
            for name, p in (("lru_wa", wa), ("lru_wx", wx)):
                r0 = SMALL_OFFSET[name]
                for h in range(LRU_HEADS):
                    o[c, r0 + h * QUARTER:r0 + (h + 1) * QUARTER, :] = p[h, c * QUARTER:(c + 1) * QUARTER, :]
            for name, p in (("lru_ba", ba), ("lru_bx", bx)):
                r0 = SMALL_OFFSET[name]
                o[c, r0:r0 + LRU_HEADS, 0:QUARTER] = p[c]
            r0 = SMALL_OFFSET["ffn_conv_w"]
            for k in range(3):
                for s in range(FFN_ROWS):
                    lo = c * FFN_SHARD + s * PACK_W
                    o[c, r0 + FFN_ROWS * k + s:r0 + FFN_ROWS * k + s + 1, :] = fcw[k:k + 1, lo:lo + PACK_W]

    return pl.pallas_call(
        body, name="split_small", out_shape=jax.ShapeDtypeStruct((N_CHIPS, SMALL_ROWS, PACK_W), F32),
        compiler_params=_params(),
    )(full["conv_short_w"], full["lru_conv_w"], full["lru_wa"], full["lru_wx"], full["ffn_conv_w"],
      bias("lru_ba"), bias("lru_bx"))


def pack_repl(dicts, loss=None):
    operands = [d[n] for d in dicts for n in REPL] + ([loss] if loss is not None else [])

    def body(*refs):
        ins, outs = refs[:len(operands)], refs[len(operands):]
        for i, o in enumerate(outs):
            o[...] = jnp.zeros_like(o)
            r0 = 0
            for p in ins[i * len(REPL):(i + 1) * len(REPL)]:
                for s in range(p.shape[1] // D_MODEL):
                    o[r0:r0 + 1, :] = p[:, s * D_MODEL:(s + 1) * D_MODEL]
                    r0 += 1
        if loss is not None:
            outs[-1][LOSS_ROW:LOSS_ROW + 1, :] = jnp.tile(ins[-1][...], (1, D_MODEL // 128))

    shape = jax.ShapeDtypeStruct((REPL_ROWS, D_MODEL), F32)
    return pl.pallas_call(body, name="pack_repl" + ("_loss" if loss is not None else ""),
                          out_shape=[shape] * len(dicts), compiler_params=_params())(*operands)


def _lane_concat(ref, r0, n):
    return jnp.concatenate([ref[r0 + s:r0 + s + 1, :] for s in range(n)], axis=1)


def unpack_small(packs):
    names = [n for n, _, _ in SMALL_PARTS]

    def body(*refs):
        ins, outs = refs[:len(packs)], refs[len(packs):]
        for i, p in enumerate(ins):
            for (name, nr, shape), o in zip(SMALL_PARTS, outs[i * len(names):(i + 1) * len(names)]):
                r0 = SMALL_OFFSET[name]
                if name in BIASES:
                    o[0] = p[r0:r0 + nr, 0:QUARTER]
                elif name == "ffn_conv_w":
                    for k in range(shape[1]):
                        o[k] = _lane_concat(p, r0 + FFN_ROWS * k, FFN_ROWS)
                elif name == "conv_short_w":
                    for k in range(nr):
                        o[k] = p[r0 + k:r0 + k + 1, :]
                else:
                    o[0] = p[r0:r0 + nr, :].reshape(shape[1:])

    shapes = [jax.ShapeDtypeStruct((s[1], 1, s[2]) if n in TAPS3 else s, F32) for n, _, s in SMALL_PARTS]
    res = pl.pallas_call(body, name="unpack_small", out_shape=shapes * len(packs), compiler_params=_params())(*packs)
    out = []
    for i in range(len(packs)):
        d = dict(zip(names, res[i * len(names):(i + 1) * len(names)]))
        for n in TAPS3:
            d[n] = d[n].transpose(1, 0, 2)
        out.append(d)
    return out


def unpack_repl(packs):
    rows = [(2 * D_FF // D_MODEL) if n == "ffn_conv_b" else 1 for n in REPL]

    def body(*refs):
        ins, outs = refs[:len(packs)], refs[len(packs):]
        for i, p in enumerate(ins):
            r0 = 0
            for nr, o in zip(rows, outs[i * len(REPL):(i + 1) * len(REPL)]):
                o[...] = _lane_concat(p, r0, nr)
                r0 += nr

    shapes = [jax.ShapeDtypeStruct((1, nr * D_MODEL), F32) for nr in rows]
    res = pl.pallas_call(body, name="unpack_repl", out_shape=shapes * len(packs), compiler_params=_params())(*packs)
    return [dict(zip(REPL, res[i * len(REPL):(i + 1) * len(REPL)])) for i in range(len(packs))]


def kernel(x, norm_mix_pre, norm_mix_post, norm_ffn_pre, norm_ffn_post, w_in, conv_short_w, w_conv_branch, lru_conv_w, lru_conv_b, lru_wa, lru_ba, lru_wx, lru_bx, lru_lambda, w_lru_branch, w_out, ffn_w_up, ffn_conv_w, ffn_conv_b, ffn_w_down, loss_target, m_norm_mix_pre, m_norm_mix_post, m_norm_ffn_pre, m_norm_ffn_post, m_w_in, m_conv_short_w, m_w_conv_branch, m_lru_conv_w, m_lru_conv_b, m_lru_wa, m_lru_ba, m_lru_wx, m_lru_bx, m_lru_lambda, m_w_lru_branch, m_w_out, m_ffn_w_up, m_ffn_conv_w, m_ffn_conv_b, m_ffn_w_down, v_norm_mix_pre, v_norm_mix_post, v_norm_ffn_pre, v_norm_ffn_post, v_w_in, v_conv_short_w, v_w_conv_branch, v_lru_conv_w, v_lru_conv_b, v_lru_wa, v_lru_ba, v_lru_wx, v_lru_bx, v_lru_lambda, v_w_lru_branch, v_w_out, v_ffn_w_up, v_ffn_conv_w, v_ffn_conv_b, v_ffn_w_down):
    given = dict(locals())
    w = {n: given[n] for n in WEIGHTS}
    m = {n: given["m_" + n] for n in WEIGHTS}
    v = {n: given["v_" + n] for n in WEIGHTS}

    xi, yi, ci = _position()
    chip_i = 2 * xi + yi
    chip = chip_i.astype(jnp.int32).reshape(1)
    core = ci.astype(jnp.int32).reshape(1)
    xs, target = x[0], loss_target[0]
    g1, g2, g3, g4 = w["norm_mix_pre"], w["norm_mix_post"], w["norm_ffn_pre"], w["norm_ffn_post"]
    shard = {n: cast_bf16(w[n][0], "cast_" + n) for n in BIG}
    small_shard, m_small, v_small = pack_small([w, m, v])

    def gathered(bufs, names):
        return [_own_slot(b, small_shard if n == "small" else shard[n], chip_i) for b, n in zip(bufs, names)]

    def chip_sums(arrays, kinds, tag):
        theirs = pair_split(arrays, kinds, "pair_split_" + tag)
        return [add_pair(g, k, t, core, "pair_add_%s_%d" % (tag, i)) for i, (g, k, t) in enumerate(zip(arrays, kinds, theirs))]

    h1, h1t = norm_in(xs, g1)
    half_in = D_MODEL // 2
    first = gather_ride([shard["w_in"], small_shard], items=[(0, 0, half_in), (0, half_in, half_in), (1, 0, SMALL_ROWS)])
    win4, small4 = gathered(run_ride(first, "gather_first"), ("w_in", "small"))
    small = full_small(small4)
    first_up = 256
    (proj,), got = matmul_cols(
        h1, win4, "proj_fwd",
        ride=gather_ride([shard["w_conv_branch"], shard["w_lru_branch"], shard["w_out"], shard["ffn_w_up"]],
                         items=[(0, 0, 256), (1, 0, 256), (2, 0, 256), (3, 0, first_up)]))
    wcb, wlb, wout = [g.reshape(-1, D_MODEL) for g in gathered(got[:3], ("w_conv_branch", "w_lru_branch", "w_out"))]
    got = got[3:]
    up_piece = lambda r0, nr, into=None: gather_ride([shard["ffn_w_up"]], items=[(0, r0, nr)], into=into)
    down_piece = lambda r0, nr, into=None: gather_ride([shard["ffn_w_down"]], items=[(0, r0, nr)], into=into)
    q, ya = mix_conv_fwd(proj, small["conv_short_w"])
    (xl, r, gi, h, yb), got = mix_lru_fwd(
        proj, small["lru_conv_w"], w["lru_conv_b"], small["lru_wa"], small["lru_ba"],
        small["lru_wx"], small["lru_bx"], w["lru_lambda"], ride=up_piece(first_up, 512, got))
    (a, b, merged), got = branch_merge_fwd(ya, yb, wcb, wlb, proj, ride=up_piece(first_up + 512, 256, got))
    (wup4,) = gathered(got, ("ffn_w_up",))
    (mix, x2, h2, h2t), got = mix_out_fwd(merged, wout, xs, g2, g3, ride=down_piece(0, 256))
    (up, act, f), got = ffn_up_act_fwd(h2, wup4, small["ffn_conv_w"], w["ffn_conv_b"], ride=down_piece(256, 512, got))
    wdown = gathered(got, ("ffn_w_down",))[0].reshape(-1, D_MODEL)
    dy, dout, loss, dg4 = ffn_down_loss(f, wdown, x2, target, g4)

    dh2, dwup, dwdown, dfw, dfb = ffn_up_bwd(dout, wdown, up, act, f, small["ffn_conv_w"], wup4, h2t)
    cs_down, cs_up = chip_sums([dwdown, dwup], ["row", "col2"], "ffn")
    down_rows = lambda r0, nr, into=None: exchange_ride([cs_down], items=[(0, r0, nr)], into=into)
    up_rows = lambda r0, nr, into=None: exchange_ride([cs_up], items=[(0, r0, nr)], into=into)
    (dx2, dmix, dg3, dg2), rx_down = norms_mid_bwd(dh2, x2, dy, mix, g3, g2, ride=down_rows(0, 128))
    (da, db, dwout, dgates), rx_down = mix_out_bwd(dmix, wout, merged, a, b, proj, ride=down_rows(128, 256, rx_down))
    (dconv, dwcb, dws), rx_up = mix_conv_bwd(da, wcb, proj, q, small["conv_short_w"], ride=up_rows(0, 176))
    cs_mid = chip_sums([dwout, dwcb], ["row", "row"], "mid")
    (dlru, dwlb, dwa, dwx, dba, dbx, dwl, dbl, dlam), rx_up = mix_lru_bwd(
        db, wlb, proj, xl, r, gi, h, small["lru_conv_w"], small["lru_wa"], small["lru_wx"], w["lru_lambda"],
        ride=up_rows(176, 336, rx_up))
    grads = dict(norm_mix_post=dg2, norm_ffn_pre=dg3, norm_ffn_post=dg4, conv_short_w=dws, lru_conv_w=dwl,
                 lru_conv_b=dbl, lru_wa=dwa, lru_ba=dba, lru_wx=dwx, lru_bx=dbx, lru_lambda=dlam,
                 ffn_conv_w=jnp.concatenate([dfw[0], dfw[1]], axis=1), ffn_conv_b=jnp.concatenate([dfb[0], dfb[1]], axis=1))
    cs_late = chip_sums([dwlb, split_small(grads)], ["row", "lead"], "late")
    dproj = [dconv, dlru, dgates]
    (dwin,), rx_all = matmul_cols_bwd(dproj, h1t, "proj_wgrad", True, ride=exchange_ride(cs_mid + cs_late))
    rx_mid, rx_late = rx_all[:2], rx_all[2:]
    cs_in = chip_sums([dwin], ["col"], "in")
    in_rows = lambda r0, nr, into=None: exchange_ride(cs_in, items=[(0, r0, nr)], into=into)
    (dh1,), rx_in = matmul_cols_bwd(dproj, win4, "proj_dgrad", False, ride=in_rows(0, 384))
    dx, grads["norm_mix_pre"] = norm_in_bwd(dh1, xs, dx2, g1)
    (rep_part,) = pack_repl([grads], loss)
    rx_in, rep_all = run_ride(exchange_ride(cs_in, items=[(0, 384, 128)], into=rx_in, rep=rep_part), "exchange_last")

    order = (("w_in", cs_in[0], rx_in), ("ffn_w_up", cs_up, rx_up[0]), ("w_conv_branch", cs_mid[1], rx_mid[1]),
             ("w_lru_branch", cs_late[0], rx_late[0]), ("w_out", cs_mid[0], rx_mid[0]),
             ("ffn_w_down", cs_down, rx_down[0]), ("small", cs_late[1], rx_late[1]))
    halves = [sum_chips(rx, cs, chip, "chip_sum_" + n) for n, cs, rx in order]
    me = 4 * xi + 2 * yi + ci
    rep_grad = sum_lead(_own_slot(rep_all, rep_part, me), "device_sum")
    others = pair_swap(halves)

    g_out, d_out, m_out, v_out = {}, {}, {}, {}
    for n, gm, go in zip(BIG, halves[:-1], others[:-1]):
        g, d, nm, nv = adamw_halves(w[n][0], gm, go, m[n][0], v[n][0], core, "adamw_" + n)
        g_out[n], d_out[n], m_out[n], v_out[n] = g[None], d[None], nm[None], nv[None]
    bufs = adamw_halves(small_shard, halves[-1], others[-1], m_small, v_small, core, "adamw_small")
    for dst, part in zip((g_out, d_out, m_out, v_out), unpack_small(bufs)):
        dst.update(part)
    w_rep, m_rep, v_rep = pack_repl([w, m, v])
    d, nm, nv = adamw(w_rep, rep_grad, m_rep, v_rep, "adamw_repl")
    for dst, part in zip((g_out, d_out, m_out, v_out), unpack_repl([rep_grad, d, nm, nv])):
        dst.update(part)

    return (rep_grad[LOSS_ROW, 0], dx[None], *[g_out[n] for n in WEIGHTS], *[d_out[n] for n in WEIGHTS],
            *[m_out[n] for n in WEIGHTS], *[v_out[n] for n in WEIGHTS])
```
